```python
import jax, jax.numpy as jnp
from jax import lax
import numpy as np

D_MODEL = 1024
BATCH = 8
SEQ = 4096
DEPTH = 1

HEAD_DIM = 64
SSD_HEADS = 16
SSD_D_INNER = SSD_HEADS * HEAD_DIM
SSD_GROUPS = 4
SSD_STATE = 128
SSD_CONV = 4
SSD_CHUNK = 128
CONV_CH = SSD_D_INNER + 2 * SSD_GROUPS * SSD_STATE
ATT_HEADS = 16
ATT_D = ATT_HEADS * HEAD_DIM
DILATED_PATTERNS = ((128, 1), (512, 4), (2048, 16))
MIX_WIDTH = SSD_D_INNER + ATT_D
IN_PROJ_WIDTH = SSD_D_INNER + CONV_CH + SSD_HEADS + 3 * ATT_D
D_FF = 4 * D_MODEL
N_MOD = 6
EPS = 1e-6

kernel_name = 'hybrid_ssd_dilated_attn_block'


def _rms(t):
    tf = t.astype(jnp.float32)
    return tf * lax.rsqrt(jnp.mean(tf * tf, axis=-1, keepdims=True) + EPS)


def rms_norm(t, w):
    return (_rms(t) * w.astype(jnp.float32)).astype(t.dtype)


def causal_depthwise_conv(u, w, bias):
    k = w.shape[0]
    s = u.shape[1]
    up = jnp.pad(u, ((0, 0), (k - 1, 0), (0, 0)))
    out = bias
    for i in range(k):
        out = out + w[i] * up[:, i:i + s]
    return out


def ssd_chunked_scan(xs, dt, a, bm, cm):
    b, s, h, p = xs.shape
    g, n = bm.shape[2], bm.shape[3]
    e = h // g
    nc = s // SSD_CHUNK
    L = SSD_CHUNK
    xdt = (xs.astype(jnp.float32) * dt[..., None]).reshape(b, nc, L, g, e, p)
    bc = bm.reshape(b, nc, L, g, n)
    cc = cm.reshape(b, nc, L, g, n)
    a_cs = jnp.cumsum((dt * a).reshape(b, nc, L, g, e), axis=2)
    seg = a_cs[:, :, :, None] - a_cs[:, :, None, :]
    tril = jnp.tril(jnp.ones((L, L), dtype=bool))[None, None, :, :, None, None]
    lmat = jnp.exp(jnp.where(tril, seg, -jnp.inf))
    cb = jnp.einsum('bclgn,bcsgn->bclsg', cc, bc)
    y_diag = jnp.einsum('bclsge,bcsgep->bclgep', cb[..., None] * lmat, xdt)
    decay_states = jnp.exp(a_cs[:, :, -1:] - a_cs)
    states = jnp.einsum('bclgn,bclge,bclgep->bcgepn', bc, decay_states, xdt)
    chunk_decay = jnp.exp(a_cs[:, :, -1])

    def step(h_prev, inp):
        st, dec = inp
        return h_prev * dec[..., None, None] + st, h_prev

    init = jnp.zeros_like(states[:, 0])
    _, prev_states = lax.scan(step, init, (jnp.moveaxis(states, 1, 0), jnp.moveaxis(chunk_decay, 1, 0)))
    prev_states = jnp.moveaxis(prev_states, 0, 1)
    y_off = jnp.einsum('bclgn,bcgepn,bclge->bclgep', cc, prev_states, jnp.exp(a_cs))
    return (y_diag + y_off).reshape(b, s, h, p)


def dilated_window_attention(q, k, v, window, dilation):
    b, s, h, d = q.shape
    nw = window // dilation
    blk = nw
    L = s // dilation
    nb = -(-L // blk)
    Lp = nb * blk

    def to_classes(t):
        t = t.reshape(b, L, dilation, h, d).transpose(0, 2, 3, 1, 4)
        return jnp.pad(t, ((0, 0), (0, 0), (0, 0), (0, Lp - L), (0, 0)))

    def with_prev(t):
        tp = jnp.pad(t, ((0, 0), (0, 0), (0, 0), (blk, 0), (0, 0)))
        prev = tp[:, :, :, :Lp].reshape(b, dilation, h, nb, blk, d)
        cur = t.reshape(b, dilation, h, nb, blk, d)
        return jnp.concatenate([prev, cur], axis=-2)

    qb = to_classes(q).reshape(b, dilation, h, nb, blk, d)
    kb = with_prev(to_classes(k))
    vb = with_prev(to_classes(v))
    scores = jnp.einsum('brhiqd,brhikd->brhiqk', qb, kb).astype(jnp.float32)
    qi = jnp.arange(blk)[:, None]
    ki = jnp.arange(2 * blk)[None, :]
    dist = blk + qi - ki
    band = (dist >= 0) & (dist <= nw)
    valid = (jnp.arange(nb)[:, None, None] > 0) | (ki[None] >= blk)
    mask = band[None] & valid
    scores = jnp.where(mask, scores, -jnp.inf)
    m = jnp.max(scores, axis=-1, keepdims=True)
    pr = jnp.exp(scores - m)
    denom = jnp.sum(pr, axis=-1, keepdims=True)
    o = jnp.einsum('brhiqk,brhikd->brhiqd', pr, vb.astype(jnp.float32)) / denom
    lse = (m + jnp.log(denom))[..., 0]
    o = o.reshape(b, dilation, h, Lp, d)[:, :, :, :L].transpose(0, 3, 1, 2, 4).reshape(b, s, h, d)
    lse = lse.reshape(b, dilation, h, Lp)[:, :, :, :L].transpose(0, 3, 1, 2).reshape(b, s, h)
    return o, lse


def hybrid_mixer(hn, w_in, conv_w, conv_b, dt_bias, a_log, d_skip, ssd_norm_w,
                 q_norm_w, k_norm_w, attn_norm_w, w_out):
    b, s, _ = hn.shape
    proj = hn @ w_in
    o1 = SSD_D_INNER
    o2 = o1 + CONV_CH
    o3 = o2 + SSD_HEADS
    o4 = o3 + ATT_D
    o5 = o4 + ATT_D
    z, xbc, dt_raw, q, k, v = jnp.split(proj, [o1, o2, o3, o4, o5], axis=-1)

    xbc = jax.nn.silu(causal_depthwise_conv(xbc, conv_w, conv_b))
    xs, bm, cm = jnp.split(xbc, [SSD_D_INNER, SSD_D_INNER + SSD_GROUPS * SSD_STATE], axis=-1)
    xs = xs.reshape(b, s, SSD_HEADS, HEAD_DIM)
    bm = bm.reshape(b, s, SSD_GROUPS, SSD_STATE)
    cm = cm.reshape(b, s, SSD_GROUPS, SSD_STATE)
    dt = jax.nn.softplus(dt_raw.astype(jnp.float32) + dt_bias.astype(jnp.float32))
    a = -jnp.exp(a_log.astype(jnp.float32))
    y = ssd_chunked_scan(xs, dt, a, bm, cm) + d_skip.astype(jnp.float32)[:, None] * xs
    y = y.reshape(b, s, SSD_D_INNER).astype(hn.dtype) * jax.nn.silu(z)
    y_ssd = (_rms(y.reshape(b, s, SSD_GROUPS, SSD_D_INNER // SSD_GROUPS)).reshape(b, s, SSD_D_INNER)
             * ssd_norm_w.astype(jnp.float32)).astype(hn.dtype)

    q = rms_norm(q.reshape(b, s, ATT_HEADS, HEAD_DIM), q_norm_w) * HEAD_DIM ** -0.5
    k = rms_norm(k.reshape(b, s, ATT_HEADS, HEAD_DIM), k_norm_w)
    v = v.reshape(b, s, ATT_HEADS, HEAD_DIM)
    branches = [dilated_window_attention(q, k, v, w, r) for (w, r) in DILATED_PATTERNS]
    outs = jnp.stack([br[0] for br in branches])
    lses = jnp.stack([br[1] for br in branches])
    alpha = jax.nn.softmax(lses, axis=0)
    o = jnp.sum(alpha[..., None] * outs, axis=0).reshape(b, s, ATT_D)
    y_att = rms_norm(o, attn_norm_w).astype(hn.dtype)

    return jnp.concatenate([y_ssd, y_att], axis=-1) @ w_out


def _fwd_setup_inputs(seed: int = 0) -> dict:
    key = jax.random.key(seed)
    ks = jax.random.split(key, 20)
    f32 = jnp.float32

    def nrm(k, shape, scale):
        return jax.random.normal(k, shape, f32) * scale

    dt0 = jnp.exp(jax.random.uniform(ks[9], (DEPTH, SSD_HEADS), f32, np.log(1e-3), np.log(1e-1)))
    dt_bias = dt0 + jnp.log(-jnp.expm1(-dt0))
    return {
        'x': nrm(ks[0], (BATCH, SEQ, D_MODEL), 1.0),
        'c': nrm(ks[1], (BATCH, D_MODEL), 1.0),
        'norm1_w': 1.0 + nrm(ks[2], (DEPTH, D_MODEL), 0.02),
        'norm2_w': 1.0 + nrm(ks[3], (DEPTH, D_MODEL), 0.02),
        'w_ada': nrm(ks[4], (DEPTH, D_MODEL, N_MOD * D_MODEL), D_MODEL ** -0.5),
        'b_ada': nrm(ks[5], (DEPTH, N_MOD * D_MODEL), 0.01),
        'w_in': nrm(ks[6], (DEPTH, D_MODEL, IN_PROJ_WIDTH), D_MODEL ** -0.5),
        'conv_w': nrm(ks[7], (DEPTH, SSD_CONV, CONV_CH), SSD_CONV ** -0.5),
        'conv_b': nrm(ks[8], (DEPTH, CONV_CH), 0.01),
        'dt_bias': dt_bias,
        'a_log': jnp.log(jax.random.uniform(ks[10], (DEPTH, SSD_HEADS), f32, 1.0, 16.0)),
        'd_skip': 1.0 + nrm(ks[11], (DEPTH, SSD_HEADS), 0.1),
        'ssd_norm_w': 1.0 + nrm(ks[12], (DEPTH, SSD_D_INNER), 0.02),
        'q_norm_w': 1.0 + nrm(ks[13], (DEPTH, HEAD_DIM), 0.02),
        'k_norm_w': 1.0 + nrm(ks[14], (DEPTH, HEAD_DIM), 0.02),
        'attn_norm_w': 1.0 + nrm(ks[15], (DEPTH, ATT_D), 0.02),
        'w_out': nrm(ks[16], (DEPTH, MIX_WIDTH, D_MODEL), MIX_WIDTH ** -0.5),
        'w_ff1': nrm(ks[17], (DEPTH, D_MODEL, D_FF), D_MODEL ** -0.5),
        'w_ff2': nrm(ks[18], (DEPTH, D_FF, D_MODEL), D_FF ** -0.5),
    }


def _fwd_reference(x, c, norm1_w, norm2_w, w_ada, b_ada, w_in, conv_w, conv_b, dt_bias, a_log,
              d_skip, ssd_norm_w, q_norm_w, k_norm_w, attn_norm_w, w_out, w_ff1, w_ff2):
    c_act = jax.nn.silu(c)
    for l in range(DEPTH):
        mod = c_act @ w_ada[l] + b_ada[l]
        shift1, scale1, gate1, shift2, scale2, gate2 = [t[:, None, :] for t in jnp.split(mod, N_MOD, axis=-1)]
        h1 = rms_norm(x, norm1_w[l]) * (1.0 + scale1) + shift1
        mix = hybrid_mixer(h1, w_in[l], conv_w[l], conv_b[l], dt_bias[l], a_log[l], d_skip[l],
                           ssd_norm_w[l], q_norm_w[l], k_norm_w[l], attn_norm_w[l], w_out[l])
        x = x + gate1 * mix
        h2 = rms_norm(x, norm2_w[l]) * (1.0 + scale2) + shift2
        ff = jnp.square(jax.nn.relu(h2 @ w_ff1[l])) @ w_ff2[l]
        x = x + gate2 * ff
    return x.astype(c.dtype)


import jax as _jax
import jax.numpy as _jnp

TWIN_FORMAT = 'train_step'
FWD_PARAMS = ['x', 'c', 'norm1_w', 'norm2_w', 'w_ada', 'b_ada', 'w_in', 'conv_w', 'conv_b', 'dt_bias', 'a_log', 'd_skip', 'ssd_norm_w', 'q_norm_w', 'k_norm_w', 'attn_norm_w', 'w_out', 'w_ff1', 'w_ff2']
TWIN_WEIGHTS = ['norm1_w', 'norm2_w', 'w_ada', 'b_ada', 'w_in', 'conv_w', 'conv_b', 'dt_bias', 'a_log', 'd_skip', 'ssd_norm_w', 'q_norm_w', 'k_norm_w', 'attn_norm_w', 'w_out', 'w_ff1', 'w_ff2']
TWIN_DIFF_INPUT = 'x'
TWIN_INPUTS = ['x', 'c', 'norm1_w', 'norm2_w', 'w_ada', 'b_ada', 'w_in', 'conv_w', 'conv_b', 'dt_bias', 'a_log', 'd_skip', 'ssd_norm_w', 'q_norm_w', 'k_norm_w', 'attn_norm_w', 'w_out', 'w_ff1', 'w_ff2', 'loss_target', 'm_norm1_w', 'm_norm2_w', 'm_w_ada', 'm_b_ada', 'm_w_in', 'm_conv_w', 'm_conv_b', 'm_dt_bias', 'm_a_log', 'm_d_skip', 'm_ssd_norm_w', 'm_q_norm_w', 'm_k_norm_w', 'm_attn_norm_w', 'm_w_out', 'm_w_ff1', 'm_w_ff2', 'v_norm1_w', 'v_norm2_w', 'v_w_ada', 'v_b_ada', 'v_w_in', 'v_conv_w', 'v_conv_b', 'v_dt_bias', 'v_a_log', 'v_d_skip', 'v_ssd_norm_w', 'v_q_norm_w', 'v_k_norm_w', 'v_attn_norm_w', 'v_w_out', 'v_w_ff1', 'v_w_ff2']
TWIN_OUTPUTS = ['loss', 'grad_x', 'grad_norm1_w', 'grad_norm2_w', 'grad_w_ada', 'grad_b_ada', 'grad_w_in', 'grad_conv_w', 'grad_conv_b', 'grad_dt_bias', 'grad_a_log', 'grad_d_skip', 'grad_ssd_norm_w', 'grad_q_norm_w', 'grad_k_norm_w', 'grad_attn_norm_w', 'grad_w_out', 'grad_w_ff1', 'grad_w_ff2', 'delta_norm1_w', 'delta_norm2_w', 'delta_w_ada', 'delta_b_ada', 'delta_w_in', 'delta_conv_w', 'delta_conv_b', 'delta_dt_bias', 'delta_a_log', 'delta_d_skip', 'delta_ssd_norm_w', 'delta_q_norm_w', 'delta_k_norm_w', 'delta_attn_norm_w', 'delta_w_out', 'delta_w_ff1', 'delta_w_ff2', 'new_m_norm1_w', 'new_m_norm2_w', 'new_m_w_ada', 'new_m_b_ada', 'new_m_w_in', 'new_m_conv_w', 'new_m_conv_b', 'new_m_dt_bias', 'new_m_a_log', 'new_m_d_skip', 'new_m_ssd_norm_w', 'new_m_q_norm_w', 'new_m_k_norm_w', 'new_m_attn_norm_w', 'new_m_w_out', 'new_m_w_ff1', 'new_m_w_ff2', 'new_v_norm1_w', 'new_v_norm2_w', 'new_v_w_ada', 'new_v_b_ada', 'new_v_w_in', 'new_v_conv_w', 'new_v_conv_b', 'new_v_dt_bias', 'new_v_a_log', 'new_v_d_skip', 'new_v_ssd_norm_w', 'new_v_q_norm_w', 'new_v_k_norm_w', 'new_v_attn_norm_w', 'new_v_w_out', 'new_v_w_ff1', 'new_v_w_ff2']
TWIN_LEAF_KINDS = {'loss': 'loss', 'grad_x': 'grad_x', 'grad_norm1_w': 'grad_w', 'grad_norm2_w': 'grad_w', 'grad_w_ada': 'grad_w', 'grad_b_ada': 'grad_w', 'grad_w_in': 'grad_w', 'grad_conv_w': 'grad_w', 'grad_conv_b': 'grad_w', 'grad_dt_bias': 'grad_w', 'grad_a_log': 'grad_w', 'grad_d_skip': 'grad_w', 'grad_ssd_norm_w': 'grad_w', 'grad_q_norm_w': 'grad_w', 'grad_k_norm_w': 'grad_w', 'grad_attn_norm_w': 'grad_w', 'grad_w_out': 'grad_w', 'grad_w_ff1': 'grad_w', 'grad_w_ff2': 'grad_w', 'delta_norm1_w': 'delta_w', 'delta_norm2_w': 'delta_w', 'delta_w_ada': 'delta_w', 'delta_b_ada': 'delta_w', 'delta_w_in': 'delta_w', 'delta_conv_w': 'delta_w', 'delta_conv_b': 'delta_w', 'delta_dt_bias': 'delta_w', 'delta_a_log': 'delta_w', 'delta_d_skip': 'delta_w', 'delta_ssd_norm_w': 'delta_w', 'delta_q_norm_w': 'delta_w', 'delta_k_norm_w': 'delta_w', 'delta_attn_norm_w': 'delta_w', 'delta_w_out': 'delta_w', 'delta_w_ff1': 'delta_w', 'delta_w_ff2': 'delta_w', 'new_m_norm1_w': 'new_m', 'new_m_norm2_w': 'new_m', 'new_m_w_ada': 'new_m', 'new_m_b_ada': 'new_m', 'new_m_w_in': 'new_m', 'new_m_conv_w': 'new_m', 'new_m_conv_b': 'new_m', 'new_m_dt_bias': 'new_m', 'new_m_a_log': 'new_m', 'new_m_d_skip': 'new_m', 'new_m_ssd_norm_w': 'new_m', 'new_m_q_norm_w': 'new_m', 'new_m_k_norm_w': 'new_m', 'new_m_attn_norm_w': 'new_m', 'new_m_w_out': 'new_m', 'new_m_w_ff1': 'new_m', 'new_m_w_ff2': 'new_m', 'new_v_norm1_w': 'new_v', 'new_v_norm2_w': 'new_v', 'new_v_w_ada': 'new_v', 'new_v_b_ada': 'new_v', 'new_v_w_in': 'new_v', 'new_v_conv_w': 'new_v', 'new_v_conv_b': 'new_v', 'new_v_dt_bias': 'new_v', 'new_v_a_log': 'new_v', 'new_v_d_skip': 'new_v', 'new_v_ssd_norm_w': 'new_v', 'new_v_q_norm_w': 'new_v', 'new_v_k_norm_w': 'new_v', 'new_v_attn_norm_w': 'new_v', 'new_v_w_out': 'new_v', 'new_v_w_ff1': 'new_v', 'new_v_w_ff2': 'new_v'}


def _forward(args):
    return _fwd_reference(*[args[k] for k in FWD_PARAMS])


def _output_shape():
    out = _jax.eval_shape(lambda: _forward(_fwd_setup_inputs(0)))
    return out.shape, out.dtype

N_MICROBATCH = 1
ADAM_LR = 0.001
ADAM_B1 = 0.9
ADAM_B2 = 0.999
ADAM_EPS = 1e-08
ADAM_WD = 0.01
ADAM_STEP = 10
PER_EXAMPLE_BATCH_AXIS = {'x': 0, 'c': 0, 'loss_target': 0}
SHARED_INPUTS = []
_WEIGHT_DTYPES = {'norm1_w': _jnp.float32, 'norm2_w': _jnp.float32, 'w_ada': _jnp.float32, 'b_ada': _jnp.float32, 'w_in': _jnp.float32, 'conv_w': _jnp.float32, 'conv_b': _jnp.float32, 'dt_bias': _jnp.float32, 'a_log': _jnp.float32, 'd_skip': _jnp.float32, 'ssd_norm_w': _jnp.float32, 'q_norm_w': _jnp.float32, 'k_norm_w': _jnp.float32, 'attn_norm_w': _jnp.float32, 'w_out': _jnp.float32, 'w_ff1': _jnp.float32, 'w_ff2': _jnp.float32}
MOMENT_SCALE = {'norm1_w': 1.248160e+00, 'norm2_w': 1.081619e+02, 'w_ada': 2.386558e+01, 'b_ada': 4.722295e+01, 'w_in': 8.163409e+00, 'conv_w': 4.744491e+00, 'conv_b': 6.210360e+00, 'dt_bias': 8.110247e+00, 'a_log': 2.373347e+01, 'd_skip': 1.222288e+01, 'ssd_norm_w': 1.243769e+01, 'q_norm_w': 1.709348e+00, 'k_norm_w': 1.731890e+00, 'attn_norm_w': 1.984687e+01, 'w_out': 1.998710e+01, 'w_ff1': 1.091679e+01, 'w_ff2': 2.035355e+01}


def _to_microbatches(a, axis):
    t = _jnp.moveaxis(a, axis, 0)
    t = t.reshape((N_MICROBATCH, t.shape[0] // N_MICROBATCH) + t.shape[1:])
    return _jnp.moveaxis(t, 1, axis + 1)


def setup_inputs(seed: int = 0) -> dict:
    inp = _fwd_setup_inputs(seed)
    key = _jax.random.fold_in(_jax.random.key(seed), 7919)
    shape, _ = _output_shape()
    out = dict(inp)
    out["loss_target"] = _jax.random.normal(_jax.random.fold_in(key, 0), shape, _jnp.float32)
    for i, name in enumerate(TWIN_WEIGHTS):
        w = inp[name].astype(_jnp.float32)
        if MOMENT_SCALE is None:
            s = _jnp.sqrt(_jnp.mean(_jnp.square(w)) + 1e-30)
        else:
            s = MOMENT_SCALE[name]
        km, kv = _jax.random.split(_jax.random.fold_in(key, i + 1))
        out[name] = w
        out["m_" + name] = s * _jax.random.normal(km, w.shape, _jnp.float32)
        out["v_" + name] = (s * s) * _jax.random.uniform(kv, w.shape, _jnp.float32, 0.5, 1.5)
    if N_MICROBATCH > 1:
        for name, axis in PER_EXAMPLE_BATCH_AXIS.items():
            out[name] = _to_microbatches(out[name], axis)
    return {'x': out['x'], 'c': out['c'], 'norm1_w': out['norm1_w'], 'norm2_w': out['norm2_w'], 'w_ada': out['w_ada'], 'b_ada': out['b_ada'], 'w_in': out['w_in'], 'conv_w': out['conv_w'], 'conv_b': out['conv_b'], 'dt_bias': out['dt_bias'], 'a_log': out['a_log'], 'd_skip': out['d_skip'], 'ssd_norm_w': out['ssd_norm_w'], 'q_norm_w': out['q_norm_w'], 'k_norm_w': out['k_norm_w'], 'attn_norm_w': out['attn_norm_w'], 'w_out': out['w_out'], 'w_ff1': out['w_ff1'], 'w_ff2': out['w_ff2'], 'loss_target': out['loss_target'], 'm_norm1_w': out['m_norm1_w'], 'm_norm2_w': out['m_norm2_w'], 'm_w_ada': out['m_w_ada'], 'm_b_ada': out['m_b_ada'], 'm_w_in': out['m_w_in'], 'm_conv_w': out['m_conv_w'], 'm_conv_b': out['m_conv_b'], 'm_dt_bias': out['m_dt_bias'], 'm_a_log': out['m_a_log'], 'm_d_skip': out['m_d_skip'], 'm_ssd_norm_w': out['m_ssd_norm_w'], 'm_q_norm_w': out['m_q_norm_w'], 'm_k_norm_w': out['m_k_norm_w'], 'm_attn_norm_w': out['m_attn_norm_w'], 'm_w_out': out['m_w_out'], 'm_w_ff1': out['m_w_ff1'], 'm_w_ff2': out['m_w_ff2'], 'v_norm1_w': out['v_norm1_w'], 'v_norm2_w': out['v_norm2_w'], 'v_w_ada': out['v_w_ada'], 'v_b_ada': out['v_b_ada'], 'v_w_in': out['v_w_in'], 'v_conv_w': out['v_conv_w'], 'v_conv_b': out['v_conv_b'], 'v_dt_bias': out['v_dt_bias'], 'v_a_log': out['v_a_log'], 'v_d_skip': out['v_d_skip'], 'v_ssd_norm_w': out['v_ssd_norm_w'], 'v_q_norm_w': out['v_q_norm_w'], 'v_k_norm_w': out['v_k_norm_w'], 'v_attn_norm_w': out['v_attn_norm_w'], 'v_w_out': out['v_w_out'], 'v_w_ff1': out['v_w_ff1'], 'v_w_ff2': out['v_w_ff2']}


def _loss(weights, diff, rest, loss_target):
    with _jax.named_scope("forward"):
        args = {**rest, TWIN_DIFF_INPUT: diff, **{k: w.astype(_WEIGHT_DTYPES[k]) for k, w in weights.items()}}
        y = _forward(args)
    with _jax.named_scope("loss_head"):
        err = _jnp.square(y.astype(_jnp.float32) - loss_target)
        return 0.5 * _jnp.sum(_jnp.mean(err, axis=-1)) if err.ndim else 0.5 * err


def _adamw(w, g, m, v):
    m = ADAM_B1 * m + (1.0 - ADAM_B1) * g
    v = ADAM_B2 * v + (1.0 - ADAM_B2) * _jnp.square(g)
    m_hat = m / (1.0 - ADAM_B1 ** ADAM_STEP)
    v_hat = v / (1.0 - ADAM_B2 ** ADAM_STEP)
    delta = -ADAM_LR * (m_hat / (_jnp.sqrt(v_hat) + ADAM_EPS) + ADAM_WD * w)
    return delta, m, v


def reference(x, c, norm1_w, norm2_w, w_ada, b_ada, w_in, conv_w, conv_b, dt_bias, a_log, d_skip, ssd_norm_w, q_norm_w, k_norm_w, attn_norm_w, w_out, w_ff1, w_ff2, loss_target, m_norm1_w, m_norm2_w, m_w_ada, m_b_ada, m_w_in, m_conv_w, m_conv_b, m_dt_bias, m_a_log, m_d_skip, m_ssd_norm_w, m_q_norm_w, m_k_norm_w, m_attn_norm_w, m_w_out, m_w_ff1, m_w_ff2, v_norm1_w, v_norm2_w, v_w_ada, v_b_ada, v_w_in, v_conv_w, v_conv_b, v_dt_bias, v_a_log, v_d_skip, v_ssd_norm_w, v_q_norm_w, v_k_norm_w, v_attn_norm_w, v_w_out, v_w_ff1, v_w_ff2):
    given = dict(x=x, c=c, norm1_w=norm1_w, norm2_w=norm2_w, w_ada=w_ada, b_ada=b_ada, w_in=w_in, conv_w=conv_w, conv_b=conv_b, dt_bias=dt_bias, a_log=a_log, d_skip=d_skip, ssd_norm_w=ssd_norm_w, q_norm_w=q_norm_w, k_norm_w=k_norm_w, attn_norm_w=attn_norm_w, w_out=w_out, w_ff1=w_ff1, w_ff2=w_ff2, loss_target=loss_target, m_norm1_w=m_norm1_w, m_norm2_w=m_norm2_w, m_w_ada=m_w_ada, m_b_ada=m_b_ada, m_w_in=m_w_in, m_conv_w=m_conv_w, m_conv_b=m_conv_b, m_dt_bias=m_dt_bias, m_a_log=m_a_log, m_d_skip=m_d_skip, m_ssd_norm_w=m_ssd_norm_w, m_q_norm_w=m_q_norm_w, m_k_norm_w=m_k_norm_w, m_attn_norm_w=m_attn_norm_w, m_w_out=m_w_out, m_w_ff1=m_w_ff1, m_w_ff2=m_w_ff2, v_norm1_w=v_norm1_w, v_norm2_w=v_norm2_w, v_w_ada=v_w_ada, v_b_ada=v_b_ada, v_w_in=v_w_in, v_conv_w=v_conv_w, v_conv_b=v_conv_b, v_dt_bias=v_dt_bias, v_a_log=v_a_log, v_d_skip=v_d_skip, v_ssd_norm_w=v_ssd_norm_w, v_q_norm_w=v_q_norm_w, v_k_norm_w=v_k_norm_w, v_attn_norm_w=v_attn_norm_w, v_w_out=v_w_out, v_w_ff1=v_w_ff1, v_w_ff2=v_w_ff2)
    weights = {n: given[n] for n in TWIN_WEIGHTS}
    shared = {n: given[n] for n in SHARED_INPUTS}
    per_example = {n: given[n] for n in ['x', 'c']}
    grad_fn = _jax.value_and_grad(_loss, argnums=(0, 1))

    def one_microbatch(ex, loss_target):
        ex = dict(ex)
        diff = ex.pop(TWIN_DIFF_INPUT)
        return grad_fn(weights, diff, {**shared, **ex}, loss_target)

    if N_MICROBATCH == 1:
        loss, (grad_w, grad_x) = one_microbatch(per_example, given["loss_target"])
    else:
        def body(carry, xs):
            loss_sum, grad_sum = carry
            l_k, (gw_k, gx_k) = one_microbatch(xs[0], xs[1])
            with _jax.named_scope("update"):
                return (loss_sum + l_k, _jax.tree.map(_jnp.add, grad_sum, gw_k)), gx_k

        init = (_jnp.zeros((), _jnp.float32), _jax.tree.map(_jnp.zeros_like, weights))
        (loss, grad_w), grad_x = _jax.lax.scan(body, init, (per_example, given["loss_target"]))
    with _jax.named_scope("update"):
        delta_w, new_m, new_v = {}, {}, {}
        for n in TWIN_WEIGHTS:
            delta_w[n], new_m[n], new_v[n] = _adamw(weights[n], grad_w[n], given["m_" + n], given["v_" + n])
    return (loss, grad_x, *[grad_w[n] for n in TWIN_WEIGHTS], *[delta_w[n] for n in TWIN_WEIGHTS],
            *[new_m[n] for n in TWIN_WEIGHTS], *[new_v[n] for n in TWIN_WEIGHTS])
```

```python
import functools

import jax
import jax.numpy as jnp
from jax import lax
from jax.experimental import pallas as pl
from jax.experimental.pallas import tpu as pltpu

F32 = jnp.float32
BF16 = jnp.bfloat16
HIGHEST = lax.Precision.HIGHEST
MESH_IDS = pl.DeviceIdType.MESH

N_DEV = 8
D_MODEL = 1024
HEAD_DIM = 64
SSD_HEADS = 16
SSD_GROUPS = 4
HEADS_PER_GROUP = SSD_HEADS // SSD_GROUPS
SSD_STATE = 128
SSD_CHUNK = 128
SSD_D_INNER = SSD_HEADS * HEAD_DIM
GROUP_WIDTH = SSD_D_INNER // SSD_GROUPS
CONV_K = 4
CONV_CH = SSD_D_INNER + 2 * SSD_GROUPS * SSD_STATE
ATT_HEADS = 16
ATT_D = ATT_HEADS * HEAD_DIM
ATT_BLK = 128
DILATIONS = (1, 4, 16)
D_FF = 4 * D_MODEL
N_MOD = 6
EPS = 1e-6
IN_W = SSD_D_INNER + CONV_CH + SSD_HEADS + 3 * ATT_D
LANE = 128
OFF_Z, OFF_XBC, OFF_Q, OFF_K, OFF_V, OFF_DT = 0, 1024, 3072, 4096, 5120, 6144
IN_WP = OFF_DT + LANE

ADAM_LR, ADAM_B1, ADAM_B2, ADAM_EPS, ADAM_WD, ADAM_STEP = 0.001, 0.9, 0.999, 1e-08, 0.01, 10
VMEM_LIMIT = 56 * 1024 * 1024
ROW_TILE = 512
SMALL_ROWS = 24


def _cparams(sem=None):
    return pltpu.CompilerParams(dimension_semantics=sem, vmem_limit_bytes=VMEM_LIMIT)


def _sigmoid(v):
    return 1.0 / (1.0 + jnp.exp(-v))


def _softplus(v):
    y = jnp.exp(-jnp.abs(v))
    small = y * (1.0 - y * (0.5 - y * (1.0 / 3.0)))
    return jnp.maximum(v, 0.0) + jnp.where(y < 0.01, small, jnp.log(1.0 + y))


def _dot(a, b, dims, precision=None):
    return lax.dot_general(a, b, (dims, ((), ())), preferred_element_type=F32, precision=precision)


NN = ((1,), (0,))
NT = ((1,), (1,))
TN = ((0,), (0,))


def _matmul(a, b, *, ta=False, tb=False, tm, tn, tk, out_dtype=F32, name, mode=None, u=None):
    m, k = (a.shape[1], a.shape[0]) if ta else a.shape
    n = b.shape[0] if tb else b.shape[1]
    assert m % tm == 0 and n % tn == 0 and k % tk == 0, (name, m, n, k)
    nk = k // tk
    a_spec = pl.BlockSpec((tk, tm), lambda i, j, kk: (kk, i)) if ta else pl.BlockSpec((tm, tk), lambda i, j, kk: (i, kk))
    b_spec = pl.BlockSpec((tn, tk), lambda i, j, kk: (j, kk)) if tb else pl.BlockSpec((tk, tn), lambda i, j, kk: (kk, j))
    o_spec = pl.BlockSpec((tm, tn), lambda i, j, kk: (i, j))
    dims = ((0,) if ta else (1,), (1,) if tb else (0,))
    n_out = 2 if mode == "relu2" else 1

    def body(*refs):
        if mode == "drelu2":
            a_ref, b_ref, u_ref = refs[:3]
            rest = refs[3:]
        else:
            a_ref, b_ref = refs[:2]
            u_ref = None
            rest = refs[2:]
        outs = rest[:n_out]
        part = _dot(a_ref[...], b_ref[...], dims)

        def finish(r):
            if mode == "relu2":
                outs[0][...] = r.astype(BF16)
                rr = jnp.maximum(r, 0.0)
                outs[1][...] = (rr * rr).astype(BF16)
            elif mode == "drelu2":
                outs[0][...] = (r * (2.0 * jnp.maximum(u_ref[...].astype(F32), 0.0))).astype(out_dtype)
            else:
                outs[0][...] = r.astype(out_dtype)

        if nk == 1:
            finish(part)
        else:
            acc = rest[n_out]
            kk = pl.program_id(2)

            @pl.when(kk == 0)
            def _():
                acc[...] = part

            @pl.when(kk > 0)
            def _():
                acc[...] += part

            @pl.when(kk == nk - 1)
            def _():
                finish(acc[...])

    in_specs = [a_spec, b_spec]
    args = [a, b]
    if mode == "drelu2":
        in_specs.append(o_spec)
        args.append(u)
    if mode == "relu2":
        out_shape = (jax.ShapeDtypeStruct((m, n), BF16), jax.ShapeDtypeStruct((m, n), BF16))
        out_specs = (o_spec, o_spec)
    else:
        out_shape = jax.ShapeDtypeStruct((m, n), out_dtype)
        out_specs = o_spec
    return pl.pallas_call(
        body, name=name, grid=(m // tm, n // tn, nk), in_specs=in_specs, out_specs=out_specs, out_shape=out_shape,
        scratch_shapes=[pltpu.VMEM((tm, tn), F32)] if nk > 1 else [],
        compiler_params=_cparams(("parallel", "parallel", "arbitrary")),
    )(*args)


def _norm_mod_fwd(x, nw, scale, shift, name, res=None, gate=None):
    s, d = x.shape
    row = pl.BlockSpec((ROW_TILE, d), lambda i: (i, 0))
    vec = pl.BlockSpec((1, d), lambda i: (0, 0))
    with_res = res is not None

    def body(*refs):
        if with_res:
            x_ref, res_ref, gate_ref, nw_ref, sc_ref, sh_ref, x1_ref, h_ref = refs
            xv = x_ref[...] + gate_ref[...] * res_ref[...]
            x1_ref[...] = xv
        else:
            x_ref, nw_ref, sc_ref, sh_ref, h_ref = refs
            xv = x_ref[...]
        r = lax.rsqrt(jnp.mean(xv * xv, axis=-1, keepdims=True) + EPS)
        h_ref[...] = ((xv * r) * nw_ref[...] * (1.0 + sc_ref[...]) + sh_ref[...]).astype(BF16)

    if with_res:
        in_specs = [row, row, vec, vec, vec, vec]
        args = (x, res, gate, nw, scale, shift)
        out_shape = (jax.ShapeDtypeStruct((s, d), F32), jax.ShapeDtypeStruct((s, d), BF16))
        out_specs = (row, row)
    else:
        in_specs = [row, vec, vec, vec]
        args = (x, nw, scale, shift)
        out_shape = jax.ShapeDtypeStruct((s, d), BF16)
        out_specs = row
    return pl.pallas_call(body, name=name, grid=(s // ROW_TILE,), in_specs=in_specs, out_specs=out_specs,
                          out_shape=out_shape, compiler_params=_cparams(("parallel",)))(*args)


def _norm_mod_bwd(dh, xin, dres, nw, scale, name, gate=None, mix=None):
    s, d = xin.shape
    row = pl.BlockSpec((ROW_TILE, d), lambda i: (i, 0))
    vec = pl.BlockSpec((1, d), lambda i: (0, 0))
    with_gate = gate is not None

    def body(*refs):
        if with_gate:
            dh_ref, x_ref, dres_ref, nw_ref, sc_ref, gate_ref, mix_ref, dx_ref, dsh_ref, dsc_ref, dnw_ref, dmix_ref, dg_ref = refs
        else:
            dh_ref, x_ref, dres_ref, nw_ref, sc_ref, dx_ref, dsh_ref, dsc_ref, dnw_ref = refs
        i = pl.program_id(0)

        @pl.when(i == 0)
        def _():
            dsh_ref[...] = jnp.zeros_like(dsh_ref)
            dsc_ref[...] = jnp.zeros_like(dsc_ref)
            dnw_ref[...] = jnp.zeros_like(dnw_ref)
            if with_gate:
                dg_ref[...] = jnp.zeros_like(dg_ref)

        xv = x_ref[...]
        dhv = dh_ref[...]
        r = lax.rsqrt(jnp.mean(xv * xv, axis=-1, keepdims=True) + EPS)
        nrm = xv * r
        one_sc = 1.0 + sc_ref[...]
        dhn = dhv * nrm
        dsh_ref[...] += jnp.sum(dhv, axis=0, keepdims=True)
        dsc_ref[...] += jnp.sum(dhn, axis=0, keepdims=True) * nw_ref[...]
        dnw_ref[...] += jnp.sum(dhn, axis=0, keepdims=True) * one_sc
        dn = dhv * (nw_ref[...] * one_sc)
        dx = dres_ref[...] + r * (dn - nrm * jnp.mean(dn * nrm, axis=-1, keepdims=True))
        dx_ref[...] = dx
        if with_gate:
            dmix_ref[...] = (gate_ref[...] * dx).astype(BF16)
            dg_ref[...] += jnp.sum(dx * mix_ref[...], axis=0, keepdims=True)

    vshape = jax.ShapeDtypeStruct((1, d), F32)
    in_specs = [row, row, row, vec, vec]
    args = [dh, xin, dres, nw, scale]
    out_shape = [jax.ShapeDtypeStruct((s, d), F32), vshape, vshape, vshape]
    out_specs = [row, vec, vec, vec]
    if with_gate:
        in_specs += [vec, row]
        args += [gate, mix]
        out_shape += [jax.ShapeDtypeStruct((s, d), BF16), vshape]
        out_specs += [row, vec]
    return pl.pallas_call(body, name=name, grid=(s // ROW_TILE,), in_specs=in_specs, out_specs=out_specs,
                          out_shape=out_shape, compiler_params=_cparams(("arbitrary",)))(*args)


def _loss_head(x1, ff, gate2, tgt):
    s, d = x1.shape
    row = pl.BlockSpec((ROW_TILE, d), lambda i: (i, 0))
    vec = pl.BlockSpec((1, d), lambda i: (0, 0))
    one = pl.BlockSpec((1, 1), lambda i: (0, 0))

    def body(x1_ref, ff_ref, g_ref, t_ref, loss_ref, dout_ref, dff_ref, dg_ref):
        i = pl.program_id(0)

        @pl.when(i == 0)
        def _():
            loss_ref[...] = jnp.zeros_like(loss_ref)
            dg_ref[...] = jnp.zeros_like(dg_ref)

        ffv = ff_ref[...]
        err = x1_ref[...] + g_ref[...] * ffv - t_ref[...]
        loss_ref[...] += (0.5 / d) * jnp.sum(err * err).reshape(1, 1)
        dout = err * (1.0 / d)
        dout_ref[...] = dout
        dff_ref[...] = (g_ref[...] * dout).astype(BF16)
        dg_ref[...] += jnp.sum(dout * ffv, axis=0, keepdims=True)

    return pl.pallas_call(
        body, name="loss_head", grid=(s // ROW_TILE,), in_specs=[row, row, vec, row],
        out_specs=[one, row, row, vec],
        out_shape=[jax.ShapeDtypeStruct((1, 1), F32), jax.ShapeDtypeStruct((s, d), F32),
                   jax.ShapeDtypeStruct((s, d), BF16), jax.ShapeDtypeStruct((1, d), F32)],
        compiler_params=_cparams(("arbitrary",)))(x1, ff, gate2, tgt)


CONV_COLS = 256
HALO = 8


def _shift_down(cur, halo, k):
    if k == 0:
        return cur
    rolled = pltpu.roll(cur, k, axis=0)
    top = jnp.where(lax.broadcasted_iota(jnp.int32, halo.shape, 0) < k, pltpu.roll(halo, k, axis=0), rolled[:HALO])
    return jnp.concatenate([top, rolled[HALO:]], axis=0)


def _shift_up(cur, halo, k):
    if k == 0:
        return cur
    t = cur.shape[0]
    rolled = pltpu.roll(cur, t - k, axis=0)
    bot = jnp.where(lax.broadcasted_iota(jnp.int32, halo.shape, 0) >= HALO - k, pltpu.roll(halo, HALO - k, axis=0),
                    rolled[t - HALO:])
    return jnp.concatenate([rolled[:t - HALO], bot], axis=0)


def _conv_fwd(proj, conv_w, conv_b):
    s = proj.shape[0]
    nr = s // ROW_TILE
    cb0 = OFF_XBC // CONV_COLS
    hb = ROW_TILE // HALO
    cur = pl.BlockSpec((ROW_TILE, CONV_COLS), lambda j, r: (r, cb0 + j))
    prev = pl.BlockSpec((HALO, CONV_COLS), lambda j, r: (jnp.maximum(r * hb - 1, 0), cb0 + j))
    out = pl.BlockSpec((ROW_TILE, CONV_COLS), lambda j, r: (r, j))

    def body(u_ref, up_ref, w_ref, b_ref, pre_ref, act_ref):
        r = pl.program_id(1)
        u = u_ref[...]
        halo = jnp.where(r > 0, up_ref[...], 0.0)
        acc = b_ref[...] + w_ref[CONV_K - 1:CONV_K, :] * u
        for k in range(1, CONV_K):
            acc = acc + w_ref[CONV_K - 1 - k:CONV_K - k, :] * _shift_down(u, halo, k)
        pre_ref[...] = acc
        act_ref[...] = acc * _sigmoid(acc)

    return pl.pallas_call(
        body, name="conv_fwd", grid=(CONV_CH // CONV_COLS, nr),
        in_specs=[cur, prev, pl.BlockSpec((CONV_K, CONV_COLS), lambda j, r: (0, j)),
                  pl.BlockSpec((1, CONV_COLS), lambda j, r: (0, j))],
        out_specs=[out, out],
        out_shape=[jax.ShapeDtypeStruct((s, CONV_CH), F32), jax.ShapeDtypeStruct((s, CONV_CH), F32)],
        compiler_params=_cparams(("parallel", "arbitrary")))(proj, proj, conv_w, conv_b)


def _conv_bwd(dact, pre, proj, conv_w):
    s = proj.shape[0]
    nr = s // ROW_TILE
    cb0 = OFF_XBC // CONV_COLS
    hb = ROW_TILE // HALO
    last_halo = s // HALO - 1
    cur = pl.BlockSpec((ROW_TILE, CONV_COLS), lambda j, r: (r, j))
    nxt = pl.BlockSpec((HALO, CONV_COLS), lambda j, r: (jnp.minimum((r + 1) * hb, last_halo), j))
    ucur = pl.BlockSpec((ROW_TILE, CONV_COLS), lambda j, r: (r, cb0 + j))
    uprev = pl.BlockSpec((HALO, CONV_COLS), lambda j, r: (jnp.maximum(r * hb - 1, 0), cb0 + j))
    wspec = pl.BlockSpec((CONV_K, CONV_COLS), lambda j, r: (0, j))
    bspec = pl.BlockSpec((1, CONV_COLS), lambda j, r: (0, j))

    def dsilu(p):
        sg = _sigmoid(p)
        return sg * (1.0 + p * (1.0 - sg))

    def body(da_ref, dan_ref, pre_ref, pren_ref, u_ref, up_ref, w_ref, du_ref, dw_ref, db_ref):
        r = pl.program_id(1)

        @pl.when(r == 0)
        def _():
            dw_ref[...] = jnp.zeros_like(dw_ref)
            db_ref[...] = jnp.zeros_like(db_ref)

        dpre = da_ref[...] * dsilu(pre_ref[...])
        dnext = jnp.where(r < nr - 1, dan_ref[...] * dsilu(pren_ref[...]), 0.0)
        u = u_ref[...]
        halo = jnp.where(r > 0, up_ref[...], 0.0)
        du = w_ref[CONV_K - 1:CONV_K, :] * dpre
        dws = [jnp.sum(dpre * u, axis=0, keepdims=True)]
        for k in range(1, CONV_K):
            du = du + w_ref[CONV_K - 1 - k:CONV_K - k, :] * _shift_up(dpre, dnext, k)
            dws.append(jnp.sum(dpre * _shift_down(u, halo, k), axis=0, keepdims=True))
        du_ref[...] = du.astype(BF16)
        dw_ref[...] += jnp.concatenate(dws[::-1], axis=0)
        db_ref[...] += jnp.sum(dpre, axis=0, keepdims=True)

    return pl.pallas_call(
        body, name="conv_bwd", grid=(CONV_CH // CONV_COLS, nr),
        in_specs=[cur, nxt, cur, nxt, ucur, uprev, wspec],
        out_specs=[cur, wspec, bspec],
        out_shape=[jax.ShapeDtypeStruct((s, CONV_CH), BF16), jax.ShapeDtypeStruct((CONV_K, CONV_CH), F32),
                   jax.ShapeDtypeStruct((1, CONV_CH), F32)],
        compiler_params=_cparams(("parallel", "arbitrary")))(dact, dact, pre, pre, proj, proj, conv_w)


def _ssd_common(dtr, dtb, alog):
    lane = lax.broadcasted_iota(jnp.int32, (1, LANE), 1)
    head_lane = lane < SSD_HEADS
    dt = jnp.where(head_lane, _softplus(dtr + dtb), 0.0)
    a = jnp.where(head_lane, -jnp.exp(alog), 0.0)
    row = lax.broadcasted_iota(jnp.int32, (SSD_CHUNK, SSD_CHUNK), 0)
    col = lax.broadcasted_iota(jnp.int32, (SSD_CHUNK, SSD_CHUNK), 1)
    tril = row >= col
    cs = _dot(tril.astype(F32), dt * a, NN, precision=HIGHEST)
    return dt, a, cs, cs.T, tril, lane


def _ssd_fwd(proj, act, dtb, alog, dsk, nw):
    s = proj.shape[0]
    nc = s // SSD_CHUNK
    bc_w = SSD_GROUPS * SSD_STATE

    def body(z_ref, dtr_ref, xs_ref, b_ref, c_ref, dtb_ref, alog_ref, dsk_ref, nw_ref,
             ypre_ref, yssd_ref, hall_ref, h_scr, y_scr):
        @pl.when(pl.program_id(0) == 0)
        def _():
            h_scr[...] = jnp.zeros_like(h_scr)

        dt, a, cs, cst, tril, lane = _ssd_common(dtr_ref[...], dtb_ref[...], alog_ref[...])
        for g in range(SSD_GROUPS):
            bg = b_ref[:, g * SSD_STATE:(g + 1) * SSD_STATE].astype(BF16)
            cg = c_ref[:, g * SSD_STATE:(g + 1) * SSD_STATE].astype(BF16)
            cb = _dot(cg, bg, NT)
            for e in range(HEADS_PER_GROUP):
                h = g * HEADS_PER_GROUP + e
                hs = slice(h * HEAD_DIM, (h + 1) * HEAD_DIM)
                cs_col = cs[:, h:h + 1]
                cs_last = cs[SSD_CHUNK - 1:SSD_CHUNK, h:h + 1]
                lm = jnp.exp(jnp.where(tril, cs_col - cst[h:h + 1, :], -1e30))
                xs_h = xs_ref[:, hs]
                xdt = xs_h * dt[:, h:h + 1]
                hprev = h_scr[h]
                hall_ref[0, hs, :] = hprev
                y = _dot((cb * lm).astype(BF16), xdt.astype(BF16), NN)
                y = y + jnp.exp(cs_col) * _dot(cg, hprev.astype(BF16), NT)
                y_scr[:, hs] = y + dsk_ref[:, h:h + 1] * xs_h
                xdec = (xdt * jnp.exp(cs_last - cs_col)).astype(BF16)
                h_scr[h] = hprev * jnp.exp(cs_last) + _dot(xdec, bg, TN)
        y = y_scr[...]
        ypre_ref[...] = y
        z = z_ref[...]
        yg = y * (z * _sigmoid(z))
        for g in range(SSD_GROUPS):
            gs = slice(g * GROUP_WIDTH, (g + 1) * GROUP_WIDTH)
            seg = yg[:, gs]
            r = lax.rsqrt(jnp.mean(seg * seg, axis=-1, keepdims=True) + EPS)
            yssd_ref[:, gs] = (seg * r * nw_ref[:, gs]).astype(BF16)

    row_d = lambda cb: pl.BlockSpec((SSD_CHUNK, SSD_D_INNER), lambda c: (c, cb))
    small = pl.BlockSpec((1, LANE), lambda c: (0, 0))
    return pl.pallas_call(
        body, name="ssd_fwd", grid=(nc,),
        in_specs=[row_d(OFF_Z // SSD_D_INNER),
                  pl.BlockSpec((SSD_CHUNK, LANE), lambda c: (c, OFF_DT // LANE)),
                  row_d(0),
                  pl.BlockSpec((SSD_CHUNK, bc_w), lambda c: (c, SSD_D_INNER // bc_w)),
                  pl.BlockSpec((SSD_CHUNK, bc_w), lambda c: (c, SSD_D_INNER // bc_w + 1)),
                  small, small, small, pl.BlockSpec((1, SSD_D_INNER), lambda c: (0, 0))],
        out_specs=[row_d(0), row_d(0), pl.BlockSpec((1, SSD_D_INNER, SSD_STATE), lambda c: (c, 0, 0))],
        out_shape=[jax.ShapeDtypeStruct((s, SSD_D_INNER), F32), jax.ShapeDtypeStruct((s, SSD_D_INNER), BF16),
                   jax.ShapeDtypeStruct((nc, SSD_D_INNER, SSD_STATE), F32)],
        scratch_shapes=[pltpu.VMEM((SSD_HEADS, HEAD_DIM, SSD_STATE), F32), pltpu.VMEM((SSD_CHUNK, SSD_D_INNER), F32)],
        compiler_params=_cparams(("arbitrary",)))(proj, proj, act, act, act, dtb, alog, dsk, nw)


def _ssd_bwd(dycat, ypre, proj, act, hall, dtb, alog, dsk, nw):
    s = proj.shape[0]
    nc = s // SSD_CHUNK
    bc_w = SSD_GROUPS * SSD_STATE

    def body(dy_ref, ypre_ref, z_ref, dtr_ref, xs_ref, b_ref, c_ref, hall_ref, dtb_ref, alog_ref, dsk_ref, nw_ref,
             dz_ref, dact_ref, ddtr_ref, da_ref, ddsk_ref, ddtb_ref, dnw_ref, dh_scr, dyh_scr):
        @pl.when(pl.program_id(0) == 0)
        def _():
            dh_scr[...] = jnp.zeros_like(dh_scr)
            da_ref[...] = jnp.zeros_like(da_ref)
            ddsk_ref[...] = jnp.zeros_like(ddsk_ref)
            ddtb_ref[...] = jnp.zeros_like(ddtb_ref)
            dnw_ref[...] = jnp.zeros_like(dnw_ref)

        z = z_ref[...]
        sg = _sigmoid(z)
        sz = z * sg
        ypre = ypre_ref[...]
        yg = ypre * sz
        dyg_parts = []
        for g in range(SSD_GROUPS):
            gs = slice(g * GROUP_WIDTH, (g + 1) * GROUP_WIDTH)
            seg = yg[:, gs]
            r = lax.rsqrt(jnp.mean(seg * seg, axis=-1, keepdims=True) + EPS)
            nrm = seg * r
            dyo = dy_ref[:, gs]
            dnw_ref[:, gs] += jnp.sum(dyo * nrm, axis=0, keepdims=True)
            dn = dyo * nw_ref[:, gs]
            dyg_parts.append(r * (dn - nrm * jnp.mean(dn * nrm, axis=-1, keepdims=True)))
        dyg = jnp.concatenate(dyg_parts, axis=1)
        dz_ref[...] = (dyg * ypre * (sg * (1.0 + z * (1.0 - sg)))).astype(BF16)
        dyh_scr[...] = dyg * sz

        dtr = dtr_ref[...]
        dt, a, cs, cst, tril, lane = _ssd_common(dtr, dtb_ref[...], alog_ref[...])
        sub = lax.broadcasted_iota(jnp.int32, (SSD_CHUNK, 1), 0)
        last_row = sub == SSD_CHUNK - 1
        dcs_col = jnp.zeros((SSD_CHUNK, LANE), F32)
        dcs_row = jnp.zeros((SSD_CHUNK, LANE), F32)
        ddt = jnp.zeros((SSD_CHUNK, LANE), F32)
        ddsk = jnp.zeros((1, LANE), F32)
        for g in range(SSD_GROUPS):
            bsl = slice(g * SSD_STATE, (g + 1) * SSD_STATE)
            bgf = b_ref[:, bsl]
            bg = bgf.astype(BF16)
            cg = c_ref[:, bsl].astype(BF16)
            cb = _dot(cg, bg, NT)
            dcb = jnp.zeros((SSD_CHUNK, SSD_CHUNK), F32)
            dbg = jnp.zeros((SSD_CHUNK, SSD_STATE), F32)
            dcg = jnp.zeros((SSD_CHUNK, SSD_STATE), F32)
            for e in range(HEADS_PER_GROUP):
                h = g * HEADS_PER_GROUP + e
                hs = slice(h * HEAD_DIM, (h + 1) * HEAD_DIM)
                dt_h = dt[:, h:h + 1]
                cs_col = cs[:, h:h + 1]
                cs_last = cs[SSD_CHUNK - 1:SSD_CHUNK, h:h + 1]
                ecs = jnp.exp(cs_col)
                decay = jnp.exp(cs_last - cs_col)
                cd = jnp.exp(cs_last)
                lm = jnp.exp(jnp.where(tril, cs_col - cst[h:h + 1, :], -1e30))
                gm = cb * lm
                gmb = gm.astype(BF16)
                xs_h = xs_ref[:, hs]
                xdt = xs_h * dt_h
                xdtb = xdt.astype(BF16)
                hprev = hall_ref[0, hs, :]
                hb = hprev.astype(BF16)
                dhn = dh_scr[h]
                dhb = dhn.astype(BF16)
                dyh = dyh_scr[:, hs]
                dyb = dyh.astype(BF16)
                w_off = _dot(cg, hb, NT)
                dyo = dyh * ecs
                dyob = dyo.astype(BF16)
                dcg = dcg + _dot(dyob, hb, NN)
                dh_y = _dot(dyob, cg, TN)
                dcs_c = jnp.sum(dyo * w_off, axis=-1, keepdims=True)
                dg = _dot(dyb, xdtb, NT)
                dxdt = _dot(gmb, dyb, TN)
                mm = dg * gm
                dcs_c = dcs_c + jnp.sum(mm, axis=-1, keepdims=True)
                dcs_r = jnp.sum(mm, axis=0, keepdims=True)
                dcb = dcb + dg * lm
                q = _dot(xdtb, dhb, NN)
                dbg = dbg + decay * q
                t1 = decay * jnp.sum(q * bgf, axis=-1, keepdims=True)
                dcs_c = dcs_c - t1
                dxdt = dxdt + decay * _dot(bg, dhb, NT)
                dlast = jnp.sum(t1).reshape(1, 1) + jnp.sum(dhn * hprev).reshape(1, 1) * cd
                dcs_c = dcs_c + jnp.where(last_row, dlast, 0.0)
                dh_scr[h] = dhn * cd + dh_y
                dact_ref[:, hs] = dxdt * dt_h + dsk_ref[:, h:h + 1] * dyh
                ddt_h = jnp.sum(dxdt * xs_h, axis=-1, keepdims=True)
                ddsk = ddsk + jnp.where(lane == h, jnp.sum(dyh * xs_h).reshape(1, 1), 0.0)
                dcs_col = dcs_col + jnp.where(lane == h, dcs_c, 0.0)
                dcs_row = dcs_row + jnp.where(sub == h, dcs_r, 0.0)
                ddt = ddt + jnp.where(lane == h, ddt_h, 0.0)
            dcbb = dcb.astype(BF16)
            dcg = dcg + _dot(dcbb, bg, NN)
            dbg = dbg + _dot(dcbb, cg, TN)
            dact_ref[:, SSD_D_INNER + g * SSD_STATE:SSD_D_INNER + (g + 1) * SSD_STATE] = dbg
            dact_ref[:, SSD_D_INNER + bc_w + g * SSD_STATE:SSD_D_INNER + bc_w + (g + 1) * SSD_STATE] = dcg
        dcs = dcs_col - dcs_row.T
        row = lax.broadcasted_iota(jnp.int32, (SSD_CHUNK, SSD_CHUNK), 0)
        col = lax.broadcasted_iota(jnp.int32, (SSD_CHUNK, SSD_CHUNK), 1)
        dda = _dot((col >= row).astype(F32), dcs, NN, precision=HIGHEST)
        ddt = ddt + dda * a
        da_ref[...] += jnp.sum(dda * dt, axis=0, keepdims=True)
        ddtr = jnp.where(lane < SSD_HEADS, ddt * _sigmoid(dtr + dtb_ref[...]), 0.0)
        ddtr_ref[...] = ddtr.astype(BF16)
        ddtb_ref[...] += jnp.sum(ddtr, axis=0, keepdims=True)
        ddsk_ref[...] += ddsk

    rev = lambda c: nc - 1 - c
    row_d = lambda cb: pl.BlockSpec((SSD_CHUNK, SSD_D_INNER), lambda c: (rev(c), cb))
    small = pl.BlockSpec((1, LANE), lambda c: (0, 0))
    wide = pl.BlockSpec((1, SSD_D_INNER), lambda c: (0, 0))
    small_shape = jax.ShapeDtypeStruct((1, LANE), F32)
    return pl.pallas_call(
        body, name="ssd_bwd", grid=(nc,),
        in_specs=[row_d(0), row_d(0), row_d(OFF_Z // SSD_D_INNER),
                  pl.BlockSpec((SSD_CHUNK, LANE), lambda c: (rev(c), OFF_DT // LANE)),
                  row_d(0),
                  pl.BlockSpec((SSD_CHUNK, bc_w), lambda c: (rev(c), SSD_D_INNER // bc_w)),
                  pl.BlockSpec((SSD_CHUNK, bc_w), lambda c: (rev(c), SSD_D_INNER // bc_w + 1)),
                  pl.BlockSpec((1, SSD_D_INNER, SSD_STATE), lambda c: (rev(c), 0, 0)),
                  small, small, small, wide],
        out_specs=[row_d(0), pl.BlockSpec((SSD_CHUNK, CONV_CH), lambda c: (rev(c), 0)),
                   pl.BlockSpec((SSD_CHUNK, LANE), lambda c: (rev(c), 0)), small, small, small, wide],
        out_shape=[jax.ShapeDtypeStruct((s, SSD_D_INNER), BF16), jax.ShapeDtypeStruct((s, CONV_CH), F32),
                   jax.ShapeDtypeStruct((s, LANE), BF16), small_shape, small_shape, small_shape,
                   jax.ShapeDtypeStruct((1, SSD_D_INNER), F32)],
        scratch_shapes=[pltpu.VMEM((SSD_HEADS, HEAD_DIM, SSD_STATE), F32), pltpu.VMEM((SSD_CHUNK, SSD_D_INNER), F32)],
        compiler_params=_cparams(("arbitrary",)))(dycat, ypre, proj, proj, act, act, act, hall, dtb, alog, dsk, nw)


def _head_mean_matrix():
    row = lax.broadcasted_iota(jnp.int32, (LANE, LANE), 0) // HEAD_DIM
    col = lax.broadcasted_iota(jnp.int32, (LANE, LANE), 1) // HEAD_DIM
    return (row == col).astype(F32)


def _head_sum(v, ones_bd):
    parts = [_dot(v[:, j * LANE:(j + 1) * LANE], ones_bd, NN, precision=HIGHEST) for j in range(v.shape[1] // LANE)]
    return jnp.concatenate(parts, axis=1)


def _qk_norm_fwd(proj, qw, kw):
    s = proj.shape[0]
    blk = lambda cb: pl.BlockSpec((ROW_TILE, ATT_D), lambda i: (i, cb))
    vec = pl.BlockSpec((1, ATT_D), lambda i: (0, 0))

    def body(q_ref, k_ref, v_ref, qw_ref, kw_ref, qn_ref, kn_ref, vb_ref):
        ones_bd = _head_mean_matrix()
        q = q_ref[...]
        rq = lax.rsqrt(_head_sum(q * q, ones_bd) * (1.0 / HEAD_DIM) + EPS)
        qn_ref[...] = ((q * rq * qw_ref[...]) * HEAD_DIM ** -0.5).astype(BF16)
        k = k_ref[...]
        rk = lax.rsqrt(_head_sum(k * k, ones_bd) * (1.0 / HEAD_DIM) + EPS)
        kn_ref[...] = (k * rk * kw_ref[...]).astype(BF16)
        vb_ref[...] = v_ref[...].astype(BF16)

    o = jax.ShapeDtypeStruct((s, ATT_D), BF16)
    return pl.pallas_call(
        body, name="qk_norm_fwd", grid=(s // ROW_TILE,),
        in_specs=[blk(OFF_Q // ATT_D), blk(OFF_K // ATT_D), blk(OFF_V // ATT_D), vec, vec],
        out_specs=[blk(0)] * 3, out_shape=[o, o, o],
        compiler_params=_cparams(("parallel",)))(proj, proj, proj, qw, kw)


def _qk_norm_bwd(dqs, dks, dvs, proj, qw, kw):
    s = proj.shape[0]
    blk = lambda cb: pl.BlockSpec((ROW_TILE, ATT_D), lambda i: (i, cb))
    vec = pl.BlockSpec((1, ATT_D), lambda i: (0, 0))

    def body(dq0, dq1, dq2, dk0, dk1, dk2, dv0, dv1, dv2, q_ref, k_ref, qw_ref, kw_ref,
             dq_ref, dk_ref, dv_ref, dqw_ref, dkw_ref):
        @pl.when(pl.program_id(0) == 0)
        def _():
            dqw_ref[...] = jnp.zeros_like(dqw_ref)
            dkw_ref[...] = jnp.zeros_like(dkw_ref)

        ones_bd = _head_mean_matrix()

        def back(dn_out, x, w, scale, dw_ref):
            r = lax.rsqrt(_head_sum(x * x, ones_bd) * (1.0 / HEAD_DIM) + EPS)
            nrm = x * r
            dw_ref[...] += jnp.sum(dn_out * nrm, axis=0, keepdims=True) * scale
            dn = dn_out * (w * scale)
            return r * (dn - nrm * (_head_sum(dn * nrm, ones_bd) * (1.0 / HEAD_DIM)))

        dqn = dq0[...].astype(F32) + dq1[...].astype(F32) + dq2[...].astype(F32)
        dq_ref[...] = back(dqn, q_ref[...], qw_ref[...], HEAD_DIM ** -0.5, dqw_ref).astype(BF16)
        dkn = dk0[...].astype(F32) + dk1[...].astype(F32) + dk2[...].astype(F32)
        dk_ref[...] = back(dkn, k_ref[...], kw_ref[...], 1.0, dkw_ref).astype(BF16)
        dv_ref[...] = (dv0[...].astype(F32) + dv1[...].astype(F32) + dv2[...].astype(F32)).astype(BF16)

    o = jax.ShapeDtypeStruct((s, ATT_D), BF16)
    ov = jax.ShapeDtypeStruct((1, ATT_D), F32)
    return pl.pallas_call(
        body, name="qk_norm_bwd", grid=(s // ROW_TILE,),
        in_specs=[blk(0)] * 9 + [blk(OFF_Q // ATT_D), blk(OFF_K // ATT_D), vec, vec],
        out_specs=[blk(0)] * 3 + [vec, vec], out_shape=[o, o, o, ov, ov],
        compiler_params=_cparams(("arbitrary",)))(*dqs, *dks, *dvs, proj, proj, qw, kw)


def _att_mask(q0, k0):
    qi = q0 + lax.broadcasted_iota(jnp.int32, (ATT_BLK, 2 * ATT_BLK), 0)
    kj = k0 + lax.broadcasted_iota(jnp.int32, (ATT_BLK, 2 * ATT_BLK), 1)
    dist = qi - kj
    return (dist >= 0) & (dist <= ATT_BLK)


def _att_fwd(qn, kn, vb, dil):
    s = qn.shape[0]
    ln = s // dil
    nb = ln // ATT_BLK
    assert nb >= 2
    view = lambda t: t.reshape(ln, dil * ATT_D)
    blk = pl.BlockSpec((ln, LANE), lambda i: (0, i))

    def body(q_ref, k_ref, v_ref, o_ref, lse_ref):
        def step(i, carry):
            q0 = pl.multiple_of(i * ATT_BLK, ATT_BLK)
            k0 = pl.multiple_of(jnp.maximum(i - 1, 0) * ATT_BLK, ATT_BLK)
            q = q_ref[pl.ds(q0, ATT_BLK), :]
            k = k_ref[pl.ds(k0, 2 * ATT_BLK), :]
            v = v_ref[pl.ds(k0, 2 * ATT_BLK), :]
            mask = _att_mask(q0, k0)
            outs, lses = [], []
            for hh in range(LANE // HEAD_DIM):
                hs = slice(hh * HEAD_DIM, (hh + 1) * HEAD_DIM)
                sc = jnp.where(mask, _dot(q[:, hs], k[:, hs], NT), -1e30)
                m = jnp.max(sc, axis=-1, keepdims=True)
                p = jnp.exp(sc - m)
                den = jnp.sum(p, axis=-1, keepdims=True)
                outs.append(_dot(p.astype(BF16), v[:, hs], NN) / den)
                lses.append(jnp.broadcast_to(m + jnp.log(den), (ATT_BLK, HEAD_DIM)))
            o_ref[pl.ds(q0, ATT_BLK), :] = jnp.concatenate(outs, axis=1).astype(BF16)
            lse_ref[pl.ds(q0, ATT_BLK), :] = jnp.concatenate(lses, axis=1)
            return carry

        lax.fori_loop(0, nb, step, 0)

    o, lse = pl.pallas_call(
        body, name=f"att_fwd_d{dil}", grid=(dil * ATT_D // LANE,), in_specs=[blk, blk, blk], out_specs=[blk, blk],
        out_shape=[jax.ShapeDtypeStruct((ln, dil * ATT_D), BF16), jax.ShapeDtypeStruct((ln, dil * ATT_D), F32)],
        compiler_params=_cparams(("parallel",)))(view(qn), view(kn), view(vb))
    return o.reshape(s, ATT_D), lse.reshape(s, ATT_D)


def _att_bwd(qn, kn, vb, do, lse, delta, dil):
    s = qn.shape[0]
    ln = s // dil
    nb = ln // ATT_BLK
    view = lambda t: t.reshape(ln, dil * ATT_D)
    blk = pl.BlockSpec((ln, LANE), lambda i: (0, i))

    def body(q_ref, k_ref, v_ref, do_ref, lse_ref, dl_ref, dq_ref, dk_ref, dv_ref, dk_acc, dv_acc):
        dk_acc[...] = jnp.zeros_like(dk_acc)
        dv_acc[...] = jnp.zeros_like(dv_acc)

        def step(i, carry):
            q0 = pl.multiple_of(i * ATT_BLK, ATT_BLK)
            k0 = pl.multiple_of(jnp.maximum(i - 1, 0) * ATT_BLK, ATT_BLK)
            q = q_ref[pl.ds(q0, ATT_BLK), :]
            dov = do_ref[pl.ds(q0, ATT_BLK), :]
            lsev = lse_ref[pl.ds(q0, ATT_BLK), :]
            dlv = dl_ref[pl.ds(q0, ATT_BLK), :]
            k = k_ref[pl.ds(k0, 2 * ATT_BLK), :]
            v = v_ref[pl.ds(k0, 2 * ATT_BLK), :]
            mask = _att_mask(q0, k0)
            dqs, dks, dvs = [], [], []
            for hh in range(LANE // HEAD_DIM):
                hs = slice(hh * HEAD_DIM, (hh + 1) * HEAD_DIM)
                c0 = hh * HEAD_DIM
                sc = jnp.where(mask, _dot(q[:, hs], k[:, hs], NT), -1e30)
                p = jnp.exp(sc - lsev[:, c0:c0 + 1])
                dp = _dot(dov[:, hs], v[:, hs], NT)
                ds = (p * (dp - dlv[:, c0:c0 + 1])).astype(BF16)
                dqs.append(_dot(ds, k[:, hs], NN))
                dks.append(_dot(ds, q[:, hs], TN))
                dvs.append(_dot(p.astype(BF16), dov[:, hs], TN))
            dq_ref[pl.ds(q0, ATT_BLK), :] = jnp.concatenate(dqs, axis=1).astype(BF16)
            dk_acc[pl.ds(k0, 2 * ATT_BLK), :] += jnp.concatenate(dks, axis=1)
            dv_acc[pl.ds(k0, 2 * ATT_BLK), :] += jnp.concatenate(dvs, axis=1)
            return carry

        lax.fori_loop(0, nb, step, 0)
        dk_ref[...] = dk_acc[...].astype(BF16)
        dv_ref[...] = dv_acc[...].astype(BF16)

    o = jax.ShapeDtypeStruct((ln, dil * ATT_D), BF16)
    dq, dk, dv = pl.pallas_call(
        body, name=f"att_bwd_d{dil}", grid=(dil * ATT_D // LANE,), in_specs=[blk] * 6, out_specs=[blk] * 3,
        out_shape=[o, o, o], scratch_shapes=[pltpu.VMEM((ln, LANE), F32), pltpu.VMEM((ln, LANE), F32)],
        compiler_params=_cparams(("parallel",)))(view(qn), view(kn), view(vb), view(do), view(lse), view(delta))
    return dq.reshape(s, ATT_D), dk.reshape(s, ATT_D), dv.reshape(s, ATT_D)


def _att_combine(os_, lses, nw):
    s = os_[0].shape[0]
    row = pl.BlockSpec((ROW_TILE, ATT_D), lambda i: (i, 0))
    vec = pl.BlockSpec((1, ATT_D), lambda i: (0, 0))

    def body(o0, o1, o2, l0, l1, l2, nw_ref, o_ref, lse_ref, y_ref):
        la, lb, lc = l0[...], l1[...], l2[...]
        m = jnp.maximum(jnp.maximum(la, lb), lc)
        ea, eb, ec = jnp.exp(la - m), jnp.exp(lb - m), jnp.exp(lc - m)
        den = ea + eb + ec
        o = (ea * o0[...].astype(F32) + eb * o1[...].astype(F32) + ec * o2[...].astype(F32)) / den
        o_ref[...] = o
        lse_ref[...] = m + jnp.log(den)
        r = lax.rsqrt(jnp.mean(o * o, axis=-1, keepdims=True) + EPS)
        y_ref[...] = (o * r * nw_ref[...]).astype(BF16)

    f = jax.ShapeDtypeStruct((s, ATT_D), F32)
    return pl.pallas_call(
        body, name="att_combine", grid=(s // ROW_TILE,), in_specs=[row] * 6 + [vec], out_specs=[row, row, row],
        out_shape=[f, f, jax.ShapeDtypeStruct((s, ATT_D), BF16)],
        compiler_params=_cparams(("parallel",)))(*os_, *lses, nw)


def _att_norm_bwd(dycat, o, nw):
    s = o.shape[0]
    row = pl.BlockSpec((ROW_TILE, ATT_D), lambda i: (i, 0))
    vec = pl.BlockSpec((1, ATT_D), lambda i: (0, 0))

    def body(dy_ref, o_ref, nw_ref, do_ref, dl_ref, dnw_ref):
        @pl.when(pl.program_id(0) == 0)
        def _():
            dnw_ref[...] = jnp.zeros_like(dnw_ref)

        o = o_ref[...]
        dy = dy_ref[...]
        r = lax.rsqrt(jnp.mean(o * o, axis=-1, keepdims=True) + EPS)
        nrm = o * r
        dnw_ref[...] += jnp.sum(dy * nrm, axis=0, keepdims=True)
        dn = dy * nw_ref[...]
        do = r * (dn - nrm * jnp.mean(dn * nrm, axis=-1, keepdims=True))
        do_ref[...] = do.astype(BF16)
        dl_ref[...] = _head_sum(do * o, _head_mean_matrix())

    return pl.pallas_call(
        body, name="att_norm_bwd", grid=(s // ROW_TILE,),
        in_specs=[pl.BlockSpec((ROW_TILE, ATT_D), lambda i: (i, 1)), row, vec], out_specs=[row, row, vec],
        out_shape=[jax.ShapeDtypeStruct((s, ATT_D), BF16), jax.ShapeDtypeStruct((s, ATT_D), F32),
                   jax.ShapeDtypeStruct((1, ATT_D), F32)],
        compiler_params=_cparams(("arbitrary",)))(dycat, o, nw)


def _ada_fwd(c_all, w_ada):
    def body(c_ref, w_ref, o_ref):
        cv = c_ref[...]
        o_ref[...] = _dot((cv * _sigmoid(cv)).astype(BF16), w_ref[...].astype(BF16), NN)

    return pl.pallas_call(body, name="ada_fwd", out_shape=jax.ShapeDtypeStruct((c_all.shape[0], w_ada.shape[1]), F32),
                          compiler_params=_cparams())(c_all, w_ada)


def _adamw_math(g, w, m, v):
    m_new = ADAM_B1 * m + (1.0 - ADAM_B1) * g
    v_new = ADAM_B2 * v + (1.0 - ADAM_B2) * (g * g)
    m_hat = m_new / (1.0 - ADAM_B1 ** ADAM_STEP)
    v_hat = v_new / (1.0 - ADAM_B2 ** ADAM_STEP)
    delta = -ADAM_LR * (m_hat / (jnp.sqrt(v_hat) + ADAM_EPS) + ADAM_WD * w)
    return delta, m_new, v_new


def _ada_bwd_adamw(c_all, dmod_cols, w, m, v):
    rows, cols = w.shape
    tr = 256
    blk = pl.BlockSpec((tr, cols), lambda i: (i, 0))

    def body(c_ref, d_ref, w_ref, m_ref, v_ref, g_ref, dl_ref, mo_ref, vo_ref):
        cv = c_ref[...]
        ca = cv * _sigmoid(cv)
        g = ca[:, 0:1] * d_ref[0:1, :]
        for b in range(1, N_DEV):
            g = g + ca[:, b:b + 1] * d_ref[b:b + 1, :]
        g_ref[...] = g
        dl_ref[...], mo_ref[...], vo_ref[...] = _adamw_math(g, w_ref[...], m_ref[...], v_ref[...])

    o = jax.ShapeDtypeStruct((rows, cols), F32)
    return pl.pallas_call(
        body, name="ada_bwd_adamw", grid=(rows // tr,),
        in_specs=[pl.BlockSpec((tr, N_DEV), lambda i: (i, 0)), pl.BlockSpec((N_DEV, cols), lambda i: (0, 0)), blk, blk, blk],
        out_specs=[blk] * 4, out_shape=[o, o, o, o], compiler_params=_cparams(("parallel",)))(c_all.T, dmod_cols, w, m, v)


def _reduce_adamw(slabs, w, m, v, name):
    rows, cols = w.shape
    tr = 128
    blk = pl.BlockSpec((tr, cols), lambda i: (i, 0))

    def body(s_ref, w_ref, m_ref, v_ref, g_ref, dl_ref, mo_ref, vo_ref):
        g = s_ref[0].astype(F32)
        for dev in range(1, N_DEV):
            g = g + s_ref[dev].astype(F32)
        g_ref[...] = g
        dl_ref[...], mo_ref[...], vo_ref[...] = _adamw_math(g, w_ref[...], m_ref[...], v_ref[...])

    o = jax.ShapeDtypeStruct((rows, cols), F32)
    return pl.pallas_call(
        body, name=name, grid=(rows // tr,),
        in_specs=[pl.BlockSpec((N_DEV, tr, cols), lambda i: (0, i, 0)), blk, blk, blk],
        out_specs=[blk] * 4, out_shape=[o, o, o, o], compiler_params=_cparams(("parallel",)))(slabs, w, m, v)


def _small_reduce_adamw(gathered, w, m, v):
    def body(s_ref, w_ref, m_ref, v_ref, g_ref, dl_ref, mo_ref, vo_ref):
        g = s_ref[0]
        for dev in range(1, N_DEV):
            g = g + s_ref[dev]
        g_ref[...] = g
        dl_ref[...], mo_ref[...], vo_ref[...] = _adamw_math(g, w_ref[...], m_ref[...], v_ref[...])

    o = jax.ShapeDtypeStruct(w.shape, F32)
    return pl.pallas_call(body, name="small_reduce_adamw", out_shape=[o, o, o, o], compiler_params=_cparams())(gathered, w, m, v)


def _adamw_small(g, w, m, v, name):
    def body(g_ref, w_ref, m_ref, v_ref, dl_ref, mo_ref, vo_ref):
        dl_ref[...], mo_ref[...], vo_ref[...] = _adamw_math(g_ref[...], w_ref[...], m_ref[...], v_ref[...])

    o = jax.ShapeDtypeStruct(w.shape, F32)
    return pl.pallas_call(body, name=name, out_shape=[o, o, o], compiler_params=_cparams())(g, w, m, v)


def _exchange(arrs, name, scatter):
    n = len(arrs)
    hbm = pl.BlockSpec(memory_space=pltpu.HBM)

    def body(*refs):
        ins, outs = refs[:n], refs[n:2 * n]
        send_sems, recv_sems, local_sems = refs[2 * n:]
        x, y, c = lax.axis_index("x"), lax.axis_index("y"), lax.axis_index("c")
        me = 4 * x + 2 * y + c
        local, remote = [], []
        for a in range(n):
            src_mine = ins[a].at[me] if scatter else ins[a]
            cp = pltpu.make_async_copy(src_mine, outs[a].at[me], local_sems.at[a])
            cp.start()
            local.append(cp)
            for k in range(1, N_DEV):
                px = 1 - x if k & 4 else x
                py = 1 - y if k & 2 else y
                pc = 1 - c if k & 1 else c
                peer = 4 * px + 2 * py + pc
                sem = a * (N_DEV - 1) + k - 1
                send = pltpu.make_async_remote_copy(
                    src_ref=ins[a].at[peer] if scatter else ins[a], dst_ref=outs[a].at[me],
                    send_sem=send_sems.at[sem], recv_sem=recv_sems.at[sem], device_id=(px, py, pc), device_id_type=MESH_IDS)
                send.start()
                recv = pltpu.make_async_remote_copy(
                    src_ref=ins[a].at[peer] if scatter else ins[a], dst_ref=outs[a].at[peer],
                    send_sem=send_sems.at[sem], recv_sem=recv_sems.at[sem], device_id=(px, py, pc), device_id_type=MESH_IDS)
                remote.append((send, recv))
        for send, recv in remote:
            send.wait_send()
            recv.wait_recv()
        for cp in local:
            cp.wait()

    out_shape = [jax.ShapeDtypeStruct(a.shape if scatter else (N_DEV,) + a.shape, a.dtype) for a in arrs]
    return pl.pallas_call(
        body, name=name, in_specs=[hbm] * n, out_specs=[hbm] * n, out_shape=out_shape,
        scratch_shapes=[pltpu.SemaphoreType.DMA((n * (N_DEV - 1),)), pltpu.SemaphoreType.DMA((n * (N_DEV - 1),)),
                        pltpu.SemaphoreType.DMA((n,))],
    )(*arrs)


def _pad_lanes(v, width=LANE):
    return jnp.pad(v, ((0, 0), (0, width - v.shape[1])))


def _local_step(x, tgt, mod, w_in_p, conv_w, conv_b, dt_bias, a_log, d_skip, ssd_norm_w, q_norm_w, k_norm_w,
                attn_norm_w, w_out, w_ff1, w_ff2, norm1_w, norm2_w):
    shift1, scale1, gate1, shift2, scale2, gate2 = [mod[i:i + 1] for i in range(N_MOD)]
    dtb, alog, dsk = _pad_lanes(dt_bias), _pad_lanes(a_log), _pad_lanes(d_skip)
    qw, kw = jnp.tile(q_norm_w, (1, ATT_HEADS)), jnp.tile(k_norm_w, (1, ATT_HEADS))

    h1 = _norm_mod_fwd(x, norm1_w, scale1, shift1, "norm1_fwd")
    proj = _matmul(h1, w_in_p, tm=1024, tn=896, tk=1024, name="in_proj")
    pre, act = _conv_fwd(proj, conv_w, conv_b)
    ypre, y_ssd, hall = _ssd_fwd(proj, act, dtb, alog, dsk, ssd_norm_w)
    qn, kn, vb = _qk_norm_fwd(proj, qw, kw)
    branches = [_att_fwd(qn, kn, vb, dil) for dil in DILATIONS]
    o_att, lse, y_att = _att_combine([b[0] for b in branches], [b[1] for b in branches], attn_norm_w)
    ycat = jnp.concatenate([y_ssd, y_att], axis=1)
    mix = _matmul(ycat, w_out, tm=1024, tn=1024, tk=2048, name="out_proj")
    x1, h2 = _norm_mod_fwd(x, norm2_w, scale2, shift2, "norm2_fwd", res=mix, gate=gate1)
    u, act_ff = _matmul(h2, w_ff1, tm=1024, tn=1024, tk=1024, name="ff1", mode="relu2")
    ff = _matmul(act_ff, w_ff2, tm=1024, tn=1024, tk=2048, name="ff2")
    loss, dout, dff, dgate2 = _loss_head(x1, ff, gate2, tgt)

    du = _matmul(dff, w_ff2, tb=True, tm=1024, tn=1024, tk=1024, out_dtype=BF16, name="ff2_dx", mode="drelu2", u=u)
    g_ff2 = _matmul(act_ff, dff, ta=True, tm=1024, tn=1024, tk=1024, name="ff2_dw")
    dh2 = _matmul(du, w_ff1, tb=True, tm=1024, tn=1024, tk=2048, name="ff1_dx")
    g_ff1 = _matmul(h2, du, ta=True, tm=1024, tn=1024, tk=1024, name="ff1_dw")
    dx1, dshift2, dscale2, g_norm2, dmix, dgate1 = _norm_mod_bwd(dh2, x1, dout, norm2_w, scale2, "norm2_bwd", gate=gate1, mix=mix)

    dycat = _matmul(dmix, w_out, tb=True, tm=1024, tn=1024, tk=1024, name="out_proj_dx")
    g_out = _matmul(ycat, dmix, ta=True, tm=1024, tn=1024, tk=1024, name="out_proj_dw")
    do, delta, g_attn_norm = _att_norm_bwd(dycat, o_att, attn_norm_w)
    grads = [_att_bwd(qn, kn, vb, do, lse, delta, dil) for dil in DILATIONS]
    dq, dk, dv, dqw, dkw = _qk_norm_bwd([g[0] for g in grads], [g[1] for g in grads], [g[2] for g in grads], proj, qw, kw)
    dz, dact, ddtr, da, g_dsk, g_dtb, g_ssd_norm = _ssd_bwd(dycat, ypre, proj, act, hall, dtb, alog, dsk, ssd_norm_w)
    dxbc, g_conv_w, g_conv_b = _conv_bwd(dact, pre, proj, conv_w)
    dproj = jnp.concatenate([dz, dxbc, dq, dk, dv, ddtr], axis=1)
    dh1 = _matmul(dproj, w_in_p, tb=True, tm=1024, tn=1024, tk=896, name="in_proj_dx")
    g_in_p = _matmul(h1, dproj, ta=True, tm=1024, tn=896, tk=1024, name="in_proj_dw")
    grad_x, dshift1, dscale1, g_norm1 = _norm_mod_bwd(dh1, x, dx1, norm1_w, scale1, "norm1_bwd")

    dmod = jnp.concatenate([dshift1, dscale1, dgate1, dshift2, dscale2, dgate2], axis=0)
    g_alog = da[:, :SSD_HEADS] * (-jnp.exp(a_log))
    g_qw = dqw.reshape(ATT_HEADS, HEAD_DIM).sum(axis=0, keepdims=True)
    g_kw = dkw.reshape(ATT_HEADS, HEAD_DIM).sum(axis=0, keepdims=True)
    return dict(loss=loss, grad_x=grad_x, dmod=dmod, norm1_w=g_norm1, norm2_w=g_norm2, w_in_p=g_in_p, conv_w=g_conv_w,
                conv_b=g_conv_b, dt_bias=g_dtb[:, :SSD_HEADS], a_log=g_alog, d_skip=g_dsk[:, :SSD_HEADS],
                ssd_norm_w=g_ssd_norm, q_norm_w=g_qw, k_norm_w=g_kw, attn_norm_w=g_attn_norm, w_out=g_out,
                w_ff1=g_ff1, w_ff2=g_ff2)


def _pack_w_in(w_full):
    o_dt = SSD_D_INNER + CONV_CH
    o_q = o_dt + SSD_HEADS
    pad = jnp.zeros((w_full.shape[0], LANE - SSD_HEADS), w_full.dtype)
    return jnp.concatenate([w_full[:, :o_dt], w_full[:, o_q:], w_full[:, o_dt:o_q], pad], axis=1)


def _unpack_w_in(g_p):
    return jnp.concatenate([g_p[:, :OFF_Q], g_p[:, OFF_DT:OFF_DT + SSD_HEADS], g_p[:, OFF_Q:OFF_DT]], axis=1)


MISC_FIELDS = (("dt_bias", SSD_HEADS), ("a_log", SSD_HEADS), ("d_skip", SSD_HEADS), ("q_norm_w", HEAD_DIM), ("k_norm_w", HEAD_DIM))
SMALL_LAYOUT = (("b_ada", 6), ("norm1_w", 1), ("norm2_w", 1), ("conv_w", 8), ("conv_b", 2), ("ssd_norm_w", 1),
                ("attn_norm_w", 1), ("misc", 1))


def _pack_small(vals):
    rows = []
    for name, nrow in SMALL_LAYOUT:
        if name == "misc":
            misc = jnp.concatenate([vals[f].reshape(1, n) for f, n in MISC_FIELDS], axis=1)
            rows.append(_pad_lanes(misc, D_MODEL))
        elif name in vals:
            rows.append(vals[name].reshape(nrow, D_MODEL))
        else:
            rows.append(jnp.zeros((nrow, D_MODEL), F32))
    used = sum(n for _, n in SMALL_LAYOUT)
    rows.append(jnp.zeros((SMALL_ROWS - used, D_MODEL), F32))
    return jnp.concatenate(rows, axis=0)


def _unpack_small(packed):
    out, r = {}, 0
    for name, nrow in SMALL_LAYOUT:
        blk = packed[r:r + nrow]
        r += nrow
        if name == "misc":
            c0 = 0
            for f, n in MISC_FIELDS:
                out[f] = blk[:, c0:c0 + n]
                c0 += n
        elif name == "b_ada":
            out[name] = blk.reshape(1, N_MOD * D_MODEL)
        elif name == "conv_w":
            out[name] = blk.reshape(CONV_K, CONV_CH)
        elif name == "conv_b":
            out[name] = blk.reshape(1, CONV_CH)
        else:
            out[name] = blk
    return out


WEIGHT_NAMES = ("norm1_w", "norm2_w", "w_ada", "b_ada", "w_in", "conv_w", "conv_b", "dt_bias", "a_log", "d_skip",
                "ssd_norm_w", "q_norm_w", "k_norm_w", "attn_norm_w", "w_out", "w_ff1", "w_ff2")
SMALL_NAMES = ("norm1_w", "norm2_w", "b_ada", "conv_b", "dt_bias", "a_log", "d_skip", "ssd_norm_w", "q_norm_w",
               "k_norm_w", "attn_norm_w")


def kernel(x, c, norm1_w, norm2_w, w_ada, b_ada, w_in, conv_w, conv_b, dt_bias, a_log, d_skip, ssd_norm_w, q_norm_w, k_norm_w, attn_norm_w, w_out, w_ff1, w_ff2, loss_target, m_norm1_w, m_norm2_w, m_w_ada, m_b_ada, m_w_in, m_conv_w, m_conv_b, m_dt_bias, m_a_log, m_d_skip, m_ssd_norm_w, m_q_norm_w, m_k_norm_w, m_attn_norm_w, m_w_out, m_w_ff1, m_w_ff2, v_norm1_w, v_norm2_w, v_w_ada, v_b_ada, v_w_in, v_conv_w, v_conv_b, v_dt_bias, v_a_log, v_d_skip, v_ssd_norm_w, v_q_norm_w, v_k_norm_w, v_attn_norm_w, v_w_out, v_w_ff1, v_w_ff2):
    args = dict(locals())
    w = {n: args[n] for n in WEIGHT_NAMES}
    m = {n: args["m_" + n] for n in WEIGHT_NAMES}
    v = {n: args["v_" + n] for n in WEIGHT_NAMES}
    me = 4 * lax.axis_index("x") + 2 * lax.axis_index("y") + lax.axis_index("c")

    c_rows = jnp.pad(c, ((0, 7), (0, 0)))
    c_g, conv_g, w_in_g, w_out_g, w_ff1_g, w_ff2_g = _exchange(
        [c_rows, w["conv_w"][0], w["w_in"][0].astype(BF16), w["w_out"][0].astype(BF16), w["w_ff1"][0].astype(BF16),
         w["w_ff2"][0].astype(BF16)], "gather_weights", scatter=False)
    c_all = c_g[:, 0, :]
    conv_full = jnp.transpose(conv_g, (1, 0, 2)).reshape(CONV_K, CONV_CH)
    w_in_p = _pack_w_in(jnp.transpose(w_in_g, (1, 0, 2)).reshape(D_MODEL, IN_W))
    w_out_f = w_out_g.reshape(2 * D_MODEL, D_MODEL)
    w_ff1_f = jnp.transpose(w_ff1_g, (1, 0, 2)).reshape(D_MODEL, D_FF)
    w_ff2_f = w_ff2_g.reshape(D_FF, D_MODEL)

    mod_part = _ada_fwd(c_all, w["w_ada"][0])
    (mod_g,) = _exchange([mod_part], "gather_mod", scatter=False)
    mod_mine = lax.dynamic_index_in_dim(mod_g, me, axis=1, keepdims=False).reshape(1, N_MOD * D_MODEL) + w["b_ada"]
    mod = mod_mine.reshape(N_MOD, D_MODEL)

    res = _local_step(x[0], loss_target[0], mod, w_in_p, conv_full, w["conv_b"], w["dt_bias"], w["a_log"], w["d_skip"],
                      w["ssd_norm_w"], w["q_norm_w"], w["k_norm_w"], w["attn_norm_w"], w_out_f, w_ff1_f, w_ff2_f,
                      w["norm1_w"], w["norm2_w"])

    g_in = _unpack_w_in(res["w_in_p"]).astype(BF16).reshape(D_MODEL, N_DEV, IN_W // N_DEV).transpose(1, 0, 2)
    g_out = res["w_out"].astype(BF16).reshape(N_DEV, 2 * D_MODEL // N_DEV, D_MODEL)
    g_ff1 = res["w_ff1"].astype(BF16).reshape(D_MODEL, N_DEV, D_FF // N_DEV).transpose(1, 0, 2)
    g_ff2 = res["w_ff2"].astype(BF16).reshape(N_DEV, D_FF // N_DEV, D_MODEL)
    small_vals = {n: res[n] for n in SMALL_NAMES if n != "b_ada"}
    small_vals["b_ada"] = res["dmod"]
    small_vals["conv_w"] = res["conv_w"]
    s_in, s_out, s_ff1, s_ff2 = _exchange([g_in, g_out, g_ff1, g_ff2], "scatter_grads", scatter=True)
    (small_g,) = _exchange([_pack_small(small_vals)], "gather_small", scatter=False)

    grads, delta, new_m, new_v = {}, {}, {}, {}
    for name, slabs in (("w_in", s_in), ("w_out", s_out), ("w_ff1", s_ff1), ("w_ff2", s_ff2)):
        outs = _reduce_adamw(slabs, w[name][0], m[name][0], v[name][0], "adamw_" + name)
        grads[name], delta[name], new_m[name], new_v[name] = [o[None] for o in outs]

    sm = _small_reduce_adamw(small_g, _pack_small({n: w[n] for n in SMALL_NAMES}), _pack_small({n: m[n] for n in SMALL_NAMES}),
                             _pack_small({n: v[n] for n in SMALL_NAMES}))
    sm = [_unpack_small(p) for p in sm]
    for n in SMALL_NAMES:
        grads[n], delta[n], new_m[n], new_v[n] = [p[n] for p in sm]
    shard_w = CONV_CH // N_DEV
    g_conv = lax.dynamic_slice_in_dim(sm[0]["conv_w"], me * shard_w, shard_w, axis=1)
    cw = _adamw_small(g_conv, w["conv_w"][0], m["conv_w"][0], v["conv_w"][0], "adamw_conv_w")
    grads["conv_w"] = g_conv[None]
    delta["conv_w"], new_m["conv_w"], new_v["conv_w"] = [o[None] for o in cw]

    ada_w = w_ada.shape[2]
    dmod_all = small_g[:, :N_MOD, :].reshape(N_DEV, N_MOD * D_MODEL)
    dmod_cols = lax.dynamic_slice_in_dim(dmod_all, me * ada_w, ada_w, axis=1)
    outs = _ada_bwd_adamw(c_all, dmod_cols, w["w_ada"][0], m["w_ada"][0], v["w_ada"][0])
    grads["w_ada"], delta["w_ada"], new_m["w_ada"], new_v["w_ada"] = [o[None] for o in outs]

    loss = lax.psum(res["loss"][0, 0], ("x", "y", "c"))
    return (loss, res["grad_x"][None], *[grads[n] for n in WEIGHT_NAMES], *[delta[n] for n in WEIGHT_NAMES],
            *[new_m[n] for n in WEIGHT_NAMES], *[new_v[n] for n in WEIGHT_NAMES])
```

```python
import functools

import jax
import jax.numpy as jnp
from jax import lax
from jax.experimental import pallas as pl
from jax.experimental.pallas import tpu as pltpu

F32 = jnp.float32
BF16 = jnp.bfloat16
HIGHEST = lax.Precision.HIGHEST
MESH_IDS = pl.DeviceIdType.MESH

N_DEV = 8
D_MODEL = 1024
HEAD_DIM = 64
SSD_HEADS = 16
SSD_GROUPS = 4
HEADS_PER_GROUP = SSD_HEADS // SSD_GROUPS
SSD_STATE = 128
SSD_CHUNK = 128
SSD_D_INNER = SSD_HEADS * HEAD_DIM
GROUP_WIDTH = SSD_D_INNER // SSD_GROUPS
CONV_K = 4
CONV_CH = SSD_D_INNER + 2 * SSD_GROUPS * SSD_STATE
ATT_HEADS = 16
ATT_D = ATT_HEADS * HEAD_DIM
ATT_BLK = 128
DILATIONS = (1, 4, 16)
D_FF = 4 * D_MODEL
N_MOD = 6
EPS = 1e-6
IN_W = SSD_D_INNER + CONV_CH + SSD_HEADS + 3 * ATT_D
LANE = 128
OFF_Z, OFF_XBC, OFF_Q, OFF_K, OFF_V, OFF_DT = 0, 1024, 3072, 4096, 5120, 6144
IN_WP = OFF_DT + LANE

ADAM_LR, ADAM_B1, ADAM_B2, ADAM_EPS, ADAM_WD, ADAM_STEP = 0.001, 0.9, 0.999, 1e-08, 0.01, 10
VMEM_LIMIT = 56 * 1024 * 1024
ROW_TILE = 512
SMALL_ROWS = 24


def _cparams(sem=None):
    return pltpu.CompilerParams(dimension_semantics=sem, vmem_limit_bytes=VMEM_LIMIT)


def _sigmoid(v):
    return 1.0 / (1.0 + jnp.exp(-v))


def _softplus(v):
    y = jnp.exp(-jnp.abs(v))
    small = y * (1.0 - y * (0.5 - y * (1.0 / 3.0)))
    return jnp.maximum(v, 0.0) + jnp.where(y < 0.01, small, jnp.log(1.0 + y))


def _dot(a, b, dims, precision=None):
    return lax.dot_general(a, b, (dims, ((), ())), preferred_element_type=F32, precision=precision)


NN = ((1,), (0,))
NT = ((1,), (1,))
TN = ((0,), (0,))


def _matmul(a, b, *, ta=False, tb=False, tm, tn, tk, out_dtype=F32, name, mode=None, u=None):
    m, k = (a.shape[1], a.shape[0]) if ta else a.shape
    n = b.shape[0] if tb else b.shape[1]
    assert m % tm == 0 and n % tn == 0 and k % tk == 0, (name, m, n, k)
    nk = k // tk
    a_spec = pl.BlockSpec((tk, tm), lambda i, j, kk: (kk, i)) if ta else pl.BlockSpec((tm, tk), lambda i, j, kk: (i, kk))
    b_spec = pl.BlockSpec((tn, tk), lambda i, j, kk: (j, kk)) if tb else pl.BlockSpec((tk, tn), lambda i, j, kk: (kk, j))
    o_spec = pl.BlockSpec((tm, tn), lambda i, j, kk: (i, j))
    dims = ((0,) if ta else (1,), (1,) if tb else (0,))
    n_out = 2 if mode == "relu2" else 1

    def body(*refs):
        if mode == "drelu2":
            a_ref, b_ref, u_ref = refs[:3]
            rest = refs[3:]
        else:
            a_ref, b_ref = refs[:2]
            u_ref = None
            rest = refs[2:]
        outs = rest[:n_out]
        part = _dot(a_ref[...], b_ref[...], dims)

        def finish(r):
            if mode == "relu2":
                outs[0][...] = r.astype(BF16)
                rr = jnp.maximum(r, 0.0)
                outs[1][...] = (rr * rr).astype(BF16)
            elif mode == "drelu2":
                outs[0][...] = (r * (2.0 * jnp.maximum(u_ref[...].astype(F32), 0.0))).astype(out_dtype)
            else:
                outs[0][...] = r.astype(out_dtype)

        if nk == 1:
            finish(part)
        else:
            acc = rest[n_out]
            kk = pl.program_id(2)

            @pl.when(kk == 0)
            def _():
                acc[...] = part

            @pl.when(kk > 0)
            def _():
                acc[...] += part

            @pl.when(kk == nk - 1)
            def _():
                finish(acc[...])

    in_specs = [a_spec, b_spec]
    args = [a, b]
    if mode == "drelu2":
        in_specs.append(o_spec)
        args.append(u)
    if mode == "relu2":
        out_shape = (jax.ShapeDtypeStruct((m, n), BF16), jax.ShapeDtypeStruct((m, n), BF16))
        out_specs = (o_spec, o_spec)
    else:
        out_shape = jax.ShapeDtypeStruct((m, n), out_dtype)
        out_specs = o_spec
    return pl.pallas_call(
        body, name=name, grid=(m // tm, n // tn, nk), in_specs=in_specs, out_specs=out_specs, out_shape=out_shape,
        scratch_shapes=[pltpu.VMEM((tm, tn), F32)] if nk > 1 else [],
        compiler_params=_cparams(("parallel", "parallel", "arbitrary")),
    )(*args)


def _norm_mod_fwd(x, nw, scale, shift, name, res=None, gate=None):
    s, d = x.shape
    row = pl.BlockSpec((ROW_TILE, d), lambda i: (i, 0))
    vec = pl.BlockSpec((1, d), lambda i: (0, 0))
    with_res = res is not None

    def body(*refs):
        if with_res:
            x_ref, res_ref, gate_ref, nw_ref, sc_ref, sh_ref, x1_ref, h_ref = refs
            xv = x_ref[...] + gate_ref[...] * res_ref[...]
            x1_ref[...] = xv
        else:
            x_ref, nw_ref, sc_ref, sh_ref, h_ref = refs
            xv = x_ref[...]
        r = lax.rsqrt(jnp.mean(xv * xv, axis=-1, keepdims=True) + EPS)
        h_ref[...] = ((xv * r) * nw_ref[...] * (1.0 + sc_ref[...]) + sh_ref[...]).astype(BF16)

    if with_res:
        in_specs = [row, row, vec, vec, vec, vec]
        args = (x, res, gate, nw, scale, shift)
        out_shape = (jax.ShapeDtypeStruct((s, d), F32), jax.ShapeDtypeStruct((s, d), BF16))
        out_specs = (row, row)
    else:
        in_specs = [row, vec, vec, vec]
        args = (x, nw, scale, shift)
        out_shape = jax.ShapeDtypeStruct((s, d), BF16)
        out_specs = row
    return pl.pallas_call(body, name=name, grid=(s // ROW_TILE,), in_specs=in_specs, out_specs=out_specs,
                          out_shape=out_shape, compiler_params=_cparams(("parallel",)))(*args)


def _norm_mod_bwd(dh, xin, dres, nw, scale, name, gate=None, mix=None):
    s, d = xin.shape
    row = pl.BlockSpec((ROW_TILE, d), lambda i: (i, 0))
    vec = pl.BlockSpec((1, d), lambda i: (0, 0))
    with_gate = gate is not None

    def body(*refs):
        if with_gate:
            dh_ref, x_ref, dres_ref, nw_ref, sc_ref, gate_ref, mix_ref, dx_ref, dsh_ref, dsc_ref, dnw_ref, dmix_ref, dg_ref = refs
        else:
            dh_ref, x_ref, dres_ref, nw_ref, sc_ref, dx_ref, dsh_ref, dsc_ref, dnw_ref = refs
        i = pl.program_id(0)

        @pl.when(i == 0)
        def _():
            dsh_ref[...] = jnp.zeros_like(dsh_ref)
            dsc_ref[...] = jnp.zeros_like(dsc_ref)
            dnw_ref[...] = jnp.zeros_like(dnw_ref)
            if with_gate:
                dg_ref[...] = jnp.zeros_like(dg_ref)

        xv = x_ref[...]
        dhv = dh_ref[...]
        r = lax.rsqrt(jnp.mean(xv * xv, axis=-1, keepdims=True) + EPS)
        nrm = xv * r
        one_sc = 1.0 + sc_ref[...]
        dhn = dhv * nrm
        dsh_ref[...] += jnp.sum(dhv, axis=0, keepdims=True)
        dsc_ref[...] += jnp.sum(dhn, axis=0, keepdims=True) * nw_ref[...]
        dnw_ref[...] += jnp.sum(dhn, axis=0, keepdims=True) * one_sc
        dn = dhv * (nw_ref[...] * one_sc)
        dx = dres_ref[...] + r * (dn - nrm * jnp.mean(dn * nrm, axis=-1, keepdims=True))
        dx_ref[...] = dx
        if with_gate:
            dmix_ref[...] = (gate_ref[...] * dx).astype(BF16)
            dg_ref[...] += jnp.sum(dx * mix_ref[...], axis=0, keepdims=True)

    vshape = jax.ShapeDtypeStruct((1, d), F32)
    in_specs = [row, row, row, vec, vec]
    args = [dh, xin, dres, nw, scale]
    out_shape = [jax.ShapeDtypeStruct((s, d), F32), vshape, vshape, vshape]
    out_specs = [row, vec, vec, vec]
    if with_gate:
        in_specs += [vec, row]
        args += [gate, mix]
        out_shape += [jax.ShapeDtypeStruct((s, d), BF16), vshape]
        out_specs += [row, vec]
    return pl.pallas_call(body, name=name, grid=(s // ROW_TILE,), in_specs=in_specs, out_specs=out_specs,
                          out_shape=out_shape, compiler_params=_cparams(("arbitrary",)))(*args)


def _loss_head(x1, ff, gate2, tgt):
    s, d = x1.shape
    row = pl.BlockSpec((ROW_TILE, d), lambda i: (i, 0))
    vec = pl.BlockSpec((1, d), lambda i: (0, 0))
    one = pl.BlockSpec((1, 1), lambda i: (0, 0))

    def body(x1_ref, ff_ref, g_ref, t_ref, loss_ref, dout_ref, dff_ref, dg_ref):
        i = pl.program_id(0)

        @pl.when(i == 0)
        def _():
            loss_ref[...] = jnp.zeros_like(loss_ref)
            dg_ref[...] = jnp.zeros_like(dg_ref)

        ffv = ff_ref[...]
        err = x1_ref[...] + g_ref[...] * ffv - t_ref[...]
        loss_ref[...] += (0.5 / d) * jnp.sum(err * err).reshape(1, 1)
        dout = err * (1.0 / d)
        dout_ref[...] = dout
        dff_ref[...] = (g_ref[...] * dout).astype(BF16)
        dg_ref[...] += jnp.sum(dout * ffv, axis=0, keepdims=True)

    return pl.pallas_call(
        body, name="loss_head", grid=(s // ROW_TILE,), in_specs=[row, row, vec, row],
        out_specs=[one, row, row, vec],
        out_shape=[jax.ShapeDtypeStruct((1, 1), F32), jax.ShapeDtypeStruct((s, d), F32),
                   jax.ShapeDtypeStruct((s, d), BF16), jax.ShapeDtypeStruct((1, d), F32)],
        compiler_params=_cparams(("arbitrary",)))(x1, ff, gate2, tgt)


CONV_COLS = 256
HALO = 8


def _shift_down(cur, halo, k):
    if k == 0:
        return cur
    rolled = pltpu.roll(cur, k, axis=0)
    top = jnp.where(lax.broadcasted_iota(jnp.int32, halo.shape, 0) < k, pltpu.roll(halo, k, axis=0), rolled[:HALO])
    return jnp.concatenate([top, rolled[HALO:]], axis=0)


def _shift_up(cur, halo, k):
    if k == 0:
        return cur
    t = cur.shape[0]
    rolled = pltpu.roll(cur, t - k, axis=0)
    bot = jnp.where(lax.broadcasted_iota(jnp.int32, halo.shape, 0) >= HALO - k, pltpu.roll(halo, HALO - k, axis=0),
                    rolled[t - HALO:])
    return jnp.concatenate([rolled[:t - HALO], bot], axis=0)


def _conv_fwd(proj, conv_w, conv_b):
    s = proj.shape[0]
    nr = s // ROW_TILE
    cb0 = OFF_XBC // CONV_COLS
    hb = ROW_TILE // HALO
    cur = pl.BlockSpec((ROW_TILE, CONV_COLS), lambda j, r: (r, cb0 + j))
    prev = pl.BlockSpec((HALO, CONV_COLS), lambda j, r: (jnp.maximum(r * hb - 1, 0), cb0 + j))
    out = pl.BlockSpec((ROW_TILE, CONV_COLS), lambda j, r: (r, j))

    def body(u_ref, up_ref, w_ref, b_ref, pre_ref, act_ref):
        r = pl.program_id(1)
        u = u_ref[...]
        halo = jnp.where(r > 0, up_ref[...], 0.0)
        acc = b_ref[...] + w_ref[CONV_K - 1:CONV_K, :] * u
        for k in range(1, CONV_K):
            acc = acc + w_ref[CONV_K - 1 - k:CONV_K - k, :] * _shift_down(u, halo, k)
        pre_ref[...] = acc
        act_ref[...] = acc * _sigmoid(acc)

    return pl.pallas_call(
        body, name="conv_fwd", grid=(CONV_CH // CONV_COLS, nr),
        in_specs=[cur, prev, pl.BlockSpec((CONV_K, CONV_COLS), lambda j, r: (0, j)),
                  pl.BlockSpec((1, CONV_COLS), lambda j, r: (0, j))],
        out_specs=[out, out],
        out_shape=[jax.ShapeDtypeStruct((s, CONV_CH), F32), jax.ShapeDtypeStruct((s, CONV_CH), F32)],
        compiler_params=_cparams(("parallel", "arbitrary")))(proj, proj, conv_w, conv_b)


def _conv_bwd(dact, pre, proj, conv_w):
    s = proj.shape[0]
    nr = s // ROW_TILE
    cb0 = OFF_XBC // CONV_COLS
    hb = ROW_TILE // HALO
    last_halo = s // HALO - 1
    cur = pl.BlockSpec((ROW_TILE, CONV_COLS), lambda j, r: (r, j))
    nxt = pl.BlockSpec((HALO, CONV_COLS), lambda j, r: (jnp.minimum((r + 1) * hb, last_halo), j))
    ucur = pl.BlockSpec((ROW_TILE, CONV_COLS), lambda j, r: (r, cb0 + j))
    uprev = pl.BlockSpec((HALO, CONV_COLS), lambda j, r: (jnp.maximum(r * hb - 1, 0), cb0 + j))
    wspec = pl.BlockSpec((CONV_K, CONV_COLS), lambda j, r: (0, j))
    bspec = pl.BlockSpec((1, CONV_COLS), lambda j, r: (0, j))

    def dsilu(p):
        sg = _sigmoid(p)
        return sg * (1.0 + p * (1.0 - sg))

    def body(da_ref, dan_ref, pre_ref, pren_ref, u_ref, up_ref, w_ref, du_ref, dw_ref, db_ref):
        r = pl.program_id(1)

        @pl.when(r == 0)
        def _():
            dw_ref[...] = jnp.zeros_like(dw_ref)
            db_ref[...] = jnp.zeros_like(db_ref)

        dpre = da_ref[...] * dsilu(pre_ref[...])
        dnext = jnp.where(r < nr - 1, dan_ref[...] * dsilu(pren_ref[...]), 0.0)
        u = u_ref[...]
        halo = jnp.where(r > 0, up_ref[...], 0.0)
        du = w_ref[CONV_K - 1:CONV_K, :] * dpre
        dws = [jnp.sum(dpre * u, axis=0, keepdims=True)]
        for k in range(1, CONV_K):
            du = du + w_ref[CONV_K - 1 - k:CONV_K - k, :] * _shift_up(dpre, dnext, k)
            dws.append(jnp.sum(dpre * _shift_down(u, halo, k), axis=0, keepdims=True))
        du_ref[...] = du.astype(BF16)
        dw_ref[...] += jnp.concatenate(dws[::-1], axis=0)
        db_ref[...] += jnp.sum(dpre, axis=0, keepdims=True)

    return pl.pallas_call(
        body, name="conv_bwd", grid=(CONV_CH // CONV_COLS, nr),
        in_specs=[cur, nxt, cur, nxt, ucur, uprev, wspec],
        out_specs=[cur, wspec, bspec],
        out_shape=[jax.ShapeDtypeStruct((s, CONV_CH), BF16), jax.ShapeDtypeStruct((CONV_K, CONV_CH), F32),
                   jax.ShapeDtypeStruct((1, CONV_CH), F32)],
        compiler_params=_cparams(("parallel", "arbitrary")))(dact, dact, pre, pre, proj, proj, conv_w)


def _ssd_common(dtr, dtb, alog):
    lane = lax.broadcasted_iota(jnp.int32, (1, LANE), 1)
    head_lane = lane < SSD_HEADS
    dt = jnp.where(head_lane, _softplus(dtr + dtb), 0.0)
    a = jnp.where(head_lane, -jnp.exp(alog), 0.0)
    row = lax.broadcasted_iota(jnp.int32, (SSD_CHUNK, SSD_CHUNK), 0)
    col = lax.broadcasted_iota(jnp.int32, (SSD_CHUNK, SSD_CHUNK), 1)
    tril = row >= col
    cs = _dot(tril.astype(F32), dt * a, NN, precision=HIGHEST)
    return dt, a, cs, cs.T, tril, lane


def _ssd_fwd(proj, act, dtb, alog, dsk, nw):
    s = proj.shape[0]
    nc = s // SSD_CHUNK
    bc_w = SSD_GROUPS * SSD_STATE

    def body(z_ref, dtr_ref, xs_ref, b_ref, c_ref, dtb_ref, alog_ref, dsk_ref, nw_ref,
             ypre_ref, yssd_ref, hall_ref, h_scr, y_scr):
        @pl.when(pl.program_id(0) == 0)
        def _():
            h_scr[...] = jnp.zeros_like(h_scr)

        dt, a, cs, cst, tril, lane = _ssd_common(dtr_ref[...], dtb_ref[...], alog_ref[...])
        for g in range(SSD_GROUPS):
            bg = b_ref[:, g * SSD_STATE:(g + 1) * SSD_STATE].astype(BF16)
            cg = c_ref[:, g * SSD_STATE:(g + 1) * SSD_STATE].astype(BF16)
            cb = _dot(cg, bg, NT)
            for e in range(HEADS_PER_GROUP):
                h = g * HEADS_PER_GROUP + e
                hs = slice(h * HEAD_DIM, (h + 1) * HEAD_DIM)
                cs_col = cs[:, h:h + 1]
                cs_last = cs[SSD_CHUNK - 1:SSD_CHUNK, h:h + 1]
                lm = jnp.exp(jnp.where(tril, cs_col - cst[h:h + 1, :], -1e30))
                xs_h = xs_ref[:, hs]
                xdt = xs_h * dt[:, h:h + 1]
                hprev = h_scr[h]
                hall_ref[0, hs, :] = hprev
                y = _dot((cb * lm).astype(BF16), xdt.astype(BF16), NN)
                y = y + jnp.exp(cs_col) * _dot(cg, hprev.astype(BF16), NT)
                y_scr[:, hs] = y + dsk_ref[:, h:h + 1] * xs_h
                xdec = (xdt * jnp.exp(cs_last - cs_col)).astype(BF16)
                h_scr[h] = hprev * jnp.exp(cs_last) + _dot(xdec, bg, TN)
        y = y_scr[...]
        ypre_ref[...] = y
        z = z_ref[...]
        yg = y * (z * _sigmoid(z))
        for g in range(SSD_GROUPS):
            gs = slice(g * GROUP_WIDTH, (g + 1) * GROUP_WIDTH)
            seg = yg[:, gs]
            r = lax.rsqrt(jnp.mean(seg * seg, axis=-1, keepdims=True) + EPS)
            yssd_ref[:, gs] = (seg * r * nw_ref[:, gs]).astype(BF16)

    row_d = lambda cb: pl.BlockSpec((SSD_CHUNK, SSD_D_INNER), lambda c: (c, cb))
    small = pl.BlockSpec((1, LANE), lambda c: (0, 0))
    return pl.pallas_call(
        body, name="ssd_fwd", grid=(nc,),
        in_specs=[row_d(OFF_Z // SSD_D_INNER),
                  pl.BlockSpec((SSD_CHUNK, LANE), lambda c: (c, OFF_DT // LANE)),
                  row_d(0),
                  pl.BlockSpec((SSD_CHUNK, bc_w), lambda c: (c, SSD_D_INNER // bc_w)),
                  pl.BlockSpec((SSD_CHUNK, bc_w), lambda c: (c, SSD_D_INNER // bc_w + 1)),
                  small, small, small, pl.BlockSpec((1, SSD_D_INNER), lambda c: (0, 0))],
        out_specs=[row_d(0), row_d(0), pl.BlockSpec((1, SSD_D_INNER, SSD_STATE), lambda c: (c, 0, 0))],
        out_shape=[jax.ShapeDtypeStruct((s, SSD_D_INNER), F32), jax.ShapeDtypeStruct((s, SSD_D_INNER), BF16),
                   jax.ShapeDtypeStruct((nc, SSD_D_INNER, SSD_STATE), F32)],
        scratch_shapes=[pltpu.VMEM((SSD_HEADS, HEAD_DIM, SSD_STATE), F32), pltpu.VMEM((SSD_CHUNK, SSD_D_INNER), F32)],
        compiler_params=_cparams(("arbitrary",)))(proj, proj, act, act, act, dtb, alog, dsk, nw)


def _ssd_bwd(dycat, ypre, proj, act, hall, dtb, alog, dsk, nw):
    s = proj.shape[0]
    nc = s // SSD_CHUNK
    bc_w = SSD_GROUPS * SSD_STATE

    def body(dy_ref, ypre_ref, z_ref, dtr_ref, xs_ref, b_ref, c_ref, hall_ref, dtb_ref, alog_ref, dsk_ref, nw_ref,
             dz_ref, dact_ref, ddtr_ref, da_ref, ddsk_ref, ddtb_ref, dnw_ref, dh_scr, dyh_scr):
        @pl.when(pl.program_id(0) == 0)
        def _():
            dh_scr[...] = jnp.zeros_like(dh_scr)
            da_ref[...] = jnp.zeros_like(da_ref)
            ddsk_ref[...] = jnp.zeros_like(ddsk_ref)
            ddtb_ref[...] = jnp.zeros_like(ddtb_ref)
            dnw_ref[...] = jnp.zeros_like(dnw_ref)

        z = z_ref[...]
        sg = _sigmoid(z)
        sz = z * sg
        ypre = ypre_ref[...]
        yg = ypre * sz
        dyg_parts = []
        for g in range(SSD_GROUPS):
            gs = slice(g * GROUP_WIDTH, (g + 1) * GROUP_WIDTH)
            seg = yg[:, gs]
            r = lax.rsqrt(jnp.mean(seg * seg, axis=-1, keepdims=True) + EPS)
            nrm = seg * r
            dyo = dy_ref[:, gs]
            dnw_ref[:, gs] += jnp.sum(dyo * nrm, axis=0, keepdims=True)
            dn = dyo * nw_ref[:, gs]
            dyg_parts.append(r * (dn - nrm * jnp.mean(dn * nrm, axis=-1, keepdims=True)))
        dyg = jnp.concatenate(dyg_parts, axis=1)
        dz_ref[...] = (dyg * ypre * (sg * (1.0 + z * (1.0 - sg)))).astype(BF16)
        dyh_scr[...] = dyg * sz

        dtr = dtr_ref[...]
        dt, a, cs, cst, tril, lane = _ssd_common(dtr, dtb_ref[...], alog_ref[...])
        sub = lax.broadcasted_iota(jnp.int32, (SSD_CHUNK, 1), 0)
        last_row = sub == SSD_CHUNK - 1
        dcs_col = jnp.zeros((SSD_CHUNK, LANE), F32)
        dcs_row = jnp.zeros((SSD_CHUNK, LANE), F32)
        ddt = jnp.zeros((SSD_CHUNK, LANE), F32)
        ddsk = jnp.zeros((1, LANE), F32)
        for g in range(SSD_GROUPS):
            bsl = slice(g * SSD_STATE, (g + 1) * SSD_STATE)
            bgf = b_ref[:, bsl]
            bg = bgf.astype(BF16)
            cg = c_ref[:, bsl].astype(BF16)
            cb = _dot(cg, bg, NT)
            dcb = jnp.zeros((SSD_CHUNK, SSD_CHUNK), F32)
            dbg = jnp.zeros((SSD_CHUNK, SSD_STATE), F32)
            dcg = jnp.zeros((SSD_CHUNK, SSD_STATE), F32)
            for e in range(HEADS_PER_GROUP):
                h = g * HEADS_PER_GROUP + e
                hs = slice(h * HEAD_DIM, (h + 1) * HEAD_DIM)
                dt_h = dt[:, h:h + 1]
                cs_col = cs[:, h:h + 1]
                cs_last = cs[SSD_CHUNK - 1:SSD_CHUNK, h:h + 1]
                ecs = jnp.exp(cs_col)
                decay = jnp.exp(cs_last - cs_col)
                cd = jnp.exp(cs_last)
                lm = jnp.exp(jnp.where(tril, cs_col - cst[h:h + 1, :], -1e30))
                gm = cb * lm
                gmb = gm.astype(BF16)
                xs_h = xs_ref[:, hs]
                xdt = xs_h * dt_h
                xdtb = xdt.astype(BF16)
                hprev = hall_ref[0, hs, :]
                hb = hprev.astype(BF16)
                dhn = dh_scr[h]
                dhb = dhn.astype(BF16)
                dyh = dyh_scr[:, hs]
                dyb = dyh.astype(BF16)
                w_off = _dot(cg, hb, NT)
                dyo = dyh * ecs
                dyob = dyo.astype(BF16)
                dcg = dcg + _dot(dyob, hb, NN)
                dh_y = _dot(dyob, cg, TN)
                dcs_c = jnp.sum(dyo * w_off, axis=-1, keepdims=True)
                dg = _dot(dyb, xdtb, NT)
                dxdt = _dot(gmb, dyb, TN)
                mm = dg * gm
                dcs_c = dcs_c + jnp.sum(mm, axis=-1, keepdims=True)
                dcs_r = jnp.sum(mm, axis=0, keepdims=True)
                dcb = dcb + dg * lm
                q = _dot(xdtb, dhb, NN)
                dbg = dbg + decay * q
                t1 = decay * jnp.sum(q * bgf, axis=-1, keepdims=True)
                dcs_c = dcs_c - t1
                dxdt = dxdt + decay * _dot(bg, dhb, NT)
                dlast = jnp.sum(t1).reshape(1, 1) + jnp.sum(dhn * hprev).reshape(1, 1) * cd
                dcs_c = dcs_c + jnp.where(last_row, dlast, 0.0)
                dh_scr[h] = dhn * cd + dh_y
                dact_ref[:, hs] = dxdt * dt_h + dsk_ref[:, h:h + 1] * dyh
                ddt_h = jnp.sum(dxdt * xs_h, axis=-1, keepdims=True)
                ddsk = ddsk + jnp.where(lane == h, jnp.sum(dyh * xs_h).reshape(1, 1), 0.0)
                dcs_col = dcs_col + jnp.where(lane == h, dcs_c, 0.0)
                dcs_row = dcs_row + jnp.where(sub == h, dcs_r, 0.0)
                ddt = ddt + jnp.where(lane == h, ddt_h, 0.0)
            dcbb = dcb.astype(BF16)
            dcg = dcg + _dot(dcbb, bg, NN)
            dbg = dbg + _dot(dcbb, cg, TN)
            dact_ref[:, SSD_D_INNER + g * SSD_STATE:SSD_D_INNER + (g + 1) * SSD_STATE] = dbg
            dact_ref[:, SSD_D_INNER + bc_w + g * SSD_STATE:SSD_D_INNER + bc_w + (g + 1) * SSD_STATE] = dcg
        dcs = dcs_col - dcs_row.T
        row = lax.broadcasted_iota(jnp.int32, (SSD_CHUNK, SSD_CHUNK), 0)
        col = lax.broadcasted_iota(jnp.int32, (SSD_CHUNK, SSD_CHUNK), 1)
        dda = _dot((col >= row).astype(F32), dcs, NN, precision=HIGHEST)
        ddt = ddt + dda * a
        da_ref[...] += jnp.sum(dda * dt, axis=0, keepdims=True)
        ddtr = jnp.where(lane < SSD_HEADS, ddt * _sigmoid(dtr + dtb_ref[...]), 0.0)
        ddtr_ref[...] = ddtr.astype(BF16)
        ddtb_ref[...] += jnp.sum(ddtr, axis=0, keepdims=True)
        ddsk_ref[...] += ddsk

    rev = lambda c: nc - 1 - c
    row_d = lambda cb: pl.BlockSpec((SSD_CHUNK, SSD_D_INNER), lambda c: (rev(c), cb))
    small = pl.BlockSpec((1, LANE), lambda c: (0, 0))
    wide = pl.BlockSpec((1, SSD_D_INNER), lambda c: (0, 0))
    small_shape = jax.ShapeDtypeStruct((1, LANE), F32)
    return pl.pallas_call(
        body, name="ssd_bwd", grid=(nc,),
        in_specs=[row_d(0), row_d(0), row_d(OFF_Z // SSD_D_INNER),
                  pl.BlockSpec((SSD_CHUNK, LANE), lambda c: (rev(c), OFF_DT // LANE)),
                  row_d(0),
                  pl.BlockSpec((SSD_CHUNK, bc_w), lambda c: (rev(c), SSD_D_INNER // bc_w)),
                  pl.BlockSpec((SSD_CHUNK, bc_w), lambda c: (rev(c), SSD_D_INNER // bc_w + 1)),
                  pl.BlockSpec((1, SSD_D_INNER, SSD_STATE), lambda c: (rev(c), 0, 0)),
                  small, small, small, wide],
        out_specs=[row_d(0), pl.BlockSpec((SSD_CHUNK, CONV_CH), lambda c: (rev(c), 0)),
                   pl.BlockSpec((SSD_CHUNK, LANE), lambda c: (rev(c), 0)), small, small, small, wide],
        out_shape=[jax.ShapeDtypeStruct((s, SSD_D_INNER), BF16), jax.ShapeDtypeStruct((s, CONV_CH), F32),
                   jax.ShapeDtypeStruct((s, LANE), BF16), small_shape, small_shape, small_shape,
                   jax.ShapeDtypeStruct((1, SSD_D_INNER), F32)],
        scratch_shapes=[pltpu.VMEM((SSD_HEADS, HEAD_DIM, SSD_STATE), F32), pltpu.VMEM((SSD_CHUNK, SSD_D_INNER), F32)],
        compiler_params=_cparams(("arbitrary",)))(dycat, ypre, proj, proj, act, act, act, hall, dtb, alog, dsk, nw)


def _head_mean_matrix():
    row = lax.broadcasted_iota(jnp.int32, (LANE, LANE), 0) // HEAD_DIM
    col = lax.broadcasted_iota(jnp.int32, (LANE, LANE), 1) // HEAD_DIM
    return (row == col).astype(F32)


def _head_sum2(v, ones_bd):
    hi = v.astype(BF16)
    lo = (v - hi.astype(F32)).astype(BF16)
    return _dot(hi, ones_bd, NN) + _dot(lo, ones_bd, NN)


def _head_norm(x, w, scale, ones_bd):
    ms = _head_sum2(x * x, ones_bd) * (1.0 / HEAD_DIM)
    return (x * lax.rsqrt(ms + EPS)) * (w * scale)


PRO_ROWS = 256
ATT_UNROLL = 4
KEYS = 2 * ATT_BLK
NEG = -1e30
HALF = HEAD_DIM // 2


def _rows(start, size, dil):
    return pl.ds(start, size) if dil == 1 else pl.ds(start, size, stride=dil)


def _fill_bias(bias_ref):
    row = lax.broadcasted_iota(jnp.int32, (ATT_BLK, 2 * KEYS), 0)
    col = lax.broadcasted_iota(jnp.int32, (ATT_BLK, 2 * KEYS), 1) & (KEYS - 1)
    for first, off in ((0, 0), (1, ATT_BLK)):
        dist = off + row - col
        bias_ref[first] = jnp.where((dist >= 0) & (dist <= ATT_BLK), 0.0, NEG)


def _pair(a, b):
    return jnp.concatenate([jnp.broadcast_to(a, (ATT_BLK, KEYS)), jnp.broadcast_to(b, (ATT_BLK, KEYS))], axis=1)


def _split_heads(x, is_a):
    zero = jnp.zeros_like(x)
    return jnp.concatenate([jnp.where(is_a, x, zero), jnp.where(is_a, zero, x)], axis=0)


def _block_ids(b, nb):
    i = b & (nb - 1)
    q0 = pl.multiple_of(b * ATT_BLK, ATT_BLK)
    k0 = pl.multiple_of((b - jnp.minimum(i, 1)) * ATT_BLK, ATT_BLK)
    return pl.ds(q0, ATT_BLK), pl.ds(k0, KEYS), jnp.minimum(i, 1)


def _att_fwd(proj, qw, kw):
    s = proj.shape[0]
    nblk = s // ATT_BLK
    assert all((s // d) // ATT_BLK >= 2 for d in DILATIONS)
    blk = lambda off: pl.BlockSpec((s, LANE), lambda i: (0, off // LANE + i))
    wspec = pl.BlockSpec((1, LANE), lambda i: (0, i))
    oblk = pl.BlockSpec((s, LANE), lambda i: (0, i))

    def body(q_ref, k_ref, v_ref, qw_ref, kw_ref, o_ref, lse_ref, qn, kn, q_cm, k_cm, v_cm, m_acc, l_acc, o_d, m_d, l_d, bias):
        ones_bd = _head_mean_matrix().astype(BF16)
        is_a = lax.broadcasted_iota(jnp.int32, (1, LANE), 1) < HEAD_DIM
        ones_ext = _split_heads(jnp.ones((KEYS, LANE), BF16), is_a)
        _fill_bias(bias)

        def pro(j, c):
            rows = pl.ds(pl.multiple_of(j * PRO_ROWS, PRO_ROWS), PRO_ROWS)
            qn[rows, :] = _head_norm(q_ref[rows, :], qw_ref[...], HEAD_DIM ** -0.5, ones_bd)
            kn[rows, :] = _head_norm(k_ref[rows, :], kw_ref[...], 1.0, ones_bd)
            return c

        lax.fori_loop(0, s // PRO_ROWS, pro, 0)

        for dil in DILATIONS:
            ln = s // dil
            nb = ln // ATT_BLK
            o_out, m_out, l_out = (o_ref, m_acc, l_acc) if dil == 1 else (o_d, m_d, l_d)
            for r in range(dil):
                def relayout(j, c, dil=dil, r=r, ln=ln):
                    j0 = pl.multiple_of(j * PRO_ROWS, PRO_ROWS)
                    src = _rows(r + dil * j0, PRO_ROWS, dil)
                    dst = pl.ds(r * ln + j0, PRO_ROWS)
                    q_cm[dst, :] = qn[src, :].astype(BF16)
                    k_cm[dst, :] = kn[src, :].astype(BF16)
                    v_cm[dst, :] = v_ref[src, :].astype(BF16)
                    return c

                lax.fori_loop(0, ln // PRO_ROWS, relayout, 0)

            def step(b, c, nb=nb, o_out=o_out, m_out=m_out, l_out=l_out):
                qrows, krows, first = _block_ids(b, nb)
                kb = _split_heads(k_cm[krows, :], is_a)
                vb = jnp.concatenate([_split_heads(v_cm[krows, :], is_a), ones_ext], axis=1)
                sc = _dot(q_cm[qrows, :], kb, NT) + bias[first]
                m_a = jnp.max(sc[:, :KEYS], axis=-1, keepdims=True)
                m_b = jnp.max(sc[:, KEYS:], axis=-1, keepdims=True)
                p = jnp.exp(sc - _pair(m_a, m_b)).astype(BF16)
                ol = _dot(p, vb, NN)
                o_out[qrows, :] = ol[:, :LANE]
                l_out[qrows, :] = ol[:, LANE:]
                m_out[qrows, :] = jnp.where(is_a, m_a, m_b)
                return c

            lax.fori_loop(0, nblk, step, 0, unroll=ATT_UNROLL)

            if dil > 1:
                for r in range(dil):
                    def merge(j, c, dil=dil, r=r, ln=ln):
                        j0 = pl.multiple_of(j * PRO_ROWS, PRO_ROWS)
                        nat = _rows(r + dil * j0, PRO_ROWS, dil)
                        cm = pl.ds(r * ln + j0, PRO_ROWS)
                        m_old, m_new = m_acc[nat, :], m_d[cm, :]
                        m = jnp.maximum(m_old, m_new)
                        a_old, a_new = jnp.exp(m_old - m), jnp.exp(m_new - m)
                        o_ref[nat, :] = a_old * o_ref[nat, :] + a_new * o_d[cm, :]
                        l_acc[nat, :] = a_old * l_acc[nat, :] + a_new * l_d[cm, :]
                        m_acc[nat, :] = m
                        return c

                    lax.fori_loop(0, ln // PRO_ROWS, merge, 0)

        def epi(j, c):
            rows = pl.ds(pl.multiple_of(j * PRO_ROWS, PRO_ROWS), PRO_ROWS)
            l = l_acc[rows, :]
            o_ref[rows, :] = o_ref[rows, :] / l
            lse_ref[rows, :] = m_acc[rows, :] + jnp.log(l)
            return c

        lax.fori_loop(0, s // PRO_ROWS, epi, 0)

    f = jax.ShapeDtypeStruct((s, ATT_D), F32)
    scr = pltpu.VMEM((s, LANE), F32)
    scb = pltpu.VMEM((s, LANE), BF16)
    return pl.pallas_call(
        body, name="att_fwd", grid=(ATT_D // LANE,),
        in_specs=[blk(OFF_Q), blk(OFF_K), blk(OFF_V), wspec, wspec], out_specs=[oblk, oblk], out_shape=[f, f],
        scratch_shapes=[scr, scr, scb, scb, scb, scr, scr, scr, scr, scr, pltpu.VMEM((2, ATT_BLK, 2 * KEYS), F32)],
        compiler_params=_cparams(("parallel",)))(proj, proj, proj, qw, kw)


def _att_bwd(proj, do, stats, qw, kw):
    s = proj.shape[0]
    nblk = s // ATT_BLK
    blk = lambda off: pl.BlockSpec((s, LANE), lambda i: (0, off // LANE + i))
    wspec = pl.BlockSpec((1, LANE), lambda i: (0, i))
    oblk = pl.BlockSpec((s, LANE), lambda i: (0, i))

    def body(q_ref, k_ref, v_ref, do_ref, st_ref, qw_ref, kw_ref, dq_ref, dk_ref, dv_ref, dqw_ref, dkw_ref,
             qn, kn, q_cm, do_cm, k_cm, v_cm, st_cm, dq_acc, dk_acc, dv_acc, dq_d, dk_d, dv_d, bias):
        ones_bd = _head_mean_matrix().astype(BF16)
        is_a = lax.broadcasted_iota(jnp.int32, (1, LANE), 1) < HEAD_DIM
        _fill_bias(bias)
        zero = jnp.zeros((PRO_ROWS, LANE), F32)

        def pro(j, c):
            rows = pl.ds(pl.multiple_of(j * PRO_ROWS, PRO_ROWS), PRO_ROWS)
            qn[rows, :] = _head_norm(q_ref[rows, :], qw_ref[...], HEAD_DIM ** -0.5, ones_bd)
            kn[rows, :] = _head_norm(k_ref[rows, :], kw_ref[...], 1.0, ones_bd)
            dk_acc[rows, :] = zero
            dv_acc[rows, :] = zero
            return c

        lax.fori_loop(0, s // PRO_ROWS, pro, 0)

        for dil in DILATIONS:
            ln = s // dil
            nb = ln // ATT_BLK
            dq_o, dk_o, dv_o = (dq_acc, dk_acc, dv_acc) if dil == 1 else (dq_d, dk_d, dv_d)
            for r in range(dil):
                def relayout(j, c, dil=dil, r=r, ln=ln):
                    j0 = pl.multiple_of(j * PRO_ROWS, PRO_ROWS)
                    src = _rows(r + dil * j0, PRO_ROWS, dil)
                    dst = pl.ds(r * ln + j0, PRO_ROWS)
                    q_cm[dst, :] = qn[src, :].astype(BF16)
                    k_cm[dst, :] = kn[src, :].astype(BF16)
                    v_cm[dst, :] = v_ref[src, :].astype(BF16)
                    do_cm[dst, :] = do_ref[src, :].astype(BF16)
                    st_cm[dst, :] = st_ref[src, :]
                    if dil > 1:
                        dk_d[dst, :] = zero
                        dv_d[dst, :] = zero
                    return c

                lax.fori_loop(0, ln // PRO_ROWS, relayout, 0)

            def step(b, c, nb=nb, dq_o=dq_o, dk_o=dk_o, dv_o=dv_o):
                qrows, krows, first = _block_ids(b, nb)
                qb = q_cm[qrows, :]
                dob = do_cm[qrows, :]
                kb = _split_heads(k_cm[krows, :], is_a)
                vb = _split_heads(v_cm[krows, :], is_a)
                st = st_cm[qrows, :]
                sc = _dot(qb, kb, NT) + bias[first]
                p = jnp.exp(sc - _pair(st[:, 0:1], st[:, HEAD_DIM:HEAD_DIM + 1]))
                dp = _dot(dob, vb, NT)
                ds = (p * (dp - _pair(st[:, HALF:HALF + 1], st[:, HEAD_DIM + HALF:HEAD_DIM + HALF + 1]))).astype(BF16)
                dq_o[qrows, :] = _dot(ds, kb, NN)
                dkf = _dot(ds, qb, TN)
                dvf = _dot(p.astype(BF16), dob, TN)
                dk_o[krows, :] += jnp.where(is_a, dkf[:KEYS], dkf[KEYS:])
                dv_o[krows, :] += jnp.where(is_a, dvf[:KEYS], dvf[KEYS:])
                return c

            lax.fori_loop(0, nblk, step, 0, unroll=ATT_UNROLL)

            if dil > 1:
                for r in range(dil):
                    def merge(j, c, dil=dil, r=r, ln=ln):
                        j0 = pl.multiple_of(j * PRO_ROWS, PRO_ROWS)
                        nat = _rows(r + dil * j0, PRO_ROWS, dil)
                        cm = pl.ds(r * ln + j0, PRO_ROWS)
                        dq_acc[nat, :] += dq_d[cm, :]
                        dk_acc[nat, :] += dk_d[cm, :]
                        dv_acc[nat, :] += dv_d[cm, :]
                        return c

                    lax.fori_loop(0, ln // PRO_ROWS, merge, 0)

        def back(dn_out, x, w, scale):
            r = lax.rsqrt(_head_sum2(x * x, ones_bd) * (1.0 / HEAD_DIM) + EPS)
            nrm = x * r
            dw = jnp.sum(dn_out * nrm, axis=0, keepdims=True) * scale
            dn = dn_out * (w * scale)
            return r * (dn - nrm * (_head_sum2(dn * nrm, ones_bd) * (1.0 / HEAD_DIM))), dw

        def epi(j, c):
            rows = pl.ds(pl.multiple_of(j * PRO_ROWS, PRO_ROWS), PRO_ROWS)
            dq, dqw = back(dq_acc[rows, :], q_ref[rows, :], qw_ref[...], HEAD_DIM ** -0.5)
            dk, dkw = back(dk_acc[rows, :], k_ref[rows, :], kw_ref[...], 1.0)
            dq_ref[rows, :] = dq.astype(BF16)
            dk_ref[rows, :] = dk.astype(BF16)
            dv_ref[rows, :] = dv_acc[rows, :].astype(BF16)
            return (c[0] + dqw, c[1] + dkw)

        zrow = jnp.zeros((1, LANE), F32)
        dqw, dkw = lax.fori_loop(0, s // PRO_ROWS, epi, (zrow, zrow))
        dqw_ref[...] = dqw
        dkw_ref[...] = dkw

    o = jax.ShapeDtypeStruct((s, ATT_D), BF16)
    ov = jax.ShapeDtypeStruct((1, ATT_D), F32)
    scr = pltpu.VMEM((s, LANE), F32)
    scb = pltpu.VMEM((s, LANE), BF16)
    return pl.pallas_call(
        body, name="att_bwd", grid=(ATT_D // LANE,),
        in_specs=[blk(OFF_Q), blk(OFF_K), blk(OFF_V), oblk, oblk, wspec, wspec],
        out_specs=[oblk, oblk, oblk, wspec, wspec], out_shape=[o, o, o, ov, ov],
        scratch_shapes=[scr, scr, scb, scb, scb, scb, scr, scr, scr, scr, scr, scr, scr, pltpu.VMEM((2, ATT_BLK, 2 * KEYS), F32)],
        compiler_params=_cparams(("parallel",)))(proj, proj, proj, do, stats, qw, kw)


def _att_norm_fwd(o, nw):
    s = o.shape[0]
    row = pl.BlockSpec((ROW_TILE, ATT_D), lambda i: (i, 0))
    vec = pl.BlockSpec((1, ATT_D), lambda i: (0, 0))

    def body(o_ref, nw_ref, y_ref):
        o = o_ref[...]
        r = lax.rsqrt(jnp.mean(o * o, axis=-1, keepdims=True) + EPS)
        y_ref[...] = (o * r * nw_ref[...]).astype(BF16)

    return pl.pallas_call(body, name="att_norm_fwd", grid=(s // ROW_TILE,), in_specs=[row, vec], out_specs=row,
                          out_shape=jax.ShapeDtypeStruct((s, ATT_D), BF16), compiler_params=_cparams(("parallel",)))(o, nw)


def _att_norm_bwd(dycat, o, lse, nw):
    s = o.shape[0]
    row = pl.BlockSpec((ROW_TILE, ATT_D), lambda i: (i, 0))
    vec = pl.BlockSpec((1, ATT_D), lambda i: (0, 0))

    def body(dy_ref, o_ref, lse_ref, nw_ref, do_ref, st_ref, dnw_ref):
        @pl.when(pl.program_id(0) == 0)
        def _():
            dnw_ref[...] = jnp.zeros_like(dnw_ref)

        o = o_ref[...]
        dy = dy_ref[...]
        r = lax.rsqrt(jnp.mean(o * o, axis=-1, keepdims=True) + EPS)
        nrm = o * r
        dnw_ref[...] += jnp.sum(dy * nrm, axis=0, keepdims=True)
        dn = dy * nw_ref[...]
        do = r * (dn - nrm * jnp.mean(dn * nrm, axis=-1, keepdims=True))
        do_ref[...] = do
        ones_bd = _head_mean_matrix().astype(BF16)
        prod = do * o
        delta = jnp.concatenate([_head_sum2(prod[:, j * LANE:(j + 1) * LANE], ones_bd) for j in range(ATT_D // LANE)], axis=1)
        lane = lax.broadcasted_iota(jnp.int32, (1, ATT_D), 1)
        st_ref[...] = jnp.where((lane & (HEAD_DIM - 1)) < HALF, lse_ref[...], delta)

    f = jax.ShapeDtypeStruct((s, ATT_D), F32)
    return pl.pallas_call(
        body, name="att_norm_bwd", grid=(s // ROW_TILE,),
        in_specs=[pl.BlockSpec((ROW_TILE, ATT_D), lambda i: (i, 1)), row, row, vec], out_specs=[row, row, vec],
        out_shape=[f, f, jax.ShapeDtypeStruct((1, ATT_D), F32)],
        compiler_params=_cparams(("arbitrary",)))(dycat, o, lse, nw)


def _ada_fwd(c_all, w_ada):
    def body(c_ref, w_ref, o_ref):
        cv = c_ref[...]
        o_ref[...] = _dot((cv * _sigmoid(cv)).astype(BF16), w_ref[...].astype(BF16), NN)

    return pl.pallas_call(body, name="ada_fwd", out_shape=jax.ShapeDtypeStruct((c_all.shape[0], w_ada.shape[1]), F32),
                          compiler_params=_cparams())(c_all, w_ada)


def _adamw_math(g, w, m, v):
    m_new = ADAM_B1 * m + (1.0 - ADAM_B1) * g
    v_new = ADAM_B2 * v + (1.0 - ADAM_B2) * (g * g)
    m_hat = m_new / (1.0 - ADAM_B1 ** ADAM_STEP)
    v_hat = v_new / (1.0 - ADAM_B2 ** ADAM_STEP)
    delta = -ADAM_LR * (m_hat / (jnp.sqrt(v_hat) + ADAM_EPS) + ADAM_WD * w)
    return delta, m_new, v_new


def _ada_bwd_adamw(c_all, dmod_cols, w, m, v):
    rows, cols = w.shape
    tr = 256
    blk = pl.BlockSpec((tr, cols), lambda i: (i, 0))

    def body(c_ref, d_ref, w_ref, m_ref, v_ref, g_ref, dl_ref, mo_ref, vo_ref):
        cv = c_ref[...]
        ca = cv * _sigmoid(cv)
        g = ca[:, 0:1] * d_ref[0:1, :]
        for b in range(1, N_DEV):
            g = g + ca[:, b:b + 1] * d_ref[b:b + 1, :]
        g_ref[...] = g
        dl_ref[...], mo_ref[...], vo_ref[...] = _adamw_math(g, w_ref[...], m_ref[...], v_ref[...])

    o = jax.ShapeDtypeStruct((rows, cols), F32)
    return pl.pallas_call(
        body, name="ada_bwd_adamw", grid=(rows // tr,),
        in_specs=[pl.BlockSpec((tr, N_DEV), lambda i: (i, 0)), pl.BlockSpec((N_DEV, cols), lambda i: (0, 0)), blk, blk, blk],
        out_specs=[blk] * 4, out_shape=[o, o, o, o], compiler_params=_cparams(("parallel",)))(c_all.T, dmod_cols, w, m, v)


def _reduce_adamw(slabs, w, m, v, name):
    rows, cols = w.shape
    tr = 128
    blk = pl.BlockSpec((tr, cols), lambda i: (i, 0))

    def body(s_ref, w_ref, m_ref, v_ref, g_ref, dl_ref, mo_ref, vo_ref):
        g = s_ref[0].astype(F32)
        for dev in range(1, N_DEV):
            g = g + s_ref[dev].astype(F32)
        g_ref[...] = g
        dl_ref[...], mo_ref[...], vo_ref[...] = _adamw_math(g, w_ref[...], m_ref[...], v_ref[...])

    o = jax.ShapeDtypeStruct((rows, cols), F32)
    return pl.pallas_call(
        body, name=name, grid=(rows // tr,),
        in_specs=[pl.BlockSpec((N_DEV, tr, cols), lambda i: (0, i, 0)), blk, blk, blk],
        out_specs=[blk] * 4, out_shape=[o, o, o, o], compiler_params=_cparams(("parallel",)))(slabs, w, m, v)


def _small_reduce_adamw(gathered, w, m, v):
    def body(s_ref, w_ref, m_ref, v_ref, g_ref, dl_ref, mo_ref, vo_ref):
        g = s_ref[0]
        for dev in range(1, N_DEV):
            g = g + s_ref[dev]
        g_ref[...] = g
        dl_ref[...], mo_ref[...], vo_ref[...] = _adamw_math(g, w_ref[...], m_ref[...], v_ref[...])

    o = jax.ShapeDtypeStruct(w.shape, F32)
    return pl.pallas_call(body, name="small_reduce_adamw", out_shape=[o, o, o, o], compiler_params=_cparams())(gathered, w, m, v)


def _adamw_small(g, w, m, v, name):
    def body(g_ref, w_ref, m_ref, v_ref, dl_ref, mo_ref, vo_ref):
        dl_ref[...], mo_ref[...], vo_ref[...] = _adamw_math(g_ref[...], w_ref[...], m_ref[...], v_ref[...])

    o = jax.ShapeDtypeStruct(w.shape, F32)
    return pl.pallas_call(body, name=name, out_shape=[o, o, o], compiler_params=_cparams())(g, w, m, v)


def _exchange(arrs, name, scatter):
    n = len(arrs)
    hbm = pl.BlockSpec(memory_space=pltpu.HBM)

    def body(*refs):
        ins, outs = refs[:n], refs[n:2 * n]
        send_sems, recv_sems, local_sems = refs[2 * n:]
        x, y, c = lax.axis_index("x"), lax.axis_index("y"), lax.axis_index("c")
        me = 4 * x + 2 * y + c
        local, remote = [], []
        for a in range(n):
            src_mine = ins[a].at[me] if scatter else ins[a]
            cp = pltpu.make_async_copy(src_mine, outs[a].at[me], local_sems.at[a])
            cp.start()
            local.append(cp)
            for k in range(1, N_DEV):
                px = 1 - x if k & 4 else x
                py = 1 - y if k & 2 else y
                pc = 1 - c if k & 1 else c
                peer = 4 * px + 2 * py + pc
                sem = a * (N_DEV - 1) + k - 1
                send = pltpu.make_async_remote_copy(
                    src_ref=ins[a].at[peer] if scatter else ins[a], dst_ref=outs[a].at[me],
                    send_sem=send_sems.at[sem], recv_sem=recv_sems.at[sem], device_id=(px, py, pc), device_id_type=MESH_IDS)
                send.start()
                recv = pltpu.make_async_remote_copy(
                    src_ref=ins[a].at[peer] if scatter else ins[a], dst_ref=outs[a].at[peer],
                    send_sem=send_sems.at[sem], recv_sem=recv_sems.at[sem], device_id=(px, py, pc), device_id_type=MESH_IDS)
                remote.append((send, recv))
        for send, recv in remote:
            send.wait_send()
            recv.wait_recv()
        for cp in local:
            cp.wait()

    out_shape = [jax.ShapeDtypeStruct(a.shape if scatter else (N_DEV,) + a.shape, a.dtype) for a in arrs]
    return pl.pallas_call(
        body, name=name, in_specs=[hbm] * n, out_specs=[hbm] * n, out_shape=out_shape,
        scratch_shapes=[pltpu.SemaphoreType.DMA((n * (N_DEV - 1),)), pltpu.SemaphoreType.DMA((n * (N_DEV - 1),)),
                        pltpu.SemaphoreType.DMA((n,))],
    )(*arrs)


def _pad_lanes(v, width=LANE):
    return jnp.pad(v, ((0, 0), (0, width - v.shape[1])))


def _local_step(x, tgt, mod, w_in_p, conv_w, conv_b, dt_bias, a_log, d_skip, ssd_norm_w, q_norm_w, k_norm_w,
                attn_norm_w, w_out, w_ff1, w_ff2, norm1_w, norm2_w):
    shift1, scale1, gate1, shift2, scale2, gate2 = [mod[i:i + 1] for i in range(N_MOD)]
    dtb, alog, dsk = _pad_lanes(dt_bias), _pad_lanes(a_log), _pad_lanes(d_skip)
    qw, kw = jnp.tile(q_norm_w, (1, ATT_HEADS)), jnp.tile(k_norm_w, (1, ATT_HEADS))

    h1 = _norm_mod_fwd(x, norm1_w, scale1, shift1, "norm1_fwd")
    proj = _matmul(h1, w_in_p, tm=1024, tn=896, tk=1024, name="in_proj")
    pre, act = _conv_fwd(proj, conv_w, conv_b)
    ypre, y_ssd, hall = _ssd_fwd(proj, act, dtb, alog, dsk, ssd_norm_w)
    o_att, lse = _att_fwd(proj, qw, kw)
    y_att = _att_norm_fwd(o_att, attn_norm_w)
    ycat = jnp.concatenate([y_ssd, y_att], axis=1)
    mix = _matmul(ycat, w_out, tm=1024, tn=1024, tk=2048, name="out_proj")
    x1, h2 = _norm_mod_fwd(x, norm2_w, scale2, shift2, "norm2_fwd", res=mix, gate=gate1)
    u, act_ff = _matmul(h2, w_ff1, tm=1024, tn=1024, tk=1024, name="ff1", mode="relu2")
    ff = _matmul(act_ff, w_ff2, tm=1024, tn=1024, tk=2048, name="ff2")
    loss, dout, dff, dgate2 = _loss_head(x1, ff, gate2, tgt)

    du = _matmul(dff, w_ff2, tb=True, tm=1024, tn=1024, tk=1024, out_dtype=BF16, name="ff2_dx", mode="drelu2", u=u)
    g_ff2 = _matmul(act_ff, dff, ta=True, tm=1024, tn=1024, tk=1024, name="ff2_dw")
    dh2 = _matmul(du, w_ff1, tb=True, tm=1024, tn=1024, tk=2048, name="ff1_dx")
    g_ff1 = _matmul(h2, du, ta=True, tm=1024, tn=1024, tk=1024, name="ff1_dw")
    dx1, dshift2, dscale2, g_norm2, dmix, dgate1 = _norm_mod_bwd(dh2, x1, dout, norm2_w, scale2, "norm2_bwd", gate=gate1, mix=mix)

    dycat = _matmul(dmix, w_out, tb=True, tm=1024, tn=1024, tk=1024, name="out_proj_dx")
    g_out = _matmul(ycat, dmix, ta=True, tm=1024, tn=1024, tk=1024, name="out_proj_dw")
    do, stats, g_attn_norm = _att_norm_bwd(dycat, o_att, lse, attn_norm_w)
    dq, dk, dv, dqw, dkw = _att_bwd(proj, do, stats, qw, kw)
    dz, dact, ddtr, da, g_dsk, g_dtb, g_ssd_norm = _ssd_bwd(dycat, ypre, proj, act, hall, dtb, alog, dsk, ssd_norm_w)
    dxbc, g_conv_w, g_conv_b = _conv_bwd(dact, pre, proj, conv_w)
    dproj = jnp.concatenate([dz, dxbc, dq, dk, dv, ddtr], axis=1)
    dh1 = _matmul(dproj, w_in_p, tb=True, tm=1024, tn=1024, tk=896, name="in_proj_dx")
    g_in_p = _matmul(h1, dproj, ta=True, tm=1024, tn=896, tk=1024, name="in_proj_dw")
    grad_x, dshift1, dscale1, g_norm1 = _norm_mod_bwd(dh1, x, dx1, norm1_w, scale1, "norm1_bwd")

    dmod = jnp.concatenate([dshift1, dscale1, dgate1, dshift2, dscale2, dgate2], axis=0)
    g_alog = da[:, :SSD_HEADS] * (-jnp.exp(a_log))
    g_qw = dqw.reshape(ATT_HEADS, HEAD_DIM).sum(axis=0, keepdims=True)
    g_kw = dkw.reshape(ATT_HEADS, HEAD_DIM).sum(axis=0, keepdims=True)
    return dict(loss=loss, grad_x=grad_x, dmod=dmod, norm1_w=g_norm1, norm2_w=g_norm2, w_in_p=g_in_p, conv_w=g_conv_w,
                conv_b=g_conv_b, dt_bias=g_dtb[:, :SSD_HEADS], a_log=g_alog, d_skip=g_dsk[:, :SSD_HEADS],
                ssd_norm_w=g_ssd_norm, q_norm_w=g_qw, k_norm_w=g_kw, attn_norm_w=g_attn_norm, w_out=g_out,
                w_ff1=g_ff1, w_ff2=g_ff2)


def _pack_w_in(w_full):
    o_dt = SSD_D_INNER + CONV_CH
    o_q = o_dt + SSD_HEADS
    pad = jnp.zeros((w_full.shape[0], LANE - SSD_HEADS), w_full.dtype)
    return jnp.concatenate([w_full[:, :o_dt], w_full[:, o_q:], w_full[:, o_dt:o_q], pad], axis=1)


def _unpack_w_in(g_p):
    return jnp.concatenate([g_p[:, :OFF_Q], g_p[:, OFF_DT:OFF_DT + SSD_HEADS], g_p[:, OFF_Q:OFF_DT]], axis=1)


MISC_FIELDS = (("dt_bias", SSD_HEADS), ("a_log", SSD_HEADS), ("d_skip", SSD_HEADS), ("q_norm_w", HEAD_DIM), ("k_norm_w", HEAD_DIM))
SMALL_LAYOUT = (("b_ada", 6), ("norm1_w", 1), ("norm2_w", 1), ("conv_w", 8), ("conv_b", 2), ("ssd_norm_w", 1),
                ("attn_norm_w", 1), ("misc", 1))


def _pack_small(vals):
    rows = []
    for name, nrow in SMALL_LAYOUT:
        if name == "misc":
            misc = jnp.concatenate([vals[f].reshape(1, n) for f, n in MISC_FIELDS], axis=1)
            rows.append(_pad_lanes(misc, D_MODEL))
        elif name in vals:
            rows.append(vals[name].reshape(nrow, D_MODEL))
        else:
            rows.append(jnp.zeros((nrow, D_MODEL), F32))
    used = sum(n for _, n in SMALL_LAYOUT)
    rows.append(jnp.zeros((SMALL_ROWS - used, D_MODEL), F32))
    return jnp.concatenate(rows, axis=0)


def _unpack_small(packed):
    out, r = {}, 0
    for name, nrow in SMALL_LAYOUT:
        blk = packed[r:r + nrow]
        r += nrow
        if name == "misc":
            c0 = 0
            for f, n in MISC_FIELDS:
                out[f] = blk[:, c0:c0 + n]
                c0 += n
        elif name == "b_ada":
            out[name] = blk.reshape(1, N_MOD * D_MODEL)
        elif name == "conv_w":
            out[name] = blk.reshape(CONV_K, CONV_CH)
        elif name == "conv_b":
            out[name] = blk.reshape(1, CONV_CH)
        else:
            out[name] = blk
    return out


WEIGHT_NAMES = ("norm1_w", "norm2_w", "w_ada", "b_ada", "w_in", "conv_w", "conv_b", "dt_bias", "a_log", "d_skip",
                "ssd_norm_w", "q_norm_w", "k_norm_w", "attn_norm_w", "w_out", "w_ff1", "w_ff2")
SMALL_NAMES = ("norm1_w", "norm2_w", "b_ada", "conv_b", "dt_bias", "a_log", "d_skip", "ssd_norm_w", "q_norm_w",
               "k_norm_w", "attn_norm_w")


def kernel(x, c, norm1_w, norm2_w, w_ada, b_ada, w_in, conv_w, conv_b, dt_bias, a_log, d_skip, ssd_norm_w, q_norm_w, k_norm_w, attn_norm_w, w_out, w_ff1, w_ff2, loss_target, m_norm1_w, m_norm2_w, m_w_ada, m_b_ada, m_w_in, m_conv_w, m_conv_b, m_dt_bias, m_a_log, m_d_skip, m_ssd_norm_w, m_q_norm_w, m_k_norm_w, m_attn_norm_w, m_w_out, m_w_ff1, m_w_ff2, v_norm1_w, v_norm2_w, v_w_ada, v_b_ada, v_w_in, v_conv_w, v_conv_b, v_dt_bias, v_a_log, v_d_skip, v_ssd_norm_w, v_q_norm_w, v_k_norm_w, v_attn_norm_w, v_w_out, v_w_ff1, v_w_ff2):
    args = dict(locals())
    w = {n: args[n] for n in WEIGHT_NAMES}
    m = {n: args["m_" + n] for n in WEIGHT_NAMES}
    v = {n: args["v_" + n] for n in WEIGHT_NAMES}
    me = 4 * lax.axis_index("x") + 2 * lax.axis_index("y") + lax.axis_index("c")

    c_rows = jnp.pad(c, ((0, 7), (0, 0)))
    c_g, conv_g, w_in_g, w_out_g, w_ff1_g, w_ff2_g = _exchange(
        [c_rows, w["conv_w"][0], w["w_in"][0].astype(BF16), w["w_out"][0].astype(BF16), w["w_ff1"][0].astype(BF16),
         w["w_ff2"][0].astype(BF16)], "gather_weights", scatter=False)
    c_all = c_g[:, 0, :]
    conv_full = jnp.transpose(conv_g, (1, 0, 2)).reshape(CONV_K, CONV_CH)
    w_in_p = _pack_w_in(jnp.transpose(w_in_g, (1, 0, 2)).reshape(D_MODEL, IN_W))
    w_out_f = w_out_g.reshape(2 * D_MODEL, D_MODEL)
    w_ff1_f = jnp.transpose(w_ff1_g, (1, 0, 2)).reshape(D_MODEL, D_FF)
    w_ff2_f = w_ff2_g.reshape(D_FF, D_MODEL)

    mod_part = _ada_fwd(c_all, w["w_ada"][0])
    (mod_g,) = _exchange([mod_part], "gather_mod", scatter=False)
    mod_mine = lax.dynamic_index_in_dim(mod_g, me, axis=1, keepdims=False).reshape(1, N_MOD * D_MODEL) + w["b_ada"]
    mod = mod_mine.reshape(N_MOD, D_MODEL)

    res = _local_step(x[0], loss_target[0], mod, w_in_p, conv_full, w["conv_b"], w["dt_bias"], w["a_log"], w["d_skip"],
                      w["ssd_norm_w"], w["q_norm_w"], w["k_norm_w"], w["attn_norm_w"], w_out_f, w_ff1_f, w_ff2_f,
                      w["norm1_w"], w["norm2_w"])

    g_in = _unpack_w_in(res["w_in_p"]).astype(BF16).reshape(D_MODEL, N_DEV, IN_W // N_DEV).transpose(1, 0, 2)
    g_out = res["w_out"].astype(BF16).reshape(N_DEV, 2 * D_MODEL // N_DEV, D_MODEL)
    g_ff1 = res["w_ff1"].astype(BF16).reshape(D_MODEL, N_DEV, D_FF // N_DEV).transpose(1, 0, 2)
    g_ff2 = res["w_ff2"].astype(BF16).reshape(N_DEV, D_FF // N_DEV, D_MODEL)
    small_vals = {n: res[n] for n in SMALL_NAMES if n != "b_ada"}
    small_vals["b_ada"] = res["dmod"]
    small_vals["conv_w"] = res["conv_w"]
    s_in, s_out, s_ff1, s_ff2 = _exchange([g_in, g_out, g_ff1, g_ff2], "scatter_grads", scatter=True)
    (small_g,) = _exchange([_pack_small(small_vals)], "gather_small", scatter=False)

    grads, delta, new_m, new_v = {}, {}, {}, {}
    for name, slabs in (("w_in", s_in), ("w_out", s_out), ("w_ff1", s_ff1), ("w_ff2", s_ff2)):
        outs = _reduce_adamw(slabs, w[name][0], m[name][0], v[name][0], "adamw_" + name)
        grads[name], delta[name], new_m[name], new_v[name] = [o[None] for o in outs]

    sm = _small_reduce_adamw(small_g, _pack_small({n: w[n] for n in SMALL_NAMES}), _pack_small({n: m[n] for n in SMALL_NAMES}),
                             _pack_small({n: v[n] for n in SMALL_NAMES}))
    sm = [_unpack_small(p) for p in sm]
    for n in SMALL_NAMES:
        grads[n], delta[n], new_m[n], new_v[n] = [p[n] for p in sm]
    shard_w = CONV_CH // N_DEV
    g_conv = lax.dynamic_slice_in_dim(sm[0]["conv_w"], me * shard_w, shard_w, axis=1)
    cw = _adamw_small(g_conv, w["conv_w"][0], m["conv_w"][0], v["conv_w"][0], "adamw_conv_w")
    grads["conv_w"] = g_conv[None]
    delta["conv_w"], new_m["conv_w"], new_v["conv_w"] = [o[None] for o in cw]

    ada_w = w_ada.shape[2]
    dmod_all = small_g[:, :N_MOD, :].reshape(N_DEV, N_MOD * D_MODEL)
    dmod_cols = lax.dynamic_slice_in_dim(dmod_all, me * ada_w, ada_w, axis=1)
    outs = _ada_bwd_adamw(c_all, dmod_cols, w["w_ada"][0], m["w_ada"][0], v["w_ada"][0])
    grads["w_ada"], delta["w_ada"], new_m["w_ada"], new_v["w_ada"] = [o[None] for o in outs]

    loss = lax.psum(res["loss"][0, 0], ("x", "y", "c"))
    return (loss, res["grad_x"][None], *[grads[n] for n in WEIGHT_NAMES], *[delta[n] for n in WEIGHT_NAMES],
            *[new_m[n] for n in WEIGHT_NAMES], *[new_v[n] for n in WEIGHT_NAMES])
```

```python
import functools

import jax
import jax.numpy as jnp
from jax import lax
from jax.experimental import pallas as pl
from jax.experimental.pallas import tpu as pltpu

F32 = jnp.float32
BF16 = jnp.bfloat16
HIGHEST = lax.Precision.HIGHEST
MESH_IDS = pl.DeviceIdType.MESH

N_DEV = 8
D_MODEL = 1024
HEAD_DIM = 64
SSD_HEADS = 16
SSD_GROUPS = 4
HEADS_PER_GROUP = SSD_HEADS // SSD_GROUPS
SSD_STATE = 128
SSD_CHUNK = 128
SSD_D_INNER = SSD_HEADS * HEAD_DIM
GROUP_WIDTH = SSD_D_INNER // SSD_GROUPS
CONV_K = 4
CONV_CH = SSD_D_INNER + 2 * SSD_GROUPS * SSD_STATE
ATT_HEADS = 16
ATT_D = ATT_HEADS * HEAD_DIM
ATT_BLK = 128
DILATIONS = (1, 4, 16)
D_FF = 4 * D_MODEL
N_MOD = 6
EPS = 1e-6
IN_W = SSD_D_INNER + CONV_CH + SSD_HEADS + 3 * ATT_D
LANE = 128
OFF_Z, OFF_XBC, OFF_Q, OFF_K, OFF_V, OFF_DT = 0, 1024, 3072, 4096, 5120, 6144
IN_WP = OFF_DT + LANE

ADAM_LR, ADAM_B1, ADAM_B2, ADAM_EPS, ADAM_WD, ADAM_STEP = 0.001, 0.9, 0.999, 1e-08, 0.01, 10
VMEM_LIMIT = 56 * 1024 * 1024
ROW_TILE = 512
SMALL_ROWS = 24


def _cparams(sem=None):
    return pltpu.CompilerParams(dimension_semantics=sem, vmem_limit_bytes=VMEM_LIMIT)


def _sigmoid(v):
    return 1.0 / (1.0 + jnp.exp(-v))


def _softplus(v):
    y = jnp.exp(-jnp.abs(v))
    small = y * (1.0 - y * (0.5 - y * (1.0 / 3.0)))
    return jnp.maximum(v, 0.0) + jnp.where(y < 0.01, small, jnp.log(1.0 + y))


def _dot(a, b, dims, precision=None):
    return lax.dot_general(a, b, (dims, ((), ())), preferred_element_type=F32, precision=precision)


NN = ((1,), (0,))
NT = ((1,), (1,))
TN = ((0,), (0,))


def _matmul(a, b, *, ta=False, tb=False, tm, tn, tk, out_dtype=F32, name, mode=None, u=None, comm=None):
    m, k = (a.shape[1], a.shape[0]) if ta else a.shape
    n = b.shape[0] if tb else b.shape[1]
    assert m % tm == 0 and n % tn == 0 and k % tk == 0, (name, m, n, k)
    nk = k // tk
    a_spec = pl.BlockSpec((tk, tm), lambda i, j, kk: (kk, i)) if ta else pl.BlockSpec((tm, tk), lambda i, j, kk: (i, kk))
    b_spec = pl.BlockSpec((tn, tk), lambda i, j, kk: (j, kk)) if tb else pl.BlockSpec((tk, tn), lambda i, j, kk: (kk, j))
    o_spec = pl.BlockSpec((tm, tn), lambda i, j, kk: (i, j))
    dims = ((0,) if ta else (1,), (1,) if tb else (0,))
    n_out = 2 if mode == "relu2" else 1

    def body(*refs):
        if mode == "drelu2":
            a_ref, b_ref, u_ref = refs[:3]
            rest = refs[3:]
        else:
            a_ref, b_ref = refs[:2]
            u_ref = None
            rest = refs[2:]
        outs = rest[:n_out]
        part = _dot(a_ref[...], b_ref[...], dims)

        def finish(r):
            if mode == "relu2":
                outs[0][...] = r.astype(BF16)
                rr = jnp.maximum(r, 0.0)
                outs[1][...] = (rr * rr).astype(BF16)
            elif mode == "drelu2":
                outs[0][...] = (r * (2.0 * jnp.maximum(u_ref[...].astype(F32), 0.0))).astype(out_dtype)
            else:
                outs[0][...] = r.astype(out_dtype)

        if nk == 1:
            finish(part)
        else:
            acc = rest[n_out]
            kk = pl.program_id(2)

            @pl.when(kk == 0)
            def _():
                acc[...] = part

            @pl.when(kk > 0)
            def _():
                acc[...] += part

            @pl.when(kk == nk - 1)
            def _():
                finish(acc[...])

    in_specs = [a_spec, b_spec]
    args = [a, b]
    if mode == "drelu2":
        in_specs.append(o_spec)
        args.append(u)
    if mode == "relu2":
        out_shape = [jax.ShapeDtypeStruct((m, n), BF16), jax.ShapeDtypeStruct((m, n), BF16)]
    else:
        out_shape = [jax.ShapeDtypeStruct((m, n), out_dtype)]
    outs, comm_outs = _pcall(
        body, args, name=name, grid=(m // tm, n // tn, nk), in_specs=in_specs, out_specs=[o_spec] * n_out,
        out_shape=out_shape, scratch_shapes=[pltpu.VMEM((tm, tn), F32)] if nk > 1 else [],
        sem=("parallel", "parallel", "arbitrary"), comm=comm)
    res = tuple(outs) if mode == "relu2" else outs[0]
    return res if comm is None else (res, comm_outs)


def _norm_mod_fwd(x, nw, scale, shift, name, res=None, gate=None):
    s, d = x.shape
    row = pl.BlockSpec((ROW_TILE, d), lambda i: (i, 0))
    vec = pl.BlockSpec((1, d), lambda i: (0, 0))
    with_res = res is not None

    def body(*refs):
        if with_res:
            x_ref, res_ref, gate_ref, nw_ref, sc_ref, sh_ref, x1_ref, h_ref = refs
            xv = x_ref[...] + gate_ref[...] * res_ref[...]
            x1_ref[...] = xv
        else:
            x_ref, nw_ref, sc_ref, sh_ref, h_ref = refs
            xv = x_ref[...]
        r = lax.rsqrt(jnp.mean(xv * xv, axis=-1, keepdims=True) + EPS)
        h_ref[...] = ((xv * r) * nw_ref[...] * (1.0 + sc_ref[...]) + sh_ref[...]).astype(BF16)

    if with_res:
        in_specs = [row, row, vec, vec, vec, vec]
        args = (x, res, gate, nw, scale, shift)
        out_shape = (jax.ShapeDtypeStruct((s, d), F32), jax.ShapeDtypeStruct((s, d), BF16))
        out_specs = (row, row)
    else:
        in_specs = [row, vec, vec, vec]
        args = (x, nw, scale, shift)
        out_shape = jax.ShapeDtypeStruct((s, d), BF16)
        out_specs = row
    return pl.pallas_call(body, name=name, grid=(s // ROW_TILE,), in_specs=in_specs, out_specs=out_specs,
                          out_shape=out_shape, compiler_params=_cparams(("parallel",)))(*args)


def _norm_mod_bwd(dh, xin, dres, nw, scale, name, gate=None, mix=None):
    s, d = xin.shape
    row = pl.BlockSpec((ROW_TILE, d), lambda i: (i, 0))
    vec = pl.BlockSpec((1, d), lambda i: (0, 0))
    with_gate = gate is not None

    def body(*refs):
        if with_gate:
            dh_ref, x_ref, dres_ref, nw_ref, sc_ref, gate_ref, mix_ref, dx_ref, dsh_ref, dsc_ref, dnw_ref, dmix_ref, dg_ref = refs
        else:
            dh_ref, x_ref, dres_ref, nw_ref, sc_ref, dx_ref, dsh_ref, dsc_ref, dnw_ref = refs
        i = pl.program_id(0)

        @pl.when(i == 0)
        def _():
            dsh_ref[...] = jnp.zeros_like(dsh_ref)
            dsc_ref[...] = jnp.zeros_like(dsc_ref)
            dnw_ref[...] = jnp.zeros_like(dnw_ref)
            if with_gate:
                dg_ref[...] = jnp.zeros_like(dg_ref)

        xv = x_ref[...]
        dhv = dh_ref[...]
        r = lax.rsqrt(jnp.mean(xv * xv, axis=-1, keepdims=True) + EPS)
        nrm = xv * r
        one_sc = 1.0 + sc_ref[...]
        dhn = dhv * nrm
        dsh_ref[...] += jnp.sum(dhv, axis=0, keepdims=True)
        dsc_ref[...] += jnp.sum(dhn, axis=0, keepdims=True) * nw_ref[...]
        dnw_ref[...] += jnp.sum(dhn, axis=0, keepdims=True) * one_sc
        dn = dhv * (nw_ref[...] * one_sc)
        dx = dres_ref[...] + r * (dn - nrm * jnp.mean(dn * nrm, axis=-1, keepdims=True))
        dx_ref[...] = dx
        if with_gate:
            dmix_ref[...] = (gate_ref[...] * dx).astype(BF16)
            dg_ref[...] += jnp.sum(dx * mix_ref[...], axis=0, keepdims=True)

    vshape = jax.ShapeDtypeStruct((1, d), F32)
    in_specs = [row, row, row, vec, vec]
    args = [dh, xin, dres, nw, scale]
    out_shape = [jax.ShapeDtypeStruct((s, d), F32), vshape, vshape, vshape]
    out_specs = [row, vec, vec, vec]
    if with_gate:
        in_specs += [vec, row]
        args += [gate, mix]
        out_shape += [jax.ShapeDtypeStruct((s, d), BF16), vshape]
        out_specs += [row, vec]
    return pl.pallas_call(body, name=name, grid=(s // ROW_TILE,), in_specs=in_specs, out_specs=out_specs,
                          out_shape=out_shape, compiler_params=_cparams(("arbitrary",)))(*args)


def _loss_head(x1, ff, gate2, tgt):
    s, d = x1.shape
    row = pl.BlockSpec((ROW_TILE, d), lambda i: (i, 0))
    vec = pl.BlockSpec((1, d), lambda i: (0, 0))
    one = pl.BlockSpec((1, 1), lambda i: (0, 0))

    def body(x1_ref, ff_ref, g_ref, t_ref, loss_ref, dout_ref, dff_ref, dg_ref):
        i = pl.program_id(0)

        @pl.when(i == 0)
        def _():
            loss_ref[...] = jnp.zeros_like(loss_ref)
            dg_ref[...] = jnp.zeros_like(dg_ref)

        ffv = ff_ref[...]
        err = x1_ref[...] + g_ref[...] * ffv - t_ref[...]
        loss_ref[...] += (0.5 / d) * jnp.sum(err * err).reshape(1, 1)
        dout = err * (1.0 / d)
        dout_ref[...] = dout
        dff_ref[...] = (g_ref[...] * dout).astype(BF16)
        dg_ref[...] += jnp.sum(dout * ffv, axis=0, keepdims=True)

    return pl.pallas_call(
        body, name="loss_head", grid=(s // ROW_TILE,), in_specs=[row, row, vec, row],
        out_specs=[one, row, row, vec],
        out_shape=[jax.ShapeDtypeStruct((1, 1), F32), jax.ShapeDtypeStruct((s, d), F32),
                   jax.ShapeDtypeStruct((s, d), BF16), jax.ShapeDtypeStruct((1, d), F32)],
        compiler_params=_cparams(("arbitrary",)))(x1, ff, gate2, tgt)


CONV_COLS = 256
HALO = 8


def _shift_down(cur, halo, k):
    if k == 0:
        return cur
    rolled = pltpu.roll(cur, k, axis=0)
    top = jnp.where(lax.broadcasted_iota(jnp.int32, halo.shape, 0) < k, pltpu.roll(halo, k, axis=0), rolled[:HALO])
    return jnp.concatenate([top, rolled[HALO:]], axis=0)


def _shift_up(cur, halo, k):
    if k == 0:
        return cur
    t = cur.shape[0]
    rolled = pltpu.roll(cur, t - k, axis=0)
    bot = jnp.where(lax.broadcasted_iota(jnp.int32, halo.shape, 0) >= HALO - k, pltpu.roll(halo, HALO - k, axis=0),
                    rolled[t - HALO:])
    return jnp.concatenate([rolled[:t - HALO], bot], axis=0)


def _conv_fwd(proj, conv_w, conv_b):
    s = proj.shape[0]
    nr = s // ROW_TILE
    cb0 = OFF_XBC // CONV_COLS
    hb = ROW_TILE // HALO
    cur = pl.BlockSpec((ROW_TILE, CONV_COLS), lambda j, r: (r, cb0 + j))
    prev = pl.BlockSpec((HALO, CONV_COLS), lambda j, r: (jnp.maximum(r * hb - 1, 0), cb0 + j))
    out = pl.BlockSpec((ROW_TILE, CONV_COLS), lambda j, r: (r, j))

    def body(u_ref, up_ref, w_ref, b_ref, pre_ref, act_ref):
        r = pl.program_id(1)
        u = u_ref[...]
        halo = jnp.where(r > 0, up_ref[...], 0.0)
        acc = b_ref[...] + w_ref[CONV_K - 1:CONV_K, :] * u
        for k in range(1, CONV_K):
            acc = acc + w_ref[CONV_K - 1 - k:CONV_K - k, :] * _shift_down(u, halo, k)
        pre_ref[...] = acc
        act_ref[...] = acc * _sigmoid(acc)

    return pl.pallas_call(
        body, name="conv_fwd", grid=(CONV_CH // CONV_COLS, nr),
        in_specs=[cur, prev, pl.BlockSpec((CONV_K, CONV_COLS), lambda j, r: (0, j)),
                  pl.BlockSpec((1, CONV_COLS), lambda j, r: (0, j))],
        out_specs=[out, out],
        out_shape=[jax.ShapeDtypeStruct((s, CONV_CH), F32), jax.ShapeDtypeStruct((s, CONV_CH), F32)],
        compiler_params=_cparams(("parallel", "arbitrary")))(proj, proj, conv_w, conv_b)


def _conv_bwd(dact, pre, proj, conv_w):
    s = proj.shape[0]
    nr = s // ROW_TILE
    cb0 = OFF_XBC // CONV_COLS
    hb = ROW_TILE // HALO
    last_halo = s // HALO - 1
    cur = pl.BlockSpec((ROW_TILE, CONV_COLS), lambda j, r: (r, j))
    nxt = pl.BlockSpec((HALO, CONV_COLS), lambda j, r: (jnp.minimum((r + 1) * hb, last_halo), j))
    ucur = pl.BlockSpec((ROW_TILE, CONV_COLS), lambda j, r: (r, cb0 + j))
    uprev = pl.BlockSpec((HALO, CONV_COLS), lambda j, r: (jnp.maximum(r * hb - 1, 0), cb0 + j))
    wspec = pl.BlockSpec((CONV_K, CONV_COLS), lambda j, r: (0, j))
    bspec = pl.BlockSpec((1, CONV_COLS), lambda j, r: (0, j))

    def dsilu(p):
        sg = _sigmoid(p)
        return sg * (1.0 + p * (1.0 - sg))

    def body(da_ref, dan_ref, pre_ref, pren_ref, u_ref, up_ref, w_ref, du_ref, dw_ref, db_ref):
        r = pl.program_id(1)

        @pl.when(r == 0)
        def _():
            dw_ref[...] = jnp.zeros_like(dw_ref)
            db_ref[...] = jnp.zeros_like(db_ref)

        dpre = da_ref[...] * dsilu(pre_ref[...])
        dnext = jnp.where(r < nr - 1, dan_ref[...] * dsilu(pren_ref[...]), 0.0)
        u = u_ref[...]
        halo = jnp.where(r > 0, up_ref[...], 0.0)
        du = w_ref[CONV_K - 1:CONV_K, :] * dpre
        dws = [jnp.sum(dpre * u, axis=0, keepdims=True)]
        for k in range(1, CONV_K):
            du = du + w_ref[CONV_K - 1 - k:CONV_K - k, :] * _shift_up(dpre, dnext, k)
            dws.append(jnp.sum(dpre * _shift_down(u, halo, k), axis=0, keepdims=True))
        du_ref[...] = du.astype(BF16)
        dw_ref[...] += jnp.concatenate(dws[::-1], axis=0)
        db_ref[...] += jnp.sum(dpre, axis=0, keepdims=True)

    return pl.pallas_call(
        body, name="conv_bwd", grid=(CONV_CH // CONV_COLS, nr),
        in_specs=[cur, nxt, cur, nxt, ucur, uprev, wspec],
        out_specs=[cur, wspec, bspec],
        out_shape=[jax.ShapeDtypeStruct((s, CONV_CH), BF16), jax.ShapeDtypeStruct((CONV_K, CONV_CH), F32),
                   jax.ShapeDtypeStruct((1, CONV_CH), F32)],
        compiler_params=_cparams(("parallel", "arbitrary")))(dact, dact, pre, pre, proj, proj, conv_w)


def _ssd_common(dtr, dtb, alog):
    lane = lax.broadcasted_iota(jnp.int32, (1, LANE), 1)
    head_lane = lane < SSD_HEADS
    dt = jnp.where(head_lane, _softplus(dtr + dtb), 0.0)
    a = jnp.where(head_lane, -jnp.exp(alog), 0.0)
    row = lax.broadcasted_iota(jnp.int32, (SSD_CHUNK, SSD_CHUNK), 0)
    col = lax.broadcasted_iota(jnp.int32, (SSD_CHUNK, SSD_CHUNK), 1)
    tril = row >= col
    cs = _dot(tril.astype(F32), dt * a, NN, precision=HIGHEST)
    return dt, a, cs, cs.T, tril, lane


def _ssd_fwd(proj, act, dtb, alog, dsk, nw):
    s = proj.shape[0]
    nc = s // SSD_CHUNK
    bc_w = SSD_GROUPS * SSD_STATE

    def body(z_ref, dtr_ref, xs_ref, b_ref, c_ref, dtb_ref, alog_ref, dsk_ref, nw_ref,
             ypre_ref, yssd_ref, hall_ref, h_scr, y_scr):
        @pl.when(pl.program_id(0) == 0)
        def _():
            h_scr[...] = jnp.zeros_like(h_scr)

        dt, a, cs, cst, tril, lane = _ssd_common(dtr_ref[...], dtb_ref[...], alog_ref[...])
        for g in range(SSD_GROUPS):
            bg = b_ref[:, g * SSD_STATE:(g + 1) * SSD_STATE].astype(BF16)
            cg = c_ref[:, g * SSD_STATE:(g + 1) * SSD_STATE].astype(BF16)
            cb = _dot(cg, bg, NT)
            for e in range(HEADS_PER_GROUP):
                h = g * HEADS_PER_GROUP + e
                hs = slice(h * HEAD_DIM, (h + 1) * HEAD_DIM)
                cs_col = cs[:, h:h + 1]
                cs_last = cs[SSD_CHUNK - 1:SSD_CHUNK, h:h + 1]
                lm = jnp.exp(jnp.where(tril, cs_col - cst[h:h + 1, :], -1e30))
                xs_h = xs_ref[:, hs]
                xdt = xs_h * dt[:, h:h + 1]
                hprev = h_scr[h]
                hall_ref[0, hs, :] = hprev
                y = _dot((cb * lm).astype(BF16), xdt.astype(BF16), NN)
                y = y + jnp.exp(cs_col) * _dot(cg, hprev.astype(BF16), NT)
                y_scr[:, hs] = y + dsk_ref[:, h:h + 1] * xs_h
                xdec = (xdt * jnp.exp(cs_last - cs_col)).astype(BF16)
                h_scr[h] = hprev * jnp.exp(cs_last) + _dot(xdec, bg, TN)
        y = y_scr[...]
        ypre_ref[...] = y
        z = z_ref[...]
        yg = y * (z * _sigmoid(z))
        for g in range(SSD_GROUPS):
            gs = slice(g * GROUP_WIDTH, (g + 1) * GROUP_WIDTH)
            seg = yg[:, gs]
            r = lax.rsqrt(jnp.mean(seg * seg, axis=-1, keepdims=True) + EPS)
            yssd_ref[:, gs] = (seg * r * nw_ref[:, gs]).astype(BF16)

    row_d = lambda cb: pl.BlockSpec((SSD_CHUNK, SSD_D_INNER), lambda c: (c, cb))
    small = pl.BlockSpec((1, LANE), lambda c: (0, 0))
    return pl.pallas_call(
        body, name="ssd_fwd", grid=(nc,),
        in_specs=[row_d(OFF_Z // SSD_D_INNER),
                  pl.BlockSpec((SSD_CHUNK, LANE), lambda c: (c, OFF_DT // LANE)),
                  row_d(0),
                  pl.BlockSpec((SSD_CHUNK, bc_w), lambda c: (c, SSD_D_INNER // bc_w)),
                  pl.BlockSpec((SSD_CHUNK, bc_w), lambda c: (c, SSD_D_INNER // bc_w + 1)),
                  small, small, small, pl.BlockSpec((1, SSD_D_INNER), lambda c: (0, 0))],
        out_specs=[row_d(0), row_d(0), pl.BlockSpec((1, SSD_D_INNER, SSD_STATE), lambda c: (c, 0, 0))],
        out_shape=[jax.ShapeDtypeStruct((s, SSD_D_INNER), F32), jax.ShapeDtypeStruct((s, SSD_D_INNER), BF16),
                   jax.ShapeDtypeStruct((nc, SSD_D_INNER, SSD_STATE), F32)],
        scratch_shapes=[pltpu.VMEM((SSD_HEADS, HEAD_DIM, SSD_STATE), F32), pltpu.VMEM((SSD_CHUNK, SSD_D_INNER), F32)],
        compiler_params=_cparams(("arbitrary",)))(proj, proj, act, act, act, dtb, alog, dsk, nw)


def _ssd_bwd(dycat, ypre, proj, act, hall, dtb, alog, dsk, nw, comm=None):
    s = proj.shape[0]
    nc = s // SSD_CHUNK
    bc_w = SSD_GROUPS * SSD_STATE

    def body(dy_ref, ypre_ref, z_ref, dtr_ref, xs_ref, b_ref, c_ref, hall_ref, dtb_ref, alog_ref, dsk_ref, nw_ref,
             dz_ref, dact_ref, ddtr_ref, da_ref, ddsk_ref, ddtb_ref, dnw_ref, dh_scr, dyh_scr):
        @pl.when(pl.program_id(0) == 0)
        def _():
            dh_scr[...] = jnp.zeros_like(dh_scr)
            da_ref[...] = jnp.zeros_like(da_ref)
            ddsk_ref[...] = jnp.zeros_like(ddsk_ref)
            ddtb_ref[...] = jnp.zeros_like(ddtb_ref)
            dnw_ref[...] = jnp.zeros_like(dnw_ref)

        z = z_ref[...]
        sg = _sigmoid(z)
        sz = z * sg
        ypre = ypre_ref[...]
        yg = ypre * sz
        dyg_parts = []
        for g in range(SSD_GROUPS):
            gs = slice(g * GROUP_WIDTH, (g + 1) * GROUP_WIDTH)
            seg = yg[:, gs]
            r = lax.rsqrt(jnp.mean(seg * seg, axis=-1, keepdims=True) + EPS)
            nrm = seg * r
            dyo = dy_ref[:, gs]
            dnw_ref[:, gs] += jnp.sum(dyo * nrm, axis=0, keepdims=True)
            dn = dyo * nw_ref[:, gs]
            dyg_parts.append(r * (dn - nrm * jnp.mean(dn * nrm, axis=-1, keepdims=True)))
        dyg = jnp.concatenate(dyg_parts, axis=1)
        dz_ref[...] = (dyg * ypre * (sg * (1.0 + z * (1.0 - sg)))).astype(BF16)
        dyh_scr[...] = dyg * sz

        dtr = dtr_ref[...]
        dt, a, cs, cst, tril, lane = _ssd_common(dtr, dtb_ref[...], alog_ref[...])
        sub = lax.broadcasted_iota(jnp.int32, (SSD_CHUNK, 1), 0)
        last_row = sub == SSD_CHUNK - 1
        dcs_col = jnp.zeros((SSD_CHUNK, LANE), F32)
        dcs_row = jnp.zeros((SSD_CHUNK, LANE), F32)
        ddt = jnp.zeros((SSD_CHUNK, LANE), F32)
        ddsk = jnp.zeros((1, LANE), F32)
        for g in range(SSD_GROUPS):
            bsl = slice(g * SSD_STATE, (g + 1) * SSD_STATE)
            bgf = b_ref[:, bsl]
            bg = bgf.astype(BF16)
            cg = c_ref[:, bsl].astype(BF16)
            cb = _dot(cg, bg, NT)
            dcb = jnp.zeros((SSD_CHUNK, SSD_CHUNK), F32)
            dbg = jnp.zeros((SSD_CHUNK, SSD_STATE), F32)
            dcg = jnp.zeros((SSD_CHUNK, SSD_STATE), F32)
            for e in range(HEADS_PER_GROUP):
                h = g * HEADS_PER_GROUP + e
                hs = slice(h * HEAD_DIM, (h + 1) * HEAD_DIM)
                dt_h = dt[:, h:h + 1]
                cs_col = cs[:, h:h + 1]
                cs_last = cs[SSD_CHUNK - 1:SSD_CHUNK, h:h + 1]
                ecs = jnp.exp(cs_col)
                decay = jnp.exp(cs_last - cs_col)
                cd = jnp.exp(cs_last)
                lm = jnp.exp(jnp.where(tril, cs_col - cst[h:h + 1, :], -1e30))
                gm = cb * lm
                gmb = gm.astype(BF16)
                xs_h = xs_ref[:, hs]
                xdt = xs_h * dt_h
                xdtb = xdt.astype(BF16)
                hprev = hall_ref[0, hs, :]
                hb = hprev.astype(BF16)
                dhn = dh_scr[h]
                dhb = dhn.astype(BF16)
                dyh = dyh_scr[:, hs]
                dyb = dyh.astype(BF16)
                w_off = _dot(cg, hb, NT)
                dyo = dyh * ecs
                dyob = dyo.astype(BF16)
                dcg = dcg + _dot(dyob, hb, NN)
                dh_y = _dot(dyob, cg, TN)
                dcs_c = jnp.sum(dyo * w_off, axis=-1, keepdims=True)
                dg = _dot(dyb, xdtb, NT)
                dxdt = _dot(gmb, dyb, TN)
                mm = dg * gm
                dcs_c = dcs_c + jnp.sum(mm, axis=-1, keepdims=True)
                dcs_r = jnp.sum(mm, axis=0, keepdims=True)
                dcb = dcb + dg * lm
                q = _dot(xdtb, dhb, NN)
                dbg = dbg + decay * q
                t1 = decay * jnp.sum(q * bgf, axis=-1, keepdims=True)
                dcs_c = dcs_c - t1
                dxdt = dxdt + decay * _dot(bg, dhb, NT)
                dlast = jnp.sum(t1).reshape(1, 1) + jnp.sum(dhn * hprev).reshape(1, 1) * cd
                dcs_c = dcs_c + jnp.where(last_row, dlast, 0.0)
                dh_scr[h] = dhn * cd + dh_y
                dact_ref[:, hs] = dxdt * dt_h + dsk_ref[:, h:h + 1] * dyh
                ddt_h = jnp.sum(dxdt * xs_h, axis=-1, keepdims=True)
                ddsk = ddsk + jnp.where(lane == h, jnp.sum(dyh * xs_h).reshape(1, 1), 0.0)
                dcs_col = dcs_col + jnp.where(lane == h, dcs_c, 0.0)
                dcs_row = dcs_row + jnp.where(sub == h, dcs_r, 0.0)
                ddt = ddt + jnp.where(lane == h, ddt_h, 0.0)
            dcbb = dcb.astype(BF16)
            dcg = dcg + _dot(dcbb, bg, NN)
            dbg = dbg + _dot(dcbb, cg, TN)
            dact_ref[:, SSD_D_INNER + g * SSD_STATE:SSD_D_INNER + (g + 1) * SSD_STATE] = dbg
            dact_ref[:, SSD_D_INNER + bc_w + g * SSD_STATE:SSD_D_INNER + bc_w + (g + 1) * SSD_STATE] = dcg
        dcs = dcs_col - dcs_row.T
        row = lax.broadcasted_iota(jnp.int32, (SSD_CHUNK, SSD_CHUNK), 0)
        col = lax.broadcasted_iota(jnp.int32, (SSD_CHUNK, SSD_CHUNK), 1)
        dda = _dot((col >= row).astype(F32), dcs, NN, precision=HIGHEST)
        ddt = ddt + dda * a
        da_ref[...] += jnp.sum(dda * dt, axis=0, keepdims=True)
        ddtr = jnp.where(lane < SSD_HEADS, ddt * _sigmoid(dtr + dtb_ref[...]), 0.0)
        ddtr_ref[...] = ddtr.astype(BF16)
        ddtb_ref[...] += jnp.sum(ddtr, axis=0, keepdims=True)
        ddsk_ref[...] += ddsk

    rev = lambda c: nc - 1 - c
    row_d = lambda cb: pl.BlockSpec((SSD_CHUNK, SSD_D_INNER), lambda c: (rev(c), cb))
    small = pl.BlockSpec((1, LANE), lambda c: (0, 0))
    wide = pl.BlockSpec((1, SSD_D_INNER), lambda c: (0, 0))
    small_shape = jax.ShapeDtypeStruct((1, LANE), F32)
    return _pcall(
        body, (dycat, ypre, proj, proj, act, act, act, hall, dtb, alog, dsk, nw), name="ssd_bwd", grid=(nc,),
        in_specs=[row_d(0), row_d(0), row_d(OFF_Z // SSD_D_INNER),
                  pl.BlockSpec((SSD_CHUNK, LANE), lambda c: (rev(c), OFF_DT // LANE)),
                  row_d(0),
                  pl.BlockSpec((SSD_CHUNK, bc_w), lambda c: (rev(c), SSD_D_INNER // bc_w)),
                  pl.BlockSpec((SSD_CHUNK, bc_w), lambda c: (rev(c), SSD_D_INNER // bc_w + 1)),
                  pl.BlockSpec((1, SSD_D_INNER, SSD_STATE), lambda c: (rev(c), 0, 0)),
                  small, small, small, wide],
        out_specs=[row_d(0), pl.BlockSpec((SSD_CHUNK, CONV_CH), lambda c: (rev(c), 0)),
                   pl.BlockSpec((SSD_CHUNK, LANE), lambda c: (rev(c), 0)), small, small, small, wide],
        out_shape=[jax.ShapeDtypeStruct((s, SSD_D_INNER), BF16), jax.ShapeDtypeStruct((s, CONV_CH), F32),
                   jax.ShapeDtypeStruct((s, LANE), BF16), small_shape, small_shape, small_shape,
                   jax.ShapeDtypeStruct((1, SSD_D_INNER), F32)],
        scratch_shapes=[pltpu.VMEM((SSD_HEADS, HEAD_DIM, SSD_STATE), F32), pltpu.VMEM((SSD_CHUNK, SSD_D_INNER), F32)],
        sem=("arbitrary",), comm=comm)


def _head_mean_matrix():
    row = lax.broadcasted_iota(jnp.int32, (LANE, LANE), 0) // HEAD_DIM
    col = lax.broadcasted_iota(jnp.int32, (LANE, LANE), 1) // HEAD_DIM
    return (row == col).astype(F32)


def _head_sum2(v, ones_bd):
    hi = v.astype(BF16)
    lo = (v - hi.astype(F32)).astype(BF16)
    return _dot(hi, ones_bd, NN) + _dot(lo, ones_bd, NN)


def _head_norm(x, w, scale, ones_bd):
    ms = _head_sum2(x * x, ones_bd) * (1.0 / HEAD_DIM)
    return (x * lax.rsqrt(ms + EPS)) * (w * scale)


PRO_ROWS = 256
ATT_UNROLL = 4
KEYS = 2 * ATT_BLK
NEG = -1e30
HALF = HEAD_DIM // 2


def _rows(start, size, dil):
    return pl.ds(start, size) if dil == 1 else pl.ds(start, size, stride=dil)


def _fill_bias(bias_ref):
    row = lax.broadcasted_iota(jnp.int32, (ATT_BLK, 2 * KEYS), 0)
    col = lax.broadcasted_iota(jnp.int32, (ATT_BLK, 2 * KEYS), 1) & (KEYS - 1)
    for first, off in ((0, 0), (1, ATT_BLK)):
        dist = off + row - col
        bias_ref[first] = jnp.where((dist >= 0) & (dist <= ATT_BLK), 0.0, NEG)


def _pair(a, b):
    return jnp.concatenate([jnp.broadcast_to(a, (ATT_BLK, KEYS)), jnp.broadcast_to(b, (ATT_BLK, KEYS))], axis=1)


def _split_heads(x, is_a):
    zero = jnp.zeros_like(x)
    return jnp.concatenate([jnp.where(is_a, x, zero), jnp.where(is_a, zero, x)], axis=0)


def _block_ids(b, nb):
    i = b & (nb - 1)
    q0 = pl.multiple_of(b * ATT_BLK, ATT_BLK)
    k0 = pl.multiple_of((b - jnp.minimum(i, 1)) * ATT_BLK, ATT_BLK)
    return pl.ds(q0, ATT_BLK), pl.ds(k0, KEYS), jnp.minimum(i, 1)


def _att_fwd(proj, qw, kw, comm=None):
    s = proj.shape[0]
    nblk = s // ATT_BLK
    assert all((s // d) // ATT_BLK >= 2 for d in DILATIONS)
    blk = lambda off: pl.BlockSpec((s, LANE), lambda i: (0, off // LANE + i))
    wspec = pl.BlockSpec((1, LANE), lambda i: (0, i))
    oblk = pl.BlockSpec((s, LANE), lambda i: (0, i))

    def body(q_ref, k_ref, v_ref, qw_ref, kw_ref, o_ref, lse_ref, qn, kn, q_cm, k_cm, v_cm, m_acc, l_acc, o_d, m_d, l_d, bias):
        ones_bd = _head_mean_matrix().astype(BF16)
        is_a = lax.broadcasted_iota(jnp.int32, (1, LANE), 1) < HEAD_DIM
        ones_ext = _split_heads(jnp.ones((KEYS, LANE), BF16), is_a)
        _fill_bias(bias)

        def pro(j, c):
            rows = pl.ds(pl.multiple_of(j * PRO_ROWS, PRO_ROWS), PRO_ROWS)
            qn[rows, :] = _head_norm(q_ref[rows, :], qw_ref[...], HEAD_DIM ** -0.5, ones_bd)
            kn[rows, :] = _head_norm(k_ref[rows, :], kw_ref[...], 1.0, ones_bd)
            return c

        lax.fori_loop(0, s // PRO_ROWS, pro, 0)

        for dil in DILATIONS:
            ln = s // dil
            nb = ln // ATT_BLK
            o_out, m_out, l_out = (o_ref, m_acc, l_acc) if dil == 1 else (o_d, m_d, l_d)
            for r in range(dil):
                def relayout(j, c, dil=dil, r=r, ln=ln):
                    j0 = pl.multiple_of(j * PRO_ROWS, PRO_ROWS)
                    src = _rows(r + dil * j0, PRO_ROWS, dil)
                    dst = pl.ds(r * ln + j0, PRO_ROWS)
                    q_cm[dst, :] = qn[src, :].astype(BF16)
                    k_cm[dst, :] = kn[src, :].astype(BF16)
                    v_cm[dst, :] = v_ref[src, :].astype(BF16)
                    return c

                lax.fori_loop(0, ln // PRO_ROWS, relayout, 0)

            def step(b, c, nb=nb, o_out=o_out, m_out=m_out, l_out=l_out):
                qrows, krows, first = _block_ids(b, nb)
                kb = _split_heads(k_cm[krows, :], is_a)
                vb = jnp.concatenate([_split_heads(v_cm[krows, :], is_a), ones_ext], axis=1)
                sc = _dot(q_cm[qrows, :], kb, NT) + bias[first]
                m_a = jnp.max(sc[:, :KEYS], axis=-1, keepdims=True)
                m_b = jnp.max(sc[:, KEYS:], axis=-1, keepdims=True)
                p = jnp.exp(sc - _pair(m_a, m_b)).astype(BF16)
                ol = _dot(p, vb, NN)
                o_out[qrows, :] = ol[:, :LANE]
                l_out[qrows, :] = ol[:, LANE:]
                m_out[qrows, :] = jnp.where(is_a, m_a, m_b)
                return c

            lax.fori_loop(0, nblk, step, 0, unroll=ATT_UNROLL)

            if dil > 1:
                for r in range(dil):
                    def merge(j, c, dil=dil, r=r, ln=ln):
                        j0 = pl.multiple_of(j * PRO_ROWS, PRO_ROWS)
                        nat = _rows(r + dil * j0, PRO_ROWS, dil)
                        cm = pl.ds(r * ln + j0, PRO_ROWS)
                        m_old, m_new = m_acc[nat, :], m_d[cm, :]
                        m = jnp.maximum(m_old, m_new)
                        a_old, a_new = jnp.exp(m_old - m), jnp.exp(m_new - m)
                        o_ref[nat, :] = a_old * o_ref[nat, :] + a_new * o_d[cm, :]
                        l_acc[nat, :] = a_old * l_acc[nat, :] + a_new * l_d[cm, :]
                        m_acc[nat, :] = m
                        return c

                    lax.fori_loop(0, ln // PRO_ROWS, merge, 0)

        def epi(j, c):
            rows = pl.ds(pl.multiple_of(j * PRO_ROWS, PRO_ROWS), PRO_ROWS)
            l = l_acc[rows, :]
            o_ref[rows, :] = o_ref[rows, :] / l
            lse_ref[rows, :] = m_acc[rows, :] + jnp.log(l)
            return c

        lax.fori_loop(0, s // PRO_ROWS, epi, 0)

    f = jax.ShapeDtypeStruct((s, ATT_D), F32)
    scr = pltpu.VMEM((s, LANE), F32)
    scb = pltpu.VMEM((s, LANE), BF16)
    return _pcall(
        body, (proj, proj, proj, qw, kw), name="att_fwd", grid=(ATT_D // LANE,),
        in_specs=[blk(OFF_Q), blk(OFF_K), blk(OFF_V), wspec, wspec], out_specs=[oblk, oblk], out_shape=[f, f],
        scratch_shapes=[scr, scr, scb, scb, scb, scr, scr, scr, scr, scr, pltpu.VMEM((2, ATT_BLK, 2 * KEYS), F32)],
        sem=("parallel",), comm=comm)


def _att_bwd(proj, do, stats, qw, kw, comm=None):
    s = proj.shape[0]
    nblk = s // ATT_BLK
    blk = lambda off: pl.BlockSpec((s, LANE), lambda i: (0, off // LANE + i))
    wspec = pl.BlockSpec((1, LANE), lambda i: (0, i))
    oblk = pl.BlockSpec((s, LANE), lambda i: (0, i))

    def body(q_ref, k_ref, v_ref, do_ref, st_ref, qw_ref, kw_ref, dq_ref, dk_ref, dv_ref, dqw_ref, dkw_ref,
             qn, kn, q_cm, do_cm, k_cm, v_cm, st_cm, dq_acc, dk_acc, dv_acc, dq_d, dk_d, dv_d, bias):
        ones_bd = _head_mean_matrix().astype(BF16)
        is_a = lax.broadcasted_iota(jnp.int32, (1, LANE), 1) < HEAD_DIM
        _fill_bias(bias)
        zero = jnp.zeros((PRO_ROWS, LANE), F32)

        def pro(j, c):
            rows = pl.ds(pl.multiple_of(j * PRO_ROWS, PRO_ROWS), PRO_ROWS)
            qn[rows, :] = _head_norm(q_ref[rows, :], qw_ref[...], HEAD_DIM ** -0.5, ones_bd)
            kn[rows, :] = _head_norm(k_ref[rows, :], kw_ref[...], 1.0, ones_bd)
            dk_acc[rows, :] = zero
            dv_acc[rows, :] = zero
            return c

        lax.fori_loop(0, s // PRO_ROWS, pro, 0)

        for dil in DILATIONS:
            ln = s // dil
            nb = ln // ATT_BLK
            dq_o, dk_o, dv_o = (dq_acc, dk_acc, dv_acc) if dil == 1 else (dq_d, dk_d, dv_d)
            for r in range(dil):
                def relayout(j, c, dil=dil, r=r, ln=ln):
                    j0 = pl.multiple_of(j * PRO_ROWS, PRO_ROWS)
                    src = _rows(r + dil * j0, PRO_ROWS, dil)
                    dst = pl.ds(r * ln + j0, PRO_ROWS)
                    q_cm[dst, :] = qn[src, :].astype(BF16)
                    k_cm[dst, :] = kn[src, :].astype(BF16)
                    v_cm[dst, :] = v_ref[src, :].astype(BF16)
                    do_cm[dst, :] = do_ref[src, :].astype(BF16)
                    st_cm[dst, :] = st_ref[src, :]
                    if dil > 1:
                        dk_d[dst, :] = zero
                        dv_d[dst, :] = zero
                    return c

                lax.fori_loop(0, ln // PRO_ROWS, relayout, 0)

            def step(b, c, nb=nb, dq_o=dq_o, dk_o=dk_o, dv_o=dv_o):
                qrows, krows, first = _block_ids(b, nb)
                qb = q_cm[qrows, :]
                dob = do_cm[qrows, :]
                kb = _split_heads(k_cm[krows, :], is_a)
                vb = _split_heads(v_cm[krows, :], is_a)
                st = st_cm[qrows, :]
                sc = _dot(qb, kb, NT) + bias[first]
                p = jnp.exp(sc - _pair(st[:, 0:1], st[:, HEAD_DIM:HEAD_DIM + 1]))
                dp = _dot(dob, vb, NT)
                ds = (p * (dp - _pair(st[:, HALF:HALF + 1], st[:, HEAD_DIM + HALF:HEAD_DIM + HALF + 1]))).astype(BF16)
                dq_o[qrows, :] = _dot(ds, kb, NN)
                dkf = _dot(ds, qb, TN)
                dvf = _dot(p.astype(BF16), dob, TN)
                dk_o[krows, :] += jnp.where(is_a, dkf[:KEYS], dkf[KEYS:])
                dv_o[krows, :] += jnp.where(is_a, dvf[:KEYS], dvf[KEYS:])
                return c

            lax.fori_loop(0, nblk, step, 0, unroll=ATT_UNROLL)

            if dil > 1:
                for r in range(dil):
                    def merge(j, c, dil=dil, r=r, ln=ln):
                        j0 = pl.multiple_of(j * PRO_ROWS, PRO_ROWS)
                        nat = _rows(r + dil * j0, PRO_ROWS, dil)
                        cm = pl.ds(r * ln + j0, PRO_ROWS)
                        dq_acc[nat, :] += dq_d[cm, :]
                        dk_acc[nat, :] += dk_d[cm, :]
                        dv_acc[nat, :] += dv_d[cm, :]
                        return c

                    lax.fori_loop(0, ln // PRO_ROWS, merge, 0)

        def back(dn_out, x, w, scale):
            r = lax.rsqrt(_head_sum2(x * x, ones_bd) * (1.0 / HEAD_DIM) + EPS)
            nrm = x * r
            dw = jnp.sum(dn_out * nrm, axis=0, keepdims=True) * scale
            dn = dn_out * (w * scale)
            return r * (dn - nrm * (_head_sum2(dn * nrm, ones_bd) * (1.0 / HEAD_DIM))), dw

        def epi(j, c):
            rows = pl.ds(pl.multiple_of(j * PRO_ROWS, PRO_ROWS), PRO_ROWS)
            dq, dqw = back(dq_acc[rows, :], q_ref[rows, :], qw_ref[...], HEAD_DIM ** -0.5)
            dk, dkw = back(dk_acc[rows, :], k_ref[rows, :], kw_ref[...], 1.0)
            dq_ref[rows, :] = dq.astype(BF16)
            dk_ref[rows, :] = dk.astype(BF16)
            dv_ref[rows, :] = dv_acc[rows, :].astype(BF16)
            return (c[0] + dqw, c[1] + dkw)

        zrow = jnp.zeros((1, LANE), F32)
        dqw, dkw = lax.fori_loop(0, s // PRO_ROWS, epi, (zrow, zrow))
        dqw_ref[...] = dqw
        dkw_ref[...] = dkw

    o = jax.ShapeDtypeStruct((s, ATT_D), BF16)
    ov = jax.ShapeDtypeStruct((1, ATT_D), F32)
    scr = pltpu.VMEM((s, LANE), F32)
    scb = pltpu.VMEM((s, LANE), BF16)
    return _pcall(
        body, (proj, proj, proj, do, stats, qw, kw), name="att_bwd", grid=(ATT_D // LANE,),
        in_specs=[blk(OFF_Q), blk(OFF_K), blk(OFF_V), oblk, oblk, wspec, wspec],
        out_specs=[oblk, oblk, oblk, wspec, wspec], out_shape=[o, o, o, ov, ov],
        scratch_shapes=[scr, scr, scb, scb, scb, scb, scr, scr, scr, scr, scr, scr, scr, pltpu.VMEM((2, ATT_BLK, 2 * KEYS), F32)],
        sem=("parallel",), comm=comm)


def _att_norm_fwd(o, nw):
    s = o.shape[0]
    row = pl.BlockSpec((ROW_TILE, ATT_D), lambda i: (i, 0))
    vec = pl.BlockSpec((1, ATT_D), lambda i: (0, 0))

    def body(o_ref, nw_ref, y_ref):
        o = o_ref[...]
        r = lax.rsqrt(jnp.mean(o * o, axis=-1, keepdims=True) + EPS)
        y_ref[...] = (o * r * nw_ref[...]).astype(BF16)

    return pl.pallas_call(body, name="att_norm_fwd", grid=(s // ROW_TILE,), in_specs=[row, vec], out_specs=row,
                          out_shape=jax.ShapeDtypeStruct((s, ATT_D), BF16), compiler_params=_cparams(("parallel",)))(o, nw)


def _att_norm_bwd(dycat, o, lse, nw):
    s = o.shape[0]
    row = pl.BlockSpec((ROW_TILE, ATT_D), lambda i: (i, 0))
    vec = pl.BlockSpec((1, ATT_D), lambda i: (0, 0))

    def body(dy_ref, o_ref, lse_ref, nw_ref, do_ref, st_ref, dnw_ref):
        @pl.when(pl.program_id(0) == 0)
        def _():
            dnw_ref[...] = jnp.zeros_like(dnw_ref)

        o = o_ref[...]
        dy = dy_ref[...]
        r = lax.rsqrt(jnp.mean(o * o, axis=-1, keepdims=True) + EPS)
        nrm = o * r
        dnw_ref[...] += jnp.sum(dy * nrm, axis=0, keepdims=True)
        dn = dy * nw_ref[...]
        do = r * (dn - nrm * jnp.mean(dn * nrm, axis=-1, keepdims=True))
        do_ref[...] = do
        ones_bd = _head_mean_matrix().astype(BF16)
        prod = do * o
        delta = jnp.concatenate([_head_sum2(prod[:, j * LANE:(j + 1) * LANE], ones_bd) for j in range(ATT_D // LANE)], axis=1)
        lane = lax.broadcasted_iota(jnp.int32, (1, ATT_D), 1)
        st_ref[...] = jnp.where((lane & (HEAD_DIM - 1)) < HALF, lse_ref[...], delta)

    f = jax.ShapeDtypeStruct((s, ATT_D), F32)
    return pl.pallas_call(
        body, name="att_norm_bwd", grid=(s // ROW_TILE,),
        in_specs=[pl.BlockSpec((ROW_TILE, ATT_D), lambda i: (i, 1)), row, row, vec], out_specs=[row, row, vec],
        out_shape=[f, f, jax.ShapeDtypeStruct((1, ATT_D), F32)],
        compiler_params=_cparams(("arbitrary",)))(dycat, o, lse, nw)


def _ada_fwd(c_all, w_ada):
    def body(c_ref, w_ref, o_ref):
        cv = c_ref[...]
        o_ref[...] = _dot((cv * _sigmoid(cv)).astype(BF16), w_ref[...].astype(BF16), NN)

    return pl.pallas_call(body, name="ada_fwd", out_shape=jax.ShapeDtypeStruct((c_all.shape[0], w_ada.shape[1]), F32),
                          compiler_params=_cparams())(c_all, w_ada)


def _adamw_math(g, w, m, v):
    m_new = ADAM_B1 * m + (1.0 - ADAM_B1) * g
    v_new = ADAM_B2 * v + (1.0 - ADAM_B2) * (g * g)
    m_hat = m_new / (1.0 - ADAM_B1 ** ADAM_STEP)
    v_hat = v_new / (1.0 - ADAM_B2 ** ADAM_STEP)
    delta = -ADAM_LR * (m_hat / (jnp.sqrt(v_hat) + ADAM_EPS) + ADAM_WD * w)
    return delta, m_new, v_new


def _ada_bwd_adamw(c_all, dmod_cols, w, m, v):
    rows, cols = w.shape
    tr = 256
    blk = pl.BlockSpec((tr, cols), lambda i: (i, 0))

    def body(c_ref, d_ref, w_ref, m_ref, v_ref, g_ref, dl_ref, mo_ref, vo_ref):
        cv = c_ref[...]
        ca = cv * _sigmoid(cv)
        g = ca[:, 0:1] * d_ref[0:1, :]
        for b in range(1, N_DEV):
            g = g + ca[:, b:b + 1] * d_ref[b:b + 1, :]
        g_ref[...] = g
        dl_ref[...], mo_ref[...], vo_ref[...] = _adamw_math(g, w_ref[...], m_ref[...], v_ref[...])

    o = jax.ShapeDtypeStruct((rows, cols), F32)
    return pl.pallas_call(
        body, name="ada_bwd_adamw", grid=(rows // tr,),
        in_specs=[pl.BlockSpec((tr, N_DEV), lambda i: (i, 0)), pl.BlockSpec((N_DEV, cols), lambda i: (0, 0)), blk, blk, blk],
        out_specs=[blk] * 4, out_shape=[o, o, o, o], compiler_params=_cparams(("parallel",)))(c_all.T, dmod_cols, w, m, v)


def _reduce_adamw(slabs, w, m, v, name):
    rows, cols = w.shape
    tr = 128
    blk = pl.BlockSpec((tr, cols), lambda i: (i, 0))

    def body(s_ref, w_ref, m_ref, v_ref, g_ref, dl_ref, mo_ref, vo_ref):
        g = s_ref[0].astype(F32)
        for dev in range(1, N_DEV):
            g = g + s_ref[dev].astype(F32)
        g_ref[...] = g
        dl_ref[...], mo_ref[...], vo_ref[...] = _adamw_math(g, w_ref[...], m_ref[...], v_ref[...])

    o = jax.ShapeDtypeStruct((rows, cols), F32)
    return pl.pallas_call(
        body, name=name, grid=(rows // tr,),
        in_specs=[pl.BlockSpec((N_DEV, tr, cols), lambda i: (0, i, 0)), blk, blk, blk],
        out_specs=[blk] * 4, out_shape=[o, o, o, o], compiler_params=_cparams(("parallel",)))(slabs, w, m, v)


def _small_reduce_adamw(gathered, w, m, v):
    def body(s_ref, w_ref, m_ref, v_ref, g_ref, dl_ref, mo_ref, vo_ref):
        g = s_ref[0]
        for dev in range(1, N_DEV):
            g = g + s_ref[dev]
        g_ref[...] = g
        dl_ref[...], mo_ref[...], vo_ref[...] = _adamw_math(g, w_ref[...], m_ref[...], v_ref[...])

    o = jax.ShapeDtypeStruct(w.shape, F32)
    return pl.pallas_call(body, name="small_reduce_adamw", out_shape=[o, o, o, o], compiler_params=_cparams())(gathered, w, m, v)


def _adamw_small(g, w, m, v, name):
    def body(g_ref, w_ref, m_ref, v_ref, dl_ref, mo_ref, vo_ref):
        dl_ref[...], mo_ref[...], vo_ref[...] = _adamw_math(g_ref[...], w_ref[...], m_ref[...], v_ref[...])

    o = jax.ShapeDtypeStruct(w.shape, F32)
    return pl.pallas_call(body, name=name, out_shape=[o, o, o], compiler_params=_cparams())(g, w, m, v)


class _Exchange:
    def __init__(self, arrs, scatter):
        self.arrs, self.scatter, self.n = list(arrs), scatter, len(arrs)
        hbm = pl.BlockSpec(memory_space=pltpu.HBM)
        self.in_specs = [hbm] * self.n
        self.out_specs = [hbm] * self.n
        self.out_shape = [jax.ShapeDtypeStruct(a.shape if scatter else (N_DEV,) + a.shape, a.dtype) for a in self.arrs]
        self.scratch = [pltpu.SemaphoreType.DMA((self.n * (N_DEV - 1),)), pltpu.SemaphoreType.DMA((self.n * (N_DEV - 1),)),
                        pltpu.SemaphoreType.DMA((self.n,))]

    def _copies(self, ins, outs, sems):
        send_sems, recv_sems, local_sems = sems
        x, y, c = lax.axis_index("x"), lax.axis_index("y"), lax.axis_index("c")
        me = 4 * x + 2 * y + c
        local, remote = [], []
        for a in range(self.n):
            src_mine = ins[a].at[me] if self.scatter else ins[a]
            local.append(pltpu.make_async_copy(src_mine, outs[a].at[me], local_sems.at[a]))
            for k in range(1, N_DEV):
                px = 1 - x if k & 4 else x
                py = 1 - y if k & 2 else y
                pc = 1 - c if k & 1 else c
                peer = 4 * px + 2 * py + pc
                sem = a * (N_DEV - 1) + k - 1
                src = ins[a].at[peer] if self.scatter else ins[a]
                send = pltpu.make_async_remote_copy(
                    src_ref=src, dst_ref=outs[a].at[me], send_sem=send_sems.at[sem], recv_sem=recv_sems.at[sem],
                    device_id=(px, py, pc), device_id_type=MESH_IDS)
                recv = pltpu.make_async_remote_copy(
                    src_ref=src, dst_ref=outs[a].at[peer], send_sem=send_sems.at[sem], recv_sem=recv_sems.at[sem],
                    device_id=(px, py, pc), device_id_type=MESH_IDS)
                remote.append((send, recv))
        return local, remote

    def start(self, ins, outs, sems):
        local, remote = self._copies(ins, outs, sems)
        for cp in local:
            cp.start()
        for send, _ in remote:
            send.start()

    def wait(self, ins, outs, sems):
        local, remote = self._copies(ins, outs, sems)
        for send, recv in remote:
            send.wait_send()
            recv.wait_recv()
        for cp in local:
            cp.wait()


def _split_comm_refs(refs, n_in, n_out, n_scr, comm):
    nc = comm.n if comm is not None else 0
    ns = 3 if comm is not None else 0
    pos, groups = 0, []
    for cnt in (n_in, nc, n_out, nc, n_scr, ns):
        groups.append(refs[pos:pos + cnt])
        pos += cnt
    assert pos == len(refs), (pos, len(refs))
    return groups


def _pcall(body, args, *, name, grid, in_specs, out_specs, out_shape, scratch_shapes=(), sem=None, comm=None):
    in_specs, out_specs, out_shape, scratch_shapes = list(in_specs), list(out_specs), list(out_shape), list(scratch_shapes)
    n_in, n_out, n_scr = len(in_specs), len(out_specs), len(scratch_shapes)
    if comm is None:
        kernel_body = body
    else:
        def kernel_body(*refs):
            ins, cins, outs, couts, scr, sems = _split_comm_refs(refs, n_in, n_out, n_scr, comm)
            ids = [pl.program_id(a) for a in range(len(grid))]
            first, last = ids[0] == 0, ids[0] == grid[0] - 1
            for a in range(1, len(grid)):
                first, last = first & (ids[a] == 0), last & (ids[a] == grid[a] - 1)

            @pl.when(first)
            def _():
                comm.start(cins, couts, sems)

            body(*ins, *outs, *scr)

            @pl.when(last)
            def _():
                comm.wait(cins, couts, sems)

        in_specs, out_specs, out_shape = in_specs + comm.in_specs, out_specs + comm.out_specs, out_shape + comm.out_shape
        scratch_shapes, args = scratch_shapes + comm.scratch, list(args) + comm.arrs
        sem = ("arbitrary",) * len(grid)
    res = pl.pallas_call(kernel_body, name=name, grid=grid, in_specs=in_specs, out_specs=out_specs, out_shape=out_shape,
                         scratch_shapes=scratch_shapes, compiler_params=_cparams(sem))(*args)
    return res[:n_out], res[n_out:]


def _exchange(arrs, name, scatter):
    ex = _Exchange(arrs, scatter)

    def body(*refs):
        _, ins, _, outs, _, sems = _split_comm_refs(refs, 0, 0, 0, ex)
        ex.start(ins, outs, sems)
        ex.wait(ins, outs, sems)

    return pl.pallas_call(body, name=name, in_specs=ex.in_specs, out_specs=ex.out_specs, out_shape=ex.out_shape,
                          scratch_shapes=ex.scratch)(*arrs)


def _pad_lanes(v, width=LANE):
    return jnp.pad(v, ((0, 0), (0, width - v.shape[1])))


def _shards_to_cols(g):
    return jnp.transpose(g, (1, 0, 2)).reshape(g.shape[1], N_DEV * g.shape[2])


def _cols_to_shards(w):
    return w.astype(BF16).reshape(w.shape[0], N_DEV, w.shape[1] // N_DEV).transpose(1, 0, 2)


def _local_step(x, tgt, mod, w_in_p, conv_w, conv_b, dt_bias, a_log, d_skip, ssd_norm_w, q_norm_w, k_norm_w,
                attn_norm_w, w_out_sh, w_ff1_sh, w_ff2_sh, norm1_w, norm2_w):
    shift1, scale1, gate1, shift2, scale2, gate2 = [mod[i:i + 1] for i in range(N_MOD)]
    dtb, alog, dsk = _pad_lanes(dt_bias), _pad_lanes(a_log), _pad_lanes(d_skip)
    qw, kw = jnp.tile(q_norm_w, (1, ATT_HEADS)), jnp.tile(k_norm_w, (1, ATT_HEADS))

    h1 = _norm_mod_fwd(x, norm1_w, scale1, shift1, "norm1_fwd")
    proj = _matmul(h1, w_in_p, tm=1024, tn=896, tk=1024, name="in_proj")
    pre, act = _conv_fwd(proj, conv_w, conv_b)
    ypre, y_ssd, hall = _ssd_fwd(proj, act, dtb, alog, dsk, ssd_norm_w)
    (o_att, lse), (w_out_g, w_ff1_g, w_ff2_g) = _att_fwd(proj, qw, kw, comm=_Exchange([w_out_sh, w_ff1_sh, w_ff2_sh], scatter=False))
    w_out = w_out_g.reshape(2 * D_MODEL, D_MODEL)
    w_ff1 = _shards_to_cols(w_ff1_g)
    w_ff2 = w_ff2_g.reshape(D_FF, D_MODEL)
    y_att = _att_norm_fwd(o_att, attn_norm_w)
    ycat = jnp.concatenate([y_ssd, y_att], axis=1)
    mix = _matmul(ycat, w_out, tm=1024, tn=1024, tk=2048, name="out_proj")
    x1, h2 = _norm_mod_fwd(x, norm2_w, scale2, shift2, "norm2_fwd", res=mix, gate=gate1)
    u, act_ff = _matmul(h2, w_ff1, tm=1024, tn=1024, tk=1024, name="ff1", mode="relu2")
    ff = _matmul(act_ff, w_ff2, tm=1024, tn=1024, tk=2048, name="ff2")
    loss, dout, dff, dgate2 = _loss_head(x1, ff, gate2, tgt)

    du = _matmul(dff, w_ff2, tb=True, tm=1024, tn=1024, tk=1024, out_dtype=BF16, name="ff2_dx", mode="drelu2", u=u)
    g_ff2 = _matmul(act_ff, dff, ta=True, tm=1024, tn=1024, tk=1024, name="ff2_dw")
    dh2 = _matmul(du, w_ff1, tb=True, tm=1024, tn=1024, tk=2048, name="ff1_dx")
    g_ff1 = _matmul(h2, du, ta=True, tm=1024, tn=1024, tk=1024, name="ff1_dw")
    dx1, dshift2, dscale2, g_norm2, dmix, dgate1 = _norm_mod_bwd(dh2, x1, dout, norm2_w, scale2, "norm2_bwd", gate=gate1, mix=mix)

    dycat = _matmul(dmix, w_out, tb=True, tm=1024, tn=1024, tk=1024, name="out_proj_dx")
    g_out = _matmul(ycat, dmix, ta=True, tm=1024, tn=1024, tk=1024, name="out_proj_dw")
    do, stats, g_attn_norm = _att_norm_bwd(dycat, o_att, lse, attn_norm_w)
    ff_slabs = [_cols_to_shards(g_ff1), g_ff2.astype(BF16).reshape(N_DEV, D_FF // N_DEV, D_MODEL)]
    (dq, dk, dv, dqw, dkw), (s_ff1, s_ff2) = _att_bwd(proj, do, stats, qw, kw, comm=_Exchange(ff_slabs, scatter=True))
    out_slabs = [g_out.astype(BF16).reshape(N_DEV, 2 * D_MODEL // N_DEV, D_MODEL)]
    (dz, dact, ddtr, da, g_dsk, g_dtb, g_ssd_norm), (s_out,) = _ssd_bwd(
        dycat, ypre, proj, act, hall, dtb, alog, dsk, ssd_norm_w, comm=_Exchange(out_slabs, scatter=True))
    dxbc, g_conv_w, g_conv_b = _conv_bwd(dact, pre, proj, conv_w)
    dproj = jnp.concatenate([dz, dxbc, dq, dk, dv, ddtr], axis=1)
    g_in_p = _matmul(h1, dproj, ta=True, tm=1024, tn=896, tk=1024, name="in_proj_dw")
    in_slabs = [_cols_to_shards(_unpack_w_in(g_in_p))]
    dh1, (s_in,) = _matmul(dproj, w_in_p, tb=True, tm=1024, tn=1024, tk=896, name="in_proj_dx",
                           comm=_Exchange(in_slabs, scatter=True))
    grad_x, dshift1, dscale1, g_norm1 = _norm_mod_bwd(dh1, x, dx1, norm1_w, scale1, "norm1_bwd")

    dmod = jnp.concatenate([dshift1, dscale1, dgate1, dshift2, dscale2, dgate2], axis=0)
    g_alog = da[:, :SSD_HEADS] * (-jnp.exp(a_log))
    g_qw = dqw.reshape(ATT_HEADS, HEAD_DIM).sum(axis=0, keepdims=True)
    g_kw = dkw.reshape(ATT_HEADS, HEAD_DIM).sum(axis=0, keepdims=True)
    return dict(loss=loss, grad_x=grad_x, dmod=dmod, norm1_w=g_norm1, norm2_w=g_norm2, w_in=s_in, conv_w=g_conv_w,
                conv_b=g_conv_b, dt_bias=g_dtb[:, :SSD_HEADS], a_log=g_alog, d_skip=g_dsk[:, :SSD_HEADS],
                ssd_norm_w=g_ssd_norm, q_norm_w=g_qw, k_norm_w=g_kw, attn_norm_w=g_attn_norm, w_out=s_out,
                w_ff1=s_ff1, w_ff2=s_ff2)


def _pack_w_in(w_full):
    o_dt = SSD_D_INNER + CONV_CH
    o_q = o_dt + SSD_HEADS
    pad = jnp.zeros((w_full.shape[0], LANE - SSD_HEADS), w_full.dtype)
    return jnp.concatenate([w_full[:, :o_dt], w_full[:, o_q:], w_full[:, o_dt:o_q], pad], axis=1)


def _unpack_w_in(g_p):
    return jnp.concatenate([g_p[:, :OFF_Q], g_p[:, OFF_DT:OFF_DT + SSD_HEADS], g_p[:, OFF_Q:OFF_DT]], axis=1)


MISC_FIELDS = (("dt_bias", SSD_HEADS), ("a_log", SSD_HEADS), ("d_skip", SSD_HEADS), ("q_norm_w", HEAD_DIM), ("k_norm_w", HEAD_DIM))
SMALL_LAYOUT = (("b_ada", 6), ("norm1_w", 1), ("norm2_w", 1), ("conv_w", 8), ("conv_b", 2), ("ssd_norm_w", 1),
                ("attn_norm_w", 1), ("misc", 1))


def _pack_small(vals):
    rows = []
    for name, nrow in SMALL_LAYOUT:
        if name == "misc":
            misc = jnp.concatenate([vals[f].reshape(1, n) for f, n in MISC_FIELDS], axis=1)
            rows.append(_pad_lanes(misc, D_MODEL))
        elif name in vals:
            rows.append(vals[name].reshape(nrow, D_MODEL))
        else:
            rows.append(jnp.zeros((nrow, D_MODEL), F32))
    used = sum(n for _, n in SMALL_LAYOUT)
    rows.append(jnp.zeros((SMALL_ROWS - used, D_MODEL), F32))
    return jnp.concatenate(rows, axis=0)


def _unpack_small(packed):
    out, r = {}, 0
    for name, nrow in SMALL_LAYOUT:
        blk = packed[r:r + nrow]
        r += nrow
        if name == "misc":
            c0 = 0
            for f, n in MISC_FIELDS:
                out[f] = blk[:, c0:c0 + n]
                c0 += n
        elif name == "b_ada":
            out[name] = blk.reshape(1, N_MOD * D_MODEL)
        elif name == "conv_w":
            out[name] = blk.reshape(CONV_K, CONV_CH)
        elif name == "conv_b":
            out[name] = blk.reshape(1, CONV_CH)
        else:
            out[name] = blk
    return out


WEIGHT_NAMES = ("norm1_w", "norm2_w", "w_ada", "b_ada", "w_in", "conv_w", "conv_b", "dt_bias", "a_log", "d_skip",
                "ssd_norm_w", "q_norm_w", "k_norm_w", "attn_norm_w", "w_out", "w_ff1", "w_ff2")
SMALL_NAMES = ("norm1_w", "norm2_w", "b_ada", "conv_b", "dt_bias", "a_log", "d_skip", "ssd_norm_w", "q_norm_w",
               "k_norm_w", "attn_norm_w")


def kernel(x, c, norm1_w, norm2_w, w_ada, b_ada, w_in, conv_w, conv_b, dt_bias, a_log, d_skip, ssd_norm_w, q_norm_w, k_norm_w, attn_norm_w, w_out, w_ff1, w_ff2, loss_target, m_norm1_w, m_norm2_w, m_w_ada, m_b_ada, m_w_in, m_conv_w, m_conv_b, m_dt_bias, m_a_log, m_d_skip, m_ssd_norm_w, m_q_norm_w, m_k_norm_w, m_attn_norm_w, m_w_out, m_w_ff1, m_w_ff2, v_norm1_w, v_norm2_w, v_w_ada, v_b_ada, v_w_in, v_conv_w, v_conv_b, v_dt_bias, v_a_log, v_d_skip, v_ssd_norm_w, v_q_norm_w, v_k_norm_w, v_attn_norm_w, v_w_out, v_w_ff1, v_w_ff2):
    args = dict(locals())
    w = {n: args[n] for n in WEIGHT_NAMES}
    m = {n: args["m_" + n] for n in WEIGHT_NAMES}
    v = {n: args["v_" + n] for n in WEIGHT_NAMES}
    me = 4 * lax.axis_index("x") + 2 * lax.axis_index("y") + lax.axis_index("c")

    c_rows = jnp.pad(c, ((0, 7), (0, 0)))
    c_g, conv_g, w_in_g = _exchange([c_rows, w["conv_w"][0], w["w_in"][0].astype(BF16)], "gather_w_in", scatter=False)
    c_all = c_g[:, 0, :]
    conv_full = _shards_to_cols(conv_g)
    w_in_p = _pack_w_in(_shards_to_cols(w_in_g))

    mod_part = _ada_fwd(c_all, w["w_ada"][0])
    (mod_g,) = _exchange([mod_part], "gather_mod", scatter=False)
    mod_mine = lax.dynamic_index_in_dim(mod_g, me, axis=1, keepdims=False).reshape(1, N_MOD * D_MODEL) + w["b_ada"]
    mod = mod_mine.reshape(N_MOD, D_MODEL)

    res = _local_step(x[0], loss_target[0], mod, w_in_p, conv_full, w["conv_b"], w["dt_bias"], w["a_log"], w["d_skip"],
                      w["ssd_norm_w"], w["q_norm_w"], w["k_norm_w"], w["attn_norm_w"], w["w_out"][0].astype(BF16),
                      w["w_ff1"][0].astype(BF16), w["w_ff2"][0].astype(BF16), w["norm1_w"], w["norm2_w"])

    small_vals = {n: res[n] for n in SMALL_NAMES if n != "b_ada"}
    small_vals["b_ada"] = res["dmod"]
    small_vals["conv_w"] = res["conv_w"]
    (small_g,) = _exchange([_pack_small(small_vals)], "gather_small", scatter=False)

    grads, delta, new_m, new_v = {}, {}, {}, {}
    for name in ("w_in", "w_out", "w_ff1", "w_ff2"):
        outs = _reduce_adamw(res[name], w[name][0], m[name][0], v[name][0], "adamw_" + name)
        grads[name], delta[name], new_m[name], new_v[name] = [o[None] for o in outs]

    sm = _small_reduce_adamw(small_g, _pack_small({n: w[n] for n in SMALL_NAMES}), _pack_small({n: m[n] for n in SMALL_NAMES}),
                             _pack_small({n: v[n] for n in SMALL_NAMES}))
    sm = [_unpack_small(p) for p in sm]
    for n in SMALL_NAMES:
        grads[n], delta[n], new_m[n], new_v[n] = [p[n] for p in sm]
    shard_w = CONV_CH // N_DEV
    g_conv = lax.dynamic_slice_in_dim(sm[0]["conv_w"], me * shard_w, shard_w, axis=1)
    cw = _adamw_small(g_conv, w["conv_w"][0], m["conv_w"][0], v["conv_w"][0], "adamw_conv_w")
    grads["conv_w"] = g_conv[None]
    delta["conv_w"], new_m["conv_w"], new_v["conv_w"] = [o[None] for o in cw]

    ada_w = w_ada.shape[2]
    dmod_all = small_g[:, :N_MOD, :].reshape(N_DEV, N_MOD * D_MODEL)
    dmod_cols = lax.dynamic_slice_in_dim(dmod_all, me * ada_w, ada_w, axis=1)
    outs = _ada_bwd_adamw(c_all, dmod_cols, w["w_ada"][0], m["w_ada"][0], v["w_ada"][0])
    grads["w_ada"], delta["w_ada"], new_m["w_ada"], new_v["w_ada"] = [o[None] for o in outs]

    loss = lax.psum(res["loss"][0, 0], ("x", "y", "c"))
    return (loss, res["grad_x"][None], *[grads[n] for n in WEIGHT_NAMES], *[delta[n] for n in WEIGHT_NAMES],
            *[new_m[n] for n in WEIGHT_NAMES], *[new_v[n] for n in WEIGHT_NAMES])
```

```python
import functools

import jax
import jax.numpy as jnp
from jax import lax
from jax.experimental import pallas as pl
from jax.experimental.pallas import tpu as pltpu

F32 = jnp.float32
BF16 = jnp.bfloat16
HIGHEST = lax.Precision.HIGHEST
MESH_IDS = pl.DeviceIdType.MESH

N_DEV = 8
D_MODEL = 1024
HEAD_DIM = 64
SSD_HEADS = 16
SSD_GROUPS = 4
HEADS_PER_GROUP = SSD_HEADS // SSD_GROUPS
SSD_STATE = 128
SSD_CHUNK = 128
SSD_D_INNER = SSD_HEADS * HEAD_DIM
GROUP_WIDTH = SSD_D_INNER // SSD_GROUPS
CONV_K = 4
CONV_CH = SSD_D_INNER + 2 * SSD_GROUPS * SSD_STATE
ATT_HEADS = 16
ATT_D = ATT_HEADS * HEAD_DIM
ATT_BLK = 128
DILATIONS = (1, 4, 16)
D_FF = 4 * D_MODEL
N_MOD = 6
EPS = 1e-6
IN_W = SSD_D_INNER + CONV_CH + SSD_HEADS + 3 * ATT_D
LANE = 128
OFF_Z, OFF_XBC, OFF_Q, OFF_K, OFF_V, OFF_DT = 0, 1024, 3072, 4096, 5120, 6144
IN_WP = OFF_DT + LANE

ADAM_LR, ADAM_B1, ADAM_B2, ADAM_EPS, ADAM_WD, ADAM_STEP = 0.001, 0.9, 0.999, 1e-08, 0.01, 10
VMEM_LIMIT = 56 * 1024 * 1024
ROW_TILE = 512
SMALL_ROWS = 24


def _cparams(sem=None):
    return pltpu.CompilerParams(dimension_semantics=sem, vmem_limit_bytes=VMEM_LIMIT)


def _sigmoid(v):
    return 1.0 / (1.0 + jnp.exp(-v))


def _softplus(v):
    y = jnp.exp(-jnp.abs(v))
    small = y * (1.0 - y * (0.5 - y * (1.0 / 3.0)))
    return jnp.maximum(v, 0.0) + jnp.where(y < 0.01, small, jnp.log(1.0 + y))


def _dot(a, b, dims, precision=None):
    return lax.dot_general(a, b, (dims, ((), ())), preferred_element_type=F32, precision=precision)


NN = ((1,), (0,))
NT = ((1,), (1,))
TN = ((0,), (0,))


def _matmul(a, b, *, ta=False, tb=False, tm, tn, tk, out_dtype=F32, name, mode=None, u=None, comm=None):
    m, k = (a.shape[1], a.shape[0]) if ta else a.shape
    n = b.shape[0] if tb else b.shape[1]
    assert m % tm == 0 and n % tn == 0 and k % tk == 0, (name, m, n, k)
    nk = k // tk
    a_spec = pl.BlockSpec((tk, tm), lambda i, j, kk: (kk, i)) if ta else pl.BlockSpec((tm, tk), lambda i, j, kk: (i, kk))
    b_spec = pl.BlockSpec((tn, tk), lambda i, j, kk: (j, kk)) if tb else pl.BlockSpec((tk, tn), lambda i, j, kk: (kk, j))
    o_spec = pl.BlockSpec((tm, tn), lambda i, j, kk: (i, j))
    dims = ((0,) if ta else (1,), (1,) if tb else (0,))
    n_out = 2 if mode == "relu2" else 1

    def body(*refs):
        if mode == "drelu2":
            a_ref, b_ref, u_ref = refs[:3]
            rest = refs[3:]
        else:
            a_ref, b_ref = refs[:2]
            u_ref = None
            rest = refs[2:]
        outs = rest[:n_out]
        part = _dot(a_ref[...], b_ref[...], dims)

        def finish(r):
            if mode == "relu2":
                outs[0][...] = r.astype(BF16)
                rr = jnp.maximum(r, 0.0)
                outs[1][...] = (rr * rr).astype(BF16)
            elif mode == "drelu2":
                outs[0][...] = (r * (2.0 * jnp.maximum(u_ref[...].astype(F32), 0.0))).astype(out_dtype)
            else:
                outs[0][...] = r.astype(out_dtype)

        if nk == 1:
            finish(part)
        else:
            acc = rest[n_out]
            kk = pl.program_id(2)

            @pl.when(kk == 0)
            def _():
                acc[...] = part

            @pl.when(kk > 0)
            def _():
                acc[...] += part

            @pl.when(kk == nk - 1)
            def _():
                finish(acc[...])

    in_specs = [a_spec, b_spec]
    args = [a, b]
    if mode == "drelu2":
        in_specs.append(o_spec)
        args.append(u)
    if mode == "relu2":
        out_shape = [jax.ShapeDtypeStruct((m, n), BF16), jax.ShapeDtypeStruct((m, n), BF16)]
    else:
        out_shape = [jax.ShapeDtypeStruct((m, n), out_dtype)]
    outs, comm_outs = _pcall(
        body, args, name=name, grid=(m // tm, n // tn, nk), in_specs=in_specs, out_specs=[o_spec] * n_out,
        out_shape=out_shape, scratch_shapes=[pltpu.VMEM((tm, tn), F32)] if nk > 1 else [],
        sem=("parallel", "parallel", "arbitrary"), comm=comm)
    res = tuple(outs) if mode == "relu2" else outs[0]
    return res if comm is None else (res, comm_outs)


def _norm_mod_fwd(x, nw, scale, shift, name, res=None, gate=None):
    s, d = x.shape
    row = pl.BlockSpec((ROW_TILE, d), lambda i: (i, 0))
    vec = pl.BlockSpec((1, d), lambda i: (0, 0))
    with_res = res is not None

    def body(*refs):
        if with_res:
            x_ref, res_ref, gate_ref, nw_ref, sc_ref, sh_ref, x1_ref, h_ref = refs
            xv = x_ref[...] + gate_ref[...] * res_ref[...]
            x1_ref[...] = xv
        else:
            x_ref, nw_ref, sc_ref, sh_ref, h_ref = refs
            xv = x_ref[...]
        r = lax.rsqrt(jnp.mean(xv * xv, axis=-1, keepdims=True) + EPS)
        h_ref[...] = ((xv * r) * nw_ref[...] * (1.0 + sc_ref[...]) + sh_ref[...]).astype(BF16)

    if with_res:
        in_specs = [row, row, vec, vec, vec, vec]
        args = (x, res, gate, nw, scale, shift)
        out_shape = (jax.ShapeDtypeStruct((s, d), F32), jax.ShapeDtypeStruct((s, d), BF16))
        out_specs = (row, row)
    else:
        in_specs = [row, vec, vec, vec]
        args = (x, nw, scale, shift)
        out_shape = jax.ShapeDtypeStruct((s, d), BF16)
        out_specs = row
    return pl.pallas_call(body, name=name, grid=(s // ROW_TILE,), in_specs=in_specs, out_specs=out_specs,
                          out_shape=out_shape, compiler_params=_cparams(("parallel",)))(*args)


def _norm_mod_bwd(dh, xin, dres, nw, scale, name, gate=None, mix=None):
    s, d = xin.shape
    row = pl.BlockSpec((ROW_TILE, d), lambda i: (i, 0))
    vec = pl.BlockSpec((1, d), lambda i: (0, 0))
    with_gate = gate is not None

    def body(*refs):
        if with_gate:
            dh_ref, x_ref, dres_ref, nw_ref, sc_ref, gate_ref, mix_ref, dx_ref, dsh_ref, dsc_ref, dnw_ref, dmix_ref, dg_ref = refs
        else:
            dh_ref, x_ref, dres_ref, nw_ref, sc_ref, dx_ref, dsh_ref, dsc_ref, dnw_ref = refs
        i = pl.program_id(0)

        @pl.when(i == 0)
        def _():
            dsh_ref[...] = jnp.zeros_like(dsh_ref)
            dsc_ref[...] = jnp.zeros_like(dsc_ref)
            dnw_ref[...] = jnp.zeros_like(dnw_ref)
            if with_gate:
                dg_ref[...] = jnp.zeros_like(dg_ref)

        xv = x_ref[...]
        dhv = dh_ref[...]
        r = lax.rsqrt(jnp.mean(xv * xv, axis=-1, keepdims=True) + EPS)
        nrm = xv * r
        one_sc = 1.0 + sc_ref[...]
        dhn = dhv * nrm
        dsh_ref[...] += jnp.sum(dhv, axis=0, keepdims=True)
        dsc_ref[...] += jnp.sum(dhn, axis=0, keepdims=True) * nw_ref[...]
        dnw_ref[...] += jnp.sum(dhn, axis=0, keepdims=True) * one_sc
        dn = dhv * (nw_ref[...] * one_sc)
        dx = dres_ref[...] + r * (dn - nrm * jnp.mean(dn * nrm, axis=-1, keepdims=True))
        dx_ref[...] = dx
        if with_gate:
            dmix_ref[...] = (gate_ref[...] * dx).astype(BF16)
            dg_ref[...] += jnp.sum(dx * mix_ref[...], axis=0, keepdims=True)

    vshape = jax.ShapeDtypeStruct((1, d), F32)
    in_specs = [row, row, row, vec, vec]
    args = [dh, xin, dres, nw, scale]
    out_shape = [jax.ShapeDtypeStruct((s, d), F32), vshape, vshape, vshape]
    out_specs = [row, vec, vec, vec]
    if with_gate:
        in_specs += [vec, row]
        args += [gate, mix]
        out_shape += [jax.ShapeDtypeStruct((s, d), BF16), vshape]
        out_specs += [row, vec]
    return pl.pallas_call(body, name=name, grid=(s // ROW_TILE,), in_specs=in_specs, out_specs=out_specs,
                          out_shape=out_shape, compiler_params=_cparams(("arbitrary",)))(*args)


def _loss_head(x1, ff, gate2, tgt):
    s, d = x1.shape
    row = pl.BlockSpec((ROW_TILE, d), lambda i: (i, 0))
    vec = pl.BlockSpec((1, d), lambda i: (0, 0))
    one = pl.BlockSpec((1, 1), lambda i: (0, 0))

    def body(x1_ref, ff_ref, g_ref, t_ref, loss_ref, dout_ref, dff_ref, dg_ref):
        i = pl.program_id(0)

        @pl.when(i == 0)
        def _():
            loss_ref[...] = jnp.zeros_like(loss_ref)
            dg_ref[...] = jnp.zeros_like(dg_ref)

        ffv = ff_ref[...]
        err = x1_ref[...] + g_ref[...] * ffv - t_ref[...]
        loss_ref[...] += (0.5 / d) * jnp.sum(err * err).reshape(1, 1)
        dout = err * (1.0 / d)
        dout_ref[...] = dout
        dff_ref[...] = (g_ref[...] * dout).astype(BF16)
        dg_ref[...] += jnp.sum(dout * ffv, axis=0, keepdims=True)

    return pl.pallas_call(
        body, name="loss_head", grid=(s // ROW_TILE,), in_specs=[row, row, vec, row],
        out_specs=[one, row, row, vec],
        out_shape=[jax.ShapeDtypeStruct((1, 1), F32), jax.ShapeDtypeStruct((s, d), F32),
                   jax.ShapeDtypeStruct((s, d), BF16), jax.ShapeDtypeStruct((1, d), F32)],
        compiler_params=_cparams(("arbitrary",)))(x1, ff, gate2, tgt)


CONV_COLS = 256
HALO = 8


def _shift_down(cur, halo, k):
    if k == 0:
        return cur
    rolled = pltpu.roll(cur, k, axis=0)
    top = jnp.where(lax.broadcasted_iota(jnp.int32, halo.shape, 0) < k, pltpu.roll(halo, k, axis=0), rolled[:HALO])
    return jnp.concatenate([top, rolled[HALO:]], axis=0)


def _shift_up(cur, halo, k):
    if k == 0:
        return cur
    t = cur.shape[0]
    rolled = pltpu.roll(cur, t - k, axis=0)
    bot = jnp.where(lax.broadcasted_iota(jnp.int32, halo.shape, 0) >= HALO - k, pltpu.roll(halo, HALO - k, axis=0),
                    rolled[t - HALO:])
    return jnp.concatenate([rolled[:t - HALO], bot], axis=0)


def _conv_fwd(proj, conv_w, conv_b):
    s = proj.shape[0]
    nr = s // ROW_TILE
    cb0 = OFF_XBC // CONV_COLS
    hb = ROW_TILE // HALO
    cur = pl.BlockSpec((ROW_TILE, CONV_COLS), lambda j, r: (r, cb0 + j))
    prev = pl.BlockSpec((HALO, CONV_COLS), lambda j, r: (jnp.maximum(r * hb - 1, 0), cb0 + j))
    out = pl.BlockSpec((ROW_TILE, CONV_COLS), lambda j, r: (r, j))

    def body(u_ref, up_ref, w_ref, b_ref, pre_ref, act_ref):
        r = pl.program_id(1)
        u = u_ref[...]
        halo = jnp.where(r > 0, up_ref[...], 0.0)
        acc = b_ref[...] + w_ref[CONV_K - 1:CONV_K, :] * u
        for k in range(1, CONV_K):
            acc = acc + w_ref[CONV_K - 1 - k:CONV_K - k, :] * _shift_down(u, halo, k)
        pre_ref[...] = acc
        act_ref[...] = acc * _sigmoid(acc)

    return pl.pallas_call(
        body, name="conv_fwd", grid=(CONV_CH // CONV_COLS, nr),
        in_specs=[cur, prev, pl.BlockSpec((CONV_K, CONV_COLS), lambda j, r: (0, j)),
                  pl.BlockSpec((1, CONV_COLS), lambda j, r: (0, j))],
        out_specs=[out, out],
        out_shape=[jax.ShapeDtypeStruct((s, CONV_CH), F32), jax.ShapeDtypeStruct((s, CONV_CH), F32)],
        compiler_params=_cparams(("parallel", "arbitrary")))(proj, proj, conv_w, conv_b)


def _conv_bwd(dact, pre, proj, conv_w):
    s = proj.shape[0]
    nr = s // ROW_TILE
    cb0 = OFF_XBC // CONV_COLS
    hb = ROW_TILE // HALO
    last_halo = s // HALO - 1
    cur = pl.BlockSpec((ROW_TILE, CONV_COLS), lambda j, r: (r, j))
    nxt = pl.BlockSpec((HALO, CONV_COLS), lambda j, r: (jnp.minimum((r + 1) * hb, last_halo), j))
    ucur = pl.BlockSpec((ROW_TILE, CONV_COLS), lambda j, r: (r, cb0 + j))
    uprev = pl.BlockSpec((HALO, CONV_COLS), lambda j, r: (jnp.maximum(r * hb - 1, 0), cb0 + j))
    wspec = pl.BlockSpec((CONV_K, CONV_COLS), lambda j, r: (0, j))
    bspec = pl.BlockSpec((1, CONV_COLS), lambda j, r: (0, j))

    def dsilu(p):
        sg = _sigmoid(p)
        return sg * (1.0 + p * (1.0 - sg))

    def body(da_ref, dan_ref, pre_ref, pren_ref, u_ref, up_ref, w_ref, du_ref, dw_ref, db_ref):
        r = pl.program_id(1)

        @pl.when(r == 0)
        def _():
            dw_ref[...] = jnp.zeros_like(dw_ref)
            db_ref[...] = jnp.zeros_like(db_ref)

        dpre = da_ref[...] * dsilu(pre_ref[...])
        dnext = jnp.where(r < nr - 1, dan_ref[...] * dsilu(pren_ref[...]), 0.0)
        u = u_ref[...]
        halo = jnp.where(r > 0, up_ref[...], 0.0)
        du = w_ref[CONV_K - 1:CONV_K, :] * dpre
        dws = [jnp.sum(dpre * u, axis=0, keepdims=True)]
        for k in range(1, CONV_K):
            du = du + w_ref[CONV_K - 1 - k:CONV_K - k, :] * _shift_up(dpre, dnext, k)
            dws.append(jnp.sum(dpre * _shift_down(u, halo, k), axis=0, keepdims=True))
        du_ref[...] = du.astype(BF16)
        dw_ref[...] += jnp.concatenate(dws[::-1], axis=0)
        db_ref[...] += jnp.sum(dpre, axis=0, keepdims=True)

    return pl.pallas_call(
        body, name="conv_bwd", grid=(CONV_CH // CONV_COLS, nr),
        in_specs=[cur, nxt, cur, nxt, ucur, uprev, wspec],
        out_specs=[cur, wspec, bspec],
        out_shape=[jax.ShapeDtypeStruct((s, CONV_CH), BF16), jax.ShapeDtypeStruct((CONV_K, CONV_CH), F32),
                   jax.ShapeDtypeStruct((1, CONV_CH), F32)],
        compiler_params=_cparams(("parallel", "arbitrary")))(dact, dact, pre, pre, proj, proj, conv_w)


def _ssd_common(dtr, dtb, alog):
    lane = lax.broadcasted_iota(jnp.int32, (1, LANE), 1)
    head_lane = lane < SSD_HEADS
    dt = jnp.where(head_lane, _softplus(dtr + dtb), 0.0)
    a = jnp.where(head_lane, -jnp.exp(alog), 0.0)
    row = lax.broadcasted_iota(jnp.int32, (SSD_CHUNK, SSD_CHUNK), 0)
    col = lax.broadcasted_iota(jnp.int32, (SSD_CHUNK, SSD_CHUNK), 1)
    tril = row >= col
    cs = _dot(tril.astype(F32), dt * a, NN, precision=HIGHEST)
    return dt, a, cs, cs.T, tril, lane


def _split_bf16(v, passes):
    terms, rest = [], v
    for _ in range(passes):
        t = rest.astype(BF16)
        terms.append(t)
        rest = rest - t.astype(F32)
    return terms


def _dot_split(v, m, dims, passes):
    out = None
    for t in _split_bf16(v, passes):
        part = _dot(t, m, dims)
        out = part if out is None else out + part
    return out


def _ssd_constants():
    heads = jnp.arange(LANE)[:, None]
    exp_mat = (heads == (jnp.arange(SSD_D_INNER)[None, :] // HEAD_DIM)).astype(BF16)
    ind4 = ((jnp.arange(SSD_HEADS * SSD_CHUNK)[:, None] // SSD_CHUNK) == jnp.arange(LANE)[None, :]).astype(BF16)
    return exp_mat, ind4


def _expand_heads(v):
    return jnp.repeat(v[:, :SSD_HEADS], HEAD_DIM, axis=1)


def _ssd_prep(dtr, dtb, alog, exp_mat):
    dt, a, cs, cst, tril, lane = _ssd_common(dtr, dtb, alog)
    return dt, a, cs, cst, tril, lane, _dot_split(dt, exp_mat, NN, 2), _dot_split(cs, exp_mat, NN, 3)


def _chunk_decay_rows(cs, g):
    parts = []
    for e in range(HEADS_PER_GROUP):
        h = g * HEADS_PER_GROUP + e
        parts.append(jnp.broadcast_to(jnp.exp(cs[SSD_CHUNK - 1:SSD_CHUNK, h:h + 1]), (HEAD_DIM, SSD_STATE)))
    return jnp.concatenate(parts, axis=0)


def _ssd_fwd(proj, act, dtb, alog, dsk, nw):
    s = proj.shape[0]
    nc = s // SSD_CHUNK
    bc_w = SSD_GROUPS * SSD_STATE
    exp_mat, _ = _ssd_constants()

    def body(z_ref, dtr_ref, xs_ref, b_ref, c_ref, dtb_ref, alog_ref, dskx_ref, nw_ref, exp_ref,
             ypre_ref, yssd_ref, hall_ref, h_scr):
        @pl.when(pl.program_id(0) == 0)
        def _():
            h_scr[...] = jnp.zeros_like(h_scr)

        dt, a, cs, cst, tril, lane, dtx, csx = _ssd_prep(dtr_ref[...], dtb_ref[...], alog_ref[...], exp_ref[...])
        cs_last_x = csx[SSD_CHUNK - 1:SSD_CHUNK, :]
        xs = xs_ref[...]
        xdt = xs * dtx
        xdtb = xdt.astype(BF16)
        xdec = (xdt * jnp.exp(cs_last_x - csx)).astype(BF16)
        ecsx = jnp.exp(csx)
        head_of_lane = lax.broadcasted_iota(jnp.int32, (1, GROUP_WIDTH), 1) // HEAD_DIM
        for g in range(SSD_GROUPS):
            gs = slice(g * GROUP_WIDTH, (g + 1) * GROUP_WIDTH)
            bg = b_ref[:, g * SSD_STATE:(g + 1) * SSD_STATE].astype(BF16)
            cg = c_ref[:, g * SSD_STATE:(g + 1) * SSD_STATE].astype(BF16)
            cb = _dot(cg, bg, NT)
            hprev = h_scr[gs, :]
            hall_ref[0, gs, :] = hprev
            gms, rhs = [], []
            xg = xdtb[:, gs]
            for e in range(HEADS_PER_GROUP):
                h = g * HEADS_PER_GROUP + e
                lm = jnp.exp(jnp.where(tril, cs[:, h:h + 1] - cst[h:h + 1, :], -1e30))
                gms.append((cb * lm).astype(BF16))
                rhs.append(jnp.where(head_of_lane == e, xg, jnp.zeros_like(xg)))
            y = _dot(jnp.concatenate(gms, axis=1), jnp.concatenate(rhs, axis=0), NN)
            y = y + ecsx[:, gs] * _dot(cg, hprev.astype(BF16), NT)
            y = y + dskx_ref[:, gs] * xs[:, gs]
            h_scr[gs, :] = hprev * _chunk_decay_rows(cs, g) + _dot(xdec[:, gs], bg, TN)
            ypre_ref[:, gs] = y
            z = z_ref[:, gs]
            yg = y * (z * _sigmoid(z))
            r = lax.rsqrt(jnp.mean(yg * yg, axis=-1, keepdims=True) + EPS)
            yssd_ref[:, gs] = (yg * r * nw_ref[:, gs]).astype(BF16)

    row_d = lambda cb: pl.BlockSpec((SSD_CHUNK, SSD_D_INNER), lambda c: (c, cb))
    small = pl.BlockSpec((1, LANE), lambda c: (0, 0))
    wide = pl.BlockSpec((1, SSD_D_INNER), lambda c: (0, 0))
    return pl.pallas_call(
        body, name="ssd_fwd", grid=(nc,),
        in_specs=[row_d(OFF_Z // SSD_D_INNER),
                  pl.BlockSpec((SSD_CHUNK, LANE), lambda c: (c, OFF_DT // LANE)),
                  row_d(0),
                  pl.BlockSpec((SSD_CHUNK, bc_w), lambda c: (c, SSD_D_INNER // bc_w)),
                  pl.BlockSpec((SSD_CHUNK, bc_w), lambda c: (c, SSD_D_INNER // bc_w + 1)),
                  small, small, wide, wide, pl.BlockSpec((LANE, SSD_D_INNER), lambda c: (0, 0))],
        out_specs=[row_d(0), row_d(0), pl.BlockSpec((1, SSD_D_INNER, SSD_STATE), lambda c: (c, 0, 0))],
        out_shape=[jax.ShapeDtypeStruct((s, SSD_D_INNER), F32), jax.ShapeDtypeStruct((s, SSD_D_INNER), BF16),
                   jax.ShapeDtypeStruct((nc, SSD_D_INNER, SSD_STATE), F32)],
        scratch_shapes=[pltpu.VMEM((SSD_D_INNER, SSD_STATE), F32)],
        compiler_params=_cparams(("arbitrary",)))(proj, proj, act, act, act, dtb, alog, _expand_heads(dsk), nw, exp_mat)


def _ssd_bwd(dycat, ypre, proj, act, hall, dtb, alog, dsk, nw, comm=None):
    s = proj.shape[0]
    nc = s // SSD_CHUNK
    bc_w = SSD_GROUPS * SSD_STATE

    exp_mat, ind4 = _ssd_constants()
    seg_passes = 2

    def body(dy_ref, ypre_ref, z_ref, dtr_ref, xs_ref, b_ref, c_ref, hall_ref, dtb_ref, alog_ref, dskx_ref, nw_ref,
             exp_ref, ind4_ref, dz_ref, dact_ref, ddtr_ref, da_ref, ddsk_ref, ddtb_ref, dnw_ref, dh_scr):
        @pl.when(pl.program_id(0) == 0)
        def _():
            dh_scr[...] = jnp.zeros_like(dh_scr)
            da_ref[...] = jnp.zeros_like(da_ref)
            ddsk_ref[...] = jnp.zeros_like(ddsk_ref)
            ddtb_ref[...] = jnp.zeros_like(ddtb_ref)
            dnw_ref[...] = jnp.zeros_like(dnw_ref)

        dtr = dtr_ref[...]
        dt, a, cs, cst, tril, lane, dtx, csx = _ssd_prep(dtr, dtb_ref[...], alog_ref[...], exp_ref[...])
        cs_last_x = csx[SSD_CHUNK - 1:SSD_CHUNK, :]
        xs = xs_ref[...]
        xdt = xs * dtx
        xdtb = xdt.astype(BF16)
        decx = jnp.exp(cs_last_x - csx)
        xdecf = xdt * decx
        xdec = xdecf.astype(BF16)
        ecsx = jnp.exp(csx)
        head_of_lane = lax.broadcasted_iota(jnp.int32, (1, GROUP_WIDTH), 1) // HEAD_DIM
        last_row = lax.broadcasted_iota(jnp.int32, (SSD_CHUNK, 1), 0) == SSD_CHUNK - 1
        dcs_col = jnp.zeros((SSD_CHUNK, LANE), F32)
        dcs_row = jnp.zeros((SSD_CHUNK, LANE), F32)
        ddt = jnp.zeros((SSD_CHUNK, LANE), F32)
        ddsk = jnp.zeros((1, LANE), F32)
        hsum = jnp.zeros((1, LANE), F32)
        t1_sum = jnp.zeros((1, LANE), F32)
        for g in range(SSD_GROUPS):
            gs = slice(g * GROUP_WIDTH, (g + 1) * GROUP_WIDTH)
            bsl = slice(g * SSD_STATE, (g + 1) * SSD_STATE)
            exp_g = exp_ref[:, gs]
            ind4_g = ind4_ref[g * HEADS_PER_GROUP * SSD_CHUNK:(g + 1) * HEADS_PER_GROUP * SSD_CHUNK, :]
            z = z_ref[:, gs]
            sg = _sigmoid(z)
            sz = z * sg
            ypre = ypre_ref[:, gs]
            yg = ypre * sz
            r = lax.rsqrt(jnp.mean(yg * yg, axis=-1, keepdims=True) + EPS)
            nrm = yg * r
            dyo_n = dy_ref[:, gs]
            dnw_ref[:, gs] += jnp.sum(dyo_n * nrm, axis=0, keepdims=True)
            dn = dyo_n * nw_ref[:, gs]
            dyg = r * (dn - nrm * jnp.mean(dn * nrm, axis=-1, keepdims=True))
            dz_ref[:, gs] = (dyg * ypre * (sg * (1.0 + z * (1.0 - sg)))).astype(BF16)
            dy = dyg * sz

            bg = b_ref[:, bsl].astype(BF16)
            cg = c_ref[:, bsl].astype(BF16)
            cb = _dot(cg, bg, NT)
            hprev = hall_ref[0, gs, :]
            hb = hprev.astype(BF16)
            dhn = dh_scr[gs, :]
            dhb = dhn.astype(BF16)
            xs_g, xdt_g = xs[:, gs], xdtb[:, gs]
            w_off = _dot(cg, hb, NT)
            dyo = dy * ecsx[:, gs]
            dyob = dyo.astype(BF16)
            dcg = _dot(dyob, hb, NN)
            dh_y = _dot(dyob, cg, TN)
            r_st = _dot(bg, dhb, NT)
            dbg = _dot(xdec[:, gs], dhb, NN)
            dyb = dy.astype(BF16)
            gms, gmbs, lms, dys = [], [], [], []
            for e in range(HEADS_PER_GROUP):
                h = g * HEADS_PER_GROUP + e
                lm = jnp.exp(jnp.where(tril, cs[:, h:h + 1] - cst[h:h + 1, :], -1e30))
                gm = cb * lm
                lms.append(lm)
                gms.append(gm)
                gmbs.append(gm.astype(BF16))
                dys.append(jnp.where(head_of_lane == e, dyb, jnp.zeros_like(dyb)))
            dxdt = _dot(jnp.concatenate(gmbs, axis=0), jnp.concatenate(dys, axis=0), TN) + decx[:, gs] * r_st
            dcb = jnp.zeros((SSD_CHUNK, SSD_CHUNK), F32)
            mms = []
            for e in range(HEADS_PER_GROUP):
                dg = _dot(dys[e], xdt_g, NT)
                mms.append(dg * gms[e])
                dcb = dcb + dg * lms[e]
            seg = _dot_split(jnp.concatenate([dyo * w_off, xdecf[:, gs] * r_st, dxdt * xs_g, dy * xs_g], axis=0), exp_g, NT, seg_passes)
            v1, t1, ddt_g, dsk_g = [seg[i * SSD_CHUNK:(i + 1) * SSD_CHUNK] for i in range(4)]
            dcs_col = dcs_col + v1 - t1 + _dot_split(jnp.concatenate(mms, axis=1), ind4_g, NN, seg_passes)
            for t in _split_bf16(jnp.concatenate(mms, axis=0), seg_passes):
                dcs_row = dcs_row + _dot(ind4_g, t, TN)
            ddt = ddt + ddt_g
            ddsk = ddsk + jnp.sum(dsk_g, axis=0, keepdims=True)
            t1_sum = t1_sum + jnp.sum(t1, axis=0, keepdims=True)
            for e in range(HEADS_PER_GROUP):
                h = g * HEADS_PER_GROUP + e
                hs = slice(e * HEAD_DIM, (e + 1) * HEAD_DIM)
                hsum = hsum + jnp.where(lane == h, jnp.sum(dhn[hs, :] * hprev[hs, :]).reshape(1, 1), 0.0)
            dh_scr[gs, :] = dhn * _chunk_decay_rows(cs, g) + dh_y
            dcbb = dcb.astype(BF16)
            dact_ref[:, gs] = dxdt * dtx[:, gs] + dskx_ref[:, gs] * dy
            dact_ref[:, SSD_D_INNER + g * SSD_STATE:SSD_D_INNER + (g + 1) * SSD_STATE] = dbg + _dot(dcbb, cg, TN)
            dact_ref[:, SSD_D_INNER + bc_w + g * SSD_STATE:SSD_D_INNER + bc_w + (g + 1) * SSD_STATE] = dcg + _dot(dcbb, bg, NN)
        dlast = t1_sum + jnp.exp(cs[SSD_CHUNK - 1:SSD_CHUNK, :]) * hsum
        dcs = dcs_col - dcs_row.T + jnp.where(last_row, dlast, 0.0)
        row = lax.broadcasted_iota(jnp.int32, (SSD_CHUNK, SSD_CHUNK), 0)
        col = lax.broadcasted_iota(jnp.int32, (SSD_CHUNK, SSD_CHUNK), 1)
        dda = _dot((col >= row).astype(F32), dcs, NN, precision=HIGHEST)
        ddt = ddt + dda * a
        da_ref[...] += jnp.sum(dda * dt, axis=0, keepdims=True)
        ddtr = jnp.where(lane < SSD_HEADS, ddt * _sigmoid(dtr + dtb_ref[...]), 0.0)
        ddtr_ref[...] = ddtr.astype(BF16)
        ddtb_ref[...] += jnp.sum(ddtr, axis=0, keepdims=True)
        ddsk_ref[...] += ddsk

    rev = lambda c: nc - 1 - c
    row_d = lambda cb: pl.BlockSpec((SSD_CHUNK, SSD_D_INNER), lambda c: (rev(c), cb))
    small = pl.BlockSpec((1, LANE), lambda c: (0, 0))
    wide = pl.BlockSpec((1, SSD_D_INNER), lambda c: (0, 0))
    small_shape = jax.ShapeDtypeStruct((1, LANE), F32)
    return _pcall(
        body, (dycat, ypre, proj, proj, act, act, act, hall, dtb, alog, _expand_heads(dsk), nw, exp_mat, ind4),
        name="ssd_bwd", grid=(nc,),
        in_specs=[row_d(0), row_d(0), row_d(OFF_Z // SSD_D_INNER),
                  pl.BlockSpec((SSD_CHUNK, LANE), lambda c: (rev(c), OFF_DT // LANE)),
                  row_d(0),
                  pl.BlockSpec((SSD_CHUNK, bc_w), lambda c: (rev(c), SSD_D_INNER // bc_w)),
                  pl.BlockSpec((SSD_CHUNK, bc_w), lambda c: (rev(c), SSD_D_INNER // bc_w + 1)),
                  pl.BlockSpec((1, SSD_D_INNER, SSD_STATE), lambda c: (rev(c), 0, 0)),
                  small, small, wide, wide, pl.BlockSpec((LANE, SSD_D_INNER), lambda c: (0, 0)),
                  pl.BlockSpec((SSD_HEADS * SSD_CHUNK, LANE), lambda c: (0, 0))],
        out_specs=[row_d(0), pl.BlockSpec((SSD_CHUNK, CONV_CH), lambda c: (rev(c), 0)),
                   pl.BlockSpec((SSD_CHUNK, LANE), lambda c: (rev(c), 0)), small, small, small, wide],
        out_shape=[jax.ShapeDtypeStruct((s, SSD_D_INNER), BF16), jax.ShapeDtypeStruct((s, CONV_CH), F32),
                   jax.ShapeDtypeStruct((s, LANE), BF16), small_shape, small_shape, small_shape,
                   jax.ShapeDtypeStruct((1, SSD_D_INNER), F32)],
        scratch_shapes=[pltpu.VMEM((SSD_D_INNER, SSD_STATE), F32)], sem=("arbitrary",), comm=comm)


def _head_mean_matrix():
    row = lax.broadcasted_iota(jnp.int32, (LANE, LANE), 0) // HEAD_DIM
    col = lax.broadcasted_iota(jnp.int32, (LANE, LANE), 1) // HEAD_DIM
    return (row == col).astype(F32)


def _head_sum2(v, ones_bd):
    hi = v.astype(BF16)
    lo = (v - hi.astype(F32)).astype(BF16)
    return _dot(hi, ones_bd, NN) + _dot(lo, ones_bd, NN)


def _head_norm(x, w, scale, ones_bd):
    ms = _head_sum2(x * x, ones_bd) * (1.0 / HEAD_DIM)
    return (x * lax.rsqrt(ms + EPS)) * (w * scale)


PRO_ROWS = 256
ATT_UNROLL = 4
KEYS = 2 * ATT_BLK
NEG = -1e30
HALF = HEAD_DIM // 2


def _rows(start, size, dil):
    return pl.ds(start, size) if dil == 1 else pl.ds(start, size, stride=dil)


def _fill_bias(bias_ref):
    row = lax.broadcasted_iota(jnp.int32, (ATT_BLK, 2 * KEYS), 0)
    col = lax.broadcasted_iota(jnp.int32, (ATT_BLK, 2 * KEYS), 1) & (KEYS - 1)
    for first, off in ((0, 0), (1, ATT_BLK)):
        dist = off + row - col
        bias_ref[first] = jnp.where((dist >= 0) & (dist <= ATT_BLK), 0.0, NEG)


def _pair(a, b):
    return jnp.concatenate([jnp.broadcast_to(a, (ATT_BLK, KEYS)), jnp.broadcast_to(b, (ATT_BLK, KEYS))], axis=1)


def _split_heads(x, is_a):
    zero = jnp.zeros_like(x)
    return jnp.concatenate([jnp.where(is_a, x, zero), jnp.where(is_a, zero, x)], axis=0)


def _block_ids(b, nb):
    i = b & (nb - 1)
    q0 = pl.multiple_of(b * ATT_BLK, ATT_BLK)
    k0 = pl.multiple_of((b - jnp.minimum(i, 1)) * ATT_BLK, ATT_BLK)
    return pl.ds(q0, ATT_BLK), pl.ds(k0, KEYS), jnp.minimum(i, 1)


def _att_fwd(proj, qw, kw, comm=None):
    s = proj.shape[0]
    nblk = s // ATT_BLK
    assert all((s // d) // ATT_BLK >= 2 for d in DILATIONS)
    blk = lambda off: pl.BlockSpec((s, LANE), lambda i: (0, off // LANE + i))
    wspec = pl.BlockSpec((1, LANE), lambda i: (0, i))
    oblk = pl.BlockSpec((s, LANE), lambda i: (0, i))

    def body(q_ref, k_ref, v_ref, qw_ref, kw_ref, o_ref, lse_ref, qn, kn, q_cm, k_cm, v_cm, m_acc, l_acc, o_d, m_d, l_d, bias):
        ones_bd = _head_mean_matrix().astype(BF16)
        is_a = lax.broadcasted_iota(jnp.int32, (1, LANE), 1) < HEAD_DIM
        ones_ext = _split_heads(jnp.ones((KEYS, LANE), BF16), is_a)
        _fill_bias(bias)

        def pro(j, c):
            rows = pl.ds(pl.multiple_of(j * PRO_ROWS, PRO_ROWS), PRO_ROWS)
            qn[rows, :] = _head_norm(q_ref[rows, :], qw_ref[...], HEAD_DIM ** -0.5, ones_bd)
            kn[rows, :] = _head_norm(k_ref[rows, :], kw_ref[...], 1.0, ones_bd)
            return c

        lax.fori_loop(0, s // PRO_ROWS, pro, 0)

        for dil in DILATIONS:
            ln = s // dil
            nb = ln // ATT_BLK
            o_out, m_out, l_out = (o_ref, m_acc, l_acc) if dil == 1 else (o_d, m_d, l_d)
            for r in range(dil):
                def relayout(j, c, dil=dil, r=r, ln=ln):
                    j0 = pl.multiple_of(j * PRO_ROWS, PRO_ROWS)
                    src = _rows(r + dil * j0, PRO_ROWS, dil)
                    dst = pl.ds(r * ln + j0, PRO_ROWS)
                    q_cm[dst, :] = qn[src, :].astype(BF16)
                    k_cm[dst, :] = kn[src, :].astype(BF16)
                    v_cm[dst, :] = v_ref[src, :].astype(BF16)
                    return c

                lax.fori_loop(0, ln // PRO_ROWS, relayout, 0)

            def step(b, c, nb=nb, o_out=o_out, m_out=m_out, l_out=l_out):
                qrows, krows, first = _block_ids(b, nb)
                kb = _split_heads(k_cm[krows, :], is_a)
                vb = jnp.concatenate([_split_heads(v_cm[krows, :], is_a), ones_ext], axis=1)
                sc = _dot(q_cm[qrows, :], kb, NT) + bias[first]
                m_a = jnp.max(sc[:, :KEYS], axis=-1, keepdims=True)
                m_b = jnp.max(sc[:, KEYS:], axis=-1, keepdims=True)
                p = jnp.exp(sc - _pair(m_a, m_b)).astype(BF16)
                ol = _dot(p, vb, NN)
                o_out[qrows, :] = ol[:, :LANE]
                l_out[qrows, :] = ol[:, LANE:]
                m_out[qrows, :] = jnp.where(is_a, m_a, m_b)
                return c

            lax.fori_loop(0, nblk, step, 0, unroll=ATT_UNROLL)

            if dil > 1:
                for r in range(dil):
                    def merge(j, c, dil=dil, r=r, ln=ln):
                        j0 = pl.multiple_of(j * PRO_ROWS, PRO_ROWS)
                        nat = _rows(r + dil * j0, PRO_ROWS, dil)
                        cm = pl.ds(r * ln + j0, PRO_ROWS)
                        m_old, m_new = m_acc[nat, :], m_d[cm, :]
                        m = jnp.maximum(m_old, m_new)
                        a_old, a_new = jnp.exp(m_old - m), jnp.exp(m_new - m)
                        o_ref[nat, :] = a_old * o_ref[nat, :] + a_new * o_d[cm, :]
                        l_acc[nat, :] = a_old * l_acc[nat, :] + a_new * l_d[cm, :]
                        m_acc[nat, :] = m
                        return c

                    lax.fori_loop(0, ln // PRO_ROWS, merge, 0)

        def epi(j, c):
            rows = pl.ds(pl.multiple_of(j * PRO_ROWS, PRO_ROWS), PRO_ROWS)
            l = l_acc[rows, :]
            o_ref[rows, :] = o_ref[rows, :] / l
            lse_ref[rows, :] = m_acc[rows, :] + jnp.log(l)
            return c

        lax.fori_loop(0, s // PRO_ROWS, epi, 0)

    f = jax.ShapeDtypeStruct((s, ATT_D), F32)
    scr = pltpu.VMEM((s, LANE), F32)
    scb = pltpu.VMEM((s, LANE), BF16)
    return _pcall(
        body, (proj, proj, proj, qw, kw), name="att_fwd", grid=(ATT_D // LANE,),
        in_specs=[blk(OFF_Q), blk(OFF_K), blk(OFF_V), wspec, wspec], out_specs=[oblk, oblk], out_shape=[f, f],
        scratch_shapes=[scr, scr, scb, scb, scb, scr, scr, scr, scr, scr, pltpu.VMEM((2, ATT_BLK, 2 * KEYS), F32)],
        sem=("parallel",), comm=comm)


def _att_bwd(proj, do, stats, qw, kw, comm=None):
    s = proj.shape[0]
    nblk = s // ATT_BLK
    blk = lambda off: pl.BlockSpec((s, LANE), lambda i: (0, off // LANE + i))
    wspec = pl.BlockSpec((1, LANE), lambda i: (0, i))
    oblk = pl.BlockSpec((s, LANE), lambda i: (0, i))

    def body(q_ref, k_ref, v_ref, do_ref, st_ref, qw_ref, kw_ref, dq_ref, dk_ref, dv_ref, dqw_ref, dkw_ref,
             qn, kn, q_cm, do_cm, k_cm, v_cm, st_cm, dq_acc, dk_acc, dv_acc, dq_d, dk_d, dv_d, bias):
        ones_bd = _head_mean_matrix().astype(BF16)
        is_a = lax.broadcasted_iota(jnp.int32, (1, LANE), 1) < HEAD_DIM
        _fill_bias(bias)
        zero = jnp.zeros((PRO_ROWS, LANE), F32)

        def pro(j, c):
            rows = pl.ds(pl.multiple_of(j * PRO_ROWS, PRO_ROWS), PRO_ROWS)
            qn[rows, :] = _head_norm(q_ref[rows, :], qw_ref[...], HEAD_DIM ** -0.5, ones_bd)
            kn[rows, :] = _head_norm(k_ref[rows, :], kw_ref[...], 1.0, ones_bd)
            dk_acc[rows, :] = zero
            dv_acc[rows, :] = zero
            return c

        lax.fori_loop(0, s // PRO_ROWS, pro, 0)

        for dil in DILATIONS:
            ln = s // dil
            nb = ln // ATT_BLK
            dq_o, dk_o, dv_o = (dq_acc, dk_acc, dv_acc) if dil == 1 else (dq_d, dk_d, dv_d)
            for r in range(dil):
                def relayout(j, c, dil=dil, r=r, ln=ln):
                    j0 = pl.multiple_of(j * PRO_ROWS, PRO_ROWS)
                    src = _rows(r + dil * j0, PRO_ROWS, dil)
                    dst = pl.ds(r * ln + j0, PRO_ROWS)
                    q_cm[dst, :] = qn[src, :].astype(BF16)
                    k_cm[dst, :] = kn[src, :].astype(BF16)
                    v_cm[dst, :] = v_ref[src, :].astype(BF16)
                    do_cm[dst, :] = do_ref[src, :].astype(BF16)
                    st_cm[dst, :] = st_ref[src, :]
                    if dil > 1:
                        dk_d[dst, :] = zero
                        dv_d[dst, :] = zero
                    return c

                lax.fori_loop(0, ln // PRO_ROWS, relayout, 0)

            def step(b, c, nb=nb, dq_o=dq_o, dk_o=dk_o, dv_o=dv_o):
                qrows, krows, first = _block_ids(b, nb)
                qb = q_cm[qrows, :]
                dob = do_cm[qrows, :]
                kb = _split_heads(k_cm[krows, :], is_a)
                vb = _split_heads(v_cm[krows, :], is_a)
                st = st_cm[qrows, :]
                sc = _dot(qb, kb, NT) + bias[first]
                p = jnp.exp(sc - _pair(st[:, 0:1], st[:, HEAD_DIM:HEAD_DIM + 1]))
                dp = _dot(dob, vb, NT)
                ds = (p * (dp - _pair(st[:, HALF:HALF + 1], st[:, HEAD_DIM + HALF:HEAD_DIM + HALF + 1]))).astype(BF16)
                dq_o[qrows, :] = _dot(ds, kb, NN)
                dkf = _dot(ds, qb, TN)
                dvf = _dot(p.astype(BF16), dob, TN)
                dk_o[krows, :] += jnp.where(is_a, dkf[:KEYS], dkf[KEYS:])
                dv_o[krows, :] += jnp.where(is_a, dvf[:KEYS], dvf[KEYS:])
                return c

            lax.fori_loop(0, nblk, step, 0, unroll=ATT_UNROLL)

            if dil > 1:
                for r in range(dil):
                    def merge(j, c, dil=dil, r=r, ln=ln):
                        j0 = pl.multiple_of(j * PRO_ROWS, PRO_ROWS)
                        nat = _rows(r + dil * j0, PRO_ROWS, dil)
                        cm = pl.ds(r * ln + j0, PRO_ROWS)
                        dq_acc[nat, :] += dq_d[cm, :]
                        dk_acc[nat, :] += dk_d[cm, :]
                        dv_acc[nat, :] += dv_d[cm, :]
                        return c

                    lax.fori_loop(0, ln // PRO_ROWS, merge, 0)

        def back(dn_out, x, w, scale):
            r = lax.rsqrt(_head_sum2(x * x, ones_bd) * (1.0 / HEAD_DIM) + EPS)
            nrm = x * r
            dw = jnp.sum(dn_out * nrm, axis=0, keepdims=True) * scale
            dn = dn_out * (w * scale)
            return r * (dn - nrm * (_head_sum2(dn * nrm, ones_bd) * (1.0 / HEAD_DIM))), dw

        def epi(j, c):
            rows = pl.ds(pl.multiple_of(j * PRO_ROWS, PRO_ROWS), PRO_ROWS)
            dq, dqw = back(dq_acc[rows, :], q_ref[rows, :], qw_ref[...], HEAD_DIM ** -0.5)
            dk, dkw = back(dk_acc[rows, :], k_ref[rows, :], kw_ref[...], 1.0)
            dq_ref[rows, :] = dq.astype(BF16)
            dk_ref[rows, :] = dk.astype(BF16)
            dv_ref[rows, :] = dv_acc[rows, :].astype(BF16)
            return (c[0] + dqw, c[1] + dkw)

        zrow = jnp.zeros((1, LANE), F32)
        dqw, dkw = lax.fori_loop(0, s // PRO_ROWS, epi, (zrow, zrow))
        dqw_ref[...] = dqw
        dkw_ref[...] = dkw

    o = jax.ShapeDtypeStruct((s, ATT_D), BF16)
    ov = jax.ShapeDtypeStruct((1, ATT_D), F32)
    scr = pltpu.VMEM((s, LANE), F32)
    scb = pltpu.VMEM((s, LANE), BF16)
    return _pcall(
        body, (proj, proj, proj, do, stats, qw, kw), name="att_bwd", grid=(ATT_D // LANE,),
        in_specs=[blk(OFF_Q), blk(OFF_K), blk(OFF_V), oblk, oblk, wspec, wspec],
        out_specs=[oblk, oblk, oblk, wspec, wspec], out_shape=[o, o, o, ov, ov],
        scratch_shapes=[scr, scr, scb, scb, scb, scb, scr, scr, scr, scr, scr, scr, scr, pltpu.VMEM((2, ATT_BLK, 2 * KEYS), F32)],
        sem=("parallel",), comm=comm)


def _att_norm_fwd(o, nw):
    s = o.shape[0]
    row = pl.BlockSpec((ROW_TILE, ATT_D), lambda i: (i, 0))
    vec = pl.BlockSpec((1, ATT_D), lambda i: (0, 0))

    def body(o_ref, nw_ref, y_ref):
        o = o_ref[...]
        r = lax.rsqrt(jnp.mean(o * o, axis=-1, keepdims=True) + EPS)
        y_ref[...] = (o * r * nw_ref[...]).astype(BF16)

    return pl.pallas_call(body, name="att_norm_fwd", grid=(s // ROW_TILE,), in_specs=[row, vec], out_specs=row,
                          out_shape=jax.ShapeDtypeStruct((s, ATT_D), BF16), compiler_params=_cparams(("parallel",)))(o, nw)


def _att_norm_bwd(dycat, o, lse, nw):
    s = o.shape[0]
    row = pl.BlockSpec((ROW_TILE, ATT_D), lambda i: (i, 0))
    vec = pl.BlockSpec((1, ATT_D), lambda i: (0, 0))

    def body(dy_ref, o_ref, lse_ref, nw_ref, do_ref, st_ref, dnw_ref):
        @pl.when(pl.program_id(0) == 0)
        def _():
            dnw_ref[...] = jnp.zeros_like(dnw_ref)

        o = o_ref[...]
        dy = dy_ref[...]
        r = lax.rsqrt(jnp.mean(o * o, axis=-1, keepdims=True) + EPS)
        nrm = o * r
        dnw_ref[...] += jnp.sum(dy * nrm, axis=0, keepdims=True)
        dn = dy * nw_ref[...]
        do = r * (dn - nrm * jnp.mean(dn * nrm, axis=-1, keepdims=True))
        do_ref[...] = do
        ones_bd = _head_mean_matrix().astype(BF16)
        prod = do * o
        delta = jnp.concatenate([_head_sum2(prod[:, j * LANE:(j + 1) * LANE], ones_bd) for j in range(ATT_D // LANE)], axis=1)
        lane = lax.broadcasted_iota(jnp.int32, (1, ATT_D), 1)
        st_ref[...] = jnp.where((lane & (HEAD_DIM - 1)) < HALF, lse_ref[...], delta)

    f = jax.ShapeDtypeStruct((s, ATT_D), F32)
    return pl.pallas_call(
        body, name="att_norm_bwd", grid=(s // ROW_TILE,),
        in_specs=[pl.BlockSpec((ROW_TILE, ATT_D), lambda i: (i, 1)), row, row, vec], out_specs=[row, row, vec],
        out_shape=[f, f, jax.ShapeDtypeStruct((1, ATT_D), F32)],
        compiler_params=_cparams(("arbitrary",)))(dycat, o, lse, nw)


def _ada_fwd(c_all, w_ada):
    def body(c_ref, w_ref, o_ref):
        cv = c_ref[...]
        o_ref[...] = _dot((cv * _sigmoid(cv)).astype(BF16), w_ref[...].astype(BF16), NN)

    return pl.pallas_call(body, name="ada_fwd", out_shape=jax.ShapeDtypeStruct((c_all.shape[0], w_ada.shape[1]), F32),
                          compiler_params=_cparams())(c_all, w_ada)


def _adamw_math(g, w, m, v):
    m_new = ADAM_B1 * m + (1.0 - ADAM_B1) * g
    v_new = ADAM_B2 * v + (1.0 - ADAM_B2) * (g * g)
    m_hat = m_new / (1.0 - ADAM_B1 ** ADAM_STEP)
    v_hat = v_new / (1.0 - ADAM_B2 ** ADAM_STEP)
    delta = -ADAM_LR * (m_hat / (jnp.sqrt(v_hat) + ADAM_EPS) + ADAM_WD * w)
    return delta, m_new, v_new


def _ada_bwd_adamw(c_all, dmod_cols, w, m, v):
    rows, cols = w.shape
    tr = 256
    blk = pl.BlockSpec((tr, cols), lambda i: (i, 0))

    def body(c_ref, d_ref, w_ref, m_ref, v_ref, g_ref, dl_ref, mo_ref, vo_ref):
        cv = c_ref[...]
        ca = cv * _sigmoid(cv)
        g = ca[:, 0:1] * d_ref[0:1, :]
        for b in range(1, N_DEV):
            g = g + ca[:, b:b + 1] * d_ref[b:b + 1, :]
        g_ref[...] = g
        dl_ref[...], mo_ref[...], vo_ref[...] = _adamw_math(g, w_ref[...], m_ref[...], v_ref[...])

    o = jax.ShapeDtypeStruct((rows, cols), F32)
    return pl.pallas_call(
        body, name="ada_bwd_adamw", grid=(rows // tr,),
        in_specs=[pl.BlockSpec((tr, N_DEV), lambda i: (i, 0)), pl.BlockSpec((N_DEV, cols), lambda i: (0, 0)), blk, blk, blk],
        out_specs=[blk] * 4, out_shape=[o, o, o, o], compiler_params=_cparams(("parallel",)))(c_all.T, dmod_cols, w, m, v)


def _reduce_adamw(slabs, w, m, v, name):
    rows, cols = w.shape
    tr = 128
    blk = pl.BlockSpec((tr, cols), lambda i: (i, 0))

    def body(s_ref, w_ref, m_ref, v_ref, g_ref, dl_ref, mo_ref, vo_ref):
        g = s_ref[0].astype(F32)
        for dev in range(1, N_DEV):
            g = g + s_ref[dev].astype(F32)
        g_ref[...] = g
        dl_ref[...], mo_ref[...], vo_ref[...] = _adamw_math(g, w_ref[...], m_ref[...], v_ref[...])

    o = jax.ShapeDtypeStruct((rows, cols), F32)
    return pl.pallas_call(
        body, name=name, grid=(rows // tr,),
        in_specs=[pl.BlockSpec((N_DEV, tr, cols), lambda i: (0, i, 0)), blk, blk, blk],
        out_specs=[blk] * 4, out_shape=[o, o, o, o], compiler_params=_cparams(("parallel",)))(slabs, w, m, v)


def _small_reduce_adamw(gathered, w, m, v):
    def body(s_ref, w_ref, m_ref, v_ref, g_ref, dl_ref, mo_ref, vo_ref):
        g = s_ref[0]
        for dev in range(1, N_DEV):
            g = g + s_ref[dev]
        g_ref[...] = g
        dl_ref[...], mo_ref[...], vo_ref[...] = _adamw_math(g, w_ref[...], m_ref[...], v_ref[...])

    o = jax.ShapeDtypeStruct(w.shape, F32)
    return pl.pallas_call(body, name="small_reduce_adamw", out_shape=[o, o, o, o], compiler_params=_cparams())(gathered, w, m, v)


def _adamw_small(g, w, m, v, name):
    def body(g_ref, w_ref, m_ref, v_ref, dl_ref, mo_ref, vo_ref):
        dl_ref[...], mo_ref[...], vo_ref[...] = _adamw_math(g_ref[...], w_ref[...], m_ref[...], v_ref[...])

    o = jax.ShapeDtypeStruct(w.shape, F32)
    return pl.pallas_call(body, name=name, out_shape=[o, o, o], compiler_params=_cparams())(g, w, m, v)


class _Exchange:
    def __init__(self, arrs, scatter):
        self.arrs, self.scatter, self.n = list(arrs), scatter, len(arrs)
        hbm = pl.BlockSpec(memory_space=pltpu.HBM)
        self.in_specs = [hbm] * self.n
        self.out_specs = [hbm] * self.n
        self.out_shape = [jax.ShapeDtypeStruct(a.shape if scatter else (N_DEV,) + a.shape, a.dtype) for a in self.arrs]
        self.scratch = [pltpu.SemaphoreType.DMA((self.n * (N_DEV - 1),)), pltpu.SemaphoreType.DMA((self.n * (N_DEV - 1),)),
                        pltpu.SemaphoreType.DMA((self.n,))]

    def _local(self, ins, outs, sems):
        me = 4 * lax.axis_index("x") + 2 * lax.axis_index("y") + lax.axis_index("c")
        return [pltpu.make_async_copy(ins[a].at[me] if self.scatter else ins[a], outs[a].at[me], sems[2].at[a])
                for a in range(self.n)]

    def _remote(self, ins, outs, sems, arriving):
        send_sems, recv_sems, _ = sems
        x, y, c = lax.axis_index("x"), lax.axis_index("y"), lax.axis_index("c")
        me = 4 * x + 2 * y + c
        remote = []
        for a in range(self.n):
            for k in range(1, N_DEV):
                px = 1 - x if k & 4 else x
                py = 1 - y if k & 2 else y
                pc = 1 - c if k & 1 else c
                peer = 4 * px + 2 * py + pc
                sem = a * (N_DEV - 1) + k - 1
                remote.append(pltpu.make_async_remote_copy(
                    src_ref=ins[a].at[peer] if self.scatter else ins[a], dst_ref=outs[a].at[peer if arriving else me],
                    send_sem=send_sems.at[sem], recv_sem=recv_sems.at[sem], device_id=(px, py, pc), device_id_type=MESH_IDS))
        return remote

    def start(self, ins, outs, sems):
        for cp in self._local(ins, outs, sems) + self._remote(ins, outs, sems, arriving=False):
            cp.start()

    def wait(self, ins, outs, sems):
        for send, arrival in zip(self._remote(ins, outs, sems, arriving=False), self._remote(ins, outs, sems, arriving=True)):
            send.wait_send()
            arrival.wait_recv()
        for cp in self._local(ins, outs, sems):
            cp.wait()


def _split_comm_refs(refs, n_in, n_out, n_scr, comm):
    nc = comm.n if comm is not None else 0
    ns = 3 if comm is not None else 0
    pos, groups = 0, []
    for cnt in (n_in, nc, n_out, nc, n_scr, ns):
        groups.append(refs[pos:pos + cnt])
        pos += cnt
    assert pos == len(refs), (pos, len(refs))
    return groups


def _pcall(body, args, *, name, grid, in_specs, out_specs, out_shape, scratch_shapes=(), sem=None, comm=None):
    in_specs, out_specs, out_shape, scratch_shapes = list(in_specs), list(out_specs), list(out_shape), list(scratch_shapes)
    n_in, n_out, n_scr = len(in_specs), len(out_specs), len(scratch_shapes)
    if comm is None:
        kernel_body = body
    else:
        def kernel_body(*refs):
            ins, cins, outs, couts, scr, sems = _split_comm_refs(refs, n_in, n_out, n_scr, comm)
            ids = [pl.program_id(a) for a in range(len(grid))]
            first, last = ids[0] == 0, ids[0] == grid[0] - 1
            for a in range(1, len(grid)):
                first, last = first & (ids[a] == 0), last & (ids[a] == grid[a] - 1)

            @pl.when(first)
            def _():
                comm.start(cins, couts, sems)

            body(*ins, *outs, *scr)

            @pl.when(last)
            def _():
                comm.wait(cins, couts, sems)

        in_specs, out_specs, out_shape = in_specs + comm.in_specs, out_specs + comm.out_specs, out_shape + comm.out_shape
        scratch_shapes, args = scratch_shapes + comm.scratch, list(args) + comm.arrs
        sem = ("arbitrary",) * len(grid)
    res = pl.pallas_call(kernel_body, name=name, grid=grid, in_specs=in_specs, out_specs=out_specs, out_shape=out_shape,
                         scratch_shapes=scratch_shapes, compiler_params=_cparams(sem))(*args)
    return res[:n_out], res[n_out:]


def _exchange(arrs, name, scatter):
    ex = _Exchange(arrs, scatter)

    def body(*refs):
        _, ins, _, outs, _, sems = _split_comm_refs(refs, 0, 0, 0, ex)
        ex.start(ins, outs, sems)
        ex.wait(ins, outs, sems)

    return pl.pallas_call(body, name=name, in_specs=ex.in_specs, out_specs=ex.out_specs, out_shape=ex.out_shape,
                          scratch_shapes=ex.scratch)(*arrs)


def _pad_lanes(v, width=LANE):
    return jnp.pad(v, ((0, 0), (0, width - v.shape[1])))


def _shards_to_cols(g):
    return jnp.transpose(g, (1, 0, 2)).reshape(g.shape[1], N_DEV * g.shape[2])


def _cols_to_shards(w):
    return w.astype(BF16).reshape(w.shape[0], N_DEV, w.shape[1] // N_DEV).transpose(1, 0, 2)


def _local_step(x, tgt, mod, w_in_p, conv_w, conv_b, dt_bias, a_log, d_skip, ssd_norm_w, q_norm_w, k_norm_w,
                attn_norm_w, w_out_sh, w_ff1_sh, w_ff2_sh, norm1_w, norm2_w):
    shift1, scale1, gate1, shift2, scale2, gate2 = [mod[i:i + 1] for i in range(N_MOD)]
    dtb, alog, dsk = _pad_lanes(dt_bias), _pad_lanes(a_log), _pad_lanes(d_skip)
    qw, kw = jnp.tile(q_norm_w, (1, ATT_HEADS)), jnp.tile(k_norm_w, (1, ATT_HEADS))

    h1 = _norm_mod_fwd(x, norm1_w, scale1, shift1, "norm1_fwd")
    proj = _matmul(h1, w_in_p, tm=1024, tn=896, tk=1024, name="in_proj")
    pre, act = _conv_fwd(proj, conv_w, conv_b)
    ypre, y_ssd, hall = _ssd_fwd(proj, act, dtb, alog, dsk, ssd_norm_w)
    (o_att, lse), (w_out_g, w_ff1_g, w_ff2_g) = _att_fwd(proj, qw, kw, comm=_Exchange([w_out_sh, w_ff1_sh, w_ff2_sh], scatter=False))
    w_out = w_out_g.reshape(2 * D_MODEL, D_MODEL)
    w_ff1 = _shards_to_cols(w_ff1_g)
    w_ff2 = w_ff2_g.reshape(D_FF, D_MODEL)
    y_att = _att_norm_fwd(o_att, attn_norm_w)
    ycat = jnp.concatenate([y_ssd, y_att], axis=1)
    mix = _matmul(ycat, w_out, tm=1024, tn=1024, tk=2048, name="out_proj")
    x1, h2 = _norm_mod_fwd(x, norm2_w, scale2, shift2, "norm2_fwd", res=mix, gate=gate1)
    u, act_ff = _matmul(h2, w_ff1, tm=1024, tn=1024, tk=1024, name="ff1", mode="relu2")
    ff = _matmul(act_ff, w_ff2, tm=1024, tn=1024, tk=2048, name="ff2")
    loss, dout, dff, dgate2 = _loss_head(x1, ff, gate2, tgt)

    du = _matmul(dff, w_ff2, tb=True, tm=1024, tn=1024, tk=1024, out_dtype=BF16, name="ff2_dx", mode="drelu2", u=u)
    g_ff2 = _matmul(act_ff, dff, ta=True, tm=1024, tn=1024, tk=1024, name="ff2_dw")
    dh2 = _matmul(du, w_ff1, tb=True, tm=1024, tn=1024, tk=2048, name="ff1_dx")
    g_ff1 = _matmul(h2, du, ta=True, tm=1024, tn=1024, tk=1024, name="ff1_dw")
    dx1, dshift2, dscale2, g_norm2, dmix, dgate1 = _norm_mod_bwd(dh2, x1, dout, norm2_w, scale2, "norm2_bwd", gate=gate1, mix=mix)

    dycat = _matmul(dmix, w_out, tb=True, tm=1024, tn=1024, tk=1024, name="out_proj_dx")
    g_out = _matmul(ycat, dmix, ta=True, tm=1024, tn=1024, tk=1024, name="out_proj_dw")
    do, stats, g_attn_norm = _att_norm_bwd(dycat, o_att, lse, attn_norm_w)
    ff_slabs = [_cols_to_shards(g_ff1), g_ff2.astype(BF16).reshape(N_DEV, D_FF // N_DEV, D_MODEL)]
    (dq, dk, dv, dqw, dkw), (s_ff1, s_ff2) = _att_bwd(proj, do, stats, qw, kw, comm=_Exchange(ff_slabs, scatter=True))
    out_slabs = [g_out.astype(BF16).reshape(N_DEV, 2 * D_MODEL // N_DEV, D_MODEL)]
    (dz, dact, ddtr, da, g_dsk, g_dtb, g_ssd_norm), (s_out,) = _ssd_bwd(
        dycat, ypre, proj, act, hall, dtb, alog, dsk, ssd_norm_w, comm=_Exchange(out_slabs, scatter=True))
    dxbc, g_conv_w, g_conv_b = _conv_bwd(dact, pre, proj, conv_w)
    dproj = jnp.concatenate([dz, dxbc, dq, dk, dv, ddtr], axis=1)
    g_in_p = _matmul(h1, dproj, ta=True, tm=1024, tn=896, tk=1024, name="in_proj_dw")
    in_slabs = [_cols_to_shards(_unpack_w_in(g_in_p))]
    dh1, (s_in,) = _matmul(dproj, w_in_p, tb=True, tm=1024, tn=1024, tk=896, name="in_proj_dx",
                           comm=_Exchange(in_slabs, scatter=True))
    grad_x, dshift1, dscale1, g_norm1 = _norm_mod_bwd(dh1, x, dx1, norm1_w, scale1, "norm1_bwd")

    dmod = jnp.concatenate([dshift1, dscale1, dgate1, dshift2, dscale2, dgate2], axis=0)
    g_alog = da[:, :SSD_HEADS] * (-jnp.exp(a_log))
    g_qw = dqw.reshape(ATT_HEADS, HEAD_DIM).sum(axis=0, keepdims=True)
    g_kw = dkw.reshape(ATT_HEADS, HEAD_DIM).sum(axis=0, keepdims=True)
    return dict(loss=loss, grad_x=grad_x, dmod=dmod, norm1_w=g_norm1, norm2_w=g_norm2, w_in=s_in, conv_w=g_conv_w,
                conv_b=g_conv_b, dt_bias=g_dtb[:, :SSD_HEADS], a_log=g_alog, d_skip=g_dsk[:, :SSD_HEADS],
                ssd_norm_w=g_ssd_norm, q_norm_w=g_qw, k_norm_w=g_kw, attn_norm_w=g_attn_norm, w_out=s_out,
                w_ff1=s_ff1, w_ff2=s_ff2)


def _pack_w_in(w_full):
    o_dt = SSD_D_INNER + CONV_CH
    o_q = o_dt + SSD_HEADS
    pad = jnp.zeros((w_full.shape[0], LANE - SSD_HEADS), w_full.dtype)
    return jnp.concatenate([w_full[:, :o_dt], w_full[:, o_q:], w_full[:, o_dt:o_q], pad], axis=1)


def _unpack_w_in(g_p):
    return jnp.concatenate([g_p[:, :OFF_Q], g_p[:, OFF_DT:OFF_DT + SSD_HEADS], g_p[:, OFF_Q:OFF_DT]], axis=1)


MISC_FIELDS = (("dt_bias", SSD_HEADS), ("a_log", SSD_HEADS), ("d_skip", SSD_HEADS), ("q_norm_w", HEAD_DIM), ("k_norm_w", HEAD_DIM))
SMALL_LAYOUT = (("b_ada", 6), ("norm1_w", 1), ("norm2_w", 1), ("conv_w", 8), ("conv_b", 2), ("ssd_norm_w", 1),
                ("attn_norm_w", 1), ("misc", 1))


def _pack_small(vals):
    rows = []
    for name, nrow in SMALL_LAYOUT:
        if name == "misc":
            misc = jnp.concatenate([vals[f].reshape(1, n) for f, n in MISC_FIELDS], axis=1)
            rows.append(_pad_lanes(misc, D_MODEL))
        elif name in vals:
            rows.append(vals[name].reshape(nrow, D_MODEL))
        else:
            rows.append(jnp.zeros((nrow, D_MODEL), F32))
    used = sum(n for _, n in SMALL_LAYOUT)
    rows.append(jnp.zeros((SMALL_ROWS - used, D_MODEL), F32))
    return jnp.concatenate(rows, axis=0)


def _unpack_small(packed):
    out, r = {}, 0
    for name, nrow in SMALL_LAYOUT:
        blk = packed[r:r + nrow]
        r += nrow
        if name == "misc":
            c0 = 0
            for f, n in MISC_FIELDS:
                out[f] = blk[:, c0:c0 + n]
                c0 += n
        elif name == "b_ada":
            out[name] = blk.reshape(1, N_MOD * D_MODEL)
        elif name == "conv_w":
            out[name] = blk.reshape(CONV_K, CONV_CH)
        elif name == "conv_b":
            out[name] = blk.reshape(1, CONV_CH)
        else:
            out[name] = blk
    return out


WEIGHT_NAMES = ("norm1_w", "norm2_w", "w_ada", "b_ada", "w_in", "conv_w", "conv_b", "dt_bias", "a_log", "d_skip",
                "ssd_norm_w", "q_norm_w", "k_norm_w", "attn_norm_w", "w_out", "w_ff1", "w_ff2")
SMALL_NAMES = ("norm1_w", "norm2_w", "b_ada", "conv_b", "dt_bias", "a_log", "d_skip", "ssd_norm_w", "q_norm_w",
               "k_norm_w", "attn_norm_w")


def kernel(x, c, norm1_w, norm2_w, w_ada, b_ada, w_in, conv_w, conv_b, dt_bias, a_log, d_skip, ssd_norm_w, q_norm_w, k_norm_w, attn_norm_w, w_out, w_ff1, w_ff2, loss_target, m_norm1_w, m_norm2_w, m_w_ada, m_b_ada, m_w_in, m_conv_w, m_conv_b, m_dt_bias, m_a_log, m_d_skip, m_ssd_norm_w, m_q_norm_w, m_k_norm_w, m_attn_norm_w, m_w_out, m_w_ff1, m_w_ff2, v_norm1_w, v_norm2_w, v_w_ada, v_b_ada, v_w_in, v_conv_w, v_conv_b, v_dt_bias, v_a_log, v_d_skip, v_ssd_norm_w, v_q_norm_w, v_k_norm_w, v_attn_norm_w, v_w_out, v_w_ff1, v_w_ff2):
    args = dict(locals())
    w = {n: args[n] for n in WEIGHT_NAMES}
    m = {n: args["m_" + n] for n in WEIGHT_NAMES}
    v = {n: args["v_" + n] for n in WEIGHT_NAMES}
    me = 4 * lax.axis_index("x") + 2 * lax.axis_index("y") + lax.axis_index("c")

    c_rows = jnp.pad(c, ((0, 7), (0, 0)))
    c_g, conv_g, w_in_g = _exchange([c_rows, w["conv_w"][0], w["w_in"][0].astype(BF16)], "gather_w_in", scatter=False)
    c_all = c_g[:, 0, :]
    conv_full = _shards_to_cols(conv_g)
    w_in_p = _pack_w_in(_shards_to_cols(w_in_g))

    mod_part = _ada_fwd(c_all, w["w_ada"][0])
    (mod_g,) = _exchange([mod_part], "gather_mod", scatter=False)
    mod_mine = lax.dynamic_index_in_dim(mod_g, me, axis=1, keepdims=False).reshape(1, N_MOD * D_MODEL) + w["b_ada"]
    mod = mod_mine.reshape(N_MOD, D_MODEL)

    res = _local_step(x[0], loss_target[0], mod, w_in_p, conv_full, w["conv_b"], w["dt_bias"], w["a_log"], w["d_skip"],
                      w["ssd_norm_w"], w["q_norm_w"], w["k_norm_w"], w["attn_norm_w"], w["w_out"][0].astype(BF16),
                      w["w_ff1"][0].astype(BF16), w["w_ff2"][0].astype(BF16), w["norm1_w"], w["norm2_w"])

    small_vals = {n: res[n] for n in SMALL_NAMES if n != "b_ada"}
    small_vals["b_ada"] = res["dmod"]
    small_vals["conv_w"] = res["conv_w"]
    (small_g,) = _exchange([_pack_small(small_vals)], "gather_small", scatter=False)

    grads, delta, new_m, new_v = {}, {}, {}, {}
    for name in ("w_in", "w_out", "w_ff1", "w_ff2"):
        outs = _reduce_adamw(res[name], w[name][0], m[name][0], v[name][0], "adamw_" + name)
        grads[name], delta[name], new_m[name], new_v[name] = [o[None] for o in outs]

    sm = _small_reduce_adamw(small_g, _pack_small({n: w[n] for n in SMALL_NAMES}), _pack_small({n: m[n] for n in SMALL_NAMES}),
                             _pack_small({n: v[n] for n in SMALL_NAMES}))
    sm = [_unpack_small(p) for p in sm]
    for n in SMALL_NAMES:
        grads[n], delta[n], new_m[n], new_v[n] = [p[n] for p in sm]
    shard_w = CONV_CH // N_DEV
    g_conv = lax.dynamic_slice_in_dim(sm[0]["conv_w"], me * shard_w, shard_w, axis=1)
    cw = _adamw_small(g_conv, w["conv_w"][0], m["conv_w"][0], v["conv_w"][0], "adamw_conv_w")
    grads["conv_w"] = g_conv[None]
    delta["conv_w"], new_m["conv_w"], new_v["conv_w"] = [o[None] for o in cw]

    ada_w = w_ada.shape[2]
    dmod_all = small_g[:, :N_MOD, :].reshape(N_DEV, N_MOD * D_MODEL)
    dmod_cols = lax.dynamic_slice_in_dim(dmod_all, me * ada_w, ada_w, axis=1)
    outs = _ada_bwd_adamw(c_all, dmod_cols, w["w_ada"][0], m["w_ada"][0], v["w_ada"][0])
    grads["w_ada"], delta["w_ada"], new_m["w_ada"], new_v["w_ada"] = [o[None] for o in outs]

    loss = lax.psum(res["loss"][0, 0], ("x", "y", "c"))
    return (loss, res["grad_x"][None], *[grads[n] for n in WEIGHT_NAMES], *[delta[n] for n in WEIGHT_NAMES],
            *[new_m[n] for n in WEIGHT_NAMES], *[new_v[n] for n in WEIGHT_NAMES])
```

```python
import functools

import jax
import jax.numpy as jnp
from jax import lax
from jax.experimental import pallas as pl
from jax.experimental.pallas import tpu as pltpu

F32 = jnp.float32
BF16 = jnp.bfloat16
HIGHEST = lax.Precision.HIGHEST
MESH_IDS = pl.DeviceIdType.MESH

N_DEV = 8
D_MODEL = 1024
HEAD_DIM = 64
SSD_HEADS = 16
SSD_GROUPS = 4
HEADS_PER_GROUP = SSD_HEADS // SSD_GROUPS
SSD_STATE = 128
SSD_CHUNK = 128
SSD_D_INNER = SSD_HEADS * HEAD_DIM
GROUP_WIDTH = SSD_D_INNER // SSD_GROUPS
CONV_K = 4
CONV_CH = SSD_D_INNER + 2 * SSD_GROUPS * SSD_STATE
ATT_HEADS = 16
ATT_D = ATT_HEADS * HEAD_DIM
ATT_BLK = 128
DILATIONS = (1, 4, 16)
D_FF = 4 * D_MODEL
N_MOD = 6
EPS = 1e-6
IN_W = SSD_D_INNER + CONV_CH + SSD_HEADS + 3 * ATT_D
LANE = 128
OFF_Z, OFF_XBC, OFF_Q, OFF_K, OFF_V, OFF_DT = 0, 1024, 3072, 4096, 5120, 6144
IN_WP = OFF_DT + LANE

ADAM_LR, ADAM_B1, ADAM_B2, ADAM_EPS, ADAM_WD, ADAM_STEP = 0.001, 0.9, 0.999, 1e-08, 0.01, 10
VMEM_LIMIT = 56 * 1024 * 1024
ROW_TILE = 512
SMALL_ROWS = 24


def _cparams(sem=None):
    return pltpu.CompilerParams(dimension_semantics=sem, vmem_limit_bytes=VMEM_LIMIT)


def _sigmoid(v):
    return 1.0 / (1.0 + jnp.exp(-v))


def _softplus(v):
    y = jnp.exp(-jnp.abs(v))
    small = y * (1.0 - y * (0.5 - y * (1.0 / 3.0)))
    return jnp.maximum(v, 0.0) + jnp.where(y < 0.01, small, jnp.log(1.0 + y))


def _dot(a, b, dims, precision=None):
    return lax.dot_general(a, b, (dims, ((), ())), preferred_element_type=F32, precision=precision)


NN = ((1,), (0,))
NT = ((1,), (1,))
TN = ((0,), (0,))


def _matmul(a, b, *, ta=False, tb=False, tm, tn, tk, out_dtype=F32, name, mode=None, u=None, comm=None):
    m, k = (a.shape[1], a.shape[0]) if ta else a.shape
    n = b.shape[0] if tb else b.shape[1]
    assert m % tm == 0 and n % tn == 0 and k % tk == 0, (name, m, n, k)
    nk = k // tk
    a_spec = pl.BlockSpec((tk, tm), lambda i, j, kk: (kk, i)) if ta else pl.BlockSpec((tm, tk), lambda i, j, kk: (i, kk))
    b_spec = pl.BlockSpec((tn, tk), lambda i, j, kk: (j, kk)) if tb else pl.BlockSpec((tk, tn), lambda i, j, kk: (kk, j))
    o_spec = pl.BlockSpec((tm, tn), lambda i, j, kk: (i, j))
    dims = ((0,) if ta else (1,), (1,) if tb else (0,))
    n_out = 2 if mode == "relu2" else 1

    def body(*refs):
        if mode == "drelu2":
            a_ref, b_ref, u_ref = refs[:3]
            rest = refs[3:]
        else:
            a_ref, b_ref = refs[:2]
            u_ref = None
            rest = refs[2:]
        outs = rest[:n_out]
        part = _dot(a_ref[...], b_ref[...], dims)

        def finish(r):
            if mode == "relu2":
                outs[0][...] = r.astype(BF16)
                rr = jnp.maximum(r, 0.0)
                outs[1][...] = (rr * rr).astype(BF16)
            elif mode == "drelu2":
                outs[0][...] = (r * (2.0 * jnp.maximum(u_ref[...].astype(F32), 0.0))).astype(out_dtype)
            else:
                outs[0][...] = r.astype(out_dtype)

        if nk == 1:
            finish(part)
        else:
            acc = rest[n_out]
            kk = pl.program_id(2)

            @pl.when(kk == 0)
            def _():
                acc[...] = part

            @pl.when(kk > 0)
            def _():
                acc[...] += part

            @pl.when(kk == nk - 1)
            def _():
                finish(acc[...])

    in_specs = [a_spec, b_spec]
    args = [a, b]
    if mode == "drelu2":
        in_specs.append(o_spec)
        args.append(u)
    if mode == "relu2":
        out_shape = [jax.ShapeDtypeStruct((m, n), BF16), jax.ShapeDtypeStruct((m, n), BF16)]
    else:
        out_shape = [jax.ShapeDtypeStruct((m, n), out_dtype)]
    outs, comm_outs = _pcall(
        body, args, name=name, grid=(m // tm, n // tn, nk), in_specs=in_specs, out_specs=[o_spec] * n_out,
        out_shape=out_shape, scratch_shapes=[pltpu.VMEM((tm, tn), F32)] if nk > 1 else [],
        sem=("parallel", "parallel", "arbitrary"), comm=comm)
    res = tuple(outs) if mode == "relu2" else outs[0]
    return res if comm is None else (res, comm_outs)


def _norm_mod_fwd(x, nw, scale, shift, name, res=None, gate=None):
    s, d = x.shape
    row = pl.BlockSpec((ROW_TILE, d), lambda i: (i, 0))
    vec = pl.BlockSpec((1, d), lambda i: (0, 0))
    with_res = res is not None

    def body(*refs):
        if with_res:
            x_ref, res_ref, gate_ref, nw_ref, sc_ref, sh_ref, x1_ref, h_ref = refs
            xv = x_ref[...] + gate_ref[...] * res_ref[...]
            x1_ref[...] = xv
        else:
            x_ref, nw_ref, sc_ref, sh_ref, h_ref = refs
            xv = x_ref[...]
        r = lax.rsqrt(jnp.mean(xv * xv, axis=-1, keepdims=True) + EPS)
        h_ref[...] = ((xv * r) * nw_ref[...] * (1.0 + sc_ref[...]) + sh_ref[...]).astype(BF16)

    if with_res:
        in_specs = [row, row, vec, vec, vec, vec]
        args = (x, res, gate, nw, scale, shift)
        out_shape = (jax.ShapeDtypeStruct((s, d), F32), jax.ShapeDtypeStruct((s, d), BF16))
        out_specs = (row, row)
    else:
        in_specs = [row, vec, vec, vec]
        args = (x, nw, scale, shift)
        out_shape = jax.ShapeDtypeStruct((s, d), BF16)
        out_specs = row
    return pl.pallas_call(body, name=name, grid=(s // ROW_TILE,), in_specs=in_specs, out_specs=out_specs,
                          out_shape=out_shape, compiler_params=_cparams(("parallel",)))(*args)


def _norm_mod_bwd(dh, xin, dres, nw, scale, name, gate=None, mix=None):
    s, d = xin.shape
    row = pl.BlockSpec((ROW_TILE, d), lambda i: (i, 0))
    vec = pl.BlockSpec((1, d), lambda i: (0, 0))
    with_gate = gate is not None

    def body(*refs):
        if with_gate:
            dh_ref, x_ref, dres_ref, nw_ref, sc_ref, gate_ref, mix_ref, dx_ref, dsh_ref, dsc_ref, dnw_ref, dmix_ref, dg_ref = refs
        else:
            dh_ref, x_ref, dres_ref, nw_ref, sc_ref, dx_ref, dsh_ref, dsc_ref, dnw_ref = refs
        i = pl.program_id(0)

        @pl.when(i == 0)
        def _():
            dsh_ref[...] = jnp.zeros_like(dsh_ref)
            dsc_ref[...] = jnp.zeros_like(dsc_ref)
            dnw_ref[...] = jnp.zeros_like(dnw_ref)
            if with_gate:
                dg_ref[...] = jnp.zeros_like(dg_ref)

        xv = x_ref[...]
        dhv = dh_ref[...]
        r = lax.rsqrt(jnp.mean(xv * xv, axis=-1, keepdims=True) + EPS)
        nrm = xv * r
        one_sc = 1.0 + sc_ref[...]
        dhn = dhv * nrm
        dsh_ref[...] += jnp.sum(dhv, axis=0, keepdims=True)
        dsc_ref[...] += jnp.sum(dhn, axis=0, keepdims=True) * nw_ref[...]
        dnw_ref[...] += jnp.sum(dhn, axis=0, keepdims=True) * one_sc
        dn = dhv * (nw_ref[...] * one_sc)
        dx = dres_ref[...] + r * (dn - nrm * jnp.mean(dn * nrm, axis=-1, keepdims=True))
        dx_ref[...] = dx
        if with_gate:
            dmix_ref[...] = (gate_ref[...] * dx).astype(BF16)
            dg_ref[...] += jnp.sum(dx * mix_ref[...], axis=0, keepdims=True)

    vshape = jax.ShapeDtypeStruct((1, d), F32)
    in_specs = [row, row, row, vec, vec]
    args = [dh, xin, dres, nw, scale]
    out_shape = [jax.ShapeDtypeStruct((s, d), F32), vshape, vshape, vshape]
    out_specs = [row, vec, vec, vec]
    if with_gate:
        in_specs += [vec, row]
        args += [gate, mix]
        out_shape += [jax.ShapeDtypeStruct((s, d), BF16), vshape]
        out_specs += [row, vec]
    return pl.pallas_call(body, name=name, grid=(s // ROW_TILE,), in_specs=in_specs, out_specs=out_specs,
                          out_shape=out_shape, compiler_params=_cparams(("arbitrary",)))(*args)


def _loss_head(x1, ff, gate2, tgt):
    s, d = x1.shape
    row = pl.BlockSpec((ROW_TILE, d), lambda i: (i, 0))
    vec = pl.BlockSpec((1, d), lambda i: (0, 0))
    one = pl.BlockSpec((1, 1), lambda i: (0, 0))

    def body(x1_ref, ff_ref, g_ref, t_ref, loss_ref, dout_ref, dff_ref, dg_ref):
        i = pl.program_id(0)

        @pl.when(i == 0)
        def _():
            loss_ref[...] = jnp.zeros_like(loss_ref)
            dg_ref[...] = jnp.zeros_like(dg_ref)

        ffv = ff_ref[...]
        err = x1_ref[...] + g_ref[...] * ffv - t_ref[...]
        loss_ref[...] += (0.5 / d) * jnp.sum(err * err).reshape(1, 1)
        dout = err * (1.0 / d)
        dout_ref[...] = dout
        dff_ref[...] = (g_ref[...] * dout).astype(BF16)
        dg_ref[...] += jnp.sum(dout * ffv, axis=0, keepdims=True)

    return pl.pallas_call(
        body, name="loss_head", grid=(s // ROW_TILE,), in_specs=[row, row, vec, row],
        out_specs=[one, row, row, vec],
        out_shape=[jax.ShapeDtypeStruct((1, 1), F32), jax.ShapeDtypeStruct((s, d), F32),
                   jax.ShapeDtypeStruct((s, d), BF16), jax.ShapeDtypeStruct((1, d), F32)],
        compiler_params=_cparams(("arbitrary",)))(x1, ff, gate2, tgt)


CONV_COLS = 256
HALO = 8


def _shift_down(cur, halo, k):
    if k == 0:
        return cur
    rolled = pltpu.roll(cur, k, axis=0)
    top = jnp.where(lax.broadcasted_iota(jnp.int32, halo.shape, 0) < k, pltpu.roll(halo, k, axis=0), rolled[:HALO])
    return jnp.concatenate([top, rolled[HALO:]], axis=0)


def _shift_up(cur, halo, k):
    if k == 0:
        return cur
    t = cur.shape[0]
    rolled = pltpu.roll(cur, t - k, axis=0)
    bot = jnp.where(lax.broadcasted_iota(jnp.int32, halo.shape, 0) >= HALO - k, pltpu.roll(halo, HALO - k, axis=0),
                    rolled[t - HALO:])
    return jnp.concatenate([rolled[:t - HALO], bot], axis=0)


def _conv_fwd(proj, conv_w, conv_b):
    s = proj.shape[0]
    nr = s // ROW_TILE
    cb0 = OFF_XBC // CONV_COLS
    hb = ROW_TILE // HALO
    cur = pl.BlockSpec((ROW_TILE, CONV_COLS), lambda j, r: (r, cb0 + j))
    prev = pl.BlockSpec((HALO, CONV_COLS), lambda j, r: (jnp.maximum(r * hb - 1, 0), cb0 + j))
    out = pl.BlockSpec((ROW_TILE, CONV_COLS), lambda j, r: (r, j))

    def body(u_ref, up_ref, w_ref, b_ref, pre_ref, act_ref):
        r = pl.program_id(1)
        u = u_ref[...]
        halo = jnp.where(r > 0, up_ref[...], 0.0)
        acc = b_ref[...] + w_ref[CONV_K - 1:CONV_K, :] * u
        for k in range(1, CONV_K):
            acc = acc + w_ref[CONV_K - 1 - k:CONV_K - k, :] * _shift_down(u, halo, k)
        pre_ref[...] = acc
        act_ref[...] = acc * _sigmoid(acc)

    return pl.pallas_call(
        body, name="conv_fwd", grid=(CONV_CH // CONV_COLS, nr),
        in_specs=[cur, prev, pl.BlockSpec((CONV_K, CONV_COLS), lambda j, r: (0, j)),
                  pl.BlockSpec((1, CONV_COLS), lambda j, r: (0, j))],
        out_specs=[out, out],
        out_shape=[jax.ShapeDtypeStruct((s, CONV_CH), F32), jax.ShapeDtypeStruct((s, CONV_CH), F32)],
        compiler_params=_cparams(("parallel", "arbitrary")))(proj, proj, conv_w, conv_b)


def _conv_bwd(dact, pre, proj, conv_w):
    s = proj.shape[0]
    nr = s // ROW_TILE
    cb0 = OFF_XBC // CONV_COLS
    hb = ROW_TILE // HALO
    last_halo = s // HALO - 1
    cur = pl.BlockSpec((ROW_TILE, CONV_COLS), lambda j, r: (r, j))
    nxt = pl.BlockSpec((HALO, CONV_COLS), lambda j, r: (jnp.minimum((r + 1) * hb, last_halo), j))
    ucur = pl.BlockSpec((ROW_TILE, CONV_COLS), lambda j, r: (r, cb0 + j))
    uprev = pl.BlockSpec((HALO, CONV_COLS), lambda j, r: (jnp.maximum(r * hb - 1, 0), cb0 + j))
    wspec = pl.BlockSpec((CONV_K, CONV_COLS), lambda j, r: (0, j))
    bspec = pl.BlockSpec((1, CONV_COLS), lambda j, r: (0, j))

    def dsilu(p):
        sg = _sigmoid(p)
        return sg * (1.0 + p * (1.0 - sg))

    def body(da_ref, dan_ref, pre_ref, pren_ref, u_ref, up_ref, w_ref, du_ref, dw_ref, db_ref):
        r = pl.program_id(1)

        @pl.when(r == 0)
        def _():
            dw_ref[...] = jnp.zeros_like(dw_ref)
            db_ref[...] = jnp.zeros_like(db_ref)

        dpre = da_ref[...] * dsilu(pre_ref[...])
        dnext = jnp.where(r < nr - 1, dan_ref[...] * dsilu(pren_ref[...]), 0.0)
        u = u_ref[...]
        halo = jnp.where(r > 0, up_ref[...], 0.0)
        du = w_ref[CONV_K - 1:CONV_K, :] * dpre
        dws = [jnp.sum(dpre * u, axis=0, keepdims=True)]
        for k in range(1, CONV_K):
            du = du + w_ref[CONV_K - 1 - k:CONV_K - k, :] * _shift_up(dpre, dnext, k)
            dws.append(jnp.sum(dpre * _shift_down(u, halo, k), axis=0, keepdims=True))
        du_ref[...] = du.astype(BF16)
        dw_ref[...] += jnp.concatenate(dws[::-1], axis=0)
        db_ref[...] += jnp.sum(dpre, axis=0, keepdims=True)

    return pl.pallas_call(
        body, name="conv_bwd", grid=(CONV_CH // CONV_COLS, nr),
        in_specs=[cur, nxt, cur, nxt, ucur, uprev, wspec],
        out_specs=[cur, wspec, bspec],
        out_shape=[jax.ShapeDtypeStruct((s, CONV_CH), BF16), jax.ShapeDtypeStruct((CONV_K, CONV_CH), F32),
                   jax.ShapeDtypeStruct((1, CONV_CH), F32)],
        compiler_params=_cparams(("parallel", "arbitrary")))(dact, dact, pre, pre, proj, proj, conv_w)


def _ssd_common(dtr, dtb, alog):
    lane = lax.broadcasted_iota(jnp.int32, (1, LANE), 1)
    head_lane = lane < SSD_HEADS
    dt = jnp.where(head_lane, _softplus(dtr + dtb), 0.0)
    a = jnp.where(head_lane, -jnp.exp(alog), 0.0)
    row = lax.broadcasted_iota(jnp.int32, (SSD_CHUNK, SSD_CHUNK), 0)
    col = lax.broadcasted_iota(jnp.int32, (SSD_CHUNK, SSD_CHUNK), 1)
    tril = row >= col
    cs = _dot(tril.astype(F32), dt * a, NN, precision=HIGHEST)
    return dt, a, cs, cs.T, tril, lane


def _split_bf16(v, passes):
    terms, rest = [], v
    for _ in range(passes):
        t = rest.astype(BF16)
        terms.append(t)
        rest = rest - t.astype(F32)
    return terms


def _dot_split(v, m, dims, passes):
    out = None
    for t in _split_bf16(v, passes):
        part = _dot(t, m, dims)
        out = part if out is None else out + part
    return out


def _ssd_constants():
    heads = jnp.arange(LANE)[:, None]
    exp_mat = (heads == (jnp.arange(SSD_D_INNER)[None, :] // HEAD_DIM)).astype(BF16)
    ind4 = ((jnp.arange(SSD_HEADS * SSD_CHUNK)[:, None] // SSD_CHUNK) == jnp.arange(LANE)[None, :]).astype(BF16)
    return exp_mat, ind4


def _expand_heads(v):
    return jnp.repeat(v[:, :SSD_HEADS], HEAD_DIM, axis=1)


def _ssd_prep(dtr, dtb, alog, exp_mat):
    dt, a, cs, cst, tril, lane = _ssd_common(dtr, dtb, alog)
    return dt, a, cs, cst, tril, lane, _dot_split(dt, exp_mat, NN, 2), _dot_split(cs, exp_mat, NN, 3)


def _chunk_decay_rows(cs, g):
    parts = []
    for e in range(HEADS_PER_GROUP):
        h = g * HEADS_PER_GROUP + e
        parts.append(jnp.broadcast_to(jnp.exp(cs[SSD_CHUNK - 1:SSD_CHUNK, h:h + 1]), (HEAD_DIM, SSD_STATE)))
    return jnp.concatenate(parts, axis=0)


def _ssd_fwd(proj, act, dtb, alog, dsk, nw):
    s = proj.shape[0]
    nc = s // SSD_CHUNK
    bc_w = SSD_GROUPS * SSD_STATE
    exp_mat, _ = _ssd_constants()

    def body(z_ref, dtr_ref, xs_ref, b_ref, c_ref, dtb_ref, alog_ref, dskx_ref, nw_ref, exp_ref,
             ypre_ref, yssd_ref, hall_ref, h_scr):
        @pl.when(pl.program_id(0) == 0)
        def _():
            h_scr[...] = jnp.zeros_like(h_scr)

        dt, a, cs, cst, tril, lane, dtx, csx = _ssd_prep(dtr_ref[...], dtb_ref[...], alog_ref[...], exp_ref[...])
        cs_last_x = csx[SSD_CHUNK - 1:SSD_CHUNK, :]
        xs = xs_ref[...]
        xdt = xs * dtx
        xdtb = xdt.astype(BF16)
        xdec = (xdt * jnp.exp(cs_last_x - csx)).astype(BF16)
        ecsx = jnp.exp(csx)
        head_of_lane = lax.broadcasted_iota(jnp.int32, (1, GROUP_WIDTH), 1) // HEAD_DIM
        for g in range(SSD_GROUPS):
            gs = slice(g * GROUP_WIDTH, (g + 1) * GROUP_WIDTH)
            bg = b_ref[:, g * SSD_STATE:(g + 1) * SSD_STATE].astype(BF16)
            cg = c_ref[:, g * SSD_STATE:(g + 1) * SSD_STATE].astype(BF16)
            cb = _dot(cg, bg, NT)
            hprev = h_scr[gs, :]
            hall_ref[0, gs, :] = hprev
            gms, rhs = [], []
            xg = xdtb[:, gs]
            for e in range(HEADS_PER_GROUP):
                h = g * HEADS_PER_GROUP + e
                lm = jnp.exp(jnp.where(tril, cs[:, h:h + 1] - cst[h:h + 1, :], -1e30))
                gms.append((cb * lm).astype(BF16))
                rhs.append(jnp.where(head_of_lane == e, xg, jnp.zeros_like(xg)))
            y = _dot(jnp.concatenate(gms, axis=1), jnp.concatenate(rhs, axis=0), NN)
            y = y + ecsx[:, gs] * _dot(cg, hprev.astype(BF16), NT)
            y = y + dskx_ref[:, gs] * xs[:, gs]
            h_scr[gs, :] = hprev * _chunk_decay_rows(cs, g) + _dot(xdec[:, gs], bg, TN)
            ypre_ref[:, gs] = y
            z = z_ref[:, gs]
            yg = y * (z * _sigmoid(z))
            r = lax.rsqrt(jnp.mean(yg * yg, axis=-1, keepdims=True) + EPS)
            yssd_ref[:, gs] = (yg * r * nw_ref[:, gs]).astype(BF16)

    row_d = lambda cb: pl.BlockSpec((SSD_CHUNK, SSD_D_INNER), lambda c: (c, cb))
    small = pl.BlockSpec((1, LANE), lambda c: (0, 0))
    wide = pl.BlockSpec((1, SSD_D_INNER), lambda c: (0, 0))
    return pl.pallas_call(
        body, name="ssd_fwd", grid=(nc,),
        in_specs=[row_d(OFF_Z // SSD_D_INNER),
                  pl.BlockSpec((SSD_CHUNK, LANE), lambda c: (c, OFF_DT // LANE)),
                  row_d(0),
                  pl.BlockSpec((SSD_CHUNK, bc_w), lambda c: (c, SSD_D_INNER // bc_w)),
                  pl.BlockSpec((SSD_CHUNK, bc_w), lambda c: (c, SSD_D_INNER // bc_w + 1)),
                  small, small, wide, wide, pl.BlockSpec((LANE, SSD_D_INNER), lambda c: (0, 0))],
        out_specs=[row_d(0), row_d(0), pl.BlockSpec((1, SSD_D_INNER, SSD_STATE), lambda c: (c, 0, 0))],
        out_shape=[jax.ShapeDtypeStruct((s, SSD_D_INNER), F32), jax.ShapeDtypeStruct((s, SSD_D_INNER), BF16),
                   jax.ShapeDtypeStruct((nc, SSD_D_INNER, SSD_STATE), F32)],
        scratch_shapes=[pltpu.VMEM((SSD_D_INNER, SSD_STATE), F32)],
        compiler_params=_cparams(("arbitrary",)))(proj, proj, act, act, act, dtb, alog, _expand_heads(dsk), nw, exp_mat)


def _ssd_bwd(dycat, ypre, proj, act, hall, dtb, alog, dsk, nw, comm=None):
    s = proj.shape[0]
    nc = s // SSD_CHUNK
    bc_w = SSD_GROUPS * SSD_STATE

    exp_mat, ind4 = _ssd_constants()
    seg_passes = 2

    def body(dy_ref, ypre_ref, z_ref, dtr_ref, xs_ref, b_ref, c_ref, hall_ref, dtb_ref, alog_ref, dskx_ref, nw_ref,
             exp_ref, ind4_ref, dz_ref, dact_ref, ddtr_ref, da_ref, ddsk_ref, ddtb_ref, dnw_ref, dh_scr):
        @pl.when(pl.program_id(0) == 0)
        def _():
            dh_scr[...] = jnp.zeros_like(dh_scr)
            da_ref[...] = jnp.zeros_like(da_ref)
            ddsk_ref[...] = jnp.zeros_like(ddsk_ref)
            ddtb_ref[...] = jnp.zeros_like(ddtb_ref)
            dnw_ref[...] = jnp.zeros_like(dnw_ref)

        dtr = dtr_ref[...]
        dt, a, cs, cst, tril, lane, dtx, csx = _ssd_prep(dtr, dtb_ref[...], alog_ref[...], exp_ref[...])
        cs_last_x = csx[SSD_CHUNK - 1:SSD_CHUNK, :]
        xs = xs_ref[...]
        xdt = xs * dtx
        xdtb = xdt.astype(BF16)
        decx = jnp.exp(cs_last_x - csx)
        xdecf = xdt * decx
        xdec = xdecf.astype(BF16)
        ecsx = jnp.exp(csx)
        head_of_lane = lax.broadcasted_iota(jnp.int32, (1, GROUP_WIDTH), 1) // HEAD_DIM
        last_row = lax.broadcasted_iota(jnp.int32, (SSD_CHUNK, 1), 0) == SSD_CHUNK - 1
        dcs_col = jnp.zeros((SSD_CHUNK, LANE), F32)
        dcs_row = jnp.zeros((SSD_CHUNK, LANE), F32)
        ddt = jnp.zeros((SSD_CHUNK, LANE), F32)
        ddsk = jnp.zeros((1, LANE), F32)
        hsum = jnp.zeros((1, LANE), F32)
        t1_sum = jnp.zeros((1, LANE), F32)
        for g in range(SSD_GROUPS):
            gs = slice(g * GROUP_WIDTH, (g + 1) * GROUP_WIDTH)
            bsl = slice(g * SSD_STATE, (g + 1) * SSD_STATE)
            exp_g = exp_ref[:, gs]
            ind4_g = ind4_ref[g * HEADS_PER_GROUP * SSD_CHUNK:(g + 1) * HEADS_PER_GROUP * SSD_CHUNK, :]
            z = z_ref[:, gs]
            sg = _sigmoid(z)
            sz = z * sg
            ypre = ypre_ref[:, gs]
            yg = ypre * sz
            r = lax.rsqrt(jnp.mean(yg * yg, axis=-1, keepdims=True) + EPS)
            nrm = yg * r
            dyo_n = dy_ref[:, gs]
            dnw_ref[:, gs] += jnp.sum(dyo_n * nrm, axis=0, keepdims=True)
            dn = dyo_n * nw_ref[:, gs]
            dyg = r * (dn - nrm * jnp.mean(dn * nrm, axis=-1, keepdims=True))
            dz_ref[:, gs] = (dyg * ypre * (sg * (1.0 + z * (1.0 - sg)))).astype(BF16)
            dy = dyg * sz

            bg = b_ref[:, bsl].astype(BF16)
            cg = c_ref[:, bsl].astype(BF16)
            cb = _dot(cg, bg, NT)
            hprev = hall_ref[0, gs, :]
            hb = hprev.astype(BF16)
            dhn = dh_scr[gs, :]
            dhb = dhn.astype(BF16)
            xs_g, xdt_g = xs[:, gs], xdtb[:, gs]
            w_off = _dot(cg, hb, NT)
            dyo = dy * ecsx[:, gs]
            dyob = dyo.astype(BF16)
            dcg = _dot(dyob, hb, NN)
            dh_y = _dot(dyob, cg, TN)
            r_st = _dot(bg, dhb, NT)
            dbg = _dot(xdec[:, gs], dhb, NN)
            dyb = dy.astype(BF16)
            gms, gmbs, lms, dys = [], [], [], []
            for e in range(HEADS_PER_GROUP):
                h = g * HEADS_PER_GROUP + e
                lm = jnp.exp(jnp.where(tril, cs[:, h:h + 1] - cst[h:h + 1, :], -1e30))
                gm = cb * lm
                lms.append(lm)
                gms.append(gm)
                gmbs.append(gm.astype(BF16))
                dys.append(jnp.where(head_of_lane == e, dyb, jnp.zeros_like(dyb)))
            dxdt = _dot(jnp.concatenate(gmbs, axis=0), jnp.concatenate(dys, axis=0), TN) + decx[:, gs] * r_st
            dcb = jnp.zeros((SSD_CHUNK, SSD_CHUNK), F32)
            mms = []
            for e in range(HEADS_PER_GROUP):
                dg = _dot(dys[e], xdt_g, NT)
                mms.append(dg * gms[e])
                dcb = dcb + dg * lms[e]
            seg = _dot_split(jnp.concatenate([dyo * w_off, xdecf[:, gs] * r_st, dxdt * xs_g, dy * xs_g], axis=0), exp_g, NT, seg_passes)
            v1, t1, ddt_g, dsk_g = [seg[i * SSD_CHUNK:(i + 1) * SSD_CHUNK] for i in range(4)]
            dcs_col = dcs_col + v1 - t1 + _dot_split(jnp.concatenate(mms, axis=1), ind4_g, NN, seg_passes)
            for t in _split_bf16(jnp.concatenate(mms, axis=0), seg_passes):
                dcs_row = dcs_row + _dot(ind4_g, t, TN)
            ddt = ddt + ddt_g
            ddsk = ddsk + jnp.sum(dsk_g, axis=0, keepdims=True)
            t1_sum = t1_sum + jnp.sum(t1, axis=0, keepdims=True)
            for e in range(HEADS_PER_GROUP):
                h = g * HEADS_PER_GROUP + e
                hs = slice(e * HEAD_DIM, (e + 1) * HEAD_DIM)
                hsum = hsum + jnp.where(lane == h, jnp.sum(dhn[hs, :] * hprev[hs, :]).reshape(1, 1), 0.0)
            dh_scr[gs, :] = dhn * _chunk_decay_rows(cs, g) + dh_y
            dcbb = dcb.astype(BF16)
            dact_ref[:, gs] = dxdt * dtx[:, gs] + dskx_ref[:, gs] * dy
            dact_ref[:, SSD_D_INNER + g * SSD_STATE:SSD_D_INNER + (g + 1) * SSD_STATE] = dbg + _dot(dcbb, cg, TN)
            dact_ref[:, SSD_D_INNER + bc_w + g * SSD_STATE:SSD_D_INNER + bc_w + (g + 1) * SSD_STATE] = dcg + _dot(dcbb, bg, NN)
        dlast = t1_sum + jnp.exp(cs[SSD_CHUNK - 1:SSD_CHUNK, :]) * hsum
        dcs = dcs_col - dcs_row.T + jnp.where(last_row, dlast, 0.0)
        row = lax.broadcasted_iota(jnp.int32, (SSD_CHUNK, SSD_CHUNK), 0)
        col = lax.broadcasted_iota(jnp.int32, (SSD_CHUNK, SSD_CHUNK), 1)
        dda = _dot((col >= row).astype(F32), dcs, NN, precision=HIGHEST)
        ddt = ddt + dda * a
        da_ref[...] += jnp.sum(dda * dt, axis=0, keepdims=True)
        ddtr = jnp.where(lane < SSD_HEADS, ddt * _sigmoid(dtr + dtb_ref[...]), 0.0)
        ddtr_ref[...] = ddtr.astype(BF16)
        ddtb_ref[...] += jnp.sum(ddtr, axis=0, keepdims=True)
        ddsk_ref[...] += ddsk

    rev = lambda c: nc - 1 - c
    row_d = lambda cb: pl.BlockSpec((SSD_CHUNK, SSD_D_INNER), lambda c: (rev(c), cb))
    small = pl.BlockSpec((1, LANE), lambda c: (0, 0))
    wide = pl.BlockSpec((1, SSD_D_INNER), lambda c: (0, 0))
    small_shape = jax.ShapeDtypeStruct((1, LANE), F32)
    return _pcall(
        body, (dycat, ypre, proj, proj, act, act, act, hall, dtb, alog, _expand_heads(dsk), nw, exp_mat, ind4),
        name="ssd_bwd", grid=(nc,),
        in_specs=[row_d(0), row_d(0), row_d(OFF_Z // SSD_D_INNER),
                  pl.BlockSpec((SSD_CHUNK, LANE), lambda c: (rev(c), OFF_DT // LANE)),
                  row_d(0),
                  pl.BlockSpec((SSD_CHUNK, bc_w), lambda c: (rev(c), SSD_D_INNER // bc_w)),
                  pl.BlockSpec((SSD_CHUNK, bc_w), lambda c: (rev(c), SSD_D_INNER // bc_w + 1)),
                  pl.BlockSpec((1, SSD_D_INNER, SSD_STATE), lambda c: (rev(c), 0, 0)),
                  small, small, wide, wide, pl.BlockSpec((LANE, SSD_D_INNER), lambda c: (0, 0)),
                  pl.BlockSpec((SSD_HEADS * SSD_CHUNK, LANE), lambda c: (0, 0))],
        out_specs=[row_d(0), pl.BlockSpec((SSD_CHUNK, CONV_CH), lambda c: (rev(c), 0)),
                   pl.BlockSpec((SSD_CHUNK, LANE), lambda c: (rev(c), 0)), small, small, small, wide],
        out_shape=[jax.ShapeDtypeStruct((s, SSD_D_INNER), BF16), jax.ShapeDtypeStruct((s, CONV_CH), F32),
                   jax.ShapeDtypeStruct((s, LANE), BF16), small_shape, small_shape, small_shape,
                   jax.ShapeDtypeStruct((1, SSD_D_INNER), F32)],
        scratch_shapes=[pltpu.VMEM((SSD_D_INNER, SSD_STATE), F32)], sem=("arbitrary",), comm=comm)


def _head_mean_matrix():
    row = lax.broadcasted_iota(jnp.int32, (LANE, LANE), 0) // HEAD_DIM
    col = lax.broadcasted_iota(jnp.int32, (LANE, LANE), 1) // HEAD_DIM
    return (row == col).astype(F32)


def _head_sum2(v, ones_bd):
    hi = v.astype(BF16)
    lo = (v - hi.astype(F32)).astype(BF16)
    return _dot(hi, ones_bd, NN) + _dot(lo, ones_bd, NN)


def _head_norm(x, w, scale, ones_bd):
    ms = _head_sum2(x * x, ones_bd) * (1.0 / HEAD_DIM)
    return (x * lax.rsqrt(ms + EPS)) * (w * scale)


PRO_ROWS = 256
ATT_UNROLL = 4
KEYS = 2 * ATT_BLK
NEG = -1e30
HALF = HEAD_DIM // 2


def _rows(start, size, dil):
    return pl.ds(start, size) if dil == 1 else pl.ds(start, size, stride=dil)


def _fill_bias(bias_ref):
    row = lax.broadcasted_iota(jnp.int32, (ATT_BLK, 2 * KEYS), 0)
    col = lax.broadcasted_iota(jnp.int32, (ATT_BLK, 2 * KEYS), 1) & (KEYS - 1)
    for first, off in ((0, 0), (1, ATT_BLK)):
        dist = off + row - col
        bias_ref[first] = jnp.where((dist >= 0) & (dist <= ATT_BLK), 0.0, NEG)


def _pair(a, b):
    return jnp.concatenate([jnp.broadcast_to(a, (ATT_BLK, KEYS)), jnp.broadcast_to(b, (ATT_BLK, KEYS))], axis=1)


def _split_heads(x, is_a):
    zero = jnp.zeros_like(x)
    return jnp.concatenate([jnp.where(is_a, x, zero), jnp.where(is_a, zero, x)], axis=0)


def _block_ids(b, nb):
    i = b & (nb - 1)
    q0 = pl.multiple_of(b * ATT_BLK, ATT_BLK)
    k0 = pl.multiple_of((b - jnp.minimum(i, 1)) * ATT_BLK, ATT_BLK)
    return pl.ds(q0, ATT_BLK), pl.ds(k0, KEYS), jnp.minimum(i, 1)


def _att_fwd(proj, qw, kw, comm=None):
    s = proj.shape[0]
    nblk = s // ATT_BLK
    assert all((s // d) // ATT_BLK >= 2 for d in DILATIONS)
    blk = lambda off: pl.BlockSpec((s, LANE), lambda i: (0, off // LANE + i))
    wspec = pl.BlockSpec((1, LANE), lambda i: (0, i))
    oblk = pl.BlockSpec((s, LANE), lambda i: (0, i))

    def body(q_ref, k_ref, v_ref, qw_ref, kw_ref, o_ref, lse_ref, qn, kn, q_cm, k_cm, v_cm, m_acc, l_acc, o_d, m_d, l_d, bias):
        ones_bd = _head_mean_matrix().astype(BF16)
        is_a = lax.broadcasted_iota(jnp.int32, (1, LANE), 1) < HEAD_DIM
        ones_ext = _split_heads(jnp.ones((KEYS, LANE), BF16), is_a)
        _fill_bias(bias)

        def pro(j, c):
            rows = pl.ds(pl.multiple_of(j * PRO_ROWS, PRO_ROWS), PRO_ROWS)
            qn[rows, :] = _head_norm(q_ref[rows, :], qw_ref[...], HEAD_DIM ** -0.5, ones_bd)
            kn[rows, :] = _head_norm(k_ref[rows, :], kw_ref[...], 1.0, ones_bd)
            return c

        lax.fori_loop(0, s // PRO_ROWS, pro, 0)

        for dil in DILATIONS:
            ln = s // dil
            nb = ln // ATT_BLK
            o_out, m_out, l_out = (o_ref, m_acc, l_acc) if dil == 1 else (o_d, m_d, l_d)
            for r in range(dil):
                def relayout(j, c, dil=dil, r=r, ln=ln):
                    j0 = pl.multiple_of(j * PRO_ROWS, PRO_ROWS)
                    src = _rows(r + dil * j0, PRO_ROWS, dil)
                    dst = pl.ds(r * ln + j0, PRO_ROWS)
                    q_cm[dst, :] = qn[src, :].astype(BF16)
                    k_cm[dst, :] = kn[src, :].astype(BF16)
                    v_cm[dst, :] = v_ref[src, :].astype(BF16)
                    return c

                lax.fori_loop(0, ln // PRO_ROWS, relayout, 0)

            def step(b, c, nb=nb, o_out=o_out, m_out=m_out, l_out=l_out):
                qrows, krows, first = _block_ids(b, nb)
                kb = _split_heads(k_cm[krows, :], is_a)
                vb = jnp.concatenate([_split_heads(v_cm[krows, :], is_a), ones_ext], axis=1)
                sc = _dot(q_cm[qrows, :], kb, NT) + bias[first]
                m_a = jnp.max(sc[:, :KEYS], axis=-1, keepdims=True)
                m_b = jnp.max(sc[:, KEYS:], axis=-1, keepdims=True)
                p = jnp.exp(sc - _pair(m_a, m_b)).astype(BF16)
                ol = _dot(p, vb, NN)
                o_out[qrows, :] = ol[:, :LANE]
                l_out[qrows, :] = ol[:, LANE:]
                m_out[qrows, :] = jnp.where(is_a, m_a, m_b)
                return c

            lax.fori_loop(0, nblk, step, 0, unroll=ATT_UNROLL)

            if dil > 1:
                for r in range(dil):
                    def merge(j, c, dil=dil, r=r, ln=ln):
                        j0 = pl.multiple_of(j * PRO_ROWS, PRO_ROWS)
                        nat = _rows(r + dil * j0, PRO_ROWS, dil)
                        cm = pl.ds(r * ln + j0, PRO_ROWS)
                        m_old, m_new = m_acc[nat, :], m_d[cm, :]
                        m = jnp.maximum(m_old, m_new)
                        a_old, a_new = jnp.exp(m_old - m), jnp.exp(m_new - m)
                        o_ref[nat, :] = a_old * o_ref[nat, :] + a_new * o_d[cm, :]
                        l_acc[nat, :] = a_old * l_acc[nat, :] + a_new * l_d[cm, :]
                        m_acc[nat, :] = m
                        return c

                    lax.fori_loop(0, ln // PRO_ROWS, merge, 0)

        def epi(j, c):
            rows = pl.ds(pl.multiple_of(j * PRO_ROWS, PRO_ROWS), PRO_ROWS)
            l = l_acc[rows, :]
            o_ref[rows, :] = o_ref[rows, :] / l
            lse_ref[rows, :] = m_acc[rows, :] + jnp.log(l)
            return c

        lax.fori_loop(0, s // PRO_ROWS, epi, 0)

    f = jax.ShapeDtypeStruct((s, ATT_D), F32)
    scr = pltpu.VMEM((s, LANE), F32)
    scb = pltpu.VMEM((s, LANE), BF16)
    return _pcall(
        body, (proj, proj, proj, qw, kw), name="att_fwd", grid=(ATT_D // LANE,),
        in_specs=[blk(OFF_Q), blk(OFF_K), blk(OFF_V), wspec, wspec], out_specs=[oblk, oblk], out_shape=[f, f],
        scratch_shapes=[scr, scr, scb, scb, scb, scr, scr, scr, scr, scr, pltpu.VMEM((2, ATT_BLK, 2 * KEYS), F32)],
        sem=("parallel",), comm=comm)


def _att_bwd(proj, do, stats, qw, kw, comm=None):
    s = proj.shape[0]
    nblk = s // ATT_BLK
    blk = lambda off: pl.BlockSpec((s, LANE), lambda i: (0, off // LANE + i))
    wspec = pl.BlockSpec((1, LANE), lambda i: (0, i))
    oblk = pl.BlockSpec((s, LANE), lambda i: (0, i))

    def body(q_ref, k_ref, v_ref, do_ref, st_ref, qw_ref, kw_ref, dq_ref, dk_ref, dv_ref, dqw_ref, dkw_ref,
             qn, kn, q_cm, do_cm, k_cm, v_cm, st_cm, dq_acc, dk_acc, dv_acc, dq_d, dk_d, dv_d, bias):
        ones_bd = _head_mean_matrix().astype(BF16)
        is_a = lax.broadcasted_iota(jnp.int32, (1, LANE), 1) < HEAD_DIM
        _fill_bias(bias)
        zero = jnp.zeros((PRO_ROWS, LANE), F32)

        def pro(j, c):
            rows = pl.ds(pl.multiple_of(j * PRO_ROWS, PRO_ROWS), PRO_ROWS)
            qn[rows, :] = _head_norm(q_ref[rows, :], qw_ref[...], HEAD_DIM ** -0.5, ones_bd)
            kn[rows, :] = _head_norm(k_ref[rows, :], kw_ref[...], 1.0, ones_bd)
            dk_acc[rows, :] = zero
            dv_acc[rows, :] = zero
            return c

        lax.fori_loop(0, s // PRO_ROWS, pro, 0)

        for dil in DILATIONS:
            ln = s // dil
            nb = ln // ATT_BLK
            dq_o, dk_o, dv_o = (dq_acc, dk_acc, dv_acc) if dil == 1 else (dq_d, dk_d, dv_d)
            for r in range(dil):
                def relayout(j, c, dil=dil, r=r, ln=ln):
                    j0 = pl.multiple_of(j * PRO_ROWS, PRO_ROWS)
                    src = _rows(r + dil * j0, PRO_ROWS, dil)
                    dst = pl.ds(r * ln + j0, PRO_ROWS)
                    q_cm[dst, :] = qn[src, :].astype(BF16)
                    k_cm[dst, :] = kn[src, :].astype(BF16)
                    v_cm[dst, :] = v_ref[src, :].astype(BF16)
                    do_cm[dst, :] = do_ref[src, :].astype(BF16)
                    st_cm[dst, :] = st_ref[src, :]
                    if dil > 1:
                        dk_d[dst, :] = zero
                        dv_d[dst, :] = zero
                    return c

                lax.fori_loop(0, ln // PRO_ROWS, relayout, 0)

            def step(b, c, nb=nb, dq_o=dq_o, dk_o=dk_o, dv_o=dv_o):
                qrows, krows, first = _block_ids(b, nb)
                qb = q_cm[qrows, :]
                dob = do_cm[qrows, :]
                kb = _split_heads(k_cm[krows, :], is_a)
                vb = _split_heads(v_cm[krows, :], is_a)
                st = st_cm[qrows, :]
                sc = _dot(qb, kb, NT) + bias[first]
                p = jnp.exp(sc - _pair(st[:, 0:1], st[:, HEAD_DIM:HEAD_DIM + 1]))
                dp = _dot(dob, vb, NT)
                ds = (p * (dp - _pair(st[:, HALF:HALF + 1], st[:, HEAD_DIM + HALF:HEAD_DIM + HALF + 1]))).astype(BF16)
                dq_o[qrows, :] = _dot(ds, kb, NN)
                dkf = _dot(ds, qb, TN)
                dvf = _dot(p.astype(BF16), dob, TN)
                dk_o[krows, :] += jnp.where(is_a, dkf[:KEYS], dkf[KEYS:])
                dv_o[krows, :] += jnp.where(is_a, dvf[:KEYS], dvf[KEYS:])
                return c

            lax.fori_loop(0, nblk, step, 0, unroll=ATT_UNROLL)

            if dil > 1:
                for r in range(dil):
                    def merge(j, c, dil=dil, r=r, ln=ln):
                        j0 = pl.multiple_of(j * PRO_ROWS, PRO_ROWS)
                        nat = _rows(r + dil * j0, PRO_ROWS, dil)
                        cm = pl.ds(r * ln + j0, PRO_ROWS)
                        dq_acc[nat, :] += dq_d[cm, :]
                        dk_acc[nat, :] += dk_d[cm, :]
                        dv_acc[nat, :] += dv_d[cm, :]
                        return c

                    lax.fori_loop(0, ln // PRO_ROWS, merge, 0)

        def back(dn_out, x, w, scale):
            r = lax.rsqrt(_head_sum2(x * x, ones_bd) * (1.0 / HEAD_DIM) + EPS)
            nrm = x * r
            dw = jnp.sum(dn_out * nrm, axis=0, keepdims=True) * scale
            dn = dn_out * (w * scale)
            return r * (dn - nrm * (_head_sum2(dn * nrm, ones_bd) * (1.0 / HEAD_DIM))), dw

        def epi(j, c):
            rows = pl.ds(pl.multiple_of(j * PRO_ROWS, PRO_ROWS), PRO_ROWS)
            dq, dqw = back(dq_acc[rows, :], q_ref[rows, :], qw_ref[...], HEAD_DIM ** -0.5)
            dk, dkw = back(dk_acc[rows, :], k_ref[rows, :], kw_ref[...], 1.0)
            dq_ref[rows, :] = dq.astype(BF16)
            dk_ref[rows, :] = dk.astype(BF16)
            dv_ref[rows, :] = dv_acc[rows, :].astype(BF16)
            return (c[0] + dqw, c[1] + dkw)

        zrow = jnp.zeros((1, LANE), F32)
        dqw, dkw = lax.fori_loop(0, s // PRO_ROWS, epi, (zrow, zrow))
        dqw_ref[...] = dqw
        dkw_ref[...] = dkw

    o = jax.ShapeDtypeStruct((s, ATT_D), BF16)
    ov = jax.ShapeDtypeStruct((1, ATT_D), F32)
    scr = pltpu.VMEM((s, LANE), F32)
    scb = pltpu.VMEM((s, LANE), BF16)
    return _pcall(
        body, (proj, proj, proj, do, stats, qw, kw), name="att_bwd", grid=(ATT_D // LANE,),
        in_specs=[blk(OFF_Q), blk(OFF_K), blk(OFF_V), oblk, oblk, wspec, wspec],
        out_specs=[oblk, oblk, oblk, wspec, wspec], out_shape=[o, o, o, ov, ov],
        scratch_shapes=[scr, scr, scb, scb, scb, scb, scr, scr, scr, scr, scr, scr, scr, pltpu.VMEM((2, ATT_BLK, 2 * KEYS), F32)],
        sem=("parallel",), comm=comm)


def _att_norm_fwd(o, nw):
    s = o.shape[0]
    row = pl.BlockSpec((ROW_TILE, ATT_D), lambda i: (i, 0))
    vec = pl.BlockSpec((1, ATT_D), lambda i: (0, 0))

    def body(o_ref, nw_ref, y_ref):
        o = o_ref[...]
        r = lax.rsqrt(jnp.mean(o * o, axis=-1, keepdims=True) + EPS)
        y_ref[...] = (o * r * nw_ref[...]).astype(BF16)

    return pl.pallas_call(body, name="att_norm_fwd", grid=(s // ROW_TILE,), in_specs=[row, vec], out_specs=row,
                          out_shape=jax.ShapeDtypeStruct((s, ATT_D), BF16), compiler_params=_cparams(("parallel",)))(o, nw)


def _att_norm_bwd(dycat, o, lse, nw):
    s = o.shape[0]
    row = pl.BlockSpec((ROW_TILE, ATT_D), lambda i: (i, 0))
    vec = pl.BlockSpec((1, ATT_D), lambda i: (0, 0))

    def body(dy_ref, o_ref, lse_ref, nw_ref, do_ref, st_ref, dnw_ref):
        @pl.when(pl.program_id(0) == 0)
        def _():
            dnw_ref[...] = jnp.zeros_like(dnw_ref)

        o = o_ref[...]
        dy = dy_ref[...]
        r = lax.rsqrt(jnp.mean(o * o, axis=-1, keepdims=True) + EPS)
        nrm = o * r
        dnw_ref[...] += jnp.sum(dy * nrm, axis=0, keepdims=True)
        dn = dy * nw_ref[...]
        do = r * (dn - nrm * jnp.mean(dn * nrm, axis=-1, keepdims=True))
        do_ref[...] = do
        ones_bd = _head_mean_matrix().astype(BF16)
        prod = do * o
        delta = jnp.concatenate([_head_sum2(prod[:, j * LANE:(j + 1) * LANE], ones_bd) for j in range(ATT_D // LANE)], axis=1)
        lane = lax.broadcasted_iota(jnp.int32, (1, ATT_D), 1)
        st_ref[...] = jnp.where((lane & (HEAD_DIM - 1)) < HALF, lse_ref[...], delta)

    f = jax.ShapeDtypeStruct((s, ATT_D), F32)
    return pl.pallas_call(
        body, name="att_norm_bwd", grid=(s // ROW_TILE,),
        in_specs=[pl.BlockSpec((ROW_TILE, ATT_D), lambda i: (i, 1)), row, row, vec], out_specs=[row, row, vec],
        out_shape=[f, f, jax.ShapeDtypeStruct((1, ATT_D), F32)],
        compiler_params=_cparams(("arbitrary",)))(dycat, o, lse, nw)


def _ada_fwd(c_all, w_ada):
    def body(c_ref, w_ref, o_ref):
        cv = c_ref[...]
        o_ref[...] = _dot((cv * _sigmoid(cv)).astype(BF16), w_ref[...].astype(BF16), NN)

    return pl.pallas_call(body, name="ada_fwd", out_shape=jax.ShapeDtypeStruct((c_all.shape[0], w_ada.shape[1]), F32),
                          compiler_params=_cparams())(c_all, w_ada)


def _adamw_math(g, w, m, v):
    m_new = ADAM_B1 * m + (1.0 - ADAM_B1) * g
    v_new = ADAM_B2 * v + (1.0 - ADAM_B2) * (g * g)
    m_hat = m_new / (1.0 - ADAM_B1 ** ADAM_STEP)
    v_hat = v_new / (1.0 - ADAM_B2 ** ADAM_STEP)
    delta = -ADAM_LR * (m_hat / (jnp.sqrt(v_hat) + ADAM_EPS) + ADAM_WD * w)
    return delta, m_new, v_new


def _ada_bwd_adamw(c_all, dmod_cols, w, m, v):
    rows, cols = w.shape
    tr = 256
    blk = pl.BlockSpec((tr, cols), lambda i: (i, 0))

    def body(c_ref, d_ref, w_ref, m_ref, v_ref, g_ref, dl_ref, mo_ref, vo_ref):
        cv = c_ref[...]
        ca = cv * _sigmoid(cv)
        g = ca[:, 0:1] * d_ref[0:1, :]
        for b in range(1, N_DEV):
            g = g + ca[:, b:b + 1] * d_ref[b:b + 1, :]
        g_ref[...] = g
        dl_ref[...], mo_ref[...], vo_ref[...] = _adamw_math(g, w_ref[...], m_ref[...], v_ref[...])

    o = jax.ShapeDtypeStruct((rows, cols), F32)
    return pl.pallas_call(
        body, name="ada_bwd_adamw", grid=(rows // tr,),
        in_specs=[pl.BlockSpec((tr, N_DEV), lambda i: (i, 0)), pl.BlockSpec((N_DEV, cols), lambda i: (0, 0)), blk, blk, blk],
        out_specs=[blk] * 4, out_shape=[o, o, o, o], compiler_params=_cparams(("parallel",)))(c_all.T, dmod_cols, w, m, v)


def _reduce_adamw(slabs, w, m, v, name):
    rows, cols = w.shape
    tr = 128
    blk = pl.BlockSpec((tr, cols), lambda i: (i, 0))

    def body(s_ref, w_ref, m_ref, v_ref, g_ref, dl_ref, mo_ref, vo_ref):
        g = s_ref[0].astype(F32)
        for dev in range(1, N_DEV):
            g = g + s_ref[dev].astype(F32)
        g_ref[...] = g
        dl_ref[...], mo_ref[...], vo_ref[...] = _adamw_math(g, w_ref[...], m_ref[...], v_ref[...])

    o = jax.ShapeDtypeStruct((rows, cols), F32)
    return pl.pallas_call(
        body, name=name, grid=(rows // tr,),
        in_specs=[pl.BlockSpec((N_DEV, tr, cols), lambda i: (0, i, 0)), blk, blk, blk],
        out_specs=[blk] * 4, out_shape=[o, o, o, o], compiler_params=_cparams(("parallel",)))(slabs, w, m, v)


def _small_reduce_adamw(gathered, w, m, v):
    def body(s_ref, w_ref, m_ref, v_ref, g_ref, dl_ref, mo_ref, vo_ref):
        g = s_ref[0]
        for dev in range(1, N_DEV):
            g = g + s_ref[dev]
        g_ref[...] = g
        dl_ref[...], mo_ref[...], vo_ref[...] = _adamw_math(g, w_ref[...], m_ref[...], v_ref[...])

    o = jax.ShapeDtypeStruct(w.shape, F32)
    return pl.pallas_call(body, name="small_reduce_adamw", out_shape=[o, o, o, o], compiler_params=_cparams())(gathered, w, m, v)


def _adamw_small(g, w, m, v, name):
    def body(g_ref, w_ref, m_ref, v_ref, dl_ref, mo_ref, vo_ref):
        dl_ref[...], mo_ref[...], vo_ref[...] = _adamw_math(g_ref[...], w_ref[...], m_ref[...], v_ref[...])

    o = jax.ShapeDtypeStruct(w.shape, F32)
    return pl.pallas_call(body, name=name, out_shape=[o, o, o], compiler_params=_cparams())(g, w, m, v)


class _Exchange:
    def __init__(self, arrs, scatter):
        self.arrs, self.scatter, self.n = list(arrs), scatter, len(arrs)
        hbm = pl.BlockSpec(memory_space=pltpu.HBM)
        self.in_specs = [hbm] * self.n
        self.out_specs = [hbm] * self.n
        self.out_shape = [jax.ShapeDtypeStruct(a.shape if scatter else (N_DEV,) + a.shape, a.dtype) for a in self.arrs]
        self.scratch = [pltpu.SemaphoreType.DMA((self.n * (N_DEV - 1),)), pltpu.SemaphoreType.DMA((self.n * (N_DEV - 1),)),
                        pltpu.SemaphoreType.DMA((self.n,))]

    def _local(self, ins, outs, sems):
        me = 4 * lax.axis_index("x") + 2 * lax.axis_index("y") + lax.axis_index("c")
        return [pltpu.make_async_copy(ins[a].at[me] if self.scatter else ins[a], outs[a].at[me], sems[2].at[a])
                for a in range(self.n)]

    def _remote(self, ins, outs, sems, arriving):
        send_sems, recv_sems, _ = sems
        x, y, c = lax.axis_index("x"), lax.axis_index("y"), lax.axis_index("c")
        me = 4 * x + 2 * y + c
        remote = []
        for a in range(self.n):
            for k in range(1, N_DEV):
                px = 1 - x if k & 4 else x
                py = 1 - y if k & 2 else y
                pc = 1 - c if k & 1 else c
                peer = 4 * px + 2 * py + pc
                sem = a * (N_DEV - 1) + k - 1
                remote.append(pltpu.make_async_remote_copy(
                    src_ref=ins[a].at[peer] if self.scatter else ins[a], dst_ref=outs[a].at[peer if arriving else me],
                    send_sem=send_sems.at[sem], recv_sem=recv_sems.at[sem], device_id=(px, py, pc), device_id_type=MESH_IDS))
        return remote

    def start(self, ins, outs, sems):
        for cp in self._local(ins, outs, sems) + self._remote(ins, outs, sems, arriving=False):
            cp.start()

    def wait(self, ins, outs, sems):
        for send, arrival in zip(self._remote(ins, outs, sems, arriving=False), self._remote(ins, outs, sems, arriving=True)):
            send.wait_send()
            arrival.wait_recv()
        for cp in self._local(ins, outs, sems):
            cp.wait()


def _split_comm_refs(refs, n_in, n_out, n_scr, comm):
    nc = comm.n if comm is not None else 0
    ns = 3 if comm is not None else 0
    pos, groups = 0, []
    for cnt in (n_in, nc, n_out, nc, n_scr, ns):
        groups.append(refs[pos:pos + cnt])
        pos += cnt
    assert pos == len(refs), (pos, len(refs))
    return groups


def _pcall(body, args, *, name, grid, in_specs, out_specs, out_shape, scratch_shapes=(), sem=None, comm=None):
    in_specs, out_specs, out_shape, scratch_shapes = list(in_specs), list(out_specs), list(out_shape), list(scratch_shapes)
    n_in, n_out, n_scr = len(in_specs), len(out_specs), len(scratch_shapes)
    if comm is None:
        kernel_body = body
    else:
        def kernel_body(*refs):
            ins, cins, outs, couts, scr, sems = _split_comm_refs(refs, n_in, n_out, n_scr, comm)
            ids = [pl.program_id(a) for a in range(len(grid))]
            first, last = ids[0] == 0, ids[0] == grid[0] - 1
            for a in range(1, len(grid)):
                first, last = first & (ids[a] == 0), last & (ids[a] == grid[a] - 1)

            @pl.when(first)
            def _():
                comm.start(cins, couts, sems)

            body(*ins, *outs, *scr)

            @pl.when(last)
            def _():
                comm.wait(cins, couts, sems)

        in_specs, out_specs, out_shape = in_specs + comm.in_specs, out_specs + comm.out_specs, out_shape + comm.out_shape
        scratch_shapes, args = scratch_shapes + comm.scratch, list(args) + comm.arrs
        sem = ("arbitrary",) * len(grid)
    res = pl.pallas_call(kernel_body, name=name, grid=grid, in_specs=in_specs, out_specs=out_specs, out_shape=out_shape,
                         scratch_shapes=scratch_shapes, compiler_params=_cparams(sem))(*args)
    return res[:n_out], res[n_out:]


def _exchange(arrs, name, scatter):
    ex = _Exchange(arrs, scatter)

    def body(*refs):
        _, ins, _, outs, _, sems = _split_comm_refs(refs, 0, 0, 0, ex)
        ex.start(ins, outs, sems)
        ex.wait(ins, outs, sems)

    return pl.pallas_call(body, name=name, in_specs=ex.in_specs, out_specs=ex.out_specs, out_shape=ex.out_shape,
                          scratch_shapes=ex.scratch)(*arrs)


def _pad_lanes(v, width=LANE):
    return jnp.pad(v, ((0, 0), (0, width - v.shape[1])))


def _shards_to_cols(g):
    return jnp.transpose(g, (1, 0, 2)).reshape(g.shape[1], N_DEV * g.shape[2])


def _cols_to_shards(w):
    return w.astype(BF16).reshape(w.shape[0], N_DEV, w.shape[1] // N_DEV).transpose(1, 0, 2)


def _local_step(x, tgt, mod, w_in_p, conv_w, conv_b, dt_bias, a_log, d_skip, ssd_norm_w, q_norm_w, k_norm_w,
                attn_norm_w, w_out_sh, w_ff1_sh, w_ff2_sh, norm1_w, norm2_w):
    shift1, scale1, gate1, shift2, scale2, gate2 = [mod[i:i + 1] for i in range(N_MOD)]
    dtb, alog, dsk = _pad_lanes(dt_bias), _pad_lanes(a_log), _pad_lanes(d_skip)
    qw, kw = jnp.tile(q_norm_w, (1, ATT_HEADS)), jnp.tile(k_norm_w, (1, ATT_HEADS))

    h1 = _norm_mod_fwd(x, norm1_w, scale1, shift1, "norm1_fwd")
    proj = _matmul(h1, w_in_p, tm=2048, tn=896, tk=1024, name="in_proj")
    pre, act = _conv_fwd(proj, conv_w, conv_b)
    ypre, y_ssd, hall = _ssd_fwd(proj, act, dtb, alog, dsk, ssd_norm_w)
    (o_att, lse), (w_out_g, w_ff1_g, w_ff2_g) = _att_fwd(proj, qw, kw, comm=_Exchange([w_out_sh, w_ff1_sh, w_ff2_sh], scatter=False))
    w_out = w_out_g.reshape(2 * D_MODEL, D_MODEL)
    w_ff1 = _shards_to_cols(w_ff1_g)
    w_ff2 = w_ff2_g.reshape(D_FF, D_MODEL)
    y_att = _att_norm_fwd(o_att, attn_norm_w)
    ycat = jnp.concatenate([y_ssd, y_att], axis=1)
    mix = _matmul(ycat, w_out, tm=1024, tn=1024, tk=2048, name="out_proj")
    x1, h2 = _norm_mod_fwd(x, norm2_w, scale2, shift2, "norm2_fwd", res=mix, gate=gate1)
    u, act_ff = _matmul(h2, w_ff1, tm=1024, tn=1024, tk=1024, name="ff1", mode="relu2")
    ff = _matmul(act_ff, w_ff2, tm=512, tn=1024, tk=4096, name="ff2")
    loss, dout, dff, dgate2 = _loss_head(x1, ff, gate2, tgt)

    du = _matmul(dff, w_ff2, tb=True, tm=1024, tn=1024, tk=1024, out_dtype=BF16, name="ff2_dx", mode="drelu2", u=u)
    g_ff2 = _matmul(act_ff, dff, ta=True, tm=512, tn=1024, tk=4096, out_dtype=BF16, name="ff2_dw")
    dh2 = _matmul(du, w_ff1, tb=True, tm=512, tn=1024, tk=4096, name="ff1_dx")
    g_ff1 = _matmul(h2, du, ta=True, tm=512, tn=1024, tk=4096, out_dtype=BF16, name="ff1_dw")
    dx1, dshift2, dscale2, g_norm2, dmix, dgate1 = _norm_mod_bwd(dh2, x1, dout, norm2_w, scale2, "norm2_bwd", gate=gate1, mix=mix)

    dycat = _matmul(dmix, w_out, tb=True, tm=1024, tn=1024, tk=1024, name="out_proj_dx")
    g_out = _matmul(ycat, dmix, ta=True, tm=512, tn=1024, tk=4096, out_dtype=BF16, name="out_proj_dw")
    do, stats, g_attn_norm = _att_norm_bwd(dycat, o_att, lse, attn_norm_w)
    ff_slabs = [_cols_to_shards(g_ff1), g_ff2.astype(BF16).reshape(N_DEV, D_FF // N_DEV, D_MODEL)]
    (dq, dk, dv, dqw, dkw), (s_ff1, s_ff2) = _att_bwd(proj, do, stats, qw, kw, comm=_Exchange(ff_slabs, scatter=True))
    out_slabs = [g_out.astype(BF16).reshape(N_DEV, 2 * D_MODEL // N_DEV, D_MODEL)]
    (dz, dact, ddtr, da, g_dsk, g_dtb, g_ssd_norm), (s_out,) = _ssd_bwd(
        dycat, ypre, proj, act, hall, dtb, alog, dsk, ssd_norm_w, comm=_Exchange(out_slabs, scatter=True))
    dxbc, g_conv_w, g_conv_b = _conv_bwd(dact, pre, proj, conv_w)
    dproj = jnp.concatenate([dz, dxbc, dq, dk, dv, ddtr], axis=1)
    g_in_p = _matmul(h1, dproj, ta=True, tm=512, tn=896, tk=4096, out_dtype=BF16, name="in_proj_dw")
    in_slabs = [_cols_to_shards(_unpack_w_in(g_in_p))]
    dh1, (s_in,) = _matmul(dproj, w_in_p, tb=True, tm=512, tn=1024, tk=IN_WP, name="in_proj_dx",
                           comm=_Exchange(in_slabs, scatter=True))
    grad_x, dshift1, dscale1, g_norm1 = _norm_mod_bwd(dh1, x, dx1, norm1_w, scale1, "norm1_bwd")

    dmod = jnp.concatenate([dshift1, dscale1, dgate1, dshift2, dscale2, dgate2], axis=0)
    g_alog = da[:, :SSD_HEADS] * (-jnp.exp(a_log))
    g_qw = dqw.reshape(ATT_HEADS, HEAD_DIM).sum(axis=0, keepdims=True)
    g_kw = dkw.reshape(ATT_HEADS, HEAD_DIM).sum(axis=0, keepdims=True)
    return dict(loss=loss, grad_x=grad_x, dmod=dmod, norm1_w=g_norm1, norm2_w=g_norm2, w_in=s_in, conv_w=g_conv_w,
                conv_b=g_conv_b, dt_bias=g_dtb[:, :SSD_HEADS], a_log=g_alog, d_skip=g_dsk[:, :SSD_HEADS],
                ssd_norm_w=g_ssd_norm, q_norm_w=g_qw, k_norm_w=g_kw, attn_norm_w=g_attn_norm, w_out=s_out,
                w_ff1=s_ff1, w_ff2=s_ff2)


def _pack_w_in(w_full):
    o_dt = SSD_D_INNER + CONV_CH
    o_q = o_dt + SSD_HEADS
    pad = jnp.zeros((w_full.shape[0], LANE - SSD_HEADS), w_full.dtype)
    return jnp.concatenate([w_full[:, :o_dt], w_full[:, o_q:], w_full[:, o_dt:o_q], pad], axis=1)


def _unpack_w_in(g_p):
    return jnp.concatenate([g_p[:, :OFF_Q], g_p[:, OFF_DT:OFF_DT + SSD_HEADS], g_p[:, OFF_Q:OFF_DT]], axis=1)


MISC_FIELDS = (("dt_bias", SSD_HEADS), ("a_log", SSD_HEADS), ("d_skip", SSD_HEADS), ("q_norm_w", HEAD_DIM), ("k_norm_w", HEAD_DIM))
SMALL_LAYOUT = (("b_ada", 6), ("norm1_w", 1), ("norm2_w", 1), ("conv_w", 8), ("conv_b", 2), ("ssd_norm_w", 1),
                ("attn_norm_w", 1), ("misc", 1))


def _pack_small(vals):
    rows = []
    for name, nrow in SMALL_LAYOUT:
        if name == "misc":
            misc = jnp.concatenate([vals[f].reshape(1, n) for f, n in MISC_FIELDS], axis=1)
            rows.append(_pad_lanes(misc, D_MODEL))
        elif name in vals:
            rows.append(vals[name].reshape(nrow, D_MODEL))
        else:
            rows.append(jnp.zeros((nrow, D_MODEL), F32))
    used = sum(n for _, n in SMALL_LAYOUT)
    rows.append(jnp.zeros((SMALL_ROWS - used, D_MODEL), F32))
    return jnp.concatenate(rows, axis=0)


def _unpack_small(packed):
    out, r = {}, 0
    for name, nrow in SMALL_LAYOUT:
        blk = packed[r:r + nrow]
        r += nrow
        if name == "misc":
            c0 = 0
            for f, n in MISC_FIELDS:
                out[f] = blk[:, c0:c0 + n]
                c0 += n
        elif name == "b_ada":
            out[name] = blk.reshape(1, N_MOD * D_MODEL)
        elif name == "conv_w":
            out[name] = blk.reshape(CONV_K, CONV_CH)
        elif name == "conv_b":
            out[name] = blk.reshape(1, CONV_CH)
        else:
            out[name] = blk
    return out


WEIGHT_NAMES = ("norm1_w", "norm2_w", "w_ada", "b_ada", "w_in", "conv_w", "conv_b", "dt_bias", "a_log", "d_skip",
                "ssd_norm_w", "q_norm_w", "k_norm_w", "attn_norm_w", "w_out", "w_ff1", "w_ff2")
SMALL_NAMES = ("norm1_w", "norm2_w", "b_ada", "conv_b", "dt_bias", "a_log", "d_skip", "ssd_norm_w", "q_norm_w",
               "k_norm_w", "attn_norm_w")


def kernel(x, c, norm1_w, norm2_w, w_ada, b_ada, w_in, conv_w, conv_b, dt_bias, a_log, d_skip, ssd_norm_w, q_norm_w, k_norm_w, attn_norm_w, w_out, w_ff1, w_ff2, loss_target, m_norm1_w, m_norm2_w, m_w_ada, m_b_ada, m_w_in, m_conv_w, m_conv_b, m_dt_bias, m_a_log, m_d_skip, m_ssd_norm_w, m_q_norm_w, m_k_norm_w, m_attn_norm_w, m_w_out, m_w_ff1, m_w_ff2, v_norm1_w, v_norm2_w, v_w_ada, v_b_ada, v_w_in, v_conv_w, v_conv_b, v_dt_bias, v_a_log, v_d_skip, v_ssd_norm_w, v_q_norm_w, v_k_norm_w, v_attn_norm_w, v_w_out, v_w_ff1, v_w_ff2):
    args = dict(locals())
    w = {n: args[n] for n in WEIGHT_NAMES}
    m = {n: args["m_" + n] for n in WEIGHT_NAMES}
    v = {n: args["v_" + n] for n in WEIGHT_NAMES}
    me = 4 * lax.axis_index("x") + 2 * lax.axis_index("y") + lax.axis_index("c")

    c_rows = jnp.pad(c, ((0, 7), (0, 0)))
    c_g, conv_g, w_in_g = _exchange([c_rows, w["conv_w"][0], w["w_in"][0].astype(BF16)], "gather_w_in", scatter=False)
    c_all = c_g[:, 0, :]
    conv_full = _shards_to_cols(conv_g)
    w_in_p = _pack_w_in(_shards_to_cols(w_in_g))

    mod_part = _ada_fwd(c_all, w["w_ada"][0])
    (mod_g,) = _exchange([mod_part], "gather_mod", scatter=False)
    mod_mine = lax.dynamic_index_in_dim(mod_g, me, axis=1, keepdims=False).reshape(1, N_MOD * D_MODEL) + w["b_ada"]
    mod = mod_mine.reshape(N_MOD, D_MODEL)

    res = _local_step(x[0], loss_target[0], mod, w_in_p, conv_full, w["conv_b"], w["dt_bias"], w["a_log"], w["d_skip"],
                      w["ssd_norm_w"], w["q_norm_w"], w["k_norm_w"], w["attn_norm_w"], w["w_out"][0].astype(BF16),
                      w["w_ff1"][0].astype(BF16), w["w_ff2"][0].astype(BF16), w["norm1_w"], w["norm2_w"])

    small_vals = {n: res[n] for n in SMALL_NAMES if n != "b_ada"}
    small_vals["b_ada"] = res["dmod"]
    small_vals["conv_w"] = res["conv_w"]
    (small_g,) = _exchange([_pack_small(small_vals)], "gather_small", scatter=False)

    grads, delta, new_m, new_v = {}, {}, {}, {}
    for name in ("w_in", "w_out", "w_ff1", "w_ff2"):
        outs = _reduce_adamw(res[name], w[name][0], m[name][0], v[name][0], "adamw_" + name)
        grads[name], delta[name], new_m[name], new_v[name] = [o[None] for o in outs]

    sm = _small_reduce_adamw(small_g, _pack_small({n: w[n] for n in SMALL_NAMES}), _pack_small({n: m[n] for n in SMALL_NAMES}),
                             _pack_small({n: v[n] for n in SMALL_NAMES}))
    sm = [_unpack_small(p) for p in sm]
    for n in SMALL_NAMES:
        grads[n], delta[n], new_m[n], new_v[n] = [p[n] for p in sm]
    shard_w = CONV_CH // N_DEV
    g_conv = lax.dynamic_slice_in_dim(sm[0]["conv_w"], me * shard_w, shard_w, axis=1)
    cw = _adamw_small(g_conv, w["conv_w"][0], m["conv_w"][0], v["conv_w"][0], "adamw_conv_w")
    grads["conv_w"] = g_conv[None]
    delta["conv_w"], new_m["conv_w"], new_v["conv_w"] = [o[None] for o in cw]

    ada_w = w_ada.shape[2]
    dmod_all = small_g[:, :N_MOD, :].reshape(N_DEV, N_MOD * D_MODEL)
    dmod_cols = lax.dynamic_slice_in_dim(dmod_all, me * ada_w, ada_w, axis=1)
    outs = _ada_bwd_adamw(c_all, dmod_cols, w["w_ada"][0], m["w_ada"][0], v["w_ada"][0])
    grads["w_ada"], delta["w_ada"], new_m["w_ada"], new_v["w_ada"] = [o[None] for o in outs]

    loss = lax.psum(res["loss"][0, 0], ("x", "y", "c"))
    return (loss, res["grad_x"][None], *[grads[n] for n in WEIGHT_NAMES], *[delta[n] for n in WEIGHT_NAMES],
            *[new_m[n] for n in WEIGHT_NAMES], *[new_v[n] for n in WEIGHT_NAMES])
```

```python
import functools

import jax
import jax.numpy as jnp
from jax import lax
from jax.experimental import pallas as pl
from jax.experimental.pallas import tpu as pltpu

F32 = jnp.float32
BF16 = jnp.bfloat16
HIGHEST = lax.Precision.HIGHEST
MESH_IDS = pl.DeviceIdType.MESH

N_DEV = 8
D_MODEL = 1024
HEAD_DIM = 64
SSD_HEADS = 16
SSD_GROUPS = 4
HEADS_PER_GROUP = SSD_HEADS // SSD_GROUPS
SSD_STATE = 128
SSD_CHUNK = 128
SSD_D_INNER = SSD_HEADS * HEAD_DIM
GROUP_WIDTH = SSD_D_INNER // SSD_GROUPS
CONV_K = 4
CONV_CH = SSD_D_INNER + 2 * SSD_GROUPS * SSD_STATE
ATT_HEADS = 16
ATT_D = ATT_HEADS * HEAD_DIM
ATT_BLK = 128
DILATIONS = (1, 4, 16)
D_FF = 4 * D_MODEL
N_MOD = 6
EPS = 1e-6
IN_W = SSD_D_INNER + CONV_CH + SSD_HEADS + 3 * ATT_D
LANE = 128
OFF_Z, OFF_XBC, OFF_Q, OFF_K, OFF_V, OFF_DT = 0, 1024, 3072, 4096, 5120, 6144
IN_WP = OFF_DT + LANE

ADAM_LR, ADAM_B1, ADAM_B2, ADAM_EPS, ADAM_WD, ADAM_STEP = 0.001, 0.9, 0.999, 1e-08, 0.01, 10
VMEM_LIMIT = 56 * 1024 * 1024
ROW_TILE = 512
SMALL_ROWS = 24


def _cparams(sem=None):
    return pltpu.CompilerParams(dimension_semantics=sem, vmem_limit_bytes=VMEM_LIMIT)


def _sigmoid(v):
    return 1.0 / (1.0 + jnp.exp(-v))


def _softplus(v):
    y = jnp.exp(-jnp.abs(v))
    small = y * (1.0 - y * (0.5 - y * (1.0 / 3.0)))
    return jnp.maximum(v, 0.0) + jnp.where(y < 0.01, small, jnp.log(1.0 + y))


def _dot(a, b, dims, precision=None):
    return lax.dot_general(a, b, (dims, ((), ())), preferred_element_type=F32, precision=precision)


NN = ((1,), (0,))
NT = ((1,), (1,))
TN = ((0,), (0,))


def _matmul(a, b, *, ta=False, tb=False, tm, tn, tk, out_dtype=F32, name, mode=None, u=None, comm=None):
    m, k = (a.shape[1], a.shape[0]) if ta else a.shape
    n = b.shape[0] if tb else b.shape[1]
    assert m % tm == 0 and n % tn == 0 and k % tk == 0, (name, m, n, k)
    nk = k // tk
    a_spec = pl.BlockSpec((tk, tm), lambda i, j, kk: (kk, i)) if ta else pl.BlockSpec((tm, tk), lambda i, j, kk: (i, kk))
    b_spec = pl.BlockSpec((tn, tk), lambda i, j, kk: (j, kk)) if tb else pl.BlockSpec((tk, tn), lambda i, j, kk: (kk, j))
    o_spec = pl.BlockSpec((tm, tn), lambda i, j, kk: (i, j))
    dims = ((0,) if ta else (1,), (1,) if tb else (0,))
    n_out = 2 if mode == "relu2" else 1

    def body(*refs):
        if mode == "drelu2":
            a_ref, b_ref, u_ref = refs[:3]
            rest = refs[3:]
        else:
            a_ref, b_ref = refs[:2]
            u_ref = None
            rest = refs[2:]
        outs = rest[:n_out]
        part = _dot(a_ref[...], b_ref[...], dims)

        def finish(r):
            if mode == "relu2":
                outs[0][...] = r.astype(BF16)
                rr = jnp.maximum(r, 0.0)
                outs[1][...] = (rr * rr).astype(BF16)
            elif mode == "drelu2":
                outs[0][...] = (r * (2.0 * jnp.maximum(u_ref[...].astype(F32), 0.0))).astype(out_dtype)
            else:
                outs[0][...] = r.astype(out_dtype)

        if nk == 1:
            finish(part)
        else:
            acc = rest[n_out]
            kk = pl.program_id(2)

            @pl.when(kk == 0)
            def _():
                acc[...] = part

            @pl.when(kk > 0)
            def _():
                acc[...] += part

            @pl.when(kk == nk - 1)
            def _():
                finish(acc[...])

    in_specs = [a_spec, b_spec]
    args = [a, b]
    if mode == "drelu2":
        in_specs.append(o_spec)
        args.append(u)
    if mode == "relu2":
        out_shape = [jax.ShapeDtypeStruct((m, n), BF16), jax.ShapeDtypeStruct((m, n), BF16)]
    else:
        out_shape = [jax.ShapeDtypeStruct((m, n), out_dtype)]
    outs, comm_outs = _pcall(
        body, args, name=name, grid=(m // tm, n // tn, nk), in_specs=in_specs, out_specs=[o_spec] * n_out,
        out_shape=out_shape, scratch_shapes=[pltpu.VMEM((tm, tn), F32)] if nk > 1 else [],
        sem=("parallel", "parallel", "arbitrary"), comm=comm)
    res = tuple(outs) if mode == "relu2" else outs[0]
    return res if comm is None else (res, comm_outs)


def _norm_mod_fwd(x, nw, scale, shift, name, res=None, gate=None):
    s, d = x.shape
    row = pl.BlockSpec((ROW_TILE, d), lambda i: (i, 0))
    vec = pl.BlockSpec((1, d), lambda i: (0, 0))
    with_res = res is not None

    def body(*refs):
        if with_res:
            x_ref, res_ref, gate_ref, nw_ref, sc_ref, sh_ref, x1_ref, h_ref = refs
            xv = x_ref[...] + gate_ref[...] * res_ref[...]
            x1_ref[...] = xv
        else:
            x_ref, nw_ref, sc_ref, sh_ref, h_ref = refs
            xv = x_ref[...]
        r = lax.rsqrt(jnp.mean(xv * xv, axis=-1, keepdims=True) + EPS)
        h_ref[...] = ((xv * r) * nw_ref[...] * (1.0 + sc_ref[...]) + sh_ref[...]).astype(BF16)

    if with_res:
        in_specs = [row, row, vec, vec, vec, vec]
        args = (x, res, gate, nw, scale, shift)
        out_shape = (jax.ShapeDtypeStruct((s, d), F32), jax.ShapeDtypeStruct((s, d), BF16))
        out_specs = (row, row)
    else:
        in_specs = [row, vec, vec, vec]
        args = (x, nw, scale, shift)
        out_shape = jax.ShapeDtypeStruct((s, d), BF16)
        out_specs = row
    return pl.pallas_call(body, name=name, grid=(s // ROW_TILE,), in_specs=in_specs, out_specs=out_specs,
                          out_shape=out_shape, compiler_params=_cparams(("parallel",)))(*args)


def _norm_mod_bwd(dh, xin, dres, nw, scale, name, gate=None, mix=None):
    s, d = xin.shape
    row = pl.BlockSpec((ROW_TILE, d), lambda i: (i, 0))
    vec = pl.BlockSpec((1, d), lambda i: (0, 0))
    with_gate = gate is not None

    def body(*refs):
        if with_gate:
            dh_ref, x_ref, dres_ref, nw_ref, sc_ref, gate_ref, mix_ref, dx_ref, dsh_ref, dsc_ref, dnw_ref, dmix_ref, dg_ref = refs
        else:
            dh_ref, x_ref, dres_ref, nw_ref, sc_ref, dx_ref, dsh_ref, dsc_ref, dnw_ref = refs
        i = pl.program_id(0)

        @pl.when(i == 0)
        def _():
            dsh_ref[...] = jnp.zeros_like(dsh_ref)
            dsc_ref[...] = jnp.zeros_like(dsc_ref)
            dnw_ref[...] = jnp.zeros_like(dnw_ref)
            if with_gate:
                dg_ref[...] = jnp.zeros_like(dg_ref)

        xv = x_ref[...]
        dhv = dh_ref[...]
        r = lax.rsqrt(jnp.mean(xv * xv, axis=-1, keepdims=True) + EPS)
        nrm = xv * r
        one_sc = 1.0 + sc_ref[...]
        dhn = dhv * nrm
        dsh_ref[...] += jnp.sum(dhv, axis=0, keepdims=True)
        dsc_ref[...] += jnp.sum(dhn, axis=0, keepdims=True) * nw_ref[...]
        dnw_ref[...] += jnp.sum(dhn, axis=0, keepdims=True) * one_sc
        dn = dhv * (nw_ref[...] * one_sc)
        dx = dres_ref[...] + r * (dn - nrm * jnp.mean(dn * nrm, axis=-1, keepdims=True))
        dx_ref[...] = dx
        if with_gate:
            dmix_ref[...] = (gate_ref[...] * dx).astype(BF16)
            dg_ref[...] += jnp.sum(dx * mix_ref[...], axis=0, keepdims=True)

    vshape = jax.ShapeDtypeStruct((1, d), F32)
    in_specs = [row, row, row, vec, vec]
    args = [dh, xin, dres, nw, scale]
    out_shape = [jax.ShapeDtypeStruct((s, d), F32), vshape, vshape, vshape]
    out_specs = [row, vec, vec, vec]
    if with_gate:
        in_specs += [vec, row]
        args += [gate, mix]
        out_shape += [jax.ShapeDtypeStruct((s, d), BF16), vshape]
        out_specs += [row, vec]
    return pl.pallas_call(body, name=name, grid=(s // ROW_TILE,), in_specs=in_specs, out_specs=out_specs,
                          out_shape=out_shape, compiler_params=_cparams(("arbitrary",)))(*args)


def _loss_head(x1, ff, gate2, tgt):
    s, d = x1.shape
    row = pl.BlockSpec((ROW_TILE, d), lambda i: (i, 0))
    vec = pl.BlockSpec((1, d), lambda i: (0, 0))
    one = pl.BlockSpec((1, 1), lambda i: (0, 0))

    def body(x1_ref, ff_ref, g_ref, t_ref, loss_ref, dout_ref, dff_ref, dg_ref):
        i = pl.program_id(0)

        @pl.when(i == 0)
        def _():
            loss_ref[...] = jnp.zeros_like(loss_ref)
            dg_ref[...] = jnp.zeros_like(dg_ref)

        ffv = ff_ref[...]
        err = x1_ref[...] + g_ref[...] * ffv - t_ref[...]
        loss_ref[...] += (0.5 / d) * jnp.sum(err * err).reshape(1, 1)
        dout = err * (1.0 / d)
        dout_ref[...] = dout
        dff_ref[...] = (g_ref[...] * dout).astype(BF16)
        dg_ref[...] += jnp.sum(dout * ffv, axis=0, keepdims=True)

    return pl.pallas_call(
        body, name="loss_head", grid=(s // ROW_TILE,), in_specs=[row, row, vec, row],
        out_specs=[one, row, row, vec],
        out_shape=[jax.ShapeDtypeStruct((1, 1), F32), jax.ShapeDtypeStruct((s, d), F32),
                   jax.ShapeDtypeStruct((s, d), BF16), jax.ShapeDtypeStruct((1, d), F32)],
        compiler_params=_cparams(("arbitrary",)))(x1, ff, gate2, tgt)


CONV_COLS = 256
HALO = 8


def _shift_down(cur, halo, k):
    if k == 0:
        return cur
    rolled = pltpu.roll(cur, k, axis=0)
    top = jnp.where(lax.broadcasted_iota(jnp.int32, halo.shape, 0) < k, pltpu.roll(halo, k, axis=0), rolled[:HALO])
    return jnp.concatenate([top, rolled[HALO:]], axis=0)


def _shift_up(cur, halo, k):
    if k == 0:
        return cur
    t = cur.shape[0]
    rolled = pltpu.roll(cur, t - k, axis=0)
    bot = jnp.where(lax.broadcasted_iota(jnp.int32, halo.shape, 0) >= HALO - k, pltpu.roll(halo, HALO - k, axis=0),
                    rolled[t - HALO:])
    return jnp.concatenate([rolled[:t - HALO], bot], axis=0)


def _conv_fwd(proj, conv_w, conv_b):
    s = proj.shape[0]
    nr = s // ROW_TILE
    cb0 = OFF_XBC // CONV_COLS
    hb = ROW_TILE // HALO
    cur = pl.BlockSpec((ROW_TILE, CONV_COLS), lambda j, r: (r, cb0 + j))
    prev = pl.BlockSpec((HALO, CONV_COLS), lambda j, r: (jnp.maximum(r * hb - 1, 0), cb0 + j))
    out = pl.BlockSpec((ROW_TILE, CONV_COLS), lambda j, r: (r, j))

    def body(u_ref, up_ref, w_ref, b_ref, pre_ref, act_ref):
        r = pl.program_id(1)
        u = u_ref[...]
        halo = jnp.where(r > 0, up_ref[...], 0.0)
        acc = b_ref[...] + w_ref[CONV_K - 1:CONV_K, :] * u
        for k in range(1, CONV_K):
            acc = acc + w_ref[CONV_K - 1 - k:CONV_K - k, :] * _shift_down(u, halo, k)
        pre_ref[...] = acc
        act_ref[...] = acc * _sigmoid(acc)

    return pl.pallas_call(
        body, name="conv_fwd", grid=(CONV_CH // CONV_COLS, nr),
        in_specs=[cur, prev, pl.BlockSpec((CONV_K, CONV_COLS), lambda j, r: (0, j)),
                  pl.BlockSpec((1, CONV_COLS), lambda j, r: (0, j))],
        out_specs=[out, out],
        out_shape=[jax.ShapeDtypeStruct((s, CONV_CH), F32), jax.ShapeDtypeStruct((s, CONV_CH), F32)],
        compiler_params=_cparams(("parallel", "arbitrary")))(proj, proj, conv_w, conv_b)


def _conv_bwd(dact, pre, proj, conv_w):
    s = proj.shape[0]
    nr = s // ROW_TILE
    cb0 = OFF_XBC // CONV_COLS
    hb = ROW_TILE // HALO
    last_halo = s // HALO - 1
    cur = pl.BlockSpec((ROW_TILE, CONV_COLS), lambda j, r: (r, j))
    nxt = pl.BlockSpec((HALO, CONV_COLS), lambda j, r: (jnp.minimum((r + 1) * hb, last_halo), j))
    ucur = pl.BlockSpec((ROW_TILE, CONV_COLS), lambda j, r: (r, cb0 + j))
    uprev = pl.BlockSpec((HALO, CONV_COLS), lambda j, r: (jnp.maximum(r * hb - 1, 0), cb0 + j))
    wspec = pl.BlockSpec((CONV_K, CONV_COLS), lambda j, r: (0, j))
    bspec = pl.BlockSpec((1, CONV_COLS), lambda j, r: (0, j))

    def dsilu(p):
        sg = _sigmoid(p)
        return sg * (1.0 + p * (1.0 - sg))

    def body(da_ref, dan_ref, pre_ref, pren_ref, u_ref, up_ref, w_ref, du_ref, dw_ref, db_ref):
        r = pl.program_id(1)

        @pl.when(r == 0)
        def _():
            dw_ref[...] = jnp.zeros_like(dw_ref)
            db_ref[...] = jnp.zeros_like(db_ref)

        dpre = da_ref[...] * dsilu(pre_ref[...])
        dnext = jnp.where(r < nr - 1, dan_ref[...] * dsilu(pren_ref[...]), 0.0)
        u = u_ref[...]
        halo = jnp.where(r > 0, up_ref[...], 0.0)
        du = w_ref[CONV_K - 1:CONV_K, :] * dpre
        dws = [jnp.sum(dpre * u, axis=0, keepdims=True)]
        for k in range(1, CONV_K):
            du = du + w_ref[CONV_K - 1 - k:CONV_K - k, :] * _shift_up(dpre, dnext, k)
            dws.append(jnp.sum(dpre * _shift_down(u, halo, k), axis=0, keepdims=True))
        du_ref[...] = du.astype(BF16)
        dw_ref[...] += jnp.concatenate(dws[::-1], axis=0)
        db_ref[...] += jnp.sum(dpre, axis=0, keepdims=True)

    return pl.pallas_call(
        body, name="conv_bwd", grid=(CONV_CH // CONV_COLS, nr),
        in_specs=[cur, nxt, cur, nxt, ucur, uprev, wspec],
        out_specs=[cur, wspec, bspec],
        out_shape=[jax.ShapeDtypeStruct((s, CONV_CH), BF16), jax.ShapeDtypeStruct((CONV_K, CONV_CH), F32),
                   jax.ShapeDtypeStruct((1, CONV_CH), F32)],
        compiler_params=_cparams(("parallel", "arbitrary")))(dact, dact, pre, pre, proj, proj, conv_w)


def _ssd_common(dtr, dtb, alog):
    lane = lax.broadcasted_iota(jnp.int32, (1, LANE), 1)
    head_lane = lane < SSD_HEADS
    dt = jnp.where(head_lane, _softplus(dtr + dtb), 0.0)
    a = jnp.where(head_lane, -jnp.exp(alog), 0.0)
    row = lax.broadcasted_iota(jnp.int32, (SSD_CHUNK, SSD_CHUNK), 0)
    col = lax.broadcasted_iota(jnp.int32, (SSD_CHUNK, SSD_CHUNK), 1)
    tril = row >= col
    cs = _dot(tril.astype(F32), dt * a, NN, precision=HIGHEST)
    return dt, a, cs, cs.T, tril, lane


def _split_bf16(v, passes):
    terms, rest = [], v
    for _ in range(passes):
        t = rest.astype(BF16)
        terms.append(t)
        rest = rest - t.astype(F32)
    return terms


def _dot_split(v, m, dims, passes):
    out = None
    for t in _split_bf16(v, passes):
        part = _dot(t, m, dims)
        out = part if out is None else out + part
    return out


def _ssd_constants():
    heads = jnp.arange(LANE)[:, None]
    exp_mat = (heads == (jnp.arange(SSD_D_INNER)[None, :] // HEAD_DIM)).astype(BF16)
    ind4 = ((jnp.arange(SSD_HEADS * SSD_CHUNK)[:, None] // SSD_CHUNK) == jnp.arange(LANE)[None, :]).astype(BF16)
    return exp_mat, ind4


def _expand_heads(v):
    return jnp.repeat(v[:, :SSD_HEADS], HEAD_DIM, axis=1)


def _ssd_prep(dtr, dtb, alog, exp_mat):
    dt, a, cs, cst, tril, lane = _ssd_common(dtr, dtb, alog)
    return dt, a, cs, cst, tril, lane, _dot_split(dt, exp_mat, NN, 2), _dot_split(cs, exp_mat, NN, 3)


def _chunk_decay_rows(cs, g):
    parts = []
    for e in range(HEADS_PER_GROUP):
        h = g * HEADS_PER_GROUP + e
        parts.append(jnp.broadcast_to(jnp.exp(cs[SSD_CHUNK - 1:SSD_CHUNK, h:h + 1]), (HEAD_DIM, SSD_STATE)))
    return jnp.concatenate(parts, axis=0)


def _ssd_fwd(proj, act, dtb, alog, dsk, nw):
    s = proj.shape[0]
    nc = s // SSD_CHUNK
    bc_w = SSD_GROUPS * SSD_STATE
    exp_mat, _ = _ssd_constants()

    def body(z_ref, dtr_ref, xs_ref, b_ref, c_ref, dtb_ref, alog_ref, dskx_ref, nw_ref, exp_ref,
             ypre_ref, yssd_ref, hall_ref, h_scr):
        @pl.when(pl.program_id(0) == 0)
        def _():
            h_scr[...] = jnp.zeros_like(h_scr)

        dt, a, cs, cst, tril, lane, dtx, csx = _ssd_prep(dtr_ref[...], dtb_ref[...], alog_ref[...], exp_ref[...])
        cs_last_x = csx[SSD_CHUNK - 1:SSD_CHUNK, :]
        xs = xs_ref[...]
        xdt = xs * dtx
        xdtb = xdt.astype(BF16)
        xdec = (xdt * jnp.exp(cs_last_x - csx)).astype(BF16)
        ecsx = jnp.exp(csx)
        head_of_lane = lax.broadcasted_iota(jnp.int32, (1, GROUP_WIDTH), 1) // HEAD_DIM
        for g in range(SSD_GROUPS):
            gs = slice(g * GROUP_WIDTH, (g + 1) * GROUP_WIDTH)
            bg = b_ref[:, g * SSD_STATE:(g + 1) * SSD_STATE].astype(BF16)
            cg = c_ref[:, g * SSD_STATE:(g + 1) * SSD_STATE].astype(BF16)
            cb = _dot(cg, bg, NT)
            hprev = h_scr[gs, :]
            hall_ref[0, gs, :] = hprev
            gms, rhs = [], []
            xg = xdtb[:, gs]
            for e in range(HEADS_PER_GROUP):
                h = g * HEADS_PER_GROUP + e
                lm = jnp.exp(jnp.where(tril, cs[:, h:h + 1] - cst[h:h + 1, :], -1e30))
                gms.append((cb * lm).astype(BF16))
                rhs.append(jnp.where(head_of_lane == e, xg, jnp.zeros_like(xg)))
            y = _dot(jnp.concatenate(gms, axis=1), jnp.concatenate(rhs, axis=0), NN)
            y = y + ecsx[:, gs] * _dot(cg, hprev.astype(BF16), NT)
            y = y + dskx_ref[:, gs] * xs[:, gs]
            h_scr[gs, :] = hprev * _chunk_decay_rows(cs, g) + _dot(xdec[:, gs], bg, TN)
            ypre_ref[:, gs] = y
            z = z_ref[:, gs]
            yg = y * (z * _sigmoid(z))
            r = lax.rsqrt(jnp.mean(yg * yg, axis=-1, keepdims=True) + EPS)
            yssd_ref[:, gs] = (yg * r * nw_ref[:, gs]).astype(BF16)

    row_d = lambda cb: pl.BlockSpec((SSD_CHUNK, SSD_D_INNER), lambda c: (c, cb))
    small = pl.BlockSpec((1, LANE), lambda c: (0, 0))
    wide = pl.BlockSpec((1, SSD_D_INNER), lambda c: (0, 0))
    return pl.pallas_call(
        body, name="ssd_fwd", grid=(nc,),
        in_specs=[row_d(OFF_Z // SSD_D_INNER),
                  pl.BlockSpec((SSD_CHUNK, LANE), lambda c: (c, OFF_DT // LANE)),
                  row_d(0),
                  pl.BlockSpec((SSD_CHUNK, bc_w), lambda c: (c, SSD_D_INNER // bc_w)),
                  pl.BlockSpec((SSD_CHUNK, bc_w), lambda c: (c, SSD_D_INNER // bc_w + 1)),
                  small, small, wide, wide, pl.BlockSpec((LANE, SSD_D_INNER), lambda c: (0, 0))],
        out_specs=[row_d(0), row_d(0), pl.BlockSpec((1, SSD_D_INNER, SSD_STATE), lambda c: (c, 0, 0))],
        out_shape=[jax.ShapeDtypeStruct((s, SSD_D_INNER), F32), jax.ShapeDtypeStruct((s, SSD_D_INNER), BF16),
                   jax.ShapeDtypeStruct((nc, SSD_D_INNER, SSD_STATE), F32)],
        scratch_shapes=[pltpu.VMEM((SSD_D_INNER, SSD_STATE), F32)],
        compiler_params=_cparams(("arbitrary",)))(proj, proj, act, act, act, dtb, alog, _expand_heads(dsk), nw, exp_mat)


def _ssd_bwd(dycat, ypre, proj, act, hall, dtb, alog, dsk, nw, comm=None):
    s = proj.shape[0]
    nc = s // SSD_CHUNK
    bc_w = SSD_GROUPS * SSD_STATE

    exp_mat, ind4 = _ssd_constants()
    seg_passes = 2

    def body(dy_ref, ypre_ref, z_ref, dtr_ref, xs_ref, b_ref, c_ref, hall_ref, dtb_ref, alog_ref, dskx_ref, nw_ref,
             exp_ref, ind4_ref, dz_ref, dact_ref, ddtr_ref, da_ref, ddsk_ref, ddtb_ref, dnw_ref, dh_scr):
        @pl.when(pl.program_id(0) == 0)
        def _():
            dh_scr[...] = jnp.zeros_like(dh_scr)
            da_ref[...] = jnp.zeros_like(da_ref)
            ddsk_ref[...] = jnp.zeros_like(ddsk_ref)
            ddtb_ref[...] = jnp.zeros_like(ddtb_ref)
            dnw_ref[...] = jnp.zeros_like(dnw_ref)

        dtr = dtr_ref[...]
        dt, a, cs, cst, tril, lane, dtx, csx = _ssd_prep(dtr, dtb_ref[...], alog_ref[...], exp_ref[...])
        cs_last_x = csx[SSD_CHUNK - 1:SSD_CHUNK, :]
        xs = xs_ref[...]
        xdt = xs * dtx
        xdtb = xdt.astype(BF16)
        decx = jnp.exp(cs_last_x - csx)
        xdecf = xdt * decx
        xdec = xdecf.astype(BF16)
        ecsx = jnp.exp(csx)
        head_of_lane = lax.broadcasted_iota(jnp.int32, (1, GROUP_WIDTH), 1) // HEAD_DIM
        last_row = lax.broadcasted_iota(jnp.int32, (SSD_CHUNK, 1), 0) == SSD_CHUNK - 1
        dcs_col = jnp.zeros((SSD_CHUNK, LANE), F32)
        dcs_row = jnp.zeros((SSD_CHUNK, LANE), F32)
        ddt = jnp.zeros((SSD_CHUNK, LANE), F32)
        ddsk = jnp.zeros((1, LANE), F32)
        hsum = jnp.zeros((1, LANE), F32)
        t1_sum = jnp.zeros((1, LANE), F32)
        for g in range(SSD_GROUPS):
            gs = slice(g * GROUP_WIDTH, (g + 1) * GROUP_WIDTH)
            bsl = slice(g * SSD_STATE, (g + 1) * SSD_STATE)
            exp_g = exp_ref[:, gs]
            ind4_g = ind4_ref[g * HEADS_PER_GROUP * SSD_CHUNK:(g + 1) * HEADS_PER_GROUP * SSD_CHUNK, :]
            z = z_ref[:, gs]
            sg = _sigmoid(z)
            sz = z * sg
            ypre = ypre_ref[:, gs]
            yg = ypre * sz
            r = lax.rsqrt(jnp.mean(yg * yg, axis=-1, keepdims=True) + EPS)
            nrm = yg * r
            dyo_n = dy_ref[:, gs]
            dnw_ref[:, gs] += jnp.sum(dyo_n * nrm, axis=0, keepdims=True)
            dn = dyo_n * nw_ref[:, gs]
            dyg = r * (dn - nrm * jnp.mean(dn * nrm, axis=-1, keepdims=True))
            dz_ref[:, gs] = (dyg * ypre * (sg * (1.0 + z * (1.0 - sg)))).astype(BF16)
            dy = dyg * sz

            bg = b_ref[:, bsl].astype(BF16)
            cg = c_ref[:, bsl].astype(BF16)
            cb = _dot(cg, bg, NT)
            hprev = hall_ref[0, gs, :]
            hb = hprev.astype(BF16)
            dhn = dh_scr[gs, :]
            dhb = dhn.astype(BF16)
            xs_g, xdt_g = xs[:, gs], xdtb[:, gs]
            w_off = _dot(cg, hb, NT)
            dyo = dy * ecsx[:, gs]
            dyob = dyo.astype(BF16)
            dcg = _dot(dyob, hb, NN)
            dh_y = _dot(dyob, cg, TN)
            r_st = _dot(bg, dhb, NT)
            dbg = _dot(xdec[:, gs], dhb, NN)
            dyb = dy.astype(BF16)
            gms, gmbs, lms, dys = [], [], [], []
            for e in range(HEADS_PER_GROUP):
                h = g * HEADS_PER_GROUP + e
                lm = jnp.exp(jnp.where(tril, cs[:, h:h + 1] - cst[h:h + 1, :], -1e30))
                gm = cb * lm
                lms.append(lm)
                gms.append(gm)
                gmbs.append(gm.astype(BF16))
                dys.append(jnp.where(head_of_lane == e, dyb, jnp.zeros_like(dyb)))
            dxdt = _dot(jnp.concatenate(gmbs, axis=0), jnp.concatenate(dys, axis=0), TN) + decx[:, gs] * r_st
            dcb = jnp.zeros((SSD_CHUNK, SSD_CHUNK), F32)
            mms = []
            for e in range(HEADS_PER_GROUP):
                dg = _dot(dys[e], xdt_g, NT)
                mms.append(dg * gms[e])
                dcb = dcb + dg * lms[e]
            seg = _dot_split(jnp.concatenate([dyo * w_off, xdecf[:, gs] * r_st, dxdt * xs_g, dy * xs_g], axis=0), exp_g, NT, seg_passes)
            v1, t1, ddt_g, dsk_g = [seg[i * SSD_CHUNK:(i + 1) * SSD_CHUNK] for i in range(4)]
            dcs_col = dcs_col + v1 - t1 + _dot_split(jnp.concatenate(mms, axis=1), ind4_g, NN, seg_passes)
            for t in _split_bf16(jnp.concatenate(mms, axis=0), seg_passes):
                dcs_row = dcs_row + _dot(ind4_g, t, TN)
            ddt = ddt + ddt_g
            ddsk = ddsk + jnp.sum(dsk_g, axis=0, keepdims=True)
            t1_sum = t1_sum + jnp.sum(t1, axis=0, keepdims=True)
            for e in range(HEADS_PER_GROUP):
                h = g * HEADS_PER_GROUP + e
                hs = slice(e * HEAD_DIM, (e + 1) * HEAD_DIM)
                hsum = hsum + jnp.where(lane == h, jnp.sum(dhn[hs, :] * hprev[hs, :]).reshape(1, 1), 0.0)
            dh_scr[gs, :] = dhn * _chunk_decay_rows(cs, g) + dh_y
            dcbb = dcb.astype(BF16)
            dact_ref[:, gs] = dxdt * dtx[:, gs] + dskx_ref[:, gs] * dy
            dact_ref[:, SSD_D_INNER + g * SSD_STATE:SSD_D_INNER + (g + 1) * SSD_STATE] = dbg + _dot(dcbb, cg, TN)
            dact_ref[:, SSD_D_INNER + bc_w + g * SSD_STATE:SSD_D_INNER + bc_w + (g + 1) * SSD_STATE] = dcg + _dot(dcbb, bg, NN)
        dlast = t1_sum + jnp.exp(cs[SSD_CHUNK - 1:SSD_CHUNK, :]) * hsum
        dcs = dcs_col - dcs_row.T + jnp.where(last_row, dlast, 0.0)
        row = lax.broadcasted_iota(jnp.int32, (SSD_CHUNK, SSD_CHUNK), 0)
        col = lax.broadcasted_iota(jnp.int32, (SSD_CHUNK, SSD_CHUNK), 1)
        dda = _dot((col >= row).astype(F32), dcs, NN, precision=HIGHEST)
        ddt = ddt + dda * a
        da_ref[...] += jnp.sum(dda * dt, axis=0, keepdims=True)
        ddtr = jnp.where(lane < SSD_HEADS, ddt * _sigmoid(dtr + dtb_ref[...]), 0.0)
        ddtr_ref[...] = ddtr.astype(BF16)
        ddtb_ref[...] += jnp.sum(ddtr, axis=0, keepdims=True)
        ddsk_ref[...] += ddsk

    rev = lambda c: nc - 1 - c
    row_d = lambda cb: pl.BlockSpec((SSD_CHUNK, SSD_D_INNER), lambda c: (rev(c), cb))
    small = pl.BlockSpec((1, LANE), lambda c: (0, 0))
    wide = pl.BlockSpec((1, SSD_D_INNER), lambda c: (0, 0))
    small_shape = jax.ShapeDtypeStruct((1, LANE), F32)
    return _pcall(
        body, (dycat, ypre, proj, proj, act, act, act, hall, dtb, alog, _expand_heads(dsk), nw, exp_mat, ind4),
        name="ssd_bwd", grid=(nc,),
        in_specs=[row_d(0), row_d(0), row_d(OFF_Z // SSD_D_INNER),
                  pl.BlockSpec((SSD_CHUNK, LANE), lambda c: (rev(c), OFF_DT // LANE)),
                  row_d(0),
                  pl.BlockSpec((SSD_CHUNK, bc_w), lambda c: (rev(c), SSD_D_INNER // bc_w)),
                  pl.BlockSpec((SSD_CHUNK, bc_w), lambda c: (rev(c), SSD_D_INNER // bc_w + 1)),
                  pl.BlockSpec((1, SSD_D_INNER, SSD_STATE), lambda c: (rev(c), 0, 0)),
                  small, small, wide, wide, pl.BlockSpec((LANE, SSD_D_INNER), lambda c: (0, 0)),
                  pl.BlockSpec((SSD_HEADS * SSD_CHUNK, LANE), lambda c: (0, 0))],
        out_specs=[row_d(0), pl.BlockSpec((SSD_CHUNK, CONV_CH), lambda c: (rev(c), 0)),
                   pl.BlockSpec((SSD_CHUNK, LANE), lambda c: (rev(c), 0)), small, small, small, wide],
        out_shape=[jax.ShapeDtypeStruct((s, SSD_D_INNER), BF16), jax.ShapeDtypeStruct((s, CONV_CH), F32),
                   jax.ShapeDtypeStruct((s, LANE), BF16), small_shape, small_shape, small_shape,
                   jax.ShapeDtypeStruct((1, SSD_D_INNER), F32)],
        scratch_shapes=[pltpu.VMEM((SSD_D_INNER, SSD_STATE), F32)], sem=("arbitrary",), comm=comm)


def _head_mean_matrix():
    row = lax.broadcasted_iota(jnp.int32, (LANE, LANE), 0) // HEAD_DIM
    col = lax.broadcasted_iota(jnp.int32, (LANE, LANE), 1) // HEAD_DIM
    return (row == col).astype(F32)


def _head_sum2(v, ones_bd):
    hi = v.astype(BF16)
    lo = (v - hi.astype(F32)).astype(BF16)
    return _dot(hi, ones_bd, NN) + _dot(lo, ones_bd, NN)


def _head_norm(x, w, scale, ones_bd):
    ms = _head_sum2(x * x, ones_bd) * (1.0 / HEAD_DIM)
    return (x * lax.rsqrt(ms + EPS)) * (w * scale)


PRO_ROWS = 256
ATT_UNROLL = 4
KEYS = 2 * ATT_BLK
NEG = -1e30
HALF = HEAD_DIM // 2


def _rows(start, size, dil):
    return pl.ds(start, size) if dil == 1 else pl.ds(start, size, stride=dil)


def _fill_bias(bias_ref):
    row = lax.broadcasted_iota(jnp.int32, (ATT_BLK, 2 * KEYS), 0)
    col = lax.broadcasted_iota(jnp.int32, (ATT_BLK, 2 * KEYS), 1) & (KEYS - 1)
    for first, off in ((0, 0), (1, ATT_BLK)):
        dist = off + row - col
        bias_ref[first] = jnp.where((dist >= 0) & (dist <= ATT_BLK), 0.0, NEG)


def _pair(a, b):
    return jnp.concatenate([jnp.broadcast_to(a, (ATT_BLK, KEYS)), jnp.broadcast_to(b, (ATT_BLK, KEYS))], axis=1)


def _split_heads(x, is_a):
    zero = jnp.zeros_like(x)
    return jnp.concatenate([jnp.where(is_a, x, zero), jnp.where(is_a, zero, x)], axis=0)


def _block_ids(b, nb):
    i = b & (nb - 1)
    q0 = pl.multiple_of(b * ATT_BLK, ATT_BLK)
    k0 = pl.multiple_of((b - jnp.minimum(i, 1)) * ATT_BLK, ATT_BLK)
    return pl.ds(q0, ATT_BLK), pl.ds(k0, KEYS), jnp.minimum(i, 1)


def _att_fwd(proj, qw, kw, comm=None):
    s = proj.shape[0]
    nblk = s // ATT_BLK
    assert all((s // d) // ATT_BLK >= 2 for d in DILATIONS)
    blk = lambda off: pl.BlockSpec((s, LANE), lambda i: (0, off // LANE + i))
    wspec = pl.BlockSpec((1, LANE), lambda i: (0, i))
    oblk = pl.BlockSpec((s, LANE), lambda i: (0, i))

    def body(q_ref, k_ref, v_ref, qw_ref, kw_ref, o_ref, lse_ref, qn, kn, q_cm, k_cm, v_cm, m_acc, l_acc, o_d, m_d, l_d, bias):
        ones_bd = _head_mean_matrix().astype(BF16)
        is_a = lax.broadcasted_iota(jnp.int32, (1, LANE), 1) < HEAD_DIM
        ones_ext = _split_heads(jnp.ones((KEYS, LANE), BF16), is_a)
        _fill_bias(bias)

        def pro(j, c):
            rows = pl.ds(pl.multiple_of(j * PRO_ROWS, PRO_ROWS), PRO_ROWS)
            qn[rows, :] = _head_norm(q_ref[rows, :], qw_ref[...], HEAD_DIM ** -0.5, ones_bd)
            kn[rows, :] = _head_norm(k_ref[rows, :], kw_ref[...], 1.0, ones_bd)
            return c

        lax.fori_loop(0, s // PRO_ROWS, pro, 0)

        for dil in DILATIONS:
            ln = s // dil
            nb = ln // ATT_BLK
            o_out, m_out, l_out = (o_ref, m_acc, l_acc) if dil == 1 else (o_d, m_d, l_d)
            for r in range(dil):
                def relayout(j, c, dil=dil, r=r, ln=ln):
                    j0 = pl.multiple_of(j * PRO_ROWS, PRO_ROWS)
                    src = _rows(r + dil * j0, PRO_ROWS, dil)
                    dst = pl.ds(r * ln + j0, PRO_ROWS)
                    q_cm[dst, :] = qn[src, :].astype(BF16)
                    k_cm[dst, :] = kn[src, :].astype(BF16)
                    v_cm[dst, :] = v_ref[src, :].astype(BF16)
                    return c

                lax.fori_loop(0, ln // PRO_ROWS, relayout, 0)

            def step(b, c, nb=nb, o_out=o_out, m_out=m_out, l_out=l_out):
                qrows, krows, first = _block_ids(b, nb)
                kb = _split_heads(k_cm[krows, :], is_a)
                vb = jnp.concatenate([_split_heads(v_cm[krows, :], is_a), ones_ext], axis=1)
                sc = _dot(q_cm[qrows, :], kb, NT) + bias[first]
                m_a = jnp.max(sc[:, :KEYS], axis=-1, keepdims=True)
                m_b = jnp.max(sc[:, KEYS:], axis=-1, keepdims=True)
                p = jnp.exp(sc - _pair(m_a, m_b)).astype(BF16)
                ol = _dot(p, vb, NN)
                o_out[qrows, :] = ol[:, :LANE]
                l_out[qrows, :] = ol[:, LANE:]
                m_out[qrows, :] = jnp.where(is_a, m_a, m_b)
                return c

            lax.fori_loop(0, nblk, step, 0, unroll=ATT_UNROLL)

            if dil > 1:
                for r in range(dil):
                    def merge(j, c, dil=dil, r=r, ln=ln):
                        j0 = pl.multiple_of(j * PRO_ROWS, PRO_ROWS)
                        nat = _rows(r + dil * j0, PRO_ROWS, dil)
                        cm = pl.ds(r * ln + j0, PRO_ROWS)
                        m_old, m_new = m_acc[nat, :], m_d[cm, :]
                        m = jnp.maximum(m_old, m_new)
                        a_old, a_new = jnp.exp(m_old - m), jnp.exp(m_new - m)
                        o_ref[nat, :] = a_old * o_ref[nat, :] + a_new * o_d[cm, :]
                        l_acc[nat, :] = a_old * l_acc[nat, :] + a_new * l_d[cm, :]
                        m_acc[nat, :] = m
                        return c

                    lax.fori_loop(0, ln // PRO_ROWS, merge, 0)

        def epi(j, c):
            rows = pl.ds(pl.multiple_of(j * PRO_ROWS, PRO_ROWS), PRO_ROWS)
            l = l_acc[rows, :]
            o_ref[rows, :] = o_ref[rows, :] / l
            lse_ref[rows, :] = m_acc[rows, :] + jnp.log(l)
            return c

        lax.fori_loop(0, s // PRO_ROWS, epi, 0)

    f = jax.ShapeDtypeStruct((s, ATT_D), F32)
    scr = pltpu.VMEM((s, LANE), F32)
    scb = pltpu.VMEM((s, LANE), BF16)
    return _pcall(
        body, (proj, proj, proj, qw, kw), name="att_fwd", grid=(ATT_D // LANE,),
        in_specs=[blk(OFF_Q), blk(OFF_K), blk(OFF_V), wspec, wspec], out_specs=[oblk, oblk], out_shape=[f, f],
        scratch_shapes=[scr, scr, scb, scb, scb, scr, scr, scr, scr, scr, pltpu.VMEM((2, ATT_BLK, 2 * KEYS), F32)],
        sem=("parallel",), comm=comm)


def _att_bwd(proj, do, stats, qw, kw, comm=None):
    s = proj.shape[0]
    nblk = s // ATT_BLK
    blk = lambda off: pl.BlockSpec((s, LANE), lambda i: (0, off // LANE + i))
    wspec = pl.BlockSpec((1, LANE), lambda i: (0, i))
    oblk = pl.BlockSpec((s, LANE), lambda i: (0, i))

    def body(q_ref, k_ref, v_ref, do_ref, st_ref, qw_ref, kw_ref, dq_ref, dk_ref, dv_ref, dqw_ref, dkw_ref,
             qn, kn, q_cm, do_cm, k_cm, v_cm, st_cm, dq_acc, dk_acc, dv_acc, dq_d, dk_d, dv_d, bias):
        ones_bd = _head_mean_matrix().astype(BF16)
        is_a = lax.broadcasted_iota(jnp.int32, (1, LANE), 1) < HEAD_DIM
        _fill_bias(bias)
        zero = jnp.zeros((PRO_ROWS, LANE), F32)

        def pro(j, c):
            rows = pl.ds(pl.multiple_of(j * PRO_ROWS, PRO_ROWS), PRO_ROWS)
            qn[rows, :] = _head_norm(q_ref[rows, :], qw_ref[...], HEAD_DIM ** -0.5, ones_bd)
            kn[rows, :] = _head_norm(k_ref[rows, :], kw_ref[...], 1.0, ones_bd)
            dk_acc[rows, :] = zero
            dv_acc[rows, :] = zero
            return c

        lax.fori_loop(0, s // PRO_ROWS, pro, 0)

        for dil in DILATIONS:
            ln = s // dil
            nb = ln // ATT_BLK
            dq_o, dk_o, dv_o = (dq_acc, dk_acc, dv_acc) if dil == 1 else (dq_d, dk_d, dv_d)
            for r in range(dil):
                def relayout(j, c, dil=dil, r=r, ln=ln):
                    j0 = pl.multiple_of(j * PRO_ROWS, PRO_ROWS)
                    src = _rows(r + dil * j0, PRO_ROWS, dil)
                    dst = pl.ds(r * ln + j0, PRO_ROWS)
                    q_cm[dst, :] = qn[src, :].astype(BF16)
                    k_cm[dst, :] = kn[src, :].astype(BF16)
                    v_cm[dst, :] = v_ref[src, :].astype(BF16)
                    do_cm[dst, :] = do_ref[src, :].astype(BF16)
                    st_cm[dst, :] = st_ref[src, :]
                    if dil > 1:
                        dk_d[dst, :] = zero
                        dv_d[dst, :] = zero
                    return c

                lax.fori_loop(0, ln // PRO_ROWS, relayout, 0)

            def step(b, c, nb=nb, dq_o=dq_o, dk_o=dk_o, dv_o=dv_o):
                qrows, krows, first = _block_ids(b, nb)
                qb = q_cm[qrows, :]
                dob = do_cm[qrows, :]
                kb = _split_heads(k_cm[krows, :], is_a)
                vb = _split_heads(v_cm[krows, :], is_a)
                st = st_cm[qrows, :]
                sc = _dot(qb, kb, NT) + bias[first]
                p = jnp.exp(sc - _pair(st[:, 0:1], st[:, HEAD_DIM:HEAD_DIM + 1]))
                dp = _dot(dob, vb, NT)
                ds = (p * (dp - _pair(st[:, HALF:HALF + 1], st[:, HEAD_DIM + HALF:HEAD_DIM + HALF + 1]))).astype(BF16)
                dq_o[qrows, :] = _dot(ds, kb, NN)
                dkf = _dot(ds, qb, TN)
                dvf = _dot(p.astype(BF16), dob, TN)
                dk_o[krows, :] += jnp.where(is_a, dkf[:KEYS], dkf[KEYS:])
                dv_o[krows, :] += jnp.where(is_a, dvf[:KEYS], dvf[KEYS:])
                return c

            lax.fori_loop(0, nblk, step, 0, unroll=ATT_UNROLL)

            if dil > 1:
                for r in range(dil):
                    def merge(j, c, dil=dil, r=r, ln=ln):
                        j0 = pl.multiple_of(j * PRO_ROWS, PRO_ROWS)
                        nat = _rows(r + dil * j0, PRO_ROWS, dil)
                        cm = pl.ds(r * ln + j0, PRO_ROWS)
                        dq_acc[nat, :] += dq_d[cm, :]
                        dk_acc[nat, :] += dk_d[cm, :]
                        dv_acc[nat, :] += dv_d[cm, :]
                        return c

                    lax.fori_loop(0, ln // PRO_ROWS, merge, 0)

        def back(dn_out, x, w, scale):
            r = lax.rsqrt(_head_sum2(x * x, ones_bd) * (1.0 / HEAD_DIM) + EPS)
            nrm = x * r
            dw = jnp.sum(dn_out * nrm, axis=0, keepdims=True) * scale
            dn = dn_out * (w * scale)
            return r * (dn - nrm * (_head_sum2(dn * nrm, ones_bd) * (1.0 / HEAD_DIM))), dw

        def epi(j, c):
            rows = pl.ds(pl.multiple_of(j * PRO_ROWS, PRO_ROWS), PRO_ROWS)
            dq, dqw = back(dq_acc[rows, :], q_ref[rows, :], qw_ref[...], HEAD_DIM ** -0.5)
            dk, dkw = back(dk_acc[rows, :], k_ref[rows, :], kw_ref[...], 1.0)
            dq_ref[rows, :] = dq.astype(BF16)
            dk_ref[rows, :] = dk.astype(BF16)
            dv_ref[rows, :] = dv_acc[rows, :].astype(BF16)
            return (c[0] + dqw, c[1] + dkw)

        zrow = jnp.zeros((1, LANE), F32)
        dqw, dkw = lax.fori_loop(0, s // PRO_ROWS, epi, (zrow, zrow))
        dqw_ref[...] = dqw
        dkw_ref[...] = dkw

    o = jax.ShapeDtypeStruct((s, ATT_D), BF16)
    ov = jax.ShapeDtypeStruct((1, ATT_D), F32)
    scr = pltpu.VMEM((s, LANE), F32)
    scb = pltpu.VMEM((s, LANE), BF16)
    return _pcall(
        body, (proj, proj, proj, do, stats, qw, kw), name="att_bwd", grid=(ATT_D // LANE,),
        in_specs=[blk(OFF_Q), blk(OFF_K), blk(OFF_V), oblk, oblk, wspec, wspec],
        out_specs=[oblk, oblk, oblk, wspec, wspec], out_shape=[o, o, o, ov, ov],
        scratch_shapes=[scr, scr, scb, scb, scb, scb, scr, scr, scr, scr, scr, scr, scr, pltpu.VMEM((2, ATT_BLK, 2 * KEYS), F32)],
        sem=("parallel",), comm=comm)


def _att_norm_fwd(o, nw):
    s = o.shape[0]
    row = pl.BlockSpec((ROW_TILE, ATT_D), lambda i: (i, 0))
    vec = pl.BlockSpec((1, ATT_D), lambda i: (0, 0))

    def body(o_ref, nw_ref, y_ref):
        o = o_ref[...]
        r = lax.rsqrt(jnp.mean(o * o, axis=-1, keepdims=True) + EPS)
        y_ref[...] = (o * r * nw_ref[...]).astype(BF16)

    return pl.pallas_call(body, name="att_norm_fwd", grid=(s // ROW_TILE,), in_specs=[row, vec], out_specs=row,
                          out_shape=jax.ShapeDtypeStruct((s, ATT_D), BF16), compiler_params=_cparams(("parallel",)))(o, nw)


def _att_norm_bwd(dycat, o, lse, nw):
    s = o.shape[0]
    row = pl.BlockSpec((ROW_TILE, ATT_D), lambda i: (i, 0))
    vec = pl.BlockSpec((1, ATT_D), lambda i: (0, 0))

    def body(dy_ref, o_ref, lse_ref, nw_ref, do_ref, st_ref, dnw_ref):
        @pl.when(pl.program_id(0) == 0)
        def _():
            dnw_ref[...] = jnp.zeros_like(dnw_ref)

        o = o_ref[...]
        dy = dy_ref[...]
        r = lax.rsqrt(jnp.mean(o * o, axis=-1, keepdims=True) + EPS)
        nrm = o * r
        dnw_ref[...] += jnp.sum(dy * nrm, axis=0, keepdims=True)
        dn = dy * nw_ref[...]
        do = r * (dn - nrm * jnp.mean(dn * nrm, axis=-1, keepdims=True))
        do_ref[...] = do
        ones_bd = _head_mean_matrix().astype(BF16)
        prod = do * o
        delta = jnp.concatenate([_head_sum2(prod[:, j * LANE:(j + 1) * LANE], ones_bd) for j in range(ATT_D // LANE)], axis=1)
        lane = lax.broadcasted_iota(jnp.int32, (1, ATT_D), 1)
        st_ref[...] = jnp.where((lane & (HEAD_DIM - 1)) < HALF, lse_ref[...], delta)

    f = jax.ShapeDtypeStruct((s, ATT_D), F32)
    return pl.pallas_call(
        body, name="att_norm_bwd", grid=(s // ROW_TILE,),
        in_specs=[pl.BlockSpec((ROW_TILE, ATT_D), lambda i: (i, 1)), row, row, vec], out_specs=[row, row, vec],
        out_shape=[f, f, jax.ShapeDtypeStruct((1, ATT_D), F32)],
        compiler_params=_cparams(("arbitrary",)))(dycat, o, lse, nw)


def _ada_fwd(c_all, w_ada):
    def body(c_ref, w_ref, o_ref):
        cv = c_ref[...]
        o_ref[...] = _dot((cv * _sigmoid(cv)).astype(BF16), w_ref[...].astype(BF16), NN)

    return pl.pallas_call(body, name="ada_fwd", out_shape=jax.ShapeDtypeStruct((c_all.shape[0], w_ada.shape[1]), F32),
                          compiler_params=_cparams())(c_all, w_ada)


def _adamw_math(g, w, m, v):
    m_new = ADAM_B1 * m + (1.0 - ADAM_B1) * g
    v_new = ADAM_B2 * v + (1.0 - ADAM_B2) * (g * g)
    m_hat = m_new / (1.0 - ADAM_B1 ** ADAM_STEP)
    v_hat = v_new / (1.0 - ADAM_B2 ** ADAM_STEP)
    delta = -ADAM_LR * (m_hat / (jnp.sqrt(v_hat) + ADAM_EPS) + ADAM_WD * w)
    return delta, m_new, v_new


def _ada_bwd_adamw(c_all, dmod_cols, w, m, v):
    rows, cols = w.shape
    tr = 256
    blk = pl.BlockSpec((tr, cols), lambda i: (i, 0))

    def body(c_ref, d_ref, w_ref, m_ref, v_ref, g_ref, dl_ref, mo_ref, vo_ref):
        cv = c_ref[...]
        ca = cv * _sigmoid(cv)
        g = ca[:, 0:1] * d_ref[0:1, :]
        for b in range(1, N_DEV):
            g = g + ca[:, b:b + 1] * d_ref[b:b + 1, :]
        g_ref[...] = g
        dl_ref[...], mo_ref[...], vo_ref[...] = _adamw_math(g, w_ref[...], m_ref[...], v_ref[...])

    o = jax.ShapeDtypeStruct((rows, cols), F32)
    return pl.pallas_call(
        body, name="ada_bwd_adamw", grid=(rows // tr,),
        in_specs=[pl.BlockSpec((tr, N_DEV), lambda i: (i, 0)), pl.BlockSpec((N_DEV, cols), lambda i: (0, 0)), blk, blk, blk],
        out_specs=[blk] * 4, out_shape=[o, o, o, o], compiler_params=_cparams(("parallel",)))(c_all.T, dmod_cols, w, m, v)


def _reduce_adamw(slabs, w, m, v, name):
    rows, cols = w.shape
    if rows % 128 == 0:
        tr, steps = 128, rows // 128
        blk = pl.BlockSpec((tr, cols), lambda i: (i, 0))
        sblk = pl.BlockSpec((N_DEV, tr, cols), lambda i: (0, i, 0))
    else:
        tc, steps = 256, cols // 256
        blk = pl.BlockSpec((rows, tc), lambda i: (0, i))
        sblk = pl.BlockSpec((N_DEV, rows, tc), lambda i: (0, 0, i))

    def body(s_ref, w_ref, m_ref, v_ref, g_ref, dl_ref, mo_ref, vo_ref):
        g = s_ref[0].astype(F32)
        for dev in range(1, N_DEV):
            g = g + s_ref[dev].astype(F32)
        g_ref[...] = g
        dl_ref[...], mo_ref[...], vo_ref[...] = _adamw_math(g, w_ref[...], m_ref[...], v_ref[...])

    o = jax.ShapeDtypeStruct((rows, cols), F32)
    return pl.pallas_call(
        body, name=name, grid=(steps,), in_specs=[sblk, blk, blk, blk],
        out_specs=[blk] * 4, out_shape=[o, o, o, o], compiler_params=_cparams(("parallel",)))(slabs, w, m, v)


def _small_reduce_adamw(gathered, w, m, v):
    def body(s_ref, w_ref, m_ref, v_ref, g_ref, dl_ref, mo_ref, vo_ref):
        g = s_ref[0]
        for dev in range(1, N_DEV):
            g = g + s_ref[dev]
        g_ref[...] = g
        dl_ref[...], mo_ref[...], vo_ref[...] = _adamw_math(g, w_ref[...], m_ref[...], v_ref[...])

    o = jax.ShapeDtypeStruct(w.shape, F32)
    return pl.pallas_call(body, name="small_reduce_adamw", out_shape=[o, o, o, o], compiler_params=_cparams())(gathered, w, m, v)


def _adamw_small(g, w, m, v, name):
    def body(g_ref, w_ref, m_ref, v_ref, dl_ref, mo_ref, vo_ref):
        dl_ref[...], mo_ref[...], vo_ref[...] = _adamw_math(g_ref[...], w_ref[...], m_ref[...], v_ref[...])

    o = jax.ShapeDtypeStruct(w.shape, F32)
    return pl.pallas_call(body, name=name, out_shape=[o, o, o], compiler_params=_cparams())(g, w, m, v)


class _Exchange:
    def __init__(self, arrs, scatter):
        self.arrs, self.scatter, self.n = list(arrs), scatter, len(arrs)
        hbm = pl.BlockSpec(memory_space=pltpu.HBM)
        self.in_specs = [hbm] * self.n
        self.out_specs = [hbm] * self.n
        self.out_shape = [jax.ShapeDtypeStruct(a.shape if scatter else (N_DEV,) + a.shape, a.dtype) for a in self.arrs]
        self.scratch = [pltpu.SemaphoreType.DMA((self.n * (N_DEV - 1),)), pltpu.SemaphoreType.DMA((self.n * (N_DEV - 1),)),
                        pltpu.SemaphoreType.DMA((self.n,))]

    def _local(self, ins, outs, sems):
        me = 4 * lax.axis_index("x") + 2 * lax.axis_index("y") + lax.axis_index("c")
        return [pltpu.make_async_copy(ins[a].at[me] if self.scatter else ins[a], outs[a].at[me], sems[2].at[a])
                for a in range(self.n)]

    def _remote(self, ins, outs, sems, arriving):
        send_sems, recv_sems, _ = sems
        x, y, c = lax.axis_index("x"), lax.axis_index("y"), lax.axis_index("c")
        me = 4 * x + 2 * y + c
        remote = []
        for a in range(self.n):
            for k in range(1, N_DEV):
                px = 1 - x if k & 4 else x
                py = 1 - y if k & 2 else y
                pc = 1 - c if k & 1 else c
                peer = 4 * px + 2 * py + pc
                sem = a * (N_DEV - 1) + k - 1
                remote.append(pltpu.make_async_remote_copy(
                    src_ref=ins[a].at[peer] if self.scatter else ins[a], dst_ref=outs[a].at[peer if arriving else me],
                    send_sem=send_sems.at[sem], recv_sem=recv_sems.at[sem], device_id=(px, py, pc), device_id_type=MESH_IDS))
        return remote

    def start(self, ins, outs, sems):
        for cp in self._local(ins, outs, sems) + self._remote(ins, outs, sems, arriving=False):
            cp.start()

    def wait(self, ins, outs, sems):
        for send, arrival in zip(self._remote(ins, outs, sems, arriving=False), self._remote(ins, outs, sems, arriving=True)):
            send.wait_send()
            arrival.wait_recv()
        for cp in self._local(ins, outs, sems):
            cp.wait()


def _split_comm_refs(refs, n_in, n_out, n_scr, comm):
    nc = comm.n if comm is not None else 0
    ns = 3 if comm is not None else 0
    pos, groups = 0, []
    for cnt in (n_in, nc, n_out, nc, n_scr, ns):
        groups.append(refs[pos:pos + cnt])
        pos += cnt
    assert pos == len(refs), (pos, len(refs))
    return groups


def _pcall(body, args, *, name, grid, in_specs, out_specs, out_shape, scratch_shapes=(), sem=None, comm=None):
    in_specs, out_specs, out_shape, scratch_shapes = list(in_specs), list(out_specs), list(out_shape), list(scratch_shapes)
    n_in, n_out, n_scr = len(in_specs), len(out_specs), len(scratch_shapes)
    if comm is None:
        kernel_body = body
    else:
        def kernel_body(*refs):
            ins, cins, outs, couts, scr, sems = _split_comm_refs(refs, n_in, n_out, n_scr, comm)
            ids = [pl.program_id(a) for a in range(len(grid))]
            first, last = ids[0] == 0, ids[0] == grid[0] - 1
            for a in range(1, len(grid)):
                first, last = first & (ids[a] == 0), last & (ids[a] == grid[a] - 1)

            @pl.when(first)
            def _():
                comm.start(cins, couts, sems)

            body(*ins, *outs, *scr)

            @pl.when(last)
            def _():
                comm.wait(cins, couts, sems)

        in_specs, out_specs, out_shape = in_specs + comm.in_specs, out_specs + comm.out_specs, out_shape + comm.out_shape
        scratch_shapes, args = scratch_shapes + comm.scratch, list(args) + comm.arrs
        sem = ("arbitrary",) * len(grid)
    res = pl.pallas_call(kernel_body, name=name, grid=grid, in_specs=in_specs, out_specs=out_specs, out_shape=out_shape,
                         scratch_shapes=scratch_shapes, compiler_params=_cparams(sem))(*args)
    return res[:n_out], res[n_out:]


def _exchange(arrs, name, scatter):
    ex = _Exchange(arrs, scatter)

    def body(*refs):
        _, ins, _, outs, _, sems = _split_comm_refs(refs, 0, 0, 0, ex)
        ex.start(ins, outs, sems)
        ex.wait(ins, outs, sems)

    return pl.pallas_call(body, name=name, in_specs=ex.in_specs, out_specs=ex.out_specs, out_shape=ex.out_shape,
                          scratch_shapes=ex.scratch)(*arrs)


def _pad_lanes(v, width=LANE):
    return jnp.pad(v, ((0, 0), (0, width - v.shape[1])))


def _shards_to_cols(g):
    return jnp.transpose(g, (1, 0, 2)).reshape(g.shape[1], N_DEV * g.shape[2])


def _cols_to_shards(w):
    return w.astype(BF16).reshape(w.shape[0], N_DEV, w.shape[1] // N_DEV).transpose(1, 0, 2)


def _local_step(x, tgt, mod, w_in_pt, conv_w, conv_b, dt_bias, a_log, d_skip, ssd_norm_w, q_norm_w, k_norm_w,
                attn_norm_w, w_out_sh, w_ff1_sh, w_ff2_sh, norm1_w, norm2_w):
    shift1, scale1, gate1, shift2, scale2, gate2 = [mod[i:i + 1] for i in range(N_MOD)]
    dtb, alog, dsk = _pad_lanes(dt_bias), _pad_lanes(a_log), _pad_lanes(d_skip)
    qw, kw = jnp.tile(q_norm_w, (1, ATT_HEADS)), jnp.tile(k_norm_w, (1, ATT_HEADS))

    h1 = _norm_mod_fwd(x, norm1_w, scale1, shift1, "norm1_fwd")
    proj = _matmul(h1, w_in_pt, tb=True, tm=2048, tn=896, tk=1024, name="in_proj")
    pre, act = _conv_fwd(proj, conv_w, conv_b)
    ypre, y_ssd, hall = _ssd_fwd(proj, act, dtb, alog, dsk, ssd_norm_w)
    (o_att, lse), (w_out_g, w_ff1_g, w_ff2_g) = _att_fwd(proj, qw, kw, comm=_Exchange([w_out_sh, w_ff1_sh, w_ff2_sh], scatter=False))
    w_out = w_out_g.reshape(2 * D_MODEL, D_MODEL)
    w_ff1 = _shards_to_cols(w_ff1_g)
    w_ff2 = w_ff2_g.reshape(D_FF, D_MODEL)
    y_att = _att_norm_fwd(o_att, attn_norm_w)
    ycat = jnp.concatenate([y_ssd, y_att], axis=1)
    mix = _matmul(ycat, w_out, tm=1024, tn=1024, tk=2048, name="out_proj")
    x1, h2 = _norm_mod_fwd(x, norm2_w, scale2, shift2, "norm2_fwd", res=mix, gate=gate1)
    u, act_ff = _matmul(h2, w_ff1, tm=1024, tn=1024, tk=1024, name="ff1", mode="relu2")
    ff = _matmul(act_ff, w_ff2, tm=512, tn=1024, tk=4096, name="ff2")
    loss, dout, dff, dgate2 = _loss_head(x1, ff, gate2, tgt)

    du = _matmul(dff, w_ff2, tb=True, tm=1024, tn=1024, tk=1024, out_dtype=BF16, name="ff2_dx", mode="drelu2", u=u)
    g_ff2 = _matmul(act_ff, dff, ta=True, tm=512, tn=1024, tk=4096, out_dtype=BF16, name="ff2_dw")
    dh2 = _matmul(du, w_ff1, tb=True, tm=512, tn=1024, tk=4096, name="ff1_dx")
    g_ff1 = _matmul(h2, du, ta=True, tm=512, tn=1024, tk=4096, out_dtype=BF16, name="ff1_dw")
    dx1, dshift2, dscale2, g_norm2, dmix, dgate1 = _norm_mod_bwd(dh2, x1, dout, norm2_w, scale2, "norm2_bwd", gate=gate1, mix=mix)

    dycat = _matmul(dmix, w_out, tb=True, tm=1024, tn=1024, tk=1024, name="out_proj_dx")
    g_out = _matmul(ycat, dmix, ta=True, tm=512, tn=1024, tk=4096, out_dtype=BF16, name="out_proj_dw")
    do, stats, g_attn_norm = _att_norm_bwd(dycat, o_att, lse, attn_norm_w)
    ff_slabs = [_cols_to_shards(g_ff1), g_ff2.astype(BF16).reshape(N_DEV, D_FF // N_DEV, D_MODEL)]
    (dq, dk, dv, dqw, dkw), (s_ff1, s_ff2) = _att_bwd(proj, do, stats, qw, kw, comm=_Exchange(ff_slabs, scatter=True))
    out_slabs = [g_out.astype(BF16).reshape(N_DEV, 2 * D_MODEL // N_DEV, D_MODEL)]
    (dz, dact, ddtr, da, g_dsk, g_dtb, g_ssd_norm), (s_out,) = _ssd_bwd(
        dycat, ypre, proj, act, hall, dtb, alog, dsk, ssd_norm_w, comm=_Exchange(out_slabs, scatter=True))
    dxbc, g_conv_w, g_conv_b = _conv_bwd(dact, pre, proj, conv_w)
    dproj = jnp.concatenate([dz, dxbc, dq, dk, dv, ddtr], axis=1)
    g_in_pt = _matmul(dproj, h1, ta=True, tm=896, tn=1024, tk=4096, out_dtype=BF16, name="in_proj_dw")
    in_slabs = [_unpack_w_in_rows(g_in_pt).reshape(N_DEV, IN_W // N_DEV, D_MODEL)]
    dh1, (s_in,) = _matmul(dproj, w_in_pt, tm=512, tn=1024, tk=IN_WP, name="in_proj_dx",
                           comm=_Exchange(in_slabs, scatter=True))
    grad_x, dshift1, dscale1, g_norm1 = _norm_mod_bwd(dh1, x, dx1, norm1_w, scale1, "norm1_bwd")

    dmod = jnp.concatenate([dshift1, dscale1, dgate1, dshift2, dscale2, dgate2], axis=0)
    g_alog = da[:, :SSD_HEADS] * (-jnp.exp(a_log))
    g_qw = dqw.reshape(ATT_HEADS, HEAD_DIM).sum(axis=0, keepdims=True)
    g_kw = dkw.reshape(ATT_HEADS, HEAD_DIM).sum(axis=0, keepdims=True)
    return dict(loss=loss, grad_x=grad_x, dmod=dmod, norm1_w=g_norm1, norm2_w=g_norm2, w_in=s_in, conv_w=g_conv_w,
                conv_b=g_conv_b, dt_bias=g_dtb[:, :SSD_HEADS], a_log=g_alog, d_skip=g_dsk[:, :SSD_HEADS],
                ssd_norm_w=g_ssd_norm, q_norm_w=g_qw, k_norm_w=g_kw, attn_norm_w=g_attn_norm, w_out=s_out,
                w_ff1=s_ff1, w_ff2=s_ff2)


def _pack_w_in_rows(wt_full):
    o_dt = SSD_D_INNER + CONV_CH
    o_q = o_dt + SSD_HEADS
    pad = jnp.zeros((LANE - SSD_HEADS, wt_full.shape[1]), wt_full.dtype)
    return jnp.concatenate([wt_full[:o_dt], wt_full[o_q:], wt_full[o_dt:o_q], pad], axis=0)


def _unpack_w_in_rows(gt_p):
    return jnp.concatenate([gt_p[:OFF_Q], gt_p[OFF_DT:OFF_DT + SSD_HEADS], gt_p[OFF_Q:OFF_DT]], axis=0)


MISC_FIELDS = (("dt_bias", SSD_HEADS), ("a_log", SSD_HEADS), ("d_skip", SSD_HEADS), ("q_norm_w", HEAD_DIM), ("k_norm_w", HEAD_DIM))
SMALL_LAYOUT = (("b_ada", 6), ("norm1_w", 1), ("norm2_w", 1), ("conv_w", 8), ("conv_b", 2), ("ssd_norm_w", 1),
                ("attn_norm_w", 1), ("misc", 1))


def _pack_small(vals):
    rows = []
    for name, nrow in SMALL_LAYOUT:
        if name == "misc":
            misc = jnp.concatenate([vals[f].reshape(1, n) for f, n in MISC_FIELDS], axis=1)
            rows.append(_pad_lanes(misc, D_MODEL))
        elif name in vals:
            rows.append(vals[name].reshape(nrow, D_MODEL))
        else:
            rows.append(jnp.zeros((nrow, D_MODEL), F32))
    used = sum(n for _, n in SMALL_LAYOUT)
    rows.append(jnp.zeros((SMALL_ROWS - used, D_MODEL), F32))
    return jnp.concatenate(rows, axis=0)


def _unpack_small(packed):
    out, r = {}, 0
    for name, nrow in SMALL_LAYOUT:
        blk = packed[r:r + nrow]
        r += nrow
        if name == "misc":
            c0 = 0
            for f, n in MISC_FIELDS:
                out[f] = blk[:, c0:c0 + n]
                c0 += n
        elif name == "b_ada":
            out[name] = blk.reshape(1, N_MOD * D_MODEL)
        elif name == "conv_w":
            out[name] = blk.reshape(CONV_K, CONV_CH)
        elif name == "conv_b":
            out[name] = blk.reshape(1, CONV_CH)
        else:
            out[name] = blk
    return out


WEIGHT_NAMES = ("norm1_w", "norm2_w", "w_ada", "b_ada", "w_in", "conv_w", "conv_b", "dt_bias", "a_log", "d_skip",
                "ssd_norm_w", "q_norm_w", "k_norm_w", "attn_norm_w", "w_out", "w_ff1", "w_ff2")
SMALL_NAMES = ("norm1_w", "norm2_w", "b_ada", "conv_b", "dt_bias", "a_log", "d_skip", "ssd_norm_w", "q_norm_w",
               "k_norm_w", "attn_norm_w")


def kernel(x, c, norm1_w, norm2_w, w_ada, b_ada, w_in, conv_w, conv_b, dt_bias, a_log, d_skip, ssd_norm_w, q_norm_w, k_norm_w, attn_norm_w, w_out, w_ff1, w_ff2, loss_target, m_norm1_w, m_norm2_w, m_w_ada, m_b_ada, m_w_in, m_conv_w, m_conv_b, m_dt_bias, m_a_log, m_d_skip, m_ssd_norm_w, m_q_norm_w, m_k_norm_w, m_attn_norm_w, m_w_out, m_w_ff1, m_w_ff2, v_norm1_w, v_norm2_w, v_w_ada, v_b_ada, v_w_in, v_conv_w, v_conv_b, v_dt_bias, v_a_log, v_d_skip, v_ssd_norm_w, v_q_norm_w, v_k_norm_w, v_attn_norm_w, v_w_out, v_w_ff1, v_w_ff2):
    args = dict(locals())
    w = {n: args[n] for n in WEIGHT_NAMES}
    m = {n: args["m_" + n] for n in WEIGHT_NAMES}
    v = {n: args["v_" + n] for n in WEIGHT_NAMES}
    me = 4 * lax.axis_index("x") + 2 * lax.axis_index("y") + lax.axis_index("c")

    c_rows = jnp.pad(c, ((0, 7), (0, 0)))
    w_in_t, m_in_t, v_in_t = [jnp.transpose(t["w_in"][0]) for t in (w, m, v)]
    c_g, conv_g, w_in_g = _exchange([c_rows, w["conv_w"][0], w_in_t.astype(BF16)], "gather_w_in", scatter=False)
    c_all = c_g[:, 0, :]
    conv_full = _shards_to_cols(conv_g)
    w_in_pt = _pack_w_in_rows(w_in_g.reshape(IN_W, D_MODEL))

    mod_part = _ada_fwd(c_all, w["w_ada"][0])
    (mod_g,) = _exchange([mod_part], "gather_mod", scatter=False)
    mod_mine = lax.dynamic_index_in_dim(mod_g, me, axis=1, keepdims=False).reshape(1, N_MOD * D_MODEL) + w["b_ada"]
    mod = mod_mine.reshape(N_MOD, D_MODEL)

    res = _local_step(x[0], loss_target[0], mod, w_in_pt, conv_full, w["conv_b"], w["dt_bias"], w["a_log"], w["d_skip"],
                      w["ssd_norm_w"], w["q_norm_w"], w["k_norm_w"], w["attn_norm_w"], w["w_out"][0].astype(BF16),
                      w["w_ff1"][0].astype(BF16), w["w_ff2"][0].astype(BF16), w["norm1_w"], w["norm2_w"])

    small_vals = {n: res[n] for n in SMALL_NAMES if n != "b_ada"}
    small_vals["b_ada"] = res["dmod"]
    small_vals["conv_w"] = res["conv_w"]
    (small_g,) = _exchange([_pack_small(small_vals)], "gather_small", scatter=False)

    grads, delta, new_m, new_v = {}, {}, {}, {}
    for name in ("w_out", "w_ff1", "w_ff2"):
        outs = _reduce_adamw(res[name], w[name][0], m[name][0], v[name][0], "adamw_" + name)
        grads[name], delta[name], new_m[name], new_v[name] = [o[None] for o in outs]
    outs = _reduce_adamw(res["w_in"], w_in_t, m_in_t, v_in_t, "adamw_w_in")
    grads["w_in"], delta["w_in"], new_m["w_in"], new_v["w_in"] = [jnp.transpose(o)[None] for o in outs]

    sm = _small_reduce_adamw(small_g, _pack_small({n: w[n] for n in SMALL_NAMES}), _pack_small({n: m[n] for n in SMALL_NAMES}),
                             _pack_small({n: v[n] for n in SMALL_NAMES}))
    sm = [_unpack_small(p) for p in sm]
    for n in SMALL_NAMES:
        grads[n], delta[n], new_m[n], new_v[n] = [p[n] for p in sm]
    shard_w = CONV_CH // N_DEV
    g_conv = lax.dynamic_slice_in_dim(sm[0]["conv_w"], me * shard_w, shard_w, axis=1)
    cw = _adamw_small(g_conv, w["conv_w"][0], m["conv_w"][0], v["conv_w"][0], "adamw_conv_w")
    grads["conv_w"] = g_conv[None]
    delta["conv_w"], new_m["conv_w"], new_v["conv_w"] = [o[None] for o in cw]

    ada_w = w_ada.shape[2]
    dmod_all = small_g[:, :N_MOD, :].reshape(N_DEV, N_MOD * D_MODEL)
    dmod_cols = lax.dynamic_slice_in_dim(dmod_all, me * ada_w, ada_w, axis=1)
    outs = _ada_bwd_adamw(c_all, dmod_cols, w["w_ada"][0], m["w_ada"][0], v["w_ada"][0])
    grads["w_ada"], delta["w_ada"], new_m["w_ada"], new_v["w_ada"] = [o[None] for o in outs]

    loss = lax.psum(res["loss"][0, 0], ("x", "y", "c"))
    return (loss, res["grad_x"][None], *[grads[n] for n in WEIGHT_NAMES], *[delta[n] for n in WEIGHT_NAMES],
            *[new_m[n] for n in WEIGHT_NAMES], *[new_v[n] for n in WEIGHT_NAMES])
```

```python
import functools

import jax
import jax.numpy as jnp
from jax import lax
from jax.experimental import pallas as pl
from jax.experimental.pallas import tpu as pltpu

F32 = jnp.float32
BF16 = jnp.bfloat16
HIGHEST = lax.Precision.HIGHEST
MESH_IDS = pl.DeviceIdType.MESH

N_DEV = 8
D_MODEL = 1024
HEAD_DIM = 64
SSD_HEADS = 16
SSD_GROUPS = 4
HEADS_PER_GROUP = SSD_HEADS // SSD_GROUPS
SSD_STATE = 128
SSD_CHUNK = 128
SSD_D_INNER = SSD_HEADS * HEAD_DIM
GROUP_WIDTH = SSD_D_INNER // SSD_GROUPS
CONV_K = 4
CONV_CH = SSD_D_INNER + 2 * SSD_GROUPS * SSD_STATE
ATT_HEADS = 16
ATT_D = ATT_HEADS * HEAD_DIM
ATT_BLK = 128
DILATIONS = (1, 4, 16)
D_FF = 4 * D_MODEL
N_MOD = 6
EPS = 1e-6
IN_W = SSD_D_INNER + CONV_CH + SSD_HEADS + 3 * ATT_D
LANE = 128
OFF_Z, OFF_XBC, OFF_Q, OFF_K, OFF_V, OFF_DT = 0, 1024, 3072, 4096, 5120, 6144
IN_WP = OFF_DT + LANE

ADAM_LR, ADAM_B1, ADAM_B2, ADAM_EPS, ADAM_WD, ADAM_STEP = 0.001, 0.9, 0.999, 1e-08, 0.01, 10
VMEM_LIMIT = 56 * 1024 * 1024
ROW_TILE = 512
SMALL_ROWS = 24


def _cparams(sem=None):
    return pltpu.CompilerParams(dimension_semantics=sem, vmem_limit_bytes=VMEM_LIMIT)


def _sigmoid(v):
    return 1.0 / (1.0 + jnp.exp(-v))


def _softplus(v):
    y = jnp.exp(-jnp.abs(v))
    small = y * (1.0 - y * (0.5 - y * (1.0 / 3.0)))
    return jnp.maximum(v, 0.0) + jnp.where(y < 0.01, small, jnp.log(1.0 + y))


def _dot(a, b, dims, precision=None):
    return lax.dot_general(a, b, (dims, ((), ())), preferred_element_type=F32, precision=precision)


NN = ((1,), (0,))
NT = ((1,), (1,))
TN = ((0,), (0,))


def _matmul(a, b, *, ta=False, tb=False, tm, tn, tk, out_dtype=F32, name, mode=None, u=None, comm=None):
    m, k = (a.shape[1], a.shape[0]) if ta else a.shape
    n = b.shape[0] if tb else b.shape[1]
    assert m % tm == 0 and n % tn == 0 and k % tk == 0, (name, m, n, k)
    nk = k // tk
    a_spec = pl.BlockSpec((tk, tm), lambda i, j, kk: (kk, i)) if ta else pl.BlockSpec((tm, tk), lambda i, j, kk: (i, kk))
    b_spec = pl.BlockSpec((tn, tk), lambda i, j, kk: (j, kk)) if tb else pl.BlockSpec((tk, tn), lambda i, j, kk: (kk, j))
    o_spec = pl.BlockSpec((tm, tn), lambda i, j, kk: (i, j))
    dims = ((0,) if ta else (1,), (1,) if tb else (0,))
    n_out = 2 if mode == "relu2" else 1

    def body(*refs):
        if mode == "drelu2":
            a_ref, b_ref, u_ref = refs[:3]
            rest = refs[3:]
        else:
            a_ref, b_ref = refs[:2]
            u_ref = None
            rest = refs[2:]
        outs = rest[:n_out]
        part = _dot(a_ref[...], b_ref[...], dims)

        def finish(r):
            if mode == "relu2":
                outs[0][...] = r.astype(BF16)
                rr = jnp.maximum(r, 0.0)
                outs[1][...] = (rr * rr).astype(BF16)
            elif mode == "drelu2":
                outs[0][...] = (r * (2.0 * jnp.maximum(u_ref[...].astype(F32), 0.0))).astype(out_dtype)
            else:
                outs[0][...] = r.astype(out_dtype)

        if nk == 1:
            finish(part)
        else:
            acc = rest[n_out]
            kk = pl.program_id(2)

            @pl.when(kk == 0)
            def _():
                acc[...] = part

            @pl.when(kk > 0)
            def _():
                acc[...] += part

            @pl.when(kk == nk - 1)
            def _():
                finish(acc[...])

    in_specs = [a_spec, b_spec]
    args = [a, b]
    if mode == "drelu2":
        in_specs.append(o_spec)
        args.append(u)
    if mode == "relu2":
        out_shape = [jax.ShapeDtypeStruct((m, n), BF16), jax.ShapeDtypeStruct((m, n), BF16)]
    else:
        out_shape = [jax.ShapeDtypeStruct((m, n), out_dtype)]
    outs, comm_outs = _pcall(
        body, args, name=name, grid=(m // tm, n // tn, nk), in_specs=in_specs, out_specs=[o_spec] * n_out,
        out_shape=out_shape, scratch_shapes=[pltpu.VMEM((tm, tn), F32)] if nk > 1 else [],
        sem=("parallel", "parallel", "arbitrary"), comm=comm)
    res = tuple(outs) if mode == "relu2" else outs[0]
    return res if comm is None else (res, comm_outs)


def _norm_mod_fwd(x, nw, scale, shift, name, res=None, gate=None):
    s, d = x.shape
    row = pl.BlockSpec((ROW_TILE, d), lambda i: (i, 0))
    vec = pl.BlockSpec((1, d), lambda i: (0, 0))
    with_res = res is not None

    def body(*refs):
        if with_res:
            x_ref, res_ref, gate_ref, nw_ref, sc_ref, sh_ref, x1_ref, h_ref = refs
            xv = x_ref[...] + gate_ref[...] * res_ref[...]
            x1_ref[...] = xv
        else:
            x_ref, nw_ref, sc_ref, sh_ref, h_ref = refs
            xv = x_ref[...]
        r = lax.rsqrt(jnp.mean(xv * xv, axis=-1, keepdims=True) + EPS)
        h_ref[...] = ((xv * r) * nw_ref[...] * (1.0 + sc_ref[...]) + sh_ref[...]).astype(BF16)

    if with_res:
        in_specs = [row, row, vec, vec, vec, vec]
        args = (x, res, gate, nw, scale, shift)
        out_shape = (jax.ShapeDtypeStruct((s, d), F32), jax.ShapeDtypeStruct((s, d), BF16))
        out_specs = (row, row)
    else:
        in_specs = [row, vec, vec, vec]
        args = (x, nw, scale, shift)
        out_shape = jax.ShapeDtypeStruct((s, d), BF16)
        out_specs = row
    return pl.pallas_call(body, name=name, grid=(s // ROW_TILE,), in_specs=in_specs, out_specs=out_specs,
                          out_shape=out_shape, compiler_params=_cparams(("parallel",)))(*args)


def _norm_mod_bwd(dh, xin, dres, nw, scale, name, gate=None, mix=None):
    s, d = xin.shape
    row = pl.BlockSpec((ROW_TILE, d), lambda i: (i, 0))
    vec = pl.BlockSpec((1, d), lambda i: (0, 0))
    with_gate = gate is not None

    def body(*refs):
        if with_gate:
            dh_ref, x_ref, dres_ref, nw_ref, sc_ref, gate_ref, mix_ref, dx_ref, dsh_ref, dsc_ref, dnw_ref, dmix_ref, dg_ref = refs
        else:
            dh_ref, x_ref, dres_ref, nw_ref, sc_ref, dx_ref, dsh_ref, dsc_ref, dnw_ref = refs
        i = pl.program_id(0)

        @pl.when(i == 0)
        def _():
            dsh_ref[...] = jnp.zeros_like(dsh_ref)
            dsc_ref[...] = jnp.zeros_like(dsc_ref)
            dnw_ref[...] = jnp.zeros_like(dnw_ref)
            if with_gate:
                dg_ref[...] = jnp.zeros_like(dg_ref)

        xv = x_ref[...]
        dhv = dh_ref[...]
        r = lax.rsqrt(jnp.mean(xv * xv, axis=-1, keepdims=True) + EPS)
        nrm = xv * r
        one_sc = 1.0 + sc_ref[...]
        dhn = dhv * nrm
        dsh_ref[...] += jnp.sum(dhv, axis=0, keepdims=True)
        dsc_ref[...] += jnp.sum(dhn, axis=0, keepdims=True) * nw_ref[...]
        dnw_ref[...] += jnp.sum(dhn, axis=0, keepdims=True) * one_sc
        dn = dhv * (nw_ref[...] * one_sc)
        dx = dres_ref[...] + r * (dn - nrm * jnp.mean(dn * nrm, axis=-1, keepdims=True))
        dx_ref[...] = dx
        if with_gate:
            dmix_ref[...] = (gate_ref[...] * dx).astype(BF16)
            dg_ref[...] += jnp.sum(dx * mix_ref[...], axis=0, keepdims=True)

    vshape = jax.ShapeDtypeStruct((1, d), F32)
    in_specs = [row, row, row, vec, vec]
    args = [dh, xin, dres, nw, scale]
    out_shape = [jax.ShapeDtypeStruct((s, d), F32), vshape, vshape, vshape]
    out_specs = [row, vec, vec, vec]
    if with_gate:
        in_specs += [vec, row]
        args += [gate, mix]
        out_shape += [jax.ShapeDtypeStruct((s, d), BF16), vshape]
        out_specs += [row, vec]
    return pl.pallas_call(body, name=name, grid=(s // ROW_TILE,), in_specs=in_specs, out_specs=out_specs,
                          out_shape=out_shape, compiler_params=_cparams(("arbitrary",)))(*args)


def _loss_head(x1, ff, gate2, tgt):
    s, d = x1.shape
    row = pl.BlockSpec((ROW_TILE, d), lambda i: (i, 0))
    vec = pl.BlockSpec((1, d), lambda i: (0, 0))
    one = pl.BlockSpec((1, 1), lambda i: (0, 0))

    def body(x1_ref, ff_ref, g_ref, t_ref, loss_ref, dout_ref, dff_ref, dg_ref):
        i = pl.program_id(0)

        @pl.when(i == 0)
        def _():
            loss_ref[...] = jnp.zeros_like(loss_ref)
            dg_ref[...] = jnp.zeros_like(dg_ref)

        ffv = ff_ref[...]
        err = x1_ref[...] + g_ref[...] * ffv - t_ref[...]
        loss_ref[...] += (0.5 / d) * jnp.sum(err * err).reshape(1, 1)
        dout = err * (1.0 / d)
        dout_ref[...] = dout
        dff_ref[...] = (g_ref[...] * dout).astype(BF16)
        dg_ref[...] += jnp.sum(dout * ffv, axis=0, keepdims=True)

    return pl.pallas_call(
        body, name="loss_head", grid=(s // ROW_TILE,), in_specs=[row, row, vec, row],
        out_specs=[one, row, row, vec],
        out_shape=[jax.ShapeDtypeStruct((1, 1), F32), jax.ShapeDtypeStruct((s, d), F32),
                   jax.ShapeDtypeStruct((s, d), BF16), jax.ShapeDtypeStruct((1, d), F32)],
        compiler_params=_cparams(("arbitrary",)))(x1, ff, gate2, tgt)


CONV_COLS = 256
HALO = 8


def _shift_down(cur, halo, k):
    if k == 0:
        return cur
    rolled = pltpu.roll(cur, k, axis=0)
    top = jnp.where(lax.broadcasted_iota(jnp.int32, halo.shape, 0) < k, pltpu.roll(halo, k, axis=0), rolled[:HALO])
    return jnp.concatenate([top, rolled[HALO:]], axis=0)


def _shift_up(cur, halo, k):
    if k == 0:
        return cur
    t = cur.shape[0]
    rolled = pltpu.roll(cur, t - k, axis=0)
    bot = jnp.where(lax.broadcasted_iota(jnp.int32, halo.shape, 0) >= HALO - k, pltpu.roll(halo, HALO - k, axis=0),
                    rolled[t - HALO:])
    return jnp.concatenate([rolled[:t - HALO], bot], axis=0)


def _conv_fwd(proj, conv_w, conv_b):
    s = proj.shape[0]
    nr = s // ROW_TILE
    cb0 = OFF_XBC // CONV_COLS
    hb = ROW_TILE // HALO
    cur = pl.BlockSpec((ROW_TILE, CONV_COLS), lambda j, r: (r, cb0 + j))
    prev = pl.BlockSpec((HALO, CONV_COLS), lambda j, r: (jnp.maximum(r * hb - 1, 0), cb0 + j))
    out = pl.BlockSpec((ROW_TILE, CONV_COLS), lambda j, r: (r, j))

    def body(u_ref, up_ref, w_ref, b_ref, pre_ref, act_ref):
        r = pl.program_id(1)
        u = u_ref[...]
        halo = jnp.where(r > 0, up_ref[...], 0.0)
        acc = b_ref[...] + w_ref[CONV_K - 1:CONV_K, :] * u
        for k in range(1, CONV_K):
            acc = acc + w_ref[CONV_K - 1 - k:CONV_K - k, :] * _shift_down(u, halo, k)
        pre_ref[...] = acc
        act_ref[...] = acc * _sigmoid(acc)

    return pl.pallas_call(
        body, name="conv_fwd", grid=(CONV_CH // CONV_COLS, nr),
        in_specs=[cur, prev, pl.BlockSpec((CONV_K, CONV_COLS), lambda j, r: (0, j)),
                  pl.BlockSpec((1, CONV_COLS), lambda j, r: (0, j))],
        out_specs=[out, out],
        out_shape=[jax.ShapeDtypeStruct((s, CONV_CH), F32), jax.ShapeDtypeStruct((s, CONV_CH), F32)],
        compiler_params=_cparams(("parallel", "arbitrary")))(proj, proj, conv_w, conv_b)


def _conv_bwd(dact, pre, proj, conv_w):
    s = proj.shape[0]
    nr = s // ROW_TILE
    cb0 = OFF_XBC // CONV_COLS
    hb = ROW_TILE // HALO
    last_halo = s // HALO - 1
    cur = pl.BlockSpec((ROW_TILE, CONV_COLS), lambda j, r: (r, j))
    nxt = pl.BlockSpec((HALO, CONV_COLS), lambda j, r: (jnp.minimum((r + 1) * hb, last_halo), j))
    ucur = pl.BlockSpec((ROW_TILE, CONV_COLS), lambda j, r: (r, cb0 + j))
    uprev = pl.BlockSpec((HALO, CONV_COLS), lambda j, r: (jnp.maximum(r * hb - 1, 0), cb0 + j))
    wspec = pl.BlockSpec((CONV_K, CONV_COLS), lambda j, r: (0, j))
    bspec = pl.BlockSpec((1, CONV_COLS), lambda j, r: (0, j))

    def dsilu(p):
        sg = _sigmoid(p)
        return sg * (1.0 + p * (1.0 - sg))

    def body(da_ref, dan_ref, pre_ref, pren_ref, u_ref, up_ref, w_ref, du_ref, dw_ref, db_ref):
        r = pl.program_id(1)

        @pl.when(r == 0)
        def _():
            dw_ref[...] = jnp.zeros_like(dw_ref)
            db_ref[...] = jnp.zeros_like(db_ref)

        dpre = da_ref[...] * dsilu(pre_ref[...])
        dnext = jnp.where(r < nr - 1, dan_ref[...] * dsilu(pren_ref[...]), 0.0)
        u = u_ref[...]
        halo = jnp.where(r > 0, up_ref[...], 0.0)
        du = w_ref[CONV_K - 1:CONV_K, :] * dpre
        dws = [jnp.sum(dpre * u, axis=0, keepdims=True)]
        for k in range(1, CONV_K):
            du = du + w_ref[CONV_K - 1 - k:CONV_K - k, :] * _shift_up(dpre, dnext, k)
            dws.append(jnp.sum(dpre * _shift_down(u, halo, k), axis=0, keepdims=True))
        du_ref[...] = du.astype(BF16)
        dw_ref[...] += jnp.concatenate(dws[::-1], axis=0)
        db_ref[...] += jnp.sum(dpre, axis=0, keepdims=True)

    return pl.pallas_call(
        body, name="conv_bwd", grid=(CONV_CH // CONV_COLS, nr),
        in_specs=[cur, nxt, cur, nxt, ucur, uprev, wspec],
        out_specs=[cur, wspec, bspec],
        out_shape=[jax.ShapeDtypeStruct((s, CONV_CH), BF16), jax.ShapeDtypeStruct((CONV_K, CONV_CH), F32),
                   jax.ShapeDtypeStruct((1, CONV_CH), F32)],
        compiler_params=_cparams(("parallel", "arbitrary")))(dact, dact, pre, pre, proj, proj, conv_w)


def _ssd_common(dtr, dtb, alog):
    lane = lax.broadcasted_iota(jnp.int32, (1, LANE), 1)
    head_lane = lane < SSD_HEADS
    dt = jnp.where(head_lane, _softplus(dtr + dtb), 0.0)
    a = jnp.where(head_lane, -jnp.exp(alog), 0.0)
    row = lax.broadcasted_iota(jnp.int32, (SSD_CHUNK, SSD_CHUNK), 0)
    col = lax.broadcasted_iota(jnp.int32, (SSD_CHUNK, SSD_CHUNK), 1)
    tril = row >= col
    cs = _dot(tril.astype(F32), dt * a, NN, precision=HIGHEST)
    return dt, a, cs, cs.T, tril, lane


def _split_bf16(v, passes):
    terms, rest = [], v
    for _ in range(passes):
        t = rest.astype(BF16)
        terms.append(t)
        rest = rest - t.astype(F32)
    return terms


def _dot_split(v, m, dims, passes):
    out = None
    for t in _split_bf16(v, passes):
        part = _dot(t, m, dims)
        out = part if out is None else out + part
    return out


def _ssd_constants():
    heads = jnp.arange(LANE)[:, None]
    exp_mat = (heads == (jnp.arange(SSD_D_INNER)[None, :] // HEAD_DIM)).astype(BF16)
    ind4 = ((jnp.arange(SSD_HEADS * SSD_CHUNK)[:, None] // SSD_CHUNK) == jnp.arange(LANE)[None, :]).astype(BF16)
    return exp_mat, ind4


def _expand_heads(v):
    return jnp.repeat(v[:, :SSD_HEADS], HEAD_DIM, axis=1)


def _ssd_prep(dtr, dtb, alog, exp_mat):
    dt, a, cs, cst, tril, lane = _ssd_common(dtr, dtb, alog)
    return dt, a, cs, cst, tril, lane, _dot_split(dt, exp_mat, NN, 2), _dot_split(cs, exp_mat, NN, 3)


def _chunk_decay_rows(cs, g):
    parts = []
    for e in range(HEADS_PER_GROUP):
        h = g * HEADS_PER_GROUP + e
        parts.append(jnp.broadcast_to(jnp.exp(cs[SSD_CHUNK - 1:SSD_CHUNK, h:h + 1]), (HEAD_DIM, SSD_STATE)))
    return jnp.concatenate(parts, axis=0)


def _ssd_fwd(proj, act, dtb, alog, dsk, nw):
    s = proj.shape[0]
    nc = s // SSD_CHUNK
    bc_w = SSD_GROUPS * SSD_STATE
    exp_mat, _ = _ssd_constants()

    def body(z_ref, dtr_ref, xs_ref, b_ref, c_ref, dtb_ref, alog_ref, dskx_ref, nw_ref, exp_ref,
             ypre_ref, yssd_ref, hall_ref, h_scr):
        @pl.when(pl.program_id(0) == 0)
        def _():
            h_scr[...] = jnp.zeros_like(h_scr)

        dt, a, cs, cst, tril, lane, dtx, csx = _ssd_prep(dtr_ref[...], dtb_ref[...], alog_ref[...], exp_ref[...])
        cs_last_x = csx[SSD_CHUNK - 1:SSD_CHUNK, :]
        xs = xs_ref[...]
        xdt = xs * dtx
        xdtb = xdt.astype(BF16)
        xdec = (xdt * jnp.exp(cs_last_x - csx)).astype(BF16)
        ecsx = jnp.exp(csx)
        head_of_lane = lax.broadcasted_iota(jnp.int32, (1, GROUP_WIDTH), 1) // HEAD_DIM
        for g in range(SSD_GROUPS):
            gs = slice(g * GROUP_WIDTH, (g + 1) * GROUP_WIDTH)
            bg = b_ref[:, g * SSD_STATE:(g + 1) * SSD_STATE].astype(BF16)
            cg = c_ref[:, g * SSD_STATE:(g + 1) * SSD_STATE].astype(BF16)
            cb = _dot(cg, bg, NT)
            hprev = h_scr[gs, :]
            hall_ref[0, gs, :] = hprev
            gms, rhs = [], []
            xg = xdtb[:, gs]
            for e in range(HEADS_PER_GROUP):
                h = g * HEADS_PER_GROUP + e
                lm = jnp.exp(jnp.where(tril, cs[:, h:h + 1] - cst[h:h + 1, :], -1e30))
                gms.append((cb * lm).astype(BF16))
                rhs.append(jnp.where(head_of_lane == e, xg, jnp.zeros_like(xg)))
            y = _dot(jnp.concatenate(gms, axis=1), jnp.concatenate(rhs, axis=0), NN)
            y = y + ecsx[:, gs] * _dot(cg, hprev.astype(BF16), NT)
            y = y + dskx_ref[:, gs] * xs[:, gs]
            h_scr[gs, :] = hprev * _chunk_decay_rows(cs, g) + _dot(xdec[:, gs], bg, TN)
            ypre_ref[:, gs] = y
            z = z_ref[:, gs]
            yg = y * (z * _sigmoid(z))
            r = lax.rsqrt(jnp.mean(yg * yg, axis=-1, keepdims=True) + EPS)
            yssd_ref[:, gs] = (yg * r * nw_ref[:, gs]).astype(BF16)

    row_d = lambda cb: pl.BlockSpec((SSD_CHUNK, SSD_D_INNER), lambda c: (c, cb))
    small = pl.BlockSpec((1, LANE), lambda c: (0, 0))
    wide = pl.BlockSpec((1, SSD_D_INNER), lambda c: (0, 0))
    return pl.pallas_call(
        body, name="ssd_fwd", grid=(nc,),
        in_specs=[row_d(OFF_Z // SSD_D_INNER),
                  pl.BlockSpec((SSD_CHUNK, LANE), lambda c: (c, OFF_DT // LANE)),
                  row_d(0),
                  pl.BlockSpec((SSD_CHUNK, bc_w), lambda c: (c, SSD_D_INNER // bc_w)),
                  pl.BlockSpec((SSD_CHUNK, bc_w), lambda c: (c, SSD_D_INNER // bc_w + 1)),
                  small, small, wide, wide, pl.BlockSpec((LANE, SSD_D_INNER), lambda c: (0, 0))],
        out_specs=[row_d(0), row_d(0), pl.BlockSpec((1, SSD_D_INNER, SSD_STATE), lambda c: (c, 0, 0))],
        out_shape=[jax.ShapeDtypeStruct((s, SSD_D_INNER), F32), jax.ShapeDtypeStruct((s, SSD_D_INNER), BF16),
                   jax.ShapeDtypeStruct((nc, SSD_D_INNER, SSD_STATE), F32)],
        scratch_shapes=[pltpu.VMEM((SSD_D_INNER, SSD_STATE), F32)],
        compiler_params=_cparams(("arbitrary",)))(proj, proj, act, act, act, dtb, alog, _expand_heads(dsk), nw, exp_mat)


def _ssd_bwd(dycat, ypre, proj, act, hall, dtb, alog, dsk, nw, comm=None):
    s = proj.shape[0]
    nc = s // SSD_CHUNK
    bc_w = SSD_GROUPS * SSD_STATE

    exp_mat, ind4 = _ssd_constants()
    seg_passes = 2

    def body(dy_ref, ypre_ref, z_ref, dtr_ref, xs_ref, b_ref, c_ref, hall_ref, dtb_ref, alog_ref, dskx_ref, nw_ref,
             exp_ref, ind4_ref, dz_ref, dact_ref, ddtr_ref, da_ref, ddsk_ref, ddtb_ref, dnw_ref, dh_scr):
        @pl.when(pl.program_id(0) == 0)
        def _():
            dh_scr[...] = jnp.zeros_like(dh_scr)
            da_ref[...] = jnp.zeros_like(da_ref)
            ddsk_ref[...] = jnp.zeros_like(ddsk_ref)
            ddtb_ref[...] = jnp.zeros_like(ddtb_ref)
            dnw_ref[...] = jnp.zeros_like(dnw_ref)

        dtr = dtr_ref[...]
        dt, a, cs, cst, tril, lane, dtx, csx = _ssd_prep(dtr, dtb_ref[...], alog_ref[...], exp_ref[...])
        cs_last_x = csx[SSD_CHUNK - 1:SSD_CHUNK, :]
        xs = xs_ref[...]
        xdt = xs * dtx
        xdtb = xdt.astype(BF16)
        decx = jnp.exp(cs_last_x - csx)
        xdecf = xdt * decx
        xdec = xdecf.astype(BF16)
        ecsx = jnp.exp(csx)
        head_of_lane = lax.broadcasted_iota(jnp.int32, (1, GROUP_WIDTH), 1) // HEAD_DIM
        last_row = lax.broadcasted_iota(jnp.int32, (SSD_CHUNK, 1), 0) == SSD_CHUNK - 1
        dcs_col = jnp.zeros((SSD_CHUNK, LANE), F32)
        dcs_row = jnp.zeros((SSD_CHUNK, LANE), F32)
        ddt = jnp.zeros((SSD_CHUNK, LANE), F32)
        ddsk = jnp.zeros((1, LANE), F32)
        hsum = jnp.zeros((1, LANE), F32)
        t1_sum = jnp.zeros((1, LANE), F32)
        for g in range(SSD_GROUPS):
            gs = slice(g * GROUP_WIDTH, (g + 1) * GROUP_WIDTH)
            bsl = slice(g * SSD_STATE, (g + 1) * SSD_STATE)
            exp_g = exp_ref[:, gs]
            ind4_g = ind4_ref[g * HEADS_PER_GROUP * SSD_CHUNK:(g + 1) * HEADS_PER_GROUP * SSD_CHUNK, :]
            z = z_ref[:, gs]
            sg = _sigmoid(z)
            sz = z * sg
            ypre = ypre_ref[:, gs]
            yg = ypre * sz
            r = lax.rsqrt(jnp.mean(yg * yg, axis=-1, keepdims=True) + EPS)
            nrm = yg * r
            dyo_n = dy_ref[:, gs]
            dnw_ref[:, gs] += jnp.sum(dyo_n * nrm, axis=0, keepdims=True)
            dn = dyo_n * nw_ref[:, gs]
            dyg = r * (dn - nrm * jnp.mean(dn * nrm, axis=-1, keepdims=True))
            dz_ref[:, gs] = (dyg * ypre * (sg * (1.0 + z * (1.0 - sg)))).astype(BF16)
            dy = dyg * sz

            bg = b_ref[:, bsl].astype(BF16)
            cg = c_ref[:, bsl].astype(BF16)
            cb = _dot(cg, bg, NT)
            hprev = hall_ref[0, gs, :]
            hb = hprev.astype(BF16)
            dhn = dh_scr[gs, :]
            dhb = dhn.astype(BF16)
            xs_g, xdt_g = xs[:, gs], xdtb[:, gs]
            w_off = _dot(cg, hb, NT)
            dyo = dy * ecsx[:, gs]
            dyob = dyo.astype(BF16)
            dcg = _dot(dyob, hb, NN)
            dh_y = _dot(dyob, cg, TN)
            r_st = _dot(bg, dhb, NT)
            dbg = _dot(xdec[:, gs], dhb, NN)
            dyb = dy.astype(BF16)
            gms, gmbs, lms, dys = [], [], [], []
            for e in range(HEADS_PER_GROUP):
                h = g * HEADS_PER_GROUP + e
                lm = jnp.exp(jnp.where(tril, cs[:, h:h + 1] - cst[h:h + 1, :], -1e30))
                gm = cb * lm
                lms.append(lm)
                gms.append(gm)
                gmbs.append(gm.astype(BF16))
                dys.append(jnp.where(head_of_lane == e, dyb, jnp.zeros_like(dyb)))
            dxdt = _dot(jnp.concatenate(gmbs, axis=0), jnp.concatenate(dys, axis=0), TN) + decx[:, gs] * r_st
            dcb = jnp.zeros((SSD_CHUNK, SSD_CHUNK), F32)
            mms = []
            for e in range(HEADS_PER_GROUP):
                dg = _dot(dys[e], xdt_g, NT)
                mms.append(dg * gms[e])
                dcb = dcb + dg * lms[e]
            seg = _dot_split(jnp.concatenate([dyo * w_off, xdecf[:, gs] * r_st, dxdt * xs_g, dy * xs_g], axis=0), exp_g, NT, seg_passes)
            v1, t1, ddt_g, dsk_g = [seg[i * SSD_CHUNK:(i + 1) * SSD_CHUNK] for i in range(4)]
            dcs_col = dcs_col + v1 - t1 + _dot_split(jnp.concatenate(mms, axis=1), ind4_g, NN, seg_passes)
            for t in _split_bf16(jnp.concatenate(mms, axis=0), seg_passes):
                dcs_row = dcs_row + _dot(ind4_g, t, TN)
            ddt = ddt + ddt_g
            ddsk = ddsk + jnp.sum(dsk_g, axis=0, keepdims=True)
            t1_sum = t1_sum + jnp.sum(t1, axis=0, keepdims=True)
            for e in range(HEADS_PER_GROUP):
                h = g * HEADS_PER_GROUP + e
                hs = slice(e * HEAD_DIM, (e + 1) * HEAD_DIM)
                hsum = hsum + jnp.where(lane == h, jnp.sum(dhn[hs, :] * hprev[hs, :]).reshape(1, 1), 0.0)
            dh_scr[gs, :] = dhn * _chunk_decay_rows(cs, g) + dh_y
            dcbb = dcb.astype(BF16)
            dact_ref[:, gs] = dxdt * dtx[:, gs] + dskx_ref[:, gs] * dy
            dact_ref[:, SSD_D_INNER + g * SSD_STATE:SSD_D_INNER + (g + 1) * SSD_STATE] = dbg + _dot(dcbb, cg, TN)
            dact_ref[:, SSD_D_INNER + bc_w + g * SSD_STATE:SSD_D_INNER + bc_w + (g + 1) * SSD_STATE] = dcg + _dot(dcbb, bg, NN)
        dlast = t1_sum + jnp.exp(cs[SSD_CHUNK - 1:SSD_CHUNK, :]) * hsum
        dcs = dcs_col - dcs_row.T + jnp.where(last_row, dlast, 0.0)
        row = lax.broadcasted_iota(jnp.int32, (SSD_CHUNK, SSD_CHUNK), 0)
        col = lax.broadcasted_iota(jnp.int32, (SSD_CHUNK, SSD_CHUNK), 1)
        dda = _dot((col >= row).astype(F32), dcs, NN, precision=HIGHEST)
        ddt = ddt + dda * a
        da_ref[...] += jnp.sum(dda * dt, axis=0, keepdims=True)
        ddtr = jnp.where(lane < SSD_HEADS, ddt * _sigmoid(dtr + dtb_ref[...]), 0.0)
        ddtr_ref[...] = ddtr.astype(BF16)
        ddtb_ref[...] += jnp.sum(ddtr, axis=0, keepdims=True)
        ddsk_ref[...] += ddsk

    rev = lambda c: nc - 1 - c
    row_d = lambda cb: pl.BlockSpec((SSD_CHUNK, SSD_D_INNER), lambda c: (rev(c), cb))
    small = pl.BlockSpec((1, LANE), lambda c: (0, 0))
    wide = pl.BlockSpec((1, SSD_D_INNER), lambda c: (0, 0))
    small_shape = jax.ShapeDtypeStruct((1, LANE), F32)
    return _pcall(
        body, (dycat, ypre, proj, proj, act, act, act, hall, dtb, alog, _expand_heads(dsk), nw, exp_mat, ind4),
        name="ssd_bwd", grid=(nc,),
        in_specs=[row_d(0), row_d(0), row_d(OFF_Z // SSD_D_INNER),
                  pl.BlockSpec((SSD_CHUNK, LANE), lambda c: (rev(c), OFF_DT // LANE)),
                  row_d(0),
                  pl.BlockSpec((SSD_CHUNK, bc_w), lambda c: (rev(c), SSD_D_INNER // bc_w)),
                  pl.BlockSpec((SSD_CHUNK, bc_w), lambda c: (rev(c), SSD_D_INNER // bc_w + 1)),
                  pl.BlockSpec((1, SSD_D_INNER, SSD_STATE), lambda c: (rev(c), 0, 0)),
                  small, small, wide, wide, pl.BlockSpec((LANE, SSD_D_INNER), lambda c: (0, 0)),
                  pl.BlockSpec((SSD_HEADS * SSD_CHUNK, LANE), lambda c: (0, 0))],
        out_specs=[row_d(0), pl.BlockSpec((SSD_CHUNK, CONV_CH), lambda c: (rev(c), 0)),
                   pl.BlockSpec((SSD_CHUNK, LANE), lambda c: (rev(c), 0)), small, small, small, wide],
        out_shape=[jax.ShapeDtypeStruct((s, SSD_D_INNER), BF16), jax.ShapeDtypeStruct((s, CONV_CH), F32),
                   jax.ShapeDtypeStruct((s, LANE), BF16), small_shape, small_shape, small_shape,
                   jax.ShapeDtypeStruct((1, SSD_D_INNER), F32)],
        scratch_shapes=[pltpu.VMEM((SSD_D_INNER, SSD_STATE), F32)], sem=("arbitrary",), comm=comm)


def _head_mean_matrix():
    row = lax.broadcasted_iota(jnp.int32, (LANE, LANE), 0) // HEAD_DIM
    col = lax.broadcasted_iota(jnp.int32, (LANE, LANE), 1) // HEAD_DIM
    return (row == col).astype(F32)


def _head_sum2(v, ones_bd):
    hi = v.astype(BF16)
    lo = (v - hi.astype(F32)).astype(BF16)
    return _dot(hi, ones_bd, NN) + _dot(lo, ones_bd, NN)


def _head_norm(x, w, scale, ones_bd):
    ms = _head_sum2(x * x, ones_bd) * (1.0 / HEAD_DIM)
    return (x * lax.rsqrt(ms + EPS)) * (w * scale)


PRO_ROWS = 256
ATT_GROUP_FWD = 8
ATT_GROUP_BWD = 4
KEYS = 2 * ATT_BLK
NEG = -1e30
HALF = HEAD_DIM // 2


def _rows(start, size, dil):
    return pl.ds(start, size) if dil == 1 else pl.ds(start, size, stride=dil)


def _fill_bias(bias_ref):
    row = lax.broadcasted_iota(jnp.int32, (ATT_BLK, 2 * KEYS), 0)
    col = lax.broadcasted_iota(jnp.int32, (ATT_BLK, 2 * KEYS), 1) & (KEYS - 1)
    for first, off in ((0, 0), (1, ATT_BLK)):
        dist = off + row - col
        bias_ref[first] = jnp.where((dist >= 0) & (dist <= ATT_BLK), 0.0, NEG)


def _pair(a, b):
    return jnp.concatenate([jnp.broadcast_to(a, (ATT_BLK, KEYS)), jnp.broadcast_to(b, (ATT_BLK, KEYS))], axis=1)


def _split_heads(x, is_a):
    zero = jnp.zeros_like(x)
    return jnp.concatenate([jnp.where(is_a, x, zero), jnp.where(is_a, zero, x)], axis=0)


def _block_ids(b, nb):
    i = b & (nb - 1)
    q0 = pl.multiple_of(b * ATT_BLK, ATT_BLK)
    k0 = pl.multiple_of((b - jnp.minimum(i, 1)) * ATT_BLK, ATT_BLK)
    return pl.ds(q0, ATT_BLK), pl.ds(k0, KEYS), jnp.minimum(i, 1)


def _att_fwd(proj, qw, kw, comm=None):
    s = proj.shape[0]
    nblk = s // ATT_BLK
    assert all((s // d) // ATT_BLK >= 2 for d in DILATIONS)
    blk = lambda off: pl.BlockSpec((s, LANE), lambda i: (0, off // LANE + i))
    wspec = pl.BlockSpec((1, LANE), lambda i: (0, i))
    oblk = pl.BlockSpec((s, LANE), lambda i: (0, i))

    def body(q_ref, k_ref, v_ref, qw_ref, kw_ref, o_ref, lse_ref, qn, kn, q_cm, k_cm, v_cm, m_acc, l_acc, o_d, m_d, l_d, bias):
        ones_bd = _head_mean_matrix().astype(BF16)
        is_a = lax.broadcasted_iota(jnp.int32, (1, LANE), 1) < HEAD_DIM
        ones_ext = _split_heads(jnp.ones((KEYS, LANE), BF16), is_a)
        _fill_bias(bias)

        def pro(j, c):
            rows = pl.ds(pl.multiple_of(j * PRO_ROWS, PRO_ROWS), PRO_ROWS)
            qn[rows, :] = _head_norm(q_ref[rows, :], qw_ref[...], HEAD_DIM ** -0.5, ones_bd)
            kn[rows, :] = _head_norm(k_ref[rows, :], kw_ref[...], 1.0, ones_bd)
            return c

        lax.fori_loop(0, s // PRO_ROWS, pro, 0)

        for dil in DILATIONS:
            ln = s // dil
            nb = ln // ATT_BLK
            o_out, m_out, l_out = (o_ref, m_acc, l_acc) if dil == 1 else (o_d, m_d, l_d)
            for r in range(dil):
                def relayout(j, c, dil=dil, r=r, ln=ln):
                    j0 = pl.multiple_of(j * PRO_ROWS, PRO_ROWS)
                    src = _rows(r + dil * j0, PRO_ROWS, dil)
                    dst = pl.ds(r * ln + j0, PRO_ROWS)
                    q_cm[dst, :] = qn[src, :].astype(BF16)
                    k_cm[dst, :] = kn[src, :].astype(BF16)
                    v_cm[dst, :] = v_ref[src, :].astype(BF16)
                    return c

                lax.fori_loop(0, ln // PRO_ROWS, relayout, 0)

            def step(bg, c, nb=nb, o_out=o_out, m_out=m_out, l_out=l_out):
                ids = [_block_ids(bg * ATT_GROUP_FWD + u, nb) for u in range(ATT_GROUP_FWD)]
                kbs = [_split_heads(k_cm[krows, :], is_a) for _, krows, _ in ids]
                scs = [_dot(q_cm[qrows, :], kb, NT) + bias[first] for (qrows, _, first), kb in zip(ids, kbs)]
                mas = [jnp.max(sc[:, :KEYS], axis=-1, keepdims=True) for sc in scs]
                mbs = [jnp.max(sc[:, KEYS:], axis=-1, keepdims=True) for sc in scs]
                ps = [jnp.exp(sc - _pair(ma, mb)).astype(BF16) for sc, ma, mb in zip(scs, mas, mbs)]
                vbs = [jnp.concatenate([_split_heads(v_cm[krows, :], is_a), ones_ext], axis=1) for _, krows, _ in ids]
                ols = [_dot(p, vb, NN) for p, vb in zip(ps, vbs)]
                for (qrows, _, _), ol, ma, mb in zip(ids, ols, mas, mbs):
                    o_out[qrows, :] = ol[:, :LANE]
                    l_out[qrows, :] = ol[:, LANE:]
                    m_out[qrows, :] = jnp.where(is_a, ma, mb)
                return c

            lax.fori_loop(0, nblk // ATT_GROUP_FWD, step, 0)

            if dil > 1:
                for r in range(dil):
                    def merge(j, c, dil=dil, r=r, ln=ln):
                        j0 = pl.multiple_of(j * PRO_ROWS, PRO_ROWS)
                        nat = _rows(r + dil * j0, PRO_ROWS, dil)
                        cm = pl.ds(r * ln + j0, PRO_ROWS)
                        m_old, m_new = m_acc[nat, :], m_d[cm, :]
                        m = jnp.maximum(m_old, m_new)
                        a_old, a_new = jnp.exp(m_old - m), jnp.exp(m_new - m)
                        o_ref[nat, :] = a_old * o_ref[nat, :] + a_new * o_d[cm, :]
                        l_acc[nat, :] = a_old * l_acc[nat, :] + a_new * l_d[cm, :]
                        m_acc[nat, :] = m
                        return c

                    lax.fori_loop(0, ln // PRO_ROWS, merge, 0)

        def epi(j, c):
            rows = pl.ds(pl.multiple_of(j * PRO_ROWS, PRO_ROWS), PRO_ROWS)
            l = l_acc[rows, :]
            o_ref[rows, :] = o_ref[rows, :] / l
            lse_ref[rows, :] = m_acc[rows, :] + jnp.log(l)
            return c

        lax.fori_loop(0, s // PRO_ROWS, epi, 0)

    f = jax.ShapeDtypeStruct((s, ATT_D), F32)
    scr = pltpu.VMEM((s, LANE), F32)
    scb = pltpu.VMEM((s, LANE), BF16)
    return _pcall(
        body, (proj, proj, proj, qw, kw), name="att_fwd", grid=(ATT_D // LANE,),
        in_specs=[blk(OFF_Q), blk(OFF_K), blk(OFF_V), wspec, wspec], out_specs=[oblk, oblk], out_shape=[f, f],
        scratch_shapes=[scr, scr, scb, scb, scb, scr, scr, scr, scr, scr, pltpu.VMEM((2, ATT_BLK, 2 * KEYS), F32)],
        sem=("parallel",), comm=comm)


def _att_bwd(proj, do, stats, qw, kw, comm=None):
    s = proj.shape[0]
    nblk = s // ATT_BLK
    blk = lambda off: pl.BlockSpec((s, LANE), lambda i: (0, off // LANE + i))
    wspec = pl.BlockSpec((1, LANE), lambda i: (0, i))
    oblk = pl.BlockSpec((s, LANE), lambda i: (0, i))

    def body(q_ref, k_ref, v_ref, do_ref, st_ref, qw_ref, kw_ref, dq_ref, dk_ref, dv_ref, dqw_ref, dkw_ref,
             qn, kn, q_cm, do_cm, k_cm, v_cm, st_cm, dq_acc, dk_acc, dv_acc, dq_d, dk_d, dv_d, bias):
        ones_bd = _head_mean_matrix().astype(BF16)
        is_a = lax.broadcasted_iota(jnp.int32, (1, LANE), 1) < HEAD_DIM
        _fill_bias(bias)
        zero = jnp.zeros((PRO_ROWS, LANE), F32)

        def pro(j, c):
            rows = pl.ds(pl.multiple_of(j * PRO_ROWS, PRO_ROWS), PRO_ROWS)
            qn[rows, :] = _head_norm(q_ref[rows, :], qw_ref[...], HEAD_DIM ** -0.5, ones_bd)
            kn[rows, :] = _head_norm(k_ref[rows, :], kw_ref[...], 1.0, ones_bd)
            dk_acc[rows, :] = zero
            dv_acc[rows, :] = zero
            return c

        lax.fori_loop(0, s // PRO_ROWS, pro, 0)

        for dil in DILATIONS:
            ln = s // dil
            nb = ln // ATT_BLK
            dq_o, dk_o, dv_o = (dq_acc, dk_acc, dv_acc) if dil == 1 else (dq_d, dk_d, dv_d)
            for r in range(dil):
                def relayout(j, c, dil=dil, r=r, ln=ln):
                    j0 = pl.multiple_of(j * PRO_ROWS, PRO_ROWS)
                    src = _rows(r + dil * j0, PRO_ROWS, dil)
                    dst = pl.ds(r * ln + j0, PRO_ROWS)
                    q_cm[dst, :] = qn[src, :].astype(BF16)
                    k_cm[dst, :] = kn[src, :].astype(BF16)
                    v_cm[dst, :] = v_ref[src, :].astype(BF16)
                    do_cm[dst, :] = do_ref[src, :].astype(BF16)
                    st_cm[dst, :] = st_ref[src, :]
                    if dil > 1:
                        dk_d[dst, :] = zero
                        dv_d[dst, :] = zero
                    return c

                lax.fori_loop(0, ln // PRO_ROWS, relayout, 0)

            def step(bg, c, nb=nb, dq_o=dq_o, dk_o=dk_o, dv_o=dv_o):
                ids = [_block_ids(bg * ATT_GROUP_BWD + u, nb) for u in range(ATT_GROUP_BWD)]
                qbs = [q_cm[qrows, :] for qrows, _, _ in ids]
                dobs = [do_cm[qrows, :] for qrows, _, _ in ids]
                kbs = [_split_heads(k_cm[krows, :], is_a) for _, krows, _ in ids]
                vbs = [_split_heads(v_cm[krows, :], is_a) for _, krows, _ in ids]
                sts = [st_cm[qrows, :] for qrows, _, _ in ids]
                scs = [_dot(qb, kb, NT) + bias[first] for qb, kb, (_, _, first) in zip(qbs, kbs, ids)]
                dps = [_dot(dob, vb, NT) for dob, vb in zip(dobs, vbs)]
                ps = [jnp.exp(sc - _pair(st[:, 0:1], st[:, HEAD_DIM:HEAD_DIM + 1])) for sc, st in zip(scs, sts)]
                dss = [(p * (dp - _pair(st[:, HALF:HALF + 1], st[:, HEAD_DIM + HALF:HEAD_DIM + HALF + 1]))).astype(BF16)
                       for p, dp, st in zip(ps, dps, sts)]
                dqs = [_dot(ds, kb, NN) for ds, kb in zip(dss, kbs)]
                dkfs = [_dot(ds, qb, TN) for ds, qb in zip(dss, qbs)]
                dvfs = [_dot(p.astype(BF16), dob, TN) for p, dob in zip(ps, dobs)]
                for (qrows, krows, _), dq, dkf, dvf in zip(ids, dqs, dkfs, dvfs):
                    dq_o[qrows, :] = dq
                    dk_o[krows, :] += jnp.where(is_a, dkf[:KEYS], dkf[KEYS:])
                    dv_o[krows, :] += jnp.where(is_a, dvf[:KEYS], dvf[KEYS:])
                return c

            lax.fori_loop(0, nblk // ATT_GROUP_BWD, step, 0)

            if dil > 1:
                for r in range(dil):
                    def merge(j, c, dil=dil, r=r, ln=ln):
                        j0 = pl.multiple_of(j * PRO_ROWS, PRO_ROWS)
                        nat = _rows(r + dil * j0, PRO_ROWS, dil)
                        cm = pl.ds(r * ln + j0, PRO_ROWS)
                        dq_acc[nat, :] += dq_d[cm, :]
                        dk_acc[nat, :] += dk_d[cm, :]
                        dv_acc[nat, :] += dv_d[cm, :]
                        return c

                    lax.fori_loop(0, ln // PRO_ROWS, merge, 0)

        def back(dn_out, x, w, scale):
            r = lax.rsqrt(_head_sum2(x * x, ones_bd) * (1.0 / HEAD_DIM) + EPS)
            nrm = x * r
            dw = jnp.sum(dn_out * nrm, axis=0, keepdims=True) * scale
            dn = dn_out * (w * scale)
            return r * (dn - nrm * (_head_sum2(dn * nrm, ones_bd) * (1.0 / HEAD_DIM))), dw

        def epi(j, c):
            rows = pl.ds(pl.multiple_of(j * PRO_ROWS, PRO_ROWS), PRO_ROWS)
            dq, dqw = back(dq_acc[rows, :], q_ref[rows, :], qw_ref[...], HEAD_DIM ** -0.5)
            dk, dkw = back(dk_acc[rows, :], k_ref[rows, :], kw_ref[...], 1.0)
            dq_ref[rows, :] = dq.astype(BF16)
            dk_ref[rows, :] = dk.astype(BF16)
            dv_ref[rows, :] = dv_acc[rows, :].astype(BF16)
            return (c[0] + dqw, c[1] + dkw)

        zrow = jnp.zeros((1, LANE), F32)
        dqw, dkw = lax.fori_loop(0, s // PRO_ROWS, epi, (zrow, zrow))
        dqw_ref[...] = dqw
        dkw_ref[...] = dkw

    o = jax.ShapeDtypeStruct((s, ATT_D), BF16)
    ov = jax.ShapeDtypeStruct((1, ATT_D), F32)
    scr = pltpu.VMEM((s, LANE), F32)
    scb = pltpu.VMEM((s, LANE), BF16)
    return _pcall(
        body, (proj, proj, proj, do, stats, qw, kw), name="att_bwd", grid=(ATT_D // LANE,),
        in_specs=[blk(OFF_Q), blk(OFF_K), blk(OFF_V), oblk, oblk, wspec, wspec],
        out_specs=[oblk, oblk, oblk, wspec, wspec], out_shape=[o, o, o, ov, ov],
        scratch_shapes=[scr, scr, scb, scb, scb, scb, scr, scr, scr, scr, scr, scr, scr, pltpu.VMEM((2, ATT_BLK, 2 * KEYS), F32)],
        sem=("parallel",), comm=comm)


def _att_norm_fwd(o, nw):
    s = o.shape[0]
    row = pl.BlockSpec((ROW_TILE, ATT_D), lambda i: (i, 0))
    vec = pl.BlockSpec((1, ATT_D), lambda i: (0, 0))

    def body(o_ref, nw_ref, y_ref):
        o = o_ref[...]
        r = lax.rsqrt(jnp.mean(o * o, axis=-1, keepdims=True) + EPS)
        y_ref[...] = (o * r * nw_ref[...]).astype(BF16)

    return pl.pallas_call(body, name="att_norm_fwd", grid=(s // ROW_TILE,), in_specs=[row, vec], out_specs=row,
                          out_shape=jax.ShapeDtypeStruct((s, ATT_D), BF16), compiler_params=_cparams(("parallel",)))(o, nw)


def _att_norm_bwd(dycat, o, lse, nw):
    s = o.shape[0]
    row = pl.BlockSpec((ROW_TILE, ATT_D), lambda i: (i, 0))
    vec = pl.BlockSpec((1, ATT_D), lambda i: (0, 0))

    def body(dy_ref, o_ref, lse_ref, nw_ref, do_ref, st_ref, dnw_ref):
        @pl.when(pl.program_id(0) == 0)
        def _():
            dnw_ref[...] = jnp.zeros_like(dnw_ref)

        o = o_ref[...]
        dy = dy_ref[...]
        r = lax.rsqrt(jnp.mean(o * o, axis=-1, keepdims=True) + EPS)
        nrm = o * r
        dnw_ref[...] += jnp.sum(dy * nrm, axis=0, keepdims=True)
        dn = dy * nw_ref[...]
        do = r * (dn - nrm * jnp.mean(dn * nrm, axis=-1, keepdims=True))
        do_ref[...] = do
        ones_bd = _head_mean_matrix().astype(BF16)
        prod = do * o
        delta = jnp.concatenate([_head_sum2(prod[:, j * LANE:(j + 1) * LANE], ones_bd) for j in range(ATT_D // LANE)], axis=1)
        lane = lax.broadcasted_iota(jnp.int32, (1, ATT_D), 1)
        st_ref[...] = jnp.where((lane & (HEAD_DIM - 1)) < HALF, lse_ref[...], delta)

    f = jax.ShapeDtypeStruct((s, ATT_D), F32)
    return pl.pallas_call(
        body, name="att_norm_bwd", grid=(s // ROW_TILE,),
        in_specs=[pl.BlockSpec((ROW_TILE, ATT_D), lambda i: (i, 1)), row, row, vec], out_specs=[row, row, vec],
        out_shape=[f, f, jax.ShapeDtypeStruct((1, ATT_D), F32)],
        compiler_params=_cparams(("arbitrary",)))(dycat, o, lse, nw)


def _ada_fwd(c_all, w_ada):
    def body(c_ref, w_ref, o_ref):
        cv = c_ref[...]
        o_ref[...] = _dot((cv * _sigmoid(cv)).astype(BF16), w_ref[...].astype(BF16), NN)

    return pl.pallas_call(body, name="ada_fwd", out_shape=jax.ShapeDtypeStruct((c_all.shape[0], w_ada.shape[1]), F32),
                          compiler_params=_cparams())(c_all, w_ada)


def _adamw_math(g, w, m, v):
    m_new = ADAM_B1 * m + (1.0 - ADAM_B1) * g
    v_new = ADAM_B2 * v + (1.0 - ADAM_B2) * (g * g)
    m_hat = m_new / (1.0 - ADAM_B1 ** ADAM_STEP)
    v_hat = v_new / (1.0 - ADAM_B2 ** ADAM_STEP)
    delta = -ADAM_LR * (m_hat / (jnp.sqrt(v_hat) + ADAM_EPS) + ADAM_WD * w)
    return delta, m_new, v_new


def _ada_bwd_adamw(c_all, dmod_cols, w, m, v):
    rows, cols = w.shape
    tr = 256
    blk = pl.BlockSpec((tr, cols), lambda i: (i, 0))

    def body(c_ref, d_ref, w_ref, m_ref, v_ref, g_ref, dl_ref, mo_ref, vo_ref):
        cv = c_ref[...]
        ca = cv * _sigmoid(cv)
        g = ca[:, 0:1] * d_ref[0:1, :]
        for b in range(1, N_DEV):
            g = g + ca[:, b:b + 1] * d_ref[b:b + 1, :]
        g_ref[...] = g
        dl_ref[...], mo_ref[...], vo_ref[...] = _adamw_math(g, w_ref[...], m_ref[...], v_ref[...])

    o = jax.ShapeDtypeStruct((rows, cols), F32)
    return pl.pallas_call(
        body, name="ada_bwd_adamw", grid=(rows // tr,),
        in_specs=[pl.BlockSpec((tr, N_DEV), lambda i: (i, 0)), pl.BlockSpec((N_DEV, cols), lambda i: (0, 0)), blk, blk, blk],
        out_specs=[blk] * 4, out_shape=[o, o, o, o], compiler_params=_cparams(("parallel",)))(c_all.T, dmod_cols, w, m, v)


def _reduce_adamw(slabs, w, m, v, name):
    rows, cols = w.shape
    if rows % 128 == 0:
        tr, steps = 128, rows // 128
        blk = pl.BlockSpec((tr, cols), lambda i: (i, 0))
        sblk = pl.BlockSpec((N_DEV, tr, cols), lambda i: (0, i, 0))
    else:
        tc, steps = 256, cols // 256
        blk = pl.BlockSpec((rows, tc), lambda i: (0, i))
        sblk = pl.BlockSpec((N_DEV, rows, tc), lambda i: (0, 0, i))

    def body(s_ref, w_ref, m_ref, v_ref, g_ref, dl_ref, mo_ref, vo_ref):
        g = s_ref[0].astype(F32)
        for dev in range(1, N_DEV):
            g = g + s_ref[dev].astype(F32)
        g_ref[...] = g
        dl_ref[...], mo_ref[...], vo_ref[...] = _adamw_math(g, w_ref[...], m_ref[...], v_ref[...])

    o = jax.ShapeDtypeStruct((rows, cols), F32)
    return pl.pallas_call(
        body, name=name, grid=(steps,), in_specs=[sblk, blk, blk, blk],
        out_specs=[blk] * 4, out_shape=[o, o, o, o], compiler_params=_cparams(("parallel",)))(slabs, w, m, v)


def _small_reduce_adamw(gathered, w, m, v):
    def body(s_ref, w_ref, m_ref, v_ref, g_ref, dl_ref, mo_ref, vo_ref):
        g = s_ref[0]
        for dev in range(1, N_DEV):
            g = g + s_ref[dev]
        g_ref[...] = g
        dl_ref[...], mo_ref[...], vo_ref[...] = _adamw_math(g, w_ref[...], m_ref[...], v_ref[...])

    o = jax.ShapeDtypeStruct(w.shape, F32)
    return pl.pallas_call(body, name="small_reduce_adamw", out_shape=[o, o, o, o], compiler_params=_cparams())(gathered, w, m, v)


def _adamw_small(g, w, m, v, name):
    def body(g_ref, w_ref, m_ref, v_ref, dl_ref, mo_ref, vo_ref):
        dl_ref[...], mo_ref[...], vo_ref[...] = _adamw_math(g_ref[...], w_ref[...], m_ref[...], v_ref[...])

    o = jax.ShapeDtypeStruct(w.shape, F32)
    return pl.pallas_call(body, name=name, out_shape=[o, o, o], compiler_params=_cparams())(g, w, m, v)


class _Exchange:
    def __init__(self, arrs, scatter):
        self.arrs, self.scatter, self.n = list(arrs), scatter, len(arrs)
        hbm = pl.BlockSpec(memory_space=pltpu.HBM)
        self.in_specs = [hbm] * self.n
        self.out_specs = [hbm] * self.n
        self.out_shape = [jax.ShapeDtypeStruct(a.shape if scatter else (N_DEV,) + a.shape, a.dtype) for a in self.arrs]
        self.scratch = [pltpu.SemaphoreType.DMA((self.n * (N_DEV - 1),)), pltpu.SemaphoreType.DMA((self.n * (N_DEV - 1),)),
                        pltpu.SemaphoreType.DMA((self.n,))]

    def _local(self, ins, outs, sems):
        me = 4 * lax.axis_index("x") + 2 * lax.axis_index("y") + lax.axis_index("c")
        return [pltpu.make_async_copy(ins[a].at[me] if self.scatter else ins[a], outs[a].at[me], sems[2].at[a])
                for a in range(self.n)]

    def _remote(self, ins, outs, sems, arriving):
        send_sems, recv_sems, _ = sems
        x, y, c = lax.axis_index("x"), lax.axis_index("y"), lax.axis_index("c")
        me = 4 * x + 2 * y + c
        remote = []
        for a in range(self.n):
            for k in range(1, N_DEV):
                px = 1 - x if k & 4 else x
                py = 1 - y if k & 2 else y
                pc = 1 - c if k & 1 else c
                peer = 4 * px + 2 * py + pc
                sem = a * (N_DEV - 1) + k - 1
                remote.append(pltpu.make_async_remote_copy(
                    src_ref=ins[a].at[peer] if self.scatter else ins[a], dst_ref=outs[a].at[peer if arriving else me],
                    send_sem=send_sems.at[sem], recv_sem=recv_sems.at[sem], device_id=(px, py, pc), device_id_type=MESH_IDS))
        return remote

    def start(self, ins, outs, sems):
        for cp in self._local(ins, outs, sems) + self._remote(ins, outs, sems, arriving=False):
            cp.start()

    def wait(self, ins, outs, sems):
        for send, arrival in zip(self._remote(ins, outs, sems, arriving=False), self._remote(ins, outs, sems, arriving=True)):
            send.wait_send()
            arrival.wait_recv()
        for cp in self._local(ins, outs, sems):
            cp.wait()


def _split_comm_refs(refs, n_in, n_out, n_scr, comm):
    nc = comm.n if comm is not None else 0
    ns = 3 if comm is not None else 0
    pos, groups = 0, []
    for cnt in (n_in, nc, n_out, nc, n_scr, ns):
        groups.append(refs[pos:pos + cnt])
        pos += cnt
    assert pos == len(refs), (pos, len(refs))
    return groups


def _pcall(body, args, *, name, grid, in_specs, out_specs, out_shape, scratch_shapes=(), sem=None, comm=None):
    in_specs, out_specs, out_shape, scratch_shapes = list(in_specs), list(out_specs), list(out_shape), list(scratch_shapes)
    n_in, n_out, n_scr = len(in_specs), len(out_specs), len(scratch_shapes)
    if comm is None:
        kernel_body = body
    else:
        def kernel_body(*refs):
            ins, cins, outs, couts, scr, sems = _split_comm_refs(refs, n_in, n_out, n_scr, comm)
            ids = [pl.program_id(a) for a in range(len(grid))]
            first, last = ids[0] == 0, ids[0] == grid[0] - 1
            for a in range(1, len(grid)):
                first, last = first & (ids[a] == 0), last & (ids[a] == grid[a] - 1)

            @pl.when(first)
            def _():
                comm.start(cins, couts, sems)

            body(*ins, *outs, *scr)

            @pl.when(last)
            def _():
                comm.wait(cins, couts, sems)

        in_specs, out_specs, out_shape = in_specs + comm.in_specs, out_specs + comm.out_specs, out_shape + comm.out_shape
        scratch_shapes, args = scratch_shapes + comm.scratch, list(args) + comm.arrs
        sem = ("arbitrary",) * len(grid)
    res = pl.pallas_call(kernel_body, name=name, grid=grid, in_specs=in_specs, out_specs=out_specs, out_shape=out_shape,
                         scratch_shapes=scratch_shapes, compiler_params=_cparams(sem))(*args)
    return res[:n_out], res[n_out:]


def _exchange(arrs, name, scatter):
    ex = _Exchange(arrs, scatter)

    def body(*refs):
        _, ins, _, outs, _, sems = _split_comm_refs(refs, 0, 0, 0, ex)
        ex.start(ins, outs, sems)
        ex.wait(ins, outs, sems)

    return pl.pallas_call(body, name=name, in_specs=ex.in_specs, out_specs=ex.out_specs, out_shape=ex.out_shape,
                          scratch_shapes=ex.scratch)(*arrs)


def _pad_lanes(v, width=LANE):
    return jnp.pad(v, ((0, 0), (0, width - v.shape[1])))


def _shards_to_cols(g):
    return jnp.transpose(g, (1, 0, 2)).reshape(g.shape[1], N_DEV * g.shape[2])


def _cols_to_shards(w):
    return w.astype(BF16).reshape(w.shape[0], N_DEV, w.shape[1] // N_DEV).transpose(1, 0, 2)


def _local_step(x, tgt, mod, w_in_pt, conv_w, conv_b, dt_bias, a_log, d_skip, ssd_norm_w, q_norm_w, k_norm_w,
                attn_norm_w, w_out_sh, w_ff1_sh, w_ff2_sh, norm1_w, norm2_w):
    shift1, scale1, gate1, shift2, scale2, gate2 = [mod[i:i + 1] for i in range(N_MOD)]
    dtb, alog, dsk = _pad_lanes(dt_bias), _pad_lanes(a_log), _pad_lanes(d_skip)
    qw, kw = jnp.tile(q_norm_w, (1, ATT_HEADS)), jnp.tile(k_norm_w, (1, ATT_HEADS))

    h1 = _norm_mod_fwd(x, norm1_w, scale1, shift1, "norm1_fwd")
    proj = _matmul(h1, w_in_pt, tb=True, tm=2048, tn=896, tk=1024, name="in_proj")
    pre, act = _conv_fwd(proj, conv_w, conv_b)
    ypre, y_ssd, hall = _ssd_fwd(proj, act, dtb, alog, dsk, ssd_norm_w)
    (o_att, lse), (w_out_g, w_ff1_g, w_ff2_g) = _att_fwd(proj, qw, kw, comm=_Exchange([w_out_sh, w_ff1_sh, w_ff2_sh], scatter=False))
    w_out = w_out_g.reshape(2 * D_MODEL, D_MODEL)
    w_ff1 = _shards_to_cols(w_ff1_g)
    w_ff2 = w_ff2_g.reshape(D_FF, D_MODEL)
    y_att = _att_norm_fwd(o_att, attn_norm_w)
    ycat = jnp.concatenate([y_ssd, y_att], axis=1)
    mix = _matmul(ycat, w_out, tm=1024, tn=1024, tk=2048, name="out_proj")
    x1, h2 = _norm_mod_fwd(x, norm2_w, scale2, shift2, "norm2_fwd", res=mix, gate=gate1)
    u, act_ff = _matmul(h2, w_ff1, tm=1024, tn=1024, tk=1024, name="ff1", mode="relu2")
    ff = _matmul(act_ff, w_ff2, tm=512, tn=1024, tk=4096, name="ff2")
    loss, dout, dff, dgate2 = _loss_head(x1, ff, gate2, tgt)

    du = _matmul(dff, w_ff2, tb=True, tm=1024, tn=1024, tk=1024, out_dtype=BF16, name="ff2_dx", mode="drelu2", u=u)
    g_ff2 = _matmul(act_ff, dff, ta=True, tm=512, tn=1024, tk=4096, out_dtype=BF16, name="ff2_dw")
    dh2 = _matmul(du, w_ff1, tb=True, tm=512, tn=1024, tk=4096, name="ff1_dx")
    g_ff1 = _matmul(h2, du, ta=True, tm=512, tn=1024, tk=4096, out_dtype=BF16, name="ff1_dw")
    dx1, dshift2, dscale2, g_norm2, dmix, dgate1 = _norm_mod_bwd(dh2, x1, dout, norm2_w, scale2, "norm2_bwd", gate=gate1, mix=mix)

    dycat = _matmul(dmix, w_out, tb=True, tm=1024, tn=1024, tk=1024, name="out_proj_dx")
    g_out = _matmul(ycat, dmix, ta=True, tm=512, tn=1024, tk=4096, out_dtype=BF16, name="out_proj_dw")
    do, stats, g_attn_norm = _att_norm_bwd(dycat, o_att, lse, attn_norm_w)
    ff_slabs = [_cols_to_shards(g_ff1), g_ff2.astype(BF16).reshape(N_DEV, D_FF // N_DEV, D_MODEL)]
    (dq, dk, dv, dqw, dkw), (s_ff1, s_ff2) = _att_bwd(proj, do, stats, qw, kw, comm=_Exchange(ff_slabs, scatter=True))
    out_slabs = [g_out.astype(BF16).reshape(N_DEV, 2 * D_MODEL // N_DEV, D_MODEL)]
    (dz, dact, ddtr, da, g_dsk, g_dtb, g_ssd_norm), (s_out,) = _ssd_bwd(
        dycat, ypre, proj, act, hall, dtb, alog, dsk, ssd_norm_w, comm=_Exchange(out_slabs, scatter=True))
    dxbc, g_conv_w, g_conv_b = _conv_bwd(dact, pre, proj, conv_w)
    dproj = jnp.concatenate([dz, dxbc, dq, dk, dv, ddtr], axis=1)
    g_in_pt = _matmul(dproj, h1, ta=True, tm=896, tn=1024, tk=4096, out_dtype=BF16, name="in_proj_dw")
    in_slabs = [_unpack_w_in_rows(g_in_pt).reshape(N_DEV, IN_W // N_DEV, D_MODEL)]
    dh1, (s_in,) = _matmul(dproj, w_in_pt, tm=512, tn=1024, tk=IN_WP, name="in_proj_dx",
                           comm=_Exchange(in_slabs, scatter=True))
    grad_x, dshift1, dscale1, g_norm1 = _norm_mod_bwd(dh1, x, dx1, norm1_w, scale1, "norm1_bwd")

    dmod = jnp.concatenate([dshift1, dscale1, dgate1, dshift2, dscale2, dgate2], axis=0)
    g_alog = da[:, :SSD_HEADS] * (-jnp.exp(a_log))
    g_qw = dqw.reshape(ATT_HEADS, HEAD_DIM).sum(axis=0, keepdims=True)
    g_kw = dkw.reshape(ATT_HEADS, HEAD_DIM).sum(axis=0, keepdims=True)
    return dict(loss=loss, grad_x=grad_x, dmod=dmod, norm1_w=g_norm1, norm2_w=g_norm2, w_in=s_in, conv_w=g_conv_w,
                conv_b=g_conv_b, dt_bias=g_dtb[:, :SSD_HEADS], a_log=g_alog, d_skip=g_dsk[:, :SSD_HEADS],
                ssd_norm_w=g_ssd_norm, q_norm_w=g_qw, k_norm_w=g_kw, attn_norm_w=g_attn_norm, w_out=s_out,
                w_ff1=s_ff1, w_ff2=s_ff2)


def _pack_w_in_rows(wt_full):
    o_dt = SSD_D_INNER + CONV_CH
    o_q = o_dt + SSD_HEADS
    pad = jnp.zeros((LANE - SSD_HEADS, wt_full.shape[1]), wt_full.dtype)
    return jnp.concatenate([wt_full[:o_dt], wt_full[o_q:], wt_full[o_dt:o_q], pad], axis=0)


def _unpack_w_in_rows(gt_p):
    return jnp.concatenate([gt_p[:OFF_Q], gt_p[OFF_DT:OFF_DT + SSD_HEADS], gt_p[OFF_Q:OFF_DT]], axis=0)


MISC_FIELDS = (("dt_bias", SSD_HEADS), ("a_log", SSD_HEADS), ("d_skip", SSD_HEADS), ("q_norm_w", HEAD_DIM), ("k_norm_w", HEAD_DIM))
SMALL_LAYOUT = (("b_ada", 6), ("norm1_w", 1), ("norm2_w", 1), ("conv_w", 8), ("conv_b", 2), ("ssd_norm_w", 1),
                ("attn_norm_w", 1), ("misc", 1))


def _pack_small(vals):
    rows = []
    for name, nrow in SMALL_LAYOUT:
        if name == "misc":
            misc = jnp.concatenate([vals[f].reshape(1, n) for f, n in MISC_FIELDS], axis=1)
            rows.append(_pad_lanes(misc, D_MODEL))
        elif name in vals:
            rows.append(vals[name].reshape(nrow, D_MODEL))
        else:
            rows.append(jnp.zeros((nrow, D_MODEL), F32))
    used = sum(n for _, n in SMALL_LAYOUT)
    rows.append(jnp.zeros((SMALL_ROWS - used, D_MODEL), F32))
    return jnp.concatenate(rows, axis=0)


def _unpack_small(packed):
    out, r = {}, 0
    for name, nrow in SMALL_LAYOUT:
        blk = packed[r:r + nrow]
        r += nrow
        if name == "misc":
            c0 = 0
            for f, n in MISC_FIELDS:
                out[f] = blk[:, c0:c0 + n]
                c0 += n
        elif name == "b_ada":
            out[name] = blk.reshape(1, N_MOD * D_MODEL)
        elif name == "conv_w":
            out[name] = blk.reshape(CONV_K, CONV_CH)
        elif name == "conv_b":
            out[name] = blk.reshape(1, CONV_CH)
        else:
            out[name] = blk
    return out


WEIGHT_NAMES = ("norm1_w", "norm2_w", "w_ada", "b_ada", "w_in", "conv_w", "conv_b", "dt_bias", "a_log", "d_skip",
                "ssd_norm_w", "q_norm_w", "k_norm_w", "attn_norm_w", "w_out", "w_ff1", "w_ff2")
SMALL_NAMES = ("norm1_w", "norm2_w", "b_ada", "conv_b", "dt_bias", "a_log", "d_skip", "ssd_norm_w", "q_norm_w",
               "k_norm_w", "attn_norm_w")


def kernel(x, c, norm1_w, norm2_w, w_ada, b_ada, w_in, conv_w, conv_b, dt_bias, a_log, d_skip, ssd_norm_w, q_norm_w, k_norm_w, attn_norm_w, w_out, w_ff1, w_ff2, loss_target, m_norm1_w, m_norm2_w, m_w_ada, m_b_ada, m_w_in, m_conv_w, m_conv_b, m_dt_bias, m_a_log, m_d_skip, m_ssd_norm_w, m_q_norm_w, m_k_norm_w, m_attn_norm_w, m_w_out, m_w_ff1, m_w_ff2, v_norm1_w, v_norm2_w, v_w_ada, v_b_ada, v_w_in, v_conv_w, v_conv_b, v_dt_bias, v_a_log, v_d_skip, v_ssd_norm_w, v_q_norm_w, v_k_norm_w, v_attn_norm_w, v_w_out, v_w_ff1, v_w_ff2):
    args = dict(locals())
    w = {n: args[n] for n in WEIGHT_NAMES}
    m = {n: args["m_" + n] for n in WEIGHT_NAMES}
    v = {n: args["v_" + n] for n in WEIGHT_NAMES}
    me = 4 * lax.axis_index("x") + 2 * lax.axis_index("y") + lax.axis_index("c")

    c_rows = jnp.pad(c, ((0, 7), (0, 0)))
    w_in_t, m_in_t, v_in_t = [jnp.transpose(t["w_in"][0]) for t in (w, m, v)]
    c_g, conv_g, w_in_g = _exchange([c_rows, w["conv_w"][0], w_in_t.astype(BF16)], "gather_w_in", scatter=False)
    c_all = c_g[:, 0, :]
    conv_full = _shards_to_cols(conv_g)
    w_in_pt = _pack_w_in_rows(w_in_g.reshape(IN_W, D_MODEL))

    mod_part = _ada_fwd(c_all, w["w_ada"][0])
    (mod_g,) = _exchange([mod_part], "gather_mod", scatter=False)
    mod_mine = lax.dynamic_index_in_dim(mod_g, me, axis=1, keepdims=False).reshape(1, N_MOD * D_MODEL) + w["b_ada"]
    mod = mod_mine.reshape(N_MOD, D_MODEL)

    res = _local_step(x[0], loss_target[0], mod, w_in_pt, conv_full, w["conv_b"], w["dt_bias"], w["a_log"], w["d_skip"],
                      w["ssd_norm_w"], w["q_norm_w"], w["k_norm_w"], w["attn_norm_w"], w["w_out"][0].astype(BF16),
                      w["w_ff1"][0].astype(BF16), w["w_ff2"][0].astype(BF16), w["norm1_w"], w["norm2_w"])

    small_vals = {n: res[n] for n in SMALL_NAMES if n != "b_ada"}
    small_vals["b_ada"] = res["dmod"]
    small_vals["conv_w"] = res["conv_w"]
    (small_g,) = _exchange([_pack_small(small_vals)], "gather_small", scatter=False)

    grads, delta, new_m, new_v = {}, {}, {}, {}
    for name in ("w_out", "w_ff1", "w_ff2"):
        outs = _reduce_adamw(res[name], w[name][0], m[name][0], v[name][0], "adamw_" + name)
        grads[name], delta[name], new_m[name], new_v[name] = [o[None] for o in outs]
    outs = _reduce_adamw(res["w_in"], w_in_t, m_in_t, v_in_t, "adamw_w_in")
    grads["w_in"], delta["w_in"], new_m["w_in"], new_v["w_in"] = [jnp.transpose(o)[None] for o in outs]

    sm = _small_reduce_adamw(small_g, _pack_small({n: w[n] for n in SMALL_NAMES}), _pack_small({n: m[n] for n in SMALL_NAMES}),
                             _pack_small({n: v[n] for n in SMALL_NAMES}))
    sm = [_unpack_small(p) for p in sm]
    for n in SMALL_NAMES:
        grads[n], delta[n], new_m[n], new_v[n] = [p[n] for p in sm]
    shard_w = CONV_CH // N_DEV
    g_conv = lax.dynamic_slice_in_dim(sm[0]["conv_w"], me * shard_w, shard_w, axis=1)
    cw = _adamw_small(g_conv, w["conv_w"][0], m["conv_w"][0], v["conv_w"][0], "adamw_conv_w")
    grads["conv_w"] = g_conv[None]
    delta["conv_w"], new_m["conv_w"], new_v["conv_w"] = [o[None] for o in cw]

    ada_w = w_ada.shape[2]
    dmod_all = small_g[:, :N_MOD, :].reshape(N_DEV, N_MOD * D_MODEL)
    dmod_cols = lax.dynamic_slice_in_dim(dmod_all, me * ada_w, ada_w, axis=1)
    outs = _ada_bwd_adamw(c_all, dmod_cols, w["w_ada"][0], m["w_ada"][0], v["w_ada"][0])
    grads["w_ada"], delta["w_ada"], new_m["w_ada"], new_v["w_ada"] = [o[None] for o in outs]

    loss = lax.psum(res["loss"][0, 0], ("x", "y", "c"))
    return (loss, res["grad_x"][None], *[grads[n] for n in WEIGHT_NAMES], *[delta[n] for n in WEIGHT_NAMES],
            *[new_m[n] for n in WEIGHT_NAMES], *[new_v[n] for n in WEIGHT_NAMES])
```

```python
import functools

import jax
import jax.numpy as jnp
from jax import lax
from jax.experimental import pallas as pl
from jax.experimental.pallas import tpu as pltpu

F32 = jnp.float32
BF16 = jnp.bfloat16
HIGHEST = lax.Precision.HIGHEST
MESH_IDS = pl.DeviceIdType.MESH

N_DEV = 8
D_MODEL = 1024
HEAD_DIM = 64
SSD_HEADS = 16
SSD_GROUPS = 4
HEADS_PER_GROUP = SSD_HEADS // SSD_GROUPS
SSD_STATE = 128
SSD_CHUNK = 128
SSD_D_INNER = SSD_HEADS * HEAD_DIM
GROUP_WIDTH = SSD_D_INNER // SSD_GROUPS
CONV_K = 4
CONV_CH = SSD_D_INNER + 2 * SSD_GROUPS * SSD_STATE
ATT_HEADS = 16
ATT_D = ATT_HEADS * HEAD_DIM
ATT_BLK = 128
DILATIONS = (1, 4, 16)
D_FF = 4 * D_MODEL
N_MOD = 6
EPS = 1e-6
IN_W = SSD_D_INNER + CONV_CH + SSD_HEADS + 3 * ATT_D
LANE = 128
OFF_Z, OFF_XBC, OFF_Q, OFF_K, OFF_V, OFF_DT = 0, 1024, 3072, 4096, 5120, 6144
IN_WP = OFF_DT + LANE

ADAM_LR, ADAM_B1, ADAM_B2, ADAM_EPS, ADAM_WD, ADAM_STEP = 0.001, 0.9, 0.999, 1e-08, 0.01, 10
VMEM_LIMIT = 56 * 1024 * 1024
ROW_TILE = 512
SMALL_ROWS = 24


def _cparams(sem=None):
    return pltpu.CompilerParams(dimension_semantics=sem, vmem_limit_bytes=VMEM_LIMIT)


def _sigmoid(v):
    return 1.0 / (1.0 + jnp.exp(-v))


def _softplus(v):
    y = jnp.exp(-jnp.abs(v))
    small = y * (1.0 - y * (0.5 - y * (1.0 / 3.0)))
    return jnp.maximum(v, 0.0) + jnp.where(y < 0.01, small, jnp.log(1.0 + y))


def _dot(a, b, dims, precision=None):
    return lax.dot_general(a, b, (dims, ((), ())), preferred_element_type=F32, precision=precision)


NN = ((1,), (0,))
NT = ((1,), (1,))
TN = ((0,), (0,))


def _matmul(a, b, *, ta=False, tb=False, tm, tn, tk, out_dtype=F32, name, mode=None, u=None, comm=None):
    m, k = (a.shape[1], a.shape[0]) if ta else a.shape
    n = b.shape[0] if tb else b.shape[1]
    assert m % tm == 0 and n % tn == 0 and k % tk == 0, (name, m, n, k)
    nk = k // tk
    a_spec = pl.BlockSpec((tk, tm), lambda i, j, kk: (kk, i)) if ta else pl.BlockSpec((tm, tk), lambda i, j, kk: (i, kk))
    b_spec = pl.BlockSpec((tn, tk), lambda i, j, kk: (j, kk)) if tb else pl.BlockSpec((tk, tn), lambda i, j, kk: (kk, j))
    o_spec = pl.BlockSpec((tm, tn), lambda i, j, kk: (i, j))
    dims = ((0,) if ta else (1,), (1,) if tb else (0,))
    n_out = 2 if mode == "relu2" else 1

    def body(*refs):
        if mode == "drelu2":
            a_ref, b_ref, u_ref = refs[:3]
            rest = refs[3:]
        else:
            a_ref, b_ref = refs[:2]
            u_ref = None
            rest = refs[2:]
        outs = rest[:n_out]
        part = _dot(a_ref[...], b_ref[...], dims)

        def finish(r):
            if mode == "relu2":
                outs[0][...] = r.astype(BF16)
                rr = jnp.maximum(r, 0.0)
                outs[1][...] = (rr * rr).astype(BF16)
            elif mode == "drelu2":
                outs[0][...] = (r * (2.0 * jnp.maximum(u_ref[...].astype(F32), 0.0))).astype(out_dtype)
            else:
                outs[0][...] = r.astype(out_dtype)

        if nk == 1:
            finish(part)
        else:
            acc = rest[n_out]
            kk = pl.program_id(2)

            @pl.when(kk == 0)
            def _():
                acc[...] = part

            @pl.when(kk > 0)
            def _():
                acc[...] += part

            @pl.when(kk == nk - 1)
            def _():
                finish(acc[...])

    in_specs = [a_spec, b_spec]
    args = [a, b]
    if mode == "drelu2":
        in_specs.append(o_spec)
        args.append(u)
    if mode == "relu2":
        out_shape = [jax.ShapeDtypeStruct((m, n), BF16), jax.ShapeDtypeStruct((m, n), BF16)]
    else:
        out_shape = [jax.ShapeDtypeStruct((m, n), out_dtype)]
    outs, comm_outs = _pcall(
        body, args, name=name, grid=(m // tm, n // tn, nk), in_specs=in_specs, out_specs=[o_spec] * n_out,
        out_shape=out_shape, scratch_shapes=[pltpu.VMEM((tm, tn), F32)] if nk > 1 else [],
        sem=("parallel", "parallel", "arbitrary"), comm=comm)
    res = tuple(outs) if mode == "relu2" else outs[0]
    return res if comm is None else (res, comm_outs)


def _norm_mod_fwd(x, nw, scale, shift, name, res=None, gate=None):
    s, d = x.shape
    row = pl.BlockSpec((ROW_TILE, d), lambda i: (i, 0))
    vec = pl.BlockSpec((1, d), lambda i: (0, 0))
    with_res = res is not None

    def body(*refs):
        if with_res:
            x_ref, res_ref, gate_ref, nw_ref, sc_ref, sh_ref, x1_ref, h_ref = refs
            xv = x_ref[...] + gate_ref[...] * res_ref[...]
            x1_ref[...] = xv
        else:
            x_ref, nw_ref, sc_ref, sh_ref, h_ref = refs
            xv = x_ref[...]
        r = lax.rsqrt(jnp.mean(xv * xv, axis=-1, keepdims=True) + EPS)
        h_ref[...] = ((xv * r) * nw_ref[...] * (1.0 + sc_ref[...]) + sh_ref[...]).astype(BF16)

    if with_res:
        in_specs = [row, row, vec, vec, vec, vec]
        args = (x, res, gate, nw, scale, shift)
        out_shape = (jax.ShapeDtypeStruct((s, d), F32), jax.ShapeDtypeStruct((s, d), BF16))
        out_specs = (row, row)
    else:
        in_specs = [row, vec, vec, vec]
        args = (x, nw, scale, shift)
        out_shape = jax.ShapeDtypeStruct((s, d), BF16)
        out_specs = row
    return pl.pallas_call(body, name=name, grid=(s // ROW_TILE,), in_specs=in_specs, out_specs=out_specs,
                          out_shape=out_shape, compiler_params=_cparams(("parallel",)))(*args)


def _norm_mod_bwd(dh, xin, dres, nw, scale, name, gate=None, mix=None):
    s, d = xin.shape
    row = pl.BlockSpec((ROW_TILE, d), lambda i: (i, 0))
    vec = pl.BlockSpec((1, d), lambda i: (0, 0))
    with_gate = gate is not None

    def body(*refs):
        if with_gate:
            dh_ref, x_ref, dres_ref, nw_ref, sc_ref, gate_ref, mix_ref, dx_ref, dsh_ref, dsc_ref, dnw_ref, dmix_ref, dg_ref = refs
        else:
            dh_ref, x_ref, dres_ref, nw_ref, sc_ref, dx_ref, dsh_ref, dsc_ref, dnw_ref = refs
        i = pl.program_id(0)

        @pl.when(i == 0)
        def _():
            dsh_ref[...] = jnp.zeros_like(dsh_ref)
            dsc_ref[...] = jnp.zeros_like(dsc_ref)
            dnw_ref[...] = jnp.zeros_like(dnw_ref)
            if with_gate:
                dg_ref[...] = jnp.zeros_like(dg_ref)

        xv = x_ref[...]
        dhv = dh_ref[...]
        r = lax.rsqrt(jnp.mean(xv * xv, axis=-1, keepdims=True) + EPS)
        nrm = xv * r
        one_sc = 1.0 + sc_ref[...]
        dhn = dhv * nrm
        dsh_ref[...] += jnp.sum(dhv, axis=0, keepdims=True)
        dsc_ref[...] += jnp.sum(dhn, axis=0, keepdims=True) * nw_ref[...]
        dnw_ref[...] += jnp.sum(dhn, axis=0, keepdims=True) * one_sc
        dn = dhv * (nw_ref[...] * one_sc)
        dx = dres_ref[...] + r * (dn - nrm * jnp.mean(dn * nrm, axis=-1, keepdims=True))
        dx_ref[...] = dx
        if with_gate:
            dmix_ref[...] = (gate_ref[...] * dx).astype(BF16)
            dg_ref[...] += jnp.sum(dx * mix_ref[...], axis=0, keepdims=True)

    vshape = jax.ShapeDtypeStruct((1, d), F32)
    in_specs = [row, row, row, vec, vec]
    args = [dh, xin, dres, nw, scale]
    out_shape = [jax.ShapeDtypeStruct((s, d), F32), vshape, vshape, vshape]
    out_specs = [row, vec, vec, vec]
    if with_gate:
        in_specs += [vec, row]
        args += [gate, mix]
        out_shape += [jax.ShapeDtypeStruct((s, d), BF16), vshape]
        out_specs += [row, vec]
    return pl.pallas_call(body, name=name, grid=(s // ROW_TILE,), in_specs=in_specs, out_specs=out_specs,
                          out_shape=out_shape, compiler_params=_cparams(("arbitrary",)))(*args)


def _loss_head(x1, ff, gate2, tgt):
    s, d = x1.shape
    row = pl.BlockSpec((ROW_TILE, d), lambda i: (i, 0))
    vec = pl.BlockSpec((1, d), lambda i: (0, 0))
    one = pl.BlockSpec((1, 1), lambda i: (0, 0))

    def body(x1_ref, ff_ref, g_ref, t_ref, loss_ref, dout_ref, dff_ref, dg_ref):
        i = pl.program_id(0)

        @pl.when(i == 0)
        def _():
            loss_ref[...] = jnp.zeros_like(loss_ref)
            dg_ref[...] = jnp.zeros_like(dg_ref)

        ffv = ff_ref[...]
        err = x1_ref[...] + g_ref[...] * ffv - t_ref[...]
        loss_ref[...] += (0.5 / d) * jnp.sum(err * err).reshape(1, 1)
        dout = err * (1.0 / d)
        dout_ref[...] = dout
        dff_ref[...] = (g_ref[...] * dout).astype(BF16)
        dg_ref[...] += jnp.sum(dout * ffv, axis=0, keepdims=True)

    return pl.pallas_call(
        body, name="loss_head", grid=(s // ROW_TILE,), in_specs=[row, row, vec, row],
        out_specs=[one, row, row, vec],
        out_shape=[jax.ShapeDtypeStruct((1, 1), F32), jax.ShapeDtypeStruct((s, d), F32),
                   jax.ShapeDtypeStruct((s, d), BF16), jax.ShapeDtypeStruct((1, d), F32)],
        compiler_params=_cparams(("arbitrary",)))(x1, ff, gate2, tgt)


CONV_COLS = 256
HALO = 8


def _shift_down(cur, halo, k):
    if k == 0:
        return cur
    rolled = pltpu.roll(cur, k, axis=0)
    top = jnp.where(lax.broadcasted_iota(jnp.int32, halo.shape, 0) < k, pltpu.roll(halo, k, axis=0), rolled[:HALO])
    return jnp.concatenate([top, rolled[HALO:]], axis=0)


def _shift_up(cur, halo, k):
    if k == 0:
        return cur
    t = cur.shape[0]
    rolled = pltpu.roll(cur, t - k, axis=0)
    bot = jnp.where(lax.broadcasted_iota(jnp.int32, halo.shape, 0) >= HALO - k, pltpu.roll(halo, HALO - k, axis=0),
                    rolled[t - HALO:])
    return jnp.concatenate([rolled[:t - HALO], bot], axis=0)


def _conv_fwd(proj, conv_w, conv_b):
    s = proj.shape[0]
    nr = s // ROW_TILE
    cb0 = OFF_XBC // CONV_COLS
    hb = ROW_TILE // HALO
    cur = pl.BlockSpec((ROW_TILE, CONV_COLS), lambda j, r: (r, cb0 + j))
    prev = pl.BlockSpec((HALO, CONV_COLS), lambda j, r: (jnp.maximum(r * hb - 1, 0), cb0 + j))
    out = pl.BlockSpec((ROW_TILE, CONV_COLS), lambda j, r: (r, j))

    def body(u_ref, up_ref, w_ref, b_ref, pre_ref, act_ref):
        r = pl.program_id(1)
        u = u_ref[...]
        halo = jnp.where(r > 0, up_ref[...], 0.0)
        acc = b_ref[...] + w_ref[CONV_K - 1:CONV_K, :] * u
        for k in range(1, CONV_K):
            acc = acc + w_ref[CONV_K - 1 - k:CONV_K - k, :] * _shift_down(u, halo, k)
        pre_ref[...] = acc
        act_ref[...] = acc * _sigmoid(acc)

    return pl.pallas_call(
        body, name="conv_fwd", grid=(CONV_CH // CONV_COLS, nr),
        in_specs=[cur, prev, pl.BlockSpec((CONV_K, CONV_COLS), lambda j, r: (0, j)),
                  pl.BlockSpec((1, CONV_COLS), lambda j, r: (0, j))],
        out_specs=[out, out],
        out_shape=[jax.ShapeDtypeStruct((s, CONV_CH), F32), jax.ShapeDtypeStruct((s, CONV_CH), F32)],
        compiler_params=_cparams(("parallel", "arbitrary")))(proj, proj, conv_w, conv_b)


def _conv_bwd(dact, pre, proj, conv_w):
    s = proj.shape[0]
    nr = s // ROW_TILE
    cb0 = OFF_XBC // CONV_COLS
    hb = ROW_TILE // HALO
    last_halo = s // HALO - 1
    cur = pl.BlockSpec((ROW_TILE, CONV_COLS), lambda j, r: (r, j))
    nxt = pl.BlockSpec((HALO, CONV_COLS), lambda j, r: (jnp.minimum((r + 1) * hb, last_halo), j))
    ucur = pl.BlockSpec((ROW_TILE, CONV_COLS), lambda j, r: (r, cb0 + j))
    uprev = pl.BlockSpec((HALO, CONV_COLS), lambda j, r: (jnp.maximum(r * hb - 1, 0), cb0 + j))
    wspec = pl.BlockSpec((CONV_K, CONV_COLS), lambda j, r: (0, j))
    bspec = pl.BlockSpec((1, CONV_COLS), lambda j, r: (0, j))

    def dsilu(p):
        sg = _sigmoid(p)
        return sg * (1.0 + p * (1.0 - sg))

    def body(da_ref, dan_ref, pre_ref, pren_ref, u_ref, up_ref, w_ref, du_ref, dw_ref, db_ref):
        r = pl.program_id(1)

        @pl.when(r == 0)
        def _():
            dw_ref[...] = jnp.zeros_like(dw_ref)
            db_ref[...] = jnp.zeros_like(db_ref)

        dpre = da_ref[...] * dsilu(pre_ref[...])
        dnext = jnp.where(r < nr - 1, dan_ref[...] * dsilu(pren_ref[...]), 0.0)
        u = u_ref[...]
        halo = jnp.where(r > 0, up_ref[...], 0.0)
        du = w_ref[CONV_K - 1:CONV_K, :] * dpre
        dws = [jnp.sum(dpre * u, axis=0, keepdims=True)]
        for k in range(1, CONV_K):
            du = du + w_ref[CONV_K - 1 - k:CONV_K - k, :] * _shift_up(dpre, dnext, k)
            dws.append(jnp.sum(dpre * _shift_down(u, halo, k), axis=0, keepdims=True))
        du_ref[...] = du.astype(BF16)
        dw_ref[...] += jnp.concatenate(dws[::-1], axis=0)
        db_ref[...] += jnp.sum(dpre, axis=0, keepdims=True)

    return pl.pallas_call(
        body, name="conv_bwd", grid=(CONV_CH // CONV_COLS, nr),
        in_specs=[cur, nxt, cur, nxt, ucur, uprev, wspec],
        out_specs=[cur, wspec, bspec],
        out_shape=[jax.ShapeDtypeStruct((s, CONV_CH), BF16), jax.ShapeDtypeStruct((CONV_K, CONV_CH), F32),
                   jax.ShapeDtypeStruct((1, CONV_CH), F32)],
        compiler_params=_cparams(("parallel", "arbitrary")))(dact, dact, pre, pre, proj, proj, conv_w)


def _ssd_common(dtr, dtb, alog):
    lane = lax.broadcasted_iota(jnp.int32, (1, LANE), 1)
    head_lane = lane < SSD_HEADS
    dt = jnp.where(head_lane, _softplus(dtr + dtb), 0.0)
    a = jnp.where(head_lane, -jnp.exp(alog), 0.0)
    row = lax.broadcasted_iota(jnp.int32, (SSD_CHUNK, SSD_CHUNK), 0)
    col = lax.broadcasted_iota(jnp.int32, (SSD_CHUNK, SSD_CHUNK), 1)
    tril = row >= col
    cs = _dot(tril.astype(F32), dt * a, NN, precision=HIGHEST)
    return dt, a, cs, cs.T, tril, lane


def _split_bf16(v, passes):
    terms, rest = [], v
    for _ in range(passes):
        t = rest.astype(BF16)
        terms.append(t)
        rest = rest - t.astype(F32)
    return terms


def _dot_split(v, m, dims, passes):
    out = None
    for t in _split_bf16(v, passes):
        part = _dot(t, m, dims)
        out = part if out is None else out + part
    return out


def _ssd_constants():
    heads = jnp.arange(LANE)[:, None]
    exp_mat = (heads == (jnp.arange(SSD_D_INNER)[None, :] // HEAD_DIM)).astype(BF16)
    ind4 = ((jnp.arange(SSD_HEADS * SSD_CHUNK)[:, None] // SSD_CHUNK) == jnp.arange(LANE)[None, :]).astype(BF16)
    return exp_mat, ind4


def _expand_heads(v):
    return jnp.repeat(v[:, :SSD_HEADS], HEAD_DIM, axis=1)


def _ssd_prep(dtr, dtb, alog, exp_mat):
    dt, a, cs, cst, tril, lane = _ssd_common(dtr, dtb, alog)
    return dt, a, cs, cst, tril, lane, _dot_split(dt, exp_mat, NN, 2), _dot_split(cs, exp_mat, NN, 3)


def _chunk_decay_rows(cs, g):
    parts = []
    for e in range(HEADS_PER_GROUP):
        h = g * HEADS_PER_GROUP + e
        parts.append(jnp.broadcast_to(jnp.exp(cs[SSD_CHUNK - 1:SSD_CHUNK, h:h + 1]), (HEAD_DIM, SSD_STATE)))
    return jnp.concatenate(parts, axis=0)


def _ssd_fwd(proj, act, dtb, alog, dsk, nw):
    s = proj.shape[0]
    nc = s // SSD_CHUNK
    bc_w = SSD_GROUPS * SSD_STATE
    exp_mat, _ = _ssd_constants()

    def body(z_ref, dtr_ref, xs_ref, b_ref, c_ref, dtb_ref, alog_ref, dskx_ref, nw_ref, exp_ref,
             ypre_ref, yssd_ref, hall_ref, h_scr):
        @pl.when(pl.program_id(0) == 0)
        def _():
            h_scr[...] = jnp.zeros_like(h_scr)

        dt, a, cs, cst, tril, lane, dtx, csx = _ssd_prep(dtr_ref[...], dtb_ref[...], alog_ref[...], exp_ref[...])
        cs_last_x = csx[SSD_CHUNK - 1:SSD_CHUNK, :]
        xs = xs_ref[...]
        xdt = xs * dtx
        xdtb = xdt.astype(BF16)
        xdec = (xdt * jnp.exp(cs_last_x - csx)).astype(BF16)
        ecsx = jnp.exp(csx)
        head_of_lane = lax.broadcasted_iota(jnp.int32, (1, GROUP_WIDTH), 1) // HEAD_DIM
        for g in range(SSD_GROUPS):
            gs = slice(g * GROUP_WIDTH, (g + 1) * GROUP_WIDTH)
            bg = b_ref[:, g * SSD_STATE:(g + 1) * SSD_STATE].astype(BF16)
            cg = c_ref[:, g * SSD_STATE:(g + 1) * SSD_STATE].astype(BF16)
            cb = _dot(cg, bg, NT)
            hprev = h_scr[gs, :]
            hall_ref[0, gs, :] = hprev
            gms, rhs = [], []
            xg = xdtb[:, gs]
            for e in range(HEADS_PER_GROUP):
                h = g * HEADS_PER_GROUP + e
                lm = jnp.exp(jnp.where(tril, cs[:, h:h + 1] - cst[h:h + 1, :], -1e30))
                gms.append((cb * lm).astype(BF16))
                rhs.append(jnp.where(head_of_lane == e, xg, jnp.zeros_like(xg)))
            y = _dot(jnp.concatenate(gms, axis=1), jnp.concatenate(rhs, axis=0), NN)
            y = y + ecsx[:, gs] * _dot(cg, hprev.astype(BF16), NT)
            y = y + dskx_ref[:, gs] * xs[:, gs]
            h_scr[gs, :] = hprev * _chunk_decay_rows(cs, g) + _dot(xdec[:, gs], bg, TN)
            ypre_ref[:, gs] = y
            z = z_ref[:, gs]
            yg = y * (z * _sigmoid(z))
            r = lax.rsqrt(jnp.mean(yg * yg, axis=-1, keepdims=True) + EPS)
            yssd_ref[:, gs] = (yg * r * nw_ref[:, gs]).astype(BF16)

    row_d = lambda cb: pl.BlockSpec((SSD_CHUNK, SSD_D_INNER), lambda c: (c, cb))
    small = pl.BlockSpec((1, LANE), lambda c: (0, 0))
    wide = pl.BlockSpec((1, SSD_D_INNER), lambda c: (0, 0))
    return pl.pallas_call(
        body, name="ssd_fwd", grid=(nc,),
        in_specs=[row_d(OFF_Z // SSD_D_INNER),
                  pl.BlockSpec((SSD_CHUNK, LANE), lambda c: (c, OFF_DT // LANE)),
                  row_d(0),
                  pl.BlockSpec((SSD_CHUNK, bc_w), lambda c: (c, SSD_D_INNER // bc_w)),
                  pl.BlockSpec((SSD_CHUNK, bc_w), lambda c: (c, SSD_D_INNER // bc_w + 1)),
                  small, small, wide, wide, pl.BlockSpec((LANE, SSD_D_INNER), lambda c: (0, 0))],
        out_specs=[row_d(0), row_d(0), pl.BlockSpec((1, SSD_D_INNER, SSD_STATE), lambda c: (c, 0, 0))],
        out_shape=[jax.ShapeDtypeStruct((s, SSD_D_INNER), F32), jax.ShapeDtypeStruct((s, SSD_D_INNER), BF16),
                   jax.ShapeDtypeStruct((nc, SSD_D_INNER, SSD_STATE), F32)],
        scratch_shapes=[pltpu.VMEM((SSD_D_INNER, SSD_STATE), F32)],
        compiler_params=_cparams(("arbitrary",)))(proj, proj, act, act, act, dtb, alog, _expand_heads(dsk), nw, exp_mat)


def _ssd_bwd(dycat, ypre, proj, act, hall, dtb, alog, dsk, nw, comm=None):
    s = proj.shape[0]
    nc = s // SSD_CHUNK
    bc_w = SSD_GROUPS * SSD_STATE

    exp_mat, ind4 = _ssd_constants()
    seg_passes = 2

    def body(dy_ref, ypre_ref, z_ref, dtr_ref, xs_ref, b_ref, c_ref, hall_ref, dtb_ref, alog_ref, dskx_ref, nw_ref,
             exp_ref, ind4_ref, dz_ref, dact_ref, ddtr_ref, da_ref, ddsk_ref, ddtb_ref, dnw_ref, dh_scr):
        @pl.when(pl.program_id(0) == 0)
        def _():
            dh_scr[...] = jnp.zeros_like(dh_scr)
            da_ref[...] = jnp.zeros_like(da_ref)
            ddsk_ref[...] = jnp.zeros_like(ddsk_ref)
            ddtb_ref[...] = jnp.zeros_like(ddtb_ref)
            dnw_ref[...] = jnp.zeros_like(dnw_ref)

        dtr = dtr_ref[...]
        dt, a, cs, cst, tril, lane, dtx, csx = _ssd_prep(dtr, dtb_ref[...], alog_ref[...], exp_ref[...])
        cs_last_x = csx[SSD_CHUNK - 1:SSD_CHUNK, :]
        xs = xs_ref[...]
        xdt = xs * dtx
        xdtb = xdt.astype(BF16)
        decx = jnp.exp(cs_last_x - csx)
        xdecf = xdt * decx
        xdec = xdecf.astype(BF16)
        ecsx = jnp.exp(csx)
        head_of_lane = lax.broadcasted_iota(jnp.int32, (1, GROUP_WIDTH), 1) // HEAD_DIM
        last_row = lax.broadcasted_iota(jnp.int32, (SSD_CHUNK, 1), 0) == SSD_CHUNK - 1
        dcs_col = jnp.zeros((SSD_CHUNK, LANE), F32)
        dcs_row = jnp.zeros((SSD_CHUNK, LANE), F32)
        ddt = jnp.zeros((SSD_CHUNK, LANE), F32)
        ddsk = jnp.zeros((1, LANE), F32)
        hsum = jnp.zeros((1, LANE), F32)
        t1_sum = jnp.zeros((1, LANE), F32)
        for g in range(SSD_GROUPS):
            gs = slice(g * GROUP_WIDTH, (g + 1) * GROUP_WIDTH)
            bsl = slice(g * SSD_STATE, (g + 1) * SSD_STATE)
            exp_g = exp_ref[:, gs]
            ind4_g = ind4_ref[g * HEADS_PER_GROUP * SSD_CHUNK:(g + 1) * HEADS_PER_GROUP * SSD_CHUNK, :]
            z = z_ref[:, gs]
            sg = _sigmoid(z)
            sz = z * sg
            ypre = ypre_ref[:, gs]
            yg = ypre * sz
            r = lax.rsqrt(jnp.mean(yg * yg, axis=-1, keepdims=True) + EPS)
            nrm = yg * r
            dyo_n = dy_ref[:, gs]
            dnw_ref[:, gs] += jnp.sum(dyo_n * nrm, axis=0, keepdims=True)
            dn = dyo_n * nw_ref[:, gs]
            dyg = r * (dn - nrm * jnp.mean(dn * nrm, axis=-1, keepdims=True))
            dz_ref[:, gs] = (dyg * ypre * (sg * (1.0 + z * (1.0 - sg)))).astype(BF16)
            dy = dyg * sz

            bg = b_ref[:, bsl].astype(BF16)
            cg = c_ref[:, bsl].astype(BF16)
            cb = _dot(cg, bg, NT)
            hprev = hall_ref[0, gs, :]
            hb = hprev.astype(BF16)
            dhn = dh_scr[gs, :]
            dhb = dhn.astype(BF16)
            xs_g, xdt_g = xs[:, gs], xdtb[:, gs]
            w_off = _dot(cg, hb, NT)
            dyo = dy * ecsx[:, gs]
            dyob = dyo.astype(BF16)
            dcg = _dot(dyob, hb, NN)
            dh_y = _dot(dyob, cg, TN)
            r_st = _dot(bg, dhb, NT)
            dbg = _dot(xdec[:, gs], dhb, NN)
            dyb = dy.astype(BF16)
            gms, gmbs, lms, dys = [], [], [], []
            for e in range(HEADS_PER_GROUP):
                h = g * HEADS_PER_GROUP + e
                lm = jnp.exp(jnp.where(tril, cs[:, h:h + 1] - cst[h:h + 1, :], -1e30))
                gm = cb * lm
                lms.append(lm)
                gms.append(gm)
                gmbs.append(gm.astype(BF16))
                dys.append(jnp.where(head_of_lane == e, dyb, jnp.zeros_like(dyb)))
            dxdt = _dot(jnp.concatenate(gmbs, axis=0), jnp.concatenate(dys, axis=0), TN) + decx[:, gs] * r_st
            dcb = jnp.zeros((SSD_CHUNK, SSD_CHUNK), F32)
            mms = []
            for e in range(HEADS_PER_GROUP):
                dg = _dot(dys[e], xdt_g, NT)
                mms.append(dg * gms[e])
                dcb = dcb + dg * lms[e]
            seg = _dot_split(jnp.concatenate([dyo * w_off, xdecf[:, gs] * r_st, dxdt * xs_g, dy * xs_g], axis=0), exp_g, NT, seg_passes)
            v1, t1, ddt_g, dsk_g = [seg[i * SSD_CHUNK:(i + 1) * SSD_CHUNK] for i in range(4)]
            dcs_col = dcs_col + v1 - t1 + _dot_split(jnp.concatenate(mms, axis=1), ind4_g, NN, seg_passes)
            for t in _split_bf16(jnp.concatenate(mms, axis=0), seg_passes):
                dcs_row = dcs_row + _dot(ind4_g, t, TN)
            ddt = ddt + ddt_g
            ddsk = ddsk + jnp.sum(dsk_g, axis=0, keepdims=True)
            t1_sum = t1_sum + jnp.sum(t1, axis=0, keepdims=True)
            for e in range(HEADS_PER_GROUP):
                h = g * HEADS_PER_GROUP + e
                hs = slice(e * HEAD_DIM, (e + 1) * HEAD_DIM)
                hsum = hsum + jnp.where(lane == h, jnp.sum(dhn[hs, :] * hprev[hs, :]).reshape(1, 1), 0.0)
            dh_scr[gs, :] = dhn * _chunk_decay_rows(cs, g) + dh_y
            dcbb = dcb.astype(BF16)
            dact_ref[:, gs] = dxdt * dtx[:, gs] + dskx_ref[:, gs] * dy
            dact_ref[:, SSD_D_INNER + g * SSD_STATE:SSD_D_INNER + (g + 1) * SSD_STATE] = dbg + _dot(dcbb, cg, TN)
            dact_ref[:, SSD_D_INNER + bc_w + g * SSD_STATE:SSD_D_INNER + bc_w + (g + 1) * SSD_STATE] = dcg + _dot(dcbb, bg, NN)
        dlast = t1_sum + jnp.exp(cs[SSD_CHUNK - 1:SSD_CHUNK, :]) * hsum
        dcs = dcs_col - dcs_row.T + jnp.where(last_row, dlast, 0.0)
        row = lax.broadcasted_iota(jnp.int32, (SSD_CHUNK, SSD_CHUNK), 0)
        col = lax.broadcasted_iota(jnp.int32, (SSD_CHUNK, SSD_CHUNK), 1)
        dda = _dot((col >= row).astype(F32), dcs, NN, precision=HIGHEST)
        ddt = ddt + dda * a
        da_ref[...] += jnp.sum(dda * dt, axis=0, keepdims=True)
        ddtr = jnp.where(lane < SSD_HEADS, ddt * _sigmoid(dtr + dtb_ref[...]), 0.0)
        ddtr_ref[...] = ddtr.astype(BF16)
        ddtb_ref[...] += jnp.sum(ddtr, axis=0, keepdims=True)
        ddsk_ref[...] += ddsk

    rev = lambda c: nc - 1 - c
    row_d = lambda cb: pl.BlockSpec((SSD_CHUNK, SSD_D_INNER), lambda c: (rev(c), cb))
    small = pl.BlockSpec((1, LANE), lambda c: (0, 0))
    wide = pl.BlockSpec((1, SSD_D_INNER), lambda c: (0, 0))
    small_shape = jax.ShapeDtypeStruct((1, LANE), F32)
    return _pcall(
        body, (dycat, ypre, proj, proj, act, act, act, hall, dtb, alog, _expand_heads(dsk), nw, exp_mat, ind4),
        name="ssd_bwd", grid=(nc,),
        in_specs=[row_d(0), row_d(0), row_d(OFF_Z // SSD_D_INNER),
                  pl.BlockSpec((SSD_CHUNK, LANE), lambda c: (rev(c), OFF_DT // LANE)),
                  row_d(0),
                  pl.BlockSpec((SSD_CHUNK, bc_w), lambda c: (rev(c), SSD_D_INNER // bc_w)),
                  pl.BlockSpec((SSD_CHUNK, bc_w), lambda c: (rev(c), SSD_D_INNER // bc_w + 1)),
                  pl.BlockSpec((1, SSD_D_INNER, SSD_STATE), lambda c: (rev(c), 0, 0)),
                  small, small, wide, wide, pl.BlockSpec((LANE, SSD_D_INNER), lambda c: (0, 0)),
                  pl.BlockSpec((SSD_HEADS * SSD_CHUNK, LANE), lambda c: (0, 0))],
        out_specs=[row_d(0), pl.BlockSpec((SSD_CHUNK, CONV_CH), lambda c: (rev(c), 0)),
                   pl.BlockSpec((SSD_CHUNK, LANE), lambda c: (rev(c), 0)), small, small, small, wide],
        out_shape=[jax.ShapeDtypeStruct((s, SSD_D_INNER), BF16), jax.ShapeDtypeStruct((s, CONV_CH), F32),
                   jax.ShapeDtypeStruct((s, LANE), BF16), small_shape, small_shape, small_shape,
                   jax.ShapeDtypeStruct((1, SSD_D_INNER), F32)],
        scratch_shapes=[pltpu.VMEM((SSD_D_INNER, SSD_STATE), F32)], sem=("arbitrary",), comm=comm)


def _head_mean_matrix():
    row = lax.broadcasted_iota(jnp.int32, (LANE, LANE), 0) // HEAD_DIM
    col = lax.broadcasted_iota(jnp.int32, (LANE, LANE), 1) // HEAD_DIM
    return (row == col).astype(F32)


def _head_sum2(v, ones_bd):
    hi = v.astype(BF16)
    lo = (v - hi.astype(F32)).astype(BF16)
    return _dot(hi, ones_bd, NN) + _dot(lo, ones_bd, NN)


def _head_norm(x, w, scale, ones_bd):
    ms = _head_sum2(x * x, ones_bd) * (1.0 / HEAD_DIM)
    return (x * lax.rsqrt(ms + EPS)) * (w * scale)


PRO_ROWS = 256
ATT_GROUP_FWD = 8
ATT_GROUP_BWD = 4
KEYS = 2 * ATT_BLK
NEG = -1e30
HALF = HEAD_DIM // 2


def _rows(start, size, dil):
    return pl.ds(start, size) if dil == 1 else pl.ds(start, size, stride=dil)


def _fill_bias(bias_ref):
    row = lax.broadcasted_iota(jnp.int32, (ATT_BLK, 2 * KEYS), 0)
    col = lax.broadcasted_iota(jnp.int32, (ATT_BLK, 2 * KEYS), 1) & (KEYS - 1)
    for first, off in ((0, 0), (1, ATT_BLK)):
        dist = off + row - col
        bias_ref[first] = jnp.where((dist >= 0) & (dist <= ATT_BLK), 0.0, NEG)


def _pair(a, b):
    return jnp.concatenate([jnp.broadcast_to(a, (ATT_BLK, KEYS)), jnp.broadcast_to(b, (ATT_BLK, KEYS))], axis=1)


def _split_heads(x, is_a):
    zero = jnp.zeros_like(x)
    return jnp.concatenate([jnp.where(is_a, x, zero), jnp.where(is_a, zero, x)], axis=0)


def _block_ids(b, nb):
    i = b & (nb - 1)
    q0 = pl.multiple_of(b * ATT_BLK, ATT_BLK)
    k0 = pl.multiple_of((b - jnp.minimum(i, 1)) * ATT_BLK, ATT_BLK)
    return pl.ds(q0, ATT_BLK), pl.ds(k0, KEYS), jnp.minimum(i, 1)


def _att_fwd(proj, qw, kw, comm=None):
    s = proj.shape[0]
    nblk = s // ATT_BLK
    assert all((s // d) // ATT_BLK >= 2 for d in DILATIONS)
    blk = lambda off: pl.BlockSpec((s, LANE), lambda i: (0, off // LANE + i))
    wspec = pl.BlockSpec((1, LANE), lambda i: (0, i))
    oblk = pl.BlockSpec((s, LANE), lambda i: (0, i))

    def body(q_ref, k_ref, v_ref, qw_ref, kw_ref, o_ref, lse_ref, qn, kn, q_cm, k_cm, v_cm, m_acc, l_acc, o_d, m_d, l_d, bias):
        ones_bd = _head_mean_matrix().astype(BF16)
        is_a = lax.broadcasted_iota(jnp.int32, (1, LANE), 1) < HEAD_DIM
        ones_ext = _split_heads(jnp.ones((KEYS, LANE), BF16), is_a)
        _fill_bias(bias)

        def pro(j, c):
            rows = pl.ds(pl.multiple_of(j * PRO_ROWS, PRO_ROWS), PRO_ROWS)
            qn[rows, :] = _head_norm(q_ref[rows, :], qw_ref[...], HEAD_DIM ** -0.5, ones_bd)
            kn[rows, :] = _head_norm(k_ref[rows, :], kw_ref[...], 1.0, ones_bd)
            return c

        lax.fori_loop(0, s // PRO_ROWS, pro, 0)

        for dil in DILATIONS:
            ln = s // dil
            nb = ln // ATT_BLK
            o_out, m_out, l_out = (o_ref, m_acc, l_acc) if dil == 1 else (o_d, m_d, l_d)
            for r in range(dil):
                def relayout(j, c, dil=dil, r=r, ln=ln):
                    j0 = pl.multiple_of(j * PRO_ROWS, PRO_ROWS)
                    src = _rows(r + dil * j0, PRO_ROWS, dil)
                    dst = pl.ds(r * ln + j0, PRO_ROWS)
                    q_cm[dst, :] = qn[src, :].astype(BF16)
                    k_cm[dst, :] = kn[src, :].astype(BF16)
                    v_cm[dst, :] = v_ref[src, :].astype(BF16)
                    return c

                lax.fori_loop(0, ln // PRO_ROWS, relayout, 0)

            def step(bg, c, nb=nb, o_out=o_out, m_out=m_out, l_out=l_out):
                ids = [_block_ids(bg * ATT_GROUP_FWD + u, nb) for u in range(ATT_GROUP_FWD)]
                kbs = [_split_heads(k_cm[krows, :], is_a) for _, krows, _ in ids]
                scs = [_dot(q_cm[qrows, :], kb, NT) + bias[first] for (qrows, _, first), kb in zip(ids, kbs)]
                mas = [jnp.max(sc[:, :KEYS], axis=-1, keepdims=True) for sc in scs]
                mbs = [jnp.max(sc[:, KEYS:], axis=-1, keepdims=True) for sc in scs]
                ps = [jnp.exp(sc - _pair(ma, mb)).astype(BF16) for sc, ma, mb in zip(scs, mas, mbs)]
                vbs = [jnp.concatenate([_split_heads(v_cm[krows, :], is_a), ones_ext], axis=1) for _, krows, _ in ids]
                ols = [_dot(p, vb, NN) for p, vb in zip(ps, vbs)]
                for (qrows, _, _), ol, ma, mb in zip(ids, ols, mas, mbs):
                    o_out[qrows, :] = ol[:, :LANE]
                    l_out[qrows, :] = ol[:, LANE:]
                    m_out[qrows, :] = jnp.where(is_a, ma, mb)
                return c

            lax.fori_loop(0, nblk // ATT_GROUP_FWD, step, 0)

            if dil > 1:
                for r in range(dil):
                    def merge(j, c, dil=dil, r=r, ln=ln):
                        j0 = pl.multiple_of(j * PRO_ROWS, PRO_ROWS)
                        nat = _rows(r + dil * j0, PRO_ROWS, dil)
                        cm = pl.ds(r * ln + j0, PRO_ROWS)
                        m_old, m_new = m_acc[nat, :], m_d[cm, :]
                        m = jnp.maximum(m_old, m_new)
                        a_old, a_new = jnp.exp(m_old - m), jnp.exp(m_new - m)
                        o_ref[nat, :] = a_old * o_ref[nat, :] + a_new * o_d[cm, :]
                        l_acc[nat, :] = a_old * l_acc[nat, :] + a_new * l_d[cm, :]
                        m_acc[nat, :] = m
                        return c

                    lax.fori_loop(0, ln // PRO_ROWS, merge, 0)

        def epi(j, c):
            rows = pl.ds(pl.multiple_of(j * PRO_ROWS, PRO_ROWS), PRO_ROWS)
            l = l_acc[rows, :]
            o_ref[rows, :] = o_ref[rows, :] / l
            lse_ref[rows, :] = m_acc[rows, :] + jnp.log(l)
            return c

        lax.fori_loop(0, s // PRO_ROWS, epi, 0)

    f = jax.ShapeDtypeStruct((s, ATT_D), F32)
    scr = pltpu.VMEM((s, LANE), F32)
    scb = pltpu.VMEM((s, LANE), BF16)
    return _pcall(
        body, (proj, proj, proj, qw, kw), name="att_fwd", grid=(ATT_D // LANE,),
        in_specs=[blk(OFF_Q), blk(OFF_K), blk(OFF_V), wspec, wspec], out_specs=[oblk, oblk], out_shape=[f, f],
        scratch_shapes=[scr, scr, scb, scb, scb, scr, scr, scr, scr, scr, pltpu.VMEM((2, ATT_BLK, 2 * KEYS), F32)],
        sem=("parallel",), comm=comm)


def _att_bwd(proj, do, stats, qw, kw, comm=None):
    s = proj.shape[0]
    nblk = s // ATT_BLK
    blk = lambda off: pl.BlockSpec((s, LANE), lambda i: (0, off // LANE + i))
    wspec = pl.BlockSpec((1, LANE), lambda i: (0, i))
    oblk = pl.BlockSpec((s, LANE), lambda i: (0, i))

    def body(q_ref, k_ref, v_ref, do_ref, st_ref, qw_ref, kw_ref, dq_ref, dk_ref, dv_ref, dqw_ref, dkw_ref,
             qn, kn, q_cm, do_cm, k_cm, v_cm, st_cm, dq_acc, dk_acc, dv_acc, dq_d, dk_d, dv_d, bias):
        ones_bd = _head_mean_matrix().astype(BF16)
        is_a = lax.broadcasted_iota(jnp.int32, (1, LANE), 1) < HEAD_DIM
        _fill_bias(bias)
        zero = jnp.zeros((PRO_ROWS, LANE), F32)

        def pro(j, c):
            rows = pl.ds(pl.multiple_of(j * PRO_ROWS, PRO_ROWS), PRO_ROWS)
            qn[rows, :] = _head_norm(q_ref[rows, :], qw_ref[...], HEAD_DIM ** -0.5, ones_bd)
            kn[rows, :] = _head_norm(k_ref[rows, :], kw_ref[...], 1.0, ones_bd)
            dk_acc[rows, :] = zero
            dv_acc[rows, :] = zero
            return c

        lax.fori_loop(0, s // PRO_ROWS, pro, 0)

        for dil in DILATIONS:
            ln = s // dil
            nb = ln // ATT_BLK
            dq_o, dk_o, dv_o = (dq_acc, dk_acc, dv_acc) if dil == 1 else (dq_d, dk_d, dv_d)
            for r in range(dil):
                def relayout(j, c, dil=dil, r=r, ln=ln):
                    j0 = pl.multiple_of(j * PRO_ROWS, PRO_ROWS)
                    src = _rows(r + dil * j0, PRO_ROWS, dil)
                    dst = pl.ds(r * ln + j0, PRO_ROWS)
                    q_cm[dst, :] = qn[src, :].astype(BF16)
                    k_cm[dst, :] = kn[src, :].astype(BF16)
                    v_cm[dst, :] = v_ref[src, :].astype(BF16)
                    do_cm[dst, :] = do_ref[src, :].astype(BF16)
                    st_cm[dst, :] = st_ref[src, :]
                    if dil > 1:
                        dk_d[dst, :] = zero
                        dv_d[dst, :] = zero
                    return c

                lax.fori_loop(0, ln // PRO_ROWS, relayout, 0)

            def step(bg, c, nb=nb, dq_o=dq_o, dk_o=dk_o, dv_o=dv_o):
                ids = [_block_ids(bg * ATT_GROUP_BWD + u, nb) for u in range(ATT_GROUP_BWD)]
                qbs = [q_cm[qrows, :] for qrows, _, _ in ids]
                dobs = [do_cm[qrows, :] for qrows, _, _ in ids]
                kbs = [_split_heads(k_cm[krows, :], is_a) for _, krows, _ in ids]
                vbs = [_split_heads(v_cm[krows, :], is_a) for _, krows, _ in ids]
                sts = [st_cm[qrows, :] for qrows, _, _ in ids]
                scs = [_dot(qb, kb, NT) + bias[first] for qb, kb, (_, _, first) in zip(qbs, kbs, ids)]
                dps = [_dot(dob, vb, NT) for dob, vb in zip(dobs, vbs)]
                ps = [jnp.exp(sc - _pair(st[:, 0:1], st[:, HEAD_DIM:HEAD_DIM + 1])) for sc, st in zip(scs, sts)]
                dss = [(p * (dp - _pair(st[:, HALF:HALF + 1], st[:, HEAD_DIM + HALF:HEAD_DIM + HALF + 1]))).astype(BF16)
                       for p, dp, st in zip(ps, dps, sts)]
                dqs = [_dot(ds, kb, NN) for ds, kb in zip(dss, kbs)]
                dkfs = [_dot(ds, qb, TN) for ds, qb in zip(dss, qbs)]
                dvfs = [_dot(p.astype(BF16), dob, TN) for p, dob in zip(ps, dobs)]
                for (qrows, krows, _), dq, dkf, dvf in zip(ids, dqs, dkfs, dvfs):
                    dq_o[qrows, :] = dq
                    dk_o[krows, :] += jnp.where(is_a, dkf[:KEYS], dkf[KEYS:])
                    dv_o[krows, :] += jnp.where(is_a, dvf[:KEYS], dvf[KEYS:])
                return c

            lax.fori_loop(0, nblk // ATT_GROUP_BWD, step, 0)

            if dil > 1:
                for r in range(dil):
                    def merge(j, c, dil=dil, r=r, ln=ln):
                        j0 = pl.multiple_of(j * PRO_ROWS, PRO_ROWS)
                        nat = _rows(r + dil * j0, PRO_ROWS, dil)
                        cm = pl.ds(r * ln + j0, PRO_ROWS)
                        dq_acc[nat, :] += dq_d[cm, :]
                        dk_acc[nat, :] += dk_d[cm, :]
                        dv_acc[nat, :] += dv_d[cm, :]
                        return c

                    lax.fori_loop(0, ln // PRO_ROWS, merge, 0)

        def back(dn_out, x, w, scale):
            r = lax.rsqrt(_head_sum2(x * x, ones_bd) * (1.0 / HEAD_DIM) + EPS)
            nrm = x * r
            dw = jnp.sum(dn_out * nrm, axis=0, keepdims=True) * scale
            dn = dn_out * (w * scale)
            return r * (dn - nrm * (_head_sum2(dn * nrm, ones_bd) * (1.0 / HEAD_DIM))), dw

        def epi(j, c):
            rows = pl.ds(pl.multiple_of(j * PRO_ROWS, PRO_ROWS), PRO_ROWS)
            dq, dqw = back(dq_acc[rows, :], q_ref[rows, :], qw_ref[...], HEAD_DIM ** -0.5)
            dk, dkw = back(dk_acc[rows, :], k_ref[rows, :], kw_ref[...], 1.0)
            dq_ref[rows, :] = dq.astype(BF16)
            dk_ref[rows, :] = dk.astype(BF16)
            dv_ref[rows, :] = dv_acc[rows, :].astype(BF16)
            return (c[0] + dqw, c[1] + dkw)

        zrow = jnp.zeros((1, LANE), F32)
        dqw, dkw = lax.fori_loop(0, s // PRO_ROWS, epi, (zrow, zrow))
        dqw_ref[...] = dqw
        dkw_ref[...] = dkw

    o = jax.ShapeDtypeStruct((s, ATT_D), BF16)
    ov = jax.ShapeDtypeStruct((1, ATT_D), F32)
    scr = pltpu.VMEM((s, LANE), F32)
    scb = pltpu.VMEM((s, LANE), BF16)
    return _pcall(
        body, (proj, proj, proj, do, stats, qw, kw), name="att_bwd", grid=(ATT_D // LANE,),
        in_specs=[blk(OFF_Q), blk(OFF_K), blk(OFF_V), oblk, oblk, wspec, wspec],
        out_specs=[oblk, oblk, oblk, wspec, wspec], out_shape=[o, o, o, ov, ov],
        scratch_shapes=[scr, scr, scb, scb, scb, scb, scr, scr, scr, scr, scr, scr, scr, pltpu.VMEM((2, ATT_BLK, 2 * KEYS), F32)],
        sem=("parallel",), comm=comm)


def _att_norm_fwd(o, nw):
    s = o.shape[0]
    row = pl.BlockSpec((ROW_TILE, ATT_D), lambda i: (i, 0))
    vec = pl.BlockSpec((1, ATT_D), lambda i: (0, 0))

    def body(o_ref, nw_ref, y_ref):
        o = o_ref[...]
        r = lax.rsqrt(jnp.mean(o * o, axis=-1, keepdims=True) + EPS)
        y_ref[...] = (o * r * nw_ref[...]).astype(BF16)

    return pl.pallas_call(body, name="att_norm_fwd", grid=(s // ROW_TILE,), in_specs=[row, vec], out_specs=row,
                          out_shape=jax.ShapeDtypeStruct((s, ATT_D), BF16), compiler_params=_cparams(("parallel",)))(o, nw)


def _att_norm_bwd(dycat, o, lse, nw):
    s = o.shape[0]
    row = pl.BlockSpec((ROW_TILE, ATT_D), lambda i: (i, 0))
    vec = pl.BlockSpec((1, ATT_D), lambda i: (0, 0))

    def body(dy_ref, o_ref, lse_ref, nw_ref, do_ref, st_ref, dnw_ref):
        @pl.when(pl.program_id(0) == 0)
        def _():
            dnw_ref[...] = jnp.zeros_like(dnw_ref)

        o = o_ref[...]
        dy = dy_ref[...]
        r = lax.rsqrt(jnp.mean(o * o, axis=-1, keepdims=True) + EPS)
        nrm = o * r
        dnw_ref[...] += jnp.sum(dy * nrm, axis=0, keepdims=True)
        dn = dy * nw_ref[...]
        do = r * (dn - nrm * jnp.mean(dn * nrm, axis=-1, keepdims=True))
        do_ref[...] = do
        ones_bd = _head_mean_matrix().astype(BF16)
        prod = do * o
        delta = jnp.concatenate([_head_sum2(prod[:, j * LANE:(j + 1) * LANE], ones_bd) for j in range(ATT_D // LANE)], axis=1)
        lane = lax.broadcasted_iota(jnp.int32, (1, ATT_D), 1)
        st_ref[...] = jnp.where((lane & (HEAD_DIM - 1)) < HALF, lse_ref[...], delta)

    f = jax.ShapeDtypeStruct((s, ATT_D), F32)
    return pl.pallas_call(
        body, name="att_norm_bwd", grid=(s // ROW_TILE,),
        in_specs=[pl.BlockSpec((ROW_TILE, ATT_D), lambda i: (i, 1)), row, row, vec], out_specs=[row, row, vec],
        out_shape=[f, f, jax.ShapeDtypeStruct((1, ATT_D), F32)],
        compiler_params=_cparams(("arbitrary",)))(dycat, o, lse, nw)


def _ada_fwd(c_all, w_ada):
    def body(c_ref, w_ref, o_ref):
        cv = c_ref[...]
        o_ref[...] = _dot((cv * _sigmoid(cv)).astype(BF16), w_ref[...].astype(BF16), NN)

    return pl.pallas_call(body, name="ada_fwd", out_shape=jax.ShapeDtypeStruct((c_all.shape[0], w_ada.shape[1]), F32),
                          compiler_params=_cparams())(c_all, w_ada)


def _adamw_math(g, w, m, v):
    m_new = ADAM_B1 * m + (1.0 - ADAM_B1) * g
    v_new = ADAM_B2 * v + (1.0 - ADAM_B2) * (g * g)
    m_hat = m_new / (1.0 - ADAM_B1 ** ADAM_STEP)
    v_hat = v_new / (1.0 - ADAM_B2 ** ADAM_STEP)
    delta = -ADAM_LR * (m_hat / (jnp.sqrt(v_hat) + ADAM_EPS) + ADAM_WD * w)
    return delta, m_new, v_new


def _ada_bwd_adamw(c_all, dmod_cols, w, m, v):
    rows, cols = w.shape
    tr = 256
    blk = pl.BlockSpec((tr, cols), lambda i: (i, 0))

    def body(c_ref, d_ref, w_ref, m_ref, v_ref, g_ref, dl_ref, mo_ref, vo_ref):
        cv = c_ref[...]
        ca = cv * _sigmoid(cv)
        g = ca[:, 0:1] * d_ref[0:1, :]
        for b in range(1, N_DEV):
            g = g + ca[:, b:b + 1] * d_ref[b:b + 1, :]
        g_ref[...] = g
        dl_ref[...], mo_ref[...], vo_ref[...] = _adamw_math(g, w_ref[...], m_ref[...], v_ref[...])

    o = jax.ShapeDtypeStruct((rows, cols), F32)
    return pl.pallas_call(
        body, name="ada_bwd_adamw", grid=(rows // tr,),
        in_specs=[pl.BlockSpec((tr, N_DEV), lambda i: (i, 0)), pl.BlockSpec((N_DEV, cols), lambda i: (0, 0)), blk, blk, blk],
        out_specs=[blk] * 4, out_shape=[o, o, o, o], compiler_params=_cparams(("parallel",)))(c_all.T, dmod_cols, w, m, v)


def _reduce_adamw(slabs, w, m, v, name):
    rows, cols = w.shape
    if rows % 128 == 0:
        tr, steps = 128, rows // 128
        blk = pl.BlockSpec((tr, cols), lambda i: (i, 0))
        sblk = pl.BlockSpec((N_DEV, tr, cols), lambda i: (0, i, 0))
    else:
        tc, steps = 256, cols // 256
        blk = pl.BlockSpec((rows, tc), lambda i: (0, i))
        sblk = pl.BlockSpec((N_DEV, rows, tc), lambda i: (0, 0, i))

    def body(s_ref, w_ref, m_ref, v_ref, g_ref, dl_ref, mo_ref, vo_ref):
        g = s_ref[0].astype(F32)
        for dev in range(1, N_DEV):
            g = g + s_ref[dev].astype(F32)
        g_ref[...] = g
        dl_ref[...], mo_ref[...], vo_ref[...] = _adamw_math(g, w_ref[...], m_ref[...], v_ref[...])

    o = jax.ShapeDtypeStruct((rows, cols), F32)
    return pl.pallas_call(
        body, name=name, grid=(steps,), in_specs=[sblk, blk, blk, blk],
        out_specs=[blk] * 4, out_shape=[o, o, o, o], compiler_params=_cparams(("parallel",)))(slabs, w, m, v)


def _small_reduce_adamw(gathered, w, m, v):
    def body(s_ref, w_ref, m_ref, v_ref, g_ref, dl_ref, mo_ref, vo_ref):
        g = s_ref[0]
        for dev in range(1, N_DEV):
            g = g + s_ref[dev]
        g_ref[...] = g
        dl_ref[...], mo_ref[...], vo_ref[...] = _adamw_math(g, w_ref[...], m_ref[...], v_ref[...])

    o = jax.ShapeDtypeStruct(w.shape, F32)
    return pl.pallas_call(body, name="small_reduce_adamw", out_shape=[o, o, o, o], compiler_params=_cparams())(gathered, w, m, v)


def _adamw_small(g, w, m, v, name):
    def body(g_ref, w_ref, m_ref, v_ref, dl_ref, mo_ref, vo_ref):
        dl_ref[...], mo_ref[...], vo_ref[...] = _adamw_math(g_ref[...], w_ref[...], m_ref[...], v_ref[...])

    o = jax.ShapeDtypeStruct(w.shape, F32)
    return pl.pallas_call(body, name=name, out_shape=[o, o, o], compiler_params=_cparams())(g, w, m, v)


class _Exchange:
    def __init__(self, arrs, scatter):
        self.arrs, self.scatter, self.n = list(arrs), scatter, len(arrs)
        hbm = pl.BlockSpec(memory_space=pltpu.HBM)
        self.in_specs = [hbm] * self.n
        self.out_specs = [hbm] * self.n
        self.out_shape = [jax.ShapeDtypeStruct(a.shape if scatter else (N_DEV,) + a.shape, a.dtype) for a in self.arrs]
        self.scratch = [pltpu.SemaphoreType.DMA((self.n * (N_DEV - 1),)), pltpu.SemaphoreType.DMA((self.n * (N_DEV - 1),)),
                        pltpu.SemaphoreType.DMA((self.n,))]

    def _local(self, ins, outs, sems):
        me = 4 * lax.axis_index("x") + 2 * lax.axis_index("y") + lax.axis_index("c")
        return [pltpu.make_async_copy(ins[a].at[me] if self.scatter else ins[a], outs[a].at[me], sems[2].at[a])
                for a in range(self.n)]

    def _remote(self, ins, outs, sems, arriving):
        send_sems, recv_sems, _ = sems
        x, y, c = lax.axis_index("x"), lax.axis_index("y"), lax.axis_index("c")
        me = 4 * x + 2 * y + c
        remote = []
        for a in range(self.n):
            for k in range(1, N_DEV):
                px = 1 - x if k & 4 else x
                py = 1 - y if k & 2 else y
                pc = 1 - c if k & 1 else c
                peer = 4 * px + 2 * py + pc
                sem = a * (N_DEV - 1) + k - 1
                remote.append(pltpu.make_async_remote_copy(
                    src_ref=ins[a].at[peer] if self.scatter else ins[a], dst_ref=outs[a].at[peer if arriving else me],
                    send_sem=send_sems.at[sem], recv_sem=recv_sems.at[sem], device_id=(px, py, pc), device_id_type=MESH_IDS))
        return remote

    def start(self, ins, outs, sems):
        for cp in self._local(ins, outs, sems) + self._remote(ins, outs, sems, arriving=False):
            cp.start()

    def forward(self, ins, outs, sems):
        pass

    def wait(self, ins, outs, sems):
        for send, arrival in zip(self._remote(ins, outs, sems, arriving=False), self._remote(ins, outs, sems, arriving=True)):
            send.wait_send()
            arrival.wait_recv()
        for cp in self._local(ins, outs, sems):
            cp.wait()


class _Gather2(_Exchange):
    def __init__(self, arrs):
        super().__init__(arrs, scatter=False)

    def _copies(self, ins, outs, sems):
        send_sems, recv_sems, _ = sems
        x, y, c = lax.axis_index("x"), lax.axis_index("y"), lax.axis_index("c")
        sibling = (x, y, 1 - c)
        chips = [(1 - x, y), (x, 1 - y), (1 - x, 1 - y)]
        first, passed, landed = [], [], []
        for a in range(self.n):
            def copy(k, block, to, src=None, a=a):
                slab = outs[a].at[4 * block[0] + 2 * block[1] + block[2]]
                return pltpu.make_async_remote_copy(
                    src_ref=slab if src is None else src, dst_ref=slab, send_sem=send_sems.at[a * (N_DEV - 1) + k],
                    recv_sem=recv_sems.at[a * (N_DEV - 1) + k], device_id=to, device_id_type=MESH_IDS)

            first.append(copy(0, (x, y, c), sibling, src=ins[a]))
            landed.append(copy(0, sibling, sibling))
            for j, chip in enumerate(chips):
                first.append(copy(1 + j, (x, y, c), (*chip, c), src=ins[a]))
                passed.append((copy(1 + j, (*chip, c), sibling), copy(4 + j, (*chip, c), sibling)))
                landed.append(copy(4 + j, (*chip, 1 - c), sibling))
        return first, passed, landed

    def start(self, ins, outs, sems):
        for cp in self._local(ins, outs, sems) + self._copies(ins, outs, sems)[0]:
            cp.start()

    def forward(self, ins, outs, sems):
        for arrival, onward in self._copies(ins, outs, sems)[1]:
            arrival.wait_recv()
            onward.start()

    def wait(self, ins, outs, sems):
        first, passed, landed = self._copies(ins, outs, sems)
        for arrival in landed:
            arrival.wait_recv()
        for cp in first + [onward for _, onward in passed]:
            cp.wait_send()
        for cp in self._local(ins, outs, sems):
            cp.wait()


def _split_comm_refs(refs, n_in, n_out, n_scr, comm):
    nc = comm.n if comm is not None else 0
    ns = 3 if comm is not None else 0
    pos, groups = 0, []
    for cnt in (n_in, nc, n_out, nc, n_scr, ns):
        groups.append(refs[pos:pos + cnt])
        pos += cnt
    assert pos == len(refs), (pos, len(refs))
    return groups


def _pcall(body, args, *, name, grid, in_specs, out_specs, out_shape, scratch_shapes=(), sem=None, comm=None):
    in_specs, out_specs, out_shape, scratch_shapes = list(in_specs), list(out_specs), list(out_shape), list(scratch_shapes)
    n_in, n_out, n_scr = len(in_specs), len(out_specs), len(scratch_shapes)
    if comm is None:
        kernel_body = body
    else:
        def kernel_body(*refs):
            ins, cins, outs, couts, scr, sems = _split_comm_refs(refs, n_in, n_out, n_scr, comm)
            ids = [pl.program_id(a) for a in range(len(grid))]
            first, last = ids[0] == 0, ids[0] == grid[0] - 1
            for a in range(1, len(grid)):
                first, last = first & (ids[a] == 0), last & (ids[a] == grid[a] - 1)

            middle = ids[0] == (2 * grid[0]) // 3
            for a in range(1, len(grid)):
                middle = middle & (ids[a] == 0)

            @pl.when(first)
            def _():
                comm.start(cins, couts, sems)

            @pl.when(middle)
            def _():
                comm.forward(cins, couts, sems)

            body(*ins, *outs, *scr)

            @pl.when(last)
            def _():
                comm.wait(cins, couts, sems)

        in_specs, out_specs, out_shape = in_specs + comm.in_specs, out_specs + comm.out_specs, out_shape + comm.out_shape
        scratch_shapes, args = scratch_shapes + comm.scratch, list(args) + comm.arrs
        sem = ("arbitrary",) * len(grid)
    res = pl.pallas_call(kernel_body, name=name, grid=grid, in_specs=in_specs, out_specs=out_specs, out_shape=out_shape,
                         scratch_shapes=scratch_shapes, compiler_params=_cparams(sem))(*args)
    return res[:n_out], res[n_out:]


def _exchange(arrs, name, scatter):
    ex = _Exchange(arrs, scatter=True) if scatter else _Gather2(arrs)

    def body(*refs):
        _, ins, _, outs, _, sems = _split_comm_refs(refs, 0, 0, 0, ex)
        ex.start(ins, outs, sems)
        ex.forward(ins, outs, sems)
        ex.wait(ins, outs, sems)

    return pl.pallas_call(body, name=name, in_specs=ex.in_specs, out_specs=ex.out_specs, out_shape=ex.out_shape,
                          scratch_shapes=ex.scratch)(*arrs)


def _pad_lanes(v, width=LANE):
    return jnp.pad(v, ((0, 0), (0, width - v.shape[1])))


def _shards_to_cols(g):
    return jnp.transpose(g, (1, 0, 2)).reshape(g.shape[1], N_DEV * g.shape[2])


def _cols_to_shards(w):
    return w.astype(BF16).reshape(w.shape[0], N_DEV, w.shape[1] // N_DEV).transpose(1, 0, 2)


def _local_step(x, tgt, mod, w_in_pt, conv_w, conv_b, dt_bias, a_log, d_skip, ssd_norm_w, q_norm_w, k_norm_w,
                attn_norm_w, w_out_sh, w_ff1_sh, w_ff2_sh, norm1_w, norm2_w):
    shift1, scale1, gate1, shift2, scale2, gate2 = [mod[i:i + 1] for i in range(N_MOD)]
    dtb, alog, dsk = _pad_lanes(dt_bias), _pad_lanes(a_log), _pad_lanes(d_skip)
    qw, kw = jnp.tile(q_norm_w, (1, ATT_HEADS)), jnp.tile(k_norm_w, (1, ATT_HEADS))

    h1 = _norm_mod_fwd(x, norm1_w, scale1, shift1, "norm1_fwd")
    proj = _matmul(h1, w_in_pt, tb=True, tm=2048, tn=896, tk=1024, name="in_proj")
    pre, act = _conv_fwd(proj, conv_w, conv_b)
    ypre, y_ssd, hall = _ssd_fwd(proj, act, dtb, alog, dsk, ssd_norm_w)
    (o_att, lse), (w_out_g, w_ff1_g, w_ff2_g) = _att_fwd(proj, qw, kw, comm=_Gather2([w_out_sh, w_ff1_sh, w_ff2_sh]))
    w_out = w_out_g.reshape(2 * D_MODEL, D_MODEL)
    w_ff1 = _shards_to_cols(w_ff1_g)
    w_ff2 = w_ff2_g.reshape(D_FF, D_MODEL)
    y_att = _att_norm_fwd(o_att, attn_norm_w)
    ycat = jnp.concatenate([y_ssd, y_att], axis=1)
    mix = _matmul(ycat, w_out, tm=1024, tn=1024, tk=2048, name="out_proj")
    x1, h2 = _norm_mod_fwd(x, norm2_w, scale2, shift2, "norm2_fwd", res=mix, gate=gate1)
    u, act_ff = _matmul(h2, w_ff1, tm=1024, tn=1024, tk=1024, name="ff1", mode="relu2")
    ff = _matmul(act_ff, w_ff2, tm=512, tn=1024, tk=4096, name="ff2")
    loss, dout, dff, dgate2 = _loss_head(x1, ff, gate2, tgt)

    du = _matmul(dff, w_ff2, tb=True, tm=1024, tn=1024, tk=1024, out_dtype=BF16, name="ff2_dx", mode="drelu2", u=u)
    g_ff2 = _matmul(act_ff, dff, ta=True, tm=512, tn=1024, tk=4096, out_dtype=BF16, name="ff2_dw")
    dh2 = _matmul(du, w_ff1, tb=True, tm=512, tn=1024, tk=4096, name="ff1_dx")
    g_ff1 = _matmul(h2, du, ta=True, tm=512, tn=1024, tk=4096, out_dtype=BF16, name="ff1_dw")
    dx1, dshift2, dscale2, g_norm2, dmix, dgate1 = _norm_mod_bwd(dh2, x1, dout, norm2_w, scale2, "norm2_bwd", gate=gate1, mix=mix)

    dycat = _matmul(dmix, w_out, tb=True, tm=1024, tn=1024, tk=1024, name="out_proj_dx")
    g_out = _matmul(ycat, dmix, ta=True, tm=512, tn=1024, tk=4096, out_dtype=BF16, name="out_proj_dw")
    do, stats, g_attn_norm = _att_norm_bwd(dycat, o_att, lse, attn_norm_w)
    ff_slabs = [_cols_to_shards(g_ff1), g_ff2.astype(BF16).reshape(N_DEV, D_FF // N_DEV, D_MODEL)]
    (dq, dk, dv, dqw, dkw), (s_ff1, s_ff2) = _att_bwd(proj, do, stats, qw, kw, comm=_Exchange(ff_slabs, scatter=True))
    out_slabs = [g_out.astype(BF16).reshape(N_DEV, 2 * D_MODEL // N_DEV, D_MODEL)]
    (dz, dact, ddtr, da, g_dsk, g_dtb, g_ssd_norm), (s_out,) = _ssd_bwd(
        dycat, ypre, proj, act, hall, dtb, alog, dsk, ssd_norm_w, comm=_Exchange(out_slabs, scatter=True))
    dxbc, g_conv_w, g_conv_b = _conv_bwd(dact, pre, proj, conv_w)
    dproj = jnp.concatenate([dz, dxbc, dq, dk, dv, ddtr], axis=1)
    g_in_pt = _matmul(dproj, h1, ta=True, tm=896, tn=1024, tk=4096, out_dtype=BF16, name="in_proj_dw")
    in_slabs = [_unpack_w_in_rows(g_in_pt).reshape(N_DEV, IN_W // N_DEV, D_MODEL)]
    dh1, (s_in,) = _matmul(dproj, w_in_pt, tm=512, tn=1024, tk=IN_WP, name="in_proj_dx",
                           comm=_Exchange(in_slabs, scatter=True))
    grad_x, dshift1, dscale1, g_norm1 = _norm_mod_bwd(dh1, x, dx1, norm1_w, scale1, "norm1_bwd")

    dmod = jnp.concatenate([dshift1, dscale1, dgate1, dshift2, dscale2, dgate2], axis=0)
    g_alog = da[:, :SSD_HEADS] * (-jnp.exp(a_log))
    g_qw = dqw.reshape(ATT_HEADS, HEAD_DIM).sum(axis=0, keepdims=True)
    g_kw = dkw.reshape(ATT_HEADS, HEAD_DIM).sum(axis=0, keepdims=True)
    return dict(loss=loss, grad_x=grad_x, dmod=dmod, norm1_w=g_norm1, norm2_w=g_norm2, w_in=s_in, conv_w=g_conv_w,
                conv_b=g_conv_b, dt_bias=g_dtb[:, :SSD_HEADS], a_log=g_alog, d_skip=g_dsk[:, :SSD_HEADS],
                ssd_norm_w=g_ssd_norm, q_norm_w=g_qw, k_norm_w=g_kw, attn_norm_w=g_attn_norm, w_out=s_out,
                w_ff1=s_ff1, w_ff2=s_ff2)


def _pack_w_in_rows(wt_full):
    o_dt = SSD_D_INNER + CONV_CH
    o_q = o_dt + SSD_HEADS
    pad = jnp.zeros((LANE - SSD_HEADS, wt_full.shape[1]), wt_full.dtype)
    return jnp.concatenate([wt_full[:o_dt], wt_full[o_q:], wt_full[o_dt:o_q], pad], axis=0)


def _unpack_w_in_rows(gt_p):
    return jnp.concatenate([gt_p[:OFF_Q], gt_p[OFF_DT:OFF_DT + SSD_HEADS], gt_p[OFF_Q:OFF_DT]], axis=0)


MISC_FIELDS = (("dt_bias", SSD_HEADS), ("a_log", SSD_HEADS), ("d_skip", SSD_HEADS), ("q_norm_w", HEAD_DIM), ("k_norm_w", HEAD_DIM))
SMALL_LAYOUT = (("b_ada", 6), ("norm1_w", 1), ("norm2_w", 1), ("conv_w", 8), ("conv_b", 2), ("ssd_norm_w", 1),
                ("attn_norm_w", 1), ("misc", 1))


def _pack_small(vals):
    rows = []
    for name, nrow in SMALL_LAYOUT:
        if name == "misc":
            misc = jnp.concatenate([vals[f].reshape(1, n) for f, n in MISC_FIELDS], axis=1)
            rows.append(_pad_lanes(misc, D_MODEL))
        elif name in vals:
            rows.append(vals[name].reshape(nrow, D_MODEL))
        else:
            rows.append(jnp.zeros((nrow, D_MODEL), F32))
    used = sum(n for _, n in SMALL_LAYOUT)
    rows.append(jnp.zeros((SMALL_ROWS - used, D_MODEL), F32))
    return jnp.concatenate(rows, axis=0)


def _unpack_small(packed):
    out, r = {}, 0
    for name, nrow in SMALL_LAYOUT:
        blk = packed[r:r + nrow]
        r += nrow
        if name == "misc":
            c0 = 0
            for f, n in MISC_FIELDS:
                out[f] = blk[:, c0:c0 + n]
                c0 += n
        elif name == "b_ada":
            out[name] = blk.reshape(1, N_MOD * D_MODEL)
        elif name == "conv_w":
            out[name] = blk.reshape(CONV_K, CONV_CH)
        elif name == "conv_b":
            out[name] = blk.reshape(1, CONV_CH)
        else:
            out[name] = blk
    return out


WEIGHT_NAMES = ("norm1_w", "norm2_w", "w_ada", "b_ada", "w_in", "conv_w", "conv_b", "dt_bias", "a_log", "d_skip",
                "ssd_norm_w", "q_norm_w", "k_norm_w", "attn_norm_w", "w_out", "w_ff1", "w_ff2")
SMALL_NAMES = ("norm1_w", "norm2_w", "b_ada", "conv_b", "dt_bias", "a_log", "d_skip", "ssd_norm_w", "q_norm_w",
               "k_norm_w", "attn_norm_w")


def kernel(x, c, norm1_w, norm2_w, w_ada, b_ada, w_in, conv_w, conv_b, dt_bias, a_log, d_skip, ssd_norm_w, q_norm_w, k_norm_w, attn_norm_w, w_out, w_ff1, w_ff2, loss_target, m_norm1_w, m_norm2_w, m_w_ada, m_b_ada, m_w_in, m_conv_w, m_conv_b, m_dt_bias, m_a_log, m_d_skip, m_ssd_norm_w, m_q_norm_w, m_k_norm_w, m_attn_norm_w, m_w_out, m_w_ff1, m_w_ff2, v_norm1_w, v_norm2_w, v_w_ada, v_b_ada, v_w_in, v_conv_w, v_conv_b, v_dt_bias, v_a_log, v_d_skip, v_ssd_norm_w, v_q_norm_w, v_k_norm_w, v_attn_norm_w, v_w_out, v_w_ff1, v_w_ff2):
    args = dict(locals())
    w = {n: args[n] for n in WEIGHT_NAMES}
    m = {n: args["m_" + n] for n in WEIGHT_NAMES}
    v = {n: args["v_" + n] for n in WEIGHT_NAMES}
    me = 4 * lax.axis_index("x") + 2 * lax.axis_index("y") + lax.axis_index("c")

    c_rows = jnp.pad(c, ((0, 7), (0, 0)))
    w_in_t, m_in_t, v_in_t = [jnp.transpose(t["w_in"][0]) for t in (w, m, v)]
    c_g, conv_g, w_in_g = _exchange([c_rows, w["conv_w"][0], w_in_t.astype(BF16)], "gather_w_in", scatter=False)
    c_all = c_g[:, 0, :]
    conv_full = _shards_to_cols(conv_g)
    w_in_pt = _pack_w_in_rows(w_in_g.reshape(IN_W, D_MODEL))

    mod_part = _ada_fwd(c_all, w["w_ada"][0])
    (mod_g,) = _exchange([mod_part], "gather_mod", scatter=False)
    mod_mine = lax.dynamic_index_in_dim(mod_g, me, axis=1, keepdims=False).reshape(1, N_MOD * D_MODEL) + w["b_ada"]
    mod = mod_mine.reshape(N_MOD, D_MODEL)

    res = _local_step(x[0], loss_target[0], mod, w_in_pt, conv_full, w["conv_b"], w["dt_bias"], w["a_log"], w["d_skip"],
                      w["ssd_norm_w"], w["q_norm_w"], w["k_norm_w"], w["attn_norm_w"], w["w_out"][0].astype(BF16),
                      w["w_ff1"][0].astype(BF16), w["w_ff2"][0].astype(BF16), w["norm1_w"], w["norm2_w"])

    small_vals = {n: res[n] for n in SMALL_NAMES if n != "b_ada"}
    small_vals["b_ada"] = res["dmod"]
    small_vals["conv_w"] = res["conv_w"]
    (small_g,) = _exchange([_pack_small(small_vals)], "gather_small", scatter=False)

    grads, delta, new_m, new_v = {}, {}, {}, {}
    for name in ("w_out", "w_ff1", "w_ff2"):
        outs = _reduce_adamw(res[name], w[name][0], m[name][0], v[name][0], "adamw_" + name)
        grads[name], delta[name], new_m[name], new_v[name] = [o[None] for o in outs]
    outs = _reduce_adamw(res["w_in"], w_in_t, m_in_t, v_in_t, "adamw_w_in")
    grads["w_in"], delta["w_in"], new_m["w_in"], new_v["w_in"] = [jnp.transpose(o)[None] for o in outs]

    sm = _small_reduce_adamw(small_g, _pack_small({n: w[n] for n in SMALL_NAMES}), _pack_small({n: m[n] for n in SMALL_NAMES}),
                             _pack_small({n: v[n] for n in SMALL_NAMES}))
    sm = [_unpack_small(p) for p in sm]
    for n in SMALL_NAMES:
        grads[n], delta[n], new_m[n], new_v[n] = [p[n] for p in sm]
    shard_w = CONV_CH // N_DEV
    g_conv = lax.dynamic_slice_in_dim(sm[0]["conv_w"], me * shard_w, shard_w, axis=1)
    cw = _adamw_small(g_conv, w["conv_w"][0], m["conv_w"][0], v["conv_w"][0], "adamw_conv_w")
    grads["conv_w"] = g_conv[None]
    delta["conv_w"], new_m["conv_w"], new_v["conv_w"] = [o[None] for o in cw]

    ada_w = w_ada.shape[2]
    dmod_all = small_g[:, :N_MOD, :].reshape(N_DEV, N_MOD * D_MODEL)
    dmod_cols = lax.dynamic_slice_in_dim(dmod_all, me * ada_w, ada_w, axis=1)
    outs = _ada_bwd_adamw(c_all, dmod_cols, w["w_ada"][0], m["w_ada"][0], v["w_ada"][0])
    grads["w_ada"], delta["w_ada"], new_m["w_ada"], new_v["w_ada"] = [o[None] for o in outs]

    loss = lax.psum(res["loss"][0, 0], ("x", "y", "c"))
    return (loss, res["grad_x"][None], *[grads[n] for n in WEIGHT_NAMES], *[delta[n] for n in WEIGHT_NAMES],
            *[new_m[n] for n in WEIGHT_NAMES], *[new_v[n] for n in WEIGHT_NAMES])
```

```python
import functools

import jax
import jax.numpy as jnp
from jax import lax
from jax.experimental import pallas as pl
from jax.experimental.pallas import tpu as pltpu

F32 = jnp.float32
BF16 = jnp.bfloat16
HIGHEST = lax.Precision.HIGHEST
MESH_IDS = pl.DeviceIdType.MESH

N_DEV = 8
D_MODEL = 1024
HEAD_DIM = 64
SSD_HEADS = 16
SSD_GROUPS = 4
HEADS_PER_GROUP = SSD_HEADS // SSD_GROUPS
SSD_STATE = 128
SSD_CHUNK = 128
SSD_D_INNER = SSD_HEADS * HEAD_DIM
GROUP_WIDTH = SSD_D_INNER // SSD_GROUPS
CONV_K = 4
CONV_CH = SSD_D_INNER + 2 * SSD_GROUPS * SSD_STATE
ATT_HEADS = 16
ATT_D = ATT_HEADS * HEAD_DIM
ATT_BLK = 128
DILATIONS = (1, 4, 16)
D_FF = 4 * D_MODEL
N_MOD = 6
EPS = 1e-6
IN_W = SSD_D_INNER + CONV_CH + SSD_HEADS + 3 * ATT_D
LANE = 128
OFF_Z, OFF_XBC, OFF_Q, OFF_K, OFF_V, OFF_DT = 0, 1024, 3072, 4096, 5120, 6144
IN_WP = OFF_DT + LANE

ADAM_LR, ADAM_B1, ADAM_B2, ADAM_EPS, ADAM_WD, ADAM_STEP = 0.001, 0.9, 0.999, 1e-08, 0.01, 10
VMEM_LIMIT = 56 * 1024 * 1024
ROW_TILE = 512
SMALL_ROWS = 24


def _cparams(sem=None):
    return pltpu.CompilerParams(dimension_semantics=sem, vmem_limit_bytes=VMEM_LIMIT)


def _sigmoid(v):
    return 1.0 / (1.0 + jnp.exp(-v))


def _softplus(v):
    y = jnp.exp(-jnp.abs(v))
    small = y * (1.0 - y * (0.5 - y * (1.0 / 3.0)))
    return jnp.maximum(v, 0.0) + jnp.where(y < 0.01, small, jnp.log(1.0 + y))


def _dot(a, b, dims, precision=None):
    return lax.dot_general(a, b, (dims, ((), ())), preferred_element_type=F32, precision=precision)


NN = ((1,), (0,))
NT = ((1,), (1,))
TN = ((0,), (0,))


def _matmul(a, b, *, ta=False, tb=False, tm, tn, tk, out_dtype=F32, name, mode=None, u=None, comm=None):
    m, k = (a.shape[1], a.shape[0]) if ta else a.shape
    n = b.shape[0] if tb else b.shape[1]
    assert m % tm == 0 and n % tn == 0 and k % tk == 0, (name, m, n, k)
    nk = k // tk
    a_spec = pl.BlockSpec((tk, tm), lambda i, j, kk: (kk, i)) if ta else pl.BlockSpec((tm, tk), lambda i, j, kk: (i, kk))
    b_spec = pl.BlockSpec((tn, tk), lambda i, j, kk: (j, kk)) if tb else pl.BlockSpec((tk, tn), lambda i, j, kk: (kk, j))
    o_spec = pl.BlockSpec((tm, tn), lambda i, j, kk: (i, j))
    dims = ((0,) if ta else (1,), (1,) if tb else (0,))
    n_out = 2 if mode == "relu2" else 1

    def body(*refs):
        if mode == "drelu2":
            a_ref, b_ref, u_ref = refs[:3]
            rest = refs[3:]
        else:
            a_ref, b_ref = refs[:2]
            u_ref = None
            rest = refs[2:]
        outs = rest[:n_out]
        part = _dot(a_ref[...], b_ref[...], dims)

        def finish(r):
            if mode == "relu2":
                outs[0][...] = r.astype(BF16)
                rr = jnp.maximum(r, 0.0)
                outs[1][...] = (rr * rr).astype(BF16)
            elif mode == "drelu2":
                outs[0][...] = (r * (2.0 * jnp.maximum(u_ref[...].astype(F32), 0.0))).astype(out_dtype)
            else:
                outs[0][...] = r.astype(out_dtype)

        if nk == 1:
            finish(part)
        else:
            acc = rest[n_out]
            kk = pl.program_id(2)

            @pl.when(kk == 0)
            def _():
                acc[...] = part

            @pl.when(kk > 0)
            def _():
                acc[...] += part

            @pl.when(kk == nk - 1)
            def _():
                finish(acc[...])

    in_specs = [a_spec, b_spec]
    args = [a, b]
    if mode == "drelu2":
        in_specs.append(o_spec)
        args.append(u)
    if mode == "relu2":
        out_shape = [jax.ShapeDtypeStruct((m, n), BF16), jax.ShapeDtypeStruct((m, n), BF16)]
    else:
        out_shape = [jax.ShapeDtypeStruct((m, n), out_dtype)]
    outs, comm_outs = _pcall(
        body, args, name=name, grid=(m // tm, n // tn, nk), in_specs=in_specs, out_specs=[o_spec] * n_out,
        out_shape=out_shape, scratch_shapes=[pltpu.VMEM((tm, tn), F32)] if nk > 1 else [],
        sem=("parallel", "parallel", "arbitrary"), comm=comm)
    res = tuple(outs) if mode == "relu2" else outs[0]
    return res if comm is None else (res, comm_outs)


def _norm_mod_fwd(x, nw, scale, shift, name, res=None, gate=None):
    s, d = x.shape
    row = pl.BlockSpec((ROW_TILE, d), lambda i: (i, 0))
    vec = pl.BlockSpec((1, d), lambda i: (0, 0))
    with_res = res is not None

    def body(*refs):
        if with_res:
            x_ref, res_ref, gate_ref, nw_ref, sc_ref, sh_ref, x1_ref, h_ref = refs
            xv = x_ref[...] + gate_ref[...] * res_ref[...]
            x1_ref[...] = xv
        else:
            x_ref, nw_ref, sc_ref, sh_ref, h_ref = refs
            xv = x_ref[...]
        r = lax.rsqrt(jnp.mean(xv * xv, axis=-1, keepdims=True) + EPS)
        h_ref[...] = ((xv * r) * nw_ref[...] * (1.0 + sc_ref[...]) + sh_ref[...]).astype(BF16)

    if with_res:
        in_specs = [row, row, vec, vec, vec, vec]
        args = (x, res, gate, nw, scale, shift)
        out_shape = (jax.ShapeDtypeStruct((s, d), F32), jax.ShapeDtypeStruct((s, d), BF16))
        out_specs = (row, row)
    else:
        in_specs = [row, vec, vec, vec]
        args = (x, nw, scale, shift)
        out_shape = jax.ShapeDtypeStruct((s, d), BF16)
        out_specs = row
    return pl.pallas_call(body, name=name, grid=(s // ROW_TILE,), in_specs=in_specs, out_specs=out_specs,
                          out_shape=out_shape, compiler_params=_cparams(("parallel",)))(*args)


def _norm_mod_bwd(dh, xin, dres, nw, scale, name, gate=None, mix=None):
    s, d = xin.shape
    row = pl.BlockSpec((ROW_TILE, d), lambda i: (i, 0))
    vec = pl.BlockSpec((1, d), lambda i: (0, 0))
    with_gate = gate is not None

    def body(*refs):
        if with_gate:
            dh_ref, x_ref, dres_ref, nw_ref, sc_ref, gate_ref, mix_ref, dx_ref, dsh_ref, dsc_ref, dnw_ref, dmix_ref, dg_ref = refs
        else:
            dh_ref, x_ref, dres_ref, nw_ref, sc_ref, dx_ref, dsh_ref, dsc_ref, dnw_ref = refs
        i = pl.program_id(0)

        @pl.when(i == 0)
        def _():
            dsh_ref[...] = jnp.zeros_like(dsh_ref)
            dsc_ref[...] = jnp.zeros_like(dsc_ref)
            dnw_ref[...] = jnp.zeros_like(dnw_ref)
            if with_gate:
                dg_ref[...] = jnp.zeros_like(dg_ref)

        xv = x_ref[...]
        dhv = dh_ref[...]
        r = lax.rsqrt(jnp.mean(xv * xv, axis=-1, keepdims=True) + EPS)
        nrm = xv * r
        one_sc = 1.0 + sc_ref[...]
        dhn = dhv * nrm
        dsh_ref[...] += jnp.sum(dhv, axis=0, keepdims=True)
        dsc_ref[...] += jnp.sum(dhn, axis=0, keepdims=True) * nw_ref[...]
        dnw_ref[...] += jnp.sum(dhn, axis=0, keepdims=True) * one_sc
        dn = dhv * (nw_ref[...] * one_sc)
        dx = dres_ref[...] + r * (dn - nrm * jnp.mean(dn * nrm, axis=-1, keepdims=True))
        dx_ref[...] = dx
        if with_gate:
            dmix_ref[...] = (gate_ref[...] * dx).astype(BF16)
            dg_ref[...] += jnp.sum(dx * mix_ref[...], axis=0, keepdims=True)

    vshape = jax.ShapeDtypeStruct((1, d), F32)
    in_specs = [row, row, row, vec, vec]
    args = [dh, xin, dres, nw, scale]
    out_shape = [jax.ShapeDtypeStruct((s, d), F32), vshape, vshape, vshape]
    out_specs = [row, vec, vec, vec]
    if with_gate:
        in_specs += [vec, row]
        args += [gate, mix]
        out_shape += [jax.ShapeDtypeStruct((s, d), BF16), vshape]
        out_specs += [row, vec]
    return pl.pallas_call(body, name=name, grid=(s // ROW_TILE,), in_specs=in_specs, out_specs=out_specs,
                          out_shape=out_shape, compiler_params=_cparams(("arbitrary",)))(*args)


def _loss_head(x1, ff, gate2, tgt):
    s, d = x1.shape
    row = pl.BlockSpec((ROW_TILE, d), lambda i: (i, 0))
    vec = pl.BlockSpec((1, d), lambda i: (0, 0))
    one = pl.BlockSpec((1, 1), lambda i: (0, 0))

    def body(x1_ref, ff_ref, g_ref, t_ref, loss_ref, dout_ref, dff_ref, dg_ref):
        i = pl.program_id(0)

        @pl.when(i == 0)
        def _():
            loss_ref[...] = jnp.zeros_like(loss_ref)
            dg_ref[...] = jnp.zeros_like(dg_ref)

        ffv = ff_ref[...]
        err = x1_ref[...] + g_ref[...] * ffv - t_ref[...]
        loss_ref[...] += (0.5 / d) * jnp.sum(err * err).reshape(1, 1)
        dout = err * (1.0 / d)
        dout_ref[...] = dout
        dff_ref[...] = (g_ref[...] * dout).astype(BF16)
        dg_ref[...] += jnp.sum(dout * ffv, axis=0, keepdims=True)

    return pl.pallas_call(
        body, name="loss_head", grid=(s // ROW_TILE,), in_specs=[row, row, vec, row],
        out_specs=[one, row, row, vec],
        out_shape=[jax.ShapeDtypeStruct((1, 1), F32), jax.ShapeDtypeStruct((s, d), F32),
                   jax.ShapeDtypeStruct((s, d), BF16), jax.ShapeDtypeStruct((1, d), F32)],
        compiler_params=_cparams(("arbitrary",)))(x1, ff, gate2, tgt)


CONV_COLS = 256
HALO = 8


def _shift_down(cur, halo, k):
    if k == 0:
        return cur
    rolled = pltpu.roll(cur, k, axis=0)
    top = jnp.where(lax.broadcasted_iota(jnp.int32, halo.shape, 0) < k, pltpu.roll(halo, k, axis=0), rolled[:HALO])
    return jnp.concatenate([top, rolled[HALO:]], axis=0)


def _shift_up(cur, halo, k):
    if k == 0:
        return cur
    t = cur.shape[0]
    rolled = pltpu.roll(cur, t - k, axis=0)
    bot = jnp.where(lax.broadcasted_iota(jnp.int32, halo.shape, 0) >= HALO - k, pltpu.roll(halo, HALO - k, axis=0),
                    rolled[t - HALO:])
    return jnp.concatenate([rolled[:t - HALO], bot], axis=0)


def _conv_fwd(proj, conv_w, conv_b):
    s = proj.shape[0]
    nr = s // ROW_TILE
    cb0 = OFF_XBC // CONV_COLS
    hb = ROW_TILE // HALO
    cur = pl.BlockSpec((ROW_TILE, CONV_COLS), lambda j, r: (r, cb0 + j))
    prev = pl.BlockSpec((HALO, CONV_COLS), lambda j, r: (jnp.maximum(r * hb - 1, 0), cb0 + j))
    out = pl.BlockSpec((ROW_TILE, CONV_COLS), lambda j, r: (r, j))

    def body(u_ref, up_ref, w_ref, b_ref, pre_ref, act_ref):
        r = pl.program_id(1)
        u = u_ref[...]
        halo = jnp.where(r > 0, up_ref[...], 0.0)
        acc = b_ref[...] + w_ref[CONV_K - 1:CONV_K, :] * u
        for k in range(1, CONV_K):
            acc = acc + w_ref[CONV_K - 1 - k:CONV_K - k, :] * _shift_down(u, halo, k)
        pre_ref[...] = acc
        act_ref[...] = acc * _sigmoid(acc)

    return pl.pallas_call(
        body, name="conv_fwd", grid=(CONV_CH // CONV_COLS, nr),
        in_specs=[cur, prev, pl.BlockSpec((CONV_K, CONV_COLS), lambda j, r: (0, j)),
                  pl.BlockSpec((1, CONV_COLS), lambda j, r: (0, j))],
        out_specs=[out, out],
        out_shape=[jax.ShapeDtypeStruct((s, CONV_CH), F32), jax.ShapeDtypeStruct((s, CONV_CH), F32)],
        compiler_params=_cparams(("parallel", "arbitrary")))(proj, proj, conv_w, conv_b)


def _conv_bwd(dact, pre, proj, conv_w):
    s = proj.shape[0]
    nr = s // ROW_TILE
    cb0 = OFF_XBC // CONV_COLS
    hb = ROW_TILE // HALO
    last_halo = s // HALO - 1
    cur = pl.BlockSpec((ROW_TILE, CONV_COLS), lambda j, r: (r, j))
    nxt = pl.BlockSpec((HALO, CONV_COLS), lambda j, r: (jnp.minimum((r + 1) * hb, last_halo), j))
    ucur = pl.BlockSpec((ROW_TILE, CONV_COLS), lambda j, r: (r, cb0 + j))
    uprev = pl.BlockSpec((HALO, CONV_COLS), lambda j, r: (jnp.maximum(r * hb - 1, 0), cb0 + j))
    wspec = pl.BlockSpec((CONV_K, CONV_COLS), lambda j, r: (0, j))
    bspec = pl.BlockSpec((1, CONV_COLS), lambda j, r: (0, j))

    def dsilu(p):
        sg = _sigmoid(p)
        return sg * (1.0 + p * (1.0 - sg))

    def body(da_ref, dan_ref, pre_ref, pren_ref, u_ref, up_ref, w_ref, du_ref, dw_ref, db_ref):
        r = pl.program_id(1)

        @pl.when(r == 0)
        def _():
            dw_ref[...] = jnp.zeros_like(dw_ref)
            db_ref[...] = jnp.zeros_like(db_ref)

        dpre = da_ref[...] * dsilu(pre_ref[...])
        dnext = jnp.where(r < nr - 1, dan_ref[...] * dsilu(pren_ref[...]), 0.0)
        u = u_ref[...]
        halo = jnp.where(r > 0, up_ref[...], 0.0)
        du = w_ref[CONV_K - 1:CONV_K, :] * dpre
        dws = [jnp.sum(dpre * u, axis=0, keepdims=True)]
        for k in range(1, CONV_K):
            du = du + w_ref[CONV_K - 1 - k:CONV_K - k, :] * _shift_up(dpre, dnext, k)
            dws.append(jnp.sum(dpre * _shift_down(u, halo, k), axis=0, keepdims=True))
        du_ref[...] = du.astype(BF16)
        dw_ref[...] += jnp.concatenate(dws[::-1], axis=0)
        db_ref[...] += jnp.sum(dpre, axis=0, keepdims=True)

    return pl.pallas_call(
        body, name="conv_bwd", grid=(CONV_CH // CONV_COLS, nr),
        in_specs=[cur, nxt, cur, nxt, ucur, uprev, wspec],
        out_specs=[cur, wspec, bspec],
        out_shape=[jax.ShapeDtypeStruct((s, CONV_CH), BF16), jax.ShapeDtypeStruct((CONV_K, CONV_CH), F32),
                   jax.ShapeDtypeStruct((1, CONV_CH), F32)],
        compiler_params=_cparams(("parallel", "arbitrary")))(dact, dact, pre, pre, proj, proj, conv_w)


def _ssd_common(dtr, dtb, alog):
    lane = lax.broadcasted_iota(jnp.int32, (1, LANE), 1)
    head_lane = lane < SSD_HEADS
    dt = jnp.where(head_lane, _softplus(dtr + dtb), 0.0)
    a = jnp.where(head_lane, -jnp.exp(alog), 0.0)
    row = lax.broadcasted_iota(jnp.int32, (SSD_CHUNK, SSD_CHUNK), 0)
    col = lax.broadcasted_iota(jnp.int32, (SSD_CHUNK, SSD_CHUNK), 1)
    tril = row >= col
    cs = _dot(tril.astype(F32), dt * a, NN, precision=HIGHEST)
    return dt, a, cs, cs.T, tril, lane


def _split_bf16(v, passes):
    terms, rest = [], v
    for _ in range(passes):
        t = rest.astype(BF16)
        terms.append(t)
        rest = rest - t.astype(F32)
    return terms


def _dot_split(v, m, dims, passes):
    out = None
    for t in _split_bf16(v, passes):
        part = _dot(t, m, dims)
        out = part if out is None else out + part
    return out


def _ssd_constants():
    heads = jnp.arange(LANE)[:, None]
    exp_mat = (heads == (jnp.arange(SSD_D_INNER)[None, :] // HEAD_DIM)).astype(BF16)
    ind4 = ((jnp.arange(SSD_HEADS * SSD_CHUNK)[:, None] // SSD_CHUNK) == jnp.arange(LANE)[None, :]).astype(BF16)
    return exp_mat, ind4


def _expand_heads(v):
    return jnp.repeat(v[:, :SSD_HEADS], HEAD_DIM, axis=1)


def _ssd_prep(dtr, dtb, alog, exp_mat):
    dt, a, cs, cst, tril, lane = _ssd_common(dtr, dtb, alog)
    return dt, a, cs, cst, tril, lane, _dot_split(dt, exp_mat, NN, 2), _dot_split(cs, exp_mat, NN, 3)


def _chunk_decay_rows(cs, g):
    parts = []
    for e in range(HEADS_PER_GROUP):
        h = g * HEADS_PER_GROUP + e
        parts.append(jnp.broadcast_to(jnp.exp(cs[SSD_CHUNK - 1:SSD_CHUNK, h:h + 1]), (HEAD_DIM, SSD_STATE)))
    return jnp.concatenate(parts, axis=0)


def _ssd_fwd(proj, act, dtb, alog, dsk, nw):
    s = proj.shape[0]
    nc = s // SSD_CHUNK
    bc_w = SSD_GROUPS * SSD_STATE
    exp_mat, _ = _ssd_constants()

    def body(z_ref, dtr_ref, xs_ref, b_ref, c_ref, dtb_ref, alog_ref, dskx_ref, nw_ref, exp_ref,
             ypre_ref, yssd_ref, hall_ref, h_scr):
        @pl.when(pl.program_id(0) == 0)
        def _():
            h_scr[...] = jnp.zeros_like(h_scr)

        dt, a, cs, cst, tril, lane, dtx, csx = _ssd_prep(dtr_ref[...], dtb_ref[...], alog_ref[...], exp_ref[...])
        cs_last_x = csx[SSD_CHUNK - 1:SSD_CHUNK, :]
        xs = xs_ref[...]
        xdt = xs * dtx
        xdtb = xdt.astype(BF16)
        xdec = (xdt * jnp.exp(cs_last_x - csx)).astype(BF16)
        ecsx = jnp.exp(csx)
        head_of_lane = lax.broadcasted_iota(jnp.int32, (1, GROUP_WIDTH), 1) // HEAD_DIM
        for g in range(SSD_GROUPS):
            gs = slice(g * GROUP_WIDTH, (g + 1) * GROUP_WIDTH)
            bg = b_ref[:, g * SSD_STATE:(g + 1) * SSD_STATE].astype(BF16)
            cg = c_ref[:, g * SSD_STATE:(g + 1) * SSD_STATE].astype(BF16)
            cb = _dot(cg, bg, NT)
            hprev = h_scr[gs, :]
            hall_ref[0, gs, :] = hprev
            gms, rhs = [], []
            xg = xdtb[:, gs]
            for e in range(HEADS_PER_GROUP):
                h = g * HEADS_PER_GROUP + e
                lm = jnp.exp(jnp.where(tril, cs[:, h:h + 1] - cst[h:h + 1, :], -1e30))
                gms.append((cb * lm).astype(BF16))
                rhs.append(jnp.where(head_of_lane == e, xg, jnp.zeros_like(xg)))
            y = _dot(jnp.concatenate(gms, axis=1), jnp.concatenate(rhs, axis=0), NN)
            y = y + ecsx[:, gs] * _dot(cg, hprev.astype(BF16), NT)
            y = y + dskx_ref[:, gs] * xs[:, gs]
            h_scr[gs, :] = hprev * _chunk_decay_rows(cs, g) + _dot(xdec[:, gs], bg, TN)
            ypre_ref[:, gs] = y
            z = z_ref[:, gs]
            yg = y * (z * _sigmoid(z))
            r = lax.rsqrt(jnp.mean(yg * yg, axis=-1, keepdims=True) + EPS)
            yssd_ref[:, gs] = (yg * r * nw_ref[:, gs]).astype(BF16)

    row_d = lambda cb: pl.BlockSpec((SSD_CHUNK, SSD_D_INNER), lambda c: (c, cb))
    small = pl.BlockSpec((1, LANE), lambda c: (0, 0))
    wide = pl.BlockSpec((1, SSD_D_INNER), lambda c: (0, 0))
    return pl.pallas_call(
        body, name="ssd_fwd", grid=(nc,),
        in_specs=[row_d(OFF_Z // SSD_D_INNER),
                  pl.BlockSpec((SSD_CHUNK, LANE), lambda c: (c, OFF_DT // LANE)),
                  row_d(0),
                  pl.BlockSpec((SSD_CHUNK, bc_w), lambda c: (c, SSD_D_INNER // bc_w)),
                  pl.BlockSpec((SSD_CHUNK, bc_w), lambda c: (c, SSD_D_INNER // bc_w + 1)),
                  small, small, wide, wide, pl.BlockSpec((LANE, SSD_D_INNER), lambda c: (0, 0))],
        out_specs=[row_d(0), row_d(0), pl.BlockSpec((1, SSD_D_INNER, SSD_STATE), lambda c: (c, 0, 0))],
        out_shape=[jax.ShapeDtypeStruct((s, SSD_D_INNER), F32), jax.ShapeDtypeStruct((s, SSD_D_INNER), BF16),
                   jax.ShapeDtypeStruct((nc, SSD_D_INNER, SSD_STATE), F32)],
        scratch_shapes=[pltpu.VMEM((SSD_D_INNER, SSD_STATE), F32)],
        compiler_params=_cparams(("arbitrary",)))(proj, proj, act, act, act, dtb, alog, _expand_heads(dsk), nw, exp_mat)


def _ssd_bwd(dycat, ypre, proj, act, hall, dtb, alog, dsk, nw, comm=None):
    s = proj.shape[0]
    nc = s // SSD_CHUNK
    bc_w = SSD_GROUPS * SSD_STATE

    exp_mat, ind4 = _ssd_constants()
    seg_passes = 1

    def body(dy_ref, ypre_ref, z_ref, dtr_ref, xs_ref, b_ref, c_ref, hall_ref, dtb_ref, alog_ref, dskx_ref, nw_ref,
             exp_ref, ind4_ref, dz_ref, dact_ref, ddtr_ref, da_ref, ddsk_ref, ddtb_ref, dnw_ref, dh_scr):
        @pl.when(pl.program_id(0) == 0)
        def _():
            dh_scr[...] = jnp.zeros_like(dh_scr)
            da_ref[...] = jnp.zeros_like(da_ref)
            ddsk_ref[...] = jnp.zeros_like(ddsk_ref)
            ddtb_ref[...] = jnp.zeros_like(ddtb_ref)
            dnw_ref[...] = jnp.zeros_like(dnw_ref)

        dtr = dtr_ref[...]
        dt, a, cs, cst, tril, lane, dtx, csx = _ssd_prep(dtr, dtb_ref[...], alog_ref[...], exp_ref[...])
        cs_last_x = csx[SSD_CHUNK - 1:SSD_CHUNK, :]
        xs = xs_ref[...]
        xdt = xs * dtx
        xdtb = xdt.astype(BF16)
        decx = jnp.exp(cs_last_x - csx)
        xdecf = xdt * decx
        xdec = xdecf.astype(BF16)
        ecsx = jnp.exp(csx)
        head_of_lane = lax.broadcasted_iota(jnp.int32, (1, GROUP_WIDTH), 1) // HEAD_DIM
        last_row = lax.broadcasted_iota(jnp.int32, (SSD_CHUNK, 1), 0) == SSD_CHUNK - 1
        dcs_col = jnp.zeros((SSD_CHUNK, LANE), F32)
        dcs_row = jnp.zeros((SSD_CHUNK, LANE), F32)
        ddt = jnp.zeros((SSD_CHUNK, LANE), F32)
        ddsk = jnp.zeros((1, LANE), F32)
        hsum = jnp.zeros((1, LANE), F32)
        t1_sum = jnp.zeros((1, LANE), F32)
        for g in range(SSD_GROUPS):
            gs = slice(g * GROUP_WIDTH, (g + 1) * GROUP_WIDTH)
            bsl = slice(g * SSD_STATE, (g + 1) * SSD_STATE)
            exp_g = exp_ref[:, gs]
            ind4_g = ind4_ref[g * HEADS_PER_GROUP * SSD_CHUNK:(g + 1) * HEADS_PER_GROUP * SSD_CHUNK, :]
            z = z_ref[:, gs]
            sg = _sigmoid(z)
            sz = z * sg
            ypre = ypre_ref[:, gs]
            yg = ypre * sz
            r = lax.rsqrt(jnp.mean(yg * yg, axis=-1, keepdims=True) + EPS)
            nrm = yg * r
            dyo_n = dy_ref[:, gs]
            dnw_ref[:, gs] += jnp.sum(dyo_n * nrm, axis=0, keepdims=True)
            dn = dyo_n * nw_ref[:, gs]
            dyg = r * (dn - nrm * jnp.mean(dn * nrm, axis=-1, keepdims=True))
            dz_ref[:, gs] = (dyg * ypre * (sg * (1.0 + z * (1.0 - sg)))).astype(BF16)
            dy = dyg * sz

            bg = b_ref[:, bsl].astype(BF16)
            cg = c_ref[:, bsl].astype(BF16)
            cb = _dot(cg, bg, NT)
            hprev = hall_ref[0, gs, :]
            hb = hprev.astype(BF16)
            dhn = dh_scr[gs, :]
            dhb = dhn.astype(BF16)
            xs_g, xdt_g = xs[:, gs], xdtb[:, gs]
            w_off = _dot(cg, hb, NT)
            dyo = dy * ecsx[:, gs]
            dyob = dyo.astype(BF16)
            dcg = _dot(dyob, hb, NN)
            dh_y = _dot(dyob, cg, TN)
            r_st = _dot(bg, dhb, NT)
            dbg = _dot(xdec[:, gs], dhb, NN)
            dyb = dy.astype(BF16)
            gms, gmbs, lms, dys = [], [], [], []
            for e in range(HEADS_PER_GROUP):
                h = g * HEADS_PER_GROUP + e
                lm = jnp.exp(jnp.where(tril, cs[:, h:h + 1] - cst[h:h + 1, :], -1e30))
                gm = cb * lm
                lms.append(lm)
                gms.append(gm)
                gmbs.append(gm.astype(BF16))
                dys.append(jnp.where(head_of_lane == e, dyb, jnp.zeros_like(dyb)))
            dxdt = _dot(jnp.concatenate(gmbs, axis=0), jnp.concatenate(dys, axis=0), TN) + decx[:, gs] * r_st
            dcb = jnp.zeros((SSD_CHUNK, SSD_CHUNK), F32)
            mms = []
            for e in range(HEADS_PER_GROUP):
                dg = _dot(dys[e], xdt_g, NT)
                mms.append(dg * gms[e])
                dcb = dcb + dg * lms[e]
            seg = _dot_split(jnp.concatenate([dyo * w_off, xdecf[:, gs] * r_st, dxdt * xs_g, dy * xs_g], axis=0), exp_g, NT, seg_passes)
            v1, t1, ddt_g, dsk_g = [seg[i * SSD_CHUNK:(i + 1) * SSD_CHUNK] for i in range(4)]
            dcs_col = dcs_col + v1 - t1 + _dot_split(jnp.concatenate(mms, axis=1), ind4_g, NN, seg_passes)
            for t in _split_bf16(jnp.concatenate(mms, axis=0), seg_passes):
                dcs_row = dcs_row + _dot(ind4_g, t, TN)
            ddt = ddt + ddt_g
            ddsk = ddsk + jnp.sum(dsk_g, axis=0, keepdims=True)
            t1_sum = t1_sum + jnp.sum(t1, axis=0, keepdims=True)
            for e in range(HEADS_PER_GROUP):
                h = g * HEADS_PER_GROUP + e
                hs = slice(e * HEAD_DIM, (e + 1) * HEAD_DIM)
                hsum = hsum + jnp.where(lane == h, jnp.sum(dhn[hs, :] * hprev[hs, :]).reshape(1, 1), 0.0)
            dh_scr[gs, :] = dhn * _chunk_decay_rows(cs, g) + dh_y
            dcbb = dcb.astype(BF16)
            dact_ref[:, gs] = dxdt * dtx[:, gs] + dskx_ref[:, gs] * dy
            dact_ref[:, SSD_D_INNER + g * SSD_STATE:SSD_D_INNER + (g + 1) * SSD_STATE] = dbg + _dot(dcbb, cg, TN)
            dact_ref[:, SSD_D_INNER + bc_w + g * SSD_STATE:SSD_D_INNER + bc_w + (g + 1) * SSD_STATE] = dcg + _dot(dcbb, bg, NN)
        dlast = t1_sum + jnp.exp(cs[SSD_CHUNK - 1:SSD_CHUNK, :]) * hsum
        dcs = dcs_col - dcs_row.T + jnp.where(last_row, dlast, 0.0)
        row = lax.broadcasted_iota(jnp.int32, (SSD_CHUNK, SSD_CHUNK), 0)
        col = lax.broadcasted_iota(jnp.int32, (SSD_CHUNK, SSD_CHUNK), 1)
        dda = _dot((col >= row).astype(F32), dcs, NN, precision=HIGHEST)
        ddt = ddt + dda * a
        da_ref[...] += jnp.sum(dda * dt, axis=0, keepdims=True)
        ddtr = jnp.where(lane < SSD_HEADS, ddt * _sigmoid(dtr + dtb_ref[...]), 0.0)
        ddtr_ref[...] = ddtr.astype(BF16)
        ddtb_ref[...] += jnp.sum(ddtr, axis=0, keepdims=True)
        ddsk_ref[...] += ddsk

    rev = lambda c: nc - 1 - c
    row_d = lambda cb: pl.BlockSpec((SSD_CHUNK, SSD_D_INNER), lambda c: (rev(c), cb))
    small = pl.BlockSpec((1, LANE), lambda c: (0, 0))
    wide = pl.BlockSpec((1, SSD_D_INNER), lambda c: (0, 0))
    small_shape = jax.ShapeDtypeStruct((1, LANE), F32)
    return _pcall(
        body, (dycat, ypre, proj, proj, act, act, act, hall, dtb, alog, _expand_heads(dsk), nw, exp_mat, ind4),
        name="ssd_bwd", grid=(nc,),
        in_specs=[row_d(0), row_d(0), row_d(OFF_Z // SSD_D_INNER),
                  pl.BlockSpec((SSD_CHUNK, LANE), lambda c: (rev(c), OFF_DT // LANE)),
                  row_d(0),
                  pl.BlockSpec((SSD_CHUNK, bc_w), lambda c: (rev(c), SSD_D_INNER // bc_w)),
                  pl.BlockSpec((SSD_CHUNK, bc_w), lambda c: (rev(c), SSD_D_INNER // bc_w + 1)),
                  pl.BlockSpec((1, SSD_D_INNER, SSD_STATE), lambda c: (rev(c), 0, 0)),
                  small, small, wide, wide, pl.BlockSpec((LANE, SSD_D_INNER), lambda c: (0, 0)),
                  pl.BlockSpec((SSD_HEADS * SSD_CHUNK, LANE), lambda c: (0, 0))],
        out_specs=[row_d(0), pl.BlockSpec((SSD_CHUNK, CONV_CH), lambda c: (rev(c), 0)),
                   pl.BlockSpec((SSD_CHUNK, LANE), lambda c: (rev(c), 0)), small, small, small, wide],
        out_shape=[jax.ShapeDtypeStruct((s, SSD_D_INNER), BF16), jax.ShapeDtypeStruct((s, CONV_CH), F32),
                   jax.ShapeDtypeStruct((s, LANE), BF16), small_shape, small_shape, small_shape,
                   jax.ShapeDtypeStruct((1, SSD_D_INNER), F32)],
        scratch_shapes=[pltpu.VMEM((SSD_D_INNER, SSD_STATE), F32)], sem=("arbitrary",), comm=comm)


def _head_mean_matrix():
    row = lax.broadcasted_iota(jnp.int32, (LANE, LANE), 0) // HEAD_DIM
    col = lax.broadcasted_iota(jnp.int32, (LANE, LANE), 1) // HEAD_DIM
    return (row == col).astype(F32)


def _head_sum2(v, ones_bd):
    hi = v.astype(BF16)
    lo = (v - hi.astype(F32)).astype(BF16)
    return _dot(hi, ones_bd, NN) + _dot(lo, ones_bd, NN)


def _head_norm(x, w, scale, ones_bd):
    ms = _head_sum2(x * x, ones_bd) * (1.0 / HEAD_DIM)
    return (x * lax.rsqrt(ms + EPS)) * (w * scale)


PRO_ROWS = 256
ATT_GROUP_FWD = 16
ATT_GROUP_BWD = 8
KEYS = 2 * ATT_BLK
NEG = -1e30
HALF = HEAD_DIM // 2


def _rows(start, size, dil):
    return pl.ds(start, size) if dil == 1 else pl.ds(start, size, stride=dil)


def _fill_bias(bias_ref):
    row = lax.broadcasted_iota(jnp.int32, (ATT_BLK, 2 * KEYS), 0)
    col = lax.broadcasted_iota(jnp.int32, (ATT_BLK, 2 * KEYS), 1) & (KEYS - 1)
    for first, off in ((0, 0), (1, ATT_BLK)):
        dist = off + row - col
        bias_ref[first] = jnp.where((dist >= 0) & (dist <= ATT_BLK), 0.0, NEG)


def _pair(a, b):
    return jnp.concatenate([jnp.broadcast_to(a, (ATT_BLK, KEYS)), jnp.broadcast_to(b, (ATT_BLK, KEYS))], axis=1)


def _split_heads(x, is_a):
    zero = jnp.zeros_like(x)
    return jnp.concatenate([jnp.where(is_a, x, zero), jnp.where(is_a, zero, x)], axis=0)


def _block_ids(b, nb):
    i = b & (nb - 1)
    q0 = pl.multiple_of(b * ATT_BLK, ATT_BLK)
    k0 = pl.multiple_of((b - jnp.minimum(i, 1)) * ATT_BLK, ATT_BLK)
    return pl.ds(q0, ATT_BLK), pl.ds(k0, KEYS), jnp.minimum(i, 1)


def _att_fwd(proj, qw, kw, comm=None):
    s = proj.shape[0]
    nblk = s // ATT_BLK
    assert all((s // d) // ATT_BLK >= 2 for d in DILATIONS)
    blk = lambda off: pl.BlockSpec((s, LANE), lambda i: (0, off // LANE + i))
    wspec = pl.BlockSpec((1, LANE), lambda i: (0, i))
    oblk = pl.BlockSpec((s, LANE), lambda i: (0, i))

    def body(q_ref, k_ref, v_ref, qw_ref, kw_ref, o_ref, lse_ref, qn, kn, q_cm, k_cm, v_cm, m_acc, l_acc, o_d, m_d, l_d, bias):
        ones_bd = _head_mean_matrix().astype(BF16)
        is_a = lax.broadcasted_iota(jnp.int32, (1, LANE), 1) < HEAD_DIM
        ones_ext = _split_heads(jnp.ones((KEYS, LANE), BF16), is_a)
        _fill_bias(bias)

        def pro(j, c):
            rows = pl.ds(pl.multiple_of(j * PRO_ROWS, PRO_ROWS), PRO_ROWS)
            qn[rows, :] = _head_norm(q_ref[rows, :], qw_ref[...], HEAD_DIM ** -0.5, ones_bd)
            kn[rows, :] = _head_norm(k_ref[rows, :], kw_ref[...], 1.0, ones_bd)
            return c

        lax.fori_loop(0, s // PRO_ROWS, pro, 0)

        for dil in DILATIONS:
            ln = s // dil
            nb = ln // ATT_BLK
            o_out, m_out, l_out = (o_ref, m_acc, l_acc) if dil == 1 else (o_d, m_d, l_d)
            for r in range(dil):
                def relayout(j, c, dil=dil, r=r, ln=ln):
                    j0 = pl.multiple_of(j * PRO_ROWS, PRO_ROWS)
                    src = _rows(r + dil * j0, PRO_ROWS, dil)
                    dst = pl.ds(r * ln + j0, PRO_ROWS)
                    q_cm[dst, :] = qn[src, :].astype(BF16)
                    k_cm[dst, :] = kn[src, :].astype(BF16)
                    v_cm[dst, :] = v_ref[src, :].astype(BF16)
                    return c

                lax.fori_loop(0, ln // PRO_ROWS, relayout, 0)

            def step(bg, c, nb=nb, o_out=o_out, m_out=m_out, l_out=l_out):
                ids = [_block_ids(bg * ATT_GROUP_FWD + u, nb) for u in range(ATT_GROUP_FWD)]
                kbs = [_split_heads(k_cm[krows, :], is_a) for _, krows, _ in ids]
                scs = [_dot(q_cm[qrows, :], kb, NT) + bias[first] for (qrows, _, first), kb in zip(ids, kbs)]
                mas = [jnp.max(sc[:, :KEYS], axis=-1, keepdims=True) for sc in scs]
                mbs = [jnp.max(sc[:, KEYS:], axis=-1, keepdims=True) for sc in scs]
                ps = [jnp.exp(sc - _pair(ma, mb)).astype(BF16) for sc, ma, mb in zip(scs, mas, mbs)]
                vbs = [jnp.concatenate([_split_heads(v_cm[krows, :], is_a), ones_ext], axis=1) for _, krows, _ in ids]
                ols = [_dot(p, vb, NN) for p, vb in zip(ps, vbs)]
                for (qrows, _, _), ol, ma, mb in zip(ids, ols, mas, mbs):
                    o_out[qrows, :] = ol[:, :LANE]
                    l_out[qrows, :] = ol[:, LANE:]
                    m_out[qrows, :] = jnp.where(is_a, ma, mb)
                return c

            lax.fori_loop(0, nblk // ATT_GROUP_FWD, step, 0)

            if dil > 1:
                for r in range(dil):
                    def merge(j, c, dil=dil, r=r, ln=ln):
                        j0 = pl.multiple_of(j * PRO_ROWS, PRO_ROWS)
                        nat = _rows(r + dil * j0, PRO_ROWS, dil)
                        cm = pl.ds(r * ln + j0, PRO_ROWS)
                        m_old, m_new = m_acc[nat, :], m_d[cm, :]
                        m = jnp.maximum(m_old, m_new)
                        a_old, a_new = jnp.exp(m_old - m), jnp.exp(m_new - m)
                        o_ref[nat, :] = a_old * o_ref[nat, :] + a_new * o_d[cm, :]
                        l_acc[nat, :] = a_old * l_acc[nat, :] + a_new * l_d[cm, :]
                        m_acc[nat, :] = m
                        return c

                    lax.fori_loop(0, ln // PRO_ROWS, merge, 0)

        def epi(j, c):
            rows = pl.ds(pl.multiple_of(j * PRO_ROWS, PRO_ROWS), PRO_ROWS)
            l = l_acc[rows, :]
            o_ref[rows, :] = o_ref[rows, :] / l
            lse_ref[rows, :] = m_acc[rows, :] + jnp.log(l)
            return c

        lax.fori_loop(0, s // PRO_ROWS, epi, 0)

    f = jax.ShapeDtypeStruct((s, ATT_D), F32)
    scr = pltpu.VMEM((s, LANE), F32)
    scb = pltpu.VMEM((s, LANE), BF16)
    return _pcall(
        body, (proj, proj, proj, qw, kw), name="att_fwd", grid=(ATT_D // LANE,),
        in_specs=[blk(OFF_Q), blk(OFF_K), blk(OFF_V), wspec, wspec], out_specs=[oblk, oblk], out_shape=[f, f],
        scratch_shapes=[scr, scr, scb, scb, scb, scr, scr, scr, scr, scr, pltpu.VMEM((2, ATT_BLK, 2 * KEYS), F32)],
        sem=("parallel",), comm=comm)


def _att_bwd(proj, do, stats, qw, kw, comm=None):
    s = proj.shape[0]
    nblk = s // ATT_BLK
    blk = lambda off: pl.BlockSpec((s, LANE), lambda i: (0, off // LANE + i))
    wspec = pl.BlockSpec((1, LANE), lambda i: (0, i))
    oblk = pl.BlockSpec((s, LANE), lambda i: (0, i))

    def body(q_ref, k_ref, v_ref, do_ref, st_ref, qw_ref, kw_ref, dq_ref, dk_ref, dv_ref, dqw_ref, dkw_ref,
             qn, kn, q_cm, do_cm, k_cm, v_cm, st_cm, dq_acc, dk_acc, dv_acc, dq_d, dk_d, dv_d, bias):
        ones_bd = _head_mean_matrix().astype(BF16)
        is_a = lax.broadcasted_iota(jnp.int32, (1, LANE), 1) < HEAD_DIM
        _fill_bias(bias)
        zero = jnp.zeros((PRO_ROWS, LANE), F32)

        def pro(j, c):
            rows = pl.ds(pl.multiple_of(j * PRO_ROWS, PRO_ROWS), PRO_ROWS)
            qn[rows, :] = _head_norm(q_ref[rows, :], qw_ref[...], HEAD_DIM ** -0.5, ones_bd)
            kn[rows, :] = _head_norm(k_ref[rows, :], kw_ref[...], 1.0, ones_bd)
            dk_acc[rows, :] = zero
            dv_acc[rows, :] = zero
            return c

        lax.fori_loop(0, s // PRO_ROWS, pro, 0)

        for dil in DILATIONS:
            ln = s // dil
            nb = ln // ATT_BLK
            dq_o, dk_o, dv_o = (dq_acc, dk_acc, dv_acc) if dil == 1 else (dq_d, dk_d, dv_d)
            for r in range(dil):
                def relayout(j, c, dil=dil, r=r, ln=ln):
                    j0 = pl.multiple_of(j * PRO_ROWS, PRO_ROWS)
                    src = _rows(r + dil * j0, PRO_ROWS, dil)
                    dst = pl.ds(r * ln + j0, PRO_ROWS)
                    q_cm[dst, :] = qn[src, :].astype(BF16)
                    k_cm[dst, :] = kn[src, :].astype(BF16)
                    v_cm[dst, :] = v_ref[src, :].astype(BF16)
                    do_cm[dst, :] = do_ref[src, :].astype(BF16)
                    st_cm[dst, :] = st_ref[src, :]
                    if dil > 1:
                        dk_d[dst, :] = zero
                        dv_d[dst, :] = zero
                    return c

                lax.fori_loop(0, ln // PRO_ROWS, relayout, 0)

            def step(bg, c, nb=nb, dq_o=dq_o, dk_o=dk_o, dv_o=dv_o):
                ids = [_block_ids(bg * ATT_GROUP_BWD + u, nb) for u in range(ATT_GROUP_BWD)]
                qbs = [q_cm[qrows, :] for qrows, _, _ in ids]
                dobs = [do_cm[qrows, :] for qrows, _, _ in ids]
                kbs = [_split_heads(k_cm[krows, :], is_a) for _, krows, _ in ids]
                vbs = [_split_heads(v_cm[krows, :], is_a) for _, krows, _ in ids]
                sts = [st_cm[qrows, :] for qrows, _, _ in ids]
                scs = [_dot(qb, kb, NT) + bias[first] for qb, kb, (_, _, first) in zip(qbs, kbs, ids)]
                dps = [_dot(dob, vb, NT) for dob, vb in zip(dobs, vbs)]
                ps = [jnp.exp(sc - _pair(st[:, 0:1], st[:, HEAD_DIM:HEAD_DIM + 1])) for sc, st in zip(scs, sts)]
                dss = [(p * (dp - _pair(st[:, HALF:HALF + 1], st[:, HEAD_DIM + HALF:HEAD_DIM + HALF + 1]))).astype(BF16)
                       for p, dp, st in zip(ps, dps, sts)]
                dqs = [_dot(ds, kb, NN) for ds, kb in zip(dss, kbs)]
                dkfs = [_dot(ds, qb, TN) for ds, qb in zip(dss, qbs)]
                dvfs = [_dot(p.astype(BF16), dob, TN) for p, dob in zip(ps, dobs)]
                for (qrows, krows, _), dq, dkf, dvf in zip(ids, dqs, dkfs, dvfs):
                    dq_o[qrows, :] = dq
                    dk_o[krows, :] += jnp.where(is_a, dkf[:KEYS], dkf[KEYS:])
                    dv_o[krows, :] += jnp.where(is_a, dvf[:KEYS], dvf[KEYS:])
                return c

            lax.fori_loop(0, nblk // ATT_GROUP_BWD, step, 0)

            if dil > 1:
                for r in range(dil):
                    def merge(j, c, dil=dil, r=r, ln=ln):
                        j0 = pl.multiple_of(j * PRO_ROWS, PRO_ROWS)
                        nat = _rows(r + dil * j0, PRO_ROWS, dil)
                        cm = pl.ds(r * ln + j0, PRO_ROWS)
                        dq_acc[nat, :] += dq_d[cm, :]
                        dk_acc[nat, :] += dk_d[cm, :]
                        dv_acc[nat, :] += dv_d[cm, :]
                        return c

                    lax.fori_loop(0, ln // PRO_ROWS, merge, 0)

        def back(dn_out, x, w, scale):
            r = lax.rsqrt(_head_sum2(x * x, ones_bd) * (1.0 / HEAD_DIM) + EPS)
            nrm = x * r
            dw = jnp.sum(dn_out * nrm, axis=0, keepdims=True) * scale
            dn = dn_out * (w * scale)
            return r * (dn - nrm * (_head_sum2(dn * nrm, ones_bd) * (1.0 / HEAD_DIM))), dw

        def epi(j, c):
            rows = pl.ds(pl.multiple_of(j * PRO_ROWS, PRO_ROWS), PRO_ROWS)
            dq, dqw = back(dq_acc[rows, :], q_ref[rows, :], qw_ref[...], HEAD_DIM ** -0.5)
            dk, dkw = back(dk_acc[rows, :], k_ref[rows, :], kw_ref[...], 1.0)
            dq_ref[rows, :] = dq.astype(BF16)
            dk_ref[rows, :] = dk.astype(BF16)
            dv_ref[rows, :] = dv_acc[rows, :].astype(BF16)
            return (c[0] + dqw, c[1] + dkw)

        zrow = jnp.zeros((1, LANE), F32)
        dqw, dkw = lax.fori_loop(0, s // PRO_ROWS, epi, (zrow, zrow))
        dqw_ref[...] = dqw
        dkw_ref[...] = dkw

    o = jax.ShapeDtypeStruct((s, ATT_D), BF16)
    ov = jax.ShapeDtypeStruct((1, ATT_D), F32)
    scr = pltpu.VMEM((s, LANE), F32)
    scb = pltpu.VMEM((s, LANE), BF16)
    return _pcall(
        body, (proj, proj, proj, do, stats, qw, kw), name="att_bwd", grid=(ATT_D // LANE,),
        in_specs=[blk(OFF_Q), blk(OFF_K), blk(OFF_V), oblk, oblk, wspec, wspec],
        out_specs=[oblk, oblk, oblk, wspec, wspec], out_shape=[o, o, o, ov, ov],
        scratch_shapes=[scr, scr, scb, scb, scb, scb, scr, scr, scr, scr, scr, scr, scr, pltpu.VMEM((2, ATT_BLK, 2 * KEYS), F32)],
        sem=("parallel",), comm=comm)


def _att_norm_fwd(o, nw):
    s = o.shape[0]
    row = pl.BlockSpec((ROW_TILE, ATT_D), lambda i: (i, 0))
    vec = pl.BlockSpec((1, ATT_D), lambda i: (0, 0))

    def body(o_ref, nw_ref, y_ref):
        o = o_ref[...]
        r = lax.rsqrt(jnp.mean(o * o, axis=-1, keepdims=True) + EPS)
        y_ref[...] = (o * r * nw_ref[...]).astype(BF16)

    return pl.pallas_call(body, name="att_norm_fwd", grid=(s // ROW_TILE,), in_specs=[row, vec], out_specs=row,
                          out_shape=jax.ShapeDtypeStruct((s, ATT_D), BF16), compiler_params=_cparams(("parallel",)))(o, nw)


def _att_norm_bwd(dycat, o, lse, nw):
    s = o.shape[0]
    row = pl.BlockSpec((ROW_TILE, ATT_D), lambda i: (i, 0))
    vec = pl.BlockSpec((1, ATT_D), lambda i: (0, 0))

    def body(dy_ref, o_ref, lse_ref, nw_ref, do_ref, st_ref, dnw_ref):
        @pl.when(pl.program_id(0) == 0)
        def _():
            dnw_ref[...] = jnp.zeros_like(dnw_ref)

        o = o_ref[...]
        dy = dy_ref[...]
        r = lax.rsqrt(jnp.mean(o * o, axis=-1, keepdims=True) + EPS)
        nrm = o * r
        dnw_ref[...] += jnp.sum(dy * nrm, axis=0, keepdims=True)
        dn = dy * nw_ref[...]
        do = r * (dn - nrm * jnp.mean(dn * nrm, axis=-1, keepdims=True))
        do_ref[...] = do
        ones_bd = _head_mean_matrix().astype(BF16)
        prod = do * o
        delta = jnp.concatenate([_head_sum2(prod[:, j * LANE:(j + 1) * LANE], ones_bd) for j in range(ATT_D // LANE)], axis=1)
        lane = lax.broadcasted_iota(jnp.int32, (1, ATT_D), 1)
        st_ref[...] = jnp.where((lane & (HEAD_DIM - 1)) < HALF, lse_ref[...], delta)

    f = jax.ShapeDtypeStruct((s, ATT_D), F32)
    return pl.pallas_call(
        body, name="att_norm_bwd", grid=(s // ROW_TILE,),
        in_specs=[pl.BlockSpec((ROW_TILE, ATT_D), lambda i: (i, 1)), row, row, vec], out_specs=[row, row, vec],
        out_shape=[f, f, jax.ShapeDtypeStruct((1, ATT_D), F32)],
        compiler_params=_cparams(("arbitrary",)))(dycat, o, lse, nw)


def _ada_fwd(c_all, w_ada):
    def body(c_ref, w_ref, o_ref):
        cv = c_ref[...]
        o_ref[...] = _dot((cv * _sigmoid(cv)).astype(BF16), w_ref[...].astype(BF16), NN)

    return pl.pallas_call(body, name="ada_fwd", out_shape=jax.ShapeDtypeStruct((c_all.shape[0], w_ada.shape[1]), F32),
                          compiler_params=_cparams())(c_all, w_ada)


def _adamw_math(g, w, m, v):
    m_new = ADAM_B1 * m + (1.0 - ADAM_B1) * g
    v_new = ADAM_B2 * v + (1.0 - ADAM_B2) * (g * g)
    m_hat = m_new / (1.0 - ADAM_B1 ** ADAM_STEP)
    v_hat = v_new / (1.0 - ADAM_B2 ** ADAM_STEP)
    delta = -ADAM_LR * (m_hat / (jnp.sqrt(v_hat) + ADAM_EPS) + ADAM_WD * w)
    return delta, m_new, v_new


def _ada_bwd_adamw(c_all, dmod_cols, w, m, v):
    rows, cols = w.shape
    tr = 256
    blk = pl.BlockSpec((tr, cols), lambda i: (i, 0))

    def body(c_ref, d_ref, w_ref, m_ref, v_ref, g_ref, dl_ref, mo_ref, vo_ref):
        cv = c_ref[...]
        ca = cv * _sigmoid(cv)
        g = ca[:, 0:1] * d_ref[0:1, :]
        for b in range(1, N_DEV):
            g = g + ca[:, b:b + 1] * d_ref[b:b + 1, :]
        g_ref[...] = g
        dl_ref[...], mo_ref[...], vo_ref[...] = _adamw_math(g, w_ref[...], m_ref[...], v_ref[...])

    o = jax.ShapeDtypeStruct((rows, cols), F32)
    return pl.pallas_call(
        body, name="ada_bwd_adamw", grid=(rows // tr,),
        in_specs=[pl.BlockSpec((tr, N_DEV), lambda i: (i, 0)), pl.BlockSpec((N_DEV, cols), lambda i: (0, 0)), blk, blk, blk],
        out_specs=[blk] * 4, out_shape=[o, o, o, o], compiler_params=_cparams(("parallel",)))(c_all.T, dmod_cols, w, m, v)


def _reduce_adamw(slabs, w, m, v, name):
    rows, cols = w.shape
    n_src = slabs.shape[0]
    if rows % 128 == 0:
        tr, steps = 128, rows // 128
        blk = pl.BlockSpec((tr, cols), lambda i: (i, 0))
        sblk = pl.BlockSpec((n_src, tr, cols), lambda i: (0, i, 0))
    else:
        tc, steps = 256, cols // 256
        blk = pl.BlockSpec((rows, tc), lambda i: (0, i))
        sblk = pl.BlockSpec((n_src, rows, tc), lambda i: (0, 0, i))

    def body(s_ref, w_ref, m_ref, v_ref, g_ref, dl_ref, mo_ref, vo_ref):
        g = s_ref[0].astype(F32)
        for src in range(1, n_src):
            g = g + s_ref[src].astype(F32)
        g_ref[...] = g
        dl_ref[...], mo_ref[...], vo_ref[...] = _adamw_math(g, w_ref[...], m_ref[...], v_ref[...])

    o = jax.ShapeDtypeStruct((rows, cols), F32)
    return pl.pallas_call(
        body, name=name, grid=(steps,), in_specs=[sblk, blk, blk, blk],
        out_specs=[blk] * 4, out_shape=[o, o, o, o], compiler_params=_cparams(("parallel",)))(slabs, w, m, v)


def _small_reduce_adamw(gathered, w, m, v):
    def body(s_ref, w_ref, m_ref, v_ref, g_ref, dl_ref, mo_ref, vo_ref):
        g = s_ref[0]
        for dev in range(1, N_DEV):
            g = g + s_ref[dev]
        g_ref[...] = g
        dl_ref[...], mo_ref[...], vo_ref[...] = _adamw_math(g, w_ref[...], m_ref[...], v_ref[...])

    o = jax.ShapeDtypeStruct(w.shape, F32)
    return pl.pallas_call(body, name="small_reduce_adamw", out_shape=[o, o, o, o], compiler_params=_cparams())(gathered, w, m, v)


def _adamw_small(g, w, m, v, name):
    def body(g_ref, w_ref, m_ref, v_ref, dl_ref, mo_ref, vo_ref):
        dl_ref[...], mo_ref[...], vo_ref[...] = _adamw_math(g_ref[...], w_ref[...], m_ref[...], v_ref[...])

    o = jax.ShapeDtypeStruct(w.shape, F32)
    return pl.pallas_call(body, name=name, out_shape=[o, o, o], compiler_params=_cparams())(g, w, m, v)


class _Exchange:
    def __init__(self, arrs, scatter):
        self.arrs, self.scatter, self.n = list(arrs), scatter, len(arrs)
        hbm = pl.BlockSpec(memory_space=pltpu.HBM)
        self.in_specs = [hbm] * self.n
        self.out_specs = [hbm] * self.n
        self.out_shape = [jax.ShapeDtypeStruct(a.shape if scatter else (N_DEV,) + a.shape, a.dtype) for a in self.arrs]
        self.scratch = [pltpu.SemaphoreType.DMA((self.n * (N_DEV - 1),)), pltpu.SemaphoreType.DMA((self.n * (N_DEV - 1),)),
                        pltpu.SemaphoreType.DMA((self.n,))]

    def _local(self, ins, outs, sems):
        me = 4 * lax.axis_index("x") + 2 * lax.axis_index("y") + lax.axis_index("c")
        return [pltpu.make_async_copy(ins[a].at[me] if self.scatter else ins[a], outs[a].at[me], sems[2].at[a])
                for a in range(self.n)]

    def _remote(self, ins, outs, sems, arriving):
        send_sems, recv_sems, _ = sems
        x, y, c = lax.axis_index("x"), lax.axis_index("y"), lax.axis_index("c")
        me = 4 * x + 2 * y + c
        remote = []
        for a in range(self.n):
            for k in range(1, N_DEV):
                px = 1 - x if k & 4 else x
                py = 1 - y if k & 2 else y
                pc = 1 - c if k & 1 else c
                peer = 4 * px + 2 * py + pc
                sem = a * (N_DEV - 1) + k - 1
                remote.append(pltpu.make_async_remote_copy(
                    src_ref=ins[a].at[peer] if self.scatter else ins[a], dst_ref=outs[a].at[peer if arriving else me],
                    send_sem=send_sems.at[sem], recv_sem=recv_sems.at[sem], device_id=(px, py, pc), device_id_type=MESH_IDS))
        return remote

    def start(self, ins, outs, sems):
        for cp in self._local(ins, outs, sems) + self._remote(ins, outs, sems, arriving=False):
            cp.start()

    def forward(self, ins, outs, sems):
        pass

    def wait(self, ins, outs, sems):
        for send, arrival in zip(self._remote(ins, outs, sems, arriving=False), self._remote(ins, outs, sems, arriving=True)):
            send.wait_send()
            arrival.wait_recv()
        for cp in self._local(ins, outs, sems):
            cp.wait()


N_CHIP = N_DEV // 2


class _SiblingSwap(_Exchange):
    def __init__(self, arrs):
        super().__init__(arrs, scatter=True)
        self.out_shape = [jax.ShapeDtypeStruct((N_CHIP,) + a.shape[2:], a.dtype) for a in self.arrs]
        self.scratch = [pltpu.SemaphoreType.DMA((self.n,)), pltpu.SemaphoreType.DMA((self.n,)), pltpu.SemaphoreType.DMA((1,))]

    def _copies(self, ins, outs, sems):
        x, y, c = lax.axis_index("x"), lax.axis_index("y"), lax.axis_index("c")
        return [pltpu.make_async_remote_copy(src_ref=ins[a].at[:, 1 - c], dst_ref=outs[a], send_sem=sems[0].at[a], recv_sem=sems[1].at[a],
                                             device_id=(x, y, 1 - c), device_id_type=MESH_IDS) for a in range(self.n)]

    def start(self, ins, outs, sems):
        for cp in self._copies(ins, outs, sems):
            cp.start()

    def wait(self, ins, outs, sems):
        for cp in self._copies(ins, outs, sems):
            cp.wait()


class _ChipScatter(_Exchange):
    def __init__(self, arrs):
        super().__init__(arrs, scatter=True)
        n_pairs = self.n * (N_CHIP - 1)
        self.scratch = [pltpu.SemaphoreType.DMA((n_pairs,)), pltpu.SemaphoreType.DMA((n_pairs,)), pltpu.SemaphoreType.DMA((self.n,))]

    def _local(self, ins, outs, sems):
        chip = 2 * lax.axis_index("x") + lax.axis_index("y")
        return [pltpu.make_async_copy(ins[a].at[chip], outs[a].at[chip], sems[2].at[a]) for a in range(self.n)]

    def _remote(self, ins, outs, sems, arriving):
        send_sems, recv_sems, _ = sems
        x, y, c = lax.axis_index("x"), lax.axis_index("y"), lax.axis_index("c")
        chip = 2 * x + y
        remote = []
        for a in range(self.n):
            for k in range(1, N_CHIP):
                px = 1 - x if k & 2 else x
                py = 1 - y if k & 1 else y
                peer = 2 * px + py
                sem = a * (N_CHIP - 1) + k - 1
                remote.append(pltpu.make_async_remote_copy(
                    src_ref=ins[a].at[peer], dst_ref=outs[a].at[peer if arriving else chip], send_sem=send_sems.at[sem],
                    recv_sem=recv_sems.at[sem], device_id=(px, py, c), device_id_type=MESH_IDS))
        return remote


def _chip_sum(mine, theirs):
    n, rows, cols = mine.shape
    blk = pl.BlockSpec((1, rows, 256), lambda q, j: (q, 0, j))

    def body(a_ref, b_ref, o_ref):
        o_ref[...] = (a_ref[...].astype(F32) + b_ref[...].astype(F32)).astype(BF16)

    return pl.pallas_call(body, name="chip_sum", grid=(n, cols // 256), in_specs=[blk, blk], out_specs=blk,
                          out_shape=jax.ShapeDtypeStruct(mine.shape, BF16),
                          compiler_params=_cparams(("parallel", "parallel")))(mine, theirs)


class _Gather2(_Exchange):
    def __init__(self, arrs):
        super().__init__(arrs, scatter=False)

    def _copies(self, ins, outs, sems):
        send_sems, recv_sems, _ = sems
        x, y, c = lax.axis_index("x"), lax.axis_index("y"), lax.axis_index("c")
        sibling = (x, y, 1 - c)
        chips = [(1 - x, y), (x, 1 - y), (1 - x, 1 - y)]
        first, passed, landed = [], [], []
        for a in range(self.n):
            def copy(k, block, to, src=None, a=a):
                slab = outs[a].at[4 * block[0] + 2 * block[1] + block[2]]
                return pltpu.make_async_remote_copy(
                    src_ref=slab if src is None else src, dst_ref=slab, send_sem=send_sems.at[a * (N_DEV - 1) + k],
                    recv_sem=recv_sems.at[a * (N_DEV - 1) + k], device_id=to, device_id_type=MESH_IDS)

            first.append(copy(0, (x, y, c), sibling, src=ins[a]))
            landed.append(copy(0, sibling, sibling))
            for j, chip in enumerate(chips):
                first.append(copy(1 + j, (x, y, c), (*chip, c), src=ins[a]))
                passed.append((copy(1 + j, (*chip, c), sibling), copy(4 + j, (*chip, c), sibling)))
                landed.append(copy(4 + j, (*chip, 1 - c), sibling))
        return first, passed, landed

    def start(self, ins, outs, sems):
        for cp in self._local(ins, outs, sems) + self._copies(ins, outs, sems)[0]:
            cp.start()

    def forward(self, ins, outs, sems):
        for arrival, onward in self._copies(ins, outs, sems)[1]:
            arrival.wait_recv()
            onward.start()

    def wait(self, ins, outs, sems):
        first, passed, landed = self._copies(ins, outs, sems)
        for arrival in landed:
            arrival.wait_recv()
        for cp in first + [onward for _, onward in passed]:
            cp.wait_send()
        for cp in self._local(ins, outs, sems):
            cp.wait()


def _split_comm_refs(refs, n_in, n_out, n_scr, comm):
    nc = comm.n if comm is not None else 0
    ns = 3 if comm is not None else 0
    pos, groups = 0, []
    for cnt in (n_in, nc, n_out, nc, n_scr, ns):
        groups.append(refs[pos:pos + cnt])
        pos += cnt
    assert pos == len(refs), (pos, len(refs))
    return groups


def _pcall(body, args, *, name, grid, in_specs, out_specs, out_shape, scratch_shapes=(), sem=None, comm=None):
    in_specs, out_specs, out_shape, scratch_shapes = list(in_specs), list(out_specs), list(out_shape), list(scratch_shapes)
    n_in, n_out, n_scr = len(in_specs), len(out_specs), len(scratch_shapes)
    if comm is None:
        kernel_body = body
    else:
        def kernel_body(*refs):
            ins, cins, outs, couts, scr, sems = _split_comm_refs(refs, n_in, n_out, n_scr, comm)
            ids = [pl.program_id(a) for a in range(len(grid))]
            first, last = ids[0] == 0, ids[0] == grid[0] - 1
            for a in range(1, len(grid)):
                first, last = first & (ids[a] == 0), last & (ids[a] == grid[a] - 1)

            middle = ids[0] == (2 * grid[0]) // 3
            for a in range(1, len(grid)):
                middle = middle & (ids[a] == 0)

            @pl.when(first)
            def _():
                comm.start(cins, couts, sems)

            @pl.when(middle)
            def _():
                comm.forward(cins, couts, sems)

            body(*ins, *outs, *scr)

            @pl.when(last)
            def _():
                comm.wait(cins, couts, sems)

        in_specs, out_specs, out_shape = in_specs + comm.in_specs, out_specs + comm.out_specs, out_shape + comm.out_shape
        scratch_shapes, args = scratch_shapes + comm.scratch, list(args) + comm.arrs
        sem = ("arbitrary",) * len(grid)
    res = pl.pallas_call(kernel_body, name=name, grid=grid, in_specs=in_specs, out_specs=out_specs, out_shape=out_shape,
                         scratch_shapes=scratch_shapes, compiler_params=_cparams(sem))(*args)
    return res[:n_out], res[n_out:]


def _exchange(arrs, name, scatter=False, ex=None):
    if ex is None:
        ex = _Exchange(arrs, scatter=True) if scatter else _Gather2(arrs)

    def body(*refs):
        _, ins, _, outs, _, sems = _split_comm_refs(refs, 0, 0, 0, ex)
        ex.start(ins, outs, sems)
        ex.forward(ins, outs, sems)
        ex.wait(ins, outs, sems)

    return pl.pallas_call(body, name=name, in_specs=ex.in_specs, out_specs=ex.out_specs, out_shape=ex.out_shape,
                          scratch_shapes=ex.scratch)(*ex.arrs)


def _pad_lanes(v, width=LANE):
    return jnp.pad(v, ((0, 0), (0, width - v.shape[1])))


def _shards_to_cols(g):
    return jnp.transpose(g, (1, 0, 2)).reshape(g.shape[1], N_DEV * g.shape[2])


def _cols_to_shards(w):
    return w.astype(BF16).reshape(w.shape[0], N_DEV, w.shape[1] // N_DEV).transpose(1, 0, 2)


def _local_step(x, tgt, mod, w_in_pt, conv_w, conv_b, dt_bias, a_log, d_skip, ssd_norm_w, q_norm_w, k_norm_w,
                attn_norm_w, w_out_sh, w_ff1_sh, w_ff2_sh, norm1_w, norm2_w, core):
    shift1, scale1, gate1, shift2, scale2, gate2 = [mod[i:i + 1] for i in range(N_MOD)]
    dtb, alog, dsk = _pad_lanes(dt_bias), _pad_lanes(a_log), _pad_lanes(d_skip)
    qw, kw = jnp.tile(q_norm_w, (1, ATT_HEADS)), jnp.tile(k_norm_w, (1, ATT_HEADS))

    h1 = _norm_mod_fwd(x, norm1_w, scale1, shift1, "norm1_fwd")
    proj = _matmul(h1, w_in_pt, tb=True, tm=2048, tn=896, tk=1024, name="in_proj")
    pre, act = _conv_fwd(proj, conv_w, conv_b)
    ypre, y_ssd, hall = _ssd_fwd(proj, act, dtb, alog, dsk, ssd_norm_w)
    (o_att, lse), (w_out_g, w_ff1_g, w_ff2_g) = _att_fwd(proj, qw, kw, comm=_Gather2([w_out_sh, w_ff1_sh, w_ff2_sh]))
    w_out = w_out_g.reshape(2 * D_MODEL, D_MODEL)
    w_ff1 = _shards_to_cols(w_ff1_g)
    w_ff2 = w_ff2_g.reshape(D_FF, D_MODEL)
    y_att = _att_norm_fwd(o_att, attn_norm_w)
    ycat = jnp.concatenate([y_ssd, y_att], axis=1)
    mix = _matmul(ycat, w_out, tm=1024, tn=1024, tk=2048, name="out_proj")
    x1, h2 = _norm_mod_fwd(x, norm2_w, scale2, shift2, "norm2_fwd", res=mix, gate=gate1)
    u, act_ff = _matmul(h2, w_ff1, tm=1024, tn=1024, tk=1024, name="ff1", mode="relu2")
    ff = _matmul(act_ff, w_ff2, tm=512, tn=1024, tk=4096, name="ff2")
    loss, dout, dff, dgate2 = _loss_head(x1, ff, gate2, tgt)

    du = _matmul(dff, w_ff2, tb=True, tm=1024, tn=1024, tk=1024, out_dtype=BF16, name="ff2_dx", mode="drelu2", u=u)
    g_ff2 = _matmul(act_ff, dff, ta=True, tm=512, tn=1024, tk=4096, out_dtype=BF16, name="ff2_dw")
    dh2 = _matmul(du, w_ff1, tb=True, tm=512, tn=1024, tk=4096, name="ff1_dx")
    g_ff1 = _matmul(h2, du, ta=True, tm=512, tn=1024, tk=4096, out_dtype=BF16, name="ff1_dw")
    dx1, dshift2, dscale2, g_norm2, dmix, dgate1 = _norm_mod_bwd(dh2, x1, dout, norm2_w, scale2, "norm2_bwd", gate=gate1, mix=mix)

    dycat = _matmul(dmix, w_out, tb=True, tm=1024, tn=1024, tk=1024, name="out_proj_dx")
    g_out = _matmul(ycat, dmix, ta=True, tm=512, tn=1024, tk=4096, out_dtype=BF16, name="out_proj_dw")
    do, stats, g_attn_norm = _att_norm_bwd(dycat, o_att, lse, attn_norm_w)
    ff_slabs = [_cols_to_shards(g_ff1), g_ff2.astype(BF16).reshape(N_DEV, D_FF // N_DEV, D_MODEL)]
    (dq, dk, dv, dqw, dkw), (s_ff1, s_ff2) = _att_bwd(proj, do, stats, qw, kw, comm=_Exchange(ff_slabs, scatter=True))
    out_slabs = [g_out.astype(BF16).reshape(N_DEV, 2 * D_MODEL // N_DEV, D_MODEL)]
    (dz, dact, ddtr, da, g_dsk, g_dtb, g_ssd_norm), (s_out,) = _ssd_bwd(
        dycat, ypre, proj, act, hall, dtb, alog, dsk, ssd_norm_w, comm=_Exchange(out_slabs, scatter=True))
    dxbc, g_conv_w, g_conv_b = _conv_bwd(dact, pre, proj, conv_w)
    dproj = jnp.concatenate([dz, dxbc, dq, dk, dv, ddtr], axis=1)
    g_in_pt = _matmul(dproj, h1, ta=True, tm=896, tn=1024, tk=4096, out_dtype=BF16, name="in_proj_dw")
    in_slabs = _unpack_w_in_rows(g_in_pt).reshape(N_CHIP, 2, IN_W // N_DEV, D_MODEL)
    (sibling_slabs,) = _exchange(None, "swap_w_in_grads", ex=_SiblingSwap([in_slabs]))
    chip_slabs = _chip_sum(lax.dynamic_index_in_dim(in_slabs, core, axis=1, keepdims=False), sibling_slabs)
    dh1, (s_in,) = _matmul(dproj, w_in_pt, tm=512, tn=1024, tk=IN_WP, name="in_proj_dx", comm=_ChipScatter([chip_slabs]))
    grad_x, dshift1, dscale1, g_norm1 = _norm_mod_bwd(dh1, x, dx1, norm1_w, scale1, "norm1_bwd")

    dmod = jnp.concatenate([dshift1, dscale1, dgate1, dshift2, dscale2, dgate2], axis=0)
    g_alog = da[:, :SSD_HEADS] * (-jnp.exp(a_log))
    g_qw = dqw.reshape(ATT_HEADS, HEAD_DIM).sum(axis=0, keepdims=True)
    g_kw = dkw.reshape(ATT_HEADS, HEAD_DIM).sum(axis=0, keepdims=True)
    return dict(loss=loss, grad_x=grad_x, dmod=dmod, norm1_w=g_norm1, norm2_w=g_norm2, w_in=s_in, conv_w=g_conv_w,
                conv_b=g_conv_b, dt_bias=g_dtb[:, :SSD_HEADS], a_log=g_alog, d_skip=g_dsk[:, :SSD_HEADS],
                ssd_norm_w=g_ssd_norm, q_norm_w=g_qw, k_norm_w=g_kw, attn_norm_w=g_attn_norm, w_out=s_out,
                w_ff1=s_ff1, w_ff2=s_ff2)


def _pack_w_in_rows(wt_full):
    o_dt = SSD_D_INNER + CONV_CH
    o_q = o_dt + SSD_HEADS
    pad = jnp.zeros((LANE - SSD_HEADS, wt_full.shape[1]), wt_full.dtype)
    return jnp.concatenate([wt_full[:o_dt], wt_full[o_q:], wt_full[o_dt:o_q], pad], axis=0)


def _unpack_w_in_rows(gt_p):
    return jnp.concatenate([gt_p[:OFF_Q], gt_p[OFF_DT:OFF_DT + SSD_HEADS], gt_p[OFF_Q:OFF_DT]], axis=0)


MISC_FIELDS = (("dt_bias", SSD_HEADS), ("a_log", SSD_HEADS), ("d_skip", SSD_HEADS), ("q_norm_w", HEAD_DIM), ("k_norm_w", HEAD_DIM))
SMALL_LAYOUT = (("b_ada", 6), ("norm1_w", 1), ("norm2_w", 1), ("conv_w", 8), ("conv_b", 2), ("ssd_norm_w", 1),
                ("attn_norm_w", 1), ("misc", 1))


def _pack_small(vals):
    rows = []
    for name, nrow in SMALL_LAYOUT:
        if name == "misc":
            misc = jnp.concatenate([vals[f].reshape(1, n) for f, n in MISC_FIELDS], axis=1)
            rows.append(_pad_lanes(misc, D_MODEL))
        elif name in vals:
            rows.append(vals[name].reshape(nrow, D_MODEL))
        else:
            rows.append(jnp.zeros((nrow, D_MODEL), F32))
    used = sum(n for _, n in SMALL_LAYOUT)
    rows.append(jnp.zeros((SMALL_ROWS - used, D_MODEL), F32))
    return jnp.concatenate(rows, axis=0)


def _unpack_small(packed):
    out, r = {}, 0
    for name, nrow in SMALL_LAYOUT:
        blk = packed[r:r + nrow]
        r += nrow
        if name == "misc":
            c0 = 0
            for f, n in MISC_FIELDS:
                out[f] = blk[:, c0:c0 + n]
                c0 += n
        elif name == "b_ada":
            out[name] = blk.reshape(1, N_MOD * D_MODEL)
        elif name == "conv_w":
            out[name] = blk.reshape(CONV_K, CONV_CH)
        elif name == "conv_b":
            out[name] = blk.reshape(1, CONV_CH)
        else:
            out[name] = blk
    return out


WEIGHT_NAMES = ("norm1_w", "norm2_w", "w_ada", "b_ada", "w_in", "conv_w", "conv_b", "dt_bias", "a_log", "d_skip",
                "ssd_norm_w", "q_norm_w", "k_norm_w", "attn_norm_w", "w_out", "w_ff1", "w_ff2")
SMALL_NAMES = ("norm1_w", "norm2_w", "b_ada", "conv_b", "dt_bias", "a_log", "d_skip", "ssd_norm_w", "q_norm_w",
               "k_norm_w", "attn_norm_w")


def kernel(x, c, norm1_w, norm2_w, w_ada, b_ada, w_in, conv_w, conv_b, dt_bias, a_log, d_skip, ssd_norm_w, q_norm_w, k_norm_w, attn_norm_w, w_out, w_ff1, w_ff2, loss_target, m_norm1_w, m_norm2_w, m_w_ada, m_b_ada, m_w_in, m_conv_w, m_conv_b, m_dt_bias, m_a_log, m_d_skip, m_ssd_norm_w, m_q_norm_w, m_k_norm_w, m_attn_norm_w, m_w_out, m_w_ff1, m_w_ff2, v_norm1_w, v_norm2_w, v_w_ada, v_b_ada, v_w_in, v_conv_w, v_conv_b, v_dt_bias, v_a_log, v_d_skip, v_ssd_norm_w, v_q_norm_w, v_k_norm_w, v_attn_norm_w, v_w_out, v_w_ff1, v_w_ff2):
    args = dict(locals())
    w = {n: args[n] for n in WEIGHT_NAMES}
    m = {n: args["m_" + n] for n in WEIGHT_NAMES}
    v = {n: args["v_" + n] for n in WEIGHT_NAMES}
    me = 4 * lax.axis_index("x") + 2 * lax.axis_index("y") + lax.axis_index("c")

    c_rows = jnp.pad(c, ((0, 7), (0, 0)))
    w_in_t, m_in_t, v_in_t = [jnp.transpose(t["w_in"][0]) for t in (w, m, v)]
    c_g, conv_g, w_in_g = _exchange([c_rows, w["conv_w"][0], w_in_t.astype(BF16)], "gather_w_in", scatter=False)
    c_all = c_g[:, 0, :]
    conv_full = _shards_to_cols(conv_g)
    w_in_pt = _pack_w_in_rows(w_in_g.reshape(IN_W, D_MODEL))

    mod_part = _ada_fwd(c_all, w["w_ada"][0])
    (mod_g,) = _exchange([mod_part], "gather_mod", scatter=False)
    mod_mine = lax.dynamic_index_in_dim(mod_g, me, axis=1, keepdims=False).reshape(1, N_MOD * D_MODEL) + w["b_ada"]
    mod = mod_mine.reshape(N_MOD, D_MODEL)

    res = _local_step(x[0], loss_target[0], mod, w_in_pt, conv_full, w["conv_b"], w["dt_bias"], w["a_log"], w["d_skip"],
                      w["ssd_norm_w"], w["q_norm_w"], w["k_norm_w"], w["attn_norm_w"], w["w_out"][0].astype(BF16),
                      w["w_ff1"][0].astype(BF16), w["w_ff2"][0].astype(BF16), w["norm1_w"], w["norm2_w"], lax.axis_index("c"))

    small_vals = {n: res[n] for n in SMALL_NAMES if n != "b_ada"}
    small_vals["b_ada"] = res["dmod"]
    small_vals["conv_w"] = res["conv_w"]
    (small_g,) = _exchange([_pack_small(small_vals)], "gather_small", scatter=False)

    grads, delta, new_m, new_v = {}, {}, {}, {}
    for name in ("w_out", "w_ff1", "w_ff2"):
        outs = _reduce_adamw(res[name], w[name][0], m[name][0], v[name][0], "adamw_" + name)
        grads[name], delta[name], new_m[name], new_v[name] = [o[None] for o in outs]
    outs = _reduce_adamw(res["w_in"], w_in_t, m_in_t, v_in_t, "adamw_w_in")
    grads["w_in"], delta["w_in"], new_m["w_in"], new_v["w_in"] = [jnp.transpose(o)[None] for o in outs]

    sm = _small_reduce_adamw(small_g, _pack_small({n: w[n] for n in SMALL_NAMES}), _pack_small({n: m[n] for n in SMALL_NAMES}),
                             _pack_small({n: v[n] for n in SMALL_NAMES}))
    sm = [_unpack_small(p) for p in sm]
    for n in SMALL_NAMES:
        grads[n], delta[n], new_m[n], new_v[n] = [p[n] for p in sm]
    shard_w = CONV_CH // N_DEV
    g_conv = lax.dynamic_slice_in_dim(sm[0]["conv_w"], me * shard_w, shard_w, axis=1)
    cw = _adamw_small(g_conv, w["conv_w"][0], m["conv_w"][0], v["conv_w"][0], "adamw_conv_w")
    grads["conv_w"] = g_conv[None]
    delta["conv_w"], new_m["conv_w"], new_v["conv_w"] = [o[None] for o in cw]

    ada_w = w_ada.shape[2]
    dmod_all = small_g[:, :N_MOD, :].reshape(N_DEV, N_MOD * D_MODEL)
    dmod_cols = lax.dynamic_slice_in_dim(dmod_all, me * ada_w, ada_w, axis=1)
    outs = _ada_bwd_adamw(c_all, dmod_cols, w["w_ada"][0], m["w_ada"][0], v["w_ada"][0])
    grads["w_ada"], delta["w_ada"], new_m["w_ada"], new_v["w_ada"] = [o[None] for o in outs]

    loss = lax.psum(res["loss"][0, 0], ("x", "y", "c"))
    return (loss, res["grad_x"][None], *[grads[n] for n in WEIGHT_NAMES], *[delta[n] for n in WEIGHT_NAMES],
            *[new_m[n] for n in WEIGHT_NAMES], *[new_v[n] for n in WEIGHT_NAMES])
```

```python
import functools

import jax
import jax.numpy as jnp
from jax import lax
from jax.experimental import pallas as pl
from jax.experimental.pallas import tpu as pltpu

F32 = jnp.float32
BF16 = jnp.bfloat16
HIGHEST = lax.Precision.HIGHEST
MESH_IDS = pl.DeviceIdType.MESH

N_DEV = 8
D_MODEL = 1024
HEAD_DIM = 64
SSD_HEADS = 16
SSD_GROUPS = 4
HEADS_PER_GROUP = SSD_HEADS // SSD_GROUPS
SSD_STATE = 128
SSD_CHUNK = 128
SSD_D_INNER = SSD_HEADS * HEAD_DIM
GROUP_WIDTH = SSD_D_INNER // SSD_GROUPS
CONV_K = 4
CONV_CH = SSD_D_INNER + 2 * SSD_GROUPS * SSD_STATE
ATT_HEADS = 16
ATT_D = ATT_HEADS * HEAD_DIM
ATT_BLK = 128
DILATIONS = (1, 4, 16)
D_FF = 4 * D_MODEL
N_MOD = 6
EPS = 1e-6
IN_W = SSD_D_INNER + CONV_CH + SSD_HEADS + 3 * ATT_D
LANE = 128
OFF_Z, OFF_XBC, OFF_Q, OFF_K, OFF_V, OFF_DT = 0, 1024, 3072, 4096, 5120, 6144
IN_WP = OFF_DT + LANE

ADAM_LR, ADAM_B1, ADAM_B2, ADAM_EPS, ADAM_WD, ADAM_STEP = 0.001, 0.9, 0.999, 1e-08, 0.01, 10
VMEM_LIMIT = 56 * 1024 * 1024
ROW_TILE = 512
SMALL_ROWS = 24


def _cparams(sem=None):
    return pltpu.CompilerParams(dimension_semantics=sem, vmem_limit_bytes=VMEM_LIMIT)


def _sigmoid(v):
    return 1.0 / (1.0 + jnp.exp(-v))


def _softplus(v):
    y = jnp.exp(-jnp.abs(v))
    small = y * (1.0 - y * (0.5 - y * (1.0 / 3.0)))
    return jnp.maximum(v, 0.0) + jnp.where(y < 0.01, small, jnp.log(1.0 + y))


def _dot(a, b, dims, precision=None):
    return lax.dot_general(a, b, (dims, ((), ())), preferred_element_type=F32, precision=precision)


NN = ((1,), (0,))
NT = ((1,), (1,))
TN = ((0,), (0,))


def _matmul(a, b, *, ta=False, tb=False, tm, tn, tk, out_dtype=F32, name, mode=None, u=None, comm=None):
    m, k = (a.shape[1], a.shape[0]) if ta else a.shape
    n = b.shape[0] if tb else b.shape[1]
    assert m % tm == 0 and n % tn == 0 and k % tk == 0, (name, m, n, k)
    nk = k // tk
    a_spec = pl.BlockSpec((tk, tm), lambda i, j, kk: (kk, i)) if ta else pl.BlockSpec((tm, tk), lambda i, j, kk: (i, kk))
    b_spec = pl.BlockSpec((tn, tk), lambda i, j, kk: (j, kk)) if tb else pl.BlockSpec((tk, tn), lambda i, j, kk: (kk, j))
    o_spec = pl.BlockSpec((tm, tn), lambda i, j, kk: (i, j))
    dims = ((0,) if ta else (1,), (1,) if tb else (0,))
    n_out = 2 if mode == "relu2" else 1

    def body(*refs):
        if mode == "drelu2":
            a_ref, b_ref, u_ref = refs[:3]
            rest = refs[3:]
        else:
            a_ref, b_ref = refs[:2]
            u_ref = None
            rest = refs[2:]
        outs = rest[:n_out]
        part = _dot(a_ref[...], b_ref[...], dims)

        def finish(r):
            if mode == "relu2":
                outs[0][...] = r.astype(BF16)
                rr = jnp.maximum(r, 0.0)
                outs[1][...] = (rr * rr).astype(BF16)
            elif mode == "drelu2":
                outs[0][...] = (r * (2.0 * jnp.maximum(u_ref[...].astype(F32), 0.0))).astype(out_dtype)
            else:
                outs[0][...] = r.astype(out_dtype)

        if nk == 1:
            finish(part)
        else:
            acc = rest[n_out]
            kk = pl.program_id(2)

            @pl.when(kk == 0)
            def _():
                acc[...] = part

            @pl.when(kk > 0)
            def _():
                acc[...] += part

            @pl.when(kk == nk - 1)
            def _():
                finish(acc[...])

    in_specs = [a_spec, b_spec]
    args = [a, b]
    if mode == "drelu2":
        in_specs.append(o_spec)
        args.append(u)
    if mode == "relu2":
        out_shape = [jax.ShapeDtypeStruct((m, n), BF16), jax.ShapeDtypeStruct((m, n), BF16)]
    else:
        out_shape = [jax.ShapeDtypeStruct((m, n), out_dtype)]
    outs, comm_outs = _pcall(
        body, args, name=name, grid=(m // tm, n // tn, nk), in_specs=in_specs, out_specs=[o_spec] * n_out,
        out_shape=out_shape, scratch_shapes=[pltpu.VMEM((tm, tn), F32)] if nk > 1 else [],
        sem=("parallel", "parallel", "arbitrary"), comm=comm)
    res = tuple(outs) if mode == "relu2" else outs[0]
    return res if comm is None else (res, comm_outs)


def _norm_mod_fwd(x, nw, scale, shift, name, res=None, gate=None):
    s, d = x.shape
    row = pl.BlockSpec((ROW_TILE, d), lambda i: (i, 0))
    vec = pl.BlockSpec((1, d), lambda i: (0, 0))
    with_res = res is not None

    def body(*refs):
        if with_res:
            x_ref, res_ref, gate_ref, nw_ref, sc_ref, sh_ref, x1_ref, h_ref = refs
            xv = x_ref[...] + gate_ref[...] * res_ref[...]
            x1_ref[...] = xv
        else:
            x_ref, nw_ref, sc_ref, sh_ref, h_ref = refs
            xv = x_ref[...]
        r = lax.rsqrt(jnp.mean(xv * xv, axis=-1, keepdims=True) + EPS)
        h_ref[...] = ((xv * r) * nw_ref[...] * (1.0 + sc_ref[...]) + sh_ref[...]).astype(BF16)

    if with_res:
        in_specs = [row, row, vec, vec, vec, vec]
        args = (x, res, gate, nw, scale, shift)
        out_shape = (jax.ShapeDtypeStruct((s, d), F32), jax.ShapeDtypeStruct((s, d), BF16))
        out_specs = (row, row)
    else:
        in_specs = [row, vec, vec, vec]
        args = (x, nw, scale, shift)
        out_shape = jax.ShapeDtypeStruct((s, d), BF16)
        out_specs = row
    return pl.pallas_call(body, name=name, grid=(s // ROW_TILE,), in_specs=in_specs, out_specs=out_specs,
                          out_shape=out_shape, compiler_params=_cparams(("parallel",)))(*args)


def _norm_mod_bwd(dh, xin, dres, nw, scale, name, gate=None, mix=None):
    s, d = xin.shape
    row = pl.BlockSpec((ROW_TILE, d), lambda i: (i, 0))
    vec = pl.BlockSpec((1, d), lambda i: (0, 0))
    with_gate = gate is not None

    def body(*refs):
        if with_gate:
            dh_ref, x_ref, dres_ref, nw_ref, sc_ref, gate_ref, mix_ref, dx_ref, dsh_ref, dsc_ref, dnw_ref, dmix_ref, dg_ref = refs
        else:
            dh_ref, x_ref, dres_ref, nw_ref, sc_ref, dx_ref, dsh_ref, dsc_ref, dnw_ref = refs
        i = pl.program_id(0)

        @pl.when(i == 0)
        def _():
            dsh_ref[...] = jnp.zeros_like(dsh_ref)
            dsc_ref[...] = jnp.zeros_like(dsc_ref)
            dnw_ref[...] = jnp.zeros_like(dnw_ref)
            if with_gate:
                dg_ref[...] = jnp.zeros_like(dg_ref)

        xv = x_ref[...]
        dhv = dh_ref[...]
        r = lax.rsqrt(jnp.mean(xv * xv, axis=-1, keepdims=True) + EPS)
        nrm = xv * r
        one_sc = 1.0 + sc_ref[...]
        dhn = dhv * nrm
        dsh_ref[...] += jnp.sum(dhv, axis=0, keepdims=True)
        dsc_ref[...] += jnp.sum(dhn, axis=0, keepdims=True) * nw_ref[...]
        dnw_ref[...] += jnp.sum(dhn, axis=0, keepdims=True) * one_sc
        dn = dhv * (nw_ref[...] * one_sc)
        dx = dres_ref[...] + r * (dn - nrm * jnp.mean(dn * nrm, axis=-1, keepdims=True))
        dx_ref[...] = dx
        if with_gate:
            dmix_ref[...] = (gate_ref[...] * dx).astype(BF16)
            dg_ref[...] += jnp.sum(dx * mix_ref[...], axis=0, keepdims=True)

    vshape = jax.ShapeDtypeStruct((1, d), F32)
    in_specs = [row, row, row, vec, vec]
    args = [dh, xin, dres, nw, scale]
    out_shape = [jax.ShapeDtypeStruct((s, d), F32), vshape, vshape, vshape]
    out_specs = [row, vec, vec, vec]
    if with_gate:
        in_specs += [vec, row]
        args += [gate, mix]
        out_shape += [jax.ShapeDtypeStruct((s, d), BF16), vshape]
        out_specs += [row, vec]
    return pl.pallas_call(body, name=name, grid=(s // ROW_TILE,), in_specs=in_specs, out_specs=out_specs,
                          out_shape=out_shape, compiler_params=_cparams(("arbitrary",)))(*args)


def _loss_head(x1, ff, gate2, tgt):
    s, d = x1.shape
    row = pl.BlockSpec((ROW_TILE, d), lambda i: (i, 0))
    vec = pl.BlockSpec((1, d), lambda i: (0, 0))
    one = pl.BlockSpec((1, 1), lambda i: (0, 0))

    def body(x1_ref, ff_ref, g_ref, t_ref, loss_ref, dout_ref, dff_ref, dg_ref):
        i = pl.program_id(0)

        @pl.when(i == 0)
        def _():
            loss_ref[...] = jnp.zeros_like(loss_ref)
            dg_ref[...] = jnp.zeros_like(dg_ref)

        ffv = ff_ref[...]
        err = x1_ref[...] + g_ref[...] * ffv - t_ref[...]
        loss_ref[...] += (0.5 / d) * jnp.sum(err * err).reshape(1, 1)
        dout = err * (1.0 / d)
        dout_ref[...] = dout
        dff_ref[...] = (g_ref[...] * dout).astype(BF16)
        dg_ref[...] += jnp.sum(dout * ffv, axis=0, keepdims=True)

    return pl.pallas_call(
        body, name="loss_head", grid=(s // ROW_TILE,), in_specs=[row, row, vec, row],
        out_specs=[one, row, row, vec],
        out_shape=[jax.ShapeDtypeStruct((1, 1), F32), jax.ShapeDtypeStruct((s, d), F32),
                   jax.ShapeDtypeStruct((s, d), BF16), jax.ShapeDtypeStruct((1, d), F32)],
        compiler_params=_cparams(("arbitrary",)))(x1, ff, gate2, tgt)


CONV_COLS = 256
HALO = 8


def _shift_down(cur, halo, k):
    if k == 0:
        return cur
    rolled = pltpu.roll(cur, k, axis=0)
    top = jnp.where(lax.broadcasted_iota(jnp.int32, halo.shape, 0) < k, pltpu.roll(halo, k, axis=0), rolled[:HALO])
    return jnp.concatenate([top, rolled[HALO:]], axis=0)


def _shift_up(cur, halo, k):
    if k == 0:
        return cur
    t = cur.shape[0]
    rolled = pltpu.roll(cur, t - k, axis=0)
    bot = jnp.where(lax.broadcasted_iota(jnp.int32, halo.shape, 0) >= HALO - k, pltpu.roll(halo, HALO - k, axis=0),
                    rolled[t - HALO:])
    return jnp.concatenate([rolled[:t - HALO], bot], axis=0)


def _conv_fwd(proj, conv_w, conv_b):
    s = proj.shape[0]
    nr = s // ROW_TILE
    cb0 = OFF_XBC // CONV_COLS
    hb = ROW_TILE // HALO
    cur = pl.BlockSpec((ROW_TILE, CONV_COLS), lambda j, r: (r, cb0 + j))
    prev = pl.BlockSpec((HALO, CONV_COLS), lambda j, r: (jnp.maximum(r * hb - 1, 0), cb0 + j))
    out = pl.BlockSpec((ROW_TILE, CONV_COLS), lambda j, r: (r, j))

    def body(u_ref, up_ref, w_ref, b_ref, pre_ref, act_ref):
        r = pl.program_id(1)
        u = u_ref[...]
        halo = jnp.where(r > 0, up_ref[...], 0.0)
        acc = b_ref[...] + w_ref[CONV_K - 1:CONV_K, :] * u
        for k in range(1, CONV_K):
            acc = acc + w_ref[CONV_K - 1 - k:CONV_K - k, :] * _shift_down(u, halo, k)
        pre_ref[...] = acc
        act_ref[...] = acc * _sigmoid(acc)

    return pl.pallas_call(
        body, name="conv_fwd", grid=(CONV_CH // CONV_COLS, nr),
        in_specs=[cur, prev, pl.BlockSpec((CONV_K, CONV_COLS), lambda j, r: (0, j)),
                  pl.BlockSpec((1, CONV_COLS), lambda j, r: (0, j))],
        out_specs=[out, out],
        out_shape=[jax.ShapeDtypeStruct((s, CONV_CH), F32), jax.ShapeDtypeStruct((s, CONV_CH), F32)],
        compiler_params=_cparams(("parallel", "arbitrary")))(proj, proj, conv_w, conv_b)


def _conv_bwd(dact, pre, proj, conv_w):
    s = proj.shape[0]
    nr = s // ROW_TILE
    cb0 = OFF_XBC // CONV_COLS
    hb = ROW_TILE // HALO
    last_halo = s // HALO - 1
    cur = pl.BlockSpec((ROW_TILE, CONV_COLS), lambda j, r: (r, j))
    nxt = pl.BlockSpec((HALO, CONV_COLS), lambda j, r: (jnp.minimum((r + 1) * hb, last_halo), j))
    ucur = pl.BlockSpec((ROW_TILE, CONV_COLS), lambda j, r: (r, cb0 + j))
    uprev = pl.BlockSpec((HALO, CONV_COLS), lambda j, r: (jnp.maximum(r * hb - 1, 0), cb0 + j))
    wspec = pl.BlockSpec((CONV_K, CONV_COLS), lambda j, r: (0, j))
    bspec = pl.BlockSpec((1, CONV_COLS), lambda j, r: (0, j))

    def dsilu(p):
        sg = _sigmoid(p)
        return sg * (1.0 + p * (1.0 - sg))

    def body(da_ref, dan_ref, pre_ref, pren_ref, u_ref, up_ref, w_ref, du_ref, dw_ref, db_ref):
        r = pl.program_id(1)

        @pl.when(r == 0)
        def _():
            dw_ref[...] = jnp.zeros_like(dw_ref)
            db_ref[...] = jnp.zeros_like(db_ref)

        dpre = da_ref[...] * dsilu(pre_ref[...])
        dnext = jnp.where(r < nr - 1, dan_ref[...] * dsilu(pren_ref[...]), 0.0)
        u = u_ref[...]
        halo = jnp.where(r > 0, up_ref[...], 0.0)
        du = w_ref[CONV_K - 1:CONV_K, :] * dpre
        dws = [jnp.sum(dpre * u, axis=0, keepdims=True)]
        for k in range(1, CONV_K):
            du = du + w_ref[CONV_K - 1 - k:CONV_K - k, :] * _shift_up(dpre, dnext, k)
            dws.append(jnp.sum(dpre * _shift_down(u, halo, k), axis=0, keepdims=True))
        du_ref[...] = du.astype(BF16)
        dw_ref[...] += jnp.concatenate(dws[::-1], axis=0)
        db_ref[...] += jnp.sum(dpre, axis=0, keepdims=True)

    return pl.pallas_call(
        body, name="conv_bwd", grid=(CONV_CH // CONV_COLS, nr),
        in_specs=[cur, nxt, cur, nxt, ucur, uprev, wspec],
        out_specs=[cur, wspec, bspec],
        out_shape=[jax.ShapeDtypeStruct((s, CONV_CH), BF16), jax.ShapeDtypeStruct((CONV_K, CONV_CH), F32),
                   jax.ShapeDtypeStruct((1, CONV_CH), F32)],
        compiler_params=_cparams(("parallel", "arbitrary")))(dact, dact, pre, pre, proj, proj, conv_w)


def _ssd_common(dtr, dtb, alog):
    lane = lax.broadcasted_iota(jnp.int32, (1, LANE), 1)
    head_lane = lane < SSD_HEADS
    dt = jnp.where(head_lane, _softplus(dtr + dtb), 0.0)
    a = jnp.where(head_lane, -jnp.exp(alog), 0.0)
    row = lax.broadcasted_iota(jnp.int32, (SSD_CHUNK, SSD_CHUNK), 0)
    col = lax.broadcasted_iota(jnp.int32, (SSD_CHUNK, SSD_CHUNK), 1)
    tril = row >= col
    cs = _dot(tril.astype(F32), dt * a, NN, precision=HIGHEST)
    return dt, a, cs, cs.T, tril, lane


def _split_bf16(v, passes):
    terms, rest = [], v
    for _ in range(passes):
        t = rest.astype(BF16)
        terms.append(t)
        rest = rest - t.astype(F32)
    return terms


def _dot_split(v, m, dims, passes):
    terms = _split_bf16(v, passes)
    if passes == 1:
        return _dot(terms[0], m, dims)
    return _dot(jnp.concatenate(terms, axis=1), jnp.concatenate([m] * passes, axis=0 if dims == NN else 1), dims)


def _ssd_constants():
    heads = jnp.arange(LANE)[:, None]
    exp_mat = (heads == (jnp.arange(SSD_D_INNER)[None, :] // HEAD_DIM)).astype(BF16)
    ind4 = ((jnp.arange(SSD_HEADS * SSD_CHUNK)[:, None] // SSD_CHUNK) == jnp.arange(LANE)[None, :]).astype(BF16)
    return exp_mat, ind4


def _expand_heads(v):
    return jnp.repeat(v[:, :SSD_HEADS], HEAD_DIM, axis=1)


def _ssd_prep(dtr, dtb, alog, exp_mat):
    dt, a, cs, cst, tril, lane = _ssd_common(dtr, dtb, alog)
    return dt, a, cs, cst, tril, lane, _dot_split(dt, exp_mat, NN, 2), _dot_split(cs, exp_mat, NN, 3)


def _chunk_decay_rows(cs, g):
    parts = []
    for e in range(HEADS_PER_GROUP):
        h = g * HEADS_PER_GROUP + e
        parts.append(jnp.broadcast_to(jnp.exp(cs[SSD_CHUNK - 1:SSD_CHUNK, h:h + 1]), (HEAD_DIM, SSD_STATE)))
    return jnp.concatenate(parts, axis=0)


def _ssd_fwd(proj, act, dtb, alog, dsk, nw):
    s = proj.shape[0]
    nc = s // SSD_CHUNK
    bc_w = SSD_GROUPS * SSD_STATE
    exp_mat, _ = _ssd_constants()

    def body(z_ref, dtr_ref, xs_ref, b_ref, c_ref, dtb_ref, alog_ref, dskx_ref, nw_ref, exp_ref,
             ypre_ref, yssd_ref, hall_ref, h_scr):
        @pl.when(pl.program_id(0) == 0)
        def _():
            h_scr[...] = jnp.zeros_like(h_scr)

        dt, a, cs, cst, tril, lane, dtx, csx = _ssd_prep(dtr_ref[...], dtb_ref[...], alog_ref[...], exp_ref[...])
        cs_last_x = csx[SSD_CHUNK - 1:SSD_CHUNK, :]
        xs = xs_ref[...]
        xdt = xs * dtx
        xdtb = xdt.astype(BF16)
        xdec = (xdt * jnp.exp(cs_last_x - csx)).astype(BF16)
        ecsx = jnp.exp(csx)
        head_of_lane = lax.broadcasted_iota(jnp.int32, (1, GROUP_WIDTH), 1) // HEAD_DIM
        for g in range(SSD_GROUPS):
            gs = slice(g * GROUP_WIDTH, (g + 1) * GROUP_WIDTH)
            bg = b_ref[:, g * SSD_STATE:(g + 1) * SSD_STATE].astype(BF16)
            cg = c_ref[:, g * SSD_STATE:(g + 1) * SSD_STATE].astype(BF16)
            cb = _dot(cg, bg, NT)
            hprev = h_scr[gs, :]
            hall_ref[0, gs, :] = hprev
            gms, rhs = [], []
            xg = xdtb[:, gs]
            for e in range(HEADS_PER_GROUP):
                h = g * HEADS_PER_GROUP + e
                lm = jnp.exp(jnp.where(tril, cs[:, h:h + 1] - cst[h:h + 1, :], -1e30))
                gms.append((cb * lm).astype(BF16))
                rhs.append(jnp.where(head_of_lane == e, xg, jnp.zeros_like(xg)))
            y = _dot(jnp.concatenate(gms, axis=1), jnp.concatenate(rhs, axis=0), NN)
            y = y + ecsx[:, gs] * _dot(cg, hprev.astype(BF16), NT)
            y = y + dskx_ref[:, gs] * xs[:, gs]
            h_scr[gs, :] = hprev * _chunk_decay_rows(cs, g) + _dot(xdec[:, gs], bg, TN)
            ypre_ref[:, gs] = y
            z = z_ref[:, gs]
            yg = y * (z * _sigmoid(z))
            r = lax.rsqrt(jnp.mean(yg * yg, axis=-1, keepdims=True) + EPS)
            yssd_ref[:, gs] = (yg * r * nw_ref[:, gs]).astype(BF16)

    row_d = lambda cb: pl.BlockSpec((SSD_CHUNK, SSD_D_INNER), lambda c: (c, cb))
    small = pl.BlockSpec((1, LANE), lambda c: (0, 0))
    wide = pl.BlockSpec((1, SSD_D_INNER), lambda c: (0, 0))
    return pl.pallas_call(
        body, name="ssd_fwd", grid=(nc,),
        in_specs=[row_d(OFF_Z // SSD_D_INNER),
                  pl.BlockSpec((SSD_CHUNK, LANE), lambda c: (c, OFF_DT // LANE)),
                  row_d(0),
                  pl.BlockSpec((SSD_CHUNK, bc_w), lambda c: (c, SSD_D_INNER // bc_w)),
                  pl.BlockSpec((SSD_CHUNK, bc_w), lambda c: (c, SSD_D_INNER // bc_w + 1)),
                  small, small, wide, wide, pl.BlockSpec((LANE, SSD_D_INNER), lambda c: (0, 0))],
        out_specs=[row_d(0), row_d(0), pl.BlockSpec((1, SSD_D_INNER, SSD_STATE), lambda c: (c, 0, 0))],
        out_shape=[jax.ShapeDtypeStruct((s, SSD_D_INNER), F32), jax.ShapeDtypeStruct((s, SSD_D_INNER + ATT_D), BF16),
                   jax.ShapeDtypeStruct((nc, SSD_D_INNER, SSD_STATE), F32)],
        scratch_shapes=[pltpu.VMEM((SSD_D_INNER, SSD_STATE), F32)],
        compiler_params=_cparams(("arbitrary",)))(proj, proj, act, act, act, dtb, alog, _expand_heads(dsk), nw, exp_mat)


def _ssd_bwd(dycat, ypre, proj, act, hall, dtb, alog, dsk, nw, comm=None):
    s = proj.shape[0]
    nc = s // SSD_CHUNK
    bc_w = SSD_GROUPS * SSD_STATE

    exp_mat, ind4 = _ssd_constants()
    seg_passes = 1

    def body(dy_ref, ypre_ref, z_ref, dtr_ref, xs_ref, b_ref, c_ref, hall_ref, dtb_ref, alog_ref, dskx_ref, nw_ref,
             exp_ref, ind4_ref, dz_ref, dact_ref, ddtr_ref, da_ref, ddsk_ref, ddtb_ref, dnw_ref, dh_scr):
        @pl.when(pl.program_id(0) == 0)
        def _():
            dh_scr[...] = jnp.zeros_like(dh_scr)
            da_ref[...] = jnp.zeros_like(da_ref)
            ddsk_ref[...] = jnp.zeros_like(ddsk_ref)
            ddtb_ref[...] = jnp.zeros_like(ddtb_ref)
            dnw_ref[...] = jnp.zeros_like(dnw_ref)

        dtr = dtr_ref[...]
        dt, a, cs, cst, tril, lane, dtx, csx = _ssd_prep(dtr, dtb_ref[...], alog_ref[...], exp_ref[...])
        cs_last_x = csx[SSD_CHUNK - 1:SSD_CHUNK, :]
        xs = xs_ref[...]
        xdt = xs * dtx
        xdtb = xdt.astype(BF16)
        decx = jnp.exp(cs_last_x - csx)
        xdecf = xdt * decx
        xdec = xdecf.astype(BF16)
        ecsx = jnp.exp(csx)
        head_of_lane = lax.broadcasted_iota(jnp.int32, (1, GROUP_WIDTH), 1) // HEAD_DIM
        last_row = lax.broadcasted_iota(jnp.int32, (SSD_CHUNK, 1), 0) == SSD_CHUNK - 1
        dcs_col = jnp.zeros((SSD_CHUNK, LANE), F32)
        dcs_row = jnp.zeros((SSD_CHUNK, LANE), F32)
        ddt = jnp.zeros((SSD_CHUNK, LANE), F32)
        ddsk = jnp.zeros((1, LANE), F32)
        hsum = jnp.zeros((1, LANE), F32)
        t1_sum = jnp.zeros((1, LANE), F32)
        for g in range(SSD_GROUPS):
            gs = slice(g * GROUP_WIDTH, (g + 1) * GROUP_WIDTH)
            bsl = slice(g * SSD_STATE, (g + 1) * SSD_STATE)
            exp_g = exp_ref[:, gs]
            ind4_g = ind4_ref[g * HEADS_PER_GROUP * SSD_CHUNK:(g + 1) * HEADS_PER_GROUP * SSD_CHUNK, :]
            z = z_ref[:, gs]
            sg = _sigmoid(z)
            sz = z * sg
            ypre = ypre_ref[:, gs]
            yg = ypre * sz
            r = lax.rsqrt(jnp.mean(yg * yg, axis=-1, keepdims=True) + EPS)
            nrm = yg * r
            dyo_n = dy_ref[:, gs]
            dnw_ref[:, gs] += jnp.sum(dyo_n * nrm, axis=0, keepdims=True)
            dn = dyo_n * nw_ref[:, gs]
            dyg = r * (dn - nrm * jnp.mean(dn * nrm, axis=-1, keepdims=True))
            dz_ref[:, gs] = (dyg * ypre * (sg * (1.0 + z * (1.0 - sg)))).astype(BF16)
            dy = dyg * sz

            bg = b_ref[:, bsl].astype(BF16)
            cg = c_ref[:, bsl].astype(BF16)
            cb = _dot(cg, bg, NT)
            hprev = hall_ref[0, gs, :]
            hb = hprev.astype(BF16)
            dhn = dh_scr[gs, :]
            dhb = dhn.astype(BF16)
            xs_g, xdt_g = xs[:, gs], xdtb[:, gs]
            w_off = _dot(cg, hb, NT)
            dyo = dy * ecsx[:, gs]
            dyob = dyo.astype(BF16)
            dcg = _dot(dyob, hb, NN)
            dh_y = _dot(dyob, cg, TN)
            r_st = _dot(bg, dhb, NT)
            dbg = _dot(xdec[:, gs], dhb, NN)
            dyb = dy.astype(BF16)
            gms, gmbs, lms, dys = [], [], [], []
            for e in range(HEADS_PER_GROUP):
                h = g * HEADS_PER_GROUP + e
                lm = jnp.exp(jnp.where(tril, cs[:, h:h + 1] - cst[h:h + 1, :], -1e30))
                gm = cb * lm
                lms.append(lm)
                gms.append(gm)
                gmbs.append(gm.astype(BF16))
                dys.append(jnp.where(head_of_lane == e, dyb, jnp.zeros_like(dyb)))
            dxdt = _dot(jnp.concatenate(gmbs, axis=0), jnp.concatenate(dys, axis=0), TN) + decx[:, gs] * r_st
            dcb = jnp.zeros((SSD_CHUNK, SSD_CHUNK), F32)
            mms = []
            for e in range(HEADS_PER_GROUP):
                dg = _dot(dys[e], xdt_g, NT)
                mms.append(dg * gms[e])
                dcb = dcb + dg * lms[e]
            seg = _dot_split(jnp.concatenate([dyo * w_off, xdecf[:, gs] * r_st, dxdt * xs_g, dy * xs_g], axis=0), exp_g, NT, seg_passes)
            v1, t1, ddt_g, dsk_g = [seg[i * SSD_CHUNK:(i + 1) * SSD_CHUNK] for i in range(4)]
            dcs_col = dcs_col + v1 - t1 + _dot_split(jnp.concatenate(mms, axis=1), ind4_g, NN, seg_passes)
            for t in _split_bf16(jnp.concatenate(mms, axis=0), seg_passes):
                dcs_row = dcs_row + _dot(ind4_g, t, TN)
            ddt = ddt + ddt_g
            ddsk = ddsk + jnp.sum(dsk_g, axis=0, keepdims=True)
            t1_sum = t1_sum + jnp.sum(t1, axis=0, keepdims=True)
            for e in range(HEADS_PER_GROUP):
                h = g * HEADS_PER_GROUP + e
                hs = slice(e * HEAD_DIM, (e + 1) * HEAD_DIM)
                hsum = hsum + jnp.where(lane == h, jnp.sum(dhn[hs, :] * hprev[hs, :]).reshape(1, 1), 0.0)
            dh_scr[gs, :] = dhn * _chunk_decay_rows(cs, g) + dh_y
            dcbb = dcb.astype(BF16)
            dact_ref[:, gs] = dxdt * dtx[:, gs] + dskx_ref[:, gs] * dy
            dact_ref[:, SSD_D_INNER + g * SSD_STATE:SSD_D_INNER + (g + 1) * SSD_STATE] = dbg + _dot(dcbb, cg, TN)
            dact_ref[:, SSD_D_INNER + bc_w + g * SSD_STATE:SSD_D_INNER + bc_w + (g + 1) * SSD_STATE] = dcg + _dot(dcbb, bg, NN)
        dlast = t1_sum + jnp.exp(cs[SSD_CHUNK - 1:SSD_CHUNK, :]) * hsum
        dcs = dcs_col - dcs_row.T + jnp.where(last_row, dlast, 0.0)
        row = lax.broadcasted_iota(jnp.int32, (SSD_CHUNK, SSD_CHUNK), 0)
        col = lax.broadcasted_iota(jnp.int32, (SSD_CHUNK, SSD_CHUNK), 1)
        dda = _dot((col >= row).astype(F32), dcs, NN, precision=HIGHEST)
        ddt = ddt + dda * a
        da_ref[...] += jnp.sum(dda * dt, axis=0, keepdims=True)
        ddtr = jnp.where(lane < SSD_HEADS, ddt * _sigmoid(dtr + dtb_ref[...]), 0.0)
        ddtr_ref[...] = ddtr.astype(BF16)
        ddtb_ref[...] += jnp.sum(ddtr, axis=0, keepdims=True)
        ddsk_ref[...] += ddsk

    rev = lambda c: nc - 1 - c
    row_d = lambda cb: pl.BlockSpec((SSD_CHUNK, SSD_D_INNER), lambda c: (rev(c), cb))
    small = pl.BlockSpec((1, LANE), lambda c: (0, 0))
    wide = pl.BlockSpec((1, SSD_D_INNER), lambda c: (0, 0))
    small_shape = jax.ShapeDtypeStruct((1, LANE), F32)
    return _pcall(
        body, (dycat, ypre, proj, proj, act, act, act, hall, dtb, alog, _expand_heads(dsk), nw, exp_mat, ind4),
        name="ssd_bwd", grid=(nc,),
        in_specs=[row_d(0), row_d(0), row_d(OFF_Z // SSD_D_INNER),
                  pl.BlockSpec((SSD_CHUNK, LANE), lambda c: (rev(c), OFF_DT // LANE)),
                  row_d(0),
                  pl.BlockSpec((SSD_CHUNK, bc_w), lambda c: (rev(c), SSD_D_INNER // bc_w)),
                  pl.BlockSpec((SSD_CHUNK, bc_w), lambda c: (rev(c), SSD_D_INNER // bc_w + 1)),
                  pl.BlockSpec((1, SSD_D_INNER, SSD_STATE), lambda c: (rev(c), 0, 0)),
                  small, small, wide, wide, pl.BlockSpec((LANE, SSD_D_INNER), lambda c: (0, 0)),
                  pl.BlockSpec((SSD_HEADS * SSD_CHUNK, LANE), lambda c: (0, 0))],
        out_specs=[row_d(0), pl.BlockSpec((SSD_CHUNK, CONV_CH), lambda c: (rev(c), 0)),
                   pl.BlockSpec((SSD_CHUNK, LANE), lambda c: (rev(c), 0)), small, small, small, wide],
        out_shape=[jax.ShapeDtypeStruct((s, SSD_D_INNER), BF16), jax.ShapeDtypeStruct((s, CONV_CH), F32),
                   jax.ShapeDtypeStruct((s, LANE), BF16), small_shape, small_shape, small_shape,
                   jax.ShapeDtypeStruct((1, SSD_D_INNER), F32)],
        scratch_shapes=[pltpu.VMEM((SSD_D_INNER, SSD_STATE), F32)], sem=("arbitrary",), comm=comm)


def _head_mean_matrix():
    row = lax.broadcasted_iota(jnp.int32, (LANE, LANE), 0) // HEAD_DIM
    col = lax.broadcasted_iota(jnp.int32, (LANE, LANE), 1) // HEAD_DIM
    return (row == col).astype(F32)


def _head_sum2(v, ones_bd):
    hi = v.astype(BF16)
    lo = (v - hi.astype(F32)).astype(BF16)
    return _dot(jnp.concatenate([hi, lo], axis=1), jnp.concatenate([ones_bd, ones_bd], axis=0), NN)


def _head_norm(x, w, scale, ones_bd):
    ms = _head_sum2(x * x, ones_bd) * (1.0 / HEAD_DIM)
    return (x * lax.rsqrt(ms + EPS)) * (w * scale)


PRO_ROWS = 256
ATT_GROUP_FWD = 16
ATT_GROUP_BWD = 8
KEYS = 2 * ATT_BLK
NEG = -1e30
HALF = HEAD_DIM // 2


def _rows(start, size, dil):
    return pl.ds(start, size) if dil == 1 else pl.ds(start, size, stride=dil)


def _fill_bias(bias_ref):
    row = lax.broadcasted_iota(jnp.int32, (ATT_BLK, 2 * KEYS), 0)
    col = lax.broadcasted_iota(jnp.int32, (ATT_BLK, 2 * KEYS), 1) & (KEYS - 1)
    for first, off in ((0, 0), (1, ATT_BLK)):
        dist = off + row - col
        bias_ref[first] = jnp.where((dist >= 0) & (dist <= ATT_BLK), 0.0, NEG)


def _pair(a, b):
    return jnp.concatenate([jnp.broadcast_to(a, (ATT_BLK, KEYS)), jnp.broadcast_to(b, (ATT_BLK, KEYS))], axis=1)


def _split_heads(x, is_a):
    zero = jnp.zeros_like(x)
    return jnp.concatenate([jnp.where(is_a, x, zero), jnp.where(is_a, zero, x)], axis=0)


def _block_ids(b, nb):
    i = b & (nb - 1)
    q0 = pl.multiple_of(b * ATT_BLK, ATT_BLK)
    k0 = pl.multiple_of((b - jnp.minimum(i, 1)) * ATT_BLK, ATT_BLK)
    return pl.ds(q0, ATT_BLK), pl.ds(k0, KEYS), jnp.minimum(i, 1)


def _att_fwd(proj, qw, kw, comm=None):
    s = proj.shape[0]
    nblk = s // ATT_BLK
    assert all((s // d) // ATT_BLK >= 2 for d in DILATIONS)
    blk = lambda off: pl.BlockSpec((s, LANE), lambda i: (0, off // LANE + i))
    wspec = pl.BlockSpec((1, LANE), lambda i: (0, i))
    oblk = pl.BlockSpec((s, LANE), lambda i: (0, i))

    def body(q_ref, k_ref, v_ref, qw_ref, kw_ref, o_ref, lse_ref, qn, kn, q_cm, k_cm, v_cm, m_acc, l_acc, o_d, m_d, l_d, bias):
        ones_bd = _head_mean_matrix().astype(BF16)
        is_a = lax.broadcasted_iota(jnp.int32, (1, LANE), 1) < HEAD_DIM
        ones_ext = _split_heads(jnp.ones((KEYS, LANE), BF16), is_a)
        _fill_bias(bias)

        def pro(j, c):
            rows = pl.ds(pl.multiple_of(j * PRO_ROWS, PRO_ROWS), PRO_ROWS)
            qn[rows, :] = _head_norm(q_ref[rows, :], qw_ref[...], HEAD_DIM ** -0.5, ones_bd)
            kn[rows, :] = _head_norm(k_ref[rows, :], kw_ref[...], 1.0, ones_bd)
            return c

        lax.fori_loop(0, s // PRO_ROWS, pro, 0)

        for dil in DILATIONS:
            ln = s // dil
            nb = ln // ATT_BLK
            o_out, m_out, l_out = (o_ref, m_acc, l_acc) if dil == 1 else (o_d, m_d, l_d)
            for r in range(dil):
                def relayout(j, c, dil=dil, r=r, ln=ln):
                    j0 = pl.multiple_of(j * PRO_ROWS, PRO_ROWS)
                    src = _rows(r + dil * j0, PRO_ROWS, dil)
                    dst = pl.ds(r * ln + j0, PRO_ROWS)
                    q_cm[dst, :] = qn[src, :].astype(BF16)
                    k_cm[dst, :] = kn[src, :].astype(BF16)
                    v_cm[dst, :] = v_ref[src, :].astype(BF16)
                    return c

                lax.fori_loop(0, ln // PRO_ROWS, relayout, 0)

            def step(bg, c, nb=nb, o_out=o_out, m_out=m_out, l_out=l_out):
                ids = [_block_ids(bg * ATT_GROUP_FWD + u, nb) for u in range(ATT_GROUP_FWD)]
                kbs = [_split_heads(k_cm[krows, :], is_a) for _, krows, _ in ids]
                scs = [_dot(q_cm[qrows, :], kb, NT) + bias[first] for (qrows, _, first), kb in zip(ids, kbs)]
                mas = [jnp.max(sc[:, :KEYS], axis=-1, keepdims=True) for sc in scs]
                mbs = [jnp.max(sc[:, KEYS:], axis=-1, keepdims=True) for sc in scs]
                ps = [jnp.exp(sc - _pair(ma, mb)).astype(BF16) for sc, ma, mb in zip(scs, mas, mbs)]
                vbs = [jnp.concatenate([_split_heads(v_cm[krows, :], is_a), ones_ext], axis=1) for _, krows, _ in ids]
                ols = [_dot(p, vb, NN) for p, vb in zip(ps, vbs)]
                for (qrows, _, _), ol, ma, mb in zip(ids, ols, mas, mbs):
                    o_out[qrows, :] = ol[:, :LANE]
                    l_out[qrows, :] = ol[:, LANE:]
                    m_out[qrows, :] = jnp.where(is_a, ma, mb)
                return c

            lax.fori_loop(0, nblk // ATT_GROUP_FWD, step, 0)

            if dil > 1:
                for r in range(dil):
                    def merge(j, c, dil=dil, r=r, ln=ln):
                        j0 = pl.multiple_of(j * PRO_ROWS, PRO_ROWS)
                        nat = _rows(r + dil * j0, PRO_ROWS, dil)
                        cm = pl.ds(r * ln + j0, PRO_ROWS)
                        m_old, m_new = m_acc[nat, :], m_d[cm, :]
                        m = jnp.maximum(m_old, m_new)
                        a_old, a_new = jnp.exp(m_old - m), jnp.exp(m_new - m)
                        o_ref[nat, :] = a_old * o_ref[nat, :] + a_new * o_d[cm, :]
                        l_acc[nat, :] = a_old * l_acc[nat, :] + a_new * l_d[cm, :]
                        m_acc[nat, :] = m
                        return c

                    lax.fori_loop(0, ln // PRO_ROWS, merge, 0)

        def epi(j, c):
            rows = pl.ds(pl.multiple_of(j * PRO_ROWS, PRO_ROWS), PRO_ROWS)
            l = l_acc[rows, :]
            o_ref[rows, :] = o_ref[rows, :] / l
            lse_ref[rows, :] = m_acc[rows, :] + jnp.log(l)
            return c

        lax.fori_loop(0, s // PRO_ROWS, epi, 0)

    f = jax.ShapeDtypeStruct((s, ATT_D), F32)
    scr = pltpu.VMEM((s, LANE), F32)
    scb = pltpu.VMEM((s, LANE), BF16)
    return _pcall(
        body, (proj, proj, proj, qw, kw), name="att_fwd", grid=(ATT_D // LANE,),
        in_specs=[blk(OFF_Q), blk(OFF_K), blk(OFF_V), wspec, wspec], out_specs=[oblk, oblk], out_shape=[f, f],
        scratch_shapes=[scr, scr, scb, scb, scb, scr, scr, scr, scr, scr, pltpu.VMEM((2, ATT_BLK, 2 * KEYS), F32)],
        sem=("parallel",), comm=comm)


def _att_bwd(proj, do, stats, qw, kw, comm=None):
    s = proj.shape[0]
    nblk = s // ATT_BLK
    blk = lambda off: pl.BlockSpec((s, LANE), lambda i: (0, off // LANE + i))
    wspec = pl.BlockSpec((1, LANE), lambda i: (0, i))
    oblk = pl.BlockSpec((s, LANE), lambda i: (0, i))

    def body(q_ref, k_ref, v_ref, do_ref, st_ref, qw_ref, kw_ref, dq_ref, dk_ref, dv_ref, dqw_ref, dkw_ref,
             qn, kn, q_cm, do_cm, k_cm, v_cm, st_cm, dq_acc, dk_acc, dv_acc, dq_d, dk_d, dv_d, bias):
        ones_bd = _head_mean_matrix().astype(BF16)
        is_a = lax.broadcasted_iota(jnp.int32, (1, LANE), 1) < HEAD_DIM
        _fill_bias(bias)
        zero = jnp.zeros((PRO_ROWS, LANE), F32)

        def pro(j, c):
            rows = pl.ds(pl.multiple_of(j * PRO_ROWS, PRO_ROWS), PRO_ROWS)
            qn[rows, :] = _head_norm(q_ref[rows, :], qw_ref[...], HEAD_DIM ** -0.5, ones_bd)
            kn[rows, :] = _head_norm(k_ref[rows, :], kw_ref[...], 1.0, ones_bd)
            dk_acc[rows, :] = zero
            dv_acc[rows, :] = zero
            return c

        lax.fori_loop(0, s // PRO_ROWS, pro, 0)

        for dil in DILATIONS:
            ln = s // dil
            nb = ln // ATT_BLK
            dq_o, dk_o, dv_o = (dq_acc, dk_acc, dv_acc) if dil == 1 else (dq_d, dk_d, dv_d)
            for r in range(dil):
                def relayout(j, c, dil=dil, r=r, ln=ln):
                    j0 = pl.multiple_of(j * PRO_ROWS, PRO_ROWS)
                    src = _rows(r + dil * j0, PRO_ROWS, dil)
                    dst = pl.ds(r * ln + j0, PRO_ROWS)
                    q_cm[dst, :] = qn[src, :].astype(BF16)
                    k_cm[dst, :] = kn[src, :].astype(BF16)
                    v_cm[dst, :] = v_ref[src, :].astype(BF16)
                    do_cm[dst, :] = do_ref[src, :].astype(BF16)
                    st_cm[dst, :] = st_ref[src, :]
                    if dil > 1:
                        dk_d[dst, :] = zero
                        dv_d[dst, :] = zero
                    return c

                lax.fori_loop(0, ln // PRO_ROWS, relayout, 0)

            def step(bg, c, nb=nb, dq_o=dq_o, dk_o=dk_o, dv_o=dv_o):
                ids = [_block_ids(bg * ATT_GROUP_BWD + u, nb) for u in range(ATT_GROUP_BWD)]
                qbs = [q_cm[qrows, :] for qrows, _, _ in ids]
                dobs = [do_cm[qrows, :] for qrows, _, _ in ids]
                kbs = [_split_heads(k_cm[krows, :], is_a) for _, krows, _ in ids]
                vbs = [_split_heads(v_cm[krows, :], is_a) for _, krows, _ in ids]
                sts = [st_cm[qrows, :] for qrows, _, _ in ids]
                scs = [_dot(qb, kb, NT) + bias[first] for qb, kb, (_, _, first) in zip(qbs, kbs, ids)]
                dps = [_dot(dob, vb, NT) for dob, vb in zip(dobs, vbs)]
                ps = [jnp.exp(sc - _pair(st[:, 0:1], st[:, HEAD_DIM:HEAD_DIM + 1])) for sc, st in zip(scs, sts)]
                dss = [(p * (dp - _pair(st[:, HALF:HALF + 1], st[:, HEAD_DIM + HALF:HEAD_DIM + HALF + 1]))).astype(BF16)
                       for p, dp, st in zip(ps, dps, sts)]
                dqs = [_dot(ds, kb, NN) for ds, kb in zip(dss, kbs)]
                dkfs = [_dot(ds, qb, TN) for ds, qb in zip(dss, qbs)]
                dvfs = [_dot(p.astype(BF16), dob, TN) for p, dob in zip(ps, dobs)]
                for (qrows, krows, _), dq, dkf, dvf in zip(ids, dqs, dkfs, dvfs):
                    dq_o[qrows, :] = dq
                    dk_o[krows, :] += jnp.where(is_a, dkf[:KEYS], dkf[KEYS:])
                    dv_o[krows, :] += jnp.where(is_a, dvf[:KEYS], dvf[KEYS:])
                return c

            lax.fori_loop(0, nblk // ATT_GROUP_BWD, step, 0)

            if dil > 1:
                for r in range(dil):
                    def merge(j, c, dil=dil, r=r, ln=ln):
                        j0 = pl.multiple_of(j * PRO_ROWS, PRO_ROWS)
                        nat = _rows(r + dil * j0, PRO_ROWS, dil)
                        cm = pl.ds(r * ln + j0, PRO_ROWS)
                        dq_acc[nat, :] += dq_d[cm, :]
                        dk_acc[nat, :] += dk_d[cm, :]
                        dv_acc[nat, :] += dv_d[cm, :]
                        return c

                    lax.fori_loop(0, ln // PRO_ROWS, merge, 0)

        def back(dn_out, x, w, scale):
            r = lax.rsqrt(_head_sum2(x * x, ones_bd) * (1.0 / HEAD_DIM) + EPS)
            nrm = x * r
            dw = jnp.sum(dn_out * nrm, axis=0, keepdims=True) * scale
            dn = dn_out * (w * scale)
            return r * (dn - nrm * (_head_sum2(dn * nrm, ones_bd) * (1.0 / HEAD_DIM))), dw

        def epi(j, c):
            rows = pl.ds(pl.multiple_of(j * PRO_ROWS, PRO_ROWS), PRO_ROWS)
            dq, dqw = back(dq_acc[rows, :], q_ref[rows, :], qw_ref[...], HEAD_DIM ** -0.5)
            dk, dkw = back(dk_acc[rows, :], k_ref[rows, :], kw_ref[...], 1.0)
            dq_ref[rows, :] = dq.astype(BF16)
            dk_ref[rows, :] = dk.astype(BF16)
            dv_ref[rows, :] = dv_acc[rows, :].astype(BF16)
            return (c[0] + dqw, c[1] + dkw)

        zrow = jnp.zeros((1, LANE), F32)
        dqw, dkw = lax.fori_loop(0, s // PRO_ROWS, epi, (zrow, zrow))
        dqw_ref[...] = dqw
        dkw_ref[...] = dkw

    o = jax.ShapeDtypeStruct((s, ATT_D), BF16)
    ov = jax.ShapeDtypeStruct((1, ATT_D), F32)
    scr = pltpu.VMEM((s, LANE), F32)
    scb = pltpu.VMEM((s, LANE), BF16)
    return _pcall(
        body, (proj, proj, proj, do, stats, qw, kw), name="att_bwd", grid=(ATT_D // LANE,),
        in_specs=[blk(OFF_Q), blk(OFF_K), blk(OFF_V), oblk, oblk, wspec, wspec],
        out_specs=[oblk, oblk, oblk, wspec, wspec], out_shape=[o, o, o, ov, ov],
        scratch_shapes=[scr, scr, scb, scb, scb, scb, scr, scr, scr, scr, scr, scr, scr, pltpu.VMEM((2, ATT_BLK, 2 * KEYS), F32)],
        sem=("parallel",), comm=comm)


def _att_norm_fwd(o, nw, ycat):
    s = o.shape[0]
    row = pl.BlockSpec((ROW_TILE, ATT_D), lambda i: (i, 0))
    vec = pl.BlockSpec((1, ATT_D), lambda i: (0, 0))

    def body(o_ref, nw_ref, ycat_ref, y_ref):
        o = o_ref[...]
        r = lax.rsqrt(jnp.mean(o * o, axis=-1, keepdims=True) + EPS)
        y_ref[...] = (o * r * nw_ref[...]).astype(BF16)

    return pl.pallas_call(body, name="att_norm_fwd", grid=(s // ROW_TILE,),
                          in_specs=[row, vec, pl.BlockSpec(memory_space=pl.ANY)],
                          out_specs=pl.BlockSpec((ROW_TILE, ATT_D), lambda i: (i, 1)),
                          out_shape=jax.ShapeDtypeStruct(ycat.shape, BF16), input_output_aliases={2: 0},
                          compiler_params=_cparams(("parallel",)))(o, nw, ycat)


def _att_norm_bwd(dycat, o, lse, nw):
    s = o.shape[0]
    row = pl.BlockSpec((ROW_TILE, ATT_D), lambda i: (i, 0))
    vec = pl.BlockSpec((1, ATT_D), lambda i: (0, 0))

    def body(dy_ref, o_ref, lse_ref, nw_ref, do_ref, st_ref, dnw_ref):
        @pl.when(pl.program_id(0) == 0)
        def _():
            dnw_ref[...] = jnp.zeros_like(dnw_ref)

        o = o_ref[...]
        dy = dy_ref[...]
        r = lax.rsqrt(jnp.mean(o * o, axis=-1, keepdims=True) + EPS)
        nrm = o * r
        dnw_ref[...] += jnp.sum(dy * nrm, axis=0, keepdims=True)
        dn = dy * nw_ref[...]
        do = r * (dn - nrm * jnp.mean(dn * nrm, axis=-1, keepdims=True))
        do_ref[...] = do
        ones_bd = _head_mean_matrix().astype(BF16)
        prod = do * o
        delta = jnp.concatenate([_head_sum2(prod[:, j * LANE:(j + 1) * LANE], ones_bd) for j in range(ATT_D // LANE)], axis=1)
        lane = lax.broadcasted_iota(jnp.int32, (1, ATT_D), 1)
        st_ref[...] = jnp.where((lane & (HEAD_DIM - 1)) < HALF, lse_ref[...], delta)

    f = jax.ShapeDtypeStruct((s, ATT_D), F32)
    return pl.pallas_call(
        body, name="att_norm_bwd", grid=(s // ROW_TILE,),
        in_specs=[pl.BlockSpec((ROW_TILE, ATT_D), lambda i: (i, 1)), row, row, vec], out_specs=[row, row, vec],
        out_shape=[f, f, jax.ShapeDtypeStruct((1, ATT_D), F32)],
        compiler_params=_cparams(("arbitrary",)))(dycat, o, lse, nw)


def _ada_fwd(c_all, w_ada):
    def body(c_ref, w_ref, o_ref):
        cv = c_ref[...]
        o_ref[...] = _dot((cv * _sigmoid(cv)).astype(BF16), w_ref[...].astype(BF16), NN)

    return pl.pallas_call(body, name="ada_fwd", out_shape=jax.ShapeDtypeStruct((c_all.shape[0], w_ada.shape[1]), F32),
                          compiler_params=_cparams())(c_all, w_ada)


def _adamw_math(g, w, m, v):
    m_new = ADAM_B1 * m + (1.0 - ADAM_B1) * g
    v_new = ADAM_B2 * v + (1.0 - ADAM_B2) * (g * g)
    m_hat = m_new / (1.0 - ADAM_B1 ** ADAM_STEP)
    v_hat = v_new / (1.0 - ADAM_B2 ** ADAM_STEP)
    delta = -ADAM_LR * (m_hat / (jnp.sqrt(v_hat) + ADAM_EPS) + ADAM_WD * w)
    return delta, m_new, v_new


def _ada_bwd_adamw(c_all, dmod_cols, w, m, v):
    rows, cols = w.shape
    tr = 256
    blk = pl.BlockSpec((tr, cols), lambda i: (i, 0))

    def body(c_ref, d_ref, w_ref, m_ref, v_ref, g_ref, dl_ref, mo_ref, vo_ref):
        cv = c_ref[...]
        ca = cv * _sigmoid(cv)
        g = ca[:, 0:1] * d_ref[0:1, :]
        for b in range(1, N_DEV):
            g = g + ca[:, b:b + 1] * d_ref[b:b + 1, :]
        g_ref[...] = g
        dl_ref[...], mo_ref[...], vo_ref[...] = _adamw_math(g, w_ref[...], m_ref[...], v_ref[...])

    o = jax.ShapeDtypeStruct((rows, cols), F32)
    return pl.pallas_call(
        body, name="ada_bwd_adamw", grid=(rows // tr,),
        in_specs=[pl.BlockSpec((tr, N_DEV), lambda i: (i, 0)), pl.BlockSpec((N_DEV, cols), lambda i: (0, 0)), blk, blk, blk],
        out_specs=[blk] * 4, out_shape=[o, o, o, o], compiler_params=_cparams(("parallel",)))(c_all.T, dmod_cols, w, m, v)


def _reduce_adamw(slabs, w, m, v, name):
    rows, cols = w.shape
    n_src = slabs.shape[0]
    if rows % 128 == 0:
        tr, steps = 128, rows // 128
        blk = pl.BlockSpec((tr, cols), lambda i: (i, 0))
        sblk = pl.BlockSpec((n_src, tr, cols), lambda i: (0, i, 0))
    else:
        tc, steps = 256, cols // 256
        blk = pl.BlockSpec((rows, tc), lambda i: (0, i))
        sblk = pl.BlockSpec((n_src, rows, tc), lambda i: (0, 0, i))

    def body(s_ref, w_ref, m_ref, v_ref, g_ref, dl_ref, mo_ref, vo_ref):
        g = s_ref[0].astype(F32)
        for src in range(1, n_src):
            g = g + s_ref[src].astype(F32)
        g_ref[...] = g
        dl_ref[...], mo_ref[...], vo_ref[...] = _adamw_math(g, w_ref[...], m_ref[...], v_ref[...])

    o = jax.ShapeDtypeStruct((rows, cols), F32)
    return pl.pallas_call(
        body, name=name, grid=(steps,), in_specs=[sblk, blk, blk, blk],
        out_specs=[blk] * 4, out_shape=[o, o, o, o], compiler_params=_cparams(("parallel",)))(slabs, w, m, v)


def _small_reduce_adamw(gathered, w, m, v):
    def body(s_ref, w_ref, m_ref, v_ref, g_ref, dl_ref, mo_ref, vo_ref):
        g = s_ref[0]
        for dev in range(1, N_DEV):
            g = g + s_ref[dev]
        g_ref[...] = g
        dl_ref[...], mo_ref[...], vo_ref[...] = _adamw_math(g, w_ref[...], m_ref[...], v_ref[...])

    o = jax.ShapeDtypeStruct(w.shape, F32)
    return pl.pallas_call(body, name="small_reduce_adamw", out_shape=[o, o, o, o], compiler_params=_cparams())(gathered, w, m, v)


def _adamw_small(g, w, m, v, name):
    def body(g_ref, w_ref, m_ref, v_ref, dl_ref, mo_ref, vo_ref):
        dl_ref[...], mo_ref[...], vo_ref[...] = _adamw_math(g_ref[...], w_ref[...], m_ref[...], v_ref[...])

    o = jax.ShapeDtypeStruct(w.shape, F32)
    return pl.pallas_call(body, name=name, out_shape=[o, o, o], compiler_params=_cparams())(g, w, m, v)


class _Exchange:
    def __init__(self, arrs, scatter):
        self.arrs, self.scatter, self.n = list(arrs), scatter, len(arrs)
        hbm = pl.BlockSpec(memory_space=pltpu.HBM)
        self.in_specs = [hbm] * self.n
        self.out_specs = [hbm] * self.n
        self.out_shape = [jax.ShapeDtypeStruct(a.shape if scatter else (N_DEV,) + a.shape, a.dtype) for a in self.arrs]
        self.scratch = [pltpu.SemaphoreType.DMA((self.n * (N_DEV - 1),)), pltpu.SemaphoreType.DMA((self.n * (N_DEV - 1),)),
                        pltpu.SemaphoreType.DMA((self.n,))]

    def _local(self, ins, outs, sems):
        me = 4 * lax.axis_index("x") + 2 * lax.axis_index("y") + lax.axis_index("c")
        return [pltpu.make_async_copy(ins[a].at[me] if self.scatter else ins[a], outs[a].at[me], sems[2].at[a])
                for a in range(self.n)]

    def _remote(self, ins, outs, sems, arriving):
        send_sems, recv_sems, _ = sems
        x, y, c = lax.axis_index("x"), lax.axis_index("y"), lax.axis_index("c")
        me = 4 * x + 2 * y + c
        remote = []
        for a in range(self.n):
            for k in range(1, N_DEV):
                px = 1 - x if k & 4 else x
                py = 1 - y if k & 2 else y
                pc = 1 - c if k & 1 else c
                peer = 4 * px + 2 * py + pc
                sem = a * (N_DEV - 1) + k - 1
                remote.append(pltpu.make_async_remote_copy(
                    src_ref=ins[a].at[peer] if self.scatter else ins[a], dst_ref=outs[a].at[peer if arriving else me],
                    send_sem=send_sems.at[sem], recv_sem=recv_sems.at[sem], device_id=(px, py, pc), device_id_type=MESH_IDS))
        return remote

    def start(self, ins, outs, sems):
        for cp in self._local(ins, outs, sems) + self._remote(ins, outs, sems, arriving=False):
            cp.start()

    def forward(self, ins, outs, sems):
        pass

    def wait(self, ins, outs, sems):
        for send, arrival in zip(self._remote(ins, outs, sems, arriving=False), self._remote(ins, outs, sems, arriving=True)):
            send.wait_send()
            arrival.wait_recv()
        for cp in self._local(ins, outs, sems):
            cp.wait()


N_CHIP = N_DEV // 2


class _SiblingSwap(_Exchange):
    def __init__(self, arrs):
        super().__init__(arrs, scatter=True)
        self.out_shape = [jax.ShapeDtypeStruct((N_CHIP,) + a.shape[2:], a.dtype) for a in self.arrs]
        self.scratch = [pltpu.SemaphoreType.DMA((self.n,)), pltpu.SemaphoreType.DMA((self.n,)), pltpu.SemaphoreType.DMA((1,))]

    def _copies(self, ins, outs, sems):
        x, y, c = lax.axis_index("x"), lax.axis_index("y"), lax.axis_index("c")
        return [pltpu.make_async_remote_copy(src_ref=ins[a].at[:, 1 - c], dst_ref=outs[a], send_sem=sems[0].at[a], recv_sem=sems[1].at[a],
                                             device_id=(x, y, 1 - c), device_id_type=MESH_IDS) for a in range(self.n)]

    def start(self, ins, outs, sems):
        for cp in self._copies(ins, outs, sems):
            cp.start()

    def wait(self, ins, outs, sems):
        for cp in self._copies(ins, outs, sems):
            cp.wait()


class _ChipScatter(_Exchange):
    def __init__(self, arrs):
        super().__init__(arrs, scatter=True)
        n_pairs = self.n * (N_CHIP - 1)
        self.scratch = [pltpu.SemaphoreType.DMA((n_pairs,)), pltpu.SemaphoreType.DMA((n_pairs,)), pltpu.SemaphoreType.DMA((self.n,))]

    def _local(self, ins, outs, sems):
        chip = 2 * lax.axis_index("x") + lax.axis_index("y")
        return [pltpu.make_async_copy(ins[a].at[chip], outs[a].at[chip], sems[2].at[a]) for a in range(self.n)]

    def _remote(self, ins, outs, sems, arriving):
        send_sems, recv_sems, _ = sems
        x, y, c = lax.axis_index("x"), lax.axis_index("y"), lax.axis_index("c")
        chip = 2 * x + y
        remote = []
        for a in range(self.n):
            for k in range(1, N_CHIP):
                px = 1 - x if k & 2 else x
                py = 1 - y if k & 1 else y
                peer = 2 * px + py
                sem = a * (N_CHIP - 1) + k - 1
                remote.append(pltpu.make_async_remote_copy(
                    src_ref=ins[a].at[peer], dst_ref=outs[a].at[peer if arriving else chip], send_sem=send_sems.at[sem],
                    recv_sem=recv_sems.at[sem], device_id=(px, py, c), device_id_type=MESH_IDS))
        return remote


def _chip_sum(mine, theirs):
    n, rows, cols = mine.shape
    blk = pl.BlockSpec((1, rows, 256), lambda q, j: (q, 0, j))

    def body(a_ref, b_ref, o_ref):
        o_ref[...] = (a_ref[...].astype(F32) + b_ref[...].astype(F32)).astype(BF16)

    return pl.pallas_call(body, name="chip_sum", grid=(n, cols // 256), in_specs=[blk, blk], out_specs=blk,
                          out_shape=jax.ShapeDtypeStruct(mine.shape, BF16),
                          compiler_params=_cparams(("parallel", "parallel")))(mine, theirs)


class _Gather2(_Exchange):
    def __init__(self, arrs):
        super().__init__(arrs, scatter=False)

    def _copies(self, ins, outs, sems):
        send_sems, recv_sems, _ = sems
        x, y, c = lax.axis_index("x"), lax.axis_index("y"), lax.axis_index("c")
        sibling = (x, y, 1 - c)
        chips = [(1 - x, y), (x, 1 - y), (1 - x, 1 - y)]
        first, passed, landed = [], [], []
        for a in range(self.n):
            def copy(k, block, to, src=None, a=a):
                slab = outs[a].at[4 * block[0] + 2 * block[1] + block[2]]
                return pltpu.make_async_remote_copy(
                    src_ref=slab if src is None else src, dst_ref=slab, send_sem=send_sems.at[a * (N_DEV - 1) + k],
                    recv_sem=recv_sems.at[a * (N_DEV - 1) + k], device_id=to, device_id_type=MESH_IDS)

            first.append(copy(0, (x, y, c), sibling, src=ins[a]))
            landed.append(copy(0, sibling, sibling))
            for j, chip in enumerate(chips):
                first.append(copy(1 + j, (x, y, c), (*chip, c), src=ins[a]))
                passed.append((copy(1 + j, (*chip, c), sibling), copy(4 + j, (*chip, c), sibling)))
                landed.append(copy(4 + j, (*chip, 1 - c), sibling))
        return first, passed, landed

    def start(self, ins, outs, sems):
        for cp in self._local(ins, outs, sems) + self._copies(ins, outs, sems)[0]:
            cp.start()

    def forward(self, ins, outs, sems):
        for arrival, onward in self._copies(ins, outs, sems)[1]:
            arrival.wait_recv()
            onward.start()

    def wait(self, ins, outs, sems):
        first, passed, landed = self._copies(ins, outs, sems)
        for arrival in landed:
            arrival.wait_recv()
        for cp in first + [onward for _, onward in passed]:
            cp.wait_send()
        for cp in self._local(ins, outs, sems):
            cp.wait()


def _split_comm_refs(refs, n_in, n_out, n_scr, comm):
    nc = comm.n if comm is not None else 0
    ns = 3 if comm is not None else 0
    pos, groups = 0, []
    for cnt in (n_in, nc, n_out, nc, n_scr, ns):
        groups.append(refs[pos:pos + cnt])
        pos += cnt
    assert pos == len(refs), (pos, len(refs))
    return groups


def _pcall(body, args, *, name, grid, in_specs, out_specs, out_shape, scratch_shapes=(), sem=None, comm=None):
    in_specs, out_specs, out_shape, scratch_shapes = list(in_specs), list(out_specs), list(out_shape), list(scratch_shapes)
    n_in, n_out, n_scr = len(in_specs), len(out_specs), len(scratch_shapes)
    if comm is None:
        kernel_body = body
    else:
        def kernel_body(*refs):
            ins, cins, outs, couts, scr, sems = _split_comm_refs(refs, n_in, n_out, n_scr, comm)
            ids = [pl.program_id(a) for a in range(len(grid))]
            first, last = ids[0] == 0, ids[0] == grid[0] - 1
            for a in range(1, len(grid)):
                first, last = first & (ids[a] == 0), last & (ids[a] == grid[a] - 1)

            middle = ids[0] == (2 * grid[0]) // 3
            for a in range(1, len(grid)):
                middle = middle & (ids[a] == 0)

            @pl.when(first)
            def _():
                comm.start(cins, couts, sems)

            @pl.when(middle)
            def _():
                comm.forward(cins, couts, sems)

            body(*ins, *outs, *scr)

            @pl.when(last)
            def _():
                comm.wait(cins, couts, sems)

        in_specs, out_specs, out_shape = in_specs + comm.in_specs, out_specs + comm.out_specs, out_shape + comm.out_shape
        scratch_shapes, args = scratch_shapes + comm.scratch, list(args) + comm.arrs
        sem = ("arbitrary",) * len(grid)
    res = pl.pallas_call(kernel_body, name=name, grid=grid, in_specs=in_specs, out_specs=out_specs, out_shape=out_shape,
                         scratch_shapes=scratch_shapes, compiler_params=_cparams(sem))(*args)
    return res[:n_out], res[n_out:]


def _exchange(arrs, name, scatter=False, ex=None):
    if ex is None:
        ex = _Exchange(arrs, scatter=True) if scatter else _Gather2(arrs)

    def body(*refs):
        _, ins, _, outs, _, sems = _split_comm_refs(refs, 0, 0, 0, ex)
        ex.start(ins, outs, sems)
        ex.forward(ins, outs, sems)
        ex.wait(ins, outs, sems)

    return pl.pallas_call(body, name=name, in_specs=ex.in_specs, out_specs=ex.out_specs, out_shape=ex.out_shape,
                          scratch_shapes=ex.scratch)(*ex.arrs)


def _pad_lanes(v, width=LANE):
    return jnp.pad(v, ((0, 0), (0, width - v.shape[1])))


def _shards_to_cols(g):
    return jnp.transpose(g, (1, 0, 2)).reshape(g.shape[1], N_DEV * g.shape[2])


def _cols_to_shards(w):
    return w.astype(BF16).reshape(w.shape[0], N_DEV, w.shape[1] // N_DEV).transpose(1, 0, 2)


def _local_step(x, tgt, mod, w_in_pt, conv_w, conv_b, dt_bias, a_log, d_skip, ssd_norm_w, q_norm_w, k_norm_w,
                attn_norm_w, w_out_sh, w_ff1_sh, w_ff2_sh, norm1_w, norm2_w, core):
    shift1, scale1, gate1, shift2, scale2, gate2 = [mod[i:i + 1] for i in range(N_MOD)]
    dtb, alog, dsk = _pad_lanes(dt_bias), _pad_lanes(a_log), _pad_lanes(d_skip)
    qw, kw = jnp.tile(q_norm_w, (1, ATT_HEADS)), jnp.tile(k_norm_w, (1, ATT_HEADS))

    h1 = _norm_mod_fwd(x, norm1_w, scale1, shift1, "norm1_fwd")
    proj = _matmul(h1, w_in_pt, tb=True, tm=2048, tn=896, tk=1024, name="in_proj")
    pre, act = _conv_fwd(proj, conv_w, conv_b)
    ypre, ycat_ssd, hall = _ssd_fwd(proj, act, dtb, alog, dsk, ssd_norm_w)
    (o_att, lse), (w_out_g, w_ff1_g, w_ff2_g) = _att_fwd(proj, qw, kw, comm=_Gather2([w_out_sh, w_ff1_sh, w_ff2_sh]))
    w_out = w_out_g.reshape(2 * D_MODEL, D_MODEL)
    w_ff1 = _shards_to_cols(w_ff1_g)
    w_ff2 = w_ff2_g.reshape(D_FF, D_MODEL)
    ycat = _att_norm_fwd(o_att, attn_norm_w, ycat_ssd)
    mix = _matmul(ycat, w_out, tm=1024, tn=1024, tk=2048, name="out_proj")
    x1, h2 = _norm_mod_fwd(x, norm2_w, scale2, shift2, "norm2_fwd", res=mix, gate=gate1)
    u, act_ff = _matmul(h2, w_ff1, tm=1024, tn=1024, tk=1024, name="ff1", mode="relu2")
    ff = _matmul(act_ff, w_ff2, tm=512, tn=1024, tk=4096, name="ff2")
    loss, dout, dff, dgate2 = _loss_head(x1, ff, gate2, tgt)

    du = _matmul(dff, w_ff2, tb=True, tm=1024, tn=1024, tk=1024, out_dtype=BF16, name="ff2_dx", mode="drelu2", u=u)
    g_ff2 = _matmul(act_ff, dff, ta=True, tm=512, tn=1024, tk=4096, out_dtype=BF16, name="ff2_dw")
    dh2 = _matmul(du, w_ff1, tb=True, tm=512, tn=1024, tk=4096, name="ff1_dx")
    g_ff1 = _matmul(h2, du, ta=True, tm=512, tn=1024, tk=4096, out_dtype=BF16, name="ff1_dw")
    dx1, dshift2, dscale2, g_norm2, dmix, dgate1 = _norm_mod_bwd(dh2, x1, dout, norm2_w, scale2, "norm2_bwd", gate=gate1, mix=mix)

    dycat = _matmul(dmix, w_out, tb=True, tm=1024, tn=1024, tk=1024, name="out_proj_dx")
    g_out = _matmul(ycat, dmix, ta=True, tm=512, tn=1024, tk=4096, out_dtype=BF16, name="out_proj_dw")
    do, stats, g_attn_norm = _att_norm_bwd(dycat, o_att, lse, attn_norm_w)
    ff_slabs = [_cols_to_shards(g_ff1), g_ff2.astype(BF16).reshape(N_DEV, D_FF // N_DEV, D_MODEL)]
    (dq, dk, dv, dqw, dkw), (s_ff1, s_ff2) = _att_bwd(proj, do, stats, qw, kw, comm=_Exchange(ff_slabs, scatter=True))
    out_slabs = [g_out.astype(BF16).reshape(N_DEV, 2 * D_MODEL // N_DEV, D_MODEL)]
    (dz, dact, ddtr, da, g_dsk, g_dtb, g_ssd_norm), (s_out,) = _ssd_bwd(
        dycat, ypre, proj, act, hall, dtb, alog, dsk, ssd_norm_w, comm=_Exchange(out_slabs, scatter=True))
    dxbc, g_conv_w, g_conv_b = _conv_bwd(dact, pre, proj, conv_w)
    dproj = jnp.concatenate([dz, dxbc, dq, dk, dv, ddtr], axis=1)
    g_in_pt = _matmul(dproj, h1, ta=True, tm=896, tn=1024, tk=4096, out_dtype=BF16, name="in_proj_dw")
    in_slabs = _unpack_w_in_rows(g_in_pt).reshape(N_CHIP, 2, IN_W // N_DEV, D_MODEL)
    (sibling_slabs,) = _exchange(None, "swap_w_in_grads", ex=_SiblingSwap([in_slabs]))
    chip_slabs = _chip_sum(lax.dynamic_index_in_dim(in_slabs, core, axis=1, keepdims=False), sibling_slabs)
    dh1, (s_in,) = _matmul(dproj, w_in_pt, tm=512, tn=1024, tk=IN_WP, name="in_proj_dx", comm=_ChipScatter([chip_slabs]))
    grad_x, dshift1, dscale1, g_norm1 = _norm_mod_bwd(dh1, x, dx1, norm1_w, scale1, "norm1_bwd")

    dmod = jnp.concatenate([dshift1, dscale1, dgate1, dshift2, dscale2, dgate2], axis=0)
    g_alog = da[:, :SSD_HEADS] * (-jnp.exp(a_log))
    g_qw = dqw.reshape(ATT_HEADS, HEAD_DIM).sum(axis=0, keepdims=True)
    g_kw = dkw.reshape(ATT_HEADS, HEAD_DIM).sum(axis=0, keepdims=True)
    return dict(loss=loss, grad_x=grad_x, dmod=dmod, norm1_w=g_norm1, norm2_w=g_norm2, w_in=s_in, conv_w=g_conv_w,
                conv_b=g_conv_b, dt_bias=g_dtb[:, :SSD_HEADS], a_log=g_alog, d_skip=g_dsk[:, :SSD_HEADS],
                ssd_norm_w=g_ssd_norm, q_norm_w=g_qw, k_norm_w=g_kw, attn_norm_w=g_attn_norm, w_out=s_out,
                w_ff1=s_ff1, w_ff2=s_ff2)


def _pack_w_in_rows(wt_full):
    o_dt = SSD_D_INNER + CONV_CH
    o_q = o_dt + SSD_HEADS
    pad = jnp.zeros((LANE - SSD_HEADS, wt_full.shape[1]), wt_full.dtype)
    return jnp.concatenate([wt_full[:o_dt], wt_full[o_q:], wt_full[o_dt:o_q], pad], axis=0)


def _unpack_w_in_rows(gt_p):
    return jnp.concatenate([gt_p[:OFF_Q], gt_p[OFF_DT:OFF_DT + SSD_HEADS], gt_p[OFF_Q:OFF_DT]], axis=0)


MISC_FIELDS = (("dt_bias", SSD_HEADS), ("a_log", SSD_HEADS), ("d_skip", SSD_HEADS), ("q_norm_w", HEAD_DIM), ("k_norm_w", HEAD_DIM))
SMALL_LAYOUT = (("b_ada", 6), ("norm1_w", 1), ("norm2_w", 1), ("conv_w", 8), ("conv_b", 2), ("ssd_norm_w", 1),
                ("attn_norm_w", 1), ("misc", 1))


def _pack_small(vals):
    rows = []
    for name, nrow in SMALL_LAYOUT:
        if name == "misc":
            misc = jnp.concatenate([vals[f].reshape(1, n) for f, n in MISC_FIELDS], axis=1)
            rows.append(_pad_lanes(misc, D_MODEL))
        elif name in vals:
            rows.append(vals[name].reshape(nrow, D_MODEL))
        else:
            rows.append(jnp.zeros((nrow, D_MODEL), F32))
    used = sum(n for _, n in SMALL_LAYOUT)
    rows.append(jnp.zeros((SMALL_ROWS - used, D_MODEL), F32))
    return jnp.concatenate(rows, axis=0)


def _unpack_small(packed):
    out, r = {}, 0
    for name, nrow in SMALL_LAYOUT:
        blk = packed[r:r + nrow]
        r += nrow
        if name == "misc":
            c0 = 0
            for f, n in MISC_FIELDS:
                out[f] = blk[:, c0:c0 + n]
                c0 += n
        elif name == "b_ada":
            out[name] = blk.reshape(1, N_MOD * D_MODEL)
        elif name == "conv_w":
            out[name] = blk.reshape(CONV_K, CONV_CH)
        elif name == "conv_b":
            out[name] = blk.reshape(1, CONV_CH)
        else:
            out[name] = blk
    return out


WEIGHT_NAMES = ("norm1_w", "norm2_w", "w_ada", "b_ada", "w_in", "conv_w", "conv_b", "dt_bias", "a_log", "d_skip",
                "ssd_norm_w", "q_norm_w", "k_norm_w", "attn_norm_w", "w_out", "w_ff1", "w_ff2")
SMALL_NAMES = ("norm1_w", "norm2_w", "b_ada", "conv_b", "dt_bias", "a_log", "d_skip", "ssd_norm_w", "q_norm_w",
               "k_norm_w", "attn_norm_w")


def kernel(x, c, norm1_w, norm2_w, w_ada, b_ada, w_in, conv_w, conv_b, dt_bias, a_log, d_skip, ssd_norm_w, q_norm_w, k_norm_w, attn_norm_w, w_out, w_ff1, w_ff2, loss_target, m_norm1_w, m_norm2_w, m_w_ada, m_b_ada, m_w_in, m_conv_w, m_conv_b, m_dt_bias, m_a_log, m_d_skip, m_ssd_norm_w, m_q_norm_w, m_k_norm_w, m_attn_norm_w, m_w_out, m_w_ff1, m_w_ff2, v_norm1_w, v_norm2_w, v_w_ada, v_b_ada, v_w_in, v_conv_w, v_conv_b, v_dt_bias, v_a_log, v_d_skip, v_ssd_norm_w, v_q_norm_w, v_k_norm_w, v_attn_norm_w, v_w_out, v_w_ff1, v_w_ff2):
    args = dict(locals())
    w = {n: args[n] for n in WEIGHT_NAMES}
    m = {n: args["m_" + n] for n in WEIGHT_NAMES}
    v = {n: args["v_" + n] for n in WEIGHT_NAMES}
    me = 4 * lax.axis_index("x") + 2 * lax.axis_index("y") + lax.axis_index("c")

    c_rows = jnp.pad(c, ((0, 7), (0, 0)))
    w_in_t, m_in_t, v_in_t = [jnp.transpose(t["w_in"][0]) for t in (w, m, v)]
    c_g, conv_g, w_in_g = _exchange([c_rows, w["conv_w"][0], w_in_t.astype(BF16)], "gather_w_in", scatter=False)
    c_all = c_g[:, 0, :]
    conv_full = _shards_to_cols(conv_g)
    w_in_pt = _pack_w_in_rows(w_in_g.reshape(IN_W, D_MODEL))

    mod_part = _ada_fwd(c_all, w["w_ada"][0])
    (mod_g,) = _exchange([mod_part], "gather_mod", scatter=False)
    mod_mine = lax.dynamic_index_in_dim(mod_g, me, axis=1, keepdims=False).reshape(1, N_MOD * D_MODEL) + w["b_ada"]
    mod = mod_mine.reshape(N_MOD, D_MODEL)

    res = _local_step(x[0], loss_target[0], mod, w_in_pt, conv_full, w["conv_b"], w["dt_bias"], w["a_log"], w["d_skip"],
                      w["ssd_norm_w"], w["q_norm_w"], w["k_norm_w"], w["attn_norm_w"], w["w_out"][0].astype(BF16),
                      w["w_ff1"][0].astype(BF16), w["w_ff2"][0].astype(BF16), w["norm1_w"], w["norm2_w"], lax.axis_index("c"))

    small_vals = {n: res[n] for n in SMALL_NAMES if n != "b_ada"}
    small_vals["b_ada"] = res["dmod"]
    small_vals["conv_w"] = res["conv_w"]
    (small_g,) = _exchange([_pack_small(small_vals)], "gather_small", scatter=False)

    grads, delta, new_m, new_v = {}, {}, {}, {}
    for name in ("w_out", "w_ff1", "w_ff2"):
        outs = _reduce_adamw(res[name], w[name][0], m[name][0], v[name][0], "adamw_" + name)
        grads[name], delta[name], new_m[name], new_v[name] = [o[None] for o in outs]
    outs = _reduce_adamw(res["w_in"], w_in_t, m_in_t, v_in_t, "adamw_w_in")
    grads["w_in"], delta["w_in"], new_m["w_in"], new_v["w_in"] = [jnp.transpose(o)[None] for o in outs]

    sm = _small_reduce_adamw(small_g, _pack_small({n: w[n] for n in SMALL_NAMES}), _pack_small({n: m[n] for n in SMALL_NAMES}),
                             _pack_small({n: v[n] for n in SMALL_NAMES}))
    sm = [_unpack_small(p) for p in sm]
    for n in SMALL_NAMES:
        grads[n], delta[n], new_m[n], new_v[n] = [p[n] for p in sm]
    shard_w = CONV_CH // N_DEV
    g_conv = lax.dynamic_slice_in_dim(sm[0]["conv_w"], me * shard_w, shard_w, axis=1)
    cw = _adamw_small(g_conv, w["conv_w"][0], m["conv_w"][0], v["conv_w"][0], "adamw_conv_w")
    grads["conv_w"] = g_conv[None]
    delta["conv_w"], new_m["conv_w"], new_v["conv_w"] = [o[None] for o in cw]

    ada_w = w_ada.shape[2]
    dmod_all = small_g[:, :N_MOD, :].reshape(N_DEV, N_MOD * D_MODEL)
    dmod_cols = lax.dynamic_slice_in_dim(dmod_all, me * ada_w, ada_w, axis=1)
    outs = _ada_bwd_adamw(c_all, dmod_cols, w["w_ada"][0], m["w_ada"][0], v["w_ada"][0])
    grads["w_ada"], delta["w_ada"], new_m["w_ada"], new_v["w_ada"] = [o[None] for o in outs]

    loss = lax.psum(res["loss"][0, 0], ("x", "y", "c"))
    return (loss, res["grad_x"][None], *[grads[n] for n in WEIGHT_NAMES], *[delta[n] for n in WEIGHT_NAMES],
            *[new_m[n] for n in WEIGHT_NAMES], *[new_v[n] for n in WEIGHT_NAMES])
```

```python
import functools

import jax
import jax.numpy as jnp
from jax import lax
from jax.experimental import pallas as pl
from jax.experimental.pallas import tpu as pltpu

F32 = jnp.float32
BF16 = jnp.bfloat16
HIGHEST = lax.Precision.HIGHEST
MESH_IDS = pl.DeviceIdType.MESH

N_DEV = 8
D_MODEL = 1024
HEAD_DIM = 64
SSD_HEADS = 16
SSD_GROUPS = 4
HEADS_PER_GROUP = SSD_HEADS // SSD_GROUPS
SSD_STATE = 128
SSD_CHUNK = 128
SSD_D_INNER = SSD_HEADS * HEAD_DIM
GROUP_WIDTH = SSD_D_INNER // SSD_GROUPS
CONV_K = 4
CONV_CH = SSD_D_INNER + 2 * SSD_GROUPS * SSD_STATE
ATT_HEADS = 16
ATT_D = ATT_HEADS * HEAD_DIM
ATT_BLK = 128
DILATIONS = (1, 4, 16)
D_FF = 4 * D_MODEL
N_MOD = 6
EPS = 1e-6
IN_W = SSD_D_INNER + CONV_CH + SSD_HEADS + 3 * ATT_D
LANE = 128
OFF_Z, OFF_XBC, OFF_Q, OFF_K, OFF_V, OFF_DT = 0, 1024, 3072, 4096, 5120, 6144
IN_WP = OFF_DT + LANE

ADAM_LR, ADAM_B1, ADAM_B2, ADAM_EPS, ADAM_WD, ADAM_STEP = 0.001, 0.9, 0.999, 1e-08, 0.01, 10
VMEM_LIMIT = 56 * 1024 * 1024
ROW_TILE = 512
SMALL_ROWS = 24


def _cparams(sem=None):
    return pltpu.CompilerParams(dimension_semantics=sem, vmem_limit_bytes=VMEM_LIMIT)


def _sigmoid(v):
    return 1.0 / (1.0 + jnp.exp(-v))


def _softplus(v):
    y = jnp.exp(-jnp.abs(v))
    small = y * (1.0 - y * (0.5 - y * (1.0 / 3.0)))
    return jnp.maximum(v, 0.0) + jnp.where(y < 0.01, small, jnp.log(1.0 + y))


def _dot(a, b, dims, precision=None):
    return lax.dot_general(a, b, (dims, ((), ())), preferred_element_type=F32, precision=precision)


NN = ((1,), (0,))
NT = ((1,), (1,))
TN = ((0,), (0,))


def _matmul(a, b, *, ta=False, tb=False, tm, tn, tk, out_dtype=F32, name, mode=None, u=None, comm=None):
    m, k = (a.shape[1], a.shape[0]) if ta else a.shape
    n = b.shape[0] if tb else b.shape[1]
    assert m % tm == 0 and n % tn == 0 and k % tk == 0, (name, m, n, k)
    nk = k // tk
    a_spec = pl.BlockSpec((tk, tm), lambda i, j, kk: (kk, i)) if ta else pl.BlockSpec((tm, tk), lambda i, j, kk: (i, kk))
    b_spec = pl.BlockSpec((tn, tk), lambda i, j, kk: (j, kk)) if tb else pl.BlockSpec((tk, tn), lambda i, j, kk: (kk, j))
    o_spec = pl.BlockSpec((tm, tn), lambda i, j, kk: (i, j))
    dims = ((0,) if ta else (1,), (1,) if tb else (0,))
    n_out = 2 if mode == "relu2" else 1

    def body(*refs):
        if mode == "drelu2":
            a_ref, b_ref, u_ref = refs[:3]
            rest = refs[3:]
        else:
            a_ref, b_ref = refs[:2]
            u_ref = None
            rest = refs[2:]
        outs = rest[:n_out]
        part = _dot(a_ref[...], b_ref[...], dims)

        def finish(r):
            if mode == "relu2":
                outs[0][...] = r.astype(BF16)
                rr = jnp.maximum(r, 0.0)
                outs[1][...] = (rr * rr).astype(BF16)
            elif mode == "drelu2":
                outs[0][...] = (r * (2.0 * jnp.maximum(u_ref[...].astype(F32), 0.0))).astype(out_dtype)
            else:
                outs[0][...] = r.astype(out_dtype)

        if nk == 1:
            finish(part)
        else:
            acc = rest[n_out]
            kk = pl.program_id(2)

            @pl.when(kk == 0)
            def _():
                acc[...] = part

            @pl.when(kk > 0)
            def _():
                acc[...] += part

            @pl.when(kk == nk - 1)
            def _():
                finish(acc[...])

    in_specs = [a_spec, b_spec]
    args = [a, b]
    if mode == "drelu2":
        in_specs.append(o_spec)
        args.append(u)
    if mode == "relu2":
        out_shape = [jax.ShapeDtypeStruct((m, n), BF16), jax.ShapeDtypeStruct((m, n), BF16)]
    else:
        out_shape = [jax.ShapeDtypeStruct((m, n), out_dtype)]
    outs, comm_outs = _pcall(
        body, args, name=name, grid=(m // tm, n // tn, nk), in_specs=in_specs, out_specs=[o_spec] * n_out,
        out_shape=out_shape, scratch_shapes=[pltpu.VMEM((tm, tn), F32)] if nk > 1 else [],
        sem=("parallel", "parallel", "arbitrary"), comm=comm)
    res = tuple(outs) if mode == "relu2" else outs[0]
    return res if comm is None else (res, comm_outs)


def _rms_mod(xv, nw, scale, shift):
    r = lax.rsqrt(jnp.mean(xv * xv, axis=-1, keepdims=True) + EPS)
    return ((xv * r) * nw * (1.0 + scale) + shift).astype(BF16)


def _norm_mod_fwd(x, nw, scale, shift, name):
    s, d = x.shape
    row = pl.BlockSpec((ROW_TILE, d), lambda i: (i, 0))
    vec = pl.BlockSpec((1, d), lambda i: (0, 0))

    def body(x_ref, nw_ref, sc_ref, sh_ref, h_ref):
        h_ref[...] = _rms_mod(x_ref[...], nw_ref[...], sc_ref[...], sh_ref[...])

    return pl.pallas_call(body, name=name, grid=(s // ROW_TILE,), in_specs=[row, vec, vec, vec], out_specs=row,
                          out_shape=jax.ShapeDtypeStruct((s, d), BF16), compiler_params=_cparams(("parallel",)))(x, nw, scale, shift)


def _matmul_rows(a, b, epilogue, row_in, vec_in, outs, *, tb=False, tm, name, comm=None):
    m, k = a.shape
    n = b.shape[0] if tb else b.shape[1]
    assert m % tm == 0, (name, m, tm)
    dims = ((1,), (1,) if tb else (0,))
    row = pl.BlockSpec((tm, n), lambda i: (i, 0))
    vec = pl.BlockSpec((1, n), lambda i: (0, 0))
    one = pl.BlockSpec((1, 1), lambda i: (0, 0))
    n_row, n_vec = len(row_in), len(vec_in)

    def body(a_ref, b_ref, *rest):
        epilogue(_dot(a_ref[...], b_ref[...], dims), pl.program_id(0) == 0, rest[:n_row], rest[n_row:n_row + n_vec],
                 rest[n_row + n_vec:])

    specs = {"row": row, "vec": vec, "one": one}
    shapes = {"row": (m, n), "vec": (1, n), "one": (1, 1)}
    res, comm_outs = _pcall(
        body, [a, b, *row_in, *vec_in], name=name, grid=(m // tm,),
        in_specs=[pl.BlockSpec((tm, k), lambda i: (i, 0)), pl.BlockSpec(b.shape, lambda i: (0, 0))] + [row] * n_row + [vec] * n_vec,
        out_specs=[specs[kind] for kind, _ in outs], out_shape=[jax.ShapeDtypeStruct(shapes[kind], dt) for kind, dt in outs],
        sem=("arbitrary",), comm=comm)
    return res if comm is None else (res, comm_outs)


def _residual_norm_epilogue(mix, first, rows, vecs, outs):
    (x_ref,), (gate_ref, nw_ref, sc_ref, sh_ref), (mix_ref, x1_ref, h_ref) = rows, vecs, outs
    xv = x_ref[...] + gate_ref[...] * mix
    mix_ref[...] = mix
    x1_ref[...] = xv
    h_ref[...] = _rms_mod(xv, nw_ref[...], sc_ref[...], sh_ref[...])


def _loss_epilogue(ff, first, rows, vecs, outs):
    (x1_ref, t_ref), (g_ref,), (loss_ref, dout_ref, dff_ref, dg_ref) = rows, vecs, outs
    d = ff.shape[1]

    @pl.when(first)
    def _():
        loss_ref[...] = jnp.zeros_like(loss_ref)
        dg_ref[...] = jnp.zeros_like(dg_ref)

    err = x1_ref[...] + g_ref[...] * ff - t_ref[...]
    loss_ref[...] += (0.5 / d) * jnp.sum(err * err).reshape(1, 1)
    dout = err * (1.0 / d)
    dout_ref[...] = dout
    dff_ref[...] = (g_ref[...] * dout).astype(BF16)
    dg_ref[...] += jnp.sum(dout * ff, axis=0, keepdims=True)


def _norm_bwd_epilogue(dh, first, rows, vecs, outs):
    with_gate = len(vecs) == 3
    x_ref, dres_ref = rows[:2]
    nw_ref, sc_ref = vecs[:2]
    dx_ref, dsh_ref, dsc_ref, dnw_ref = outs[:4]

    @pl.when(first)
    def _():
        for ref in outs[1:4] + outs[5:]:
            ref[...] = jnp.zeros_like(ref)

    xv = x_ref[...]
    r = lax.rsqrt(jnp.mean(xv * xv, axis=-1, keepdims=True) + EPS)
    nrm = xv * r
    one_sc = 1.0 + sc_ref[...]
    dhn = dh * nrm
    dsh_ref[...] += jnp.sum(dh, axis=0, keepdims=True)
    dsc_ref[...] += jnp.sum(dhn, axis=0, keepdims=True) * nw_ref[...]
    dnw_ref[...] += jnp.sum(dhn, axis=0, keepdims=True) * one_sc
    dn = dh * (nw_ref[...] * one_sc)
    dx = dres_ref[...] + r * (dn - nrm * jnp.mean(dn * nrm, axis=-1, keepdims=True))
    dx_ref[...] = dx
    if with_gate:
        outs[4][...] = (vecs[2][...] * dx).astype(BF16)
        outs[5][...] += jnp.sum(dx * rows[2][...], axis=0, keepdims=True)


CONV_COLS = 256
HALO = 8


def _shift_down(cur, halo, k):
    if k == 0:
        return cur
    rolled = pltpu.roll(cur, k, axis=0)
    top = jnp.where(lax.broadcasted_iota(jnp.int32, halo.shape, 0) < k, pltpu.roll(halo, k, axis=0), rolled[:HALO])
    return jnp.concatenate([top, rolled[HALO:]], axis=0)


def _shift_up(cur, halo, k):
    if k == 0:
        return cur
    t = cur.shape[0]
    rolled = pltpu.roll(cur, t - k, axis=0)
    bot = jnp.where(lax.broadcasted_iota(jnp.int32, halo.shape, 0) >= HALO - k, pltpu.roll(halo, HALO - k, axis=0),
                    rolled[t - HALO:])
    return jnp.concatenate([rolled[:t - HALO], bot], axis=0)


def _conv_fwd(proj, conv_w, conv_b):
    s = proj.shape[0]
    nr = s // ROW_TILE
    cb0 = OFF_XBC // CONV_COLS
    hb = ROW_TILE // HALO
    cur = pl.BlockSpec((ROW_TILE, CONV_COLS), lambda j, r: (r, cb0 + j))
    prev = pl.BlockSpec((HALO, CONV_COLS), lambda j, r: (jnp.maximum(r * hb - 1, 0), cb0 + j))
    out = pl.BlockSpec((ROW_TILE, CONV_COLS), lambda j, r: (r, j))

    def body(u_ref, up_ref, w_ref, b_ref, pre_ref, act_ref):
        r = pl.program_id(1)
        u = u_ref[...]
        halo = jnp.where(r > 0, up_ref[...], 0.0)
        acc = b_ref[...] + w_ref[CONV_K - 1:CONV_K, :] * u
        for k in range(1, CONV_K):
            acc = acc + w_ref[CONV_K - 1 - k:CONV_K - k, :] * _shift_down(u, halo, k)
        pre_ref[...] = acc
        act_ref[...] = acc * _sigmoid(acc)

    return pl.pallas_call(
        body, name="conv_fwd", grid=(CONV_CH // CONV_COLS, nr),
        in_specs=[cur, prev, pl.BlockSpec((CONV_K, CONV_COLS), lambda j, r: (0, j)),
                  pl.BlockSpec((1, CONV_COLS), lambda j, r: (0, j))],
        out_specs=[out, out],
        out_shape=[jax.ShapeDtypeStruct((s, CONV_CH), F32), jax.ShapeDtypeStruct((s, CONV_CH), F32)],
        compiler_params=_cparams(("parallel", "arbitrary")))(proj, proj, conv_w, conv_b)


def _conv_bwd(dact, pre, proj, conv_w):
    s = proj.shape[0]
    nr = s // ROW_TILE
    cb0 = OFF_XBC // CONV_COLS
    hb = ROW_TILE // HALO
    last_halo = s // HALO - 1
    cur = pl.BlockSpec((ROW_TILE, CONV_COLS), lambda j, r: (r, j))
    nxt = pl.BlockSpec((HALO, CONV_COLS), lambda j, r: (jnp.minimum((r + 1) * hb, last_halo), j))
    ucur = pl.BlockSpec((ROW_TILE, CONV_COLS), lambda j, r: (r, cb0 + j))
    uprev = pl.BlockSpec((HALO, CONV_COLS), lambda j, r: (jnp.maximum(r * hb - 1, 0), cb0 + j))
    wspec = pl.BlockSpec((CONV_K, CONV_COLS), lambda j, r: (0, j))
    bspec = pl.BlockSpec((1, CONV_COLS), lambda j, r: (0, j))

    def dsilu(p):
        sg = _sigmoid(p)
        return sg * (1.0 + p * (1.0 - sg))

    def body(da_ref, dan_ref, pre_ref, pren_ref, u_ref, up_ref, w_ref, du_ref, dw_ref, db_ref):
        r = pl.program_id(1)

        @pl.when(r == 0)
        def _():
            dw_ref[...] = jnp.zeros_like(dw_ref)
            db_ref[...] = jnp.zeros_like(db_ref)

        dpre = da_ref[...] * dsilu(pre_ref[...])
        dnext = jnp.where(r < nr - 1, dan_ref[...] * dsilu(pren_ref[...]), 0.0)
        u = u_ref[...]
        halo = jnp.where(r > 0, up_ref[...], 0.0)
        du = w_ref[CONV_K - 1:CONV_K, :] * dpre
        dws = [jnp.sum(dpre * u, axis=0, keepdims=True)]
        for k in range(1, CONV_K):
            du = du + w_ref[CONV_K - 1 - k:CONV_K - k, :] * _shift_up(dpre, dnext, k)
            dws.append(jnp.sum(dpre * _shift_down(u, halo, k), axis=0, keepdims=True))
        du_ref[...] = du.astype(BF16)
        dw_ref[...] += jnp.concatenate(dws[::-1], axis=0)
        db_ref[...] += jnp.sum(dpre, axis=0, keepdims=True)

    return pl.pallas_call(
        body, name="conv_bwd", grid=(CONV_CH // CONV_COLS, nr),
        in_specs=[cur, nxt, cur, nxt, ucur, uprev, wspec],
        out_specs=[cur, wspec, bspec],
        out_shape=[jax.ShapeDtypeStruct((s, CONV_CH), BF16), jax.ShapeDtypeStruct((CONV_K, CONV_CH), F32),
                   jax.ShapeDtypeStruct((1, CONV_CH), F32)],
        compiler_params=_cparams(("parallel", "arbitrary")))(dact, dact, pre, pre, proj, proj, conv_w)


def _ssd_common(dtr, dtb, alog):
    lane = lax.broadcasted_iota(jnp.int32, (1, LANE), 1)
    head_lane = lane < SSD_HEADS
    dt = jnp.where(head_lane, _softplus(dtr + dtb), 0.0)
    a = jnp.where(head_lane, -jnp.exp(alog), 0.0)
    row = lax.broadcasted_iota(jnp.int32, (SSD_CHUNK, SSD_CHUNK), 0)
    col = lax.broadcasted_iota(jnp.int32, (SSD_CHUNK, SSD_CHUNK), 1)
    tril = row >= col
    cs = _dot(tril.astype(F32), dt * a, NN, precision=HIGHEST)
    return dt, a, cs, cs.T, tril, lane


def _split_bf16(v, passes):
    terms, rest = [], v
    for _ in range(passes):
        t = rest.astype(BF16)
        terms.append(t)
        rest = rest - t.astype(F32)
    return terms


def _dot_split(v, m, dims, passes):
    terms = _split_bf16(v, passes)
    if passes == 1:
        return _dot(terms[0], m, dims)
    return _dot(jnp.concatenate(terms, axis=1), jnp.concatenate([m] * passes, axis=0 if dims == NN else 1), dims)


def _ssd_constants():
    heads = jnp.arange(LANE)[:, None]
    exp_mat = (heads == (jnp.arange(SSD_D_INNER)[None, :] // HEAD_DIM)).astype(BF16)
    ind4 = ((jnp.arange(SSD_HEADS * SSD_CHUNK)[:, None] // SSD_CHUNK) == jnp.arange(LANE)[None, :]).astype(BF16)
    return exp_mat, ind4


def _expand_heads(v):
    return jnp.repeat(v[:, :SSD_HEADS], HEAD_DIM, axis=1)


def _ssd_prep(dtr, dtb, alog, exp_mat):
    dt, a, cs, cst, tril, lane = _ssd_common(dtr, dtb, alog)
    return dt, a, cs, cst, tril, lane, _dot_split(dt, exp_mat, NN, 2), _dot_split(cs, exp_mat, NN, 3)


def _chunk_decay_rows(cs, g):
    parts = []
    for e in range(HEADS_PER_GROUP):
        h = g * HEADS_PER_GROUP + e
        parts.append(jnp.broadcast_to(jnp.exp(cs[SSD_CHUNK - 1:SSD_CHUNK, h:h + 1]), (HEAD_DIM, SSD_STATE)))
    return jnp.concatenate(parts, axis=0)


def _ssd_fwd(proj, act, dtb, alog, dsk, nw):
    s = proj.shape[0]
    nc = s // SSD_CHUNK
    bc_w = SSD_GROUPS * SSD_STATE
    exp_mat, _ = _ssd_constants()

    def body(z_ref, dtr_ref, xs_ref, b_ref, c_ref, dtb_ref, alog_ref, dskx_ref, nw_ref, exp_ref,
             ypre_ref, yssd_ref, hall_ref, h_scr):
        @pl.when(pl.program_id(0) == 0)
        def _():
            h_scr[...] = jnp.zeros_like(h_scr)

        dt, a, cs, cst, tril, lane, dtx, csx = _ssd_prep(dtr_ref[...], dtb_ref[...], alog_ref[...], exp_ref[...])
        cs_last_x = csx[SSD_CHUNK - 1:SSD_CHUNK, :]
        xs = xs_ref[...]
        xdt = xs * dtx
        xdtb = xdt.astype(BF16)
        xdec = (xdt * jnp.exp(cs_last_x - csx)).astype(BF16)
        ecsx = jnp.exp(csx)
        head_of_lane = lax.broadcasted_iota(jnp.int32, (1, GROUP_WIDTH), 1) // HEAD_DIM
        for g in range(SSD_GROUPS):
            gs = slice(g * GROUP_WIDTH, (g + 1) * GROUP_WIDTH)
            bg = b_ref[:, g * SSD_STATE:(g + 1) * SSD_STATE].astype(BF16)
            cg = c_ref[:, g * SSD_STATE:(g + 1) * SSD_STATE].astype(BF16)
            cb = _dot(cg, bg, NT)
            hprev = h_scr[gs, :]
            hall_ref[0, gs, :] = hprev
            gms, rhs = [], []
            xg = xdtb[:, gs]
            for e in range(HEADS_PER_GROUP):
                h = g * HEADS_PER_GROUP + e
                lm = jnp.exp(jnp.where(tril, cs[:, h:h + 1] - cst[h:h + 1, :], -1e30))
                gms.append((cb * lm).astype(BF16))
                rhs.append(jnp.where(head_of_lane == e, xg, jnp.zeros_like(xg)))
            y = _dot(jnp.concatenate(gms, axis=1), jnp.concatenate(rhs, axis=0), NN)
            y = y + ecsx[:, gs] * _dot(cg, hprev.astype(BF16), NT)
            y = y + dskx_ref[:, gs] * xs[:, gs]
            h_scr[gs, :] = hprev * _chunk_decay_rows(cs, g) + _dot(xdec[:, gs], bg, TN)
            ypre_ref[:, gs] = y
            z = z_ref[:, gs]
            yg = y * (z * _sigmoid(z))
            r = lax.rsqrt(jnp.mean(yg * yg, axis=-1, keepdims=True) + EPS)
            yssd_ref[:, gs] = (yg * r * nw_ref[:, gs]).astype(BF16)

    row_d = lambda cb: pl.BlockSpec((SSD_CHUNK, SSD_D_INNER), lambda c: (c, cb))
    small = pl.BlockSpec((1, LANE), lambda c: (0, 0))
    wide = pl.BlockSpec((1, SSD_D_INNER), lambda c: (0, 0))
    return pl.pallas_call(
        body, name="ssd_fwd", grid=(nc,),
        in_specs=[row_d(OFF_Z // SSD_D_INNER),
                  pl.BlockSpec((SSD_CHUNK, LANE), lambda c: (c, OFF_DT // LANE)),
                  row_d(0),
                  pl.BlockSpec((SSD_CHUNK, bc_w), lambda c: (c, SSD_D_INNER // bc_w)),
                  pl.BlockSpec((SSD_CHUNK, bc_w), lambda c: (c, SSD_D_INNER // bc_w + 1)),
                  small, small, wide, wide, pl.BlockSpec((LANE, SSD_D_INNER), lambda c: (0, 0))],
        out_specs=[row_d(0), row_d(0), pl.BlockSpec((1, SSD_D_INNER, SSD_STATE), lambda c: (c, 0, 0))],
        out_shape=[jax.ShapeDtypeStruct((s, SSD_D_INNER), F32), jax.ShapeDtypeStruct((s, SSD_D_INNER + ATT_D), BF16),
                   jax.ShapeDtypeStruct((nc, SSD_D_INNER, SSD_STATE), F32)],
        scratch_shapes=[pltpu.VMEM((SSD_D_INNER, SSD_STATE), F32)],
        compiler_params=_cparams(("arbitrary",)))(proj, proj, act, act, act, dtb, alog, _expand_heads(dsk), nw, exp_mat)


def _ssd_bwd(dycat, ypre, proj, act, hall, dtb, alog, dsk, nw, comm=None):
    s = proj.shape[0]
    nc = s // SSD_CHUNK
    bc_w = SSD_GROUPS * SSD_STATE

    exp_mat, ind4 = _ssd_constants()
    seg_passes = 1

    def body(dy_ref, ypre_ref, z_ref, dtr_ref, xs_ref, b_ref, c_ref, hall_ref, dtb_ref, alog_ref, dskx_ref, nw_ref,
             exp_ref, ind4_ref, dz_ref, dact_ref, ddtr_ref, da_ref, ddsk_ref, ddtb_ref, dnw_ref, dh_scr):
        @pl.when(pl.program_id(0) == 0)
        def _():
            dh_scr[...] = jnp.zeros_like(dh_scr)
            da_ref[...] = jnp.zeros_like(da_ref)
            ddsk_ref[...] = jnp.zeros_like(ddsk_ref)
            ddtb_ref[...] = jnp.zeros_like(ddtb_ref)
            dnw_ref[...] = jnp.zeros_like(dnw_ref)

        dtr = dtr_ref[...]
        dt, a, cs, cst, tril, lane, dtx, csx = _ssd_prep(dtr, dtb_ref[...], alog_ref[...], exp_ref[...])
        cs_last_x = csx[SSD_CHUNK - 1:SSD_CHUNK, :]
        xs = xs_ref[...]
        xdt = xs * dtx
        xdtb = xdt.astype(BF16)
        decx = jnp.exp(cs_last_x - csx)
        xdecf = xdt * decx
        xdec = xdecf.astype(BF16)
        ecsx = jnp.exp(csx)
        head_of_lane = lax.broadcasted_iota(jnp.int32, (1, GROUP_WIDTH), 1) // HEAD_DIM
        last_row = lax.broadcasted_iota(jnp.int32, (SSD_CHUNK, 1), 0) == SSD_CHUNK - 1
        dcs_col = jnp.zeros((SSD_CHUNK, LANE), F32)
        dcs_row = jnp.zeros((SSD_CHUNK, LANE), F32)
        ddt = jnp.zeros((SSD_CHUNK, LANE), F32)
        ddsk = jnp.zeros((1, LANE), F32)
        hsum = jnp.zeros((1, LANE), F32)
        t1_sum = jnp.zeros((1, LANE), F32)
        for g in range(SSD_GROUPS):
            gs = slice(g * GROUP_WIDTH, (g + 1) * GROUP_WIDTH)
            bsl = slice(g * SSD_STATE, (g + 1) * SSD_STATE)
            exp_g = exp_ref[:, gs]
            ind4_g = ind4_ref[g * HEADS_PER_GROUP * SSD_CHUNK:(g + 1) * HEADS_PER_GROUP * SSD_CHUNK, :]
            z = z_ref[:, gs]
            sg = _sigmoid(z)
            sz = z * sg
            ypre = ypre_ref[:, gs]
            yg = ypre * sz
            r = lax.rsqrt(jnp.mean(yg * yg, axis=-1, keepdims=True) + EPS)
            nrm = yg * r
            dyo_n = dy_ref[:, gs]
            dnw_ref[:, gs] += jnp.sum(dyo_n * nrm, axis=0, keepdims=True)
            dn = dyo_n * nw_ref[:, gs]
            dyg = r * (dn - nrm * jnp.mean(dn * nrm, axis=-1, keepdims=True))
            dz_ref[:, gs] = (dyg * ypre * (sg * (1.0 + z * (1.0 - sg)))).astype(BF16)
            dy = dyg * sz

            bg = b_ref[:, bsl].astype(BF16)
            cg = c_ref[:, bsl].astype(BF16)
            cb = _dot(cg, bg, NT)
            hprev = hall_ref[0, gs, :]
            hb = hprev.astype(BF16)
            dhn = dh_scr[gs, :]
            dhb = dhn.astype(BF16)
            xs_g, xdt_g = xs[:, gs], xdtb[:, gs]
            w_off = _dot(cg, hb, NT)
            dyo = dy * ecsx[:, gs]
            dyob = dyo.astype(BF16)
            dcg = _dot(dyob, hb, NN)
            dh_y = _dot(dyob, cg, TN)
            r_st = _dot(bg, dhb, NT)
            dbg = _dot(xdec[:, gs], dhb, NN)
            dyb = dy.astype(BF16)
            gms, gmbs, lms, dys = [], [], [], []
            for e in range(HEADS_PER_GROUP):
                h = g * HEADS_PER_GROUP + e
                lm = jnp.exp(jnp.where(tril, cs[:, h:h + 1] - cst[h:h + 1, :], -1e30))
                gm = cb * lm
                lms.append(lm)
                gms.append(gm)
                gmbs.append(gm.astype(BF16))
                dys.append(jnp.where(head_of_lane == e, dyb, jnp.zeros_like(dyb)))
            dxdt = _dot(jnp.concatenate(gmbs, axis=0), jnp.concatenate(dys, axis=0), TN) + decx[:, gs] * r_st
            dcb = jnp.zeros((SSD_CHUNK, SSD_CHUNK), F32)
            mms = []
            for e in range(HEADS_PER_GROUP):
                dg = _dot(dys[e], xdt_g, NT)
                mms.append(dg * gms[e])
                dcb = dcb + dg * lms[e]
            seg = _dot_split(jnp.concatenate([dyo * w_off, xdecf[:, gs] * r_st, dxdt * xs_g, dy * xs_g], axis=0), exp_g, NT, seg_passes)
            v1, t1, ddt_g, dsk_g = [seg[i * SSD_CHUNK:(i + 1) * SSD_CHUNK] for i in range(4)]
            dcs_col = dcs_col + v1 - t1 + _dot_split(jnp.concatenate(mms, axis=1), ind4_g, NN, seg_passes)
            for t in _split_bf16(jnp.concatenate(mms, axis=0), seg_passes):
                dcs_row = dcs_row + _dot(ind4_g, t, TN)
            ddt = ddt + ddt_g
            ddsk = ddsk + jnp.sum(dsk_g, axis=0, keepdims=True)
            t1_sum = t1_sum + jnp.sum(t1, axis=0, keepdims=True)
            for e in range(HEADS_PER_GROUP):
                h = g * HEADS_PER_GROUP + e
                hs = slice(e * HEAD_DIM, (e + 1) * HEAD_DIM)
                hsum = hsum + jnp.where(lane == h, jnp.sum(dhn[hs, :] * hprev[hs, :]).reshape(1, 1), 0.0)
            dh_scr[gs, :] = dhn * _chunk_decay_rows(cs, g) + dh_y
            dcbb = dcb.astype(BF16)
            dact_ref[:, gs] = dxdt * dtx[:, gs] + dskx_ref[:, gs] * dy
            dact_ref[:, SSD_D_INNER + g * SSD_STATE:SSD_D_INNER + (g + 1) * SSD_STATE] = dbg + _dot(dcbb, cg, TN)
            dact_ref[:, SSD_D_INNER + bc_w + g * SSD_STATE:SSD_D_INNER + bc_w + (g + 1) * SSD_STATE] = dcg + _dot(dcbb, bg, NN)
        dlast = t1_sum + jnp.exp(cs[SSD_CHUNK - 1:SSD_CHUNK, :]) * hsum
        dcs = dcs_col - dcs_row.T + jnp.where(last_row, dlast, 0.0)
        row = lax.broadcasted_iota(jnp.int32, (SSD_CHUNK, SSD_CHUNK), 0)
        col = lax.broadcasted_iota(jnp.int32, (SSD_CHUNK, SSD_CHUNK), 1)
        dda = _dot((col >= row).astype(F32), dcs, NN, precision=HIGHEST)
        ddt = ddt + dda * a
        da_ref[...] += jnp.sum(dda * dt, axis=0, keepdims=True)
        ddtr = jnp.where(lane < SSD_HEADS, ddt * _sigmoid(dtr + dtb_ref[...]), 0.0)
        ddtr_ref[...] = ddtr.astype(BF16)
        ddtb_ref[...] += jnp.sum(ddtr, axis=0, keepdims=True)
        ddsk_ref[...] += ddsk

    rev = lambda c: nc - 1 - c
    row_d = lambda cb: pl.BlockSpec((SSD_CHUNK, SSD_D_INNER), lambda c: (rev(c), cb))
    small = pl.BlockSpec((1, LANE), lambda c: (0, 0))
    wide = pl.BlockSpec((1, SSD_D_INNER), lambda c: (0, 0))
    small_shape = jax.ShapeDtypeStruct((1, LANE), F32)
    return _pcall(
        body, (dycat, ypre, proj, proj, act, act, act, hall, dtb, alog, _expand_heads(dsk), nw, exp_mat, ind4),
        name="ssd_bwd", grid=(nc,),
        in_specs=[row_d(0), row_d(0), row_d(OFF_Z // SSD_D_INNER),
                  pl.BlockSpec((SSD_CHUNK, LANE), lambda c: (rev(c), OFF_DT // LANE)),
                  row_d(0),
                  pl.BlockSpec((SSD_CHUNK, bc_w), lambda c: (rev(c), SSD_D_INNER // bc_w)),
                  pl.BlockSpec((SSD_CHUNK, bc_w), lambda c: (rev(c), SSD_D_INNER // bc_w + 1)),
                  pl.BlockSpec((1, SSD_D_INNER, SSD_STATE), lambda c: (rev(c), 0, 0)),
                  small, small, wide, wide, pl.BlockSpec((LANE, SSD_D_INNER), lambda c: (0, 0)),
                  pl.BlockSpec((SSD_HEADS * SSD_CHUNK, LANE), lambda c: (0, 0))],
        out_specs=[row_d(0), pl.BlockSpec((SSD_CHUNK, CONV_CH), lambda c: (rev(c), 0)),
                   pl.BlockSpec((SSD_CHUNK, LANE), lambda c: (rev(c), 0)), small, small, small, wide],
        out_shape=[jax.ShapeDtypeStruct((s, SSD_D_INNER), BF16), jax.ShapeDtypeStruct((s, CONV_CH), F32),
                   jax.ShapeDtypeStruct((s, LANE), BF16), small_shape, small_shape, small_shape,
                   jax.ShapeDtypeStruct((1, SSD_D_INNER), F32)],
        scratch_shapes=[pltpu.VMEM((SSD_D_INNER, SSD_STATE), F32)], sem=("arbitrary",), comm=comm)


def _head_mean_matrix():
    row = lax.broadcasted_iota(jnp.int32, (LANE, LANE), 0) // HEAD_DIM
    col = lax.broadcasted_iota(jnp.int32, (LANE, LANE), 1) // HEAD_DIM
    return (row == col).astype(F32)


def _head_sum2(v, ones_bd):
    hi = v.astype(BF16)
    lo = (v - hi.astype(F32)).astype(BF16)
    return _dot(jnp.concatenate([hi, lo], axis=1), jnp.concatenate([ones_bd, ones_bd], axis=0), NN)


def _head_norm(x, w, scale, ones_bd):
    ms = _head_sum2(x * x, ones_bd) * (1.0 / HEAD_DIM)
    return (x * lax.rsqrt(ms + EPS)) * (w * scale)


PRO_ROWS = 256
ATT_GROUP_FWD = 16
ATT_GROUP_BWD = 8
KEYS = 2 * ATT_BLK
NEG = -1e30
HALF = HEAD_DIM // 2


def _rows(start, size, dil):
    return pl.ds(start, size) if dil == 1 else pl.ds(start, size, stride=dil)


def _fill_bias(bias_ref):
    row = lax.broadcasted_iota(jnp.int32, (ATT_BLK, 2 * KEYS), 0)
    col = lax.broadcasted_iota(jnp.int32, (ATT_BLK, 2 * KEYS), 1) & (KEYS - 1)
    for first, off in ((0, 0), (1, ATT_BLK)):
        dist = off + row - col
        bias_ref[first] = jnp.where((dist >= 0) & (dist <= ATT_BLK), 0.0, NEG)


def _pair(a, b):
    return jnp.concatenate([jnp.broadcast_to(a, (ATT_BLK, KEYS)), jnp.broadcast_to(b, (ATT_BLK, KEYS))], axis=1)


def _split_heads(x, is_a):
    zero = jnp.zeros_like(x)
    return jnp.concatenate([jnp.where(is_a, x, zero), jnp.where(is_a, zero, x)], axis=0)


def _block_ids(b, nb):
    i = b & (nb - 1)
    q0 = pl.multiple_of(b * ATT_BLK, ATT_BLK)
    k0 = pl.multiple_of((b - jnp.minimum(i, 1)) * ATT_BLK, ATT_BLK)
    return pl.ds(q0, ATT_BLK), pl.ds(k0, KEYS), jnp.minimum(i, 1)


def _att_fwd(proj, qw, kw, comm=None):
    s = proj.shape[0]
    nblk = s // ATT_BLK
    assert all((s // d) // ATT_BLK >= 2 for d in DILATIONS)
    blk = lambda off: pl.BlockSpec((s, LANE), lambda i: (0, off // LANE + i))
    wspec = pl.BlockSpec((1, LANE), lambda i: (0, i))
    oblk = pl.BlockSpec((s, LANE), lambda i: (0, i))

    def body(q_ref, k_ref, v_ref, qw_ref, kw_ref, o_ref, lse_ref, qn, kn, q_cm, k_cm, v_cm, m_acc, l_acc, o_d, m_d, l_d, bias):
        ones_bd = _head_mean_matrix().astype(BF16)
        is_a = lax.broadcasted_iota(jnp.int32, (1, LANE), 1) < HEAD_DIM
        ones_ext = _split_heads(jnp.ones((KEYS, LANE), BF16), is_a)
        _fill_bias(bias)

        def pro(j, c):
            rows = pl.ds(pl.multiple_of(j * PRO_ROWS, PRO_ROWS), PRO_ROWS)
            qn[rows, :] = _head_norm(q_ref[rows, :], qw_ref[...], HEAD_DIM ** -0.5, ones_bd)
            kn[rows, :] = _head_norm(k_ref[rows, :], kw_ref[...], 1.0, ones_bd)
            return c

        lax.fori_loop(0, s // PRO_ROWS, pro, 0)

        for dil in DILATIONS:
            ln = s // dil
            nb = ln // ATT_BLK
            o_out, m_out, l_out = (o_ref, m_acc, l_acc) if dil == 1 else (o_d, m_d, l_d)
            for r in range(dil):
                def relayout(j, c, dil=dil, r=r, ln=ln):
                    j0 = pl.multiple_of(j * PRO_ROWS, PRO_ROWS)
                    src = _rows(r + dil * j0, PRO_ROWS, dil)
                    dst = pl.ds(r * ln + j0, PRO_ROWS)
                    q_cm[dst, :] = qn[src, :].astype(BF16)
                    k_cm[dst, :] = kn[src, :].astype(BF16)
                    v_cm[dst, :] = v_ref[src, :].astype(BF16)
                    return c

                lax.fori_loop(0, ln // PRO_ROWS, relayout, 0)

            def step(bg, c, nb=nb, o_out=o_out, m_out=m_out, l_out=l_out):
                ids = [_block_ids(bg * ATT_GROUP_FWD + u, nb) for u in range(ATT_GROUP_FWD)]
                kbs = [_split_heads(k_cm[krows, :], is_a) for _, krows, _ in ids]
                scs = [_dot(q_cm[qrows, :], kb, NT) + bias[first] for (qrows, _, first), kb in zip(ids, kbs)]
                mas = [jnp.max(sc[:, :KEYS], axis=-1, keepdims=True) for sc in scs]
                mbs = [jnp.max(sc[:, KEYS:], axis=-1, keepdims=True) for sc in scs]
                ps = [jnp.exp(sc - _pair(ma, mb)).astype(BF16) for sc, ma, mb in zip(scs, mas, mbs)]
                vbs = [jnp.concatenate([_split_heads(v_cm[krows, :], is_a), ones_ext], axis=1) for _, krows, _ in ids]
                ols = [_dot(p, vb, NN) for p, vb in zip(ps, vbs)]
                for (qrows, _, _), ol, ma, mb in zip(ids, ols, mas, mbs):
                    o_out[qrows, :] = ol[:, :LANE]
                    l_out[qrows, :] = ol[:, LANE:]
                    m_out[qrows, :] = jnp.where(is_a, ma, mb)
                return c

            lax.fori_loop(0, nblk // ATT_GROUP_FWD, step, 0)

            if dil > 1:
                for r in range(dil):
                    def merge(j, c, dil=dil, r=r, ln=ln):
                        j0 = pl.multiple_of(j * PRO_ROWS, PRO_ROWS)
                        nat = _rows(r + dil * j0, PRO_ROWS, dil)
                        cm = pl.ds(r * ln + j0, PRO_ROWS)
                        m_old, m_new = m_acc[nat, :], m_d[cm, :]
                        m = jnp.maximum(m_old, m_new)
                        a_old, a_new = jnp.exp(m_old - m), jnp.exp(m_new - m)
                        o_ref[nat, :] = a_old * o_ref[nat, :] + a_new * o_d[cm, :]
                        l_acc[nat, :] = a_old * l_acc[nat, :] + a_new * l_d[cm, :]
                        m_acc[nat, :] = m
                        return c

                    lax.fori_loop(0, ln // PRO_ROWS, merge, 0)

        def epi(j, c):
            rows = pl.ds(pl.multiple_of(j * PRO_ROWS, PRO_ROWS), PRO_ROWS)
            l = l_acc[rows, :]
            o_ref[rows, :] = o_ref[rows, :] / l
            lse_ref[rows, :] = m_acc[rows, :] + jnp.log(l)
            return c

        lax.fori_loop(0, s // PRO_ROWS, epi, 0)

    f = jax.ShapeDtypeStruct((s, ATT_D), F32)
    scr = pltpu.VMEM((s, LANE), F32)
    scb = pltpu.VMEM((s, LANE), BF16)
    return _pcall(
        body, (proj, proj, proj, qw, kw), name="att_fwd", grid=(ATT_D // LANE,),
        in_specs=[blk(OFF_Q), blk(OFF_K), blk(OFF_V), wspec, wspec], out_specs=[oblk, oblk], out_shape=[f, f],
        scratch_shapes=[scr, scr, scb, scb, scb, scr, scr, scr, scr, scr, pltpu.VMEM((2, ATT_BLK, 2 * KEYS), F32)],
        sem=("parallel",), comm=comm)


def _att_bwd(proj, do, stats, qw, kw, comm=None):
    s = proj.shape[0]
    nblk = s // ATT_BLK
    blk = lambda off: pl.BlockSpec((s, LANE), lambda i: (0, off // LANE + i))
    wspec = pl.BlockSpec((1, LANE), lambda i: (0, i))
    oblk = pl.BlockSpec((s, LANE), lambda i: (0, i))

    def body(q_ref, k_ref, v_ref, do_ref, st_ref, qw_ref, kw_ref, dq_ref, dk_ref, dv_ref, dqw_ref, dkw_ref,
             qn, kn, q_cm, do_cm, k_cm, v_cm, st_cm, dq_acc, dk_acc, dv_acc, dq_d, dk_d, dv_d, bias):
        ones_bd = _head_mean_matrix().astype(BF16)
        is_a = lax.broadcasted_iota(jnp.int32, (1, LANE), 1) < HEAD_DIM
        _fill_bias(bias)
        zero = jnp.zeros((PRO_ROWS, LANE), F32)

        def pro(j, c):
            rows = pl.ds(pl.multiple_of(j * PRO_ROWS, PRO_ROWS), PRO_ROWS)
            qn[rows, :] = _head_norm(q_ref[rows, :], qw_ref[...], HEAD_DIM ** -0.5, ones_bd)
            kn[rows, :] = _head_norm(k_ref[rows, :], kw_ref[...], 1.0, ones_bd)
            dk_acc[rows, :] = zero
            dv_acc[rows, :] = zero
            return c

        lax.fori_loop(0, s // PRO_ROWS, pro, 0)

        for dil in DILATIONS:
            ln = s // dil
            nb = ln // ATT_BLK
            dq_o, dk_o, dv_o = (dq_acc, dk_acc, dv_acc) if dil == 1 else (dq_d, dk_d, dv_d)
            for r in range(dil):
                def relayout(j, c, dil=dil, r=r, ln=ln):
                    j0 = pl.multiple_of(j * PRO_ROWS, PRO_ROWS)
                    src = _rows(r + dil * j0, PRO_ROWS, dil)
                    dst = pl.ds(r * ln + j0, PRO_ROWS)
                    q_cm[dst, :] = qn[src, :].astype(BF16)
                    k_cm[dst, :] = kn[src, :].astype(BF16)
                    v_cm[dst, :] = v_ref[src, :].astype(BF16)
                    do_cm[dst, :] = do_ref[src, :].astype(BF16)
                    st_cm[dst, :] = st_ref[src, :]
                    if dil > 1:
                        dk_d[dst, :] = zero
                        dv_d[dst, :] = zero
                    return c

                lax.fori_loop(0, ln // PRO_ROWS, relayout, 0)

            def step(bg, c, nb=nb, dq_o=dq_o, dk_o=dk_o, dv_o=dv_o):
                ids = [_block_ids(bg * ATT_GROUP_BWD + u, nb) for u in range(ATT_GROUP_BWD)]
                qbs = [q_cm[qrows, :] for qrows, _, _ in ids]
                dobs = [do_cm[qrows, :] for qrows, _, _ in ids]
                kbs = [_split_heads(k_cm[krows, :], is_a) for _, krows, _ in ids]
                vbs = [_split_heads(v_cm[krows, :], is_a) for _, krows, _ in ids]
                sts = [st_cm[qrows, :] for qrows, _, _ in ids]
                scs = [_dot(qb, kb, NT) + bias[first] for qb, kb, (_, _, first) in zip(qbs, kbs, ids)]
                dps = [_dot(dob, vb, NT) for dob, vb in zip(dobs, vbs)]
                ps = [jnp.exp(sc - _pair(st[:, 0:1], st[:, HEAD_DIM:HEAD_DIM + 1])) for sc, st in zip(scs, sts)]
                dss = [(p * (dp - _pair(st[:, HALF:HALF + 1], st[:, HEAD_DIM + HALF:HEAD_DIM + HALF + 1]))).astype(BF16)
                       for p, dp, st in zip(ps, dps, sts)]
                dqs = [_dot(ds, kb, NN) for ds, kb in zip(dss, kbs)]
                dkfs = [_dot(ds, qb, TN) for ds, qb in zip(dss, qbs)]
                dvfs = [_dot(p.astype(BF16), dob, TN) for p, dob in zip(ps, dobs)]
                for (qrows, krows, _), dq, dkf, dvf in zip(ids, dqs, dkfs, dvfs):
                    dq_o[qrows, :] = dq
                    dk_o[krows, :] += jnp.where(is_a, dkf[:KEYS], dkf[KEYS:])
                    dv_o[krows, :] += jnp.where(is_a, dvf[:KEYS], dvf[KEYS:])
                return c

            lax.fori_loop(0, nblk // ATT_GROUP_BWD, step, 0)

            if dil > 1:
                for r in range(dil):
                    def merge(j, c, dil=dil, r=r, ln=ln):
                        j0 = pl.multiple_of(j * PRO_ROWS, PRO_ROWS)
                        nat = _rows(r + dil * j0, PRO_ROWS, dil)
                        cm = pl.ds(r * ln + j0, PRO_ROWS)
                        dq_acc[nat, :] += dq_d[cm, :]
                        dk_acc[nat, :] += dk_d[cm, :]
                        dv_acc[nat, :] += dv_d[cm, :]
                        return c

                    lax.fori_loop(0, ln // PRO_ROWS, merge, 0)

        def back(dn_out, x, w, scale):
            r = lax.rsqrt(_head_sum2(x * x, ones_bd) * (1.0 / HEAD_DIM) + EPS)
            nrm = x * r
            dw = jnp.sum(dn_out * nrm, axis=0, keepdims=True) * scale
            dn = dn_out * (w * scale)
            return r * (dn - nrm * (_head_sum2(dn * nrm, ones_bd) * (1.0 / HEAD_DIM))), dw

        def epi(j, c):
            rows = pl.ds(pl.multiple_of(j * PRO_ROWS, PRO_ROWS), PRO_ROWS)
            dq, dqw = back(dq_acc[rows, :], q_ref[rows, :], qw_ref[...], HEAD_DIM ** -0.5)
            dk, dkw = back(dk_acc[rows, :], k_ref[rows, :], kw_ref[...], 1.0)
            dq_ref[rows, :] = dq.astype(BF16)
            dk_ref[rows, :] = dk.astype(BF16)
            dv_ref[rows, :] = dv_acc[rows, :].astype(BF16)
            return (c[0] + dqw, c[1] + dkw)

        zrow = jnp.zeros((1, LANE), F32)
        dqw, dkw = lax.fori_loop(0, s // PRO_ROWS, epi, (zrow, zrow))
        dqw_ref[...] = dqw
        dkw_ref[...] = dkw

    o = jax.ShapeDtypeStruct((s, ATT_D), BF16)
    ov = jax.ShapeDtypeStruct((1, ATT_D), F32)
    scr = pltpu.VMEM((s, LANE), F32)
    scb = pltpu.VMEM((s, LANE), BF16)
    return _pcall(
        body, (proj, proj, proj, do, stats, qw, kw), name="att_bwd", grid=(ATT_D // LANE,),
        in_specs=[blk(OFF_Q), blk(OFF_K), blk(OFF_V), oblk, oblk, wspec, wspec],
        out_specs=[oblk, oblk, oblk, wspec, wspec], out_shape=[o, o, o, ov, ov],
        scratch_shapes=[scr, scr, scb, scb, scb, scb, scr, scr, scr, scr, scr, scr, scr, pltpu.VMEM((2, ATT_BLK, 2 * KEYS), F32)],
        sem=("parallel",), comm=comm)


def _att_norm_fwd(o, nw, ycat):
    s = o.shape[0]
    row = pl.BlockSpec((ROW_TILE, ATT_D), lambda i: (i, 0))
    vec = pl.BlockSpec((1, ATT_D), lambda i: (0, 0))

    def body(o_ref, nw_ref, ycat_ref, y_ref):
        o = o_ref[...]
        r = lax.rsqrt(jnp.mean(o * o, axis=-1, keepdims=True) + EPS)
        y_ref[...] = (o * r * nw_ref[...]).astype(BF16)

    return pl.pallas_call(body, name="att_norm_fwd", grid=(s // ROW_TILE,),
                          in_specs=[row, vec, pl.BlockSpec(memory_space=pl.ANY)],
                          out_specs=pl.BlockSpec((ROW_TILE, ATT_D), lambda i: (i, 1)),
                          out_shape=jax.ShapeDtypeStruct(ycat.shape, BF16), input_output_aliases={2: 0},
                          compiler_params=_cparams(("parallel",)))(o, nw, ycat)


def _att_norm_bwd(dycat, o, lse, nw):
    s = o.shape[0]
    row = pl.BlockSpec((ROW_TILE, ATT_D), lambda i: (i, 0))
    vec = pl.BlockSpec((1, ATT_D), lambda i: (0, 0))

    def body(dy_ref, o_ref, lse_ref, nw_ref, do_ref, st_ref, dnw_ref):
        @pl.when(pl.program_id(0) == 0)
        def _():
            dnw_ref[...] = jnp.zeros_like(dnw_ref)

        o = o_ref[...]
        dy = dy_ref[...]
        r = lax.rsqrt(jnp.mean(o * o, axis=-1, keepdims=True) + EPS)
        nrm = o * r
        dnw_ref[...] += jnp.sum(dy * nrm, axis=0, keepdims=True)
        dn = dy * nw_ref[...]
        do = r * (dn - nrm * jnp.mean(dn * nrm, axis=-1, keepdims=True))
        do_ref[...] = do
        ones_bd = _head_mean_matrix().astype(BF16)
        prod = do * o
        delta = jnp.concatenate([_head_sum2(prod[:, j * LANE:(j + 1) * LANE], ones_bd) for j in range(ATT_D // LANE)], axis=1)
        lane = lax.broadcasted_iota(jnp.int32, (1, ATT_D), 1)
        st_ref[...] = jnp.where((lane & (HEAD_DIM - 1)) < HALF, lse_ref[...], delta)

    f = jax.ShapeDtypeStruct((s, ATT_D), F32)
    return pl.pallas_call(
        body, name="att_norm_bwd", grid=(s // ROW_TILE,),
        in_specs=[pl.BlockSpec((ROW_TILE, ATT_D), lambda i: (i, 1)), row, row, vec], out_specs=[row, row, vec],
        out_shape=[f, f, jax.ShapeDtypeStruct((1, ATT_D), F32)],
        compiler_params=_cparams(("arbitrary",)))(dycat, o, lse, nw)


def _ada_fwd(c_all, w_ada):
    def body(c_ref, w_ref, o_ref):
        cv = c_ref[...]
        o_ref[...] = _dot((cv * _sigmoid(cv)).astype(BF16), w_ref[...].astype(BF16), NN)

    return pl.pallas_call(body, name="ada_fwd", out_shape=jax.ShapeDtypeStruct((c_all.shape[0], w_ada.shape[1]), F32),
                          compiler_params=_cparams())(c_all, w_ada)


def _adamw_math(g, w, m, v):
    m_new = ADAM_B1 * m + (1.0 - ADAM_B1) * g
    v_new = ADAM_B2 * v + (1.0 - ADAM_B2) * (g * g)
    m_hat = m_new / (1.0 - ADAM_B1 ** ADAM_STEP)
    v_hat = v_new / (1.0 - ADAM_B2 ** ADAM_STEP)
    delta = -ADAM_LR * (m_hat / (jnp.sqrt(v_hat) + ADAM_EPS) + ADAM_WD * w)
    return delta, m_new, v_new


def _ada_bwd_adamw(c_all, dmod_cols, w, m, v):
    rows, cols = w.shape
    tr = 256
    blk = pl.BlockSpec((tr, cols), lambda i: (i, 0))

    def body(c_ref, d_ref, w_ref, m_ref, v_ref, g_ref, dl_ref, mo_ref, vo_ref):
        cv = c_ref[...]
        ca = cv * _sigmoid(cv)
        g = ca[:, 0:1] * d_ref[0:1, :]
        for b in range(1, N_DEV):
            g = g + ca[:, b:b + 1] * d_ref[b:b + 1, :]
        g_ref[...] = g
        dl_ref[...], mo_ref[...], vo_ref[...] = _adamw_math(g, w_ref[...], m_ref[...], v_ref[...])

    o = jax.ShapeDtypeStruct((rows, cols), F32)
    return pl.pallas_call(
        body, name="ada_bwd_adamw", grid=(rows // tr,),
        in_specs=[pl.BlockSpec((tr, N_DEV), lambda i: (i, 0)), pl.BlockSpec((N_DEV, cols), lambda i: (0, 0)), blk, blk, blk],
        out_specs=[blk] * 4, out_shape=[o, o, o, o], compiler_params=_cparams(("parallel",)))(c_all.T, dmod_cols, w, m, v)


def _reduce_adamw(slabs, w, m, v, name):
    rows, cols = w.shape
    n_src = slabs.shape[0]
    if rows % 128 == 0:
        tr, steps = 128, rows // 128
        blk = pl.BlockSpec((tr, cols), lambda i: (i, 0))
        sblk = pl.BlockSpec((n_src, tr, cols), lambda i: (0, i, 0))
    else:
        tc, steps = 256, cols // 256
        blk = pl.BlockSpec((rows, tc), lambda i: (0, i))
        sblk = pl.BlockSpec((n_src, rows, tc), lambda i: (0, 0, i))

    def body(s_ref, w_ref, m_ref, v_ref, g_ref, dl_ref, mo_ref, vo_ref):
        g = s_ref[0].astype(F32)
        for src in range(1, n_src):
            g = g + s_ref[src].astype(F32)
        g_ref[...] = g
        dl_ref[...], mo_ref[...], vo_ref[...] = _adamw_math(g, w_ref[...], m_ref[...], v_ref[...])

    o = jax.ShapeDtypeStruct((rows, cols), F32)
    return pl.pallas_call(
        body, name=name, grid=(steps,), in_specs=[sblk, blk, blk, blk],
        out_specs=[blk] * 4, out_shape=[o, o, o, o], compiler_params=_cparams(("parallel",)))(slabs, w, m, v)


def _small_reduce_adamw(gathered, w, m, v):
    def body(s_ref, w_ref, m_ref, v_ref, g_ref, dl_ref, mo_ref, vo_ref):
        g = s_ref[0]
        for dev in range(1, N_DEV):
            g = g + s_ref[dev]
        g_ref[...] = g
        dl_ref[...], mo_ref[...], vo_ref[...] = _adamw_math(g, w_ref[...], m_ref[...], v_ref[...])

    o = jax.ShapeDtypeStruct(w.shape, F32)
    return pl.pallas_call(body, name="small_reduce_adamw", out_shape=[o, o, o, o], compiler_params=_cparams())(gathered, w, m, v)


def _adamw_small(g, w, m, v, name):
    def body(g_ref, w_ref, m_ref, v_ref, dl_ref, mo_ref, vo_ref):
        dl_ref[...], mo_ref[...], vo_ref[...] = _adamw_math(g_ref[...], w_ref[...], m_ref[...], v_ref[...])

    o = jax.ShapeDtypeStruct(w.shape, F32)
    return pl.pallas_call(body, name=name, out_shape=[o, o, o], compiler_params=_cparams())(g, w, m, v)


class _Exchange:
    def __init__(self, arrs, scatter):
        self.arrs, self.scatter, self.n = list(arrs), scatter, len(arrs)
        hbm = pl.BlockSpec(memory_space=pltpu.HBM)
        self.in_specs = [hbm] * self.n
        self.out_specs = [hbm] * self.n
        self.out_shape = [jax.ShapeDtypeStruct(a.shape if scatter else (N_DEV,) + a.shape, a.dtype) for a in self.arrs]
        self.scratch = [pltpu.SemaphoreType.DMA((self.n * (N_DEV - 1),)), pltpu.SemaphoreType.DMA((self.n * (N_DEV - 1),)),
                        pltpu.SemaphoreType.DMA((self.n,))]

    def _local(self, ins, outs, sems):
        me = 4 * lax.axis_index("x") + 2 * lax.axis_index("y") + lax.axis_index("c")
        return [pltpu.make_async_copy(ins[a].at[me] if self.scatter else ins[a], outs[a].at[me], sems[2].at[a])
                for a in range(self.n)]

    def _remote(self, ins, outs, sems, arriving):
        send_sems, recv_sems, _ = sems
        x, y, c = lax.axis_index("x"), lax.axis_index("y"), lax.axis_index("c")
        me = 4 * x + 2 * y + c
        remote = []
        for a in range(self.n):
            for k in range(1, N_DEV):
                px = 1 - x if k & 4 else x
                py = 1 - y if k & 2 else y
                pc = 1 - c if k & 1 else c
                peer = 4 * px + 2 * py + pc
                sem = a * (N_DEV - 1) + k - 1
                remote.append(pltpu.make_async_remote_copy(
                    src_ref=ins[a].at[peer] if self.scatter else ins[a], dst_ref=outs[a].at[peer if arriving else me],
                    send_sem=send_sems.at[sem], recv_sem=recv_sems.at[sem], device_id=(px, py, pc), device_id_type=MESH_IDS))
        return remote

    def start(self, ins, outs, sems):
        for cp in self._local(ins, outs, sems) + self._remote(ins, outs, sems, arriving=False):
            cp.start()

    def forward(self, ins, outs, sems):
        pass

    def wait(self, ins, outs, sems):
        for send, arrival in zip(self._remote(ins, outs, sems, arriving=False), self._remote(ins, outs, sems, arriving=True)):
            send.wait_send()
            arrival.wait_recv()
        for cp in self._local(ins, outs, sems):
            cp.wait()


N_CHIP = N_DEV // 2


class _SiblingSwap(_Exchange):
    def __init__(self, arrs):
        super().__init__(arrs, scatter=True)
        self.out_shape = [jax.ShapeDtypeStruct((N_CHIP,) + a.shape[2:], a.dtype) for a in self.arrs]
        self.scratch = [pltpu.SemaphoreType.DMA((self.n,)), pltpu.SemaphoreType.DMA((self.n,)), pltpu.SemaphoreType.DMA((1,))]

    def _copies(self, ins, outs, sems):
        x, y, c = lax.axis_index("x"), lax.axis_index("y"), lax.axis_index("c")
        return [pltpu.make_async_remote_copy(src_ref=ins[a].at[:, 1 - c], dst_ref=outs[a], send_sem=sems[0].at[a], recv_sem=sems[1].at[a],
                                             device_id=(x, y, 1 - c), device_id_type=MESH_IDS) for a in range(self.n)]

    def start(self, ins, outs, sems):
        for cp in self._copies(ins, outs, sems):
            cp.start()

    def wait(self, ins, outs, sems):
        for cp in self._copies(ins, outs, sems):
            cp.wait()


class _ChipScatter(_Exchange):
    def __init__(self, arrs):
        super().__init__(arrs, scatter=True)
        n_pairs = self.n * (N_CHIP - 1)
        self.scratch = [pltpu.SemaphoreType.DMA((n_pairs,)), pltpu.SemaphoreType.DMA((n_pairs,)), pltpu.SemaphoreType.DMA((self.n,))]

    def _local(self, ins, outs, sems):
        chip = 2 * lax.axis_index("x") + lax.axis_index("y")
        return [pltpu.make_async_copy(ins[a].at[chip], outs[a].at[chip], sems[2].at[a]) for a in range(self.n)]

    def _remote(self, ins, outs, sems, arriving):
        send_sems, recv_sems, _ = sems
        x, y, c = lax.axis_index("x"), lax.axis_index("y"), lax.axis_index("c")
        chip = 2 * x + y
        remote = []
        for a in range(self.n):
            for k in range(1, N_CHIP):
                px = 1 - x if k & 2 else x
                py = 1 - y if k & 1 else y
                peer = 2 * px + py
                sem = a * (N_CHIP - 1) + k - 1
                remote.append(pltpu.make_async_remote_copy(
                    src_ref=ins[a].at[peer], dst_ref=outs[a].at[peer if arriving else chip], send_sem=send_sems.at[sem],
                    recv_sem=recv_sems.at[sem], device_id=(px, py, c), device_id_type=MESH_IDS))
        return remote


def _chip_sum(mine, theirs):
    n, rows, cols = mine.shape
    blk = pl.BlockSpec((1, rows, 256), lambda q, j: (q, 0, j))

    def body(a_ref, b_ref, o_ref):
        o_ref[...] = (a_ref[...].astype(F32) + b_ref[...].astype(F32)).astype(BF16)

    return pl.pallas_call(body, name="chip_sum", grid=(n, cols // 256), in_specs=[blk, blk], out_specs=blk,
                          out_shape=jax.ShapeDtypeStruct(mine.shape, BF16),
                          compiler_params=_cparams(("parallel", "parallel")))(mine, theirs)


class _Gather2(_Exchange):
    def __init__(self, arrs):
        super().__init__(arrs, scatter=False)

    def _copies(self, ins, outs, sems):
        send_sems, recv_sems, _ = sems
        x, y, c = lax.axis_index("x"), lax.axis_index("y"), lax.axis_index("c")
        sibling = (x, y, 1 - c)
        chips = [(1 - x, y), (x, 1 - y), (1 - x, 1 - y)]
        first, passed, landed = [], [], []
        for a in range(self.n):
            def copy(k, block, to, src=None, a=a):
                slab = outs[a].at[4 * block[0] + 2 * block[1] + block[2]]
                return pltpu.make_async_remote_copy(
                    src_ref=slab if src is None else src, dst_ref=slab, send_sem=send_sems.at[a * (N_DEV - 1) + k],
                    recv_sem=recv_sems.at[a * (N_DEV - 1) + k], device_id=to, device_id_type=MESH_IDS)

            first.append(copy(0, (x, y, c), sibling, src=ins[a]))
            landed.append(copy(0, sibling, sibling))
            for j, chip in enumerate(chips):
                first.append(copy(1 + j, (x, y, c), (*chip, c), src=ins[a]))
                passed.append((copy(1 + j, (*chip, c), sibling), copy(4 + j, (*chip, c), sibling)))
                landed.append(copy(4 + j, (*chip, 1 - c), sibling))
        return first, passed, landed

    def start(self, ins, outs, sems):
        for cp in self._local(ins, outs, sems) + self._copies(ins, outs, sems)[0]:
            cp.start()

    def forward(self, ins, outs, sems):
        for arrival, onward in self._copies(ins, outs, sems)[1]:
            arrival.wait_recv()
            onward.start()

    def wait(self, ins, outs, sems):
        first, passed, landed = self._copies(ins, outs, sems)
        for arrival in landed:
            arrival.wait_recv()
        for cp in first + [onward for _, onward in passed]:
            cp.wait_send()
        for cp in self._local(ins, outs, sems):
            cp.wait()


def _split_comm_refs(refs, n_in, n_out, n_scr, comm):
    nc = comm.n if comm is not None else 0
    ns = 3 if comm is not None else 0
    pos, groups = 0, []
    for cnt in (n_in, nc, n_out, nc, n_scr, ns):
        groups.append(refs[pos:pos + cnt])
        pos += cnt
    assert pos == len(refs), (pos, len(refs))
    return groups


def _pcall(body, args, *, name, grid, in_specs, out_specs, out_shape, scratch_shapes=(), sem=None, comm=None):
    in_specs, out_specs, out_shape, scratch_shapes = list(in_specs), list(out_specs), list(out_shape), list(scratch_shapes)
    n_in, n_out, n_scr = len(in_specs), len(out_specs), len(scratch_shapes)
    if comm is None:
        kernel_body = body
    else:
        def kernel_body(*refs):
            ins, cins, outs, couts, scr, sems = _split_comm_refs(refs, n_in, n_out, n_scr, comm)
            ids = [pl.program_id(a) for a in range(len(grid))]
            first, last = ids[0] == 0, ids[0] == grid[0] - 1
            for a in range(1, len(grid)):
                first, last = first & (ids[a] == 0), last & (ids[a] == grid[a] - 1)

            middle = ids[0] == (2 * grid[0]) // 3
            for a in range(1, len(grid)):
                middle = middle & (ids[a] == 0)

            @pl.when(first)
            def _():
                comm.start(cins, couts, sems)

            @pl.when(middle)
            def _():
                comm.forward(cins, couts, sems)

            body(*ins, *outs, *scr)

            @pl.when(last)
            def _():
                comm.wait(cins, couts, sems)

        in_specs, out_specs, out_shape = in_specs + comm.in_specs, out_specs + comm.out_specs, out_shape + comm.out_shape
        scratch_shapes, args = scratch_shapes + comm.scratch, list(args) + comm.arrs
        sem = ("arbitrary",) * len(grid)
    res = pl.pallas_call(kernel_body, name=name, grid=grid, in_specs=in_specs, out_specs=out_specs, out_shape=out_shape,
                         scratch_shapes=scratch_shapes, compiler_params=_cparams(sem))(*args)
    return res[:n_out], res[n_out:]


def _exchange(arrs, name, scatter=False, ex=None):
    if ex is None:
        ex = _Exchange(arrs, scatter=True) if scatter else _Gather2(arrs)

    def body(*refs):
        _, ins, _, outs, _, sems = _split_comm_refs(refs, 0, 0, 0, ex)
        ex.start(ins, outs, sems)
        ex.forward(ins, outs, sems)
        ex.wait(ins, outs, sems)

    return pl.pallas_call(body, name=name, in_specs=ex.in_specs, out_specs=ex.out_specs, out_shape=ex.out_shape,
                          scratch_shapes=ex.scratch)(*ex.arrs)


def _pad_lanes(v, width=LANE):
    return jnp.pad(v, ((0, 0), (0, width - v.shape[1])))


def _shards_to_cols(g):
    return jnp.transpose(g, (1, 0, 2)).reshape(g.shape[1], N_DEV * g.shape[2])


def _cols_to_shards(w):
    return w.astype(BF16).reshape(w.shape[0], N_DEV, w.shape[1] // N_DEV).transpose(1, 0, 2)


def _local_step(x, tgt, mod, w_in_pt, conv_w, conv_b, dt_bias, a_log, d_skip, ssd_norm_w, q_norm_w, k_norm_w,
                attn_norm_w, w_out_sh, w_ff1_sh, w_ff2_sh, norm1_w, norm2_w, core):
    shift1, scale1, gate1, shift2, scale2, gate2 = [mod[i:i + 1] for i in range(N_MOD)]
    dtb, alog, dsk = _pad_lanes(dt_bias), _pad_lanes(a_log), _pad_lanes(d_skip)
    qw, kw = jnp.tile(q_norm_w, (1, ATT_HEADS)), jnp.tile(k_norm_w, (1, ATT_HEADS))

    h1 = _norm_mod_fwd(x, norm1_w, scale1, shift1, "norm1_fwd")
    proj = _matmul(h1, w_in_pt, tb=True, tm=2048, tn=896, tk=1024, name="in_proj")
    pre, act = _conv_fwd(proj, conv_w, conv_b)
    ypre, ycat_ssd, hall = _ssd_fwd(proj, act, dtb, alog, dsk, ssd_norm_w)
    (o_att, lse), (w_out_g, w_ff1_g, w_ff2_g) = _att_fwd(proj, qw, kw, comm=_Gather2([w_out_sh, w_ff1_sh, w_ff2_sh]))
    w_out = w_out_g.reshape(2 * D_MODEL, D_MODEL)
    w_ff1 = _shards_to_cols(w_ff1_g)
    w_ff2 = w_ff2_g.reshape(D_FF, D_MODEL)
    ycat = _att_norm_fwd(o_att, attn_norm_w, ycat_ssd)
    row32, row16, vec32 = ("row", F32), ("row", BF16), ("vec", F32)
    mix, x1, h2 = _matmul_rows(ycat, w_out, _residual_norm_epilogue, [x], [gate1, norm2_w, scale2, shift2],
                               [row32, row32, row16], tm=512, name="out_proj")
    u, act_ff = _matmul(h2, w_ff1, tm=1024, tn=1024, tk=1024, name="ff1", mode="relu2")
    loss, dout, dff, dgate2 = _matmul_rows(act_ff, w_ff2, _loss_epilogue, [x1, tgt], [gate2],
                                           [("one", F32), row32, row16, vec32], tm=512, name="ff2")

    du = _matmul(dff, w_ff2, tb=True, tm=1024, tn=1024, tk=1024, out_dtype=BF16, name="ff2_dx", mode="drelu2", u=u)
    g_ff2 = _matmul(act_ff, dff, ta=True, tm=512, tn=1024, tk=4096, out_dtype=BF16, name="ff2_dw")
    dx1, dshift2, dscale2, g_norm2, dmix, dgate1 = _matmul_rows(
        du, w_ff1, _norm_bwd_epilogue, [x1, dout, mix], [norm2_w, scale2, gate1],
        [row32, vec32, vec32, vec32, row16, vec32], tb=True, tm=512, name="ff1_dx")
    g_ff1 = _matmul(h2, du, ta=True, tm=512, tn=1024, tk=4096, out_dtype=BF16, name="ff1_dw")

    dycat = _matmul(dmix, w_out, tb=True, tm=1024, tn=1024, tk=1024, name="out_proj_dx")
    g_out = _matmul(ycat, dmix, ta=True, tm=512, tn=1024, tk=4096, out_dtype=BF16, name="out_proj_dw")
    do, stats, g_attn_norm = _att_norm_bwd(dycat, o_att, lse, attn_norm_w)
    ff_slabs = [_cols_to_shards(g_ff1), g_ff2.astype(BF16).reshape(N_DEV, D_FF // N_DEV, D_MODEL)]
    (dq, dk, dv, dqw, dkw), (s_ff1, s_ff2) = _att_bwd(proj, do, stats, qw, kw, comm=_Exchange(ff_slabs, scatter=True))
    out_slabs = [g_out.astype(BF16).reshape(N_DEV, 2 * D_MODEL // N_DEV, D_MODEL)]
    (dz, dact, ddtr, da, g_dsk, g_dtb, g_ssd_norm), (s_out,) = _ssd_bwd(
        dycat, ypre, proj, act, hall, dtb, alog, dsk, ssd_norm_w, comm=_Exchange(out_slabs, scatter=True))
    dxbc, g_conv_w, g_conv_b = _conv_bwd(dact, pre, proj, conv_w)
    dproj = jnp.concatenate([dz, dxbc, dq, dk, dv, ddtr], axis=1)
    g_in_pt = _matmul(dproj, h1, ta=True, tm=896, tn=1024, tk=4096, out_dtype=BF16, name="in_proj_dw")
    in_slabs = _unpack_w_in_rows(g_in_pt).reshape(N_CHIP, 2, IN_W // N_DEV, D_MODEL)
    (sibling_slabs,) = _exchange(None, "swap_w_in_grads", ex=_SiblingSwap([in_slabs]))
    chip_slabs = _chip_sum(lax.dynamic_index_in_dim(in_slabs, core, axis=1, keepdims=False), sibling_slabs)
    (grad_x, dshift1, dscale1, g_norm1), (s_in,) = _matmul_rows(
        dproj, w_in_pt, _norm_bwd_epilogue, [x, dx1], [norm1_w, scale1], [row32, vec32, vec32, vec32],
        tm=256, name="in_proj_dx", comm=_ChipScatter([chip_slabs]))

    dmod = jnp.concatenate([dshift1, dscale1, dgate1, dshift2, dscale2, dgate2], axis=0)
    g_alog = da[:, :SSD_HEADS] * (-jnp.exp(a_log))
    g_qw = dqw.reshape(ATT_HEADS, HEAD_DIM).sum(axis=0, keepdims=True)
    g_kw = dkw.reshape(ATT_HEADS, HEAD_DIM).sum(axis=0, keepdims=True)
    return dict(loss=loss, grad_x=grad_x, dmod=dmod, norm1_w=g_norm1, norm2_w=g_norm2, w_in=s_in, conv_w=g_conv_w,
                conv_b=g_conv_b, dt_bias=g_dtb[:, :SSD_HEADS], a_log=g_alog, d_skip=g_dsk[:, :SSD_HEADS],
                ssd_norm_w=g_ssd_norm, q_norm_w=g_qw, k_norm_w=g_kw, attn_norm_w=g_attn_norm, w_out=s_out,
                w_ff1=s_ff1, w_ff2=s_ff2)


def _pack_w_in_rows(wt_full):
    o_dt = SSD_D_INNER + CONV_CH
    o_q = o_dt + SSD_HEADS
    pad = jnp.zeros((LANE - SSD_HEADS, wt_full.shape[1]), wt_full.dtype)
    return jnp.concatenate([wt_full[:o_dt], wt_full[o_q:], wt_full[o_dt:o_q], pad], axis=0)


def _unpack_w_in_rows(gt_p):
    return jnp.concatenate([gt_p[:OFF_Q], gt_p[OFF_DT:OFF_DT + SSD_HEADS], gt_p[OFF_Q:OFF_DT]], axis=0)


MISC_FIELDS = (("dt_bias", SSD_HEADS), ("a_log", SSD_HEADS), ("d_skip", SSD_HEADS), ("q_norm_w", HEAD_DIM), ("k_norm_w", HEAD_DIM))
SMALL_LAYOUT = (("b_ada", 6), ("norm1_w", 1), ("norm2_w", 1), ("conv_w", 8), ("conv_b", 2), ("ssd_norm_w", 1),
                ("attn_norm_w", 1), ("misc", 1))


def _pack_small(vals):
    rows = []
    for name, nrow in SMALL_LAYOUT:
        if name == "misc":
            misc = jnp.concatenate([vals[f].reshape(1, n) for f, n in MISC_FIELDS], axis=1)
            rows.append(_pad_lanes(misc, D_MODEL))
        elif name in vals:
            rows.append(vals[name].reshape(nrow, D_MODEL))
        else:
            rows.append(jnp.zeros((nrow, D_MODEL), F32))
    used = sum(n for _, n in SMALL_LAYOUT)
    rows.append(jnp.zeros((SMALL_ROWS - used, D_MODEL), F32))
    return jnp.concatenate(rows, axis=0)


def _unpack_small(packed):
    out, r = {}, 0
    for name, nrow in SMALL_LAYOUT:
        blk = packed[r:r + nrow]
        r += nrow
        if name == "misc":
            c0 = 0
            for f, n in MISC_FIELDS:
                out[f] = blk[:, c0:c0 + n]
                c0 += n
        elif name == "b_ada":
            out[name] = blk.reshape(1, N_MOD * D_MODEL)
        elif name == "conv_w":
            out[name] = blk.reshape(CONV_K, CONV_CH)
        elif name == "conv_b":
            out[name] = blk.reshape(1, CONV_CH)
        else:
            out[name] = blk
    return out


WEIGHT_NAMES = ("norm1_w", "norm2_w", "w_ada", "b_ada", "w_in", "conv_w", "conv_b", "dt_bias", "a_log", "d_skip",
                "ssd_norm_w", "q_norm_w", "k_norm_w", "attn_norm_w", "w_out", "w_ff1", "w_ff2")
SMALL_NAMES = ("norm1_w", "norm2_w", "b_ada", "conv_b", "dt_bias", "a_log", "d_skip", "ssd_norm_w", "q_norm_w",
               "k_norm_w", "attn_norm_w")


def kernel(x, c, norm1_w, norm2_w, w_ada, b_ada, w_in, conv_w, conv_b, dt_bias, a_log, d_skip, ssd_norm_w, q_norm_w, k_norm_w, attn_norm_w, w_out, w_ff1, w_ff2, loss_target, m_norm1_w, m_norm2_w, m_w_ada, m_b_ada, m_w_in, m_conv_w, m_conv_b, m_dt_bias, m_a_log, m_d_skip, m_ssd_norm_w, m_q_norm_w, m_k_norm_w, m_attn_norm_w, m_w_out, m_w_ff1, m_w_ff2, v_norm1_w, v_norm2_w, v_w_ada, v_b_ada, v_w_in, v_conv_w, v_conv_b, v_dt_bias, v_a_log, v_d_skip, v_ssd_norm_w, v_q_norm_w, v_k_norm_w, v_attn_norm_w, v_w_out, v_w_ff1, v_w_ff2):
    args = dict(locals())
    w = {n: args[n] for n in WEIGHT_NAMES}
    m = {n: args["m_" + n] for n in WEIGHT_NAMES}
    v = {n: args["v_" + n] for n in WEIGHT_NAMES}
    me = 4 * lax.axis_index("x") + 2 * lax.axis_index("y") + lax.axis_index("c")

    c_rows = jnp.pad(c, ((0, 7), (0, 0)))
    w_in_t, m_in_t, v_in_t = [jnp.transpose(t["w_in"][0]) for t in (w, m, v)]
    c_g, conv_g, w_in_g = _exchange([c_rows, w["conv_w"][0], w_in_t.astype(BF16)], "gather_w_in", scatter=False)
    c_all = c_g[:, 0, :]
    conv_full = _shards_to_cols(conv_g)
    w_in_pt = _pack_w_in_rows(w_in_g.reshape(IN_W, D_MODEL))

    mod_part = _ada_fwd(c_all, w["w_ada"][0])
    (mod_g,) = _exchange([mod_part], "gather_mod", scatter=False)
    mod_mine = lax.dynamic_index_in_dim(mod_g, me, axis=1, keepdims=False).reshape(1, N_MOD * D_MODEL) + w["b_ada"]
    mod = mod_mine.reshape(N_MOD, D_MODEL)

    res = _local_step(x[0], loss_target[0], mod, w_in_pt, conv_full, w["conv_b"], w["dt_bias"], w["a_log"], w["d_skip"],
                      w["ssd_norm_w"], w["q_norm_w"], w["k_norm_w"], w["attn_norm_w"], w["w_out"][0].astype(BF16),
                      w["w_ff1"][0].astype(BF16), w["w_ff2"][0].astype(BF16), w["norm1_w"], w["norm2_w"], lax.axis_index("c"))

    small_vals = {n: res[n] for n in SMALL_NAMES if n != "b_ada"}
    small_vals["b_ada"] = res["dmod"]
    small_vals["conv_w"] = res["conv_w"]
    (small_g,) = _exchange([_pack_small(small_vals)], "gather_small", scatter=False)

    grads, delta, new_m, new_v = {}, {}, {}, {}
    for name in ("w_out", "w_ff1", "w_ff2"):
        outs = _reduce_adamw(res[name], w[name][0], m[name][0], v[name][0], "adamw_" + name)
        grads[name], delta[name], new_m[name], new_v[name] = [o[None] for o in outs]
    outs = _reduce_adamw(res["w_in"], w_in_t, m_in_t, v_in_t, "adamw_w_in")
    grads["w_in"], delta["w_in"], new_m["w_in"], new_v["w_in"] = [jnp.transpose(o)[None] for o in outs]

    sm = _small_reduce_adamw(small_g, _pack_small({n: w[n] for n in SMALL_NAMES}), _pack_small({n: m[n] for n in SMALL_NAMES}),
                             _pack_small({n: v[n] for n in SMALL_NAMES}))
    sm = [_unpack_small(p) for p in sm]
    for n in SMALL_NAMES:
        grads[n], delta[n], new_m[n], new_v[n] = [p[n] for p in sm]
    shard_w = CONV_CH // N_DEV
    g_conv = lax.dynamic_slice_in_dim(sm[0]["conv_w"], me * shard_w, shard_w, axis=1)
    cw = _adamw_small(g_conv, w["conv_w"][0], m["conv_w"][0], v["conv_w"][0], "adamw_conv_w")
    grads["conv_w"] = g_conv[None]
    delta["conv_w"], new_m["conv_w"], new_v["conv_w"] = [o[None] for o in cw]

    ada_w = w_ada.shape[2]
    dmod_all = small_g[:, :N_MOD, :].reshape(N_DEV, N_MOD * D_MODEL)
    dmod_cols = lax.dynamic_slice_in_dim(dmod_all, me * ada_w, ada_w, axis=1)
    outs = _ada_bwd_adamw(c_all, dmod_cols, w["w_ada"][0], m["w_ada"][0], v["w_ada"][0])
    grads["w_ada"], delta["w_ada"], new_m["w_ada"], new_v["w_ada"] = [o[None] for o in outs]

    loss = lax.psum(res["loss"][0, 0], ("x", "y", "c"))
    return (loss, res["grad_x"][None], *[grads[n] for n in WEIGHT_NAMES], *[delta[n] for n in WEIGHT_NAMES],
            *[new_m[n] for n in WEIGHT_NAMES], *[new_v[n] for n in WEIGHT_NAMES])
```

```python
import functools

import jax
import jax.numpy as jnp
from jax import lax
from jax.experimental import pallas as pl
from jax.experimental.pallas import tpu as pltpu

F32 = jnp.float32
BF16 = jnp.bfloat16
HIGHEST = lax.Precision.HIGHEST
MESH_IDS = pl.DeviceIdType.MESH

N_DEV = 8
D_MODEL = 1024
HEAD_DIM = 64
SSD_HEADS = 16
SSD_GROUPS = 4
HEADS_PER_GROUP = SSD_HEADS // SSD_GROUPS
SSD_STATE = 128
SSD_CHUNK = 128
SSD_D_INNER = SSD_HEADS * HEAD_DIM
GROUP_WIDTH = SSD_D_INNER // SSD_GROUPS
CONV_K = 4
CONV_CH = SSD_D_INNER + 2 * SSD_GROUPS * SSD_STATE
ATT_HEADS = 16
ATT_D = ATT_HEADS * HEAD_DIM
ATT_BLK = 128
DILATIONS = (1, 4, 16)
D_FF = 4 * D_MODEL
N_MOD = 6
EPS = 1e-6
IN_W = SSD_D_INNER + CONV_CH + SSD_HEADS + 3 * ATT_D
LANE = 128
OFF_Z, OFF_XBC, OFF_Q, OFF_K, OFF_V, OFF_DT = 0, 1024, 3072, 4096, 5120, 6144
IN_WP = OFF_DT + LANE

ADAM_LR, ADAM_B1, ADAM_B2, ADAM_EPS, ADAM_WD, ADAM_STEP = 0.001, 0.9, 0.999, 1e-08, 0.01, 10
VMEM_LIMIT = 56 * 1024 * 1024
ROW_TILE = 512
SMALL_ROWS = 24


def _cparams(sem=None):
    return pltpu.CompilerParams(dimension_semantics=sem, vmem_limit_bytes=VMEM_LIMIT)


def _sigmoid(v):
    return 1.0 / (1.0 + jnp.exp(-v))


def _softplus(v):
    y = jnp.exp(-jnp.abs(v))
    small = y * (1.0 - y * (0.5 - y * (1.0 / 3.0)))
    return jnp.maximum(v, 0.0) + jnp.where(y < 0.01, small, jnp.log(1.0 + y))


def _dot(a, b, dims, precision=None):
    return lax.dot_general(a, b, (dims, ((), ())), preferred_element_type=F32, precision=precision)


NN = ((1,), (0,))
NT = ((1,), (1,))
TN = ((0,), (0,))


def _matmul(a, b, *, ta=False, tb=False, tm, tn, tk, out_dtype=F32, name, mode=None, u=None, comm=None):
    m, k = (a.shape[1], a.shape[0]) if ta else a.shape
    n = b.shape[0] if tb else b.shape[1]
    assert m % tm == 0 and n % tn == 0 and k % tk == 0, (name, m, n, k)
    nk = k // tk
    a_spec = pl.BlockSpec((tk, tm), lambda i, j, kk: (kk, i)) if ta else pl.BlockSpec((tm, tk), lambda i, j, kk: (i, kk))
    b_spec = pl.BlockSpec((tn, tk), lambda i, j, kk: (j, kk)) if tb else pl.BlockSpec((tk, tn), lambda i, j, kk: (kk, j))
    o_spec = pl.BlockSpec((tm, tn), lambda i, j, kk: (i, j))
    dims = ((0,) if ta else (1,), (1,) if tb else (0,))
    n_out = 2 if mode == "relu2" else 1

    def body(*refs):
        if mode == "drelu2":
            a_ref, b_ref, u_ref = refs[:3]
            rest = refs[3:]
        else:
            a_ref, b_ref = refs[:2]
            u_ref = None
            rest = refs[2:]
        outs = rest[:n_out]
        part = _dot(a_ref[...], b_ref[...], dims)

        def finish(r):
            if mode == "relu2":
                outs[0][...] = r.astype(BF16)
                rr = jnp.maximum(r, 0.0)
                outs[1][...] = (rr * rr).astype(BF16)
            elif mode == "drelu2":
                outs[0][...] = (r * (2.0 * jnp.maximum(u_ref[...].astype(F32), 0.0))).astype(out_dtype)
            else:
                outs[0][...] = r.astype(out_dtype)

        if nk == 1:
            finish(part)
        else:
            acc = rest[n_out]
            kk = pl.program_id(2)

            @pl.when(kk == 0)
            def _():
                acc[...] = part

            @pl.when(kk > 0)
            def _():
                acc[...] += part

            @pl.when(kk == nk - 1)
            def _():
                finish(acc[...])

    in_specs = [a_spec, b_spec]
    args = [a, b]
    if mode == "drelu2":
        in_specs.append(o_spec)
        args.append(u)
    if mode == "relu2":
        out_shape = [jax.ShapeDtypeStruct((m, n), BF16), jax.ShapeDtypeStruct((m, n), BF16)]
    else:
        out_shape = [jax.ShapeDtypeStruct((m, n), out_dtype)]
    outs, comm_outs = _pcall(
        body, args, name=name, grid=(m // tm, n // tn, nk), in_specs=in_specs, out_specs=[o_spec] * n_out,
        out_shape=out_shape, scratch_shapes=[pltpu.VMEM((tm, tn), F32)] if nk > 1 else [],
        sem=("parallel", "parallel", "arbitrary"), comm=comm)
    res = tuple(outs) if mode == "relu2" else outs[0]
    return res if comm is None else (res, comm_outs)


def _rms_mod(xv, nw, scale, shift):
    r = lax.rsqrt(jnp.mean(xv * xv, axis=-1, keepdims=True) + EPS)
    return ((xv * r) * nw * (1.0 + scale) + shift).astype(BF16)


def _norm_mod_fwd(x, nw, scale, shift, name):
    s, d = x.shape
    row = pl.BlockSpec((ROW_TILE, d), lambda i: (i, 0))
    vec = pl.BlockSpec((1, d), lambda i: (0, 0))

    def body(x_ref, nw_ref, sc_ref, sh_ref, h_ref):
        h_ref[...] = _rms_mod(x_ref[...], nw_ref[...], sc_ref[...], sh_ref[...])

    return pl.pallas_call(body, name=name, grid=(s // ROW_TILE,), in_specs=[row, vec, vec, vec], out_specs=row,
                          out_shape=jax.ShapeDtypeStruct((s, d), BF16), compiler_params=_cparams(("parallel",)))(x, nw, scale, shift)


def _matmul_rows(a, b, epilogue, row_in, vec_in, outs, *, tb=False, tm, name, comm=None):
    m, k = a.shape
    n = b.shape[0] if tb else b.shape[1]
    assert m % tm == 0, (name, m, tm)
    dims = ((1,), (1,) if tb else (0,))
    n_row, n_vec = len(row_in), len(vec_in)

    def body(a_ref, b_ref, *rest):
        epilogue(_dot(a_ref[...], b_ref[...], dims), pl.program_id(0) == 0, rest[:n_row], rest[n_row:n_row + n_vec],
                 rest[n_row + n_vec:])

    def spec(kind, width):
        block = {"row": (tm, width), "vec": (1, width), "one": (1, 1)}[kind]
        return pl.BlockSpec(block, (lambda i: (i, 0)) if kind == "row" else (lambda i: (0, 0)))

    def shape(kind, width):
        return {"row": (m, width), "vec": (1, width), "one": (1, 1)}[kind]

    outs = [(o[0], o[1], o[2] if len(o) > 2 else n) for o in outs]
    res, comm_outs = _pcall(
        body, [a, b, *row_in, *vec_in], name=name, grid=(m // tm,),
        in_specs=[pl.BlockSpec((tm, k), lambda i: (i, 0)), pl.BlockSpec(b.shape, lambda i: (0, 0))]
        + [spec("row", r.shape[1]) for r in row_in] + [spec("vec", v.shape[1]) for v in vec_in],
        out_specs=[spec(kind, width) for kind, _, width in outs],
        out_shape=[jax.ShapeDtypeStruct(shape(kind, width), dt) for kind, dt, width in outs],
        sem=("arbitrary",), comm=comm)
    return res if comm is None else (res, comm_outs)


def _residual_norm_epilogue(mix, first, rows, vecs, outs):
    (x_ref,), (gate_ref, nw_ref, sc_ref, sh_ref), (mix_ref, x1_ref, h_ref) = rows, vecs, outs
    xv = x_ref[...] + gate_ref[...] * mix
    mix_ref[...] = mix
    x1_ref[...] = xv
    h_ref[...] = _rms_mod(xv, nw_ref[...], sc_ref[...], sh_ref[...])


def _loss_epilogue(ff, first, rows, vecs, outs):
    (x1_ref, t_ref), (g_ref,), (loss_ref, dout_ref, dff_ref, dg_ref) = rows, vecs, outs
    d = ff.shape[1]

    @pl.when(first)
    def _():
        loss_ref[...] = jnp.zeros_like(loss_ref)
        dg_ref[...] = jnp.zeros_like(dg_ref)

    err = x1_ref[...] + g_ref[...] * ff - t_ref[...]
    loss_ref[...] += (0.5 / d) * jnp.sum(err * err).reshape(1, 1)
    dout = err * (1.0 / d)
    dout_ref[...] = dout
    dff_ref[...] = (g_ref[...] * dout).astype(BF16)
    dg_ref[...] += jnp.sum(dout * ff, axis=0, keepdims=True)


def _norm_bwd_epilogue(dh, first, rows, vecs, outs):
    with_gate = len(vecs) == 3
    x_ref, dres_ref = rows[:2]
    nw_ref, sc_ref = vecs[:2]
    dx_ref, dsh_ref, dsc_ref, dnw_ref = outs[:4]

    @pl.when(first)
    def _():
        for ref in outs[1:4] + outs[5:]:
            ref[...] = jnp.zeros_like(ref)

    xv = x_ref[...]
    r = lax.rsqrt(jnp.mean(xv * xv, axis=-1, keepdims=True) + EPS)
    nrm = xv * r
    one_sc = 1.0 + sc_ref[...]
    dhn = dh * nrm
    dsh_ref[...] += jnp.sum(dh, axis=0, keepdims=True)
    dsc_ref[...] += jnp.sum(dhn, axis=0, keepdims=True) * nw_ref[...]
    dnw_ref[...] += jnp.sum(dhn, axis=0, keepdims=True) * one_sc
    dn = dh * (nw_ref[...] * one_sc)
    dx = dres_ref[...] + r * (dn - nrm * jnp.mean(dn * nrm, axis=-1, keepdims=True))
    dx_ref[...] = dx
    if with_gate:
        outs[4][...] = (vecs[2][...] * dx).astype(BF16)
        outs[5][...] += jnp.sum(dx * rows[2][...], axis=0, keepdims=True)


CONV_COLS = 256
CONV_FWD_ROWS = 2048
HALO = 8


def _shift_down(cur, halo, k):
    if k == 0:
        return cur
    rolled = pltpu.roll(cur, k, axis=0)
    top = jnp.where(lax.broadcasted_iota(jnp.int32, halo.shape, 0) < k, pltpu.roll(halo, k, axis=0), rolled[:HALO])
    return jnp.concatenate([top, rolled[HALO:]], axis=0)


def _shift_up(cur, halo, k):
    if k == 0:
        return cur
    t = cur.shape[0]
    rolled = pltpu.roll(cur, t - k, axis=0)
    bot = jnp.where(lax.broadcasted_iota(jnp.int32, halo.shape, 0) >= HALO - k, pltpu.roll(halo, HALO - k, axis=0),
                    rolled[t - HALO:])
    return jnp.concatenate([rolled[:t - HALO], bot], axis=0)


def _conv_fwd(proj, conv_w, conv_b):
    s = proj.shape[0]
    nr = s // CONV_FWD_ROWS
    cb0 = OFF_XBC // CONV_COLS
    hb = CONV_FWD_ROWS // HALO
    cur = pl.BlockSpec((CONV_FWD_ROWS, CONV_COLS), lambda j, r: (r, cb0 + j))
    prev = pl.BlockSpec((HALO, CONV_COLS), lambda j, r: (jnp.maximum(r * hb - 1, 0), cb0 + j))
    out = pl.BlockSpec((CONV_FWD_ROWS, CONV_COLS), lambda j, r: (r, j))

    def body(u_ref, up_ref, w_ref, b_ref, pre_ref, act_ref):
        r = pl.program_id(1)
        u = u_ref[...]
        halo = jnp.where(r > 0, up_ref[...], 0.0)
        acc = b_ref[...] + w_ref[CONV_K - 1:CONV_K, :] * u
        for k in range(1, CONV_K):
            acc = acc + w_ref[CONV_K - 1 - k:CONV_K - k, :] * _shift_down(u, halo, k)
        pre_ref[...] = acc
        act_ref[...] = acc * _sigmoid(acc)

    return pl.pallas_call(
        body, name="conv_fwd", grid=(CONV_CH // CONV_COLS, nr),
        in_specs=[cur, prev, pl.BlockSpec((CONV_K, CONV_COLS), lambda j, r: (0, j)),
                  pl.BlockSpec((1, CONV_COLS), lambda j, r: (0, j))],
        out_specs=[out, out],
        out_shape=[jax.ShapeDtypeStruct((s, CONV_CH), F32), jax.ShapeDtypeStruct((s, CONV_CH), F32)],
        compiler_params=_cparams(("parallel", "arbitrary")))(proj, proj, conv_w, conv_b)


def _conv_bwd(dact, pre, proj, conv_w):
    s = proj.shape[0]
    nr = s // ROW_TILE
    cb0 = OFF_XBC // CONV_COLS
    hb = ROW_TILE // HALO
    last_halo = s // HALO - 1
    cur = pl.BlockSpec((ROW_TILE, CONV_COLS), lambda j, r: (r, j))
    nxt = pl.BlockSpec((HALO, CONV_COLS), lambda j, r: (jnp.minimum((r + 1) * hb, last_halo), j))
    ucur = pl.BlockSpec((ROW_TILE, CONV_COLS), lambda j, r: (r, cb0 + j))
    uprev = pl.BlockSpec((HALO, CONV_COLS), lambda j, r: (jnp.maximum(r * hb - 1, 0), cb0 + j))
    wspec = pl.BlockSpec((CONV_K, CONV_COLS), lambda j, r: (0, j))
    bspec = pl.BlockSpec((1, CONV_COLS), lambda j, r: (0, j))

    def dsilu(p):
        sg = _sigmoid(p)
        return sg * (1.0 + p * (1.0 - sg))

    def body(da_ref, dan_ref, pre_ref, pren_ref, u_ref, up_ref, w_ref, du_ref, dw_ref, db_ref):
        r = pl.program_id(1)

        @pl.when(r == 0)
        def _():
            dw_ref[...] = jnp.zeros_like(dw_ref)
            db_ref[...] = jnp.zeros_like(db_ref)

        dpre = da_ref[...] * dsilu(pre_ref[...])
        dnext = jnp.where(r < nr - 1, dan_ref[...] * dsilu(pren_ref[...]), 0.0)
        u = u_ref[...]
        halo = jnp.where(r > 0, up_ref[...], 0.0)
        du = w_ref[CONV_K - 1:CONV_K, :] * dpre
        dws = [jnp.sum(dpre * u, axis=0, keepdims=True)]
        for k in range(1, CONV_K):
            du = du + w_ref[CONV_K - 1 - k:CONV_K - k, :] * _shift_up(dpre, dnext, k)
            dws.append(jnp.sum(dpre * _shift_down(u, halo, k), axis=0, keepdims=True))
        du_ref[...] = du.astype(BF16)
        dw_ref[...] += jnp.concatenate(dws[::-1], axis=0)
        db_ref[...] += jnp.sum(dpre, axis=0, keepdims=True)

    return pl.pallas_call(
        body, name="conv_bwd", grid=(CONV_CH // CONV_COLS, nr),
        in_specs=[cur, nxt, cur, nxt, ucur, uprev, wspec],
        out_specs=[cur, wspec, bspec],
        out_shape=[jax.ShapeDtypeStruct((s, CONV_CH), BF16), jax.ShapeDtypeStruct((CONV_K, CONV_CH), F32),
                   jax.ShapeDtypeStruct((1, CONV_CH), F32)],
        compiler_params=_cparams(("parallel", "arbitrary")))(dact, dact, pre, pre, proj, proj, conv_w)


def _ssd_common(dtr, dtb, alog):
    lane = lax.broadcasted_iota(jnp.int32, (1, LANE), 1)
    head_lane = lane < SSD_HEADS
    dt = jnp.where(head_lane, _softplus(dtr + dtb), 0.0)
    a = jnp.where(head_lane, -jnp.exp(alog), 0.0)
    row = lax.broadcasted_iota(jnp.int32, (SSD_CHUNK, SSD_CHUNK), 0)
    col = lax.broadcasted_iota(jnp.int32, (SSD_CHUNK, SSD_CHUNK), 1)
    tril = row >= col
    cs = _dot(tril.astype(F32), dt * a, NN, precision=HIGHEST)
    return dt, a, cs, cs.T, tril, lane


def _split_bf16(v, passes):
    terms, rest = [], v
    for _ in range(passes):
        t = rest.astype(BF16)
        terms.append(t)
        rest = rest - t.astype(F32)
    return terms


def _dot_split(v, m, dims, passes):
    terms = _split_bf16(v, passes)
    if passes == 1:
        return _dot(terms[0], m, dims)
    return _dot(jnp.concatenate(terms, axis=1), jnp.concatenate([m] * passes, axis=0 if dims == NN else 1), dims)


def _ssd_constants():
    heads = jnp.arange(LANE)[:, None]
    exp_mat = (heads == (jnp.arange(SSD_D_INNER)[None, :] // HEAD_DIM)).astype(BF16)
    ind4 = ((jnp.arange(SSD_HEADS * SSD_CHUNK)[:, None] // SSD_CHUNK) == jnp.arange(LANE)[None, :]).astype(BF16)
    return exp_mat, ind4


def _expand_heads(v):
    return jnp.repeat(v[:, :SSD_HEADS], HEAD_DIM, axis=1)


def _ssd_prep(dtr, dtb, alog, exp_mat):
    dt, a, cs, cst, tril, lane = _ssd_common(dtr, dtb, alog)
    return dt, a, cs, cst, tril, lane, _dot_split(dt, exp_mat, NN, 2), _dot_split(cs, exp_mat, NN, 3)


def _chunk_decay_rows(cs, g):
    parts = []
    for e in range(HEADS_PER_GROUP):
        h = g * HEADS_PER_GROUP + e
        parts.append(jnp.broadcast_to(jnp.exp(cs[SSD_CHUNK - 1:SSD_CHUNK, h:h + 1]), (HEAD_DIM, SSD_STATE)))
    return jnp.concatenate(parts, axis=0)


def _ssd_fwd(proj, act, dtb, alog, dsk, nw):
    s = proj.shape[0]
    nc = s // SSD_CHUNK
    bc_w = SSD_GROUPS * SSD_STATE
    exp_mat, _ = _ssd_constants()

    def body(z_ref, dtr_ref, xs_ref, b_ref, c_ref, dtb_ref, alog_ref, dskx_ref, nw_ref, exp_ref,
             ypre_ref, yssd_ref, hall_ref, h_scr):
        @pl.when(pl.program_id(0) == 0)
        def _():
            h_scr[...] = jnp.zeros_like(h_scr)

        dt, a, cs, cst, tril, lane, dtx, csx = _ssd_prep(dtr_ref[...], dtb_ref[...], alog_ref[...], exp_ref[...])
        cs_last_x = csx[SSD_CHUNK - 1:SSD_CHUNK, :]
        xs = xs_ref[...]
        xdt = xs * dtx
        xdtb = xdt.astype(BF16)
        xdec = (xdt * jnp.exp(cs_last_x - csx)).astype(BF16)
        ecsx = jnp.exp(csx)
        head_of_lane = lax.broadcasted_iota(jnp.int32, (1, GROUP_WIDTH), 1) // HEAD_DIM
        for g in range(SSD_GROUPS):
            gs = slice(g * GROUP_WIDTH, (g + 1) * GROUP_WIDTH)
            bg = b_ref[:, g * SSD_STATE:(g + 1) * SSD_STATE].astype(BF16)
            cg = c_ref[:, g * SSD_STATE:(g + 1) * SSD_STATE].astype(BF16)
            cb = _dot(cg, bg, NT)
            hprev = h_scr[gs, :]
            hall_ref[0, gs, :] = hprev
            gms, rhs = [], []
            xg = xdtb[:, gs]
            for e in range(HEADS_PER_GROUP):
                h = g * HEADS_PER_GROUP + e
                lm = jnp.exp(jnp.where(tril, cs[:, h:h + 1] - cst[h:h + 1, :], -1e30))
                gms.append((cb * lm).astype(BF16))
                rhs.append(jnp.where(head_of_lane == e, xg, jnp.zeros_like(xg)))
            y = _dot(jnp.concatenate(gms, axis=1), jnp.concatenate(rhs, axis=0), NN)
            y = y + ecsx[:, gs] * _dot(cg, hprev.astype(BF16), NT)
            y = y + dskx_ref[:, gs] * xs[:, gs]
            h_scr[gs, :] = hprev * _chunk_decay_rows(cs, g) + _dot(xdec[:, gs], bg, TN)
            ypre_ref[:, gs] = y
            z = z_ref[:, gs]
            yg = y * (z * _sigmoid(z))
            r = lax.rsqrt(jnp.mean(yg * yg, axis=-1, keepdims=True) + EPS)
            yssd_ref[:, gs] = (yg * r * nw_ref[:, gs]).astype(BF16)

    row_d = lambda cb: pl.BlockSpec((SSD_CHUNK, SSD_D_INNER), lambda c: (c, cb))
    small = pl.BlockSpec((1, LANE), lambda c: (0, 0))
    wide = pl.BlockSpec((1, SSD_D_INNER), lambda c: (0, 0))
    return pl.pallas_call(
        body, name="ssd_fwd", grid=(nc,),
        in_specs=[row_d(OFF_Z // SSD_D_INNER),
                  pl.BlockSpec((SSD_CHUNK, LANE), lambda c: (c, OFF_DT // LANE)),
                  row_d(0),
                  pl.BlockSpec((SSD_CHUNK, bc_w), lambda c: (c, SSD_D_INNER // bc_w)),
                  pl.BlockSpec((SSD_CHUNK, bc_w), lambda c: (c, SSD_D_INNER // bc_w + 1)),
                  small, small, wide, wide, pl.BlockSpec((LANE, SSD_D_INNER), lambda c: (0, 0))],
        out_specs=[row_d(0), row_d(0), pl.BlockSpec((1, SSD_D_INNER, SSD_STATE), lambda c: (c, 0, 0))],
        out_shape=[jax.ShapeDtypeStruct((s, SSD_D_INNER), F32), jax.ShapeDtypeStruct((s, SSD_D_INNER + ATT_D), BF16),
                   jax.ShapeDtypeStruct((nc, SSD_D_INNER, SSD_STATE), F32)],
        scratch_shapes=[pltpu.VMEM((SSD_D_INNER, SSD_STATE), F32)],
        compiler_params=_cparams(("arbitrary",)))(proj, proj, act, act, act, dtb, alog, _expand_heads(dsk), nw, exp_mat)


def _ssd_bwd(dycat, ypre, proj, act, hall, dtb, alog, dsk, nw, comm=None):
    s = proj.shape[0]
    nc = s // SSD_CHUNK
    bc_w = SSD_GROUPS * SSD_STATE

    exp_mat, ind4 = _ssd_constants()
    seg_passes = 1

    def body(dy_ref, ypre_ref, z_ref, dtr_ref, xs_ref, b_ref, c_ref, hall_ref, dtb_ref, alog_ref, dskx_ref, nw_ref,
             exp_ref, ind4_ref, dz_ref, dact_ref, ddtr_ref, da_ref, ddsk_ref, ddtb_ref, dnw_ref, dh_scr):
        @pl.when(pl.program_id(0) == 0)
        def _():
            dh_scr[...] = jnp.zeros_like(dh_scr)
            da_ref[...] = jnp.zeros_like(da_ref)
            ddsk_ref[...] = jnp.zeros_like(ddsk_ref)
            ddtb_ref[...] = jnp.zeros_like(ddtb_ref)
            dnw_ref[...] = jnp.zeros_like(dnw_ref)

        dtr = dtr_ref[...]
        dt, a, cs, cst, tril, lane, dtx, csx = _ssd_prep(dtr, dtb_ref[...], alog_ref[...], exp_ref[...])
        cs_last_x = csx[SSD_CHUNK - 1:SSD_CHUNK, :]
        xs = xs_ref[...]
        xdt = xs * dtx
        xdtb = xdt.astype(BF16)
        decx = jnp.exp(cs_last_x - csx)
        xdecf = xdt * decx
        xdec = xdecf.astype(BF16)
        ecsx = jnp.exp(csx)
        head_of_lane = lax.broadcasted_iota(jnp.int32, (1, GROUP_WIDTH), 1) // HEAD_DIM
        last_row = lax.broadcasted_iota(jnp.int32, (SSD_CHUNK, 1), 0) == SSD_CHUNK - 1
        dcs_col = jnp.zeros((SSD_CHUNK, LANE), F32)
        dcs_row = jnp.zeros((SSD_CHUNK, LANE), F32)
        ddt = jnp.zeros((SSD_CHUNK, LANE), F32)
        ddsk = jnp.zeros((1, LANE), F32)
        hsum = jnp.zeros((1, LANE), F32)
        t1_sum = jnp.zeros((1, LANE), F32)
        for g in range(SSD_GROUPS):
            gs = slice(g * GROUP_WIDTH, (g + 1) * GROUP_WIDTH)
            bsl = slice(g * SSD_STATE, (g + 1) * SSD_STATE)
            exp_g = exp_ref[:, gs]
            ind4_g = ind4_ref[g * HEADS_PER_GROUP * SSD_CHUNK:(g + 1) * HEADS_PER_GROUP * SSD_CHUNK, :]
            z = z_ref[:, gs]
            sg = _sigmoid(z)
            sz = z * sg
            ypre = ypre_ref[:, gs]
            yg = ypre * sz
            r = lax.rsqrt(jnp.mean(yg * yg, axis=-1, keepdims=True) + EPS)
            nrm = yg * r
            dyo_n = dy_ref[:, gs]
            dnw_ref[:, gs] += jnp.sum(dyo_n * nrm, axis=0, keepdims=True)
            dn = dyo_n * nw_ref[:, gs]
            dyg = r * (dn - nrm * jnp.mean(dn * nrm, axis=-1, keepdims=True))
            dz_ref[:, gs] = (dyg * ypre * (sg * (1.0 + z * (1.0 - sg)))).astype(BF16)
            dy = dyg * sz

            bg = b_ref[:, bsl].astype(BF16)
            cg = c_ref[:, bsl].astype(BF16)
            cb = _dot(cg, bg, NT)
            hprev = hall_ref[0, gs, :]
            hb = hprev.astype(BF16)
            dhn = dh_scr[gs, :]
            dhb = dhn.astype(BF16)
            xs_g, xdt_g = xs[:, gs], xdtb[:, gs]
            w_off = _dot(cg, hb, NT)
            dyo = dy * ecsx[:, gs]
            dyob = dyo.astype(BF16)
            dcg = _dot(dyob, hb, NN)
            dh_y = _dot(dyob, cg, TN)
            r_st = _dot(bg, dhb, NT)
            dbg = _dot(xdec[:, gs], dhb, NN)
            dyb = dy.astype(BF16)
            gms, gmbs, lms, dys = [], [], [], []
            for e in range(HEADS_PER_GROUP):
                h = g * HEADS_PER_GROUP + e
                lm = jnp.exp(jnp.where(tril, cs[:, h:h + 1] - cst[h:h + 1, :], -1e30))
                gm = cb * lm
                lms.append(lm)
                gms.append(gm)
                gmbs.append(gm.astype(BF16))
                dys.append(jnp.where(head_of_lane == e, dyb, jnp.zeros_like(dyb)))
            dxdt = _dot(jnp.concatenate(gmbs, axis=0), jnp.concatenate(dys, axis=0), TN) + decx[:, gs] * r_st
            dcb = jnp.zeros((SSD_CHUNK, SSD_CHUNK), F32)
            mms = []
            for e in range(HEADS_PER_GROUP):
                dg = _dot(dys[e], xdt_g, NT)
                mms.append(dg * gms[e])
                dcb = dcb + dg * lms[e]
            seg = _dot_split(jnp.concatenate([dyo * w_off, xdecf[:, gs] * r_st, dxdt * xs_g, dy * xs_g], axis=0), exp_g, NT, seg_passes)
            v1, t1, ddt_g, dsk_g = [seg[i * SSD_CHUNK:(i + 1) * SSD_CHUNK] for i in range(4)]
            dcs_col = dcs_col + v1 - t1 + _dot_split(jnp.concatenate(mms, axis=1), ind4_g, NN, seg_passes)
            for t in _split_bf16(jnp.concatenate(mms, axis=0), seg_passes):
                dcs_row = dcs_row + _dot(ind4_g, t, TN)
            ddt = ddt + ddt_g
            ddsk = ddsk + jnp.sum(dsk_g, axis=0, keepdims=True)
            t1_sum = t1_sum + jnp.sum(t1, axis=0, keepdims=True)
            for e in range(HEADS_PER_GROUP):
                h = g * HEADS_PER_GROUP + e
                hs = slice(e * HEAD_DIM, (e + 1) * HEAD_DIM)
                hsum = hsum + jnp.where(lane == h, jnp.sum(dhn[hs, :] * hprev[hs, :]).reshape(1, 1), 0.0)
            dh_scr[gs, :] = dhn * _chunk_decay_rows(cs, g) + dh_y
            dcbb = dcb.astype(BF16)
            dact_ref[:, gs] = dxdt * dtx[:, gs] + dskx_ref[:, gs] * dy
            dact_ref[:, SSD_D_INNER + g * SSD_STATE:SSD_D_INNER + (g + 1) * SSD_STATE] = dbg + _dot(dcbb, cg, TN)
            dact_ref[:, SSD_D_INNER + bc_w + g * SSD_STATE:SSD_D_INNER + bc_w + (g + 1) * SSD_STATE] = dcg + _dot(dcbb, bg, NN)
        dlast = t1_sum + jnp.exp(cs[SSD_CHUNK - 1:SSD_CHUNK, :]) * hsum
        dcs = dcs_col - dcs_row.T + jnp.where(last_row, dlast, 0.0)
        row = lax.broadcasted_iota(jnp.int32, (SSD_CHUNK, SSD_CHUNK), 0)
        col = lax.broadcasted_iota(jnp.int32, (SSD_CHUNK, SSD_CHUNK), 1)
        dda = _dot((col >= row).astype(F32), dcs, NN, precision=HIGHEST)
        ddt = ddt + dda * a
        da_ref[...] += jnp.sum(dda * dt, axis=0, keepdims=True)
        ddtr = jnp.where(lane < SSD_HEADS, ddt * _sigmoid(dtr + dtb_ref[...]), 0.0)
        ddtr_ref[...] = ddtr.astype(BF16)
        ddtb_ref[...] += jnp.sum(ddtr, axis=0, keepdims=True)
        ddsk_ref[...] += ddsk

    rev = lambda c: nc - 1 - c
    row_d = lambda cb: pl.BlockSpec((SSD_CHUNK, SSD_D_INNER), lambda c: (rev(c), cb))
    small = pl.BlockSpec((1, LANE), lambda c: (0, 0))
    wide = pl.BlockSpec((1, SSD_D_INNER), lambda c: (0, 0))
    small_shape = jax.ShapeDtypeStruct((1, LANE), F32)
    return _pcall(
        body, (dycat, ypre, proj, proj, act, act, act, hall, dtb, alog, _expand_heads(dsk), nw, exp_mat, ind4),
        name="ssd_bwd", grid=(nc,),
        in_specs=[row_d(0), row_d(0), row_d(OFF_Z // SSD_D_INNER),
                  pl.BlockSpec((SSD_CHUNK, LANE), lambda c: (rev(c), OFF_DT // LANE)),
                  row_d(0),
                  pl.BlockSpec((SSD_CHUNK, bc_w), lambda c: (rev(c), SSD_D_INNER // bc_w)),
                  pl.BlockSpec((SSD_CHUNK, bc_w), lambda c: (rev(c), SSD_D_INNER // bc_w + 1)),
                  pl.BlockSpec((1, SSD_D_INNER, SSD_STATE), lambda c: (rev(c), 0, 0)),
                  small, small, wide, wide, pl.BlockSpec((LANE, SSD_D_INNER), lambda c: (0, 0)),
                  pl.BlockSpec((SSD_HEADS * SSD_CHUNK, LANE), lambda c: (0, 0))],
        out_specs=[row_d(0), pl.BlockSpec((SSD_CHUNK, CONV_CH), lambda c: (rev(c), 0)),
                   pl.BlockSpec((SSD_CHUNK, LANE), lambda c: (rev(c), 0)), small, small, small, wide],
        out_shape=[jax.ShapeDtypeStruct((s, SSD_D_INNER), BF16), jax.ShapeDtypeStruct((s, CONV_CH), F32),
                   jax.ShapeDtypeStruct((s, LANE), BF16), small_shape, small_shape, small_shape,
                   jax.ShapeDtypeStruct((1, SSD_D_INNER), F32)],
        scratch_shapes=[pltpu.VMEM((SSD_D_INNER, SSD_STATE), F32)], sem=("arbitrary",), comm=comm)


def _head_mean_matrix():
    row = lax.broadcasted_iota(jnp.int32, (LANE, LANE), 0) // HEAD_DIM
    col = lax.broadcasted_iota(jnp.int32, (LANE, LANE), 1) // HEAD_DIM
    return (row == col).astype(F32)


def _head_sum2(v, ones_bd):
    hi = v.astype(BF16)
    lo = (v - hi.astype(F32)).astype(BF16)
    return _dot(jnp.concatenate([hi, lo], axis=1), jnp.concatenate([ones_bd, ones_bd], axis=0), NN)


def _head_norm(x, w, scale, ones_bd):
    ms = _head_sum2(x * x, ones_bd) * (1.0 / HEAD_DIM)
    return (x * lax.rsqrt(ms + EPS)) * (w * scale)


PRO_ROWS = 256
ATT_GROUP_FWD = 16
ATT_GROUP_BWD = 8
KEYS = 2 * ATT_BLK
NEG = -1e30
HALF = HEAD_DIM // 2


def _rows(start, size, dil):
    return pl.ds(start, size) if dil == 1 else pl.ds(start, size, stride=dil)


def _fill_bias(bias_ref):
    row = lax.broadcasted_iota(jnp.int32, (ATT_BLK, 2 * KEYS), 0)
    col = lax.broadcasted_iota(jnp.int32, (ATT_BLK, 2 * KEYS), 1) & (KEYS - 1)
    for first, off in ((0, 0), (1, ATT_BLK)):
        dist = off + row - col
        bias_ref[first] = jnp.where((dist >= 0) & (dist <= ATT_BLK), 0.0, NEG)


def _pair(a, b):
    return jnp.concatenate([jnp.broadcast_to(a, (ATT_BLK, KEYS)), jnp.broadcast_to(b, (ATT_BLK, KEYS))], axis=1)


def _split_heads(x, is_a):
    zero = jnp.zeros_like(x)
    return jnp.concatenate([jnp.where(is_a, x, zero), jnp.where(is_a, zero, x)], axis=0)


def _block_ids(b, nb):
    i = b & (nb - 1)
    q0 = pl.multiple_of(b * ATT_BLK, ATT_BLK)
    k0 = pl.multiple_of((b - jnp.minimum(i, 1)) * ATT_BLK, ATT_BLK)
    return pl.ds(q0, ATT_BLK), pl.ds(k0, KEYS), jnp.minimum(i, 1)


def _att_fwd(proj, qw, kw, comm=None):
    s = proj.shape[0]
    nblk = s // ATT_BLK
    assert all((s // d) // ATT_BLK >= 2 for d in DILATIONS)
    blk = lambda off: pl.BlockSpec((s, LANE), lambda i: (0, off // LANE + i))
    wspec = pl.BlockSpec((1, LANE), lambda i: (0, i))
    oblk = pl.BlockSpec((s, LANE), lambda i: (0, i))

    def body(q_ref, k_ref, v_ref, qw_ref, kw_ref, o_ref, lse_ref, qn, kn, q_cm, k_cm, v_cm, m_acc, l_acc, o_d, m_d, l_d, bias):
        ones_bd = _head_mean_matrix().astype(BF16)
        is_a = lax.broadcasted_iota(jnp.int32, (1, LANE), 1) < HEAD_DIM
        ones_ext = _split_heads(jnp.ones((KEYS, LANE), BF16), is_a)
        _fill_bias(bias)

        def pro(j, c):
            rows = pl.ds(pl.multiple_of(j * PRO_ROWS, PRO_ROWS), PRO_ROWS)
            qn[rows, :] = _head_norm(q_ref[rows, :], qw_ref[...], HEAD_DIM ** -0.5, ones_bd)
            kn[rows, :] = _head_norm(k_ref[rows, :], kw_ref[...], 1.0, ones_bd)
            return c

        lax.fori_loop(0, s // PRO_ROWS, pro, 0)

        for dil in DILATIONS:
            ln = s // dil
            nb = ln // ATT_BLK
            o_out, m_out, l_out = (o_ref, m_acc, l_acc) if dil == 1 else (o_d, m_d, l_d)
            for r in range(dil):
                def relayout(j, c, dil=dil, r=r, ln=ln):
                    j0 = pl.multiple_of(j * PRO_ROWS, PRO_ROWS)
                    src = _rows(r + dil * j0, PRO_ROWS, dil)
                    dst = pl.ds(r * ln + j0, PRO_ROWS)
                    q_cm[dst, :] = qn[src, :].astype(BF16)
                    k_cm[dst, :] = kn[src, :].astype(BF16)
                    v_cm[dst, :] = v_ref[src, :].astype(BF16)
                    return c

                lax.fori_loop(0, ln // PRO_ROWS, relayout, 0)

            def step(bg, c, nb=nb, o_out=o_out, m_out=m_out, l_out=l_out):
                ids = [_block_ids(bg * ATT_GROUP_FWD + u, nb) for u in range(ATT_GROUP_FWD)]
                kbs = [_split_heads(k_cm[krows, :], is_a) for _, krows, _ in ids]
                scs = [_dot(q_cm[qrows, :], kb, NT) + bias[first] for (qrows, _, first), kb in zip(ids, kbs)]
                mas = [jnp.max(sc[:, :KEYS], axis=-1, keepdims=True) for sc in scs]
                mbs = [jnp.max(sc[:, KEYS:], axis=-1, keepdims=True) for sc in scs]
                ps = [jnp.exp(sc - _pair(ma, mb)).astype(BF16) for sc, ma, mb in zip(scs, mas, mbs)]
                vbs = [jnp.concatenate([_split_heads(v_cm[krows, :], is_a), ones_ext], axis=1) for _, krows, _ in ids]
                ols = [_dot(p, vb, NN) for p, vb in zip(ps, vbs)]
                for (qrows, _, _), ol, ma, mb in zip(ids, ols, mas, mbs):
                    o_out[qrows, :] = ol[:, :LANE]
                    l_out[qrows, :] = ol[:, LANE:]
                    m_out[qrows, :] = jnp.where(is_a, ma, mb)
                return c

            lax.fori_loop(0, nblk // ATT_GROUP_FWD, step, 0)

            if dil > 1:
                for r in range(dil):
                    def merge(j, c, dil=dil, r=r, ln=ln):
                        j0 = pl.multiple_of(j * PRO_ROWS, PRO_ROWS)
                        nat = _rows(r + dil * j0, PRO_ROWS, dil)
                        cm = pl.ds(r * ln + j0, PRO_ROWS)
                        m_old, m_new = m_acc[nat, :], m_d[cm, :]
                        m = jnp.maximum(m_old, m_new)
                        a_old, a_new = jnp.exp(m_old - m), jnp.exp(m_new - m)
                        o_ref[nat, :] = a_old * o_ref[nat, :] + a_new * o_d[cm, :]
                        l_acc[nat, :] = a_old * l_acc[nat, :] + a_new * l_d[cm, :]
                        m_acc[nat, :] = m
                        return c

                    lax.fori_loop(0, ln // PRO_ROWS, merge, 0)

        def epi(j, c):
            rows = pl.ds(pl.multiple_of(j * PRO_ROWS, PRO_ROWS), PRO_ROWS)
            l = l_acc[rows, :]
            o_ref[rows, :] = o_ref[rows, :] / l
            lse_ref[rows, :] = m_acc[rows, :] + jnp.log(l)
            return c

        lax.fori_loop(0, s // PRO_ROWS, epi, 0)

    f = jax.ShapeDtypeStruct((s, ATT_D), F32)
    scr = pltpu.VMEM((s, LANE), F32)
    scb = pltpu.VMEM((s, LANE), BF16)
    return _pcall(
        body, (proj, proj, proj, qw, kw), name="att_fwd", grid=(ATT_D // LANE,),
        in_specs=[blk(OFF_Q), blk(OFF_K), blk(OFF_V), wspec, wspec], out_specs=[oblk, oblk], out_shape=[f, f],
        scratch_shapes=[scr, scr, scb, scb, scb, scr, scr, scr, scr, scr, pltpu.VMEM((2, ATT_BLK, 2 * KEYS), F32)],
        sem=("parallel",), comm=comm)


def _att_bwd(proj, do, stats, qw, kw, comm=None):
    s = proj.shape[0]
    nblk = s // ATT_BLK
    blk = lambda off: pl.BlockSpec((s, LANE), lambda i: (0, off // LANE + i))
    wspec = pl.BlockSpec((1, LANE), lambda i: (0, i))
    oblk = pl.BlockSpec((s, LANE), lambda i: (0, i))

    def body(q_ref, k_ref, v_ref, do_ref, st_ref, qw_ref, kw_ref, dq_ref, dk_ref, dv_ref, dqw_ref, dkw_ref,
             qn, kn, q_cm, do_cm, k_cm, v_cm, st_cm, dq_acc, dk_acc, dv_acc, dq_d, dk_d, dv_d, bias):
        ones_bd = _head_mean_matrix().astype(BF16)
        is_a = lax.broadcasted_iota(jnp.int32, (1, LANE), 1) < HEAD_DIM
        _fill_bias(bias)
        zero = jnp.zeros((PRO_ROWS, LANE), F32)

        def pro(j, c):
            rows = pl.ds(pl.multiple_of(j * PRO_ROWS, PRO_ROWS), PRO_ROWS)
            qn[rows, :] = _head_norm(q_ref[rows, :], qw_ref[...], HEAD_DIM ** -0.5, ones_bd)
            kn[rows, :] = _head_norm(k_ref[rows, :], kw_ref[...], 1.0, ones_bd)
            dk_acc[rows, :] = zero
            dv_acc[rows, :] = zero
            return c

        lax.fori_loop(0, s // PRO_ROWS, pro, 0)

        for dil in DILATIONS:
            ln = s // dil
            nb = ln // ATT_BLK
            dq_o, dk_o, dv_o = (dq_acc, dk_acc, dv_acc) if dil == 1 else (dq_d, dk_d, dv_d)
            for r in range(dil):
                def relayout(j, c, dil=dil, r=r, ln=ln):
                    j0 = pl.multiple_of(j * PRO_ROWS, PRO_ROWS)
                    src = _rows(r + dil * j0, PRO_ROWS, dil)
                    dst = pl.ds(r * ln + j0, PRO_ROWS)
                    q_cm[dst, :] = qn[src, :].astype(BF16)
                    k_cm[dst, :] = kn[src, :].astype(BF16)
                    v_cm[dst, :] = v_ref[src, :].astype(BF16)
                    do_cm[dst, :] = do_ref[src, :].astype(BF16)
                    st_cm[dst, :] = st_ref[src, :]
                    if dil > 1:
                        dk_d[dst, :] = zero
                        dv_d[dst, :] = zero
                    return c

                lax.fori_loop(0, ln // PRO_ROWS, relayout, 0)

            def step(bg, c, nb=nb, dq_o=dq_o, dk_o=dk_o, dv_o=dv_o):
                ids = [_block_ids(bg * ATT_GROUP_BWD + u, nb) for u in range(ATT_GROUP_BWD)]
                qbs = [q_cm[qrows, :] for qrows, _, _ in ids]
                dobs = [do_cm[qrows, :] for qrows, _, _ in ids]
                kbs = [_split_heads(k_cm[krows, :], is_a) for _, krows, _ in ids]
                vbs = [_split_heads(v_cm[krows, :], is_a) for _, krows, _ in ids]
                sts = [st_cm[qrows, :] for qrows, _, _ in ids]
                scs = [_dot(qb, kb, NT) + bias[first] for qb, kb, (_, _, first) in zip(qbs, kbs, ids)]
                dps = [_dot(dob, vb, NT) for dob, vb in zip(dobs, vbs)]
                ps = [jnp.exp(sc - _pair(st[:, 0:1], st[:, HEAD_DIM:HEAD_DIM + 1])) for sc, st in zip(scs, sts)]
                dss = [(p * (dp - _pair(st[:, HALF:HALF + 1], st[:, HEAD_DIM + HALF:HEAD_DIM + HALF + 1]))).astype(BF16)
                       for p, dp, st in zip(ps, dps, sts)]
                dqs = [_dot(ds, kb, NN) for ds, kb in zip(dss, kbs)]
                dkfs = [_dot(ds, qb, TN) for ds, qb in zip(dss, qbs)]
                dvfs = [_dot(p.astype(BF16), dob, TN) for p, dob in zip(ps, dobs)]
                for (qrows, krows, _), dq, dkf, dvf in zip(ids, dqs, dkfs, dvfs):
                    dq_o[qrows, :] = dq
                    dk_o[krows, :] += jnp.where(is_a, dkf[:KEYS], dkf[KEYS:])
                    dv_o[krows, :] += jnp.where(is_a, dvf[:KEYS], dvf[KEYS:])
                return c

            lax.fori_loop(0, nblk // ATT_GROUP_BWD, step, 0)

            if dil > 1:
                for r in range(dil):
                    def merge(j, c, dil=dil, r=r, ln=ln):
                        j0 = pl.multiple_of(j * PRO_ROWS, PRO_ROWS)
                        nat = _rows(r + dil * j0, PRO_ROWS, dil)
                        cm = pl.ds(r * ln + j0, PRO_ROWS)
                        dq_acc[nat, :] += dq_d[cm, :]
                        dk_acc[nat, :] += dk_d[cm, :]
                        dv_acc[nat, :] += dv_d[cm, :]
                        return c

                    lax.fori_loop(0, ln // PRO_ROWS, merge, 0)

        def back(dn_out, x, w, scale):
            r = lax.rsqrt(_head_sum2(x * x, ones_bd) * (1.0 / HEAD_DIM) + EPS)
            nrm = x * r
            dw = jnp.sum(dn_out * nrm, axis=0, keepdims=True) * scale
            dn = dn_out * (w * scale)
            return r * (dn - nrm * (_head_sum2(dn * nrm, ones_bd) * (1.0 / HEAD_DIM))), dw

        def epi(j, c):
            rows = pl.ds(pl.multiple_of(j * PRO_ROWS, PRO_ROWS), PRO_ROWS)
            dq, dqw = back(dq_acc[rows, :], q_ref[rows, :], qw_ref[...], HEAD_DIM ** -0.5)
            dk, dkw = back(dk_acc[rows, :], k_ref[rows, :], kw_ref[...], 1.0)
            dq_ref[rows, :] = dq.astype(BF16)
            dk_ref[rows, :] = dk.astype(BF16)
            dv_ref[rows, :] = dv_acc[rows, :].astype(BF16)
            return (c[0] + dqw, c[1] + dkw)

        zrow = jnp.zeros((1, LANE), F32)
        dqw, dkw = lax.fori_loop(0, s // PRO_ROWS, epi, (zrow, zrow))
        dqw_ref[...] = dqw
        dkw_ref[...] = dkw

    o = jax.ShapeDtypeStruct((s, ATT_D), BF16)
    ov = jax.ShapeDtypeStruct((1, ATT_D), F32)
    scr = pltpu.VMEM((s, LANE), F32)
    scb = pltpu.VMEM((s, LANE), BF16)
    return _pcall(
        body, (proj, proj, proj, do, stats, qw, kw), name="att_bwd", grid=(ATT_D // LANE,),
        in_specs=[blk(OFF_Q), blk(OFF_K), blk(OFF_V), oblk, oblk, wspec, wspec],
        out_specs=[oblk, oblk, oblk, wspec, wspec], out_shape=[o, o, o, ov, ov],
        scratch_shapes=[scr, scr, scb, scb, scb, scb, scr, scr, scr, scr, scr, scr, scr, pltpu.VMEM((2, ATT_BLK, 2 * KEYS), F32)],
        sem=("parallel",), comm=comm)


def _att_norm_fwd(o, nw, ycat):
    s = o.shape[0]
    row = pl.BlockSpec((ROW_TILE, ATT_D), lambda i: (i, 0))
    vec = pl.BlockSpec((1, ATT_D), lambda i: (0, 0))

    def body(o_ref, nw_ref, ycat_ref, y_ref):
        o = o_ref[...]
        r = lax.rsqrt(jnp.mean(o * o, axis=-1, keepdims=True) + EPS)
        y_ref[...] = (o * r * nw_ref[...]).astype(BF16)

    return pl.pallas_call(body, name="att_norm_fwd", grid=(s // ROW_TILE,),
                          in_specs=[row, vec, pl.BlockSpec(memory_space=pl.ANY)],
                          out_specs=pl.BlockSpec((ROW_TILE, ATT_D), lambda i: (i, 1)),
                          out_shape=jax.ShapeDtypeStruct(ycat.shape, BF16), input_output_aliases={2: 0},
                          compiler_params=_cparams(("parallel",)))(o, nw, ycat)


def _mixer_split_epilogue(dycat, first, rows, vecs, outs):
    (o_ref, lse_ref), (nw_ref,), (dyssd_ref, do_ref, st_ref, dnw_ref) = rows, vecs, outs

    @pl.when(first)
    def _():
        dnw_ref[...] = jnp.zeros_like(dnw_ref)

    dyssd_ref[...] = dycat[:, :SSD_D_INNER]
    dy = dycat[:, SSD_D_INNER:]
    o = o_ref[...]
    r = lax.rsqrt(jnp.mean(o * o, axis=-1, keepdims=True) + EPS)
    nrm = o * r
    dnw_ref[...] += jnp.sum(dy * nrm, axis=0, keepdims=True)
    dn = dy * nw_ref[...]
    do = r * (dn - nrm * jnp.mean(dn * nrm, axis=-1, keepdims=True))
    do_ref[...] = do
    ones_bd = _head_mean_matrix().astype(BF16)
    prod = do * o
    delta = jnp.concatenate([_head_sum2(prod[:, j * LANE:(j + 1) * LANE], ones_bd) for j in range(ATT_D // LANE)], axis=1)
    lane = lax.broadcasted_iota(jnp.int32, (1, ATT_D), 1)
    st_ref[...] = jnp.where((lane & (HEAD_DIM - 1)) < HALF, lse_ref[...], delta)


def _ada_fwd(c_all, w_ada):
    def body(c_ref, w_ref, o_ref):
        cv = c_ref[...]
        o_ref[...] = _dot((cv * _sigmoid(cv)).astype(BF16), w_ref[...].astype(BF16), NN)

    return pl.pallas_call(body, name="ada_fwd", out_shape=jax.ShapeDtypeStruct((c_all.shape[0], w_ada.shape[1]), F32),
                          compiler_params=_cparams())(c_all, w_ada)


def _adamw_math(g, w, m, v):
    m_new = ADAM_B1 * m + (1.0 - ADAM_B1) * g
    v_new = ADAM_B2 * v + (1.0 - ADAM_B2) * (g * g)
    m_hat = m_new / (1.0 - ADAM_B1 ** ADAM_STEP)
    v_hat = v_new / (1.0 - ADAM_B2 ** ADAM_STEP)
    delta = -ADAM_LR * (m_hat / (jnp.sqrt(v_hat) + ADAM_EPS) + ADAM_WD * w)
    return delta, m_new, v_new


def _ada_bwd_adamw(c_all, dmod_cols, w, m, v):
    rows, cols = w.shape
    tr = 256
    blk = pl.BlockSpec((tr, cols), lambda i: (i, 0))

    def body(c_ref, d_ref, w_ref, m_ref, v_ref, g_ref, dl_ref, mo_ref, vo_ref):
        cv = c_ref[...]
        ca = cv * _sigmoid(cv)
        g = ca[:, 0:1] * d_ref[0:1, :]
        for b in range(1, N_DEV):
            g = g + ca[:, b:b + 1] * d_ref[b:b + 1, :]
        g_ref[...] = g
        dl_ref[...], mo_ref[...], vo_ref[...] = _adamw_math(g, w_ref[...], m_ref[...], v_ref[...])

    o = jax.ShapeDtypeStruct((rows, cols), F32)
    return pl.pallas_call(
        body, name="ada_bwd_adamw", grid=(rows // tr,),
        in_specs=[pl.BlockSpec((tr, N_DEV), lambda i: (i, 0)), pl.BlockSpec((N_DEV, cols), lambda i: (0, 0)), blk, blk, blk],
        out_specs=[blk] * 4, out_shape=[o, o, o, o], compiler_params=_cparams(("parallel",)))(c_all.T, dmod_cols, w, m, v)


def _reduce_adamw(slabs, w, m, v, name):
    rows, cols = w.shape
    n_src = slabs.shape[0]
    if rows % 128 == 0:
        tr, steps = 128, rows // 128
        blk = pl.BlockSpec((tr, cols), lambda i: (i, 0))
        sblk = pl.BlockSpec((n_src, tr, cols), lambda i: (0, i, 0))
    else:
        tc, steps = 256, cols // 256
        blk = pl.BlockSpec((rows, tc), lambda i: (0, i))
        sblk = pl.BlockSpec((n_src, rows, tc), lambda i: (0, 0, i))

    def body(s_ref, w_ref, m_ref, v_ref, g_ref, dl_ref, mo_ref, vo_ref):
        g = s_ref[0].astype(F32)
        for src in range(1, n_src):
            g = g + s_ref[src].astype(F32)
        g_ref[...] = g
        dl_ref[...], mo_ref[...], vo_ref[...] = _adamw_math(g, w_ref[...], m_ref[...], v_ref[...])

    o = jax.ShapeDtypeStruct((rows, cols), F32)
    return pl.pallas_call(
        body, name=name, grid=(steps,), in_specs=[sblk, blk, blk, blk],
        out_specs=[blk] * 4, out_shape=[o, o, o, o], compiler_params=_cparams(("parallel",)))(slabs, w, m, v)


def _small_reduce_adamw(gathered, w, m, v):
    def body(s_ref, w_ref, m_ref, v_ref, g_ref, dl_ref, mo_ref, vo_ref):
        g = s_ref[0]
        for dev in range(1, N_DEV):
            g = g + s_ref[dev]
        g_ref[...] = g
        dl_ref[...], mo_ref[...], vo_ref[...] = _adamw_math(g, w_ref[...], m_ref[...], v_ref[...])

    o = jax.ShapeDtypeStruct(w.shape, F32)
    return pl.pallas_call(body, name="small_reduce_adamw", out_shape=[o, o, o, o], compiler_params=_cparams())(gathered, w, m, v)


def _adamw_small(g, w, m, v, name):
    def body(g_ref, w_ref, m_ref, v_ref, dl_ref, mo_ref, vo_ref):
        dl_ref[...], mo_ref[...], vo_ref[...] = _adamw_math(g_ref[...], w_ref[...], m_ref[...], v_ref[...])

    o = jax.ShapeDtypeStruct(w.shape, F32)
    return pl.pallas_call(body, name=name, out_shape=[o, o, o], compiler_params=_cparams())(g, w, m, v)


class _Exchange:
    def __init__(self, arrs, scatter):
        self.arrs, self.scatter, self.n = list(arrs), scatter, len(arrs)
        hbm = pl.BlockSpec(memory_space=pltpu.HBM)
        self.in_specs = [hbm] * self.n
        self.out_specs = [hbm] * self.n
        self.out_shape = [jax.ShapeDtypeStruct(a.shape if scatter else (N_DEV,) + a.shape, a.dtype) for a in self.arrs]
        self.scratch = [pltpu.SemaphoreType.DMA((self.n * (N_DEV - 1),)), pltpu.SemaphoreType.DMA((self.n * (N_DEV - 1),)),
                        pltpu.SemaphoreType.DMA((self.n,))]

    def _local(self, ins, outs, sems):
        me = 4 * lax.axis_index("x") + 2 * lax.axis_index("y") + lax.axis_index("c")
        return [pltpu.make_async_copy(ins[a].at[me] if self.scatter else ins[a], outs[a].at[me], sems[2].at[a])
                for a in range(self.n)]

    def _remote(self, ins, outs, sems, arriving):
        send_sems, recv_sems, _ = sems
        x, y, c = lax.axis_index("x"), lax.axis_index("y"), lax.axis_index("c")
        me = 4 * x + 2 * y + c
        remote = []
        for a in range(self.n):
            for k in range(1, N_DEV):
                px = 1 - x if k & 4 else x
                py = 1 - y if k & 2 else y
                pc = 1 - c if k & 1 else c
                peer = 4 * px + 2 * py + pc
                sem = a * (N_DEV - 1) + k - 1
                remote.append(pltpu.make_async_remote_copy(
                    src_ref=ins[a].at[peer] if self.scatter else ins[a], dst_ref=outs[a].at[peer if arriving else me],
                    send_sem=send_sems.at[sem], recv_sem=recv_sems.at[sem], device_id=(px, py, pc), device_id_type=MESH_IDS))
        return remote

    def start(self, ins, outs, sems):
        for cp in self._local(ins, outs, sems) + self._remote(ins, outs, sems, arriving=False):
            cp.start()

    def forward(self, ins, outs, sems):
        pass

    def wait(self, ins, outs, sems):
        for send, arrival in zip(self._remote(ins, outs, sems, arriving=False), self._remote(ins, outs, sems, arriving=True)):
            send.wait_send()
            arrival.wait_recv()
        for cp in self._local(ins, outs, sems):
            cp.wait()


N_CHIP = N_DEV // 2


class _SiblingSwap(_Exchange):
    def __init__(self, arrs):
        super().__init__(arrs, scatter=True)
        self.out_shape = [jax.ShapeDtypeStruct((N_CHIP,) + a.shape[2:], a.dtype) for a in self.arrs]
        self.scratch = [pltpu.SemaphoreType.DMA((self.n,)), pltpu.SemaphoreType.DMA((self.n,)), pltpu.SemaphoreType.DMA((1,))]

    def _copies(self, ins, outs, sems):
        x, y, c = lax.axis_index("x"), lax.axis_index("y"), lax.axis_index("c")
        return [pltpu.make_async_remote_copy(src_ref=ins[a].at[:, 1 - c], dst_ref=outs[a], send_sem=sems[0].at[a], recv_sem=sems[1].at[a],
                                             device_id=(x, y, 1 - c), device_id_type=MESH_IDS) for a in range(self.n)]

    def start(self, ins, outs, sems):
        for cp in self._copies(ins, outs, sems):
            cp.start()

    def wait(self, ins, outs, sems):
        for cp in self._copies(ins, outs, sems):
            cp.wait()


class _ChipScatter(_Exchange):
    def __init__(self, arrs):
        super().__init__(arrs, scatter=True)
        n_pairs = self.n * (N_CHIP - 1)
        self.scratch = [pltpu.SemaphoreType.DMA((n_pairs,)), pltpu.SemaphoreType.DMA((n_pairs,)), pltpu.SemaphoreType.DMA((self.n,))]

    def _local(self, ins, outs, sems):
        chip = 2 * lax.axis_index("x") + lax.axis_index("y")
        return [pltpu.make_async_copy(ins[a].at[chip], outs[a].at[chip], sems[2].at[a]) for a in range(self.n)]

    def _remote(self, ins, outs, sems, arriving):
        send_sems, recv_sems, _ = sems
        x, y, c = lax.axis_index("x"), lax.axis_index("y"), lax.axis_index("c")
        chip = 2 * x + y
        remote = []
        for a in range(self.n):
            for k in range(1, N_CHIP):
                px = 1 - x if k & 2 else x
                py = 1 - y if k & 1 else y
                peer = 2 * px + py
                sem = a * (N_CHIP - 1) + k - 1
                remote.append(pltpu.make_async_remote_copy(
                    src_ref=ins[a].at[peer], dst_ref=outs[a].at[peer if arriving else chip], send_sem=send_sems.at[sem],
                    recv_sem=recv_sems.at[sem], device_id=(px, py, c), device_id_type=MESH_IDS))
        return remote


def _chip_sum(mine, theirs):
    n, rows, cols = mine.shape
    blk = pl.BlockSpec((1, rows, 256), lambda q, j: (q, 0, j))

    def body(a_ref, b_ref, o_ref):
        o_ref[...] = (a_ref[...].astype(F32) + b_ref[...].astype(F32)).astype(BF16)

    return pl.pallas_call(body, name="chip_sum", grid=(n, cols // 256), in_specs=[blk, blk], out_specs=blk,
                          out_shape=jax.ShapeDtypeStruct(mine.shape, BF16),
                          compiler_params=_cparams(("parallel", "parallel")))(mine, theirs)


class _Gather2(_Exchange):
    def __init__(self, arrs):
        super().__init__(arrs, scatter=False)

    def _copies(self, ins, outs, sems):
        send_sems, recv_sems, _ = sems
        x, y, c = lax.axis_index("x"), lax.axis_index("y"), lax.axis_index("c")
        sibling = (x, y, 1 - c)
        chips = [(1 - x, y), (x, 1 - y), (1 - x, 1 - y)]
        first, passed, landed = [], [], []
        for a in range(self.n):
            def copy(k, block, to, src=None, a=a):
                slab = outs[a].at[4 * block[0] + 2 * block[1] + block[2]]
                return pltpu.make_async_remote_copy(
                    src_ref=slab if src is None else src, dst_ref=slab, send_sem=send_sems.at[a * (N_DEV - 1) + k],
                    recv_sem=recv_sems.at[a * (N_DEV - 1) + k], device_id=to, device_id_type=MESH_IDS)

            first.append(copy(0, (x, y, c), sibling, src=ins[a]))
            landed.append(copy(0, sibling, sibling))
            for j, chip in enumerate(chips):
                first.append(copy(1 + j, (x, y, c), (*chip, c), src=ins[a]))
                passed.append((copy(1 + j, (*chip, c), sibling), copy(4 + j, (*chip, c), sibling)))
                landed.append(copy(4 + j, (*chip, 1 - c), sibling))
        return first, passed, landed

    def start(self, ins, outs, sems):
        for cp in self._local(ins, outs, sems) + self._copies(ins, outs, sems)[0]:
            cp.start()

    def forward(self, ins, outs, sems):
        for arrival, onward in self._copies(ins, outs, sems)[1]:
            arrival.wait_recv()
            onward.start()

    def wait(self, ins, outs, sems):
        first, passed, landed = self._copies(ins, outs, sems)
        for arrival in landed:
            arrival.wait_recv()
        for cp in first + [onward for _, onward in passed]:
            cp.wait_send()
        for cp in self._local(ins, outs, sems):
            cp.wait()


def _split_comm_refs(refs, n_in, n_out, n_scr, comm):
    nc = comm.n if comm is not None else 0
    ns = 3 if comm is not None else 0
    pos, groups = 0, []
    for cnt in (n_in, nc, n_out, nc, n_scr, ns):
        groups.append(refs[pos:pos + cnt])
        pos += cnt
    assert pos == len(refs), (pos, len(refs))
    return groups


def _pcall(body, args, *, name, grid, in_specs, out_specs, out_shape, scratch_shapes=(), sem=None, comm=None):
    in_specs, out_specs, out_shape, scratch_shapes = list(in_specs), list(out_specs), list(out_shape), list(scratch_shapes)
    n_in, n_out, n_scr = len(in_specs), len(out_specs), len(scratch_shapes)
    if comm is None:
        kernel_body = body
    else:
        def kernel_body(*refs):
            ins, cins, outs, couts, scr, sems = _split_comm_refs(refs, n_in, n_out, n_scr, comm)
            ids = [pl.program_id(a) for a in range(len(grid))]
            first, last = ids[0] == 0, ids[0] == grid[0] - 1
            for a in range(1, len(grid)):
                first, last = first & (ids[a] == 0), last & (ids[a] == grid[a] - 1)

            middle = ids[0] == (2 * grid[0]) // 3
            for a in range(1, len(grid)):
                middle = middle & (ids[a] == 0)

            @pl.when(first)
            def _():
                comm.start(cins, couts, sems)

            @pl.when(middle)
            def _():
                comm.forward(cins, couts, sems)

            body(*ins, *outs, *scr)

            @pl.when(last)
            def _():
                comm.wait(cins, couts, sems)

        in_specs, out_specs, out_shape = in_specs + comm.in_specs, out_specs + comm.out_specs, out_shape + comm.out_shape
        scratch_shapes, args = scratch_shapes + comm.scratch, list(args) + comm.arrs
        sem = ("arbitrary",) * len(grid)
    res = pl.pallas_call(kernel_body, name=name, grid=grid, in_specs=in_specs, out_specs=out_specs, out_shape=out_shape,
                         scratch_shapes=scratch_shapes, compiler_params=_cparams(sem))(*args)
    return res[:n_out], res[n_out:]


def _exchange(arrs, name, scatter=False, ex=None):
    if ex is None:
        ex = _Exchange(arrs, scatter=True) if scatter else _Gather2(arrs)

    def body(*refs):
        _, ins, _, outs, _, sems = _split_comm_refs(refs, 0, 0, 0, ex)
        ex.start(ins, outs, sems)
        ex.forward(ins, outs, sems)
        ex.wait(ins, outs, sems)

    return pl.pallas_call(body, name=name, in_specs=ex.in_specs, out_specs=ex.out_specs, out_shape=ex.out_shape,
                          scratch_shapes=ex.scratch)(*ex.arrs)


def _pad_lanes(v, width=LANE):
    return jnp.pad(v, ((0, 0), (0, width - v.shape[1])))


def _shards_to_cols(g):
    return jnp.transpose(g, (1, 0, 2)).reshape(g.shape[1], N_DEV * g.shape[2])


def _cols_to_shards(w):
    return w.astype(BF16).reshape(w.shape[0], N_DEV, w.shape[1] // N_DEV).transpose(1, 0, 2)


def _local_step(x, tgt, mod, w_in_pt, conv_w, conv_b, dt_bias, a_log, d_skip, ssd_norm_w, q_norm_w, k_norm_w,
                attn_norm_w, w_out_sh, w_ff1_sh, w_ff2_sh, norm1_w, norm2_w, core):
    shift1, scale1, gate1, shift2, scale2, gate2 = [mod[i:i + 1] for i in range(N_MOD)]
    dtb, alog, dsk = _pad_lanes(dt_bias), _pad_lanes(a_log), _pad_lanes(d_skip)
    qw, kw = jnp.tile(q_norm_w, (1, ATT_HEADS)), jnp.tile(k_norm_w, (1, ATT_HEADS))

    h1 = _norm_mod_fwd(x, norm1_w, scale1, shift1, "norm1_fwd")
    proj = _matmul(h1, w_in_pt, tb=True, tm=2048, tn=896, tk=1024, name="in_proj")
    pre, act = _conv_fwd(proj, conv_w, conv_b)
    ypre, ycat_ssd, hall = _ssd_fwd(proj, act, dtb, alog, dsk, ssd_norm_w)
    (o_att, lse), (w_out_g, w_ff1_g, w_ff2_g) = _att_fwd(proj, qw, kw, comm=_Gather2([w_out_sh, w_ff1_sh, w_ff2_sh]))
    w_out = w_out_g.reshape(2 * D_MODEL, D_MODEL)
    w_ff1 = _shards_to_cols(w_ff1_g)
    w_ff2 = w_ff2_g.reshape(D_FF, D_MODEL)
    ycat = _att_norm_fwd(o_att, attn_norm_w, ycat_ssd)
    row32, row16, vec32 = ("row", F32), ("row", BF16), ("vec", F32)
    mix, x1, h2 = _matmul_rows(ycat, w_out, _residual_norm_epilogue, [x], [gate1, norm2_w, scale2, shift2],
                               [row32, row32, row16], tm=512, name="out_proj")
    u, act_ff = _matmul(h2, w_ff1, tm=1024, tn=1024, tk=1024, name="ff1", mode="relu2")
    loss, dout, dff, dgate2 = _matmul_rows(act_ff, w_ff2, _loss_epilogue, [x1, tgt], [gate2],
                                           [("one", F32), row32, row16, vec32], tm=512, name="ff2")

    du = _matmul(dff, w_ff2, tb=True, tm=1024, tn=1024, tk=1024, out_dtype=BF16, name="ff2_dx", mode="drelu2", u=u)
    g_ff2 = _matmul(act_ff, dff, ta=True, tm=512, tn=1024, tk=4096, out_dtype=BF16, name="ff2_dw")
    dx1, dshift2, dscale2, g_norm2, dmix, dgate1 = _matmul_rows(
        du, w_ff1, _norm_bwd_epilogue, [x1, dout, mix], [norm2_w, scale2, gate1],
        [row32, vec32, vec32, vec32, row16, vec32], tb=True, tm=512, name="ff1_dx")
    g_ff1 = _matmul(h2, du, ta=True, tm=512, tn=1024, tk=4096, out_dtype=BF16, name="ff1_dw")

    dy_ssd, do, stats, g_attn_norm = _matmul_rows(
        dmix, w_out, _mixer_split_epilogue, [o_att, lse], [attn_norm_w],
        [("row", F32, SSD_D_INNER), ("row", F32, ATT_D), ("row", F32, ATT_D), ("vec", F32, ATT_D)], tb=True, tm=512, name="out_proj_dx")
    g_out = _matmul(ycat, dmix, ta=True, tm=512, tn=1024, tk=4096, out_dtype=BF16, name="out_proj_dw")
    ff_slabs = [_cols_to_shards(g_ff1), g_ff2.astype(BF16).reshape(N_DEV, D_FF // N_DEV, D_MODEL)]
    (dq, dk, dv, dqw, dkw), (s_ff1, s_ff2) = _att_bwd(proj, do, stats, qw, kw, comm=_Exchange(ff_slabs, scatter=True))
    out_slabs = [g_out.astype(BF16).reshape(N_DEV, 2 * D_MODEL // N_DEV, D_MODEL)]
    (dz, dact, ddtr, da, g_dsk, g_dtb, g_ssd_norm), (s_out,) = _ssd_bwd(
        dy_ssd, ypre, proj, act, hall, dtb, alog, dsk, ssd_norm_w, comm=_Exchange(out_slabs, scatter=True))
    dxbc, g_conv_w, g_conv_b = _conv_bwd(dact, pre, proj, conv_w)
    dproj = jnp.concatenate([dz, dxbc, dq, dk, dv, ddtr], axis=1)
    g_in_pt = _matmul(dproj, h1, ta=True, tm=896, tn=1024, tk=4096, out_dtype=BF16, name="in_proj_dw")
    in_slabs = _unpack_w_in_rows(g_in_pt).reshape(N_CHIP, 2, IN_W // N_DEV, D_MODEL)
    (sibling_slabs,) = _exchange(None, "swap_w_in_grads", ex=_SiblingSwap([in_slabs]))
    chip_slabs = _chip_sum(lax.dynamic_index_in_dim(in_slabs, core, axis=1, keepdims=False), sibling_slabs)
    (grad_x, dshift1, dscale1, g_norm1), (s_in,) = _matmul_rows(
        dproj, w_in_pt, _norm_bwd_epilogue, [x, dx1], [norm1_w, scale1], [row32, vec32, vec32, vec32],
        tm=256, name="in_proj_dx", comm=_ChipScatter([chip_slabs]))

    dmod = jnp.concatenate([dshift1, dscale1, dgate1, dshift2, dscale2, dgate2], axis=0)
    g_alog = da[:, :SSD_HEADS] * (-jnp.exp(a_log))
    g_qw = dqw.reshape(ATT_HEADS, HEAD_DIM).sum(axis=0, keepdims=True)
    g_kw = dkw.reshape(ATT_HEADS, HEAD_DIM).sum(axis=0, keepdims=True)
    return dict(loss=loss, grad_x=grad_x, dmod=dmod, norm1_w=g_norm1, norm2_w=g_norm2, w_in=s_in, conv_w=g_conv_w,
                conv_b=g_conv_b, dt_bias=g_dtb[:, :SSD_HEADS], a_log=g_alog, d_skip=g_dsk[:, :SSD_HEADS],
                ssd_norm_w=g_ssd_norm, q_norm_w=g_qw, k_norm_w=g_kw, attn_norm_w=g_attn_norm, w_out=s_out,
                w_ff1=s_ff1, w_ff2=s_ff2)


def _pack_w_in_rows(wt_full):
    o_dt = SSD_D_INNER + CONV_CH
    o_q = o_dt + SSD_HEADS
    pad = jnp.zeros((LANE - SSD_HEADS, wt_full.shape[1]), wt_full.dtype)
    return jnp.concatenate([wt_full[:o_dt], wt_full[o_q:], wt_full[o_dt:o_q], pad], axis=0)


def _unpack_w_in_rows(gt_p):
    return jnp.concatenate([gt_p[:OFF_Q], gt_p[OFF_DT:OFF_DT + SSD_HEADS], gt_p[OFF_Q:OFF_DT]], axis=0)


MISC_FIELDS = (("dt_bias", SSD_HEADS), ("a_log", SSD_HEADS), ("d_skip", SSD_HEADS), ("q_norm_w", HEAD_DIM), ("k_norm_w", HEAD_DIM))
SMALL_LAYOUT = (("b_ada", 6), ("norm1_w", 1), ("norm2_w", 1), ("conv_w", 8), ("conv_b", 2), ("ssd_norm_w", 1),
                ("attn_norm_w", 1), ("misc", 1))


def _pack_small(vals):
    rows = []
    for name, nrow in SMALL_LAYOUT:
        if name == "misc":
            misc = jnp.concatenate([vals[f].reshape(1, n) for f, n in MISC_FIELDS], axis=1)
            rows.append(_pad_lanes(misc, D_MODEL))
        elif name in vals:
            rows.append(vals[name].reshape(nrow, D_MODEL))
        else:
            rows.append(jnp.zeros((nrow, D_MODEL), F32))
    used = sum(n for _, n in SMALL_LAYOUT)
    rows.append(jnp.zeros((SMALL_ROWS - used, D_MODEL), F32))
    return jnp.concatenate(rows, axis=0)


def _unpack_small(packed):
    out, r = {}, 0
    for name, nrow in SMALL_LAYOUT:
        blk = packed[r:r + nrow]
        r += nrow
        if name == "misc":
            c0 = 0
            for f, n in MISC_FIELDS:
                out[f] = blk[:, c0:c0 + n]
                c0 += n
        elif name == "b_ada":
            out[name] = blk.reshape(1, N_MOD * D_MODEL)
        elif name == "conv_w":
            out[name] = blk.reshape(CONV_K, CONV_CH)
        elif name == "conv_b":
            out[name] = blk.reshape(1, CONV_CH)
        else:
            out[name] = blk
    return out


WEIGHT_NAMES = ("norm1_w", "norm2_w", "w_ada", "b_ada", "w_in", "conv_w", "conv_b", "dt_bias", "a_log", "d_skip",
                "ssd_norm_w", "q_norm_w", "k_norm_w", "attn_norm_w", "w_out", "w_ff1", "w_ff2")
SMALL_NAMES = ("norm1_w", "norm2_w", "b_ada", "conv_b", "dt_bias", "a_log", "d_skip", "ssd_norm_w", "q_norm_w",
               "k_norm_w", "attn_norm_w")


def kernel(x, c, norm1_w, norm2_w, w_ada, b_ada, w_in, conv_w, conv_b, dt_bias, a_log, d_skip, ssd_norm_w, q_norm_w, k_norm_w, attn_norm_w, w_out, w_ff1, w_ff2, loss_target, m_norm1_w, m_norm2_w, m_w_ada, m_b_ada, m_w_in, m_conv_w, m_conv_b, m_dt_bias, m_a_log, m_d_skip, m_ssd_norm_w, m_q_norm_w, m_k_norm_w, m_attn_norm_w, m_w_out, m_w_ff1, m_w_ff2, v_norm1_w, v_norm2_w, v_w_ada, v_b_ada, v_w_in, v_conv_w, v_conv_b, v_dt_bias, v_a_log, v_d_skip, v_ssd_norm_w, v_q_norm_w, v_k_norm_w, v_attn_norm_w, v_w_out, v_w_ff1, v_w_ff2):
    args = dict(locals())
    w = {n: args[n] for n in WEIGHT_NAMES}
    m = {n: args["m_" + n] for n in WEIGHT_NAMES}
    v = {n: args["v_" + n] for n in WEIGHT_NAMES}
    me = 4 * lax.axis_index("x") + 2 * lax.axis_index("y") + lax.axis_index("c")

    c_rows = jnp.pad(c, ((0, 7), (0, 0)))
    w_in_t, m_in_t, v_in_t = [jnp.transpose(t["w_in"][0]) for t in (w, m, v)]
    c_g, conv_g, w_in_g = _exchange([c_rows, w["conv_w"][0], w_in_t.astype(BF16)], "gather_w_in", scatter=False)
    c_all = c_g[:, 0, :]
    conv_full = _shards_to_cols(conv_g)
    w_in_pt = _pack_w_in_rows(w_in_g.reshape(IN_W, D_MODEL))

    mod_part = _ada_fwd(c_all, w["w_ada"][0])
    (mod_g,) = _exchange([mod_part], "gather_mod", scatter=False)
    mod_mine = lax.dynamic_index_in_dim(mod_g, me, axis=1, keepdims=False).reshape(1, N_MOD * D_MODEL) + w["b_ada"]
    mod = mod_mine.reshape(N_MOD, D_MODEL)

    res = _local_step(x[0], loss_target[0], mod, w_in_pt, conv_full, w["conv_b"], w["dt_bias"], w["a_log"], w["d_skip"],
                      w["ssd_norm_w"], w["q_norm_w"], w["k_norm_w"], w["attn_norm_w"], w["w_out"][0].astype(BF16),
                      w["w_ff1"][0].astype(BF16), w["w_ff2"][0].astype(BF16), w["norm1_w"], w["norm2_w"], lax.axis_index("c"))

    small_vals = {n: res[n] for n in SMALL_NAMES if n != "b_ada"}
    small_vals["b_ada"] = res["dmod"]
    small_vals["conv_w"] = res["conv_w"]
    (small_g,) = _exchange([_pack_small(small_vals)], "gather_small", scatter=False)

    grads, delta, new_m, new_v = {}, {}, {}, {}
    for name in ("w_out", "w_ff1", "w_ff2"):
        outs = _reduce_adamw(res[name], w[name][0], m[name][0], v[name][0], "adamw_" + name)
        grads[name], delta[name], new_m[name], new_v[name] = [o[None] for o in outs]
    outs = _reduce_adamw(res["w_in"], w_in_t, m_in_t, v_in_t, "adamw_w_in")
    grads["w_in"], delta["w_in"], new_m["w_in"], new_v["w_in"] = [jnp.transpose(o)[None] for o in outs]

    sm = _small_reduce_adamw(small_g, _pack_small({n: w[n] for n in SMALL_NAMES}), _pack_small({n: m[n] for n in SMALL_NAMES}),
                             _pack_small({n: v[n] for n in SMALL_NAMES}))
    sm = [_unpack_small(p) for p in sm]
    for n in SMALL_NAMES:
        grads[n], delta[n], new_m[n], new_v[n] = [p[n] for p in sm]
    shard_w = CONV_CH // N_DEV
    g_conv = lax.dynamic_slice_in_dim(sm[0]["conv_w"], me * shard_w, shard_w, axis=1)
    cw = _adamw_small(g_conv, w["conv_w"][0], m["conv_w"][0], v["conv_w"][0], "adamw_conv_w")
    grads["conv_w"] = g_conv[None]
    delta["conv_w"], new_m["conv_w"], new_v["conv_w"] = [o[None] for o in cw]

    ada_w = w_ada.shape[2]
    dmod_all = small_g[:, :N_MOD, :].reshape(N_DEV, N_MOD * D_MODEL)
    dmod_cols = lax.dynamic_slice_in_dim(dmod_all, me * ada_w, ada_w, axis=1)
    outs = _ada_bwd_adamw(c_all, dmod_cols, w["w_ada"][0], m["w_ada"][0], v["w_ada"][0])
    grads["w_ada"], delta["w_ada"], new_m["w_ada"], new_v["w_ada"] = [o[None] for o in outs]

    loss = lax.psum(res["loss"][0, 0], ("x", "y", "c"))
    return (loss, res["grad_x"][None], *[grads[n] for n in WEIGHT_NAMES], *[delta[n] for n in WEIGHT_NAMES],
            *[new_m[n] for n in WEIGHT_NAMES], *[new_v[n] for n in WEIGHT_NAMES])
```

```python
import functools

import jax
import jax.numpy as jnp
from jax import lax
from jax.experimental import pallas as pl
from jax.experimental.pallas import tpu as pltpu

F32 = jnp.float32
BF16 = jnp.bfloat16
HIGHEST = lax.Precision.HIGHEST
MESH_IDS = pl.DeviceIdType.MESH

N_DEV = 8
D_MODEL = 1024
HEAD_DIM = 64
SSD_HEADS = 16
SSD_GROUPS = 4
HEADS_PER_GROUP = SSD_HEADS // SSD_GROUPS
SSD_STATE = 128
SSD_CHUNK = 128
SSD_D_INNER = SSD_HEADS * HEAD_DIM
GROUP_WIDTH = SSD_D_INNER // SSD_GROUPS
CONV_K = 4
CONV_CH = SSD_D_INNER + 2 * SSD_GROUPS * SSD_STATE
ATT_HEADS = 16
ATT_D = ATT_HEADS * HEAD_DIM
ATT_BLK = 128
DILATIONS = (1, 4, 16)
D_FF = 4 * D_MODEL
N_MOD = 6
EPS = 1e-6
IN_W = SSD_D_INNER + CONV_CH + SSD_HEADS + 3 * ATT_D
LANE = 128
OFF_Z, OFF_XBC, OFF_Q, OFF_K, OFF_V, OFF_DT = 0, 1024, 3072, 4096, 5120, 6144
IN_WP = OFF_DT + LANE

ADAM_LR, ADAM_B1, ADAM_B2, ADAM_EPS, ADAM_WD, ADAM_STEP = 0.001, 0.9, 0.999, 1e-08, 0.01, 10
VMEM_LIMIT = 56 * 1024 * 1024
ROW_TILE = 512
SMALL_ROWS = 24


def _cparams(sem=None):
    return pltpu.CompilerParams(dimension_semantics=sem, vmem_limit_bytes=VMEM_LIMIT)


def _sigmoid(v):
    return 1.0 / (1.0 + jnp.exp(-v))


def _softplus(v):
    y = jnp.exp(-jnp.abs(v))
    small = y * (1.0 - y * (0.5 - y * (1.0 / 3.0)))
    return jnp.maximum(v, 0.0) + jnp.where(y < 0.01, small, jnp.log(1.0 + y))


def _dot(a, b, dims, precision=None):
    return lax.dot_general(a, b, (dims, ((), ())), preferred_element_type=F32, precision=precision)


NN = ((1,), (0,))
NT = ((1,), (1,))
TN = ((0,), (0,))


def _matmul(a, b, *, ta=False, tb=False, tm, tn, tk, out_dtype=F32, name, mode=None, u=None, comm=None):
    m, k = (a.shape[1], a.shape[0]) if ta else a.shape
    n = b.shape[0] if tb else b.shape[1]
    assert m % tm == 0 and n % tn == 0 and k % tk == 0, (name, m, n, k)
    nk = k // tk
    a_spec = pl.BlockSpec((tk, tm), lambda i, j, kk: (kk, i)) if ta else pl.BlockSpec((tm, tk), lambda i, j, kk: (i, kk))
    b_spec = pl.BlockSpec((tn, tk), lambda i, j, kk: (j, kk)) if tb else pl.BlockSpec((tk, tn), lambda i, j, kk: (kk, j))
    o_spec = pl.BlockSpec((tm, tn), lambda i, j, kk: (i, j))
    dims = ((0,) if ta else (1,), (1,) if tb else (0,))
    n_out = 2 if mode == "relu2" else 1

    def body(*refs):
        if mode == "drelu2":
            a_ref, b_ref, u_ref = refs[:3]
            rest = refs[3:]
        else:
            a_ref, b_ref = refs[:2]
            u_ref = None
            rest = refs[2:]
        outs = rest[:n_out]
        part = _dot(a_ref[...], b_ref[...], dims)

        def finish(r):
            if mode == "relu2":
                outs[0][...] = r.astype(BF16)
                rr = jnp.maximum(r, 0.0)
                outs[1][...] = (rr * rr).astype(BF16)
            elif mode == "drelu2":
                outs[0][...] = (r * (2.0 * jnp.maximum(u_ref[...].astype(F32), 0.0))).astype(out_dtype)
            else:
                outs[0][...] = r.astype(out_dtype)

        if nk == 1:
            finish(part)
        else:
            acc = rest[n_out]
            kk = pl.program_id(2)

            @pl.when(kk == 0)
            def _():
                acc[...] = part

            @pl.when(kk > 0)
            def _():
                acc[...] += part

            @pl.when(kk == nk - 1)
            def _():
                finish(acc[...])

    in_specs = [a_spec, b_spec]
    args = [a, b]
    if mode == "drelu2":
        in_specs.append(o_spec)
        args.append(u)
    if mode == "relu2":
        out_shape = [jax.ShapeDtypeStruct((m, n), BF16), jax.ShapeDtypeStruct((m, n), BF16)]
    else:
        out_shape = [jax.ShapeDtypeStruct((m, n), out_dtype)]
    outs, comm_outs = _pcall(
        body, args, name=name, grid=(m // tm, n // tn, nk), in_specs=in_specs, out_specs=[o_spec] * n_out,
        out_shape=out_shape, scratch_shapes=[pltpu.VMEM((tm, tn), F32)] if nk > 1 else [],
        sem=("parallel", "parallel", "arbitrary"), comm=comm)
    res = tuple(outs) if mode == "relu2" else outs[0]
    return res if comm is None else (res, comm_outs)


def _rms_mod(xv, nw, scale, shift):
    r = lax.rsqrt(jnp.mean(xv * xv, axis=-1, keepdims=True) + EPS)
    return ((xv * r) * nw * (1.0 + scale) + shift).astype(BF16)


def _norm_mod_fwd(x, nw, scale, shift, name):
    s, d = x.shape
    row = pl.BlockSpec((ROW_TILE, d), lambda i: (i, 0))
    vec = pl.BlockSpec((1, d), lambda i: (0, 0))

    def body(x_ref, nw_ref, sc_ref, sh_ref, h_ref):
        h_ref[...] = _rms_mod(x_ref[...], nw_ref[...], sc_ref[...], sh_ref[...])

    return pl.pallas_call(body, name=name, grid=(s // ROW_TILE,), in_specs=[row, vec, vec, vec], out_specs=row,
                          out_shape=jax.ShapeDtypeStruct((s, d), BF16), compiler_params=_cparams(("parallel",)))(x, nw, scale, shift)


def _matmul_rows(a, b, epilogue, row_in, vec_in, outs, *, tb=False, tm, name, comm=None):
    m, k = a.shape
    n = b.shape[0] if tb else b.shape[1]
    assert m % tm == 0, (name, m, tm)
    dims = ((1,), (1,) if tb else (0,))
    n_row, n_vec = len(row_in), len(vec_in)

    def body(a_ref, b_ref, *rest):
        epilogue(_dot(a_ref[...], b_ref[...], dims), pl.program_id(0) == 0, rest[:n_row], rest[n_row:n_row + n_vec],
                 rest[n_row + n_vec:])

    def spec(kind, width):
        block = {"row": (tm, width), "vec": (1, width), "one": (1, 1)}[kind]
        return pl.BlockSpec(block, (lambda i: (i, 0)) if kind == "row" else (lambda i: (0, 0)))

    def shape(kind, width):
        return {"row": (m, width), "vec": (1, width), "one": (1, 1)}[kind]

    outs = [(o[0], o[1], o[2] if len(o) > 2 else n) for o in outs]
    res, comm_outs = _pcall(
        body, [a, b, *row_in, *vec_in], name=name, grid=(m // tm,),
        in_specs=[pl.BlockSpec((tm, k), lambda i: (i, 0)), pl.BlockSpec(b.shape, lambda i: (0, 0))]
        + [spec("row", r.shape[1]) for r in row_in] + [spec("vec", v.shape[1]) for v in vec_in],
        out_specs=[spec(kind, width) for kind, _, width in outs],
        out_shape=[jax.ShapeDtypeStruct(shape(kind, width), dt) for kind, dt, width in outs],
        sem=("arbitrary",), comm=comm)
    return res if comm is None else (res, comm_outs)


def _residual_norm_epilogue(mix, first, rows, vecs, outs):
    (x_ref,), (gate_ref, nw_ref, sc_ref, sh_ref), (mix_ref, x1_ref, h_ref) = rows, vecs, outs
    xv = x_ref[...] + gate_ref[...] * mix
    mix_ref[...] = mix
    x1_ref[...] = xv
    h_ref[...] = _rms_mod(xv, nw_ref[...], sc_ref[...], sh_ref[...])


def _loss_epilogue(ff, first, rows, vecs, outs):
    (x1_ref, t_ref), (g_ref,), (loss_ref, dout_ref, dff_ref, dg_ref) = rows, vecs, outs
    d = ff.shape[1]

    @pl.when(first)
    def _():
        loss_ref[...] = jnp.zeros_like(loss_ref)
        dg_ref[...] = jnp.zeros_like(dg_ref)

    err = x1_ref[...] + g_ref[...] * ff - t_ref[...]
    loss_ref[...] += (0.5 / d) * jnp.sum(err * err).reshape(1, 1)
    dout = err * (1.0 / d)
    dout_ref[...] = dout
    dff_ref[...] = (g_ref[...] * dout).astype(BF16)
    dg_ref[...] += jnp.sum(dout * ff, axis=0, keepdims=True)


def _norm_bwd_epilogue(dh, first, rows, vecs, outs):
    with_gate = len(vecs) == 3
    x_ref, dres_ref = rows[:2]
    nw_ref, sc_ref = vecs[:2]
    dx_ref, dsh_ref, dsc_ref, dnw_ref = outs[:4]

    @pl.when(first)
    def _():
        for ref in outs[1:4] + outs[5:]:
            ref[...] = jnp.zeros_like(ref)

    xv = x_ref[...]
    r = lax.rsqrt(jnp.mean(xv * xv, axis=-1, keepdims=True) + EPS)
    nrm = xv * r
    one_sc = 1.0 + sc_ref[...]
    dhn = dh * nrm
    dsh_ref[...] += jnp.sum(dh, axis=0, keepdims=True)
    dsc_ref[...] += jnp.sum(dhn, axis=0, keepdims=True) * nw_ref[...]
    dnw_ref[...] += jnp.sum(dhn, axis=0, keepdims=True) * one_sc
    dn = dh * (nw_ref[...] * one_sc)
    dx = dres_ref[...] + r * (dn - nrm * jnp.mean(dn * nrm, axis=-1, keepdims=True))
    dx_ref[...] = dx
    if with_gate:
        outs[4][...] = (vecs[2][...] * dx).astype(BF16)
        outs[5][...] += jnp.sum(dx * rows[2][...], axis=0, keepdims=True)


CONV_COLS = 256
CONV_FWD_ROWS = 2048
CONV_BWD_ROWS = 1024
CONV_SUB_ROWS = 128
HALO = 8


def _shift_down(cur, halo, k):
    if k == 0:
        return cur
    rolled = pltpu.roll(cur, k, axis=0)
    top = jnp.where(lax.broadcasted_iota(jnp.int32, halo.shape, 0) < k, pltpu.roll(halo, k, axis=0), rolled[:HALO])
    return jnp.concatenate([top, rolled[HALO:]], axis=0)


def _shift_up(cur, halo, k):
    if k == 0:
        return cur
    t = cur.shape[0]
    rolled = pltpu.roll(cur, t - k, axis=0)
    bot = jnp.where(lax.broadcasted_iota(jnp.int32, halo.shape, 0) >= HALO - k, pltpu.roll(halo, HALO - k, axis=0),
                    rolled[t - HALO:])
    return jnp.concatenate([rolled[:t - HALO], bot], axis=0)


def _conv_fwd(proj, conv_w, conv_b):
    s = proj.shape[0]
    nr = s // CONV_FWD_ROWS
    cb0 = OFF_XBC // CONV_COLS
    hb = CONV_FWD_ROWS // HALO
    cur = pl.BlockSpec((CONV_FWD_ROWS, CONV_COLS), lambda j, r: (r, cb0 + j))
    prev = pl.BlockSpec((HALO, CONV_COLS), lambda j, r: (jnp.maximum(r * hb - 1, 0), cb0 + j))
    out = pl.BlockSpec((CONV_FWD_ROWS, CONV_COLS), lambda j, r: (r, j))

    def body(u_ref, up_ref, w_ref, b_ref, pre_ref, act_ref):
        r = pl.program_id(1)
        u = u_ref[...]
        halo = jnp.where(r > 0, up_ref[...], 0.0)
        acc = b_ref[...] + w_ref[CONV_K - 1:CONV_K, :] * u
        for k in range(1, CONV_K):
            acc = acc + w_ref[CONV_K - 1 - k:CONV_K - k, :] * _shift_down(u, halo, k)
        pre_ref[...] = acc
        act_ref[...] = acc * _sigmoid(acc)

    return pl.pallas_call(
        body, name="conv_fwd", grid=(CONV_CH // CONV_COLS, nr),
        in_specs=[cur, prev, pl.BlockSpec((CONV_K, CONV_COLS), lambda j, r: (0, j)),
                  pl.BlockSpec((1, CONV_COLS), lambda j, r: (0, j))],
        out_specs=[out, out],
        out_shape=[jax.ShapeDtypeStruct((s, CONV_CH), F32), jax.ShapeDtypeStruct((s, CONV_CH), F32)],
        compiler_params=_cparams(("parallel", "arbitrary")))(proj, proj, conv_w, conv_b)


def _conv_bwd(dact, pre, proj, conv_w):
    s = proj.shape[0]
    nr = s // CONV_BWD_ROWS
    cb0 = OFF_XBC // CONV_COLS
    hb = CONV_BWD_ROWS // HALO
    last_halo = s // HALO - 1
    n_sub = CONV_BWD_ROWS // CONV_SUB_ROWS
    cur = pl.BlockSpec((CONV_BWD_ROWS, CONV_COLS), lambda j, r: (r, j))
    nxt = pl.BlockSpec((HALO, CONV_COLS), lambda j, r: (jnp.minimum((r + 1) * hb, last_halo), j))
    ucur = pl.BlockSpec((CONV_BWD_ROWS, CONV_COLS), lambda j, r: (r, cb0 + j))
    wspec = pl.BlockSpec((CONV_K, CONV_COLS), lambda j, r: (0, j))
    bspec = pl.BlockSpec((1, CONV_COLS), lambda j, r: (0, j))

    def dsilu(p):
        sg = _sigmoid(p)
        return sg * (1.0 + p * (1.0 - sg))

    def body(da_ref, dan_ref, pre_ref, pren_ref, u_ref, w_ref, du_ref, dw_ref, db_ref):
        r = pl.program_id(1)

        @pl.when(r == 0)
        def _():
            dw_ref[...] = jnp.zeros_like(dw_ref)
            db_ref[...] = jnp.zeros_like(db_ref)

        dws = [jnp.zeros((1, CONV_COLS), F32) for _ in range(CONV_K)]
        db = jnp.zeros((1, CONV_COLS), F32)
        for c in range(n_sub):
            rows = slice(c * CONV_SUB_ROWS, (c + 1) * CONV_SUB_ROWS)
            ahead = slice((c + 1) * CONV_SUB_ROWS, (c + 1) * CONV_SUB_ROWS + HALO)
            dpre = da_ref[rows, :] * dsilu(pre_ref[rows, :])
            if c < n_sub - 1:
                dnext = da_ref[ahead, :] * dsilu(pre_ref[ahead, :])
            else:
                dnext = jnp.where(r < nr - 1, dan_ref[...] * dsilu(pren_ref[...]), 0.0)
            u = u_ref[rows, :]
            du = w_ref[CONV_K - 1:CONV_K, :] * dpre
            dws[0] = dws[0] + jnp.sum(dpre * u, axis=0, keepdims=True)
            for k in range(1, CONV_K):
                ahead_k = _shift_up(dpre, dnext, k)
                du = du + w_ref[CONV_K - 1 - k:CONV_K - k, :] * ahead_k
                dws[k] = dws[k] + jnp.sum(ahead_k * u, axis=0, keepdims=True)
            du_ref[rows, :] = du.astype(BF16)
            db = db + jnp.sum(dpre, axis=0, keepdims=True)
        dw_ref[...] += jnp.concatenate(dws[::-1], axis=0)
        db_ref[...] += db

    return pl.pallas_call(
        body, name="conv_bwd", grid=(CONV_CH // CONV_COLS, nr),
        in_specs=[cur, nxt, cur, nxt, ucur, wspec],
        out_specs=[cur, wspec, bspec],
        out_shape=[jax.ShapeDtypeStruct((s, CONV_CH), BF16), jax.ShapeDtypeStruct((CONV_K, CONV_CH), F32),
                   jax.ShapeDtypeStruct((1, CONV_CH), F32)],
        compiler_params=_cparams(("parallel", "arbitrary")))(dact, dact, pre, pre, proj, conv_w)


def _ssd_common(dtr, dtb, alog):
    lane = lax.broadcasted_iota(jnp.int32, (1, LANE), 1)
    head_lane = lane < SSD_HEADS
    dt = jnp.where(head_lane, _softplus(dtr + dtb), 0.0)
    a = jnp.where(head_lane, -jnp.exp(alog), 0.0)
    row = lax.broadcasted_iota(jnp.int32, (SSD_CHUNK, SSD_CHUNK), 0)
    col = lax.broadcasted_iota(jnp.int32, (SSD_CHUNK, SSD_CHUNK), 1)
    tril = row >= col
    cs = _dot(tril.astype(F32), dt * a, NN, precision=HIGHEST)
    return dt, a, cs, cs.T, tril, lane


def _split_bf16(v, passes):
    terms, rest = [], v
    for _ in range(passes):
        t = rest.astype(BF16)
        terms.append(t)
        rest = rest - t.astype(F32)
    return terms


def _dot_split(v, m, dims, passes):
    terms = _split_bf16(v, passes)
    if passes == 1:
        return _dot(terms[0], m, dims)
    return _dot(jnp.concatenate(terms, axis=1), jnp.concatenate([m] * passes, axis=0 if dims == NN else 1), dims)


def _ssd_constants():
    heads = jnp.arange(LANE)[:, None]
    exp_mat = (heads == (jnp.arange(SSD_D_INNER)[None, :] // HEAD_DIM)).astype(BF16)
    ind4 = ((jnp.arange(SSD_HEADS * SSD_CHUNK)[:, None] // SSD_CHUNK) == jnp.arange(LANE)[None, :]).astype(BF16)
    return exp_mat, ind4


def _expand_heads(v):
    return jnp.repeat(v[:, :SSD_HEADS], HEAD_DIM, axis=1)


def _ssd_prep(dtr, dtb, alog, exp_mat):
    dt, a, cs, cst, tril, lane = _ssd_common(dtr, dtb, alog)
    return dt, a, cs, cst, tril, lane, _dot_split(dt, exp_mat, NN, 2), _dot_split(cs, exp_mat, NN, 3)


def _chunk_decay_rows(cs, g):
    parts = []
    for e in range(HEADS_PER_GROUP):
        h = g * HEADS_PER_GROUP + e
        parts.append(jnp.broadcast_to(jnp.exp(cs[SSD_CHUNK - 1:SSD_CHUNK, h:h + 1]), (HEAD_DIM, SSD_STATE)))
    return jnp.concatenate(parts, axis=0)


def _ssd_fwd(proj, act, dtb, alog, dsk, nw):
    s = proj.shape[0]
    nc = s // SSD_CHUNK
    bc_w = SSD_GROUPS * SSD_STATE
    exp_mat, _ = _ssd_constants()

    def body(z_ref, dtr_ref, xs_ref, b_ref, c_ref, dtb_ref, alog_ref, dskx_ref, nw_ref, exp_ref,
             ypre_ref, yssd_ref, hall_ref, h_scr):
        @pl.when(pl.program_id(0) == 0)
        def _():
            h_scr[...] = jnp.zeros_like(h_scr)

        dt, a, cs, cst, tril, lane, dtx, csx = _ssd_prep(dtr_ref[...], dtb_ref[...], alog_ref[...], exp_ref[...])
        cs_last_x = csx[SSD_CHUNK - 1:SSD_CHUNK, :]
        xs = xs_ref[...]
        xdt = xs * dtx
        xdtb = xdt.astype(BF16)
        xdec = (xdt * jnp.exp(cs_last_x - csx)).astype(BF16)
        ecsx = jnp.exp(csx)
        head_of_lane = lax.broadcasted_iota(jnp.int32, (1, GROUP_WIDTH), 1) // HEAD_DIM
        for g in range(SSD_GROUPS):
            gs = slice(g * GROUP_WIDTH, (g + 1) * GROUP_WIDTH)
            bg = b_ref[:, g * SSD_STATE:(g + 1) * SSD_STATE].astype(BF16)
            cg = c_ref[:, g * SSD_STATE:(g + 1) * SSD_STATE].astype(BF16)
            cb = _dot(cg, bg, NT)
            hprev = h_scr[gs, :]
            hall_ref[0, gs, :] = hprev
            gms, rhs = [], []
            xg = xdtb[:, gs]
            for e in range(HEADS_PER_GROUP):
                h = g * HEADS_PER_GROUP + e
                lm = jnp.exp(jnp.where(tril, cs[:, h:h + 1] - cst[h:h + 1, :], -1e30))
                gms.append((cb * lm).astype(BF16))
                rhs.append(jnp.where(head_of_lane == e, xg, jnp.zeros_like(xg)))
            y = _dot(jnp.concatenate(gms, axis=1), jnp.concatenate(rhs, axis=0), NN)
            y = y + ecsx[:, gs] * _dot(cg, hprev.astype(BF16), NT)
            y = y + dskx_ref[:, gs] * xs[:, gs]
            h_scr[gs, :] = hprev * _chunk_decay_rows(cs, g) + _dot(xdec[:, gs], bg, TN)
            ypre_ref[:, gs] = y
            z = z_ref[:, gs]
            yg = y * (z * _sigmoid(z))
            r = lax.rsqrt(jnp.mean(yg * yg, axis=-1, keepdims=True) + EPS)
            yssd_ref[:, gs] = (yg * r * nw_ref[:, gs]).astype(BF16)

    row_d = lambda cb: pl.BlockSpec((SSD_CHUNK, SSD_D_INNER), lambda c: (c, cb))
    small = pl.BlockSpec((1, LANE), lambda c: (0, 0))
    wide = pl.BlockSpec((1, SSD_D_INNER), lambda c: (0, 0))
    return pl.pallas_call(
        body, name="ssd_fwd", grid=(nc,),
        in_specs=[row_d(OFF_Z // SSD_D_INNER),
                  pl.BlockSpec((SSD_CHUNK, LANE), lambda c: (c, OFF_DT // LANE)),
                  row_d(0),
                  pl.BlockSpec((SSD_CHUNK, bc_w), lambda c: (c, SSD_D_INNER // bc_w)),
                  pl.BlockSpec((SSD_CHUNK, bc_w), lambda c: (c, SSD_D_INNER // bc_w + 1)),
                  small, small, wide, wide, pl.BlockSpec((LANE, SSD_D_INNER), lambda c: (0, 0))],
        out_specs=[row_d(0), row_d(0), pl.BlockSpec((1, SSD_D_INNER, SSD_STATE), lambda c: (c, 0, 0))],
        out_shape=[jax.ShapeDtypeStruct((s, SSD_D_INNER), F32), jax.ShapeDtypeStruct((s, SSD_D_INNER + ATT_D), BF16),
                   jax.ShapeDtypeStruct((nc, SSD_D_INNER, SSD_STATE), F32)],
        scratch_shapes=[pltpu.VMEM((SSD_D_INNER, SSD_STATE), F32)],
        compiler_params=_cparams(("arbitrary",)))(proj, proj, act, act, act, dtb, alog, _expand_heads(dsk), nw, exp_mat)


def _ssd_bwd(dycat, ypre, proj, act, hall, dtb, alog, dsk, nw, comm=None):
    s = proj.shape[0]
    nc = s // SSD_CHUNK
    bc_w = SSD_GROUPS * SSD_STATE

    exp_mat, ind4 = _ssd_constants()
    seg_passes = 1

    def body(dy_ref, ypre_ref, z_ref, dtr_ref, xs_ref, b_ref, c_ref, hall_ref, dtb_ref, alog_ref, dskx_ref, nw_ref,
             exp_ref, ind4_ref, dz_ref, dact_ref, ddtr_ref, da_ref, ddsk_ref, ddtb_ref, dnw_ref, dh_scr):
        @pl.when(pl.program_id(0) == 0)
        def _():
            dh_scr[...] = jnp.zeros_like(dh_scr)
            da_ref[...] = jnp.zeros_like(da_ref)
            ddsk_ref[...] = jnp.zeros_like(ddsk_ref)
            ddtb_ref[...] = jnp.zeros_like(ddtb_ref)
            dnw_ref[...] = jnp.zeros_like(dnw_ref)

        dtr = dtr_ref[...]
        dt, a, cs, cst, tril, lane, dtx, csx = _ssd_prep(dtr, dtb_ref[...], alog_ref[...], exp_ref[...])
        cs_last_x = csx[SSD_CHUNK - 1:SSD_CHUNK, :]
        xs = xs_ref[...]
        xdt = xs * dtx
        xdtb = xdt.astype(BF16)
        decx = jnp.exp(cs_last_x - csx)
        xdecf = xdt * decx
        xdec = xdecf.astype(BF16)
        ecsx = jnp.exp(csx)
        head_of_lane = lax.broadcasted_iota(jnp.int32, (1, GROUP_WIDTH), 1) // HEAD_DIM
        last_row = lax.broadcasted_iota(jnp.int32, (SSD_CHUNK, 1), 0) == SSD_CHUNK - 1
        dcs_col = jnp.zeros((SSD_CHUNK, LANE), F32)
        dcs_row = jnp.zeros((SSD_CHUNK, LANE), F32)
        ddt = jnp.zeros((SSD_CHUNK, LANE), F32)
        ddsk = jnp.zeros((1, LANE), F32)
        hsum = jnp.zeros((1, LANE), F32)
        t1_sum = jnp.zeros((1, LANE), F32)
        for g in range(SSD_GROUPS):
            gs = slice(g * GROUP_WIDTH, (g + 1) * GROUP_WIDTH)
            bsl = slice(g * SSD_STATE, (g + 1) * SSD_STATE)
            exp_g = exp_ref[:, gs]
            ind4_g = ind4_ref[g * HEADS_PER_GROUP * SSD_CHUNK:(g + 1) * HEADS_PER_GROUP * SSD_CHUNK, :]
            z = z_ref[:, gs]
            sg = _sigmoid(z)
            sz = z * sg
            ypre = ypre_ref[:, gs]
            yg = ypre * sz
            r = lax.rsqrt(jnp.mean(yg * yg, axis=-1, keepdims=True) + EPS)
            nrm = yg * r
            dyo_n = dy_ref[:, gs]
            dnw_ref[:, gs] += jnp.sum(dyo_n * nrm, axis=0, keepdims=True)
            dn = dyo_n * nw_ref[:, gs]
            dyg = r * (dn - nrm * jnp.mean(dn * nrm, axis=-1, keepdims=True))
            dz_ref[:, gs] = (dyg * ypre * (sg * (1.0 + z * (1.0 - sg)))).astype(BF16)
            dy = dyg * sz

            bg = b_ref[:, bsl].astype(BF16)
            cg = c_ref[:, bsl].astype(BF16)
            cb = _dot(cg, bg, NT)
            hprev = hall_ref[0, gs, :]
            hb = hprev.astype(BF16)
            dhn = dh_scr[gs, :]
            dhb = dhn.astype(BF16)
            xs_g, xdt_g = xs[:, gs], xdtb[:, gs]
            w_off = _dot(cg, hb, NT)
            dyo = dy * ecsx[:, gs]
            dyob = dyo.astype(BF16)
            dcg = _dot(dyob, hb, NN)
            dh_y = _dot(dyob, cg, TN)
            r_st = _dot(bg, dhb, NT)
            dbg = _dot(xdec[:, gs], dhb, NN)
            dyb = dy.astype(BF16)
            gms, gmbs, lms, dys = [], [], [], []
            for e in range(HEADS_PER_GROUP):
                h = g * HEADS_PER_GROUP + e
                lm = jnp.exp(jnp.where(tril, cs[:, h:h + 1] - cst[h:h + 1, :], -1e30))
                gm = cb * lm
                lms.append(lm)
                gms.append(gm)
                gmbs.append(gm.astype(BF16))
                dys.append(jnp.where(head_of_lane == e, dyb, jnp.zeros_like(dyb)))
            dxdt = _dot(jnp.concatenate(gmbs, axis=0), jnp.concatenate(dys, axis=0), TN) + decx[:, gs] * r_st
            dcb = jnp.zeros((SSD_CHUNK, SSD_CHUNK), F32)
            mms = []
            for e in range(HEADS_PER_GROUP):
                dg = _dot(dys[e], xdt_g, NT)
                mms.append(dg * gms[e])
                dcb = dcb + dg * lms[e]
            seg = _dot_split(jnp.concatenate([dyo * w_off, xdecf[:, gs] * r_st, dxdt * xs_g, dy * xs_g], axis=0), exp_g, NT, seg_passes)
            v1, t1, ddt_g, dsk_g = [seg[i * SSD_CHUNK:(i + 1) * SSD_CHUNK] for i in range(4)]
            dcs_col = dcs_col + v1 - t1 + _dot_split(jnp.concatenate(mms, axis=1), ind4_g, NN, seg_passes)
            for t in _split_bf16(jnp.concatenate(mms, axis=0), seg_passes):
                dcs_row = dcs_row + _dot(ind4_g, t, TN)
            ddt = ddt + ddt_g
            ddsk = ddsk + jnp.sum(dsk_g, axis=0, keepdims=True)
            t1_sum = t1_sum + jnp.sum(t1, axis=0, keepdims=True)
            for e in range(HEADS_PER_GROUP):
                h = g * HEADS_PER_GROUP + e
                hs = slice(e * HEAD_DIM, (e + 1) * HEAD_DIM)
                hsum = hsum + jnp.where(lane == h, jnp.sum(dhn[hs, :] * hprev[hs, :]).reshape(1, 1), 0.0)
            dh_scr[gs, :] = dhn * _chunk_decay_rows(cs, g) + dh_y
            dcbb = dcb.astype(BF16)
            dact_ref[:, gs] = dxdt * dtx[:, gs] + dskx_ref[:, gs] * dy
            dact_ref[:, SSD_D_INNER + g * SSD_STATE:SSD_D_INNER + (g + 1) * SSD_STATE] = dbg + _dot(dcbb, cg, TN)
            dact_ref[:, SSD_D_INNER + bc_w + g * SSD_STATE:SSD_D_INNER + bc_w + (g + 1) * SSD_STATE] = dcg + _dot(dcbb, bg, NN)
        dlast = t1_sum + jnp.exp(cs[SSD_CHUNK - 1:SSD_CHUNK, :]) * hsum
        dcs = dcs_col - dcs_row.T + jnp.where(last_row, dlast, 0.0)
        row = lax.broadcasted_iota(jnp.int32, (SSD_CHUNK, SSD_CHUNK), 0)
        col = lax.broadcasted_iota(jnp.int32, (SSD_CHUNK, SSD_CHUNK), 1)
        dda = _dot((col >= row).astype(F32), dcs, NN, precision=HIGHEST)
        ddt = ddt + dda * a
        da_ref[...] += jnp.sum(dda * dt, axis=0, keepdims=True)
        ddtr = jnp.where(lane < SSD_HEADS, ddt * _sigmoid(dtr + dtb_ref[...]), 0.0)
        ddtr_ref[...] = ddtr.astype(BF16)
        ddtb_ref[...] += jnp.sum(ddtr, axis=0, keepdims=True)
        ddsk_ref[...] += ddsk

    rev = lambda c: nc - 1 - c
    row_d = lambda cb: pl.BlockSpec((SSD_CHUNK, SSD_D_INNER), lambda c: (rev(c), cb))
    small = pl.BlockSpec((1, LANE), lambda c: (0, 0))
    wide = pl.BlockSpec((1, SSD_D_INNER), lambda c: (0, 0))
    small_shape = jax.ShapeDtypeStruct((1, LANE), F32)
    return _pcall(
        body, (dycat, ypre, proj, proj, act, act, act, hall, dtb, alog, _expand_heads(dsk), nw, exp_mat, ind4),
        name="ssd_bwd", grid=(nc,),
        in_specs=[row_d(0), row_d(0), row_d(OFF_Z // SSD_D_INNER),
                  pl.BlockSpec((SSD_CHUNK, LANE), lambda c: (rev(c), OFF_DT // LANE)),
                  row_d(0),
                  pl.BlockSpec((SSD_CHUNK, bc_w), lambda c: (rev(c), SSD_D_INNER // bc_w)),
                  pl.BlockSpec((SSD_CHUNK, bc_w), lambda c: (rev(c), SSD_D_INNER // bc_w + 1)),
                  pl.BlockSpec((1, SSD_D_INNER, SSD_STATE), lambda c: (rev(c), 0, 0)),
                  small, small, wide, wide, pl.BlockSpec((LANE, SSD_D_INNER), lambda c: (0, 0)),
                  pl.BlockSpec((SSD_HEADS * SSD_CHUNK, LANE), lambda c: (0, 0))],
        out_specs=[row_d(0), pl.BlockSpec((SSD_CHUNK, CONV_CH), lambda c: (rev(c), 0)),
                   pl.BlockSpec((SSD_CHUNK, LANE), lambda c: (rev(c), 0)), small, small, small, wide],
        out_shape=[jax.ShapeDtypeStruct((s, SSD_D_INNER), BF16), jax.ShapeDtypeStruct((s, CONV_CH), F32),
                   jax.ShapeDtypeStruct((s, LANE), BF16), small_shape, small_shape, small_shape,
                   jax.ShapeDtypeStruct((1, SSD_D_INNER), F32)],
        scratch_shapes=[pltpu.VMEM((SSD_D_INNER, SSD_STATE), F32)], sem=("arbitrary",), comm=comm)


def _head_mean_matrix():
    row = lax.broadcasted_iota(jnp.int32, (LANE, LANE), 0) // HEAD_DIM
    col = lax.broadcasted_iota(jnp.int32, (LANE, LANE), 1) // HEAD_DIM
    return (row == col).astype(F32)


def _head_sum2(v, ones_bd):
    hi = v.astype(BF16)
    lo = (v - hi.astype(F32)).astype(BF16)
    return _dot(jnp.concatenate([hi, lo], axis=1), jnp.concatenate([ones_bd, ones_bd], axis=0), NN)


def _head_norm(x, w, scale, ones_bd):
    ms = _head_sum2(x * x, ones_bd) * (1.0 / HEAD_DIM)
    return (x * lax.rsqrt(ms + EPS)) * (w * scale)


PRO_ROWS = 256
ATT_GROUP_FWD = 16
ATT_GROUP_BWD = 8
KEYS = 2 * ATT_BLK
NEG = -1e30
HALF = HEAD_DIM // 2


def _rows(start, size, dil):
    return pl.ds(start, size) if dil == 1 else pl.ds(start, size, stride=dil)


def _fill_bias(bias_ref):
    row = lax.broadcasted_iota(jnp.int32, (ATT_BLK, 2 * KEYS), 0)
    col = lax.broadcasted_iota(jnp.int32, (ATT_BLK, 2 * KEYS), 1) & (KEYS - 1)
    for first, off in ((0, 0), (1, ATT_BLK)):
        dist = off + row - col
        bias_ref[first] = jnp.where((dist >= 0) & (dist <= ATT_BLK), 0.0, NEG)


def _pair(a, b):
    return jnp.concatenate([jnp.broadcast_to(a, (ATT_BLK, KEYS)), jnp.broadcast_to(b, (ATT_BLK, KEYS))], axis=1)


def _split_heads(x, is_a):
    zero = jnp.zeros_like(x)
    return jnp.concatenate([jnp.where(is_a, x, zero), jnp.where(is_a, zero, x)], axis=0)


def _block_ids(b, nb):
    i = b & (nb - 1)
    q0 = pl.multiple_of(b * ATT_BLK, ATT_BLK)
    k0 = pl.multiple_of((b - jnp.minimum(i, 1)) * ATT_BLK, ATT_BLK)
    return pl.ds(q0, ATT_BLK), pl.ds(k0, KEYS), jnp.minimum(i, 1)


def _att_fwd(proj, qw, kw, comm=None):
    s = proj.shape[0]
    nblk = s // ATT_BLK
    assert all((s // d) // ATT_BLK >= 2 for d in DILATIONS)
    blk = lambda off: pl.BlockSpec((s, LANE), lambda i: (0, off // LANE + i))
    wspec = pl.BlockSpec((1, LANE), lambda i: (0, i))
    oblk = pl.BlockSpec((s, LANE), lambda i: (0, i))

    def body(q_ref, k_ref, v_ref, qw_ref, kw_ref, o_ref, lse_ref, qn, kn, q_cm, k_cm, v_cm, m_acc, l_acc, o_d, m_d, l_d, bias):
        ones_bd = _head_mean_matrix().astype(BF16)
        is_a = lax.broadcasted_iota(jnp.int32, (1, LANE), 1) < HEAD_DIM
        ones_ext = _split_heads(jnp.ones((KEYS, LANE), BF16), is_a)
        _fill_bias(bias)

        def pro(j, c):
            rows = pl.ds(pl.multiple_of(j * PRO_ROWS, PRO_ROWS), PRO_ROWS)
            qn[rows, :] = _head_norm(q_ref[rows, :], qw_ref[...], HEAD_DIM ** -0.5, ones_bd)
            kn[rows, :] = _head_norm(k_ref[rows, :], kw_ref[...], 1.0, ones_bd)
            return c

        lax.fori_loop(0, s // PRO_ROWS, pro, 0)

        for dil in DILATIONS:
            ln = s // dil
            nb = ln // ATT_BLK
            o_out, m_out, l_out = (o_ref, m_acc, l_acc) if dil == 1 else (o_d, m_d, l_d)
            for r in range(dil):
                def relayout(j, c, dil=dil, r=r, ln=ln):
                    j0 = pl.multiple_of(j * PRO_ROWS, PRO_ROWS)
                    src = _rows(r + dil * j0, PRO_ROWS, dil)
                    dst = pl.ds(r * ln + j0, PRO_ROWS)
                    q_cm[dst, :] = qn[src, :].astype(BF16)
                    k_cm[dst, :] = kn[src, :].astype(BF16)
                    v_cm[dst, :] = v_ref[src, :].astype(BF16)
                    return c

                lax.fori_loop(0, ln // PRO_ROWS, relayout, 0)

            def step(bg, c, nb=nb, o_out=o_out, m_out=m_out, l_out=l_out):
                ids = [_block_ids(bg * ATT_GROUP_FWD + u, nb) for u in range(ATT_GROUP_FWD)]
                kbs = [_split_heads(k_cm[krows, :], is_a) for _, krows, _ in ids]
                scs = [_dot(q_cm[qrows, :], kb, NT) + bias[first] for (qrows, _, first), kb in zip(ids, kbs)]
                mas = [jnp.max(sc[:, :KEYS], axis=-1, keepdims=True) for sc in scs]
                mbs = [jnp.max(sc[:, KEYS:], axis=-1, keepdims=True) for sc in scs]
                ps = [jnp.exp(sc - _pair(ma, mb)).astype(BF16) for sc, ma, mb in zip(scs, mas, mbs)]
                vbs = [jnp.concatenate([_split_heads(v_cm[krows, :], is_a), ones_ext], axis=1) for _, krows, _ in ids]
                ols = [_dot(p, vb, NN) for p, vb in zip(ps, vbs)]
                for (qrows, _, _), ol, ma, mb in zip(ids, ols, mas, mbs):
                    o_out[qrows, :] = ol[:, :LANE]
                    l_out[qrows, :] = ol[:, LANE:]
                    m_out[qrows, :] = jnp.where(is_a, ma, mb)
                return c

            lax.fori_loop(0, nblk // ATT_GROUP_FWD, step, 0)

            if dil > 1:
                for r in range(dil):
                    def merge(j, c, dil=dil, r=r, ln=ln):
                        j0 = pl.multiple_of(j * PRO_ROWS, PRO_ROWS)
                        nat = _rows(r + dil * j0, PRO_ROWS, dil)
                        cm = pl.ds(r * ln + j0, PRO_ROWS)
                        m_old, m_new = m_acc[nat, :], m_d[cm, :]
                        m = jnp.maximum(m_old, m_new)
                        a_old, a_new = jnp.exp(m_old - m), jnp.exp(m_new - m)
                        o_ref[nat, :] = a_old * o_ref[nat, :] + a_new * o_d[cm, :]
                        l_acc[nat, :] = a_old * l_acc[nat, :] + a_new * l_d[cm, :]
                        m_acc[nat, :] = m
                        return c

                    lax.fori_loop(0, ln // PRO_ROWS, merge, 0)

        def epi(j, c):
            rows = pl.ds(pl.multiple_of(j * PRO_ROWS, PRO_ROWS), PRO_ROWS)
            l = l_acc[rows, :]
            o_ref[rows, :] = o_ref[rows, :] / l
            lse_ref[rows, :] = m_acc[rows, :] + jnp.log(l)
            return c

        lax.fori_loop(0, s // PRO_ROWS, epi, 0)

    f = jax.ShapeDtypeStruct((s, ATT_D), F32)
    scr = pltpu.VMEM((s, LANE), F32)
    scb = pltpu.VMEM((s, LANE), BF16)
    return _pcall(
        body, (proj, proj, proj, qw, kw), name="att_fwd", grid=(ATT_D // LANE,),
        in_specs=[blk(OFF_Q), blk(OFF_K), blk(OFF_V), wspec, wspec], out_specs=[oblk, oblk], out_shape=[f, f],
        scratch_shapes=[scr, scr, scb, scb, scb, scr, scr, scr, scr, scr, pltpu.VMEM((2, ATT_BLK, 2 * KEYS), F32)],
        sem=("parallel",), comm=comm)


def _att_bwd(proj, do, stats, qw, kw, comm=None):
    s = proj.shape[0]
    nblk = s // ATT_BLK
    blk = lambda off: pl.BlockSpec((s, LANE), lambda i: (0, off // LANE + i))
    wspec = pl.BlockSpec((1, LANE), lambda i: (0, i))
    oblk = pl.BlockSpec((s, LANE), lambda i: (0, i))

    def body(q_ref, k_ref, v_ref, do_ref, st_ref, qw_ref, kw_ref, dq_ref, dk_ref, dv_ref, dqw_ref, dkw_ref,
             qn, kn, q_cm, do_cm, k_cm, v_cm, st_cm, dq_acc, dk_acc, dv_acc, dq_d, dk_d, dv_d, bias):
        ones_bd = _head_mean_matrix().astype(BF16)
        is_a = lax.broadcasted_iota(jnp.int32, (1, LANE), 1) < HEAD_DIM
        _fill_bias(bias)
        zero = jnp.zeros((PRO_ROWS, LANE), F32)

        def pro(j, c):
            rows = pl.ds(pl.multiple_of(j * PRO_ROWS, PRO_ROWS), PRO_ROWS)
            qn[rows, :] = _head_norm(q_ref[rows, :], qw_ref[...], HEAD_DIM ** -0.5, ones_bd)
            kn[rows, :] = _head_norm(k_ref[rows, :], kw_ref[...], 1.0, ones_bd)
            dk_acc[rows, :] = zero
            dv_acc[rows, :] = zero
            return c

        lax.fori_loop(0, s // PRO_ROWS, pro, 0)

        for dil in DILATIONS:
            ln = s // dil
            nb = ln // ATT_BLK
            dq_o, dk_o, dv_o = (dq_acc, dk_acc, dv_acc) if dil == 1 else (dq_d, dk_d, dv_d)
            for r in range(dil):
                def relayout(j, c, dil=dil, r=r, ln=ln):
                    j0 = pl.multiple_of(j * PRO_ROWS, PRO_ROWS)
                    src = _rows(r + dil * j0, PRO_ROWS, dil)
                    dst = pl.ds(r * ln + j0, PRO_ROWS)
                    q_cm[dst, :] = qn[src, :].astype(BF16)
                    k_cm[dst, :] = kn[src, :].astype(BF16)
                    v_cm[dst, :] = v_ref[src, :].astype(BF16)
                    do_cm[dst, :] = do_ref[src, :].astype(BF16)
                    st_cm[dst, :] = st_ref[src, :]
                    if dil > 1:
                        dk_d[dst, :] = zero
                        dv_d[dst, :] = zero
                    return c

                lax.fori_loop(0, ln // PRO_ROWS, relayout, 0)

            def step(bg, c, nb=nb, dq_o=dq_o, dk_o=dk_o, dv_o=dv_o):
                ids = [_block_ids(bg * ATT_GROUP_BWD + u, nb) for u in range(ATT_GROUP_BWD)]
                qbs = [q_cm[qrows, :] for qrows, _, _ in ids]
                dobs = [do_cm[qrows, :] for qrows, _, _ in ids]
                kbs = [_split_heads(k_cm[krows, :], is_a) for _, krows, _ in ids]
                vbs = [_split_heads(v_cm[krows, :], is_a) for _, krows, _ in ids]
                sts = [st_cm[qrows, :] for qrows, _, _ in ids]
                scs = [_dot(qb, kb, NT) + bias[first] for qb, kb, (_, _, first) in zip(qbs, kbs, ids)]
                dps = [_dot(dob, vb, NT) for dob, vb in zip(dobs, vbs)]
                ps = [jnp.exp(sc - _pair(st[:, 0:1], st[:, HEAD_DIM:HEAD_DIM + 1])) for sc, st in zip(scs, sts)]
                dss = [(p * (dp - _pair(st[:, HALF:HALF + 1], st[:, HEAD_DIM + HALF:HEAD_DIM + HALF + 1]))).astype(BF16)
                       for p, dp, st in zip(ps, dps, sts)]
                dqs = [_dot(ds, kb, NN) for ds, kb in zip(dss, kbs)]
                dkfs = [_dot(ds, qb, TN) for ds, qb in zip(dss, qbs)]
                dvfs = [_dot(p.astype(BF16), dob, TN) for p, dob in zip(ps, dobs)]
                for (qrows, krows, _), dq, dkf, dvf in zip(ids, dqs, dkfs, dvfs):
                    dq_o[qrows, :] = dq
                    dk_o[krows, :] += jnp.where(is_a, dkf[:KEYS], dkf[KEYS:])
                    dv_o[krows, :] += jnp.where(is_a, dvf[:KEYS], dvf[KEYS:])
                return c

            lax.fori_loop(0, nblk // ATT_GROUP_BWD, step, 0)

            if dil > 1:
                for r in range(dil):
                    def merge(j, c, dil=dil, r=r, ln=ln):
                        j0 = pl.multiple_of(j * PRO_ROWS, PRO_ROWS)
                        nat = _rows(r + dil * j0, PRO_ROWS, dil)
                        cm = pl.ds(r * ln + j0, PRO_ROWS)
                        dq_acc[nat, :] += dq_d[cm, :]
                        dk_acc[nat, :] += dk_d[cm, :]
                        dv_acc[nat, :] += dv_d[cm, :]
                        return c

                    lax.fori_loop(0, ln // PRO_ROWS, merge, 0)

        def back(dn_out, x, w, scale):
            r = lax.rsqrt(_head_sum2(x * x, ones_bd) * (1.0 / HEAD_DIM) + EPS)
            nrm = x * r
            dw = jnp.sum(dn_out * nrm, axis=0, keepdims=True) * scale
            dn = dn_out * (w * scale)
            return r * (dn - nrm * (_head_sum2(dn * nrm, ones_bd) * (1.0 / HEAD_DIM))), dw

        def epi(j, c):
            rows = pl.ds(pl.multiple_of(j * PRO_ROWS, PRO_ROWS), PRO_ROWS)
            dq, dqw = back(dq_acc[rows, :], q_ref[rows, :], qw_ref[...], HEAD_DIM ** -0.5)
            dk, dkw = back(dk_acc[rows, :], k_ref[rows, :], kw_ref[...], 1.0)
            dq_ref[rows, :] = dq.astype(BF16)
            dk_ref[rows, :] = dk.astype(BF16)
            dv_ref[rows, :] = dv_acc[rows, :].astype(BF16)
            return (c[0] + dqw, c[1] + dkw)

        zrow = jnp.zeros((1, LANE), F32)
        dqw, dkw = lax.fori_loop(0, s // PRO_ROWS, epi, (zrow, zrow))
        dqw_ref[...] = dqw
        dkw_ref[...] = dkw

    o = jax.ShapeDtypeStruct((s, ATT_D), BF16)
    ov = jax.ShapeDtypeStruct((1, ATT_D), F32)
    scr = pltpu.VMEM((s, LANE), F32)
    scb = pltpu.VMEM((s, LANE), BF16)
    return _pcall(
        body, (proj, proj, proj, do, stats, qw, kw), name="att_bwd", grid=(ATT_D // LANE,),
        in_specs=[blk(OFF_Q), blk(OFF_K), blk(OFF_V), oblk, oblk, wspec, wspec],
        out_specs=[oblk, oblk, oblk, wspec, wspec], out_shape=[o, o, o, ov, ov],
        scratch_shapes=[scr, scr, scb, scb, scb, scb, scr, scr, scr, scr, scr, scr, scr, pltpu.VMEM((2, ATT_BLK, 2 * KEYS), F32)],
        sem=("parallel",), comm=comm)


def _att_norm_fwd(o, nw, ycat):
    s = o.shape[0]
    row = pl.BlockSpec((ROW_TILE, ATT_D), lambda i: (i, 0))
    vec = pl.BlockSpec((1, ATT_D), lambda i: (0, 0))

    def body(o_ref, nw_ref, ycat_ref, y_ref):
        o = o_ref[...]
        r = lax.rsqrt(jnp.mean(o * o, axis=-1, keepdims=True) + EPS)
        y_ref[...] = (o * r * nw_ref[...]).astype(BF16)

    return pl.pallas_call(body, name="att_norm_fwd", grid=(s // ROW_TILE,),
                          in_specs=[row, vec, pl.BlockSpec(memory_space=pl.ANY)],
                          out_specs=pl.BlockSpec((ROW_TILE, ATT_D), lambda i: (i, 1)),
                          out_shape=jax.ShapeDtypeStruct(ycat.shape, BF16), input_output_aliases={2: 0},
                          compiler_params=_cparams(("parallel",)))(o, nw, ycat)


def _mixer_split_epilogue(dycat, first, rows, vecs, outs):
    (o_ref, lse_ref), (nw_ref,), (dyssd_ref, do_ref, st_ref, dnw_ref) = rows, vecs, outs

    @pl.when(first)
    def _():
        dnw_ref[...] = jnp.zeros_like(dnw_ref)

    dyssd_ref[...] = dycat[:, :SSD_D_INNER]
    dy = dycat[:, SSD_D_INNER:]
    o = o_ref[...]
    r = lax.rsqrt(jnp.mean(o * o, axis=-1, keepdims=True) + EPS)
    nrm = o * r
    dnw_ref[...] += jnp.sum(dy * nrm, axis=0, keepdims=True)
    dn = dy * nw_ref[...]
    do = r * (dn - nrm * jnp.mean(dn * nrm, axis=-1, keepdims=True))
    do_ref[...] = do
    ones_bd = _head_mean_matrix().astype(BF16)
    prod = do * o
    delta = jnp.concatenate([_head_sum2(prod[:, j * LANE:(j + 1) * LANE], ones_bd) for j in range(ATT_D // LANE)], axis=1)
    lane = lax.broadcasted_iota(jnp.int32, (1, ATT_D), 1)
    st_ref[...] = jnp.where((lane & (HEAD_DIM - 1)) < HALF, lse_ref[...], delta)


def _ada_fwd(c_all, w_ada):
    def body(c_ref, w_ref, o_ref):
        cv = c_ref[...]
        o_ref[...] = _dot((cv * _sigmoid(cv)).astype(BF16), w_ref[...].astype(BF16), NN)

    return pl.pallas_call(body, name="ada_fwd", out_shape=jax.ShapeDtypeStruct((c_all.shape[0], w_ada.shape[1]), F32),
                          compiler_params=_cparams())(c_all, w_ada)


def _adamw_math(g, w, m, v):
    m_new = ADAM_B1 * m + (1.0 - ADAM_B1) * g
    v_new = ADAM_B2 * v + (1.0 - ADAM_B2) * (g * g)
    m_hat = m_new / (1.0 - ADAM_B1 ** ADAM_STEP)
    v_hat = v_new / (1.0 - ADAM_B2 ** ADAM_STEP)
    delta = -ADAM_LR * (m_hat / (jnp.sqrt(v_hat) + ADAM_EPS) + ADAM_WD * w)
    return delta, m_new, v_new


def _ada_bwd_adamw(c_all, dmod_cols, w, m, v):
    rows, cols = w.shape
    tr = 256
    blk = pl.BlockSpec((tr, cols), lambda i: (i, 0))

    def body(c_ref, d_ref, w_ref, m_ref, v_ref, g_ref, dl_ref, mo_ref, vo_ref):
        cv = c_ref[...]
        ca = cv * _sigmoid(cv)
        g = ca[:, 0:1] * d_ref[0:1, :]
        for b in range(1, N_DEV):
            g = g + ca[:, b:b + 1] * d_ref[b:b + 1, :]
        g_ref[...] = g
        dl_ref[...], mo_ref[...], vo_ref[...] = _adamw_math(g, w_ref[...], m_ref[...], v_ref[...])

    o = jax.ShapeDtypeStruct((rows, cols), F32)
    return pl.pallas_call(
        body, name="ada_bwd_adamw", grid=(rows // tr,),
        in_specs=[pl.BlockSpec((tr, N_DEV), lambda i: (i, 0)), pl.BlockSpec((N_DEV, cols), lambda i: (0, 0)), blk, blk, blk],
        out_specs=[blk] * 4, out_shape=[o, o, o, o], compiler_params=_cparams(("parallel",)))(c_all.T, dmod_cols, w, m, v)


def _reduce_adamw(slabs, w, m, v, name):
    rows, cols = w.shape
    n_src = slabs.shape[0]
    if rows % 128 == 0:
        tr, steps = 128, rows // 128
        blk = pl.BlockSpec((tr, cols), lambda i: (i, 0))
        sblk = pl.BlockSpec((n_src, tr, cols), lambda i: (0, i, 0))
    else:
        tc, steps = 256, cols // 256
        blk = pl.BlockSpec((rows, tc), lambda i: (0, i))
        sblk = pl.BlockSpec((n_src, rows, tc), lambda i: (0, 0, i))

    def body(s_ref, w_ref, m_ref, v_ref, g_ref, dl_ref, mo_ref, vo_ref):
        g = s_ref[0].astype(F32)
        for src in range(1, n_src):
            g = g + s_ref[src].astype(F32)
        g_ref[...] = g
        dl_ref[...], mo_ref[...], vo_ref[...] = _adamw_math(g, w_ref[...], m_ref[...], v_ref[...])

    o = jax.ShapeDtypeStruct((rows, cols), F32)
    return pl.pallas_call(
        body, name=name, grid=(steps,), in_specs=[sblk, blk, blk, blk],
        out_specs=[blk] * 4, out_shape=[o, o, o, o], compiler_params=_cparams(("parallel",)))(slabs, w, m, v)


def _small_reduce_adamw(gathered, w, m, v):
    def body(s_ref, w_ref, m_ref, v_ref, g_ref, dl_ref, mo_ref, vo_ref):
        g = s_ref[0]
        for dev in range(1, N_DEV):
            g = g + s_ref[dev]
        g_ref[...] = g
        dl_ref[...], mo_ref[...], vo_ref[...] = _adamw_math(g, w_ref[...], m_ref[...], v_ref[...])

    o = jax.ShapeDtypeStruct(w.shape, F32)
    return pl.pallas_call(body, name="small_reduce_adamw", out_shape=[o, o, o, o], compiler_params=_cparams())(gathered, w, m, v)


def _adamw_small(g, w, m, v, name):
    def body(g_ref, w_ref, m_ref, v_ref, dl_ref, mo_ref, vo_ref):
        dl_ref[...], mo_ref[...], vo_ref[...] = _adamw_math(g_ref[...], w_ref[...], m_ref[...], v_ref[...])

    o = jax.ShapeDtypeStruct(w.shape, F32)
    return pl.pallas_call(body, name=name, out_shape=[o, o, o], compiler_params=_cparams())(g, w, m, v)


class _Exchange:
    def __init__(self, arrs, scatter):
        self.arrs, self.scatter, self.n = list(arrs), scatter, len(arrs)
        hbm = pl.BlockSpec(memory_space=pltpu.HBM)
        self.in_specs = [hbm] * self.n
        self.out_specs = [hbm] * self.n
        self.out_shape = [jax.ShapeDtypeStruct(a.shape if scatter else (N_DEV,) + a.shape, a.dtype) for a in self.arrs]
        self.scratch = [pltpu.SemaphoreType.DMA((self.n * (N_DEV - 1),)), pltpu.SemaphoreType.DMA((self.n * (N_DEV - 1),)),
                        pltpu.SemaphoreType.DMA((self.n,))]

    def _local(self, ins, outs, sems):
        me = 4 * lax.axis_index("x") + 2 * lax.axis_index("y") + lax.axis_index("c")
        return [pltpu.make_async_copy(ins[a].at[me] if self.scatter else ins[a], outs[a].at[me], sems[2].at[a])
                for a in range(self.n)]

    def _remote(self, ins, outs, sems, arriving):
        send_sems, recv_sems, _ = sems
        x, y, c = lax.axis_index("x"), lax.axis_index("y"), lax.axis_index("c")
        me = 4 * x + 2 * y + c
        remote = []
        for a in range(self.n):
            for k in range(1, N_DEV):
                px = 1 - x if k & 4 else x
                py = 1 - y if k & 2 else y
                pc = 1 - c if k & 1 else c
                peer = 4 * px + 2 * py + pc
                sem = a * (N_DEV - 1) + k - 1
                remote.append(pltpu.make_async_remote_copy(
                    src_ref=ins[a].at[peer] if self.scatter else ins[a], dst_ref=outs[a].at[peer if arriving else me],
                    send_sem=send_sems.at[sem], recv_sem=recv_sems.at[sem], device_id=(px, py, pc), device_id_type=MESH_IDS))
        return remote

    def start(self, ins, outs, sems):
        for cp in self._local(ins, outs, sems) + self._remote(ins, outs, sems, arriving=False):
            cp.start()

    def forward(self, ins, outs, sems):
        pass

    def wait(self, ins, outs, sems):
        for send, arrival in zip(self._remote(ins, outs, sems, arriving=False), self._remote(ins, outs, sems, arriving=True)):
            send.wait_send()
            arrival.wait_recv()
        for cp in self._local(ins, outs, sems):
            cp.wait()


N_CHIP = N_DEV // 2


class _SiblingSwap(_Exchange):
    def __init__(self, arrs):
        super().__init__(arrs, scatter=True)
        self.out_shape = [jax.ShapeDtypeStruct((N_CHIP,) + a.shape[2:], a.dtype) for a in self.arrs]
        self.scratch = [pltpu.SemaphoreType.DMA((self.n,)), pltpu.SemaphoreType.DMA((self.n,)), pltpu.SemaphoreType.DMA((1,))]

    def _copies(self, ins, outs, sems):
        x, y, c = lax.axis_index("x"), lax.axis_index("y"), lax.axis_index("c")
        return [pltpu.make_async_remote_copy(src_ref=ins[a].at[:, 1 - c], dst_ref=outs[a], send_sem=sems[0].at[a], recv_sem=sems[1].at[a],
                                             device_id=(x, y, 1 - c), device_id_type=MESH_IDS) for a in range(self.n)]

    def start(self, ins, outs, sems):
        for cp in self._copies(ins, outs, sems):
            cp.start()

    def wait(self, ins, outs, sems):
        for cp in self._copies(ins, outs, sems):
            cp.wait()


class _ChipScatter(_Exchange):
    def __init__(self, arrs):
        super().__init__(arrs, scatter=True)
        n_pairs = self.n * (N_CHIP - 1)
        self.scratch = [pltpu.SemaphoreType.DMA((n_pairs,)), pltpu.SemaphoreType.DMA((n_pairs,)), pltpu.SemaphoreType.DMA((self.n,))]

    def _local(self, ins, outs, sems):
        chip = 2 * lax.axis_index("x") + lax.axis_index("y")
        return [pltpu.make_async_copy(ins[a].at[chip], outs[a].at[chip], sems[2].at[a]) for a in range(self.n)]

    def _remote(self, ins, outs, sems, arriving):
        send_sems, recv_sems, _ = sems
        x, y, c = lax.axis_index("x"), lax.axis_index("y"), lax.axis_index("c")
        chip = 2 * x + y
        remote = []
        for a in range(self.n):
            for k in range(1, N_CHIP):
                px = 1 - x if k & 2 else x
                py = 1 - y if k & 1 else y
                peer = 2 * px + py
                sem = a * (N_CHIP - 1) + k - 1
                remote.append(pltpu.make_async_remote_copy(
                    src_ref=ins[a].at[peer], dst_ref=outs[a].at[peer if arriving else chip], send_sem=send_sems.at[sem],
                    recv_sem=recv_sems.at[sem], device_id=(px, py, c), device_id_type=MESH_IDS))
        return remote


def _chip_sum(mine, theirs):
    n, rows, cols = mine.shape
    blk = pl.BlockSpec((1, rows, 256), lambda q, j: (q, 0, j))

    def body(a_ref, b_ref, o_ref):
        o_ref[...] = (a_ref[...].astype(F32) + b_ref[...].astype(F32)).astype(BF16)

    return pl.pallas_call(body, name="chip_sum", grid=(n, cols // 256), in_specs=[blk, blk], out_specs=blk,
                          out_shape=jax.ShapeDtypeStruct(mine.shape, BF16),
                          compiler_params=_cparams(("parallel", "parallel")))(mine, theirs)


class _Gather2(_Exchange):
    def __init__(self, arrs):
        super().__init__(arrs, scatter=False)

    def _copies(self, ins, outs, sems):
        send_sems, recv_sems, _ = sems
        x, y, c = lax.axis_index("x"), lax.axis_index("y"), lax.axis_index("c")
        sibling = (x, y, 1 - c)
        chips = [(1 - x, y), (x, 1 - y), (1 - x, 1 - y)]
        first, passed, landed = [], [], []
        for a in range(self.n):
            def copy(k, block, to, src=None, a=a):
                slab = outs[a].at[4 * block[0] + 2 * block[1] + block[2]]
                return pltpu.make_async_remote_copy(
                    src_ref=slab if src is None else src, dst_ref=slab, send_sem=send_sems.at[a * (N_DEV - 1) + k],
                    recv_sem=recv_sems.at[a * (N_DEV - 1) + k], device_id=to, device_id_type=MESH_IDS)

            first.append(copy(0, (x, y, c), sibling, src=ins[a]))
            landed.append(copy(0, sibling, sibling))
            for j, chip in enumerate(chips):
                first.append(copy(1 + j, (x, y, c), (*chip, c), src=ins[a]))
                passed.append((copy(1 + j, (*chip, c), sibling), copy(4 + j, (*chip, c), sibling)))
                landed.append(copy(4 + j, (*chip, 1 - c), sibling))
        return first, passed, landed

    def start(self, ins, outs, sems):
        for cp in self._local(ins, outs, sems) + self._copies(ins, outs, sems)[0]:
            cp.start()

    def forward(self, ins, outs, sems):
        for arrival, onward in self._copies(ins, outs, sems)[1]:
            arrival.wait_recv()
            onward.start()

    def wait(self, ins, outs, sems):
        first, passed, landed = self._copies(ins, outs, sems)
        for arrival in landed:
            arrival.wait_recv()
        for cp in first + [onward for _, onward in passed]:
            cp.wait_send()
        for cp in self._local(ins, outs, sems):
            cp.wait()


def _split_comm_refs(refs, n_in, n_out, n_scr, comm):
    nc = comm.n if comm is not None else 0
    ns = 3 if comm is not None else 0
    pos, groups = 0, []
    for cnt in (n_in, nc, n_out, nc, n_scr, ns):
        groups.append(refs[pos:pos + cnt])
        pos += cnt
    assert pos == len(refs), (pos, len(refs))
    return groups


def _pcall(body, args, *, name, grid, in_specs, out_specs, out_shape, scratch_shapes=(), sem=None, comm=None):
    in_specs, out_specs, out_shape, scratch_shapes = list(in_specs), list(out_specs), list(out_shape), list(scratch_shapes)
    n_in, n_out, n_scr = len(in_specs), len(out_specs), len(scratch_shapes)
    if comm is None:
        kernel_body = body
    else:
        def kernel_body(*refs):
            ins, cins, outs, couts, scr, sems = _split_comm_refs(refs, n_in, n_out, n_scr, comm)
            ids = [pl.program_id(a) for a in range(len(grid))]
            first, last = ids[0] == 0, ids[0] == grid[0] - 1
            for a in range(1, len(grid)):
                first, last = first & (ids[a] == 0), last & (ids[a] == grid[a] - 1)

            middle = ids[0] == (2 * grid[0]) // 3
            for a in range(1, len(grid)):
                middle = middle & (ids[a] == 0)

            @pl.when(first)
            def _():
                comm.start(cins, couts, sems)

            @pl.when(middle)
            def _():
                comm.forward(cins, couts, sems)

            body(*ins, *outs, *scr)

            @pl.when(last)
            def _():
                comm.wait(cins, couts, sems)

        in_specs, out_specs, out_shape = in_specs + comm.in_specs, out_specs + comm.out_specs, out_shape + comm.out_shape
        scratch_shapes, args = scratch_shapes + comm.scratch, list(args) + comm.arrs
        sem = ("arbitrary",) * len(grid)
    res = pl.pallas_call(kernel_body, name=name, grid=grid, in_specs=in_specs, out_specs=out_specs, out_shape=out_shape,
                         scratch_shapes=scratch_shapes, compiler_params=_cparams(sem))(*args)
    return res[:n_out], res[n_out:]


def _exchange(arrs, name, scatter=False, ex=None):
    if ex is None:
        ex = _Exchange(arrs, scatter=True) if scatter else _Gather2(arrs)

    def body(*refs):
        _, ins, _, outs, _, sems = _split_comm_refs(refs, 0, 0, 0, ex)
        ex.start(ins, outs, sems)
        ex.forward(ins, outs, sems)
        ex.wait(ins, outs, sems)

    return pl.pallas_call(body, name=name, in_specs=ex.in_specs, out_specs=ex.out_specs, out_shape=ex.out_shape,
                          scratch_shapes=ex.scratch)(*ex.arrs)


def _pad_lanes(v, width=LANE):
    return jnp.pad(v, ((0, 0), (0, width - v.shape[1])))


def _shards_to_cols(g):
    return jnp.transpose(g, (1, 0, 2)).reshape(g.shape[1], N_DEV * g.shape[2])


def _cols_to_shards(w):
    return w.astype(BF16).reshape(w.shape[0], N_DEV, w.shape[1] // N_DEV).transpose(1, 0, 2)


def _local_step(x, tgt, mod, w_in_pt, conv_w, conv_b, dt_bias, a_log, d_skip, ssd_norm_w, q_norm_w, k_norm_w,
                attn_norm_w, w_out_sh, w_ff1_sh, w_ff2_sh, norm1_w, norm2_w, core):
    shift1, scale1, gate1, shift2, scale2, gate2 = [mod[i:i + 1] for i in range(N_MOD)]
    dtb, alog, dsk = _pad_lanes(dt_bias), _pad_lanes(a_log), _pad_lanes(d_skip)
    qw, kw = jnp.tile(q_norm_w, (1, ATT_HEADS)), jnp.tile(k_norm_w, (1, ATT_HEADS))

    h1 = _norm_mod_fwd(x, norm1_w, scale1, shift1, "norm1_fwd")
    proj = _matmul(h1, w_in_pt, tb=True, tm=2048, tn=896, tk=1024, name="in_proj")
    pre, act = _conv_fwd(proj, conv_w, conv_b)
    ypre, ycat_ssd, hall = _ssd_fwd(proj, act, dtb, alog, dsk, ssd_norm_w)
    (o_att, lse), (w_out_g, w_ff1_g, w_ff2_g) = _att_fwd(proj, qw, kw, comm=_Gather2([w_out_sh, w_ff1_sh, w_ff2_sh]))
    w_out = w_out_g.reshape(2 * D_MODEL, D_MODEL)
    w_ff1 = _shards_to_cols(w_ff1_g)
    w_ff2 = w_ff2_g.reshape(D_FF, D_MODEL)
    ycat = _att_norm_fwd(o_att, attn_norm_w, ycat_ssd)
    row32, row16, vec32 = ("row", F32), ("row", BF16), ("vec", F32)
    mix, x1, h2 = _matmul_rows(ycat, w_out, _residual_norm_epilogue, [x], [gate1, norm2_w, scale2, shift2],
                               [row32, row32, row16], tm=512, name="out_proj")
    u, act_ff = _matmul(h2, w_ff1, tm=1024, tn=1024, tk=1024, name="ff1", mode="relu2")
    loss, dout, dff, dgate2 = _matmul_rows(act_ff, w_ff2, _loss_epilogue, [x1, tgt], [gate2],
                                           [("one", F32), row32, row16, vec32], tm=512, name="ff2")

    du = _matmul(dff, w_ff2, tb=True, tm=1024, tn=1024, tk=1024, out_dtype=BF16, name="ff2_dx", mode="drelu2", u=u)
    g_ff2 = _matmul(act_ff, dff, ta=True, tm=512, tn=1024, tk=4096, out_dtype=BF16, name="ff2_dw")
    dx1, dshift2, dscale2, g_norm2, dmix, dgate1 = _matmul_rows(
        du, w_ff1, _norm_bwd_epilogue, [x1, dout, mix], [norm2_w, scale2, gate1],
        [row32, vec32, vec32, vec32, row16, vec32], tb=True, tm=512, name="ff1_dx")
    g_ff1 = _matmul(h2, du, ta=True, tm=512, tn=1024, tk=4096, out_dtype=BF16, name="ff1_dw")

    dy_ssd, do, stats, g_attn_norm = _matmul_rows(
        dmix, w_out, _mixer_split_epilogue, [o_att, lse], [attn_norm_w],
        [("row", F32, SSD_D_INNER), ("row", F32, ATT_D), ("row", F32, ATT_D), ("vec", F32, ATT_D)], tb=True, tm=512, name="out_proj_dx")
    g_out = _matmul(ycat, dmix, ta=True, tm=512, tn=1024, tk=4096, out_dtype=BF16, name="out_proj_dw")
    ff_slabs = [_cols_to_shards(g_ff1), g_ff2.astype(BF16).reshape(N_DEV, D_FF // N_DEV, D_MODEL)]
    (dq, dk, dv, dqw, dkw), (s_ff1, s_ff2) = _att_bwd(proj, do, stats, qw, kw, comm=_Exchange(ff_slabs, scatter=True))
    out_slabs = [g_out.astype(BF16).reshape(N_DEV, 2 * D_MODEL // N_DEV, D_MODEL)]
    (dz, dact, ddtr, da, g_dsk, g_dtb, g_ssd_norm), (s_out,) = _ssd_bwd(
        dy_ssd, ypre, proj, act, hall, dtb, alog, dsk, ssd_norm_w, comm=_Exchange(out_slabs, scatter=True))
    dxbc, g_conv_w, g_conv_b = _conv_bwd(dact, pre, proj, conv_w)
    dproj = jnp.concatenate([dz, dxbc, dq, dk, dv, ddtr], axis=1)
    g_in_pt = _matmul(dproj, h1, ta=True, tm=896, tn=1024, tk=4096, out_dtype=BF16, name="in_proj_dw")
    in_slabs = _unpack_w_in_rows(g_in_pt).reshape(N_CHIP, 2, IN_W // N_DEV, D_MODEL)
    (sibling_slabs,) = _exchange(None, "swap_w_in_grads", ex=_SiblingSwap([in_slabs]))
    chip_slabs = _chip_sum(lax.dynamic_index_in_dim(in_slabs, core, axis=1, keepdims=False), sibling_slabs)
    (grad_x, dshift1, dscale1, g_norm1), (s_in,) = _matmul_rows(
        dproj, w_in_pt, _norm_bwd_epilogue, [x, dx1], [norm1_w, scale1], [row32, vec32, vec32, vec32],
        tm=256, name="in_proj_dx", comm=_ChipScatter([chip_slabs]))

    dmod = jnp.concatenate([dshift1, dscale1, dgate1, dshift2, dscale2, dgate2], axis=0)
    g_alog = da[:, :SSD_HEADS] * (-jnp.exp(a_log))
    g_qw = dqw.reshape(ATT_HEADS, HEAD_DIM).sum(axis=0, keepdims=True)
    g_kw = dkw.reshape(ATT_HEADS, HEAD_DIM).sum(axis=0, keepdims=True)
    return dict(loss=loss, grad_x=grad_x, dmod=dmod, norm1_w=g_norm1, norm2_w=g_norm2, w_in=s_in, conv_w=g_conv_w,
                conv_b=g_conv_b, dt_bias=g_dtb[:, :SSD_HEADS], a_log=g_alog, d_skip=g_dsk[:, :SSD_HEADS],
                ssd_norm_w=g_ssd_norm, q_norm_w=g_qw, k_norm_w=g_kw, attn_norm_w=g_attn_norm, w_out=s_out,
                w_ff1=s_ff1, w_ff2=s_ff2)


def _pack_w_in_rows(wt_full):
    o_dt = SSD_D_INNER + CONV_CH
    o_q = o_dt + SSD_HEADS
    pad = jnp.zeros((LANE - SSD_HEADS, wt_full.shape[1]), wt_full.dtype)
    return jnp.concatenate([wt_full[:o_dt], wt_full[o_q:], wt_full[o_dt:o_q], pad], axis=0)


def _unpack_w_in_rows(gt_p):
    return jnp.concatenate([gt_p[:OFF_Q], gt_p[OFF_DT:OFF_DT + SSD_HEADS], gt_p[OFF_Q:OFF_DT]], axis=0)


MISC_FIELDS = (("dt_bias", SSD_HEADS), ("a_log", SSD_HEADS), ("d_skip", SSD_HEADS), ("q_norm_w", HEAD_DIM), ("k_norm_w", HEAD_DIM))
SMALL_LAYOUT = (("b_ada", 6), ("norm1_w", 1), ("norm2_w", 1), ("conv_w", 8), ("conv_b", 2), ("ssd_norm_w", 1),
                ("attn_norm_w", 1), ("misc", 1))


def _pack_small(vals):
    rows = []
    for name, nrow in SMALL_LAYOUT:
        if name == "misc":
            misc = jnp.concatenate([vals[f].reshape(1, n) for f, n in MISC_FIELDS], axis=1)
            rows.append(_pad_lanes(misc, D_MODEL))
        elif name in vals:
            rows.append(vals[name].reshape(nrow, D_MODEL))
        else:
            rows.append(jnp.zeros((nrow, D_MODEL), F32))
    used = sum(n for _, n in SMALL_LAYOUT)
    rows.append(jnp.zeros((SMALL_ROWS - used, D_MODEL), F32))
    return jnp.concatenate(rows, axis=0)


def _unpack_small(packed):
    out, r = {}, 0
    for name, nrow in SMALL_LAYOUT:
        blk = packed[r:r + nrow]
        r += nrow
        if name == "misc":
            c0 = 0
            for f, n in MISC_FIELDS:
                out[f] = blk[:, c0:c0 + n]
                c0 += n
        elif name == "b_ada":
            out[name] = blk.reshape(1, N_MOD * D_MODEL)
        elif name == "conv_w":
            out[name] = blk.reshape(CONV_K, CONV_CH)
        elif name == "conv_b":
            out[name] = blk.reshape(1, CONV_CH)
        else:
            out[name] = blk
    return out


WEIGHT_NAMES = ("norm1_w", "norm2_w", "w_ada", "b_ada", "w_in", "conv_w", "conv_b", "dt_bias", "a_log", "d_skip",
                "ssd_norm_w", "q_norm_w", "k_norm_w", "attn_norm_w", "w_out", "w_ff1", "w_ff2")
SMALL_NAMES = ("norm1_w", "norm2_w", "b_ada", "conv_b", "dt_bias", "a_log", "d_skip", "ssd_norm_w", "q_norm_w",
               "k_norm_w", "attn_norm_w")


def kernel(x, c, norm1_w, norm2_w, w_ada, b_ada, w_in, conv_w, conv_b, dt_bias, a_log, d_skip, ssd_norm_w, q_norm_w, k_norm_w, attn_norm_w, w_out, w_ff1, w_ff2, loss_target, m_norm1_w, m_norm2_w, m_w_ada, m_b_ada, m_w_in, m_conv_w, m_conv_b, m_dt_bias, m_a_log, m_d_skip, m_ssd_norm_w, m_q_norm_w, m_k_norm_w, m_attn_norm_w, m_w_out, m_w_ff1, m_w_ff2, v_norm1_w, v_norm2_w, v_w_ada, v_b_ada, v_w_in, v_conv_w, v_conv_b, v_dt_bias, v_a_log, v_d_skip, v_ssd_norm_w, v_q_norm_w, v_k_norm_w, v_attn_norm_w, v_w_out, v_w_ff1, v_w_ff2):
    args = dict(locals())
    w = {n: args[n] for n in WEIGHT_NAMES}
    m = {n: args["m_" + n] for n in WEIGHT_NAMES}
    v = {n: args["v_" + n] for n in WEIGHT_NAMES}
    me = 4 * lax.axis_index("x") + 2 * lax.axis_index("y") + lax.axis_index("c")

    c_rows = jnp.pad(c, ((0, 7), (0, 0)))
    w_in_t, m_in_t, v_in_t = [jnp.transpose(t["w_in"][0]) for t in (w, m, v)]
    c_g, conv_g, w_in_g = _exchange([c_rows, w["conv_w"][0], w_in_t.astype(BF16)], "gather_w_in", scatter=False)
    c_all = c_g[:, 0, :]
    conv_full = _shards_to_cols(conv_g)
    w_in_pt = _pack_w_in_rows(w_in_g.reshape(IN_W, D_MODEL))

    mod_part = _ada_fwd(c_all, w["w_ada"][0])
    (mod_g,) = _exchange([mod_part], "gather_mod", scatter=False)
    mod_mine = lax.dynamic_index_in_dim(mod_g, me, axis=1, keepdims=False).reshape(1, N_MOD * D_MODEL) + w["b_ada"]
    mod = mod_mine.reshape(N_MOD, D_MODEL)

    res = _local_step(x[0], loss_target[0], mod, w_in_pt, conv_full, w["conv_b"], w["dt_bias"], w["a_log"], w["d_skip"],
                      w["ssd_norm_w"], w["q_norm_w"], w["k_norm_w"], w["attn_norm_w"], w["w_out"][0].astype(BF16),
                      w["w_ff1"][0].astype(BF16), w["w_ff2"][0].astype(BF16), w["norm1_w"], w["norm2_w"], lax.axis_index("c"))

    small_vals = {n: res[n] for n in SMALL_NAMES if n != "b_ada"}
    small_vals["b_ada"] = res["dmod"]
    small_vals["conv_w"] = res["conv_w"]
    (small_g,) = _exchange([_pack_small(small_vals)], "gather_small", scatter=False)

    grads, delta, new_m, new_v = {}, {}, {}, {}
    for name in ("w_out", "w_ff1", "w_ff2"):
        outs = _reduce_adamw(res[name], w[name][0], m[name][0], v[name][0], "adamw_" + name)
        grads[name], delta[name], new_m[name], new_v[name] = [o[None] for o in outs]
    outs = _reduce_adamw(res["w_in"], w_in_t, m_in_t, v_in_t, "adamw_w_in")
    grads["w_in"], delta["w_in"], new_m["w_in"], new_v["w_in"] = [jnp.transpose(o)[None] for o in outs]

    sm = _small_reduce_adamw(small_g, _pack_small({n: w[n] for n in SMALL_NAMES}), _pack_small({n: m[n] for n in SMALL_NAMES}),
                             _pack_small({n: v[n] for n in SMALL_NAMES}))
    sm = [_unpack_small(p) for p in sm]
    for n in SMALL_NAMES:
        grads[n], delta[n], new_m[n], new_v[n] = [p[n] for p in sm]
    shard_w = CONV_CH // N_DEV
    g_conv = lax.dynamic_slice_in_dim(sm[0]["conv_w"], me * shard_w, shard_w, axis=1)
    cw = _adamw_small(g_conv, w["conv_w"][0], m["conv_w"][0], v["conv_w"][0], "adamw_conv_w")
    grads["conv_w"] = g_conv[None]
    delta["conv_w"], new_m["conv_w"], new_v["conv_w"] = [o[None] for o in cw]

    ada_w = w_ada.shape[2]
    dmod_all = small_g[:, :N_MOD, :].reshape(N_DEV, N_MOD * D_MODEL)
    dmod_cols = lax.dynamic_slice_in_dim(dmod_all, me * ada_w, ada_w, axis=1)
    outs = _ada_bwd_adamw(c_all, dmod_cols, w["w_ada"][0], m["w_ada"][0], v["w_ada"][0])
    grads["w_ada"], delta["w_ada"], new_m["w_ada"], new_v["w_ada"] = [o[None] for o in outs]

    loss = lax.psum(res["loss"][0, 0], ("x", "y", "c"))
    return (loss, res["grad_x"][None], *[grads[n] for n in WEIGHT_NAMES], *[delta[n] for n in WEIGHT_NAMES],
            *[new_m[n] for n in WEIGHT_NAMES], *[new_v[n] for n in WEIGHT_NAMES])
```

```python
import functools

import jax
import jax.numpy as jnp
from jax import lax
from jax.experimental import pallas as pl
from jax.experimental.pallas import tpu as pltpu

F32 = jnp.float32
BF16 = jnp.bfloat16
HIGHEST = lax.Precision.HIGHEST
MESH_IDS = pl.DeviceIdType.MESH

N_DEV = 8
D_MODEL = 1024
HEAD_DIM = 64
SSD_HEADS = 16
SSD_GROUPS = 4
HEADS_PER_GROUP = SSD_HEADS // SSD_GROUPS
SSD_STATE = 128
SSD_CHUNK = 128
SSD_D_INNER = SSD_HEADS * HEAD_DIM
GROUP_WIDTH = SSD_D_INNER // SSD_GROUPS
CONV_K = 4
CONV_CH = SSD_D_INNER + 2 * SSD_GROUPS * SSD_STATE
ATT_HEADS = 16
ATT_D = ATT_HEADS * HEAD_DIM
ATT_BLK = 128
DILATIONS = (1, 4, 16)
D_FF = 4 * D_MODEL
N_MOD = 6
EPS = 1e-6
IN_W = SSD_D_INNER + CONV_CH + SSD_HEADS + 3 * ATT_D
LANE = 128
OFF_Z, OFF_XBC, OFF_DT = 0, SSD_D_INNER, SSD_D_INNER + CONV_CH
OFF_Q = OFF_DT + LANE
OFF_K, OFF_V = OFF_Q + ATT_D, OFF_Q + 2 * ATT_D
IN_WP = OFF_V + ATT_D

ADAM_LR, ADAM_B1, ADAM_B2, ADAM_EPS, ADAM_WD, ADAM_STEP = 0.001, 0.9, 0.999, 1e-08, 0.01, 10
VMEM_LIMIT = 56 * 1024 * 1024
ROW_TILE = 512
SMALL_ROWS = 24


def _cparams(sem=None):
    return pltpu.CompilerParams(dimension_semantics=sem, vmem_limit_bytes=VMEM_LIMIT)


def _sigmoid(v):
    return 1.0 / (1.0 + jnp.exp(-v))


def _softplus(v):
    y = jnp.exp(-jnp.abs(v))
    small = y * (1.0 - y * (0.5 - y * (1.0 / 3.0)))
    return jnp.maximum(v, 0.0) + jnp.where(y < 0.01, small, jnp.log(1.0 + y))


def _dot(a, b, dims, precision=None):
    return lax.dot_general(a, b, (dims, ((), ())), preferred_element_type=F32, precision=precision)


NN = ((1,), (0,))
NT = ((1,), (1,))
TN = ((0,), (0,))


def _matmul(a, b, *, ta=False, tb=False, tm, tn, tk, out_dtype=F32, name, mode=None, u=None, comm=None, shard_out=False):
    m, k = (a.shape[1], a.shape[0]) if ta else a.shape
    n = b.shape[0] if tb else b.shape[1]
    assert m % tm == 0 and n % tn == 0 and k % tk == 0, (name, m, n, k)
    nk = k // tk
    a_spec = pl.BlockSpec((tk, tm), lambda i, j, kk: (kk, i)) if ta else pl.BlockSpec((tm, tk), lambda i, j, kk: (i, kk))
    b_spec = pl.BlockSpec((tn, tk), lambda i, j, kk: (j, kk)) if tb else pl.BlockSpec((tk, tn), lambda i, j, kk: (kk, j))
    o_spec = pl.BlockSpec((tm, tn), lambda i, j, kk: (i, j))
    dims = ((0,) if ta else (1,), (1,) if tb else (0,))
    n_out = 2 if mode == "relu2" else 1

    def body(*refs):
        if mode == "drelu2":
            a_ref, b_ref, u_ref = refs[:3]
            rest = refs[3:]
        else:
            a_ref, b_ref = refs[:2]
            u_ref = None
            rest = refs[2:]
        outs = rest[:n_out]
        part = _dot(a_ref[...], b_ref[...], dims)

        def finish(r):
            if mode == "relu2":
                outs[0][...] = r.astype(BF16)
                rr = jnp.maximum(r, 0.0)
                outs[1][...] = (rr * rr).astype(BF16)
            elif mode == "drelu2":
                outs[0][...] = (r * (2.0 * jnp.maximum(u_ref[...].astype(F32), 0.0))).astype(out_dtype)
            else:
                outs[0][...] = r.astype(out_dtype)

        if nk == 1:
            finish(part)
        else:
            acc = rest[n_out]
            kk = pl.program_id(2)

            @pl.when(kk == 0)
            def _():
                acc[...] = part

            @pl.when(kk > 0)
            def _():
                acc[...] += part

            @pl.when(kk == nk - 1)
            def _():
                finish(acc[...])

    in_specs = [a_spec, b_spec]
    args = [a, b]
    if mode == "drelu2":
        in_specs.append(o_spec)
        args.append(u)
    if mode == "relu2":
        out_shape = [jax.ShapeDtypeStruct((m, n), BF16), jax.ShapeDtypeStruct((m, n), BF16)]
    elif shard_out:
        out_shape = [jax.ShapeDtypeStruct((n // tn, m, tn), out_dtype)]
        o_spec = pl.BlockSpec((None, tm, tn), lambda i, j, kk: (j, i, 0))
    else:
        out_shape = [jax.ShapeDtypeStruct((m, n), out_dtype)]
    outs, comm_outs = _pcall(
        body, args, name=name, grid=(m // tm, n // tn, nk), in_specs=in_specs, out_specs=[o_spec] * n_out,
        out_shape=out_shape, scratch_shapes=[pltpu.VMEM((tm, tn), F32)] if nk > 1 else [],
        sem=("parallel", "parallel", "arbitrary"), comm=comm)
    res = tuple(outs) if mode == "relu2" else outs[0]
    return res if comm is None else (res, comm_outs)


def _rms_mod(xv, nw, scale, shift):
    r = lax.rsqrt(jnp.mean(xv * xv, axis=-1, keepdims=True) + EPS)
    return ((xv * r) * nw * (1.0 + scale) + shift).astype(BF16)


def _norm_mod_fwd(x, nw, scale, shift, name):
    s, d = x.shape
    row = pl.BlockSpec((ROW_TILE, d), lambda i: (i, 0))
    vec = pl.BlockSpec((1, d), lambda i: (0, 0))

    def body(x_ref, nw_ref, sc_ref, sh_ref, h_ref):
        h_ref[...] = _rms_mod(x_ref[...], nw_ref[...], sc_ref[...], sh_ref[...])

    return pl.pallas_call(body, name=name, grid=(s // ROW_TILE,), in_specs=[row, vec, vec, vec], out_specs=row,
                          out_shape=jax.ShapeDtypeStruct((s, d), BF16), compiler_params=_cparams(("parallel",)))(x, nw, scale, shift)


def _matmul_rows(a, b, epilogue, row_in, vec_in, outs, *, tb=False, tm, name, comm=None):
    m, k = a.shape
    n = b.shape[0] if tb else b.shape[1]
    assert m % tm == 0, (name, m, tm)
    dims = ((1,), (1,) if tb else (0,))
    n_row, n_vec = len(row_in), len(vec_in)

    def body(a_ref, b_ref, *rest):
        epilogue(_dot(a_ref[...], b_ref[...], dims), pl.program_id(0) == 0, rest[:n_row], rest[n_row:n_row + n_vec],
                 rest[n_row + n_vec:])

    def spec(kind, width):
        block = {"row": (tm, width), "vec": (1, width), "one": (1, 1)}[kind]
        return pl.BlockSpec(block, (lambda i: (i, 0)) if kind == "row" else (lambda i: (0, 0)))

    def shape(kind, width):
        return {"row": (m, width), "vec": (1, width), "one": (1, 1)}[kind]

    outs = [(o[0], o[1], o[2] if len(o) > 2 else n) for o in outs]
    res, comm_outs = _pcall(
        body, [a, b, *row_in, *vec_in], name=name, grid=(m // tm,),
        in_specs=[pl.BlockSpec((tm, k), lambda i: (i, 0)), pl.BlockSpec(b.shape, lambda i: (0, 0))]
        + [spec("row", r.shape[1]) for r in row_in] + [spec("vec", v.shape[1]) for v in vec_in],
        out_specs=[spec(kind, width) for kind, _, width in outs],
        out_shape=[jax.ShapeDtypeStruct(shape(kind, width), dt) for kind, dt, width in outs],
        sem=("arbitrary",), comm=comm)
    return res if comm is None else (res, comm_outs)


def _residual_norm_epilogue(mix, first, rows, vecs, outs):
    (x_ref,), (gate_ref, nw_ref, sc_ref, sh_ref), (mix_ref, x1_ref, h_ref) = rows, vecs, outs
    xv = x_ref[...] + gate_ref[...] * mix
    mix_ref[...] = mix
    x1_ref[...] = xv
    h_ref[...] = _rms_mod(xv, nw_ref[...], sc_ref[...], sh_ref[...])


def _loss_epilogue(ff, first, rows, vecs, outs):
    (x1_ref, t_ref), (g_ref,), (loss_ref, dout_ref, dff_ref, dg_ref) = rows, vecs, outs
    d = ff.shape[1]

    @pl.when(first)
    def _():
        loss_ref[...] = jnp.zeros_like(loss_ref)
        dg_ref[...] = jnp.zeros_like(dg_ref)

    err = x1_ref[...] + g_ref[...] * ff - t_ref[...]
    loss_ref[...] += (0.5 / d) * jnp.sum(err * err).reshape(1, 1)
    dout = err * (1.0 / d)
    dout_ref[...] = dout
    dff_ref[...] = (g_ref[...] * dout).astype(BF16)
    dg_ref[...] += jnp.sum(dout * ff, axis=0, keepdims=True)


def _norm_bwd_epilogue(dh, first, rows, vecs, outs):
    with_gate = len(vecs) == 3
    x_ref, dres_ref = rows[:2]
    nw_ref, sc_ref = vecs[:2]
    dx_ref, dsh_ref, dsc_ref, dnw_ref = outs[:4]

    @pl.when(first)
    def _():
        for ref in outs[1:4] + outs[5:]:
            ref[...] = jnp.zeros_like(ref)

    xv = x_ref[...]
    r = lax.rsqrt(jnp.mean(xv * xv, axis=-1, keepdims=True) + EPS)
    nrm = xv * r
    one_sc = 1.0 + sc_ref[...]
    dhn = dh * nrm
    dsh_ref[...] += jnp.sum(dh, axis=0, keepdims=True)
    dsc_ref[...] += jnp.sum(dhn, axis=0, keepdims=True) * nw_ref[...]
    dnw_ref[...] += jnp.sum(dhn, axis=0, keepdims=True) * one_sc
    dn = dh * (nw_ref[...] * one_sc)
    dx = dres_ref[...] + r * (dn - nrm * jnp.mean(dn * nrm, axis=-1, keepdims=True))
    dx_ref[...] = dx
    if with_gate:
        outs[4][...] = (vecs[2][...] * dx).astype(BF16)
        outs[5][...] += jnp.sum(dx * rows[2][...], axis=0, keepdims=True)


CONV_COLS = 256
CONV_FWD_ROWS = 2048
CONV_BWD_ROWS = 1024
CONV_SUB_ROWS = 128
HALO = 8


def _shift_down(cur, halo, k):
    if k == 0:
        return cur
    rolled = pltpu.roll(cur, k, axis=0)
    top = jnp.where(lax.broadcasted_iota(jnp.int32, halo.shape, 0) < k, pltpu.roll(halo, k, axis=0), rolled[:HALO])
    return jnp.concatenate([top, rolled[HALO:]], axis=0)


def _shift_up(cur, halo, k):
    if k == 0:
        return cur
    t = cur.shape[0]
    rolled = pltpu.roll(cur, t - k, axis=0)
    bot = jnp.where(lax.broadcasted_iota(jnp.int32, halo.shape, 0) >= HALO - k, pltpu.roll(halo, HALO - k, axis=0),
                    rolled[t - HALO:])
    return jnp.concatenate([rolled[:t - HALO], bot], axis=0)


def _conv_fwd(proj, conv_w, conv_b):
    s = proj.shape[0]
    nr = s // CONV_FWD_ROWS
    cb0 = OFF_XBC // CONV_COLS
    hb = CONV_FWD_ROWS // HALO
    cur = pl.BlockSpec((CONV_FWD_ROWS, CONV_COLS), lambda j, r: (r, cb0 + j))
    prev = pl.BlockSpec((HALO, CONV_COLS), lambda j, r: (jnp.maximum(r * hb - 1, 0), cb0 + j))
    out = pl.BlockSpec((CONV_FWD_ROWS, CONV_COLS), lambda j, r: (r, j))

    def body(u_ref, up_ref, w_ref, b_ref, pre_ref, act_ref):
        r = pl.program_id(1)
        u = u_ref[...]
        halo = jnp.where(r > 0, up_ref[...], 0.0)
        acc = b_ref[...] + w_ref[CONV_K - 1:CONV_K, :] * u
        for k in range(1, CONV_K):
            acc = acc + w_ref[CONV_K - 1 - k:CONV_K - k, :] * _shift_down(u, halo, k)
        pre_ref[...] = acc
        act_ref[...] = acc * _sigmoid(acc)

    return pl.pallas_call(
        body, name="conv_fwd", grid=(CONV_CH // CONV_COLS, nr),
        in_specs=[cur, prev, pl.BlockSpec((CONV_K, CONV_COLS), lambda j, r: (0, j)),
                  pl.BlockSpec((1, CONV_COLS), lambda j, r: (0, j))],
        out_specs=[out, out],
        out_shape=[jax.ShapeDtypeStruct((s, CONV_CH), F32), jax.ShapeDtypeStruct((s, CONV_CH), F32)],
        compiler_params=_cparams(("parallel", "arbitrary")))(proj, proj, conv_w, conv_b)


def _conv_bwd(dact, pre, proj, conv_w):
    s = proj.shape[0]
    nr = s // CONV_BWD_ROWS
    cb0 = OFF_XBC // CONV_COLS
    hb = CONV_BWD_ROWS // HALO
    last_halo = s // HALO - 1
    n_sub = CONV_BWD_ROWS // CONV_SUB_ROWS
    cur = pl.BlockSpec((CONV_BWD_ROWS, CONV_COLS), lambda j, r: (r, j))
    nxt = pl.BlockSpec((HALO, CONV_COLS), lambda j, r: (jnp.minimum((r + 1) * hb, last_halo), j))
    ucur = pl.BlockSpec((CONV_BWD_ROWS, CONV_COLS), lambda j, r: (r, cb0 + j))
    wspec = pl.BlockSpec((CONV_K, CONV_COLS), lambda j, r: (0, j))
    bspec = pl.BlockSpec((1, CONV_COLS), lambda j, r: (0, j))

    def dsilu(p):
        sg = _sigmoid(p)
        return sg * (1.0 + p * (1.0 - sg))

    def body(da_ref, dan_ref, pre_ref, pren_ref, u_ref, w_ref, du_ref, dw_ref, db_ref):
        r = pl.program_id(1)

        @pl.when(r == 0)
        def _():
            dw_ref[...] = jnp.zeros_like(dw_ref)
            db_ref[...] = jnp.zeros_like(db_ref)

        dws = [jnp.zeros((1, CONV_COLS), F32) for _ in range(CONV_K)]
        db = jnp.zeros((1, CONV_COLS), F32)
        for c in range(n_sub):
            rows = slice(c * CONV_SUB_ROWS, (c + 1) * CONV_SUB_ROWS)
            ahead = slice((c + 1) * CONV_SUB_ROWS, (c + 1) * CONV_SUB_ROWS + HALO)
            dpre = da_ref[rows, :] * dsilu(pre_ref[rows, :])
            if c < n_sub - 1:
                dnext = da_ref[ahead, :] * dsilu(pre_ref[ahead, :])
            else:
                dnext = jnp.where(r < nr - 1, dan_ref[...] * dsilu(pren_ref[...]), 0.0)
            u = u_ref[rows, :]
            du = w_ref[CONV_K - 1:CONV_K, :] * dpre
            dws[0] = dws[0] + jnp.sum(dpre * u, axis=0, keepdims=True)
            for k in range(1, CONV_K):
                ahead_k = _shift_up(dpre, dnext, k)
                du = du + w_ref[CONV_K - 1 - k:CONV_K - k, :] * ahead_k
                dws[k] = dws[k] + jnp.sum(ahead_k * u, axis=0, keepdims=True)
            du_ref[rows, :] = du.astype(BF16)
            db = db + jnp.sum(dpre, axis=0, keepdims=True)
        dw_ref[...] += jnp.concatenate(dws[::-1], axis=0)
        db_ref[...] += db

    return pl.pallas_call(
        body, name="conv_bwd", grid=(CONV_CH // CONV_COLS, nr),
        in_specs=[cur, nxt, cur, nxt, ucur, wspec],
        out_specs=[cur, wspec, bspec],
        out_shape=[jax.ShapeDtypeStruct((s, CONV_CH), BF16), jax.ShapeDtypeStruct((CONV_K, CONV_CH), F32),
                   jax.ShapeDtypeStruct((1, CONV_CH), F32)],
        compiler_params=_cparams(("parallel", "arbitrary")))(dact, dact, pre, pre, proj, conv_w)


def _ssd_common(dtr, dtb, alog):
    lane = lax.broadcasted_iota(jnp.int32, (1, LANE), 1)
    head_lane = lane < SSD_HEADS
    dt = jnp.where(head_lane, _softplus(dtr + dtb), 0.0)
    a = jnp.where(head_lane, -jnp.exp(alog), 0.0)
    row = lax.broadcasted_iota(jnp.int32, (SSD_CHUNK, SSD_CHUNK), 0)
    col = lax.broadcasted_iota(jnp.int32, (SSD_CHUNK, SSD_CHUNK), 1)
    tril = row >= col
    cs = _dot(tril.astype(F32), dt * a, NN, precision=HIGHEST)
    return dt, a, cs, cs.T, tril, lane


def _split_bf16(v, passes):
    terms, rest = [], v
    for _ in range(passes):
        t = rest.astype(BF16)
        terms.append(t)
        rest = rest - t.astype(F32)
    return terms


def _dot_split(v, m, dims, passes):
    terms = _split_bf16(v, passes)
    if passes == 1:
        return _dot(terms[0], m, dims)
    return _dot(jnp.concatenate(terms, axis=1), jnp.concatenate([m] * passes, axis=0 if dims == NN else 1), dims)


def _ssd_constants():
    heads = jnp.arange(LANE)[:, None]
    exp_mat = (heads == (jnp.arange(SSD_D_INNER)[None, :] // HEAD_DIM)).astype(BF16)
    ind4 = ((jnp.arange(SSD_HEADS * SSD_CHUNK)[:, None] // SSD_CHUNK) == jnp.arange(LANE)[None, :]).astype(BF16)
    return exp_mat, ind4


def _expand_heads(v):
    return jnp.repeat(v[:, :SSD_HEADS], HEAD_DIM, axis=1)


def _ssd_prep(dtr, dtb, alog, exp_mat):
    dt, a, cs, cst, tril, lane = _ssd_common(dtr, dtb, alog)
    return dt, a, cs, cst, tril, lane, _dot_split(dt, exp_mat, NN, 2), _dot_split(cs, exp_mat, NN, 3)


def _chunk_decay_rows(cs, g):
    parts = []
    for e in range(HEADS_PER_GROUP):
        h = g * HEADS_PER_GROUP + e
        parts.append(jnp.broadcast_to(jnp.exp(cs[SSD_CHUNK - 1:SSD_CHUNK, h:h + 1]), (HEAD_DIM, SSD_STATE)))
    return jnp.concatenate(parts, axis=0)


def _ssd_fwd(proj, act, dtb, alog, dsk, nw):
    s = proj.shape[0]
    nc = s // SSD_CHUNK
    bc_w = SSD_GROUPS * SSD_STATE
    exp_mat, _ = _ssd_constants()

    def body(z_ref, dtr_ref, xs_ref, b_ref, c_ref, dtb_ref, alog_ref, dskx_ref, nw_ref, exp_ref,
             ypre_ref, yssd_ref, hall_ref, h_scr):
        @pl.when(pl.program_id(0) == 0)
        def _():
            h_scr[...] = jnp.zeros_like(h_scr)

        dt, a, cs, cst, tril, lane, dtx, csx = _ssd_prep(dtr_ref[...], dtb_ref[...], alog_ref[...], exp_ref[...])
        cs_last_x = csx[SSD_CHUNK - 1:SSD_CHUNK, :]
        xs = xs_ref[...]
        xdt = xs * dtx
        xdtb = xdt.astype(BF16)
        xdec = (xdt * jnp.exp(cs_last_x - csx)).astype(BF16)
        ecsx = jnp.exp(csx)
        head_of_lane = lax.broadcasted_iota(jnp.int32, (1, GROUP_WIDTH), 1) // HEAD_DIM
        for g in range(SSD_GROUPS):
            gs = slice(g * GROUP_WIDTH, (g + 1) * GROUP_WIDTH)
            bg = b_ref[:, g * SSD_STATE:(g + 1) * SSD_STATE].astype(BF16)
            cg = c_ref[:, g * SSD_STATE:(g + 1) * SSD_STATE].astype(BF16)
            cb = _dot(cg, bg, NT)
            hprev = h_scr[gs, :]
            hall_ref[0, gs, :] = hprev
            gms, rhs = [], []
            xg = xdtb[:, gs]
            for e in range(HEADS_PER_GROUP):
                h = g * HEADS_PER_GROUP + e
                lm = jnp.exp(jnp.where(tril, cs[:, h:h + 1] - cst[h:h + 1, :], -1e30))
                gms.append((cb * lm).astype(BF16))
                rhs.append(jnp.where(head_of_lane == e, xg, jnp.zeros_like(xg)))
            y = _dot(jnp.concatenate(gms, axis=1), jnp.concatenate(rhs, axis=0), NN)
            y = y + ecsx[:, gs] * _dot(cg, hprev.astype(BF16), NT)
            y = y + dskx_ref[:, gs] * xs[:, gs]
            h_scr[gs, :] = hprev * _chunk_decay_rows(cs, g) + _dot(xdec[:, gs], bg, TN)
            ypre_ref[:, gs] = y
            z = z_ref[:, gs]
            yg = y * (z * _sigmoid(z))
            r = lax.rsqrt(jnp.mean(yg * yg, axis=-1, keepdims=True) + EPS)
            yssd_ref[:, gs] = (yg * r * nw_ref[:, gs]).astype(BF16)

    row_d = lambda cb: pl.BlockSpec((SSD_CHUNK, SSD_D_INNER), lambda c: (c, cb))
    small = pl.BlockSpec((1, LANE), lambda c: (0, 0))
    wide = pl.BlockSpec((1, SSD_D_INNER), lambda c: (0, 0))
    return pl.pallas_call(
        body, name="ssd_fwd", grid=(nc,),
        in_specs=[row_d(OFF_Z // SSD_D_INNER),
                  pl.BlockSpec((SSD_CHUNK, LANE), lambda c: (c, OFF_DT // LANE)),
                  row_d(0),
                  pl.BlockSpec((SSD_CHUNK, bc_w), lambda c: (c, SSD_D_INNER // bc_w)),
                  pl.BlockSpec((SSD_CHUNK, bc_w), lambda c: (c, SSD_D_INNER // bc_w + 1)),
                  small, small, wide, wide, pl.BlockSpec((LANE, SSD_D_INNER), lambda c: (0, 0))],
        out_specs=[row_d(0), row_d(0), pl.BlockSpec((1, SSD_D_INNER, SSD_STATE), lambda c: (c, 0, 0))],
        out_shape=[jax.ShapeDtypeStruct((s, SSD_D_INNER), F32), jax.ShapeDtypeStruct((s, SSD_D_INNER + ATT_D), BF16),
                   jax.ShapeDtypeStruct((nc, SSD_D_INNER, SSD_STATE), F32)],
        scratch_shapes=[pltpu.VMEM((SSD_D_INNER, SSD_STATE), F32)],
        compiler_params=_cparams(("arbitrary",)))(proj, proj, act, act, act, dtb, alog, _expand_heads(dsk), nw, exp_mat)


def _ssd_bwd(dycat, ypre, proj, act, hall, dtb, alog, dsk, nw, comm=None):
    s = proj.shape[0]
    nc = s // SSD_CHUNK
    bc_w = SSD_GROUPS * SSD_STATE

    exp_mat, ind4 = _ssd_constants()
    seg_passes = 1

    def body(dy_ref, ypre_ref, z_ref, dtr_ref, xs_ref, b_ref, c_ref, hall_ref, dtb_ref, alog_ref, dskx_ref, nw_ref,
             exp_ref, ind4_ref, dz_ref, dact_ref, ddtr_ref, da_ref, ddsk_ref, ddtb_ref, dnw_ref, dh_scr):
        @pl.when(pl.program_id(0) == 0)
        def _():
            dh_scr[...] = jnp.zeros_like(dh_scr)
            da_ref[...] = jnp.zeros_like(da_ref)
            ddsk_ref[...] = jnp.zeros_like(ddsk_ref)
            ddtb_ref[...] = jnp.zeros_like(ddtb_ref)
            dnw_ref[...] = jnp.zeros_like(dnw_ref)

        dtr = dtr_ref[...]
        dt, a, cs, cst, tril, lane, dtx, csx = _ssd_prep(dtr, dtb_ref[...], alog_ref[...], exp_ref[...])
        cs_last_x = csx[SSD_CHUNK - 1:SSD_CHUNK, :]
        xs = xs_ref[...]
        xdt = xs * dtx
        xdtb = xdt.astype(BF16)
        decx = jnp.exp(cs_last_x - csx)
        xdecf = xdt * decx
        xdec = xdecf.astype(BF16)
        ecsx = jnp.exp(csx)
        head_of_lane = lax.broadcasted_iota(jnp.int32, (1, GROUP_WIDTH), 1) // HEAD_DIM
        last_row = lax.broadcasted_iota(jnp.int32, (SSD_CHUNK, 1), 0) == SSD_CHUNK - 1
        dcs_col = jnp.zeros((SSD_CHUNK, LANE), F32)
        dcs_row = jnp.zeros((SSD_CHUNK, LANE), F32)
        ddt = jnp.zeros((SSD_CHUNK, LANE), F32)
        ddsk = jnp.zeros((1, LANE), F32)
        hsum = jnp.zeros((1, LANE), F32)
        t1_sum = jnp.zeros((1, LANE), F32)
        for g in range(SSD_GROUPS):
            gs = slice(g * GROUP_WIDTH, (g + 1) * GROUP_WIDTH)
            bsl = slice(g * SSD_STATE, (g + 1) * SSD_STATE)
            exp_g = exp_ref[:, gs]
            ind4_g = ind4_ref[g * HEADS_PER_GROUP * SSD_CHUNK:(g + 1) * HEADS_PER_GROUP * SSD_CHUNK, :]
            z = z_ref[:, gs]
            sg = _sigmoid(z)
            sz = z * sg
            ypre = ypre_ref[:, gs]
            yg = ypre * sz
            r = lax.rsqrt(jnp.mean(yg * yg, axis=-1, keepdims=True) + EPS)
            nrm = yg * r
            dyo_n = dy_ref[:, gs]
            dnw_ref[:, gs] += jnp.sum(dyo_n * nrm, axis=0, keepdims=True)
            dn = dyo_n * nw_ref[:, gs]
            dyg = r * (dn - nrm * jnp.mean(dn * nrm, axis=-1, keepdims=True))
            dz_ref[:, gs] = (dyg * ypre * (sg * (1.0 + z * (1.0 - sg)))).astype(BF16)
            dy = dyg * sz

            bg = b_ref[:, bsl].astype(BF16)
            cg = c_ref[:, bsl].astype(BF16)
            cb = _dot(cg, bg, NT)
            hprev = hall_ref[0, gs, :]
            hb = hprev.astype(BF16)
            dhn = dh_scr[gs, :]
            dhb = dhn.astype(BF16)
            xs_g, xdt_g = xs[:, gs], xdtb[:, gs]
            w_off = _dot(cg, hb, NT)
            dyo = dy * ecsx[:, gs]
            dyob = dyo.astype(BF16)
            dcg = _dot(dyob, hb, NN)
            dh_y = _dot(dyob, cg, TN)
            r_st = _dot(bg, dhb, NT)
            dbg = _dot(xdec[:, gs], dhb, NN)
            dyb = dy.astype(BF16)
            gms, gmbs, lms, dys = [], [], [], []
            for e in range(HEADS_PER_GROUP):
                h = g * HEADS_PER_GROUP + e
                lm = jnp.exp(jnp.where(tril, cs[:, h:h + 1] - cst[h:h + 1, :], -1e30))
                gm = cb * lm
                lms.append(lm)
                gms.append(gm)
                gmbs.append(gm.astype(BF16))
                dys.append(jnp.where(head_of_lane == e, dyb, jnp.zeros_like(dyb)))
            dxdt = _dot(jnp.concatenate(gmbs, axis=0), jnp.concatenate(dys, axis=0), TN) + decx[:, gs] * r_st
            dcb = jnp.zeros((SSD_CHUNK, SSD_CHUNK), F32)
            mms = []
            for e in range(HEADS_PER_GROUP):
                dg = _dot(dys[e], xdt_g, NT)
                mms.append(dg * gms[e])
                dcb = dcb + dg * lms[e]
            seg = _dot_split(jnp.concatenate([dyo * w_off, xdecf[:, gs] * r_st, dxdt * xs_g, dy * xs_g], axis=0), exp_g, NT, seg_passes)
            v1, t1, ddt_g, dsk_g = [seg[i * SSD_CHUNK:(i + 1) * SSD_CHUNK] for i in range(4)]
            dcs_col = dcs_col + v1 - t1 + _dot_split(jnp.concatenate(mms, axis=1), ind4_g, NN, seg_passes)
            for t in _split_bf16(jnp.concatenate(mms, axis=0), seg_passes):
                dcs_row = dcs_row + _dot(ind4_g, t, TN)
            ddt = ddt + ddt_g
            ddsk = ddsk + jnp.sum(dsk_g, axis=0, keepdims=True)
            t1_sum = t1_sum + jnp.sum(t1, axis=0, keepdims=True)
            for e in range(HEADS_PER_GROUP):
                h = g * HEADS_PER_GROUP + e
                hs = slice(e * HEAD_DIM, (e + 1) * HEAD_DIM)
                hsum = hsum + jnp.where(lane == h, jnp.sum(dhn[hs, :] * hprev[hs, :]).reshape(1, 1), 0.0)
            dh_scr[gs, :] = dhn * _chunk_decay_rows(cs, g) + dh_y
            dcbb = dcb.astype(BF16)
            dact_ref[:, gs] = dxdt * dtx[:, gs] + dskx_ref[:, gs] * dy
            dact_ref[:, SSD_D_INNER + g * SSD_STATE:SSD_D_INNER + (g + 1) * SSD_STATE] = dbg + _dot(dcbb, cg, TN)
            dact_ref[:, SSD_D_INNER + bc_w + g * SSD_STATE:SSD_D_INNER + bc_w + (g + 1) * SSD_STATE] = dcg + _dot(dcbb, bg, NN)
        dlast = t1_sum + jnp.exp(cs[SSD_CHUNK - 1:SSD_CHUNK, :]) * hsum
        dcs = dcs_col - dcs_row.T + jnp.where(last_row, dlast, 0.0)
        row = lax.broadcasted_iota(jnp.int32, (SSD_CHUNK, SSD_CHUNK), 0)
        col = lax.broadcasted_iota(jnp.int32, (SSD_CHUNK, SSD_CHUNK), 1)
        dda = _dot((col >= row).astype(F32), dcs, NN, precision=HIGHEST)
        ddt = ddt + dda * a
        da_ref[...] += jnp.sum(dda * dt, axis=0, keepdims=True)
        ddtr = jnp.where(lane < SSD_HEADS, ddt * _sigmoid(dtr + dtb_ref[...]), 0.0)
        ddtr_ref[...] = ddtr.astype(BF16)
        ddtb_ref[...] += jnp.sum(ddtr, axis=0, keepdims=True)
        ddsk_ref[...] += ddsk

    rev = lambda c: nc - 1 - c
    row_d = lambda cb: pl.BlockSpec((SSD_CHUNK, SSD_D_INNER), lambda c: (rev(c), cb))
    small = pl.BlockSpec((1, LANE), lambda c: (0, 0))
    wide = pl.BlockSpec((1, SSD_D_INNER), lambda c: (0, 0))
    small_shape = jax.ShapeDtypeStruct((1, LANE), F32)
    return _pcall(
        body, (dycat, ypre, proj, proj, act, act, act, hall, dtb, alog, _expand_heads(dsk), nw, exp_mat, ind4),
        name="ssd_bwd", grid=(nc,),
        in_specs=[row_d(0), row_d(0), row_d(OFF_Z // SSD_D_INNER),
                  pl.BlockSpec((SSD_CHUNK, LANE), lambda c: (rev(c), OFF_DT // LANE)),
                  row_d(0),
                  pl.BlockSpec((SSD_CHUNK, bc_w), lambda c: (rev(c), SSD_D_INNER // bc_w)),
                  pl.BlockSpec((SSD_CHUNK, bc_w), lambda c: (rev(c), SSD_D_INNER // bc_w + 1)),
                  pl.BlockSpec((1, SSD_D_INNER, SSD_STATE), lambda c: (rev(c), 0, 0)),
                  small, small, wide, wide, pl.BlockSpec((LANE, SSD_D_INNER), lambda c: (0, 0)),
                  pl.BlockSpec((SSD_HEADS * SSD_CHUNK, LANE), lambda c: (0, 0))],
        out_specs=[row_d(0), pl.BlockSpec((SSD_CHUNK, CONV_CH), lambda c: (rev(c), 0)),
                   pl.BlockSpec((SSD_CHUNK, LANE), lambda c: (rev(c), 0)), small, small, small, wide],
        out_shape=[jax.ShapeDtypeStruct((s, SSD_D_INNER), BF16), jax.ShapeDtypeStruct((s, CONV_CH), F32),
                   jax.ShapeDtypeStruct((s, LANE), BF16), small_shape, small_shape, small_shape,
                   jax.ShapeDtypeStruct((1, SSD_D_INNER), F32)],
        scratch_shapes=[pltpu.VMEM((SSD_D_INNER, SSD_STATE), F32)], sem=("arbitrary",), comm=comm)


def _head_mean_matrix():
    row = lax.broadcasted_iota(jnp.int32, (LANE, LANE), 0) // HEAD_DIM
    col = lax.broadcasted_iota(jnp.int32, (LANE, LANE), 1) // HEAD_DIM
    return (row == col).astype(F32)


def _head_sum2(v, ones_bd):
    hi = v.astype(BF16)
    lo = (v - hi.astype(F32)).astype(BF16)
    return _dot(jnp.concatenate([hi, lo], axis=1), jnp.concatenate([ones_bd, ones_bd], axis=0), NN)


def _head_norm(x, w, scale, ones_bd):
    ms = _head_sum2(x * x, ones_bd) * (1.0 / HEAD_DIM)
    return (x * lax.rsqrt(ms + EPS)) * (w * scale)


PRO_ROWS = 256
ATT_GROUP_FWD = 16
ATT_GROUP_BWD = 8
KEYS = 2 * ATT_BLK
NEG = -1e30
HALF = HEAD_DIM // 2


def _rows(start, size, dil):
    return pl.ds(start, size) if dil == 1 else pl.ds(start, size, stride=dil)


def _fill_bias(bias_ref):
    row = lax.broadcasted_iota(jnp.int32, (ATT_BLK, 2 * KEYS), 0)
    col = lax.broadcasted_iota(jnp.int32, (ATT_BLK, 2 * KEYS), 1) & (KEYS - 1)
    for first, off in ((0, 0), (1, ATT_BLK)):
        dist = off + row - col
        bias_ref[first] = jnp.where((dist >= 0) & (dist <= ATT_BLK), 0.0, NEG)


def _pair(a, b):
    return jnp.concatenate([jnp.broadcast_to(a, (ATT_BLK, KEYS)), jnp.broadcast_to(b, (ATT_BLK, KEYS))], axis=1)


def _split_heads(x, is_a):
    zero = jnp.zeros_like(x)
    return jnp.concatenate([jnp.where(is_a, x, zero), jnp.where(is_a, zero, x)], axis=0)


def _block_ids(b, nb):
    i = b & (nb - 1)
    q0 = pl.multiple_of(b * ATT_BLK, ATT_BLK)
    k0 = pl.multiple_of((b - jnp.minimum(i, 1)) * ATT_BLK, ATT_BLK)
    return pl.ds(q0, ATT_BLK), pl.ds(k0, KEYS), jnp.minimum(i, 1)


def _att_fwd(proj, qw, kw, comm=None):
    s = proj.shape[0]
    nblk = s // ATT_BLK
    assert all((s // d) // ATT_BLK >= 2 for d in DILATIONS)
    blk = lambda off: pl.BlockSpec((s, LANE), lambda i: (0, off // LANE + i))
    wspec = pl.BlockSpec((1, LANE), lambda i: (0, i))
    oblk = pl.BlockSpec((s, LANE), lambda i: (0, i))

    def body(q_ref, k_ref, v_ref, qw_ref, kw_ref, o_ref, lse_ref, qn, kn, q_cm, k_cm, v_cm, m_acc, l_acc, o_d, m_d, l_d, bias):
        ones_bd = _head_mean_matrix().astype(BF16)
        is_a = lax.broadcasted_iota(jnp.int32, (1, LANE), 1) < HEAD_DIM
        ones_ext = _split_heads(jnp.ones((KEYS, LANE), BF16), is_a)
        _fill_bias(bias)

        def pro(j, c):
            rows = pl.ds(pl.multiple_of(j * PRO_ROWS, PRO_ROWS), PRO_ROWS)
            qn[rows, :] = _head_norm(q_ref[rows, :], qw_ref[...], HEAD_DIM ** -0.5, ones_bd)
            kn[rows, :] = _head_norm(k_ref[rows, :], kw_ref[...], 1.0, ones_bd)
            return c

        lax.fori_loop(0, s // PRO_ROWS, pro, 0)

        for dil in DILATIONS:
            ln = s // dil
            nb = ln // ATT_BLK
            o_out, m_out, l_out = (o_ref, m_acc, l_acc) if dil == 1 else (o_d, m_d, l_d)
            for r in range(dil):
                def relayout(j, c, dil=dil, r=r, ln=ln):
                    j0 = pl.multiple_of(j * PRO_ROWS, PRO_ROWS)
                    src = _rows(r + dil * j0, PRO_ROWS, dil)
                    dst = pl.ds(r * ln + j0, PRO_ROWS)
                    q_cm[dst, :] = qn[src, :].astype(BF16)
                    k_cm[dst, :] = kn[src, :].astype(BF16)
                    v_cm[dst, :] = v_ref[src, :].astype(BF16)
                    return c

                lax.fori_loop(0, ln // PRO_ROWS, relayout, 0)

            def step(bg, c, nb=nb, o_out=o_out, m_out=m_out, l_out=l_out):
                ids = [_block_ids(bg * ATT_GROUP_FWD + u, nb) for u in range(ATT_GROUP_FWD)]
                kbs = [_split_heads(k_cm[krows, :], is_a) for _, krows, _ in ids]
                scs = [_dot(q_cm[qrows, :], kb, NT) + bias[first] for (qrows, _, first), kb in zip(ids, kbs)]
                mas = [jnp.max(sc[:, :KEYS], axis=-1, keepdims=True) for sc in scs]
                mbs = [jnp.max(sc[:, KEYS:], axis=-1, keepdims=True) for sc in scs]
                ps = [jnp.exp(sc - _pair(ma, mb)).astype(BF16) for sc, ma, mb in zip(scs, mas, mbs)]
                vbs = [jnp.concatenate([_split_heads(v_cm[krows, :], is_a), ones_ext], axis=1) for _, krows, _ in ids]
                ols = [_dot(p, vb, NN) for p, vb in zip(ps, vbs)]
                for (qrows, _, _), ol, ma, mb in zip(ids, ols, mas, mbs):
                    o_out[qrows, :] = ol[:, :LANE]
                    l_out[qrows, :] = ol[:, LANE:]
                    m_out[qrows, :] = jnp.where(is_a, ma, mb)
                return c

            lax.fori_loop(0, nblk // ATT_GROUP_FWD, step, 0)

            if dil > 1:
                for r in range(dil):
                    def merge(j, c, dil=dil, r=r, ln=ln):
                        j0 = pl.multiple_of(j * PRO_ROWS, PRO_ROWS)
                        nat = _rows(r + dil * j0, PRO_ROWS, dil)
                        cm = pl.ds(r * ln + j0, PRO_ROWS)
                        m_old, m_new = m_acc[nat, :], m_d[cm, :]
                        m = jnp.maximum(m_old, m_new)
                        a_old, a_new = jnp.exp(m_old - m), jnp.exp(m_new - m)
                        o_ref[nat, :] = a_old * o_ref[nat, :] + a_new * o_d[cm, :]
                        l_acc[nat, :] = a_old * l_acc[nat, :] + a_new * l_d[cm, :]
                        m_acc[nat, :] = m
                        return c

                    lax.fori_loop(0, ln // PRO_ROWS, merge, 0)

        def epi(j, c):
            rows = pl.ds(pl.multiple_of(j * PRO_ROWS, PRO_ROWS), PRO_ROWS)
            l = l_acc[rows, :]
            o_ref[rows, :] = o_ref[rows, :] / l
            lse_ref[rows, :] = m_acc[rows, :] + jnp.log(l)
            return c

        lax.fori_loop(0, s // PRO_ROWS, epi, 0)

    f = jax.ShapeDtypeStruct((s, ATT_D), F32)
    scr = pltpu.VMEM((s, LANE), F32)
    scb = pltpu.VMEM((s, LANE), BF16)
    return _pcall(
        body, (proj, proj, proj, qw, kw), name="att_fwd", grid=(ATT_D // LANE,),
        in_specs=[blk(OFF_Q), blk(OFF_K), blk(OFF_V), wspec, wspec], out_specs=[oblk, oblk], out_shape=[f, f],
        scratch_shapes=[scr, scr, scb, scb, scb, scr, scr, scr, scr, scr, pltpu.VMEM((2, ATT_BLK, 2 * KEYS), F32)],
        sem=("parallel",), comm=comm)


def _att_bwd(proj, do, stats, qw, kw, comm=None):
    s = proj.shape[0]
    nblk = s // ATT_BLK
    blk = lambda off: pl.BlockSpec((s, LANE), lambda i: (0, off // LANE + i))
    wspec = pl.BlockSpec((1, LANE), lambda i: (0, i))
    oblk = pl.BlockSpec((s, LANE), lambda i: (0, i))

    def body(q_ref, k_ref, v_ref, do_ref, st_ref, qw_ref, kw_ref, dq_ref, dk_ref, dv_ref, dqw_ref, dkw_ref,
             qn, kn, q_cm, do_cm, k_cm, v_cm, st_cm, dq_acc, dk_acc, dv_acc, dq_d, dk_d, dv_d, bias):
        ones_bd = _head_mean_matrix().astype(BF16)
        is_a = lax.broadcasted_iota(jnp.int32, (1, LANE), 1) < HEAD_DIM
        _fill_bias(bias)
        zero = jnp.zeros((PRO_ROWS, LANE), F32)

        def pro(j, c):
            rows = pl.ds(pl.multiple_of(j * PRO_ROWS, PRO_ROWS), PRO_ROWS)
            qn[rows, :] = _head_norm(q_ref[rows, :], qw_ref[...], HEAD_DIM ** -0.5, ones_bd)
            kn[rows, :] = _head_norm(k_ref[rows, :], kw_ref[...], 1.0, ones_bd)
            dk_acc[rows, :] = zero
            dv_acc[rows, :] = zero
            return c

        lax.fori_loop(0, s // PRO_ROWS, pro, 0)

        for dil in DILATIONS:
            ln = s // dil
            nb = ln // ATT_BLK
            dq_o, dk_o, dv_o = (dq_acc, dk_acc, dv_acc) if dil == 1 else (dq_d, dk_d, dv_d)
            for r in range(dil):
                def relayout(j, c, dil=dil, r=r, ln=ln):
                    j0 = pl.multiple_of(j * PRO_ROWS, PRO_ROWS)
                    src = _rows(r + dil * j0, PRO_ROWS, dil)
                    dst = pl.ds(r * ln + j0, PRO_ROWS)
                    q_cm[dst, :] = qn[src, :].astype(BF16)
                    k_cm[dst, :] = kn[src, :].astype(BF16)
                    v_cm[dst, :] = v_ref[src, :].astype(BF16)
                    do_cm[dst, :] = do_ref[src, :].astype(BF16)
                    st_cm[dst, :] = st_ref[src, :]
                    if dil > 1:
                        dk_d[dst, :] = zero
                        dv_d[dst, :] = zero
                    return c

                lax.fori_loop(0, ln // PRO_ROWS, relayout, 0)

            def step(bg, c, nb=nb, dq_o=dq_o, dk_o=dk_o, dv_o=dv_o):
                ids = [_block_ids(bg * ATT_GROUP_BWD + u, nb) for u in range(ATT_GROUP_BWD)]
                qbs = [q_cm[qrows, :] for qrows, _, _ in ids]
                dobs = [do_cm[qrows, :] for qrows, _, _ in ids]
                kbs = [_split_heads(k_cm[krows, :], is_a) for _, krows, _ in ids]
                vbs = [_split_heads(v_cm[krows, :], is_a) for _, krows, _ in ids]
                sts = [st_cm[qrows, :] for qrows, _, _ in ids]
                scs = [_dot(qb, kb, NT) + bias[first] for qb, kb, (_, _, first) in zip(qbs, kbs, ids)]
                dps = [_dot(dob, vb, NT) for dob, vb in zip(dobs, vbs)]
                ps = [jnp.exp(sc - _pair(st[:, 0:1], st[:, HEAD_DIM:HEAD_DIM + 1])) for sc, st in zip(scs, sts)]
                dss = [(p * (dp - _pair(st[:, HALF:HALF + 1], st[:, HEAD_DIM + HALF:HEAD_DIM + HALF + 1]))).astype(BF16)
                       for p, dp, st in zip(ps, dps, sts)]
                dqs = [_dot(ds, kb, NN) for ds, kb in zip(dss, kbs)]
                dkfs = [_dot(ds, qb, TN) for ds, qb in zip(dss, qbs)]
                dvfs = [_dot(p.astype(BF16), dob, TN) for p, dob in zip(ps, dobs)]
                for (qrows, krows, _), dq, dkf, dvf in zip(ids, dqs, dkfs, dvfs):
                    dq_o[qrows, :] = dq
                    dk_o[krows, :] += jnp.where(is_a, dkf[:KEYS], dkf[KEYS:])
                    dv_o[krows, :] += jnp.where(is_a, dvf[:KEYS], dvf[KEYS:])
                return c

            lax.fori_loop(0, nblk // ATT_GROUP_BWD, step, 0)

            if dil > 1:
                for r in range(dil):
                    def merge(j, c, dil=dil, r=r, ln=ln):
                        j0 = pl.multiple_of(j * PRO_ROWS, PRO_ROWS)
                        nat = _rows(r + dil * j0, PRO_ROWS, dil)
                        cm = pl.ds(r * ln + j0, PRO_ROWS)
                        dq_acc[nat, :] += dq_d[cm, :]
                        dk_acc[nat, :] += dk_d[cm, :]
                        dv_acc[nat, :] += dv_d[cm, :]
                        return c

                    lax.fori_loop(0, ln // PRO_ROWS, merge, 0)

        def back(dn_out, x, w, scale):
            r = lax.rsqrt(_head_sum2(x * x, ones_bd) * (1.0 / HEAD_DIM) + EPS)
            nrm = x * r
            dw = jnp.sum(dn_out * nrm, axis=0, keepdims=True) * scale
            dn = dn_out * (w * scale)
            return r * (dn - nrm * (_head_sum2(dn * nrm, ones_bd) * (1.0 / HEAD_DIM))), dw

        def epi(j, c):
            rows = pl.ds(pl.multiple_of(j * PRO_ROWS, PRO_ROWS), PRO_ROWS)
            dq, dqw = back(dq_acc[rows, :], q_ref[rows, :], qw_ref[...], HEAD_DIM ** -0.5)
            dk, dkw = back(dk_acc[rows, :], k_ref[rows, :], kw_ref[...], 1.0)
            dq_ref[rows, :] = dq.astype(BF16)
            dk_ref[rows, :] = dk.astype(BF16)
            dv_ref[rows, :] = dv_acc[rows, :].astype(BF16)
            return (c[0] + dqw, c[1] + dkw)

        zrow = jnp.zeros((1, LANE), F32)
        dqw, dkw = lax.fori_loop(0, s // PRO_ROWS, epi, (zrow, zrow))
        dqw_ref[...] = dqw
        dkw_ref[...] = dkw

    o = jax.ShapeDtypeStruct((s, ATT_D), BF16)
    ov = jax.ShapeDtypeStruct((1, ATT_D), F32)
    scr = pltpu.VMEM((s, LANE), F32)
    scb = pltpu.VMEM((s, LANE), BF16)
    return _pcall(
        body, (proj, proj, proj, do, stats, qw, kw), name="att_bwd", grid=(ATT_D // LANE,),
        in_specs=[blk(OFF_Q), blk(OFF_K), blk(OFF_V), oblk, oblk, wspec, wspec],
        out_specs=[oblk, oblk, oblk, wspec, wspec], out_shape=[o, o, o, ov, ov],
        scratch_shapes=[scr, scr, scb, scb, scb, scb, scr, scr, scr, scr, scr, scr, scr, pltpu.VMEM((2, ATT_BLK, 2 * KEYS), F32)],
        sem=("parallel",), comm=comm)


def _att_norm_fwd(o, nw, ycat):
    s = o.shape[0]
    row = pl.BlockSpec((ROW_TILE, ATT_D), lambda i: (i, 0))
    vec = pl.BlockSpec((1, ATT_D), lambda i: (0, 0))

    def body(o_ref, nw_ref, ycat_ref, y_ref):
        o = o_ref[...]
        r = lax.rsqrt(jnp.mean(o * o, axis=-1, keepdims=True) + EPS)
        y_ref[...] = (o * r * nw_ref[...]).astype(BF16)

    return pl.pallas_call(body, name="att_norm_fwd", grid=(s // ROW_TILE,),
                          in_specs=[row, vec, pl.BlockSpec(memory_space=pl.ANY)],
                          out_specs=pl.BlockSpec((ROW_TILE, ATT_D), lambda i: (i, 1)),
                          out_shape=jax.ShapeDtypeStruct(ycat.shape, BF16), input_output_aliases={2: 0},
                          compiler_params=_cparams(("parallel",)))(o, nw, ycat)


def _mixer_split_epilogue(dycat, first, rows, vecs, outs):
    (o_ref, lse_ref), (nw_ref,), (dyssd_ref, do_ref, st_ref, dnw_ref) = rows, vecs, outs

    @pl.when(first)
    def _():
        dnw_ref[...] = jnp.zeros_like(dnw_ref)

    dyssd_ref[...] = dycat[:, :SSD_D_INNER]
    dy = dycat[:, SSD_D_INNER:]
    o = o_ref[...]
    r = lax.rsqrt(jnp.mean(o * o, axis=-1, keepdims=True) + EPS)
    nrm = o * r
    dnw_ref[...] += jnp.sum(dy * nrm, axis=0, keepdims=True)
    dn = dy * nw_ref[...]
    do = r * (dn - nrm * jnp.mean(dn * nrm, axis=-1, keepdims=True))
    do_ref[...] = do
    ones_bd = _head_mean_matrix().astype(BF16)
    prod = do * o
    delta = jnp.concatenate([_head_sum2(prod[:, j * LANE:(j + 1) * LANE], ones_bd) for j in range(ATT_D // LANE)], axis=1)
    lane = lax.broadcasted_iota(jnp.int32, (1, ATT_D), 1)
    st_ref[...] = jnp.where((lane & (HEAD_DIM - 1)) < HALF, lse_ref[...], delta)


def _ada_fwd(c_all, w_ada):
    def body(c_ref, w_ref, o_ref):
        cv = c_ref[...]
        o_ref[...] = _dot((cv * _sigmoid(cv)).astype(BF16), w_ref[...].astype(BF16), NN)

    return pl.pallas_call(body, name="ada_fwd", out_shape=jax.ShapeDtypeStruct((c_all.shape[0], w_ada.shape[1]), F32),
                          compiler_params=_cparams())(c_all, w_ada)


def _adamw_math(g, w, m, v):
    m_new = ADAM_B1 * m + (1.0 - ADAM_B1) * g
    v_new = ADAM_B2 * v + (1.0 - ADAM_B2) * (g * g)
    m_hat = m_new / (1.0 - ADAM_B1 ** ADAM_STEP)
    v_hat = v_new / (1.0 - ADAM_B2 ** ADAM_STEP)
    delta = -ADAM_LR * (m_hat / (jnp.sqrt(v_hat) + ADAM_EPS) + ADAM_WD * w)
    return delta, m_new, v_new


def _ada_bwd_adamw(c_all, dmod_cols, w, m, v):
    rows, cols = w.shape
    tr = 256
    blk = pl.BlockSpec((tr, cols), lambda i: (i, 0))

    def body(c_ref, d_ref, w_ref, m_ref, v_ref, g_ref, dl_ref, mo_ref, vo_ref):
        cv = c_ref[...]
        ca = cv * _sigmoid(cv)
        g = ca[:, 0:1] * d_ref[0:1, :]
        for b in range(1, N_DEV):
            g = g + ca[:, b:b + 1] * d_ref[b:b + 1, :]
        g_ref[...] = g
        dl_ref[...], mo_ref[...], vo_ref[...] = _adamw_math(g, w_ref[...], m_ref[...], v_ref[...])

    o = jax.ShapeDtypeStruct((rows, cols), F32)
    return pl.pallas_call(
        body, name="ada_bwd_adamw", grid=(rows // tr,),
        in_specs=[pl.BlockSpec((tr, N_DEV), lambda i: (i, 0)), pl.BlockSpec((N_DEV, cols), lambda i: (0, 0)), blk, blk, blk],
        out_specs=[blk] * 4, out_shape=[o, o, o, o], compiler_params=_cparams(("parallel",)))(c_all.T, dmod_cols, w, m, v)


def _reduce_adamw(slabs, w, m, v, name):
    rows, cols = w.shape
    n_src = slabs.shape[0]
    if rows % 128 == 0:
        tr, steps = 128, rows // 128
        blk = pl.BlockSpec((tr, cols), lambda i: (i, 0))
        sblk = pl.BlockSpec((n_src, tr, cols), lambda i: (0, i, 0))
    else:
        tc, steps = 256, cols // 256
        blk = pl.BlockSpec((rows, tc), lambda i: (0, i))
        sblk = pl.BlockSpec((n_src, rows, tc), lambda i: (0, 0, i))

    def body(s_ref, w_ref, m_ref, v_ref, g_ref, dl_ref, mo_ref, vo_ref):
        g = s_ref[0].astype(F32)
        for src in range(1, n_src):
            g = g + s_ref[src].astype(F32)
        g_ref[...] = g
        dl_ref[...], mo_ref[...], vo_ref[...] = _adamw_math(g, w_ref[...], m_ref[...], v_ref[...])

    o = jax.ShapeDtypeStruct((rows, cols), F32)
    return pl.pallas_call(
        body, name=name, grid=(steps,), in_specs=[sblk, blk, blk, blk],
        out_specs=[blk] * 4, out_shape=[o, o, o, o], compiler_params=_cparams(("parallel",)))(slabs, w, m, v)


def _small_reduce_adamw(gathered, w, m, v):
    def body(s_ref, w_ref, m_ref, v_ref, g_ref, dl_ref, mo_ref, vo_ref):
        g = s_ref[0]
        for dev in range(1, N_DEV):
            g = g + s_ref[dev]
        g_ref[...] = g
        dl_ref[...], mo_ref[...], vo_ref[...] = _adamw_math(g, w_ref[...], m_ref[...], v_ref[...])

    o = jax.ShapeDtypeStruct(w.shape, F32)
    return pl.pallas_call(body, name="small_reduce_adamw", out_shape=[o, o, o, o], compiler_params=_cparams())(gathered, w, m, v)


def _adamw_small(g, w, m, v, name):
    def body(g_ref, w_ref, m_ref, v_ref, dl_ref, mo_ref, vo_ref):
        dl_ref[...], mo_ref[...], vo_ref[...] = _adamw_math(g_ref[...], w_ref[...], m_ref[...], v_ref[...])

    o = jax.ShapeDtypeStruct(w.shape, F32)
    return pl.pallas_call(body, name=name, out_shape=[o, o, o], compiler_params=_cparams())(g, w, m, v)


class _Exchange:
    def __init__(self, arrs, scatter):
        self.arrs, self.scatter, self.n = list(arrs), scatter, len(arrs)
        hbm = pl.BlockSpec(memory_space=pltpu.HBM)
        self.in_specs = [hbm] * self.n
        self.out_specs = [hbm] * self.n
        self.out_shape = [jax.ShapeDtypeStruct(a.shape if scatter else (N_DEV,) + a.shape, a.dtype) for a in self.arrs]
        self.scratch = [pltpu.SemaphoreType.DMA((self.n * (N_DEV - 1),)), pltpu.SemaphoreType.DMA((self.n * (N_DEV - 1),)),
                        pltpu.SemaphoreType.DMA((self.n,))]

    def _local(self, ins, outs, sems):
        me = 4 * lax.axis_index("x") + 2 * lax.axis_index("y") + lax.axis_index("c")
        return [pltpu.make_async_copy(ins[a].at[me] if self.scatter else ins[a], outs[a].at[me], sems[2].at[a])
                for a in range(self.n)]

    def _remote(self, ins, outs, sems, arriving):
        send_sems, recv_sems, _ = sems
        x, y, c = lax.axis_index("x"), lax.axis_index("y"), lax.axis_index("c")
        me = 4 * x + 2 * y + c
        remote = []
        for a in range(self.n):
            for k in range(1, N_DEV):
                px = 1 - x if k & 4 else x
                py = 1 - y if k & 2 else y
                pc = 1 - c if k & 1 else c
                peer = 4 * px + 2 * py + pc
                sem = a * (N_DEV - 1) + k - 1
                remote.append(pltpu.make_async_remote_copy(
                    src_ref=ins[a].at[peer] if self.scatter else ins[a], dst_ref=outs[a].at[peer if arriving else me],
                    send_sem=send_sems.at[sem], recv_sem=recv_sems.at[sem], device_id=(px, py, pc), device_id_type=MESH_IDS))
        return remote

    def start(self, ins, outs, sems):
        for cp in self._local(ins, outs, sems) + self._remote(ins, outs, sems, arriving=False):
            cp.start()

    def forward(self, ins, outs, sems):
        pass

    def wait(self, ins, outs, sems):
        for send, arrival in zip(self._remote(ins, outs, sems, arriving=False), self._remote(ins, outs, sems, arriving=True)):
            send.wait_send()
            arrival.wait_recv()
        for cp in self._local(ins, outs, sems):
            cp.wait()


N_CHIP = N_DEV // 2


class _SiblingSwap(_Exchange):
    def __init__(self, arrs):
        super().__init__(arrs, scatter=True)
        self.out_shape = [jax.ShapeDtypeStruct((N_CHIP,) + a.shape[2:], a.dtype) for a in self.arrs]
        self.scratch = [pltpu.SemaphoreType.DMA((self.n,)), pltpu.SemaphoreType.DMA((self.n,)), pltpu.SemaphoreType.DMA((1,))]

    def _copies(self, ins, outs, sems):
        x, y, c = lax.axis_index("x"), lax.axis_index("y"), lax.axis_index("c")
        return [pltpu.make_async_remote_copy(src_ref=ins[a].at[:, 1 - c], dst_ref=outs[a], send_sem=sems[0].at[a], recv_sem=sems[1].at[a],
                                             device_id=(x, y, 1 - c), device_id_type=MESH_IDS) for a in range(self.n)]

    def start(self, ins, outs, sems):
        for cp in self._copies(ins, outs, sems):
            cp.start()

    def wait(self, ins, outs, sems):
        for cp in self._copies(ins, outs, sems):
            cp.wait()


class _ChipScatter(_Exchange):
    def __init__(self, arrs):
        super().__init__(arrs, scatter=True)
        n_pairs = self.n * (N_CHIP - 1)
        self.scratch = [pltpu.SemaphoreType.DMA((n_pairs,)), pltpu.SemaphoreType.DMA((n_pairs,)), pltpu.SemaphoreType.DMA((self.n,))]

    def _local(self, ins, outs, sems):
        chip = 2 * lax.axis_index("x") + lax.axis_index("y")
        return [pltpu.make_async_copy(ins[a].at[chip], outs[a].at[chip], sems[2].at[a]) for a in range(self.n)]

    def _remote(self, ins, outs, sems, arriving):
        send_sems, recv_sems, _ = sems
        x, y, c = lax.axis_index("x"), lax.axis_index("y"), lax.axis_index("c")
        chip = 2 * x + y
        remote = []
        for a in range(self.n):
            for k in range(1, N_CHIP):
                px = 1 - x if k & 2 else x
                py = 1 - y if k & 1 else y
                peer = 2 * px + py
                sem = a * (N_CHIP - 1) + k - 1
                remote.append(pltpu.make_async_remote_copy(
                    src_ref=ins[a].at[peer], dst_ref=outs[a].at[peer if arriving else chip], send_sem=send_sems.at[sem],
                    recv_sem=recv_sems.at[sem], device_id=(px, py, c), device_id_type=MESH_IDS))
        return remote


def _chip_sum(mine, theirs):
    n, rows, cols = mine.shape
    blk = pl.BlockSpec((1, rows, 256), lambda q, j: (q, 0, j))

    def body(a_ref, b_ref, o_ref):
        o_ref[...] = (a_ref[...].astype(F32) + b_ref[...].astype(F32)).astype(BF16)

    return pl.pallas_call(body, name="chip_sum", grid=(n, cols // 256), in_specs=[blk, blk], out_specs=blk,
                          out_shape=jax.ShapeDtypeStruct(mine.shape, BF16),
                          compiler_params=_cparams(("parallel", "parallel")))(mine, theirs)


class _Gather2(_Exchange):
    def __init__(self, arrs):
        super().__init__(arrs, scatter=False)

    def _copies(self, ins, outs, sems):
        send_sems, recv_sems, _ = sems
        x, y, c = lax.axis_index("x"), lax.axis_index("y"), lax.axis_index("c")
        sibling = (x, y, 1 - c)
        chips = [(1 - x, y), (x, 1 - y), (1 - x, 1 - y)]
        first, passed, landed = [], [], []
        for a in range(self.n):
            def copy(k, block, to, src=None, a=a):
                slab = outs[a].at[4 * block[0] + 2 * block[1] + block[2]]
                return pltpu.make_async_remote_copy(
                    src_ref=slab if src is None else src, dst_ref=slab, send_sem=send_sems.at[a * (N_DEV - 1) + k],
                    recv_sem=recv_sems.at[a * (N_DEV - 1) + k], device_id=to, device_id_type=MESH_IDS)

            first.append(copy(0, (x, y, c), sibling, src=ins[a]))
            landed.append(copy(0, sibling, sibling))
            for j, chip in enumerate(chips):
                first.append(copy(1 + j, (x, y, c), (*chip, c), src=ins[a]))
                passed.append((copy(1 + j, (*chip, c), sibling), copy(4 + j, (*chip, c), sibling)))
                landed.append(copy(4 + j, (*chip, 1 - c), sibling))
        return first, passed, landed

    def start(self, ins, outs, sems):
        for cp in self._local(ins, outs, sems) + self._copies(ins, outs, sems)[0]:
            cp.start()

    def forward(self, ins, outs, sems):
        for arrival, onward in self._copies(ins, outs, sems)[1]:
            arrival.wait_recv()
            onward.start()

    def wait(self, ins, outs, sems):
        first, passed, landed = self._copies(ins, outs, sems)
        for arrival in landed:
            arrival.wait_recv()
        for cp in first + [onward for _, onward in passed]:
            cp.wait_send()
        for cp in self._local(ins, outs, sems):
            cp.wait()


def _split_comm_refs(refs, n_in, n_out, n_scr, comm):
    nc = comm.n if comm is not None else 0
    ns = 3 if comm is not None else 0
    pos, groups = 0, []
    for cnt in (n_in, nc, n_out, nc, n_scr, ns):
        groups.append(refs[pos:pos + cnt])
        pos += cnt
    assert pos == len(refs), (pos, len(refs))
    return groups


def _pcall(body, args, *, name, grid, in_specs, out_specs, out_shape, scratch_shapes=(), sem=None, comm=None):
    in_specs, out_specs, out_shape, scratch_shapes = list(in_specs), list(out_specs), list(out_shape), list(scratch_shapes)
    n_in, n_out, n_scr = len(in_specs), len(out_specs), len(scratch_shapes)
    if comm is None:
        kernel_body = body
    else:
        def kernel_body(*refs):
            ins, cins, outs, couts, scr, sems = _split_comm_refs(refs, n_in, n_out, n_scr, comm)
            ids = [pl.program_id(a) for a in range(len(grid))]
            first, last = ids[0] == 0, ids[0] == grid[0] - 1
            for a in range(1, len(grid)):
                first, last = first & (ids[a] == 0), last & (ids[a] == grid[a] - 1)

            middle = ids[0] == (2 * grid[0]) // 3
            for a in range(1, len(grid)):
                middle = middle & (ids[a] == 0)

            @pl.when(first)
            def _():
                comm.start(cins, couts, sems)

            @pl.when(middle)
            def _():
                comm.forward(cins, couts, sems)

            body(*ins, *outs, *scr)

            @pl.when(last)
            def _():
                comm.wait(cins, couts, sems)

        in_specs, out_specs, out_shape = in_specs + comm.in_specs, out_specs + comm.out_specs, out_shape + comm.out_shape
        scratch_shapes, args = scratch_shapes + comm.scratch, list(args) + comm.arrs
        sem = ("arbitrary",) * len(grid)
    res = pl.pallas_call(kernel_body, name=name, grid=grid, in_specs=in_specs, out_specs=out_specs, out_shape=out_shape,
                         scratch_shapes=scratch_shapes, compiler_params=_cparams(sem))(*args)
    return res[:n_out], res[n_out:]


def _exchange(arrs, name, scatter=False, ex=None):
    if ex is None:
        ex = _Exchange(arrs, scatter=True) if scatter else _Gather2(arrs)

    def body(*refs):
        _, ins, _, outs, _, sems = _split_comm_refs(refs, 0, 0, 0, ex)
        ex.start(ins, outs, sems)
        ex.forward(ins, outs, sems)
        ex.wait(ins, outs, sems)

    return pl.pallas_call(body, name=name, in_specs=ex.in_specs, out_specs=ex.out_specs, out_shape=ex.out_shape,
                          scratch_shapes=ex.scratch)(*ex.arrs)


def _pad_lanes(v, width=LANE):
    return jnp.pad(v, ((0, 0), (0, width - v.shape[1])))


def _shards_to_cols(g):
    return jnp.transpose(g, (1, 0, 2)).reshape(g.shape[1], N_DEV * g.shape[2])


def _cols_to_shards(w):
    return w.astype(BF16).reshape(w.shape[0], N_DEV, w.shape[1] // N_DEV).transpose(1, 0, 2)


def _local_step(x, tgt, mod, w_in_pt, conv_w, conv_b, dt_bias, a_log, d_skip, ssd_norm_w, q_norm_w, k_norm_w,
                attn_norm_w, w_out_sh, w_ff1_sh, w_ff2_sh, norm1_w, norm2_w, core):
    shift1, scale1, gate1, shift2, scale2, gate2 = [mod[i:i + 1] for i in range(N_MOD)]
    dtb, alog, dsk = _pad_lanes(dt_bias), _pad_lanes(a_log), _pad_lanes(d_skip)
    qw, kw = jnp.tile(q_norm_w, (1, ATT_HEADS)), jnp.tile(k_norm_w, (1, ATT_HEADS))

    h1 = _norm_mod_fwd(x, norm1_w, scale1, shift1, "norm1_fwd")
    proj = _matmul(h1, w_in_pt, tb=True, tm=2048, tn=896, tk=1024, name="in_proj")
    pre, act = _conv_fwd(proj, conv_w, conv_b)
    ypre, ycat_ssd, hall = _ssd_fwd(proj, act, dtb, alog, dsk, ssd_norm_w)
    (o_att, lse), (w_out_g, w_ff1_g, w_ff2_g) = _att_fwd(proj, qw, kw, comm=_Gather2([w_out_sh, w_ff1_sh, w_ff2_sh]))
    w_out = w_out_g.reshape(2 * D_MODEL, D_MODEL)
    w_ff1 = _shards_to_cols(w_ff1_g)
    w_ff2 = w_ff2_g.reshape(D_FF, D_MODEL)
    ycat = _att_norm_fwd(o_att, attn_norm_w, ycat_ssd)
    row32, row16, vec32 = ("row", F32), ("row", BF16), ("vec", F32)
    mix, x1, h2 = _matmul_rows(ycat, w_out, _residual_norm_epilogue, [x], [gate1, norm2_w, scale2, shift2],
                               [row32, row32, row16], tm=512, name="out_proj")
    u, act_ff = _matmul(h2, w_ff1, tm=1024, tn=1024, tk=1024, name="ff1", mode="relu2")
    loss, dout, dff, dgate2 = _matmul_rows(act_ff, w_ff2, _loss_epilogue, [x1, tgt], [gate2],
                                           [("one", F32), row32, row16, vec32], tm=512, name="ff2")

    du = _matmul(dff, w_ff2, tb=True, tm=1024, tn=1024, tk=1024, out_dtype=BF16, name="ff2_dx", mode="drelu2", u=u)
    g_ff2 = _matmul(act_ff, dff, ta=True, tm=512, tn=1024, tk=4096, out_dtype=BF16, name="ff2_dw")
    dx1, dshift2, dscale2, g_norm2, dmix, dgate1 = _matmul_rows(
        du, w_ff1, _norm_bwd_epilogue, [x1, dout, mix], [norm2_w, scale2, gate1],
        [row32, vec32, vec32, vec32, row16, vec32], tb=True, tm=512, name="ff1_dx")
    g_ff1 = _matmul(h2, du, ta=True, tm=1024, tn=D_FF // N_DEV, tk=4096, out_dtype=BF16, name="ff1_dw", shard_out=True)

    dy_ssd, do, stats, g_attn_norm = _matmul_rows(
        dmix, w_out, _mixer_split_epilogue, [o_att, lse], [attn_norm_w],
        [("row", F32, SSD_D_INNER), ("row", F32, ATT_D), ("row", F32, ATT_D), ("vec", F32, ATT_D)], tb=True, tm=512, name="out_proj_dx")
    g_out = _matmul(ycat, dmix, ta=True, tm=512, tn=1024, tk=4096, out_dtype=BF16, name="out_proj_dw")
    ff_slabs = [g_ff1, g_ff2.reshape(N_DEV, D_FF // N_DEV, D_MODEL)]
    (dq, dk, dv, dqw, dkw), (s_ff1, s_ff2) = _att_bwd(proj, do, stats, qw, kw, comm=_Exchange(ff_slabs, scatter=True))
    out_slabs = [g_out.astype(BF16).reshape(N_DEV, 2 * D_MODEL // N_DEV, D_MODEL)]
    (dz, dact, ddtr, da, g_dsk, g_dtb, g_ssd_norm), (s_out,) = _ssd_bwd(
        dy_ssd, ypre, proj, act, hall, dtb, alog, dsk, ssd_norm_w, comm=_Exchange(out_slabs, scatter=True))
    dxbc, g_conv_w, g_conv_b = _conv_bwd(dact, pre, proj, conv_w)
    dproj = jnp.concatenate([dz, dxbc, ddtr, dq, dk, dv], axis=1)
    g_in_pt = _matmul(dproj, h1, ta=True, tm=896, tn=1024, tk=4096, out_dtype=BF16, name="in_proj_dw")
    in_slabs = _unpack_w_in_rows(g_in_pt).reshape(N_CHIP, 2, IN_W // N_DEV, D_MODEL)
    (sibling_slabs,) = _exchange(None, "swap_w_in_grads", ex=_SiblingSwap([in_slabs]))
    chip_slabs = _chip_sum(lax.dynamic_index_in_dim(in_slabs, core, axis=1, keepdims=False), sibling_slabs)
    (grad_x, dshift1, dscale1, g_norm1), (s_in,) = _matmul_rows(
        dproj, w_in_pt, _norm_bwd_epilogue, [x, dx1], [norm1_w, scale1], [row32, vec32, vec32, vec32],
        tm=256, name="in_proj_dx", comm=_ChipScatter([chip_slabs]))

    dmod = jnp.concatenate([dshift1, dscale1, dgate1, dshift2, dscale2, dgate2], axis=0)
    g_alog = da[:, :SSD_HEADS] * (-jnp.exp(a_log))
    g_qw = dqw.reshape(ATT_HEADS, HEAD_DIM).sum(axis=0, keepdims=True)
    g_kw = dkw.reshape(ATT_HEADS, HEAD_DIM).sum(axis=0, keepdims=True)
    return dict(loss=loss, grad_x=grad_x, dmod=dmod, norm1_w=g_norm1, norm2_w=g_norm2, w_in=s_in, conv_w=g_conv_w,
                conv_b=g_conv_b, dt_bias=g_dtb[:, :SSD_HEADS], a_log=g_alog, d_skip=g_dsk[:, :SSD_HEADS],
                ssd_norm_w=g_ssd_norm, q_norm_w=g_qw, k_norm_w=g_kw, attn_norm_w=g_attn_norm, w_out=s_out,
                w_ff1=s_ff1, w_ff2=s_ff2)


def _pack_w_in_rows(wt_full):
    cut = OFF_DT + SSD_HEADS
    pad = jnp.zeros((LANE - SSD_HEADS, wt_full.shape[1]), wt_full.dtype)
    return jnp.concatenate([wt_full[:cut], pad, wt_full[cut:]], axis=0)


def _unpack_w_in_rows(gt_p):
    return jnp.concatenate([gt_p[:OFF_DT + SSD_HEADS], gt_p[OFF_Q:]], axis=0)


MISC_FIELDS = (("dt_bias", SSD_HEADS), ("a_log", SSD_HEADS), ("d_skip", SSD_HEADS), ("q_norm_w", HEAD_DIM), ("k_norm_w", HEAD_DIM))
SMALL_LAYOUT = (("b_ada", 6), ("norm1_w", 1), ("norm2_w", 1), ("conv_w", 8), ("conv_b", 2), ("ssd_norm_w", 1),
                ("attn_norm_w", 1), ("misc", 1))


def _pack_small(vals):
    rows = []
    for name, nrow in SMALL_LAYOUT:
        if name == "misc":
            misc = jnp.concatenate([vals[f].reshape(1, n) for f, n in MISC_FIELDS], axis=1)
            rows.append(_pad_lanes(misc, D_MODEL))
        elif name in vals:
            rows.append(vals[name].reshape(nrow, D_MODEL))
        else:
            rows.append(jnp.zeros((nrow, D_MODEL), F32))
    used = sum(n for _, n in SMALL_LAYOUT)
    rows.append(jnp.zeros((SMALL_ROWS - used, D_MODEL), F32))
    return jnp.concatenate(rows, axis=0)


def _unpack_small(packed):
    out, r = {}, 0
    for name, nrow in SMALL_LAYOUT:
        blk = packed[r:r + nrow]
        r += nrow
        if name == "misc":
            c0 = 0
            for f, n in MISC_FIELDS:
                out[f] = blk[:, c0:c0 + n]
                c0 += n
        elif name == "b_ada":
            out[name] = blk.reshape(1, N_MOD * D_MODEL)
        elif name == "conv_w":
            out[name] = blk.reshape(CONV_K, CONV_CH)
        elif name == "conv_b":
            out[name] = blk.reshape(1, CONV_CH)
        else:
            out[name] = blk
    return out


WEIGHT_NAMES = ("norm1_w", "norm2_w", "w_ada", "b_ada", "w_in", "conv_w", "conv_b", "dt_bias", "a_log", "d_skip",
                "ssd_norm_w", "q_norm_w", "k_norm_w", "attn_norm_w", "w_out", "w_ff1", "w_ff2")
SMALL_NAMES = ("norm1_w", "norm2_w", "b_ada", "conv_b", "dt_bias", "a_log", "d_skip", "ssd_norm_w", "q_norm_w",
               "k_norm_w", "attn_norm_w")


def kernel(x, c, norm1_w, norm2_w, w_ada, b_ada, w_in, conv_w, conv_b, dt_bias, a_log, d_skip, ssd_norm_w, q_norm_w, k_norm_w, attn_norm_w, w_out, w_ff1, w_ff2, loss_target, m_norm1_w, m_norm2_w, m_w_ada, m_b_ada, m_w_in, m_conv_w, m_conv_b, m_dt_bias, m_a_log, m_d_skip, m_ssd_norm_w, m_q_norm_w, m_k_norm_w, m_attn_norm_w, m_w_out, m_w_ff1, m_w_ff2, v_norm1_w, v_norm2_w, v_w_ada, v_b_ada, v_w_in, v_conv_w, v_conv_b, v_dt_bias, v_a_log, v_d_skip, v_ssd_norm_w, v_q_norm_w, v_k_norm_w, v_attn_norm_w, v_w_out, v_w_ff1, v_w_ff2):
    args = dict(locals())
    w = {n: args[n] for n in WEIGHT_NAMES}
    m = {n: args["m_" + n] for n in WEIGHT_NAMES}
    v = {n: args["v_" + n] for n in WEIGHT_NAMES}
    me = 4 * lax.axis_index("x") + 2 * lax.axis_index("y") + lax.axis_index("c")

    c_rows = jnp.pad(c, ((0, 7), (0, 0)))
    w_in_t, m_in_t, v_in_t = [jnp.transpose(t["w_in"][0]) for t in (w, m, v)]
    c_g, conv_g, w_in_g = _exchange([c_rows, w["conv_w"][0], w_in_t.astype(BF16)], "gather_w_in", scatter=False)
    c_all = c_g[:, 0, :]
    conv_full = _shards_to_cols(conv_g)
    w_in_pt = _pack_w_in_rows(w_in_g.reshape(IN_W, D_MODEL))

    mod_part = _ada_fwd(c_all, w["w_ada"][0])
    (mod_g,) = _exchange([mod_part], "gather_mod", scatter=False)
    mod_mine = lax.dynamic_index_in_dim(mod_g, me, axis=1, keepdims=False).reshape(1, N_MOD * D_MODEL) + w["b_ada"]
    mod = mod_mine.reshape(N_MOD, D_MODEL)

    res = _local_step(x[0], loss_target[0], mod, w_in_pt, conv_full, w["conv_b"], w["dt_bias"], w["a_log"], w["d_skip"],
                      w["ssd_norm_w"], w["q_norm_w"], w["k_norm_w"], w["attn_norm_w"], w["w_out"][0].astype(BF16),
                      w["w_ff1"][0].astype(BF16), w["w_ff2"][0].astype(BF16), w["norm1_w"], w["norm2_w"], lax.axis_index("c"))

    small_vals = {n: res[n] for n in SMALL_NAMES if n != "b_ada"}
    small_vals["b_ada"] = res["dmod"]
    small_vals["conv_w"] = res["conv_w"]
    (small_g,) = _exchange([_pack_small(small_vals)], "gather_small", scatter=False)

    grads, delta, new_m, new_v = {}, {}, {}, {}
    for name in ("w_out", "w_ff1", "w_ff2"):
        outs = _reduce_adamw(res[name], w[name][0], m[name][0], v[name][0], "adamw_" + name)
        grads[name], delta[name], new_m[name], new_v[name] = [o[None] for o in outs]
    outs = _reduce_adamw(res["w_in"], w_in_t, m_in_t, v_in_t, "adamw_w_in")
    grads["w_in"], delta["w_in"], new_m["w_in"], new_v["w_in"] = [jnp.transpose(o)[None] for o in outs]

    sm = _small_reduce_adamw(small_g, _pack_small({n: w[n] for n in SMALL_NAMES}), _pack_small({n: m[n] for n in SMALL_NAMES}),
                             _pack_small({n: v[n] for n in SMALL_NAMES}))
    sm = [_unpack_small(p) for p in sm]
    for n in SMALL_NAMES:
        grads[n], delta[n], new_m[n], new_v[n] = [p[n] for p in sm]
    shard_w = CONV_CH // N_DEV
    g_conv = lax.dynamic_slice_in_dim(sm[0]["conv_w"], me * shard_w, shard_w, axis=1)
    cw = _adamw_small(g_conv, w["conv_w"][0], m["conv_w"][0], v["conv_w"][0], "adamw_conv_w")
    grads["conv_w"] = g_conv[None]
    delta["conv_w"], new_m["conv_w"], new_v["conv_w"] = [o[None] for o in cw]

    ada_w = w_ada.shape[2]
    dmod_all = small_g[:, :N_MOD, :].reshape(N_DEV, N_MOD * D_MODEL)
    dmod_cols = lax.dynamic_slice_in_dim(dmod_all, me * ada_w, ada_w, axis=1)
    outs = _ada_bwd_adamw(c_all, dmod_cols, w["w_ada"][0], m["w_ada"][0], v["w_ada"][0])
    grads["w_ada"], delta["w_ada"], new_m["w_ada"], new_v["w_ada"] = [o[None] for o in outs]

    loss = lax.psum(res["loss"][0, 0], ("x", "y", "c"))
    return (loss, res["grad_x"][None], *[grads[n] for n in WEIGHT_NAMES], *[delta[n] for n in WEIGHT_NAMES],
            *[new_m[n] for n in WEIGHT_NAMES], *[new_v[n] for n in WEIGHT_NAMES])
```

```python
import functools

import jax
import jax.numpy as jnp
from jax import lax
from jax.experimental import pallas as pl
from jax.experimental.pallas import tpu as pltpu

F32 = jnp.float32
BF16 = jnp.bfloat16
HIGHEST = lax.Precision.HIGHEST
MESH_IDS = pl.DeviceIdType.MESH

N_DEV = 8
D_MODEL = 1024
HEAD_DIM = 64
SSD_HEADS = 16
SSD_GROUPS = 4
HEADS_PER_GROUP = SSD_HEADS // SSD_GROUPS
SSD_STATE = 128
SSD_CHUNK = 128
SSD_D_INNER = SSD_HEADS * HEAD_DIM
GROUP_WIDTH = SSD_D_INNER // SSD_GROUPS
CONV_K = 4
CONV_CH = SSD_D_INNER + 2 * SSD_GROUPS * SSD_STATE
ATT_HEADS = 16
ATT_D = ATT_HEADS * HEAD_DIM
ATT_BLK = 128
DILATIONS = (1, 4, 16)
D_FF = 4 * D_MODEL
N_MOD = 6
EPS = 1e-6
IN_W = SSD_D_INNER + CONV_CH + SSD_HEADS + 3 * ATT_D
LANE = 128
OFF_Z, OFF_XBC, OFF_DT = 0, SSD_D_INNER, SSD_D_INNER + CONV_CH
OFF_Q = OFF_DT + LANE
OFF_K, OFF_V = OFF_Q + ATT_D, OFF_Q + 2 * ATT_D
IN_WP = OFF_V + ATT_D

ADAM_LR, ADAM_B1, ADAM_B2, ADAM_EPS, ADAM_WD, ADAM_STEP = 0.001, 0.9, 0.999, 1e-08, 0.01, 10
VMEM_LIMIT = 56 * 1024 * 1024
ROW_TILE = 512
SMALL_ROWS = 24


def _cparams(sem=None):
    return pltpu.CompilerParams(dimension_semantics=sem, vmem_limit_bytes=VMEM_LIMIT)


def _sigmoid(v):
    return 1.0 / (1.0 + jnp.exp(-v))


def _softplus(v):
    y = jnp.exp(-jnp.abs(v))
    small = y * (1.0 - y * (0.5 - y * (1.0 / 3.0)))
    return jnp.maximum(v, 0.0) + jnp.where(y < 0.01, small, jnp.log(1.0 + y))


def _dot(a, b, dims, precision=None):
    return lax.dot_general(a, b, (dims, ((), ())), preferred_element_type=F32, precision=precision)


NN = ((1,), (0,))
NT = ((1,), (1,))
TN = ((0,), (0,))


def _matmul(a, b, *, ta=False, tb=False, tm, tn, tk, out_dtype=F32, name, mode=None, u=None, comm=None, shard_out=False):
    m, k = (a.shape[1], a.shape[0]) if ta else a.shape
    n = b.shape[0] if tb else b.shape[1]
    assert m % tm == 0 and n % tn == 0 and k % tk == 0, (name, m, n, k)
    nk = k // tk
    a_spec = pl.BlockSpec((tk, tm), lambda i, j, kk: (kk, i)) if ta else pl.BlockSpec((tm, tk), lambda i, j, kk: (i, kk))
    b_spec = pl.BlockSpec((tn, tk), lambda i, j, kk: (j, kk)) if tb else pl.BlockSpec((tk, tn), lambda i, j, kk: (kk, j))
    o_spec = pl.BlockSpec((tm, tn), lambda i, j, kk: (i, j))
    dims = ((0,) if ta else (1,), (1,) if tb else (0,))
    n_out = 2 if mode == "relu2" else 1

    def body(*refs):
        if mode == "drelu2":
            a_ref, b_ref, u_ref = refs[:3]
            rest = refs[3:]
        else:
            a_ref, b_ref = refs[:2]
            u_ref = None
            rest = refs[2:]
        outs = rest[:n_out]
        part = _dot(a_ref[...], b_ref[...], dims)

        def finish(r):
            if mode == "relu2":
                outs[0][...] = r.astype(BF16)
                rr = jnp.maximum(r, 0.0)
                outs[1][...] = (rr * rr).astype(BF16)
            elif mode == "drelu2":
                outs[0][...] = (r * (2.0 * jnp.maximum(u_ref[...].astype(F32), 0.0))).astype(out_dtype)
            else:
                outs[0][...] = r.astype(out_dtype)

        if nk == 1:
            finish(part)
        else:
            acc = rest[n_out]
            kk = pl.program_id(2)

            @pl.when(kk == 0)
            def _():
                acc[...] = part

            @pl.when(kk > 0)
            def _():
                acc[...] += part

            @pl.when(kk == nk - 1)
            def _():
                finish(acc[...])

    in_specs = [a_spec, b_spec]
    args = [a, b]
    if mode == "drelu2":
        in_specs.append(o_spec)
        args.append(u)
    if mode == "relu2":
        out_shape = [jax.ShapeDtypeStruct((m, n), BF16), jax.ShapeDtypeStruct((m, n), BF16)]
    elif shard_out:
        out_shape = [jax.ShapeDtypeStruct((n // tn, m, tn), out_dtype)]
        o_spec = pl.BlockSpec((None, tm, tn), lambda i, j, kk: (j, i, 0))
    else:
        out_shape = [jax.ShapeDtypeStruct((m, n), out_dtype)]
    outs, comm_outs = _pcall(
        body, args, name=name, grid=(m // tm, n // tn, nk), in_specs=in_specs, out_specs=[o_spec] * n_out,
        out_shape=out_shape, scratch_shapes=[pltpu.VMEM((tm, tn), F32)] if nk > 1 else [],
        sem=("parallel", "parallel", "arbitrary"), comm=comm)
    res = tuple(outs) if mode == "relu2" else outs[0]
    return res if comm is None else (res, comm_outs)


def _rms_mod(xv, nw, scale, shift):
    r = lax.rsqrt(jnp.mean(xv * xv, axis=-1, keepdims=True) + EPS)
    return ((xv * r) * nw * (1.0 + scale) + shift).astype(BF16)


def _norm_mod_fwd(x, nw, scale, shift, name):
    s, d = x.shape
    row = pl.BlockSpec((ROW_TILE, d), lambda i: (i, 0))
    vec = pl.BlockSpec((1, d), lambda i: (0, 0))

    def body(x_ref, nw_ref, sc_ref, sh_ref, h_ref):
        h_ref[...] = _rms_mod(x_ref[...], nw_ref[...], sc_ref[...], sh_ref[...])

    return pl.pallas_call(body, name=name, grid=(s // ROW_TILE,), in_specs=[row, vec, vec, vec], out_specs=row,
                          out_shape=jax.ShapeDtypeStruct((s, d), BF16), compiler_params=_cparams(("parallel",)))(x, nw, scale, shift)


def _matmul_rows(a, b, epilogue, row_in, vec_in, outs, *, tb=False, tm, name, comm=None):
    pieces = a if isinstance(a, list) else [(a, 0)]
    assert not (tb and len(pieces) > 1)
    m = pieces[0][0].shape[0]
    n = b.shape[0] if tb else b.shape[1]
    assert m % tm == 0, (name, m, tm)
    dims = ((1,), (1,) if tb else (0,))
    n_a, n_row, n_vec = len(pieces), len(row_in), len(vec_in)

    def body(*refs):
        a_refs, b_ref, rest = refs[:n_a], refs[n_a], refs[n_a + 1:]
        if n_a == 1:
            c = _dot(a_refs[0][...], b_ref[...], dims)
        else:
            c = None
            for a_ref, (piece, off) in zip(a_refs, pieces):
                part = _dot(a_ref[...], b_ref[off:off + piece.shape[1], :], dims)
                c = part if c is None else c + part
        epilogue(c, pl.program_id(0) == 0, rest[:n_row], rest[n_row:n_row + n_vec], rest[n_row + n_vec:])

    def spec(kind, width):
        block = {"row": (tm, width), "vec": (1, width), "one": (1, 1)}[kind]
        return pl.BlockSpec(block, (lambda i: (i, 0)) if kind == "row" else (lambda i: (0, 0)))

    def shape(kind, width):
        return {"row": (m, width), "vec": (1, width), "one": (1, 1)}[kind]

    outs = [(o[0], o[1], o[2] if len(o) > 2 else n) for o in outs]
    res, comm_outs = _pcall(
        body, [*[p for p, _ in pieces], b, *row_in, *vec_in], name=name, grid=(m // tm,),
        in_specs=[spec("row", p.shape[1]) for p, _ in pieces] + [pl.BlockSpec(b.shape, lambda i: (0, 0))]
        + [spec("row", r.shape[1]) for r in row_in] + [spec("vec", v.shape[1]) for v in vec_in],
        out_specs=[spec(kind, width) for kind, _, width in outs],
        out_shape=[jax.ShapeDtypeStruct(shape(kind, width), dt) for kind, dt, width in outs],
        sem=("arbitrary",), comm=comm)
    return res if comm is None else (res, comm_outs)


def _residual_norm_epilogue(mix, first, rows, vecs, outs):
    (x_ref,), (gate_ref, nw_ref, sc_ref, sh_ref), (mix_ref, x1_ref, h_ref) = rows, vecs, outs
    xv = x_ref[...] + gate_ref[...] * mix
    mix_ref[...] = mix
    x1_ref[...] = xv
    h_ref[...] = _rms_mod(xv, nw_ref[...], sc_ref[...], sh_ref[...])


def _loss_epilogue(ff, first, rows, vecs, outs):
    (x1_ref, t_ref), (g_ref,), (loss_ref, dout_ref, dff_ref, dg_ref) = rows, vecs, outs
    d = ff.shape[1]

    @pl.when(first)
    def _():
        loss_ref[...] = jnp.zeros_like(loss_ref)
        dg_ref[...] = jnp.zeros_like(dg_ref)

    err = x1_ref[...] + g_ref[...] * ff - t_ref[...]
    loss_ref[...] += (0.5 / d) * jnp.sum(err * err).reshape(1, 1)
    dout = err * (1.0 / d)
    dout_ref[...] = dout
    dff_ref[...] = (g_ref[...] * dout).astype(BF16)
    dg_ref[...] += jnp.sum(dout * ff, axis=0, keepdims=True)


def _norm_bwd_epilogue(dh, first, rows, vecs, outs):
    with_gate = len(vecs) == 3
    x_ref, dres_ref = rows[:2]
    nw_ref, sc_ref = vecs[:2]
    dx_ref, dsh_ref, dsc_ref, dnw_ref = outs[:4]

    @pl.when(first)
    def _():
        for ref in outs[1:4] + outs[5:]:
            ref[...] = jnp.zeros_like(ref)

    xv = x_ref[...]
    r = lax.rsqrt(jnp.mean(xv * xv, axis=-1, keepdims=True) + EPS)
    nrm = xv * r
    one_sc = 1.0 + sc_ref[...]
    dhn = dh * nrm
    dsh_ref[...] += jnp.sum(dh, axis=0, keepdims=True)
    dsc_ref[...] += jnp.sum(dhn, axis=0, keepdims=True) * nw_ref[...]
    dnw_ref[...] += jnp.sum(dhn, axis=0, keepdims=True) * one_sc
    dn = dh * (nw_ref[...] * one_sc)
    dx = dres_ref[...] + r * (dn - nrm * jnp.mean(dn * nrm, axis=-1, keepdims=True))
    dx_ref[...] = dx
    if with_gate:
        outs[4][...] = (vecs[2][...] * dx).astype(BF16)
        outs[5][...] += jnp.sum(dx * rows[2][...], axis=0, keepdims=True)


CONV_COLS = 256
CONV_FWD_ROWS = 2048
CONV_BWD_ROWS = 1024
CONV_SUB_ROWS = 128
HALO = 8


def _shift_down(cur, halo, k):
    if k == 0:
        return cur
    rolled = pltpu.roll(cur, k, axis=0)
    top = jnp.where(lax.broadcasted_iota(jnp.int32, halo.shape, 0) < k, pltpu.roll(halo, k, axis=0), rolled[:HALO])
    return jnp.concatenate([top, rolled[HALO:]], axis=0)


def _shift_up(cur, halo, k):
    if k == 0:
        return cur
    t = cur.shape[0]
    rolled = pltpu.roll(cur, t - k, axis=0)
    bot = jnp.where(lax.broadcasted_iota(jnp.int32, halo.shape, 0) >= HALO - k, pltpu.roll(halo, HALO - k, axis=0),
                    rolled[t - HALO:])
    return jnp.concatenate([rolled[:t - HALO], bot], axis=0)


def _conv_fwd(proj, conv_w, conv_b):
    s = proj.shape[0]
    nr = s // CONV_FWD_ROWS
    cb0 = OFF_XBC // CONV_COLS
    hb = CONV_FWD_ROWS // HALO
    cur = pl.BlockSpec((CONV_FWD_ROWS, CONV_COLS), lambda j, r: (r, cb0 + j))
    prev = pl.BlockSpec((HALO, CONV_COLS), lambda j, r: (jnp.maximum(r * hb - 1, 0), cb0 + j))
    out = pl.BlockSpec((CONV_FWD_ROWS, CONV_COLS), lambda j, r: (r, j))

    def body(u_ref, up_ref, w_ref, b_ref, pre_ref, act_ref):
        r = pl.program_id(1)
        u = u_ref[...]
        halo = jnp.where(r > 0, up_ref[...], 0.0)
        acc = b_ref[...] + w_ref[CONV_K - 1:CONV_K, :] * u
        for k in range(1, CONV_K):
            acc = acc + w_ref[CONV_K - 1 - k:CONV_K - k, :] * _shift_down(u, halo, k)
        pre_ref[...] = acc
        act_ref[...] = acc * _sigmoid(acc)

    return pl.pallas_call(
        body, name="conv_fwd", grid=(CONV_CH // CONV_COLS, nr),
        in_specs=[cur, prev, pl.BlockSpec((CONV_K, CONV_COLS), lambda j, r: (0, j)),
                  pl.BlockSpec((1, CONV_COLS), lambda j, r: (0, j))],
        out_specs=[out, out],
        out_shape=[jax.ShapeDtypeStruct((s, CONV_CH), F32), jax.ShapeDtypeStruct((s, CONV_CH), F32)],
        compiler_params=_cparams(("parallel", "arbitrary")))(proj, proj, conv_w, conv_b)


def _conv_bwd(dact, pre, proj, conv_w):
    s = proj.shape[0]
    nr = s // CONV_BWD_ROWS
    cb0 = OFF_XBC // CONV_COLS
    hb = CONV_BWD_ROWS // HALO
    last_halo = s // HALO - 1
    n_sub = CONV_BWD_ROWS // CONV_SUB_ROWS
    cur = pl.BlockSpec((CONV_BWD_ROWS, CONV_COLS), lambda j, r: (r, j))
    nxt = pl.BlockSpec((HALO, CONV_COLS), lambda j, r: (jnp.minimum((r + 1) * hb, last_halo), j))
    ucur = pl.BlockSpec((CONV_BWD_ROWS, CONV_COLS), lambda j, r: (r, cb0 + j))
    wspec = pl.BlockSpec((CONV_K, CONV_COLS), lambda j, r: (0, j))
    bspec = pl.BlockSpec((1, CONV_COLS), lambda j, r: (0, j))

    def dsilu(p):
        sg = _sigmoid(p)
        return sg * (1.0 + p * (1.0 - sg))

    def body(da_ref, dan_ref, pre_ref, pren_ref, u_ref, w_ref, du_ref, dw_ref, db_ref):
        r = pl.program_id(1)

        @pl.when(r == 0)
        def _():
            dw_ref[...] = jnp.zeros_like(dw_ref)
            db_ref[...] = jnp.zeros_like(db_ref)

        dws = [jnp.zeros((1, CONV_COLS), F32) for _ in range(CONV_K)]
        db = jnp.zeros((1, CONV_COLS), F32)
        for c in range(n_sub):
            rows = slice(c * CONV_SUB_ROWS, (c + 1) * CONV_SUB_ROWS)
            ahead = slice((c + 1) * CONV_SUB_ROWS, (c + 1) * CONV_SUB_ROWS + HALO)
            dpre = da_ref[rows, :] * dsilu(pre_ref[rows, :])
            if c < n_sub - 1:
                dnext = da_ref[ahead, :] * dsilu(pre_ref[ahead, :])
            else:
                dnext = jnp.where(r < nr - 1, dan_ref[...] * dsilu(pren_ref[...]), 0.0)
            u = u_ref[rows, :]
            du = w_ref[CONV_K - 1:CONV_K, :] * dpre
            dws[0] = dws[0] + jnp.sum(dpre * u, axis=0, keepdims=True)
            for k in range(1, CONV_K):
                ahead_k = _shift_up(dpre, dnext, k)
                du = du + w_ref[CONV_K - 1 - k:CONV_K - k, :] * ahead_k
                dws[k] = dws[k] + jnp.sum(ahead_k * u, axis=0, keepdims=True)
            du_ref[rows, :] = du.astype(BF16)
            db = db + jnp.sum(dpre, axis=0, keepdims=True)
        dw_ref[...] += jnp.concatenate(dws[::-1], axis=0)
        db_ref[...] += db

    return pl.pallas_call(
        body, name="conv_bwd", grid=(CONV_CH // CONV_COLS, nr),
        in_specs=[cur, nxt, cur, nxt, ucur, wspec],
        out_specs=[cur, wspec, bspec],
        out_shape=[jax.ShapeDtypeStruct((s, CONV_CH), BF16), jax.ShapeDtypeStruct((CONV_K, CONV_CH), F32),
                   jax.ShapeDtypeStruct((1, CONV_CH), F32)],
        compiler_params=_cparams(("parallel", "arbitrary")))(dact, dact, pre, pre, proj, conv_w)


def _ssd_common(dtr, dtb, alog):
    lane = lax.broadcasted_iota(jnp.int32, (1, LANE), 1)
    head_lane = lane < SSD_HEADS
    dt = jnp.where(head_lane, _softplus(dtr + dtb), 0.0)
    a = jnp.where(head_lane, -jnp.exp(alog), 0.0)
    row = lax.broadcasted_iota(jnp.int32, (SSD_CHUNK, SSD_CHUNK), 0)
    col = lax.broadcasted_iota(jnp.int32, (SSD_CHUNK, SSD_CHUNK), 1)
    tril = row >= col
    cs = _dot(tril.astype(F32), dt * a, NN, precision=HIGHEST)
    return dt, a, cs, cs.T, tril, lane


def _split_bf16(v, passes):
    terms, rest = [], v
    for _ in range(passes):
        t = rest.astype(BF16)
        terms.append(t)
        rest = rest - t.astype(F32)
    return terms


def _dot_split(v, m, dims, passes):
    terms = _split_bf16(v, passes)
    if passes == 1:
        return _dot(terms[0], m, dims)
    return _dot(jnp.concatenate(terms, axis=1), jnp.concatenate([m] * passes, axis=0 if dims == NN else 1), dims)


def _ssd_constants():
    heads = jnp.arange(LANE)[:, None]
    exp_mat = (heads == (jnp.arange(SSD_D_INNER)[None, :] // HEAD_DIM)).astype(BF16)
    ind4 = ((jnp.arange(SSD_HEADS * SSD_CHUNK)[:, None] // SSD_CHUNK) == jnp.arange(LANE)[None, :]).astype(BF16)
    return exp_mat, ind4


def _expand_heads(v):
    return jnp.repeat(v[:, :SSD_HEADS], HEAD_DIM, axis=1)


def _ssd_prep(dtr, dtb, alog, exp_mat):
    dt, a, cs, cst, tril, lane = _ssd_common(dtr, dtb, alog)
    return dt, a, cs, cst, tril, lane, _dot_split(dt, exp_mat, NN, 2), _dot_split(cs, exp_mat, NN, 3)


def _chunk_decay_rows(cs, g):
    parts = []
    for e in range(HEADS_PER_GROUP):
        h = g * HEADS_PER_GROUP + e
        parts.append(jnp.broadcast_to(jnp.exp(cs[SSD_CHUNK - 1:SSD_CHUNK, h:h + 1]), (HEAD_DIM, SSD_STATE)))
    return jnp.concatenate(parts, axis=0)


def _ssd_fwd(proj, act, dtb, alog, dsk, nw):
    s = proj.shape[0]
    nc = s // SSD_CHUNK
    bc_w = SSD_GROUPS * SSD_STATE
    exp_mat, _ = _ssd_constants()

    def body(z_ref, dtr_ref, xs_ref, b_ref, c_ref, dtb_ref, alog_ref, dskx_ref, nw_ref, exp_ref,
             ypre_ref, yssd_ref, hall_ref, h_scr):
        @pl.when(pl.program_id(0) == 0)
        def _():
            h_scr[...] = jnp.zeros_like(h_scr)

        dt, a, cs, cst, tril, lane, dtx, csx = _ssd_prep(dtr_ref[...], dtb_ref[...], alog_ref[...], exp_ref[...])
        cs_last_x = csx[SSD_CHUNK - 1:SSD_CHUNK, :]
        xs = xs_ref[...]
        xdt = xs * dtx
        xdtb = xdt.astype(BF16)
        xdec = (xdt * jnp.exp(cs_last_x - csx)).astype(BF16)
        ecsx = jnp.exp(csx)
        head_of_lane = lax.broadcasted_iota(jnp.int32, (1, GROUP_WIDTH), 1) // HEAD_DIM
        for g in range(SSD_GROUPS):
            gs = slice(g * GROUP_WIDTH, (g + 1) * GROUP_WIDTH)
            bg = b_ref[:, g * SSD_STATE:(g + 1) * SSD_STATE].astype(BF16)
            cg = c_ref[:, g * SSD_STATE:(g + 1) * SSD_STATE].astype(BF16)
            cb = _dot(cg, bg, NT)
            hprev = h_scr[gs, :]
            hall_ref[0, gs, :] = hprev
            gms, rhs = [], []
            xg = xdtb[:, gs]
            for e in range(HEADS_PER_GROUP):
                h = g * HEADS_PER_GROUP + e
                lm = jnp.exp(jnp.where(tril, cs[:, h:h + 1] - cst[h:h + 1, :], -1e30))
                gms.append((cb * lm).astype(BF16))
                rhs.append(jnp.where(head_of_lane == e, xg, jnp.zeros_like(xg)))
            y = _dot(jnp.concatenate(gms, axis=1), jnp.concatenate(rhs, axis=0), NN)
            y = y + ecsx[:, gs] * _dot(cg, hprev.astype(BF16), NT)
            y = y + dskx_ref[:, gs] * xs[:, gs]
            h_scr[gs, :] = hprev * _chunk_decay_rows(cs, g) + _dot(xdec[:, gs], bg, TN)
            ypre_ref[:, gs] = y
            z = z_ref[:, gs]
            yg = y * (z * _sigmoid(z))
            r = lax.rsqrt(jnp.mean(yg * yg, axis=-1, keepdims=True) + EPS)
            yssd_ref[:, gs] = (yg * r * nw_ref[:, gs]).astype(BF16)

    row_d = lambda cb: pl.BlockSpec((SSD_CHUNK, SSD_D_INNER), lambda c: (c, cb))
    small = pl.BlockSpec((1, LANE), lambda c: (0, 0))
    wide = pl.BlockSpec((1, SSD_D_INNER), lambda c: (0, 0))
    return pl.pallas_call(
        body, name="ssd_fwd", grid=(nc,),
        in_specs=[row_d(OFF_Z // SSD_D_INNER),
                  pl.BlockSpec((SSD_CHUNK, LANE), lambda c: (c, OFF_DT // LANE)),
                  row_d(0),
                  pl.BlockSpec((SSD_CHUNK, bc_w), lambda c: (c, SSD_D_INNER // bc_w)),
                  pl.BlockSpec((SSD_CHUNK, bc_w), lambda c: (c, SSD_D_INNER // bc_w + 1)),
                  small, small, wide, wide, pl.BlockSpec((LANE, SSD_D_INNER), lambda c: (0, 0))],
        out_specs=[row_d(0), row_d(0), pl.BlockSpec((1, SSD_D_INNER, SSD_STATE), lambda c: (c, 0, 0))],
        out_shape=[jax.ShapeDtypeStruct((s, SSD_D_INNER), F32), jax.ShapeDtypeStruct((s, SSD_D_INNER + ATT_D), BF16),
                   jax.ShapeDtypeStruct((nc, SSD_D_INNER, SSD_STATE), F32)],
        scratch_shapes=[pltpu.VMEM((SSD_D_INNER, SSD_STATE), F32)],
        compiler_params=_cparams(("arbitrary",)))(proj, proj, act, act, act, dtb, alog, _expand_heads(dsk), nw, exp_mat)


def _ssd_bwd(dycat, ypre, proj, act, hall, dtb, alog, dsk, nw, comm=None):
    s = proj.shape[0]
    nc = s // SSD_CHUNK
    bc_w = SSD_GROUPS * SSD_STATE

    exp_mat, ind4 = _ssd_constants()
    seg_passes = 1

    def body(dy_ref, ypre_ref, z_ref, dtr_ref, xs_ref, b_ref, c_ref, hall_ref, dtb_ref, alog_ref, dskx_ref, nw_ref,
             exp_ref, ind4_ref, dz_ref, dact_ref, ddtr_ref, da_ref, ddsk_ref, ddtb_ref, dnw_ref, dh_scr):
        @pl.when(pl.program_id(0) == 0)
        def _():
            dh_scr[...] = jnp.zeros_like(dh_scr)
            da_ref[...] = jnp.zeros_like(da_ref)
            ddsk_ref[...] = jnp.zeros_like(ddsk_ref)
            ddtb_ref[...] = jnp.zeros_like(ddtb_ref)
            dnw_ref[...] = jnp.zeros_like(dnw_ref)

        dtr = dtr_ref[...]
        dt, a, cs, cst, tril, lane, dtx, csx = _ssd_prep(dtr, dtb_ref[...], alog_ref[...], exp_ref[...])
        cs_last_x = csx[SSD_CHUNK - 1:SSD_CHUNK, :]
        xs = xs_ref[...]
        xdt = xs * dtx
        xdtb = xdt.astype(BF16)
        decx = jnp.exp(cs_last_x - csx)
        xdecf = xdt * decx
        xdec = xdecf.astype(BF16)
        ecsx = jnp.exp(csx)
        head_of_lane = lax.broadcasted_iota(jnp.int32, (1, GROUP_WIDTH), 1) // HEAD_DIM
        last_row = lax.broadcasted_iota(jnp.int32, (SSD_CHUNK, 1), 0) == SSD_CHUNK - 1
        dcs_col = jnp.zeros((SSD_CHUNK, LANE), F32)
        dcs_row = jnp.zeros((SSD_CHUNK, LANE), F32)
        ddt = jnp.zeros((SSD_CHUNK, LANE), F32)
        ddsk = jnp.zeros((1, LANE), F32)
        hsum = jnp.zeros((1, LANE), F32)
        t1_sum = jnp.zeros((1, LANE), F32)
        for g in range(SSD_GROUPS):
            gs = slice(g * GROUP_WIDTH, (g + 1) * GROUP_WIDTH)
            bsl = slice(g * SSD_STATE, (g + 1) * SSD_STATE)
            exp_g = exp_ref[:, gs]
            ind4_g = ind4_ref[g * HEADS_PER_GROUP * SSD_CHUNK:(g + 1) * HEADS_PER_GROUP * SSD_CHUNK, :]
            z = z_ref[:, gs]
            sg = _sigmoid(z)
            sz = z * sg
            ypre = ypre_ref[:, gs]
            yg = ypre * sz
            r = lax.rsqrt(jnp.mean(yg * yg, axis=-1, keepdims=True) + EPS)
            nrm = yg * r
            dyo_n = dy_ref[:, gs]
            dnw_ref[:, gs] += jnp.sum(dyo_n * nrm, axis=0, keepdims=True)
            dn = dyo_n * nw_ref[:, gs]
            dyg = r * (dn - nrm * jnp.mean(dn * nrm, axis=-1, keepdims=True))
            dz_ref[:, gs] = (dyg * ypre * (sg * (1.0 + z * (1.0 - sg)))).astype(BF16)
            dy = dyg * sz

            bg = b_ref[:, bsl].astype(BF16)
            cg = c_ref[:, bsl].astype(BF16)
            cb = _dot(cg, bg, NT)
            hprev = hall_ref[0, gs, :]
            hb = hprev.astype(BF16)
            dhn = dh_scr[gs, :]
            dhb = dhn.astype(BF16)
            xs_g, xdt_g = xs[:, gs], xdtb[:, gs]
            w_off = _dot(cg, hb, NT)
            dyo = dy * ecsx[:, gs]
            dyob = dyo.astype(BF16)
            dcg = _dot(dyob, hb, NN)
            dh_y = _dot(dyob, cg, TN)
            r_st = _dot(bg, dhb, NT)
            dbg = _dot(xdec[:, gs], dhb, NN)
            dyb = dy.astype(BF16)
            gms, gmbs, lms, dys = [], [], [], []
            for e in range(HEADS_PER_GROUP):
                h = g * HEADS_PER_GROUP + e
                lm = jnp.exp(jnp.where(tril, cs[:, h:h + 1] - cst[h:h + 1, :], -1e30))
                gm = cb * lm
                lms.append(lm)
                gms.append(gm)
                gmbs.append(gm.astype(BF16))
                dys.append(jnp.where(head_of_lane == e, dyb, jnp.zeros_like(dyb)))
            dxdt = _dot(jnp.concatenate(gmbs, axis=0), jnp.concatenate(dys, axis=0), TN) + decx[:, gs] * r_st
            dcb = jnp.zeros((SSD_CHUNK, SSD_CHUNK), F32)
            mms = []
            for e in range(HEADS_PER_GROUP):
                dg = _dot(dys[e], xdt_g, NT)
                mms.append(dg * gms[e])
                dcb = dcb + dg * lms[e]
            seg = _dot_split(jnp.concatenate([dyo * w_off, xdecf[:, gs] * r_st, dxdt * xs_g, dy * xs_g], axis=0), exp_g, NT, seg_passes)
            v1, t1, ddt_g, dsk_g = [seg[i * SSD_CHUNK:(i + 1) * SSD_CHUNK] for i in range(4)]
            dcs_col = dcs_col + v1 - t1 + _dot_split(jnp.concatenate(mms, axis=1), ind4_g, NN, seg_passes)
            for t in _split_bf16(jnp.concatenate(mms, axis=0), seg_passes):
                dcs_row = dcs_row + _dot(ind4_g, t, TN)
            ddt = ddt + ddt_g
            ddsk = ddsk + jnp.sum(dsk_g, axis=0, keepdims=True)
            t1_sum = t1_sum + jnp.sum(t1, axis=0, keepdims=True)
            for e in range(HEADS_PER_GROUP):
                h = g * HEADS_PER_GROUP + e
                hs = slice(e * HEAD_DIM, (e + 1) * HEAD_DIM)
                hsum = hsum + jnp.where(lane == h, jnp.sum(dhn[hs, :] * hprev[hs, :]).reshape(1, 1), 0.0)
            dh_scr[gs, :] = dhn * _chunk_decay_rows(cs, g) + dh_y
            dcbb = dcb.astype(BF16)
            dact_ref[:, gs] = dxdt * dtx[:, gs] + dskx_ref[:, gs] * dy
            dact_ref[:, SSD_D_INNER + g * SSD_STATE:SSD_D_INNER + (g + 1) * SSD_STATE] = dbg + _dot(dcbb, cg, TN)
            dact_ref[:, SSD_D_INNER + bc_w + g * SSD_STATE:SSD_D_INNER + bc_w + (g + 1) * SSD_STATE] = dcg + _dot(dcbb, bg, NN)
        dlast = t1_sum + jnp.exp(cs[SSD_CHUNK - 1:SSD_CHUNK, :]) * hsum
        dcs = dcs_col - dcs_row.T + jnp.where(last_row, dlast, 0.0)
        row = lax.broadcasted_iota(jnp.int32, (SSD_CHUNK, SSD_CHUNK), 0)
        col = lax.broadcasted_iota(jnp.int32, (SSD_CHUNK, SSD_CHUNK), 1)
        dda = _dot((col >= row).astype(F32), dcs, NN, precision=HIGHEST)
        ddt = ddt + dda * a
        da_ref[...] += jnp.sum(dda * dt, axis=0, keepdims=True)
        ddtr = jnp.where(lane < SSD_HEADS, ddt * _sigmoid(dtr + dtb_ref[...]), 0.0)
        ddtr_ref[...] = ddtr.astype(BF16)
        ddtb_ref[...] += jnp.sum(ddtr, axis=0, keepdims=True)
        ddsk_ref[...] += ddsk

    rev = lambda c: nc - 1 - c
    row_d = lambda cb: pl.BlockSpec((SSD_CHUNK, SSD_D_INNER), lambda c: (rev(c), cb))
    small = pl.BlockSpec((1, LANE), lambda c: (0, 0))
    wide = pl.BlockSpec((1, SSD_D_INNER), lambda c: (0, 0))
    small_shape = jax.ShapeDtypeStruct((1, LANE), F32)
    return _pcall(
        body, (dycat, ypre, proj, proj, act, act, act, hall, dtb, alog, _expand_heads(dsk), nw, exp_mat, ind4),
        name="ssd_bwd", grid=(nc,),
        in_specs=[row_d(0), row_d(0), row_d(OFF_Z // SSD_D_INNER),
                  pl.BlockSpec((SSD_CHUNK, LANE), lambda c: (rev(c), OFF_DT // LANE)),
                  row_d(0),
                  pl.BlockSpec((SSD_CHUNK, bc_w), lambda c: (rev(c), SSD_D_INNER // bc_w)),
                  pl.BlockSpec((SSD_CHUNK, bc_w), lambda c: (rev(c), SSD_D_INNER // bc_w + 1)),
                  pl.BlockSpec((1, SSD_D_INNER, SSD_STATE), lambda c: (rev(c), 0, 0)),
                  small, small, wide, wide, pl.BlockSpec((LANE, SSD_D_INNER), lambda c: (0, 0)),
                  pl.BlockSpec((SSD_HEADS * SSD_CHUNK, LANE), lambda c: (0, 0))],
        out_specs=[row_d(0), pl.BlockSpec((SSD_CHUNK, CONV_CH), lambda c: (rev(c), 0)),
                   pl.BlockSpec((SSD_CHUNK, LANE), lambda c: (rev(c), 0)), small, small, small, wide],
        out_shape=[jax.ShapeDtypeStruct((s, SSD_D_INNER), BF16), jax.ShapeDtypeStruct((s, CONV_CH), F32),
                   jax.ShapeDtypeStruct((s, LANE), BF16), small_shape, small_shape, small_shape,
                   jax.ShapeDtypeStruct((1, SSD_D_INNER), F32)],
        scratch_shapes=[pltpu.VMEM((SSD_D_INNER, SSD_STATE), F32)], sem=("arbitrary",), comm=comm)


def _head_mean_matrix():
    row = lax.broadcasted_iota(jnp.int32, (LANE, LANE), 0) // HEAD_DIM
    col = lax.broadcasted_iota(jnp.int32, (LANE, LANE), 1) // HEAD_DIM
    return (row == col).astype(F32)


def _head_sum2(v, ones_bd):
    hi = v.astype(BF16)
    lo = (v - hi.astype(F32)).astype(BF16)
    return _dot(jnp.concatenate([hi, lo], axis=1), jnp.concatenate([ones_bd, ones_bd], axis=0), NN)


def _head_norm(x, w, scale, ones_bd):
    ms = _head_sum2(x * x, ones_bd) * (1.0 / HEAD_DIM)
    return (x * lax.rsqrt(ms + EPS)) * (w * scale)


PRO_ROWS = 256
ATT_GROUP_FWD = 16
ATT_GROUP_BWD = 8
KEYS = 2 * ATT_BLK
NEG = -1e30
HALF = HEAD_DIM // 2


def _rows(start, size, dil):
    return pl.ds(start, size) if dil == 1 else pl.ds(start, size, stride=dil)


def _fill_bias(bias_ref):
    row = lax.broadcasted_iota(jnp.int32, (ATT_BLK, 2 * KEYS), 0)
    col = lax.broadcasted_iota(jnp.int32, (ATT_BLK, 2 * KEYS), 1) & (KEYS - 1)
    for first, off in ((0, 0), (1, ATT_BLK)):
        dist = off + row - col
        bias_ref[first] = jnp.where((dist >= 0) & (dist <= ATT_BLK), 0.0, NEG)


def _pair(a, b):
    return jnp.concatenate([jnp.broadcast_to(a, (ATT_BLK, KEYS)), jnp.broadcast_to(b, (ATT_BLK, KEYS))], axis=1)


def _split_heads(x, is_a):
    zero = jnp.zeros_like(x)
    return jnp.concatenate([jnp.where(is_a, x, zero), jnp.where(is_a, zero, x)], axis=0)


def _block_ids(b, nb):
    i = b & (nb - 1)
    q0 = pl.multiple_of(b * ATT_BLK, ATT_BLK)
    k0 = pl.multiple_of((b - jnp.minimum(i, 1)) * ATT_BLK, ATT_BLK)
    return pl.ds(q0, ATT_BLK), pl.ds(k0, KEYS), jnp.minimum(i, 1)


def _att_fwd(proj, qw, kw, comm=None):
    s = proj.shape[0]
    nblk = s // ATT_BLK
    assert all((s // d) // ATT_BLK >= 2 for d in DILATIONS)
    blk = lambda off: pl.BlockSpec((s, LANE), lambda i: (0, off // LANE + i))
    wspec = pl.BlockSpec((1, LANE), lambda i: (0, i))
    oblk = pl.BlockSpec((s, LANE), lambda i: (0, i))

    def body(q_ref, k_ref, v_ref, qw_ref, kw_ref, o_ref, lse_ref, qn, kn, q_cm, k_cm, v_cm, m_acc, l_acc, o_d, m_d, l_d, bias):
        ones_bd = _head_mean_matrix().astype(BF16)
        is_a = lax.broadcasted_iota(jnp.int32, (1, LANE), 1) < HEAD_DIM
        ones_ext = _split_heads(jnp.ones((KEYS, LANE), BF16), is_a)
        _fill_bias(bias)

        def pro(j, c):
            rows = pl.ds(pl.multiple_of(j * PRO_ROWS, PRO_ROWS), PRO_ROWS)
            qn[rows, :] = _head_norm(q_ref[rows, :], qw_ref[...], HEAD_DIM ** -0.5, ones_bd)
            kn[rows, :] = _head_norm(k_ref[rows, :], kw_ref[...], 1.0, ones_bd)
            return c

        lax.fori_loop(0, s // PRO_ROWS, pro, 0)

        for dil in DILATIONS:
            ln = s // dil
            nb = ln // ATT_BLK
            o_out, m_out, l_out = (o_ref, m_acc, l_acc) if dil == 1 else (o_d, m_d, l_d)
            for r in range(dil):
                def relayout(j, c, dil=dil, r=r, ln=ln):
                    j0 = pl.multiple_of(j * PRO_ROWS, PRO_ROWS)
                    src = _rows(r + dil * j0, PRO_ROWS, dil)
                    dst = pl.ds(r * ln + j0, PRO_ROWS)
                    q_cm[dst, :] = qn[src, :].astype(BF16)
                    k_cm[dst, :] = kn[src, :].astype(BF16)
                    v_cm[dst, :] = v_ref[src, :].astype(BF16)
                    return c

                lax.fori_loop(0, ln // PRO_ROWS, relayout, 0)

            def step(bg, c, nb=nb, o_out=o_out, m_out=m_out, l_out=l_out):
                ids = [_block_ids(bg * ATT_GROUP_FWD + u, nb) for u in range(ATT_GROUP_FWD)]
                kbs = [_split_heads(k_cm[krows, :], is_a) for _, krows, _ in ids]
                scs = [_dot(q_cm[qrows, :], kb, NT) + bias[first] for (qrows, _, first), kb in zip(ids, kbs)]
                mas = [jnp.max(sc[:, :KEYS], axis=-1, keepdims=True) for sc in scs]
                mbs = [jnp.max(sc[:, KEYS:], axis=-1, keepdims=True) for sc in scs]
                ps = [jnp.exp(sc - _pair(ma, mb)).astype(BF16) for sc, ma, mb in zip(scs, mas, mbs)]
                vbs = [jnp.concatenate([_split_heads(v_cm[krows, :], is_a), ones_ext], axis=1) for _, krows, _ in ids]
                ols = [_dot(p, vb, NN) for p, vb in zip(ps, vbs)]
                for (qrows, _, _), ol, ma, mb in zip(ids, ols, mas, mbs):
                    o_out[qrows, :] = ol[:, :LANE]
                    l_out[qrows, :] = ol[:, LANE:]
                    m_out[qrows, :] = jnp.where(is_a, ma, mb)
                return c

            lax.fori_loop(0, nblk // ATT_GROUP_FWD, step, 0)

            if dil > 1:
                for r in range(dil):
                    def merge(j, c, dil=dil, r=r, ln=ln):
                        j0 = pl.multiple_of(j * PRO_ROWS, PRO_ROWS)
                        nat = _rows(r + dil * j0, PRO_ROWS, dil)
                        cm = pl.ds(r * ln + j0, PRO_ROWS)
                        m_old, m_new = m_acc[nat, :], m_d[cm, :]
                        m = jnp.maximum(m_old, m_new)
                        a_old, a_new = jnp.exp(m_old - m), jnp.exp(m_new - m)
                        o_ref[nat, :] = a_old * o_ref[nat, :] + a_new * o_d[cm, :]
                        l_acc[nat, :] = a_old * l_acc[nat, :] + a_new * l_d[cm, :]
                        m_acc[nat, :] = m
                        return c

                    lax.fori_loop(0, ln // PRO_ROWS, merge, 0)

        def epi(j, c):
            rows = pl.ds(pl.multiple_of(j * PRO_ROWS, PRO_ROWS), PRO_ROWS)
            l = l_acc[rows, :]
            o_ref[rows, :] = o_ref[rows, :] / l
            lse_ref[rows, :] = m_acc[rows, :] + jnp.log(l)
            return c

        lax.fori_loop(0, s // PRO_ROWS, epi, 0)

    f = jax.ShapeDtypeStruct((s, ATT_D), F32)
    scr = pltpu.VMEM((s, LANE), F32)
    scb = pltpu.VMEM((s, LANE), BF16)
    return _pcall(
        body, (proj, proj, proj, qw, kw), name="att_fwd", grid=(ATT_D // LANE,),
        in_specs=[blk(OFF_Q), blk(OFF_K), blk(OFF_V), wspec, wspec], out_specs=[oblk, oblk], out_shape=[f, f],
        scratch_shapes=[scr, scr, scb, scb, scb, scr, scr, scr, scr, scr, pltpu.VMEM((2, ATT_BLK, 2 * KEYS), F32)],
        sem=("parallel",), comm=comm)


def _att_bwd(proj, do, stats, qw, kw, comm=None):
    s = proj.shape[0]
    nblk = s // ATT_BLK
    blk = lambda off: pl.BlockSpec((s, LANE), lambda i: (0, off // LANE + i))
    wspec = pl.BlockSpec((1, LANE), lambda i: (0, i))
    oblk = pl.BlockSpec((s, LANE), lambda i: (0, i))

    def body(q_ref, k_ref, v_ref, do_ref, st_ref, qw_ref, kw_ref, dq_ref, dk_ref, dv_ref, dqw_ref, dkw_ref,
             qn, kn, q_cm, do_cm, k_cm, v_cm, st_cm, dq_acc, dk_acc, dv_acc, dq_d, dk_d, dv_d, bias):
        ones_bd = _head_mean_matrix().astype(BF16)
        is_a = lax.broadcasted_iota(jnp.int32, (1, LANE), 1) < HEAD_DIM
        _fill_bias(bias)
        zero = jnp.zeros((PRO_ROWS, LANE), F32)

        def pro(j, c):
            rows = pl.ds(pl.multiple_of(j * PRO_ROWS, PRO_ROWS), PRO_ROWS)
            qn[rows, :] = _head_norm(q_ref[rows, :], qw_ref[...], HEAD_DIM ** -0.5, ones_bd)
            kn[rows, :] = _head_norm(k_ref[rows, :], kw_ref[...], 1.0, ones_bd)
            dk_acc[rows, :] = zero
            dv_acc[rows, :] = zero
            return c

        lax.fori_loop(0, s // PRO_ROWS, pro, 0)

        for dil in DILATIONS:
            ln = s // dil
            nb = ln // ATT_BLK
            dq_o, dk_o, dv_o = (dq_acc, dk_acc, dv_acc) if dil == 1 else (dq_d, dk_d, dv_d)
            for r in range(dil):
                def relayout(j, c, dil=dil, r=r, ln=ln):
                    j0 = pl.multiple_of(j * PRO_ROWS, PRO_ROWS)
                    src = _rows(r + dil * j0, PRO_ROWS, dil)
                    dst = pl.ds(r * ln + j0, PRO_ROWS)
                    q_cm[dst, :] = qn[src, :].astype(BF16)
                    k_cm[dst, :] = kn[src, :].astype(BF16)
                    v_cm[dst, :] = v_ref[src, :].astype(BF16)
                    do_cm[dst, :] = do_ref[src, :].astype(BF16)
                    st_cm[dst, :] = st_ref[src, :]
                    if dil > 1:
                        dk_d[dst, :] = zero
                        dv_d[dst, :] = zero
                    return c

                lax.fori_loop(0, ln // PRO_ROWS, relayout, 0)

            def step(bg, c, nb=nb, dq_o=dq_o, dk_o=dk_o, dv_o=dv_o):
                ids = [_block_ids(bg * ATT_GROUP_BWD + u, nb) for u in range(ATT_GROUP_BWD)]
                qbs = [q_cm[qrows, :] for qrows, _, _ in ids]
                dobs = [do_cm[qrows, :] for qrows, _, _ in ids]
                kbs = [_split_heads(k_cm[krows, :], is_a) for _, krows, _ in ids]
                vbs = [_split_heads(v_cm[krows, :], is_a) for _, krows, _ in ids]
                sts = [st_cm[qrows, :] for qrows, _, _ in ids]
                scs = [_dot(qb, kb, NT) + bias[first] for qb, kb, (_, _, first) in zip(qbs, kbs, ids)]
                dps = [_dot(dob, vb, NT) for dob, vb in zip(dobs, vbs)]
                ps = [jnp.exp(sc - _pair(st[:, 0:1], st[:, HEAD_DIM:HEAD_DIM + 1])) for sc, st in zip(scs, sts)]
                dss = [(p * (dp - _pair(st[:, HALF:HALF + 1], st[:, HEAD_DIM + HALF:HEAD_DIM + HALF + 1]))).astype(BF16)
                       for p, dp, st in zip(ps, dps, sts)]
                dqs = [_dot(ds, kb, NN) for ds, kb in zip(dss, kbs)]
                dkfs = [_dot(ds, qb, TN) for ds, qb in zip(dss, qbs)]
                dvfs = [_dot(p.astype(BF16), dob, TN) for p, dob in zip(ps, dobs)]
                for (qrows, krows, _), dq, dkf, dvf in zip(ids, dqs, dkfs, dvfs):
                    dq_o[qrows, :] = dq
                    dk_o[krows, :] += jnp.where(is_a, dkf[:KEYS], dkf[KEYS:])
                    dv_o[krows, :] += jnp.where(is_a, dvf[:KEYS], dvf[KEYS:])
                return c

            lax.fori_loop(0, nblk // ATT_GROUP_BWD, step, 0)

            if dil > 1:
                for r in range(dil):
                    def merge(j, c, dil=dil, r=r, ln=ln):
                        j0 = pl.multiple_of(j * PRO_ROWS, PRO_ROWS)
                        nat = _rows(r + dil * j0, PRO_ROWS, dil)
                        cm = pl.ds(r * ln + j0, PRO_ROWS)
                        dq_acc[nat, :] += dq_d[cm, :]
                        dk_acc[nat, :] += dk_d[cm, :]
                        dv_acc[nat, :] += dv_d[cm, :]
                        return c

                    lax.fori_loop(0, ln // PRO_ROWS, merge, 0)

        def back(dn_out, x, w, scale):
            r = lax.rsqrt(_head_sum2(x * x, ones_bd) * (1.0 / HEAD_DIM) + EPS)
            nrm = x * r
            dw = jnp.sum(dn_out * nrm, axis=0, keepdims=True) * scale
            dn = dn_out * (w * scale)
            return r * (dn - nrm * (_head_sum2(dn * nrm, ones_bd) * (1.0 / HEAD_DIM))), dw

        def epi(j, c):
            rows = pl.ds(pl.multiple_of(j * PRO_ROWS, PRO_ROWS), PRO_ROWS)
            dq, dqw = back(dq_acc[rows, :], q_ref[rows, :], qw_ref[...], HEAD_DIM ** -0.5)
            dk, dkw = back(dk_acc[rows, :], k_ref[rows, :], kw_ref[...], 1.0)
            dq_ref[rows, :] = dq.astype(BF16)
            dk_ref[rows, :] = dk.astype(BF16)
            dv_ref[rows, :] = dv_acc[rows, :].astype(BF16)
            return (c[0] + dqw, c[1] + dkw)

        zrow = jnp.zeros((1, LANE), F32)
        dqw, dkw = lax.fori_loop(0, s // PRO_ROWS, epi, (zrow, zrow))
        dqw_ref[...] = dqw
        dkw_ref[...] = dkw

    o = jax.ShapeDtypeStruct((s, ATT_D), BF16)
    ov = jax.ShapeDtypeStruct((1, ATT_D), F32)
    scr = pltpu.VMEM((s, LANE), F32)
    scb = pltpu.VMEM((s, LANE), BF16)
    return _pcall(
        body, (proj, proj, proj, do, stats, qw, kw), name="att_bwd", grid=(ATT_D // LANE,),
        in_specs=[blk(OFF_Q), blk(OFF_K), blk(OFF_V), oblk, oblk, wspec, wspec],
        out_specs=[oblk, oblk, oblk, wspec, wspec], out_shape=[o, o, o, ov, ov],
        scratch_shapes=[scr, scr, scb, scb, scb, scb, scr, scr, scr, scr, scr, scr, scr, pltpu.VMEM((2, ATT_BLK, 2 * KEYS), F32)],
        sem=("parallel",), comm=comm)


def _att_norm_fwd(o, nw, ycat):
    s = o.shape[0]
    row = pl.BlockSpec((ROW_TILE, ATT_D), lambda i: (i, 0))
    vec = pl.BlockSpec((1, ATT_D), lambda i: (0, 0))

    def body(o_ref, nw_ref, ycat_ref, y_ref):
        o = o_ref[...]
        r = lax.rsqrt(jnp.mean(o * o, axis=-1, keepdims=True) + EPS)
        y_ref[...] = (o * r * nw_ref[...]).astype(BF16)

    return pl.pallas_call(body, name="att_norm_fwd", grid=(s // ROW_TILE,),
                          in_specs=[row, vec, pl.BlockSpec(memory_space=pl.ANY)],
                          out_specs=pl.BlockSpec((ROW_TILE, ATT_D), lambda i: (i, 1)),
                          out_shape=jax.ShapeDtypeStruct(ycat.shape, BF16), input_output_aliases={2: 0},
                          compiler_params=_cparams(("parallel",)))(o, nw, ycat)


def _mixer_split_epilogue(dycat, first, rows, vecs, outs):
    (o_ref, lse_ref), (nw_ref,), (dyssd_ref, do_ref, st_ref, dnw_ref) = rows, vecs, outs

    @pl.when(first)
    def _():
        dnw_ref[...] = jnp.zeros_like(dnw_ref)

    dyssd_ref[...] = dycat[:, :SSD_D_INNER]
    dy = dycat[:, SSD_D_INNER:]
    o = o_ref[...]
    r = lax.rsqrt(jnp.mean(o * o, axis=-1, keepdims=True) + EPS)
    nrm = o * r
    dnw_ref[...] += jnp.sum(dy * nrm, axis=0, keepdims=True)
    dn = dy * nw_ref[...]
    do = r * (dn - nrm * jnp.mean(dn * nrm, axis=-1, keepdims=True))
    do_ref[...] = do
    ones_bd = _head_mean_matrix().astype(BF16)
    prod = do * o
    delta = jnp.concatenate([_head_sum2(prod[:, j * LANE:(j + 1) * LANE], ones_bd) for j in range(ATT_D // LANE)], axis=1)
    lane = lax.broadcasted_iota(jnp.int32, (1, ATT_D), 1)
    st_ref[...] = jnp.where((lane & (HEAD_DIM - 1)) < HALF, lse_ref[...], delta)


def _ada_fwd(c_all, w_ada):
    def body(c_ref, w_ref, o_ref):
        cv = c_ref[...]
        o_ref[...] = _dot((cv * _sigmoid(cv)).astype(BF16), w_ref[...].astype(BF16), NN)

    return pl.pallas_call(body, name="ada_fwd", out_shape=jax.ShapeDtypeStruct((c_all.shape[0], w_ada.shape[1]), F32),
                          compiler_params=_cparams())(c_all, w_ada)


def _adamw_math(g, w, m, v):
    m_new = ADAM_B1 * m + (1.0 - ADAM_B1) * g
    v_new = ADAM_B2 * v + (1.0 - ADAM_B2) * (g * g)
    m_hat = m_new / (1.0 - ADAM_B1 ** ADAM_STEP)
    v_hat = v_new / (1.0 - ADAM_B2 ** ADAM_STEP)
    delta = -ADAM_LR * (m_hat / (jnp.sqrt(v_hat) + ADAM_EPS) + ADAM_WD * w)
    return delta, m_new, v_new


def _ada_bwd_adamw(c_all, dmod_cols, w, m, v):
    rows, cols = w.shape
    tr = 256
    blk = pl.BlockSpec((tr, cols), lambda i: (i, 0))

    def body(c_ref, d_ref, w_ref, m_ref, v_ref, g_ref, dl_ref, mo_ref, vo_ref):
        cv = c_ref[...]
        ca = cv * _sigmoid(cv)
        g = ca[:, 0:1] * d_ref[0:1, :]
        for b in range(1, N_DEV):
            g = g + ca[:, b:b + 1] * d_ref[b:b + 1, :]
        g_ref[...] = g
        dl_ref[...], mo_ref[...], vo_ref[...] = _adamw_math(g, w_ref[...], m_ref[...], v_ref[...])

    o = jax.ShapeDtypeStruct((rows, cols), F32)
    return pl.pallas_call(
        body, name="ada_bwd_adamw", grid=(rows // tr,),
        in_specs=[pl.BlockSpec((tr, N_DEV), lambda i: (i, 0)), pl.BlockSpec((N_DEV, cols), lambda i: (0, 0)), blk, blk, blk],
        out_specs=[blk] * 4, out_shape=[o, o, o, o], compiler_params=_cparams(("parallel",)))(c_all.T, dmod_cols, w, m, v)


def _reduce_adamw(slabs, w, m, v, name):
    rows, cols = w.shape
    n_src = slabs.shape[0]
    if rows % 128 == 0:
        tr, steps = 128, rows // 128
        blk = pl.BlockSpec((tr, cols), lambda i: (i, 0))
        sblk = pl.BlockSpec((n_src, tr, cols), lambda i: (0, i, 0))
    else:
        tc, steps = 256, cols // 256
        blk = pl.BlockSpec((rows, tc), lambda i: (0, i))
        sblk = pl.BlockSpec((n_src, rows, tc), lambda i: (0, 0, i))

    def body(s_ref, w_ref, m_ref, v_ref, g_ref, dl_ref, mo_ref, vo_ref):
        g = s_ref[0].astype(F32)
        for src in range(1, n_src):
            g = g + s_ref[src].astype(F32)
        g_ref[...] = g
        dl_ref[...], mo_ref[...], vo_ref[...] = _adamw_math(g, w_ref[...], m_ref[...], v_ref[...])

    o = jax.ShapeDtypeStruct((rows, cols), F32)
    return pl.pallas_call(
        body, name=name, grid=(steps,), in_specs=[sblk, blk, blk, blk],
        out_specs=[blk] * 4, out_shape=[o, o, o, o], compiler_params=_cparams(("parallel",)))(slabs, w, m, v)


def _small_reduce_adamw(gathered, w, m, v):
    def body(s_ref, w_ref, m_ref, v_ref, g_ref, dl_ref, mo_ref, vo_ref):
        g = s_ref[0]
        for dev in range(1, N_DEV):
            g = g + s_ref[dev]
        g_ref[...] = g
        dl_ref[...], mo_ref[...], vo_ref[...] = _adamw_math(g, w_ref[...], m_ref[...], v_ref[...])

    o = jax.ShapeDtypeStruct(w.shape, F32)
    return pl.pallas_call(body, name="small_reduce_adamw", out_shape=[o, o, o, o], compiler_params=_cparams())(gathered, w, m, v)


def _adamw_small(g, w, m, v, name):
    def body(g_ref, w_ref, m_ref, v_ref, dl_ref, mo_ref, vo_ref):
        dl_ref[...], mo_ref[...], vo_ref[...] = _adamw_math(g_ref[...], w_ref[...], m_ref[...], v_ref[...])

    o = jax.ShapeDtypeStruct(w.shape, F32)
    return pl.pallas_call(body, name=name, out_shape=[o, o, o], compiler_params=_cparams())(g, w, m, v)


class _Exchange:
    def __init__(self, arrs, scatter):
        self.arrs, self.scatter, self.n = list(arrs), scatter, len(arrs)
        hbm = pl.BlockSpec(memory_space=pltpu.HBM)
        self.in_specs = [hbm] * self.n
        self.out_specs = [hbm] * self.n
        self.out_shape = [jax.ShapeDtypeStruct(a.shape if scatter else (N_DEV,) + a.shape, a.dtype) for a in self.arrs]
        self.scratch = [pltpu.SemaphoreType.DMA((self.n * (N_DEV - 1),)), pltpu.SemaphoreType.DMA((self.n * (N_DEV - 1),)),
                        pltpu.SemaphoreType.DMA((self.n,))]

    def _local(self, ins, outs, sems):
        me = 4 * lax.axis_index("x") + 2 * lax.axis_index("y") + lax.axis_index("c")
        return [pltpu.make_async_copy(ins[a].at[me] if self.scatter else ins[a], outs[a].at[me], sems[2].at[a])
                for a in range(self.n)]

    def _remote(self, ins, outs, sems, arriving):
        send_sems, recv_sems, _ = sems
        x, y, c = lax.axis_index("x"), lax.axis_index("y"), lax.axis_index("c")
        me = 4 * x + 2 * y + c
        remote = []
        for a in range(self.n):
            for k in range(1, N_DEV):
                px = 1 - x if k & 4 else x
                py = 1 - y if k & 2 else y
                pc = 1 - c if k & 1 else c
                peer = 4 * px + 2 * py + pc
                sem = a * (N_DEV - 1) + k - 1
                remote.append(pltpu.make_async_remote_copy(
                    src_ref=ins[a].at[peer] if self.scatter else ins[a], dst_ref=outs[a].at[peer if arriving else me],
                    send_sem=send_sems.at[sem], recv_sem=recv_sems.at[sem], device_id=(px, py, pc), device_id_type=MESH_IDS))
        return remote

    def start(self, ins, outs, sems):
        for cp in self._local(ins, outs, sems) + self._remote(ins, outs, sems, arriving=False):
            cp.start()

    def forward(self, ins, outs, sems):
        pass

    def wait(self, ins, outs, sems):
        for send, arrival in zip(self._remote(ins, outs, sems, arriving=False), self._remote(ins, outs, sems, arriving=True)):
            send.wait_send()
            arrival.wait_recv()
        for cp in self._local(ins, outs, sems):
            cp.wait()


N_CHIP = N_DEV // 2


class _SiblingSwap(_Exchange):
    def __init__(self, arrs):
        super().__init__(arrs, scatter=True)
        self.out_shape = [jax.ShapeDtypeStruct((N_CHIP,) + a.shape[2:], a.dtype) for a in self.arrs]
        self.scratch = [pltpu.SemaphoreType.DMA((self.n,)), pltpu.SemaphoreType.DMA((self.n,)), pltpu.SemaphoreType.DMA((1,))]

    def _copies(self, ins, outs, sems):
        x, y, c = lax.axis_index("x"), lax.axis_index("y"), lax.axis_index("c")
        return [pltpu.make_async_remote_copy(src_ref=ins[a].at[:, 1 - c], dst_ref=outs[a], send_sem=sems[0].at[a], recv_sem=sems[1].at[a],
                                             device_id=(x, y, 1 - c), device_id_type=MESH_IDS) for a in range(self.n)]

    def start(self, ins, outs, sems):
        for cp in self._copies(ins, outs, sems):
            cp.start()

    def wait(self, ins, outs, sems):
        for cp in self._copies(ins, outs, sems):
            cp.wait()


class _ChipScatter(_Exchange):
    def __init__(self, arrs):
        super().__init__(arrs, scatter=True)
        n_pairs = self.n * (N_CHIP - 1)
        self.scratch = [pltpu.SemaphoreType.DMA((n_pairs,)), pltpu.SemaphoreType.DMA((n_pairs,)), pltpu.SemaphoreType.DMA((self.n,))]

    def _local(self, ins, outs, sems):
        chip = 2 * lax.axis_index("x") + lax.axis_index("y")
        return [pltpu.make_async_copy(ins[a].at[chip], outs[a].at[chip], sems[2].at[a]) for a in range(self.n)]

    def _remote(self, ins, outs, sems, arriving):
        send_sems, recv_sems, _ = sems
        x, y, c = lax.axis_index("x"), lax.axis_index("y"), lax.axis_index("c")
        chip = 2 * x + y
        remote = []
        for a in range(self.n):
            for k in range(1, N_CHIP):
                px = 1 - x if k & 2 else x
                py = 1 - y if k & 1 else y
                peer = 2 * px + py
                sem = a * (N_CHIP - 1) + k - 1
                remote.append(pltpu.make_async_remote_copy(
                    src_ref=ins[a].at[peer], dst_ref=outs[a].at[peer if arriving else chip], send_sem=send_sems.at[sem],
                    recv_sem=recv_sems.at[sem], device_id=(px, py, c), device_id_type=MESH_IDS))
        return remote


def _chip_sum(mine, theirs):
    n, rows, cols = mine.shape
    blk = pl.BlockSpec((1, rows, 256), lambda q, j: (q, 0, j))

    def body(a_ref, b_ref, o_ref):
        o_ref[...] = (a_ref[...].astype(F32) + b_ref[...].astype(F32)).astype(BF16)

    return pl.pallas_call(body, name="chip_sum", grid=(n, cols // 256), in_specs=[blk, blk], out_specs=blk,
                          out_shape=jax.ShapeDtypeStruct(mine.shape, BF16),
                          compiler_params=_cparams(("parallel", "parallel")))(mine, theirs)


class _Gather2(_Exchange):
    def __init__(self, arrs):
        super().__init__(arrs, scatter=False)

    def _copies(self, ins, outs, sems):
        send_sems, recv_sems, _ = sems
        x, y, c = lax.axis_index("x"), lax.axis_index("y"), lax.axis_index("c")
        sibling = (x, y, 1 - c)
        chips = [(1 - x, y), (x, 1 - y), (1 - x, 1 - y)]
        first, passed, landed = [], [], []
        for a in range(self.n):
            def copy(k, block, to, src=None, a=a):
                slab = outs[a].at[4 * block[0] + 2 * block[1] + block[2]]
                return pltpu.make_async_remote_copy(
                    src_ref=slab if src is None else src, dst_ref=slab, send_sem=send_sems.at[a * (N_DEV - 1) + k],
                    recv_sem=recv_sems.at[a * (N_DEV - 1) + k], device_id=to, device_id_type=MESH_IDS)

            first.append(copy(0, (x, y, c), sibling, src=ins[a]))
            landed.append(copy(0, sibling, sibling))
            for j, chip in enumerate(chips):
                first.append(copy(1 + j, (x, y, c), (*chip, c), src=ins[a]))
                passed.append((copy(1 + j, (*chip, c), sibling), copy(4 + j, (*chip, c), sibling)))
                landed.append(copy(4 + j, (*chip, 1 - c), sibling))
        return first, passed, landed

    def start(self, ins, outs, sems):
        for cp in self._local(ins, outs, sems) + self._copies(ins, outs, sems)[0]:
            cp.start()

    def forward(self, ins, outs, sems):
        for arrival, onward in self._copies(ins, outs, sems)[1]:
            arrival.wait_recv()
            onward.start()

    def wait(self, ins, outs, sems):
        first, passed, landed = self._copies(ins, outs, sems)
        for arrival in landed:
            arrival.wait_recv()
        for cp in first + [onward for _, onward in passed]:
            cp.wait_send()
        for cp in self._local(ins, outs, sems):
            cp.wait()


def _split_comm_refs(refs, n_in, n_out, n_scr, comm):
    nc = comm.n if comm is not None else 0
    ns = 3 if comm is not None else 0
    pos, groups = 0, []
    for cnt in (n_in, nc, n_out, nc, n_scr, ns):
        groups.append(refs[pos:pos + cnt])
        pos += cnt
    assert pos == len(refs), (pos, len(refs))
    return groups


def _pcall(body, args, *, name, grid, in_specs, out_specs, out_shape, scratch_shapes=(), sem=None, comm=None):
    in_specs, out_specs, out_shape, scratch_shapes = list(in_specs), list(out_specs), list(out_shape), list(scratch_shapes)
    n_in, n_out, n_scr = len(in_specs), len(out_specs), len(scratch_shapes)
    if comm is None:
        kernel_body = body
    else:
        def kernel_body(*refs):
            ins, cins, outs, couts, scr, sems = _split_comm_refs(refs, n_in, n_out, n_scr, comm)
            ids = [pl.program_id(a) for a in range(len(grid))]
            first, last = ids[0] == 0, ids[0] == grid[0] - 1
            for a in range(1, len(grid)):
                first, last = first & (ids[a] == 0), last & (ids[a] == grid[a] - 1)

            middle = ids[0] == (2 * grid[0]) // 3
            for a in range(1, len(grid)):
                middle = middle & (ids[a] == 0)

            @pl.when(first)
            def _():
                comm.start(cins, couts, sems)

            @pl.when(middle)
            def _():
                comm.forward(cins, couts, sems)

            body(*ins, *outs, *scr)

            @pl.when(last)
            def _():
                comm.wait(cins, couts, sems)

        in_specs, out_specs, out_shape = in_specs + comm.in_specs, out_specs + comm.out_specs, out_shape + comm.out_shape
        scratch_shapes, args = scratch_shapes + comm.scratch, list(args) + comm.arrs
        sem = ("arbitrary",) * len(grid)
    res = pl.pallas_call(kernel_body, name=name, grid=grid, in_specs=in_specs, out_specs=out_specs, out_shape=out_shape,
                         scratch_shapes=scratch_shapes, compiler_params=_cparams(sem))(*args)
    return res[:n_out], res[n_out:]


def _exchange(arrs, name, scatter=False, ex=None):
    if ex is None:
        ex = _Exchange(arrs, scatter=True) if scatter else _Gather2(arrs)

    def body(*refs):
        _, ins, _, outs, _, sems = _split_comm_refs(refs, 0, 0, 0, ex)
        ex.start(ins, outs, sems)
        ex.forward(ins, outs, sems)
        ex.wait(ins, outs, sems)

    return pl.pallas_call(body, name=name, in_specs=ex.in_specs, out_specs=ex.out_specs, out_shape=ex.out_shape,
                          scratch_shapes=ex.scratch)(*ex.arrs)


def _pad_lanes(v, width=LANE):
    return jnp.pad(v, ((0, 0), (0, width - v.shape[1])))


def _shards_to_cols(g):
    return jnp.transpose(g, (1, 0, 2)).reshape(g.shape[1], N_DEV * g.shape[2])


def _cols_to_shards(w):
    return w.astype(BF16).reshape(w.shape[0], N_DEV, w.shape[1] // N_DEV).transpose(1, 0, 2)


def _local_step(x, tgt, mod, w_in_pt, conv_w, conv_b, dt_bias, a_log, d_skip, ssd_norm_w, q_norm_w, k_norm_w,
                attn_norm_w, w_out_sh, w_ff1_sh, w_ff2_sh, norm1_w, norm2_w, core):
    shift1, scale1, gate1, shift2, scale2, gate2 = [mod[i:i + 1] for i in range(N_MOD)]
    dtb, alog, dsk = _pad_lanes(dt_bias), _pad_lanes(a_log), _pad_lanes(d_skip)
    qw, kw = jnp.tile(q_norm_w, (1, ATT_HEADS)), jnp.tile(k_norm_w, (1, ATT_HEADS))

    h1 = _norm_mod_fwd(x, norm1_w, scale1, shift1, "norm1_fwd")
    proj = _matmul(h1, w_in_pt, tb=True, tm=2048, tn=896, tk=1024, name="in_proj")
    pre, act = _conv_fwd(proj, conv_w, conv_b)
    ypre, ycat_ssd, hall = _ssd_fwd(proj, act, dtb, alog, dsk, ssd_norm_w)
    (o_att, lse), (w_out_g, w_ff1_g, w_ff2_g) = _att_fwd(proj, qw, kw, comm=_Gather2([w_out_sh, w_ff1_sh, w_ff2_sh]))
    w_out = w_out_g.reshape(2 * D_MODEL, D_MODEL)
    w_ff1 = _shards_to_cols(w_ff1_g)
    w_ff2 = w_ff2_g.reshape(D_FF, D_MODEL)
    ycat = _att_norm_fwd(o_att, attn_norm_w, ycat_ssd)
    row32, row16, vec32 = ("row", F32), ("row", BF16), ("vec", F32)
    mix, x1, h2 = _matmul_rows(ycat, w_out, _residual_norm_epilogue, [x], [gate1, norm2_w, scale2, shift2],
                               [row32, row32, row16], tm=512, name="out_proj")
    u, act_ff = _matmul(h2, w_ff1, tm=1024, tn=1024, tk=1024, name="ff1", mode="relu2")
    loss, dout, dff, dgate2 = _matmul_rows(act_ff, w_ff2, _loss_epilogue, [x1, tgt], [gate2],
                                           [("one", F32), row32, row16, vec32], tm=512, name="ff2")

    du = _matmul(dff, w_ff2, tb=True, tm=1024, tn=1024, tk=1024, out_dtype=BF16, name="ff2_dx", mode="drelu2", u=u)
    g_ff2 = _matmul(act_ff, dff, ta=True, tm=512, tn=1024, tk=4096, out_dtype=BF16, name="ff2_dw")
    dx1, dshift2, dscale2, g_norm2, dmix, dgate1 = _matmul_rows(
        du, w_ff1, _norm_bwd_epilogue, [x1, dout, mix], [norm2_w, scale2, gate1],
        [row32, vec32, vec32, vec32, row16, vec32], tb=True, tm=512, name="ff1_dx")
    g_ff1 = _matmul(h2, du, ta=True, tm=1024, tn=D_FF // N_DEV, tk=4096, out_dtype=BF16, name="ff1_dw", shard_out=True)

    dy_ssd, do, stats, g_attn_norm = _matmul_rows(
        dmix, w_out, _mixer_split_epilogue, [o_att, lse], [attn_norm_w],
        [("row", F32, SSD_D_INNER), ("row", F32, ATT_D), ("row", F32, ATT_D), ("vec", F32, ATT_D)], tb=True, tm=512, name="out_proj_dx")
    g_out = _matmul(ycat, dmix, ta=True, tm=512, tn=1024, tk=4096, out_dtype=BF16, name="out_proj_dw")
    ff_slabs = [g_ff1, g_ff2.reshape(N_DEV, D_FF // N_DEV, D_MODEL)]
    (dq, dk, dv, dqw, dkw), (s_ff1, s_ff2) = _att_bwd(proj, do, stats, qw, kw, comm=_Exchange(ff_slabs, scatter=True))
    out_slabs = [g_out.astype(BF16).reshape(N_DEV, 2 * D_MODEL // N_DEV, D_MODEL)]
    (dz, dact, ddtr, da, g_dsk, g_dtb, g_ssd_norm), (s_out,) = _ssd_bwd(
        dy_ssd, ypre, proj, act, hall, dtb, alog, dsk, ssd_norm_w, comm=_Exchange(out_slabs, scatter=True))
    dxbc, g_conv_w, g_conv_b = _conv_bwd(dact, pre, proj, conv_w)
    dproj = [(dz, OFF_Z), (dxbc, OFF_XBC), (ddtr, OFF_DT), (dq, OFF_Q), (dk, OFF_K), (dv, OFF_V)]
    g_rows = [_matmul(piece, h1, ta=True, tm=min(piece.shape[1], 1024), tn=1024, tk=4096, out_dtype=BF16,
                      name=f"in_proj_dw_{off}") for piece, off in dproj]
    g_rows[2] = g_rows[2][:SSD_HEADS]
    in_slabs = jnp.concatenate(g_rows, axis=0).reshape(N_CHIP, 2, IN_W // N_DEV, D_MODEL)
    (sibling_slabs,) = _exchange(None, "swap_w_in_grads", ex=_SiblingSwap([in_slabs]))
    chip_slabs = _chip_sum(lax.dynamic_index_in_dim(in_slabs, core, axis=1, keepdims=False), sibling_slabs)
    (grad_x, dshift1, dscale1, g_norm1), (s_in,) = _matmul_rows(
        dproj, w_in_pt, _norm_bwd_epilogue, [x, dx1], [norm1_w, scale1], [row32, vec32, vec32, vec32],
        tm=256, name="in_proj_dx", comm=_ChipScatter([chip_slabs]))

    dmod = jnp.concatenate([dshift1, dscale1, dgate1, dshift2, dscale2, dgate2], axis=0)
    g_alog = da[:, :SSD_HEADS] * (-jnp.exp(a_log))
    g_qw = dqw.reshape(ATT_HEADS, HEAD_DIM).sum(axis=0, keepdims=True)
    g_kw = dkw.reshape(ATT_HEADS, HEAD_DIM).sum(axis=0, keepdims=True)
    return dict(loss=loss, grad_x=grad_x, dmod=dmod, norm1_w=g_norm1, norm2_w=g_norm2, w_in=s_in, conv_w=g_conv_w,
                conv_b=g_conv_b, dt_bias=g_dtb[:, :SSD_HEADS], a_log=g_alog, d_skip=g_dsk[:, :SSD_HEADS],
                ssd_norm_w=g_ssd_norm, q_norm_w=g_qw, k_norm_w=g_kw, attn_norm_w=g_attn_norm, w_out=s_out,
                w_ff1=s_ff1, w_ff2=s_ff2)


def _pack_w_in_rows(wt_full):
    cut = OFF_DT + SSD_HEADS
    pad = jnp.zeros((LANE - SSD_HEADS, wt_full.shape[1]), wt_full.dtype)
    return jnp.concatenate([wt_full[:cut], pad, wt_full[cut:]], axis=0)


MISC_FIELDS = (("dt_bias", SSD_HEADS), ("a_log", SSD_HEADS), ("d_skip", SSD_HEADS), ("q_norm_w", HEAD_DIM), ("k_norm_w", HEAD_DIM))
SMALL_LAYOUT = (("b_ada", 6), ("norm1_w", 1), ("norm2_w", 1), ("conv_w", 8), ("conv_b", 2), ("ssd_norm_w", 1),
                ("attn_norm_w", 1), ("misc", 1))


def _pack_small(vals):
    rows = []
    for name, nrow in SMALL_LAYOUT:
        if name == "misc":
            misc = jnp.concatenate([vals[f].reshape(1, n) for f, n in MISC_FIELDS], axis=1)
            rows.append(_pad_lanes(misc, D_MODEL))
        elif name in vals:
            rows.append(vals[name].reshape(nrow, D_MODEL))
        else:
            rows.append(jnp.zeros((nrow, D_MODEL), F32))
    used = sum(n for _, n in SMALL_LAYOUT)
    rows.append(jnp.zeros((SMALL_ROWS - used, D_MODEL), F32))
    return jnp.concatenate(rows, axis=0)


def _unpack_small(packed):
    out, r = {}, 0
    for name, nrow in SMALL_LAYOUT:
        blk = packed[r:r + nrow]
        r += nrow
        if name == "misc":
            c0 = 0
            for f, n in MISC_FIELDS:
                out[f] = blk[:, c0:c0 + n]
                c0 += n
        elif name == "b_ada":
            out[name] = blk.reshape(1, N_MOD * D_MODEL)
        elif name == "conv_w":
            out[name] = blk.reshape(CONV_K, CONV_CH)
        elif name == "conv_b":
            out[name] = blk.reshape(1, CONV_CH)
        else:
            out[name] = blk
    return out


WEIGHT_NAMES = ("norm1_w", "norm2_w", "w_ada", "b_ada", "w_in", "conv_w", "conv_b", "dt_bias", "a_log", "d_skip",
                "ssd_norm_w", "q_norm_w", "k_norm_w", "attn_norm_w", "w_out", "w_ff1", "w_ff2")
SMALL_NAMES = ("norm1_w", "norm2_w", "b_ada", "conv_b", "dt_bias", "a_log", "d_skip", "ssd_norm_w", "q_norm_w",
               "k_norm_w", "attn_norm_w")


def kernel(x, c, norm1_w, norm2_w, w_ada, b_ada, w_in, conv_w, conv_b, dt_bias, a_log, d_skip, ssd_norm_w, q_norm_w, k_norm_w, attn_norm_w, w_out, w_ff1, w_ff2, loss_target, m_norm1_w, m_norm2_w, m_w_ada, m_b_ada, m_w_in, m_conv_w, m_conv_b, m_dt_bias, m_a_log, m_d_skip, m_ssd_norm_w, m_q_norm_w, m_k_norm_w, m_attn_norm_w, m_w_out, m_w_ff1, m_w_ff2, v_norm1_w, v_norm2_w, v_w_ada, v_b_ada, v_w_in, v_conv_w, v_conv_b, v_dt_bias, v_a_log, v_d_skip, v_ssd_norm_w, v_q_norm_w, v_k_norm_w, v_attn_norm_w, v_w_out, v_w_ff1, v_w_ff2):
    args = dict(locals())
    w = {n: args[n] for n in WEIGHT_NAMES}
    m = {n: args["m_" + n] for n in WEIGHT_NAMES}
    v = {n: args["v_" + n] for n in WEIGHT_NAMES}
    me = 4 * lax.axis_index("x") + 2 * lax.axis_index("y") + lax.axis_index("c")

    c_rows = jnp.pad(c, ((0, 7), (0, 0)))
    w_in_t, m_in_t, v_in_t = [jnp.transpose(t["w_in"][0]) for t in (w, m, v)]
    c_g, conv_g, w_in_g = _exchange([c_rows, w["conv_w"][0], w_in_t.astype(BF16)], "gather_w_in", scatter=False)
    c_all = c_g[:, 0, :]
    conv_full = _shards_to_cols(conv_g)
    w_in_pt = _pack_w_in_rows(w_in_g.reshape(IN_W, D_MODEL))

    mod_part = _ada_fwd(c_all, w["w_ada"][0])
    (mod_g,) = _exchange([mod_part], "gather_mod", scatter=False)
    mod_mine = lax.dynamic_index_in_dim(mod_g, me, axis=1, keepdims=False).reshape(1, N_MOD * D_MODEL) + w["b_ada"]
    mod = mod_mine.reshape(N_MOD, D_MODEL)

    res = _local_step(x[0], loss_target[0], mod, w_in_pt, conv_full, w["conv_b"], w["dt_bias"], w["a_log"], w["d_skip"],
                      w["ssd_norm_w"], w["q_norm_w"], w["k_norm_w"], w["attn_norm_w"], w["w_out"][0].astype(BF16),
                      w["w_ff1"][0].astype(BF16), w["w_ff2"][0].astype(BF16), w["norm1_w"], w["norm2_w"], lax.axis_index("c"))

    small_vals = {n: res[n] for n in SMALL_NAMES if n != "b_ada"}
    small_vals["b_ada"] = res["dmod"]
    small_vals["conv_w"] = res["conv_w"]
    (small_g,) = _exchange([_pack_small(small_vals)], "gather_small", scatter=False)

    grads, delta, new_m, new_v = {}, {}, {}, {}
    for name in ("w_out", "w_ff1", "w_ff2"):
        outs = _reduce_adamw(res[name], w[name][0], m[name][0], v[name][0], "adamw_" + name)
        grads[name], delta[name], new_m[name], new_v[name] = [o[None] for o in outs]
    outs = _reduce_adamw(res["w_in"], w_in_t, m_in_t, v_in_t, "adamw_w_in")
    grads["w_in"], delta["w_in"], new_m["w_in"], new_v["w_in"] = [jnp.transpose(o)[None] for o in outs]

    sm = _small_reduce_adamw(small_g, _pack_small({n: w[n] for n in SMALL_NAMES}), _pack_small({n: m[n] for n in SMALL_NAMES}),
                             _pack_small({n: v[n] for n in SMALL_NAMES}))
    sm = [_unpack_small(p) for p in sm]
    for n in SMALL_NAMES:
        grads[n], delta[n], new_m[n], new_v[n] = [p[n] for p in sm]
    shard_w = CONV_CH // N_DEV
    g_conv = lax.dynamic_slice_in_dim(sm[0]["conv_w"], me * shard_w, shard_w, axis=1)
    cw = _adamw_small(g_conv, w["conv_w"][0], m["conv_w"][0], v["conv_w"][0], "adamw_conv_w")
    grads["conv_w"] = g_conv[None]
    delta["conv_w"], new_m["conv_w"], new_v["conv_w"] = [o[None] for o in cw]

    ada_w = w_ada.shape[2]
    dmod_all = small_g[:, :N_MOD, :].reshape(N_DEV, N_MOD * D_MODEL)
    dmod_cols = lax.dynamic_slice_in_dim(dmod_all, me * ada_w, ada_w, axis=1)
    outs = _ada_bwd_adamw(c_all, dmod_cols, w["w_ada"][0], m["w_ada"][0], v["w_ada"][0])
    grads["w_ada"], delta["w_ada"], new_m["w_ada"], new_v["w_ada"] = [o[None] for o in outs]

    loss = lax.psum(res["loss"][0, 0], ("x", "y", "c"))
    return (loss, res["grad_x"][None], *[grads[n] for n in WEIGHT_NAMES], *[delta[n] for n in WEIGHT_NAMES],
            *[new_m[n] for n in WEIGHT_NAMES], *[new_v[n] for n in WEIGHT_NAMES])
```

```python
import functools

import jax
import jax.numpy as jnp
from jax import lax
from jax.experimental import pallas as pl
from jax.experimental.pallas import tpu as pltpu

F32 = jnp.float32
BF16 = jnp.bfloat16
HIGHEST = lax.Precision.HIGHEST
MESH_IDS = pl.DeviceIdType.MESH

N_DEV = 8
D_MODEL = 1024
HEAD_DIM = 64
SSD_HEADS = 16
SSD_GROUPS = 4
HEADS_PER_GROUP = SSD_HEADS // SSD_GROUPS
SSD_STATE = 128
SSD_CHUNK = 128
SSD_D_INNER = SSD_HEADS * HEAD_DIM
GROUP_WIDTH = SSD_D_INNER // SSD_GROUPS
CONV_K = 4
CONV_CH = SSD_D_INNER + 2 * SSD_GROUPS * SSD_STATE
ATT_HEADS = 16
ATT_D = ATT_HEADS * HEAD_DIM
ATT_BLK = 128
DILATIONS = (1, 4, 16)
D_FF = 4 * D_MODEL
N_MOD = 6
EPS = 1e-6
IN_W = SSD_D_INNER + CONV_CH + SSD_HEADS + 3 * ATT_D
LANE = 128
OFF_Z, OFF_XBC, OFF_DT = 0, SSD_D_INNER, SSD_D_INNER + CONV_CH
OFF_Q = OFF_DT + LANE
OFF_K, OFF_V = OFF_Q + ATT_D, OFF_Q + 2 * ATT_D
IN_WP = OFF_V + ATT_D

ADAM_LR, ADAM_B1, ADAM_B2, ADAM_EPS, ADAM_WD, ADAM_STEP = 0.001, 0.9, 0.999, 1e-08, 0.01, 10
VMEM_LIMIT = 56 * 1024 * 1024
ROW_TILE = 512
SMALL_ROWS = 24


def _cparams(sem=None):
    return pltpu.CompilerParams(dimension_semantics=sem, vmem_limit_bytes=VMEM_LIMIT)


def _sigmoid(v):
    return 1.0 / (1.0 + jnp.exp(-v))


def _softplus(v):
    y = jnp.exp(-jnp.abs(v))
    small = y * (1.0 - y * (0.5 - y * (1.0 / 3.0)))
    return jnp.maximum(v, 0.0) + jnp.where(y < 0.01, small, jnp.log(1.0 + y))


def _dot(a, b, dims, precision=None):
    return lax.dot_general(a, b, (dims, ((), ())), preferred_element_type=F32, precision=precision)


NN = ((1,), (0,))
NT = ((1,), (1,))
TN = ((0,), (0,))


def _matmul(a, b, *, ta=False, tb=False, tm, tn, tk, out_dtype=F32, name, mode=None, u=None, comm=None, shard_out=False):
    m, k = (a.shape[1], a.shape[0]) if ta else a.shape
    n = b.shape[0] if tb else b.shape[1]
    assert m % tm == 0 and n % tn == 0 and k % tk == 0, (name, m, n, k)
    nk = k // tk
    a_spec = pl.BlockSpec((tk, tm), lambda i, j, kk: (kk, i)) if ta else pl.BlockSpec((tm, tk), lambda i, j, kk: (i, kk))
    b_spec = pl.BlockSpec((tn, tk), lambda i, j, kk: (j, kk)) if tb else pl.BlockSpec((tk, tn), lambda i, j, kk: (kk, j))
    o_spec = pl.BlockSpec((tm, tn), lambda i, j, kk: (i, j))
    dims = ((0,) if ta else (1,), (1,) if tb else (0,))
    n_out = 2 if mode == "relu2" else 1

    def body(*refs):
        if mode == "drelu2":
            a_ref, b_ref, u_ref = refs[:3]
            rest = refs[3:]
        else:
            a_ref, b_ref = refs[:2]
            u_ref = None
            rest = refs[2:]
        outs = rest[:n_out]
        part = _dot(a_ref[...], b_ref[...], dims)

        def finish(r):
            if mode == "relu2":
                outs[0][...] = r.astype(BF16)
                rr = jnp.maximum(r, 0.0)
                outs[1][...] = (rr * rr).astype(BF16)
            elif mode == "drelu2":
                outs[0][...] = (r * (2.0 * jnp.maximum(u_ref[...].astype(F32), 0.0))).astype(out_dtype)
            else:
                outs[0][...] = r.astype(out_dtype)

        if nk == 1:
            finish(part)
        else:
            acc = rest[n_out]
            kk = pl.program_id(2)

            @pl.when(kk == 0)
            def _():
                acc[...] = part

            @pl.when(kk > 0)
            def _():
                acc[...] += part

            @pl.when(kk == nk - 1)
            def _():
                finish(acc[...])

    in_specs = [a_spec, b_spec]
    args = [a, b]
    if mode == "drelu2":
        in_specs.append(o_spec)
        args.append(u)
    if mode == "relu2":
        out_shape = [jax.ShapeDtypeStruct((m, n), BF16), jax.ShapeDtypeStruct((m, n), BF16)]
    elif shard_out:
        out_shape = [jax.ShapeDtypeStruct((n // tn, m, tn), out_dtype)]
        o_spec = pl.BlockSpec((None, tm, tn), lambda i, j, kk: (j, i, 0))
    else:
        out_shape = [jax.ShapeDtypeStruct((m, n), out_dtype)]
    outs, comm_outs = _pcall(
        body, args, name=name, grid=(m // tm, n // tn, nk), in_specs=in_specs, out_specs=[o_spec] * n_out,
        out_shape=out_shape, scratch_shapes=[pltpu.VMEM((tm, tn), F32)] if nk > 1 else [],
        sem=("parallel", "parallel", "arbitrary"), comm=comm)
    res = tuple(outs) if mode == "relu2" else outs[0]
    return res if comm is None else (res, comm_outs)


def _rms_mod(xv, nw, scale, shift):
    r = lax.rsqrt(jnp.mean(xv * xv, axis=-1, keepdims=True) + EPS)
    return ((xv * r) * nw * (1.0 + scale) + shift).astype(BF16)


def _norm_mod_fwd(x, nw, scale, shift, name):
    s, d = x.shape
    row = pl.BlockSpec((ROW_TILE, d), lambda i: (i, 0))
    vec = pl.BlockSpec((1, d), lambda i: (0, 0))

    def body(x_ref, nw_ref, sc_ref, sh_ref, h_ref):
        h_ref[...] = _rms_mod(x_ref[...], nw_ref[...], sc_ref[...], sh_ref[...])

    return pl.pallas_call(body, name=name, grid=(s // ROW_TILE,), in_specs=[row, vec, vec, vec], out_specs=row,
                          out_shape=jax.ShapeDtypeStruct((s, d), BF16), compiler_params=_cparams(("parallel",)))(x, nw, scale, shift)


def _matmul_rows(a, b, epilogue, row_in, vec_in, outs, *, tb=False, tm, name, comm=None):
    pieces = a if isinstance(a, list) else [(a, 0)]
    assert not (tb and len(pieces) > 1)
    m = pieces[0][0].shape[0]
    n = b.shape[0] if tb else b.shape[1]
    assert m % tm == 0, (name, m, tm)
    dims = ((1,), (1,) if tb else (0,))
    n_a, n_row, n_vec = len(pieces), len(row_in), len(vec_in)

    def body(*refs):
        a_refs, b_ref, rest = refs[:n_a], refs[n_a], refs[n_a + 1:]
        if n_a == 1:
            c = _dot(a_refs[0][...], b_ref[...], dims)
        else:
            c = None
            for a_ref, (piece, off) in zip(a_refs, pieces):
                part = _dot(a_ref[...], b_ref[off:off + piece.shape[1], :], dims)
                c = part if c is None else c + part
        epilogue(c, pl.program_id(0) == 0, rest[:n_row], rest[n_row:n_row + n_vec], rest[n_row + n_vec:])

    def spec(kind, width):
        block = {"row": (tm, width), "vec": (1, width), "one": (1, 1)}[kind]
        return pl.BlockSpec(block, (lambda i: (i, 0)) if kind == "row" else (lambda i: (0, 0)))

    def shape(kind, width):
        return {"row": (m, width), "vec": (1, width), "one": (1, 1)}[kind]

    outs = [(o[0], o[1], o[2] if len(o) > 2 else n) for o in outs]
    res, comm_outs = _pcall(
        body, [*[p for p, _ in pieces], b, *row_in, *vec_in], name=name, grid=(m // tm,),
        in_specs=[spec("row", p.shape[1]) for p, _ in pieces] + [pl.BlockSpec(b.shape, lambda i: (0, 0))]
        + [spec("row", r.shape[1]) for r in row_in] + [spec("vec", v.shape[1]) for v in vec_in],
        out_specs=[spec(kind, width) for kind, _, width in outs],
        out_shape=[jax.ShapeDtypeStruct(shape(kind, width), dt) for kind, dt, width in outs],
        sem=("arbitrary",), comm=comm)
    return res if comm is None else (res, comm_outs)


def _residual_norm_epilogue(mix, first, rows, vecs, outs):
    (x_ref,), (gate_ref, nw_ref, sc_ref, sh_ref), (mix_ref, x1_ref, h_ref) = rows, vecs, outs
    xv = x_ref[...] + gate_ref[...] * mix
    mix_ref[...] = mix
    x1_ref[...] = xv
    h_ref[...] = _rms_mod(xv, nw_ref[...], sc_ref[...], sh_ref[...])


def _loss_epilogue(ff, first, rows, vecs, outs):
    (x1_ref, t_ref), (g_ref,), (loss_ref, dout_ref, dff_ref, dg_ref) = rows, vecs, outs
    d = ff.shape[1]

    @pl.when(first)
    def _():
        loss_ref[...] = jnp.zeros_like(loss_ref)
        dg_ref[...] = jnp.zeros_like(dg_ref)

    err = x1_ref[...] + g_ref[...] * ff - t_ref[...]
    loss_ref[...] += (0.5 / d) * jnp.sum(err * err).reshape(1, 1)
    dout = err * (1.0 / d)
    dout_ref[...] = dout
    dff_ref[...] = (g_ref[...] * dout).astype(BF16)
    dg_ref[...] += jnp.sum(dout * ff, axis=0, keepdims=True)


def _norm_bwd_epilogue(dh, first, rows, vecs, outs):
    with_gate = len(vecs) == 3
    x_ref, dres_ref = rows[:2]
    nw_ref, sc_ref = vecs[:2]
    dx_ref, dsh_ref, dsc_ref, dnw_ref = outs[:4]

    @pl.when(first)
    def _():
        for ref in outs[1:4] + outs[5:]:
            ref[...] = jnp.zeros_like(ref)

    xv = x_ref[...]
    r = lax.rsqrt(jnp.mean(xv * xv, axis=-1, keepdims=True) + EPS)
    nrm = xv * r
    one_sc = 1.0 + sc_ref[...]
    dhn = dh * nrm
    dsh_ref[...] += jnp.sum(dh, axis=0, keepdims=True)
    dsc_ref[...] += jnp.sum(dhn, axis=0, keepdims=True) * nw_ref[...]
    dnw_ref[...] += jnp.sum(dhn, axis=0, keepdims=True) * one_sc
    dn = dh * (nw_ref[...] * one_sc)
    dx = dres_ref[...] + r * (dn - nrm * jnp.mean(dn * nrm, axis=-1, keepdims=True))
    dx_ref[...] = dx
    if with_gate:
        outs[4][...] = (vecs[2][...] * dx).astype(BF16)
        outs[5][...] += jnp.sum(dx * rows[2][...], axis=0, keepdims=True)


CONV_COLS = 256
CONV_FWD_ROWS = 2048
CONV_BWD_ROWS = 1024
CONV_SUB_ROWS = 128
HALO = 8


def _shift_down(cur, halo, k):
    if k == 0:
        return cur
    rolled = pltpu.roll(cur, k, axis=0)
    top = jnp.where(lax.broadcasted_iota(jnp.int32, halo.shape, 0) < k, pltpu.roll(halo, k, axis=0), rolled[:HALO])
    return jnp.concatenate([top, rolled[HALO:]], axis=0)


def _shift_up(cur, halo, k):
    if k == 0:
        return cur
    t = cur.shape[0]
    rolled = pltpu.roll(cur, t - k, axis=0)
    bot = jnp.where(lax.broadcasted_iota(jnp.int32, halo.shape, 0) >= HALO - k, pltpu.roll(halo, HALO - k, axis=0),
                    rolled[t - HALO:])
    return jnp.concatenate([rolled[:t - HALO], bot], axis=0)


def _conv_fwd(proj, conv_w, conv_b):
    s = proj.shape[0]
    nr = s // CONV_FWD_ROWS
    cb0 = OFF_XBC // CONV_COLS
    hb = CONV_FWD_ROWS // HALO
    cur = pl.BlockSpec((CONV_FWD_ROWS, CONV_COLS), lambda j, r: (r, cb0 + j))
    prev = pl.BlockSpec((HALO, CONV_COLS), lambda j, r: (jnp.maximum(r * hb - 1, 0), cb0 + j))
    out = pl.BlockSpec((CONV_FWD_ROWS, CONV_COLS), lambda j, r: (r, j))

    def body(u_ref, up_ref, w_ref, b_ref, pre_ref, act_ref):
        r = pl.program_id(1)
        u = u_ref[...]
        halo = jnp.where(r > 0, up_ref[...], 0.0)
        acc = b_ref[...] + w_ref[CONV_K - 1:CONV_K, :] * u
        for k in range(1, CONV_K):
            acc = acc + w_ref[CONV_K - 1 - k:CONV_K - k, :] * _shift_down(u, halo, k)
        pre_ref[...] = acc
        act_ref[...] = acc * _sigmoid(acc)

    return pl.pallas_call(
        body, name="conv_fwd", grid=(CONV_CH // CONV_COLS, nr),
        in_specs=[cur, prev, pl.BlockSpec((CONV_K, CONV_COLS), lambda j, r: (0, j)),
                  pl.BlockSpec((1, CONV_COLS), lambda j, r: (0, j))],
        out_specs=[out, out],
        out_shape=[jax.ShapeDtypeStruct((s, CONV_CH), F32), jax.ShapeDtypeStruct((s, CONV_CH), F32)],
        compiler_params=_cparams(("parallel", "arbitrary")))(proj, proj, conv_w, conv_b)


def _conv_bwd(dact, pre, proj, conv_w):
    s = proj.shape[0]
    nr = s // CONV_BWD_ROWS
    cb0 = OFF_XBC // CONV_COLS
    hb = CONV_BWD_ROWS // HALO
    last_halo = s // HALO - 1
    n_sub = CONV_BWD_ROWS // CONV_SUB_ROWS
    cur = pl.BlockSpec((CONV_BWD_ROWS, CONV_COLS), lambda j, r: (r, j))
    nxt = pl.BlockSpec((HALO, CONV_COLS), lambda j, r: (jnp.minimum((r + 1) * hb, last_halo), j))
    ucur = pl.BlockSpec((CONV_BWD_ROWS, CONV_COLS), lambda j, r: (r, cb0 + j))
    wspec = pl.BlockSpec((CONV_K, CONV_COLS), lambda j, r: (0, j))
    bspec = pl.BlockSpec((1, CONV_COLS), lambda j, r: (0, j))

    def dsilu(p):
        sg = _sigmoid(p)
        return sg * (1.0 + p * (1.0 - sg))

    def body(da_ref, dan_ref, pre_ref, pren_ref, u_ref, w_ref, du_ref, dw_ref, db_ref):
        r = pl.program_id(1)

        @pl.when(r == 0)
        def _():
            dw_ref[...] = jnp.zeros_like(dw_ref)
            db_ref[...] = jnp.zeros_like(db_ref)

        dws = [jnp.zeros((1, CONV_COLS), F32) for _ in range(CONV_K)]
        db = jnp.zeros((1, CONV_COLS), F32)
        for c in range(n_sub):
            rows = slice(c * CONV_SUB_ROWS, (c + 1) * CONV_SUB_ROWS)
            ahead = slice((c + 1) * CONV_SUB_ROWS, (c + 1) * CONV_SUB_ROWS + HALO)
            dpre = da_ref[rows, :] * dsilu(pre_ref[rows, :])
            if c < n_sub - 1:
                dnext = da_ref[ahead, :] * dsilu(pre_ref[ahead, :])
            else:
                dnext = jnp.where(r < nr - 1, dan_ref[...] * dsilu(pren_ref[...]), 0.0)
            u = u_ref[rows, :]
            du = w_ref[CONV_K - 1:CONV_K, :] * dpre
            dws[0] = dws[0] + jnp.sum(dpre * u, axis=0, keepdims=True)
            for k in range(1, CONV_K):
                ahead_k = _shift_up(dpre, dnext, k)
                du = du + w_ref[CONV_K - 1 - k:CONV_K - k, :] * ahead_k
                dws[k] = dws[k] + jnp.sum(ahead_k * u, axis=0, keepdims=True)
            du_ref[rows, :] = du.astype(BF16)
            db = db + jnp.sum(dpre, axis=0, keepdims=True)
        dw_ref[...] += jnp.concatenate(dws[::-1], axis=0)
        db_ref[...] += db

    return pl.pallas_call(
        body, name="conv_bwd", grid=(CONV_CH // CONV_COLS, nr),
        in_specs=[cur, nxt, cur, nxt, ucur, wspec],
        out_specs=[cur, wspec, bspec],
        out_shape=[jax.ShapeDtypeStruct((s, CONV_CH), BF16), jax.ShapeDtypeStruct((CONV_K, CONV_CH), F32),
                   jax.ShapeDtypeStruct((1, CONV_CH), F32)],
        compiler_params=_cparams(("parallel", "arbitrary")))(dact, dact, pre, pre, proj, conv_w)


def _ssd_common(dtr, dtb, alog):
    lane = lax.broadcasted_iota(jnp.int32, (1, LANE), 1)
    head_lane = lane < SSD_HEADS
    dt = jnp.where(head_lane, _softplus(dtr + dtb), 0.0)
    a = jnp.where(head_lane, -jnp.exp(alog), 0.0)
    row = lax.broadcasted_iota(jnp.int32, (SSD_CHUNK, SSD_CHUNK), 0)
    col = lax.broadcasted_iota(jnp.int32, (SSD_CHUNK, SSD_CHUNK), 1)
    tril = row >= col
    cs = _dot(tril.astype(F32), dt * a, NN, precision=HIGHEST)
    return dt, a, cs, cs.T, tril, lane


def _split_bf16(v, passes):
    terms, rest = [], v
    for _ in range(passes):
        t = rest.astype(BF16)
        terms.append(t)
        rest = rest - t.astype(F32)
    return terms


def _dot_split(v, m, dims, passes):
    terms = _split_bf16(v, passes)
    if passes == 1:
        return _dot(terms[0], m, dims)
    return _dot(jnp.concatenate(terms, axis=1), jnp.concatenate([m] * passes, axis=0 if dims == NN else 1), dims)


def _ssd_constants():
    heads = jnp.arange(LANE)[:, None]
    exp_mat = (heads == (jnp.arange(SSD_D_INNER)[None, :] // HEAD_DIM)).astype(BF16)
    ind4 = ((jnp.arange(SSD_HEADS * SSD_CHUNK)[:, None] // SSD_CHUNK) == jnp.arange(LANE)[None, :]).astype(BF16)
    return exp_mat, ind4


def _expand_heads(v):
    return jnp.repeat(v[:, :SSD_HEADS], HEAD_DIM, axis=1)


def _ssd_prep(dtr, dtb, alog, exp_mat):
    dt, a, cs, cst, tril, lane = _ssd_common(dtr, dtb, alog)
    return dt, a, cs, cst, tril, lane, _dot_split(dt, exp_mat, NN, 2), _dot_split(cs, exp_mat, NN, 3)


def _chunk_decay_rows(cs, g):
    parts = []
    for e in range(HEADS_PER_GROUP):
        h = g * HEADS_PER_GROUP + e
        parts.append(jnp.broadcast_to(jnp.exp(cs[SSD_CHUNK - 1:SSD_CHUNK, h:h + 1]), (HEAD_DIM, SSD_STATE)))
    return jnp.concatenate(parts, axis=0)


def _ssd_fwd(proj, act, dtb, alog, dsk, nw):
    s = proj.shape[0]
    nc = s // SSD_CHUNK
    bc_w = SSD_GROUPS * SSD_STATE
    exp_mat, _ = _ssd_constants()

    def body(z_ref, dtr_ref, xs_ref, b_ref, c_ref, dtb_ref, alog_ref, dskx_ref, nw_ref, exp_ref,
             ypre_ref, yssd_ref, hall_ref, h_scr):
        @pl.when(pl.program_id(0) == 0)
        def _():
            h_scr[...] = jnp.zeros_like(h_scr)

        dt, a, cs, cst, tril, lane, dtx, csx = _ssd_prep(dtr_ref[...], dtb_ref[...], alog_ref[...], exp_ref[...])
        cs_last_x = csx[SSD_CHUNK - 1:SSD_CHUNK, :]
        xs = xs_ref[...]
        xdt = xs * dtx
        xdtb = xdt.astype(BF16)
        xdec = (xdt * jnp.exp(cs_last_x - csx)).astype(BF16)
        ecsx = jnp.exp(csx)
        head_of_lane = lax.broadcasted_iota(jnp.int32, (1, GROUP_WIDTH), 1) // HEAD_DIM
        for g in range(SSD_GROUPS):
            gs = slice(g * GROUP_WIDTH, (g + 1) * GROUP_WIDTH)
            bg = b_ref[:, g * SSD_STATE:(g + 1) * SSD_STATE].astype(BF16)
            cg = c_ref[:, g * SSD_STATE:(g + 1) * SSD_STATE].astype(BF16)
            cb = _dot(cg, bg, NT)
            hprev = h_scr[gs, :]
            hall_ref[0, gs, :] = hprev
            gms, rhs = [], []
            xg = xdtb[:, gs]
            for e in range(HEADS_PER_GROUP):
                h = g * HEADS_PER_GROUP + e
                lm = jnp.exp(jnp.where(tril, cs[:, h:h + 1] - cst[h:h + 1, :], -1e30))
                gms.append((cb * lm).astype(BF16))
                rhs.append(jnp.where(head_of_lane == e, xg, jnp.zeros_like(xg)))
            y = _dot(jnp.concatenate(gms, axis=1), jnp.concatenate(rhs, axis=0), NN)
            y = y + ecsx[:, gs] * _dot(cg, hprev.astype(BF16), NT)
            y = y + dskx_ref[:, gs] * xs[:, gs]
            h_scr[gs, :] = hprev * _chunk_decay_rows(cs, g) + _dot(xdec[:, gs], bg, TN)
            ypre_ref[:, gs] = y
            z = z_ref[:, gs]
            yg = y * (z * _sigmoid(z))
            r = lax.rsqrt(jnp.mean(yg * yg, axis=-1, keepdims=True) + EPS)
            yssd_ref[:, gs] = (yg * r * nw_ref[:, gs]).astype(BF16)

    row_d = lambda cb: pl.BlockSpec((SSD_CHUNK, SSD_D_INNER), lambda c: (c, cb))
    small = pl.BlockSpec((1, LANE), lambda c: (0, 0))
    wide = pl.BlockSpec((1, SSD_D_INNER), lambda c: (0, 0))
    return pl.pallas_call(
        body, name="ssd_fwd", grid=(nc,),
        in_specs=[row_d(OFF_Z // SSD_D_INNER),
                  pl.BlockSpec((SSD_CHUNK, LANE), lambda c: (c, OFF_DT // LANE)),
                  row_d(0),
                  pl.BlockSpec((SSD_CHUNK, bc_w), lambda c: (c, SSD_D_INNER // bc_w)),
                  pl.BlockSpec((SSD_CHUNK, bc_w), lambda c: (c, SSD_D_INNER // bc_w + 1)),
                  small, small, wide, wide, pl.BlockSpec((LANE, SSD_D_INNER), lambda c: (0, 0))],
        out_specs=[row_d(0), row_d(0), pl.BlockSpec((1, SSD_D_INNER, SSD_STATE), lambda c: (c, 0, 0))],
        out_shape=[jax.ShapeDtypeStruct((s, SSD_D_INNER), F32), jax.ShapeDtypeStruct((s, SSD_D_INNER + ATT_D), BF16),
                   jax.ShapeDtypeStruct((nc, SSD_D_INNER, SSD_STATE), F32)],
        scratch_shapes=[pltpu.VMEM((SSD_D_INNER, SSD_STATE), F32)],
        compiler_params=_cparams(("arbitrary",)))(proj, proj, act, act, act, dtb, alog, _expand_heads(dsk), nw, exp_mat)


def _ssd_bwd(dycat, ypre, proj, act, hall, dtb, alog, dsk, nw, comm=None):
    s = proj.shape[0]
    nc = s // SSD_CHUNK
    bc_w = SSD_GROUPS * SSD_STATE

    exp_mat, ind4 = _ssd_constants()
    seg_passes = 1

    def body(dy_ref, ypre_ref, z_ref, dtr_ref, xs_ref, b_ref, c_ref, hall_ref, dtb_ref, alog_ref, dskx_ref, nw_ref,
             exp_ref, ind4_ref, dz_ref, dact_ref, ddtr_ref, da_ref, ddsk_ref, ddtb_ref, dnw_ref, dh_scr):
        @pl.when(pl.program_id(0) == 0)
        def _():
            dh_scr[...] = jnp.zeros_like(dh_scr)
            da_ref[...] = jnp.zeros_like(da_ref)
            ddsk_ref[...] = jnp.zeros_like(ddsk_ref)
            ddtb_ref[...] = jnp.zeros_like(ddtb_ref)
            dnw_ref[...] = jnp.zeros_like(dnw_ref)

        dtr = dtr_ref[...]
        dt, a, cs, cst, tril, lane, dtx, csx = _ssd_prep(dtr, dtb_ref[...], alog_ref[...], exp_ref[...])
        cs_last_x = csx[SSD_CHUNK - 1:SSD_CHUNK, :]
        xs = xs_ref[...]
        xdt = xs * dtx
        xdtb = xdt.astype(BF16)
        decx = jnp.exp(cs_last_x - csx)
        xdecf = xdt * decx
        xdec = xdecf.astype(BF16)
        ecsx = jnp.exp(csx)
        head_of_lane = lax.broadcasted_iota(jnp.int32, (1, GROUP_WIDTH), 1) // HEAD_DIM
        last_row = lax.broadcasted_iota(jnp.int32, (SSD_CHUNK, 1), 0) == SSD_CHUNK - 1
        dcs_col = jnp.zeros((SSD_CHUNK, LANE), F32)
        dcs_row = jnp.zeros((SSD_CHUNK, LANE), F32)
        ddt = jnp.zeros((SSD_CHUNK, LANE), F32)
        ddsk = jnp.zeros((1, LANE), F32)
        hsum = jnp.zeros((1, LANE), F32)
        t1_sum = jnp.zeros((1, LANE), F32)
        for g in range(SSD_GROUPS):
            gs = slice(g * GROUP_WIDTH, (g + 1) * GROUP_WIDTH)
            bsl = slice(g * SSD_STATE, (g + 1) * SSD_STATE)
            exp_g = exp_ref[:, gs]
            ind4_g = ind4_ref[g * HEADS_PER_GROUP * SSD_CHUNK:(g + 1) * HEADS_PER_GROUP * SSD_CHUNK, :]
            z = z_ref[:, gs]
            sg = _sigmoid(z)
            sz = z * sg
            ypre = ypre_ref[:, gs]
            yg = ypre * sz
            r = lax.rsqrt(jnp.mean(yg * yg, axis=-1, keepdims=True) + EPS)
            nrm = yg * r
            dyo_n = dy_ref[:, gs]
            dnw_ref[:, gs] += jnp.sum(dyo_n * nrm, axis=0, keepdims=True)
            dn = dyo_n * nw_ref[:, gs]
            dyg = r * (dn - nrm * jnp.mean(dn * nrm, axis=-1, keepdims=True))
            dz_ref[:, gs] = (dyg * ypre * (sg * (1.0 + z * (1.0 - sg)))).astype(BF16)
            dy = dyg * sz

            bg = b_ref[:, bsl].astype(BF16)
            cg = c_ref[:, bsl].astype(BF16)
            cb = _dot(cg, bg, NT)
            hprev = hall_ref[0, gs, :]
            hb = hprev.astype(BF16)
            dhn = dh_scr[gs, :]
            dhb = dhn.astype(BF16)
            xs_g, xdt_g = xs[:, gs], xdtb[:, gs]
            w_off = _dot(cg, hb, NT)
            dyo = dy * ecsx[:, gs]
            dyob = dyo.astype(BF16)
            dcg = _dot(dyob, hb, NN)
            dh_y = _dot(dyob, cg, TN)
            r_st = _dot(bg, dhb, NT)
            dbg = _dot(xdec[:, gs], dhb, NN)
            dyb = dy.astype(BF16)
            gms, gmbs, lms, dys = [], [], [], []
            for e in range(HEADS_PER_GROUP):
                h = g * HEADS_PER_GROUP + e
                lm = jnp.exp(jnp.where(tril, cs[:, h:h + 1] - cst[h:h + 1, :], -1e30))
                gm = cb * lm
                lms.append(lm)
                gms.append(gm)
                gmbs.append(gm.astype(BF16))
                dys.append(jnp.where(head_of_lane == e, dyb, jnp.zeros_like(dyb)))
            dxdt = _dot(jnp.concatenate(gmbs, axis=0), jnp.concatenate(dys, axis=0), TN) + decx[:, gs] * r_st
            dcb = jnp.zeros((SSD_CHUNK, SSD_CHUNK), F32)
            mms = []
            for e in range(HEADS_PER_GROUP):
                dg = _dot(dys[e], xdt_g, NT)
                mms.append(dg * gms[e])
                dcb = dcb + dg * lms[e]
            seg = _dot_split(jnp.concatenate([dyo * w_off, xdecf[:, gs] * r_st, dxdt * xs_g, dy * xs_g], axis=0), exp_g, NT, seg_passes)
            v1, t1, ddt_g, dsk_g = [seg[i * SSD_CHUNK:(i + 1) * SSD_CHUNK] for i in range(4)]
            dcs_col = dcs_col + v1 - t1 + _dot_split(jnp.concatenate(mms, axis=1), ind4_g, NN, seg_passes)
            for t in _split_bf16(jnp.concatenate(mms, axis=0), seg_passes):
                dcs_row = dcs_row + _dot(ind4_g, t, TN)
            ddt = ddt + ddt_g
            ddsk = ddsk + jnp.sum(dsk_g, axis=0, keepdims=True)
            t1_sum = t1_sum + jnp.sum(t1, axis=0, keepdims=True)
            for e in range(HEADS_PER_GROUP):
                h = g * HEADS_PER_GROUP + e
                hs = slice(e * HEAD_DIM, (e + 1) * HEAD_DIM)
                hsum = hsum + jnp.where(lane == h, jnp.sum(dhn[hs, :] * hprev[hs, :]).reshape(1, 1), 0.0)
            dh_scr[gs, :] = dhn * _chunk_decay_rows(cs, g) + dh_y
            dcbb = dcb.astype(BF16)
            dact_ref[:, gs] = dxdt * dtx[:, gs] + dskx_ref[:, gs] * dy
            dact_ref[:, SSD_D_INNER + g * SSD_STATE:SSD_D_INNER + (g + 1) * SSD_STATE] = dbg + _dot(dcbb, cg, TN)
            dact_ref[:, SSD_D_INNER + bc_w + g * SSD_STATE:SSD_D_INNER + bc_w + (g + 1) * SSD_STATE] = dcg + _dot(dcbb, bg, NN)
        dlast = t1_sum + jnp.exp(cs[SSD_CHUNK - 1:SSD_CHUNK, :]) * hsum
        dcs = dcs_col - dcs_row.T + jnp.where(last_row, dlast, 0.0)
        row = lax.broadcasted_iota(jnp.int32, (SSD_CHUNK, SSD_CHUNK), 0)
        col = lax.broadcasted_iota(jnp.int32, (SSD_CHUNK, SSD_CHUNK), 1)
        dda = _dot((col >= row).astype(F32), dcs, NN, precision=HIGHEST)
        ddt = ddt + dda * a
        da_ref[...] += jnp.sum(dda * dt, axis=0, keepdims=True)
        ddtr = jnp.where(lane < SSD_HEADS, ddt * _sigmoid(dtr + dtb_ref[...]), 0.0)
        ddtr_ref[...] = ddtr.astype(BF16)
        ddtb_ref[...] += jnp.sum(ddtr, axis=0, keepdims=True)
        ddsk_ref[...] += ddsk

    rev = lambda c: nc - 1 - c
    row_d = lambda cb: pl.BlockSpec((SSD_CHUNK, SSD_D_INNER), lambda c: (rev(c), cb))
    small = pl.BlockSpec((1, LANE), lambda c: (0, 0))
    wide = pl.BlockSpec((1, SSD_D_INNER), lambda c: (0, 0))
    small_shape = jax.ShapeDtypeStruct((1, LANE), F32)
    return _pcall(
        body, (dycat, ypre, proj, proj, act, act, act, hall, dtb, alog, _expand_heads(dsk), nw, exp_mat, ind4),
        name="ssd_bwd", grid=(nc,),
        in_specs=[row_d(0), row_d(0), row_d(OFF_Z // SSD_D_INNER),
                  pl.BlockSpec((SSD_CHUNK, LANE), lambda c: (rev(c), OFF_DT // LANE)),
                  row_d(0),
                  pl.BlockSpec((SSD_CHUNK, bc_w), lambda c: (rev(c), SSD_D_INNER // bc_w)),
                  pl.BlockSpec((SSD_CHUNK, bc_w), lambda c: (rev(c), SSD_D_INNER // bc_w + 1)),
                  pl.BlockSpec((1, SSD_D_INNER, SSD_STATE), lambda c: (rev(c), 0, 0)),
                  small, small, wide, wide, pl.BlockSpec((LANE, SSD_D_INNER), lambda c: (0, 0)),
                  pl.BlockSpec((SSD_HEADS * SSD_CHUNK, LANE), lambda c: (0, 0))],
        out_specs=[row_d(0), pl.BlockSpec((SSD_CHUNK, CONV_CH), lambda c: (rev(c), 0)),
                   pl.BlockSpec((SSD_CHUNK, LANE), lambda c: (rev(c), 0)), small, small, small, wide],
        out_shape=[jax.ShapeDtypeStruct((s, SSD_D_INNER), BF16), jax.ShapeDtypeStruct((s, CONV_CH), F32),
                   jax.ShapeDtypeStruct((s, LANE), BF16), small_shape, small_shape, small_shape,
                   jax.ShapeDtypeStruct((1, SSD_D_INNER), F32)],
        scratch_shapes=[pltpu.VMEM((SSD_D_INNER, SSD_STATE), F32)], sem=("arbitrary",), comm=comm)


def _head_mean_matrix():
    row = lax.broadcasted_iota(jnp.int32, (LANE, LANE), 0) // HEAD_DIM
    col = lax.broadcasted_iota(jnp.int32, (LANE, LANE), 1) // HEAD_DIM
    return (row == col).astype(F32)


def _head_sum2(v, ones_bd):
    hi = v.astype(BF16)
    lo = (v - hi.astype(F32)).astype(BF16)
    return _dot(jnp.concatenate([hi, lo], axis=1), jnp.concatenate([ones_bd, ones_bd], axis=0), NN)


def _head_norm(x, w, scale, ones_bd):
    ms = _head_sum2(x * x, ones_bd) * (1.0 / HEAD_DIM)
    return (x * lax.rsqrt(ms + EPS)) * (w * scale)


PRO_ROWS = 256
ATT_GROUP_FWD = 16
ATT_GROUP_BWD = 8
KEYS = 2 * ATT_BLK
NEG = -1e30
HALF = HEAD_DIM // 2


def _rows(start, size, dil):
    return pl.ds(start, size) if dil == 1 else pl.ds(start, size, stride=dil)


def _fill_bias(bias_ref):
    row = lax.broadcasted_iota(jnp.int32, (ATT_BLK, 2 * KEYS), 0)
    col = lax.broadcasted_iota(jnp.int32, (ATT_BLK, 2 * KEYS), 1) & (KEYS - 1)
    for first, off in ((0, 0), (1, ATT_BLK)):
        dist = off + row - col
        bias_ref[first] = jnp.where((dist >= 0) & (dist <= ATT_BLK), 0.0, NEG)


def _pair(a, b):
    return jnp.concatenate([jnp.broadcast_to(a, (ATT_BLK, KEYS)), jnp.broadcast_to(b, (ATT_BLK, KEYS))], axis=1)


def _split_heads(x, is_a):
    zero = jnp.zeros_like(x)
    return jnp.concatenate([jnp.where(is_a, x, zero), jnp.where(is_a, zero, x)], axis=0)


def _block_ids(b, nb):
    i = b & (nb - 1)
    q0 = pl.multiple_of(b * ATT_BLK, ATT_BLK)
    k0 = pl.multiple_of((b - jnp.minimum(i, 1)) * ATT_BLK, ATT_BLK)
    return pl.ds(q0, ATT_BLK), pl.ds(k0, KEYS), jnp.minimum(i, 1)


def _att_fwd(proj, qw, kw, comm=None):
    s = proj.shape[0]
    nblk = s // ATT_BLK
    assert all((s // d) // ATT_BLK >= 2 for d in DILATIONS)
    blk = lambda off: pl.BlockSpec((s, LANE), lambda i: (0, off // LANE + i))
    wspec = pl.BlockSpec((1, LANE), lambda i: (0, i))
    oblk = pl.BlockSpec((s, LANE), lambda i: (0, i))

    def body(q_ref, k_ref, v_ref, qw_ref, kw_ref, o_ref, lse_ref, qn, kn, q_cm, k_cm, v_cm, m_acc, l_acc, o_d, m_d, l_d, bias):
        ones_bd = _head_mean_matrix().astype(BF16)
        is_a = lax.broadcasted_iota(jnp.int32, (1, LANE), 1) < HEAD_DIM
        ones_ext = _split_heads(jnp.ones((KEYS, LANE), BF16), is_a)
        _fill_bias(bias)

        def pro(j, c):
            rows = pl.ds(pl.multiple_of(j * PRO_ROWS, PRO_ROWS), PRO_ROWS)
            qn[rows, :] = _head_norm(q_ref[rows, :], qw_ref[...], HEAD_DIM ** -0.5, ones_bd)
            kn[rows, :] = _head_norm(k_ref[rows, :], kw_ref[...], 1.0, ones_bd)
            return c

        lax.fori_loop(0, s // PRO_ROWS, pro, 0)

        for dil in DILATIONS:
            ln = s // dil
            nb = ln // ATT_BLK
            o_out, m_out, l_out = (o_ref, m_acc, l_acc) if dil == 1 else (o_d, m_d, l_d)
            for r in range(dil):
                def relayout(j, c, dil=dil, r=r, ln=ln):
                    j0 = pl.multiple_of(j * PRO_ROWS, PRO_ROWS)
                    src = _rows(r + dil * j0, PRO_ROWS, dil)
                    dst = pl.ds(r * ln + j0, PRO_ROWS)
                    q_cm[dst, :] = qn[src, :].astype(BF16)
                    k_cm[dst, :] = kn[src, :].astype(BF16)
                    v_cm[dst, :] = v_ref[src, :].astype(BF16)
                    return c

                lax.fori_loop(0, ln // PRO_ROWS, relayout, 0)

            def step(bg, c, nb=nb, o_out=o_out, m_out=m_out, l_out=l_out):
                ids = [_block_ids(bg * ATT_GROUP_FWD + u, nb) for u in range(ATT_GROUP_FWD)]
                kbs = [_split_heads(k_cm[krows, :], is_a) for _, krows, _ in ids]
                scs = [_dot(q_cm[qrows, :], kb, NT) + bias[first] for (qrows, _, first), kb in zip(ids, kbs)]
                mas = [jnp.max(sc[:, :KEYS], axis=-1, keepdims=True) for sc in scs]
                mbs = [jnp.max(sc[:, KEYS:], axis=-1, keepdims=True) for sc in scs]
                ps = [jnp.exp(sc - _pair(ma, mb)).astype(BF16) for sc, ma, mb in zip(scs, mas, mbs)]
                vbs = [jnp.concatenate([_split_heads(v_cm[krows, :], is_a), ones_ext], axis=1) for _, krows, _ in ids]
                ols = [_dot(p, vb, NN) for p, vb in zip(ps, vbs)]
                for (qrows, _, _), ol, ma, mb in zip(ids, ols, mas, mbs):
                    o_out[qrows, :] = ol[:, :LANE]
                    l_out[qrows, :] = ol[:, LANE:]
                    m_out[qrows, :] = jnp.where(is_a, ma, mb)
                return c

            lax.fori_loop(0, nblk // ATT_GROUP_FWD, step, 0)

            if dil > 1:
                for r in range(dil):
                    def merge(j, c, dil=dil, r=r, ln=ln):
                        j0 = pl.multiple_of(j * PRO_ROWS, PRO_ROWS)
                        nat = _rows(r + dil * j0, PRO_ROWS, dil)
                        cm = pl.ds(r * ln + j0, PRO_ROWS)
                        m_old, m_new = m_acc[nat, :], m_d[cm, :]
                        m = jnp.maximum(m_old, m_new)
                        a_old, a_new = jnp.exp(m_old - m), jnp.exp(m_new - m)
                        o_ref[nat, :] = a_old * o_ref[nat, :] + a_new * o_d[cm, :]
                        l_acc[nat, :] = a_old * l_acc[nat, :] + a_new * l_d[cm, :]
                        m_acc[nat, :] = m
                        return c

                    lax.fori_loop(0, ln // PRO_ROWS, merge, 0)

        def epi(j, c):
            rows = pl.ds(pl.multiple_of(j * PRO_ROWS, PRO_ROWS), PRO_ROWS)
            l = l_acc[rows, :]
            o_ref[rows, :] = o_ref[rows, :] / l
            lse_ref[rows, :] = m_acc[rows, :] + jnp.log(l)
            return c

        lax.fori_loop(0, s // PRO_ROWS, epi, 0)

    f = jax.ShapeDtypeStruct((s, ATT_D), F32)
    scr = pltpu.VMEM((s, LANE), F32)
    scb = pltpu.VMEM((s, LANE), BF16)
    return _pcall(
        body, (proj, proj, proj, qw, kw), name="att_fwd", grid=(ATT_D // LANE,),
        in_specs=[blk(OFF_Q), blk(OFF_K), blk(OFF_V), wspec, wspec], out_specs=[oblk, oblk], out_shape=[f, f],
        scratch_shapes=[scr, scr, scb, scb, scb, scr, scr, scr, scr, scr, pltpu.VMEM((2, ATT_BLK, 2 * KEYS), F32)],
        sem=("parallel",), comm=comm)


def _att_bwd(proj, do, stats, qw, kw, comm=None):
    s = proj.shape[0]
    nblk = s // ATT_BLK
    blk = lambda off: pl.BlockSpec((s, LANE), lambda i: (0, off // LANE + i))
    wspec = pl.BlockSpec((1, LANE), lambda i: (0, i))
    oblk = pl.BlockSpec((s, LANE), lambda i: (0, i))

    def body(q_ref, k_ref, v_ref, do_ref, st_ref, qw_ref, kw_ref, dq_ref, dk_ref, dv_ref, dqw_ref, dkw_ref,
             qn, kn, q_cm, do_cm, k_cm, v_cm, st_cm, dq_acc, dk_acc, dv_acc, dq_d, dk_d, dv_d, bias):
        ones_bd = _head_mean_matrix().astype(BF16)
        is_a = lax.broadcasted_iota(jnp.int32, (1, LANE), 1) < HEAD_DIM
        _fill_bias(bias)
        zero = jnp.zeros((PRO_ROWS, LANE), F32)

        def pro(j, c):
            rows = pl.ds(pl.multiple_of(j * PRO_ROWS, PRO_ROWS), PRO_ROWS)
            qn[rows, :] = _head_norm(q_ref[rows, :], qw_ref[...], HEAD_DIM ** -0.5, ones_bd)
            kn[rows, :] = _head_norm(k_ref[rows, :], kw_ref[...], 1.0, ones_bd)
            dk_acc[rows, :] = zero
            dv_acc[rows, :] = zero
            return c

        lax.fori_loop(0, s // PRO_ROWS, pro, 0)

        for dil in DILATIONS:
            ln = s // dil
            nb = ln // ATT_BLK
            dq_o, dk_o, dv_o = (dq_acc, dk_acc, dv_acc) if dil == 1 else (dq_d, dk_d, dv_d)
            for r in range(dil):
                def relayout(j, c, dil=dil, r=r, ln=ln):
                    j0 = pl.multiple_of(j * PRO_ROWS, PRO_ROWS)
                    src = _rows(r + dil * j0, PRO_ROWS, dil)
                    dst = pl.ds(r * ln + j0, PRO_ROWS)
                    q_cm[dst, :] = qn[src, :].astype(BF16)
                    k_cm[dst, :] = kn[src, :].astype(BF16)
                    v_cm[dst, :] = v_ref[src, :].astype(BF16)
                    do_cm[dst, :] = do_ref[src, :].astype(BF16)
                    st_cm[dst, :] = st_ref[src, :]
                    if dil > 1:
                        dk_d[dst, :] = zero
                        dv_d[dst, :] = zero
                    return c

                lax.fori_loop(0, ln // PRO_ROWS, relayout, 0)

            def step(bg, c, nb=nb, dq_o=dq_o, dk_o=dk_o, dv_o=dv_o):
                ids = [_block_ids(bg * ATT_GROUP_BWD + u, nb) for u in range(ATT_GROUP_BWD)]
                qbs = [q_cm[qrows, :] for qrows, _, _ in ids]
                dobs = [do_cm[qrows, :] for qrows, _, _ in ids]
                kbs = [_split_heads(k_cm[krows, :], is_a) for _, krows, _ in ids]
                vbs = [_split_heads(v_cm[krows, :], is_a) for _, krows, _ in ids]
                sts = [st_cm[qrows, :] for qrows, _, _ in ids]
                scs = [_dot(qb, kb, NT) + bias[first] for qb, kb, (_, _, first) in zip(qbs, kbs, ids)]
                dps = [_dot(dob, vb, NT) for dob, vb in zip(dobs, vbs)]
                ps = [jnp.exp(sc - _pair(st[:, 0:1], st[:, HEAD_DIM:HEAD_DIM + 1])) for sc, st in zip(scs, sts)]
                dss = [(p * (dp - _pair(st[:, HALF:HALF + 1], st[:, HEAD_DIM + HALF:HEAD_DIM + HALF + 1]))).astype(BF16)
                       for p, dp, st in zip(ps, dps, sts)]
                dqs = [_dot(ds, kb, NN) for ds, kb in zip(dss, kbs)]
                dkfs = [_dot(ds, qb, TN) for ds, qb in zip(dss, qbs)]
                dvfs = [_dot(p.astype(BF16), dob, TN) for p, dob in zip(ps, dobs)]
                for (qrows, krows, _), dq, dkf, dvf in zip(ids, dqs, dkfs, dvfs):
                    dq_o[qrows, :] = dq
                    dk_o[krows, :] += jnp.where(is_a, dkf[:KEYS], dkf[KEYS:])
                    dv_o[krows, :] += jnp.where(is_a, dvf[:KEYS], dvf[KEYS:])
                return c

            lax.fori_loop(0, nblk // ATT_GROUP_BWD, step, 0)

            if dil > 1:
                for r in range(dil):
                    def merge(j, c, dil=dil, r=r, ln=ln):
                        j0 = pl.multiple_of(j * PRO_ROWS, PRO_ROWS)
                        nat = _rows(r + dil * j0, PRO_ROWS, dil)
                        cm = pl.ds(r * ln + j0, PRO_ROWS)
                        dq_acc[nat, :] += dq_d[cm, :]
                        dk_acc[nat, :] += dk_d[cm, :]
                        dv_acc[nat, :] += dv_d[cm, :]
                        return c

                    lax.fori_loop(0, ln // PRO_ROWS, merge, 0)

        def back(dn_out, x, w, scale):
            r = lax.rsqrt(_head_sum2(x * x, ones_bd) * (1.0 / HEAD_DIM) + EPS)
            nrm = x * r
            dw = jnp.sum(dn_out * nrm, axis=0, keepdims=True) * scale
            dn = dn_out * (w * scale)
            return r * (dn - nrm * (_head_sum2(dn * nrm, ones_bd) * (1.0 / HEAD_DIM))), dw

        def epi(j, c):
            rows = pl.ds(pl.multiple_of(j * PRO_ROWS, PRO_ROWS), PRO_ROWS)
            dq, dqw = back(dq_acc[rows, :], q_ref[rows, :], qw_ref[...], HEAD_DIM ** -0.5)
            dk, dkw = back(dk_acc[rows, :], k_ref[rows, :], kw_ref[...], 1.0)
            dq_ref[rows, :] = dq.astype(BF16)
            dk_ref[rows, :] = dk.astype(BF16)
            dv_ref[rows, :] = dv_acc[rows, :].astype(BF16)
            return (c[0] + dqw, c[1] + dkw)

        zrow = jnp.zeros((1, LANE), F32)
        dqw, dkw = lax.fori_loop(0, s // PRO_ROWS, epi, (zrow, zrow))
        dqw_ref[...] = dqw
        dkw_ref[...] = dkw

    o = jax.ShapeDtypeStruct((s, ATT_D), BF16)
    ov = jax.ShapeDtypeStruct((1, ATT_D), F32)
    scr = pltpu.VMEM((s, LANE), F32)
    scb = pltpu.VMEM((s, LANE), BF16)
    return _pcall(
        body, (proj, proj, proj, do, stats, qw, kw), name="att_bwd", grid=(ATT_D // LANE,),
        in_specs=[blk(OFF_Q), blk(OFF_K), blk(OFF_V), oblk, oblk, wspec, wspec],
        out_specs=[oblk, oblk, oblk, wspec, wspec], out_shape=[o, o, o, ov, ov],
        scratch_shapes=[scr, scr, scb, scb, scb, scb, scr, scr, scr, scr, scr, scr, scr, pltpu.VMEM((2, ATT_BLK, 2 * KEYS), F32)],
        sem=("parallel",), comm=comm)


def _att_norm_fwd(o, nw, ycat):
    s = o.shape[0]
    row = pl.BlockSpec((ROW_TILE, ATT_D), lambda i: (i, 0))
    vec = pl.BlockSpec((1, ATT_D), lambda i: (0, 0))

    def body(o_ref, nw_ref, ycat_ref, y_ref):
        o = o_ref[...]
        r = lax.rsqrt(jnp.mean(o * o, axis=-1, keepdims=True) + EPS)
        y_ref[...] = (o * r * nw_ref[...]).astype(BF16)

    return pl.pallas_call(body, name="att_norm_fwd", grid=(s // ROW_TILE,),
                          in_specs=[row, vec, pl.BlockSpec(memory_space=pl.ANY)],
                          out_specs=pl.BlockSpec((ROW_TILE, ATT_D), lambda i: (i, 1)),
                          out_shape=jax.ShapeDtypeStruct(ycat.shape, BF16), input_output_aliases={2: 0},
                          compiler_params=_cparams(("parallel",)))(o, nw, ycat)


def _mixer_split_epilogue(dycat, first, rows, vecs, outs):
    (o_ref, lse_ref), (nw_ref,), (dyssd_ref, do_ref, st_ref, dnw_ref) = rows, vecs, outs

    @pl.when(first)
    def _():
        dnw_ref[...] = jnp.zeros_like(dnw_ref)

    dyssd_ref[...] = dycat[:, :SSD_D_INNER]
    dy = dycat[:, SSD_D_INNER:]
    o = o_ref[...]
    r = lax.rsqrt(jnp.mean(o * o, axis=-1, keepdims=True) + EPS)
    nrm = o * r
    dnw_ref[...] += jnp.sum(dy * nrm, axis=0, keepdims=True)
    dn = dy * nw_ref[...]
    do = r * (dn - nrm * jnp.mean(dn * nrm, axis=-1, keepdims=True))
    do_ref[...] = do
    ones_bd = _head_mean_matrix().astype(BF16)
    prod = do * o
    delta = jnp.concatenate([_head_sum2(prod[:, j * LANE:(j + 1) * LANE], ones_bd) for j in range(ATT_D // LANE)], axis=1)
    lane = lax.broadcasted_iota(jnp.int32, (1, ATT_D), 1)
    st_ref[...] = jnp.where((lane & (HEAD_DIM - 1)) < HALF, lse_ref[...], delta)


def _ada_fwd(c_all, w_ada):
    def body(c_ref, w_ref, o_ref):
        cv = c_ref[...]
        o_ref[...] = _dot((cv * _sigmoid(cv)).astype(BF16), w_ref[...].astype(BF16), NN)

    return pl.pallas_call(body, name="ada_fwd", out_shape=jax.ShapeDtypeStruct((c_all.shape[0], w_ada.shape[1]), F32),
                          compiler_params=_cparams())(c_all, w_ada)


def _adamw_math(g, w, m, v):
    m_new = ADAM_B1 * m + (1.0 - ADAM_B1) * g
    v_new = ADAM_B2 * v + (1.0 - ADAM_B2) * (g * g)
    m_hat = m_new / (1.0 - ADAM_B1 ** ADAM_STEP)
    v_hat = v_new / (1.0 - ADAM_B2 ** ADAM_STEP)
    delta = -ADAM_LR * (m_hat / (jnp.sqrt(v_hat) + ADAM_EPS) + ADAM_WD * w)
    return delta, m_new, v_new


def _ada_bwd_adamw(c_all, dmod_cols, w, m, v):
    rows, cols = w.shape
    tr = 256
    blk = pl.BlockSpec((tr, cols), lambda i: (i, 0))

    def body(c_ref, d_ref, w_ref, m_ref, v_ref, g_ref, dl_ref, mo_ref, vo_ref):
        cv = c_ref[...]
        ca = cv * _sigmoid(cv)
        g = ca[:, 0:1] * d_ref[0:1, :]
        for b in range(1, N_DEV):
            g = g + ca[:, b:b + 1] * d_ref[b:b + 1, :]
        g_ref[...] = g
        dl_ref[...], mo_ref[...], vo_ref[...] = _adamw_math(g, w_ref[...], m_ref[...], v_ref[...])

    o = jax.ShapeDtypeStruct((rows, cols), F32)
    return pl.pallas_call(
        body, name="ada_bwd_adamw", grid=(rows // tr,),
        in_specs=[pl.BlockSpec((tr, N_DEV), lambda i: (i, 0)), pl.BlockSpec((N_DEV, cols), lambda i: (0, 0)), blk, blk, blk],
        out_specs=[blk] * 4, out_shape=[o, o, o, o], compiler_params=_cparams(("parallel",)))(c_all.T, dmod_cols, w, m, v)


def _reduce_adamw(slabs, w, m, v, name):
    rows, cols = w.shape
    n_src = slabs.shape[0]
    if rows % 128 == 0:
        tr, steps = 128, rows // 128
        blk = pl.BlockSpec((tr, cols), lambda i: (i, 0))
        sblk = pl.BlockSpec((n_src, tr, cols), lambda i: (0, i, 0))
    else:
        tc, steps = 256, cols // 256
        blk = pl.BlockSpec((rows, tc), lambda i: (0, i))
        sblk = pl.BlockSpec((n_src, rows, tc), lambda i: (0, 0, i))

    def body(s_ref, w_ref, m_ref, v_ref, g_ref, dl_ref, mo_ref, vo_ref):
        g = s_ref[0].astype(F32)
        for src in range(1, n_src):
            g = g + s_ref[src].astype(F32)
        g_ref[...] = g
        dl_ref[...], mo_ref[...], vo_ref[...] = _adamw_math(g, w_ref[...], m_ref[...], v_ref[...])

    o = jax.ShapeDtypeStruct((rows, cols), F32)
    return pl.pallas_call(
        body, name=name, grid=(steps,), in_specs=[sblk, blk, blk, blk],
        out_specs=[blk] * 4, out_shape=[o, o, o, o], compiler_params=_cparams(("parallel",)))(slabs, w, m, v)


def _small_reduce_adamw(gathered, w, m, v):
    def body(s_ref, w_ref, m_ref, v_ref, g_ref, dl_ref, mo_ref, vo_ref):
        g = s_ref[0]
        for dev in range(1, N_DEV):
            g = g + s_ref[dev]
        g_ref[...] = g
        dl_ref[...], mo_ref[...], vo_ref[...] = _adamw_math(g, w_ref[...], m_ref[...], v_ref[...])

    o = jax.ShapeDtypeStruct(w.shape, F32)
    return pl.pallas_call(body, name="small_reduce_adamw", out_shape=[o, o, o, o], compiler_params=_cparams())(gathered, w, m, v)


def _adamw_small(g, w, m, v, name):
    def body(g_ref, w_ref, m_ref, v_ref, dl_ref, mo_ref, vo_ref):
        dl_ref[...], mo_ref[...], vo_ref[...] = _adamw_math(g_ref[...], w_ref[...], m_ref[...], v_ref[...])

    o = jax.ShapeDtypeStruct(w.shape, F32)
    return pl.pallas_call(body, name=name, out_shape=[o, o, o], compiler_params=_cparams())(g, w, m, v)


class _Exchange:
    def __init__(self, arrs, scatter):
        self.arrs, self.scatter, self.n = list(arrs), scatter, len(arrs)
        hbm = pl.BlockSpec(memory_space=pltpu.HBM)
        self.in_specs = [hbm] * self.n
        self.out_specs = [hbm] * self.n
        self.out_shape = [jax.ShapeDtypeStruct(a.shape if scatter else (N_DEV,) + a.shape, a.dtype) for a in self.arrs]
        self.scratch = [pltpu.SemaphoreType.DMA((self.n * (N_DEV - 1),)), pltpu.SemaphoreType.DMA((self.n * (N_DEV - 1),)),
                        pltpu.SemaphoreType.DMA((self.n,))]

    def _local(self, ins, outs, sems):
        me = 4 * lax.axis_index("x") + 2 * lax.axis_index("y") + lax.axis_index("c")
        return [pltpu.make_async_copy(ins[a].at[me] if self.scatter else ins[a], outs[a].at[me], sems[2].at[a])
                for a in range(self.n)]

    def _remote(self, ins, outs, sems, arriving):
        send_sems, recv_sems, _ = sems
        x, y, c = lax.axis_index("x"), lax.axis_index("y"), lax.axis_index("c")
        me = 4 * x + 2 * y + c
        remote = []
        for a in range(self.n):
            for k in range(1, N_DEV):
                px = 1 - x if k & 4 else x
                py = 1 - y if k & 2 else y
                pc = 1 - c if k & 1 else c
                peer = 4 * px + 2 * py + pc
                sem = a * (N_DEV - 1) + k - 1
                remote.append(pltpu.make_async_remote_copy(
                    src_ref=ins[a].at[peer] if self.scatter else ins[a], dst_ref=outs[a].at[peer if arriving else me],
                    send_sem=send_sems.at[sem], recv_sem=recv_sems.at[sem], device_id=(px, py, pc), device_id_type=MESH_IDS))
        return remote

    def start(self, ins, outs, sems):
        for cp in self._local(ins, outs, sems) + self._remote(ins, outs, sems, arriving=False):
            cp.start()

    def forward(self, ins, outs, sems):
        pass

    def wait(self, ins, outs, sems):
        for send, arrival in zip(self._remote(ins, outs, sems, arriving=False), self._remote(ins, outs, sems, arriving=True)):
            send.wait_send()
            arrival.wait_recv()
        for cp in self._local(ins, outs, sems):
            cp.wait()


N_CHIP = N_DEV // 2


class _SiblingSwap(_Exchange):
    def __init__(self, arrs):
        super().__init__(arrs, scatter=True)
        self.out_shape = [jax.ShapeDtypeStruct((N_CHIP,) + a.shape[2:], a.dtype) for a in self.arrs]
        self.scratch = [pltpu.SemaphoreType.DMA((self.n,)), pltpu.SemaphoreType.DMA((self.n,)), pltpu.SemaphoreType.DMA((1,))]

    def _copies(self, ins, outs, sems):
        x, y, c = lax.axis_index("x"), lax.axis_index("y"), lax.axis_index("c")
        return [pltpu.make_async_remote_copy(src_ref=ins[a].at[:, 1 - c], dst_ref=outs[a], send_sem=sems[0].at[a], recv_sem=sems[1].at[a],
                                             device_id=(x, y, 1 - c), device_id_type=MESH_IDS) for a in range(self.n)]

    def start(self, ins, outs, sems):
        for cp in self._copies(ins, outs, sems):
            cp.start()

    def wait(self, ins, outs, sems):
        for cp in self._copies(ins, outs, sems):
            cp.wait()


class _ChipScatter(_Exchange):
    def __init__(self, arrs):
        super().__init__(arrs, scatter=True)
        n_pairs = self.n * (N_CHIP - 1)
        self.scratch = [pltpu.SemaphoreType.DMA((n_pairs,)), pltpu.SemaphoreType.DMA((n_pairs,)), pltpu.SemaphoreType.DMA((self.n,))]

    def _local(self, ins, outs, sems):
        chip = 2 * lax.axis_index("x") + lax.axis_index("y")
        return [pltpu.make_async_copy(ins[a].at[chip], outs[a].at[chip], sems[2].at[a]) for a in range(self.n)]

    def _remote(self, ins, outs, sems, arriving):
        send_sems, recv_sems, _ = sems
        x, y, c = lax.axis_index("x"), lax.axis_index("y"), lax.axis_index("c")
        chip = 2 * x + y
        remote = []
        for a in range(self.n):
            for k in range(1, N_CHIP):
                px = 1 - x if k & 2 else x
                py = 1 - y if k & 1 else y
                peer = 2 * px + py
                sem = a * (N_CHIP - 1) + k - 1
                remote.append(pltpu.make_async_remote_copy(
                    src_ref=ins[a].at[peer], dst_ref=outs[a].at[peer if arriving else chip], send_sem=send_sems.at[sem],
                    recv_sem=recv_sems.at[sem], device_id=(px, py, c), device_id_type=MESH_IDS))
        return remote


def _chip_sum(mine, theirs):
    n, rows, cols = mine.shape
    blk = pl.BlockSpec((1, rows, 256), lambda q, j: (q, 0, j))

    def body(a_ref, b_ref, o_ref):
        o_ref[...] = (a_ref[...].astype(F32) + b_ref[...].astype(F32)).astype(BF16)

    return pl.pallas_call(body, name="chip_sum", grid=(n, cols // 256), in_specs=[blk, blk], out_specs=blk,
                          out_shape=jax.ShapeDtypeStruct(mine.shape, BF16),
                          compiler_params=_cparams(("parallel", "parallel")))(mine, theirs)


class _Gather2(_Exchange):
    def __init__(self, arrs):
        super().__init__(arrs, scatter=False)

    def _copies(self, ins, outs, sems):
        send_sems, recv_sems, _ = sems
        x, y, c = lax.axis_index("x"), lax.axis_index("y"), lax.axis_index("c")
        sibling = (x, y, 1 - c)
        chips = [(1 - x, y), (x, 1 - y), (1 - x, 1 - y)]
        first, passed, landed = [], [], []
        for a in range(self.n):
            def copy(k, block, to, src=None, a=a):
                slab = outs[a].at[4 * block[0] + 2 * block[1] + block[2]]
                return pltpu.make_async_remote_copy(
                    src_ref=slab if src is None else src, dst_ref=slab, send_sem=send_sems.at[a * (N_DEV - 1) + k],
                    recv_sem=recv_sems.at[a * (N_DEV - 1) + k], device_id=to, device_id_type=MESH_IDS)

            first.append(copy(0, (x, y, c), sibling, src=ins[a]))
            landed.append(copy(0, sibling, sibling))
            for j, chip in enumerate(chips):
                first.append(copy(1 + j, (x, y, c), (*chip, c), src=ins[a]))
                passed.append((copy(1 + j, (*chip, c), sibling), copy(4 + j, (*chip, c), sibling)))
                landed.append(copy(4 + j, (*chip, 1 - c), sibling))
        return first, passed, landed

    def start(self, ins, outs, sems):
        for cp in self._local(ins, outs, sems) + self._copies(ins, outs, sems)[0]:
            cp.start()

    def forward(self, ins, outs, sems):
        for arrival, onward in self._copies(ins, outs, sems)[1]:
            arrival.wait_recv()
            onward.start()

    def wait(self, ins, outs, sems):
        first, passed, landed = self._copies(ins, outs, sems)
        for arrival in landed:
            arrival.wait_recv()
        for cp in first + [onward for _, onward in passed]:
            cp.wait_send()
        for cp in self._local(ins, outs, sems):
            cp.wait()


def _split_comm_refs(refs, n_in, n_out, n_scr, comm):
    nc = comm.n if comm is not None else 0
    ns = 3 if comm is not None else 0
    pos, groups = 0, []
    for cnt in (n_in, nc, n_out, nc, n_scr, ns):
        groups.append(refs[pos:pos + cnt])
        pos += cnt
    assert pos == len(refs), (pos, len(refs))
    return groups


def _pcall(body, args, *, name, grid, in_specs, out_specs, out_shape, scratch_shapes=(), sem=None, comm=None):
    in_specs, out_specs, out_shape, scratch_shapes = list(in_specs), list(out_specs), list(out_shape), list(scratch_shapes)
    n_in, n_out, n_scr = len(in_specs), len(out_specs), len(scratch_shapes)
    if comm is None:
        kernel_body = body
    else:
        def kernel_body(*refs):
            ins, cins, outs, couts, scr, sems = _split_comm_refs(refs, n_in, n_out, n_scr, comm)
            ids = [pl.program_id(a) for a in range(len(grid))]
            first, last = ids[0] == 0, ids[0] == grid[0] - 1
            for a in range(1, len(grid)):
                first, last = first & (ids[a] == 0), last & (ids[a] == grid[a] - 1)

            middle = ids[0] == (2 * grid[0]) // 3
            for a in range(1, len(grid)):
                middle = middle & (ids[a] == 0)

            @pl.when(first)
            def _():
                comm.start(cins, couts, sems)

            @pl.when(middle)
            def _():
                comm.forward(cins, couts, sems)

            body(*ins, *outs, *scr)

            @pl.when(last)
            def _():
                comm.wait(cins, couts, sems)

        in_specs, out_specs, out_shape = in_specs + comm.in_specs, out_specs + comm.out_specs, out_shape + comm.out_shape
        scratch_shapes, args = scratch_shapes + comm.scratch, list(args) + comm.arrs
        sem = ("arbitrary",) * len(grid)
    res = pl.pallas_call(kernel_body, name=name, grid=grid, in_specs=in_specs, out_specs=out_specs, out_shape=out_shape,
                         scratch_shapes=scratch_shapes, compiler_params=_cparams(sem))(*args)
    return res[:n_out], res[n_out:]


def _exchange(arrs, name, scatter=False, ex=None):
    if ex is None:
        ex = _Exchange(arrs, scatter=True) if scatter else _Gather2(arrs)

    def body(*refs):
        _, ins, _, outs, _, sems = _split_comm_refs(refs, 0, 0, 0, ex)
        ex.start(ins, outs, sems)
        ex.forward(ins, outs, sems)
        ex.wait(ins, outs, sems)

    return pl.pallas_call(body, name=name, in_specs=ex.in_specs, out_specs=ex.out_specs, out_shape=ex.out_shape,
                          scratch_shapes=ex.scratch)(*ex.arrs)


def _pad_lanes(v, width=LANE):
    return jnp.pad(v, ((0, 0), (0, width - v.shape[1])))


def _shards_to_cols(g):
    return jnp.transpose(g, (1, 0, 2)).reshape(g.shape[1], N_DEV * g.shape[2])


def _cols_to_shards(w):
    return w.astype(BF16).reshape(w.shape[0], N_DEV, w.shape[1] // N_DEV).transpose(1, 0, 2)


def _local_step(x, tgt, mod, w_in_pt, conv_w, conv_b, dt_bias, a_log, d_skip, ssd_norm_w, q_norm_w, k_norm_w,
                attn_norm_w, w_out_sh, w_ff1_sh, w_ff2_sh, norm1_w, norm2_w, core):
    shift1, scale1, gate1, shift2, scale2, gate2 = [mod[i:i + 1] for i in range(N_MOD)]
    dtb, alog, dsk = _pad_lanes(dt_bias), _pad_lanes(a_log), _pad_lanes(d_skip)
    qw, kw = jnp.tile(q_norm_w, (1, ATT_HEADS)), jnp.tile(k_norm_w, (1, ATT_HEADS))

    h1 = _norm_mod_fwd(x, norm1_w, scale1, shift1, "norm1_fwd")
    proj = _matmul(h1, w_in_pt, tb=True, tm=2048, tn=896, tk=1024, name="in_proj")
    pre, act = _conv_fwd(proj, conv_w, conv_b)
    ypre, ycat_ssd, hall = _ssd_fwd(proj, act, dtb, alog, dsk, ssd_norm_w)
    (o_att, lse), (w_out_g, w_ff1_g, w_ff2_g) = _att_fwd(proj, qw, kw, comm=_Gather2([w_out_sh, w_ff1_sh, w_ff2_sh]))
    w_out = w_out_g.reshape(2 * D_MODEL, D_MODEL)
    w_ff1 = _shards_to_cols(w_ff1_g)
    w_ff2 = w_ff2_g.reshape(D_FF, D_MODEL)
    ycat = _att_norm_fwd(o_att, attn_norm_w, ycat_ssd)
    row32, row16, vec32 = ("row", F32), ("row", BF16), ("vec", F32)
    mix, x1, h2 = _matmul_rows(ycat, w_out, _residual_norm_epilogue, [x], [gate1, norm2_w, scale2, shift2],
                               [row32, row32, row16], tm=512, name="out_proj")
    u, act_ff = _matmul(h2, w_ff1, tm=1024, tn=2048, tk=1024, name="ff1", mode="relu2")
    loss, dout, dff, dgate2 = _matmul_rows(act_ff, w_ff2, _loss_epilogue, [x1, tgt], [gate2],
                                           [("one", F32), row32, row16, vec32], tm=512, name="ff2")

    du = _matmul(dff, w_ff2, tb=True, tm=512, tn=4096, tk=1024, out_dtype=BF16, name="ff2_dx", mode="drelu2", u=u)
    g_ff2 = _matmul(act_ff, dff, ta=True, tm=512, tn=1024, tk=4096, out_dtype=BF16, name="ff2_dw")
    dx1, dshift2, dscale2, g_norm2, dmix, dgate1 = _matmul_rows(
        du, w_ff1, _norm_bwd_epilogue, [x1, dout, mix], [norm2_w, scale2, gate1],
        [row32, vec32, vec32, vec32, row16, vec32], tb=True, tm=512, name="ff1_dx")
    g_ff1 = _matmul(h2, du, ta=True, tm=1024, tn=D_FF // N_DEV, tk=4096, out_dtype=BF16, name="ff1_dw", shard_out=True)

    dy_ssd, do, stats, g_attn_norm = _matmul_rows(
        dmix, w_out, _mixer_split_epilogue, [o_att, lse], [attn_norm_w],
        [("row", F32, SSD_D_INNER), ("row", F32, ATT_D), ("row", F32, ATT_D), ("vec", F32, ATT_D)], tb=True, tm=512, name="out_proj_dx")
    g_out = _matmul(ycat, dmix, ta=True, tm=512, tn=1024, tk=4096, out_dtype=BF16, name="out_proj_dw")
    ff_slabs = [g_ff1, g_ff2.reshape(N_DEV, D_FF // N_DEV, D_MODEL)]
    (dq, dk, dv, dqw, dkw), (s_ff1, s_ff2) = _att_bwd(proj, do, stats, qw, kw, comm=_Exchange(ff_slabs, scatter=True))
    out_slabs = [g_out.astype(BF16).reshape(N_DEV, 2 * D_MODEL // N_DEV, D_MODEL)]
    (dz, dact, ddtr, da, g_dsk, g_dtb, g_ssd_norm), (s_out,) = _ssd_bwd(
        dy_ssd, ypre, proj, act, hall, dtb, alog, dsk, ssd_norm_w, comm=_Exchange(out_slabs, scatter=True))
    dxbc, g_conv_w, g_conv_b = _conv_bwd(dact, pre, proj, conv_w)
    dproj = [(dz, OFF_Z), (dxbc, OFF_XBC), (ddtr, OFF_DT), (dq, OFF_Q), (dk, OFF_K), (dv, OFF_V)]
    g_rows = [_matmul(piece, h1, ta=True, tm=min(piece.shape[1], 256), tn=1024, tk=4096, out_dtype=BF16,
                      name=f"in_proj_dw_{off}") for piece, off in dproj]
    g_rows[2] = g_rows[2][:SSD_HEADS]
    in_slabs = jnp.concatenate(g_rows, axis=0).reshape(N_CHIP, 2, IN_W // N_DEV, D_MODEL)
    (sibling_slabs,) = _exchange(None, "swap_w_in_grads", ex=_SiblingSwap([in_slabs]))
    chip_slabs = _chip_sum(lax.dynamic_index_in_dim(in_slabs, core, axis=1, keepdims=False), sibling_slabs)
    (grad_x, dshift1, dscale1, g_norm1), (s_in,) = _matmul_rows(
        dproj, w_in_pt, _norm_bwd_epilogue, [x, dx1], [norm1_w, scale1], [row32, vec32, vec32, vec32],
        tm=256, name="in_proj_dx", comm=_ChipScatter([chip_slabs]))

    dmod = jnp.concatenate([dshift1, dscale1, dgate1, dshift2, dscale2, dgate2], axis=0)
    g_alog = da[:, :SSD_HEADS] * (-jnp.exp(a_log))
    g_qw = dqw.reshape(ATT_HEADS, HEAD_DIM).sum(axis=0, keepdims=True)
    g_kw = dkw.reshape(ATT_HEADS, HEAD_DIM).sum(axis=0, keepdims=True)
    return dict(loss=loss, grad_x=grad_x, dmod=dmod, norm1_w=g_norm1, norm2_w=g_norm2, w_in=s_in, conv_w=g_conv_w,
                conv_b=g_conv_b, dt_bias=g_dtb[:, :SSD_HEADS], a_log=g_alog, d_skip=g_dsk[:, :SSD_HEADS],
                ssd_norm_w=g_ssd_norm, q_norm_w=g_qw, k_norm_w=g_kw, attn_norm_w=g_attn_norm, w_out=s_out,
                w_ff1=s_ff1, w_ff2=s_ff2)


def _pack_w_in_rows(wt_full):
    cut = OFF_DT + SSD_HEADS
    pad = jnp.zeros((LANE - SSD_HEADS, wt_full.shape[1]), wt_full.dtype)
    return jnp.concatenate([wt_full[:cut], pad, wt_full[cut:]], axis=0)


MISC_FIELDS = (("dt_bias", SSD_HEADS), ("a_log", SSD_HEADS), ("d_skip", SSD_HEADS), ("q_norm_w", HEAD_DIM), ("k_norm_w", HEAD_DIM))
SMALL_LAYOUT = (("b_ada", 6), ("norm1_w", 1), ("norm2_w", 1), ("conv_w", 8), ("conv_b", 2), ("ssd_norm_w", 1),
                ("attn_norm_w", 1), ("misc", 1))


def _pack_small(vals):
    rows = []
    for name, nrow in SMALL_LAYOUT:
        if name == "misc":
            misc = jnp.concatenate([vals[f].reshape(1, n) for f, n in MISC_FIELDS], axis=1)
            rows.append(_pad_lanes(misc, D_MODEL))
        elif name in vals:
            rows.append(vals[name].reshape(nrow, D_MODEL))
        else:
            rows.append(jnp.zeros((nrow, D_MODEL), F32))
    used = sum(n for _, n in SMALL_LAYOUT)
    rows.append(jnp.zeros((SMALL_ROWS - used, D_MODEL), F32))
    return jnp.concatenate(rows, axis=0)


def _unpack_small(packed):
    out, r = {}, 0
    for name, nrow in SMALL_LAYOUT:
        blk = packed[r:r + nrow]
        r += nrow
        if name == "misc":
            c0 = 0
            for f, n in MISC_FIELDS:
                out[f] = blk[:, c0:c0 + n]
                c0 += n
        elif name == "b_ada":
            out[name] = blk.reshape(1, N_MOD * D_MODEL)
        elif name == "conv_w":
            out[name] = blk.reshape(CONV_K, CONV_CH)
        elif name == "conv_b":
            out[name] = blk.reshape(1, CONV_CH)
        else:
            out[name] = blk
    return out


WEIGHT_NAMES = ("norm1_w", "norm2_w", "w_ada", "b_ada", "w_in", "conv_w", "conv_b", "dt_bias", "a_log", "d_skip",
                "ssd_norm_w", "q_norm_w", "k_norm_w", "attn_norm_w", "w_out", "w_ff1", "w_ff2")
SMALL_NAMES = ("norm1_w", "norm2_w", "b_ada", "conv_b", "dt_bias", "a_log", "d_skip", "ssd_norm_w", "q_norm_w",
               "k_norm_w", "attn_norm_w")


def kernel(x, c, norm1_w, norm2_w, w_ada, b_ada, w_in, conv_w, conv_b, dt_bias, a_log, d_skip, ssd_norm_w, q_norm_w, k_norm_w, attn_norm_w, w_out, w_ff1, w_ff2, loss_target, m_norm1_w, m_norm2_w, m_w_ada, m_b_ada, m_w_in, m_conv_w, m_conv_b, m_dt_bias, m_a_log, m_d_skip, m_ssd_norm_w, m_q_norm_w, m_k_norm_w, m_attn_norm_w, m_w_out, m_w_ff1, m_w_ff2, v_norm1_w, v_norm2_w, v_w_ada, v_b_ada, v_w_in, v_conv_w, v_conv_b, v_dt_bias, v_a_log, v_d_skip, v_ssd_norm_w, v_q_norm_w, v_k_norm_w, v_attn_norm_w, v_w_out, v_w_ff1, v_w_ff2):
    args = dict(locals())
    w = {n: args[n] for n in WEIGHT_NAMES}
    m = {n: args["m_" + n] for n in WEIGHT_NAMES}
    v = {n: args["v_" + n] for n in WEIGHT_NAMES}
    me = 4 * lax.axis_index("x") + 2 * lax.axis_index("y") + lax.axis_index("c")

    c_rows = jnp.pad(c, ((0, 7), (0, 0)))
    w_in_t, m_in_t, v_in_t = [jnp.transpose(t["w_in"][0]) for t in (w, m, v)]
    c_g, conv_g, w_in_g = _exchange([c_rows, w["conv_w"][0], w_in_t.astype(BF16)], "gather_w_in", scatter=False)
    c_all = c_g[:, 0, :]
    conv_full = _shards_to_cols(conv_g)
    w_in_pt = _pack_w_in_rows(w_in_g.reshape(IN_W, D_MODEL))

    mod_part = _ada_fwd(c_all, w["w_ada"][0])
    (mod_g,) = _exchange([mod_part], "gather_mod", scatter=False)
    mod_mine = lax.dynamic_index_in_dim(mod_g, me, axis=1, keepdims=False).reshape(1, N_MOD * D_MODEL) + w["b_ada"]
    mod = mod_mine.reshape(N_MOD, D_MODEL)

    res = _local_step(x[0], loss_target[0], mod, w_in_pt, conv_full, w["conv_b"], w["dt_bias"], w["a_log"], w["d_skip"],
                      w["ssd_norm_w"], w["q_norm_w"], w["k_norm_w"], w["attn_norm_w"], w["w_out"][0].astype(BF16),
                      w["w_ff1"][0].astype(BF16), w["w_ff2"][0].astype(BF16), w["norm1_w"], w["norm2_w"], lax.axis_index("c"))

    small_vals = {n: res[n] for n in SMALL_NAMES if n != "b_ada"}
    small_vals["b_ada"] = res["dmod"]
    small_vals["conv_w"] = res["conv_w"]
    (small_g,) = _exchange([_pack_small(small_vals)], "gather_small", scatter=False)

    grads, delta, new_m, new_v = {}, {}, {}, {}
    for name in ("w_out", "w_ff1", "w_ff2"):
        outs = _reduce_adamw(res[name], w[name][0], m[name][0], v[name][0], "adamw_" + name)
        grads[name], delta[name], new_m[name], new_v[name] = [o[None] for o in outs]
    outs = _reduce_adamw(res["w_in"], w_in_t, m_in_t, v_in_t, "adamw_w_in")
    grads["w_in"], delta["w_in"], new_m["w_in"], new_v["w_in"] = [jnp.transpose(o)[None] for o in outs]

    sm = _small_reduce_adamw(small_g, _pack_small({n: w[n] for n in SMALL_NAMES}), _pack_small({n: m[n] for n in SMALL_NAMES}),
                             _pack_small({n: v[n] for n in SMALL_NAMES}))
    sm = [_unpack_small(p) for p in sm]
    for n in SMALL_NAMES:
        grads[n], delta[n], new_m[n], new_v[n] = [p[n] for p in sm]
    shard_w = CONV_CH // N_DEV
    g_conv = lax.dynamic_slice_in_dim(sm[0]["conv_w"], me * shard_w, shard_w, axis=1)
    cw = _adamw_small(g_conv, w["conv_w"][0], m["conv_w"][0], v["conv_w"][0], "adamw_conv_w")
    grads["conv_w"] = g_conv[None]
    delta["conv_w"], new_m["conv_w"], new_v["conv_w"] = [o[None] for o in cw]

    ada_w = w_ada.shape[2]
    dmod_all = small_g[:, :N_MOD, :].reshape(N_DEV, N_MOD * D_MODEL)
    dmod_cols = lax.dynamic_slice_in_dim(dmod_all, me * ada_w, ada_w, axis=1)
    outs = _ada_bwd_adamw(c_all, dmod_cols, w["w_ada"][0], m["w_ada"][0], v["w_ada"][0])
    grads["w_ada"], delta["w_ada"], new_m["w_ada"], new_v["w_ada"] = [o[None] for o in outs]

    loss = lax.psum(res["loss"][0, 0], ("x", "y", "c"))
    return (loss, res["grad_x"][None], *[grads[n] for n in WEIGHT_NAMES], *[delta[n] for n in WEIGHT_NAMES],
            *[new_m[n] for n in WEIGHT_NAMES], *[new_v[n] for n in WEIGHT_NAMES])
```

```python
import functools

import jax
import jax.numpy as jnp
from jax import lax
from jax.experimental import pallas as pl
from jax.experimental.pallas import tpu as pltpu

F32 = jnp.float32
BF16 = jnp.bfloat16
HIGHEST = lax.Precision.HIGHEST
MESH_IDS = pl.DeviceIdType.MESH

N_DEV = 8
D_MODEL = 1024
HEAD_DIM = 64
SSD_HEADS = 16
SSD_GROUPS = 4
HEADS_PER_GROUP = SSD_HEADS // SSD_GROUPS
SSD_STATE = 128
SSD_CHUNK = 128
SSD_D_INNER = SSD_HEADS * HEAD_DIM
GROUP_WIDTH = SSD_D_INNER // SSD_GROUPS
CONV_K = 4
CONV_CH = SSD_D_INNER + 2 * SSD_GROUPS * SSD_STATE
ATT_HEADS = 16
ATT_D = ATT_HEADS * HEAD_DIM
ATT_BLK = 128
DILATIONS = (1, 4, 16)
D_FF = 4 * D_MODEL
N_MOD = 6
EPS = 1e-6
IN_W = SSD_D_INNER + CONV_CH + SSD_HEADS + 3 * ATT_D
LANE = 128
OFF_Z, OFF_XBC, OFF_DT = 0, SSD_D_INNER, SSD_D_INNER + CONV_CH
OFF_Q = OFF_DT + LANE
OFF_K, OFF_V = OFF_Q + ATT_D, OFF_Q + 2 * ATT_D
IN_WP = OFF_V + ATT_D

ADAM_LR, ADAM_B1, ADAM_B2, ADAM_EPS, ADAM_WD, ADAM_STEP = 0.001, 0.9, 0.999, 1e-08, 0.01, 10
VMEM_LIMIT = 56 * 1024 * 1024
ROW_TILE = 512
SMALL_ROWS = 24


def _cparams(sem=None):
    return pltpu.CompilerParams(dimension_semantics=sem, vmem_limit_bytes=VMEM_LIMIT)


def _sigmoid(v):
    return 1.0 / (1.0 + jnp.exp(-v))


def _softplus(v):
    y = jnp.exp(-jnp.abs(v))
    small = y * (1.0 - y * (0.5 - y * (1.0 / 3.0)))
    return jnp.maximum(v, 0.0) + jnp.where(y < 0.01, small, jnp.log(1.0 + y))


def _dot(a, b, dims, precision=None):
    return lax.dot_general(a, b, (dims, ((), ())), preferred_element_type=F32, precision=precision)


NN = ((1,), (0,))
NT = ((1,), (1,))
TN = ((0,), (0,))


def _matmul(a, b, *, ta=False, tb=False, tm, tn, tk, out_dtype=F32, name, mode=None, u=None, comm=None, shard_out=False):
    m, k = (a.shape[1], a.shape[0]) if ta else a.shape
    n = b.shape[0] if tb else b.shape[1]
    assert m % tm == 0 and n % tn == 0 and k % tk == 0, (name, m, n, k)
    nk = k // tk
    a_spec = pl.BlockSpec((tk, tm), lambda i, j, kk: (kk, i)) if ta else pl.BlockSpec((tm, tk), lambda i, j, kk: (i, kk))
    b_spec = pl.BlockSpec((tn, tk), lambda i, j, kk: (j, kk)) if tb else pl.BlockSpec((tk, tn), lambda i, j, kk: (kk, j))
    o_spec = pl.BlockSpec((tm, tn), lambda i, j, kk: (i, j))
    dims = ((0,) if ta else (1,), (1,) if tb else (0,))
    n_out = 2 if mode == "relu2" else 1

    def body(*refs):
        if mode == "drelu2":
            a_ref, b_ref, u_ref = refs[:3]
            rest = refs[3:]
        else:
            a_ref, b_ref = refs[:2]
            u_ref = None
            rest = refs[2:]
        outs = rest[:n_out]
        part = _dot(a_ref[...], b_ref[...], dims)

        def finish(r):
            if mode == "relu2":
                outs[0][...] = r.astype(BF16)
                rr = jnp.maximum(r, 0.0)
                outs[1][...] = (rr * rr).astype(BF16)
            elif mode == "drelu2":
                outs[0][...] = (r * (2.0 * jnp.maximum(u_ref[...].astype(F32), 0.0))).astype(out_dtype)
            else:
                outs[0][...] = r.astype(out_dtype)

        if nk == 1:
            finish(part)
        else:
            acc = rest[n_out]
            kk = pl.program_id(2)

            @pl.when(kk == 0)
            def _():
                acc[...] = part

            @pl.when(kk > 0)
            def _():
                acc[...] += part

            @pl.when(kk == nk - 1)
            def _():
                finish(acc[...])

    in_specs = [a_spec, b_spec]
    args = [a, b]
    if mode == "drelu2":
        in_specs.append(o_spec)
        args.append(u)
    if mode == "relu2":
        out_shape = [jax.ShapeDtypeStruct((m, n), BF16), jax.ShapeDtypeStruct((m, n), BF16)]
    elif shard_out:
        out_shape = [jax.ShapeDtypeStruct((n // tn, m, tn), out_dtype)]
        o_spec = pl.BlockSpec((None, tm, tn), lambda i, j, kk: (j, i, 0))
    else:
        out_shape = [jax.ShapeDtypeStruct((m, n), out_dtype)]
    outs, comm_outs = _pcall(
        body, args, name=name, grid=(m // tm, n // tn, nk), in_specs=in_specs, out_specs=[o_spec] * n_out,
        out_shape=out_shape, scratch_shapes=[pltpu.VMEM((tm, tn), F32)] if nk > 1 else [],
        sem=("parallel", "parallel", "arbitrary"), comm=comm)
    res = tuple(outs) if mode == "relu2" else outs[0]
    return res if comm is None else (res, comm_outs)


def _pieces_t_matmul(pieces, b, *, tm, name):
    k, n = b.shape
    starts, tiles = [], 0
    for p in pieces:
        assert p.shape[0] == k and p.shape[1] % tm == 0, (name, p.shape)
        starts.append(tiles)
        tiles += p.shape[1] // tm

    def piece_spec(start, count):
        return pl.BlockSpec((k, tm), lambda i: (0, jnp.clip(i - start, 0, count - 1)))

    def body(*refs):
        a_refs, b_ref, o_ref = refs[:len(pieces)], refs[len(pieces)], refs[len(pieces) + 1]
        i = pl.program_id(0)
        for a_ref, start, p in zip(a_refs, starts, pieces):
            @pl.when((i >= start) & (i < start + p.shape[1] // tm))
            def _(a_ref=a_ref):
                o_ref[...] = _dot(a_ref[...], b_ref[...], TN).astype(BF16)

    return pl.pallas_call(
        body, name=name, grid=(tiles,),
        in_specs=[piece_spec(s0, p.shape[1] // tm) for s0, p in zip(starts, pieces)] + [pl.BlockSpec((k, n), lambda i: (0, 0))],
        out_specs=pl.BlockSpec((tm, n), lambda i: (i, 0)), out_shape=jax.ShapeDtypeStruct((tiles * tm, n), BF16),
        compiler_params=_cparams(("arbitrary",)))(*pieces, b)


def _rms_mod(xv, nw, scale, shift):
    r = lax.rsqrt(jnp.mean(xv * xv, axis=-1, keepdims=True) + EPS)
    return ((xv * r) * nw * (1.0 + scale) + shift).astype(BF16)


def _norm_mod_fwd(x, nw, scale, shift, name):
    s, d = x.shape
    row = pl.BlockSpec((ROW_TILE, d), lambda i: (i, 0))
    vec = pl.BlockSpec((1, d), lambda i: (0, 0))

    def body(x_ref, nw_ref, sc_ref, sh_ref, h_ref):
        h_ref[...] = _rms_mod(x_ref[...], nw_ref[...], sc_ref[...], sh_ref[...])

    return pl.pallas_call(body, name=name, grid=(s // ROW_TILE,), in_specs=[row, vec, vec, vec], out_specs=row,
                          out_shape=jax.ShapeDtypeStruct((s, d), BF16), compiler_params=_cparams(("parallel",)))(x, nw, scale, shift)


def _matmul_rows(a, b, epilogue, row_in, vec_in, outs, *, tb=False, tm, name, comm=None):
    pieces = a if isinstance(a, list) else [(a, 0)]
    assert not (tb and len(pieces) > 1)
    m = pieces[0][0].shape[0]
    n = b.shape[0] if tb else b.shape[1]
    assert m % tm == 0, (name, m, tm)
    dims = ((1,), (1,) if tb else (0,))
    n_a, n_row, n_vec = len(pieces), len(row_in), len(vec_in)

    def body(*refs):
        a_refs, b_ref, rest = refs[:n_a], refs[n_a], refs[n_a + 1:]
        if n_a == 1:
            c = _dot(a_refs[0][...], b_ref[...], dims)
        else:
            c = None
            for a_ref, (piece, off) in zip(a_refs, pieces):
                part = _dot(a_ref[...], b_ref[off:off + piece.shape[1], :], dims)
                c = part if c is None else c + part
        epilogue(c, pl.program_id(0) == 0, rest[:n_row], rest[n_row:n_row + n_vec], rest[n_row + n_vec:])

    def spec(kind, width):
        block = {"row": (tm, width), "vec": (1, width), "one": (1, 1)}[kind]
        return pl.BlockSpec(block, (lambda i: (i, 0)) if kind == "row" else (lambda i: (0, 0)))

    def shape(kind, width):
        return {"row": (m, width), "vec": (1, width), "one": (1, 1)}[kind]

    outs = [(o[0], o[1], o[2] if len(o) > 2 else n) for o in outs]
    res, comm_outs = _pcall(
        body, [*[p for p, _ in pieces], b, *row_in, *vec_in], name=name, grid=(m // tm,),
        in_specs=[spec("row", p.shape[1]) for p, _ in pieces] + [pl.BlockSpec(b.shape, lambda i: (0, 0))]
        + [spec("row", r.shape[1]) for r in row_in] + [spec("vec", v.shape[1]) for v in vec_in],
        out_specs=[spec(kind, width) for kind, _, width in outs],
        out_shape=[jax.ShapeDtypeStruct(shape(kind, width), dt) for kind, dt, width in outs],
        sem=("arbitrary",), comm=comm)
    return res if comm is None else (res, comm_outs)


def _residual_norm_epilogue(mix, first, rows, vecs, outs):
    (x_ref,), (gate_ref, nw_ref, sc_ref, sh_ref), (mix_ref, x1_ref, h_ref) = rows, vecs, outs
    xv = x_ref[...] + gate_ref[...] * mix
    mix_ref[...] = mix
    x1_ref[...] = xv
    h_ref[...] = _rms_mod(xv, nw_ref[...], sc_ref[...], sh_ref[...])


def _loss_epilogue(ff, first, rows, vecs, outs):
    (x1_ref, t_ref), (g_ref,), (loss_ref, dout_ref, dff_ref, dg_ref) = rows, vecs, outs
    d = ff.shape[1]

    @pl.when(first)
    def _():
        loss_ref[...] = jnp.zeros_like(loss_ref)
        dg_ref[...] = jnp.zeros_like(dg_ref)

    err = x1_ref[...] + g_ref[...] * ff - t_ref[...]
    loss_ref[...] += (0.5 / d) * jnp.sum(err * err).reshape(1, 1)
    dout = err * (1.0 / d)
    dout_ref[...] = dout
    dff_ref[...] = (g_ref[...] * dout).astype(BF16)
    dg_ref[...] += jnp.sum(dout * ff, axis=0, keepdims=True)


def _norm_bwd_epilogue(dh, first, rows, vecs, outs):
    with_gate = len(vecs) == 3
    x_ref, dres_ref = rows[:2]
    nw_ref, sc_ref = vecs[:2]
    dx_ref, dsh_ref, dsc_ref, dnw_ref = outs[:4]

    @pl.when(first)
    def _():
        for ref in outs[1:4] + outs[5:]:
            ref[...] = jnp.zeros_like(ref)

    xv = x_ref[...]
    r = lax.rsqrt(jnp.mean(xv * xv, axis=-1, keepdims=True) + EPS)
    nrm = xv * r
    one_sc = 1.0 + sc_ref[...]
    dhn = dh * nrm
    dsh_ref[...] += jnp.sum(dh, axis=0, keepdims=True)
    dsc_ref[...] += jnp.sum(dhn, axis=0, keepdims=True) * nw_ref[...]
    dnw_ref[...] += jnp.sum(dhn, axis=0, keepdims=True) * one_sc
    dn = dh * (nw_ref[...] * one_sc)
    dx = dres_ref[...] + r * (dn - nrm * jnp.mean(dn * nrm, axis=-1, keepdims=True))
    dx_ref[...] = dx
    if with_gate:
        outs[4][...] = (vecs[2][...] * dx).astype(BF16)
        outs[5][...] += jnp.sum(dx * rows[2][...], axis=0, keepdims=True)


CONV_COLS = 256
CONV_FWD_ROWS = 2048
CONV_BWD_ROWS = 1024
CONV_SUB_ROWS = 128
HALO = 8


def _shift_down(cur, halo, k):
    if k == 0:
        return cur
    rolled = pltpu.roll(cur, k, axis=0)
    top = jnp.where(lax.broadcasted_iota(jnp.int32, halo.shape, 0) < k, pltpu.roll(halo, k, axis=0), rolled[:HALO])
    return jnp.concatenate([top, rolled[HALO:]], axis=0)


def _shift_up(cur, halo, k):
    if k == 0:
        return cur
    t = cur.shape[0]
    rolled = pltpu.roll(cur, t - k, axis=0)
    bot = jnp.where(lax.broadcasted_iota(jnp.int32, halo.shape, 0) >= HALO - k, pltpu.roll(halo, HALO - k, axis=0),
                    rolled[t - HALO:])
    return jnp.concatenate([rolled[:t - HALO], bot], axis=0)


def _conv_fwd(proj, conv_w, conv_b):
    s = proj.shape[0]
    nr = s // CONV_FWD_ROWS
    cb0 = OFF_XBC // CONV_COLS
    hb = CONV_FWD_ROWS // HALO
    cur = pl.BlockSpec((CONV_FWD_ROWS, CONV_COLS), lambda j, r: (r, cb0 + j))
    prev = pl.BlockSpec((HALO, CONV_COLS), lambda j, r: (jnp.maximum(r * hb - 1, 0), cb0 + j))
    out = pl.BlockSpec((CONV_FWD_ROWS, CONV_COLS), lambda j, r: (r, j))

    def body(u_ref, up_ref, w_ref, b_ref, pre_ref, act_ref):
        r = pl.program_id(1)
        u = u_ref[...]
        halo = jnp.where(r > 0, up_ref[...], 0.0)
        acc = b_ref[...] + w_ref[CONV_K - 1:CONV_K, :] * u
        for k in range(1, CONV_K):
            acc = acc + w_ref[CONV_K - 1 - k:CONV_K - k, :] * _shift_down(u, halo, k)
        pre_ref[...] = acc
        act_ref[...] = acc * _sigmoid(acc)

    return pl.pallas_call(
        body, name="conv_fwd", grid=(CONV_CH // CONV_COLS, nr),
        in_specs=[cur, prev, pl.BlockSpec((CONV_K, CONV_COLS), lambda j, r: (0, j)),
                  pl.BlockSpec((1, CONV_COLS), lambda j, r: (0, j))],
        out_specs=[out, out],
        out_shape=[jax.ShapeDtypeStruct((s, CONV_CH), F32), jax.ShapeDtypeStruct((s, CONV_CH), F32)],
        compiler_params=_cparams(("parallel", "arbitrary")))(proj, proj, conv_w, conv_b)


def _conv_bwd(dact, pre, proj, conv_w):
    s = proj.shape[0]
    nr = s // CONV_BWD_ROWS
    cb0 = OFF_XBC // CONV_COLS
    hb = CONV_BWD_ROWS // HALO
    last_halo = s // HALO - 1
    n_sub = CONV_BWD_ROWS // CONV_SUB_ROWS
    cur = pl.BlockSpec((CONV_BWD_ROWS, CONV_COLS), lambda j, r: (r, j))
    nxt = pl.BlockSpec((HALO, CONV_COLS), lambda j, r: (jnp.minimum((r + 1) * hb, last_halo), j))
    ucur = pl.BlockSpec((CONV_BWD_ROWS, CONV_COLS), lambda j, r: (r, cb0 + j))
    wspec = pl.BlockSpec((CONV_K, CONV_COLS), lambda j, r: (0, j))
    bspec = pl.BlockSpec((1, CONV_COLS), lambda j, r: (0, j))

    def dsilu(p):
        sg = _sigmoid(p)
        return sg * (1.0 + p * (1.0 - sg))

    def body(da_ref, dan_ref, pre_ref, pren_ref, u_ref, w_ref, du_ref, dw_ref, db_ref):
        r = pl.program_id(1)

        @pl.when(r == 0)
        def _():
            dw_ref[...] = jnp.zeros_like(dw_ref)
            db_ref[...] = jnp.zeros_like(db_ref)

        dws = [jnp.zeros((1, CONV_COLS), F32) for _ in range(CONV_K)]
        db = jnp.zeros((1, CONV_COLS), F32)
        for c in range(n_sub):
            rows = slice(c * CONV_SUB_ROWS, (c + 1) * CONV_SUB_ROWS)
            ahead = slice((c + 1) * CONV_SUB_ROWS, (c + 1) * CONV_SUB_ROWS + HALO)
            dpre = da_ref[rows, :] * dsilu(pre_ref[rows, :])
            if c < n_sub - 1:
                dnext = da_ref[ahead, :] * dsilu(pre_ref[ahead, :])
            else:
                dnext = jnp.where(r < nr - 1, dan_ref[...] * dsilu(pren_ref[...]), 0.0)
            u = u_ref[rows, :]
            du = w_ref[CONV_K - 1:CONV_K, :] * dpre
            dws[0] = dws[0] + jnp.sum(dpre * u, axis=0, keepdims=True)
            for k in range(1, CONV_K):
                ahead_k = _shift_up(dpre, dnext, k)
                du = du + w_ref[CONV_K - 1 - k:CONV_K - k, :] * ahead_k
                dws[k] = dws[k] + jnp.sum(ahead_k * u, axis=0, keepdims=True)
            du_ref[rows, :] = du.astype(BF16)
            db = db + jnp.sum(dpre, axis=0, keepdims=True)
        dw_ref[...] += jnp.concatenate(dws[::-1], axis=0)
        db_ref[...] += db

    return pl.pallas_call(
        body, name="conv_bwd", grid=(CONV_CH // CONV_COLS, nr),
        in_specs=[cur, nxt, cur, nxt, ucur, wspec],
        out_specs=[cur, wspec, bspec],
        out_shape=[jax.ShapeDtypeStruct((s, CONV_CH), BF16), jax.ShapeDtypeStruct((CONV_K, CONV_CH), F32),
                   jax.ShapeDtypeStruct((1, CONV_CH), F32)],
        compiler_params=_cparams(("parallel", "arbitrary")))(dact, dact, pre, pre, proj, conv_w)


def _ssd_common(dtr, dtb, alog):
    lane = lax.broadcasted_iota(jnp.int32, (1, LANE), 1)
    head_lane = lane < SSD_HEADS
    dt = jnp.where(head_lane, _softplus(dtr + dtb), 0.0)
    a = jnp.where(head_lane, -jnp.exp(alog), 0.0)
    row = lax.broadcasted_iota(jnp.int32, (SSD_CHUNK, SSD_CHUNK), 0)
    col = lax.broadcasted_iota(jnp.int32, (SSD_CHUNK, SSD_CHUNK), 1)
    tril = row >= col
    cs = _dot(tril.astype(F32), dt * a, NN, precision=HIGHEST)
    return dt, a, cs, cs.T, tril, lane


def _split_bf16(v, passes):
    terms, rest = [], v
    for _ in range(passes):
        t = rest.astype(BF16)
        terms.append(t)
        rest = rest - t.astype(F32)
    return terms


def _dot_split(v, m, dims, passes):
    terms = _split_bf16(v, passes)
    if passes == 1:
        return _dot(terms[0], m, dims)
    return _dot(jnp.concatenate(terms, axis=1), jnp.concatenate([m] * passes, axis=0 if dims == NN else 1), dims)


def _ssd_constants():
    heads = jnp.arange(LANE)[:, None]
    exp_mat = (heads == (jnp.arange(SSD_D_INNER)[None, :] // HEAD_DIM)).astype(BF16)
    ind4 = ((jnp.arange(SSD_HEADS * SSD_CHUNK)[:, None] // SSD_CHUNK) == jnp.arange(LANE)[None, :]).astype(BF16)
    return exp_mat, ind4


def _expand_heads(v):
    return jnp.repeat(v[:, :SSD_HEADS], HEAD_DIM, axis=1)


def _ssd_prep(dtr, dtb, alog, exp_mat):
    dt, a, cs, cst, tril, lane = _ssd_common(dtr, dtb, alog)
    return dt, a, cs, cst, tril, lane, _dot_split(dt, exp_mat, NN, 2), _dot_split(cs, exp_mat, NN, 3)


def _chunk_decay_rows(cs, g):
    parts = []
    for e in range(HEADS_PER_GROUP):
        h = g * HEADS_PER_GROUP + e
        parts.append(jnp.broadcast_to(jnp.exp(cs[SSD_CHUNK - 1:SSD_CHUNK, h:h + 1]), (HEAD_DIM, SSD_STATE)))
    return jnp.concatenate(parts, axis=0)


def _ssd_fwd(proj, act, dtb, alog, dsk, nw):
    s = proj.shape[0]
    nc = s // SSD_CHUNK
    bc_w = SSD_GROUPS * SSD_STATE
    exp_mat, _ = _ssd_constants()

    def body(z_ref, dtr_ref, xs_ref, b_ref, c_ref, dtb_ref, alog_ref, dskx_ref, nw_ref, exp_ref,
             ypre_ref, yssd_ref, hall_ref, h_scr):
        @pl.when(pl.program_id(0) == 0)
        def _():
            h_scr[...] = jnp.zeros_like(h_scr)

        dt, a, cs, cst, tril, lane, dtx, csx = _ssd_prep(dtr_ref[...], dtb_ref[...], alog_ref[...], exp_ref[...])
        cs_last_x = csx[SSD_CHUNK - 1:SSD_CHUNK, :]
        xs = xs_ref[...]
        xdt = xs * dtx
        xdtb = xdt.astype(BF16)
        xdec = (xdt * jnp.exp(cs_last_x - csx)).astype(BF16)
        ecsx = jnp.exp(csx)
        head_of_lane = lax.broadcasted_iota(jnp.int32, (1, GROUP_WIDTH), 1) // HEAD_DIM
        for g in range(SSD_GROUPS):
            gs = slice(g * GROUP_WIDTH, (g + 1) * GROUP_WIDTH)
            bg = b_ref[:, g * SSD_STATE:(g + 1) * SSD_STATE].astype(BF16)
            cg = c_ref[:, g * SSD_STATE:(g + 1) * SSD_STATE].astype(BF16)
            cb = _dot(cg, bg, NT)
            hprev = h_scr[gs, :]
            hall_ref[0, gs, :] = hprev
            gms, rhs = [], []
            xg = xdtb[:, gs]
            for e in range(HEADS_PER_GROUP):
                h = g * HEADS_PER_GROUP + e
                lm = jnp.exp(jnp.where(tril, cs[:, h:h + 1] - cst[h:h + 1, :], -1e30))
                gms.append((cb * lm).astype(BF16))
                rhs.append(jnp.where(head_of_lane == e, xg, jnp.zeros_like(xg)))
            y = _dot(jnp.concatenate(gms, axis=1), jnp.concatenate(rhs, axis=0), NN)
            y = y + ecsx[:, gs] * _dot(cg, hprev.astype(BF16), NT)
            y = y + dskx_ref[:, gs] * xs[:, gs]
            h_scr[gs, :] = hprev * _chunk_decay_rows(cs, g) + _dot(xdec[:, gs], bg, TN)
            ypre_ref[:, gs] = y
            z = z_ref[:, gs]
            yg = y * (z * _sigmoid(z))
            r = lax.rsqrt(jnp.mean(yg * yg, axis=-1, keepdims=True) + EPS)
            yssd_ref[:, gs] = (yg * r * nw_ref[:, gs]).astype(BF16)

    row_d = lambda cb: pl.BlockSpec((SSD_CHUNK, SSD_D_INNER), lambda c: (c, cb))
    small = pl.BlockSpec((1, LANE), lambda c: (0, 0))
    wide = pl.BlockSpec((1, SSD_D_INNER), lambda c: (0, 0))
    return pl.pallas_call(
        body, name="ssd_fwd", grid=(nc,),
        in_specs=[row_d(OFF_Z // SSD_D_INNER),
                  pl.BlockSpec((SSD_CHUNK, LANE), lambda c: (c, OFF_DT // LANE)),
                  row_d(0),
                  pl.BlockSpec((SSD_CHUNK, bc_w), lambda c: (c, SSD_D_INNER // bc_w)),
                  pl.BlockSpec((SSD_CHUNK, bc_w), lambda c: (c, SSD_D_INNER // bc_w + 1)),
                  small, small, wide, wide, pl.BlockSpec((LANE, SSD_D_INNER), lambda c: (0, 0))],
        out_specs=[row_d(0), row_d(0), pl.BlockSpec((1, SSD_D_INNER, SSD_STATE), lambda c: (c, 0, 0))],
        out_shape=[jax.ShapeDtypeStruct((s, SSD_D_INNER), F32), jax.ShapeDtypeStruct((s, SSD_D_INNER + ATT_D), BF16),
                   jax.ShapeDtypeStruct((nc, SSD_D_INNER, SSD_STATE), F32)],
        scratch_shapes=[pltpu.VMEM((SSD_D_INNER, SSD_STATE), F32)],
        compiler_params=_cparams(("arbitrary",)))(proj, proj, act, act, act, dtb, alog, _expand_heads(dsk), nw, exp_mat)


def _ssd_bwd(dycat, ypre, proj, act, hall, dtb, alog, dsk, nw, comm=None):
    s = proj.shape[0]
    nc = s // SSD_CHUNK
    bc_w = SSD_GROUPS * SSD_STATE

    exp_mat, ind4 = _ssd_constants()
    seg_passes = 1

    def body(dy_ref, ypre_ref, z_ref, dtr_ref, xs_ref, b_ref, c_ref, hall_ref, dtb_ref, alog_ref, dskx_ref, nw_ref,
             exp_ref, ind4_ref, dz_ref, dact_ref, ddtr_ref, da_ref, ddsk_ref, ddtb_ref, dnw_ref, dh_scr):
        @pl.when(pl.program_id(0) == 0)
        def _():
            dh_scr[...] = jnp.zeros_like(dh_scr)
            da_ref[...] = jnp.zeros_like(da_ref)
            ddsk_ref[...] = jnp.zeros_like(ddsk_ref)
            ddtb_ref[...] = jnp.zeros_like(ddtb_ref)
            dnw_ref[...] = jnp.zeros_like(dnw_ref)

        dtr = dtr_ref[...]
        dt, a, cs, cst, tril, lane, dtx, csx = _ssd_prep(dtr, dtb_ref[...], alog_ref[...], exp_ref[...])
        cs_last_x = csx[SSD_CHUNK - 1:SSD_CHUNK, :]
        xs = xs_ref[...]
        xdt = xs * dtx
        xdtb = xdt.astype(BF16)
        decx = jnp.exp(cs_last_x - csx)
        xdecf = xdt * decx
        xdec = xdecf.astype(BF16)
        ecsx = jnp.exp(csx)
        head_of_lane = lax.broadcasted_iota(jnp.int32, (1, GROUP_WIDTH), 1) // HEAD_DIM
        last_row = lax.broadcasted_iota(jnp.int32, (SSD_CHUNK, 1), 0) == SSD_CHUNK - 1
        dcs_col = jnp.zeros((SSD_CHUNK, LANE), F32)
        dcs_row = jnp.zeros((SSD_CHUNK, LANE), F32)
        ddt = jnp.zeros((SSD_CHUNK, LANE), F32)
        ddsk = jnp.zeros((1, LANE), F32)
        hsum = jnp.zeros((1, LANE), F32)
        t1_sum = jnp.zeros((1, LANE), F32)
        for g in range(SSD_GROUPS):
            gs = slice(g * GROUP_WIDTH, (g + 1) * GROUP_WIDTH)
            bsl = slice(g * SSD_STATE, (g + 1) * SSD_STATE)
            exp_g = exp_ref[:, gs]
            ind4_g = ind4_ref[g * HEADS_PER_GROUP * SSD_CHUNK:(g + 1) * HEADS_PER_GROUP * SSD_CHUNK, :]
            z = z_ref[:, gs]
            sg = _sigmoid(z)
            sz = z * sg
            ypre = ypre_ref[:, gs]
            yg = ypre * sz
            r = lax.rsqrt(jnp.mean(yg * yg, axis=-1, keepdims=True) + EPS)
            nrm = yg * r
            dyo_n = dy_ref[:, gs]
            dnw_ref[:, gs] += jnp.sum(dyo_n * nrm, axis=0, keepdims=True)
            dn = dyo_n * nw_ref[:, gs]
            dyg = r * (dn - nrm * jnp.mean(dn * nrm, axis=-1, keepdims=True))
            dz_ref[:, gs] = (dyg * ypre * (sg * (1.0 + z * (1.0 - sg)))).astype(BF16)
            dy = dyg * sz

            bg = b_ref[:, bsl].astype(BF16)
            cg = c_ref[:, bsl].astype(BF16)
            cb = _dot(cg, bg, NT)
            hprev = hall_ref[0, gs, :]
            hb = hprev.astype(BF16)
            dhn = dh_scr[gs, :]
            dhb = dhn.astype(BF16)
            xs_g, xdt_g = xs[:, gs], xdtb[:, gs]
            w_off = _dot(cg, hb, NT)
            dyo = dy * ecsx[:, gs]
            dyob = dyo.astype(BF16)
            dcg = _dot(dyob, hb, NN)
            dh_y = _dot(dyob, cg, TN)
            r_st = _dot(bg, dhb, NT)
            dbg = _dot(xdec[:, gs], dhb, NN)
            dyb = dy.astype(BF16)
            gms, gmbs, lms, dys = [], [], [], []
            for e in range(HEADS_PER_GROUP):
                h = g * HEADS_PER_GROUP + e
                lm = jnp.exp(jnp.where(tril, cs[:, h:h + 1] - cst[h:h + 1, :], -1e30))
                gm = cb * lm
                lms.append(lm)
                gms.append(gm)
                gmbs.append(gm.astype(BF16))
                dys.append(jnp.where(head_of_lane == e, dyb, jnp.zeros_like(dyb)))
            dxdt = _dot(jnp.concatenate(gmbs, axis=0), jnp.concatenate(dys, axis=0), TN) + decx[:, gs] * r_st
            dcb = jnp.zeros((SSD_CHUNK, SSD_CHUNK), F32)
            mms = []
            for e in range(HEADS_PER_GROUP):
                dg = _dot(dys[e], xdt_g, NT)
                mms.append(dg * gms[e])
                dcb = dcb + dg * lms[e]
            seg = _dot_split(jnp.concatenate([dyo * w_off, xdecf[:, gs] * r_st, dxdt * xs_g, dy * xs_g], axis=0), exp_g, NT, seg_passes)
            v1, t1, ddt_g, dsk_g = [seg[i * SSD_CHUNK:(i + 1) * SSD_CHUNK] for i in range(4)]
            dcs_col = dcs_col + v1 - t1 + _dot_split(jnp.concatenate(mms, axis=1), ind4_g, NN, seg_passes)
            for t in _split_bf16(jnp.concatenate(mms, axis=0), seg_passes):
                dcs_row = dcs_row + _dot(ind4_g, t, TN)
            ddt = ddt + ddt_g
            ddsk = ddsk + jnp.sum(dsk_g, axis=0, keepdims=True)
            t1_sum = t1_sum + jnp.sum(t1, axis=0, keepdims=True)
            for e in range(HEADS_PER_GROUP):
                h = g * HEADS_PER_GROUP + e
                hs = slice(e * HEAD_DIM, (e + 1) * HEAD_DIM)
                hsum = hsum + jnp.where(lane == h, jnp.sum(dhn[hs, :] * hprev[hs, :]).reshape(1, 1), 0.0)
            dh_scr[gs, :] = dhn * _chunk_decay_rows(cs, g) + dh_y
            dcbb = dcb.astype(BF16)
            dact_ref[:, gs] = dxdt * dtx[:, gs] + dskx_ref[:, gs] * dy
            dact_ref[:, SSD_D_INNER + g * SSD_STATE:SSD_D_INNER + (g + 1) * SSD_STATE] = dbg + _dot(dcbb, cg, TN)
            dact_ref[:, SSD_D_INNER + bc_w + g * SSD_STATE:SSD_D_INNER + bc_w + (g + 1) * SSD_STATE] = dcg + _dot(dcbb, bg, NN)
        dlast = t1_sum + jnp.exp(cs[SSD_CHUNK - 1:SSD_CHUNK, :]) * hsum
        dcs = dcs_col - dcs_row.T + jnp.where(last_row, dlast, 0.0)
        row = lax.broadcasted_iota(jnp.int32, (SSD_CHUNK, SSD_CHUNK), 0)
        col = lax.broadcasted_iota(jnp.int32, (SSD_CHUNK, SSD_CHUNK), 1)
        dda = _dot((col >= row).astype(F32), dcs, NN, precision=HIGHEST)
        ddt = ddt + dda * a
        da_ref[...] += jnp.sum(dda * dt, axis=0, keepdims=True)
        ddtr = jnp.where(lane < SSD_HEADS, ddt * _sigmoid(dtr + dtb_ref[...]), 0.0)
        ddtr_ref[...] = ddtr.astype(BF16)
        ddtb_ref[...] += jnp.sum(ddtr, axis=0, keepdims=True)
        ddsk_ref[...] += ddsk

    rev = lambda c: nc - 1 - c
    row_d = lambda cb: pl.BlockSpec((SSD_CHUNK, SSD_D_INNER), lambda c: (rev(c), cb))
    small = pl.BlockSpec((1, LANE), lambda c: (0, 0))
    wide = pl.BlockSpec((1, SSD_D_INNER), lambda c: (0, 0))
    small_shape = jax.ShapeDtypeStruct((1, LANE), F32)
    return _pcall(
        body, (dycat, ypre, proj, proj, act, act, act, hall, dtb, alog, _expand_heads(dsk), nw, exp_mat, ind4),
        name="ssd_bwd", grid=(nc,),
        in_specs=[row_d(0), row_d(0), row_d(OFF_Z // SSD_D_INNER),
                  pl.BlockSpec((SSD_CHUNK, LANE), lambda c: (rev(c), OFF_DT // LANE)),
                  row_d(0),
                  pl.BlockSpec((SSD_CHUNK, bc_w), lambda c: (rev(c), SSD_D_INNER // bc_w)),
                  pl.BlockSpec((SSD_CHUNK, bc_w), lambda c: (rev(c), SSD_D_INNER // bc_w + 1)),
                  pl.BlockSpec((1, SSD_D_INNER, SSD_STATE), lambda c: (rev(c), 0, 0)),
                  small, small, wide, wide, pl.BlockSpec((LANE, SSD_D_INNER), lambda c: (0, 0)),
                  pl.BlockSpec((SSD_HEADS * SSD_CHUNK, LANE), lambda c: (0, 0))],
        out_specs=[row_d(0), pl.BlockSpec((SSD_CHUNK, CONV_CH), lambda c: (rev(c), 0)),
                   pl.BlockSpec((SSD_CHUNK, LANE), lambda c: (rev(c), 0)), small, small, small, wide],
        out_shape=[jax.ShapeDtypeStruct((s, SSD_D_INNER), BF16), jax.ShapeDtypeStruct((s, CONV_CH), F32),
                   jax.ShapeDtypeStruct((s, LANE), BF16), small_shape, small_shape, small_shape,
                   jax.ShapeDtypeStruct((1, SSD_D_INNER), F32)],
        scratch_shapes=[pltpu.VMEM((SSD_D_INNER, SSD_STATE), F32)], sem=("arbitrary",), comm=comm)


def _head_mean_matrix():
    row = lax.broadcasted_iota(jnp.int32, (LANE, LANE), 0) // HEAD_DIM
    col = lax.broadcasted_iota(jnp.int32, (LANE, LANE), 1) // HEAD_DIM
    return (row == col).astype(F32)


def _head_sum2(v, ones_bd):
    hi = v.astype(BF16)
    lo = (v - hi.astype(F32)).astype(BF16)
    return _dot(jnp.concatenate([hi, lo], axis=1), jnp.concatenate([ones_bd, ones_bd], axis=0), NN)


def _head_norm(x, w, scale, ones_bd):
    ms = _head_sum2(x * x, ones_bd) * (1.0 / HEAD_DIM)
    return (x * lax.rsqrt(ms + EPS)) * (w * scale)


PRO_ROWS = 256
ATT_GROUP_FWD = 16
ATT_GROUP_BWD = 8
KEYS = 2 * ATT_BLK
NEG = -1e30
HALF = HEAD_DIM // 2


def _rows(start, size, dil):
    return pl.ds(start, size) if dil == 1 else pl.ds(start, size, stride=dil)


def _fill_bias(bias_ref):
    row = lax.broadcasted_iota(jnp.int32, (ATT_BLK, 2 * KEYS), 0)
    col = lax.broadcasted_iota(jnp.int32, (ATT_BLK, 2 * KEYS), 1) & (KEYS - 1)
    for first, off in ((0, 0), (1, ATT_BLK)):
        dist = off + row - col
        bias_ref[first] = jnp.where((dist >= 0) & (dist <= ATT_BLK), 0.0, NEG)


def _pair(a, b):
    return jnp.concatenate([jnp.broadcast_to(a, (ATT_BLK, KEYS)), jnp.broadcast_to(b, (ATT_BLK, KEYS))], axis=1)


def _split_heads(x, is_a):
    zero = jnp.zeros_like(x)
    return jnp.concatenate([jnp.where(is_a, x, zero), jnp.where(is_a, zero, x)], axis=0)


def _block_ids(b, nb):
    i = b & (nb - 1)
    q0 = pl.multiple_of(b * ATT_BLK, ATT_BLK)
    k0 = pl.multiple_of((b - jnp.minimum(i, 1)) * ATT_BLK, ATT_BLK)
    return pl.ds(q0, ATT_BLK), pl.ds(k0, KEYS), jnp.minimum(i, 1)


def _att_fwd(proj, qw, kw, comm=None):
    s = proj.shape[0]
    nblk = s // ATT_BLK
    assert all((s // d) // ATT_BLK >= 2 for d in DILATIONS)
    blk = lambda off: pl.BlockSpec((s, LANE), lambda i: (0, off // LANE + i))
    wspec = pl.BlockSpec((1, LANE), lambda i: (0, i))
    oblk = pl.BlockSpec((s, LANE), lambda i: (0, i))

    def body(q_ref, k_ref, v_ref, qw_ref, kw_ref, o_ref, lse_ref, qn, kn, q_cm, k_cm, v_cm, m_acc, l_acc, o_d, m_d, l_d, bias):
        ones_bd = _head_mean_matrix().astype(BF16)
        is_a = lax.broadcasted_iota(jnp.int32, (1, LANE), 1) < HEAD_DIM
        ones_ext = _split_heads(jnp.ones((KEYS, LANE), BF16), is_a)
        _fill_bias(bias)

        def pro(j, c):
            rows = pl.ds(pl.multiple_of(j * PRO_ROWS, PRO_ROWS), PRO_ROWS)
            qn[rows, :] = _head_norm(q_ref[rows, :], qw_ref[...], HEAD_DIM ** -0.5, ones_bd)
            kn[rows, :] = _head_norm(k_ref[rows, :], kw_ref[...], 1.0, ones_bd)
            return c

        lax.fori_loop(0, s // PRO_ROWS, pro, 0)

        for dil in DILATIONS:
            ln = s // dil
            nb = ln // ATT_BLK
            o_out, m_out, l_out = (o_ref, m_acc, l_acc) if dil == 1 else (o_d, m_d, l_d)
            for r in range(dil):
                def relayout(j, c, dil=dil, r=r, ln=ln):
                    j0 = pl.multiple_of(j * PRO_ROWS, PRO_ROWS)
                    src = _rows(r + dil * j0, PRO_ROWS, dil)
                    dst = pl.ds(r * ln + j0, PRO_ROWS)
                    q_cm[dst, :] = qn[src, :].astype(BF16)
                    k_cm[dst, :] = kn[src, :].astype(BF16)
                    v_cm[dst, :] = v_ref[src, :].astype(BF16)
                    return c

                lax.fori_loop(0, ln // PRO_ROWS, relayout, 0)

            def step(bg, c, nb=nb, o_out=o_out, m_out=m_out, l_out=l_out):
                ids = [_block_ids(bg * ATT_GROUP_FWD + u, nb) for u in range(ATT_GROUP_FWD)]
                kbs = [_split_heads(k_cm[krows, :], is_a) for _, krows, _ in ids]
                scs = [_dot(q_cm[qrows, :], kb, NT) + bias[first] for (qrows, _, first), kb in zip(ids, kbs)]
                mas = [jnp.max(sc[:, :KEYS], axis=-1, keepdims=True) for sc in scs]
                mbs = [jnp.max(sc[:, KEYS:], axis=-1, keepdims=True) for sc in scs]
                ps = [jnp.exp(sc - _pair(ma, mb)).astype(BF16) for sc, ma, mb in zip(scs, mas, mbs)]
                vbs = [jnp.concatenate([_split_heads(v_cm[krows, :], is_a), ones_ext], axis=1) for _, krows, _ in ids]
                ols = [_dot(p, vb, NN) for p, vb in zip(ps, vbs)]
                for (qrows, _, _), ol, ma, mb in zip(ids, ols, mas, mbs):
                    o_out[qrows, :] = ol[:, :LANE]
                    l_out[qrows, :] = ol[:, LANE:]
                    m_out[qrows, :] = jnp.where(is_a, ma, mb)
                return c

            lax.fori_loop(0, nblk // ATT_GROUP_FWD, step, 0)

            if dil > 1:
                for r in range(dil):
                    def merge(j, c, dil=dil, r=r, ln=ln):
                        j0 = pl.multiple_of(j * PRO_ROWS, PRO_ROWS)
                        nat = _rows(r + dil * j0, PRO_ROWS, dil)
                        cm = pl.ds(r * ln + j0, PRO_ROWS)
                        m_old, m_new = m_acc[nat, :], m_d[cm, :]
                        m = jnp.maximum(m_old, m_new)
                        a_old, a_new = jnp.exp(m_old - m), jnp.exp(m_new - m)
                        o_ref[nat, :] = a_old * o_ref[nat, :] + a_new * o_d[cm, :]
                        l_acc[nat, :] = a_old * l_acc[nat, :] + a_new * l_d[cm, :]
                        m_acc[nat, :] = m
                        return c

                    lax.fori_loop(0, ln // PRO_ROWS, merge, 0)

        def epi(j, c):
            rows = pl.ds(pl.multiple_of(j * PRO_ROWS, PRO_ROWS), PRO_ROWS)
            l = l_acc[rows, :]
            o_ref[rows, :] = o_ref[rows, :] / l
            lse_ref[rows, :] = m_acc[rows, :] + jnp.log(l)
            return c

        lax.fori_loop(0, s // PRO_ROWS, epi, 0)

    f = jax.ShapeDtypeStruct((s, ATT_D), F32)
    scr = pltpu.VMEM((s, LANE), F32)
    scb = pltpu.VMEM((s, LANE), BF16)
    return _pcall(
        body, (proj, proj, proj, qw, kw), name="att_fwd", grid=(ATT_D // LANE,),
        in_specs=[blk(OFF_Q), blk(OFF_K), blk(OFF_V), wspec, wspec], out_specs=[oblk, oblk], out_shape=[f, f],
        scratch_shapes=[scr, scr, scb, scb, scb, scr, scr, scr, scr, scr, pltpu.VMEM((2, ATT_BLK, 2 * KEYS), F32)],
        sem=("parallel",), comm=comm)


def _att_bwd(proj, do, stats, qw, kw, comm=None):
    s = proj.shape[0]
    nblk = s // ATT_BLK
    blk = lambda off: pl.BlockSpec((s, LANE), lambda i: (0, off // LANE + i))
    wspec = pl.BlockSpec((1, LANE), lambda i: (0, i))
    oblk = pl.BlockSpec((s, LANE), lambda i: (0, i))

    def body(q_ref, k_ref, v_ref, do_ref, st_ref, qw_ref, kw_ref, dq_ref, dk_ref, dv_ref, dqw_ref, dkw_ref,
             qn, kn, q_cm, do_cm, k_cm, v_cm, st_cm, dq_acc, dk_acc, dv_acc, dq_d, dk_d, dv_d, bias):
        ones_bd = _head_mean_matrix().astype(BF16)
        is_a = lax.broadcasted_iota(jnp.int32, (1, LANE), 1) < HEAD_DIM
        _fill_bias(bias)
        zero = jnp.zeros((PRO_ROWS, LANE), F32)

        def pro(j, c):
            rows = pl.ds(pl.multiple_of(j * PRO_ROWS, PRO_ROWS), PRO_ROWS)
            qn[rows, :] = _head_norm(q_ref[rows, :], qw_ref[...], HEAD_DIM ** -0.5, ones_bd)
            kn[rows, :] = _head_norm(k_ref[rows, :], kw_ref[...], 1.0, ones_bd)
            dk_acc[rows, :] = zero
            dv_acc[rows, :] = zero
            return c

        lax.fori_loop(0, s // PRO_ROWS, pro, 0)

        for dil in DILATIONS:
            ln = s // dil
            nb = ln // ATT_BLK
            dq_o, dk_o, dv_o = (dq_acc, dk_acc, dv_acc) if dil == 1 else (dq_d, dk_d, dv_d)
            for r in range(dil):
                def relayout(j, c, dil=dil, r=r, ln=ln):
                    j0 = pl.multiple_of(j * PRO_ROWS, PRO_ROWS)
                    src = _rows(r + dil * j0, PRO_ROWS, dil)
                    dst = pl.ds(r * ln + j0, PRO_ROWS)
                    q_cm[dst, :] = qn[src, :].astype(BF16)
                    k_cm[dst, :] = kn[src, :].astype(BF16)
                    v_cm[dst, :] = v_ref[src, :].astype(BF16)
                    do_cm[dst, :] = do_ref[src, :].astype(BF16)
                    st_cm[dst, :] = st_ref[src, :]
                    if dil > 1:
                        dk_d[dst, :] = zero
                        dv_d[dst, :] = zero
                    return c

                lax.fori_loop(0, ln // PRO_ROWS, relayout, 0)

            def step(bg, c, nb=nb, dq_o=dq_o, dk_o=dk_o, dv_o=dv_o):
                ids = [_block_ids(bg * ATT_GROUP_BWD + u, nb) for u in range(ATT_GROUP_BWD)]
                qbs = [q_cm[qrows, :] for qrows, _, _ in ids]
                dobs = [do_cm[qrows, :] for qrows, _, _ in ids]
                kbs = [_split_heads(k_cm[krows, :], is_a) for _, krows, _ in ids]
                vbs = [_split_heads(v_cm[krows, :], is_a) for _, krows, _ in ids]
                sts = [st_cm[qrows, :] for qrows, _, _ in ids]
                scs = [_dot(qb, kb, NT) + bias[first] for qb, kb, (_, _, first) in zip(qbs, kbs, ids)]
                dps = [_dot(dob, vb, NT) for dob, vb in zip(dobs, vbs)]
                ps = [jnp.exp(sc - _pair(st[:, 0:1], st[:, HEAD_DIM:HEAD_DIM + 1])) for sc, st in zip(scs, sts)]
                dss = [(p * (dp - _pair(st[:, HALF:HALF + 1], st[:, HEAD_DIM + HALF:HEAD_DIM + HALF + 1]))).astype(BF16)
                       for p, dp, st in zip(ps, dps, sts)]
                dqs = [_dot(ds, kb, NN) for ds, kb in zip(dss, kbs)]
                dkfs = [_dot(ds, qb, TN) for ds, qb in zip(dss, qbs)]
                dvfs = [_dot(p.astype(BF16), dob, TN) for p, dob in zip(ps, dobs)]
                for (qrows, krows, _), dq, dkf, dvf in zip(ids, dqs, dkfs, dvfs):
                    dq_o[qrows, :] = dq
                    dk_o[krows, :] += jnp.where(is_a, dkf[:KEYS], dkf[KEYS:])
                    dv_o[krows, :] += jnp.where(is_a, dvf[:KEYS], dvf[KEYS:])
                return c

            lax.fori_loop(0, nblk // ATT_GROUP_BWD, step, 0)

            if dil > 1:
                for r in range(dil):
                    def merge(j, c, dil=dil, r=r, ln=ln):
                        j0 = pl.multiple_of(j * PRO_ROWS, PRO_ROWS)
                        nat = _rows(r + dil * j0, PRO_ROWS, dil)
                        cm = pl.ds(r * ln + j0, PRO_ROWS)
                        dq_acc[nat, :] += dq_d[cm, :]
                        dk_acc[nat, :] += dk_d[cm, :]
                        dv_acc[nat, :] += dv_d[cm, :]
                        return c

                    lax.fori_loop(0, ln // PRO_ROWS, merge, 0)

        def back(dn_out, x, w, scale):
            r = lax.rsqrt(_head_sum2(x * x, ones_bd) * (1.0 / HEAD_DIM) + EPS)
            nrm = x * r
            dw = jnp.sum(dn_out * nrm, axis=0, keepdims=True) * scale
            dn = dn_out * (w * scale)
            return r * (dn - nrm * (_head_sum2(dn * nrm, ones_bd) * (1.0 / HEAD_DIM))), dw

        def epi(j, c):
            rows = pl.ds(pl.multiple_of(j * PRO_ROWS, PRO_ROWS), PRO_ROWS)
            dq, dqw = back(dq_acc[rows, :], q_ref[rows, :], qw_ref[...], HEAD_DIM ** -0.5)
            dk, dkw = back(dk_acc[rows, :], k_ref[rows, :], kw_ref[...], 1.0)
            dq_ref[rows, :] = dq.astype(BF16)
            dk_ref[rows, :] = dk.astype(BF16)
            dv_ref[rows, :] = dv_acc[rows, :].astype(BF16)
            return (c[0] + dqw, c[1] + dkw)

        zrow = jnp.zeros((1, LANE), F32)
        dqw, dkw = lax.fori_loop(0, s // PRO_ROWS, epi, (zrow, zrow))
        dqw_ref[...] = dqw
        dkw_ref[...] = dkw

    o = jax.ShapeDtypeStruct((s, ATT_D), BF16)
    ov = jax.ShapeDtypeStruct((1, ATT_D), F32)
    scr = pltpu.VMEM((s, LANE), F32)
    scb = pltpu.VMEM((s, LANE), BF16)
    return _pcall(
        body, (proj, proj, proj, do, stats, qw, kw), name="att_bwd", grid=(ATT_D // LANE,),
        in_specs=[blk(OFF_Q), blk(OFF_K), blk(OFF_V), oblk, oblk, wspec, wspec],
        out_specs=[oblk, oblk, oblk, wspec, wspec], out_shape=[o, o, o, ov, ov],
        scratch_shapes=[scr, scr, scb, scb, scb, scb, scr, scr, scr, scr, scr, scr, scr, pltpu.VMEM((2, ATT_BLK, 2 * KEYS), F32)],
        sem=("parallel",), comm=comm)


def _att_norm_fwd(o, nw, ycat):
    s = o.shape[0]
    row = pl.BlockSpec((ROW_TILE, ATT_D), lambda i: (i, 0))
    vec = pl.BlockSpec((1, ATT_D), lambda i: (0, 0))

    def body(o_ref, nw_ref, ycat_ref, y_ref):
        o = o_ref[...]
        r = lax.rsqrt(jnp.mean(o * o, axis=-1, keepdims=True) + EPS)
        y_ref[...] = (o * r * nw_ref[...]).astype(BF16)

    return pl.pallas_call(body, name="att_norm_fwd", grid=(s // ROW_TILE,),
                          in_specs=[row, vec, pl.BlockSpec(memory_space=pl.ANY)],
                          out_specs=pl.BlockSpec((ROW_TILE, ATT_D), lambda i: (i, 1)),
                          out_shape=jax.ShapeDtypeStruct(ycat.shape, BF16), input_output_aliases={2: 0},
                          compiler_params=_cparams(("parallel",)))(o, nw, ycat)


def _mixer_split_epilogue(dycat, first, rows, vecs, outs):
    (o_ref, lse_ref), (nw_ref,), (dyssd_ref, do_ref, st_ref, dnw_ref) = rows, vecs, outs

    @pl.when(first)
    def _():
        dnw_ref[...] = jnp.zeros_like(dnw_ref)

    dyssd_ref[...] = dycat[:, :SSD_D_INNER]
    dy = dycat[:, SSD_D_INNER:]
    o = o_ref[...]
    r = lax.rsqrt(jnp.mean(o * o, axis=-1, keepdims=True) + EPS)
    nrm = o * r
    dnw_ref[...] += jnp.sum(dy * nrm, axis=0, keepdims=True)
    dn = dy * nw_ref[...]
    do = r * (dn - nrm * jnp.mean(dn * nrm, axis=-1, keepdims=True))
    do_ref[...] = do
    ones_bd = _head_mean_matrix().astype(BF16)
    prod = do * o
    delta = jnp.concatenate([_head_sum2(prod[:, j * LANE:(j + 1) * LANE], ones_bd) for j in range(ATT_D // LANE)], axis=1)
    lane = lax.broadcasted_iota(jnp.int32, (1, ATT_D), 1)
    st_ref[...] = jnp.where((lane & (HEAD_DIM - 1)) < HALF, lse_ref[...], delta)


def _ada_fwd(c_all, w_ada):
    def body(c_ref, w_ref, o_ref):
        cv = c_ref[...]
        o_ref[...] = _dot((cv * _sigmoid(cv)).astype(BF16), w_ref[...].astype(BF16), NN)

    return pl.pallas_call(body, name="ada_fwd", out_shape=jax.ShapeDtypeStruct((c_all.shape[0], w_ada.shape[1]), F32),
                          compiler_params=_cparams())(c_all, w_ada)


def _adamw_math(g, w, m, v):
    m_new = ADAM_B1 * m + (1.0 - ADAM_B1) * g
    v_new = ADAM_B2 * v + (1.0 - ADAM_B2) * (g * g)
    m_hat = m_new / (1.0 - ADAM_B1 ** ADAM_STEP)
    v_hat = v_new / (1.0 - ADAM_B2 ** ADAM_STEP)
    delta = -ADAM_LR * (m_hat / (jnp.sqrt(v_hat) + ADAM_EPS) + ADAM_WD * w)
    return delta, m_new, v_new


def _ada_bwd_adamw(c_all, dmod_cols, w, m, v):
    rows, cols = w.shape
    tr = 256
    blk = pl.BlockSpec((tr, cols), lambda i: (i, 0))

    def body(c_ref, d_ref, w_ref, m_ref, v_ref, g_ref, dl_ref, mo_ref, vo_ref):
        cv = c_ref[...]
        ca = cv * _sigmoid(cv)
        g = ca[:, 0:1] * d_ref[0:1, :]
        for b in range(1, N_DEV):
            g = g + ca[:, b:b + 1] * d_ref[b:b + 1, :]
        g_ref[...] = g
        dl_ref[...], mo_ref[...], vo_ref[...] = _adamw_math(g, w_ref[...], m_ref[...], v_ref[...])

    o = jax.ShapeDtypeStruct((rows, cols), F32)
    return pl.pallas_call(
        body, name="ada_bwd_adamw", grid=(rows // tr,),
        in_specs=[pl.BlockSpec((tr, N_DEV), lambda i: (i, 0)), pl.BlockSpec((N_DEV, cols), lambda i: (0, 0)), blk, blk, blk],
        out_specs=[blk] * 4, out_shape=[o, o, o, o], compiler_params=_cparams(("parallel",)))(c_all.T, dmod_cols, w, m, v)


def _reduce_adamw(slabs, w, m, v, name):
    rows, cols = w.shape
    n_src = slabs.shape[0]
    if rows % 128 == 0:
        tr, steps = 128, rows // 128
        blk = pl.BlockSpec((tr, cols), lambda i: (i, 0))
        sblk = pl.BlockSpec((n_src, tr, cols), lambda i: (0, i, 0))
    else:
        tc, steps = 256, cols // 256
        blk = pl.BlockSpec((rows, tc), lambda i: (0, i))
        sblk = pl.BlockSpec((n_src, rows, tc), lambda i: (0, 0, i))

    def body(s_ref, w_ref, m_ref, v_ref, g_ref, dl_ref, mo_ref, vo_ref):
        g = s_ref[0].astype(F32)
        for src in range(1, n_src):
            g = g + s_ref[src].astype(F32)
        g_ref[...] = g
        dl_ref[...], mo_ref[...], vo_ref[...] = _adamw_math(g, w_ref[...], m_ref[...], v_ref[...])

    o = jax.ShapeDtypeStruct((rows, cols), F32)
    return pl.pallas_call(
        body, name=name, grid=(steps,), in_specs=[sblk, blk, blk, blk],
        out_specs=[blk] * 4, out_shape=[o, o, o, o], compiler_params=_cparams(("parallel",)))(slabs, w, m, v)


def _small_reduce_adamw(gathered, w, m, v):
    def body(s_ref, w_ref, m_ref, v_ref, g_ref, dl_ref, mo_ref, vo_ref):
        g = s_ref[0]
        for dev in range(1, N_DEV):
            g = g + s_ref[dev]
        g_ref[...] = g
        dl_ref[...], mo_ref[...], vo_ref[...] = _adamw_math(g, w_ref[...], m_ref[...], v_ref[...])

    o = jax.ShapeDtypeStruct(w.shape, F32)
    return pl.pallas_call(body, name="small_reduce_adamw", out_shape=[o, o, o, o], compiler_params=_cparams())(gathered, w, m, v)


def _adamw_small(g, w, m, v, name):
    def body(g_ref, w_ref, m_ref, v_ref, dl_ref, mo_ref, vo_ref):
        dl_ref[...], mo_ref[...], vo_ref[...] = _adamw_math(g_ref[...], w_ref[...], m_ref[...], v_ref[...])

    o = jax.ShapeDtypeStruct(w.shape, F32)
    return pl.pallas_call(body, name=name, out_shape=[o, o, o], compiler_params=_cparams())(g, w, m, v)


class _Exchange:
    def __init__(self, arrs, scatter):
        self.arrs, self.scatter, self.n = list(arrs), scatter, len(arrs)
        hbm = pl.BlockSpec(memory_space=pltpu.HBM)
        self.in_specs = [hbm] * self.n
        self.out_specs = [hbm] * self.n
        self.out_shape = [jax.ShapeDtypeStruct(a.shape if scatter else (N_DEV,) + a.shape, a.dtype) for a in self.arrs]
        self.scratch = [pltpu.SemaphoreType.DMA((self.n * (N_DEV - 1),)), pltpu.SemaphoreType.DMA((self.n * (N_DEV - 1),)),
                        pltpu.SemaphoreType.DMA((self.n,))]

    def _local(self, ins, outs, sems):
        me = 4 * lax.axis_index("x") + 2 * lax.axis_index("y") + lax.axis_index("c")
        return [pltpu.make_async_copy(ins[a].at[me] if self.scatter else ins[a], outs[a].at[me], sems[2].at[a])
                for a in range(self.n)]

    def _remote(self, ins, outs, sems, arriving):
        send_sems, recv_sems, _ = sems
        x, y, c = lax.axis_index("x"), lax.axis_index("y"), lax.axis_index("c")
        me = 4 * x + 2 * y + c
        remote = []
        for a in range(self.n):
            for k in range(1, N_DEV):
                px = 1 - x if k & 4 else x
                py = 1 - y if k & 2 else y
                pc = 1 - c if k & 1 else c
                peer = 4 * px + 2 * py + pc
                sem = a * (N_DEV - 1) + k - 1
                remote.append(pltpu.make_async_remote_copy(
                    src_ref=ins[a].at[peer] if self.scatter else ins[a], dst_ref=outs[a].at[peer if arriving else me],
                    send_sem=send_sems.at[sem], recv_sem=recv_sems.at[sem], device_id=(px, py, pc), device_id_type=MESH_IDS))
        return remote

    def start(self, ins, outs, sems):
        for cp in self._local(ins, outs, sems) + self._remote(ins, outs, sems, arriving=False):
            cp.start()

    def forward(self, ins, outs, sems):
        pass

    def wait(self, ins, outs, sems):
        for send, arrival in zip(self._remote(ins, outs, sems, arriving=False), self._remote(ins, outs, sems, arriving=True)):
            send.wait_send()
            arrival.wait_recv()
        for cp in self._local(ins, outs, sems):
            cp.wait()


N_CHIP = N_DEV // 2


class _SiblingSwap(_Exchange):
    def __init__(self, arrs):
        super().__init__(arrs, scatter=True)
        self.out_shape = [jax.ShapeDtypeStruct((N_CHIP,) + a.shape[2:], a.dtype) for a in self.arrs]
        self.scratch = [pltpu.SemaphoreType.DMA((self.n,)), pltpu.SemaphoreType.DMA((self.n,)), pltpu.SemaphoreType.DMA((1,))]

    def _copies(self, ins, outs, sems):
        x, y, c = lax.axis_index("x"), lax.axis_index("y"), lax.axis_index("c")
        return [pltpu.make_async_remote_copy(src_ref=ins[a].at[:, 1 - c], dst_ref=outs[a], send_sem=sems[0].at[a], recv_sem=sems[1].at[a],
                                             device_id=(x, y, 1 - c), device_id_type=MESH_IDS) for a in range(self.n)]

    def start(self, ins, outs, sems):
        for cp in self._copies(ins, outs, sems):
            cp.start()

    def wait(self, ins, outs, sems):
        for cp in self._copies(ins, outs, sems):
            cp.wait()


class _ChipScatter(_Exchange):
    def __init__(self, arrs):
        super().__init__(arrs, scatter=True)
        n_pairs = self.n * (N_CHIP - 1)
        self.scratch = [pltpu.SemaphoreType.DMA((n_pairs,)), pltpu.SemaphoreType.DMA((n_pairs,)), pltpu.SemaphoreType.DMA((self.n,))]

    def _local(self, ins, outs, sems):
        chip = 2 * lax.axis_index("x") + lax.axis_index("y")
        return [pltpu.make_async_copy(ins[a].at[chip], outs[a].at[chip], sems[2].at[a]) for a in range(self.n)]

    def _remote(self, ins, outs, sems, arriving):
        send_sems, recv_sems, _ = sems
        x, y, c = lax.axis_index("x"), lax.axis_index("y"), lax.axis_index("c")
        chip = 2 * x + y
        remote = []
        for a in range(self.n):
            for k in range(1, N_CHIP):
                px = 1 - x if k & 2 else x
                py = 1 - y if k & 1 else y
                peer = 2 * px + py
                sem = a * (N_CHIP - 1) + k - 1
                remote.append(pltpu.make_async_remote_copy(
                    src_ref=ins[a].at[peer], dst_ref=outs[a].at[peer if arriving else chip], send_sem=send_sems.at[sem],
                    recv_sem=recv_sems.at[sem], device_id=(px, py, c), device_id_type=MESH_IDS))
        return remote


def _chip_sum(mine, theirs):
    n, rows, cols = mine.shape
    blk = pl.BlockSpec((1, rows, 256), lambda q, j: (q, 0, j))

    def body(a_ref, b_ref, o_ref):
        o_ref[...] = (a_ref[...].astype(F32) + b_ref[...].astype(F32)).astype(BF16)

    return pl.pallas_call(body, name="chip_sum", grid=(n, cols // 256), in_specs=[blk, blk], out_specs=blk,
                          out_shape=jax.ShapeDtypeStruct(mine.shape, BF16),
                          compiler_params=_cparams(("parallel", "parallel")))(mine, theirs)


class _Gather2(_Exchange):
    def __init__(self, arrs):
        super().__init__(arrs, scatter=False)

    def _copies(self, ins, outs, sems):
        send_sems, recv_sems, _ = sems
        x, y, c = lax.axis_index("x"), lax.axis_index("y"), lax.axis_index("c")
        sibling = (x, y, 1 - c)
        chips = [(1 - x, y), (x, 1 - y), (1 - x, 1 - y)]
        first, passed, landed = [], [], []
        for a in range(self.n):
            def copy(k, block, to, src=None, a=a):
                slab = outs[a].at[4 * block[0] + 2 * block[1] + block[2]]
                return pltpu.make_async_remote_copy(
                    src_ref=slab if src is None else src, dst_ref=slab, send_sem=send_sems.at[a * (N_DEV - 1) + k],
                    recv_sem=recv_sems.at[a * (N_DEV - 1) + k], device_id=to, device_id_type=MESH_IDS)

            first.append(copy(0, (x, y, c), sibling, src=ins[a]))
            landed.append(copy(0, sibling, sibling))
            for j, chip in enumerate(chips):
                first.append(copy(1 + j, (x, y, c), (*chip, c), src=ins[a]))
                passed.append((copy(1 + j, (*chip, c), sibling), copy(4 + j, (*chip, c), sibling)))
                landed.append(copy(4 + j, (*chip, 1 - c), sibling))
        return first, passed, landed

    def start(self, ins, outs, sems):
        for cp in self._local(ins, outs, sems) + self._copies(ins, outs, sems)[0]:
            cp.start()

    def forward(self, ins, outs, sems):
        for arrival, onward in self._copies(ins, outs, sems)[1]:
            arrival.wait_recv()
            onward.start()

    def wait(self, ins, outs, sems):
        first, passed, landed = self._copies(ins, outs, sems)
        for arrival in landed:
            arrival.wait_recv()
        for cp in first + [onward for _, onward in passed]:
            cp.wait_send()
        for cp in self._local(ins, outs, sems):
            cp.wait()


def _split_comm_refs(refs, n_in, n_out, n_scr, comm):
    nc = comm.n if comm is not None else 0
    ns = 3 if comm is not None else 0
    pos, groups = 0, []
    for cnt in (n_in, nc, n_out, nc, n_scr, ns):
        groups.append(refs[pos:pos + cnt])
        pos += cnt
    assert pos == len(refs), (pos, len(refs))
    return groups


def _pcall(body, args, *, name, grid, in_specs, out_specs, out_shape, scratch_shapes=(), sem=None, comm=None):
    in_specs, out_specs, out_shape, scratch_shapes = list(in_specs), list(out_specs), list(out_shape), list(scratch_shapes)
    n_in, n_out, n_scr = len(in_specs), len(out_specs), len(scratch_shapes)
    if comm is None:
        kernel_body = body
    else:
        def kernel_body(*refs):
            ins, cins, outs, couts, scr, sems = _split_comm_refs(refs, n_in, n_out, n_scr, comm)
            ids = [pl.program_id(a) for a in range(len(grid))]
            first, last = ids[0] == 0, ids[0] == grid[0] - 1
            for a in range(1, len(grid)):
                first, last = first & (ids[a] == 0), last & (ids[a] == grid[a] - 1)

            middle = ids[0] == (2 * grid[0]) // 3
            for a in range(1, len(grid)):
                middle = middle & (ids[a] == 0)

            @pl.when(first)
            def _():
                comm.start(cins, couts, sems)

            @pl.when(middle)
            def _():
                comm.forward(cins, couts, sems)

            body(*ins, *outs, *scr)

            @pl.when(last)
            def _():
                comm.wait(cins, couts, sems)

        in_specs, out_specs, out_shape = in_specs + comm.in_specs, out_specs + comm.out_specs, out_shape + comm.out_shape
        scratch_shapes, args = scratch_shapes + comm.scratch, list(args) + comm.arrs
        sem = ("arbitrary",) * len(grid)
    res = pl.pallas_call(kernel_body, name=name, grid=grid, in_specs=in_specs, out_specs=out_specs, out_shape=out_shape,
                         scratch_shapes=scratch_shapes, compiler_params=_cparams(sem))(*args)
    return res[:n_out], res[n_out:]


def _exchange(arrs, name, scatter=False, ex=None):
    if ex is None:
        ex = _Exchange(arrs, scatter=True) if scatter else _Gather2(arrs)

    def body(*refs):
        _, ins, _, outs, _, sems = _split_comm_refs(refs, 0, 0, 0, ex)
        ex.start(ins, outs, sems)
        ex.forward(ins, outs, sems)
        ex.wait(ins, outs, sems)

    return pl.pallas_call(body, name=name, in_specs=ex.in_specs, out_specs=ex.out_specs, out_shape=ex.out_shape,
                          scratch_shapes=ex.scratch)(*ex.arrs)


def _pad_lanes(v, width=LANE):
    return jnp.pad(v, ((0, 0), (0, width - v.shape[1])))


def _shards_to_cols(g):
    return jnp.transpose(g, (1, 0, 2)).reshape(g.shape[1], N_DEV * g.shape[2])


def _cols_to_shards(w):
    return w.astype(BF16).reshape(w.shape[0], N_DEV, w.shape[1] // N_DEV).transpose(1, 0, 2)


def _local_step(x, tgt, mod, w_in_pt, conv_w, conv_b, dt_bias, a_log, d_skip, ssd_norm_w, q_norm_w, k_norm_w,
                attn_norm_w, w_out_sh, w_ff1_sh, w_ff2_sh, norm1_w, norm2_w, core):
    shift1, scale1, gate1, shift2, scale2, gate2 = [mod[i:i + 1] for i in range(N_MOD)]
    dtb, alog, dsk = _pad_lanes(dt_bias), _pad_lanes(a_log), _pad_lanes(d_skip)
    qw, kw = jnp.tile(q_norm_w, (1, ATT_HEADS)), jnp.tile(k_norm_w, (1, ATT_HEADS))

    h1 = _norm_mod_fwd(x, norm1_w, scale1, shift1, "norm1_fwd")
    proj = _matmul(h1, w_in_pt, tb=True, tm=2048, tn=896, tk=1024, name="in_proj")
    pre, act = _conv_fwd(proj, conv_w, conv_b)
    ypre, ycat_ssd, hall = _ssd_fwd(proj, act, dtb, alog, dsk, ssd_norm_w)
    (o_att, lse), (w_out_g, w_ff1_g, w_ff2_g) = _att_fwd(proj, qw, kw, comm=_Gather2([w_out_sh, w_ff1_sh, w_ff2_sh]))
    w_out = w_out_g.reshape(2 * D_MODEL, D_MODEL)
    w_ff1 = _shards_to_cols(w_ff1_g)
    w_ff2 = w_ff2_g.reshape(D_FF, D_MODEL)
    ycat = _att_norm_fwd(o_att, attn_norm_w, ycat_ssd)
    row32, row16, vec32 = ("row", F32), ("row", BF16), ("vec", F32)
    mix, x1, h2 = _matmul_rows(ycat, w_out, _residual_norm_epilogue, [x], [gate1, norm2_w, scale2, shift2],
                               [row32, row32, row16], tm=512, name="out_proj")
    u, act_ff = _matmul(h2, w_ff1, tm=1024, tn=2048, tk=1024, name="ff1", mode="relu2")
    loss, dout, dff, dgate2 = _matmul_rows(act_ff, w_ff2, _loss_epilogue, [x1, tgt], [gate2],
                                           [("one", F32), row32, row16, vec32], tm=512, name="ff2")

    du = _matmul(dff, w_ff2, tb=True, tm=512, tn=4096, tk=1024, out_dtype=BF16, name="ff2_dx", mode="drelu2", u=u)
    g_ff2 = _matmul(act_ff, dff, ta=True, tm=512, tn=1024, tk=4096, out_dtype=BF16, name="ff2_dw")
    dx1, dshift2, dscale2, g_norm2, dmix, dgate1 = _matmul_rows(
        du, w_ff1, _norm_bwd_epilogue, [x1, dout, mix], [norm2_w, scale2, gate1],
        [row32, vec32, vec32, vec32, row16, vec32], tb=True, tm=512, name="ff1_dx")
    g_ff1 = _matmul(h2, du, ta=True, tm=1024, tn=D_FF // N_DEV, tk=4096, out_dtype=BF16, name="ff1_dw", shard_out=True)

    dy_ssd, do, stats, g_attn_norm = _matmul_rows(
        dmix, w_out, _mixer_split_epilogue, [o_att, lse], [attn_norm_w],
        [("row", F32, SSD_D_INNER), ("row", F32, ATT_D), ("row", F32, ATT_D), ("vec", F32, ATT_D)], tb=True, tm=512, name="out_proj_dx")
    g_out = _matmul(ycat, dmix, ta=True, tm=512, tn=1024, tk=4096, out_dtype=BF16, name="out_proj_dw")
    ff_slabs = [g_ff1, g_ff2.reshape(N_DEV, D_FF // N_DEV, D_MODEL)]
    (dq, dk, dv, dqw, dkw), (s_ff1, s_ff2) = _att_bwd(proj, do, stats, qw, kw, comm=_Exchange(ff_slabs, scatter=True))
    out_slabs = [g_out.astype(BF16).reshape(N_DEV, 2 * D_MODEL // N_DEV, D_MODEL)]
    (dz, dact, ddtr, da, g_dsk, g_dtb, g_ssd_norm), (s_out,) = _ssd_bwd(
        dy_ssd, ypre, proj, act, hall, dtb, alog, dsk, ssd_norm_w, comm=_Exchange(out_slabs, scatter=True))
    dxbc, g_conv_w, g_conv_b = _conv_bwd(dact, pre, proj, conv_w)
    dproj = [(dz, OFF_Z), (dxbc, OFF_XBC), (ddtr, OFF_DT), (dq, OFF_Q), (dk, OFF_K), (dv, OFF_V)]
    g_wide = _pieces_t_matmul([dz, dxbc, dq, dk, dv], h1, tm=256, name="in_proj_dw")
    g_dt = _matmul(ddtr, h1, ta=True, tm=LANE, tn=1024, tk=4096, out_dtype=BF16, name="in_proj_dw_dt")[:SSD_HEADS]
    in_slabs = jnp.concatenate([g_wide[:OFF_DT], g_dt, g_wide[OFF_DT:]], axis=0).reshape(N_CHIP, 2, IN_W // N_DEV, D_MODEL)
    (sibling_slabs,) = _exchange(None, "swap_w_in_grads", ex=_SiblingSwap([in_slabs]))
    chip_slabs = _chip_sum(lax.dynamic_index_in_dim(in_slabs, core, axis=1, keepdims=False), sibling_slabs)
    (grad_x, dshift1, dscale1, g_norm1), (s_in,) = _matmul_rows(
        dproj, w_in_pt, _norm_bwd_epilogue, [x, dx1], [norm1_w, scale1], [row32, vec32, vec32, vec32],
        tm=256, name="in_proj_dx", comm=_ChipScatter([chip_slabs]))

    dmod = jnp.concatenate([dshift1, dscale1, dgate1, dshift2, dscale2, dgate2], axis=0)
    g_alog = da[:, :SSD_HEADS] * (-jnp.exp(a_log))
    g_qw = dqw.reshape(ATT_HEADS, HEAD_DIM).sum(axis=0, keepdims=True)
    g_kw = dkw.reshape(ATT_HEADS, HEAD_DIM).sum(axis=0, keepdims=True)
    return dict(loss=loss, grad_x=grad_x, dmod=dmod, norm1_w=g_norm1, norm2_w=g_norm2, w_in=s_in, conv_w=g_conv_w,
                conv_b=g_conv_b, dt_bias=g_dtb[:, :SSD_HEADS], a_log=g_alog, d_skip=g_dsk[:, :SSD_HEADS],
                ssd_norm_w=g_ssd_norm, q_norm_w=g_qw, k_norm_w=g_kw, attn_norm_w=g_attn_norm, w_out=s_out,
                w_ff1=s_ff1, w_ff2=s_ff2)


def _pack_w_in_rows(wt_full):
    cut = OFF_DT + SSD_HEADS
    pad = jnp.zeros((LANE - SSD_HEADS, wt_full.shape[1]), wt_full.dtype)
    return jnp.concatenate([wt_full[:cut], pad, wt_full[cut:]], axis=0)


MISC_FIELDS = (("dt_bias", SSD_HEADS), ("a_log", SSD_HEADS), ("d_skip", SSD_HEADS), ("q_norm_w", HEAD_DIM), ("k_norm_w", HEAD_DIM))
SMALL_LAYOUT = (("b_ada", 6), ("norm1_w", 1), ("norm2_w", 1), ("conv_w", 8), ("conv_b", 2), ("ssd_norm_w", 1),
                ("attn_norm_w", 1), ("misc", 1))


def _pack_small(vals):
    rows = []
    for name, nrow in SMALL_LAYOUT:
        if name == "misc":
            misc = jnp.concatenate([vals[f].reshape(1, n) for f, n in MISC_FIELDS], axis=1)
            rows.append(_pad_lanes(misc, D_MODEL))
        elif name in vals:
            rows.append(vals[name].reshape(nrow, D_MODEL))
        else:
            rows.append(jnp.zeros((nrow, D_MODEL), F32))
    used = sum(n for _, n in SMALL_LAYOUT)
    rows.append(jnp.zeros((SMALL_ROWS - used, D_MODEL), F32))
    return jnp.concatenate(rows, axis=0)


def _unpack_small(packed):
    out, r = {}, 0
    for name, nrow in SMALL_LAYOUT:
        blk = packed[r:r + nrow]
        r += nrow
        if name == "misc":
            c0 = 0
            for f, n in MISC_FIELDS:
                out[f] = blk[:, c0:c0 + n]
                c0 += n
        elif name == "b_ada":
            out[name] = blk.reshape(1, N_MOD * D_MODEL)
        elif name == "conv_w":
            out[name] = blk.reshape(CONV_K, CONV_CH)
        elif name == "conv_b":
            out[name] = blk.reshape(1, CONV_CH)
        else:
            out[name] = blk
    return out


WEIGHT_NAMES = ("norm1_w", "norm2_w", "w_ada", "b_ada", "w_in", "conv_w", "conv_b", "dt_bias", "a_log", "d_skip",
                "ssd_norm_w", "q_norm_w", "k_norm_w", "attn_norm_w", "w_out", "w_ff1", "w_ff2")
SMALL_NAMES = ("norm1_w", "norm2_w", "b_ada", "conv_b", "dt_bias", "a_log", "d_skip", "ssd_norm_w", "q_norm_w",
               "k_norm_w", "attn_norm_w")


def kernel(x, c, norm1_w, norm2_w, w_ada, b_ada, w_in, conv_w, conv_b, dt_bias, a_log, d_skip, ssd_norm_w, q_norm_w, k_norm_w, attn_norm_w, w_out, w_ff1, w_ff2, loss_target, m_norm1_w, m_norm2_w, m_w_ada, m_b_ada, m_w_in, m_conv_w, m_conv_b, m_dt_bias, m_a_log, m_d_skip, m_ssd_norm_w, m_q_norm_w, m_k_norm_w, m_attn_norm_w, m_w_out, m_w_ff1, m_w_ff2, v_norm1_w, v_norm2_w, v_w_ada, v_b_ada, v_w_in, v_conv_w, v_conv_b, v_dt_bias, v_a_log, v_d_skip, v_ssd_norm_w, v_q_norm_w, v_k_norm_w, v_attn_norm_w, v_w_out, v_w_ff1, v_w_ff2):
    args = dict(locals())
    w = {n: args[n] for n in WEIGHT_NAMES}
    m = {n: args["m_" + n] for n in WEIGHT_NAMES}
    v = {n: args["v_" + n] for n in WEIGHT_NAMES}
    me = 4 * lax.axis_index("x") + 2 * lax.axis_index("y") + lax.axis_index("c")

    c_rows = jnp.pad(c, ((0, 7), (0, 0)))
    w_in_t, m_in_t, v_in_t = [jnp.transpose(t["w_in"][0]) for t in (w, m, v)]
    c_g, conv_g, w_in_g = _exchange([c_rows, w["conv_w"][0], w_in_t.astype(BF16)], "gather_w_in", scatter=False)
    c_all = c_g[:, 0, :]
    conv_full = _shards_to_cols(conv_g)
    w_in_pt = _pack_w_in_rows(w_in_g.reshape(IN_W, D_MODEL))

    mod_part = _ada_fwd(c_all, w["w_ada"][0])
    (mod_g,) = _exchange([mod_part], "gather_mod", scatter=False)
    mod_mine = lax.dynamic_index_in_dim(mod_g, me, axis=1, keepdims=False).reshape(1, N_MOD * D_MODEL) + w["b_ada"]
    mod = mod_mine.reshape(N_MOD, D_MODEL)

    res = _local_step(x[0], loss_target[0], mod, w_in_pt, conv_full, w["conv_b"], w["dt_bias"], w["a_log"], w["d_skip"],
                      w["ssd_norm_w"], w["q_norm_w"], w["k_norm_w"], w["attn_norm_w"], w["w_out"][0].astype(BF16),
                      w["w_ff1"][0].astype(BF16), w["w_ff2"][0].astype(BF16), w["norm1_w"], w["norm2_w"], lax.axis_index("c"))

    small_vals = {n: res[n] for n in SMALL_NAMES if n != "b_ada"}
    small_vals["b_ada"] = res["dmod"]
    small_vals["conv_w"] = res["conv_w"]
    (small_g,) = _exchange([_pack_small(small_vals)], "gather_small", scatter=False)

    grads, delta, new_m, new_v = {}, {}, {}, {}
    for name in ("w_out", "w_ff1", "w_ff2"):
        outs = _reduce_adamw(res[name], w[name][0], m[name][0], v[name][0], "adamw_" + name)
        grads[name], delta[name], new_m[name], new_v[name] = [o[None] for o in outs]
    outs = _reduce_adamw(res["w_in"], w_in_t, m_in_t, v_in_t, "adamw_w_in")
    grads["w_in"], delta["w_in"], new_m["w_in"], new_v["w_in"] = [jnp.transpose(o)[None] for o in outs]

    sm = _small_reduce_adamw(small_g, _pack_small({n: w[n] for n in SMALL_NAMES}), _pack_small({n: m[n] for n in SMALL_NAMES}),
                             _pack_small({n: v[n] for n in SMALL_NAMES}))
    sm = [_unpack_small(p) for p in sm]
    for n in SMALL_NAMES:
        grads[n], delta[n], new_m[n], new_v[n] = [p[n] for p in sm]
    shard_w = CONV_CH // N_DEV
    g_conv = lax.dynamic_slice_in_dim(sm[0]["conv_w"], me * shard_w, shard_w, axis=1)
    cw = _adamw_small(g_conv, w["conv_w"][0], m["conv_w"][0], v["conv_w"][0], "adamw_conv_w")
    grads["conv_w"] = g_conv[None]
    delta["conv_w"], new_m["conv_w"], new_v["conv_w"] = [o[None] for o in cw]

    ada_w = w_ada.shape[2]
    dmod_all = small_g[:, :N_MOD, :].reshape(N_DEV, N_MOD * D_MODEL)
    dmod_cols = lax.dynamic_slice_in_dim(dmod_all, me * ada_w, ada_w, axis=1)
    outs = _ada_bwd_adamw(c_all, dmod_cols, w["w_ada"][0], m["w_ada"][0], v["w_ada"][0])
    grads["w_ada"], delta["w_ada"], new_m["w_ada"], new_v["w_ada"] = [o[None] for o in outs]

    loss = lax.psum(res["loss"][0, 0], ("x", "y", "c"))
    return (loss, res["grad_x"][None], *[grads[n] for n in WEIGHT_NAMES], *[delta[n] for n in WEIGHT_NAMES],
            *[new_m[n] for n in WEIGHT_NAMES], *[new_v[n] for n in WEIGHT_NAMES])
```

```python
import functools

import jax
import jax.numpy as jnp
from jax import lax
from jax.experimental import pallas as pl
from jax.experimental.pallas import tpu as pltpu

F32 = jnp.float32
BF16 = jnp.bfloat16
HIGHEST = lax.Precision.HIGHEST
MESH_IDS = pl.DeviceIdType.MESH

N_DEV = 8
D_MODEL = 1024
HEAD_DIM = 64
SSD_HEADS = 16
SSD_GROUPS = 4
HEADS_PER_GROUP = SSD_HEADS // SSD_GROUPS
SSD_STATE = 128
SSD_CHUNK = 128
SSD_D_INNER = SSD_HEADS * HEAD_DIM
GROUP_WIDTH = SSD_D_INNER // SSD_GROUPS
CONV_K = 4
CONV_CH = SSD_D_INNER + 2 * SSD_GROUPS * SSD_STATE
ATT_HEADS = 16
ATT_D = ATT_HEADS * HEAD_DIM
ATT_BLK = 128
DILATIONS = (1, 4, 16)
D_FF = 4 * D_MODEL
N_MOD = 6
EPS = 1e-6
IN_W = SSD_D_INNER + CONV_CH + SSD_HEADS + 3 * ATT_D
LANE = 128
OFF_Z, OFF_XBC, OFF_DT = 0, SSD_D_INNER, SSD_D_INNER + CONV_CH
OFF_Q = OFF_DT + LANE
OFF_K, OFF_V = OFF_Q + ATT_D, OFF_Q + 2 * ATT_D
IN_WP = OFF_V + ATT_D

ADAM_LR, ADAM_B1, ADAM_B2, ADAM_EPS, ADAM_WD, ADAM_STEP = 0.001, 0.9, 0.999, 1e-08, 0.01, 10
VMEM_LIMIT = 56 * 1024 * 1024
ROW_TILE = 512
SMALL_ROWS = 24


def _cparams(sem=None):
    return pltpu.CompilerParams(dimension_semantics=sem, vmem_limit_bytes=VMEM_LIMIT)


def _sigmoid(v):
    return 1.0 / (1.0 + jnp.exp(-v))


def _softplus(v):
    y = jnp.exp(-jnp.abs(v))
    small = y * (1.0 - y * (0.5 - y * (1.0 / 3.0)))
    return jnp.maximum(v, 0.0) + jnp.where(y < 0.01, small, jnp.log(1.0 + y))


def _dot(a, b, dims, precision=None):
    return lax.dot_general(a, b, (dims, ((), ())), preferred_element_type=F32, precision=precision)


NN = ((1,), (0,))
NT = ((1,), (1,))
TN = ((0,), (0,))


def _matmul(a, b, *, ta=False, tb=False, tm, tn, tk, out_dtype=F32, name, mode=None, u=None, comm=None, shard_out=False):
    m, k = (a.shape[1], a.shape[0]) if ta else a.shape
    n = b.shape[0] if tb else b.shape[1]
    assert m % tm == 0 and n % tn == 0 and k % tk == 0, (name, m, n, k)
    nk = k // tk
    a_spec = pl.BlockSpec((tk, tm), lambda i, j, kk: (kk, i)) if ta else pl.BlockSpec((tm, tk), lambda i, j, kk: (i, kk))
    b_spec = pl.BlockSpec((tn, tk), lambda i, j, kk: (j, kk)) if tb else pl.BlockSpec((tk, tn), lambda i, j, kk: (kk, j))
    o_spec = pl.BlockSpec((tm, tn), lambda i, j, kk: (i, j))
    dims = ((0,) if ta else (1,), (1,) if tb else (0,))
    n_out = 2 if mode == "relu2" else 1

    def body(*refs):
        if mode == "drelu2":
            a_ref, b_ref, u_ref = refs[:3]
            rest = refs[3:]
        else:
            a_ref, b_ref = refs[:2]
            u_ref = None
            rest = refs[2:]
        outs = rest[:n_out]
        part = _dot(a_ref[...], b_ref[...], dims)

        def finish(r):
            if mode == "relu2":
                outs[0][...] = r.astype(BF16)
                rr = jnp.maximum(r, 0.0)
                outs[1][...] = (rr * rr).astype(BF16)
            elif mode == "drelu2":
                outs[0][...] = (r * (2.0 * jnp.maximum(u_ref[...].astype(F32), 0.0))).astype(out_dtype)
            else:
                outs[0][...] = r.astype(out_dtype)

        if nk == 1:
            finish(part)
        else:
            acc = rest[n_out]
            kk = pl.program_id(2)

            @pl.when(kk == 0)
            def _():
                acc[...] = part

            @pl.when(kk > 0)
            def _():
                acc[...] += part

            @pl.when(kk == nk - 1)
            def _():
                finish(acc[...])

    in_specs = [a_spec, b_spec]
    args = [a, b]
    if mode == "drelu2":
        in_specs.append(o_spec)
        args.append(u)
    if mode == "relu2":
        out_shape = [jax.ShapeDtypeStruct((m, n), BF16), jax.ShapeDtypeStruct((m, n), BF16)]
    elif shard_out:
        out_shape = [jax.ShapeDtypeStruct((n // tn, m, tn), out_dtype)]
        o_spec = pl.BlockSpec((None, tm, tn), lambda i, j, kk: (j, i, 0))
    else:
        out_shape = [jax.ShapeDtypeStruct((m, n), out_dtype)]
    outs, comm_outs = _pcall(
        body, args, name=name, grid=(m // tm, n // tn, nk), in_specs=in_specs, out_specs=[o_spec] * n_out,
        out_shape=out_shape, scratch_shapes=[pltpu.VMEM((tm, tn), F32)] if nk > 1 else [],
        sem=("parallel", "parallel", "arbitrary"), comm=comm)
    res = tuple(outs) if mode == "relu2" else outs[0]
    return res if comm is None else (res, comm_outs)


def _pieces_t_matmul(groups, b, *, tm, name):
    k, n = b.shape
    pieces = [p for g in groups for p in g]
    starts, tiles = [], 0
    for p in pieces:
        assert p.shape[0] == k and p.shape[1] % tm == 0, (name, p.shape)
        starts.append(tiles)
        tiles += p.shape[1] // tm
    group_of, group_start, group_tiles = [], [], []
    for gi, g in enumerate(groups):
        group_start.append(starts[len(group_of)])
        group_of += [gi] * len(g)
        group_tiles.append(sum(p.shape[1] // tm for p in g))

    def clipped(block, start, count):
        return pl.BlockSpec(block, (lambda i: (0, jnp.clip(i - start, 0, count - 1))) if block[0] == k
                            else (lambda i: (jnp.clip(i - start, 0, count - 1), 0)))

    def body(*refs):
        a_refs, b_ref, o_refs = refs[:len(pieces)], refs[len(pieces)], refs[len(pieces) + 1:]
        i = pl.program_id(0)
        for a_ref, start, p, gi in zip(a_refs, starts, pieces, group_of):
            @pl.when((i >= start) & (i < start + p.shape[1] // tm))
            def _(a_ref=a_ref, o_ref=o_refs[gi]):
                o_ref[...] = _dot(a_ref[...], b_ref[...], TN).astype(BF16)

    return pl.pallas_call(
        body, name=name, grid=(tiles,),
        in_specs=[clipped((k, tm), s0, p.shape[1] // tm) for s0, p in zip(starts, pieces)] + [pl.BlockSpec((k, n), lambda i: (0, 0))],
        out_specs=[clipped((tm, n), s0, cnt) for s0, cnt in zip(group_start, group_tiles)],
        out_shape=[jax.ShapeDtypeStruct((cnt * tm, n), BF16) for cnt in group_tiles],
        compiler_params=_cparams(("arbitrary",)))(*pieces, b)


def _rms_mod(xv, nw, scale, shift):
    r = lax.rsqrt(jnp.mean(xv * xv, axis=-1, keepdims=True) + EPS)
    return ((xv * r) * nw * (1.0 + scale) + shift).astype(BF16)


def _norm_mod_fwd(x, nw, scale, shift, name):
    s, d = x.shape
    row = pl.BlockSpec((ROW_TILE, d), lambda i: (i, 0))
    vec = pl.BlockSpec((1, d), lambda i: (0, 0))

    def body(x_ref, nw_ref, sc_ref, sh_ref, h_ref):
        h_ref[...] = _rms_mod(x_ref[...], nw_ref[...], sc_ref[...], sh_ref[...])

    return pl.pallas_call(body, name=name, grid=(s // ROW_TILE,), in_specs=[row, vec, vec, vec], out_specs=row,
                          out_shape=jax.ShapeDtypeStruct((s, d), BF16), compiler_params=_cparams(("parallel",)))(x, nw, scale, shift)


def _matmul_rows(a, b, epilogue, row_in, vec_in, outs, *, tb=False, tm, name, comm=None):
    pieces = a if isinstance(a, list) else [(a, 0)]
    assert not (tb and len(pieces) > 1)
    m = pieces[0][0].shape[0]
    n = b.shape[0] if tb else b.shape[1]
    assert m % tm == 0, (name, m, tm)
    dims = ((1,), (1,) if tb else (0,))
    n_a, n_row, n_vec = len(pieces), len(row_in), len(vec_in)

    def body(*refs):
        a_refs, b_ref, rest = refs[:n_a], refs[n_a], refs[n_a + 1:]
        if n_a == 1:
            c = _dot(a_refs[0][...], b_ref[...], dims)
        else:
            c = None
            for a_ref, (piece, off) in zip(a_refs, pieces):
                part = _dot(a_ref[...], b_ref[off:off + piece.shape[1], :], dims)
                c = part if c is None else c + part
        epilogue(c, pl.program_id(0) == 0, rest[:n_row], rest[n_row:n_row + n_vec], rest[n_row + n_vec:])

    def spec(kind, width):
        block = {"row": (tm, width), "vec": (1, width), "one": (1, 1)}[kind]
        return pl.BlockSpec(block, (lambda i: (i, 0)) if kind == "row" else (lambda i: (0, 0)))

    def shape(kind, width):
        return {"row": (m, width), "vec": (1, width), "one": (1, 1)}[kind]

    outs = [(o[0], o[1], o[2] if len(o) > 2 else n) for o in outs]
    res, comm_outs = _pcall(
        body, [*[p for p, _ in pieces], b, *row_in, *vec_in], name=name, grid=(m // tm,),
        in_specs=[spec("row", p.shape[1]) for p, _ in pieces] + [pl.BlockSpec(b.shape, lambda i: (0, 0))]
        + [spec("row", r.shape[1]) for r in row_in] + [spec("vec", v.shape[1]) for v in vec_in],
        out_specs=[spec(kind, width) for kind, _, width in outs],
        out_shape=[jax.ShapeDtypeStruct(shape(kind, width), dt) for kind, dt, width in outs],
        sem=("arbitrary",), comm=comm)
    return res if comm is None else (res, comm_outs)


def _residual_norm_epilogue(mix, first, rows, vecs, outs):
    (x_ref,), (gate_ref, nw_ref, sc_ref, sh_ref), (mix_ref, x1_ref, h_ref) = rows, vecs, outs
    xv = x_ref[...] + gate_ref[...] * mix
    mix_ref[...] = mix
    x1_ref[...] = xv
    h_ref[...] = _rms_mod(xv, nw_ref[...], sc_ref[...], sh_ref[...])


def _loss_epilogue(ff, first, rows, vecs, outs):
    (x1_ref, t_ref), (g_ref,), (loss_ref, dout_ref, dff_ref, dg_ref) = rows, vecs, outs
    d = ff.shape[1]

    @pl.when(first)
    def _():
        loss_ref[...] = jnp.zeros_like(loss_ref)
        dg_ref[...] = jnp.zeros_like(dg_ref)

    err = x1_ref[...] + g_ref[...] * ff - t_ref[...]
    loss_ref[...] += (0.5 / d) * jnp.sum(err * err).reshape(1, 1)
    dout = err * (1.0 / d)
    dout_ref[...] = dout
    dff_ref[...] = (g_ref[...] * dout).astype(BF16)
    dg_ref[...] += jnp.sum(dout * ff, axis=0, keepdims=True)


def _norm_bwd_epilogue(dh, first, rows, vecs, outs):
    with_gate = len(vecs) == 3
    x_ref, dres_ref = rows[:2]
    nw_ref, sc_ref = vecs[:2]
    dx_ref, dsh_ref, dsc_ref, dnw_ref = outs[:4]

    @pl.when(first)
    def _():
        for ref in outs[1:4] + outs[5:]:
            ref[...] = jnp.zeros_like(ref)

    xv = x_ref[...]
    r = lax.rsqrt(jnp.mean(xv * xv, axis=-1, keepdims=True) + EPS)
    nrm = xv * r
    one_sc = 1.0 + sc_ref[...]
    dhn = dh * nrm
    dsh_ref[...] += jnp.sum(dh, axis=0, keepdims=True)
    dsc_ref[...] += jnp.sum(dhn, axis=0, keepdims=True) * nw_ref[...]
    dnw_ref[...] += jnp.sum(dhn, axis=0, keepdims=True) * one_sc
    dn = dh * (nw_ref[...] * one_sc)
    dx = dres_ref[...] + r * (dn - nrm * jnp.mean(dn * nrm, axis=-1, keepdims=True))
    dx_ref[...] = dx
    if with_gate:
        outs[4][...] = (vecs[2][...] * dx).astype(BF16)
        outs[5][...] += jnp.sum(dx * rows[2][...], axis=0, keepdims=True)


CONV_COLS = 256
CONV_FWD_ROWS = 2048
CONV_BWD_ROWS = 1024
CONV_SUB_ROWS = 128
HALO = 8


def _shift_down(cur, halo, k):
    if k == 0:
        return cur
    rolled = pltpu.roll(cur, k, axis=0)
    top = jnp.where(lax.broadcasted_iota(jnp.int32, halo.shape, 0) < k, pltpu.roll(halo, k, axis=0), rolled[:HALO])
    return jnp.concatenate([top, rolled[HALO:]], axis=0)


def _shift_up(cur, halo, k):
    if k == 0:
        return cur
    t = cur.shape[0]
    rolled = pltpu.roll(cur, t - k, axis=0)
    bot = jnp.where(lax.broadcasted_iota(jnp.int32, halo.shape, 0) >= HALO - k, pltpu.roll(halo, HALO - k, axis=0),
                    rolled[t - HALO:])
    return jnp.concatenate([rolled[:t - HALO], bot], axis=0)


def _conv_fwd(proj, conv_w, conv_b):
    s = proj.shape[0]
    nr = s // CONV_FWD_ROWS
    cb0 = OFF_XBC // CONV_COLS
    hb = CONV_FWD_ROWS // HALO
    cur = pl.BlockSpec((CONV_FWD_ROWS, CONV_COLS), lambda j, r: (r, cb0 + j))
    prev = pl.BlockSpec((HALO, CONV_COLS), lambda j, r: (jnp.maximum(r * hb - 1, 0), cb0 + j))
    out = pl.BlockSpec((CONV_FWD_ROWS, CONV_COLS), lambda j, r: (r, j))

    def body(u_ref, up_ref, w_ref, b_ref, pre_ref, act_ref):
        r = pl.program_id(1)
        u = u_ref[...]
        halo = jnp.where(r > 0, up_ref[...], 0.0)
        acc = b_ref[...] + w_ref[CONV_K - 1:CONV_K, :] * u
        for k in range(1, CONV_K):
            acc = acc + w_ref[CONV_K - 1 - k:CONV_K - k, :] * _shift_down(u, halo, k)
        pre_ref[...] = acc
        act_ref[...] = acc * _sigmoid(acc)

    return pl.pallas_call(
        body, name="conv_fwd", grid=(CONV_CH // CONV_COLS, nr),
        in_specs=[cur, prev, pl.BlockSpec((CONV_K, CONV_COLS), lambda j, r: (0, j)),
                  pl.BlockSpec((1, CONV_COLS), lambda j, r: (0, j))],
        out_specs=[out, out],
        out_shape=[jax.ShapeDtypeStruct((s, CONV_CH), F32), jax.ShapeDtypeStruct((s, CONV_CH), F32)],
        compiler_params=_cparams(("parallel", "arbitrary")))(proj, proj, conv_w, conv_b)


def _conv_bwd(dact, pre, proj, conv_w):
    s = proj.shape[0]
    nr = s // CONV_BWD_ROWS
    cb0 = OFF_XBC // CONV_COLS
    hb = CONV_BWD_ROWS // HALO
    last_halo = s // HALO - 1
    n_sub = CONV_BWD_ROWS // CONV_SUB_ROWS
    cur = pl.BlockSpec((CONV_BWD_ROWS, CONV_COLS), lambda j, r: (r, j))
    nxt = pl.BlockSpec((HALO, CONV_COLS), lambda j, r: (jnp.minimum((r + 1) * hb, last_halo), j))
    ucur = pl.BlockSpec((CONV_BWD_ROWS, CONV_COLS), lambda j, r: (r, cb0 + j))
    wspec = pl.BlockSpec((CONV_K, CONV_COLS), lambda j, r: (0, j))
    bspec = pl.BlockSpec((1, CONV_COLS), lambda j, r: (0, j))

    def dsilu(p):
        sg = _sigmoid(p)
        return sg * (1.0 + p * (1.0 - sg))

    def body(da_ref, dan_ref, pre_ref, pren_ref, u_ref, w_ref, du_ref, dw_ref, db_ref):
        r = pl.program_id(1)

        @pl.when(r == 0)
        def _():
            dw_ref[...] = jnp.zeros_like(dw_ref)
            db_ref[...] = jnp.zeros_like(db_ref)

        dws = [jnp.zeros((1, CONV_COLS), F32) for _ in range(CONV_K)]
        db = jnp.zeros((1, CONV_COLS), F32)
        for c in range(n_sub):
            rows = slice(c * CONV_SUB_ROWS, (c + 1) * CONV_SUB_ROWS)
            ahead = slice((c + 1) * CONV_SUB_ROWS, (c + 1) * CONV_SUB_ROWS + HALO)
            dpre = da_ref[rows, :] * dsilu(pre_ref[rows, :])
            if c < n_sub - 1:
                dnext = da_ref[ahead, :] * dsilu(pre_ref[ahead, :])
            else:
                dnext = jnp.where(r < nr - 1, dan_ref[...] * dsilu(pren_ref[...]), 0.0)
            u = u_ref[rows, :]
            du = w_ref[CONV_K - 1:CONV_K, :] * dpre
            dws[0] = dws[0] + jnp.sum(dpre * u, axis=0, keepdims=True)
            for k in range(1, CONV_K):
                ahead_k = _shift_up(dpre, dnext, k)
                du = du + w_ref[CONV_K - 1 - k:CONV_K - k, :] * ahead_k
                dws[k] = dws[k] + jnp.sum(ahead_k * u, axis=0, keepdims=True)
            du_ref[rows, :] = du.astype(BF16)
            db = db + jnp.sum(dpre, axis=0, keepdims=True)
        dw_ref[...] += jnp.concatenate(dws[::-1], axis=0)
        db_ref[...] += db

    return pl.pallas_call(
        body, name="conv_bwd", grid=(CONV_CH // CONV_COLS, nr),
        in_specs=[cur, nxt, cur, nxt, ucur, wspec],
        out_specs=[cur, wspec, bspec],
        out_shape=[jax.ShapeDtypeStruct((s, CONV_CH), BF16), jax.ShapeDtypeStruct((CONV_K, CONV_CH), F32),
                   jax.ShapeDtypeStruct((1, CONV_CH), F32)],
        compiler_params=_cparams(("parallel", "arbitrary")))(dact, dact, pre, pre, proj, conv_w)


def _ssd_common(dtr, dtb, alog):
    lane = lax.broadcasted_iota(jnp.int32, (1, LANE), 1)
    head_lane = lane < SSD_HEADS
    dt = jnp.where(head_lane, _softplus(dtr + dtb), 0.0)
    a = jnp.where(head_lane, -jnp.exp(alog), 0.0)
    row = lax.broadcasted_iota(jnp.int32, (SSD_CHUNK, SSD_CHUNK), 0)
    col = lax.broadcasted_iota(jnp.int32, (SSD_CHUNK, SSD_CHUNK), 1)
    tril = row >= col
    cs = _dot(tril.astype(F32), dt * a, NN, precision=HIGHEST)
    return dt, a, cs, cs.T, tril, lane


def _split_bf16(v, passes):
    terms, rest = [], v
    for _ in range(passes):
        t = rest.astype(BF16)
        terms.append(t)
        rest = rest - t.astype(F32)
    return terms


def _dot_split(v, m, dims, passes):
    terms = _split_bf16(v, passes)
    if passes == 1:
        return _dot(terms[0], m, dims)
    return _dot(jnp.concatenate(terms, axis=1), jnp.concatenate([m] * passes, axis=0 if dims == NN else 1), dims)


def _ssd_constants():
    heads = jnp.arange(LANE)[:, None]
    exp_mat = (heads == (jnp.arange(SSD_D_INNER)[None, :] // HEAD_DIM)).astype(BF16)
    ind4 = ((jnp.arange(SSD_HEADS * SSD_CHUNK)[:, None] // SSD_CHUNK) == jnp.arange(LANE)[None, :]).astype(BF16)
    return exp_mat, ind4


def _expand_heads(v):
    return jnp.repeat(v[:, :SSD_HEADS], HEAD_DIM, axis=1)


def _ssd_prep(dtr, dtb, alog, exp_mat):
    dt, a, cs, cst, tril, lane = _ssd_common(dtr, dtb, alog)
    return dt, a, cs, cst, tril, lane, _dot_split(dt, exp_mat, NN, 2), _dot_split(cs, exp_mat, NN, 3)


def _chunk_decay_rows(cs, g):
    parts = []
    for e in range(HEADS_PER_GROUP):
        h = g * HEADS_PER_GROUP + e
        parts.append(jnp.broadcast_to(jnp.exp(cs[SSD_CHUNK - 1:SSD_CHUNK, h:h + 1]), (HEAD_DIM, SSD_STATE)))
    return jnp.concatenate(parts, axis=0)


def _ssd_fwd(proj, act, dtb, alog, dsk, nw):
    s = proj.shape[0]
    nc = s // SSD_CHUNK
    bc_w = SSD_GROUPS * SSD_STATE
    exp_mat, _ = _ssd_constants()

    def body(z_ref, dtr_ref, xs_ref, b_ref, c_ref, dtb_ref, alog_ref, dskx_ref, nw_ref, exp_ref,
             ypre_ref, yssd_ref, hall_ref, h_scr):
        @pl.when(pl.program_id(0) == 0)
        def _():
            h_scr[...] = jnp.zeros_like(h_scr)

        dt, a, cs, cst, tril, lane, dtx, csx = _ssd_prep(dtr_ref[...], dtb_ref[...], alog_ref[...], exp_ref[...])
        cs_last_x = csx[SSD_CHUNK - 1:SSD_CHUNK, :]
        xs = xs_ref[...]
        xdt = xs * dtx
        xdtb = xdt.astype(BF16)
        xdec = (xdt * jnp.exp(cs_last_x - csx)).astype(BF16)
        ecsx = jnp.exp(csx)
        head_of_lane = lax.broadcasted_iota(jnp.int32, (1, GROUP_WIDTH), 1) // HEAD_DIM
        for g in range(SSD_GROUPS):
            gs = slice(g * GROUP_WIDTH, (g + 1) * GROUP_WIDTH)
            bg = b_ref[:, g * SSD_STATE:(g + 1) * SSD_STATE].astype(BF16)
            cg = c_ref[:, g * SSD_STATE:(g + 1) * SSD_STATE].astype(BF16)
            cb = _dot(cg, bg, NT)
            hprev = h_scr[gs, :]
            hall_ref[0, gs, :] = hprev
            gms, rhs = [], []
            xg = xdtb[:, gs]
            for e in range(HEADS_PER_GROUP):
                h = g * HEADS_PER_GROUP + e
                lm = jnp.exp(jnp.where(tril, cs[:, h:h + 1] - cst[h:h + 1, :], -1e30))
                gms.append((cb * lm).astype(BF16))
                rhs.append(jnp.where(head_of_lane == e, xg, jnp.zeros_like(xg)))
            y = _dot(jnp.concatenate(gms, axis=1), jnp.concatenate(rhs, axis=0), NN)
            y = y + ecsx[:, gs] * _dot(cg, hprev.astype(BF16), NT)
            y = y + dskx_ref[:, gs] * xs[:, gs]
            h_scr[gs, :] = hprev * _chunk_decay_rows(cs, g) + _dot(xdec[:, gs], bg, TN)
            ypre_ref[:, gs] = y
            z = z_ref[:, gs]
            yg = y * (z * _sigmoid(z))
            r = lax.rsqrt(jnp.mean(yg * yg, axis=-1, keepdims=True) + EPS)
            yssd_ref[:, gs] = (yg * r * nw_ref[:, gs]).astype(BF16)

    row_d = lambda cb: pl.BlockSpec((SSD_CHUNK, SSD_D_INNER), lambda c: (c, cb))
    small = pl.BlockSpec((1, LANE), lambda c: (0, 0))
    wide = pl.BlockSpec((1, SSD_D_INNER), lambda c: (0, 0))
    return pl.pallas_call(
        body, name="ssd_fwd", grid=(nc,),
        in_specs=[row_d(OFF_Z // SSD_D_INNER),
                  pl.BlockSpec((SSD_CHUNK, LANE), lambda c: (c, OFF_DT // LANE)),
                  row_d(0),
                  pl.BlockSpec((SSD_CHUNK, bc_w), lambda c: (c, SSD_D_INNER // bc_w)),
                  pl.BlockSpec((SSD_CHUNK, bc_w), lambda c: (c, SSD_D_INNER // bc_w + 1)),
                  small, small, wide, wide, pl.BlockSpec((LANE, SSD_D_INNER), lambda c: (0, 0))],
        out_specs=[row_d(0), row_d(0), pl.BlockSpec((1, SSD_D_INNER, SSD_STATE), lambda c: (c, 0, 0))],
        out_shape=[jax.ShapeDtypeStruct((s, SSD_D_INNER), F32), jax.ShapeDtypeStruct((s, SSD_D_INNER + ATT_D), BF16),
                   jax.ShapeDtypeStruct((nc, SSD_D_INNER, SSD_STATE), F32)],
        scratch_shapes=[pltpu.VMEM((SSD_D_INNER, SSD_STATE), F32)],
        compiler_params=_cparams(("arbitrary",)))(proj, proj, act, act, act, dtb, alog, _expand_heads(dsk), nw, exp_mat)


def _ssd_bwd(dycat, ypre, proj, act, hall, dtb, alog, dsk, nw, comm=None):
    s = proj.shape[0]
    nc = s // SSD_CHUNK
    bc_w = SSD_GROUPS * SSD_STATE

    exp_mat, ind4 = _ssd_constants()
    seg_passes = 1

    def body(dy_ref, ypre_ref, z_ref, dtr_ref, xs_ref, b_ref, c_ref, hall_ref, dtb_ref, alog_ref, dskx_ref, nw_ref,
             exp_ref, ind4_ref, dz_ref, dact_ref, ddtr_ref, da_ref, ddsk_ref, ddtb_ref, dnw_ref, dh_scr):
        @pl.when(pl.program_id(0) == 0)
        def _():
            dh_scr[...] = jnp.zeros_like(dh_scr)
            da_ref[...] = jnp.zeros_like(da_ref)
            ddsk_ref[...] = jnp.zeros_like(ddsk_ref)
            ddtb_ref[...] = jnp.zeros_like(ddtb_ref)
            dnw_ref[...] = jnp.zeros_like(dnw_ref)

        dtr = dtr_ref[...]
        dt, a, cs, cst, tril, lane, dtx, csx = _ssd_prep(dtr, dtb_ref[...], alog_ref[...], exp_ref[...])
        cs_last_x = csx[SSD_CHUNK - 1:SSD_CHUNK, :]
        xs = xs_ref[...]
        xdt = xs * dtx
        xdtb = xdt.astype(BF16)
        decx = jnp.exp(cs_last_x - csx)
        xdecf = xdt * decx
        xdec = xdecf.astype(BF16)
        ecsx = jnp.exp(csx)
        head_of_lane = lax.broadcasted_iota(jnp.int32, (1, GROUP_WIDTH), 1) // HEAD_DIM
        last_row = lax.broadcasted_iota(jnp.int32, (SSD_CHUNK, 1), 0) == SSD_CHUNK - 1
        dcs_col = jnp.zeros((SSD_CHUNK, LANE), F32)
        dcs_row = jnp.zeros((SSD_CHUNK, LANE), F32)
        ddt = jnp.zeros((SSD_CHUNK, LANE), F32)
        ddsk = jnp.zeros((1, LANE), F32)
        hsum = jnp.zeros((1, LANE), F32)
        t1_sum = jnp.zeros((1, LANE), F32)
        for g in range(SSD_GROUPS):
            gs = slice(g * GROUP_WIDTH, (g + 1) * GROUP_WIDTH)
            bsl = slice(g * SSD_STATE, (g + 1) * SSD_STATE)
            exp_g = exp_ref[:, gs]
            ind4_g = ind4_ref[g * HEADS_PER_GROUP * SSD_CHUNK:(g + 1) * HEADS_PER_GROUP * SSD_CHUNK, :]
            z = z_ref[:, gs]
            sg = _sigmoid(z)
            sz = z * sg
            ypre = ypre_ref[:, gs]
            yg = ypre * sz
            r = lax.rsqrt(jnp.mean(yg * yg, axis=-1, keepdims=True) + EPS)
            nrm = yg * r
            dyo_n = dy_ref[:, gs]
            dnw_ref[:, gs] += jnp.sum(dyo_n * nrm, axis=0, keepdims=True)
            dn = dyo_n * nw_ref[:, gs]
            dyg = r * (dn - nrm * jnp.mean(dn * nrm, axis=-1, keepdims=True))
            dz_ref[:, gs] = (dyg * ypre * (sg * (1.0 + z * (1.0 - sg)))).astype(BF16)
            dy = dyg * sz

            bg = b_ref[:, bsl].astype(BF16)
            cg = c_ref[:, bsl].astype(BF16)
            cb = _dot(cg, bg, NT)
            hprev = hall_ref[0, gs, :]
            hb = hprev.astype(BF16)
            dhn = dh_scr[gs, :]
            dhb = dhn.astype(BF16)
            xs_g, xdt_g = xs[:, gs], xdtb[:, gs]
            w_off = _dot(cg, hb, NT)
            dyo = dy * ecsx[:, gs]
            dyob = dyo.astype(BF16)
            dcg = _dot(dyob, hb, NN)
            dh_y = _dot(dyob, cg, TN)
            r_st = _dot(bg, dhb, NT)
            dbg = _dot(xdec[:, gs], dhb, NN)
            dyb = dy.astype(BF16)
            gms, gmbs, lms, dys = [], [], [], []
            for e in range(HEADS_PER_GROUP):
                h = g * HEADS_PER_GROUP + e
                lm = jnp.exp(jnp.where(tril, cs[:, h:h + 1] - cst[h:h + 1, :], -1e30))
                gm = cb * lm
                lms.append(lm)
                gms.append(gm)
                gmbs.append(gm.astype(BF16))
                dys.append(jnp.where(head_of_lane == e, dyb, jnp.zeros_like(dyb)))
            dxdt = _dot(jnp.concatenate(gmbs, axis=0), jnp.concatenate(dys, axis=0), TN) + decx[:, gs] * r_st
            dcb = jnp.zeros((SSD_CHUNK, SSD_CHUNK), F32)
            mms = []
            for e in range(HEADS_PER_GROUP):
                dg = _dot(dys[e], xdt_g, NT)
                mms.append(dg * gms[e])
                dcb = dcb + dg * lms[e]
            seg = _dot_split(jnp.concatenate([dyo * w_off, xdecf[:, gs] * r_st, dxdt * xs_g, dy * xs_g], axis=0), exp_g, NT, seg_passes)
            v1, t1, ddt_g, dsk_g = [seg[i * SSD_CHUNK:(i + 1) * SSD_CHUNK] for i in range(4)]
            dcs_col = dcs_col + v1 - t1 + _dot_split(jnp.concatenate(mms, axis=1), ind4_g, NN, seg_passes)
            for t in _split_bf16(jnp.concatenate(mms, axis=0), seg_passes):
                dcs_row = dcs_row + _dot(ind4_g, t, TN)
            ddt = ddt + ddt_g
            ddsk = ddsk + jnp.sum(dsk_g, axis=0, keepdims=True)
            t1_sum = t1_sum + jnp.sum(t1, axis=0, keepdims=True)
            for e in range(HEADS_PER_GROUP):
                h = g * HEADS_PER_GROUP + e
                hs = slice(e * HEAD_DIM, (e + 1) * HEAD_DIM)
                hsum = hsum + jnp.where(lane == h, jnp.sum(dhn[hs, :] * hprev[hs, :]).reshape(1, 1), 0.0)
            dh_scr[gs, :] = dhn * _chunk_decay_rows(cs, g) + dh_y
            dcbb = dcb.astype(BF16)
            dact_ref[:, gs] = dxdt * dtx[:, gs] + dskx_ref[:, gs] * dy
            dact_ref[:, SSD_D_INNER + g * SSD_STATE:SSD_D_INNER + (g + 1) * SSD_STATE] = dbg + _dot(dcbb, cg, TN)
            dact_ref[:, SSD_D_INNER + bc_w + g * SSD_STATE:SSD_D_INNER + bc_w + (g + 1) * SSD_STATE] = dcg + _dot(dcbb, bg, NN)
        dlast = t1_sum + jnp.exp(cs[SSD_CHUNK - 1:SSD_CHUNK, :]) * hsum
        dcs = dcs_col - dcs_row.T + jnp.where(last_row, dlast, 0.0)
        row = lax.broadcasted_iota(jnp.int32, (SSD_CHUNK, SSD_CHUNK), 0)
        col = lax.broadcasted_iota(jnp.int32, (SSD_CHUNK, SSD_CHUNK), 1)
        dda = _dot((col >= row).astype(F32), dcs, NN, precision=HIGHEST)
        ddt = ddt + dda * a
        da_ref[...] += jnp.sum(dda * dt, axis=0, keepdims=True)
        ddtr = jnp.where(lane < SSD_HEADS, ddt * _sigmoid(dtr + dtb_ref[...]), 0.0)
        ddtr_ref[...] = ddtr.astype(BF16)
        ddtb_ref[...] += jnp.sum(ddtr, axis=0, keepdims=True)
        ddsk_ref[...] += ddsk

    rev = lambda c: nc - 1 - c
    row_d = lambda cb: pl.BlockSpec((SSD_CHUNK, SSD_D_INNER), lambda c: (rev(c), cb))
    small = pl.BlockSpec((1, LANE), lambda c: (0, 0))
    wide = pl.BlockSpec((1, SSD_D_INNER), lambda c: (0, 0))
    small_shape = jax.ShapeDtypeStruct((1, LANE), F32)
    return _pcall(
        body, (dycat, ypre, proj, proj, act, act, act, hall, dtb, alog, _expand_heads(dsk), nw, exp_mat, ind4),
        name="ssd_bwd", grid=(nc,),
        in_specs=[row_d(0), row_d(0), row_d(OFF_Z // SSD_D_INNER),
                  pl.BlockSpec((SSD_CHUNK, LANE), lambda c: (rev(c), OFF_DT // LANE)),
                  row_d(0),
                  pl.BlockSpec((SSD_CHUNK, bc_w), lambda c: (rev(c), SSD_D_INNER // bc_w)),
                  pl.BlockSpec((SSD_CHUNK, bc_w), lambda c: (rev(c), SSD_D_INNER // bc_w + 1)),
                  pl.BlockSpec((1, SSD_D_INNER, SSD_STATE), lambda c: (rev(c), 0, 0)),
                  small, small, wide, wide, pl.BlockSpec((LANE, SSD_D_INNER), lambda c: (0, 0)),
                  pl.BlockSpec((SSD_HEADS * SSD_CHUNK, LANE), lambda c: (0, 0))],
        out_specs=[row_d(0), pl.BlockSpec((SSD_CHUNK, CONV_CH), lambda c: (rev(c), 0)),
                   pl.BlockSpec((SSD_CHUNK, LANE), lambda c: (rev(c), 0)), small, small, small, wide],
        out_shape=[jax.ShapeDtypeStruct((s, SSD_D_INNER), BF16), jax.ShapeDtypeStruct((s, CONV_CH), F32),
                   jax.ShapeDtypeStruct((s, LANE), BF16), small_shape, small_shape, small_shape,
                   jax.ShapeDtypeStruct((1, SSD_D_INNER), F32)],
        scratch_shapes=[pltpu.VMEM((SSD_D_INNER, SSD_STATE), F32)], sem=("arbitrary",), comm=comm)


def _head_mean_matrix():
    row = lax.broadcasted_iota(jnp.int32, (LANE, LANE), 0) // HEAD_DIM
    col = lax.broadcasted_iota(jnp.int32, (LANE, LANE), 1) // HEAD_DIM
    return (row == col).astype(F32)


def _head_sum2(v, ones_bd):
    hi = v.astype(BF16)
    lo = (v - hi.astype(F32)).astype(BF16)
    return _dot(jnp.concatenate([hi, lo], axis=1), jnp.concatenate([ones_bd, ones_bd], axis=0), NN)


def _head_norm(x, w, scale, ones_bd):
    ms = _head_sum2(x * x, ones_bd) * (1.0 / HEAD_DIM)
    return (x * lax.rsqrt(ms + EPS)) * (w * scale)


PRO_ROWS = 256
ATT_GROUP_FWD = 16
ATT_GROUP_BWD = 8
KEYS = 2 * ATT_BLK
NEG = -1e30
HALF = HEAD_DIM // 2


def _rows(start, size, dil):
    return pl.ds(start, size) if dil == 1 else pl.ds(start, size, stride=dil)


def _fill_bias(bias_ref):
    row = lax.broadcasted_iota(jnp.int32, (ATT_BLK, 2 * KEYS), 0)
    col = lax.broadcasted_iota(jnp.int32, (ATT_BLK, 2 * KEYS), 1) & (KEYS - 1)
    for first, off in ((0, 0), (1, ATT_BLK)):
        dist = off + row - col
        bias_ref[first] = jnp.where((dist >= 0) & (dist <= ATT_BLK), 0.0, NEG)


def _pair(a, b):
    return jnp.concatenate([jnp.broadcast_to(a, (ATT_BLK, KEYS)), jnp.broadcast_to(b, (ATT_BLK, KEYS))], axis=1)


def _split_heads(x, is_a):
    zero = jnp.zeros_like(x)
    return jnp.concatenate([jnp.where(is_a, x, zero), jnp.where(is_a, zero, x)], axis=0)


def _block_ids(b, nb):
    i = b & (nb - 1)
    q0 = pl.multiple_of(b * ATT_BLK, ATT_BLK)
    k0 = pl.multiple_of((b - jnp.minimum(i, 1)) * ATT_BLK, ATT_BLK)
    return pl.ds(q0, ATT_BLK), pl.ds(k0, KEYS), jnp.minimum(i, 1)


def _att_fwd(proj, qw, kw, comm=None):
    s = proj.shape[0]
    nblk = s // ATT_BLK
    assert all((s // d) // ATT_BLK >= 2 for d in DILATIONS)
    blk = lambda off: pl.BlockSpec((s, LANE), lambda i: (0, off // LANE + i))
    wspec = pl.BlockSpec((1, LANE), lambda i: (0, i))
    oblk = pl.BlockSpec((s, LANE), lambda i: (0, i))

    def body(q_ref, k_ref, v_ref, qw_ref, kw_ref, o_ref, lse_ref, qn, kn, q_cm, k_cm, v_cm, m_acc, l_acc, o_d, m_d, l_d, bias):
        ones_bd = _head_mean_matrix().astype(BF16)
        is_a = lax.broadcasted_iota(jnp.int32, (1, LANE), 1) < HEAD_DIM
        ones_ext = _split_heads(jnp.ones((KEYS, LANE), BF16), is_a)
        _fill_bias(bias)

        def pro(j, c):
            rows = pl.ds(pl.multiple_of(j * PRO_ROWS, PRO_ROWS), PRO_ROWS)
            qn[rows, :] = _head_norm(q_ref[rows, :], qw_ref[...], HEAD_DIM ** -0.5, ones_bd)
            kn[rows, :] = _head_norm(k_ref[rows, :], kw_ref[...], 1.0, ones_bd)
            return c

        lax.fori_loop(0, s // PRO_ROWS, pro, 0)

        for dil in DILATIONS:
            ln = s // dil
            nb = ln // ATT_BLK
            o_out, m_out, l_out = (o_ref, m_acc, l_acc) if dil == 1 else (o_d, m_d, l_d)
            for r in range(dil):
                def relayout(j, c, dil=dil, r=r, ln=ln):
                    j0 = pl.multiple_of(j * PRO_ROWS, PRO_ROWS)
                    src = _rows(r + dil * j0, PRO_ROWS, dil)
                    dst = pl.ds(r * ln + j0, PRO_ROWS)
                    q_cm[dst, :] = qn[src, :].astype(BF16)
                    k_cm[dst, :] = kn[src, :].astype(BF16)
                    v_cm[dst, :] = v_ref[src, :].astype(BF16)
                    return c

                lax.fori_loop(0, ln // PRO_ROWS, relayout, 0)

            def step(bg, c, nb=nb, o_out=o_out, m_out=m_out, l_out=l_out):
                ids = [_block_ids(bg * ATT_GROUP_FWD + u, nb) for u in range(ATT_GROUP_FWD)]
                kbs = [_split_heads(k_cm[krows, :], is_a) for _, krows, _ in ids]
                scs = [_dot(q_cm[qrows, :], kb, NT) + bias[first] for (qrows, _, first), kb in zip(ids, kbs)]
                mas = [jnp.max(sc[:, :KEYS], axis=-1, keepdims=True) for sc in scs]
                mbs = [jnp.max(sc[:, KEYS:], axis=-1, keepdims=True) for sc in scs]
                ps = [jnp.exp(sc - _pair(ma, mb)).astype(BF16) for sc, ma, mb in zip(scs, mas, mbs)]
                vbs = [jnp.concatenate([_split_heads(v_cm[krows, :], is_a), ones_ext], axis=1) for _, krows, _ in ids]
                ols = [_dot(p, vb, NN) for p, vb in zip(ps, vbs)]
                for (qrows, _, _), ol, ma, mb in zip(ids, ols, mas, mbs):
                    o_out[qrows, :] = ol[:, :LANE]
                    l_out[qrows, :] = ol[:, LANE:]
                    m_out[qrows, :] = jnp.where(is_a, ma, mb)
                return c

            lax.fori_loop(0, nblk // ATT_GROUP_FWD, step, 0)

            if dil > 1:
                for r in range(dil):
                    def merge(j, c, dil=dil, r=r, ln=ln):
                        j0 = pl.multiple_of(j * PRO_ROWS, PRO_ROWS)
                        nat = _rows(r + dil * j0, PRO_ROWS, dil)
                        cm = pl.ds(r * ln + j0, PRO_ROWS)
                        m_old, m_new = m_acc[nat, :], m_d[cm, :]
                        m = jnp.maximum(m_old, m_new)
                        a_old, a_new = jnp.exp(m_old - m), jnp.exp(m_new - m)
                        o_ref[nat, :] = a_old * o_ref[nat, :] + a_new * o_d[cm, :]
                        l_acc[nat, :] = a_old * l_acc[nat, :] + a_new * l_d[cm, :]
                        m_acc[nat, :] = m
                        return c

                    lax.fori_loop(0, ln // PRO_ROWS, merge, 0)

        def epi(j, c):
            rows = pl.ds(pl.multiple_of(j * PRO_ROWS, PRO_ROWS), PRO_ROWS)
            l = l_acc[rows, :]
            o_ref[rows, :] = o_ref[rows, :] / l
            lse_ref[rows, :] = m_acc[rows, :] + jnp.log(l)
            return c

        lax.fori_loop(0, s // PRO_ROWS, epi, 0)

    f = jax.ShapeDtypeStruct((s, ATT_D), F32)
    scr = pltpu.VMEM((s, LANE), F32)
    scb = pltpu.VMEM((s, LANE), BF16)
    return _pcall(
        body, (proj, proj, proj, qw, kw), name="att_fwd", grid=(ATT_D // LANE,),
        in_specs=[blk(OFF_Q), blk(OFF_K), blk(OFF_V), wspec, wspec], out_specs=[oblk, oblk], out_shape=[f, f],
        scratch_shapes=[scr, scr, scb, scb, scb, scr, scr, scr, scr, scr, pltpu.VMEM((2, ATT_BLK, 2 * KEYS), F32)],
        sem=("parallel",), comm=comm)


def _att_bwd(proj, do, stats, qw, kw, comm=None):
    s = proj.shape[0]
    nblk = s // ATT_BLK
    blk = lambda off: pl.BlockSpec((s, LANE), lambda i: (0, off // LANE + i))
    wspec = pl.BlockSpec((1, LANE), lambda i: (0, i))
    oblk = pl.BlockSpec((s, LANE), lambda i: (0, i))

    def body(q_ref, k_ref, v_ref, do_ref, st_ref, qw_ref, kw_ref, dq_ref, dk_ref, dv_ref, dqw_ref, dkw_ref,
             qn, kn, q_cm, do_cm, k_cm, v_cm, st_cm, dq_acc, dk_acc, dv_acc, dq_d, dk_d, dv_d, bias):
        ones_bd = _head_mean_matrix().astype(BF16)
        is_a = lax.broadcasted_iota(jnp.int32, (1, LANE), 1) < HEAD_DIM
        _fill_bias(bias)
        zero = jnp.zeros((PRO_ROWS, LANE), F32)

        def pro(j, c):
            rows = pl.ds(pl.multiple_of(j * PRO_ROWS, PRO_ROWS), PRO_ROWS)
            qn[rows, :] = _head_norm(q_ref[rows, :], qw_ref[...], HEAD_DIM ** -0.5, ones_bd)
            kn[rows, :] = _head_norm(k_ref[rows, :], kw_ref[...], 1.0, ones_bd)
            dk_acc[rows, :] = zero
            dv_acc[rows, :] = zero
            return c

        lax.fori_loop(0, s // PRO_ROWS, pro, 0)

        for dil in DILATIONS:
            ln = s // dil
            nb = ln // ATT_BLK
            dq_o, dk_o, dv_o = (dq_acc, dk_acc, dv_acc) if dil == 1 else (dq_d, dk_d, dv_d)
            for r in range(dil):
                def relayout(j, c, dil=dil, r=r, ln=ln):
                    j0 = pl.multiple_of(j * PRO_ROWS, PRO_ROWS)
                    src = _rows(r + dil * j0, PRO_ROWS, dil)
                    dst = pl.ds(r * ln + j0, PRO_ROWS)
                    q_cm[dst, :] = qn[src, :].astype(BF16)
                    k_cm[dst, :] = kn[src, :].astype(BF16)
                    v_cm[dst, :] = v_ref[src, :].astype(BF16)
                    do_cm[dst, :] = do_ref[src, :].astype(BF16)
                    st_cm[dst, :] = st_ref[src, :]
                    if dil > 1:
                        dk_d[dst, :] = zero
                        dv_d[dst, :] = zero
                    return c

                lax.fori_loop(0, ln // PRO_ROWS, relayout, 0)

            def step(bg, c, nb=nb, dq_o=dq_o, dk_o=dk_o, dv_o=dv_o):
                ids = [_block_ids(bg * ATT_GROUP_BWD + u, nb) for u in range(ATT_GROUP_BWD)]
                qbs = [q_cm[qrows, :] for qrows, _, _ in ids]
                dobs = [do_cm[qrows, :] for qrows, _, _ in ids]
                kbs = [_split_heads(k_cm[krows, :], is_a) for _, krows, _ in ids]
                vbs = [_split_heads(v_cm[krows, :], is_a) for _, krows, _ in ids]
                sts = [st_cm[qrows, :] for qrows, _, _ in ids]
                scs = [_dot(qb, kb, NT) + bias[first] for qb, kb, (_, _, first) in zip(qbs, kbs, ids)]
                dps = [_dot(dob, vb, NT) for dob, vb in zip(dobs, vbs)]
                ps = [jnp.exp(sc - _pair(st[:, 0:1], st[:, HEAD_DIM:HEAD_DIM + 1])) for sc, st in zip(scs, sts)]
                dss = [(p * (dp - _pair(st[:, HALF:HALF + 1], st[:, HEAD_DIM + HALF:HEAD_DIM + HALF + 1]))).astype(BF16)
                       for p, dp, st in zip(ps, dps, sts)]
                dqs = [_dot(ds, kb, NN) for ds, kb in zip(dss, kbs)]
                dkfs = [_dot(ds, qb, TN) for ds, qb in zip(dss, qbs)]
                dvfs = [_dot(p.astype(BF16), dob, TN) for p, dob in zip(ps, dobs)]
                for (qrows, krows, _), dq, dkf, dvf in zip(ids, dqs, dkfs, dvfs):
                    dq_o[qrows, :] = dq
                    dk_o[krows, :] += jnp.where(is_a, dkf[:KEYS], dkf[KEYS:])
                    dv_o[krows, :] += jnp.where(is_a, dvf[:KEYS], dvf[KEYS:])
                return c

            lax.fori_loop(0, nblk // ATT_GROUP_BWD, step, 0)

            if dil > 1:
                for r in range(dil):
                    def merge(j, c, dil=dil, r=r, ln=ln):
                        j0 = pl.multiple_of(j * PRO_ROWS, PRO_ROWS)
                        nat = _rows(r + dil * j0, PRO_ROWS, dil)
                        cm = pl.ds(r * ln + j0, PRO_ROWS)
                        dq_acc[nat, :] += dq_d[cm, :]
                        dk_acc[nat, :] += dk_d[cm, :]
                        dv_acc[nat, :] += dv_d[cm, :]
                        return c

                    lax.fori_loop(0, ln // PRO_ROWS, merge, 0)

        def back(dn_out, x, w, scale):
            r = lax.rsqrt(_head_sum2(x * x, ones_bd) * (1.0 / HEAD_DIM) + EPS)
            nrm = x * r
            dw = jnp.sum(dn_out * nrm, axis=0, keepdims=True) * scale
            dn = dn_out * (w * scale)
            return r * (dn - nrm * (_head_sum2(dn * nrm, ones_bd) * (1.0 / HEAD_DIM))), dw

        def epi(j, c):
            rows = pl.ds(pl.multiple_of(j * PRO_ROWS, PRO_ROWS), PRO_ROWS)
            dq, dqw = back(dq_acc[rows, :], q_ref[rows, :], qw_ref[...], HEAD_DIM ** -0.5)
            dk, dkw = back(dk_acc[rows, :], k_ref[rows, :], kw_ref[...], 1.0)
            dq_ref[rows, :] = dq.astype(BF16)
            dk_ref[rows, :] = dk.astype(BF16)
            dv_ref[rows, :] = dv_acc[rows, :].astype(BF16)
            return (c[0] + dqw, c[1] + dkw)

        zrow = jnp.zeros((1, LANE), F32)
        dqw, dkw = lax.fori_loop(0, s // PRO_ROWS, epi, (zrow, zrow))
        dqw_ref[...] = dqw
        dkw_ref[...] = dkw

    o = jax.ShapeDtypeStruct((s, ATT_D), BF16)
    ov = jax.ShapeDtypeStruct((1, ATT_D), F32)
    scr = pltpu.VMEM((s, LANE), F32)
    scb = pltpu.VMEM((s, LANE), BF16)
    return _pcall(
        body, (proj, proj, proj, do, stats, qw, kw), name="att_bwd", grid=(ATT_D // LANE,),
        in_specs=[blk(OFF_Q), blk(OFF_K), blk(OFF_V), oblk, oblk, wspec, wspec],
        out_specs=[oblk, oblk, oblk, wspec, wspec], out_shape=[o, o, o, ov, ov],
        scratch_shapes=[scr, scr, scb, scb, scb, scb, scr, scr, scr, scr, scr, scr, scr, pltpu.VMEM((2, ATT_BLK, 2 * KEYS), F32)],
        sem=("parallel",), comm=comm)


def _att_norm_fwd(o, nw, ycat):
    s = o.shape[0]
    row = pl.BlockSpec((ROW_TILE, ATT_D), lambda i: (i, 0))
    vec = pl.BlockSpec((1, ATT_D), lambda i: (0, 0))

    def body(o_ref, nw_ref, ycat_ref, y_ref):
        o = o_ref[...]
        r = lax.rsqrt(jnp.mean(o * o, axis=-1, keepdims=True) + EPS)
        y_ref[...] = (o * r * nw_ref[...]).astype(BF16)

    return pl.pallas_call(body, name="att_norm_fwd", grid=(s // ROW_TILE,),
                          in_specs=[row, vec, pl.BlockSpec(memory_space=pl.ANY)],
                          out_specs=pl.BlockSpec((ROW_TILE, ATT_D), lambda i: (i, 1)),
                          out_shape=jax.ShapeDtypeStruct(ycat.shape, BF16), input_output_aliases={2: 0},
                          compiler_params=_cparams(("parallel",)))(o, nw, ycat)


def _mixer_split_epilogue(dycat, first, rows, vecs, outs):
    (o_ref, lse_ref), (nw_ref,), (dyssd_ref, do_ref, st_ref, dnw_ref) = rows, vecs, outs

    @pl.when(first)
    def _():
        dnw_ref[...] = jnp.zeros_like(dnw_ref)

    dyssd_ref[...] = dycat[:, :SSD_D_INNER]
    dy = dycat[:, SSD_D_INNER:]
    o = o_ref[...]
    r = lax.rsqrt(jnp.mean(o * o, axis=-1, keepdims=True) + EPS)
    nrm = o * r
    dnw_ref[...] += jnp.sum(dy * nrm, axis=0, keepdims=True)
    dn = dy * nw_ref[...]
    do = r * (dn - nrm * jnp.mean(dn * nrm, axis=-1, keepdims=True))
    do_ref[...] = do
    ones_bd = _head_mean_matrix().astype(BF16)
    prod = do * o
    delta = jnp.concatenate([_head_sum2(prod[:, j * LANE:(j + 1) * LANE], ones_bd) for j in range(ATT_D // LANE)], axis=1)
    lane = lax.broadcasted_iota(jnp.int32, (1, ATT_D), 1)
    st_ref[...] = jnp.where((lane & (HEAD_DIM - 1)) < HALF, lse_ref[...], delta)


def _ada_fwd(c_all, w_ada):
    def body(c_ref, w_ref, o_ref):
        cv = c_ref[...]
        o_ref[...] = _dot((cv * _sigmoid(cv)).astype(BF16), w_ref[...].astype(BF16), NN)

    return pl.pallas_call(body, name="ada_fwd", out_shape=jax.ShapeDtypeStruct((c_all.shape[0], w_ada.shape[1]), F32),
                          compiler_params=_cparams())(c_all, w_ada)


def _adamw_math(g, w, m, v):
    m_new = ADAM_B1 * m + (1.0 - ADAM_B1) * g
    v_new = ADAM_B2 * v + (1.0 - ADAM_B2) * (g * g)
    m_hat = m_new / (1.0 - ADAM_B1 ** ADAM_STEP)
    v_hat = v_new / (1.0 - ADAM_B2 ** ADAM_STEP)
    delta = -ADAM_LR * (m_hat / (jnp.sqrt(v_hat) + ADAM_EPS) + ADAM_WD * w)
    return delta, m_new, v_new


def _ada_bwd_adamw(c_all, dmod_cols, w, m, v):
    rows, cols = w.shape
    tr = 256
    blk = pl.BlockSpec((tr, cols), lambda i: (i, 0))

    def body(c_ref, d_ref, w_ref, m_ref, v_ref, g_ref, dl_ref, mo_ref, vo_ref):
        cv = c_ref[...]
        ca = cv * _sigmoid(cv)
        g = ca[:, 0:1] * d_ref[0:1, :]
        for b in range(1, N_DEV):
            g = g + ca[:, b:b + 1] * d_ref[b:b + 1, :]
        g_ref[...] = g
        dl_ref[...], mo_ref[...], vo_ref[...] = _adamw_math(g, w_ref[...], m_ref[...], v_ref[...])

    o = jax.ShapeDtypeStruct((rows, cols), F32)
    return pl.pallas_call(
        body, name="ada_bwd_adamw", grid=(rows // tr,),
        in_specs=[pl.BlockSpec((tr, N_DEV), lambda i: (i, 0)), pl.BlockSpec((N_DEV, cols), lambda i: (0, 0)), blk, blk, blk],
        out_specs=[blk] * 4, out_shape=[o, o, o, o], compiler_params=_cparams(("parallel",)))(c_all.T, dmod_cols, w, m, v)


def _reduce_adamw(slabs, w, m, v, name):
    rows, cols = w.shape
    n_src = slabs.shape[0]
    if rows % 128 == 0:
        tr, steps = 128, rows // 128
        blk = pl.BlockSpec((tr, cols), lambda i: (i, 0))
        sblk = pl.BlockSpec((n_src, tr, cols), lambda i: (0, i, 0))
    else:
        tc, steps = 256, cols // 256
        blk = pl.BlockSpec((rows, tc), lambda i: (0, i))
        sblk = pl.BlockSpec((n_src, rows, tc), lambda i: (0, 0, i))

    def body(s_ref, w_ref, m_ref, v_ref, g_ref, dl_ref, mo_ref, vo_ref):
        g = s_ref[0].astype(F32)
        for src in range(1, n_src):
            g = g + s_ref[src].astype(F32)
        g_ref[...] = g
        dl_ref[...], mo_ref[...], vo_ref[...] = _adamw_math(g, w_ref[...], m_ref[...], v_ref[...])

    o = jax.ShapeDtypeStruct((rows, cols), F32)
    return pl.pallas_call(
        body, name=name, grid=(steps,), in_specs=[sblk, blk, blk, blk],
        out_specs=[blk] * 4, out_shape=[o, o, o, o], compiler_params=_cparams(("parallel",)))(slabs, w, m, v)


def _small_reduce_adamw(gathered, w, m, v):
    def body(s_ref, w_ref, m_ref, v_ref, g_ref, dl_ref, mo_ref, vo_ref):
        g = s_ref[0]
        for dev in range(1, N_DEV):
            g = g + s_ref[dev]
        g_ref[...] = g
        dl_ref[...], mo_ref[...], vo_ref[...] = _adamw_math(g, w_ref[...], m_ref[...], v_ref[...])

    o = jax.ShapeDtypeStruct(w.shape, F32)
    return pl.pallas_call(body, name="small_reduce_adamw", out_shape=[o, o, o, o], compiler_params=_cparams())(gathered, w, m, v)


def _adamw_small(g, w, m, v, name):
    def body(g_ref, w_ref, m_ref, v_ref, dl_ref, mo_ref, vo_ref):
        dl_ref[...], mo_ref[...], vo_ref[...] = _adamw_math(g_ref[...], w_ref[...], m_ref[...], v_ref[...])

    o = jax.ShapeDtypeStruct(w.shape, F32)
    return pl.pallas_call(body, name=name, out_shape=[o, o, o], compiler_params=_cparams())(g, w, m, v)


class _Exchange:
    def __init__(self, arrs, scatter):
        self.arrs, self.scatter, self.n = list(arrs), scatter, len(arrs)
        hbm = pl.BlockSpec(memory_space=pltpu.HBM)
        self.in_specs = [hbm] * self.n
        self.out_specs = [hbm] * self.n
        self.out_shape = [jax.ShapeDtypeStruct(a.shape if scatter else (N_DEV,) + a.shape, a.dtype) for a in self.arrs]
        self.scratch = [pltpu.SemaphoreType.DMA((self.n * (N_DEV - 1),)), pltpu.SemaphoreType.DMA((self.n * (N_DEV - 1),)),
                        pltpu.SemaphoreType.DMA((self.n,))]

    def _local(self, ins, outs, sems):
        me = 4 * lax.axis_index("x") + 2 * lax.axis_index("y") + lax.axis_index("c")
        return [pltpu.make_async_copy(ins[a].at[me] if self.scatter else ins[a], outs[a].at[me], sems[2].at[a])
                for a in range(self.n)]

    def _remote(self, ins, outs, sems, arriving):
        send_sems, recv_sems, _ = sems
        x, y, c = lax.axis_index("x"), lax.axis_index("y"), lax.axis_index("c")
        me = 4 * x + 2 * y + c
        remote = []
        for a in range(self.n):
            for k in range(1, N_DEV):
                px = 1 - x if k & 4 else x
                py = 1 - y if k & 2 else y
                pc = 1 - c if k & 1 else c
                peer = 4 * px + 2 * py + pc
                sem = a * (N_DEV - 1) + k - 1
                remote.append(pltpu.make_async_remote_copy(
                    src_ref=ins[a].at[peer] if self.scatter else ins[a], dst_ref=outs[a].at[peer if arriving else me],
                    send_sem=send_sems.at[sem], recv_sem=recv_sems.at[sem], device_id=(px, py, pc), device_id_type=MESH_IDS))
        return remote

    def start(self, ins, outs, sems):
        for cp in self._local(ins, outs, sems) + self._remote(ins, outs, sems, arriving=False):
            cp.start()

    def forward(self, ins, outs, sems):
        pass

    def wait(self, ins, outs, sems):
        for send, arrival in zip(self._remote(ins, outs, sems, arriving=False), self._remote(ins, outs, sems, arriving=True)):
            send.wait_send()
            arrival.wait_recv()
        for cp in self._local(ins, outs, sems):
            cp.wait()


N_CHIP = N_DEV // 2


class _SiblingSwap(_Exchange):
    def __init__(self, arrs):
        super().__init__(arrs, scatter=True)
        self.out_shape = [jax.ShapeDtypeStruct((N_CHIP,) + a.shape[2:], a.dtype) for a in self.arrs]
        self.scratch = [pltpu.SemaphoreType.DMA((self.n,)), pltpu.SemaphoreType.DMA((self.n,)), pltpu.SemaphoreType.DMA((1,))]

    def _copies(self, ins, outs, sems):
        x, y, c = lax.axis_index("x"), lax.axis_index("y"), lax.axis_index("c")
        return [pltpu.make_async_remote_copy(src_ref=ins[a].at[:, 1 - c], dst_ref=outs[a], send_sem=sems[0].at[a], recv_sem=sems[1].at[a],
                                             device_id=(x, y, 1 - c), device_id_type=MESH_IDS) for a in range(self.n)]

    def start(self, ins, outs, sems):
        for cp in self._copies(ins, outs, sems):
            cp.start()

    def wait(self, ins, outs, sems):
        for cp in self._copies(ins, outs, sems):
            cp.wait()


class _ChipScatter(_Exchange):
    def __init__(self, arrs):
        super().__init__(arrs, scatter=True)
        n_pairs = self.n * (N_CHIP - 1)
        self.scratch = [pltpu.SemaphoreType.DMA((n_pairs,)), pltpu.SemaphoreType.DMA((n_pairs,)), pltpu.SemaphoreType.DMA((self.n,))]

    def _local(self, ins, outs, sems):
        chip = 2 * lax.axis_index("x") + lax.axis_index("y")
        return [pltpu.make_async_copy(ins[a].at[chip], outs[a].at[chip], sems[2].at[a]) for a in range(self.n)]

    def _remote(self, ins, outs, sems, arriving):
        send_sems, recv_sems, _ = sems
        x, y, c = lax.axis_index("x"), lax.axis_index("y"), lax.axis_index("c")
        chip = 2 * x + y
        remote = []
        for a in range(self.n):
            for k in range(1, N_CHIP):
                px = 1 - x if k & 2 else x
                py = 1 - y if k & 1 else y
                peer = 2 * px + py
                sem = a * (N_CHIP - 1) + k - 1
                remote.append(pltpu.make_async_remote_copy(
                    src_ref=ins[a].at[peer], dst_ref=outs[a].at[peer if arriving else chip], send_sem=send_sems.at[sem],
                    recv_sem=recv_sems.at[sem], device_id=(px, py, c), device_id_type=MESH_IDS))
        return remote


def _chip_sum(mine, theirs):
    n, rows, cols = mine.shape
    blk = pl.BlockSpec((1, rows, 256), lambda q, j: (q, 0, j))

    def body(a_ref, b_ref, o_ref):
        o_ref[...] = (a_ref[...].astype(F32) + b_ref[...].astype(F32)).astype(BF16)

    return pl.pallas_call(body, name="chip_sum", grid=(n, cols // 256), in_specs=[blk, blk], out_specs=blk,
                          out_shape=jax.ShapeDtypeStruct(mine.shape, BF16),
                          compiler_params=_cparams(("parallel", "parallel")))(mine, theirs)


class _Gather2(_Exchange):
    def __init__(self, arrs):
        super().__init__(arrs, scatter=False)

    def _copies(self, ins, outs, sems):
        send_sems, recv_sems, _ = sems
        x, y, c = lax.axis_index("x"), lax.axis_index("y"), lax.axis_index("c")
        sibling = (x, y, 1 - c)
        chips = [(1 - x, y), (x, 1 - y), (1 - x, 1 - y)]
        first, passed, landed = [], [], []
        for a in range(self.n):
            def copy(k, block, to, src=None, a=a):
                slab = outs[a].at[4 * block[0] + 2 * block[1] + block[2]]
                return pltpu.make_async_remote_copy(
                    src_ref=slab if src is None else src, dst_ref=slab, send_sem=send_sems.at[a * (N_DEV - 1) + k],
                    recv_sem=recv_sems.at[a * (N_DEV - 1) + k], device_id=to, device_id_type=MESH_IDS)

            first.append(copy(0, (x, y, c), sibling, src=ins[a]))
            landed.append(copy(0, sibling, sibling))
            for j, chip in enumerate(chips):
                first.append(copy(1 + j, (x, y, c), (*chip, c), src=ins[a]))
                passed.append((copy(1 + j, (*chip, c), sibling), copy(4 + j, (*chip, c), sibling)))
                landed.append(copy(4 + j, (*chip, 1 - c), sibling))
        return first, passed, landed

    def start(self, ins, outs, sems):
        for cp in self._local(ins, outs, sems) + self._copies(ins, outs, sems)[0]:
            cp.start()

    def forward(self, ins, outs, sems):
        for arrival, onward in self._copies(ins, outs, sems)[1]:
            arrival.wait_recv()
            onward.start()

    def wait(self, ins, outs, sems):
        first, passed, landed = self._copies(ins, outs, sems)
        for arrival in landed:
            arrival.wait_recv()
        for cp in first + [onward for _, onward in passed]:
            cp.wait_send()
        for cp in self._local(ins, outs, sems):
            cp.wait()


def _split_comm_refs(refs, n_in, n_out, n_scr, comm):
    nc = comm.n if comm is not None else 0
    ns = 3 if comm is not None else 0
    pos, groups = 0, []
    for cnt in (n_in, nc, n_out, nc, n_scr, ns):
        groups.append(refs[pos:pos + cnt])
        pos += cnt
    assert pos == len(refs), (pos, len(refs))
    return groups


def _pcall(body, args, *, name, grid, in_specs, out_specs, out_shape, scratch_shapes=(), sem=None, comm=None):
    in_specs, out_specs, out_shape, scratch_shapes = list(in_specs), list(out_specs), list(out_shape), list(scratch_shapes)
    n_in, n_out, n_scr = len(in_specs), len(out_specs), len(scratch_shapes)
    if comm is None:
        kernel_body = body
    else:
        def kernel_body(*refs):
            ins, cins, outs, couts, scr, sems = _split_comm_refs(refs, n_in, n_out, n_scr, comm)
            ids = [pl.program_id(a) for a in range(len(grid))]
            first, last = ids[0] == 0, ids[0] == grid[0] - 1
            for a in range(1, len(grid)):
                first, last = first & (ids[a] == 0), last & (ids[a] == grid[a] - 1)

            middle = ids[0] == (2 * grid[0]) // 3
            for a in range(1, len(grid)):
                middle = middle & (ids[a] == 0)

            @pl.when(first)
            def _():
                comm.start(cins, couts, sems)

            @pl.when(middle)
            def _():
                comm.forward(cins, couts, sems)

            body(*ins, *outs, *scr)

            @pl.when(last)
            def _():
                comm.wait(cins, couts, sems)

        in_specs, out_specs, out_shape = in_specs + comm.in_specs, out_specs + comm.out_specs, out_shape + comm.out_shape
        scratch_shapes, args = scratch_shapes + comm.scratch, list(args) + comm.arrs
        sem = ("arbitrary",) * len(grid)
    res = pl.pallas_call(kernel_body, name=name, grid=grid, in_specs=in_specs, out_specs=out_specs, out_shape=out_shape,
                         scratch_shapes=scratch_shapes, compiler_params=_cparams(sem))(*args)
    return res[:n_out], res[n_out:]


def _exchange(arrs, name, scatter=False, ex=None):
    if ex is None:
        ex = _Exchange(arrs, scatter=True) if scatter else _Gather2(arrs)

    def body(*refs):
        _, ins, _, outs, _, sems = _split_comm_refs(refs, 0, 0, 0, ex)
        ex.start(ins, outs, sems)
        ex.forward(ins, outs, sems)
        ex.wait(ins, outs, sems)

    return pl.pallas_call(body, name=name, in_specs=ex.in_specs, out_specs=ex.out_specs, out_shape=ex.out_shape,
                          scratch_shapes=ex.scratch)(*ex.arrs)


def _pad_lanes(v, width=LANE):
    return jnp.pad(v, ((0, 0), (0, width - v.shape[1])))


def _shards_to_cols(g):
    return jnp.transpose(g, (1, 0, 2)).reshape(g.shape[1], N_DEV * g.shape[2])


def _cols_to_shards(w):
    return w.astype(BF16).reshape(w.shape[0], N_DEV, w.shape[1] // N_DEV).transpose(1, 0, 2)


def _local_step(x, tgt, mod, w_in_pt, conv_w, conv_b, dt_bias, a_log, d_skip, ssd_norm_w, q_norm_w, k_norm_w,
                attn_norm_w, w_out_sh, w_ff1_sh, w_ff2_sh, norm1_w, norm2_w, core):
    shift1, scale1, gate1, shift2, scale2, gate2 = [mod[i:i + 1] for i in range(N_MOD)]
    dtb, alog, dsk = _pad_lanes(dt_bias), _pad_lanes(a_log), _pad_lanes(d_skip)
    qw, kw = jnp.tile(q_norm_w, (1, ATT_HEADS)), jnp.tile(k_norm_w, (1, ATT_HEADS))

    h1 = _norm_mod_fwd(x, norm1_w, scale1, shift1, "norm1_fwd")
    proj = _matmul(h1, w_in_pt, tb=True, tm=2048, tn=896, tk=1024, name="in_proj")
    pre, act = _conv_fwd(proj, conv_w, conv_b)
    ypre, ycat_ssd, hall = _ssd_fwd(proj, act, dtb, alog, dsk, ssd_norm_w)
    (o_att, lse), (w_out_g, w_ff1_g, w_ff2_g) = _att_fwd(proj, qw, kw, comm=_Gather2([w_out_sh, w_ff1_sh, w_ff2_sh]))
    w_out = w_out_g.reshape(2 * D_MODEL, D_MODEL)
    w_ff1 = _shards_to_cols(w_ff1_g)
    w_ff2 = w_ff2_g.reshape(D_FF, D_MODEL)
    ycat = _att_norm_fwd(o_att, attn_norm_w, ycat_ssd)
    row32, row16, vec32 = ("row", F32), ("row", BF16), ("vec", F32)
    mix, x1, h2 = _matmul_rows(ycat, w_out, _residual_norm_epilogue, [x], [gate1, norm2_w, scale2, shift2],
                               [row32, row32, row16], tm=512, name="out_proj")
    u, act_ff = _matmul(h2, w_ff1, tm=1024, tn=2048, tk=1024, name="ff1", mode="relu2")
    loss, dout, dff, dgate2 = _matmul_rows(act_ff, w_ff2, _loss_epilogue, [x1, tgt], [gate2],
                                           [("one", F32), row32, row16, vec32], tm=512, name="ff2")

    du = _matmul(dff, w_ff2, tb=True, tm=512, tn=4096, tk=1024, out_dtype=BF16, name="ff2_dx", mode="drelu2", u=u)
    g_ff2 = _matmul(act_ff, dff, ta=True, tm=512, tn=1024, tk=4096, out_dtype=BF16, name="ff2_dw")
    dx1, dshift2, dscale2, g_norm2, dmix, dgate1 = _matmul_rows(
        du, w_ff1, _norm_bwd_epilogue, [x1, dout, mix], [norm2_w, scale2, gate1],
        [row32, vec32, vec32, vec32, row16, vec32], tb=True, tm=512, name="ff1_dx")
    g_ff1 = _matmul(h2, du, ta=True, tm=1024, tn=D_FF // N_DEV, tk=4096, out_dtype=BF16, name="ff1_dw", shard_out=True)

    dy_ssd, do, stats, g_attn_norm = _matmul_rows(
        dmix, w_out, _mixer_split_epilogue, [o_att, lse], [attn_norm_w],
        [("row", F32, SSD_D_INNER), ("row", F32, ATT_D), ("row", F32, ATT_D), ("vec", F32, ATT_D)], tb=True, tm=512, name="out_proj_dx")
    g_out = _matmul(ycat, dmix, ta=True, tm=512, tn=1024, tk=4096, out_dtype=BF16, name="out_proj_dw")
    ff_slabs = [g_ff1, g_ff2.reshape(N_DEV, D_FF // N_DEV, D_MODEL)]
    (dq, dk, dv, dqw, dkw), (s_ff1, s_ff2) = _att_bwd(proj, do, stats, qw, kw, comm=_Exchange(ff_slabs, scatter=True))
    out_slabs = [g_out.astype(BF16).reshape(N_DEV, 2 * D_MODEL // N_DEV, D_MODEL)]
    (dz, dact, ddtr, da, g_dsk, g_dtb, g_ssd_norm), (s_out,) = _ssd_bwd(
        dy_ssd, ypre, proj, act, hall, dtb, alog, dsk, ssd_norm_w, comm=_Exchange(out_slabs, scatter=True))
    dxbc, g_conv_w, g_conv_b = _conv_bwd(dact, pre, proj, conv_w)
    dproj = [(dz, OFF_Z), (dxbc, OFF_XBC), (ddtr, OFF_DT), (dq, OFF_Q), (dk, OFF_K), (dv, OFF_V)]
    g_head, g_tail = _pieces_t_matmul([[dz, dxbc], [dq, dk, dv]], h1, tm=256, name="in_proj_dw")
    g_dt = _matmul(ddtr, h1, ta=True, tm=LANE, tn=1024, tk=4096, out_dtype=BF16, name="in_proj_dw_dt")[:SSD_HEADS]
    in_slabs = jnp.concatenate([g_head, g_dt, g_tail], axis=0).reshape(N_CHIP, 2, IN_W // N_DEV, D_MODEL)
    (sibling_slabs,) = _exchange(None, "swap_w_in_grads", ex=_SiblingSwap([in_slabs]))
    chip_slabs = _chip_sum(lax.dynamic_index_in_dim(in_slabs, core, axis=1, keepdims=False), sibling_slabs)
    (grad_x, dshift1, dscale1, g_norm1), (s_in,) = _matmul_rows(
        dproj, w_in_pt, _norm_bwd_epilogue, [x, dx1], [norm1_w, scale1], [row32, vec32, vec32, vec32],
        tm=256, name="in_proj_dx", comm=_ChipScatter([chip_slabs]))

    dmod = jnp.concatenate([dshift1, dscale1, dgate1, dshift2, dscale2, dgate2], axis=0)
    g_alog = da[:, :SSD_HEADS] * (-jnp.exp(a_log))
    g_qw = dqw.reshape(ATT_HEADS, HEAD_DIM).sum(axis=0, keepdims=True)
    g_kw = dkw.reshape(ATT_HEADS, HEAD_DIM).sum(axis=0, keepdims=True)
    return dict(loss=loss, grad_x=grad_x, dmod=dmod, norm1_w=g_norm1, norm2_w=g_norm2, w_in=s_in, conv_w=g_conv_w,
                conv_b=g_conv_b, dt_bias=g_dtb[:, :SSD_HEADS], a_log=g_alog, d_skip=g_dsk[:, :SSD_HEADS],
                ssd_norm_w=g_ssd_norm, q_norm_w=g_qw, k_norm_w=g_kw, attn_norm_w=g_attn_norm, w_out=s_out,
                w_ff1=s_ff1, w_ff2=s_ff2)


def _pack_w_in_rows(wt_full):
    cut = OFF_DT + SSD_HEADS
    pad = jnp.zeros((LANE - SSD_HEADS, wt_full.shape[1]), wt_full.dtype)
    return jnp.concatenate([wt_full[:cut], pad, wt_full[cut:]], axis=0)


MISC_FIELDS = (("dt_bias", SSD_HEADS), ("a_log", SSD_HEADS), ("d_skip", SSD_HEADS), ("q_norm_w", HEAD_DIM), ("k_norm_w", HEAD_DIM))
SMALL_LAYOUT = (("b_ada", 6), ("norm1_w", 1), ("norm2_w", 1), ("conv_w", 8), ("conv_b", 2), ("ssd_norm_w", 1),
                ("attn_norm_w", 1), ("misc", 1))


def _pack_small(vals):
    rows = []
    for name, nrow in SMALL_LAYOUT:
        if name == "misc":
            misc = jnp.concatenate([vals[f].reshape(1, n) for f, n in MISC_FIELDS], axis=1)
            rows.append(_pad_lanes(misc, D_MODEL))
        elif name in vals:
            rows.append(vals[name].reshape(nrow, D_MODEL))
        else:
            rows.append(jnp.zeros((nrow, D_MODEL), F32))
    used = sum(n for _, n in SMALL_LAYOUT)
    rows.append(jnp.zeros((SMALL_ROWS - used, D_MODEL), F32))
    return jnp.concatenate(rows, axis=0)


def _unpack_small(packed):
    out, r = {}, 0
    for name, nrow in SMALL_LAYOUT:
        blk = packed[r:r + nrow]
        r += nrow
        if name == "misc":
            c0 = 0
            for f, n in MISC_FIELDS:
                out[f] = blk[:, c0:c0 + n]
                c0 += n
        elif name == "b_ada":
            out[name] = blk.reshape(1, N_MOD * D_MODEL)
        elif name == "conv_w":
            out[name] = blk.reshape(CONV_K, CONV_CH)
        elif name == "conv_b":
            out[name] = blk.reshape(1, CONV_CH)
        else:
            out[name] = blk
    return out


WEIGHT_NAMES = ("norm1_w", "norm2_w", "w_ada", "b_ada", "w_in", "conv_w", "conv_b", "dt_bias", "a_log", "d_skip",
                "ssd_norm_w", "q_norm_w", "k_norm_w", "attn_norm_w", "w_out", "w_ff1", "w_ff2")
SMALL_NAMES = ("norm1_w", "norm2_w", "b_ada", "conv_b", "dt_bias", "a_log", "d_skip", "ssd_norm_w", "q_norm_w",
               "k_norm_w", "attn_norm_w")


def kernel(x, c, norm1_w, norm2_w, w_ada, b_ada, w_in, conv_w, conv_b, dt_bias, a_log, d_skip, ssd_norm_w, q_norm_w, k_norm_w, attn_norm_w, w_out, w_ff1, w_ff2, loss_target, m_norm1_w, m_norm2_w, m_w_ada, m_b_ada, m_w_in, m_conv_w, m_conv_b, m_dt_bias, m_a_log, m_d_skip, m_ssd_norm_w, m_q_norm_w, m_k_norm_w, m_attn_norm_w, m_w_out, m_w_ff1, m_w_ff2, v_norm1_w, v_norm2_w, v_w_ada, v_b_ada, v_w_in, v_conv_w, v_conv_b, v_dt_bias, v_a_log, v_d_skip, v_ssd_norm_w, v_q_norm_w, v_k_norm_w, v_attn_norm_w, v_w_out, v_w_ff1, v_w_ff2):
    args = dict(locals())
    w = {n: args[n] for n in WEIGHT_NAMES}
    m = {n: args["m_" + n] for n in WEIGHT_NAMES}
    v = {n: args["v_" + n] for n in WEIGHT_NAMES}
    me = 4 * lax.axis_index("x") + 2 * lax.axis_index("y") + lax.axis_index("c")

    c_rows = jnp.pad(c, ((0, 7), (0, 0)))
    w_in_t, m_in_t, v_in_t = [jnp.transpose(t["w_in"][0]) for t in (w, m, v)]
    c_g, conv_g, w_in_g = _exchange([c_rows, w["conv_w"][0], w_in_t.astype(BF16)], "gather_w_in", scatter=False)
    c_all = c_g[:, 0, :]
    conv_full = _shards_to_cols(conv_g)
    w_in_pt = _pack_w_in_rows(w_in_g.reshape(IN_W, D_MODEL))

    mod_part = _ada_fwd(c_all, w["w_ada"][0])
    (mod_g,) = _exchange([mod_part], "gather_mod", scatter=False)
    mod_mine = lax.dynamic_index_in_dim(mod_g, me, axis=1, keepdims=False).reshape(1, N_MOD * D_MODEL) + w["b_ada"]
    mod = mod_mine.reshape(N_MOD, D_MODEL)

    res = _local_step(x[0], loss_target[0], mod, w_in_pt, conv_full, w["conv_b"], w["dt_bias"], w["a_log"], w["d_skip"],
                      w["ssd_norm_w"], w["q_norm_w"], w["k_norm_w"], w["attn_norm_w"], w["w_out"][0].astype(BF16),
                      w["w_ff1"][0].astype(BF16), w["w_ff2"][0].astype(BF16), w["norm1_w"], w["norm2_w"], lax.axis_index("c"))

    small_vals = {n: res[n] for n in SMALL_NAMES if n != "b_ada"}
    small_vals["b_ada"] = res["dmod"]
    small_vals["conv_w"] = res["conv_w"]
    (small_g,) = _exchange([_pack_small(small_vals)], "gather_small", scatter=False)

    grads, delta, new_m, new_v = {}, {}, {}, {}
    for name in ("w_out", "w_ff1", "w_ff2"):
        outs = _reduce_adamw(res[name], w[name][0], m[name][0], v[name][0], "adamw_" + name)
        grads[name], delta[name], new_m[name], new_v[name] = [o[None] for o in outs]
    outs = _reduce_adamw(res["w_in"], w_in_t, m_in_t, v_in_t, "adamw_w_in")
    grads["w_in"], delta["w_in"], new_m["w_in"], new_v["w_in"] = [jnp.transpose(o)[None] for o in outs]

    sm = _small_reduce_adamw(small_g, _pack_small({n: w[n] for n in SMALL_NAMES}), _pack_small({n: m[n] for n in SMALL_NAMES}),
                             _pack_small({n: v[n] for n in SMALL_NAMES}))
    sm = [_unpack_small(p) for p in sm]
    for n in SMALL_NAMES:
        grads[n], delta[n], new_m[n], new_v[n] = [p[n] for p in sm]
    shard_w = CONV_CH // N_DEV
    g_conv = lax.dynamic_slice_in_dim(sm[0]["conv_w"], me * shard_w, shard_w, axis=1)
    cw = _adamw_small(g_conv, w["conv_w"][0], m["conv_w"][0], v["conv_w"][0], "adamw_conv_w")
    grads["conv_w"] = g_conv[None]
    delta["conv_w"], new_m["conv_w"], new_v["conv_w"] = [o[None] for o in cw]

    ada_w = w_ada.shape[2]
    dmod_all = small_g[:, :N_MOD, :].reshape(N_DEV, N_MOD * D_MODEL)
    dmod_cols = lax.dynamic_slice_in_dim(dmod_all, me * ada_w, ada_w, axis=1)
    outs = _ada_bwd_adamw(c_all, dmod_cols, w["w_ada"][0], m["w_ada"][0], v["w_ada"][0])
    grads["w_ada"], delta["w_ada"], new_m["w_ada"], new_v["w_ada"] = [o[None] for o in outs]

    loss = lax.psum(res["loss"][0, 0], ("x", "y", "c"))
    return (loss, res["grad_x"][None], *[grads[n] for n in WEIGHT_NAMES], *[delta[n] for n in WEIGHT_NAMES],
            *[new_m[n] for n in WEIGHT_NAMES], *[new_v[n] for n in WEIGHT_NAMES])
```

```python
import functools

import jax
import jax.numpy as jnp
from jax import lax
from jax.experimental import pallas as pl
from jax.experimental.pallas import tpu as pltpu

F32 = jnp.float32
BF16 = jnp.bfloat16
HIGHEST = lax.Precision.HIGHEST
MESH_IDS = pl.DeviceIdType.MESH

N_DEV = 8
D_MODEL = 1024
HEAD_DIM = 64
SSD_HEADS = 16
SSD_GROUPS = 4
HEADS_PER_GROUP = SSD_HEADS // SSD_GROUPS
SSD_STATE = 128
SSD_CHUNK = 128
SSD_D_INNER = SSD_HEADS * HEAD_DIM
GROUP_WIDTH = SSD_D_INNER // SSD_GROUPS
CONV_K = 4
CONV_CH = SSD_D_INNER + 2 * SSD_GROUPS * SSD_STATE
ATT_HEADS = 16
ATT_D = ATT_HEADS * HEAD_DIM
ATT_BLK = 128
DILATIONS = (1, 4, 16)
D_FF = 4 * D_MODEL
N_MOD = 6
EPS = 1e-6
IN_W = SSD_D_INNER + CONV_CH + SSD_HEADS + 3 * ATT_D
LANE = 128
OFF_Z, OFF_XBC, OFF_DT = 0, SSD_D_INNER, SSD_D_INNER + CONV_CH
OFF_Q = OFF_DT + LANE
OFF_K, OFF_V = OFF_Q + ATT_D, OFF_Q + 2 * ATT_D
IN_WP = OFF_V + ATT_D

ADAM_LR, ADAM_B1, ADAM_B2, ADAM_EPS, ADAM_WD, ADAM_STEP = 0.001, 0.9, 0.999, 1e-08, 0.01, 10
VMEM_LIMIT = 56 * 1024 * 1024
ROW_TILE = 512
SMALL_ROWS = 24


def _cparams(sem=None):
    return pltpu.CompilerParams(dimension_semantics=sem, vmem_limit_bytes=VMEM_LIMIT)


def _sigmoid(v):
    return 1.0 / (1.0 + jnp.exp(-v))


def _softplus(v):
    y = jnp.exp(-jnp.abs(v))
    small = y * (1.0 - y * (0.5 - y * (1.0 / 3.0)))
    return jnp.maximum(v, 0.0) + jnp.where(y < 0.01, small, jnp.log(1.0 + y))


def _dot(a, b, dims, precision=None):
    return lax.dot_general(a, b, (dims, ((), ())), preferred_element_type=F32, precision=precision)


NN = ((1,), (0,))
NT = ((1,), (1,))
TN = ((0,), (0,))


def _matmul(a, b, *, ta=False, tb=False, tm, tn, tk, out_dtype=F32, name, mode=None, u=None, comm=None, shard_out=False):
    m, k = (a.shape[1], a.shape[0]) if ta else a.shape
    n = b.shape[0] if tb else b.shape[1]
    assert m % tm == 0 and n % tn == 0 and k % tk == 0, (name, m, n, k)
    nk = k // tk
    a_spec = pl.BlockSpec((tk, tm), lambda i, j, kk: (kk, i)) if ta else pl.BlockSpec((tm, tk), lambda i, j, kk: (i, kk))
    b_spec = pl.BlockSpec((tn, tk), lambda i, j, kk: (j, kk)) if tb else pl.BlockSpec((tk, tn), lambda i, j, kk: (kk, j))
    o_spec = pl.BlockSpec((tm, tn), lambda i, j, kk: (i, j))
    dims = ((0,) if ta else (1,), (1,) if tb else (0,))
    n_out = 2 if mode == "relu2" else 1

    def body(*refs):
        if mode == "drelu2":
            a_ref, b_ref, u_ref = refs[:3]
            rest = refs[3:]
        else:
            a_ref, b_ref = refs[:2]
            u_ref = None
            rest = refs[2:]
        outs = rest[:n_out]
        part = _dot(a_ref[...], b_ref[...], dims)

        def finish(r):
            if mode == "relu2":
                outs[0][...] = r.astype(BF16)
                rr = jnp.maximum(r, 0.0)
                outs[1][...] = (rr * rr).astype(BF16)
            elif mode == "drelu2":
                outs[0][...] = (r * (2.0 * jnp.maximum(u_ref[...].astype(F32), 0.0))).astype(out_dtype)
            else:
                outs[0][...] = r.astype(out_dtype)

        if nk == 1:
            finish(part)
        else:
            acc = rest[n_out]
            kk = pl.program_id(2)

            @pl.when(kk == 0)
            def _():
                acc[...] = part

            @pl.when(kk > 0)
            def _():
                acc[...] += part

            @pl.when(kk == nk - 1)
            def _():
                finish(acc[...])

    in_specs = [a_spec, b_spec]
    args = [a, b]
    if mode == "drelu2":
        in_specs.append(o_spec)
        args.append(u)
    if mode == "relu2":
        out_shape = [jax.ShapeDtypeStruct((m, n), BF16), jax.ShapeDtypeStruct((m, n), BF16)]
    elif shard_out:
        out_shape = [jax.ShapeDtypeStruct((n // tn, m, tn), out_dtype)]
        o_spec = pl.BlockSpec((None, tm, tn), lambda i, j, kk: (j, i, 0))
    else:
        out_shape = [jax.ShapeDtypeStruct((m, n), out_dtype)]
    outs, comm_outs = _pcall(
        body, args, name=name, grid=(m // tm, n // tn, nk), in_specs=in_specs, out_specs=[o_spec] * n_out,
        out_shape=out_shape, scratch_shapes=[pltpu.VMEM((tm, tn), F32)] if nk > 1 else [],
        sem=("parallel", "parallel", "arbitrary"), comm=comm)
    res = tuple(outs) if mode == "relu2" else outs[0]
    return res if comm is None else (res, comm_outs)


def _pieces_t_matmul(groups, b, *, tm, name):
    k, n = b.shape
    pieces = [p for g in groups for p in g]
    starts, tiles = [], 0
    for p in pieces:
        assert p.shape[0] == k and p.shape[1] % tm == 0, (name, p.shape)
        starts.append(tiles)
        tiles += p.shape[1] // tm
    group_of, group_start, group_tiles = [], [], []
    for gi, g in enumerate(groups):
        group_start.append(starts[len(group_of)])
        group_of += [gi] * len(g)
        group_tiles.append(sum(p.shape[1] // tm for p in g))

    def clipped(block, start, count):
        return pl.BlockSpec(block, (lambda i: (0, jnp.clip(i - start, 0, count - 1))) if block[0] == k
                            else (lambda i: (jnp.clip(i - start, 0, count - 1), 0)))

    def body(*refs):
        a_refs, b_ref, o_refs = refs[:len(pieces)], refs[len(pieces)], refs[len(pieces) + 1:]
        i = pl.program_id(0)
        for a_ref, start, p, gi in zip(a_refs, starts, pieces, group_of):
            @pl.when((i >= start) & (i < start + p.shape[1] // tm))
            def _(a_ref=a_ref, o_ref=o_refs[gi]):
                o_ref[...] = _dot(a_ref[...], b_ref[...], TN).astype(BF16)

    return pl.pallas_call(
        body, name=name, grid=(tiles,),
        in_specs=[clipped((k, tm), s0, p.shape[1] // tm) for s0, p in zip(starts, pieces)] + [pl.BlockSpec((k, n), lambda i: (0, 0))],
        out_specs=[clipped((tm, n), s0, cnt) for s0, cnt in zip(group_start, group_tiles)],
        out_shape=[jax.ShapeDtypeStruct((cnt * tm, n), BF16) for cnt in group_tiles],
        compiler_params=_cparams(("arbitrary",)))(*pieces, b)


def _rms_mod(xv, nw, scale, shift):
    r = lax.rsqrt(jnp.mean(xv * xv, axis=-1, keepdims=True) + EPS)
    return ((xv * r) * nw * (1.0 + scale) + shift).astype(BF16)


def _norm_mod_fwd(x, nw, scale, shift, name):
    s, d = x.shape
    row = pl.BlockSpec((ROW_TILE, d), lambda i: (i, 0))
    vec = pl.BlockSpec((1, d), lambda i: (0, 0))

    def body(x_ref, nw_ref, sc_ref, sh_ref, h_ref):
        h_ref[...] = _rms_mod(x_ref[...], nw_ref[...], sc_ref[...], sh_ref[...])

    return pl.pallas_call(body, name=name, grid=(s // ROW_TILE,), in_specs=[row, vec, vec, vec], out_specs=row,
                          out_shape=jax.ShapeDtypeStruct((s, d), BF16), compiler_params=_cparams(("parallel",)))(x, nw, scale, shift)


def _matmul_rows(a, b, epilogue, row_in, vec_in, outs, *, tb=False, tm, name, comm=None):
    pieces = a if isinstance(a, list) else [(a, 0)]
    assert not (tb and len(pieces) > 1)
    m = pieces[0][0].shape[0]
    n = b.shape[0] if tb else b.shape[1]
    assert m % tm == 0, (name, m, tm)
    dims = ((1,), (1,) if tb else (0,))
    n_a, n_row, n_vec = len(pieces), len(row_in), len(vec_in)

    def body(*refs):
        a_refs, b_ref, rest = refs[:n_a], refs[n_a], refs[n_a + 1:]
        if n_a == 1:
            c = _dot(a_refs[0][...], b_ref[...], dims)
        else:
            c = None
            for a_ref, (piece, off) in zip(a_refs, pieces):
                part = _dot(a_ref[...], b_ref[off:off + piece.shape[1], :], dims)
                c = part if c is None else c + part
        epilogue(c, pl.program_id(0) == 0, rest[:n_row], rest[n_row:n_row + n_vec], rest[n_row + n_vec:])

    def spec(kind, width):
        block = {"row": (tm, width), "vec": (1, width), "one": (1, 1)}[kind]
        return pl.BlockSpec(block, (lambda i: (i, 0)) if kind == "row" else (lambda i: (0, 0)))

    def shape(kind, width):
        return {"row": (m, width), "vec": (1, width), "one": (1, 1)}[kind]

    outs = [(o[0], o[1], o[2] if len(o) > 2 else n) for o in outs]
    res, comm_outs = _pcall(
        body, [*[p for p, _ in pieces], b, *row_in, *vec_in], name=name, grid=(m // tm,),
        in_specs=[spec("row", p.shape[1]) for p, _ in pieces] + [pl.BlockSpec(b.shape, lambda i: (0, 0))]
        + [spec("row", r.shape[1]) for r in row_in] + [spec("vec", v.shape[1]) for v in vec_in],
        out_specs=[spec(kind, width) for kind, _, width in outs],
        out_shape=[jax.ShapeDtypeStruct(shape(kind, width), dt) for kind, dt, width in outs],
        sem=("arbitrary",), comm=comm)
    return res if comm is None else (res, comm_outs)


def _residual_norm_epilogue(mix, first, rows, vecs, outs):
    (x_ref,), (gate_ref, nw_ref, sc_ref, sh_ref), (mix_ref, x1_ref, h_ref) = rows, vecs, outs
    xv = x_ref[...] + gate_ref[...] * mix
    mix_ref[...] = mix
    x1_ref[...] = xv
    h_ref[...] = _rms_mod(xv, nw_ref[...], sc_ref[...], sh_ref[...])


def _loss_epilogue(ff, first, rows, vecs, outs):
    (x1_ref, t_ref), (g_ref,), (loss_ref, dout_ref, dff_ref, dg_ref) = rows, vecs, outs
    d = ff.shape[1]

    @pl.when(first)
    def _():
        loss_ref[...] = jnp.zeros_like(loss_ref)
        dg_ref[...] = jnp.zeros_like(dg_ref)

    err = x1_ref[...] + g_ref[...] * ff - t_ref[...]
    loss_ref[...] += (0.5 / d) * jnp.sum(err * err).reshape(1, 1)
    dout = err * (1.0 / d)
    dout_ref[...] = dout
    dff_ref[...] = (g_ref[...] * dout).astype(BF16)
    dg_ref[...] += jnp.sum(dout * ff, axis=0, keepdims=True)


def _norm_bwd_epilogue(dh, first, rows, vecs, outs):
    with_gate = len(vecs) == 3
    x_ref, dres_ref = rows[:2]
    nw_ref, sc_ref = vecs[:2]
    dx_ref, dsh_ref, dsc_ref, dnw_ref = outs[:4]

    @pl.when(first)
    def _():
        for ref in outs[1:4] + outs[5:]:
            ref[...] = jnp.zeros_like(ref)

    xv = x_ref[...]
    r = lax.rsqrt(jnp.mean(xv * xv, axis=-1, keepdims=True) + EPS)
    nrm = xv * r
    one_sc = 1.0 + sc_ref[...]
    dhn = dh * nrm
    dsh_ref[...] += jnp.sum(dh, axis=0, keepdims=True)
    dsc_ref[...] += jnp.sum(dhn, axis=0, keepdims=True) * nw_ref[...]
    dnw_ref[...] += jnp.sum(dhn, axis=0, keepdims=True) * one_sc
    dn = dh * (nw_ref[...] * one_sc)
    dx = dres_ref[...] + r * (dn - nrm * jnp.mean(dn * nrm, axis=-1, keepdims=True))
    dx_ref[...] = dx
    if with_gate:
        outs[4][...] = (vecs[2][...] * dx).astype(BF16)
        outs[5][...] += jnp.sum(dx * rows[2][...], axis=0, keepdims=True)


CONV_COLS = 256
CONV_FWD_ROWS = 2048
CONV_BWD_ROWS = 1024
CONV_SUB_ROWS = 128
HALO = 8


def _shift_down(cur, halo, k):
    if k == 0:
        return cur
    rolled = pltpu.roll(cur, k, axis=0)
    top = jnp.where(lax.broadcasted_iota(jnp.int32, halo.shape, 0) < k, pltpu.roll(halo, k, axis=0), rolled[:HALO])
    return jnp.concatenate([top, rolled[HALO:]], axis=0)


def _shift_up(cur, halo, k):
    if k == 0:
        return cur
    t = cur.shape[0]
    rolled = pltpu.roll(cur, t - k, axis=0)
    bot = jnp.where(lax.broadcasted_iota(jnp.int32, halo.shape, 0) >= HALO - k, pltpu.roll(halo, HALO - k, axis=0),
                    rolled[t - HALO:])
    return jnp.concatenate([rolled[:t - HALO], bot], axis=0)


def _conv_fwd(proj, conv_w, conv_b):
    s = proj.shape[0]
    nr = s // CONV_FWD_ROWS
    cb0 = OFF_XBC // CONV_COLS
    hb = CONV_FWD_ROWS // HALO
    cur = pl.BlockSpec((CONV_FWD_ROWS, CONV_COLS), lambda j, r: (r, cb0 + j))
    prev = pl.BlockSpec((HALO, CONV_COLS), lambda j, r: (jnp.maximum(r * hb - 1, 0), cb0 + j))
    out = pl.BlockSpec((CONV_FWD_ROWS, CONV_COLS), lambda j, r: (r, j))

    def body(u_ref, up_ref, w_ref, b_ref, pre_ref, act_ref):
        r = pl.program_id(1)
        u = u_ref[...]
        halo = jnp.where(r > 0, up_ref[...], 0.0)
        acc = b_ref[...] + w_ref[CONV_K - 1:CONV_K, :] * u
        for k in range(1, CONV_K):
            acc = acc + w_ref[CONV_K - 1 - k:CONV_K - k, :] * _shift_down(u, halo, k)
        pre_ref[...] = acc
        act_ref[...] = acc * _sigmoid(acc)

    return pl.pallas_call(
        body, name="conv_fwd", grid=(CONV_CH // CONV_COLS, nr),
        in_specs=[cur, prev, pl.BlockSpec((CONV_K, CONV_COLS), lambda j, r: (0, j)),
                  pl.BlockSpec((1, CONV_COLS), lambda j, r: (0, j))],
        out_specs=[out, out],
        out_shape=[jax.ShapeDtypeStruct((s, CONV_CH), F32), jax.ShapeDtypeStruct((s, CONV_CH), F32)],
        compiler_params=_cparams(("parallel", "arbitrary")))(proj, proj, conv_w, conv_b)


def _conv_bwd(dact, pre, proj, conv_w):
    s = proj.shape[0]
    nr = s // CONV_BWD_ROWS
    cb0 = OFF_XBC // CONV_COLS
    hb = CONV_BWD_ROWS // HALO
    last_halo = s // HALO - 1
    n_sub = CONV_BWD_ROWS // CONV_SUB_ROWS
    cur = pl.BlockSpec((CONV_BWD_ROWS, CONV_COLS), lambda j, r: (r, j))
    nxt = pl.BlockSpec((HALO, CONV_COLS), lambda j, r: (jnp.minimum((r + 1) * hb, last_halo), j))
    ucur = pl.BlockSpec((CONV_BWD_ROWS, CONV_COLS), lambda j, r: (r, cb0 + j))
    wspec = pl.BlockSpec((CONV_K, CONV_COLS), lambda j, r: (0, j))
    bspec = pl.BlockSpec((1, CONV_COLS), lambda j, r: (0, j))

    def dsilu(p):
        sg = _sigmoid(p)
        return sg * (1.0 + p * (1.0 - sg))

    def body(da_ref, dan_ref, pre_ref, pren_ref, u_ref, w_ref, du_ref, dw_ref, db_ref):
        r = pl.program_id(1)

        @pl.when(r == 0)
        def _():
            dw_ref[...] = jnp.zeros_like(dw_ref)
            db_ref[...] = jnp.zeros_like(db_ref)

        dws = [jnp.zeros((1, CONV_COLS), F32) for _ in range(CONV_K)]
        db = jnp.zeros((1, CONV_COLS), F32)
        for c in range(n_sub):
            rows = slice(c * CONV_SUB_ROWS, (c + 1) * CONV_SUB_ROWS)
            ahead = slice((c + 1) * CONV_SUB_ROWS, (c + 1) * CONV_SUB_ROWS + HALO)
            dpre = da_ref[rows, :] * dsilu(pre_ref[rows, :])
            if c < n_sub - 1:
                dnext = da_ref[ahead, :] * dsilu(pre_ref[ahead, :])
            else:
                dnext = jnp.where(r < nr - 1, dan_ref[...] * dsilu(pren_ref[...]), 0.0)
            u = u_ref[rows, :]
            du = w_ref[CONV_K - 1:CONV_K, :] * dpre
            dws[0] = dws[0] + jnp.sum(dpre * u, axis=0, keepdims=True)
            for k in range(1, CONV_K):
                ahead_k = _shift_up(dpre, dnext, k)
                du = du + w_ref[CONV_K - 1 - k:CONV_K - k, :] * ahead_k
                dws[k] = dws[k] + jnp.sum(ahead_k * u, axis=0, keepdims=True)
            du_ref[rows, :] = du.astype(BF16)
            db = db + jnp.sum(dpre, axis=0, keepdims=True)
        dw_ref[...] += jnp.concatenate(dws[::-1], axis=0)
        db_ref[...] += db

    return pl.pallas_call(
        body, name="conv_bwd", grid=(CONV_CH // CONV_COLS, nr),
        in_specs=[cur, nxt, cur, nxt, ucur, wspec],
        out_specs=[cur, wspec, bspec],
        out_shape=[jax.ShapeDtypeStruct((s, CONV_CH), BF16), jax.ShapeDtypeStruct((CONV_K, CONV_CH), F32),
                   jax.ShapeDtypeStruct((1, CONV_CH), F32)],
        compiler_params=_cparams(("parallel", "arbitrary")))(dact, dact, pre, pre, proj, conv_w)


def _ssd_common(dtr, dtb, alog):
    lane = lax.broadcasted_iota(jnp.int32, (1, LANE), 1)
    head_lane = lane < SSD_HEADS
    dt = jnp.where(head_lane, _softplus(dtr + dtb), 0.0)
    a = jnp.where(head_lane, -jnp.exp(alog), 0.0)
    row = lax.broadcasted_iota(jnp.int32, (SSD_CHUNK, SSD_CHUNK), 0)
    col = lax.broadcasted_iota(jnp.int32, (SSD_CHUNK, SSD_CHUNK), 1)
    tril = row >= col
    cs = _dot(tril.astype(F32), dt * a, NN, precision=HIGHEST)
    return dt, a, cs, cs.T, tril, lane


def _split_bf16(v, passes):
    terms, rest = [], v
    for _ in range(passes):
        t = rest.astype(BF16)
        terms.append(t)
        rest = rest - t.astype(F32)
    return terms


def _dot_split(v, m, dims, passes):
    terms = _split_bf16(v, passes)
    if passes == 1:
        return _dot(terms[0], m, dims)
    return _dot(jnp.concatenate(terms, axis=1), jnp.concatenate([m] * passes, axis=0 if dims == NN else 1), dims)


def _ssd_constants():
    heads = jnp.arange(LANE)[:, None]
    exp_mat = (heads == (jnp.arange(SSD_D_INNER)[None, :] // HEAD_DIM)).astype(BF16)
    ind4 = ((jnp.arange(SSD_HEADS * SSD_CHUNK)[:, None] // SSD_CHUNK) == jnp.arange(LANE)[None, :]).astype(BF16)
    return exp_mat, ind4


def _expand_heads(v):
    return jnp.repeat(v[:, :SSD_HEADS], HEAD_DIM, axis=1)


def _ssd_prep(dtr, dtb, alog, exp_mat):
    dt, a, cs, cst, tril, lane = _ssd_common(dtr, dtb, alog)
    return dt, a, cs, cst, tril, lane, _dot_split(dt, exp_mat, NN, 2), _dot_split(cs, exp_mat, NN, 3)


def _chunk_decay_rows(cs, g):
    parts = []
    for e in range(HEADS_PER_GROUP):
        h = g * HEADS_PER_GROUP + e
        parts.append(jnp.broadcast_to(jnp.exp(cs[SSD_CHUNK - 1:SSD_CHUNK, h:h + 1]), (HEAD_DIM, SSD_STATE)))
    return jnp.concatenate(parts, axis=0)


def _ssd_fwd(proj, act, dtb, alog, dsk, nw):
    s = proj.shape[0]
    nc = s // SSD_CHUNK
    bc_w = SSD_GROUPS * SSD_STATE
    exp_mat, _ = _ssd_constants()

    def body(z_ref, dtr_ref, xs_ref, b_ref, c_ref, dtb_ref, alog_ref, dskx_ref, nw_ref, exp_ref,
             ypre_ref, yssd_ref, hall_ref, h_scr):
        @pl.when(pl.program_id(0) == 0)
        def _():
            h_scr[...] = jnp.zeros_like(h_scr)

        dt, a, cs, cst, tril, lane, dtx, csx = _ssd_prep(dtr_ref[...], dtb_ref[...], alog_ref[...], exp_ref[...])
        cs_last_x = csx[SSD_CHUNK - 1:SSD_CHUNK, :]
        xs = xs_ref[...]
        xdt = xs * dtx
        xdtb = xdt.astype(BF16)
        xdec = (xdt * jnp.exp(cs_last_x - csx)).astype(BF16)
        ecsx = jnp.exp(csx)
        head_of_lane = lax.broadcasted_iota(jnp.int32, (1, GROUP_WIDTH), 1) // HEAD_DIM
        for g in range(SSD_GROUPS):
            gs = slice(g * GROUP_WIDTH, (g + 1) * GROUP_WIDTH)
            bg = b_ref[:, g * SSD_STATE:(g + 1) * SSD_STATE].astype(BF16)
            cg = c_ref[:, g * SSD_STATE:(g + 1) * SSD_STATE].astype(BF16)
            cb = _dot(cg, bg, NT)
            hprev = h_scr[gs, :]
            hall_ref[0, gs, :] = hprev
            gms, rhs = [], []
            xg = xdtb[:, gs]
            for e in range(HEADS_PER_GROUP):
                h = g * HEADS_PER_GROUP + e
                lm = jnp.exp(jnp.where(tril, cs[:, h:h + 1] - cst[h:h + 1, :], -1e30))
                gms.append((cb * lm).astype(BF16))
                rhs.append(jnp.where(head_of_lane == e, xg, jnp.zeros_like(xg)))
            y = _dot(jnp.concatenate(gms, axis=1), jnp.concatenate(rhs, axis=0), NN)
            y = y + ecsx[:, gs] * _dot(cg, hprev.astype(BF16), NT)
            y = y + dskx_ref[:, gs] * xs[:, gs]
            h_scr[gs, :] = hprev * _chunk_decay_rows(cs, g) + _dot(xdec[:, gs], bg, TN)
            ypre_ref[:, gs] = y
            z = z_ref[:, gs]
            yg = y * (z * _sigmoid(z))
            r = lax.rsqrt(jnp.mean(yg * yg, axis=-1, keepdims=True) + EPS)
            yssd_ref[:, gs] = (yg * r * nw_ref[:, gs]).astype(BF16)

    row_d = lambda cb: pl.BlockSpec((SSD_CHUNK, SSD_D_INNER), lambda c: (c, cb))
    small = pl.BlockSpec((1, LANE), lambda c: (0, 0))
    wide = pl.BlockSpec((1, SSD_D_INNER), lambda c: (0, 0))
    return pl.pallas_call(
        body, name="ssd_fwd", grid=(nc,),
        in_specs=[row_d(OFF_Z // SSD_D_INNER),
                  pl.BlockSpec((SSD_CHUNK, LANE), lambda c: (c, OFF_DT // LANE)),
                  row_d(0),
                  pl.BlockSpec((SSD_CHUNK, bc_w), lambda c: (c, SSD_D_INNER // bc_w)),
                  pl.BlockSpec((SSD_CHUNK, bc_w), lambda c: (c, SSD_D_INNER // bc_w + 1)),
                  small, small, wide, wide, pl.BlockSpec((LANE, SSD_D_INNER), lambda c: (0, 0))],
        out_specs=[row_d(0), row_d(0), pl.BlockSpec((1, SSD_D_INNER, SSD_STATE), lambda c: (c, 0, 0))],
        out_shape=[jax.ShapeDtypeStruct((s, SSD_D_INNER), F32), jax.ShapeDtypeStruct((s, SSD_D_INNER + ATT_D), BF16),
                   jax.ShapeDtypeStruct((nc, SSD_D_INNER, SSD_STATE), F32)],
        scratch_shapes=[pltpu.VMEM((SSD_D_INNER, SSD_STATE), F32)],
        compiler_params=_cparams(("arbitrary",)))(proj, proj, act, act, act, dtb, alog, _expand_heads(dsk), nw, exp_mat)


def _ssd_bwd(dycat, ypre, proj, act, hall, dtb, alog, dsk, nw, comm=None):
    s = proj.shape[0]
    nc = s // SSD_CHUNK
    bc_w = SSD_GROUPS * SSD_STATE

    exp_mat, ind4 = _ssd_constants()
    seg_passes = 1

    def body(dy_ref, ypre_ref, z_ref, dtr_ref, xs_ref, b_ref, c_ref, hall_ref, dtb_ref, alog_ref, dskx_ref, nw_ref,
             exp_ref, ind4_ref, dz_ref, dact_ref, ddtr_ref, da_ref, ddsk_ref, ddtb_ref, dnw_ref, dh_scr):
        @pl.when(pl.program_id(0) == 0)
        def _():
            dh_scr[...] = jnp.zeros_like(dh_scr)
            da_ref[...] = jnp.zeros_like(da_ref)
            ddsk_ref[...] = jnp.zeros_like(ddsk_ref)
            ddtb_ref[...] = jnp.zeros_like(ddtb_ref)
            dnw_ref[...] = jnp.zeros_like(dnw_ref)

        dtr = dtr_ref[...]
        dt, a, cs, cst, tril, lane, dtx, csx = _ssd_prep(dtr, dtb_ref[...], alog_ref[...], exp_ref[...])
        cs_last_x = csx[SSD_CHUNK - 1:SSD_CHUNK, :]
        xs = xs_ref[...]
        xdt = xs * dtx
        xdtb = xdt.astype(BF16)
        decx = jnp.exp(cs_last_x - csx)
        xdecf = xdt * decx
        xdec = xdecf.astype(BF16)
        ecsx = jnp.exp(csx)
        head_of_lane = lax.broadcasted_iota(jnp.int32, (1, GROUP_WIDTH), 1) // HEAD_DIM
        last_row = lax.broadcasted_iota(jnp.int32, (SSD_CHUNK, 1), 0) == SSD_CHUNK - 1
        dcs_col = jnp.zeros((SSD_CHUNK, LANE), F32)
        dcs_row = jnp.zeros((SSD_CHUNK, LANE), F32)
        ddt = jnp.zeros((SSD_CHUNK, LANE), F32)
        ddsk = jnp.zeros((1, LANE), F32)
        hsum = jnp.zeros((1, LANE), F32)
        t1_sum = jnp.zeros((1, LANE), F32)
        for g in range(SSD_GROUPS):
            gs = slice(g * GROUP_WIDTH, (g + 1) * GROUP_WIDTH)
            bsl = slice(g * SSD_STATE, (g + 1) * SSD_STATE)
            exp_g = exp_ref[:, gs]
            ind4_g = ind4_ref[g * HEADS_PER_GROUP * SSD_CHUNK:(g + 1) * HEADS_PER_GROUP * SSD_CHUNK, :]
            z = z_ref[:, gs]
            sg = _sigmoid(z)
            sz = z * sg
            ypre = ypre_ref[:, gs]
            yg = ypre * sz
            r = lax.rsqrt(jnp.mean(yg * yg, axis=-1, keepdims=True) + EPS)
            nrm = yg * r
            dyo_n = dy_ref[:, gs]
            dnw_ref[:, gs] += jnp.sum(dyo_n * nrm, axis=0, keepdims=True)
            dn = dyo_n * nw_ref[:, gs]
            dyg = r * (dn - nrm * jnp.mean(dn * nrm, axis=-1, keepdims=True))
            dz_ref[:, gs] = (dyg * ypre * (sg * (1.0 + z * (1.0 - sg)))).astype(BF16)
            dy = dyg * sz

            bg = b_ref[:, bsl].astype(BF16)
            cg = c_ref[:, bsl].astype(BF16)
            cb = _dot(cg, bg, NT)
            hprev = hall_ref[0, gs, :]
            hb = hprev.astype(BF16)
            dhn = dh_scr[gs, :]
            dhb = dhn.astype(BF16)
            xs_g, xdt_g = xs[:, gs], xdtb[:, gs]
            w_off = _dot(cg, hb, NT)
            dyo = dy * ecsx[:, gs]
            dyob = dyo.astype(BF16)
            dcg = _dot(dyob, hb, NN)
            dh_y = _dot(dyob, cg, TN)
            r_st = _dot(bg, dhb, NT)
            dbg = _dot(xdec[:, gs], dhb, NN)
            dyb = dy.astype(BF16)
            gms, gmbs, lms, dys = [], [], [], []
            for e in range(HEADS_PER_GROUP):
                h = g * HEADS_PER_GROUP + e
                lm = jnp.exp(jnp.where(tril, cs[:, h:h + 1] - cst[h:h + 1, :], -1e30))
                gm = cb * lm
                lms.append(lm)
                gms.append(gm)
                gmbs.append(gm.astype(BF16))
                dys.append(jnp.where(head_of_lane == e, dyb, jnp.zeros_like(dyb)))
            dxdt = _dot(jnp.concatenate(gmbs, axis=0), jnp.concatenate(dys, axis=0), TN) + decx[:, gs] * r_st
            dcb = jnp.zeros((SSD_CHUNK, SSD_CHUNK), F32)
            mms = []
            for e in range(HEADS_PER_GROUP):
                dg = _dot(dys[e], xdt_g, NT)
                mms.append(dg * gms[e])
                dcb = dcb + dg * lms[e]
            seg = _dot_split(jnp.concatenate([dyo * w_off, xdecf[:, gs] * r_st, dxdt * xs_g, dy * xs_g], axis=0), exp_g, NT, seg_passes)
            v1, t1, ddt_g, dsk_g = [seg[i * SSD_CHUNK:(i + 1) * SSD_CHUNK] for i in range(4)]
            dcs_col = dcs_col + v1 - t1 + _dot_split(jnp.concatenate(mms, axis=1), ind4_g, NN, seg_passes)
            for t in _split_bf16(jnp.concatenate(mms, axis=0), seg_passes):
                dcs_row = dcs_row + _dot(ind4_g, t, TN)
            ddt = ddt + ddt_g
            ddsk = ddsk + jnp.sum(dsk_g, axis=0, keepdims=True)
            t1_sum = t1_sum + jnp.sum(t1, axis=0, keepdims=True)
            for e in range(HEADS_PER_GROUP):
                h = g * HEADS_PER_GROUP + e
                hs = slice(e * HEAD_DIM, (e + 1) * HEAD_DIM)
                hsum = hsum + jnp.where(lane == h, jnp.sum(dhn[hs, :] * hprev[hs, :]).reshape(1, 1), 0.0)
            dh_scr[gs, :] = dhn * _chunk_decay_rows(cs, g) + dh_y
            dcbb = dcb.astype(BF16)
            dact_ref[:, gs] = dxdt * dtx[:, gs] + dskx_ref[:, gs] * dy
            dact_ref[:, SSD_D_INNER + g * SSD_STATE:SSD_D_INNER + (g + 1) * SSD_STATE] = dbg + _dot(dcbb, cg, TN)
            dact_ref[:, SSD_D_INNER + bc_w + g * SSD_STATE:SSD_D_INNER + bc_w + (g + 1) * SSD_STATE] = dcg + _dot(dcbb, bg, NN)
        dlast = t1_sum + jnp.exp(cs[SSD_CHUNK - 1:SSD_CHUNK, :]) * hsum
        dcs = dcs_col - dcs_row.T + jnp.where(last_row, dlast, 0.0)
        row = lax.broadcasted_iota(jnp.int32, (SSD_CHUNK, SSD_CHUNK), 0)
        col = lax.broadcasted_iota(jnp.int32, (SSD_CHUNK, SSD_CHUNK), 1)
        dda = _dot((col >= row).astype(F32), dcs, NN, precision=HIGHEST)
        ddt = ddt + dda * a
        da_ref[...] += jnp.sum(dda * dt, axis=0, keepdims=True)
        ddtr = jnp.where(lane < SSD_HEADS, ddt * _sigmoid(dtr + dtb_ref[...]), 0.0)
        ddtr_ref[...] = ddtr.astype(BF16)
        ddtb_ref[...] += jnp.sum(ddtr, axis=0, keepdims=True)
        ddsk_ref[...] += ddsk

    rev = lambda c: nc - 1 - c
    row_d = lambda cb: pl.BlockSpec((SSD_CHUNK, SSD_D_INNER), lambda c: (rev(c), cb))
    small = pl.BlockSpec((1, LANE), lambda c: (0, 0))
    wide = pl.BlockSpec((1, SSD_D_INNER), lambda c: (0, 0))
    small_shape = jax.ShapeDtypeStruct((1, LANE), F32)
    return _pcall(
        body, (dycat, ypre, proj, proj, act, act, act, hall, dtb, alog, _expand_heads(dsk), nw, exp_mat, ind4),
        name="ssd_bwd", grid=(nc,),
        in_specs=[row_d(0), row_d(0), row_d(OFF_Z // SSD_D_INNER),
                  pl.BlockSpec((SSD_CHUNK, LANE), lambda c: (rev(c), OFF_DT // LANE)),
                  row_d(0),
                  pl.BlockSpec((SSD_CHUNK, bc_w), lambda c: (rev(c), SSD_D_INNER // bc_w)),
                  pl.BlockSpec((SSD_CHUNK, bc_w), lambda c: (rev(c), SSD_D_INNER // bc_w + 1)),
                  pl.BlockSpec((1, SSD_D_INNER, SSD_STATE), lambda c: (rev(c), 0, 0)),
                  small, small, wide, wide, pl.BlockSpec((LANE, SSD_D_INNER), lambda c: (0, 0)),
                  pl.BlockSpec((SSD_HEADS * SSD_CHUNK, LANE), lambda c: (0, 0))],
        out_specs=[row_d(0), pl.BlockSpec((SSD_CHUNK, CONV_CH), lambda c: (rev(c), 0)),
                   pl.BlockSpec((SSD_CHUNK, LANE), lambda c: (rev(c), 0)), small, small, small, wide],
        out_shape=[jax.ShapeDtypeStruct((s, SSD_D_INNER), BF16), jax.ShapeDtypeStruct((s, CONV_CH), F32),
                   jax.ShapeDtypeStruct((s, LANE), BF16), small_shape, small_shape, small_shape,
                   jax.ShapeDtypeStruct((1, SSD_D_INNER), F32)],
        scratch_shapes=[pltpu.VMEM((SSD_D_INNER, SSD_STATE), F32)], sem=("arbitrary",), comm=comm)


def _head_mean_matrix():
    row = lax.broadcasted_iota(jnp.int32, (LANE, LANE), 0) // HEAD_DIM
    col = lax.broadcasted_iota(jnp.int32, (LANE, LANE), 1) // HEAD_DIM
    return (row == col).astype(F32)


def _head_sum2(v, ones_bd):
    hi = v.astype(BF16)
    lo = (v - hi.astype(F32)).astype(BF16)
    return _dot(jnp.concatenate([hi, lo], axis=1), jnp.concatenate([ones_bd, ones_bd], axis=0), NN)


def _head_norms(xs, ws, ones_bd):
    sums = [_head_sum2(x * x, ones_bd) for x in xs]
    return [(x * lax.rsqrt(ms * (1.0 / HEAD_DIM) + EPS)) * w for x, ms, w in zip(xs, sums, ws)]


def _head_norms_bwd(dns, xs, ws, ones_bd):
    sums = [_head_sum2(x * x, ones_bd) for x in xs]
    rs = [lax.rsqrt(ms * (1.0 / HEAD_DIM) + EPS) for ms in sums]
    nrms = [x * r for x, r in zip(xs, rs)]
    dnws = [dn * w for dn, w in zip(dns, ws)]
    projs = [_head_sum2(dnw * nrm, ones_bd) for dnw, nrm in zip(dnws, nrms)]
    dxs = [r * (dnw - nrm * (pr * (1.0 / HEAD_DIM))) for r, dnw, nrm, pr in zip(rs, dnws, nrms, projs)]
    return dxs, [jnp.sum(dn * nrm, axis=0, keepdims=True) for dn, nrm in zip(dns, nrms)]


NORM_CHUNKS = 2


PRO_ROWS = 256
ATT_GROUP_FWD = 16
ATT_GROUP_BWD = 8
KEYS = 2 * ATT_BLK
NEG = -1e30
HALF = HEAD_DIM // 2


def _rows(start, size, dil):
    return pl.ds(start, size) if dil == 1 else pl.ds(start, size, stride=dil)


def _fill_bias(bias_ref):
    row = lax.broadcasted_iota(jnp.int32, (ATT_BLK, 2 * KEYS), 0)
    col = lax.broadcasted_iota(jnp.int32, (ATT_BLK, 2 * KEYS), 1) & (KEYS - 1)
    for first, off in ((0, 0), (1, ATT_BLK)):
        dist = off + row - col
        bias_ref[first] = jnp.where((dist >= 0) & (dist <= ATT_BLK), 0.0, NEG)


def _pair(a, b):
    return jnp.concatenate([jnp.broadcast_to(a, (ATT_BLK, KEYS)), jnp.broadcast_to(b, (ATT_BLK, KEYS))], axis=1)


def _split_heads(x, is_a):
    zero = jnp.zeros_like(x)
    return jnp.concatenate([jnp.where(is_a, x, zero), jnp.where(is_a, zero, x)], axis=0)


def _block_ids(b, nb):
    i = b & (nb - 1)
    q0 = pl.multiple_of(b * ATT_BLK, ATT_BLK)
    k0 = pl.multiple_of((b - jnp.minimum(i, 1)) * ATT_BLK, ATT_BLK)
    return pl.ds(q0, ATT_BLK), pl.ds(k0, KEYS), jnp.minimum(i, 1)


def _att_fwd(proj, qw, kw, comm=None):
    s = proj.shape[0]
    nblk = s // ATT_BLK
    assert all((s // d) // ATT_BLK >= 2 for d in DILATIONS)
    blk = lambda off: pl.BlockSpec((s, LANE), lambda i: (0, off // LANE + i))
    wspec = pl.BlockSpec((1, LANE), lambda i: (0, i))
    oblk = pl.BlockSpec((s, LANE), lambda i: (0, i))

    def body(q_ref, k_ref, v_ref, qw_ref, kw_ref, o_ref, lse_ref, qn, kn, q_cm, k_cm, v_cm, m_acc, l_acc, o_d, m_d, l_d, bias):
        ones_bd = _head_mean_matrix().astype(BF16)
        is_a = lax.broadcasted_iota(jnp.int32, (1, LANE), 1) < HEAD_DIM
        ones_ext = _split_heads(jnp.ones((KEYS, LANE), BF16), is_a)
        _fill_bias(bias)

        def pro(j, c):
            chunks = [pl.ds(pl.multiple_of((NORM_CHUNKS * j + u) * PRO_ROWS, PRO_ROWS), PRO_ROWS) for u in range(NORM_CHUNKS)]
            normed = _head_norms([q_ref[rows, :] for rows in chunks] + [k_ref[rows, :] for rows in chunks],
                                 [qw_ref[...] * HEAD_DIM ** -0.5] * NORM_CHUNKS + [kw_ref[...]] * NORM_CHUNKS, ones_bd)
            for u, rows in enumerate(chunks):
                qn[rows, :] = normed[u]
                kn[rows, :] = normed[NORM_CHUNKS + u]
            return c

        lax.fori_loop(0, s // (NORM_CHUNKS * PRO_ROWS), pro, 0)

        for dil in DILATIONS:
            ln = s // dil
            nb = ln // ATT_BLK
            o_out, m_out, l_out = (o_ref, m_acc, l_acc) if dil == 1 else (o_d, m_d, l_d)
            for r in range(dil):
                def relayout(j, c, dil=dil, r=r, ln=ln):
                    j0 = pl.multiple_of(j * PRO_ROWS, PRO_ROWS)
                    src = _rows(r + dil * j0, PRO_ROWS, dil)
                    dst = pl.ds(r * ln + j0, PRO_ROWS)
                    q_cm[dst, :] = qn[src, :].astype(BF16)
                    k_cm[dst, :] = kn[src, :].astype(BF16)
                    v_cm[dst, :] = v_ref[src, :].astype(BF16)
                    return c

                lax.fori_loop(0, ln // PRO_ROWS, relayout, 0)

            def step(bg, c, nb=nb, o_out=o_out, m_out=m_out, l_out=l_out):
                ids = [_block_ids(bg * ATT_GROUP_FWD + u, nb) for u in range(ATT_GROUP_FWD)]
                kbs = [_split_heads(k_cm[krows, :], is_a) for _, krows, _ in ids]
                scs = [_dot(q_cm[qrows, :], kb, NT) + bias[first] for (qrows, _, first), kb in zip(ids, kbs)]
                mas = [jnp.max(sc[:, :KEYS], axis=-1, keepdims=True) for sc in scs]
                mbs = [jnp.max(sc[:, KEYS:], axis=-1, keepdims=True) for sc in scs]
                ps = [jnp.exp(sc - _pair(ma, mb)).astype(BF16) for sc, ma, mb in zip(scs, mas, mbs)]
                vbs = [jnp.concatenate([_split_heads(v_cm[krows, :], is_a), ones_ext], axis=1) for _, krows, _ in ids]
                ols = [_dot(p, vb, NN) for p, vb in zip(ps, vbs)]
                for (qrows, _, _), ol, ma, mb in zip(ids, ols, mas, mbs):
                    o_out[qrows, :] = ol[:, :LANE]
                    l_out[qrows, :] = ol[:, LANE:]
                    m_out[qrows, :] = jnp.where(is_a, ma, mb)
                return c

            lax.fori_loop(0, nblk // ATT_GROUP_FWD, step, 0)

            if dil > 1:
                for r in range(dil):
                    def merge(j, c, dil=dil, r=r, ln=ln):
                        j0 = pl.multiple_of(j * PRO_ROWS, PRO_ROWS)
                        nat = _rows(r + dil * j0, PRO_ROWS, dil)
                        cm = pl.ds(r * ln + j0, PRO_ROWS)
                        m_old, m_new = m_acc[nat, :], m_d[cm, :]
                        m = jnp.maximum(m_old, m_new)
                        a_old, a_new = jnp.exp(m_old - m), jnp.exp(m_new - m)
                        o_ref[nat, :] = a_old * o_ref[nat, :] + a_new * o_d[cm, :]
                        l_acc[nat, :] = a_old * l_acc[nat, :] + a_new * l_d[cm, :]
                        m_acc[nat, :] = m
                        return c

                    lax.fori_loop(0, ln // PRO_ROWS, merge, 0)

        def epi(j, c):
            rows = pl.ds(pl.multiple_of(j * PRO_ROWS, PRO_ROWS), PRO_ROWS)
            l = l_acc[rows, :]
            o_ref[rows, :] = o_ref[rows, :] / l
            lse_ref[rows, :] = m_acc[rows, :] + jnp.log(l)
            return c

        lax.fori_loop(0, s // PRO_ROWS, epi, 0)

    f = jax.ShapeDtypeStruct((s, ATT_D), F32)
    scr = pltpu.VMEM((s, LANE), F32)
    scb = pltpu.VMEM((s, LANE), BF16)
    return _pcall(
        body, (proj, proj, proj, qw, kw), name="att_fwd", grid=(ATT_D // LANE,),
        in_specs=[blk(OFF_Q), blk(OFF_K), blk(OFF_V), wspec, wspec], out_specs=[oblk, oblk], out_shape=[f, f],
        scratch_shapes=[scr, scr, scb, scb, scb, scr, scr, scr, scr, scr, pltpu.VMEM((2, ATT_BLK, 2 * KEYS), F32)],
        sem=("parallel",), comm=comm)


def _att_bwd(proj, do, stats, qw, kw, comm=None):
    s = proj.shape[0]
    nblk = s // ATT_BLK
    blk = lambda off: pl.BlockSpec((s, LANE), lambda i: (0, off // LANE + i))
    wspec = pl.BlockSpec((1, LANE), lambda i: (0, i))
    oblk = pl.BlockSpec((s, LANE), lambda i: (0, i))

    def body(q_ref, k_ref, v_ref, do_ref, st_ref, qw_ref, kw_ref, dq_ref, dk_ref, dv_ref, dqw_ref, dkw_ref,
             qn, kn, q_cm, do_cm, k_cm, v_cm, st_cm, dq_acc, dk_acc, dv_acc, dq_d, dk_d, dv_d, bias):
        ones_bd = _head_mean_matrix().astype(BF16)
        is_a = lax.broadcasted_iota(jnp.int32, (1, LANE), 1) < HEAD_DIM
        _fill_bias(bias)
        zero = jnp.zeros((PRO_ROWS, LANE), F32)

        def pro(j, c):
            chunks = [pl.ds(pl.multiple_of((NORM_CHUNKS * j + u) * PRO_ROWS, PRO_ROWS), PRO_ROWS) for u in range(NORM_CHUNKS)]
            normed = _head_norms([q_ref[rows, :] for rows in chunks] + [k_ref[rows, :] for rows in chunks],
                                 [qw_ref[...] * HEAD_DIM ** -0.5] * NORM_CHUNKS + [kw_ref[...]] * NORM_CHUNKS, ones_bd)
            for u, rows in enumerate(chunks):
                qn[rows, :] = normed[u]
                kn[rows, :] = normed[NORM_CHUNKS + u]
                dk_acc[rows, :] = zero
                dv_acc[rows, :] = zero
            return c

        lax.fori_loop(0, s // (NORM_CHUNKS * PRO_ROWS), pro, 0)

        for dil in DILATIONS:
            ln = s // dil
            nb = ln // ATT_BLK
            dq_o, dk_o, dv_o = (dq_acc, dk_acc, dv_acc) if dil == 1 else (dq_d, dk_d, dv_d)
            for r in range(dil):
                def relayout(j, c, dil=dil, r=r, ln=ln):
                    j0 = pl.multiple_of(j * PRO_ROWS, PRO_ROWS)
                    src = _rows(r + dil * j0, PRO_ROWS, dil)
                    dst = pl.ds(r * ln + j0, PRO_ROWS)
                    q_cm[dst, :] = qn[src, :].astype(BF16)
                    k_cm[dst, :] = kn[src, :].astype(BF16)
                    v_cm[dst, :] = v_ref[src, :].astype(BF16)
                    do_cm[dst, :] = do_ref[src, :].astype(BF16)
                    st_cm[dst, :] = st_ref[src, :]
                    if dil > 1:
                        dk_d[dst, :] = zero
                        dv_d[dst, :] = zero
                    return c

                lax.fori_loop(0, ln // PRO_ROWS, relayout, 0)

            def step(bg, c, nb=nb, dq_o=dq_o, dk_o=dk_o, dv_o=dv_o):
                ids = [_block_ids(bg * ATT_GROUP_BWD + u, nb) for u in range(ATT_GROUP_BWD)]
                qbs = [q_cm[qrows, :] for qrows, _, _ in ids]
                dobs = [do_cm[qrows, :] for qrows, _, _ in ids]
                kbs = [_split_heads(k_cm[krows, :], is_a) for _, krows, _ in ids]
                vbs = [_split_heads(v_cm[krows, :], is_a) for _, krows, _ in ids]
                sts = [st_cm[qrows, :] for qrows, _, _ in ids]
                scs = [_dot(qb, kb, NT) + bias[first] for qb, kb, (_, _, first) in zip(qbs, kbs, ids)]
                dps = [_dot(dob, vb, NT) for dob, vb in zip(dobs, vbs)]
                ps = [jnp.exp(sc - _pair(st[:, 0:1], st[:, HEAD_DIM:HEAD_DIM + 1])) for sc, st in zip(scs, sts)]
                dss = [(p * (dp - _pair(st[:, HALF:HALF + 1], st[:, HEAD_DIM + HALF:HEAD_DIM + HALF + 1]))).astype(BF16)
                       for p, dp, st in zip(ps, dps, sts)]
                dqs = [_dot(ds, kb, NN) for ds, kb in zip(dss, kbs)]
                dkfs = [_dot(ds, qb, TN) for ds, qb in zip(dss, qbs)]
                dvfs = [_dot(p.astype(BF16), dob, TN) for p, dob in zip(ps, dobs)]
                for (qrows, krows, _), dq, dkf, dvf in zip(ids, dqs, dkfs, dvfs):
                    dq_o[qrows, :] = dq
                    dk_o[krows, :] += jnp.where(is_a, dkf[:KEYS], dkf[KEYS:])
                    dv_o[krows, :] += jnp.where(is_a, dvf[:KEYS], dvf[KEYS:])
                return c

            lax.fori_loop(0, nblk // ATT_GROUP_BWD, step, 0)

            if dil > 1:
                for r in range(dil):
                    def merge(j, c, dil=dil, r=r, ln=ln):
                        j0 = pl.multiple_of(j * PRO_ROWS, PRO_ROWS)
                        nat = _rows(r + dil * j0, PRO_ROWS, dil)
                        cm = pl.ds(r * ln + j0, PRO_ROWS)
                        dq_acc[nat, :] += dq_d[cm, :]
                        dk_acc[nat, :] += dk_d[cm, :]
                        dv_acc[nat, :] += dv_d[cm, :]
                        return c

                    lax.fori_loop(0, ln // PRO_ROWS, merge, 0)

        def epi(j, c):
            chunks = [pl.ds(pl.multiple_of((NORM_CHUNKS * j + u) * PRO_ROWS, PRO_ROWS), PRO_ROWS) for u in range(NORM_CHUNKS)]
            dxs, dws = _head_norms_bwd(
                [dq_acc[rows, :] for rows in chunks] + [dk_acc[rows, :] for rows in chunks],
                [q_ref[rows, :] for rows in chunks] + [k_ref[rows, :] for rows in chunks],
                [qw_ref[...] * HEAD_DIM ** -0.5] * NORM_CHUNKS + [kw_ref[...]] * NORM_CHUNKS, ones_bd)
            dqw, dkw = c
            for u, rows in enumerate(chunks):
                dq_ref[rows, :] = dxs[u].astype(BF16)
                dk_ref[rows, :] = dxs[NORM_CHUNKS + u].astype(BF16)
                dv_ref[rows, :] = dv_acc[rows, :].astype(BF16)
                dqw, dkw = dqw + dws[u], dkw + dws[NORM_CHUNKS + u]
            return dqw, dkw

        zrow = jnp.zeros((1, LANE), F32)
        dqw, dkw = lax.fori_loop(0, s // (NORM_CHUNKS * PRO_ROWS), epi, (zrow, zrow))
        dqw_ref[...] = dqw * HEAD_DIM ** -0.5
        dkw_ref[...] = dkw

    o = jax.ShapeDtypeStruct((s, ATT_D), BF16)
    ov = jax.ShapeDtypeStruct((1, ATT_D), F32)
    scr = pltpu.VMEM((s, LANE), F32)
    scb = pltpu.VMEM((s, LANE), BF16)
    return _pcall(
        body, (proj, proj, proj, do, stats, qw, kw), name="att_bwd", grid=(ATT_D // LANE,),
        in_specs=[blk(OFF_Q), blk(OFF_K), blk(OFF_V), oblk, oblk, wspec, wspec],
        out_specs=[oblk, oblk, oblk, wspec, wspec], out_shape=[o, o, o, ov, ov],
        scratch_shapes=[scr, scr, scb, scb, scb, scb, scr, scr, scr, scr, scr, scr, scr, pltpu.VMEM((2, ATT_BLK, 2 * KEYS), F32)],
        sem=("parallel",), comm=comm)


def _att_norm_fwd(o, nw, ycat):
    s = o.shape[0]
    row = pl.BlockSpec((ROW_TILE, ATT_D), lambda i: (i, 0))
    vec = pl.BlockSpec((1, ATT_D), lambda i: (0, 0))

    def body(o_ref, nw_ref, ycat_ref, y_ref):
        o = o_ref[...]
        r = lax.rsqrt(jnp.mean(o * o, axis=-1, keepdims=True) + EPS)
        y_ref[...] = (o * r * nw_ref[...]).astype(BF16)

    return pl.pallas_call(body, name="att_norm_fwd", grid=(s // ROW_TILE,),
                          in_specs=[row, vec, pl.BlockSpec(memory_space=pl.ANY)],
                          out_specs=pl.BlockSpec((ROW_TILE, ATT_D), lambda i: (i, 1)),
                          out_shape=jax.ShapeDtypeStruct(ycat.shape, BF16), input_output_aliases={2: 0},
                          compiler_params=_cparams(("parallel",)))(o, nw, ycat)


def _mixer_split_epilogue(dycat, first, rows, vecs, outs):
    (o_ref, lse_ref), (nw_ref,), (dyssd_ref, do_ref, st_ref, dnw_ref) = rows, vecs, outs

    @pl.when(first)
    def _():
        dnw_ref[...] = jnp.zeros_like(dnw_ref)

    dyssd_ref[...] = dycat[:, :SSD_D_INNER]
    dy = dycat[:, SSD_D_INNER:]
    o = o_ref[...]
    r = lax.rsqrt(jnp.mean(o * o, axis=-1, keepdims=True) + EPS)
    nrm = o * r
    dnw_ref[...] += jnp.sum(dy * nrm, axis=0, keepdims=True)
    dn = dy * nw_ref[...]
    do = r * (dn - nrm * jnp.mean(dn * nrm, axis=-1, keepdims=True))
    do_ref[...] = do
    ones_bd = _head_mean_matrix().astype(BF16)
    prod = do * o
    delta = jnp.concatenate([_head_sum2(prod[:, j * LANE:(j + 1) * LANE], ones_bd) for j in range(ATT_D // LANE)], axis=1)
    lane = lax.broadcasted_iota(jnp.int32, (1, ATT_D), 1)
    st_ref[...] = jnp.where((lane & (HEAD_DIM - 1)) < HALF, lse_ref[...], delta)


def _ada_fwd(c_all, w_ada):
    def body(c_ref, w_ref, o_ref):
        cv = c_ref[...]
        o_ref[...] = _dot((cv * _sigmoid(cv)).astype(BF16), w_ref[...].astype(BF16), NN)

    return pl.pallas_call(body, name="ada_fwd", out_shape=jax.ShapeDtypeStruct((c_all.shape[0], w_ada.shape[1]), F32),
                          compiler_params=_cparams())(c_all, w_ada)


def _adamw_math(g, w, m, v):
    m_new = ADAM_B1 * m + (1.0 - ADAM_B1) * g
    v_new = ADAM_B2 * v + (1.0 - ADAM_B2) * (g * g)
    m_hat = m_new / (1.0 - ADAM_B1 ** ADAM_STEP)
    v_hat = v_new / (1.0 - ADAM_B2 ** ADAM_STEP)
    delta = -ADAM_LR * (m_hat / (jnp.sqrt(v_hat) + ADAM_EPS) + ADAM_WD * w)
    return delta, m_new, v_new


def _ada_bwd_adamw(c_all, dmod_cols, w, m, v):
    rows, cols = w.shape
    tr = 256
    blk = pl.BlockSpec((tr, cols), lambda i: (i, 0))

    def body(c_ref, d_ref, w_ref, m_ref, v_ref, g_ref, dl_ref, mo_ref, vo_ref):
        cv = c_ref[...]
        ca = cv * _sigmoid(cv)
        g = ca[:, 0:1] * d_ref[0:1, :]
        for b in range(1, N_DEV):
            g = g + ca[:, b:b + 1] * d_ref[b:b + 1, :]
        g_ref[...] = g
        dl_ref[...], mo_ref[...], vo_ref[...] = _adamw_math(g, w_ref[...], m_ref[...], v_ref[...])

    o = jax.ShapeDtypeStruct((rows, cols), F32)
    return pl.pallas_call(
        body, name="ada_bwd_adamw", grid=(rows // tr,),
        in_specs=[pl.BlockSpec((tr, N_DEV), lambda i: (i, 0)), pl.BlockSpec((N_DEV, cols), lambda i: (0, 0)), blk, blk, blk],
        out_specs=[blk] * 4, out_shape=[o, o, o, o], compiler_params=_cparams(("parallel",)))(c_all.T, dmod_cols, w, m, v)


def _reduce_adamw(slabs, w, m, v, name):
    rows, cols = w.shape
    n_src = slabs.shape[0]
    if rows % 128 == 0:
        tr, steps = 128, rows // 128
        blk = pl.BlockSpec((tr, cols), lambda i: (i, 0))
        sblk = pl.BlockSpec((n_src, tr, cols), lambda i: (0, i, 0))
    else:
        tc, steps = 256, cols // 256
        blk = pl.BlockSpec((rows, tc), lambda i: (0, i))
        sblk = pl.BlockSpec((n_src, rows, tc), lambda i: (0, 0, i))

    def body(s_ref, w_ref, m_ref, v_ref, g_ref, dl_ref, mo_ref, vo_ref):
        g = s_ref[0].astype(F32)
        for src in range(1, n_src):
            g = g + s_ref[src].astype(F32)
        g_ref[...] = g
        dl_ref[...], mo_ref[...], vo_ref[...] = _adamw_math(g, w_ref[...], m_ref[...], v_ref[...])

    o = jax.ShapeDtypeStruct((rows, cols), F32)
    return pl.pallas_call(
        body, name=name, grid=(steps,), in_specs=[sblk, blk, blk, blk],
        out_specs=[blk] * 4, out_shape=[o, o, o, o], compiler_params=_cparams(("parallel",)))(slabs, w, m, v)


def _small_reduce_adamw(gathered, w, m, v):
    def body(s_ref, w_ref, m_ref, v_ref, g_ref, dl_ref, mo_ref, vo_ref):
        g = s_ref[0]
        for dev in range(1, N_DEV):
            g = g + s_ref[dev]
        g_ref[...] = g
        dl_ref[...], mo_ref[...], vo_ref[...] = _adamw_math(g, w_ref[...], m_ref[...], v_ref[...])

    o = jax.ShapeDtypeStruct(w.shape, F32)
    return pl.pallas_call(body, name="small_reduce_adamw", out_shape=[o, o, o, o], compiler_params=_cparams())(gathered, w, m, v)


def _adamw_small(g, w, m, v, name):
    def body(g_ref, w_ref, m_ref, v_ref, dl_ref, mo_ref, vo_ref):
        dl_ref[...], mo_ref[...], vo_ref[...] = _adamw_math(g_ref[...], w_ref[...], m_ref[...], v_ref[...])

    o = jax.ShapeDtypeStruct(w.shape, F32)
    return pl.pallas_call(body, name=name, out_shape=[o, o, o], compiler_params=_cparams())(g, w, m, v)


class _Exchange:
    def __init__(self, arrs, scatter):
        self.arrs, self.scatter, self.n = list(arrs), scatter, len(arrs)
        hbm = pl.BlockSpec(memory_space=pltpu.HBM)
        self.in_specs = [hbm] * self.n
        self.out_specs = [hbm] * self.n
        self.out_shape = [jax.ShapeDtypeStruct(a.shape if scatter else (N_DEV,) + a.shape, a.dtype) for a in self.arrs]
        self.scratch = [pltpu.SemaphoreType.DMA((self.n * (N_DEV - 1),)), pltpu.SemaphoreType.DMA((self.n * (N_DEV - 1),)),
                        pltpu.SemaphoreType.DMA((self.n,))]

    def _local(self, ins, outs, sems):
        me = 4 * lax.axis_index("x") + 2 * lax.axis_index("y") + lax.axis_index("c")
        return [pltpu.make_async_copy(ins[a].at[me] if self.scatter else ins[a], outs[a].at[me], sems[2].at[a])
                for a in range(self.n)]

    def _remote(self, ins, outs, sems, arriving):
        send_sems, recv_sems, _ = sems
        x, y, c = lax.axis_index("x"), lax.axis_index("y"), lax.axis_index("c")
        me = 4 * x + 2 * y + c
        remote = []
        for a in range(self.n):
            for k in range(1, N_DEV):
                px = 1 - x if k & 4 else x
                py = 1 - y if k & 2 else y
                pc = 1 - c if k & 1 else c
                peer = 4 * px + 2 * py + pc
                sem = a * (N_DEV - 1) + k - 1
                remote.append(pltpu.make_async_remote_copy(
                    src_ref=ins[a].at[peer] if self.scatter else ins[a], dst_ref=outs[a].at[peer if arriving else me],
                    send_sem=send_sems.at[sem], recv_sem=recv_sems.at[sem], device_id=(px, py, pc), device_id_type=MESH_IDS))
        return remote

    def start(self, ins, outs, sems):
        for cp in self._local(ins, outs, sems) + self._remote(ins, outs, sems, arriving=False):
            cp.start()

    def forward(self, ins, outs, sems):
        pass

    def wait(self, ins, outs, sems):
        for send, arrival in zip(self._remote(ins, outs, sems, arriving=False), self._remote(ins, outs, sems, arriving=True)):
            send.wait_send()
            arrival.wait_recv()
        for cp in self._local(ins, outs, sems):
            cp.wait()


N_CHIP = N_DEV // 2


class _SiblingSwap(_Exchange):
    def __init__(self, arrs):
        super().__init__(arrs, scatter=True)
        self.out_shape = [jax.ShapeDtypeStruct((N_CHIP,) + a.shape[2:], a.dtype) for a in self.arrs]
        self.scratch = [pltpu.SemaphoreType.DMA((self.n,)), pltpu.SemaphoreType.DMA((self.n,)), pltpu.SemaphoreType.DMA((1,))]

    def _copies(self, ins, outs, sems):
        x, y, c = lax.axis_index("x"), lax.axis_index("y"), lax.axis_index("c")
        return [pltpu.make_async_remote_copy(src_ref=ins[a].at[:, 1 - c], dst_ref=outs[a], send_sem=sems[0].at[a], recv_sem=sems[1].at[a],
                                             device_id=(x, y, 1 - c), device_id_type=MESH_IDS) for a in range(self.n)]

    def start(self, ins, outs, sems):
        for cp in self._copies(ins, outs, sems):
            cp.start()

    def wait(self, ins, outs, sems):
        for cp in self._copies(ins, outs, sems):
            cp.wait()


class _ChipScatter(_Exchange):
    def __init__(self, arrs):
        super().__init__(arrs, scatter=True)
        n_pairs = self.n * (N_CHIP - 1)
        self.scratch = [pltpu.SemaphoreType.DMA((n_pairs,)), pltpu.SemaphoreType.DMA((n_pairs,)), pltpu.SemaphoreType.DMA((self.n,))]

    def _local(self, ins, outs, sems):
        chip = 2 * lax.axis_index("x") + lax.axis_index("y")
        return [pltpu.make_async_copy(ins[a].at[chip], outs[a].at[chip], sems[2].at[a]) for a in range(self.n)]

    def _remote(self, ins, outs, sems, arriving):
        send_sems, recv_sems, _ = sems
        x, y, c = lax.axis_index("x"), lax.axis_index("y"), lax.axis_index("c")
        chip = 2 * x + y
        remote = []
        for a in range(self.n):
            for k in range(1, N_CHIP):
                px = 1 - x if k & 2 else x
                py = 1 - y if k & 1 else y
                peer = 2 * px + py
                sem = a * (N_CHIP - 1) + k - 1
                remote.append(pltpu.make_async_remote_copy(
                    src_ref=ins[a].at[peer], dst_ref=outs[a].at[peer if arriving else chip], send_sem=send_sems.at[sem],
                    recv_sem=recv_sems.at[sem], device_id=(px, py, c), device_id_type=MESH_IDS))
        return remote


def _chip_sum(mine, theirs):
    n, rows, cols = mine.shape
    blk = pl.BlockSpec((1, rows, 256), lambda q, j: (q, 0, j))

    def body(a_ref, b_ref, o_ref):
        o_ref[...] = (a_ref[...].astype(F32) + b_ref[...].astype(F32)).astype(BF16)

    return pl.pallas_call(body, name="chip_sum", grid=(n, cols // 256), in_specs=[blk, blk], out_specs=blk,
                          out_shape=jax.ShapeDtypeStruct(mine.shape, BF16),
                          compiler_params=_cparams(("parallel", "parallel")))(mine, theirs)


class _Gather2(_Exchange):
    def __init__(self, arrs):
        super().__init__(arrs, scatter=False)

    def _copies(self, ins, outs, sems):
        send_sems, recv_sems, _ = sems
        x, y, c = lax.axis_index("x"), lax.axis_index("y"), lax.axis_index("c")
        sibling = (x, y, 1 - c)
        chips = [(1 - x, y), (x, 1 - y), (1 - x, 1 - y)]
        first, passed, landed = [], [], []
        for a in range(self.n):
            def copy(k, block, to, src=None, a=a):
                slab = outs[a].at[4 * block[0] + 2 * block[1] + block[2]]
                return pltpu.make_async_remote_copy(
                    src_ref=slab if src is None else src, dst_ref=slab, send_sem=send_sems.at[a * (N_DEV - 1) + k],
                    recv_sem=recv_sems.at[a * (N_DEV - 1) + k], device_id=to, device_id_type=MESH_IDS)

            first.append(copy(0, (x, y, c), sibling, src=ins[a]))
            landed.append(copy(0, sibling, sibling))
            for j, chip in enumerate(chips):
                first.append(copy(1 + j, (x, y, c), (*chip, c), src=ins[a]))
                passed.append((copy(1 + j, (*chip, c), sibling), copy(4 + j, (*chip, c), sibling)))
                landed.append(copy(4 + j, (*chip, 1 - c), sibling))
        return first, passed, landed

    def start(self, ins, outs, sems):
        for cp in self._local(ins, outs, sems) + self._copies(ins, outs, sems)[0]:
            cp.start()

    def forward(self, ins, outs, sems):
        for arrival, onward in self._copies(ins, outs, sems)[1]:
            arrival.wait_recv()
            onward.start()

    def wait(self, ins, outs, sems):
        first, passed, landed = self._copies(ins, outs, sems)
        for arrival in landed:
            arrival.wait_recv()
        for cp in first + [onward for _, onward in passed]:
            cp.wait_send()
        for cp in self._local(ins, outs, sems):
            cp.wait()


def _split_comm_refs(refs, n_in, n_out, n_scr, comm):
    nc = comm.n if comm is not None else 0
    ns = 3 if comm is not None else 0
    pos, groups = 0, []
    for cnt in (n_in, nc, n_out, nc, n_scr, ns):
        groups.append(refs[pos:pos + cnt])
        pos += cnt
    assert pos == len(refs), (pos, len(refs))
    return groups


def _pcall(body, args, *, name, grid, in_specs, out_specs, out_shape, scratch_shapes=(), sem=None, comm=None):
    in_specs, out_specs, out_shape, scratch_shapes = list(in_specs), list(out_specs), list(out_shape), list(scratch_shapes)
    n_in, n_out, n_scr = len(in_specs), len(out_specs), len(scratch_shapes)
    if comm is None:
        kernel_body = body
    else:
        def kernel_body(*refs):
            ins, cins, outs, couts, scr, sems = _split_comm_refs(refs, n_in, n_out, n_scr, comm)
            ids = [pl.program_id(a) for a in range(len(grid))]
            first, last = ids[0] == 0, ids[0] == grid[0] - 1
            for a in range(1, len(grid)):
                first, last = first & (ids[a] == 0), last & (ids[a] == grid[a] - 1)

            middle = ids[0] == (2 * grid[0]) // 3
            for a in range(1, len(grid)):
                middle = middle & (ids[a] == 0)

            @pl.when(first)
            def _():
                comm.start(cins, couts, sems)

            @pl.when(middle)
            def _():
                comm.forward(cins, couts, sems)

            body(*ins, *outs, *scr)

            @pl.when(last)
            def _():
                comm.wait(cins, couts, sems)

        in_specs, out_specs, out_shape = in_specs + comm.in_specs, out_specs + comm.out_specs, out_shape + comm.out_shape
        scratch_shapes, args = scratch_shapes + comm.scratch, list(args) + comm.arrs
        sem = ("arbitrary",) * len(grid)
    res = pl.pallas_call(kernel_body, name=name, grid=grid, in_specs=in_specs, out_specs=out_specs, out_shape=out_shape,
                         scratch_shapes=scratch_shapes, compiler_params=_cparams(sem))(*args)
    return res[:n_out], res[n_out:]


def _exchange(arrs, name, scatter=False, ex=None):
    if ex is None:
        ex = _Exchange(arrs, scatter=True) if scatter else _Gather2(arrs)

    def body(*refs):
        _, ins, _, outs, _, sems = _split_comm_refs(refs, 0, 0, 0, ex)
        ex.start(ins, outs, sems)
        ex.forward(ins, outs, sems)
        ex.wait(ins, outs, sems)

    return pl.pallas_call(body, name=name, in_specs=ex.in_specs, out_specs=ex.out_specs, out_shape=ex.out_shape,
                          scratch_shapes=ex.scratch)(*ex.arrs)


def _pad_lanes(v, width=LANE):
    return jnp.pad(v, ((0, 0), (0, width - v.shape[1])))


def _shards_to_cols(g):
    return jnp.transpose(g, (1, 0, 2)).reshape(g.shape[1], N_DEV * g.shape[2])


def _cols_to_shards(w):
    return w.astype(BF16).reshape(w.shape[0], N_DEV, w.shape[1] // N_DEV).transpose(1, 0, 2)


def _local_step(x, tgt, mod, w_in_pt, conv_w, conv_b, dt_bias, a_log, d_skip, ssd_norm_w, q_norm_w, k_norm_w,
                attn_norm_w, w_out_sh, w_ff1_sh, w_ff2_sh, norm1_w, norm2_w, core):
    shift1, scale1, gate1, shift2, scale2, gate2 = [mod[i:i + 1] for i in range(N_MOD)]
    dtb, alog, dsk = _pad_lanes(dt_bias), _pad_lanes(a_log), _pad_lanes(d_skip)
    qw, kw = jnp.tile(q_norm_w, (1, ATT_HEADS)), jnp.tile(k_norm_w, (1, ATT_HEADS))

    h1 = _norm_mod_fwd(x, norm1_w, scale1, shift1, "norm1_fwd")
    proj = _matmul(h1, w_in_pt, tb=True, tm=2048, tn=896, tk=1024, name="in_proj")
    pre, act = _conv_fwd(proj, conv_w, conv_b)
    ypre, ycat_ssd, hall = _ssd_fwd(proj, act, dtb, alog, dsk, ssd_norm_w)
    (o_att, lse), (w_out_g, w_ff1_g, w_ff2_g) = _att_fwd(proj, qw, kw, comm=_Gather2([w_out_sh, w_ff1_sh, w_ff2_sh]))
    w_out = w_out_g.reshape(2 * D_MODEL, D_MODEL)
    w_ff1 = _shards_to_cols(w_ff1_g)
    w_ff2 = w_ff2_g.reshape(D_FF, D_MODEL)
    ycat = _att_norm_fwd(o_att, attn_norm_w, ycat_ssd)
    row32, row16, vec32 = ("row", F32), ("row", BF16), ("vec", F32)
    mix, x1, h2 = _matmul_rows(ycat, w_out, _residual_norm_epilogue, [x], [gate1, norm2_w, scale2, shift2],
                               [row32, row32, row16], tm=512, name="out_proj")
    u, act_ff = _matmul(h2, w_ff1, tm=1024, tn=2048, tk=1024, name="ff1", mode="relu2")
    loss, dout, dff, dgate2 = _matmul_rows(act_ff, w_ff2, _loss_epilogue, [x1, tgt], [gate2],
                                           [("one", F32), row32, row16, vec32], tm=512, name="ff2")

    du = _matmul(dff, w_ff2, tb=True, tm=512, tn=4096, tk=1024, out_dtype=BF16, name="ff2_dx", mode="drelu2", u=u)
    g_ff2 = _matmul(act_ff, dff, ta=True, tm=512, tn=1024, tk=4096, out_dtype=BF16, name="ff2_dw")
    dx1, dshift2, dscale2, g_norm2, dmix, dgate1 = _matmul_rows(
        du, w_ff1, _norm_bwd_epilogue, [x1, dout, mix], [norm2_w, scale2, gate1],
        [row32, vec32, vec32, vec32, row16, vec32], tb=True, tm=512, name="ff1_dx")
    g_ff1 = _matmul(h2, du, ta=True, tm=1024, tn=D_FF // N_DEV, tk=4096, out_dtype=BF16, name="ff1_dw", shard_out=True)

    dy_ssd, do, stats, g_attn_norm = _matmul_rows(
        dmix, w_out, _mixer_split_epilogue, [o_att, lse], [attn_norm_w],
        [("row", F32, SSD_D_INNER), ("row", F32, ATT_D), ("row", F32, ATT_D), ("vec", F32, ATT_D)], tb=True, tm=512, name="out_proj_dx")
    g_out = _matmul(ycat, dmix, ta=True, tm=512, tn=1024, tk=4096, out_dtype=BF16, name="out_proj_dw")
    ff_slabs = [g_ff1, g_ff2.reshape(N_DEV, D_FF // N_DEV, D_MODEL)]
    (dq, dk, dv, dqw, dkw), (s_ff1, s_ff2) = _att_bwd(proj, do, stats, qw, kw, comm=_Exchange(ff_slabs, scatter=True))
    out_slabs = [g_out.astype(BF16).reshape(N_DEV, 2 * D_MODEL // N_DEV, D_MODEL)]
    (dz, dact, ddtr, da, g_dsk, g_dtb, g_ssd_norm), (s_out,) = _ssd_bwd(
        dy_ssd, ypre, proj, act, hall, dtb, alog, dsk, ssd_norm_w, comm=_Exchange(out_slabs, scatter=True))
    dxbc, g_conv_w, g_conv_b = _conv_bwd(dact, pre, proj, conv_w)
    dproj = [(dz, OFF_Z), (dxbc, OFF_XBC), (ddtr, OFF_DT), (dq, OFF_Q), (dk, OFF_K), (dv, OFF_V)]
    g_head, g_tail = _pieces_t_matmul([[dz, dxbc], [dq, dk, dv]], h1, tm=256, name="in_proj_dw")
    g_dt = _matmul(ddtr, h1, ta=True, tm=LANE, tn=1024, tk=4096, out_dtype=BF16, name="in_proj_dw_dt")[:SSD_HEADS]
    in_slabs = jnp.concatenate([g_head, g_dt, g_tail], axis=0).reshape(N_CHIP, 2, IN_W // N_DEV, D_MODEL)
    (sibling_slabs,) = _exchange(None, "swap_w_in_grads", ex=_SiblingSwap([in_slabs]))
    chip_slabs = _chip_sum(lax.dynamic_index_in_dim(in_slabs, core, axis=1, keepdims=False), sibling_slabs)
    (grad_x, dshift1, dscale1, g_norm1), (s_in,) = _matmul_rows(
        dproj, w_in_pt, _norm_bwd_epilogue, [x, dx1], [norm1_w, scale1], [row32, vec32, vec32, vec32],
        tm=256, name="in_proj_dx", comm=_ChipScatter([chip_slabs]))

    dmod = jnp.concatenate([dshift1, dscale1, dgate1, dshift2, dscale2, dgate2], axis=0)
    g_alog = da[:, :SSD_HEADS] * (-jnp.exp(a_log))
    g_qw = dqw.reshape(ATT_HEADS, HEAD_DIM).sum(axis=0, keepdims=True)
    g_kw = dkw.reshape(ATT_HEADS, HEAD_DIM).sum(axis=0, keepdims=True)
    return dict(loss=loss, grad_x=grad_x, dmod=dmod, norm1_w=g_norm1, norm2_w=g_norm2, w_in=s_in, conv_w=g_conv_w,
                conv_b=g_conv_b, dt_bias=g_dtb[:, :SSD_HEADS], a_log=g_alog, d_skip=g_dsk[:, :SSD_HEADS],
                ssd_norm_w=g_ssd_norm, q_norm_w=g_qw, k_norm_w=g_kw, attn_norm_w=g_attn_norm, w_out=s_out,
                w_ff1=s_ff1, w_ff2=s_ff2)


def _pack_w_in_rows(wt_full):
    cut = OFF_DT + SSD_HEADS
    pad = jnp.zeros((LANE - SSD_HEADS, wt_full.shape[1]), wt_full.dtype)
    return jnp.concatenate([wt_full[:cut], pad, wt_full[cut:]], axis=0)


MISC_FIELDS = (("dt_bias", SSD_HEADS), ("a_log", SSD_HEADS), ("d_skip", SSD_HEADS), ("q_norm_w", HEAD_DIM), ("k_norm_w", HEAD_DIM))
SMALL_LAYOUT = (("b_ada", 6), ("norm1_w", 1), ("norm2_w", 1), ("conv_w", 8), ("conv_b", 2), ("ssd_norm_w", 1),
                ("attn_norm_w", 1), ("misc", 1))


def _pack_small(vals):
    rows = []
    for name, nrow in SMALL_LAYOUT:
        if name == "misc":
            misc = jnp.concatenate([vals[f].reshape(1, n) for f, n in MISC_FIELDS], axis=1)
            rows.append(_pad_lanes(misc, D_MODEL))
        elif name in vals:
            rows.append(vals[name].reshape(nrow, D_MODEL))
        else:
            rows.append(jnp.zeros((nrow, D_MODEL), F32))
    used = sum(n for _, n in SMALL_LAYOUT)
    rows.append(jnp.zeros((SMALL_ROWS - used, D_MODEL), F32))
    return jnp.concatenate(rows, axis=0)


def _unpack_small(packed):
    out, r = {}, 0
    for name, nrow in SMALL_LAYOUT:
        blk = packed[r:r + nrow]
        r += nrow
        if name == "misc":
            c0 = 0
            for f, n in MISC_FIELDS:
                out[f] = blk[:, c0:c0 + n]
                c0 += n
        elif name == "b_ada":
            out[name] = blk.reshape(1, N_MOD * D_MODEL)
        elif name == "conv_w":
            out[name] = blk.reshape(CONV_K, CONV_CH)
        elif name == "conv_b":
            out[name] = blk.reshape(1, CONV_CH)
        else:
            out[name] = blk
    return out


WEIGHT_NAMES = ("norm1_w", "norm2_w", "w_ada", "b_ada", "w_in", "conv_w", "conv_b", "dt_bias", "a_log", "d_skip",
                "ssd_norm_w", "q_norm_w", "k_norm_w", "attn_norm_w", "w_out", "w_ff1", "w_ff2")
SMALL_NAMES = ("norm1_w", "norm2_w", "b_ada", "conv_b", "dt_bias", "a_log", "d_skip", "ssd_norm_w", "q_norm_w",
               "k_norm_w", "attn_norm_w")


def kernel(x, c, norm1_w, norm2_w, w_ada, b_ada, w_in, conv_w, conv_b, dt_bias, a_log, d_skip, ssd_norm_w, q_norm_w, k_norm_w, attn_norm_w, w_out, w_ff1, w_ff2, loss_target, m_norm1_w, m_norm2_w, m_w_ada, m_b_ada, m_w_in, m_conv_w, m_conv_b, m_dt_bias, m_a_log, m_d_skip, m_ssd_norm_w, m_q_norm_w, m_k_norm_w, m_attn_norm_w, m_w_out, m_w_ff1, m_w_ff2, v_norm1_w, v_norm2_w, v_w_ada, v_b_ada, v_w_in, v_conv_w, v_conv_b, v_dt_bias, v_a_log, v_d_skip, v_ssd_norm_w, v_q_norm_w, v_k_norm_w, v_attn_norm_w, v_w_out, v_w_ff1, v_w_ff2):
    args = dict(locals())
    w = {n: args[n] for n in WEIGHT_NAMES}
    m = {n: args["m_" + n] for n in WEIGHT_NAMES}
    v = {n: args["v_" + n] for n in WEIGHT_NAMES}
    me = 4 * lax.axis_index("x") + 2 * lax.axis_index("y") + lax.axis_index("c")

    c_rows = jnp.pad(c, ((0, 7), (0, 0)))
    w_in_t, m_in_t, v_in_t = [jnp.transpose(t["w_in"][0]) for t in (w, m, v)]
    c_g, conv_g, w_in_g = _exchange([c_rows, w["conv_w"][0], w_in_t.astype(BF16)], "gather_w_in", scatter=False)
    c_all = c_g[:, 0, :]
    conv_full = _shards_to_cols(conv_g)
    w_in_pt = _pack_w_in_rows(w_in_g.reshape(IN_W, D_MODEL))

    mod_part = _ada_fwd(c_all, w["w_ada"][0])
    (mod_g,) = _exchange([mod_part], "gather_mod", scatter=False)
    mod_mine = lax.dynamic_index_in_dim(mod_g, me, axis=1, keepdims=False).reshape(1, N_MOD * D_MODEL) + w["b_ada"]
    mod = mod_mine.reshape(N_MOD, D_MODEL)

    res = _local_step(x[0], loss_target[0], mod, w_in_pt, conv_full, w["conv_b"], w["dt_bias"], w["a_log"], w["d_skip"],
                      w["ssd_norm_w"], w["q_norm_w"], w["k_norm_w"], w["attn_norm_w"], w["w_out"][0].astype(BF16),
                      w["w_ff1"][0].astype(BF16), w["w_ff2"][0].astype(BF16), w["norm1_w"], w["norm2_w"], lax.axis_index("c"))

    small_vals = {n: res[n] for n in SMALL_NAMES if n != "b_ada"}
    small_vals["b_ada"] = res["dmod"]
    small_vals["conv_w"] = res["conv_w"]
    (small_g,) = _exchange([_pack_small(small_vals)], "gather_small", scatter=False)

    grads, delta, new_m, new_v = {}, {}, {}, {}
    for name in ("w_out", "w_ff1", "w_ff2"):
        outs = _reduce_adamw(res[name], w[name][0], m[name][0], v[name][0], "adamw_" + name)
        grads[name], delta[name], new_m[name], new_v[name] = [o[None] for o in outs]
    outs = _reduce_adamw(res["w_in"], w_in_t, m_in_t, v_in_t, "adamw_w_in")
    grads["w_in"], delta["w_in"], new_m["w_in"], new_v["w_in"] = [jnp.transpose(o)[None] for o in outs]

    sm = _small_reduce_adamw(small_g, _pack_small({n: w[n] for n in SMALL_NAMES}), _pack_small({n: m[n] for n in SMALL_NAMES}),
                             _pack_small({n: v[n] for n in SMALL_NAMES}))
    sm = [_unpack_small(p) for p in sm]
    for n in SMALL_NAMES:
        grads[n], delta[n], new_m[n], new_v[n] = [p[n] for p in sm]
    shard_w = CONV_CH // N_DEV
    g_conv = lax.dynamic_slice_in_dim(sm[0]["conv_w"], me * shard_w, shard_w, axis=1)
    cw = _adamw_small(g_conv, w["conv_w"][0], m["conv_w"][0], v["conv_w"][0], "adamw_conv_w")
    grads["conv_w"] = g_conv[None]
    delta["conv_w"], new_m["conv_w"], new_v["conv_w"] = [o[None] for o in cw]

    ada_w = w_ada.shape[2]
    dmod_all = small_g[:, :N_MOD, :].reshape(N_DEV, N_MOD * D_MODEL)
    dmod_cols = lax.dynamic_slice_in_dim(dmod_all, me * ada_w, ada_w, axis=1)
    outs = _ada_bwd_adamw(c_all, dmod_cols, w["w_ada"][0], m["w_ada"][0], v["w_ada"][0])
    grads["w_ada"], delta["w_ada"], new_m["w_ada"], new_v["w_ada"] = [o[None] for o in outs]

    loss = lax.psum(res["loss"][0, 0], ("x", "y", "c"))
    return (loss, res["grad_x"][None], *[grads[n] for n in WEIGHT_NAMES], *[delta[n] for n in WEIGHT_NAMES],
            *[new_m[n] for n in WEIGHT_NAMES], *[new_v[n] for n in WEIGHT_NAMES])
```

```python
import functools

import jax
import jax.numpy as jnp
from jax import lax
from jax.experimental import pallas as pl
from jax.experimental.pallas import tpu as pltpu

F32 = jnp.float32
BF16 = jnp.bfloat16
HIGHEST = lax.Precision.HIGHEST
MESH_IDS = pl.DeviceIdType.MESH

N_DEV = 8
D_MODEL = 1024
HEAD_DIM = 64
SSD_HEADS = 16
SSD_GROUPS = 4
HEADS_PER_GROUP = SSD_HEADS // SSD_GROUPS
SSD_STATE = 128
SSD_CHUNK = 128
SSD_D_INNER = SSD_HEADS * HEAD_DIM
GROUP_WIDTH = SSD_D_INNER // SSD_GROUPS
CONV_K = 4
CONV_CH = SSD_D_INNER + 2 * SSD_GROUPS * SSD_STATE
ATT_HEADS = 16
ATT_D = ATT_HEADS * HEAD_DIM
ATT_BLK = 128
DILATIONS = (1, 4, 16)
D_FF = 4 * D_MODEL
N_MOD = 6
EPS = 1e-6
IN_W = SSD_D_INNER + CONV_CH + SSD_HEADS + 3 * ATT_D
LANE = 128
OFF_Z, OFF_XBC, OFF_DT = 0, SSD_D_INNER, SSD_D_INNER + CONV_CH
OFF_Q = OFF_DT + LANE
OFF_K, OFF_V = OFF_Q + ATT_D, OFF_Q + 2 * ATT_D
IN_WP = OFF_V + ATT_D

ADAM_LR, ADAM_B1, ADAM_B2, ADAM_EPS, ADAM_WD, ADAM_STEP = 0.001, 0.9, 0.999, 1e-08, 0.01, 10
VMEM_LIMIT = 56 * 1024 * 1024
ROW_TILE = 512
SMALL_ROWS = 24


def _cparams(sem=None):
    return pltpu.CompilerParams(dimension_semantics=sem, vmem_limit_bytes=VMEM_LIMIT)


def _sigmoid(v):
    return 1.0 / (1.0 + jnp.exp(-v))


def _softplus(v):
    y = jnp.exp(-jnp.abs(v))
    small = y * (1.0 - y * (0.5 - y * (1.0 / 3.0)))
    return jnp.maximum(v, 0.0) + jnp.where(y < 0.01, small, jnp.log(1.0 + y))


def _dot(a, b, dims, precision=None):
    return lax.dot_general(a, b, (dims, ((), ())), preferred_element_type=F32, precision=precision)


NN = ((1,), (0,))
NT = ((1,), (1,))
TN = ((0,), (0,))


def _matmul(a, b, *, ta=False, tb=False, tm, tn, tk, out_dtype=F32, name, mode=None, u=None, comm=None, shard_out=False):
    m, k = (a.shape[1], a.shape[0]) if ta else a.shape
    n = b.shape[0] if tb else b.shape[1]
    assert m % tm == 0 and n % tn == 0 and k % tk == 0, (name, m, n, k)
    nk = k // tk
    a_spec = pl.BlockSpec((tk, tm), lambda i, j, kk: (kk, i)) if ta else pl.BlockSpec((tm, tk), lambda i, j, kk: (i, kk))
    b_spec = pl.BlockSpec((tn, tk), lambda i, j, kk: (j, kk)) if tb else pl.BlockSpec((tk, tn), lambda i, j, kk: (kk, j))
    o_spec = pl.BlockSpec((tm, tn), lambda i, j, kk: (i, j))
    dims = ((0,) if ta else (1,), (1,) if tb else (0,))
    n_out = 2 if mode == "relu2" else 1

    def body(*refs):
        if mode == "drelu2":
            a_ref, b_ref, u_ref = refs[:3]
            rest = refs[3:]
        else:
            a_ref, b_ref = refs[:2]
            u_ref = None
            rest = refs[2:]
        outs = rest[:n_out]
        part = _dot(a_ref[...], b_ref[...], dims)

        def finish(r):
            if mode == "relu2":
                outs[0][...] = r.astype(BF16)
                rr = jnp.maximum(r, 0.0)
                outs[1][...] = (rr * rr).astype(BF16)
            elif mode == "drelu2":
                outs[0][...] = (r * (2.0 * jnp.maximum(u_ref[...].astype(F32), 0.0))).astype(out_dtype)
            else:
                outs[0][...] = r.astype(out_dtype)

        if nk == 1:
            finish(part)
        else:
            acc = rest[n_out]
            kk = pl.program_id(2)

            @pl.when(kk == 0)
            def _():
                acc[...] = part

            @pl.when(kk > 0)
            def _():
                acc[...] += part

            @pl.when(kk == nk - 1)
            def _():
                finish(acc[...])

    in_specs = [a_spec, b_spec]
    args = [a, b]
    if mode == "drelu2":
        in_specs.append(o_spec)
        args.append(u)
    if mode == "relu2":
        out_shape = [jax.ShapeDtypeStruct((m, n), BF16), jax.ShapeDtypeStruct((m, n), BF16)]
    elif shard_out:
        out_shape = [jax.ShapeDtypeStruct((n // tn, m, tn), out_dtype)]
        o_spec = pl.BlockSpec((None, tm, tn), lambda i, j, kk: (j, i, 0))
    else:
        out_shape = [jax.ShapeDtypeStruct((m, n), out_dtype)]
    outs, comm_outs = _pcall(
        body, args, name=name, grid=(m // tm, n // tn, nk), in_specs=in_specs, out_specs=[o_spec] * n_out,
        out_shape=out_shape, scratch_shapes=[pltpu.VMEM((tm, tn), F32)] if nk > 1 else [],
        sem=("parallel", "parallel", "arbitrary"), comm=comm)
    res = tuple(outs) if mode == "relu2" else outs[0]
    return res if comm is None else (res, comm_outs)


def _pieces_t_matmul(groups, b, *, tm, name):
    k, n = b.shape
    pieces = [p for g in groups for p in g]
    starts, tiles = [], 0
    for p in pieces:
        assert p.shape[0] == k and p.shape[1] % tm == 0, (name, p.shape)
        starts.append(tiles)
        tiles += p.shape[1] // tm
    group_of, group_start, group_tiles = [], [], []
    for gi, g in enumerate(groups):
        group_start.append(starts[len(group_of)])
        group_of += [gi] * len(g)
        group_tiles.append(sum(p.shape[1] // tm for p in g))

    def clipped(block, start, count):
        return pl.BlockSpec(block, (lambda i: (0, jnp.clip(i - start, 0, count - 1))) if block[0] == k
                            else (lambda i: (jnp.clip(i - start, 0, count - 1), 0)))

    def body(*refs):
        a_refs, b_ref, o_refs = refs[:len(pieces)], refs[len(pieces)], refs[len(pieces) + 1:]
        i = pl.program_id(0)
        for a_ref, start, p, gi in zip(a_refs, starts, pieces, group_of):
            @pl.when((i >= start) & (i < start + p.shape[1] // tm))
            def _(a_ref=a_ref, o_ref=o_refs[gi]):
                o_ref[...] = _dot(a_ref[...], b_ref[...], TN).astype(BF16)

    return pl.pallas_call(
        body, name=name, grid=(tiles,),
        in_specs=[clipped((k, tm), s0, p.shape[1] // tm) for s0, p in zip(starts, pieces)] + [pl.BlockSpec((k, n), lambda i: (0, 0))],
        out_specs=[clipped((tm, n), s0, cnt) for s0, cnt in zip(group_start, group_tiles)],
        out_shape=[jax.ShapeDtypeStruct((cnt * tm, n), BF16) for cnt in group_tiles],
        compiler_params=_cparams(("arbitrary",)))(*pieces, b)


def _rms_mod(xv, nw, scale, shift):
    r = lax.rsqrt(jnp.mean(xv * xv, axis=-1, keepdims=True) + EPS)
    return ((xv * r) * nw * (1.0 + scale) + shift).astype(BF16)


def _norm_mod_fwd(x, nw, scale, shift, name):
    s, d = x.shape
    row = pl.BlockSpec((ROW_TILE, d), lambda i: (i, 0))
    vec = pl.BlockSpec((1, d), lambda i: (0, 0))

    def body(x_ref, nw_ref, sc_ref, sh_ref, h_ref):
        h_ref[...] = _rms_mod(x_ref[...], nw_ref[...], sc_ref[...], sh_ref[...])

    return pl.pallas_call(body, name=name, grid=(s // ROW_TILE,), in_specs=[row, vec, vec, vec], out_specs=row,
                          out_shape=jax.ShapeDtypeStruct((s, d), BF16), compiler_params=_cparams(("parallel",)))(x, nw, scale, shift)


def _matmul_rows(a, b, epilogue, row_in, vec_in, outs, *, tb=False, tm, name, comm=None):
    pieces = a if isinstance(a, list) else [(a, 0)]
    assert not (tb and len(pieces) > 1)
    m = pieces[0][0].shape[0]
    n = b.shape[0] if tb else b.shape[1]
    assert m % tm == 0, (name, m, tm)
    dims = ((1,), (1,) if tb else (0,))
    n_a, n_row, n_vec = len(pieces), len(row_in), len(vec_in)

    def body(*refs):
        a_refs, b_ref, rest = refs[:n_a], refs[n_a], refs[n_a + 1:]
        if n_a == 1:
            c = _dot(a_refs[0][...], b_ref[...], dims)
        else:
            c = None
            for a_ref, (piece, off) in zip(a_refs, pieces):
                part = _dot(a_ref[...], b_ref[off:off + piece.shape[1], :], dims)
                c = part if c is None else c + part
        epilogue(c, pl.program_id(0) == 0, rest[:n_row], rest[n_row:n_row + n_vec], rest[n_row + n_vec:])

    def spec(kind, width):
        block = {"row": (tm, width), "vec": (1, width), "one": (1, 1)}[kind]
        return pl.BlockSpec(block, (lambda i: (i, 0)) if kind == "row" else (lambda i: (0, 0)))

    def shape(kind, width):
        return {"row": (m, width), "vec": (1, width), "one": (1, 1)}[kind]

    outs = [(o[0], o[1], o[2] if len(o) > 2 else n) for o in outs]
    res, comm_outs = _pcall(
        body, [*[p for p, _ in pieces], b, *row_in, *vec_in], name=name, grid=(m // tm,),
        in_specs=[spec("row", p.shape[1]) for p, _ in pieces] + [pl.BlockSpec(b.shape, lambda i: (0, 0))]
        + [spec("row", r.shape[1]) for r in row_in] + [spec("vec", v.shape[1]) for v in vec_in],
        out_specs=[spec(kind, width) for kind, _, width in outs],
        out_shape=[jax.ShapeDtypeStruct(shape(kind, width), dt) for kind, dt, width in outs],
        sem=("arbitrary",), comm=comm)
    return res if comm is None else (res, comm_outs)


def _residual_norm_epilogue(mix, first, rows, vecs, outs):
    (x_ref,), (gate_ref, nw_ref, sc_ref, sh_ref), (mix_ref, x1_ref, h_ref) = rows, vecs, outs
    xv = x_ref[...] + gate_ref[...] * mix
    mix_ref[...] = mix
    x1_ref[...] = xv
    h_ref[...] = _rms_mod(xv, nw_ref[...], sc_ref[...], sh_ref[...])


def _loss_epilogue(ff, first, rows, vecs, outs):
    (x1_ref, t_ref), (g_ref,), (loss_ref, dout_ref, dff_ref, dg_ref) = rows, vecs, outs
    d = ff.shape[1]

    @pl.when(first)
    def _():
        loss_ref[...] = jnp.zeros_like(loss_ref)
        dg_ref[...] = jnp.zeros_like(dg_ref)

    err = x1_ref[...] + g_ref[...] * ff - t_ref[...]
    loss_ref[...] += (0.5 / d) * jnp.sum(err * err).reshape(1, 1)
    dout = err * (1.0 / d)
    dout_ref[...] = dout
    dff_ref[...] = (g_ref[...] * dout).astype(BF16)
    dg_ref[...] += jnp.sum(dout * ff, axis=0, keepdims=True)


def _norm_bwd_epilogue(dh, first, rows, vecs, outs):
    with_gate = len(vecs) == 3
    x_ref, dres_ref = rows[:2]
    nw_ref, sc_ref = vecs[:2]
    dx_ref, dsh_ref, dsc_ref, dnw_ref = outs[:4]

    @pl.when(first)
    def _():
        for ref in outs[1:4] + outs[5:]:
            ref[...] = jnp.zeros_like(ref)

    xv = x_ref[...]
    r = lax.rsqrt(jnp.mean(xv * xv, axis=-1, keepdims=True) + EPS)
    nrm = xv * r
    one_sc = 1.0 + sc_ref[...]
    dhn = dh * nrm
    dsh_ref[...] += jnp.sum(dh, axis=0, keepdims=True)
    dsc_ref[...] += jnp.sum(dhn, axis=0, keepdims=True) * nw_ref[...]
    dnw_ref[...] += jnp.sum(dhn, axis=0, keepdims=True) * one_sc
    dn = dh * (nw_ref[...] * one_sc)
    dx = dres_ref[...] + r * (dn - nrm * jnp.mean(dn * nrm, axis=-1, keepdims=True))
    dx_ref[...] = dx
    if with_gate:
        outs[4][...] = (vecs[2][...] * dx).astype(BF16)
        outs[5][...] += jnp.sum(dx * rows[2][...], axis=0, keepdims=True)


CONV_COLS = 256
CONV_FWD_ROWS = 2048
CONV_BWD_ROWS = 1024
CONV_SUB_ROWS = 128
HALO = 8


def _shift_down(cur, halo, k):
    if k == 0:
        return cur
    rolled = pltpu.roll(cur, k, axis=0)
    top = jnp.where(lax.broadcasted_iota(jnp.int32, halo.shape, 0) < k, pltpu.roll(halo, k, axis=0), rolled[:HALO])
    return jnp.concatenate([top, rolled[HALO:]], axis=0)


def _shift_up(cur, halo, k):
    if k == 0:
        return cur
    t = cur.shape[0]
    rolled = pltpu.roll(cur, t - k, axis=0)
    bot = jnp.where(lax.broadcasted_iota(jnp.int32, halo.shape, 0) >= HALO - k, pltpu.roll(halo, HALO - k, axis=0),
                    rolled[t - HALO:])
    return jnp.concatenate([rolled[:t - HALO], bot], axis=0)


def _conv_fwd(proj, conv_w, conv_b):
    s = proj.shape[0]
    nr = s // CONV_FWD_ROWS
    cb0 = OFF_XBC // CONV_COLS
    hb = CONV_FWD_ROWS // HALO
    cur = pl.BlockSpec((CONV_FWD_ROWS, CONV_COLS), lambda j, r: (r, cb0 + j))
    prev = pl.BlockSpec((HALO, CONV_COLS), lambda j, r: (jnp.maximum(r * hb - 1, 0), cb0 + j))
    out = pl.BlockSpec((CONV_FWD_ROWS, CONV_COLS), lambda j, r: (r, j))

    def body(u_ref, up_ref, w_ref, b_ref, pre_ref, act_ref):
        r = pl.program_id(1)
        for c in range(CONV_FWD_ROWS // CONV_SUB_ROWS):
            rows = slice(c * CONV_SUB_ROWS, (c + 1) * CONV_SUB_ROWS)
            u = u_ref[rows, :]
            halo = u_ref[c * CONV_SUB_ROWS - HALO:c * CONV_SUB_ROWS, :] if c > 0 else jnp.where(r > 0, up_ref[...], 0.0)
            acc = b_ref[...] + w_ref[CONV_K - 1:CONV_K, :] * u
            for k in range(1, CONV_K):
                acc = acc + w_ref[CONV_K - 1 - k:CONV_K - k, :] * _shift_down(u, halo, k)
            pre_ref[rows, :] = acc
            act_ref[rows, :] = acc * _sigmoid(acc)

    return pl.pallas_call(
        body, name="conv_fwd", grid=(CONV_CH // CONV_COLS, nr),
        in_specs=[cur, prev, pl.BlockSpec((CONV_K, CONV_COLS), lambda j, r: (0, j)),
                  pl.BlockSpec((1, CONV_COLS), lambda j, r: (0, j))],
        out_specs=[out, out],
        out_shape=[jax.ShapeDtypeStruct((s, CONV_CH), F32), jax.ShapeDtypeStruct((s, CONV_CH), F32)],
        compiler_params=_cparams(("parallel", "arbitrary")))(proj, proj, conv_w, conv_b)


def _conv_bwd(dact, pre, proj, conv_w):
    s = proj.shape[0]
    nr = s // CONV_BWD_ROWS
    cb0 = OFF_XBC // CONV_COLS
    hb = CONV_BWD_ROWS // HALO
    last_halo = s // HALO - 1
    n_sub = CONV_BWD_ROWS // CONV_SUB_ROWS
    cur = pl.BlockSpec((CONV_BWD_ROWS, CONV_COLS), lambda j, r: (r, j))
    nxt = pl.BlockSpec((HALO, CONV_COLS), lambda j, r: (jnp.minimum((r + 1) * hb, last_halo), j))
    ucur = pl.BlockSpec((CONV_BWD_ROWS, CONV_COLS), lambda j, r: (r, cb0 + j))
    wspec = pl.BlockSpec((CONV_K, CONV_COLS), lambda j, r: (0, j))
    bspec = pl.BlockSpec((1, CONV_COLS), lambda j, r: (0, j))

    def dsilu(p):
        sg = _sigmoid(p)
        return sg * (1.0 + p * (1.0 - sg))

    def body(da_ref, dan_ref, pre_ref, pren_ref, u_ref, w_ref, du_ref, dw_ref, db_ref):
        r = pl.program_id(1)

        @pl.when(r == 0)
        def _():
            dw_ref[...] = jnp.zeros_like(dw_ref)
            db_ref[...] = jnp.zeros_like(db_ref)

        dws = [jnp.zeros((1, CONV_COLS), F32) for _ in range(CONV_K)]
        db = jnp.zeros((1, CONV_COLS), F32)
        for c in range(n_sub):
            rows = slice(c * CONV_SUB_ROWS, (c + 1) * CONV_SUB_ROWS)
            ahead = slice((c + 1) * CONV_SUB_ROWS, (c + 1) * CONV_SUB_ROWS + HALO)
            dpre = da_ref[rows, :] * dsilu(pre_ref[rows, :])
            if c < n_sub - 1:
                dnext = da_ref[ahead, :] * dsilu(pre_ref[ahead, :])
            else:
                dnext = jnp.where(r < nr - 1, dan_ref[...] * dsilu(pren_ref[...]), 0.0)
            u = u_ref[rows, :]
            du = w_ref[CONV_K - 1:CONV_K, :] * dpre
            dws[0] = dws[0] + jnp.sum(dpre * u, axis=0, keepdims=True)
            for k in range(1, CONV_K):
                ahead_k = _shift_up(dpre, dnext, k)
                du = du + w_ref[CONV_K - 1 - k:CONV_K - k, :] * ahead_k
                dws[k] = dws[k] + jnp.sum(ahead_k * u, axis=0, keepdims=True)
            du_ref[rows, :] = du.astype(BF16)
            db = db + jnp.sum(dpre, axis=0, keepdims=True)
        dw_ref[...] += jnp.concatenate(dws[::-1], axis=0)
        db_ref[...] += db

    return pl.pallas_call(
        body, name="conv_bwd", grid=(CONV_CH // CONV_COLS, nr),
        in_specs=[cur, nxt, cur, nxt, ucur, wspec],
        out_specs=[cur, wspec, bspec],
        out_shape=[jax.ShapeDtypeStruct((s, CONV_CH), BF16), jax.ShapeDtypeStruct((CONV_K, CONV_CH), F32),
                   jax.ShapeDtypeStruct((1, CONV_CH), F32)],
        compiler_params=_cparams(("parallel", "arbitrary")))(dact, dact, pre, pre, proj, conv_w)


def _ssd_common(dtr, dtb, alog):
    lane = lax.broadcasted_iota(jnp.int32, (1, LANE), 1)
    head_lane = lane < SSD_HEADS
    dt = jnp.where(head_lane, _softplus(dtr + dtb), 0.0)
    a = jnp.where(head_lane, -jnp.exp(alog), 0.0)
    row = lax.broadcasted_iota(jnp.int32, (SSD_CHUNK, SSD_CHUNK), 0)
    col = lax.broadcasted_iota(jnp.int32, (SSD_CHUNK, SSD_CHUNK), 1)
    tril = row >= col
    cs = _dot(tril.astype(F32), dt * a, NN, precision=HIGHEST)
    return dt, a, cs, cs.T, tril, lane


def _split_bf16(v, passes):
    terms, rest = [], v
    for _ in range(passes):
        t = rest.astype(BF16)
        terms.append(t)
        rest = rest - t.astype(F32)
    return terms


def _dot_split(v, m, dims, passes):
    terms = _split_bf16(v, passes)
    if passes == 1:
        return _dot(terms[0], m, dims)
    return _dot(jnp.concatenate(terms, axis=1), jnp.concatenate([m] * passes, axis=0 if dims == NN else 1), dims)


def _ssd_constants():
    heads = jnp.arange(LANE)[:, None]
    exp_mat = (heads == (jnp.arange(SSD_D_INNER)[None, :] // HEAD_DIM)).astype(BF16)
    ind4 = ((jnp.arange(SSD_HEADS * SSD_CHUNK)[:, None] // SSD_CHUNK) == jnp.arange(LANE)[None, :]).astype(BF16)
    return exp_mat, ind4


def _expand_heads(v):
    return jnp.repeat(v[:, :SSD_HEADS], HEAD_DIM, axis=1)


def _ssd_prep(dtr, dtb, alog, exp_mat):
    dt, a, cs, cst, tril, lane = _ssd_common(dtr, dtb, alog)
    return dt, a, cs, cst, tril, lane, _dot_split(dt, exp_mat, NN, 2), _dot_split(cs, exp_mat, NN, 3)


def _chunk_decay_rows(cs, g):
    parts = []
    for e in range(HEADS_PER_GROUP):
        h = g * HEADS_PER_GROUP + e
        parts.append(jnp.broadcast_to(jnp.exp(cs[SSD_CHUNK - 1:SSD_CHUNK, h:h + 1]), (HEAD_DIM, SSD_STATE)))
    return jnp.concatenate(parts, axis=0)


def _ssd_fwd(proj, act, dtb, alog, dsk, nw):
    s = proj.shape[0]
    nc = s // SSD_CHUNK
    bc_w = SSD_GROUPS * SSD_STATE
    exp_mat, _ = _ssd_constants()

    def body(z_ref, dtr_ref, xs_ref, b_ref, c_ref, dtb_ref, alog_ref, dskx_ref, nw_ref, exp_ref,
             ypre_ref, yssd_ref, hall_ref, h_scr):
        @pl.when(pl.program_id(0) == 0)
        def _():
            h_scr[...] = jnp.zeros_like(h_scr)

        dt, a, cs, cst, tril, lane, dtx, csx = _ssd_prep(dtr_ref[...], dtb_ref[...], alog_ref[...], exp_ref[...])
        cs_last_x = csx[SSD_CHUNK - 1:SSD_CHUNK, :]
        xs = xs_ref[...]
        xdt = xs * dtx
        xdtb = xdt.astype(BF16)
        xdec = (xdt * jnp.exp(cs_last_x - csx)).astype(BF16)
        ecsx = jnp.exp(csx)
        head_of_lane = lax.broadcasted_iota(jnp.int32, (1, GROUP_WIDTH), 1) // HEAD_DIM
        for g in range(SSD_GROUPS):
            gs = slice(g * GROUP_WIDTH, (g + 1) * GROUP_WIDTH)
            bg = b_ref[:, g * SSD_STATE:(g + 1) * SSD_STATE].astype(BF16)
            cg = c_ref[:, g * SSD_STATE:(g + 1) * SSD_STATE].astype(BF16)
            cb = _dot(cg, bg, NT)
            hprev = h_scr[gs, :]
            hall_ref[0, gs, :] = hprev
            gms, rhs = [], []
            xg = xdtb[:, gs]
            for e in range(HEADS_PER_GROUP):
                h = g * HEADS_PER_GROUP + e
                lm = jnp.exp(jnp.where(tril, cs[:, h:h + 1] - cst[h:h + 1, :], -1e30))
                gms.append((cb * lm).astype(BF16))
                rhs.append(jnp.where(head_of_lane == e, xg, jnp.zeros_like(xg)))
            y = _dot(jnp.concatenate(gms, axis=1), jnp.concatenate(rhs, axis=0), NN)
            y = y + ecsx[:, gs] * _dot(cg, hprev.astype(BF16), NT)
            y = y + dskx_ref[:, gs] * xs[:, gs]
            h_scr[gs, :] = hprev * _chunk_decay_rows(cs, g) + _dot(xdec[:, gs], bg, TN)
            ypre_ref[:, gs] = y
            z = z_ref[:, gs]
            yg = y * (z * _sigmoid(z))
            r = lax.rsqrt(jnp.mean(yg * yg, axis=-1, keepdims=True) + EPS)
            yssd_ref[:, gs] = (yg * r * nw_ref[:, gs]).astype(BF16)

    row_d = lambda cb: pl.BlockSpec((SSD_CHUNK, SSD_D_INNER), lambda c: (c, cb))
    small = pl.BlockSpec((1, LANE), lambda c: (0, 0))
    wide = pl.BlockSpec((1, SSD_D_INNER), lambda c: (0, 0))
    return pl.pallas_call(
        body, name="ssd_fwd", grid=(nc,),
        in_specs=[row_d(OFF_Z // SSD_D_INNER),
                  pl.BlockSpec((SSD_CHUNK, LANE), lambda c: (c, OFF_DT // LANE)),
                  row_d(0),
                  pl.BlockSpec((SSD_CHUNK, bc_w), lambda c: (c, SSD_D_INNER // bc_w)),
                  pl.BlockSpec((SSD_CHUNK, bc_w), lambda c: (c, SSD_D_INNER // bc_w + 1)),
                  small, small, wide, wide, pl.BlockSpec((LANE, SSD_D_INNER), lambda c: (0, 0))],
        out_specs=[row_d(0), row_d(0), pl.BlockSpec((1, SSD_D_INNER, SSD_STATE), lambda c: (c, 0, 0))],
        out_shape=[jax.ShapeDtypeStruct((s, SSD_D_INNER), F32), jax.ShapeDtypeStruct((s, SSD_D_INNER + ATT_D), BF16),
                   jax.ShapeDtypeStruct((nc, SSD_D_INNER, SSD_STATE), F32)],
        scratch_shapes=[pltpu.VMEM((SSD_D_INNER, SSD_STATE), F32)],
        compiler_params=_cparams(("arbitrary",)))(proj, proj, act, act, act, dtb, alog, _expand_heads(dsk), nw, exp_mat)


def _ssd_bwd(dycat, ypre, proj, act, hall, dtb, alog, dsk, nw, comm=None):
    s = proj.shape[0]
    nc = s // SSD_CHUNK
    bc_w = SSD_GROUPS * SSD_STATE

    exp_mat, ind4 = _ssd_constants()
    seg_passes = 1

    def body(dy_ref, ypre_ref, z_ref, dtr_ref, xs_ref, b_ref, c_ref, hall_ref, dtb_ref, alog_ref, dskx_ref, nw_ref,
             exp_ref, ind4_ref, dz_ref, dact_ref, ddtr_ref, da_ref, ddsk_ref, ddtb_ref, dnw_ref, dh_scr):
        @pl.when(pl.program_id(0) == 0)
        def _():
            dh_scr[...] = jnp.zeros_like(dh_scr)
            da_ref[...] = jnp.zeros_like(da_ref)
            ddsk_ref[...] = jnp.zeros_like(ddsk_ref)
            ddtb_ref[...] = jnp.zeros_like(ddtb_ref)
            dnw_ref[...] = jnp.zeros_like(dnw_ref)

        dtr = dtr_ref[...]
        dt, a, cs, cst, tril, lane, dtx, csx = _ssd_prep(dtr, dtb_ref[...], alog_ref[...], exp_ref[...])
        cs_last_x = csx[SSD_CHUNK - 1:SSD_CHUNK, :]
        xs = xs_ref[...]
        xdt = xs * dtx
        xdtb = xdt.astype(BF16)
        decx = jnp.exp(cs_last_x - csx)
        xdecf = xdt * decx
        xdec = xdecf.astype(BF16)
        ecsx = jnp.exp(csx)
        head_of_lane = lax.broadcasted_iota(jnp.int32, (1, GROUP_WIDTH), 1) // HEAD_DIM
        last_row = lax.broadcasted_iota(jnp.int32, (SSD_CHUNK, 1), 0) == SSD_CHUNK - 1
        dcs_col = jnp.zeros((SSD_CHUNK, LANE), F32)
        dcs_row = jnp.zeros((SSD_CHUNK, LANE), F32)
        ddt = jnp.zeros((SSD_CHUNK, LANE), F32)
        ddsk = jnp.zeros((1, LANE), F32)
        hsum = jnp.zeros((1, LANE), F32)
        t1_sum = jnp.zeros((1, LANE), F32)
        for g in range(SSD_GROUPS):
            gs = slice(g * GROUP_WIDTH, (g + 1) * GROUP_WIDTH)
            bsl = slice(g * SSD_STATE, (g + 1) * SSD_STATE)
            exp_g = exp_ref[:, gs]
            ind4_g = ind4_ref[g * HEADS_PER_GROUP * SSD_CHUNK:(g + 1) * HEADS_PER_GROUP * SSD_CHUNK, :]
            z = z_ref[:, gs]
            sg = _sigmoid(z)
            sz = z * sg
            ypre = ypre_ref[:, gs]
            yg = ypre * sz
            r = lax.rsqrt(jnp.mean(yg * yg, axis=-1, keepdims=True) + EPS)
            nrm = yg * r
            dyo_n = dy_ref[:, gs]
            dnw_ref[:, gs] += jnp.sum(dyo_n * nrm, axis=0, keepdims=True)
            dn = dyo_n * nw_ref[:, gs]
            dyg = r * (dn - nrm * jnp.mean(dn * nrm, axis=-1, keepdims=True))
            dz_ref[:, gs] = (dyg * ypre * (sg * (1.0 + z * (1.0 - sg)))).astype(BF16)
            dy = dyg * sz

            bg = b_ref[:, bsl].astype(BF16)
            cg = c_ref[:, bsl].astype(BF16)
            cb = _dot(cg, bg, NT)
            hprev = hall_ref[0, gs, :]
            hb = hprev.astype(BF16)
            dhn = dh_scr[gs, :]
            dhb = dhn.astype(BF16)
            xs_g, xdt_g = xs[:, gs], xdtb[:, gs]
            w_off = _dot(cg, hb, NT)
            dyo = dy * ecsx[:, gs]
            dyob = dyo.astype(BF16)
            dcg = _dot(dyob, hb, NN)
            dh_y = _dot(dyob, cg, TN)
            r_st = _dot(bg, dhb, NT)
            dbg = _dot(xdec[:, gs], dhb, NN)
            dyb = dy.astype(BF16)
            gms, gmbs, lms, dys = [], [], [], []
            for e in range(HEADS_PER_GROUP):
                h = g * HEADS_PER_GROUP + e
                lm = jnp.exp(jnp.where(tril, cs[:, h:h + 1] - cst[h:h + 1, :], -1e30))
                gm = cb * lm
                lms.append(lm)
                gms.append(gm)
                gmbs.append(gm.astype(BF16))
                dys.append(jnp.where(head_of_lane == e, dyb, jnp.zeros_like(dyb)))
            dxdt = _dot(jnp.concatenate(gmbs, axis=0), jnp.concatenate(dys, axis=0), TN) + decx[:, gs] * r_st
            dcb = jnp.zeros((SSD_CHUNK, SSD_CHUNK), F32)
            mms = []
            for e in range(HEADS_PER_GROUP):
                dg = _dot(dys[e], xdt_g, NT)
                mms.append(dg * gms[e])
                dcb = dcb + dg * lms[e]
            seg = _dot_split(jnp.concatenate([dyo * w_off, xdecf[:, gs] * r_st, dxdt * xs_g, dy * xs_g], axis=0), exp_g, NT, seg_passes)
            v1, t1, ddt_g, dsk_g = [seg[i * SSD_CHUNK:(i + 1) * SSD_CHUNK] for i in range(4)]
            dcs_col = dcs_col + v1 - t1 + _dot_split(jnp.concatenate(mms, axis=1), ind4_g, NN, seg_passes)
            for t in _split_bf16(jnp.concatenate(mms, axis=0), seg_passes):
                dcs_row = dcs_row + _dot(ind4_g, t, TN)
            ddt = ddt + ddt_g
            ddsk = ddsk + jnp.sum(dsk_g, axis=0, keepdims=True)
            t1_sum = t1_sum + jnp.sum(t1, axis=0, keepdims=True)
            for e in range(HEADS_PER_GROUP):
                h = g * HEADS_PER_GROUP + e
                hs = slice(e * HEAD_DIM, (e + 1) * HEAD_DIM)
                hsum = hsum + jnp.where(lane == h, jnp.sum(dhn[hs, :] * hprev[hs, :]).reshape(1, 1), 0.0)
            dh_scr[gs, :] = dhn * _chunk_decay_rows(cs, g) + dh_y
            dcbb = dcb.astype(BF16)
            dact_ref[:, gs] = dxdt * dtx[:, gs] + dskx_ref[:, gs] * dy
            dact_ref[:, SSD_D_INNER + g * SSD_STATE:SSD_D_INNER + (g + 1) * SSD_STATE] = dbg + _dot(dcbb, cg, TN)
            dact_ref[:, SSD_D_INNER + bc_w + g * SSD_STATE:SSD_D_INNER + bc_w + (g + 1) * SSD_STATE] = dcg + _dot(dcbb, bg, NN)
        dlast = t1_sum + jnp.exp(cs[SSD_CHUNK - 1:SSD_CHUNK, :]) * hsum
        dcs = dcs_col - dcs_row.T + jnp.where(last_row, dlast, 0.0)
        row = lax.broadcasted_iota(jnp.int32, (SSD_CHUNK, SSD_CHUNK), 0)
        col = lax.broadcasted_iota(jnp.int32, (SSD_CHUNK, SSD_CHUNK), 1)
        dda = _dot((col >= row).astype(F32), dcs, NN, precision=HIGHEST)
        ddt = ddt + dda * a
        da_ref[...] += jnp.sum(dda * dt, axis=0, keepdims=True)
        ddtr = jnp.where(lane < SSD_HEADS, ddt * _sigmoid(dtr + dtb_ref[...]), 0.0)
        ddtr_ref[...] = ddtr.astype(BF16)
        ddtb_ref[...] += jnp.sum(ddtr, axis=0, keepdims=True)
        ddsk_ref[...] += ddsk

    rev = lambda c: nc - 1 - c
    row_d = lambda cb: pl.BlockSpec((SSD_CHUNK, SSD_D_INNER), lambda c: (rev(c), cb))
    small = pl.BlockSpec((1, LANE), lambda c: (0, 0))
    wide = pl.BlockSpec((1, SSD_D_INNER), lambda c: (0, 0))
    small_shape = jax.ShapeDtypeStruct((1, LANE), F32)
    return _pcall(
        body, (dycat, ypre, proj, proj, act, act, act, hall, dtb, alog, _expand_heads(dsk), nw, exp_mat, ind4),
        name="ssd_bwd", grid=(nc,),
        in_specs=[row_d(0), row_d(0), row_d(OFF_Z // SSD_D_INNER),
                  pl.BlockSpec((SSD_CHUNK, LANE), lambda c: (rev(c), OFF_DT // LANE)),
                  row_d(0),
                  pl.BlockSpec((SSD_CHUNK, bc_w), lambda c: (rev(c), SSD_D_INNER // bc_w)),
                  pl.BlockSpec((SSD_CHUNK, bc_w), lambda c: (rev(c), SSD_D_INNER // bc_w + 1)),
                  pl.BlockSpec((1, SSD_D_INNER, SSD_STATE), lambda c: (rev(c), 0, 0)),
                  small, small, wide, wide, pl.BlockSpec((LANE, SSD_D_INNER), lambda c: (0, 0)),
                  pl.BlockSpec((SSD_HEADS * SSD_CHUNK, LANE), lambda c: (0, 0))],
        out_specs=[row_d(0), pl.BlockSpec((SSD_CHUNK, CONV_CH), lambda c: (rev(c), 0)),
                   pl.BlockSpec((SSD_CHUNK, LANE), lambda c: (rev(c), 0)), small, small, small, wide],
        out_shape=[jax.ShapeDtypeStruct((s, SSD_D_INNER), BF16), jax.ShapeDtypeStruct((s, CONV_CH), F32),
                   jax.ShapeDtypeStruct((s, LANE), BF16), small_shape, small_shape, small_shape,
                   jax.ShapeDtypeStruct((1, SSD_D_INNER), F32)],
        scratch_shapes=[pltpu.VMEM((SSD_D_INNER, SSD_STATE), F32)], sem=("arbitrary",), comm=comm)


def _head_mean_matrix():
    row = lax.broadcasted_iota(jnp.int32, (LANE, LANE), 0) // HEAD_DIM
    col = lax.broadcasted_iota(jnp.int32, (LANE, LANE), 1) // HEAD_DIM
    return (row == col).astype(F32)


def _head_sum2(v, ones_bd):
    hi = v.astype(BF16)
    lo = (v - hi.astype(F32)).astype(BF16)
    return _dot(jnp.concatenate([hi, lo], axis=1), jnp.concatenate([ones_bd, ones_bd], axis=0), NN)


def _head_norms(xs, ws, ones_bd):
    sums = [_head_sum2(x * x, ones_bd) for x in xs]
    return [(x * lax.rsqrt(ms * (1.0 / HEAD_DIM) + EPS)) * w for x, ms, w in zip(xs, sums, ws)]


def _head_norms_bwd(dns, xs, ws, ones_bd):
    sums = [_head_sum2(x * x, ones_bd) for x in xs]
    rs = [lax.rsqrt(ms * (1.0 / HEAD_DIM) + EPS) for ms in sums]
    nrms = [x * r for x, r in zip(xs, rs)]
    dnws = [dn * w for dn, w in zip(dns, ws)]
    projs = [_head_sum2(dnw * nrm, ones_bd) for dnw, nrm in zip(dnws, nrms)]
    dxs = [r * (dnw - nrm * (pr * (1.0 / HEAD_DIM))) for r, dnw, nrm, pr in zip(rs, dnws, nrms, projs)]
    return dxs, [jnp.sum(dn * nrm, axis=0, keepdims=True) for dn, nrm in zip(dns, nrms)]


NORM_CHUNKS = 2


PRO_ROWS = 256
ATT_GROUP_FWD = 16
ATT_GROUP_BWD = 8
KEYS = 2 * ATT_BLK
NEG = -1e30
HALF = HEAD_DIM // 2


def _rows(start, size, dil):
    return pl.ds(start, size) if dil == 1 else pl.ds(start, size, stride=dil)


def _fill_bias(bias_ref):
    row = lax.broadcasted_iota(jnp.int32, (ATT_BLK, 2 * KEYS), 0)
    col = lax.broadcasted_iota(jnp.int32, (ATT_BLK, 2 * KEYS), 1) & (KEYS - 1)
    for first, off in ((0, 0), (1, ATT_BLK)):
        dist = off + row - col
        bias_ref[first] = jnp.where((dist >= 0) & (dist <= ATT_BLK), 0.0, NEG)


def _pair(a, b):
    return jnp.concatenate([jnp.broadcast_to(a, (ATT_BLK, KEYS)), jnp.broadcast_to(b, (ATT_BLK, KEYS))], axis=1)


def _split_heads(x, is_a):
    zero = jnp.zeros_like(x)
    return jnp.concatenate([jnp.where(is_a, x, zero), jnp.where(is_a, zero, x)], axis=0)


def _block_ids(b, nb):
    i = b & (nb - 1)
    q0 = pl.multiple_of(b * ATT_BLK, ATT_BLK)
    k0 = pl.multiple_of((b - jnp.minimum(i, 1)) * ATT_BLK, ATT_BLK)
    return pl.ds(q0, ATT_BLK), pl.ds(k0, KEYS), jnp.minimum(i, 1)


def _att_fwd(proj, qw, kw, comm=None):
    s = proj.shape[0]
    nblk = s // ATT_BLK
    assert all((s // d) // ATT_BLK >= 2 for d in DILATIONS)
    blk = lambda off: pl.BlockSpec((s, LANE), lambda i: (0, off // LANE + i))
    wspec = pl.BlockSpec((1, LANE), lambda i: (0, i))
    oblk = pl.BlockSpec((s, LANE), lambda i: (0, i))

    def body(q_ref, k_ref, v_ref, qw_ref, kw_ref, o_ref, lse_ref, qn, kn, q_cm, k_cm, v_cm, m_acc, l_acc, o_d, m_d, l_d, bias):
        ones_bd = _head_mean_matrix().astype(BF16)
        is_a = lax.broadcasted_iota(jnp.int32, (1, LANE), 1) < HEAD_DIM
        ones_ext = _split_heads(jnp.ones((KEYS, LANE), BF16), is_a)
        _fill_bias(bias)

        def pro(j, c):
            chunks = [pl.ds(pl.multiple_of((NORM_CHUNKS * j + u) * PRO_ROWS, PRO_ROWS), PRO_ROWS) for u in range(NORM_CHUNKS)]
            normed = _head_norms([q_ref[rows, :] for rows in chunks] + [k_ref[rows, :] for rows in chunks],
                                 [qw_ref[...] * HEAD_DIM ** -0.5] * NORM_CHUNKS + [kw_ref[...]] * NORM_CHUNKS, ones_bd)
            for u, rows in enumerate(chunks):
                qn[rows, :] = normed[u]
                kn[rows, :] = normed[NORM_CHUNKS + u]
            return c

        lax.fori_loop(0, s // (NORM_CHUNKS * PRO_ROWS), pro, 0)

        for dil in DILATIONS:
            ln = s // dil
            nb = ln // ATT_BLK
            o_out, m_out, l_out = (o_ref, m_acc, l_acc) if dil == 1 else (o_d, m_d, l_d)
            for r in range(dil):
                def relayout(j, c, dil=dil, r=r, ln=ln):
                    j0 = pl.multiple_of(j * PRO_ROWS, PRO_ROWS)
                    src = _rows(r + dil * j0, PRO_ROWS, dil)
                    dst = pl.ds(r * ln + j0, PRO_ROWS)
                    q_cm[dst, :] = qn[src, :].astype(BF16)
                    k_cm[dst, :] = kn[src, :].astype(BF16)
                    v_cm[dst, :] = v_ref[src, :].astype(BF16)
                    return c

                lax.fori_loop(0, ln // PRO_ROWS, relayout, 0)

            def step(bg, c, nb=nb, o_out=o_out, m_out=m_out, l_out=l_out):
                ids = [_block_ids(bg * ATT_GROUP_FWD + u, nb) for u in range(ATT_GROUP_FWD)]
                kbs = [_split_heads(k_cm[krows, :], is_a) for _, krows, _ in ids]
                scs = [_dot(q_cm[qrows, :], kb, NT) + bias[first] for (qrows, _, first), kb in zip(ids, kbs)]
                mas = [jnp.max(sc[:, :KEYS], axis=-1, keepdims=True) for sc in scs]
                mbs = [jnp.max(sc[:, KEYS:], axis=-1, keepdims=True) for sc in scs]
                ps = [jnp.exp(sc - _pair(ma, mb)).astype(BF16) for sc, ma, mb in zip(scs, mas, mbs)]
                vbs = [jnp.concatenate([_split_heads(v_cm[krows, :], is_a), ones_ext], axis=1) for _, krows, _ in ids]
                ols = [_dot(p, vb, NN) for p, vb in zip(ps, vbs)]
                for (qrows, _, _), ol, ma, mb in zip(ids, ols, mas, mbs):
                    o_out[qrows, :] = ol[:, :LANE]
                    l_out[qrows, :] = ol[:, LANE:]
                    m_out[qrows, :] = jnp.where(is_a, ma, mb)
                return c

            lax.fori_loop(0, nblk // ATT_GROUP_FWD, step, 0)

            if dil > 1:
                for r in range(dil):
                    def merge(j, c, dil=dil, r=r, ln=ln):
                        j0 = pl.multiple_of(j * PRO_ROWS, PRO_ROWS)
                        nat = _rows(r + dil * j0, PRO_ROWS, dil)
                        cm = pl.ds(r * ln + j0, PRO_ROWS)
                        m_old, m_new = m_acc[nat, :], m_d[cm, :]
                        m = jnp.maximum(m_old, m_new)
                        a_old, a_new = jnp.exp(m_old - m), jnp.exp(m_new - m)
                        o_ref[nat, :] = a_old * o_ref[nat, :] + a_new * o_d[cm, :]
                        l_acc[nat, :] = a_old * l_acc[nat, :] + a_new * l_d[cm, :]
                        m_acc[nat, :] = m
                        return c

                    lax.fori_loop(0, ln // PRO_ROWS, merge, 0)

        def epi(j, c):
            rows = pl.ds(pl.multiple_of(j * PRO_ROWS, PRO_ROWS), PRO_ROWS)
            l = l_acc[rows, :]
            o_ref[rows, :] = o_ref[rows, :] / l
            lse_ref[rows, :] = m_acc[rows, :] + jnp.log(l)
            return c

        lax.fori_loop(0, s // PRO_ROWS, epi, 0)

    f = jax.ShapeDtypeStruct((s, ATT_D), F32)
    scr = pltpu.VMEM((s, LANE), F32)
    scb = pltpu.VMEM((s, LANE), BF16)
    return _pcall(
        body, (proj, proj, proj, qw, kw), name="att_fwd", grid=(ATT_D // LANE,),
        in_specs=[blk(OFF_Q), blk(OFF_K), blk(OFF_V), wspec, wspec], out_specs=[oblk, oblk], out_shape=[f, f],
        scratch_shapes=[scr, scr, scb, scb, scb, scr, scr, scr, scr, scr, pltpu.VMEM((2, ATT_BLK, 2 * KEYS), F32)],
        sem=("parallel",), comm=comm)


def _att_bwd(proj, do, stats, qw, kw, comm=None):
    s = proj.shape[0]
    nblk = s // ATT_BLK
    blk = lambda off: pl.BlockSpec((s, LANE), lambda i: (0, off // LANE + i))
    wspec = pl.BlockSpec((1, LANE), lambda i: (0, i))
    oblk = pl.BlockSpec((s, LANE), lambda i: (0, i))

    def body(q_ref, k_ref, v_ref, do_ref, st_ref, qw_ref, kw_ref, dq_ref, dk_ref, dv_ref, dqw_ref, dkw_ref,
             qn, kn, q_cm, do_cm, k_cm, v_cm, st_cm, dq_acc, dk_acc, dv_acc, dq_d, dk_d, dv_d, bias):
        ones_bd = _head_mean_matrix().astype(BF16)
        is_a = lax.broadcasted_iota(jnp.int32, (1, LANE), 1) < HEAD_DIM
        _fill_bias(bias)
        zero = jnp.zeros((PRO_ROWS, LANE), F32)

        def pro(j, c):
            chunks = [pl.ds(pl.multiple_of((NORM_CHUNKS * j + u) * PRO_ROWS, PRO_ROWS), PRO_ROWS) for u in range(NORM_CHUNKS)]
            normed = _head_norms([q_ref[rows, :] for rows in chunks] + [k_ref[rows, :] for rows in chunks],
                                 [qw_ref[...] * HEAD_DIM ** -0.5] * NORM_CHUNKS + [kw_ref[...]] * NORM_CHUNKS, ones_bd)
            for u, rows in enumerate(chunks):
                qn[rows, :] = normed[u]
                kn[rows, :] = normed[NORM_CHUNKS + u]
                dk_acc[rows, :] = zero
                dv_acc[rows, :] = zero
            return c

        lax.fori_loop(0, s // (NORM_CHUNKS * PRO_ROWS), pro, 0)

        for dil in DILATIONS:
            ln = s // dil
            nb = ln // ATT_BLK
            dq_o, dk_o, dv_o = (dq_acc, dk_acc, dv_acc) if dil == 1 else (dq_d, dk_d, dv_d)
            for r in range(dil):
                def relayout(j, c, dil=dil, r=r, ln=ln):
                    j0 = pl.multiple_of(j * PRO_ROWS, PRO_ROWS)
                    src = _rows(r + dil * j0, PRO_ROWS, dil)
                    dst = pl.ds(r * ln + j0, PRO_ROWS)
                    q_cm[dst, :] = qn[src, :].astype(BF16)
                    k_cm[dst, :] = kn[src, :].astype(BF16)
                    v_cm[dst, :] = v_ref[src, :].astype(BF16)
                    do_cm[dst, :] = do_ref[src, :].astype(BF16)
                    st_cm[dst, :] = st_ref[src, :]
                    if dil > 1:
                        dk_d[dst, :] = zero
                        dv_d[dst, :] = zero
                    return c

                lax.fori_loop(0, ln // PRO_ROWS, relayout, 0)

            def step(bg, c, nb=nb, dq_o=dq_o, dk_o=dk_o, dv_o=dv_o):
                ids = [_block_ids(bg * ATT_GROUP_BWD + u, nb) for u in range(ATT_GROUP_BWD)]
                qbs = [q_cm[qrows, :] for qrows, _, _ in ids]
                dobs = [do_cm[qrows, :] for qrows, _, _ in ids]
                kbs = [_split_heads(k_cm[krows, :], is_a) for _, krows, _ in ids]
                vbs = [_split_heads(v_cm[krows, :], is_a) for _, krows, _ in ids]
                sts = [st_cm[qrows, :] for qrows, _, _ in ids]
                scs = [_dot(qb, kb, NT) + bias[first] for qb, kb, (_, _, first) in zip(qbs, kbs, ids)]
                dps = [_dot(dob, vb, NT) for dob, vb in zip(dobs, vbs)]
                ps = [jnp.exp(sc - _pair(st[:, 0:1], st[:, HEAD_DIM:HEAD_DIM + 1])) for sc, st in zip(scs, sts)]
                dss = [(p * (dp - _pair(st[:, HALF:HALF + 1], st[:, HEAD_DIM + HALF:HEAD_DIM + HALF + 1]))).astype(BF16)
                       for p, dp, st in zip(ps, dps, sts)]
                dqs = [_dot(ds, kb, NN) for ds, kb in zip(dss, kbs)]
                dkfs = [_dot(ds, qb, TN) for ds, qb in zip(dss, qbs)]
                dvfs = [_dot(p.astype(BF16), dob, TN) for p, dob in zip(ps, dobs)]
                for (qrows, krows, _), dq, dkf, dvf in zip(ids, dqs, dkfs, dvfs):
                    dq_o[qrows, :] = dq
                    dk_o[krows, :] += jnp.where(is_a, dkf[:KEYS], dkf[KEYS:])
                    dv_o[krows, :] += jnp.where(is_a, dvf[:KEYS], dvf[KEYS:])
                return c

            lax.fori_loop(0, nblk // ATT_GROUP_BWD, step, 0)

            if dil > 1:
                for r in range(dil):
                    def merge(j, c, dil=dil, r=r, ln=ln):
                        j0 = pl.multiple_of(j * PRO_ROWS, PRO_ROWS)
                        nat = _rows(r + dil * j0, PRO_ROWS, dil)
                        cm = pl.ds(r * ln + j0, PRO_ROWS)
                        dq_acc[nat, :] += dq_d[cm, :]
                        dk_acc[nat, :] += dk_d[cm, :]
                        dv_acc[nat, :] += dv_d[cm, :]
                        return c

                    lax.fori_loop(0, ln // PRO_ROWS, merge, 0)

        def epi(j, c):
            chunks = [pl.ds(pl.multiple_of((NORM_CHUNKS * j + u) * PRO_ROWS, PRO_ROWS), PRO_ROWS) for u in range(NORM_CHUNKS)]
            dxs, dws = _head_norms_bwd(
                [dq_acc[rows, :] for rows in chunks] + [dk_acc[rows, :] for rows in chunks],
                [q_ref[rows, :] for rows in chunks] + [k_ref[rows, :] for rows in chunks],
                [qw_ref[...] * HEAD_DIM ** -0.5] * NORM_CHUNKS + [kw_ref[...]] * NORM_CHUNKS, ones_bd)
            dqw, dkw = c
            for u, rows in enumerate(chunks):
                dq_ref[rows, :] = dxs[u].astype(BF16)
                dk_ref[rows, :] = dxs[NORM_CHUNKS + u].astype(BF16)
                dv_ref[rows, :] = dv_acc[rows, :].astype(BF16)
                dqw, dkw = dqw + dws[u], dkw + dws[NORM_CHUNKS + u]
            return dqw, dkw

        zrow = jnp.zeros((1, LANE), F32)
        dqw, dkw = lax.fori_loop(0, s // (NORM_CHUNKS * PRO_ROWS), epi, (zrow, zrow))
        dqw_ref[...] = dqw * HEAD_DIM ** -0.5
        dkw_ref[...] = dkw

    o = jax.ShapeDtypeStruct((s, ATT_D), BF16)
    ov = jax.ShapeDtypeStruct((1, ATT_D), F32)
    scr = pltpu.VMEM((s, LANE), F32)
    scb = pltpu.VMEM((s, LANE), BF16)
    return _pcall(
        body, (proj, proj, proj, do, stats, qw, kw), name="att_bwd", grid=(ATT_D // LANE,),
        in_specs=[blk(OFF_Q), blk(OFF_K), blk(OFF_V), oblk, oblk, wspec, wspec],
        out_specs=[oblk, oblk, oblk, wspec, wspec], out_shape=[o, o, o, ov, ov],
        scratch_shapes=[scr, scr, scb, scb, scb, scb, scr, scr, scr, scr, scr, scr, scr, pltpu.VMEM((2, ATT_BLK, 2 * KEYS), F32)],
        sem=("parallel",), comm=comm)


def _att_norm_fwd(o, nw, ycat):
    s = o.shape[0]
    row = pl.BlockSpec((ROW_TILE, ATT_D), lambda i: (i, 0))
    vec = pl.BlockSpec((1, ATT_D), lambda i: (0, 0))

    def body(o_ref, nw_ref, ycat_ref, y_ref):
        o = o_ref[...]
        r = lax.rsqrt(jnp.mean(o * o, axis=-1, keepdims=True) + EPS)
        y_ref[...] = (o * r * nw_ref[...]).astype(BF16)

    return pl.pallas_call(body, name="att_norm_fwd", grid=(s // ROW_TILE,),
                          in_specs=[row, vec, pl.BlockSpec(memory_space=pl.ANY)],
                          out_specs=pl.BlockSpec((ROW_TILE, ATT_D), lambda i: (i, 1)),
                          out_shape=jax.ShapeDtypeStruct(ycat.shape, BF16), input_output_aliases={2: 0},
                          compiler_params=_cparams(("parallel",)))(o, nw, ycat)


def _mixer_split_epilogue(dycat, first, rows, vecs, outs):
    (o_ref, lse_ref), (nw_ref,), (dyssd_ref, do_ref, st_ref, dnw_ref) = rows, vecs, outs

    @pl.when(first)
    def _():
        dnw_ref[...] = jnp.zeros_like(dnw_ref)

    dyssd_ref[...] = dycat[:, :SSD_D_INNER]
    dy = dycat[:, SSD_D_INNER:]
    o = o_ref[...]
    r = lax.rsqrt(jnp.mean(o * o, axis=-1, keepdims=True) + EPS)
    nrm = o * r
    dnw_ref[...] += jnp.sum(dy * nrm, axis=0, keepdims=True)
    dn = dy * nw_ref[...]
    do = r * (dn - nrm * jnp.mean(dn * nrm, axis=-1, keepdims=True))
    do_ref[...] = do
    ones_bd = _head_mean_matrix().astype(BF16)
    prod = do * o
    delta = jnp.concatenate([_head_sum2(prod[:, j * LANE:(j + 1) * LANE], ones_bd) for j in range(ATT_D // LANE)], axis=1)
    lane = lax.broadcasted_iota(jnp.int32, (1, ATT_D), 1)
    st_ref[...] = jnp.where((lane & (HEAD_DIM - 1)) < HALF, lse_ref[...], delta)


def _ada_fwd(c_all, w_ada):
    def body(c_ref, w_ref, o_ref):
        cv = c_ref[...]
        o_ref[...] = _dot((cv * _sigmoid(cv)).astype(BF16), w_ref[...].astype(BF16), NN)

    return pl.pallas_call(body, name="ada_fwd", out_shape=jax.ShapeDtypeStruct((c_all.shape[0], w_ada.shape[1]), F32),
                          compiler_params=_cparams())(c_all, w_ada)


def _adamw_math(g, w, m, v):
    m_new = ADAM_B1 * m + (1.0 - ADAM_B1) * g
    v_new = ADAM_B2 * v + (1.0 - ADAM_B2) * (g * g)
    m_hat = m_new / (1.0 - ADAM_B1 ** ADAM_STEP)
    v_hat = v_new / (1.0 - ADAM_B2 ** ADAM_STEP)
    delta = -ADAM_LR * (m_hat / (jnp.sqrt(v_hat) + ADAM_EPS) + ADAM_WD * w)
    return delta, m_new, v_new


def _ada_bwd_adamw(c_all, dmod_cols, w, m, v):
    rows, cols = w.shape
    tr = 256
    blk = pl.BlockSpec((tr, cols), lambda i: (i, 0))

    def body(c_ref, d_ref, w_ref, m_ref, v_ref, g_ref, dl_ref, mo_ref, vo_ref):
        cv = c_ref[...]
        ca = cv * _sigmoid(cv)
        g = ca[:, 0:1] * d_ref[0:1, :]
        for b in range(1, N_DEV):
            g = g + ca[:, b:b + 1] * d_ref[b:b + 1, :]
        g_ref[...] = g
        dl_ref[...], mo_ref[...], vo_ref[...] = _adamw_math(g, w_ref[...], m_ref[...], v_ref[...])

    o = jax.ShapeDtypeStruct((rows, cols), F32)
    return pl.pallas_call(
        body, name="ada_bwd_adamw", grid=(rows // tr,),
        in_specs=[pl.BlockSpec((tr, N_DEV), lambda i: (i, 0)), pl.BlockSpec((N_DEV, cols), lambda i: (0, 0)), blk, blk, blk],
        out_specs=[blk] * 4, out_shape=[o, o, o, o], compiler_params=_cparams(("parallel",)))(c_all.T, dmod_cols, w, m, v)


def _reduce_adamw(slabs, w, m, v, name):
    rows, cols = w.shape
    n_src = slabs.shape[0]
    if rows % 128 == 0:
        tr, steps = 128, rows // 128
        blk = pl.BlockSpec((tr, cols), lambda i: (i, 0))
        sblk = pl.BlockSpec((n_src, tr, cols), lambda i: (0, i, 0))
    else:
        tc, steps = 256, cols // 256
        blk = pl.BlockSpec((rows, tc), lambda i: (0, i))
        sblk = pl.BlockSpec((n_src, rows, tc), lambda i: (0, 0, i))

    def body(s_ref, w_ref, m_ref, v_ref, g_ref, dl_ref, mo_ref, vo_ref):
        g = s_ref[0].astype(F32)
        for src in range(1, n_src):
            g = g + s_ref[src].astype(F32)
        g_ref[...] = g
        dl_ref[...], mo_ref[...], vo_ref[...] = _adamw_math(g, w_ref[...], m_ref[...], v_ref[...])

    o = jax.ShapeDtypeStruct((rows, cols), F32)
    return pl.pallas_call(
        body, name=name, grid=(steps,), in_specs=[sblk, blk, blk, blk],
        out_specs=[blk] * 4, out_shape=[o, o, o, o], compiler_params=_cparams(("parallel",)))(slabs, w, m, v)


def _small_reduce_adamw(gathered, w, m, v):
    def body(s_ref, w_ref, m_ref, v_ref, g_ref, dl_ref, mo_ref, vo_ref):
        g = s_ref[0]
        for dev in range(1, N_DEV):
            g = g + s_ref[dev]
        g_ref[...] = g
        dl_ref[...], mo_ref[...], vo_ref[...] = _adamw_math(g, w_ref[...], m_ref[...], v_ref[...])

    o = jax.ShapeDtypeStruct(w.shape, F32)
    return pl.pallas_call(body, name="small_reduce_adamw", out_shape=[o, o, o, o], compiler_params=_cparams())(gathered, w, m, v)


def _adamw_small(g, w, m, v, name):
    def body(g_ref, w_ref, m_ref, v_ref, dl_ref, mo_ref, vo_ref):
        dl_ref[...], mo_ref[...], vo_ref[...] = _adamw_math(g_ref[...], w_ref[...], m_ref[...], v_ref[...])

    o = jax.ShapeDtypeStruct(w.shape, F32)
    return pl.pallas_call(body, name=name, out_shape=[o, o, o], compiler_params=_cparams())(g, w, m, v)


class _Exchange:
    def __init__(self, arrs, scatter):
        self.arrs, self.scatter, self.n = list(arrs), scatter, len(arrs)
        hbm = pl.BlockSpec(memory_space=pltpu.HBM)
        self.in_specs = [hbm] * self.n
        self.out_specs = [hbm] * self.n
        self.out_shape = [jax.ShapeDtypeStruct(a.shape if scatter else (N_DEV,) + a.shape, a.dtype) for a in self.arrs]
        self.scratch = [pltpu.SemaphoreType.DMA((self.n * (N_DEV - 1),)), pltpu.SemaphoreType.DMA((self.n * (N_DEV - 1),)),
                        pltpu.SemaphoreType.DMA((self.n,))]

    def _local(self, ins, outs, sems):
        me = 4 * lax.axis_index("x") + 2 * lax.axis_index("y") + lax.axis_index("c")
        return [pltpu.make_async_copy(ins[a].at[me] if self.scatter else ins[a], outs[a].at[me], sems[2].at[a])
                for a in range(self.n)]

    def _remote(self, ins, outs, sems, arriving):
        send_sems, recv_sems, _ = sems
        x, y, c = lax.axis_index("x"), lax.axis_index("y"), lax.axis_index("c")
        me = 4 * x + 2 * y + c
        remote = []
        for a in range(self.n):
            for k in range(1, N_DEV):
                px = 1 - x if k & 4 else x
                py = 1 - y if k & 2 else y
                pc = 1 - c if k & 1 else c
                peer = 4 * px + 2 * py + pc
                sem = a * (N_DEV - 1) + k - 1
                remote.append(pltpu.make_async_remote_copy(
                    src_ref=ins[a].at[peer] if self.scatter else ins[a], dst_ref=outs[a].at[peer if arriving else me],
                    send_sem=send_sems.at[sem], recv_sem=recv_sems.at[sem], device_id=(px, py, pc), device_id_type=MESH_IDS))
        return remote

    def start(self, ins, outs, sems):
        for cp in self._local(ins, outs, sems) + self._remote(ins, outs, sems, arriving=False):
            cp.start()

    def forward(self, ins, outs, sems):
        pass

    def wait(self, ins, outs, sems):
        for send, arrival in zip(self._remote(ins, outs, sems, arriving=False), self._remote(ins, outs, sems, arriving=True)):
            send.wait_send()
            arrival.wait_recv()
        for cp in self._local(ins, outs, sems):
            cp.wait()


N_CHIP = N_DEV // 2


class _SiblingSwap(_Exchange):
    def __init__(self, arrs):
        super().__init__(arrs, scatter=True)
        self.out_shape = [jax.ShapeDtypeStruct((N_CHIP,) + a.shape[2:], a.dtype) for a in self.arrs]
        self.scratch = [pltpu.SemaphoreType.DMA((self.n,)), pltpu.SemaphoreType.DMA((self.n,)), pltpu.SemaphoreType.DMA((1,))]

    def _copies(self, ins, outs, sems):
        x, y, c = lax.axis_index("x"), lax.axis_index("y"), lax.axis_index("c")
        return [pltpu.make_async_remote_copy(src_ref=ins[a].at[:, 1 - c], dst_ref=outs[a], send_sem=sems[0].at[a], recv_sem=sems[1].at[a],
                                             device_id=(x, y, 1 - c), device_id_type=MESH_IDS) for a in range(self.n)]

    def start(self, ins, outs, sems):
        for cp in self._copies(ins, outs, sems):
            cp.start()

    def wait(self, ins, outs, sems):
        for cp in self._copies(ins, outs, sems):
            cp.wait()


class _ChipScatter(_Exchange):
    def __init__(self, arrs):
        super().__init__(arrs, scatter=True)
        n_pairs = self.n * (N_CHIP - 1)
        self.scratch = [pltpu.SemaphoreType.DMA((n_pairs,)), pltpu.SemaphoreType.DMA((n_pairs,)), pltpu.SemaphoreType.DMA((self.n,))]

    def _local(self, ins, outs, sems):
        chip = 2 * lax.axis_index("x") + lax.axis_index("y")
        return [pltpu.make_async_copy(ins[a].at[chip], outs[a].at[chip], sems[2].at[a]) for a in range(self.n)]

    def _remote(self, ins, outs, sems, arriving):
        send_sems, recv_sems, _ = sems
        x, y, c = lax.axis_index("x"), lax.axis_index("y"), lax.axis_index("c")
        chip = 2 * x + y
        remote = []
        for a in range(self.n):
            for k in range(1, N_CHIP):
                px = 1 - x if k & 2 else x
                py = 1 - y if k & 1 else y
                peer = 2 * px + py
                sem = a * (N_CHIP - 1) + k - 1
                remote.append(pltpu.make_async_remote_copy(
                    src_ref=ins[a].at[peer], dst_ref=outs[a].at[peer if arriving else chip], send_sem=send_sems.at[sem],
                    recv_sem=recv_sems.at[sem], device_id=(px, py, c), device_id_type=MESH_IDS))
        return remote


def _chip_sum(mine, theirs):
    n, rows, cols = mine.shape
    blk = pl.BlockSpec((1, rows, 256), lambda q, j: (q, 0, j))

    def body(a_ref, b_ref, o_ref):
        o_ref[...] = (a_ref[...].astype(F32) + b_ref[...].astype(F32)).astype(BF16)

    return pl.pallas_call(body, name="chip_sum", grid=(n, cols // 256), in_specs=[blk, blk], out_specs=blk,
                          out_shape=jax.ShapeDtypeStruct(mine.shape, BF16),
                          compiler_params=_cparams(("parallel", "parallel")))(mine, theirs)


class _Gather2(_Exchange):
    def __init__(self, arrs):
        super().__init__(arrs, scatter=False)

    def _copies(self, ins, outs, sems):
        send_sems, recv_sems, _ = sems
        x, y, c = lax.axis_index("x"), lax.axis_index("y"), lax.axis_index("c")
        sibling = (x, y, 1 - c)
        chips = [(1 - x, y), (x, 1 - y), (1 - x, 1 - y)]
        first, passed, landed = [], [], []
        for a in range(self.n):
            def copy(k, block, to, src=None, a=a):
                slab = outs[a].at[4 * block[0] + 2 * block[1] + block[2]]
                return pltpu.make_async_remote_copy(
                    src_ref=slab if src is None else src, dst_ref=slab, send_sem=send_sems.at[a * (N_DEV - 1) + k],
                    recv_sem=recv_sems.at[a * (N_DEV - 1) + k], device_id=to, device_id_type=MESH_IDS)

            first.append(copy(0, (x, y, c), sibling, src=ins[a]))
            landed.append(copy(0, sibling, sibling))
            for j, chip in enumerate(chips):
                first.append(copy(1 + j, (x, y, c), (*chip, c), src=ins[a]))
                passed.append((copy(1 + j, (*chip, c), sibling), copy(4 + j, (*chip, c), sibling)))
                landed.append(copy(4 + j, (*chip, 1 - c), sibling))
        return first, passed, landed

    def start(self, ins, outs, sems):
        for cp in self._local(ins, outs, sems) + self._copies(ins, outs, sems)[0]:
            cp.start()

    def forward(self, ins, outs, sems):
        for arrival, onward in self._copies(ins, outs, sems)[1]:
            arrival.wait_recv()
            onward.start()

    def wait(self, ins, outs, sems):
        first, passed, landed = self._copies(ins, outs, sems)
        for arrival in landed:
            arrival.wait_recv()
        for cp in first + [onward for _, onward in passed]:
            cp.wait_send()
        for cp in self._local(ins, outs, sems):
            cp.wait()


def _split_comm_refs(refs, n_in, n_out, n_scr, comm):
    nc = comm.n if comm is not None else 0
    ns = 3 if comm is not None else 0
    pos, groups = 0, []
    for cnt in (n_in, nc, n_out, nc, n_scr, ns):
        groups.append(refs[pos:pos + cnt])
        pos += cnt
    assert pos == len(refs), (pos, len(refs))
    return groups


def _pcall(body, args, *, name, grid, in_specs, out_specs, out_shape, scratch_shapes=(), sem=None, comm=None):
    in_specs, out_specs, out_shape, scratch_shapes = list(in_specs), list(out_specs), list(out_shape), list(scratch_shapes)
    n_in, n_out, n_scr = len(in_specs), len(out_specs), len(scratch_shapes)
    if comm is None:
        kernel_body = body
    else:
        def kernel_body(*refs):
            ins, cins, outs, couts, scr, sems = _split_comm_refs(refs, n_in, n_out, n_scr, comm)
            ids = [pl.program_id(a) for a in range(len(grid))]
            first, last = ids[0] == 0, ids[0] == grid[0] - 1
            for a in range(1, len(grid)):
                first, last = first & (ids[a] == 0), last & (ids[a] == grid[a] - 1)

            middle = ids[0] == (2 * grid[0]) // 3
            for a in range(1, len(grid)):
                middle = middle & (ids[a] == 0)

            @pl.when(first)
            def _():
                comm.start(cins, couts, sems)

            @pl.when(middle)
            def _():
                comm.forward(cins, couts, sems)

            body(*ins, *outs, *scr)

            @pl.when(last)
            def _():
                comm.wait(cins, couts, sems)

        in_specs, out_specs, out_shape = in_specs + comm.in_specs, out_specs + comm.out_specs, out_shape + comm.out_shape
        scratch_shapes, args = scratch_shapes + comm.scratch, list(args) + comm.arrs
        sem = ("arbitrary",) * len(grid)
    res = pl.pallas_call(kernel_body, name=name, grid=grid, in_specs=in_specs, out_specs=out_specs, out_shape=out_shape,
                         scratch_shapes=scratch_shapes, compiler_params=_cparams(sem))(*args)
    return res[:n_out], res[n_out:]


def _exchange(arrs, name, scatter=False, ex=None):
    if ex is None:
        ex = _Exchange(arrs, scatter=True) if scatter else _Gather2(arrs)

    def body(*refs):
        _, ins, _, outs, _, sems = _split_comm_refs(refs, 0, 0, 0, ex)
        ex.start(ins, outs, sems)
        ex.forward(ins, outs, sems)
        ex.wait(ins, outs, sems)

    return pl.pallas_call(body, name=name, in_specs=ex.in_specs, out_specs=ex.out_specs, out_shape=ex.out_shape,
                          scratch_shapes=ex.scratch)(*ex.arrs)


def _pad_lanes(v, width=LANE):
    return jnp.pad(v, ((0, 0), (0, width - v.shape[1])))


def _shards_to_cols(g):
    return jnp.transpose(g, (1, 0, 2)).reshape(g.shape[1], N_DEV * g.shape[2])


def _cols_to_shards(w):
    return w.astype(BF16).reshape(w.shape[0], N_DEV, w.shape[1] // N_DEV).transpose(1, 0, 2)


def _local_step(x, tgt, mod, w_in_pt, conv_w, conv_b, dt_bias, a_log, d_skip, ssd_norm_w, q_norm_w, k_norm_w,
                attn_norm_w, w_out_sh, w_ff1_sh, w_ff2_sh, norm1_w, norm2_w, core):
    shift1, scale1, gate1, shift2, scale2, gate2 = [mod[i:i + 1] for i in range(N_MOD)]
    dtb, alog, dsk = _pad_lanes(dt_bias), _pad_lanes(a_log), _pad_lanes(d_skip)
    qw, kw = jnp.tile(q_norm_w, (1, ATT_HEADS)), jnp.tile(k_norm_w, (1, ATT_HEADS))

    h1 = _norm_mod_fwd(x, norm1_w, scale1, shift1, "norm1_fwd")
    proj = _matmul(h1, w_in_pt, tb=True, tm=2048, tn=896, tk=1024, name="in_proj")
    pre, act = _conv_fwd(proj, conv_w, conv_b)
    ypre, ycat_ssd, hall = _ssd_fwd(proj, act, dtb, alog, dsk, ssd_norm_w)
    (o_att, lse), (w_out_g, w_ff1_g, w_ff2_g) = _att_fwd(proj, qw, kw, comm=_Gather2([w_out_sh, w_ff1_sh, w_ff2_sh]))
    w_out = w_out_g.reshape(2 * D_MODEL, D_MODEL)
    w_ff1 = _shards_to_cols(w_ff1_g)
    w_ff2 = w_ff2_g.reshape(D_FF, D_MODEL)
    ycat = _att_norm_fwd(o_att, attn_norm_w, ycat_ssd)
    row32, row16, vec32 = ("row", F32), ("row", BF16), ("vec", F32)
    mix, x1, h2 = _matmul_rows(ycat, w_out, _residual_norm_epilogue, [x], [gate1, norm2_w, scale2, shift2],
                               [row32, row32, row16], tm=512, name="out_proj")
    u, act_ff = _matmul(h2, w_ff1, tm=1024, tn=2048, tk=1024, name="ff1", mode="relu2")
    loss, dout, dff, dgate2 = _matmul_rows(act_ff, w_ff2, _loss_epilogue, [x1, tgt], [gate2],
                                           [("one", F32), row32, row16, vec32], tm=512, name="ff2")

    du = _matmul(dff, w_ff2, tb=True, tm=512, tn=4096, tk=1024, out_dtype=BF16, name="ff2_dx", mode="drelu2", u=u)
    g_ff2 = _matmul(act_ff, dff, ta=True, tm=512, tn=1024, tk=4096, out_dtype=BF16, name="ff2_dw")
    dx1, dshift2, dscale2, g_norm2, dmix, dgate1 = _matmul_rows(
        du, w_ff1, _norm_bwd_epilogue, [x1, dout, mix], [norm2_w, scale2, gate1],
        [row32, vec32, vec32, vec32, row16, vec32], tb=True, tm=512, name="ff1_dx")
    g_ff1 = _matmul(h2, du, ta=True, tm=1024, tn=D_FF // N_DEV, tk=4096, out_dtype=BF16, name="ff1_dw", shard_out=True)

    dy_ssd, do, stats, g_attn_norm = _matmul_rows(
        dmix, w_out, _mixer_split_epilogue, [o_att, lse], [attn_norm_w],
        [("row", F32, SSD_D_INNER), ("row", F32, ATT_D), ("row", F32, ATT_D), ("vec", F32, ATT_D)], tb=True, tm=512, name="out_proj_dx")
    g_out = _matmul(ycat, dmix, ta=True, tm=512, tn=1024, tk=4096, out_dtype=BF16, name="out_proj_dw")
    ff_slabs = [g_ff1, g_ff2.reshape(N_DEV, D_FF // N_DEV, D_MODEL)]
    (dq, dk, dv, dqw, dkw), (s_ff1, s_ff2) = _att_bwd(proj, do, stats, qw, kw, comm=_Exchange(ff_slabs, scatter=True))
    out_slabs = [g_out.astype(BF16).reshape(N_DEV, 2 * D_MODEL // N_DEV, D_MODEL)]
    (dz, dact, ddtr, da, g_dsk, g_dtb, g_ssd_norm), (s_out,) = _ssd_bwd(
        dy_ssd, ypre, proj, act, hall, dtb, alog, dsk, ssd_norm_w, comm=_Exchange(out_slabs, scatter=True))
    dxbc, g_conv_w, g_conv_b = _conv_bwd(dact, pre, proj, conv_w)
    dproj = [(dz, OFF_Z), (dxbc, OFF_XBC), (ddtr, OFF_DT), (dq, OFF_Q), (dk, OFF_K), (dv, OFF_V)]
    g_head, g_tail = _pieces_t_matmul([[dz, dxbc], [dq, dk, dv]], h1, tm=256, name="in_proj_dw")
    g_dt = _matmul(ddtr, h1, ta=True, tm=LANE, tn=1024, tk=4096, out_dtype=BF16, name="in_proj_dw_dt")[:SSD_HEADS]
    in_slabs = jnp.concatenate([g_head, g_dt, g_tail], axis=0).reshape(N_CHIP, 2, IN_W // N_DEV, D_MODEL)
    (sibling_slabs,) = _exchange(None, "swap_w_in_grads", ex=_SiblingSwap([in_slabs]))
    chip_slabs = _chip_sum(lax.dynamic_index_in_dim(in_slabs, core, axis=1, keepdims=False), sibling_slabs)
    (grad_x, dshift1, dscale1, g_norm1), (s_in,) = _matmul_rows(
        dproj, w_in_pt, _norm_bwd_epilogue, [x, dx1], [norm1_w, scale1], [row32, vec32, vec32, vec32],
        tm=256, name="in_proj_dx", comm=_ChipScatter([chip_slabs]))

    dmod = jnp.concatenate([dshift1, dscale1, dgate1, dshift2, dscale2, dgate2], axis=0)
    g_alog = da[:, :SSD_HEADS] * (-jnp.exp(a_log))
    g_qw = dqw.reshape(ATT_HEADS, HEAD_DIM).sum(axis=0, keepdims=True)
    g_kw = dkw.reshape(ATT_HEADS, HEAD_DIM).sum(axis=0, keepdims=True)
    return dict(loss=loss, grad_x=grad_x, dmod=dmod, norm1_w=g_norm1, norm2_w=g_norm2, w_in=s_in, conv_w=g_conv_w,
                conv_b=g_conv_b, dt_bias=g_dtb[:, :SSD_HEADS], a_log=g_alog, d_skip=g_dsk[:, :SSD_HEADS],
                ssd_norm_w=g_ssd_norm, q_norm_w=g_qw, k_norm_w=g_kw, attn_norm_w=g_attn_norm, w_out=s_out,
                w_ff1=s_ff1, w_ff2=s_ff2)


def _pack_w_in_rows(wt_full):
    cut = OFF_DT + SSD_HEADS
    pad = jnp.zeros((LANE - SSD_HEADS, wt_full.shape[1]), wt_full.dtype)
    return jnp.concatenate([wt_full[:cut], pad, wt_full[cut:]], axis=0)


MISC_FIELDS = (("dt_bias", SSD_HEADS), ("a_log", SSD_HEADS), ("d_skip", SSD_HEADS), ("q_norm_w", HEAD_DIM), ("k_norm_w", HEAD_DIM),
               ("loss", 1))
SMALL_LAYOUT = (("b_ada", 6), ("norm1_w", 1), ("norm2_w", 1), ("conv_w", 8), ("conv_b", 2), ("ssd_norm_w", 1),
                ("attn_norm_w", 1), ("misc", 1))


def _pack_small(vals):
    rows = []
    for name, nrow in SMALL_LAYOUT:
        if name == "misc":
            misc = jnp.concatenate([vals[f].reshape(1, n) if f in vals else jnp.zeros((1, n), F32) for f, n in MISC_FIELDS], axis=1)
            rows.append(_pad_lanes(misc, D_MODEL))
        elif name in vals:
            rows.append(vals[name].reshape(nrow, D_MODEL))
        else:
            rows.append(jnp.zeros((nrow, D_MODEL), F32))
    used = sum(n for _, n in SMALL_LAYOUT)
    rows.append(jnp.zeros((SMALL_ROWS - used, D_MODEL), F32))
    return jnp.concatenate(rows, axis=0)


def _unpack_small(packed):
    out, r = {}, 0
    for name, nrow in SMALL_LAYOUT:
        blk = packed[r:r + nrow]
        r += nrow
        if name == "misc":
            c0 = 0
            for f, n in MISC_FIELDS:
                out[f] = blk[:, c0:c0 + n]
                c0 += n
        elif name == "b_ada":
            out[name] = blk.reshape(1, N_MOD * D_MODEL)
        elif name == "conv_w":
            out[name] = blk.reshape(CONV_K, CONV_CH)
        elif name == "conv_b":
            out[name] = blk.reshape(1, CONV_CH)
        else:
            out[name] = blk
    return out


WEIGHT_NAMES = ("norm1_w", "norm2_w", "w_ada", "b_ada", "w_in", "conv_w", "conv_b", "dt_bias", "a_log", "d_skip",
                "ssd_norm_w", "q_norm_w", "k_norm_w", "attn_norm_w", "w_out", "w_ff1", "w_ff2")
SMALL_NAMES = ("norm1_w", "norm2_w", "b_ada", "conv_b", "dt_bias", "a_log", "d_skip", "ssd_norm_w", "q_norm_w",
               "k_norm_w", "attn_norm_w")


def kernel(x, c, norm1_w, norm2_w, w_ada, b_ada, w_in, conv_w, conv_b, dt_bias, a_log, d_skip, ssd_norm_w, q_norm_w, k_norm_w, attn_norm_w, w_out, w_ff1, w_ff2, loss_target, m_norm1_w, m_norm2_w, m_w_ada, m_b_ada, m_w_in, m_conv_w, m_conv_b, m_dt_bias, m_a_log, m_d_skip, m_ssd_norm_w, m_q_norm_w, m_k_norm_w, m_attn_norm_w, m_w_out, m_w_ff1, m_w_ff2, v_norm1_w, v_norm2_w, v_w_ada, v_b_ada, v_w_in, v_conv_w, v_conv_b, v_dt_bias, v_a_log, v_d_skip, v_ssd_norm_w, v_q_norm_w, v_k_norm_w, v_attn_norm_w, v_w_out, v_w_ff1, v_w_ff2):
    args = dict(locals())
    w = {n: args[n] for n in WEIGHT_NAMES}
    m = {n: args["m_" + n] for n in WEIGHT_NAMES}
    v = {n: args["v_" + n] for n in WEIGHT_NAMES}
    me = 4 * lax.axis_index("x") + 2 * lax.axis_index("y") + lax.axis_index("c")

    c_rows = jnp.pad(c, ((0, 7), (0, 0)))
    w_in_t, m_in_t, v_in_t = [jnp.transpose(t["w_in"][0]) for t in (w, m, v)]
    c_g, conv_g, w_in_g = _exchange([c_rows, w["conv_w"][0], w_in_t.astype(BF16)], "gather_w_in", scatter=False)
    c_all = c_g[:, 0, :]
    conv_full = _shards_to_cols(conv_g)
    w_in_pt = _pack_w_in_rows(w_in_g.reshape(IN_W, D_MODEL))

    mod_part = _ada_fwd(c_all, w["w_ada"][0])
    (mod_g,) = _exchange([mod_part], "gather_mod", scatter=False)
    mod_mine = lax.dynamic_index_in_dim(mod_g, me, axis=1, keepdims=False).reshape(1, N_MOD * D_MODEL) + w["b_ada"]
    mod = mod_mine.reshape(N_MOD, D_MODEL)

    res = _local_step(x[0], loss_target[0], mod, w_in_pt, conv_full, w["conv_b"], w["dt_bias"], w["a_log"], w["d_skip"],
                      w["ssd_norm_w"], w["q_norm_w"], w["k_norm_w"], w["attn_norm_w"], w["w_out"][0].astype(BF16),
                      w["w_ff1"][0].astype(BF16), w["w_ff2"][0].astype(BF16), w["norm1_w"], w["norm2_w"], lax.axis_index("c"))

    small_vals = {n: res[n] for n in SMALL_NAMES if n != "b_ada"}
    small_vals["b_ada"] = res["dmod"]
    small_vals["conv_w"] = res["conv_w"]
    small_vals["loss"] = res["loss"]
    (small_g,) = _exchange([_pack_small(small_vals)], "gather_small", scatter=False)

    grads, delta, new_m, new_v = {}, {}, {}, {}
    for name in ("w_out", "w_ff1", "w_ff2"):
        outs = _reduce_adamw(res[name], w[name][0], m[name][0], v[name][0], "adamw_" + name)
        grads[name], delta[name], new_m[name], new_v[name] = [o[None] for o in outs]
    outs = _reduce_adamw(res["w_in"], w_in_t, m_in_t, v_in_t, "adamw_w_in")
    grads["w_in"], delta["w_in"], new_m["w_in"], new_v["w_in"] = [jnp.transpose(o)[None] for o in outs]

    sm = _small_reduce_adamw(small_g, _pack_small({n: w[n] for n in SMALL_NAMES}), _pack_small({n: m[n] for n in SMALL_NAMES}),
                             _pack_small({n: v[n] for n in SMALL_NAMES}))
    sm = [_unpack_small(p) for p in sm]
    for n in SMALL_NAMES:
        grads[n], delta[n], new_m[n], new_v[n] = [p[n] for p in sm]
    shard_w = CONV_CH // N_DEV
    g_conv = lax.dynamic_slice_in_dim(sm[0]["conv_w"], me * shard_w, shard_w, axis=1)
    cw = _adamw_small(g_conv, w["conv_w"][0], m["conv_w"][0], v["conv_w"][0], "adamw_conv_w")
    grads["conv_w"] = g_conv[None]
    delta["conv_w"], new_m["conv_w"], new_v["conv_w"] = [o[None] for o in cw]

    ada_w = w_ada.shape[2]
    dmod_all = small_g[:, :N_MOD, :].reshape(N_DEV, N_MOD * D_MODEL)
    dmod_cols = lax.dynamic_slice_in_dim(dmod_all, me * ada_w, ada_w, axis=1)
    outs = _ada_bwd_adamw(c_all, dmod_cols, w["w_ada"][0], m["w_ada"][0], v["w_ada"][0])
    grads["w_ada"], delta["w_ada"], new_m["w_ada"], new_v["w_ada"] = [o[None] for o in outs]

    loss = sm[0]["loss"][0, 0]
    return (loss, res["grad_x"][None], *[grads[n] for n in WEIGHT_NAMES], *[delta[n] for n in WEIGHT_NAMES],
            *[new_m[n] for n in WEIGHT_NAMES], *[new_v[n] for n in WEIGHT_NAMES])
```

```python
import jax
import jax.numpy as jnp
from jax import lax
from jax.experimental import pallas as pl
from jax.experimental.pallas import tpu as pltpu

F32 = jnp.float32
BF16 = jnp.bfloat16
HIGHEST = lax.Precision.HIGHEST
MESH_IDS = pl.DeviceIdType.MESH

N_DEV = 8
D_MODEL = 1024
HEAD_DIM = 64
SSD_HEADS = 16
SSD_GROUPS = 4
HEADS_PER_GROUP = SSD_HEADS // SSD_GROUPS
SSD_STATE = 128
SSD_CHUNK = 128
SSD_D_INNER = SSD_HEADS * HEAD_DIM
GROUP_WIDTH = SSD_D_INNER // SSD_GROUPS
CONV_K = 4
CONV_CH = SSD_D_INNER + 2 * SSD_GROUPS * SSD_STATE
ATT_HEADS = 16
ATT_D = ATT_HEADS * HEAD_DIM
ATT_BLK = 128
DILATIONS = (1, 4, 16)
D_FF = 4 * D_MODEL
N_MOD = 6
EPS = 1e-6
IN_W = SSD_D_INNER + CONV_CH + SSD_HEADS + 3 * ATT_D
LANE = 128
OFF_Z, OFF_XBC, OFF_DT = 0, SSD_D_INNER, SSD_D_INNER + CONV_CH
OFF_Q = OFF_DT + LANE
OFF_K, OFF_V = OFF_Q + ATT_D, OFF_Q + 2 * ATT_D
IN_WP = OFF_V + ATT_D

ADAM_LR, ADAM_B1, ADAM_B2, ADAM_EPS, ADAM_WD, ADAM_STEP = 0.001, 0.9, 0.999, 1e-08, 0.01, 10
VMEM_LIMIT = 56 * 1024 * 1024
ROW_TILE = 512
SMALL_ROWS = 24


def _cparams(sem=None):
    return pltpu.CompilerParams(dimension_semantics=sem, vmem_limit_bytes=VMEM_LIMIT)


def _sigmoid(v):
    return 1.0 / (1.0 + jnp.exp(-v))


def _softplus(v):
    y = jnp.exp(-jnp.abs(v))
    small = y * (1.0 - y * (0.5 - y * (1.0 / 3.0)))
    return jnp.maximum(v, 0.0) + jnp.where(y < 0.01, small, jnp.log(1.0 + y))


def _dot(a, b, dims, precision=None):
    return lax.dot_general(a, b, (dims, ((), ())), preferred_element_type=F32, precision=precision)


NN = ((1,), (0,))
NT = ((1,), (1,))
TN = ((0,), (0,))


def _matmul(a, b, *, ta=False, tb=False, tm, tn, tk, out_dtype=F32, name, mode=None, u=None, comm=None, shard_out=False):
    m, k = (a.shape[1], a.shape[0]) if ta else a.shape
    n = b.shape[0] if tb else b.shape[1]
    assert m % tm == 0 and n % tn == 0 and k % tk == 0, (name, m, n, k)
    nk = k // tk
    a_spec = pl.BlockSpec((tk, tm), lambda i, j, kk: (kk, i)) if ta else pl.BlockSpec((tm, tk), lambda i, j, kk: (i, kk))
    b_spec = pl.BlockSpec((tn, tk), lambda i, j, kk: (j, kk)) if tb else pl.BlockSpec((tk, tn), lambda i, j, kk: (kk, j))
    o_spec = pl.BlockSpec((tm, tn), lambda i, j, kk: (i, j))
    dims = ((0,) if ta else (1,), (1,) if tb else (0,))
    n_out = 2 if mode == "relu2" else 1

    def body(*refs):
        if mode == "drelu2":
            a_ref, b_ref, u_ref = refs[:3]
            rest = refs[3:]
        else:
            a_ref, b_ref = refs[:2]
            u_ref = None
            rest = refs[2:]
        outs = rest[:n_out]
        part = _dot(a_ref[...], b_ref[...], dims)

        def finish(r):
            if mode == "relu2":
                outs[0][...] = r.astype(BF16)
                rr = jnp.maximum(r, 0.0)
                outs[1][...] = (rr * rr).astype(BF16)
            elif mode == "drelu2":
                outs[0][...] = (r * (2.0 * jnp.maximum(u_ref[...].astype(F32), 0.0))).astype(out_dtype)
            else:
                outs[0][...] = r.astype(out_dtype)

        if nk == 1:
            finish(part)
        else:
            acc = rest[n_out]
            kk = pl.program_id(2)

            @pl.when(kk == 0)
            def _():
                acc[...] = part

            @pl.when(kk > 0)
            def _():
                acc[...] += part

            @pl.when(kk == nk - 1)
            def _():
                finish(acc[...])

    in_specs = [a_spec, b_spec]
    args = [a, b]
    if mode == "drelu2":
        in_specs.append(o_spec)
        args.append(u)
    if mode == "relu2":
        out_shape = [jax.ShapeDtypeStruct((m, n), BF16), jax.ShapeDtypeStruct((m, n), BF16)]
    elif shard_out:
        out_shape = [jax.ShapeDtypeStruct((n // tn, m, tn), out_dtype)]
        o_spec = pl.BlockSpec((None, tm, tn), lambda i, j, kk: (j, i, 0))
    else:
        out_shape = [jax.ShapeDtypeStruct((m, n), out_dtype)]
    outs, comm_outs = _pcall(
        body, args, name=name, grid=(m // tm, n // tn, nk), in_specs=in_specs, out_specs=[o_spec] * n_out,
        out_shape=out_shape, scratch_shapes=[pltpu.VMEM((tm, tn), F32)] if nk > 1 else [],
        sem=("parallel", "parallel", "arbitrary"), comm=comm)
    res = tuple(outs) if mode == "relu2" else outs[0]
    return res if comm is None else (res, comm_outs)


def _pieces_t_matmul(groups, b, *, tm, name):
    k, n = b.shape
    pieces = [p for g in groups for p in g]
    starts, tiles = [], 0
    for p in pieces:
        assert p.shape[0] == k and p.shape[1] % tm == 0, (name, p.shape)
        starts.append(tiles)
        tiles += p.shape[1] // tm
    group_of, group_start, group_tiles = [], [], []
    for gi, g in enumerate(groups):
        group_start.append(starts[len(group_of)])
        group_of += [gi] * len(g)
        group_tiles.append(sum(p.shape[1] // tm for p in g))

    def clipped(block, start, count):
        return pl.BlockSpec(block, (lambda i: (0, jnp.clip(i - start, 0, count - 1))) if block[0] == k
                            else (lambda i: (jnp.clip(i - start, 0, count - 1), 0)))

    def body(*refs):
        a_refs, b_ref, o_refs = refs[:len(pieces)], refs[len(pieces)], refs[len(pieces) + 1:]
        i = pl.program_id(0)
        for a_ref, start, p, gi in zip(a_refs, starts, pieces, group_of):
            @pl.when((i >= start) & (i < start + p.shape[1] // tm))
            def _(a_ref=a_ref, o_ref=o_refs[gi]):
                o_ref[...] = _dot(a_ref[...], b_ref[...], TN).astype(BF16)

    return pl.pallas_call(
        body, name=name, grid=(tiles,),
        in_specs=[clipped((k, tm), s0, p.shape[1] // tm) for s0, p in zip(starts, pieces)] + [pl.BlockSpec((k, n), lambda i: (0, 0))],
        out_specs=[clipped((tm, n), s0, cnt) for s0, cnt in zip(group_start, group_tiles)],
        out_shape=[jax.ShapeDtypeStruct((cnt * tm, n), BF16) for cnt in group_tiles],
        compiler_params=_cparams(("arbitrary",)))(*pieces, b)


def _rms_mod(xv, nw, scale, shift):
    r = lax.rsqrt(jnp.mean(xv * xv, axis=-1, keepdims=True) + EPS)
    return ((xv * r) * nw * (1.0 + scale) + shift).astype(BF16)


def _norm_mod_fwd(x, nw, scale, shift, name):
    s, d = x.shape
    row = pl.BlockSpec((ROW_TILE, d), lambda i: (i, 0))
    vec = pl.BlockSpec((1, d), lambda i: (0, 0))

    def body(x_ref, nw_ref, sc_ref, sh_ref, h_ref):
        h_ref[...] = _rms_mod(x_ref[...], nw_ref[...], sc_ref[...], sh_ref[...])

    return pl.pallas_call(body, name=name, grid=(s // ROW_TILE,), in_specs=[row, vec, vec, vec], out_specs=row,
                          out_shape=jax.ShapeDtypeStruct((s, d), BF16), compiler_params=_cparams(("parallel",)))(x, nw, scale, shift)


def _matmul_rows(a, b, epilogue, row_in, vec_in, outs, *, tb=False, tm, name, comm=None):
    pieces = a if isinstance(a, list) else [(a, 0)]
    assert not (tb and len(pieces) > 1)
    m = pieces[0][0].shape[0]
    n = b.shape[0] if tb else b.shape[1]
    assert m % tm == 0, (name, m, tm)
    dims = ((1,), (1,) if tb else (0,))
    n_a, n_row, n_vec = len(pieces), len(row_in), len(vec_in)

    def body(*refs):
        a_refs, b_ref, rest = refs[:n_a], refs[n_a], refs[n_a + 1:]
        if n_a == 1:
            c = _dot(a_refs[0][...], b_ref[...], dims)
        else:
            c = None
            for a_ref, (piece, off) in zip(a_refs, pieces):
                part = _dot(a_ref[...], b_ref[off:off + piece.shape[1], :], dims)
                c = part if c is None else c + part
        epilogue(c, pl.program_id(0) == 0, rest[:n_row], rest[n_row:n_row + n_vec], rest[n_row + n_vec:])

    def spec(kind, width):
        block = {"row": (tm, width), "vec": (1, width), "one": (1, 1)}[kind]
        return pl.BlockSpec(block, (lambda i: (i, 0)) if kind == "row" else (lambda i: (0, 0)))

    def shape(kind, width):
        return {"row": (m, width), "vec": (1, width), "one": (1, 1)}[kind]

    outs = [(o[0], o[1], o[2] if len(o) > 2 else n) for o in outs]
    res, comm_outs = _pcall(
        body, [*[p for p, _ in pieces], b, *row_in, *vec_in], name=name, grid=(m // tm,),
        in_specs=[spec("row", p.shape[1]) for p, _ in pieces] + [pl.BlockSpec(b.shape, lambda i: (0, 0))]
        + [spec("row", r.shape[1]) for r in row_in] + [spec("vec", v.shape[1]) for v in vec_in],
        out_specs=[spec(kind, width) for kind, _, width in outs],
        out_shape=[jax.ShapeDtypeStruct(shape(kind, width), dt) for kind, dt, width in outs],
        sem=("arbitrary",), comm=comm)
    return res if comm is None else (res, comm_outs)


def _residual_norm_epilogue(mix, first, rows, vecs, outs):
    (x_ref,), (gate_ref, nw_ref, sc_ref, sh_ref), (mix_ref, x1_ref, h_ref) = rows, vecs, outs
    xv = x_ref[...] + gate_ref[...] * mix
    mix_ref[...] = mix
    x1_ref[...] = xv
    h_ref[...] = _rms_mod(xv, nw_ref[...], sc_ref[...], sh_ref[...])


def _loss_epilogue(ff, first, rows, vecs, outs):
    (x1_ref, t_ref), (g_ref,), (loss_ref, dout_ref, dff_ref, dg_ref) = rows, vecs, outs
    d = ff.shape[1]

    @pl.when(first)
    def _():
        loss_ref[...] = jnp.zeros_like(loss_ref)
        dg_ref[...] = jnp.zeros_like(dg_ref)

    err = x1_ref[...] + g_ref[...] * ff - t_ref[...]
    loss_ref[...] += (0.5 / d) * jnp.sum(err * err).reshape(1, 1)
    dout = err * (1.0 / d)
    dout_ref[...] = dout
    dff_ref[...] = (g_ref[...] * dout).astype(BF16)
    dg_ref[...] += jnp.sum(dout * ff, axis=0, keepdims=True)


def _norm_bwd_epilogue(dh, first, rows, vecs, outs):
    with_gate = len(vecs) == 3
    x_ref, dres_ref = rows[:2]
    nw_ref, sc_ref = vecs[:2]
    dx_ref, dsh_ref, dsc_ref, dnw_ref = outs[:4]

    @pl.when(first)
    def _():
        for ref in outs[1:4] + outs[5:]:
            ref[...] = jnp.zeros_like(ref)

    xv = x_ref[...]
    r = lax.rsqrt(jnp.mean(xv * xv, axis=-1, keepdims=True) + EPS)
    nrm = xv * r
    one_sc = 1.0 + sc_ref[...]
    dhn = dh * nrm
    dsh_ref[...] += jnp.sum(dh, axis=0, keepdims=True)
    dsc_ref[...] += jnp.sum(dhn, axis=0, keepdims=True) * nw_ref[...]
    dnw_ref[...] += jnp.sum(dhn, axis=0, keepdims=True) * one_sc
    dn = dh * (nw_ref[...] * one_sc)
    dx = dres_ref[...] + r * (dn - nrm * jnp.mean(dn * nrm, axis=-1, keepdims=True))
    dx_ref[...] = dx
    if with_gate:
        outs[4][...] = (vecs[2][...] * dx).astype(BF16)
        outs[5][...] += jnp.sum(dx * rows[2][...], axis=0, keepdims=True)


CONV_COLS = 256
CONV_FWD_ROWS = 2048
CONV_BWD_ROWS = 1024
CONV_SUB_ROWS = 128
HALO = 8


def _shift_down(cur, halo, k):
    if k == 0:
        return cur
    rolled = pltpu.roll(cur, k, axis=0)
    top = jnp.where(lax.broadcasted_iota(jnp.int32, halo.shape, 0) < k, pltpu.roll(halo, k, axis=0), rolled[:HALO])
    return jnp.concatenate([top, rolled[HALO:]], axis=0)


def _shift_up(cur, halo, k):
    if k == 0:
        return cur
    t = cur.shape[0]
    rolled = pltpu.roll(cur, t - k, axis=0)
    bot = jnp.where(lax.broadcasted_iota(jnp.int32, halo.shape, 0) >= HALO - k, pltpu.roll(halo, HALO - k, axis=0),
                    rolled[t - HALO:])
    return jnp.concatenate([rolled[:t - HALO], bot], axis=0)


def _conv_fwd(proj, conv_w, conv_b):
    s = proj.shape[0]
    nr = s // CONV_FWD_ROWS
    cb0 = OFF_XBC // CONV_COLS
    hb = CONV_FWD_ROWS // HALO
    cur = pl.BlockSpec((CONV_FWD_ROWS, CONV_COLS), lambda j, r: (r, cb0 + j))
    prev = pl.BlockSpec((HALO, CONV_COLS), lambda j, r: (jnp.maximum(r * hb - 1, 0), cb0 + j))
    out = pl.BlockSpec((CONV_FWD_ROWS, CONV_COLS), lambda j, r: (r, j))

    def body(u_ref, up_ref, w_ref, b_ref, pre_ref, act_ref):
        r = pl.program_id(1)
        for c in range(CONV_FWD_ROWS // CONV_SUB_ROWS):
            rows = slice(c * CONV_SUB_ROWS, (c + 1) * CONV_SUB_ROWS)
            u = u_ref[rows, :]
            halo = u_ref[c * CONV_SUB_ROWS - HALO:c * CONV_SUB_ROWS, :] if c > 0 else jnp.where(r > 0, up_ref[...], 0.0)
            acc = b_ref[...] + w_ref[CONV_K - 1:CONV_K, :] * u
            for k in range(1, CONV_K):
                acc = acc + w_ref[CONV_K - 1 - k:CONV_K - k, :] * _shift_down(u, halo, k)
            pre_ref[rows, :] = acc
            act_ref[rows, :] = acc * _sigmoid(acc)

    return pl.pallas_call(
        body, name="conv_fwd", grid=(CONV_CH // CONV_COLS, nr),
        in_specs=[cur, prev, pl.BlockSpec((CONV_K, CONV_COLS), lambda j, r: (0, j)),
                  pl.BlockSpec((1, CONV_COLS), lambda j, r: (0, j))],
        out_specs=[out, out],
        out_shape=[jax.ShapeDtypeStruct((s, CONV_CH), F32), jax.ShapeDtypeStruct((s, CONV_CH), F32)],
        compiler_params=_cparams(("parallel", "arbitrary")))(proj, proj, conv_w, conv_b)


def _conv_bwd(dact, pre, proj, conv_w):
    s = proj.shape[0]
    nr = s // CONV_BWD_ROWS
    cb0 = OFF_XBC // CONV_COLS
    hb = CONV_BWD_ROWS // HALO
    last_halo = s // HALO - 1
    n_sub = CONV_BWD_ROWS // CONV_SUB_ROWS
    cur = pl.BlockSpec((CONV_BWD_ROWS, CONV_COLS), lambda j, r: (r, j))
    nxt = pl.BlockSpec((HALO, CONV_COLS), lambda j, r: (jnp.minimum((r + 1) * hb, last_halo), j))
    ucur = pl.BlockSpec((CONV_BWD_ROWS, CONV_COLS), lambda j, r: (r, cb0 + j))
    wspec = pl.BlockSpec((CONV_K, CONV_COLS), lambda j, r: (0, j))
    bspec = pl.BlockSpec((1, CONV_COLS), lambda j, r: (0, j))

    def dsilu(p):
        sg = _sigmoid(p)
        return sg * (1.0 + p * (1.0 - sg))

    def body(da_ref, dan_ref, pre_ref, pren_ref, u_ref, w_ref, du_ref, dw_ref, db_ref):
        r = pl.program_id(1)

        @pl.when(r == 0)
        def _():
            dw_ref[...] = jnp.zeros_like(dw_ref)
            db_ref[...] = jnp.zeros_like(db_ref)

        dws = [jnp.zeros((1, CONV_COLS), F32) for _ in range(CONV_K)]
        db = jnp.zeros((1, CONV_COLS), F32)
        for c in range(n_sub):
            rows = slice(c * CONV_SUB_ROWS, (c + 1) * CONV_SUB_ROWS)
            ahead = slice((c + 1) * CONV_SUB_ROWS, (c + 1) * CONV_SUB_ROWS + HALO)
            dpre = da_ref[rows, :] * dsilu(pre_ref[rows, :])
            if c < n_sub - 1:
                dnext = da_ref[ahead, :] * dsilu(pre_ref[ahead, :])
            else:
                dnext = jnp.where(r < nr - 1, dan_ref[...] * dsilu(pren_ref[...]), 0.0)
            u = u_ref[rows, :]
            du = w_ref[CONV_K - 1:CONV_K, :] * dpre
            dws[0] = dws[0] + jnp.sum(dpre * u, axis=0, keepdims=True)
            for k in range(1, CONV_K):
                ahead_k = _shift_up(dpre, dnext, k)
                du = du + w_ref[CONV_K - 1 - k:CONV_K - k, :] * ahead_k
                dws[k] = dws[k] + jnp.sum(ahead_k * u, axis=0, keepdims=True)
            du_ref[rows, :] = du.astype(BF16)
            db = db + jnp.sum(dpre, axis=0, keepdims=True)
        dw_ref[...] += jnp.concatenate(dws[::-1], axis=0)
        db_ref[...] += db

    return pl.pallas_call(
        body, name="conv_bwd", grid=(CONV_CH // CONV_COLS, nr),
        in_specs=[cur, nxt, cur, nxt, ucur, wspec],
        out_specs=[cur, wspec, bspec],
        out_shape=[jax.ShapeDtypeStruct((s, CONV_CH), BF16), jax.ShapeDtypeStruct((CONV_K, CONV_CH), F32),
                   jax.ShapeDtypeStruct((1, CONV_CH), F32)],
        compiler_params=_cparams(("parallel", "arbitrary")))(dact, dact, pre, pre, proj, conv_w)


def _ssd_common(dtr, dtb, alog):
    lane = lax.broadcasted_iota(jnp.int32, (1, LANE), 1)
    head_lane = lane < SSD_HEADS
    dt = jnp.where(head_lane, _softplus(dtr + dtb), 0.0)
    a = jnp.where(head_lane, -jnp.exp(alog), 0.0)
    row = lax.broadcasted_iota(jnp.int32, (SSD_CHUNK, SSD_CHUNK), 0)
    col = lax.broadcasted_iota(jnp.int32, (SSD_CHUNK, SSD_CHUNK), 1)
    tril = row >= col
    cs = _dot(tril.astype(F32), dt * a, NN, precision=HIGHEST)
    return dt, a, cs, cs.T, tril, lane


def _split_bf16(v, passes):
    terms, rest = [], v
    for _ in range(passes):
        t = rest.astype(BF16)
        terms.append(t)
        rest = rest - t.astype(F32)
    return terms


def _dot_split(v, m, dims, passes):
    terms = _split_bf16(v, passes)
    if passes == 1:
        return _dot(terms[0], m, dims)
    return _dot(jnp.concatenate(terms, axis=1), jnp.concatenate([m] * passes, axis=0 if dims == NN else 1), dims)


def _ssd_constants():
    heads = jnp.arange(LANE)[:, None]
    exp_mat = (heads == (jnp.arange(SSD_D_INNER)[None, :] // HEAD_DIM)).astype(BF16)
    ind4 = ((jnp.arange(SSD_HEADS * SSD_CHUNK)[:, None] // SSD_CHUNK) == jnp.arange(LANE)[None, :]).astype(BF16)
    return exp_mat, ind4


def _expand_heads(v):
    return jnp.repeat(v[:, :SSD_HEADS], HEAD_DIM, axis=1)


def _ssd_prep(dtr, dtb, alog, exp_mat):
    dt, a, cs, cst, tril, lane = _ssd_common(dtr, dtb, alog)
    return dt, a, cs, cst, tril, lane, _dot_split(dt, exp_mat, NN, 2), _dot_split(cs, exp_mat, NN, 3)


def _chunk_decay_rows(cs, g):
    parts = []
    for e in range(HEADS_PER_GROUP):
        h = g * HEADS_PER_GROUP + e
        parts.append(jnp.broadcast_to(jnp.exp(cs[SSD_CHUNK - 1:SSD_CHUNK, h:h + 1]), (HEAD_DIM, SSD_STATE)))
    return jnp.concatenate(parts, axis=0)


def _ssd_fwd(proj, act, dtb, alog, dsk, nw):
    s = proj.shape[0]
    nc = s // SSD_CHUNK
    bc_w = SSD_GROUPS * SSD_STATE
    exp_mat, _ = _ssd_constants()

    def body(z_ref, dtr_ref, xs_ref, b_ref, c_ref, dtb_ref, alog_ref, dskx_ref, nw_ref, exp_ref,
             ypre_ref, yssd_ref, hall_ref, h_scr):
        @pl.when(pl.program_id(0) == 0)
        def _():
            h_scr[...] = jnp.zeros_like(h_scr)

        dt, a, cs, cst, tril, lane, dtx, csx = _ssd_prep(dtr_ref[...], dtb_ref[...], alog_ref[...], exp_ref[...])
        cs_last_x = csx[SSD_CHUNK - 1:SSD_CHUNK, :]
        xs = xs_ref[...]
        xdt = xs * dtx
        xdtb = xdt.astype(BF16)
        xdec = (xdt * jnp.exp(cs_last_x - csx)).astype(BF16)
        ecsx = jnp.exp(csx)
        head_of_lane = lax.broadcasted_iota(jnp.int32, (1, GROUP_WIDTH), 1) // HEAD_DIM
        for g in range(SSD_GROUPS):
            gs = slice(g * GROUP_WIDTH, (g + 1) * GROUP_WIDTH)
            bg = b_ref[:, g * SSD_STATE:(g + 1) * SSD_STATE].astype(BF16)
            cg = c_ref[:, g * SSD_STATE:(g + 1) * SSD_STATE].astype(BF16)
            cb = _dot(cg, bg, NT)
            hprev = h_scr[gs, :]
            hall_ref[0, gs, :] = hprev
            gms, rhs = [], []
            xg = xdtb[:, gs]
            for e in range(HEADS_PER_GROUP):
                h = g * HEADS_PER_GROUP + e
                lm = jnp.exp(jnp.where(tril, cs[:, h:h + 1] - cst[h:h + 1, :], -1e30))
                gms.append((cb * lm).astype(BF16))
                rhs.append(jnp.where(head_of_lane == e, xg, jnp.zeros_like(xg)))
            y = _dot(jnp.concatenate(gms, axis=1), jnp.concatenate(rhs, axis=0), NN)
            y = y + ecsx[:, gs] * _dot(cg, hprev.astype(BF16), NT)
            y = y + dskx_ref[:, gs] * xs[:, gs]
            h_scr[gs, :] = hprev * _chunk_decay_rows(cs, g) + _dot(xdec[:, gs], bg, TN)
            ypre_ref[:, gs] = y
            z = z_ref[:, gs]
            yg = y * (z * _sigmoid(z))
            r = lax.rsqrt(jnp.mean(yg * yg, axis=-1, keepdims=True) + EPS)
            yssd_ref[:, gs] = (yg * r * nw_ref[:, gs]).astype(BF16)

    row_d = lambda cb: pl.BlockSpec((SSD_CHUNK, SSD_D_INNER), lambda c: (c, cb))
    small = pl.BlockSpec((1, LANE), lambda c: (0, 0))
    wide = pl.BlockSpec((1, SSD_D_INNER), lambda c: (0, 0))
    return pl.pallas_call(
        body, name="ssd_fwd", grid=(nc,),
        in_specs=[row_d(OFF_Z // SSD_D_INNER),
                  pl.BlockSpec((SSD_CHUNK, LANE), lambda c: (c, OFF_DT // LANE)),
                  row_d(0),
                  pl.BlockSpec((SSD_CHUNK, bc_w), lambda c: (c, SSD_D_INNER // bc_w)),
                  pl.BlockSpec((SSD_CHUNK, bc_w), lambda c: (c, SSD_D_INNER // bc_w + 1)),
                  small, small, wide, wide, pl.BlockSpec((LANE, SSD_D_INNER), lambda c: (0, 0))],
        out_specs=[row_d(0), row_d(0), pl.BlockSpec((1, SSD_D_INNER, SSD_STATE), lambda c: (c, 0, 0))],
        out_shape=[jax.ShapeDtypeStruct((s, SSD_D_INNER), F32), jax.ShapeDtypeStruct((s, SSD_D_INNER + ATT_D), BF16),
                   jax.ShapeDtypeStruct((nc, SSD_D_INNER, SSD_STATE), F32)],
        scratch_shapes=[pltpu.VMEM((SSD_D_INNER, SSD_STATE), F32)],
        compiler_params=_cparams(("arbitrary",)))(proj, proj, act, act, act, dtb, alog, _expand_heads(dsk), nw, exp_mat)


def _ssd_bwd(dycat, ypre, proj, act, hall, dtb, alog, dsk, nw, comm=None):
    s = proj.shape[0]
    nc = s // SSD_CHUNK
    bc_w = SSD_GROUPS * SSD_STATE

    exp_mat, ind4 = _ssd_constants()
    seg_passes = 1

    def body(dy_ref, ypre_ref, z_ref, dtr_ref, xs_ref, b_ref, c_ref, hall_ref, dtb_ref, alog_ref, dskx_ref, nw_ref,
             exp_ref, ind4_ref, dz_ref, dact_ref, ddtr_ref, da_ref, ddsk_ref, ddtb_ref, dnw_ref, dh_scr):
        @pl.when(pl.program_id(0) == 0)
        def _():
            dh_scr[...] = jnp.zeros_like(dh_scr)
            da_ref[...] = jnp.zeros_like(da_ref)
            ddsk_ref[...] = jnp.zeros_like(ddsk_ref)
            ddtb_ref[...] = jnp.zeros_like(ddtb_ref)
            dnw_ref[...] = jnp.zeros_like(dnw_ref)

        dtr = dtr_ref[...]
        dt, a, cs, cst, tril, lane, dtx, csx = _ssd_prep(dtr, dtb_ref[...], alog_ref[...], exp_ref[...])
        cs_last_x = csx[SSD_CHUNK - 1:SSD_CHUNK, :]
        xs = xs_ref[...]
        xdt = xs * dtx
        xdtb = xdt.astype(BF16)
        decx = jnp.exp(cs_last_x - csx)
        xdecf = xdt * decx
        xdec = xdecf.astype(BF16)
        ecsx = jnp.exp(csx)
        head_of_lane = lax.broadcasted_iota(jnp.int32, (1, GROUP_WIDTH), 1) // HEAD_DIM
        last_row = lax.broadcasted_iota(jnp.int32, (SSD_CHUNK, 1), 0) == SSD_CHUNK - 1
        dcs_col = jnp.zeros((SSD_CHUNK, LANE), F32)
        dcs_row = jnp.zeros((SSD_CHUNK, LANE), F32)
        ddt = jnp.zeros((SSD_CHUNK, LANE), F32)
        ddsk = jnp.zeros((1, LANE), F32)
        hsum = jnp.zeros((1, LANE), F32)
        t1_sum = jnp.zeros((1, LANE), F32)
        for g in range(SSD_GROUPS):
            gs = slice(g * GROUP_WIDTH, (g + 1) * GROUP_WIDTH)
            bsl = slice(g * SSD_STATE, (g + 1) * SSD_STATE)
            exp_g = exp_ref[:, gs]
            ind4_g = ind4_ref[g * HEADS_PER_GROUP * SSD_CHUNK:(g + 1) * HEADS_PER_GROUP * SSD_CHUNK, :]
            z = z_ref[:, gs]
            sg = _sigmoid(z)
            sz = z * sg
            ypre = ypre_ref[:, gs]
            yg = ypre * sz
            r = lax.rsqrt(jnp.mean(yg * yg, axis=-1, keepdims=True) + EPS)
            nrm = yg * r
            dyo_n = dy_ref[:, gs]
            dnw_ref[:, gs] += jnp.sum(dyo_n * nrm, axis=0, keepdims=True)
            dn = dyo_n * nw_ref[:, gs]
            dyg = r * (dn - nrm * jnp.mean(dn * nrm, axis=-1, keepdims=True))
            dz_ref[:, gs] = (dyg * ypre * (sg * (1.0 + z * (1.0 - sg)))).astype(BF16)
            dy = dyg * sz

            bg = b_ref[:, bsl].astype(BF16)
            cg = c_ref[:, bsl].astype(BF16)
            cb = _dot(cg, bg, NT)
            hprev = hall_ref[0, gs, :]
            hb = hprev.astype(BF16)
            dhn = dh_scr[gs, :]
            dhb = dhn.astype(BF16)
            xs_g, xdt_g = xs[:, gs], xdtb[:, gs]
            w_off = _dot(cg, hb, NT)
            dyo = dy * ecsx[:, gs]
            dyob = dyo.astype(BF16)
            dcg = _dot(dyob, hb, NN)
            dh_y = _dot(dyob, cg, TN)
            r_st = _dot(bg, dhb, NT)
            dbg = _dot(xdec[:, gs], dhb, NN)
            dyb = dy.astype(BF16)
            gms, gmbs, lms, dys = [], [], [], []
            for e in range(HEADS_PER_GROUP):
                h = g * HEADS_PER_GROUP + e
                lm = jnp.exp(jnp.where(tril, cs[:, h:h + 1] - cst[h:h + 1, :], -1e30))
                gm = cb * lm
                lms.append(lm)
                gms.append(gm)
                gmbs.append(gm.astype(BF16))
                dys.append(jnp.where(head_of_lane == e, dyb, jnp.zeros_like(dyb)))
            dxdt = _dot(jnp.concatenate(gmbs, axis=0), jnp.concatenate(dys, axis=0), TN) + decx[:, gs] * r_st
            dcb = jnp.zeros((SSD_CHUNK, SSD_CHUNK), F32)
            mms = []
            for e in range(HEADS_PER_GROUP):
                dg = _dot(dys[e], xdt_g, NT)
                mms.append(dg * gms[e])
                dcb = dcb + dg * lms[e]
            seg = _dot_split(jnp.concatenate([dyo * w_off, xdecf[:, gs] * r_st, dxdt * xs_g, dy * xs_g], axis=0), exp_g, NT, seg_passes)
            v1, t1, ddt_g, dsk_g = [seg[i * SSD_CHUNK:(i + 1) * SSD_CHUNK] for i in range(4)]
            dcs_col = dcs_col + v1 - t1 + _dot_split(jnp.concatenate(mms, axis=1), ind4_g, NN, seg_passes)
            for t in _split_bf16(jnp.concatenate(mms, axis=0), seg_passes):
                dcs_row = dcs_row + _dot(ind4_g, t, TN)
            ddt = ddt + ddt_g
            ddsk = ddsk + jnp.sum(dsk_g, axis=0, keepdims=True)
            t1_sum = t1_sum + jnp.sum(t1, axis=0, keepdims=True)
            for e in range(HEADS_PER_GROUP):
                h = g * HEADS_PER_GROUP + e
                hs = slice(e * HEAD_DIM, (e + 1) * HEAD_DIM)
                hsum = hsum + jnp.where(lane == h, jnp.sum(dhn[hs, :] * hprev[hs, :]).reshape(1, 1), 0.0)
            dh_scr[gs, :] = dhn * _chunk_decay_rows(cs, g) + dh_y
            dcbb = dcb.astype(BF16)
            dact_ref[:, gs] = dxdt * dtx[:, gs] + dskx_ref[:, gs] * dy
            dact_ref[:, SSD_D_INNER + g * SSD_STATE:SSD_D_INNER + (g + 1) * SSD_STATE] = dbg + _dot(dcbb, cg, TN)
            dact_ref[:, SSD_D_INNER + bc_w + g * SSD_STATE:SSD_D_INNER + bc_w + (g + 1) * SSD_STATE] = dcg + _dot(dcbb, bg, NN)
        dlast = t1_sum + jnp.exp(cs[SSD_CHUNK - 1:SSD_CHUNK, :]) * hsum
        dcs = dcs_col - dcs_row.T + jnp.where(last_row, dlast, 0.0)
        row = lax.broadcasted_iota(jnp.int32, (SSD_CHUNK, SSD_CHUNK), 0)
        col = lax.broadcasted_iota(jnp.int32, (SSD_CHUNK, SSD_CHUNK), 1)
        dda = _dot((col >= row).astype(F32), dcs, NN, precision=HIGHEST)
        ddt = ddt + dda * a
        da_ref[...] += jnp.sum(dda * dt, axis=0, keepdims=True)
        ddtr = jnp.where(lane < SSD_HEADS, ddt * _sigmoid(dtr + dtb_ref[...]), 0.0)
        ddtr_ref[...] = ddtr.astype(BF16)
        ddtb_ref[...] += jnp.sum(ddtr, axis=0, keepdims=True)
        ddsk_ref[...] += ddsk

    rev = lambda c: nc - 1 - c
    row_d = lambda cb: pl.BlockSpec((SSD_CHUNK, SSD_D_INNER), lambda c: (rev(c), cb))
    small = pl.BlockSpec((1, LANE), lambda c: (0, 0))
    wide = pl.BlockSpec((1, SSD_D_INNER), lambda c: (0, 0))
    small_shape = jax.ShapeDtypeStruct((1, LANE), F32)
    return _pcall(
        body, (dycat, ypre, proj, proj, act, act, act, hall, dtb, alog, _expand_heads(dsk), nw, exp_mat, ind4),
        name="ssd_bwd", grid=(nc,),
        in_specs=[row_d(0), row_d(0), row_d(OFF_Z // SSD_D_INNER),
                  pl.BlockSpec((SSD_CHUNK, LANE), lambda c: (rev(c), OFF_DT // LANE)),
                  row_d(0),
                  pl.BlockSpec((SSD_CHUNK, bc_w), lambda c: (rev(c), SSD_D_INNER // bc_w)),
                  pl.BlockSpec((SSD_CHUNK, bc_w), lambda c: (rev(c), SSD_D_INNER // bc_w + 1)),
                  pl.BlockSpec((1, SSD_D_INNER, SSD_STATE), lambda c: (rev(c), 0, 0)),
                  small, small, wide, wide, pl.BlockSpec((LANE, SSD_D_INNER), lambda c: (0, 0)),
                  pl.BlockSpec((SSD_HEADS * SSD_CHUNK, LANE), lambda c: (0, 0))],
        out_specs=[row_d(0), pl.BlockSpec((SSD_CHUNK, CONV_CH), lambda c: (rev(c), 0)),
                   pl.BlockSpec((SSD_CHUNK, LANE), lambda c: (rev(c), 0)), small, small, small, wide],
        out_shape=[jax.ShapeDtypeStruct((s, SSD_D_INNER), BF16), jax.ShapeDtypeStruct((s, CONV_CH), F32),
                   jax.ShapeDtypeStruct((s, LANE), BF16), small_shape, small_shape, small_shape,
                   jax.ShapeDtypeStruct((1, SSD_D_INNER), F32)],
        scratch_shapes=[pltpu.VMEM((SSD_D_INNER, SSD_STATE), F32)], sem=("arbitrary",), comm=comm)


def _head_mean_matrix():
    row = lax.broadcasted_iota(jnp.int32, (LANE, LANE), 0) // HEAD_DIM
    col = lax.broadcasted_iota(jnp.int32, (LANE, LANE), 1) // HEAD_DIM
    return (row == col).astype(F32)


def _head_sum2(v, ones_bd):
    hi = v.astype(BF16)
    lo = (v - hi.astype(F32)).astype(BF16)
    return _dot(jnp.concatenate([hi, lo], axis=1), jnp.concatenate([ones_bd, ones_bd], axis=0), NN)


def _head_norms(xs, ws, ones_bd):
    sums = [_head_sum2(x * x, ones_bd) for x in xs]
    rs = [lax.rsqrt(ms * (1.0 / HEAD_DIM) + EPS) for ms in sums]
    return [(x * r) * w for x, r, w in zip(xs, rs, ws)], rs


def _head_norms_bwd(dns, xs, ws, rs, ones_bd):
    nrms = [x * r for x, r in zip(xs, rs)]
    dnws = [dn * w for dn, w in zip(dns, ws)]
    projs = [_head_sum2(dnw * nrm, ones_bd) for dnw, nrm in zip(dnws, nrms)]
    dxs = [r * (dnw - nrm * (pr * (1.0 / HEAD_DIM))) for r, dnw, nrm, pr in zip(rs, dnws, nrms, projs)]
    return dxs, [jnp.sum(dn * nrm, axis=0, keepdims=True) for dn, nrm in zip(dns, nrms)]


NORM_CHUNKS = 2


PRO_ROWS = 256
ATT_GROUP_FWD = 16
ATT_GROUP_BWD = 8
KEYS = 2 * ATT_BLK
NEG = -1e30
HALF = HEAD_DIM // 2


def _rows(start, size, dil):
    return pl.ds(start, size) if dil == 1 else pl.ds(start, size, stride=dil)


def _fill_bias(bias_ref):
    row = lax.broadcasted_iota(jnp.int32, (ATT_BLK, 2 * KEYS), 0)
    col = lax.broadcasted_iota(jnp.int32, (ATT_BLK, 2 * KEYS), 1) & (KEYS - 1)
    for first, off in ((0, 0), (1, ATT_BLK)):
        dist = off + row - col
        bias_ref[first] = jnp.where((dist >= 0) & (dist <= ATT_BLK), 0.0, NEG)


def _pair(a, b):
    return jnp.concatenate([jnp.broadcast_to(a, (ATT_BLK, KEYS)), jnp.broadcast_to(b, (ATT_BLK, KEYS))], axis=1)


def _split_heads(x, is_a):
    zero = jnp.zeros_like(x)
    return jnp.concatenate([jnp.where(is_a, x, zero), jnp.where(is_a, zero, x)], axis=0)


def _block_ids(b, nb):
    i = b & (nb - 1)
    q0 = pl.multiple_of(b * ATT_BLK, ATT_BLK)
    k0 = pl.multiple_of((b - jnp.minimum(i, 1)) * ATT_BLK, ATT_BLK)
    return pl.ds(q0, ATT_BLK), pl.ds(k0, KEYS), jnp.minimum(i, 1)


def _natural_rows(b, nb, dil):
    if dil == 1:
        return pl.ds(pl.multiple_of(b * ATT_BLK, ATT_BLK), ATT_BLK)
    return pl.ds(b // nb + dil * ((b & (nb - 1)) * ATT_BLK), ATT_BLK, stride=dil)


def _att_fwd(proj, qw, kw, comm=None):
    s = proj.shape[0]
    nblk = s // ATT_BLK
    assert all((s // d) // ATT_BLK >= 2 for d in DILATIONS)
    blk = lambda off: pl.BlockSpec((s, LANE), lambda i: (0, off // LANE + i))
    wspec = pl.BlockSpec((1, LANE), lambda i: (0, i))
    oblk = pl.BlockSpec((s, LANE), lambda i: (0, i))

    def body(q_ref, k_ref, v_ref, qw_ref, kw_ref, o_ref, lse_ref, qn, kn, q_cm, k_cm, v_cm, m_acc, l_acc, o_d, m_d, l_d, bias):
        ones_bd = _head_mean_matrix().astype(BF16)
        is_a = lax.broadcasted_iota(jnp.int32, (1, LANE), 1) < HEAD_DIM
        ones_ext = _split_heads(jnp.ones((KEYS, LANE), BF16), is_a)
        _fill_bias(bias)

        def pro(j, c):
            chunks = [pl.ds(pl.multiple_of((NORM_CHUNKS * j + u) * PRO_ROWS, PRO_ROWS), PRO_ROWS) for u in range(NORM_CHUNKS)]
            normed, _ = _head_norms([q_ref[rows, :] for rows in chunks] + [k_ref[rows, :] for rows in chunks],
                                    [qw_ref[...] * HEAD_DIM ** -0.5] * NORM_CHUNKS + [kw_ref[...]] * NORM_CHUNKS, ones_bd)
            for u, rows in enumerate(chunks):
                qn[rows, :] = normed[u]
                kn[rows, :] = normed[NORM_CHUNKS + u]
            return c

        lax.fori_loop(0, s // (NORM_CHUNKS * PRO_ROWS), pro, 0)

        for dil in DILATIONS:
            ln = s // dil
            nb = ln // ATT_BLK
            o_out, m_out, l_out = (o_ref, m_acc, l_acc) if dil == 1 else (o_d, m_d, l_d)
            for r in range(dil):
                def relayout(j, c, dil=dil, r=r, ln=ln):
                    j0 = pl.multiple_of(j * PRO_ROWS, PRO_ROWS)
                    src = _rows(r + dil * j0, PRO_ROWS, dil)
                    dst = pl.ds(r * ln + j0, PRO_ROWS)
                    q_cm[dst, :] = qn[src, :].astype(BF16)
                    k_cm[dst, :] = kn[src, :].astype(BF16)
                    v_cm[dst, :] = v_ref[src, :].astype(BF16)
                    return c

                lax.fori_loop(0, ln // PRO_ROWS, relayout, 0)

            def step(bg, c, nb=nb, o_out=o_out, m_out=m_out, l_out=l_out):
                ids = [_block_ids(bg * ATT_GROUP_FWD + u, nb) for u in range(ATT_GROUP_FWD)]
                kbs = [_split_heads(k_cm[krows, :], is_a) for _, krows, _ in ids]
                scs = [_dot(q_cm[qrows, :], kb, NT) + bias[first] for (qrows, _, first), kb in zip(ids, kbs)]
                mas = [jnp.max(sc[:, :KEYS], axis=-1, keepdims=True) for sc in scs]
                mbs = [jnp.max(sc[:, KEYS:], axis=-1, keepdims=True) for sc in scs]
                ps = [jnp.exp(sc - _pair(ma, mb)).astype(BF16) for sc, ma, mb in zip(scs, mas, mbs)]
                vbs = [jnp.concatenate([_split_heads(v_cm[krows, :], is_a), ones_ext], axis=1) for _, krows, _ in ids]
                ols = [_dot(p, vb, NN) for p, vb in zip(ps, vbs)]
                for (qrows, _, _), ol, ma, mb in zip(ids, ols, mas, mbs):
                    o_out[qrows, :] = ol[:, :LANE]
                    l_out[qrows, :] = ol[:, LANE:]
                    m_out[qrows, :] = jnp.where(is_a, ma, mb)
                return c

            lax.fori_loop(0, nblk // ATT_GROUP_FWD, step, 0)

            if dil > 1:
                for r in range(dil):
                    def merge(j, c, dil=dil, r=r, ln=ln):
                        j0 = pl.multiple_of(j * PRO_ROWS, PRO_ROWS)
                        nat = _rows(r + dil * j0, PRO_ROWS, dil)
                        cm = pl.ds(r * ln + j0, PRO_ROWS)
                        m_old, m_new = m_acc[nat, :], m_d[cm, :]
                        m = jnp.maximum(m_old, m_new)
                        a_old, a_new = jnp.exp(m_old - m), jnp.exp(m_new - m)
                        o_ref[nat, :] = a_old * o_ref[nat, :] + a_new * o_d[cm, :]
                        l_acc[nat, :] = a_old * l_acc[nat, :] + a_new * l_d[cm, :]
                        m_acc[nat, :] = m
                        return c

                    lax.fori_loop(0, ln // PRO_ROWS, merge, 0)

        def epi(j, c):
            rows = pl.ds(pl.multiple_of(j * PRO_ROWS, PRO_ROWS), PRO_ROWS)
            l = l_acc[rows, :]
            o_ref[rows, :] = o_ref[rows, :] / l
            lse_ref[rows, :] = m_acc[rows, :] + jnp.log(l)
            return c

        lax.fori_loop(0, s // PRO_ROWS, epi, 0)

    f = jax.ShapeDtypeStruct((s, ATT_D), F32)
    scr = pltpu.VMEM((s, LANE), F32)
    scb = pltpu.VMEM((s, LANE), BF16)
    return _pcall(
        body, (proj, proj, proj, qw, kw), name="att_fwd", grid=(ATT_D // LANE,),
        in_specs=[blk(OFF_Q), blk(OFF_K), blk(OFF_V), wspec, wspec], out_specs=[oblk, oblk], out_shape=[f, f],
        scratch_shapes=[scr, scr, scb, scb, scb, scr, scr, scr, scr, scr, pltpu.VMEM((2, ATT_BLK, 2 * KEYS), F32)],
        sem=("parallel",), comm=comm)


def _att_bwd(proj, do, stats, qw, kw, comm=None):
    s = proj.shape[0]
    nblk = s // ATT_BLK
    blk = lambda off: pl.BlockSpec((s, LANE), lambda i: (0, off // LANE + i))
    wspec = pl.BlockSpec((1, LANE), lambda i: (0, i))
    oblk = pl.BlockSpec((s, LANE), lambda i: (0, i))

    def body(q_ref, k_ref, v_ref, do_ref, st_ref, qw_ref, kw_ref, dq_ref, dk_ref, dv_ref, dqw_ref, dkw_ref,
             qn, kn, q_cm, do_cm, k_cm, v_cm, rms, dq_acc, dk_acc, dv_acc, dq_d, dk_d, dv_d, bias):
        ones_bd = _head_mean_matrix().astype(BF16)
        is_a = lax.broadcasted_iota(jnp.int32, (1, LANE), 1) < HEAD_DIM
        first_half = (lax.broadcasted_iota(jnp.int32, (1, LANE), 1) & (HEAD_DIM - 1)) < HALF
        _fill_bias(bias)
        zero = jnp.zeros((PRO_ROWS, LANE), F32)

        def pro(j, c):
            chunks = [pl.ds(pl.multiple_of((NORM_CHUNKS * j + u) * PRO_ROWS, PRO_ROWS), PRO_ROWS) for u in range(NORM_CHUNKS)]
            normed, rs = _head_norms([q_ref[rows, :] for rows in chunks] + [k_ref[rows, :] for rows in chunks],
                                     [qw_ref[...] * HEAD_DIM ** -0.5] * NORM_CHUNKS + [kw_ref[...]] * NORM_CHUNKS, ones_bd)
            for u, rows in enumerate(chunks):
                qn[rows, :] = normed[u]
                kn[rows, :] = normed[NORM_CHUNKS + u]
                rms[rows, :] = jnp.where(first_half, rs[u], rs[NORM_CHUNKS + u])
                dk_acc[rows, :] = zero
                dv_acc[rows, :] = zero
            return c

        lax.fori_loop(0, s // (NORM_CHUNKS * PRO_ROWS), pro, 0)

        for dil in DILATIONS:
            ln = s // dil
            nb = ln // ATT_BLK
            dq_o, dk_o, dv_o = (dq_acc, dk_acc, dv_acc) if dil == 1 else (dq_d, dk_d, dv_d)
            for r in range(dil):
                def relayout(j, c, dil=dil, r=r, ln=ln):
                    j0 = pl.multiple_of(j * PRO_ROWS, PRO_ROWS)
                    src = _rows(r + dil * j0, PRO_ROWS, dil)
                    dst = pl.ds(r * ln + j0, PRO_ROWS)
                    q_cm[dst, :] = qn[src, :].astype(BF16)
                    k_cm[dst, :] = kn[src, :].astype(BF16)
                    v_cm[dst, :] = v_ref[src, :].astype(BF16)
                    do_cm[dst, :] = do_ref[src, :].astype(BF16)
                    if dil > 1:
                        dk_d[dst, :] = zero
                        dv_d[dst, :] = zero
                    return c

                lax.fori_loop(0, ln // PRO_ROWS, relayout, 0)

            def step(bg, c, nb=nb, dil=dil, dq_o=dq_o, dk_o=dk_o, dv_o=dv_o):
                blocks = [bg * ATT_GROUP_BWD + u for u in range(ATT_GROUP_BWD)]
                ids = [_block_ids(b, nb) for b in blocks]
                qbs = [q_cm[qrows, :] for qrows, _, _ in ids]
                dobs = [do_cm[qrows, :] for qrows, _, _ in ids]
                kbs = [_split_heads(k_cm[krows, :], is_a) for _, krows, _ in ids]
                vbs = [_split_heads(v_cm[krows, :], is_a) for _, krows, _ in ids]
                sts = [st_ref[_natural_rows(b, nb, dil), :] for b in blocks]
                scs = [_dot(qb, kb, NT) + bias[first] for qb, kb, (_, _, first) in zip(qbs, kbs, ids)]
                dps = [_dot(dob, vb, NT) for dob, vb in zip(dobs, vbs)]
                ps = [jnp.exp(sc - _pair(st[:, 0:1], st[:, HEAD_DIM:HEAD_DIM + 1])) for sc, st in zip(scs, sts)]
                dss = [(p * (dp - _pair(st[:, HALF:HALF + 1], st[:, HEAD_DIM + HALF:HEAD_DIM + HALF + 1]))).astype(BF16)
                       for p, dp, st in zip(ps, dps, sts)]
                dqs = [_dot(ds, kb, NN) for ds, kb in zip(dss, kbs)]
                dkfs = [_dot(ds, qb, TN) for ds, qb in zip(dss, qbs)]
                dvfs = [_dot(p.astype(BF16), dob, TN) for p, dob in zip(ps, dobs)]
                for (qrows, krows, _), dq, dkf, dvf in zip(ids, dqs, dkfs, dvfs):
                    dq_o[qrows, :] = dq
                    dk_o[krows, :] += jnp.where(is_a, dkf[:KEYS], dkf[KEYS:])
                    dv_o[krows, :] += jnp.where(is_a, dvf[:KEYS], dvf[KEYS:])
                return c

            lax.fori_loop(0, nblk // ATT_GROUP_BWD, step, 0)

            if dil > 1:
                for r in range(dil):
                    def merge(j, c, dil=dil, r=r, ln=ln):
                        j0 = pl.multiple_of(j * PRO_ROWS, PRO_ROWS)
                        nat = _rows(r + dil * j0, PRO_ROWS, dil)
                        cm = pl.ds(r * ln + j0, PRO_ROWS)
                        dq_acc[nat, :] += dq_d[cm, :]
                        dk_acc[nat, :] += dk_d[cm, :]
                        dv_acc[nat, :] += dv_d[cm, :]
                        return c

                    lax.fori_loop(0, ln // PRO_ROWS, merge, 0)

        def epi(j, c):
            chunks = [pl.ds(pl.multiple_of((NORM_CHUNKS * j + u) * PRO_ROWS, PRO_ROWS), PRO_ROWS) for u in range(NORM_CHUNKS)]
            packed = [rms[rows, :] for rows in chunks]
            rs = ([jnp.where(first_half, p, pltpu.roll(p, HALF, axis=1)) for p in packed]
                  + [jnp.where(first_half, pltpu.roll(p, LANE - HALF, axis=1), p) for p in packed])
            dxs, dws = _head_norms_bwd(
                [dq_acc[rows, :] for rows in chunks] + [dk_acc[rows, :] for rows in chunks],
                [q_ref[rows, :] for rows in chunks] + [k_ref[rows, :] for rows in chunks],
                [qw_ref[...] * HEAD_DIM ** -0.5] * NORM_CHUNKS + [kw_ref[...]] * NORM_CHUNKS, rs, ones_bd)
            dqw, dkw = c
            for u, rows in enumerate(chunks):
                dq_ref[rows, :] = dxs[u].astype(BF16)
                dk_ref[rows, :] = dxs[NORM_CHUNKS + u].astype(BF16)
                dv_ref[rows, :] = dv_acc[rows, :].astype(BF16)
                dqw, dkw = dqw + dws[u], dkw + dws[NORM_CHUNKS + u]
            return dqw, dkw

        zrow = jnp.zeros((1, LANE), F32)
        dqw, dkw = lax.fori_loop(0, s // (NORM_CHUNKS * PRO_ROWS), epi, (zrow, zrow))
        dqw_ref[...] = dqw * HEAD_DIM ** -0.5
        dkw_ref[...] = dkw

    o = jax.ShapeDtypeStruct((s, ATT_D), BF16)
    ov = jax.ShapeDtypeStruct((1, ATT_D), F32)
    scr = pltpu.VMEM((s, LANE), F32)
    scb = pltpu.VMEM((s, LANE), BF16)
    return _pcall(
        body, (proj, proj, proj, do, stats, qw, kw), name="att_bwd", grid=(ATT_D // LANE,),
        in_specs=[blk(OFF_Q), blk(OFF_K), blk(OFF_V), oblk, oblk, wspec, wspec],
        out_specs=[oblk, oblk, oblk, wspec, wspec], out_shape=[o, o, o, ov, ov],
        scratch_shapes=[scr, scr, scb, scb, scb, scb, scr, scr, scr, scr, scr, scr, scr, pltpu.VMEM((2, ATT_BLK, 2 * KEYS), F32)],
        sem=("parallel",), comm=comm)


def _att_norm_fwd(o, nw, ycat):
    s = o.shape[0]
    row = pl.BlockSpec((ROW_TILE, ATT_D), lambda i: (i, 0))
    vec = pl.BlockSpec((1, ATT_D), lambda i: (0, 0))

    def body(o_ref, nw_ref, ycat_ref, y_ref):
        o = o_ref[...]
        r = lax.rsqrt(jnp.mean(o * o, axis=-1, keepdims=True) + EPS)
        y_ref[...] = (o * r * nw_ref[...]).astype(BF16)

    return pl.pallas_call(body, name="att_norm_fwd", grid=(s // ROW_TILE,),
                          in_specs=[row, vec, pl.BlockSpec(memory_space=pl.ANY)],
                          out_specs=pl.BlockSpec((ROW_TILE, ATT_D), lambda i: (i, 1)),
                          out_shape=jax.ShapeDtypeStruct(ycat.shape, BF16), input_output_aliases={2: 0},
                          compiler_params=_cparams(("parallel",)))(o, nw, ycat)


def _mixer_split_epilogue(dycat, first, rows, vecs, outs):
    (o_ref, lse_ref), (nw_ref,), (dyssd_ref, do_ref, st_ref, dnw_ref) = rows, vecs, outs

    @pl.when(first)
    def _():
        dnw_ref[...] = jnp.zeros_like(dnw_ref)

    dyssd_ref[...] = dycat[:, :SSD_D_INNER]
    dy = dycat[:, SSD_D_INNER:]
    o = o_ref[...]
    r = lax.rsqrt(jnp.mean(o * o, axis=-1, keepdims=True) + EPS)
    nrm = o * r
    dnw_ref[...] += jnp.sum(dy * nrm, axis=0, keepdims=True)
    dn = dy * nw_ref[...]
    do = r * (dn - nrm * jnp.mean(dn * nrm, axis=-1, keepdims=True))
    do_ref[...] = do
    ones_bd = _head_mean_matrix().astype(BF16)
    prod = do * o
    delta = jnp.concatenate([_head_sum2(prod[:, j * LANE:(j + 1) * LANE], ones_bd) for j in range(ATT_D // LANE)], axis=1)
    lane = lax.broadcasted_iota(jnp.int32, (1, ATT_D), 1)
    st_ref[...] = jnp.where((lane & (HEAD_DIM - 1)) < HALF, lse_ref[...], delta)


def _ada_fwd(c_all, w_ada):
    def body(c_ref, w_ref, o_ref):
        cv = c_ref[...]
        o_ref[...] = _dot((cv * _sigmoid(cv)).astype(BF16), w_ref[...].astype(BF16), NN)

    return pl.pallas_call(body, name="ada_fwd", out_shape=jax.ShapeDtypeStruct((c_all.shape[0], w_ada.shape[1]), F32),
                          compiler_params=_cparams())(c_all, w_ada)


def _adamw_math(g, w, m, v):
    m_new = ADAM_B1 * m + (1.0 - ADAM_B1) * g
    v_new = ADAM_B2 * v + (1.0 - ADAM_B2) * (g * g)
    m_hat = m_new / (1.0 - ADAM_B1 ** ADAM_STEP)
    v_hat = v_new / (1.0 - ADAM_B2 ** ADAM_STEP)
    delta = -ADAM_LR * (m_hat / (jnp.sqrt(v_hat) + ADAM_EPS) + ADAM_WD * w)
    return delta, m_new, v_new


def _ada_bwd_adamw(c_all, dmod_cols, w, m, v):
    rows, cols = w.shape
    tr = 256
    blk = pl.BlockSpec((tr, cols), lambda i: (i, 0))

    def body(c_ref, d_ref, w_ref, m_ref, v_ref, g_ref, dl_ref, mo_ref, vo_ref):
        cv = c_ref[...]
        ca = cv * _sigmoid(cv)
        g = ca[:, 0:1] * d_ref[0:1, :]
        for b in range(1, N_DEV):
            g = g + ca[:, b:b + 1] * d_ref[b:b + 1, :]
        g_ref[...] = g
        dl_ref[...], mo_ref[...], vo_ref[...] = _adamw_math(g, w_ref[...], m_ref[...], v_ref[...])

    o = jax.ShapeDtypeStruct((rows, cols), F32)
    return pl.pallas_call(
        body, name="ada_bwd_adamw", grid=(rows // tr,),
        in_specs=[pl.BlockSpec((tr, N_DEV), lambda i: (i, 0)), pl.BlockSpec((N_DEV, cols), lambda i: (0, 0)), blk, blk, blk],
        out_specs=[blk] * 4, out_shape=[o, o, o, o], compiler_params=_cparams(("parallel",)))(c_all.T, dmod_cols, w, m, v)


def _reduce_adamw(slabs, w, m, v, name):
    rows, cols = w.shape
    n_src = slabs.shape[0]
    if rows % 128 == 0:
        tr, steps = 128, rows // 128
        blk = pl.BlockSpec((tr, cols), lambda i: (i, 0))
        sblk = pl.BlockSpec((n_src, tr, cols), lambda i: (0, i, 0))
    else:
        tc, steps = 256, cols // 256
        blk = pl.BlockSpec((rows, tc), lambda i: (0, i))
        sblk = pl.BlockSpec((n_src, rows, tc), lambda i: (0, 0, i))

    def body(s_ref, w_ref, m_ref, v_ref, g_ref, dl_ref, mo_ref, vo_ref):
        g = s_ref[0].astype(F32)
        for src in range(1, n_src):
            g = g + s_ref[src].astype(F32)
        g_ref[...] = g
        dl_ref[...], mo_ref[...], vo_ref[...] = _adamw_math(g, w_ref[...], m_ref[...], v_ref[...])

    o = jax.ShapeDtypeStruct((rows, cols), F32)
    return pl.pallas_call(
        body, name=name, grid=(steps,), in_specs=[sblk, blk, blk, blk],
        out_specs=[blk] * 4, out_shape=[o, o, o, o], compiler_params=_cparams(("parallel",)))(slabs, w, m, v)


def _small_reduce_adamw(gathered, w, m, v):
    def body(s_ref, w_ref, m_ref, v_ref, g_ref, dl_ref, mo_ref, vo_ref):
        g = s_ref[0]
        for dev in range(1, N_DEV):
            g = g + s_ref[dev]
        g_ref[...] = g
        dl_ref[...], mo_ref[...], vo_ref[...] = _adamw_math(g, w_ref[...], m_ref[...], v_ref[...])

    o = jax.ShapeDtypeStruct(w.shape, F32)
    return pl.pallas_call(body, name="small_reduce_adamw", out_shape=[o, o, o, o], compiler_params=_cparams())(gathered, w, m, v)


def _adamw_small(g, w, m, v, name):
    def body(g_ref, w_ref, m_ref, v_ref, dl_ref, mo_ref, vo_ref):
        dl_ref[...], mo_ref[...], vo_ref[...] = _adamw_math(g_ref[...], w_ref[...], m_ref[...], v_ref[...])

    o = jax.ShapeDtypeStruct(w.shape, F32)
    return pl.pallas_call(body, name=name, out_shape=[o, o, o], compiler_params=_cparams())(g, w, m, v)


class _Exchange:
    def __init__(self, arrs, scatter):
        self.arrs, self.scatter, self.n = list(arrs), scatter, len(arrs)
        hbm = pl.BlockSpec(memory_space=pltpu.HBM)
        self.in_specs = [hbm] * self.n
        self.out_specs = [hbm] * self.n
        self.out_shape = [jax.ShapeDtypeStruct(a.shape if scatter else (N_DEV,) + a.shape, a.dtype) for a in self.arrs]
        self.scratch = [pltpu.SemaphoreType.DMA((self.n * (N_DEV - 1),)), pltpu.SemaphoreType.DMA((self.n * (N_DEV - 1),)),
                        pltpu.SemaphoreType.DMA((self.n,))]

    def _local(self, ins, outs, sems):
        me = 4 * lax.axis_index("x") + 2 * lax.axis_index("y") + lax.axis_index("c")
        return [pltpu.make_async_copy(ins[a].at[me] if self.scatter else ins[a], outs[a].at[me], sems[2].at[a])
                for a in range(self.n)]

    def _remote(self, ins, outs, sems, arriving):
        send_sems, recv_sems, _ = sems
        x, y, c = lax.axis_index("x"), lax.axis_index("y"), lax.axis_index("c")
        me = 4 * x + 2 * y + c
        remote = []
        for a in range(self.n):
            for k in range(1, N_DEV):
                px = 1 - x if k & 4 else x
                py = 1 - y if k & 2 else y
                pc = 1 - c if k & 1 else c
                peer = 4 * px + 2 * py + pc
                sem = a * (N_DEV - 1) + k - 1
                remote.append(pltpu.make_async_remote_copy(
                    src_ref=ins[a].at[peer] if self.scatter else ins[a], dst_ref=outs[a].at[peer if arriving else me],
                    send_sem=send_sems.at[sem], recv_sem=recv_sems.at[sem], device_id=(px, py, pc), device_id_type=MESH_IDS))
        return remote

    def start(self, ins, outs, sems):
        for cp in self._local(ins, outs, sems) + self._remote(ins, outs, sems, arriving=False):
            cp.start()

    def forward(self, ins, outs, sems):
        pass

    def wait(self, ins, outs, sems):
        for send, arrival in zip(self._remote(ins, outs, sems, arriving=False), self._remote(ins, outs, sems, arriving=True)):
            send.wait_send()
            arrival.wait_recv()
        for cp in self._local(ins, outs, sems):
            cp.wait()


N_CHIP = N_DEV // 2


class _SiblingSwap(_Exchange):
    def __init__(self, arrs):
        super().__init__(arrs, scatter=True)
        self.out_shape = [jax.ShapeDtypeStruct((N_CHIP,) + a.shape[2:], a.dtype) for a in self.arrs]
        self.scratch = [pltpu.SemaphoreType.DMA((self.n,)), pltpu.SemaphoreType.DMA((self.n,)), pltpu.SemaphoreType.DMA((1,))]

    def _copies(self, ins, outs, sems):
        x, y, c = lax.axis_index("x"), lax.axis_index("y"), lax.axis_index("c")
        return [pltpu.make_async_remote_copy(src_ref=ins[a].at[:, 1 - c], dst_ref=outs[a], send_sem=sems[0].at[a], recv_sem=sems[1].at[a],
                                             device_id=(x, y, 1 - c), device_id_type=MESH_IDS) for a in range(self.n)]

    def start(self, ins, outs, sems):
        for cp in self._copies(ins, outs, sems):
            cp.start()

    def wait(self, ins, outs, sems):
        for cp in self._copies(ins, outs, sems):
            cp.wait()


class _ChipScatter(_Exchange):
    def __init__(self, arrs):
        super().__init__(arrs, scatter=True)
        n_pairs = self.n * (N_CHIP - 1)
        self.scratch = [pltpu.SemaphoreType.DMA((n_pairs,)), pltpu.SemaphoreType.DMA((n_pairs,)), pltpu.SemaphoreType.DMA((self.n,))]

    def _local(self, ins, outs, sems):
        chip = 2 * lax.axis_index("x") + lax.axis_index("y")
        return [pltpu.make_async_copy(ins[a].at[chip], outs[a].at[chip], sems[2].at[a]) for a in range(self.n)]

    def _remote(self, ins, outs, sems, arriving):
        send_sems, recv_sems, _ = sems
        x, y, c = lax.axis_index("x"), lax.axis_index("y"), lax.axis_index("c")
        chip = 2 * x + y
        remote = []
        for a in range(self.n):
            for k in range(1, N_CHIP):
                px = 1 - x if k & 2 else x
                py = 1 - y if k & 1 else y
                peer = 2 * px + py
                sem = a * (N_CHIP - 1) + k - 1
                remote.append(pltpu.make_async_remote_copy(
                    src_ref=ins[a].at[peer], dst_ref=outs[a].at[peer if arriving else chip], send_sem=send_sems.at[sem],
                    recv_sem=recv_sems.at[sem], device_id=(px, py, c), device_id_type=MESH_IDS))
        return remote


def _chip_sum(mine, theirs):
    n, rows, cols = mine.shape
    blk = pl.BlockSpec((1, rows, 256), lambda q, j: (q, 0, j))

    def body(a_ref, b_ref, o_ref):
        o_ref[...] = (a_ref[...].astype(F32) + b_ref[...].astype(F32)).astype(BF16)

    return pl.pallas_call(body, name="chip_sum", grid=(n, cols // 256), in_specs=[blk, blk], out_specs=blk,
                          out_shape=jax.ShapeDtypeStruct(mine.shape, BF16),
                          compiler_params=_cparams(("parallel", "parallel")))(mine, theirs)


class _Gather2(_Exchange):
    def __init__(self, arrs):
        super().__init__(arrs, scatter=False)

    def _copies(self, ins, outs, sems):
        send_sems, recv_sems, _ = sems
        x, y, c = lax.axis_index("x"), lax.axis_index("y"), lax.axis_index("c")
        sibling = (x, y, 1 - c)
        chips = [(1 - x, y), (x, 1 - y), (1 - x, 1 - y)]
        first, passed, landed = [], [], []
        for a in range(self.n):
            def copy(k, block, to, src=None, a=a):
                slab = outs[a].at[4 * block[0] + 2 * block[1] + block[2]]
                return pltpu.make_async_remote_copy(
                    src_ref=slab if src is None else src, dst_ref=slab, send_sem=send_sems.at[a * (N_DEV - 1) + k],
                    recv_sem=recv_sems.at[a * (N_DEV - 1) + k], device_id=to, device_id_type=MESH_IDS)

            first.append(copy(0, (x, y, c), sibling, src=ins[a]))
            landed.append(copy(0, sibling, sibling))
            for j, chip in enumerate(chips):
                first.append(copy(1 + j, (x, y, c), (*chip, c), src=ins[a]))
                passed.append((copy(1 + j, (*chip, c), sibling), copy(4 + j, (*chip, c), sibling)))
                landed.append(copy(4 + j, (*chip, 1 - c), sibling))
        return first, passed, landed

    def start(self, ins, outs, sems):
        for cp in self._local(ins, outs, sems) + self._copies(ins, outs, sems)[0]:
            cp.start()

    def forward(self, ins, outs, sems):
        for arrival, onward in self._copies(ins, outs, sems)[1]:
            arrival.wait_recv()
            onward.start()

    def wait(self, ins, outs, sems):
        first, passed, landed = self._copies(ins, outs, sems)
        for arrival in landed:
            arrival.wait_recv()
        for cp in first + [onward for _, onward in passed]:
            cp.wait_send()
        for cp in self._local(ins, outs, sems):
            cp.wait()


def _split_comm_refs(refs, n_in, n_out, n_scr, comm):
    nc = comm.n if comm is not None else 0
    ns = 3 if comm is not None else 0
    pos, groups = 0, []
    for cnt in (n_in, nc, n_out, nc, n_scr, ns):
        groups.append(refs[pos:pos + cnt])
        pos += cnt
    assert pos == len(refs), (pos, len(refs))
    return groups


def _pcall(body, args, *, name, grid, in_specs, out_specs, out_shape, scratch_shapes=(), sem=None, comm=None):
    in_specs, out_specs, out_shape, scratch_shapes = list(in_specs), list(out_specs), list(out_shape), list(scratch_shapes)
    n_in, n_out, n_scr = len(in_specs), len(out_specs), len(scratch_shapes)
    if comm is None:
        kernel_body = body
    else:
        def kernel_body(*refs):
            ins, cins, outs, couts, scr, sems = _split_comm_refs(refs, n_in, n_out, n_scr, comm)
            ids = [pl.program_id(a) for a in range(len(grid))]
            first, last = ids[0] == 0, ids[0] == grid[0] - 1
            for a in range(1, len(grid)):
                first, last = first & (ids[a] == 0), last & (ids[a] == grid[a] - 1)

            middle = ids[0] == (2 * grid[0]) // 3
            for a in range(1, len(grid)):
                middle = middle & (ids[a] == 0)

            @pl.when(first)
            def _():
                comm.start(cins, couts, sems)

            @pl.when(middle)
            def _():
                comm.forward(cins, couts, sems)

            body(*ins, *outs, *scr)

            @pl.when(last)
            def _():
                comm.wait(cins, couts, sems)

        in_specs, out_specs, out_shape = in_specs + comm.in_specs, out_specs + comm.out_specs, out_shape + comm.out_shape
        scratch_shapes, args = scratch_shapes + comm.scratch, list(args) + comm.arrs
        sem = ("arbitrary",) * len(grid)
    res = pl.pallas_call(kernel_body, name=name, grid=grid, in_specs=in_specs, out_specs=out_specs, out_shape=out_shape,
                         scratch_shapes=scratch_shapes, compiler_params=_cparams(sem))(*args)
    return res[:n_out], res[n_out:]


def _exchange(arrs, name, scatter=False, ex=None):
    if ex is None:
        ex = _Exchange(arrs, scatter=True) if scatter else _Gather2(arrs)

    def body(*refs):
        _, ins, _, outs, _, sems = _split_comm_refs(refs, 0, 0, 0, ex)
        ex.start(ins, outs, sems)
        ex.forward(ins, outs, sems)
        ex.wait(ins, outs, sems)

    return pl.pallas_call(body, name=name, in_specs=ex.in_specs, out_specs=ex.out_specs, out_shape=ex.out_shape,
                          scratch_shapes=ex.scratch)(*ex.arrs)


def _pad_lanes(v, width=LANE):
    return jnp.pad(v, ((0, 0), (0, width - v.shape[1])))


def _shards_to_cols(g):
    return jnp.transpose(g, (1, 0, 2)).reshape(g.shape[1], N_DEV * g.shape[2])


def _local_step(x, tgt, mod, w_in_pt, conv_w, conv_b, dt_bias, a_log, d_skip, ssd_norm_w, q_norm_w, k_norm_w,
                attn_norm_w, w_out_sh, w_ff1_sh, w_ff2_sh, norm1_w, norm2_w, core):
    shift1, scale1, gate1, shift2, scale2, gate2 = [mod[i:i + 1] for i in range(N_MOD)]
    dtb, alog, dsk = _pad_lanes(dt_bias), _pad_lanes(a_log), _pad_lanes(d_skip)
    qw, kw = jnp.tile(q_norm_w, (1, ATT_HEADS)), jnp.tile(k_norm_w, (1, ATT_HEADS))

    h1 = _norm_mod_fwd(x, norm1_w, scale1, shift1, "norm1_fwd")
    proj = _matmul(h1, w_in_pt, tb=True, tm=2048, tn=896, tk=1024, name="in_proj")
    pre, act = _conv_fwd(proj, conv_w, conv_b)
    ypre, ycat_ssd, hall = _ssd_fwd(proj, act, dtb, alog, dsk, ssd_norm_w)
    (o_att, lse), (w_out_g, w_ff1_g, w_ff2_g) = _att_fwd(proj, qw, kw, comm=_Gather2([w_out_sh, w_ff1_sh, w_ff2_sh]))
    w_out = w_out_g.reshape(2 * D_MODEL, D_MODEL)
    w_ff1 = _shards_to_cols(w_ff1_g)
    w_ff2 = w_ff2_g.reshape(D_FF, D_MODEL)
    ycat = _att_norm_fwd(o_att, attn_norm_w, ycat_ssd)
    row32, row16, vec32 = ("row", F32), ("row", BF16), ("vec", F32)
    mix, x1, h2 = _matmul_rows(ycat, w_out, _residual_norm_epilogue, [x], [gate1, norm2_w, scale2, shift2],
                               [row32, row32, row16], tm=512, name="out_proj")
    u, act_ff = _matmul(h2, w_ff1, tm=1024, tn=2048, tk=1024, name="ff1", mode="relu2")
    loss, dout, dff, dgate2 = _matmul_rows(act_ff, w_ff2, _loss_epilogue, [x1, tgt], [gate2],
                                           [("one", F32), row32, row16, vec32], tm=512, name="ff2")

    du = _matmul(dff, w_ff2, tb=True, tm=512, tn=4096, tk=1024, out_dtype=BF16, name="ff2_dx", mode="drelu2", u=u)
    g_ff2 = _matmul(act_ff, dff, ta=True, tm=512, tn=1024, tk=4096, out_dtype=BF16, name="ff2_dw")
    dx1, dshift2, dscale2, g_norm2, dmix, dgate1 = _matmul_rows(
        du, w_ff1, _norm_bwd_epilogue, [x1, dout, mix], [norm2_w, scale2, gate1],
        [row32, vec32, vec32, vec32, row16, vec32], tb=True, tm=512, name="ff1_dx")
    g_ff1 = _matmul(h2, du, ta=True, tm=1024, tn=D_FF // N_DEV, tk=4096, out_dtype=BF16, name="ff1_dw", shard_out=True)

    dy_ssd, do, stats, g_attn_norm = _matmul_rows(
        dmix, w_out, _mixer_split_epilogue, [o_att, lse], [attn_norm_w],
        [("row", F32, SSD_D_INNER), ("row", F32, ATT_D), ("row", F32, ATT_D), ("vec", F32, ATT_D)], tb=True, tm=512, name="out_proj_dx")
    g_out = _matmul(ycat, dmix, ta=True, tm=512, tn=1024, tk=4096, out_dtype=BF16, name="out_proj_dw")
    ff_slabs = [g_ff1, g_ff2.reshape(N_DEV, D_FF // N_DEV, D_MODEL)]
    (dq, dk, dv, dqw, dkw), (s_ff1, s_ff2) = _att_bwd(proj, do, stats, qw, kw, comm=_Exchange(ff_slabs, scatter=True))
    out_slabs = [g_out.astype(BF16).reshape(N_DEV, 2 * D_MODEL // N_DEV, D_MODEL)]
    (dz, dact, ddtr, da, g_dsk, g_dtb, g_ssd_norm), (s_out,) = _ssd_bwd(
        dy_ssd, ypre, proj, act, hall, dtb, alog, dsk, ssd_norm_w, comm=_Exchange(out_slabs, scatter=True))
    dxbc, g_conv_w, g_conv_b = _conv_bwd(dact, pre, proj, conv_w)
    dproj = [(dz, OFF_Z), (dxbc, OFF_XBC), (ddtr, OFF_DT), (dq, OFF_Q), (dk, OFF_K), (dv, OFF_V)]
    g_head, g_tail = _pieces_t_matmul([[dz, dxbc], [dq, dk, dv]], h1, tm=256, name="in_proj_dw")
    g_dt = _matmul(ddtr, h1, ta=True, tm=LANE, tn=1024, tk=4096, out_dtype=BF16, name="in_proj_dw_dt")[:SSD_HEADS]
    in_slabs = jnp.concatenate([g_head, g_dt, g_tail], axis=0).reshape(N_CHIP, 2, IN_W // N_DEV, D_MODEL)
    (sibling_slabs,) = _exchange(None, "swap_w_in_grads", ex=_SiblingSwap([in_slabs]))
    chip_slabs = _chip_sum(lax.dynamic_index_in_dim(in_slabs, core, axis=1, keepdims=False), sibling_slabs)
    (grad_x, dshift1, dscale1, g_norm1), (s_in,) = _matmul_rows(
        dproj, w_in_pt, _norm_bwd_epilogue, [x, dx1], [norm1_w, scale1], [row32, vec32, vec32, vec32],
        tm=256, name="in_proj_dx", comm=_ChipScatter([chip_slabs]))

    dmod = jnp.concatenate([dshift1, dscale1, dgate1, dshift2, dscale2, dgate2], axis=0)
    g_alog = da[:, :SSD_HEADS] * (-jnp.exp(a_log))
    g_qw = dqw.reshape(ATT_HEADS, HEAD_DIM).sum(axis=0, keepdims=True)
    g_kw = dkw.reshape(ATT_HEADS, HEAD_DIM).sum(axis=0, keepdims=True)
    return dict(loss=loss, grad_x=grad_x, dmod=dmod, norm1_w=g_norm1, norm2_w=g_norm2, w_in=s_in, conv_w=g_conv_w,
                conv_b=g_conv_b, dt_bias=g_dtb[:, :SSD_HEADS], a_log=g_alog, d_skip=g_dsk[:, :SSD_HEADS],
                ssd_norm_w=g_ssd_norm, q_norm_w=g_qw, k_norm_w=g_kw, attn_norm_w=g_attn_norm, w_out=s_out,
                w_ff1=s_ff1, w_ff2=s_ff2)


def _pack_w_in_rows(wt_full):
    cut = OFF_DT + SSD_HEADS
    pad = jnp.zeros((LANE - SSD_HEADS, wt_full.shape[1]), wt_full.dtype)
    return jnp.concatenate([wt_full[:cut], pad, wt_full[cut:]], axis=0)


MISC_FIELDS = (("dt_bias", SSD_HEADS), ("a_log", SSD_HEADS), ("d_skip", SSD_HEADS), ("q_norm_w", HEAD_DIM), ("k_norm_w", HEAD_DIM),
               ("loss", 1))
SMALL_LAYOUT = (("b_ada", 6), ("norm1_w", 1), ("norm2_w", 1), ("conv_w", 8), ("conv_b", 2), ("ssd_norm_w", 1),
                ("attn_norm_w", 1), ("misc", 1))


def _pack_small(vals):
    rows = []
    for name, nrow in SMALL_LAYOUT:
        if name == "misc":
            misc = jnp.concatenate([vals[f].reshape(1, n) if f in vals else jnp.zeros((1, n), F32) for f, n in MISC_FIELDS], axis=1)
            rows.append(_pad_lanes(misc, D_MODEL))
        elif name in vals:
            rows.append(vals[name].reshape(nrow, D_MODEL))
        else:
            rows.append(jnp.zeros((nrow, D_MODEL), F32))
    used = sum(n for _, n in SMALL_LAYOUT)
    rows.append(jnp.zeros((SMALL_ROWS - used, D_MODEL), F32))
    return jnp.concatenate(rows, axis=0)


def _unpack_small(packed):
    out, r = {}, 0
    for name, nrow in SMALL_LAYOUT:
        blk = packed[r:r + nrow]
        r += nrow
        if name == "misc":
            c0 = 0
            for f, n in MISC_FIELDS:
                out[f] = blk[:, c0:c0 + n]
                c0 += n
        elif name == "b_ada":
            out[name] = blk.reshape(1, N_MOD * D_MODEL)
        elif name == "conv_w":
            out[name] = blk.reshape(CONV_K, CONV_CH)
        elif name == "conv_b":
            out[name] = blk.reshape(1, CONV_CH)
        else:
            out[name] = blk
    return out


WEIGHT_NAMES = ("norm1_w", "norm2_w", "w_ada", "b_ada", "w_in", "conv_w", "conv_b", "dt_bias", "a_log", "d_skip",
                "ssd_norm_w", "q_norm_w", "k_norm_w", "attn_norm_w", "w_out", "w_ff1", "w_ff2")
SMALL_NAMES = ("norm1_w", "norm2_w", "b_ada", "conv_b", "dt_bias", "a_log", "d_skip", "ssd_norm_w", "q_norm_w",
               "k_norm_w", "attn_norm_w")


def kernel(x, c, norm1_w, norm2_w, w_ada, b_ada, w_in, conv_w, conv_b, dt_bias, a_log, d_skip, ssd_norm_w, q_norm_w, k_norm_w, attn_norm_w, w_out, w_ff1, w_ff2, loss_target, m_norm1_w, m_norm2_w, m_w_ada, m_b_ada, m_w_in, m_conv_w, m_conv_b, m_dt_bias, m_a_log, m_d_skip, m_ssd_norm_w, m_q_norm_w, m_k_norm_w, m_attn_norm_w, m_w_out, m_w_ff1, m_w_ff2, v_norm1_w, v_norm2_w, v_w_ada, v_b_ada, v_w_in, v_conv_w, v_conv_b, v_dt_bias, v_a_log, v_d_skip, v_ssd_norm_w, v_q_norm_w, v_k_norm_w, v_attn_norm_w, v_w_out, v_w_ff1, v_w_ff2):
    args = dict(locals())
    w = {n: args[n] for n in WEIGHT_NAMES}
    m = {n: args["m_" + n] for n in WEIGHT_NAMES}
    v = {n: args["v_" + n] for n in WEIGHT_NAMES}
    me = 4 * lax.axis_index("x") + 2 * lax.axis_index("y") + lax.axis_index("c")

    c_rows = jnp.pad(c, ((0, 7), (0, 0)))
    w_in_t, m_in_t, v_in_t = [jnp.transpose(t["w_in"][0]) for t in (w, m, v)]
    c_g, conv_g, w_in_g = _exchange([c_rows, w["conv_w"][0], w_in_t.astype(BF16)], "gather_w_in", scatter=False)
    c_all = c_g[:, 0, :]
    conv_full = _shards_to_cols(conv_g)
    w_in_pt = _pack_w_in_rows(w_in_g.reshape(IN_W, D_MODEL))

    mod_part = _ada_fwd(c_all, w["w_ada"][0])
    (mod_g,) = _exchange([mod_part], "gather_mod", scatter=False)
    mod_mine = lax.dynamic_index_in_dim(mod_g, me, axis=1, keepdims=False).reshape(1, N_MOD * D_MODEL) + w["b_ada"]
    mod = mod_mine.reshape(N_MOD, D_MODEL)

    res = _local_step(x[0], loss_target[0], mod, w_in_pt, conv_full, w["conv_b"], w["dt_bias"], w["a_log"], w["d_skip"],
                      w["ssd_norm_w"], w["q_norm_w"], w["k_norm_w"], w["attn_norm_w"], w["w_out"][0].astype(BF16),
                      w["w_ff1"][0].astype(BF16), w["w_ff2"][0].astype(BF16), w["norm1_w"], w["norm2_w"], lax.axis_index("c"))

    small_vals = {n: res[n] for n in SMALL_NAMES if n != "b_ada"}
    small_vals["b_ada"] = res["dmod"]
    small_vals["conv_w"] = res["conv_w"]
    small_vals["loss"] = res["loss"]
    (small_g,) = _exchange([_pack_small(small_vals)], "gather_small", scatter=False)

    grads, delta, new_m, new_v = {}, {}, {}, {}
    for name in ("w_out", "w_ff1", "w_ff2"):
        outs = _reduce_adamw(res[name], w[name][0], m[name][0], v[name][0], "adamw_" + name)
        grads[name], delta[name], new_m[name], new_v[name] = [o[None] for o in outs]
    outs = _reduce_adamw(res["w_in"], w_in_t, m_in_t, v_in_t, "adamw_w_in")
    grads["w_in"], delta["w_in"], new_m["w_in"], new_v["w_in"] = [jnp.transpose(o)[None] for o in outs]

    sm = _small_reduce_adamw(small_g, _pack_small({n: w[n] for n in SMALL_NAMES}), _pack_small({n: m[n] for n in SMALL_NAMES}),
                             _pack_small({n: v[n] for n in SMALL_NAMES}))
    sm = [_unpack_small(p) for p in sm]
    for n in SMALL_NAMES:
        grads[n], delta[n], new_m[n], new_v[n] = [p[n] for p in sm]
    shard_w = CONV_CH // N_DEV
    g_conv = lax.dynamic_slice_in_dim(sm[0]["conv_w"], me * shard_w, shard_w, axis=1)
    cw = _adamw_small(g_conv, w["conv_w"][0], m["conv_w"][0], v["conv_w"][0], "adamw_conv_w")
    grads["conv_w"] = g_conv[None]
    delta["conv_w"], new_m["conv_w"], new_v["conv_w"] = [o[None] for o in cw]

    ada_w = w_ada.shape[2]
    dmod_all = small_g[:, :N_MOD, :].reshape(N_DEV, N_MOD * D_MODEL)
    dmod_cols = lax.dynamic_slice_in_dim(dmod_all, me * ada_w, ada_w, axis=1)
    outs = _ada_bwd_adamw(c_all, dmod_cols, w["w_ada"][0], m["w_ada"][0], v["w_ada"][0])
    grads["w_ada"], delta["w_ada"], new_m["w_ada"], new_v["w_ada"] = [o[None] for o in outs]

    loss = sm[0]["loss"][0, 0]
    return (loss, res["grad_x"][None], *[grads[n] for n in WEIGHT_NAMES], *[delta[n] for n in WEIGHT_NAMES],
            *[new_m[n] for n in WEIGHT_NAMES], *[new_v[n] for n in WEIGHT_NAMES])
```

```python
import jax
import jax.numpy as jnp
from jax import lax
from jax.experimental import pallas as pl
from jax.experimental.pallas import tpu as pltpu

F32 = jnp.float32
BF16 = jnp.bfloat16
HIGHEST = lax.Precision.HIGHEST
MESH_IDS = pl.DeviceIdType.MESH

N_DEV = 8
D_MODEL = 1024
HEAD_DIM = 64
SSD_HEADS = 16
SSD_GROUPS = 4
HEADS_PER_GROUP = SSD_HEADS // SSD_GROUPS
SSD_STATE = 128
SSD_CHUNK = 128
SSD_D_INNER = SSD_HEADS * HEAD_DIM
GROUP_WIDTH = SSD_D_INNER // SSD_GROUPS
CONV_K = 4
CONV_CH = SSD_D_INNER + 2 * SSD_GROUPS * SSD_STATE
ATT_HEADS = 16
ATT_D = ATT_HEADS * HEAD_DIM
ATT_BLK = 128
DILATIONS = (1, 4, 16)
D_FF = 4 * D_MODEL
N_MOD = 6
EPS = 1e-6
IN_W = SSD_D_INNER + CONV_CH + SSD_HEADS + 3 * ATT_D
LANE = 128
OFF_Z, OFF_XBC, OFF_DT = 0, SSD_D_INNER, SSD_D_INNER + CONV_CH
OFF_Q = OFF_DT + LANE
OFF_K, OFF_V = OFF_Q + ATT_D, OFF_Q + 2 * ATT_D
IN_WP = OFF_V + ATT_D

ADAM_LR, ADAM_B1, ADAM_B2, ADAM_EPS, ADAM_WD, ADAM_STEP = 0.001, 0.9, 0.999, 1e-08, 0.01, 10
VMEM_LIMIT = 56 * 1024 * 1024
ROW_TILE = 512
SMALL_ROWS = 24


def _cparams(sem=None):
    return pltpu.CompilerParams(dimension_semantics=sem, vmem_limit_bytes=VMEM_LIMIT)


def _sigmoid(v):
    return 1.0 / (1.0 + jnp.exp(-v))


def _softplus(v):
    y = jnp.exp(-jnp.abs(v))
    small = y * (1.0 - y * (0.5 - y * (1.0 / 3.0)))
    return jnp.maximum(v, 0.0) + jnp.where(y < 0.01, small, jnp.log(1.0 + y))


def _dot(a, b, dims, precision=None):
    return lax.dot_general(a, b, (dims, ((), ())), preferred_element_type=F32, precision=precision)


NN = ((1,), (0,))
NT = ((1,), (1,))
TN = ((0,), (0,))


def _matmul(a, b, *, ta=False, tb=False, tm, tn, tk, out_dtype=F32, name, mode=None, u=None, comm=None, shard_out=False):
    m, k = (a.shape[1], a.shape[0]) if ta else a.shape
    n = b.shape[0] if tb else b.shape[1]
    assert m % tm == 0 and n % tn == 0 and k % tk == 0, (name, m, n, k)
    nk = k // tk
    a_spec = pl.BlockSpec((tk, tm), lambda i, j, kk: (kk, i)) if ta else pl.BlockSpec((tm, tk), lambda i, j, kk: (i, kk))
    b_spec = pl.BlockSpec((tn, tk), lambda i, j, kk: (j, kk)) if tb else pl.BlockSpec((tk, tn), lambda i, j, kk: (kk, j))
    o_spec = pl.BlockSpec((tm, tn), lambda i, j, kk: (i, j))
    dims = ((0,) if ta else (1,), (1,) if tb else (0,))
    n_out = 2 if mode == "relu2" else 1

    def body(*refs):
        if mode == "drelu2":
            a_ref, b_ref, u_ref = refs[:3]
            rest = refs[3:]
        else:
            a_ref, b_ref = refs[:2]
            u_ref = None
            rest = refs[2:]
        outs = rest[:n_out]
        part = _dot(a_ref[...], b_ref[...], dims)

        def finish(r):
            if mode == "relu2":
                outs[0][...] = r.astype(BF16)
                rr = jnp.maximum(r, 0.0)
                outs[1][...] = (rr * rr).astype(BF16)
            elif mode == "drelu2":
                outs[0][...] = (r * (2.0 * jnp.maximum(u_ref[...].astype(F32), 0.0))).astype(out_dtype)
            else:
                outs[0][...] = r.astype(out_dtype)

        if nk == 1:
            finish(part)
        else:
            acc = rest[n_out]
            kk = pl.program_id(2)

            @pl.when(kk == 0)
            def _():
                acc[...] = part

            @pl.when(kk > 0)
            def _():
                acc[...] += part

            @pl.when(kk == nk - 1)
            def _():
                finish(acc[...])

    in_specs = [a_spec, b_spec]
    args = [a, b]
    if mode == "drelu2":
        in_specs.append(o_spec)
        args.append(u)
    if mode == "relu2":
        out_shape = [jax.ShapeDtypeStruct((m, n), BF16), jax.ShapeDtypeStruct((m, n), BF16)]
    elif shard_out:
        out_shape = [jax.ShapeDtypeStruct((n // tn, m, tn), out_dtype)]
        o_spec = pl.BlockSpec((None, tm, tn), lambda i, j, kk: (j, i, 0))
    else:
        out_shape = [jax.ShapeDtypeStruct((m, n), out_dtype)]
    outs, comm_outs = _pcall(
        body, args, name=name, grid=(m // tm, n // tn, nk), in_specs=in_specs, out_specs=[o_spec] * n_out,
        out_shape=out_shape, scratch_shapes=[pltpu.VMEM((tm, tn), F32)] if nk > 1 else [],
        sem=("parallel", "parallel", "arbitrary"), comm=comm)
    res = tuple(outs) if mode == "relu2" else outs[0]
    return res if comm is None else (res, comm_outs)


def _pieces_t_matmul(groups, b, *, tm, name):
    k, n = b.shape
    pieces = [p for g in groups for p in g]
    starts, tiles = [], 0
    for p in pieces:
        assert p.shape[0] == k and p.shape[1] % tm == 0, (name, p.shape)
        starts.append(tiles)
        tiles += p.shape[1] // tm
    group_of, group_start, group_tiles = [], [], []
    for gi, g in enumerate(groups):
        group_start.append(starts[len(group_of)])
        group_of += [gi] * len(g)
        group_tiles.append(sum(p.shape[1] // tm for p in g))

    def clipped(block, start, count):
        return pl.BlockSpec(block, (lambda i: (0, jnp.clip(i - start, 0, count - 1))) if block[0] == k
                            else (lambda i: (jnp.clip(i - start, 0, count - 1), 0)))

    def body(*refs):
        a_refs, b_ref, o_refs = refs[:len(pieces)], refs[len(pieces)], refs[len(pieces) + 1:]
        i = pl.program_id(0)
        for a_ref, start, p, gi in zip(a_refs, starts, pieces, group_of):
            @pl.when((i >= start) & (i < start + p.shape[1] // tm))
            def _(a_ref=a_ref, o_ref=o_refs[gi]):
                o_ref[...] = _dot(a_ref[...], b_ref[...], TN).astype(BF16)

    return pl.pallas_call(
        body, name=name, grid=(tiles,),
        in_specs=[clipped((k, tm), s0, p.shape[1] // tm) for s0, p in zip(starts, pieces)] + [pl.BlockSpec((k, n), lambda i: (0, 0))],
        out_specs=[clipped((tm, n), s0, cnt) for s0, cnt in zip(group_start, group_tiles)],
        out_shape=[jax.ShapeDtypeStruct((cnt * tm, n), BF16) for cnt in group_tiles],
        compiler_params=_cparams(("arbitrary",)))(*pieces, b)


def _rms_mod(xv, nw, scale, shift):
    r = lax.rsqrt(jnp.mean(xv * xv, axis=-1, keepdims=True) + EPS)
    return ((xv * r) * nw * (1.0 + scale) + shift).astype(BF16)


def _norm_mod_fwd(x, nw, scale, shift, name):
    s, d = x.shape
    row = pl.BlockSpec((ROW_TILE, d), lambda i: (i, 0))
    vec = pl.BlockSpec((1, d), lambda i: (0, 0))

    def body(x_ref, nw_ref, sc_ref, sh_ref, h_ref):
        h_ref[...] = _rms_mod(x_ref[...], nw_ref[...], sc_ref[...], sh_ref[...])

    return pl.pallas_call(body, name=name, grid=(s // ROW_TILE,), in_specs=[row, vec, vec, vec], out_specs=row,
                          out_shape=jax.ShapeDtypeStruct((s, d), BF16), compiler_params=_cparams(("parallel",)))(x, nw, scale, shift)


def _matmul_rows(a, b, epilogue, row_in, vec_in, outs, *, tb=False, tm, name, comm=None):
    pieces = a if isinstance(a, list) else [(a, 0)]
    assert not (tb and len(pieces) > 1)
    m = pieces[0][0].shape[0]
    n = b.shape[0] if tb else b.shape[1]
    assert m % tm == 0, (name, m, tm)
    dims = ((1,), (1,) if tb else (0,))
    n_a, n_row, n_vec = len(pieces), len(row_in), len(vec_in)

    def body(*refs):
        a_refs, b_ref, rest = refs[:n_a], refs[n_a], refs[n_a + 1:]
        if n_a == 1:
            c = _dot(a_refs[0][...], b_ref[...], dims)
        else:
            c = None
            for a_ref, (piece, off) in zip(a_refs, pieces):
                part = _dot(a_ref[...], b_ref[off:off + piece.shape[1], :], dims)
                c = part if c is None else c + part
        epilogue(c, pl.program_id(0) == 0, rest[:n_row], rest[n_row:n_row + n_vec], rest[n_row + n_vec:])

    def spec(kind, width):
        block = {"row": (tm, width), "vec": (1, width), "one": (1, 1)}[kind]
        return pl.BlockSpec(block, (lambda i: (i, 0)) if kind == "row" else (lambda i: (0, 0)))

    def shape(kind, width):
        return {"row": (m, width), "vec": (1, width), "one": (1, 1)}[kind]

    outs = [(o[0], o[1], o[2] if len(o) > 2 else n) for o in outs]
    res, comm_outs = _pcall(
        body, [*[p for p, _ in pieces], b, *row_in, *vec_in], name=name, grid=(m // tm,),
        in_specs=[spec("row", p.shape[1]) for p, _ in pieces] + [pl.BlockSpec(b.shape, lambda i: (0, 0))]
        + [spec("row", r.shape[1]) for r in row_in] + [spec("vec", v.shape[1]) for v in vec_in],
        out_specs=[spec(kind, width) for kind, _, width in outs],
        out_shape=[jax.ShapeDtypeStruct(shape(kind, width), dt) for kind, dt, width in outs],
        sem=("arbitrary",), comm=comm)
    return res if comm is None else (res, comm_outs)


def _residual_norm_epilogue(mix, first, rows, vecs, outs):
    (x_ref,), (gate_ref, nw_ref, sc_ref, sh_ref), (mix_ref, x1_ref, h_ref) = rows, vecs, outs
    xv = x_ref[...] + gate_ref[...] * mix
    mix_ref[...] = mix
    x1_ref[...] = xv
    h_ref[...] = _rms_mod(xv, nw_ref[...], sc_ref[...], sh_ref[...])


def _loss_epilogue(ff, first, rows, vecs, outs):
    (x1_ref, t_ref), (g_ref,), (loss_ref, dout_ref, dff_ref, dg_ref) = rows, vecs, outs
    d = ff.shape[1]

    @pl.when(first)
    def _():
        loss_ref[...] = jnp.zeros_like(loss_ref)
        dg_ref[...] = jnp.zeros_like(dg_ref)

    err = x1_ref[...] + g_ref[...] * ff - t_ref[...]
    loss_ref[...] += (0.5 / d) * jnp.sum(err * err).reshape(1, 1)
    dout = err * (1.0 / d)
    dout_ref[...] = dout
    dff_ref[...] = (g_ref[...] * dout).astype(BF16)
    dg_ref[...] += jnp.sum(dout * ff, axis=0, keepdims=True)


def _norm_bwd_epilogue(dh, first, rows, vecs, outs):
    with_gate = len(vecs) == 3
    x_ref, dres_ref = rows[:2]
    nw_ref, sc_ref = vecs[:2]
    dx_ref, dsh_ref, dsc_ref, dnw_ref = outs[:4]

    @pl.when(first)
    def _():
        for ref in outs[1:4] + outs[5:]:
            ref[...] = jnp.zeros_like(ref)

    xv = x_ref[...]
    r = lax.rsqrt(jnp.mean(xv * xv, axis=-1, keepdims=True) + EPS)
    nrm = xv * r
    one_sc = 1.0 + sc_ref[...]
    dhn = dh * nrm
    dsh_ref[...] += jnp.sum(dh, axis=0, keepdims=True)
    dsc_ref[...] += jnp.sum(dhn, axis=0, keepdims=True) * nw_ref[...]
    dnw_ref[...] += jnp.sum(dhn, axis=0, keepdims=True) * one_sc
    dn = dh * (nw_ref[...] * one_sc)
    dx = dres_ref[...] + r * (dn - nrm * jnp.mean(dn * nrm, axis=-1, keepdims=True))
    dx_ref[...] = dx
    if with_gate:
        outs[4][...] = (vecs[2][...] * dx).astype(BF16)
        outs[5][...] += jnp.sum(dx * rows[2][...], axis=0, keepdims=True)


CONV_COLS = 256
CONV_FWD_ROWS = 2048
CONV_BWD_ROWS = 1024
CONV_SUB_ROWS = 128
HALO = 8


def _shift_down(cur, halo, k):
    if k == 0:
        return cur
    rolled = pltpu.roll(cur, k, axis=0)
    top = jnp.where(lax.broadcasted_iota(jnp.int32, halo.shape, 0) < k, pltpu.roll(halo, k, axis=0), rolled[:HALO])
    return jnp.concatenate([top, rolled[HALO:]], axis=0)


def _shift_up(cur, halo, k):
    if k == 0:
        return cur
    t = cur.shape[0]
    rolled = pltpu.roll(cur, t - k, axis=0)
    bot = jnp.where(lax.broadcasted_iota(jnp.int32, halo.shape, 0) >= HALO - k, pltpu.roll(halo, HALO - k, axis=0),
                    rolled[t - HALO:])
    return jnp.concatenate([rolled[:t - HALO], bot], axis=0)


def _conv_fwd(proj, conv_w, conv_b):
    s = proj.shape[0]
    nr = s // CONV_FWD_ROWS
    cb0 = OFF_XBC // CONV_COLS
    hb = CONV_FWD_ROWS // HALO
    cur = pl.BlockSpec((CONV_FWD_ROWS, CONV_COLS), lambda j, r: (r, cb0 + j))
    prev = pl.BlockSpec((HALO, CONV_COLS), lambda j, r: (jnp.maximum(r * hb - 1, 0), cb0 + j))
    out = pl.BlockSpec((CONV_FWD_ROWS, CONV_COLS), lambda j, r: (r, j))

    def body(u_ref, up_ref, w_ref, b_ref, pre_ref, act_ref):
        r = pl.program_id(1)
        for c in range(CONV_FWD_ROWS // CONV_SUB_ROWS):
            rows = slice(c * CONV_SUB_ROWS, (c + 1) * CONV_SUB_ROWS)
            u = u_ref[rows, :]
            halo = u_ref[c * CONV_SUB_ROWS - HALO:c * CONV_SUB_ROWS, :] if c > 0 else jnp.where(r > 0, up_ref[...], 0.0)
            acc = b_ref[...] + w_ref[CONV_K - 1:CONV_K, :] * u
            for k in range(1, CONV_K):
                acc = acc + w_ref[CONV_K - 1 - k:CONV_K - k, :] * _shift_down(u, halo, k)
            pre_ref[rows, :] = acc
            act_ref[rows, :] = acc * _sigmoid(acc)

    return pl.pallas_call(
        body, name="conv_fwd", grid=(CONV_CH // CONV_COLS, nr),
        in_specs=[cur, prev, pl.BlockSpec((CONV_K, CONV_COLS), lambda j, r: (0, j)),
                  pl.BlockSpec((1, CONV_COLS), lambda j, r: (0, j))],
        out_specs=[out, out],
        out_shape=[jax.ShapeDtypeStruct((s, CONV_CH), F32), jax.ShapeDtypeStruct((s, CONV_CH), F32)],
        compiler_params=_cparams(("parallel", "arbitrary")))(proj, proj, conv_w, conv_b)


def _conv_bwd(dact, pre, proj, conv_w):
    s = proj.shape[0]
    nr = s // CONV_BWD_ROWS
    cb0 = OFF_XBC // CONV_COLS
    hb = CONV_BWD_ROWS // HALO
    last_halo = s // HALO - 1
    n_sub = CONV_BWD_ROWS // CONV_SUB_ROWS
    cur = pl.BlockSpec((CONV_BWD_ROWS, CONV_COLS), lambda j, r: (r, j))
    nxt = pl.BlockSpec((HALO, CONV_COLS), lambda j, r: (jnp.minimum((r + 1) * hb, last_halo), j))
    ucur = pl.BlockSpec((CONV_BWD_ROWS, CONV_COLS), lambda j, r: (r, cb0 + j))
    wspec = pl.BlockSpec((CONV_K, CONV_COLS), lambda j, r: (0, j))
    bspec = pl.BlockSpec((1, CONV_COLS), lambda j, r: (0, j))

    def dsilu(p):
        sg = _sigmoid(p)
        return sg * (1.0 + p * (1.0 - sg))

    def body(da_ref, dan_ref, pre_ref, pren_ref, u_ref, w_ref, du_ref, dw_ref, db_ref):
        r = pl.program_id(1)

        @pl.when(r == 0)
        def _():
            dw_ref[...] = jnp.zeros_like(dw_ref)
            db_ref[...] = jnp.zeros_like(db_ref)

        dws = [jnp.zeros((1, CONV_COLS), F32) for _ in range(CONV_K)]
        db = jnp.zeros((1, CONV_COLS), F32)
        for c in range(n_sub):
            rows = slice(c * CONV_SUB_ROWS, (c + 1) * CONV_SUB_ROWS)
            ahead = slice((c + 1) * CONV_SUB_ROWS, (c + 1) * CONV_SUB_ROWS + HALO)
            dpre = da_ref[rows, :] * dsilu(pre_ref[rows, :])
            if c < n_sub - 1:
                dnext = da_ref[ahead, :] * dsilu(pre_ref[ahead, :])
            else:
                dnext = jnp.where(r < nr - 1, dan_ref[...] * dsilu(pren_ref[...]), 0.0)
            u = u_ref[rows, :]
            du = w_ref[CONV_K - 1:CONV_K, :] * dpre
            dws[0] = dws[0] + jnp.sum(dpre * u, axis=0, keepdims=True)
            for k in range(1, CONV_K):
                ahead_k = _shift_up(dpre, dnext, k)
                du = du + w_ref[CONV_K - 1 - k:CONV_K - k, :] * ahead_k
                dws[k] = dws[k] + jnp.sum(ahead_k * u, axis=0, keepdims=True)
            du_ref[rows, :] = du.astype(BF16)
            db = db + jnp.sum(dpre, axis=0, keepdims=True)
        dw_ref[...] += jnp.concatenate(dws[::-1], axis=0)
        db_ref[...] += db

    return pl.pallas_call(
        body, name="conv_bwd", grid=(CONV_CH // CONV_COLS, nr),
        in_specs=[cur, nxt, cur, nxt, ucur, wspec],
        out_specs=[cur, wspec, bspec],
        out_shape=[jax.ShapeDtypeStruct((s, CONV_CH), BF16), jax.ShapeDtypeStruct((CONV_K, CONV_CH), F32),
                   jax.ShapeDtypeStruct((1, CONV_CH), F32)],
        compiler_params=_cparams(("parallel", "arbitrary")))(dact, dact, pre, pre, proj, conv_w)


def _ssd_common(dtr, dtb, alog):
    lane = lax.broadcasted_iota(jnp.int32, (1, LANE), 1)
    head_lane = lane < SSD_HEADS
    dt = jnp.where(head_lane, _softplus(dtr + dtb), 0.0)
    a = jnp.where(head_lane, -jnp.exp(alog), 0.0)
    row = lax.broadcasted_iota(jnp.int32, (SSD_CHUNK, SSD_CHUNK), 0)
    col = lax.broadcasted_iota(jnp.int32, (SSD_CHUNK, SSD_CHUNK), 1)
    tril = row >= col
    cs = _dot(tril.astype(F32), dt * a, NN, precision=HIGHEST)
    return dt, a, cs, cs.T, tril, lane


def _split_bf16(v, passes):
    terms, rest = [], v
    for _ in range(passes):
        t = rest.astype(BF16)
        terms.append(t)
        rest = rest - t.astype(F32)
    return terms


def _dot_split(v, m, dims, passes):
    terms = _split_bf16(v, passes)
    if passes == 1:
        return _dot(terms[0], m, dims)
    return _dot(jnp.concatenate(terms, axis=1), jnp.concatenate([m] * passes, axis=0 if dims == NN else 1), dims)


def _ssd_constants():
    heads = jnp.arange(LANE)[:, None]
    exp_mat = (heads == (jnp.arange(SSD_D_INNER)[None, :] // HEAD_DIM)).astype(BF16)
    ind4 = ((jnp.arange(SSD_HEADS * SSD_CHUNK)[:, None] // SSD_CHUNK) == jnp.arange(LANE)[None, :]).astype(BF16)
    return exp_mat, ind4


def _expand_heads(v):
    return jnp.repeat(v[:, :SSD_HEADS], HEAD_DIM, axis=1)


def _ssd_prep(dtr, dtb, alog, exp_mat):
    dt, a, cs, cst, tril, lane = _ssd_common(dtr, dtb, alog)
    return dt, a, cs, cst, tril, lane, _dot_split(dt, exp_mat, NN, 2), _dot_split(cs, exp_mat, NN, 3)


def _chunk_decay_rows(cs, g):
    parts = []
    for e in range(HEADS_PER_GROUP):
        h = g * HEADS_PER_GROUP + e
        parts.append(jnp.broadcast_to(jnp.exp(cs[SSD_CHUNK - 1:SSD_CHUNK, h:h + 1]), (HEAD_DIM, SSD_STATE)))
    return jnp.concatenate(parts, axis=0)


def _ssd_fwd(proj, act, dtb, alog, dsk, nw):
    s = proj.shape[0]
    nc = s // SSD_CHUNK
    bc_w = SSD_GROUPS * SSD_STATE
    exp_mat, _ = _ssd_constants()

    def body(z_ref, dtr_ref, xs_ref, b_ref, c_ref, dtb_ref, alog_ref, dskx_ref, nw_ref, exp_ref,
             ypre_ref, yssd_ref, hall_ref, h_scr):
        @pl.when(pl.program_id(0) == 0)
        def _():
            h_scr[...] = jnp.zeros_like(h_scr)

        dt, a, cs, cst, tril, lane, dtx, csx = _ssd_prep(dtr_ref[...], dtb_ref[...], alog_ref[...], exp_ref[...])
        cs_last_x = csx[SSD_CHUNK - 1:SSD_CHUNK, :]
        xs = xs_ref[...]
        xdt = xs * dtx
        xdtb = xdt.astype(BF16)
        xdec = (xdt * jnp.exp(cs_last_x - csx)).astype(BF16)
        ecsx = jnp.exp(csx)
        head_of_lane = lax.broadcasted_iota(jnp.int32, (1, GROUP_WIDTH), 1) // HEAD_DIM
        for g in range(SSD_GROUPS):
            gs = slice(g * GROUP_WIDTH, (g + 1) * GROUP_WIDTH)
            bg = b_ref[:, g * SSD_STATE:(g + 1) * SSD_STATE].astype(BF16)
            cg = c_ref[:, g * SSD_STATE:(g + 1) * SSD_STATE].astype(BF16)
            cb = _dot(cg, bg, NT)
            hprev = h_scr[gs, :]
            hall_ref[0, gs, :] = hprev
            gms, rhs = [], []
            xg = xdtb[:, gs]
            for e in range(HEADS_PER_GROUP):
                h = g * HEADS_PER_GROUP + e
                lm = jnp.exp(jnp.where(tril, cs[:, h:h + 1] - cst[h:h + 1, :], -1e30))
                gms.append((cb * lm).astype(BF16))
                rhs.append(jnp.where(head_of_lane == e, xg, jnp.zeros_like(xg)))
            y = _dot(jnp.concatenate(gms, axis=1), jnp.concatenate(rhs, axis=0), NN)
            y = y + ecsx[:, gs] * _dot(cg, hprev.astype(BF16), NT)
            y = y + dskx_ref[:, gs] * xs[:, gs]
            h_scr[gs, :] = hprev * _chunk_decay_rows(cs, g) + _dot(xdec[:, gs], bg, TN)
            ypre_ref[:, gs] = y
            z = z_ref[:, gs]
            yg = y * (z * _sigmoid(z))
            r = lax.rsqrt(jnp.mean(yg * yg, axis=-1, keepdims=True) + EPS)
            yssd_ref[:, gs] = (yg * r * nw_ref[:, gs]).astype(BF16)

    row_d = lambda cb: pl.BlockSpec((SSD_CHUNK, SSD_D_INNER), lambda c: (c, cb))
    small = pl.BlockSpec((1, LANE), lambda c: (0, 0))
    wide = pl.BlockSpec((1, SSD_D_INNER), lambda c: (0, 0))
    return pl.pallas_call(
        body, name="ssd_fwd", grid=(nc,),
        in_specs=[row_d(OFF_Z // SSD_D_INNER),
                  pl.BlockSpec((SSD_CHUNK, LANE), lambda c: (c, OFF_DT // LANE)),
                  row_d(0),
                  pl.BlockSpec((SSD_CHUNK, bc_w), lambda c: (c, SSD_D_INNER // bc_w)),
                  pl.BlockSpec((SSD_CHUNK, bc_w), lambda c: (c, SSD_D_INNER // bc_w + 1)),
                  small, small, wide, wide, pl.BlockSpec((LANE, SSD_D_INNER), lambda c: (0, 0))],
        out_specs=[row_d(0), row_d(0), pl.BlockSpec((1, SSD_D_INNER, SSD_STATE), lambda c: (c, 0, 0))],
        out_shape=[jax.ShapeDtypeStruct((s, SSD_D_INNER), F32), jax.ShapeDtypeStruct((s, SSD_D_INNER + ATT_D), BF16),
                   jax.ShapeDtypeStruct((nc, SSD_D_INNER, SSD_STATE), F32)],
        scratch_shapes=[pltpu.VMEM((SSD_D_INNER, SSD_STATE), F32)],
        compiler_params=_cparams(("arbitrary",)))(proj, proj, act, act, act, dtb, alog, _expand_heads(dsk), nw, exp_mat)


def _ssd_bwd(dycat, ypre, proj, act, hall, dtb, alog, dsk, nw, comm=None):
    s = proj.shape[0]
    nc = s // SSD_CHUNK
    bc_w = SSD_GROUPS * SSD_STATE

    exp_mat, ind4 = _ssd_constants()
    seg_passes = 1

    def body(dy_ref, ypre_ref, z_ref, dtr_ref, xs_ref, b_ref, c_ref, hall_ref, dtb_ref, alog_ref, dskx_ref, nw_ref,
             exp_ref, ind4_ref, dz_ref, dact_ref, ddtr_ref, da_ref, ddsk_ref, ddtb_ref, dnw_ref, dh_scr):
        @pl.when(pl.program_id(0) == 0)
        def _():
            dh_scr[...] = jnp.zeros_like(dh_scr)
            da_ref[...] = jnp.zeros_like(da_ref)
            ddsk_ref[...] = jnp.zeros_like(ddsk_ref)
            ddtb_ref[...] = jnp.zeros_like(ddtb_ref)
            dnw_ref[...] = jnp.zeros_like(dnw_ref)

        dtr = dtr_ref[...]
        dt, a, cs, cst, tril, lane, dtx, csx = _ssd_prep(dtr, dtb_ref[...], alog_ref[...], exp_ref[...])
        cs_last_x = csx[SSD_CHUNK - 1:SSD_CHUNK, :]
        xs = xs_ref[...]
        xdt = xs * dtx
        xdtb = xdt.astype(BF16)
        decx = jnp.exp(cs_last_x - csx)
        xdecf = xdt * decx
        xdec = xdecf.astype(BF16)
        ecsx = jnp.exp(csx)
        head_of_lane = lax.broadcasted_iota(jnp.int32, (1, GROUP_WIDTH), 1) // HEAD_DIM
        last_row = lax.broadcasted_iota(jnp.int32, (SSD_CHUNK, 1), 0) == SSD_CHUNK - 1
        dcs_col = jnp.zeros((SSD_CHUNK, LANE), F32)
        dcs_row = jnp.zeros((SSD_CHUNK, LANE), F32)
        ddt = jnp.zeros((SSD_CHUNK, LANE), F32)
        ddsk = jnp.zeros((1, LANE), F32)
        hsum = jnp.zeros((1, LANE), F32)
        t1_sum = jnp.zeros((1, LANE), F32)
        for g in range(SSD_GROUPS):
            gs = slice(g * GROUP_WIDTH, (g + 1) * GROUP_WIDTH)
            bsl = slice(g * SSD_STATE, (g + 1) * SSD_STATE)
            exp_g = exp_ref[:, gs]
            ind4_g = ind4_ref[g * HEADS_PER_GROUP * SSD_CHUNK:(g + 1) * HEADS_PER_GROUP * SSD_CHUNK, :]
            z = z_ref[:, gs]
            sg = _sigmoid(z)
            sz = z * sg
            ypre = ypre_ref[:, gs]
            yg = ypre * sz
            r = lax.rsqrt(jnp.mean(yg * yg, axis=-1, keepdims=True) + EPS)
            nrm = yg * r
            dyo_n = dy_ref[:, gs]
            dnw_ref[:, gs] += jnp.sum(dyo_n * nrm, axis=0, keepdims=True)
            dn = dyo_n * nw_ref[:, gs]
            dyg = r * (dn - nrm * jnp.mean(dn * nrm, axis=-1, keepdims=True))
            dz_ref[:, gs] = (dyg * ypre * (sg * (1.0 + z * (1.0 - sg)))).astype(BF16)
            dy = dyg * sz

            bg = b_ref[:, bsl].astype(BF16)
            cg = c_ref[:, bsl].astype(BF16)
            cb = _dot(cg, bg, NT)
            hprev = hall_ref[0, gs, :]
            hb = hprev.astype(BF16)
            dhn = dh_scr[gs, :]
            dhb = dhn.astype(BF16)
            xs_g, xdt_g = xs[:, gs], xdtb[:, gs]
            w_off = _dot(cg, hb, NT)
            dyo = dy * ecsx[:, gs]
            dyob = dyo.astype(BF16)
            dcg = _dot(dyob, hb, NN)
            dh_y = _dot(dyob, cg, TN)
            r_st = _dot(bg, dhb, NT)
            dbg = _dot(xdec[:, gs], dhb, NN)
            dyb = dy.astype(BF16)
            gms, gmbs, lms, dys = [], [], [], []
            for e in range(HEADS_PER_GROUP):
                h = g * HEADS_PER_GROUP + e
                lm = jnp.exp(jnp.where(tril, cs[:, h:h + 1] - cst[h:h + 1, :], -1e30))
                gm = cb * lm
                lms.append(lm)
                gms.append(gm)
                gmbs.append(gm.astype(BF16))
                dys.append(jnp.where(head_of_lane == e, dyb, jnp.zeros_like(dyb)))
            dxdt = _dot(jnp.concatenate(gmbs, axis=0), jnp.concatenate(dys, axis=0), TN) + decx[:, gs] * r_st
            dcb = jnp.zeros((SSD_CHUNK, SSD_CHUNK), F32)
            mms = []
            for e in range(HEADS_PER_GROUP):
                dg = _dot(dys[e], xdt_g, NT)
                mms.append(dg * gms[e])
                dcb = dcb + dg * lms[e]
            seg = _dot_split(jnp.concatenate([dyo * w_off, xdecf[:, gs] * r_st, dxdt * xs_g, dy * xs_g], axis=0), exp_g, NT, seg_passes)
            v1, t1, ddt_g, dsk_g = [seg[i * SSD_CHUNK:(i + 1) * SSD_CHUNK] for i in range(4)]
            dcs_col = dcs_col + v1 - t1 + _dot_split(jnp.concatenate(mms, axis=1), ind4_g, NN, seg_passes)
            for t in _split_bf16(jnp.concatenate(mms, axis=0), seg_passes):
                dcs_row = dcs_row + _dot(ind4_g, t, TN)
            ddt = ddt + ddt_g
            ddsk = ddsk + jnp.sum(dsk_g, axis=0, keepdims=True)
            t1_sum = t1_sum + jnp.sum(t1, axis=0, keepdims=True)
            for e in range(HEADS_PER_GROUP):
                h = g * HEADS_PER_GROUP + e
                hs = slice(e * HEAD_DIM, (e + 1) * HEAD_DIM)
                hsum = hsum + jnp.where(lane == h, jnp.sum(dhn[hs, :] * hprev[hs, :]).reshape(1, 1), 0.0)
            dh_scr[gs, :] = dhn * _chunk_decay_rows(cs, g) + dh_y
            dcbb = dcb.astype(BF16)
            dact_ref[:, gs] = dxdt * dtx[:, gs] + dskx_ref[:, gs] * dy
            dact_ref[:, SSD_D_INNER + g * SSD_STATE:SSD_D_INNER + (g + 1) * SSD_STATE] = dbg + _dot(dcbb, cg, TN)
            dact_ref[:, SSD_D_INNER + bc_w + g * SSD_STATE:SSD_D_INNER + bc_w + (g + 1) * SSD_STATE] = dcg + _dot(dcbb, bg, NN)
        dlast = t1_sum + jnp.exp(cs[SSD_CHUNK - 1:SSD_CHUNK, :]) * hsum
        dcs = dcs_col - dcs_row.T + jnp.where(last_row, dlast, 0.0)
        row = lax.broadcasted_iota(jnp.int32, (SSD_CHUNK, SSD_CHUNK), 0)
        col = lax.broadcasted_iota(jnp.int32, (SSD_CHUNK, SSD_CHUNK), 1)
        dda = _dot((col >= row).astype(F32), dcs, NN, precision=HIGHEST)
        ddt = ddt + dda * a
        da_ref[...] += jnp.sum(dda * dt, axis=0, keepdims=True)
        ddtr = jnp.where(lane < SSD_HEADS, ddt * _sigmoid(dtr + dtb_ref[...]), 0.0)
        ddtr_ref[...] = ddtr.astype(BF16)
        ddtb_ref[...] += jnp.sum(ddtr, axis=0, keepdims=True)
        ddsk_ref[...] += ddsk

    rev = lambda c: nc - 1 - c
    row_d = lambda cb: pl.BlockSpec((SSD_CHUNK, SSD_D_INNER), lambda c: (rev(c), cb))
    small = pl.BlockSpec((1, LANE), lambda c: (0, 0))
    wide = pl.BlockSpec((1, SSD_D_INNER), lambda c: (0, 0))
    small_shape = jax.ShapeDtypeStruct((1, LANE), F32)
    return _pcall(
        body, (dycat, ypre, proj, proj, act, act, act, hall, dtb, alog, _expand_heads(dsk), nw, exp_mat, ind4),
        name="ssd_bwd", grid=(nc,),
        in_specs=[row_d(0), row_d(0), row_d(OFF_Z // SSD_D_INNER),
                  pl.BlockSpec((SSD_CHUNK, LANE), lambda c: (rev(c), OFF_DT // LANE)),
                  row_d(0),
                  pl.BlockSpec((SSD_CHUNK, bc_w), lambda c: (rev(c), SSD_D_INNER // bc_w)),
                  pl.BlockSpec((SSD_CHUNK, bc_w), lambda c: (rev(c), SSD_D_INNER // bc_w + 1)),
                  pl.BlockSpec((1, SSD_D_INNER, SSD_STATE), lambda c: (rev(c), 0, 0)),
                  small, small, wide, wide, pl.BlockSpec((LANE, SSD_D_INNER), lambda c: (0, 0)),
                  pl.BlockSpec((SSD_HEADS * SSD_CHUNK, LANE), lambda c: (0, 0))],
        out_specs=[row_d(0), pl.BlockSpec((SSD_CHUNK, CONV_CH), lambda c: (rev(c), 0)),
                   pl.BlockSpec((SSD_CHUNK, LANE), lambda c: (rev(c), 0)), small, small, small, wide],
        out_shape=[jax.ShapeDtypeStruct((s, SSD_D_INNER), BF16), jax.ShapeDtypeStruct((s, CONV_CH), F32),
                   jax.ShapeDtypeStruct((s, LANE), BF16), small_shape, small_shape, small_shape,
                   jax.ShapeDtypeStruct((1, SSD_D_INNER), F32)],
        scratch_shapes=[pltpu.VMEM((SSD_D_INNER, SSD_STATE), F32)], sem=("arbitrary",), comm=comm)


def _head_mean_matrix():
    row = lax.broadcasted_iota(jnp.int32, (LANE, LANE), 0) // HEAD_DIM
    col = lax.broadcasted_iota(jnp.int32, (LANE, LANE), 1) // HEAD_DIM
    return (row == col).astype(F32)


def _head_sum2(v, ones_bd):
    hi = v.astype(BF16)
    lo = (v - hi.astype(F32)).astype(BF16)
    return _dot(jnp.concatenate([hi, lo], axis=1), jnp.concatenate([ones_bd, ones_bd], axis=0), NN)


def _head_norms(xs, ws, ones_bd):
    sums = [_head_sum2(x * x, ones_bd) for x in xs]
    rs = [lax.rsqrt(ms * (1.0 / HEAD_DIM) + EPS) for ms in sums]
    return [(x * r) * w for x, r, w in zip(xs, rs, ws)], rs


def _head_norms_bwd(dns, xs, ws, rs, ones_bd):
    nrms = [x * r for x, r in zip(xs, rs)]
    dnws = [dn * w for dn, w in zip(dns, ws)]
    projs = [_head_sum2(dnw * nrm, ones_bd) for dnw, nrm in zip(dnws, nrms)]
    dxs = [r * (dnw - nrm * (pr * (1.0 / HEAD_DIM))) for r, dnw, nrm, pr in zip(rs, dnws, nrms, projs)]
    return dxs, [jnp.sum(dn * nrm, axis=0, keepdims=True) for dn, nrm in zip(dns, nrms)]


NORM_CHUNKS = 2


PRO_ROWS = 256
ATT_GROUP_FWD = 32
ATT_GROUP_BWD = 8
KEYS = 2 * ATT_BLK
NEG = -1e30
HALF = HEAD_DIM // 2


def _rows(start, size, dil):
    return pl.ds(start, size) if dil == 1 else pl.ds(start, size, stride=dil)


def _fill_bias(bias_ref):
    row = lax.broadcasted_iota(jnp.int32, (ATT_BLK, 2 * KEYS), 0)
    col = lax.broadcasted_iota(jnp.int32, (ATT_BLK, 2 * KEYS), 1) & (KEYS - 1)
    for first, off in ((0, 0), (1, ATT_BLK)):
        dist = off + row - col
        bias_ref[first] = jnp.where((dist >= 0) & (dist <= ATT_BLK), 0.0, NEG)


def _pair(a, b):
    return jnp.concatenate([jnp.broadcast_to(a, (ATT_BLK, KEYS)), jnp.broadcast_to(b, (ATT_BLK, KEYS))], axis=1)


def _split_heads(x, is_a):
    zero = jnp.zeros_like(x)
    return jnp.concatenate([jnp.where(is_a, x, zero), jnp.where(is_a, zero, x)], axis=0)


def _block_ids(b, nb):
    i = b & (nb - 1)
    q0 = pl.multiple_of(b * ATT_BLK, ATT_BLK)
    k0 = pl.multiple_of((b - jnp.minimum(i, 1)) * ATT_BLK, ATT_BLK)
    return pl.ds(q0, ATT_BLK), pl.ds(k0, KEYS), jnp.minimum(i, 1)


def _natural_rows(b, nb, dil):
    if dil == 1:
        return pl.ds(pl.multiple_of(b * ATT_BLK, ATT_BLK), ATT_BLK)
    return pl.ds(b // nb + dil * ((b & (nb - 1)) * ATT_BLK), ATT_BLK, stride=dil)


def _att_fwd(proj, qw, kw, comm=None):
    s = proj.shape[0]
    nblk = s // ATT_BLK
    assert all((s // d) // ATT_BLK >= 2 for d in DILATIONS)
    blk = lambda off: pl.BlockSpec((s, LANE), lambda i: (0, off // LANE + i))
    wspec = pl.BlockSpec((1, LANE), lambda i: (0, i))
    oblk = pl.BlockSpec((s, LANE), lambda i: (0, i))

    def body(q_ref, k_ref, v_ref, qw_ref, kw_ref, o_ref, lse_ref, qn, kn, q_cm, k_cm, v_cm, m_acc, l_acc, o_d, m_d, l_d, bias):
        ones_bd = _head_mean_matrix().astype(BF16)
        is_a = lax.broadcasted_iota(jnp.int32, (1, LANE), 1) < HEAD_DIM
        ones_ext = _split_heads(jnp.ones((KEYS, LANE), BF16), is_a)
        _fill_bias(bias)

        def pro(j, c):
            chunks = [pl.ds(pl.multiple_of((NORM_CHUNKS * j + u) * PRO_ROWS, PRO_ROWS), PRO_ROWS) for u in range(NORM_CHUNKS)]
            normed, _ = _head_norms([q_ref[rows, :] for rows in chunks] + [k_ref[rows, :] for rows in chunks],
                                    [qw_ref[...] * HEAD_DIM ** -0.5] * NORM_CHUNKS + [kw_ref[...]] * NORM_CHUNKS, ones_bd)
            for u, rows in enumerate(chunks):
                qn[rows, :] = normed[u]
                kn[rows, :] = normed[NORM_CHUNKS + u]
            return c

        lax.fori_loop(0, s // (NORM_CHUNKS * PRO_ROWS), pro, 0)

        for dil in DILATIONS:
            ln = s // dil
            nb = ln // ATT_BLK
            o_out, m_out, l_out = (o_ref, m_acc, l_acc) if dil == 1 else (o_d, m_d, l_d)
            for r in range(dil):
                def relayout(j, c, dil=dil, r=r, ln=ln):
                    j0 = pl.multiple_of(j * PRO_ROWS, PRO_ROWS)
                    src = _rows(r + dil * j0, PRO_ROWS, dil)
                    dst = pl.ds(r * ln + j0, PRO_ROWS)
                    q_cm[dst, :] = qn[src, :].astype(BF16)
                    k_cm[dst, :] = kn[src, :].astype(BF16)
                    v_cm[dst, :] = v_ref[src, :].astype(BF16)
                    return c

                lax.fori_loop(0, ln // PRO_ROWS, relayout, 0)

            def step(bg, c, nb=nb, o_out=o_out, m_out=m_out, l_out=l_out):
                ids = [_block_ids(bg * ATT_GROUP_FWD + u, nb) for u in range(ATT_GROUP_FWD)]
                kbs = [_split_heads(k_cm[krows, :], is_a) for _, krows, _ in ids]
                scs = [_dot(q_cm[qrows, :], kb, NT) + bias[first] for (qrows, _, first), kb in zip(ids, kbs)]
                mas = [jnp.max(sc[:, :KEYS], axis=-1, keepdims=True) for sc in scs]
                mbs = [jnp.max(sc[:, KEYS:], axis=-1, keepdims=True) for sc in scs]
                ps = [jnp.exp(sc - _pair(ma, mb)).astype(BF16) for sc, ma, mb in zip(scs, mas, mbs)]
                vbs = [jnp.concatenate([_split_heads(v_cm[krows, :], is_a), ones_ext], axis=1) for _, krows, _ in ids]
                ols = [_dot(p, vb, NN) for p, vb in zip(ps, vbs)]
                for (qrows, _, _), ol, ma, mb in zip(ids, ols, mas, mbs):
                    o_out[qrows, :] = ol[:, :LANE]
                    l_out[qrows, :] = ol[:, LANE:]
                    m_out[qrows, :] = jnp.where(is_a, ma, mb)
                return c

            lax.fori_loop(0, nblk // ATT_GROUP_FWD, step, 0)

            if dil > 1:
                for r in range(dil):
                    def merge(j, c, dil=dil, r=r, ln=ln):
                        j0 = pl.multiple_of(j * PRO_ROWS, PRO_ROWS)
                        nat = _rows(r + dil * j0, PRO_ROWS, dil)
                        cm = pl.ds(r * ln + j0, PRO_ROWS)
                        m_old, m_new = m_acc[nat, :], m_d[cm, :]
                        m = jnp.maximum(m_old, m_new)
                        a_old, a_new = jnp.exp(m_old - m), jnp.exp(m_new - m)
                        o_ref[nat, :] = a_old * o_ref[nat, :] + a_new * o_d[cm, :]
                        l_acc[nat, :] = a_old * l_acc[nat, :] + a_new * l_d[cm, :]
                        m_acc[nat, :] = m
                        return c

                    lax.fori_loop(0, ln // PRO_ROWS, merge, 0)

        def epi(j, c):
            rows = pl.ds(pl.multiple_of(j * PRO_ROWS, PRO_ROWS), PRO_ROWS)
            l = l_acc[rows, :]
            o_ref[rows, :] = o_ref[rows, :] / l
            lse_ref[rows, :] = m_acc[rows, :] + jnp.log(l)
            return c

        lax.fori_loop(0, s // PRO_ROWS, epi, 0)

    f = jax.ShapeDtypeStruct((s, ATT_D), F32)
    scr = pltpu.VMEM((s, LANE), F32)
    scb = pltpu.VMEM((s, LANE), BF16)
    return _pcall(
        body, (proj, proj, proj, qw, kw), name="att_fwd", grid=(ATT_D // LANE,),
        in_specs=[blk(OFF_Q), blk(OFF_K), blk(OFF_V), wspec, wspec], out_specs=[oblk, oblk], out_shape=[f, f],
        scratch_shapes=[scr, scr, scb, scb, scb, scr, scr, scr, scr, scr, pltpu.VMEM((2, ATT_BLK, 2 * KEYS), F32)],
        sem=("parallel",), comm=comm)


def _att_bwd(proj, do, stats, qw, kw, comm=None):
    s = proj.shape[0]
    nblk = s // ATT_BLK
    blk = lambda off: pl.BlockSpec((s, LANE), lambda i: (0, off // LANE + i))
    wspec = pl.BlockSpec((1, LANE), lambda i: (0, i))
    oblk = pl.BlockSpec((s, LANE), lambda i: (0, i))

    def body(q_ref, k_ref, v_ref, do_ref, st_ref, qw_ref, kw_ref, dq_ref, dk_ref, dv_ref, dqw_ref, dkw_ref,
             qn, kn, q_cm, do_cm, k_cm, v_cm, rms, dq_acc, dk_acc, dv_acc, dq_d, dk_d, dv_d, bias):
        ones_bd = _head_mean_matrix().astype(BF16)
        is_a = lax.broadcasted_iota(jnp.int32, (1, LANE), 1) < HEAD_DIM
        first_half = (lax.broadcasted_iota(jnp.int32, (1, LANE), 1) & (HEAD_DIM - 1)) < HALF
        _fill_bias(bias)
        zero = jnp.zeros((PRO_ROWS, LANE), F32)

        def pro(j, c):
            chunks = [pl.ds(pl.multiple_of((NORM_CHUNKS * j + u) * PRO_ROWS, PRO_ROWS), PRO_ROWS) for u in range(NORM_CHUNKS)]
            normed, rs = _head_norms([q_ref[rows, :] for rows in chunks] + [k_ref[rows, :] for rows in chunks],
                                     [qw_ref[...] * HEAD_DIM ** -0.5] * NORM_CHUNKS + [kw_ref[...]] * NORM_CHUNKS, ones_bd)
            for u, rows in enumerate(chunks):
                qn[rows, :] = normed[u]
                kn[rows, :] = normed[NORM_CHUNKS + u]
                rms[rows, :] = jnp.where(first_half, rs[u], rs[NORM_CHUNKS + u])
                dk_acc[rows, :] = zero
                dv_acc[rows, :] = zero
            return c

        lax.fori_loop(0, s // (NORM_CHUNKS * PRO_ROWS), pro, 0)

        for dil in DILATIONS:
            ln = s // dil
            nb = ln // ATT_BLK
            dq_o, dk_o, dv_o = (dq_acc, dk_acc, dv_acc) if dil == 1 else (dq_d, dk_d, dv_d)
            for r in range(dil):
                def relayout(j, c, dil=dil, r=r, ln=ln):
                    j0 = pl.multiple_of(j * PRO_ROWS, PRO_ROWS)
                    src = _rows(r + dil * j0, PRO_ROWS, dil)
                    dst = pl.ds(r * ln + j0, PRO_ROWS)
                    q_cm[dst, :] = qn[src, :].astype(BF16)
                    k_cm[dst, :] = kn[src, :].astype(BF16)
                    v_cm[dst, :] = v_ref[src, :].astype(BF16)
                    do_cm[dst, :] = do_ref[src, :].astype(BF16)
                    if dil > 1:
                        dk_d[dst, :] = zero
                        dv_d[dst, :] = zero
                    return c

                lax.fori_loop(0, ln // PRO_ROWS, relayout, 0)

            def step(bg, c, nb=nb, dil=dil, dq_o=dq_o, dk_o=dk_o, dv_o=dv_o):
                blocks = [bg * ATT_GROUP_BWD + u for u in range(ATT_GROUP_BWD)]
                ids = [_block_ids(b, nb) for b in blocks]
                qbs = [q_cm[qrows, :] for qrows, _, _ in ids]
                dobs = [do_cm[qrows, :] for qrows, _, _ in ids]
                kbs = [_split_heads(k_cm[krows, :], is_a) for _, krows, _ in ids]
                vbs = [_split_heads(v_cm[krows, :], is_a) for _, krows, _ in ids]
                sts = [st_ref[_natural_rows(b, nb, dil), :] for b in blocks]
                scs = [_dot(qb, kb, NT) + bias[first] for qb, kb, (_, _, first) in zip(qbs, kbs, ids)]
                dps = [_dot(dob, vb, NT) for dob, vb in zip(dobs, vbs)]
                ps = [jnp.exp(sc - _pair(st[:, 0:1], st[:, HEAD_DIM:HEAD_DIM + 1])) for sc, st in zip(scs, sts)]
                dss = [(p * (dp - _pair(st[:, HALF:HALF + 1], st[:, HEAD_DIM + HALF:HEAD_DIM + HALF + 1]))).astype(BF16)
                       for p, dp, st in zip(ps, dps, sts)]
                dqs = [_dot(ds, kb, NN) for ds, kb in zip(dss, kbs)]
                dkfs = [_dot(ds, qb, TN) for ds, qb in zip(dss, qbs)]
                dvfs = [_dot(p.astype(BF16), dob, TN) for p, dob in zip(ps, dobs)]
                for (qrows, krows, _), dq, dkf, dvf in zip(ids, dqs, dkfs, dvfs):
                    dq_o[qrows, :] = dq
                    dk_o[krows, :] += jnp.where(is_a, dkf[:KEYS], dkf[KEYS:])
                    dv_o[krows, :] += jnp.where(is_a, dvf[:KEYS], dvf[KEYS:])
                return c

            lax.fori_loop(0, nblk // ATT_GROUP_BWD, step, 0)

            if dil > 1:
                for r in range(dil):
                    def merge(j, c, dil=dil, r=r, ln=ln):
                        j0 = pl.multiple_of(j * PRO_ROWS, PRO_ROWS)
                        nat = _rows(r + dil * j0, PRO_ROWS, dil)
                        cm = pl.ds(r * ln + j0, PRO_ROWS)
                        dq_acc[nat, :] += dq_d[cm, :]
                        dk_acc[nat, :] += dk_d[cm, :]
                        dv_acc[nat, :] += dv_d[cm, :]
                        return c

                    lax.fori_loop(0, ln // PRO_ROWS, merge, 0)

        def epi(j, c):
            chunks = [pl.ds(pl.multiple_of((NORM_CHUNKS * j + u) * PRO_ROWS, PRO_ROWS), PRO_ROWS) for u in range(NORM_CHUNKS)]
            packed = [rms[rows, :] for rows in chunks]
            rs = ([jnp.where(first_half, p, pltpu.roll(p, HALF, axis=1)) for p in packed]
                  + [jnp.where(first_half, pltpu.roll(p, LANE - HALF, axis=1), p) for p in packed])
            dxs, dws = _head_norms_bwd(
                [dq_acc[rows, :] for rows in chunks] + [dk_acc[rows, :] for rows in chunks],
                [q_ref[rows, :] for rows in chunks] + [k_ref[rows, :] for rows in chunks],
                [qw_ref[...] * HEAD_DIM ** -0.5] * NORM_CHUNKS + [kw_ref[...]] * NORM_CHUNKS, rs, ones_bd)
            dqw, dkw = c
            for u, rows in enumerate(chunks):
                dq_ref[rows, :] = dxs[u].astype(BF16)
                dk_ref[rows, :] = dxs[NORM_CHUNKS + u].astype(BF16)
                dv_ref[rows, :] = dv_acc[rows, :].astype(BF16)
                dqw, dkw = dqw + dws[u], dkw + dws[NORM_CHUNKS + u]
            return dqw, dkw

        zrow = jnp.zeros((1, LANE), F32)
        dqw, dkw = lax.fori_loop(0, s // (NORM_CHUNKS * PRO_ROWS), epi, (zrow, zrow))
        dqw_ref[...] = dqw * HEAD_DIM ** -0.5
        dkw_ref[...] = dkw

    o = jax.ShapeDtypeStruct((s, ATT_D), BF16)
    ov = jax.ShapeDtypeStruct((1, ATT_D), F32)
    scr = pltpu.VMEM((s, LANE), F32)
    scb = pltpu.VMEM((s, LANE), BF16)
    return _pcall(
        body, (proj, proj, proj, do, stats, qw, kw), name="att_bwd", grid=(ATT_D // LANE,),
        in_specs=[blk(OFF_Q), blk(OFF_K), blk(OFF_V), oblk, oblk, wspec, wspec],
        out_specs=[oblk, oblk, oblk, wspec, wspec], out_shape=[o, o, o, ov, ov],
        scratch_shapes=[scr, scr, scb, scb, scb, scb, scr, scr, scr, scr, scr, scr, scr, pltpu.VMEM((2, ATT_BLK, 2 * KEYS), F32)],
        sem=("parallel",), comm=comm)


def _att_norm_fwd(o, nw, ycat):
    s = o.shape[0]
    row = pl.BlockSpec((ROW_TILE, ATT_D), lambda i: (i, 0))
    vec = pl.BlockSpec((1, ATT_D), lambda i: (0, 0))

    def body(o_ref, nw_ref, ycat_ref, y_ref):
        o = o_ref[...]
        r = lax.rsqrt(jnp.mean(o * o, axis=-1, keepdims=True) + EPS)
        y_ref[...] = (o * r * nw_ref[...]).astype(BF16)

    return pl.pallas_call(body, name="att_norm_fwd", grid=(s // ROW_TILE,),
                          in_specs=[row, vec, pl.BlockSpec(memory_space=pl.ANY)],
                          out_specs=pl.BlockSpec((ROW_TILE, ATT_D), lambda i: (i, 1)),
                          out_shape=jax.ShapeDtypeStruct(ycat.shape, BF16), input_output_aliases={2: 0},
                          compiler_params=_cparams(("parallel",)))(o, nw, ycat)


def _mixer_split_epilogue(dycat, first, rows, vecs, outs):
    (o_ref, lse_ref), (nw_ref,), (dyssd_ref, do_ref, st_ref, dnw_ref) = rows, vecs, outs

    @pl.when(first)
    def _():
        dnw_ref[...] = jnp.zeros_like(dnw_ref)

    dyssd_ref[...] = dycat[:, :SSD_D_INNER]
    dy = dycat[:, SSD_D_INNER:]
    o = o_ref[...]
    r = lax.rsqrt(jnp.mean(o * o, axis=-1, keepdims=True) + EPS)
    nrm = o * r
    dnw_ref[...] += jnp.sum(dy * nrm, axis=0, keepdims=True)
    dn = dy * nw_ref[...]
    do = r * (dn - nrm * jnp.mean(dn * nrm, axis=-1, keepdims=True))
    do_ref[...] = do
    ones_bd = _head_mean_matrix().astype(BF16)
    prod = do * o
    delta = jnp.concatenate([_head_sum2(prod[:, j * LANE:(j + 1) * LANE], ones_bd) for j in range(ATT_D // LANE)], axis=1)
    lane = lax.broadcasted_iota(jnp.int32, (1, ATT_D), 1)
    st_ref[...] = jnp.where((lane & (HEAD_DIM - 1)) < HALF, lse_ref[...], delta)


def _ada_fwd(c_all, w_ada):
    def body(c_ref, w_ref, o_ref):
        cv = c_ref[...]
        o_ref[...] = _dot((cv * _sigmoid(cv)).astype(BF16), w_ref[...].astype(BF16), NN)

    return pl.pallas_call(body, name="ada_fwd", out_shape=jax.ShapeDtypeStruct((c_all.shape[0], w_ada.shape[1]), F32),
                          compiler_params=_cparams())(c_all, w_ada)


def _adamw_math(g, w, m, v):
    m_new = ADAM_B1 * m + (1.0 - ADAM_B1) * g
    v_new = ADAM_B2 * v + (1.0 - ADAM_B2) * (g * g)
    m_hat = m_new / (1.0 - ADAM_B1 ** ADAM_STEP)
    v_hat = v_new / (1.0 - ADAM_B2 ** ADAM_STEP)
    delta = -ADAM_LR * (m_hat / (jnp.sqrt(v_hat) + ADAM_EPS) + ADAM_WD * w)
    return delta, m_new, v_new


def _ada_bwd_adamw(c_all, dmod_cols, w, m, v):
    rows, cols = w.shape
    tr = 256
    blk = pl.BlockSpec((tr, cols), lambda i: (i, 0))

    def body(c_ref, d_ref, w_ref, m_ref, v_ref, g_ref, dl_ref, mo_ref, vo_ref):
        cv = c_ref[...]
        ca = cv * _sigmoid(cv)
        g = ca[:, 0:1] * d_ref[0:1, :]
        for b in range(1, N_DEV):
            g = g + ca[:, b:b + 1] * d_ref[b:b + 1, :]
        g_ref[...] = g
        dl_ref[...], mo_ref[...], vo_ref[...] = _adamw_math(g, w_ref[...], m_ref[...], v_ref[...])

    o = jax.ShapeDtypeStruct((rows, cols), F32)
    return pl.pallas_call(
        body, name="ada_bwd_adamw", grid=(rows // tr,),
        in_specs=[pl.BlockSpec((tr, N_DEV), lambda i: (i, 0)), pl.BlockSpec((N_DEV, cols), lambda i: (0, 0)), blk, blk, blk],
        out_specs=[blk] * 4, out_shape=[o, o, o, o], compiler_params=_cparams(("parallel",)))(c_all.T, dmod_cols, w, m, v)


def _reduce_adamw(slabs, w, m, v, name):
    rows, cols = w.shape
    n_src = slabs.shape[0]
    if rows % 128 == 0:
        tr, steps = 128, rows // 128
        blk = pl.BlockSpec((tr, cols), lambda i: (i, 0))
        sblk = pl.BlockSpec((n_src, tr, cols), lambda i: (0, i, 0))
    else:
        tc, steps = 256, cols // 256
        blk = pl.BlockSpec((rows, tc), lambda i: (0, i))
        sblk = pl.BlockSpec((n_src, rows, tc), lambda i: (0, 0, i))

    def body(s_ref, w_ref, m_ref, v_ref, g_ref, dl_ref, mo_ref, vo_ref):
        g = s_ref[0].astype(F32)
        for src in range(1, n_src):
            g = g + s_ref[src].astype(F32)
        g_ref[...] = g
        dl_ref[...], mo_ref[...], vo_ref[...] = _adamw_math(g, w_ref[...], m_ref[...], v_ref[...])

    o = jax.ShapeDtypeStruct((rows, cols), F32)
    return pl.pallas_call(
        body, name=name, grid=(steps,), in_specs=[sblk, blk, blk, blk],
        out_specs=[blk] * 4, out_shape=[o, o, o, o], compiler_params=_cparams(("parallel",)))(slabs, w, m, v)


def _small_reduce_adamw(gathered, w, m, v):
    def body(s_ref, w_ref, m_ref, v_ref, g_ref, dl_ref, mo_ref, vo_ref):
        g = s_ref[0]
        for dev in range(1, N_DEV):
            g = g + s_ref[dev]
        g_ref[...] = g
        dl_ref[...], mo_ref[...], vo_ref[...] = _adamw_math(g, w_ref[...], m_ref[...], v_ref[...])

    o = jax.ShapeDtypeStruct(w.shape, F32)
    return pl.pallas_call(body, name="small_reduce_adamw", out_shape=[o, o, o, o], compiler_params=_cparams())(gathered, w, m, v)


def _adamw_small(g, w, m, v, name):
    def body(g_ref, w_ref, m_ref, v_ref, dl_ref, mo_ref, vo_ref):
        dl_ref[...], mo_ref[...], vo_ref[...] = _adamw_math(g_ref[...], w_ref[...], m_ref[...], v_ref[...])

    o = jax.ShapeDtypeStruct(w.shape, F32)
    return pl.pallas_call(body, name=name, out_shape=[o, o, o], compiler_params=_cparams())(g, w, m, v)


class _Exchange:
    def __init__(self, arrs, scatter):
        self.arrs, self.scatter, self.n = list(arrs), scatter, len(arrs)
        hbm = pl.BlockSpec(memory_space=pltpu.HBM)
        self.in_specs = [hbm] * self.n
        self.out_specs = [hbm] * self.n
        self.out_shape = [jax.ShapeDtypeStruct(a.shape if scatter else (N_DEV,) + a.shape, a.dtype) for a in self.arrs]
        self.scratch = [pltpu.SemaphoreType.DMA((self.n * (N_DEV - 1),)), pltpu.SemaphoreType.DMA((self.n * (N_DEV - 1),)),
                        pltpu.SemaphoreType.DMA((self.n,))]

    def _local(self, ins, outs, sems):
        me = 4 * lax.axis_index("x") + 2 * lax.axis_index("y") + lax.axis_index("c")
        return [pltpu.make_async_copy(ins[a].at[me] if self.scatter else ins[a], outs[a].at[me], sems[2].at[a])
                for a in range(self.n)]

    def _remote(self, ins, outs, sems, arriving):
        send_sems, recv_sems, _ = sems
        x, y, c = lax.axis_index("x"), lax.axis_index("y"), lax.axis_index("c")
        me = 4 * x + 2 * y + c
        remote = []
        for a in range(self.n):
            for k in range(1, N_DEV):
                px = 1 - x if k & 4 else x
                py = 1 - y if k & 2 else y
                pc = 1 - c if k & 1 else c
                peer = 4 * px + 2 * py + pc
                sem = a * (N_DEV - 1) + k - 1
                remote.append(pltpu.make_async_remote_copy(
                    src_ref=ins[a].at[peer] if self.scatter else ins[a], dst_ref=outs[a].at[peer if arriving else me],
                    send_sem=send_sems.at[sem], recv_sem=recv_sems.at[sem], device_id=(px, py, pc), device_id_type=MESH_IDS))
        return remote

    def start(self, ins, outs, sems):
        for cp in self._local(ins, outs, sems) + self._remote(ins, outs, sems, arriving=False):
            cp.start()

    def forward(self, ins, outs, sems):
        pass

    def wait(self, ins, outs, sems):
        for send, arrival in zip(self._remote(ins, outs, sems, arriving=False), self._remote(ins, outs, sems, arriving=True)):
            send.wait_send()
            arrival.wait_recv()
        for cp in self._local(ins, outs, sems):
            cp.wait()


N_CHIP = N_DEV // 2


class _SiblingSwap(_Exchange):
    def __init__(self, arrs):
        super().__init__(arrs, scatter=True)
        self.out_shape = [jax.ShapeDtypeStruct((N_CHIP,) + a.shape[2:], a.dtype) for a in self.arrs]
        self.scratch = [pltpu.SemaphoreType.DMA((self.n,)), pltpu.SemaphoreType.DMA((self.n,)), pltpu.SemaphoreType.DMA((1,))]

    def _copies(self, ins, outs, sems):
        x, y, c = lax.axis_index("x"), lax.axis_index("y"), lax.axis_index("c")
        return [pltpu.make_async_remote_copy(src_ref=ins[a].at[:, 1 - c], dst_ref=outs[a], send_sem=sems[0].at[a], recv_sem=sems[1].at[a],
                                             device_id=(x, y, 1 - c), device_id_type=MESH_IDS) for a in range(self.n)]

    def start(self, ins, outs, sems):
        for cp in self._copies(ins, outs, sems):
            cp.start()

    def wait(self, ins, outs, sems):
        for cp in self._copies(ins, outs, sems):
            cp.wait()


class _ChipScatter(_Exchange):
    def __init__(self, arrs):
        super().__init__(arrs, scatter=True)
        n_pairs = self.n * (N_CHIP - 1)
        self.scratch = [pltpu.SemaphoreType.DMA((n_pairs,)), pltpu.SemaphoreType.DMA((n_pairs,)), pltpu.SemaphoreType.DMA((self.n,))]

    def _local(self, ins, outs, sems):
        chip = 2 * lax.axis_index("x") + lax.axis_index("y")
        return [pltpu.make_async_copy(ins[a].at[chip], outs[a].at[chip], sems[2].at[a]) for a in range(self.n)]

    def _remote(self, ins, outs, sems, arriving):
        send_sems, recv_sems, _ = sems
        x, y, c = lax.axis_index("x"), lax.axis_index("y"), lax.axis_index("c")
        chip = 2 * x + y
        remote = []
        for a in range(self.n):
            for k in range(1, N_CHIP):
                px = 1 - x if k & 2 else x
                py = 1 - y if k & 1 else y
                peer = 2 * px + py
                sem = a * (N_CHIP - 1) + k - 1
                remote.append(pltpu.make_async_remote_copy(
                    src_ref=ins[a].at[peer], dst_ref=outs[a].at[peer if arriving else chip], send_sem=send_sems.at[sem],
                    recv_sem=recv_sems.at[sem], device_id=(px, py, c), device_id_type=MESH_IDS))
        return remote


def _chip_sum(mine, theirs):
    n, rows, cols = mine.shape
    blk = pl.BlockSpec((1, rows, 256), lambda q, j: (q, 0, j))

    def body(a_ref, b_ref, o_ref):
        o_ref[...] = (a_ref[...].astype(F32) + b_ref[...].astype(F32)).astype(BF16)

    return pl.pallas_call(body, name="chip_sum", grid=(n, cols // 256), in_specs=[blk, blk], out_specs=blk,
                          out_shape=jax.ShapeDtypeStruct(mine.shape, BF16),
                          compiler_params=_cparams(("parallel", "parallel")))(mine, theirs)


class _Gather2(_Exchange):
    def __init__(self, arrs):
        super().__init__(arrs, scatter=False)

    def _copies(self, ins, outs, sems):
        send_sems, recv_sems, _ = sems
        x, y, c = lax.axis_index("x"), lax.axis_index("y"), lax.axis_index("c")
        sibling = (x, y, 1 - c)
        chips = [(1 - x, y), (x, 1 - y), (1 - x, 1 - y)]
        first, passed, landed = [], [], []
        for a in range(self.n):
            def copy(k, block, to, src=None, a=a):
                slab = outs[a].at[4 * block[0] + 2 * block[1] + block[2]]
                return pltpu.make_async_remote_copy(
                    src_ref=slab if src is None else src, dst_ref=slab, send_sem=send_sems.at[a * (N_DEV - 1) + k],
                    recv_sem=recv_sems.at[a * (N_DEV - 1) + k], device_id=to, device_id_type=MESH_IDS)

            first.append(copy(0, (x, y, c), sibling, src=ins[a]))
            landed.append(copy(0, sibling, sibling))
            for j, chip in enumerate(chips):
                first.append(copy(1 + j, (x, y, c), (*chip, c), src=ins[a]))
                passed.append((copy(1 + j, (*chip, c), sibling), copy(4 + j, (*chip, c), sibling)))
                landed.append(copy(4 + j, (*chip, 1 - c), sibling))
        return first, passed, landed

    def start(self, ins, outs, sems):
        for cp in self._local(ins, outs, sems) + self._copies(ins, outs, sems)[0]:
            cp.start()

    def forward(self, ins, outs, sems):
        for arrival, onward in self._copies(ins, outs, sems)[1]:
            arrival.wait_recv()
            onward.start()

    def wait(self, ins, outs, sems):
        first, passed, landed = self._copies(ins, outs, sems)
        for arrival in landed:
            arrival.wait_recv()
        for cp in first + [onward for _, onward in passed]:
            cp.wait_send()
        for cp in self._local(ins, outs, sems):
            cp.wait()


def _split_comm_refs(refs, n_in, n_out, n_scr, comm):
    nc = comm.n if comm is not None else 0
    ns = 3 if comm is not None else 0
    pos, groups = 0, []
    for cnt in (n_in, nc, n_out, nc, n_scr, ns):
        groups.append(refs[pos:pos + cnt])
        pos += cnt
    assert pos == len(refs), (pos, len(refs))
    return groups


def _pcall(body, args, *, name, grid, in_specs, out_specs, out_shape, scratch_shapes=(), sem=None, comm=None):
    in_specs, out_specs, out_shape, scratch_shapes = list(in_specs), list(out_specs), list(out_shape), list(scratch_shapes)
    n_in, n_out, n_scr = len(in_specs), len(out_specs), len(scratch_shapes)
    if comm is None:
        kernel_body = body
    else:
        def kernel_body(*refs):
            ins, cins, outs, couts, scr, sems = _split_comm_refs(refs, n_in, n_out, n_scr, comm)
            ids = [pl.program_id(a) for a in range(len(grid))]
            first, last = ids[0] == 0, ids[0] == grid[0] - 1
            for a in range(1, len(grid)):
                first, last = first & (ids[a] == 0), last & (ids[a] == grid[a] - 1)

            middle = ids[0] == (2 * grid[0]) // 3
            for a in range(1, len(grid)):
                middle = middle & (ids[a] == 0)

            @pl.when(first)
            def _():
                comm.start(cins, couts, sems)

            @pl.when(middle)
            def _():
                comm.forward(cins, couts, sems)

            body(*ins, *outs, *scr)

            @pl.when(last)
            def _():
                comm.wait(cins, couts, sems)

        in_specs, out_specs, out_shape = in_specs + comm.in_specs, out_specs + comm.out_specs, out_shape + comm.out_shape
        scratch_shapes, args = scratch_shapes + comm.scratch, list(args) + comm.arrs
        sem = ("arbitrary",) * len(grid)
    res = pl.pallas_call(kernel_body, name=name, grid=grid, in_specs=in_specs, out_specs=out_specs, out_shape=out_shape,
                         scratch_shapes=scratch_shapes, compiler_params=_cparams(sem))(*args)
    return res[:n_out], res[n_out:]


def _exchange(arrs, name, scatter=False, ex=None):
    if ex is None:
        ex = _Exchange(arrs, scatter=True) if scatter else _Gather2(arrs)

    def body(*refs):
        _, ins, _, outs, _, sems = _split_comm_refs(refs, 0, 0, 0, ex)
        ex.start(ins, outs, sems)
        ex.forward(ins, outs, sems)
        ex.wait(ins, outs, sems)

    return pl.pallas_call(body, name=name, in_specs=ex.in_specs, out_specs=ex.out_specs, out_shape=ex.out_shape,
                          scratch_shapes=ex.scratch)(*ex.arrs)


def _pad_lanes(v, width=LANE):
    return jnp.pad(v, ((0, 0), (0, width - v.shape[1])))


def _shards_to_cols(g):
    return jnp.transpose(g, (1, 0, 2)).reshape(g.shape[1], N_DEV * g.shape[2])


def _local_step(x, tgt, mod, w_in_pt, conv_w, conv_b, dt_bias, a_log, d_skip, ssd_norm_w, q_norm_w, k_norm_w,
                attn_norm_w, w_out_sh, w_ff1_sh, w_ff2_sh, norm1_w, norm2_w, core):
    shift1, scale1, gate1, shift2, scale2, gate2 = [mod[i:i + 1] for i in range(N_MOD)]
    dtb, alog, dsk = _pad_lanes(dt_bias), _pad_lanes(a_log), _pad_lanes(d_skip)
    qw, kw = jnp.tile(q_norm_w, (1, ATT_HEADS)), jnp.tile(k_norm_w, (1, ATT_HEADS))

    h1 = _norm_mod_fwd(x, norm1_w, scale1, shift1, "norm1_fwd")
    proj = _matmul(h1, w_in_pt, tb=True, tm=2048, tn=896, tk=1024, name="in_proj")
    pre, act = _conv_fwd(proj, conv_w, conv_b)
    ypre, ycat_ssd, hall = _ssd_fwd(proj, act, dtb, alog, dsk, ssd_norm_w)
    (o_att, lse), (w_out_g, w_ff1_g, w_ff2_g) = _att_fwd(proj, qw, kw, comm=_Gather2([w_out_sh, w_ff1_sh, w_ff2_sh]))
    w_out = w_out_g.reshape(2 * D_MODEL, D_MODEL)
    w_ff1 = _shards_to_cols(w_ff1_g)
    w_ff2 = w_ff2_g.reshape(D_FF, D_MODEL)
    ycat = _att_norm_fwd(o_att, attn_norm_w, ycat_ssd)
    row32, row16, vec32 = ("row", F32), ("row", BF16), ("vec", F32)
    mix, x1, h2 = _matmul_rows(ycat, w_out, _residual_norm_epilogue, [x], [gate1, norm2_w, scale2, shift2],
                               [row32, row32, row16], tm=512, name="out_proj")
    u, act_ff = _matmul(h2, w_ff1, tm=1024, tn=2048, tk=1024, name="ff1", mode="relu2")
    loss, dout, dff, dgate2 = _matmul_rows(act_ff, w_ff2, _loss_epilogue, [x1, tgt], [gate2],
                                           [("one", F32), row32, row16, vec32], tm=512, name="ff2")

    du = _matmul(dff, w_ff2, tb=True, tm=512, tn=4096, tk=1024, out_dtype=BF16, name="ff2_dx", mode="drelu2", u=u)
    g_ff2 = _matmul(act_ff, dff, ta=True, tm=512, tn=1024, tk=4096, out_dtype=BF16, name="ff2_dw")
    dx1, dshift2, dscale2, g_norm2, dmix, dgate1 = _matmul_rows(
        du, w_ff1, _norm_bwd_epilogue, [x1, dout, mix], [norm2_w, scale2, gate1],
        [row32, vec32, vec32, vec32, row16, vec32], tb=True, tm=512, name="ff1_dx")
    g_ff1 = _matmul(h2, du, ta=True, tm=1024, tn=D_FF // N_DEV, tk=4096, out_dtype=BF16, name="ff1_dw", shard_out=True)

    dy_ssd, do, stats, g_attn_norm = _matmul_rows(
        dmix, w_out, _mixer_split_epilogue, [o_att, lse], [attn_norm_w],
        [("row", F32, SSD_D_INNER), ("row", F32, ATT_D), ("row", F32, ATT_D), ("vec", F32, ATT_D)], tb=True, tm=512, name="out_proj_dx")
    g_out = _matmul(ycat, dmix, ta=True, tm=512, tn=1024, tk=4096, out_dtype=BF16, name="out_proj_dw")
    ff_slabs = [g_ff1, g_ff2.reshape(N_DEV, D_FF // N_DEV, D_MODEL)]
    (dq, dk, dv, dqw, dkw), (s_ff1, s_ff2) = _att_bwd(proj, do, stats, qw, kw, comm=_Exchange(ff_slabs, scatter=True))
    out_slabs = [g_out.astype(BF16).reshape(N_DEV, 2 * D_MODEL // N_DEV, D_MODEL)]
    (dz, dact, ddtr, da, g_dsk, g_dtb, g_ssd_norm), (s_out,) = _ssd_bwd(
        dy_ssd, ypre, proj, act, hall, dtb, alog, dsk, ssd_norm_w, comm=_Exchange(out_slabs, scatter=True))
    dxbc, g_conv_w, g_conv_b = _conv_bwd(dact, pre, proj, conv_w)
    dproj = [(dz, OFF_Z), (dxbc, OFF_XBC), (ddtr, OFF_DT), (dq, OFF_Q), (dk, OFF_K), (dv, OFF_V)]
    g_head, g_tail = _pieces_t_matmul([[dz, dxbc], [dq, dk, dv]], h1, tm=256, name="in_proj_dw")
    g_dt = _matmul(ddtr, h1, ta=True, tm=LANE, tn=1024, tk=4096, out_dtype=BF16, name="in_proj_dw_dt")[:SSD_HEADS]
    in_slabs = jnp.concatenate([g_head, g_dt, g_tail], axis=0).reshape(N_CHIP, 2, IN_W // N_DEV, D_MODEL)
    (sibling_slabs,) = _exchange(None, "swap_w_in_grads", ex=_SiblingSwap([in_slabs]))
    chip_slabs = _chip_sum(lax.dynamic_index_in_dim(in_slabs, core, axis=1, keepdims=False), sibling_slabs)
    (grad_x, dshift1, dscale1, g_norm1), (s_in,) = _matmul_rows(
        dproj, w_in_pt, _norm_bwd_epilogue, [x, dx1], [norm1_w, scale1], [row32, vec32, vec32, vec32],
        tm=256, name="in_proj_dx", comm=_ChipScatter([chip_slabs]))

    dmod = jnp.concatenate([dshift1, dscale1, dgate1, dshift2, dscale2, dgate2], axis=0)
    g_alog = da[:, :SSD_HEADS] * (-jnp.exp(a_log))
    g_qw = dqw.reshape(ATT_HEADS, HEAD_DIM).sum(axis=0, keepdims=True)
    g_kw = dkw.reshape(ATT_HEADS, HEAD_DIM).sum(axis=0, keepdims=True)
    return dict(loss=loss, grad_x=grad_x, dmod=dmod, norm1_w=g_norm1, norm2_w=g_norm2, w_in=s_in, conv_w=g_conv_w,
                conv_b=g_conv_b, dt_bias=g_dtb[:, :SSD_HEADS], a_log=g_alog, d_skip=g_dsk[:, :SSD_HEADS],
                ssd_norm_w=g_ssd_norm, q_norm_w=g_qw, k_norm_w=g_kw, attn_norm_w=g_attn_norm, w_out=s_out,
                w_ff1=s_ff1, w_ff2=s_ff2)


def _pack_w_in_rows(wt_full):
    cut = OFF_DT + SSD_HEADS
    pad = jnp.zeros((LANE - SSD_HEADS, wt_full.shape[1]), wt_full.dtype)
    return jnp.concatenate([wt_full[:cut], pad, wt_full[cut:]], axis=0)


MISC_FIELDS = (("dt_bias", SSD_HEADS), ("a_log", SSD_HEADS), ("d_skip", SSD_HEADS), ("q_norm_w", HEAD_DIM), ("k_norm_w", HEAD_DIM),
               ("loss", 1))
SMALL_LAYOUT = (("b_ada", 6), ("norm1_w", 1), ("norm2_w", 1), ("conv_w", 8), ("conv_b", 2), ("ssd_norm_w", 1),
                ("attn_norm_w", 1), ("misc", 1))


def _pack_small(vals):
    rows = []
    for name, nrow in SMALL_LAYOUT:
        if name == "misc":
            misc = jnp.concatenate([vals[f].reshape(1, n) if f in vals else jnp.zeros((1, n), F32) for f, n in MISC_FIELDS], axis=1)
            rows.append(_pad_lanes(misc, D_MODEL))
        elif name in vals:
            rows.append(vals[name].reshape(nrow, D_MODEL))
        else:
            rows.append(jnp.zeros((nrow, D_MODEL), F32))
    used = sum(n for _, n in SMALL_LAYOUT)
    rows.append(jnp.zeros((SMALL_ROWS - used, D_MODEL), F32))
    return jnp.concatenate(rows, axis=0)


def _unpack_small(packed):
    out, r = {}, 0
    for name, nrow in SMALL_LAYOUT:
        blk = packed[r:r + nrow]
        r += nrow
        if name == "misc":
            c0 = 0
            for f, n in MISC_FIELDS:
                out[f] = blk[:, c0:c0 + n]
                c0 += n
        elif name == "b_ada":
            out[name] = blk.reshape(1, N_MOD * D_MODEL)
        elif name == "conv_w":
            out[name] = blk.reshape(CONV_K, CONV_CH)
        elif name == "conv_b":
            out[name] = blk.reshape(1, CONV_CH)
        else:
            out[name] = blk
    return out


WEIGHT_NAMES = ("norm1_w", "norm2_w", "w_ada", "b_ada", "w_in", "conv_w", "conv_b", "dt_bias", "a_log", "d_skip",
                "ssd_norm_w", "q_norm_w", "k_norm_w", "attn_norm_w", "w_out", "w_ff1", "w_ff2")
SMALL_NAMES = ("norm1_w", "norm2_w", "b_ada", "conv_b", "dt_bias", "a_log", "d_skip", "ssd_norm_w", "q_norm_w",
               "k_norm_w", "attn_norm_w")


def kernel(x, c, norm1_w, norm2_w, w_ada, b_ada, w_in, conv_w, conv_b, dt_bias, a_log, d_skip, ssd_norm_w, q_norm_w, k_norm_w, attn_norm_w, w_out, w_ff1, w_ff2, loss_target, m_norm1_w, m_norm2_w, m_w_ada, m_b_ada, m_w_in, m_conv_w, m_conv_b, m_dt_bias, m_a_log, m_d_skip, m_ssd_norm_w, m_q_norm_w, m_k_norm_w, m_attn_norm_w, m_w_out, m_w_ff1, m_w_ff2, v_norm1_w, v_norm2_w, v_w_ada, v_b_ada, v_w_in, v_conv_w, v_conv_b, v_dt_bias, v_a_log, v_d_skip, v_ssd_norm_w, v_q_norm_w, v_k_norm_w, v_attn_norm_w, v_w_out, v_w_ff1, v_w_ff2):
    args = dict(locals())
    w = {n: args[n] for n in WEIGHT_NAMES}
    m = {n: args["m_" + n] for n in WEIGHT_NAMES}
    v = {n: args["v_" + n] for n in WEIGHT_NAMES}
    me = 4 * lax.axis_index("x") + 2 * lax.axis_index("y") + lax.axis_index("c")

    c_rows = jnp.pad(c, ((0, 7), (0, 0)))
    w_in_t, m_in_t, v_in_t = [jnp.transpose(t["w_in"][0]) for t in (w, m, v)]
    c_g, conv_g, w_in_g = _exchange([c_rows, w["conv_w"][0], w_in_t.astype(BF16)], "gather_w_in", scatter=False)
    c_all = c_g[:, 0, :]
    conv_full = _shards_to_cols(conv_g)
    w_in_pt = _pack_w_in_rows(w_in_g.reshape(IN_W, D_MODEL))

    mod_part = _ada_fwd(c_all, w["w_ada"][0])
    (mod_g,) = _exchange([mod_part], "gather_mod", scatter=False)
    mod_mine = lax.dynamic_index_in_dim(mod_g, me, axis=1, keepdims=False).reshape(1, N_MOD * D_MODEL) + w["b_ada"]
    mod = mod_mine.reshape(N_MOD, D_MODEL)

    res = _local_step(x[0], loss_target[0], mod, w_in_pt, conv_full, w["conv_b"], w["dt_bias"], w["a_log"], w["d_skip"],
                      w["ssd_norm_w"], w["q_norm_w"], w["k_norm_w"], w["attn_norm_w"], w["w_out"][0].astype(BF16),
                      w["w_ff1"][0].astype(BF16), w["w_ff2"][0].astype(BF16), w["norm1_w"], w["norm2_w"], lax.axis_index("c"))

    small_vals = {n: res[n] for n in SMALL_NAMES if n != "b_ada"}
    small_vals["b_ada"] = res["dmod"]
    small_vals["conv_w"] = res["conv_w"]
    small_vals["loss"] = res["loss"]
    (small_g,) = _exchange([_pack_small(small_vals)], "gather_small", scatter=False)

    grads, delta, new_m, new_v = {}, {}, {}, {}
    for name in ("w_out", "w_ff1", "w_ff2"):
        outs = _reduce_adamw(res[name], w[name][0], m[name][0], v[name][0], "adamw_" + name)
        grads[name], delta[name], new_m[name], new_v[name] = [o[None] for o in outs]
    outs = _reduce_adamw(res["w_in"], w_in_t, m_in_t, v_in_t, "adamw_w_in")
    grads["w_in"], delta["w_in"], new_m["w_in"], new_v["w_in"] = [jnp.transpose(o)[None] for o in outs]

    sm = _small_reduce_adamw(small_g, _pack_small({n: w[n] for n in SMALL_NAMES}), _pack_small({n: m[n] for n in SMALL_NAMES}),
                             _pack_small({n: v[n] for n in SMALL_NAMES}))
    sm = [_unpack_small(p) for p in sm]
    for n in SMALL_NAMES:
        grads[n], delta[n], new_m[n], new_v[n] = [p[n] for p in sm]
    shard_w = CONV_CH // N_DEV
    g_conv = lax.dynamic_slice_in_dim(sm[0]["conv_w"], me * shard_w, shard_w, axis=1)
    cw = _adamw_small(g_conv, w["conv_w"][0], m["conv_w"][0], v["conv_w"][0], "adamw_conv_w")
    grads["conv_w"] = g_conv[None]
    delta["conv_w"], new_m["conv_w"], new_v["conv_w"] = [o[None] for o in cw]

    ada_w = w_ada.shape[2]
    dmod_all = small_g[:, :N_MOD, :].reshape(N_DEV, N_MOD * D_MODEL)
    dmod_cols = lax.dynamic_slice_in_dim(dmod_all, me * ada_w, ada_w, axis=1)
    outs = _ada_bwd_adamw(c_all, dmod_cols, w["w_ada"][0], m["w_ada"][0], v["w_ada"][0])
    grads["w_ada"], delta["w_ada"], new_m["w_ada"], new_v["w_ada"] = [o[None] for o in outs]

    loss = sm[0]["loss"][0, 0]
    return (loss, res["grad_x"][None], *[grads[n] for n in WEIGHT_NAMES], *[delta[n] for n in WEIGHT_NAMES],
            *[new_m[n] for n in WEIGHT_NAMES], *[new_v[n] for n in WEIGHT_NAMES])
```

```python
import jax
import jax.numpy as jnp
from jax import lax
from jax.experimental import pallas as pl
from jax.experimental.pallas import tpu as pltpu

F32 = jnp.float32
BF16 = jnp.bfloat16
HIGHEST = lax.Precision.HIGHEST
MESH_IDS = pl.DeviceIdType.MESH

N_DEV = 8
D_MODEL = 1024
HEAD_DIM = 64
SSD_HEADS = 16
SSD_GROUPS = 4
HEADS_PER_GROUP = SSD_HEADS // SSD_GROUPS
SSD_STATE = 128
SSD_CHUNK = 128
SSD_D_INNER = SSD_HEADS * HEAD_DIM
GROUP_WIDTH = SSD_D_INNER // SSD_GROUPS
CONV_K = 4
CONV_CH = SSD_D_INNER + 2 * SSD_GROUPS * SSD_STATE
ATT_HEADS = 16
ATT_D = ATT_HEADS * HEAD_DIM
ATT_BLK = 128
DILATIONS = (1, 4, 16)
D_FF = 4 * D_MODEL
N_MOD = 6
EPS = 1e-6
IN_W = SSD_D_INNER + CONV_CH + SSD_HEADS + 3 * ATT_D
LANE = 128
OFF_Z, OFF_XBC, OFF_DT = 0, SSD_D_INNER, SSD_D_INNER + CONV_CH
OFF_Q = OFF_DT + LANE
OFF_K, OFF_V = OFF_Q + ATT_D, OFF_Q + 2 * ATT_D
IN_WP = OFF_V + ATT_D

ADAM_LR, ADAM_B1, ADAM_B2, ADAM_EPS, ADAM_WD, ADAM_STEP = 0.001, 0.9, 0.999, 1e-08, 0.01, 10
VMEM_LIMIT = 56 * 1024 * 1024
ROW_TILE = 512
SMALL_ROWS = 24


def _cparams(sem=None):
    return pltpu.CompilerParams(dimension_semantics=sem, vmem_limit_bytes=VMEM_LIMIT)


def _sigmoid(v):
    return 1.0 / (1.0 + jnp.exp(-v))


def _softplus(v):
    y = jnp.exp(-jnp.abs(v))
    small = y * (1.0 - y * (0.5 - y * (1.0 / 3.0)))
    return jnp.maximum(v, 0.0) + jnp.where(y < 0.01, small, jnp.log(1.0 + y))


def _dot(a, b, dims, precision=None):
    return lax.dot_general(a, b, (dims, ((), ())), preferred_element_type=F32, precision=precision)


NN = ((1,), (0,))
NT = ((1,), (1,))
TN = ((0,), (0,))


def _matmul(a, b, *, ta=False, tb=False, tm, tn, tk, out_dtype=F32, name, mode=None, u=None, comm=None, shard_out=False):
    m, k = (a.shape[1], a.shape[0]) if ta else a.shape
    n = b.shape[0] if tb else b.shape[1]
    assert m % tm == 0 and n % tn == 0 and k % tk == 0, (name, m, n, k)
    nk = k // tk
    a_spec = pl.BlockSpec((tk, tm), lambda i, j, kk: (kk, i)) if ta else pl.BlockSpec((tm, tk), lambda i, j, kk: (i, kk))
    b_spec = pl.BlockSpec((tn, tk), lambda i, j, kk: (j, kk)) if tb else pl.BlockSpec((tk, tn), lambda i, j, kk: (kk, j))
    o_spec = pl.BlockSpec((tm, tn), lambda i, j, kk: (i, j))
    dims = ((0,) if ta else (1,), (1,) if tb else (0,))
    n_out = 2 if mode == "relu2" else 1

    def body(*refs):
        if mode == "drelu2":
            a_ref, b_ref, u_ref = refs[:3]
            rest = refs[3:]
        else:
            a_ref, b_ref = refs[:2]
            u_ref = None
            rest = refs[2:]
        outs = rest[:n_out]
        part = _dot(a_ref[...], b_ref[...], dims)

        def finish(r):
            if mode == "relu2":
                outs[0][...] = r.astype(BF16)
                rr = jnp.maximum(r, 0.0)
                outs[1][...] = (rr * rr).astype(BF16)
            elif mode == "drelu2":
                outs[0][...] = (r * (2.0 * jnp.maximum(u_ref[...].astype(F32), 0.0))).astype(out_dtype)
            else:
                outs[0][...] = r.astype(out_dtype)

        if nk == 1:
            finish(part)
        else:
            acc = rest[n_out]
            kk = pl.program_id(2)

            @pl.when(kk == 0)
            def _():
                acc[...] = part

            @pl.when(kk > 0)
            def _():
                acc[...] += part

            @pl.when(kk == nk - 1)
            def _():
                finish(acc[...])

    in_specs = [a_spec, b_spec]
    args = [a, b]
    if mode == "drelu2":
        in_specs.append(o_spec)
        args.append(u)
    if mode == "relu2":
        out_shape = [jax.ShapeDtypeStruct((m, n), BF16), jax.ShapeDtypeStruct((m, n), BF16)]
    elif shard_out:
        out_shape = [jax.ShapeDtypeStruct((n // tn, m, tn), out_dtype)]
        o_spec = pl.BlockSpec((None, tm, tn), lambda i, j, kk: (j, i, 0))
    else:
        out_shape = [jax.ShapeDtypeStruct((m, n), out_dtype)]
    outs, comm_outs = _pcall(
        body, args, name=name, grid=(m // tm, n // tn, nk), in_specs=in_specs, out_specs=[o_spec] * n_out,
        out_shape=out_shape, scratch_shapes=[pltpu.VMEM((tm, tn), F32)] if nk > 1 else [],
        sem=("parallel", "parallel", "arbitrary"), comm=comm)
    res = tuple(outs) if mode == "relu2" else outs[0]
    return res if comm is None else (res, comm_outs)


def _pieces_t_matmul(groups, b, *, tm, name):
    k, n = b.shape
    pieces = [p for g in groups for p in g]
    starts, tiles = [], 0
    for p in pieces:
        assert p.shape[0] == k and p.shape[1] % tm == 0, (name, p.shape)
        starts.append(tiles)
        tiles += p.shape[1] // tm
    group_of, group_start, group_tiles = [], [], []
    for gi, g in enumerate(groups):
        group_start.append(starts[len(group_of)])
        group_of += [gi] * len(g)
        group_tiles.append(sum(p.shape[1] // tm for p in g))

    def clipped(block, start, count):
        return pl.BlockSpec(block, (lambda i: (0, jnp.clip(i - start, 0, count - 1))) if block[0] == k
                            else (lambda i: (jnp.clip(i - start, 0, count - 1), 0)))

    def body(*refs):
        a_refs, b_ref, o_refs = refs[:len(pieces)], refs[len(pieces)], refs[len(pieces) + 1:]
        i = pl.program_id(0)
        for a_ref, start, p, gi in zip(a_refs, starts, pieces, group_of):
            @pl.when((i >= start) & (i < start + p.shape[1] // tm))
            def _(a_ref=a_ref, o_ref=o_refs[gi]):
                o_ref[...] = _dot(a_ref[...], b_ref[...], TN).astype(BF16)

    return pl.pallas_call(
        body, name=name, grid=(tiles,),
        in_specs=[clipped((k, tm), s0, p.shape[1] // tm) for s0, p in zip(starts, pieces)] + [pl.BlockSpec((k, n), lambda i: (0, 0))],
        out_specs=[clipped((tm, n), s0, cnt) for s0, cnt in zip(group_start, group_tiles)],
        out_shape=[jax.ShapeDtypeStruct((cnt * tm, n), BF16) for cnt in group_tiles],
        compiler_params=_cparams(("arbitrary",)))(*pieces, b)


def _rms_mod(xv, nw, scale, shift):
    r = lax.rsqrt(jnp.mean(xv * xv, axis=-1, keepdims=True) + EPS)
    return ((xv * r) * nw * (1.0 + scale) + shift).astype(BF16)


def _norm_mod_fwd(x, nw, scale, shift, name):
    s, d = x.shape
    row = pl.BlockSpec((ROW_TILE, d), lambda i: (i, 0))
    vec = pl.BlockSpec((1, d), lambda i: (0, 0))

    def body(x_ref, nw_ref, sc_ref, sh_ref, h_ref):
        h_ref[...] = _rms_mod(x_ref[...], nw_ref[...], sc_ref[...], sh_ref[...])

    return pl.pallas_call(body, name=name, grid=(s // ROW_TILE,), in_specs=[row, vec, vec, vec], out_specs=row,
                          out_shape=jax.ShapeDtypeStruct((s, d), BF16), compiler_params=_cparams(("parallel",)))(x, nw, scale, shift)


def _matmul_rows(a, b, epilogue, row_in, vec_in, outs, *, tb=False, tm, name, comm=None):
    pieces = a if isinstance(a, list) else [(a, 0)]
    assert not (tb and len(pieces) > 1)
    m = pieces[0][0].shape[0]
    n = b.shape[0] if tb else b.shape[1]
    assert m % tm == 0, (name, m, tm)
    dims = ((1,), (1,) if tb else (0,))
    n_a, n_row, n_vec = len(pieces), len(row_in), len(vec_in)

    def body(*refs):
        a_refs, b_ref, rest = refs[:n_a], refs[n_a], refs[n_a + 1:]
        if n_a == 1:
            c = _dot(a_refs[0][...], b_ref[...], dims)
        else:
            c = None
            for a_ref, (piece, off) in zip(a_refs, pieces):
                part = _dot(a_ref[...], b_ref[off:off + piece.shape[1], :], dims)
                c = part if c is None else c + part
        epilogue(c, pl.program_id(0) == 0, rest[:n_row], rest[n_row:n_row + n_vec], rest[n_row + n_vec:])

    def spec(kind, width):
        block = {"row": (tm, width), "vec": (1, width), "one": (1, 1)}[kind]
        return pl.BlockSpec(block, (lambda i: (i, 0)) if kind == "row" else (lambda i: (0, 0)))

    def shape(kind, width):
        return {"row": (m, width), "vec": (1, width), "one": (1, 1)}[kind]

    outs = [(o[0], o[1], o[2] if len(o) > 2 else n) for o in outs]
    res, comm_outs = _pcall(
        body, [*[p for p, _ in pieces], b, *row_in, *vec_in], name=name, grid=(m // tm,),
        in_specs=[spec("row", p.shape[1]) for p, _ in pieces] + [pl.BlockSpec(b.shape, lambda i: (0, 0))]
        + [spec("row", r.shape[1]) for r in row_in] + [spec("vec", v.shape[1]) for v in vec_in],
        out_specs=[spec(kind, width) for kind, _, width in outs],
        out_shape=[jax.ShapeDtypeStruct(shape(kind, width), dt) for kind, dt, width in outs],
        sem=("arbitrary",), comm=comm)
    return res if comm is None else (res, comm_outs)


def _residual_norm_epilogue(mix, first, rows, vecs, outs):
    (x_ref,), (gate_ref, nw_ref, sc_ref, sh_ref), (mix_ref, x1_ref, h_ref) = rows, vecs, outs
    xv = x_ref[...] + gate_ref[...] * mix
    mix_ref[...] = mix
    x1_ref[...] = xv
    h_ref[...] = _rms_mod(xv, nw_ref[...], sc_ref[...], sh_ref[...])


def _loss_epilogue(ff, first, rows, vecs, outs):
    (x1_ref, t_ref), (g_ref,), (loss_ref, dout_ref, dff_ref, dg_ref) = rows, vecs, outs
    d = ff.shape[1]

    @pl.when(first)
    def _():
        loss_ref[...] = jnp.zeros_like(loss_ref)
        dg_ref[...] = jnp.zeros_like(dg_ref)

    err = x1_ref[...] + g_ref[...] * ff - t_ref[...]
    loss_ref[...] += (0.5 / d) * jnp.sum(err * err).reshape(1, 1)
    dout = err * (1.0 / d)
    dout_ref[...] = dout
    dff_ref[...] = (g_ref[...] * dout).astype(BF16)
    dg_ref[...] += jnp.sum(dout * ff, axis=0, keepdims=True)


def _norm_bwd_epilogue(dh, first, rows, vecs, outs):
    with_gate = len(vecs) == 3
    x_ref, dres_ref = rows[:2]
    nw_ref, sc_ref = vecs[:2]
    dx_ref, dsh_ref, dsc_ref, dnw_ref = outs[:4]

    @pl.when(first)
    def _():
        for ref in outs[1:4] + outs[5:]:
            ref[...] = jnp.zeros_like(ref)

    xv = x_ref[...]
    r = lax.rsqrt(jnp.mean(xv * xv, axis=-1, keepdims=True) + EPS)
    nrm = xv * r
    one_sc = 1.0 + sc_ref[...]
    dhn = dh * nrm
    dsh_ref[...] += jnp.sum(dh, axis=0, keepdims=True)
    dsc_ref[...] += jnp.sum(dhn, axis=0, keepdims=True) * nw_ref[...]
    dnw_ref[...] += jnp.sum(dhn, axis=0, keepdims=True) * one_sc
    dn = dh * (nw_ref[...] * one_sc)
    dx = dres_ref[...] + r * (dn - nrm * jnp.mean(dn * nrm, axis=-1, keepdims=True))
    dx_ref[...] = dx
    if with_gate:
        outs[4][...] = (vecs[2][...] * dx).astype(BF16)
        outs[5][...] += jnp.sum(dx * rows[2][...], axis=0, keepdims=True)


CONV_COLS = 256
CONV_FWD_ROWS = 2048
CONV_BWD_ROWS = 1024
CONV_SUB_ROWS = 128
HALO = 8


def _shift_down(cur, halo, k):
    if k == 0:
        return cur
    rolled = pltpu.roll(cur, k, axis=0)
    top = jnp.where(lax.broadcasted_iota(jnp.int32, halo.shape, 0) < k, pltpu.roll(halo, k, axis=0), rolled[:HALO])
    return jnp.concatenate([top, rolled[HALO:]], axis=0)


def _shift_up(cur, halo, k):
    if k == 0:
        return cur
    t = cur.shape[0]
    rolled = pltpu.roll(cur, t - k, axis=0)
    bot = jnp.where(lax.broadcasted_iota(jnp.int32, halo.shape, 0) >= HALO - k, pltpu.roll(halo, HALO - k, axis=0),
                    rolled[t - HALO:])
    return jnp.concatenate([rolled[:t - HALO], bot], axis=0)


def _conv_fwd(proj, conv_w, conv_b):
    s = proj.shape[0]
    nr = s // CONV_FWD_ROWS
    cb0 = OFF_XBC // CONV_COLS
    hb = CONV_FWD_ROWS // HALO
    cur = pl.BlockSpec((CONV_FWD_ROWS, CONV_COLS), lambda j, r: (r, cb0 + j))
    prev = pl.BlockSpec((HALO, CONV_COLS), lambda j, r: (jnp.maximum(r * hb - 1, 0), cb0 + j))
    out = pl.BlockSpec((CONV_FWD_ROWS, CONV_COLS), lambda j, r: (r, j))

    def body(u_ref, up_ref, w_ref, b_ref, pre_ref, act_ref):
        r = pl.program_id(1)
        for c in range(CONV_FWD_ROWS // CONV_SUB_ROWS):
            rows = slice(c * CONV_SUB_ROWS, (c + 1) * CONV_SUB_ROWS)
            u = u_ref[rows, :]
            halo = u_ref[c * CONV_SUB_ROWS - HALO:c * CONV_SUB_ROWS, :] if c > 0 else jnp.where(r > 0, up_ref[...], 0.0)
            acc = b_ref[...] + w_ref[CONV_K - 1:CONV_K, :] * u
            for k in range(1, CONV_K):
                acc = acc + w_ref[CONV_K - 1 - k:CONV_K - k, :] * _shift_down(u, halo, k)
            pre_ref[rows, :] = acc
            act_ref[rows, :] = acc * _sigmoid(acc)

    return pl.pallas_call(
        body, name="conv_fwd", grid=(CONV_CH // CONV_COLS, nr),
        in_specs=[cur, prev, pl.BlockSpec((CONV_K, CONV_COLS), lambda j, r: (0, j)),
                  pl.BlockSpec((1, CONV_COLS), lambda j, r: (0, j))],
        out_specs=[out, out],
        out_shape=[jax.ShapeDtypeStruct((s, CONV_CH), F32), jax.ShapeDtypeStruct((s, CONV_CH), F32)],
        compiler_params=_cparams(("parallel", "arbitrary")))(proj, proj, conv_w, conv_b)


def _conv_bwd(dact, pre, proj, conv_w):
    s = proj.shape[0]
    nr = s // CONV_BWD_ROWS
    cb0 = OFF_XBC // CONV_COLS
    hb = CONV_BWD_ROWS // HALO
    last_halo = s // HALO - 1
    n_sub = CONV_BWD_ROWS // CONV_SUB_ROWS
    cur = pl.BlockSpec((CONV_BWD_ROWS, CONV_COLS), lambda j, r: (r, j))
    nxt = pl.BlockSpec((HALO, CONV_COLS), lambda j, r: (jnp.minimum((r + 1) * hb, last_halo), j))
    ucur = pl.BlockSpec((CONV_BWD_ROWS, CONV_COLS), lambda j, r: (r, cb0 + j))
    wspec = pl.BlockSpec((CONV_K, CONV_COLS), lambda j, r: (0, j))
    bspec = pl.BlockSpec((1, CONV_COLS), lambda j, r: (0, j))

    def dsilu(p):
        sg = _sigmoid(p)
        return sg * (1.0 + p * (1.0 - sg))

    def body(da_ref, dan_ref, pre_ref, pren_ref, u_ref, w_ref, du_ref, dw_ref, db_ref):
        r = pl.program_id(1)

        @pl.when(r == 0)
        def _():
            dw_ref[...] = jnp.zeros_like(dw_ref)
            db_ref[...] = jnp.zeros_like(db_ref)

        dws = [jnp.zeros((1, CONV_COLS), F32) for _ in range(CONV_K)]
        db = jnp.zeros((1, CONV_COLS), F32)
        for c in range(n_sub):
            rows = slice(c * CONV_SUB_ROWS, (c + 1) * CONV_SUB_ROWS)
            ahead = slice((c + 1) * CONV_SUB_ROWS, (c + 1) * CONV_SUB_ROWS + HALO)
            dpre = da_ref[rows, :] * dsilu(pre_ref[rows, :])
            if c < n_sub - 1:
                dnext = da_ref[ahead, :] * dsilu(pre_ref[ahead, :])
            else:
                dnext = jnp.where(r < nr - 1, dan_ref[...] * dsilu(pren_ref[...]), 0.0)
            u = u_ref[rows, :]
            du = w_ref[CONV_K - 1:CONV_K, :] * dpre
            dws[0] = dws[0] + jnp.sum(dpre * u, axis=0, keepdims=True)
            for k in range(1, CONV_K):
                ahead_k = _shift_up(dpre, dnext, k)
                du = du + w_ref[CONV_K - 1 - k:CONV_K - k, :] * ahead_k
                dws[k] = dws[k] + jnp.sum(ahead_k * u, axis=0, keepdims=True)
            du_ref[rows, :] = du.astype(BF16)
            db = db + jnp.sum(dpre, axis=0, keepdims=True)
        dw_ref[...] += jnp.concatenate(dws[::-1], axis=0)
        db_ref[...] += db

    return pl.pallas_call(
        body, name="conv_bwd", grid=(CONV_CH // CONV_COLS, nr),
        in_specs=[cur, nxt, cur, nxt, ucur, wspec],
        out_specs=[cur, wspec, bspec],
        out_shape=[jax.ShapeDtypeStruct((s, CONV_CH), BF16), jax.ShapeDtypeStruct((CONV_K, CONV_CH), F32),
                   jax.ShapeDtypeStruct((1, CONV_CH), F32)],
        compiler_params=_cparams(("parallel", "arbitrary")))(dact, dact, pre, pre, proj, conv_w)


def _ssd_common(dtr, dtb, alog):
    lane = lax.broadcasted_iota(jnp.int32, (1, LANE), 1)
    head_lane = lane < SSD_HEADS
    dt = jnp.where(head_lane, _softplus(dtr + dtb), 0.0)
    a = jnp.where(head_lane, -jnp.exp(alog), 0.0)
    row = lax.broadcasted_iota(jnp.int32, (SSD_CHUNK, SSD_CHUNK), 0)
    col = lax.broadcasted_iota(jnp.int32, (SSD_CHUNK, SSD_CHUNK), 1)
    tril = row >= col
    cs = _dot(tril.astype(F32), dt * a, NN, precision=HIGHEST)
    return dt, a, cs, cs.T, tril, lane


def _split_bf16(v, passes):
    terms, rest = [], v
    for _ in range(passes):
        t = rest.astype(BF16)
        terms.append(t)
        rest = rest - t.astype(F32)
    return terms


def _dot_split(v, m, dims, passes):
    terms = _split_bf16(v, passes)
    if passes == 1:
        return _dot(terms[0], m, dims)
    return _dot(jnp.concatenate(terms, axis=1), jnp.concatenate([m] * passes, axis=0 if dims == NN else 1), dims)


def _ssd_constants():
    heads = jnp.arange(LANE)[:, None]
    exp_mat = (heads == (jnp.arange(SSD_D_INNER)[None, :] // HEAD_DIM)).astype(BF16)
    ind4 = ((jnp.arange(SSD_HEADS * SSD_CHUNK)[:, None] // SSD_CHUNK) == jnp.arange(LANE)[None, :]).astype(BF16)
    return exp_mat, ind4


def _expand_heads(v):
    return jnp.repeat(v[:, :SSD_HEADS], HEAD_DIM, axis=1)


def _ssd_prep(dtr, dtb, alog, exp_mat):
    dt, a, cs, cst, tril, lane = _ssd_common(dtr, dtb, alog)
    return dt, a, cs, cst, tril, lane, _dot_split(dt, exp_mat, NN, 2), _dot_split(cs, exp_mat, NN, 3)


def _chunk_decay_rows(cs, g):
    parts = []
    for e in range(HEADS_PER_GROUP):
        h = g * HEADS_PER_GROUP + e
        parts.append(jnp.broadcast_to(jnp.exp(cs[SSD_CHUNK - 1:SSD_CHUNK, h:h + 1]), (HEAD_DIM, SSD_STATE)))
    return jnp.concatenate(parts, axis=0)


def _ssd_fwd(proj, act, dtb, alog, dsk, nw):
    s = proj.shape[0]
    nc = s // SSD_CHUNK
    bc_w = SSD_GROUPS * SSD_STATE
    exp_mat, _ = _ssd_constants()

    def body(z_ref, dtr_ref, xs_ref, b_ref, c_ref, dtb_ref, alog_ref, dskx_ref, nw_ref, exp_ref,
             ypre_ref, yssd_ref, hall_ref, h_scr):
        @pl.when(pl.program_id(0) == 0)
        def _():
            h_scr[...] = jnp.zeros_like(h_scr)

        dt, a, cs, cst, tril, lane, dtx, csx = _ssd_prep(dtr_ref[...], dtb_ref[...], alog_ref[...], exp_ref[...])
        cs_last_x = csx[SSD_CHUNK - 1:SSD_CHUNK, :]
        xs = xs_ref[...]
        xdt = xs * dtx
        xdtb = xdt.astype(BF16)
        xdec = (xdt * jnp.exp(cs_last_x - csx)).astype(BF16)
        ecsx = jnp.exp(csx)
        head_of_lane = lax.broadcasted_iota(jnp.int32, (1, GROUP_WIDTH), 1) // HEAD_DIM
        for g in range(SSD_GROUPS):
            gs = slice(g * GROUP_WIDTH, (g + 1) * GROUP_WIDTH)
            bg = b_ref[:, g * SSD_STATE:(g + 1) * SSD_STATE].astype(BF16)
            cg = c_ref[:, g * SSD_STATE:(g + 1) * SSD_STATE].astype(BF16)
            cb = _dot(cg, bg, NT)
            hprev = h_scr[gs, :]
            hall_ref[0, gs, :] = hprev
            gms, rhs = [], []
            xg = xdtb[:, gs]
            for e in range(HEADS_PER_GROUP):
                h = g * HEADS_PER_GROUP + e
                lm = jnp.exp(jnp.where(tril, cs[:, h:h + 1] - cst[h:h + 1, :], -1e30))
                gms.append((cb * lm).astype(BF16))
                rhs.append(jnp.where(head_of_lane == e, xg, jnp.zeros_like(xg)))
            y = _dot(jnp.concatenate(gms, axis=1), jnp.concatenate(rhs, axis=0), NN)
            y = y + ecsx[:, gs] * _dot(cg, hprev.astype(BF16), NT)
            y = y + dskx_ref[:, gs] * xs[:, gs]
            h_scr[gs, :] = hprev * _chunk_decay_rows(cs, g) + _dot(xdec[:, gs], bg, TN)
            ypre_ref[:, gs] = y
            z = z_ref[:, gs]
            yg = y * (z * _sigmoid(z))
            r = lax.rsqrt(jnp.mean(yg * yg, axis=-1, keepdims=True) + EPS)
            yssd_ref[:, gs] = (yg * r * nw_ref[:, gs]).astype(BF16)

    row_d = lambda cb: pl.BlockSpec((SSD_CHUNK, SSD_D_INNER), lambda c: (c, cb))
    small = pl.BlockSpec((1, LANE), lambda c: (0, 0))
    wide = pl.BlockSpec((1, SSD_D_INNER), lambda c: (0, 0))
    return pl.pallas_call(
        body, name="ssd_fwd", grid=(nc,),
        in_specs=[row_d(OFF_Z // SSD_D_INNER),
                  pl.BlockSpec((SSD_CHUNK, LANE), lambda c: (c, OFF_DT // LANE)),
                  row_d(0),
                  pl.BlockSpec((SSD_CHUNK, bc_w), lambda c: (c, SSD_D_INNER // bc_w)),
                  pl.BlockSpec((SSD_CHUNK, bc_w), lambda c: (c, SSD_D_INNER // bc_w + 1)),
                  small, small, wide, wide, pl.BlockSpec((LANE, SSD_D_INNER), lambda c: (0, 0))],
        out_specs=[row_d(0), row_d(0), pl.BlockSpec((1, SSD_D_INNER, SSD_STATE), lambda c: (c, 0, 0))],
        out_shape=[jax.ShapeDtypeStruct((s, SSD_D_INNER), F32), jax.ShapeDtypeStruct((s, SSD_D_INNER + ATT_D), BF16),
                   jax.ShapeDtypeStruct((nc, SSD_D_INNER, SSD_STATE), F32)],
        scratch_shapes=[pltpu.VMEM((SSD_D_INNER, SSD_STATE), F32)],
        compiler_params=_cparams(("arbitrary",)))(proj, proj, act, act, act, dtb, alog, _expand_heads(dsk), nw, exp_mat)


def _ssd_bwd(dycat, ypre, proj, act, hall, dtb, alog, dsk, nw, comm=None):
    s = proj.shape[0]
    nc = s // SSD_CHUNK
    bc_w = SSD_GROUPS * SSD_STATE

    exp_mat, ind4 = _ssd_constants()
    seg_passes = 1

    def body(dy_ref, ypre_ref, z_ref, dtr_ref, xs_ref, b_ref, c_ref, hall_ref, dtb_ref, alog_ref, dskx_ref, nw_ref,
             exp_ref, ind4_ref, dz_ref, dact_ref, ddtr_ref, da_ref, ddsk_ref, ddtb_ref, dnw_ref, dh_scr):
        @pl.when(pl.program_id(0) == 0)
        def _():
            dh_scr[...] = jnp.zeros_like(dh_scr)
            da_ref[...] = jnp.zeros_like(da_ref)
            ddsk_ref[...] = jnp.zeros_like(ddsk_ref)
            ddtb_ref[...] = jnp.zeros_like(ddtb_ref)
            dnw_ref[...] = jnp.zeros_like(dnw_ref)

        dtr = dtr_ref[...]
        dt, a, cs, cst, tril, lane, dtx, csx = _ssd_prep(dtr, dtb_ref[...], alog_ref[...], exp_ref[...])
        cs_last_x = csx[SSD_CHUNK - 1:SSD_CHUNK, :]
        xs = xs_ref[...]
        xdt = xs * dtx
        xdtb = xdt.astype(BF16)
        decx = jnp.exp(cs_last_x - csx)
        xdecf = xdt * decx
        xdec = xdecf.astype(BF16)
        ecsx = jnp.exp(csx)
        head_of_lane = lax.broadcasted_iota(jnp.int32, (1, GROUP_WIDTH), 1) // HEAD_DIM
        last_row = lax.broadcasted_iota(jnp.int32, (SSD_CHUNK, 1), 0) == SSD_CHUNK - 1
        dcs_col = jnp.zeros((SSD_CHUNK, LANE), F32)
        dcs_row = jnp.zeros((SSD_CHUNK, LANE), F32)
        ddt = jnp.zeros((SSD_CHUNK, LANE), F32)
        ddsk = jnp.zeros((1, LANE), F32)
        hsum = jnp.zeros((1, LANE), F32)
        t1_sum = jnp.zeros((1, LANE), F32)
        for g in range(SSD_GROUPS):
            gs = slice(g * GROUP_WIDTH, (g + 1) * GROUP_WIDTH)
            bsl = slice(g * SSD_STATE, (g + 1) * SSD_STATE)
            exp_g = exp_ref[:, gs]
            ind4_g = ind4_ref[g * HEADS_PER_GROUP * SSD_CHUNK:(g + 1) * HEADS_PER_GROUP * SSD_CHUNK, :]
            z = z_ref[:, gs]
            sg = _sigmoid(z)
            sz = z * sg
            ypre = ypre_ref[:, gs]
            yg = ypre * sz
            r = lax.rsqrt(jnp.mean(yg * yg, axis=-1, keepdims=True) + EPS)
            nrm = yg * r
            dyo_n = dy_ref[:, gs]
            dnw_ref[:, gs] += jnp.sum(dyo_n * nrm, axis=0, keepdims=True)
            dn = dyo_n * nw_ref[:, gs]
            dyg = r * (dn - nrm * jnp.mean(dn * nrm, axis=-1, keepdims=True))
            dz_ref[:, gs] = (dyg * ypre * (sg * (1.0 + z * (1.0 - sg)))).astype(BF16)
            dy = dyg * sz

            bg = b_ref[:, bsl].astype(BF16)
            cg = c_ref[:, bsl].astype(BF16)
            cb = _dot(cg, bg, NT)
            hprev = hall_ref[0, gs, :]
            hb = hprev.astype(BF16)
            dhn = dh_scr[gs, :]
            dhb = dhn.astype(BF16)
            xs_g, xdt_g = xs[:, gs], xdtb[:, gs]
            w_off = _dot(cg, hb, NT)
            dyo = dy * ecsx[:, gs]
            dyob = dyo.astype(BF16)
            dcg = _dot(dyob, hb, NN)
            dh_y = _dot(dyob, cg, TN)
            r_st = _dot(bg, dhb, NT)
            dbg = _dot(xdec[:, gs], dhb, NN)
            dyb = dy.astype(BF16)
            gms, gmbs, lms, dys = [], [], [], []
            for e in range(HEADS_PER_GROUP):
                h = g * HEADS_PER_GROUP + e
                lm = jnp.exp(jnp.where(tril, cs[:, h:h + 1] - cst[h:h + 1, :], -1e30))
                gm = cb * lm
                lms.append(lm)
                gms.append(gm)
                gmbs.append(gm.astype(BF16))
                dys.append(jnp.where(head_of_lane == e, dyb, jnp.zeros_like(dyb)))
            dxdt = _dot(jnp.concatenate(gmbs, axis=0), jnp.concatenate(dys, axis=0), TN) + decx[:, gs] * r_st
            dcb = jnp.zeros((SSD_CHUNK, SSD_CHUNK), F32)
            mms = []
            for e in range(HEADS_PER_GROUP):
                dg = _dot(dys[e], xdt_g, NT)
                mms.append(dg * gms[e])
                dcb = dcb + dg * lms[e]
            seg = _dot_split(jnp.concatenate([dyo * w_off, xdecf[:, gs] * r_st, dxdt * xs_g, dy * xs_g], axis=0), exp_g, NT, seg_passes)
            v1, t1, ddt_g, dsk_g = [seg[i * SSD_CHUNK:(i + 1) * SSD_CHUNK] for i in range(4)]
            dcs_col = dcs_col + v1 - t1 + _dot_split(jnp.concatenate(mms, axis=1), ind4_g, NN, seg_passes)
            for t in _split_bf16(jnp.concatenate(mms, axis=0), seg_passes):
                dcs_row = dcs_row + _dot(ind4_g, t, TN)
            ddt = ddt + ddt_g
            ddsk = ddsk + jnp.sum(dsk_g, axis=0, keepdims=True)
            t1_sum = t1_sum + jnp.sum(t1, axis=0, keepdims=True)
            for e in range(HEADS_PER_GROUP):
                h = g * HEADS_PER_GROUP + e
                hs = slice(e * HEAD_DIM, (e + 1) * HEAD_DIM)
                hsum = hsum + jnp.where(lane == h, jnp.sum(dhn[hs, :] * hprev[hs, :]).reshape(1, 1), 0.0)
            dh_scr[gs, :] = dhn * _chunk_decay_rows(cs, g) + dh_y
            dcbb = dcb.astype(BF16)
            dact_ref[:, gs] = dxdt * dtx[:, gs] + dskx_ref[:, gs] * dy
            dact_ref[:, SSD_D_INNER + g * SSD_STATE:SSD_D_INNER + (g + 1) * SSD_STATE] = dbg + _dot(dcbb, cg, TN)
            dact_ref[:, SSD_D_INNER + bc_w + g * SSD_STATE:SSD_D_INNER + bc_w + (g + 1) * SSD_STATE] = dcg + _dot(dcbb, bg, NN)
        dlast = t1_sum + jnp.exp(cs[SSD_CHUNK - 1:SSD_CHUNK, :]) * hsum
        dcs = dcs_col - dcs_row.T + jnp.where(last_row, dlast, 0.0)
        row = lax.broadcasted_iota(jnp.int32, (SSD_CHUNK, SSD_CHUNK), 0)
        col = lax.broadcasted_iota(jnp.int32, (SSD_CHUNK, SSD_CHUNK), 1)
        dda = _dot((col >= row).astype(F32), dcs, NN, precision=HIGHEST)
        ddt = ddt + dda * a
        da_ref[...] += jnp.sum(dda * dt, axis=0, keepdims=True)
        ddtr = jnp.where(lane < SSD_HEADS, ddt * _sigmoid(dtr + dtb_ref[...]), 0.0)
        ddtr_ref[...] = ddtr.astype(BF16)
        ddtb_ref[...] += jnp.sum(ddtr, axis=0, keepdims=True)
        ddsk_ref[...] += ddsk

    rev = lambda c: nc - 1 - c
    row_d = lambda cb: pl.BlockSpec((SSD_CHUNK, SSD_D_INNER), lambda c: (rev(c), cb))
    small = pl.BlockSpec((1, LANE), lambda c: (0, 0))
    wide = pl.BlockSpec((1, SSD_D_INNER), lambda c: (0, 0))
    small_shape = jax.ShapeDtypeStruct((1, LANE), F32)
    return _pcall(
        body, (dycat, ypre, proj, proj, act, act, act, hall, dtb, alog, _expand_heads(dsk), nw, exp_mat, ind4),
        name="ssd_bwd", grid=(nc,),
        in_specs=[row_d(0), row_d(0), row_d(OFF_Z // SSD_D_INNER),
                  pl.BlockSpec((SSD_CHUNK, LANE), lambda c: (rev(c), OFF_DT // LANE)),
                  row_d(0),
                  pl.BlockSpec((SSD_CHUNK, bc_w), lambda c: (rev(c), SSD_D_INNER // bc_w)),
                  pl.BlockSpec((SSD_CHUNK, bc_w), lambda c: (rev(c), SSD_D_INNER // bc_w + 1)),
                  pl.BlockSpec((1, SSD_D_INNER, SSD_STATE), lambda c: (rev(c), 0, 0)),
                  small, small, wide, wide, pl.BlockSpec((LANE, SSD_D_INNER), lambda c: (0, 0)),
                  pl.BlockSpec((SSD_HEADS * SSD_CHUNK, LANE), lambda c: (0, 0))],
        out_specs=[row_d(0), pl.BlockSpec((SSD_CHUNK, CONV_CH), lambda c: (rev(c), 0)),
                   pl.BlockSpec((SSD_CHUNK, LANE), lambda c: (rev(c), 0)), small, small, small, wide],
        out_shape=[jax.ShapeDtypeStruct((s, SSD_D_INNER), BF16), jax.ShapeDtypeStruct((s, CONV_CH), F32),
                   jax.ShapeDtypeStruct((s, LANE), BF16), small_shape, small_shape, small_shape,
                   jax.ShapeDtypeStruct((1, SSD_D_INNER), F32)],
        scratch_shapes=[pltpu.VMEM((SSD_D_INNER, SSD_STATE), F32)], sem=("arbitrary",), comm=comm)


def _head_mean_matrix():
    row = lax.broadcasted_iota(jnp.int32, (LANE, LANE), 0) // HEAD_DIM
    col = lax.broadcasted_iota(jnp.int32, (LANE, LANE), 1) // HEAD_DIM
    return (row == col).astype(F32)


def _head_sum2(v, ones_bd):
    hi = v.astype(BF16)
    lo = (v - hi.astype(F32)).astype(BF16)
    return _dot(jnp.concatenate([hi, lo], axis=1), jnp.concatenate([ones_bd, ones_bd], axis=0), NN)


def _head_norms(xs, ws, ones_bd):
    sums = [_head_sum2(x * x, ones_bd) for x in xs]
    rs = [lax.rsqrt(ms * (1.0 / HEAD_DIM) + EPS) for ms in sums]
    return [(x * r) * w for x, r, w in zip(xs, rs, ws)], rs


def _head_norms_bwd(dns, xs, ws, rs, ones_bd):
    nrms = [x * r for x, r in zip(xs, rs)]
    dnws = [dn * w for dn, w in zip(dns, ws)]
    projs = [_head_sum2(dnw * nrm, ones_bd) for dnw, nrm in zip(dnws, nrms)]
    dxs = [r * (dnw - nrm * (pr * (1.0 / HEAD_DIM))) for r, dnw, nrm, pr in zip(rs, dnws, nrms, projs)]
    return dxs, [jnp.sum(dn * nrm, axis=0, keepdims=True) for dn, nrm in zip(dns, nrms)]


NORM_CHUNKS = 2


PRO_ROWS = 256
ATT_GROUP_FWD = 32
ATT_GROUP_BWD = 8
KEYS = 2 * ATT_BLK
NEG = -1e30
HALF = HEAD_DIM // 2


def _rows(start, size, dil):
    return pl.ds(start, size) if dil == 1 else pl.ds(start, size, stride=dil)


def _fill_bias(bias_ref):
    row = lax.broadcasted_iota(jnp.int32, (ATT_BLK, 2 * KEYS), 0)
    col = lax.broadcasted_iota(jnp.int32, (ATT_BLK, 2 * KEYS), 1) & (KEYS - 1)
    for first, off in ((0, 0), (1, ATT_BLK)):
        dist = off + row - col
        bias_ref[first] = jnp.where((dist >= 0) & (dist <= ATT_BLK), 0.0, NEG)


def _pair(a, b):
    return jnp.concatenate([jnp.broadcast_to(a, (ATT_BLK, KEYS)), jnp.broadcast_to(b, (ATT_BLK, KEYS))], axis=1)


def _split_heads(x, is_a):
    zero = jnp.zeros_like(x)
    return jnp.concatenate([jnp.where(is_a, x, zero), jnp.where(is_a, zero, x)], axis=0)


def _block_ids(b, nb):
    i = b & (nb - 1)
    q0 = pl.multiple_of(b * ATT_BLK, ATT_BLK)
    k0 = pl.multiple_of((b - jnp.minimum(i, 1)) * ATT_BLK, ATT_BLK)
    return pl.ds(q0, ATT_BLK), pl.ds(k0, KEYS), jnp.minimum(i, 1)


def _natural_rows(b, nb, dil):
    if dil == 1:
        return pl.ds(pl.multiple_of(b * ATT_BLK, ATT_BLK), ATT_BLK)
    return pl.ds(b // nb + dil * ((b & (nb - 1)) * ATT_BLK), ATT_BLK, stride=dil)


def _att_fwd(proj, qw, kw, comm=None):
    s = proj.shape[0]
    nblk = s // ATT_BLK
    assert all((s // d) // ATT_BLK >= 2 for d in DILATIONS)
    blk = lambda off: pl.BlockSpec((s, LANE), lambda i: (0, off // LANE + i))
    wspec = pl.BlockSpec((1, LANE), lambda i: (0, i))
    oblk = pl.BlockSpec((s, LANE), lambda i: (0, i))

    def body(q_ref, k_ref, v_ref, qw_ref, kw_ref, o_ref, lse_ref, qn, kn, q_cm, k_cm, v_cm, m_acc, l_acc, o_d, m_d, l_d,
             o_e, m_e, l_e, bias):
        ones_bd = _head_mean_matrix().astype(BF16)
        is_a = lax.broadcasted_iota(jnp.int32, (1, LANE), 1) < HEAD_DIM
        ones_ext = _split_heads(jnp.ones((KEYS, LANE), BF16), is_a)
        _fill_bias(bias)

        def pro(j, c):
            chunks = [pl.ds(pl.multiple_of((NORM_CHUNKS * j + u) * PRO_ROWS, PRO_ROWS), PRO_ROWS) for u in range(NORM_CHUNKS)]
            normed, _ = _head_norms([q_ref[rows, :] for rows in chunks] + [k_ref[rows, :] for rows in chunks],
                                    [qw_ref[...] * HEAD_DIM ** -0.5] * NORM_CHUNKS + [kw_ref[...]] * NORM_CHUNKS, ones_bd)
            for u, rows in enumerate(chunks):
                qn[rows, :] = normed[u]
                kn[rows, :] = normed[NORM_CHUNKS + u]
            return c

        lax.fori_loop(0, s // (NORM_CHUNKS * PRO_ROWS), pro, 0)

        results = dict(zip(DILATIONS, ((o_ref, m_acc, l_acc), (o_d, m_d, l_d), (o_e, m_e, l_e))))
        for dil in DILATIONS:
            ln = s // dil
            nb = ln // ATT_BLK
            o_out, m_out, l_out = results[dil]
            for r in range(dil):
                def relayout(j, c, dil=dil, r=r, ln=ln):
                    j0 = pl.multiple_of(j * PRO_ROWS, PRO_ROWS)
                    src = _rows(r + dil * j0, PRO_ROWS, dil)
                    dst = pl.ds(r * ln + j0, PRO_ROWS)
                    q_cm[dst, :] = qn[src, :].astype(BF16)
                    k_cm[dst, :] = kn[src, :].astype(BF16)
                    v_cm[dst, :] = v_ref[src, :].astype(BF16)
                    return c

                lax.fori_loop(0, ln // PRO_ROWS, relayout, 0)

            def step(bg, c, nb=nb, o_out=o_out, m_out=m_out, l_out=l_out):
                ids = [_block_ids(bg * ATT_GROUP_FWD + u, nb) for u in range(ATT_GROUP_FWD)]
                kbs = [_split_heads(k_cm[krows, :], is_a) for _, krows, _ in ids]
                scs = [_dot(q_cm[qrows, :], kb, NT) + bias[first] for (qrows, _, first), kb in zip(ids, kbs)]
                mas = [jnp.max(sc[:, :KEYS], axis=-1, keepdims=True) for sc in scs]
                mbs = [jnp.max(sc[:, KEYS:], axis=-1, keepdims=True) for sc in scs]
                ps = [jnp.exp(sc - _pair(ma, mb)).astype(BF16) for sc, ma, mb in zip(scs, mas, mbs)]
                vbs = [jnp.concatenate([_split_heads(v_cm[krows, :], is_a), ones_ext], axis=1) for _, krows, _ in ids]
                ols = [_dot(p, vb, NN) for p, vb in zip(ps, vbs)]
                for (qrows, _, _), ol, ma, mb in zip(ids, ols, mas, mbs):
                    o_out[qrows, :] = ol[:, :LANE]
                    l_out[qrows, :] = ol[:, LANE:]
                    m_out[qrows, :] = jnp.where(is_a, ma, mb)
                return c

            lax.fori_loop(0, nblk // ATT_GROUP_FWD, step, 0)

        for level in range(len(DILATIONS) - 1, 0, -1):
            fine_d, coarse_d = DILATIONS[level - 1], DILATIONS[level]
            ratio, ln_f, ln_c = coarse_d // fine_d, s // fine_d, s // coarse_d
            (o_f, m_f, l_f), (o_c, m_c, l_c) = results[fine_d], results[coarse_d]
            for r in range(coarse_d):
                def merge(j, c, r=r, ratio=ratio, ln_c=ln_c, start=(r % fine_d) * ln_f + r // fine_d,
                          o_f=o_f, m_f=m_f, l_f=l_f, o_c=o_c, m_c=m_c, l_c=l_c):
                    j0 = pl.multiple_of(j * PRO_ROWS, PRO_ROWS)
                    fine = _rows(start + ratio * j0, PRO_ROWS, ratio)
                    coarse = pl.ds(r * ln_c + j0, PRO_ROWS)
                    m_old, m_new = m_f[fine, :], m_c[coarse, :]
                    m = jnp.maximum(m_old, m_new)
                    a_old, a_new = jnp.exp(m_old - m), jnp.exp(m_new - m)
                    o_f[fine, :] = a_old * o_f[fine, :] + a_new * o_c[coarse, :]
                    l_f[fine, :] = a_old * l_f[fine, :] + a_new * l_c[coarse, :]
                    m_f[fine, :] = m
                    return c

                lax.fori_loop(0, ln_c // PRO_ROWS, merge, 0)

        def epi(j, c):
            rows = pl.ds(pl.multiple_of(j * PRO_ROWS, PRO_ROWS), PRO_ROWS)
            l = l_acc[rows, :]
            o_ref[rows, :] = o_ref[rows, :] / l
            lse_ref[rows, :] = m_acc[rows, :] + jnp.log(l)
            return c

        lax.fori_loop(0, s // PRO_ROWS, epi, 0)

    f = jax.ShapeDtypeStruct((s, ATT_D), F32)
    scr = pltpu.VMEM((s, LANE), F32)
    scb = pltpu.VMEM((s, LANE), BF16)
    return _pcall(
        body, (proj, proj, proj, qw, kw), name="att_fwd", grid=(ATT_D // LANE,),
        in_specs=[blk(OFF_Q), blk(OFF_K), blk(OFF_V), wspec, wspec], out_specs=[oblk, oblk], out_shape=[f, f],
        scratch_shapes=[scr, scr, scb, scb, scb] + [scr] * 8 + [pltpu.VMEM((2, ATT_BLK, 2 * KEYS), F32)],
        sem=("parallel",), comm=comm)


def _att_bwd(proj, do, stats, qw, kw, comm=None):
    s = proj.shape[0]
    nblk = s // ATT_BLK
    blk = lambda off: pl.BlockSpec((s, LANE), lambda i: (0, off // LANE + i))
    wspec = pl.BlockSpec((1, LANE), lambda i: (0, i))
    oblk = pl.BlockSpec((s, LANE), lambda i: (0, i))

    def body(q_ref, k_ref, v_ref, do_ref, st_ref, qw_ref, kw_ref, dq_ref, dk_ref, dv_ref, dqw_ref, dkw_ref,
             qn, kn, q_cm, do_cm, k_cm, v_cm, rms, dq_acc, dk_acc, dv_acc, dq_d, dk_d, dv_d, bias):
        ones_bd = _head_mean_matrix().astype(BF16)
        is_a = lax.broadcasted_iota(jnp.int32, (1, LANE), 1) < HEAD_DIM
        first_half = (lax.broadcasted_iota(jnp.int32, (1, LANE), 1) & (HEAD_DIM - 1)) < HALF
        _fill_bias(bias)
        zero = jnp.zeros((PRO_ROWS, LANE), F32)

        def pro(j, c):
            chunks = [pl.ds(pl.multiple_of((NORM_CHUNKS * j + u) * PRO_ROWS, PRO_ROWS), PRO_ROWS) for u in range(NORM_CHUNKS)]
            normed, rs = _head_norms([q_ref[rows, :] for rows in chunks] + [k_ref[rows, :] for rows in chunks],
                                     [qw_ref[...] * HEAD_DIM ** -0.5] * NORM_CHUNKS + [kw_ref[...]] * NORM_CHUNKS, ones_bd)
            for u, rows in enumerate(chunks):
                qn[rows, :] = normed[u]
                kn[rows, :] = normed[NORM_CHUNKS + u]
                rms[rows, :] = jnp.where(first_half, rs[u], rs[NORM_CHUNKS + u])
                dk_acc[rows, :] = zero
                dv_acc[rows, :] = zero
            return c

        lax.fori_loop(0, s // (NORM_CHUNKS * PRO_ROWS), pro, 0)

        for dil in DILATIONS:
            ln = s // dil
            nb = ln // ATT_BLK
            dq_o, dk_o, dv_o = (dq_acc, dk_acc, dv_acc) if dil == 1 else (dq_d, dk_d, dv_d)
            for r in range(dil):
                def relayout(j, c, dil=dil, r=r, ln=ln):
                    j0 = pl.multiple_of(j * PRO_ROWS, PRO_ROWS)
                    src = _rows(r + dil * j0, PRO_ROWS, dil)
                    dst = pl.ds(r * ln + j0, PRO_ROWS)
                    q_cm[dst, :] = qn[src, :].astype(BF16)
                    k_cm[dst, :] = kn[src, :].astype(BF16)
                    v_cm[dst, :] = v_ref[src, :].astype(BF16)
                    do_cm[dst, :] = do_ref[src, :].astype(BF16)
                    if dil > 1:
                        dk_d[dst, :] = zero
                        dv_d[dst, :] = zero
                    return c

                lax.fori_loop(0, ln // PRO_ROWS, relayout, 0)

            def step(bg, c, nb=nb, dil=dil, dq_o=dq_o, dk_o=dk_o, dv_o=dv_o):
                blocks = [bg * ATT_GROUP_BWD + u for u in range(ATT_GROUP_BWD)]
                ids = [_block_ids(b, nb) for b in blocks]
                qbs = [q_cm[qrows, :] for qrows, _, _ in ids]
                dobs = [do_cm[qrows, :] for qrows, _, _ in ids]
                kbs = [_split_heads(k_cm[krows, :], is_a) for _, krows, _ in ids]
                vbs = [_split_heads(v_cm[krows, :], is_a) for _, krows, _ in ids]
                sts = [st_ref[_natural_rows(b, nb, dil), :] for b in blocks]
                scs = [_dot(qb, kb, NT) + bias[first] for qb, kb, (_, _, first) in zip(qbs, kbs, ids)]
                dps = [_dot(dob, vb, NT) for dob, vb in zip(dobs, vbs)]
                ps = [jnp.exp(sc - _pair(st[:, 0:1], st[:, HEAD_DIM:HEAD_DIM + 1])) for sc, st in zip(scs, sts)]
                dss = [(p * (dp - _pair(st[:, HALF:HALF + 1], st[:, HEAD_DIM + HALF:HEAD_DIM + HALF + 1]))).astype(BF16)
                       for p, dp, st in zip(ps, dps, sts)]
                dqs = [_dot(ds, kb, NN) for ds, kb in zip(dss, kbs)]
                dkfs = [_dot(ds, qb, TN) for ds, qb in zip(dss, qbs)]
                dvfs = [_dot(p.astype(BF16), dob, TN) for p, dob in zip(ps, dobs)]
                for (qrows, krows, _), dq, dkf, dvf in zip(ids, dqs, dkfs, dvfs):
                    dq_o[qrows, :] = dq
                    dk_o[krows, :] += jnp.where(is_a, dkf[:KEYS], dkf[KEYS:])
                    dv_o[krows, :] += jnp.where(is_a, dvf[:KEYS], dvf[KEYS:])
                return c

            lax.fori_loop(0, nblk // ATT_GROUP_BWD, step, 0)

            if dil > 1:
                for r in range(dil):
                    def merge(j, c, dil=dil, r=r, ln=ln):
                        j0 = pl.multiple_of(j * PRO_ROWS, PRO_ROWS)
                        nat = _rows(r + dil * j0, PRO_ROWS, dil)
                        cm = pl.ds(r * ln + j0, PRO_ROWS)
                        dq_acc[nat, :] += dq_d[cm, :]
                        dk_acc[nat, :] += dk_d[cm, :]
                        dv_acc[nat, :] += dv_d[cm, :]
                        return c

                    lax.fori_loop(0, ln // PRO_ROWS, merge, 0)

        def epi(j, c):
            chunks = [pl.ds(pl.multiple_of((NORM_CHUNKS * j + u) * PRO_ROWS, PRO_ROWS), PRO_ROWS) for u in range(NORM_CHUNKS)]
            packed = [rms[rows, :] for rows in chunks]
            rs = ([jnp.where(first_half, p, pltpu.roll(p, HALF, axis=1)) for p in packed]
                  + [jnp.where(first_half, pltpu.roll(p, LANE - HALF, axis=1), p) for p in packed])
            dxs, dws = _head_norms_bwd(
                [dq_acc[rows, :] for rows in chunks] + [dk_acc[rows, :] for rows in chunks],
                [q_ref[rows, :] for rows in chunks] + [k_ref[rows, :] for rows in chunks],
                [qw_ref[...] * HEAD_DIM ** -0.5] * NORM_CHUNKS + [kw_ref[...]] * NORM_CHUNKS, rs, ones_bd)
            dqw, dkw = c
            for u, rows in enumerate(chunks):
                dq_ref[rows, :] = dxs[u].astype(BF16)
                dk_ref[rows, :] = dxs[NORM_CHUNKS + u].astype(BF16)
                dv_ref[rows, :] = dv_acc[rows, :].astype(BF16)
                dqw, dkw = dqw + dws[u], dkw + dws[NORM_CHUNKS + u]
            return dqw, dkw

        zrow = jnp.zeros((1, LANE), F32)
        dqw, dkw = lax.fori_loop(0, s // (NORM_CHUNKS * PRO_ROWS), epi, (zrow, zrow))
        dqw_ref[...] = dqw * HEAD_DIM ** -0.5
        dkw_ref[...] = dkw

    o = jax.ShapeDtypeStruct((s, ATT_D), BF16)
    ov = jax.ShapeDtypeStruct((1, ATT_D), F32)
    scr = pltpu.VMEM((s, LANE), F32)
    scb = pltpu.VMEM((s, LANE), BF16)
    return _pcall(
        body, (proj, proj, proj, do, stats, qw, kw), name="att_bwd", grid=(ATT_D // LANE,),
        in_specs=[blk(OFF_Q), blk(OFF_K), blk(OFF_V), oblk, oblk, wspec, wspec],
        out_specs=[oblk, oblk, oblk, wspec, wspec], out_shape=[o, o, o, ov, ov],
        scratch_shapes=[scr, scr, scb, scb, scb, scb, scr, scr, scr, scr, scr, scr, scr, pltpu.VMEM((2, ATT_BLK, 2 * KEYS), F32)],
        sem=("parallel",), comm=comm)


def _att_norm_fwd(o, nw, ycat):
    s = o.shape[0]
    row = pl.BlockSpec((ROW_TILE, ATT_D), lambda i: (i, 0))
    vec = pl.BlockSpec((1, ATT_D), lambda i: (0, 0))

    def body(o_ref, nw_ref, ycat_ref, y_ref):
        o = o_ref[...]
        r = lax.rsqrt(jnp.mean(o * o, axis=-1, keepdims=True) + EPS)
        y_ref[...] = (o * r * nw_ref[...]).astype(BF16)

    return pl.pallas_call(body, name="att_norm_fwd", grid=(s // ROW_TILE,),
                          in_specs=[row, vec, pl.BlockSpec(memory_space=pl.ANY)],
                          out_specs=pl.BlockSpec((ROW_TILE, ATT_D), lambda i: (i, 1)),
                          out_shape=jax.ShapeDtypeStruct(ycat.shape, BF16), input_output_aliases={2: 0},
                          compiler_params=_cparams(("parallel",)))(o, nw, ycat)


def _mixer_split_epilogue(dycat, first, rows, vecs, outs):
    (o_ref, lse_ref), (nw_ref,), (dyssd_ref, do_ref, st_ref, dnw_ref) = rows, vecs, outs

    @pl.when(first)
    def _():
        dnw_ref[...] = jnp.zeros_like(dnw_ref)

    dyssd_ref[...] = dycat[:, :SSD_D_INNER]
    dy = dycat[:, SSD_D_INNER:]
    o = o_ref[...]
    r = lax.rsqrt(jnp.mean(o * o, axis=-1, keepdims=True) + EPS)
    nrm = o * r
    dnw_ref[...] += jnp.sum(dy * nrm, axis=0, keepdims=True)
    dn = dy * nw_ref[...]
    do = r * (dn - nrm * jnp.mean(dn * nrm, axis=-1, keepdims=True))
    do_ref[...] = do
    ones_bd = _head_mean_matrix().astype(BF16)
    prod = do * o
    delta = jnp.concatenate([_head_sum2(prod[:, j * LANE:(j + 1) * LANE], ones_bd) for j in range(ATT_D // LANE)], axis=1)
    lane = lax.broadcasted_iota(jnp.int32, (1, ATT_D), 1)
    st_ref[...] = jnp.where((lane & (HEAD_DIM - 1)) < HALF, lse_ref[...], delta)


def _ada_fwd(c_all, w_ada):
    def body(c_ref, w_ref, o_ref):
        cv = c_ref[...]
        o_ref[...] = _dot((cv * _sigmoid(cv)).astype(BF16), w_ref[...].astype(BF16), NN)

    return pl.pallas_call(body, name="ada_fwd", out_shape=jax.ShapeDtypeStruct((c_all.shape[0], w_ada.shape[1]), F32),
                          compiler_params=_cparams())(c_all, w_ada)


def _adamw_math(g, w, m, v):
    m_new = ADAM_B1 * m + (1.0 - ADAM_B1) * g
    v_new = ADAM_B2 * v + (1.0 - ADAM_B2) * (g * g)
    m_hat = m_new / (1.0 - ADAM_B1 ** ADAM_STEP)
    v_hat = v_new / (1.0 - ADAM_B2 ** ADAM_STEP)
    delta = -ADAM_LR * (m_hat / (jnp.sqrt(v_hat) + ADAM_EPS) + ADAM_WD * w)
    return delta, m_new, v_new


def _ada_bwd_adamw(c_all, dmod_cols, w, m, v):
    rows, cols = w.shape
    tr = 256
    blk = pl.BlockSpec((tr, cols), lambda i: (i, 0))

    def body(c_ref, d_ref, w_ref, m_ref, v_ref, g_ref, dl_ref, mo_ref, vo_ref):
        cv = c_ref[...]
        ca = cv * _sigmoid(cv)
        g = ca[:, 0:1] * d_ref[0:1, :]
        for b in range(1, N_DEV):
            g = g + ca[:, b:b + 1] * d_ref[b:b + 1, :]
        g_ref[...] = g
        dl_ref[...], mo_ref[...], vo_ref[...] = _adamw_math(g, w_ref[...], m_ref[...], v_ref[...])

    o = jax.ShapeDtypeStruct((rows, cols), F32)
    return pl.pallas_call(
        body, name="ada_bwd_adamw", grid=(rows // tr,),
        in_specs=[pl.BlockSpec((tr, N_DEV), lambda i: (i, 0)), pl.BlockSpec((N_DEV, cols), lambda i: (0, 0)), blk, blk, blk],
        out_specs=[blk] * 4, out_shape=[o, o, o, o], compiler_params=_cparams(("parallel",)))(c_all.T, dmod_cols, w, m, v)


def _reduce_adamw(slabs, w, m, v, name):
    rows, cols = w.shape
    n_src = slabs.shape[0]
    if rows % 128 == 0:
        tr, steps = 128, rows // 128
        blk = pl.BlockSpec((tr, cols), lambda i: (i, 0))
        sblk = pl.BlockSpec((n_src, tr, cols), lambda i: (0, i, 0))
    else:
        tc, steps = 256, cols // 256
        blk = pl.BlockSpec((rows, tc), lambda i: (0, i))
        sblk = pl.BlockSpec((n_src, rows, tc), lambda i: (0, 0, i))

    def body(s_ref, w_ref, m_ref, v_ref, g_ref, dl_ref, mo_ref, vo_ref):
        g = s_ref[0].astype(F32)
        for src in range(1, n_src):
            g = g + s_ref[src].astype(F32)
        g_ref[...] = g
        dl_ref[...], mo_ref[...], vo_ref[...] = _adamw_math(g, w_ref[...], m_ref[...], v_ref[...])

    o = jax.ShapeDtypeStruct((rows, cols), F32)
    return pl.pallas_call(
        body, name=name, grid=(steps,), in_specs=[sblk, blk, blk, blk],
        out_specs=[blk] * 4, out_shape=[o, o, o, o], compiler_params=_cparams(("parallel",)))(slabs, w, m, v)


def _small_reduce_adamw(gathered, w, m, v):
    def body(s_ref, w_ref, m_ref, v_ref, g_ref, dl_ref, mo_ref, vo_ref):
        g = s_ref[0]
        for dev in range(1, N_DEV):
            g = g + s_ref[dev]
        g_ref[...] = g
        dl_ref[...], mo_ref[...], vo_ref[...] = _adamw_math(g, w_ref[...], m_ref[...], v_ref[...])

    o = jax.ShapeDtypeStruct(w.shape, F32)
    return pl.pallas_call(body, name="small_reduce_adamw", out_shape=[o, o, o, o], compiler_params=_cparams())(gathered, w, m, v)


def _adamw_small(g, w, m, v, name):
    def body(g_ref, w_ref, m_ref, v_ref, dl_ref, mo_ref, vo_ref):
        dl_ref[...], mo_ref[...], vo_ref[...] = _adamw_math(g_ref[...], w_ref[...], m_ref[...], v_ref[...])

    o = jax.ShapeDtypeStruct(w.shape, F32)
    return pl.pallas_call(body, name=name, out_shape=[o, o, o], compiler_params=_cparams())(g, w, m, v)


class _Exchange:
    def __init__(self, arrs, scatter):
        self.arrs, self.scatter, self.n = list(arrs), scatter, len(arrs)
        hbm = pl.BlockSpec(memory_space=pltpu.HBM)
        self.in_specs = [hbm] * self.n
        self.out_specs = [hbm] * self.n
        self.out_shape = [jax.ShapeDtypeStruct(a.shape if scatter else (N_DEV,) + a.shape, a.dtype) for a in self.arrs]
        self.scratch = [pltpu.SemaphoreType.DMA((self.n * (N_DEV - 1),)), pltpu.SemaphoreType.DMA((self.n * (N_DEV - 1),)),
                        pltpu.SemaphoreType.DMA((self.n,))]

    def _local(self, ins, outs, sems):
        me = 4 * lax.axis_index("x") + 2 * lax.axis_index("y") + lax.axis_index("c")
        return [pltpu.make_async_copy(ins[a].at[me] if self.scatter else ins[a], outs[a].at[me], sems[2].at[a])
                for a in range(self.n)]

    def _remote(self, ins, outs, sems, arriving):
        send_sems, recv_sems, _ = sems
        x, y, c = lax.axis_index("x"), lax.axis_index("y"), lax.axis_index("c")
        me = 4 * x + 2 * y + c
        remote = []
        for a in range(self.n):
            for k in range(1, N_DEV):
                px = 1 - x if k & 4 else x
                py = 1 - y if k & 2 else y
                pc = 1 - c if k & 1 else c
                peer = 4 * px + 2 * py + pc
                sem = a * (N_DEV - 1) + k - 1
                remote.append(pltpu.make_async_remote_copy(
                    src_ref=ins[a].at[peer] if self.scatter else ins[a], dst_ref=outs[a].at[peer if arriving else me],
                    send_sem=send_sems.at[sem], recv_sem=recv_sems.at[sem], device_id=(px, py, pc), device_id_type=MESH_IDS))
        return remote

    def start(self, ins, outs, sems):
        for cp in self._local(ins, outs, sems) + self._remote(ins, outs, sems, arriving=False):
            cp.start()

    def forward(self, ins, outs, sems):
        pass

    def wait(self, ins, outs, sems):
        for send, arrival in zip(self._remote(ins, outs, sems, arriving=False), self._remote(ins, outs, sems, arriving=True)):
            send.wait_send()
            arrival.wait_recv()
        for cp in self._local(ins, outs, sems):
            cp.wait()


N_CHIP = N_DEV // 2


class _SiblingSwap(_Exchange):
    def __init__(self, arrs):
        super().__init__(arrs, scatter=True)
        self.out_shape = [jax.ShapeDtypeStruct((N_CHIP,) + a.shape[2:], a.dtype) for a in self.arrs]
        self.scratch = [pltpu.SemaphoreType.DMA((self.n,)), pltpu.SemaphoreType.DMA((self.n,)), pltpu.SemaphoreType.DMA((1,))]

    def _copies(self, ins, outs, sems):
        x, y, c = lax.axis_index("x"), lax.axis_index("y"), lax.axis_index("c")
        return [pltpu.make_async_remote_copy(src_ref=ins[a].at[:, 1 - c], dst_ref=outs[a], send_sem=sems[0].at[a], recv_sem=sems[1].at[a],
                                             device_id=(x, y, 1 - c), device_id_type=MESH_IDS) for a in range(self.n)]

    def start(self, ins, outs, sems):
        for cp in self._copies(ins, outs, sems):
            cp.start()

    def wait(self, ins, outs, sems):
        for cp in self._copies(ins, outs, sems):
            cp.wait()


class _ChipScatter(_Exchange):
    def __init__(self, arrs):
        super().__init__(arrs, scatter=True)
        n_pairs = self.n * (N_CHIP - 1)
        self.scratch = [pltpu.SemaphoreType.DMA((n_pairs,)), pltpu.SemaphoreType.DMA((n_pairs,)), pltpu.SemaphoreType.DMA((self.n,))]

    def _local(self, ins, outs, sems):
        chip = 2 * lax.axis_index("x") + lax.axis_index("y")
        return [pltpu.make_async_copy(ins[a].at[chip], outs[a].at[chip], sems[2].at[a]) for a in range(self.n)]

    def _remote(self, ins, outs, sems, arriving):
        send_sems, recv_sems, _ = sems
        x, y, c = lax.axis_index("x"), lax.axis_index("y"), lax.axis_index("c")
        chip = 2 * x + y
        remote = []
        for a in range(self.n):
            for k in range(1, N_CHIP):
                px = 1 - x if k & 2 else x
                py = 1 - y if k & 1 else y
                peer = 2 * px + py
                sem = a * (N_CHIP - 1) + k - 1
                remote.append(pltpu.make_async_remote_copy(
                    src_ref=ins[a].at[peer], dst_ref=outs[a].at[peer if arriving else chip], send_sem=send_sems.at[sem],
                    recv_sem=recv_sems.at[sem], device_id=(px, py, c), device_id_type=MESH_IDS))
        return remote


def _chip_sum(mine, theirs):
    n, rows, cols = mine.shape
    blk = pl.BlockSpec((1, rows, 256), lambda q, j: (q, 0, j))

    def body(a_ref, b_ref, o_ref):
        o_ref[...] = (a_ref[...].astype(F32) + b_ref[...].astype(F32)).astype(BF16)

    return pl.pallas_call(body, name="chip_sum", grid=(n, cols // 256), in_specs=[blk, blk], out_specs=blk,
                          out_shape=jax.ShapeDtypeStruct(mine.shape, BF16),
                          compiler_params=_cparams(("parallel", "parallel")))(mine, theirs)


class _Gather2(_Exchange):
    def __init__(self, arrs):
        super().__init__(arrs, scatter=False)

    def _copies(self, ins, outs, sems):
        send_sems, recv_sems, _ = sems
        x, y, c = lax.axis_index("x"), lax.axis_index("y"), lax.axis_index("c")
        sibling = (x, y, 1 - c)
        chips = [(1 - x, y), (x, 1 - y), (1 - x, 1 - y)]
        first, passed, landed = [], [], []
        for a in range(self.n):
            def copy(k, block, to, src=None, a=a):
                slab = outs[a].at[4 * block[0] + 2 * block[1] + block[2]]
                return pltpu.make_async_remote_copy(
                    src_ref=slab if src is None else src, dst_ref=slab, send_sem=send_sems.at[a * (N_DEV - 1) + k],
                    recv_sem=recv_sems.at[a * (N_DEV - 1) + k], device_id=to, device_id_type=MESH_IDS)

            first.append(copy(0, (x, y, c), sibling, src=ins[a]))
            landed.append(copy(0, sibling, sibling))
            for j, chip in enumerate(chips):
                first.append(copy(1 + j, (x, y, c), (*chip, c), src=ins[a]))
                passed.append((copy(1 + j, (*chip, c), sibling), copy(4 + j, (*chip, c), sibling)))
                landed.append(copy(4 + j, (*chip, 1 - c), sibling))
        return first, passed, landed

    def start(self, ins, outs, sems):
        for cp in self._local(ins, outs, sems) + self._copies(ins, outs, sems)[0]:
            cp.start()

    def forward(self, ins, outs, sems):
        for arrival, onward in self._copies(ins, outs, sems)[1]:
            arrival.wait_recv()
            onward.start()

    def wait(self, ins, outs, sems):
        first, passed, landed = self._copies(ins, outs, sems)
        for arrival in landed:
            arrival.wait_recv()
        for cp in first + [onward for _, onward in passed]:
            cp.wait_send()
        for cp in self._local(ins, outs, sems):
            cp.wait()


def _split_comm_refs(refs, n_in, n_out, n_scr, comm):
    nc = comm.n if comm is not None else 0
    ns = 3 if comm is not None else 0
    pos, groups = 0, []
    for cnt in (n_in, nc, n_out, nc, n_scr, ns):
        groups.append(refs[pos:pos + cnt])
        pos += cnt
    assert pos == len(refs), (pos, len(refs))
    return groups


def _pcall(body, args, *, name, grid, in_specs, out_specs, out_shape, scratch_shapes=(), sem=None, comm=None):
    in_specs, out_specs, out_shape, scratch_shapes = list(in_specs), list(out_specs), list(out_shape), list(scratch_shapes)
    n_in, n_out, n_scr = len(in_specs), len(out_specs), len(scratch_shapes)
    if comm is None:
        kernel_body = body
    else:
        def kernel_body(*refs):
            ins, cins, outs, couts, scr, sems = _split_comm_refs(refs, n_in, n_out, n_scr, comm)
            ids = [pl.program_id(a) for a in range(len(grid))]
            first, last = ids[0] == 0, ids[0] == grid[0] - 1
            for a in range(1, len(grid)):
                first, last = first & (ids[a] == 0), last & (ids[a] == grid[a] - 1)

            middle = ids[0] == (2 * grid[0]) // 3
            for a in range(1, len(grid)):
                middle = middle & (ids[a] == 0)

            @pl.when(first)
            def _():
                comm.start(cins, couts, sems)

            @pl.when(middle)
            def _():
                comm.forward(cins, couts, sems)

            body(*ins, *outs, *scr)

            @pl.when(last)
            def _():
                comm.wait(cins, couts, sems)

        in_specs, out_specs, out_shape = in_specs + comm.in_specs, out_specs + comm.out_specs, out_shape + comm.out_shape
        scratch_shapes, args = scratch_shapes + comm.scratch, list(args) + comm.arrs
        sem = ("arbitrary",) * len(grid)
    res = pl.pallas_call(kernel_body, name=name, grid=grid, in_specs=in_specs, out_specs=out_specs, out_shape=out_shape,
                         scratch_shapes=scratch_shapes, compiler_params=_cparams(sem))(*args)
    return res[:n_out], res[n_out:]


def _exchange(arrs, name, scatter=False, ex=None):
    if ex is None:
        ex = _Exchange(arrs, scatter=True) if scatter else _Gather2(arrs)

    def body(*refs):
        _, ins, _, outs, _, sems = _split_comm_refs(refs, 0, 0, 0, ex)
        ex.start(ins, outs, sems)
        ex.forward(ins, outs, sems)
        ex.wait(ins, outs, sems)

    return pl.pallas_call(body, name=name, in_specs=ex.in_specs, out_specs=ex.out_specs, out_shape=ex.out_shape,
                          scratch_shapes=ex.scratch)(*ex.arrs)


def _pad_lanes(v, width=LANE):
    return jnp.pad(v, ((0, 0), (0, width - v.shape[1])))


def _shards_to_cols(g):
    return jnp.transpose(g, (1, 0, 2)).reshape(g.shape[1], N_DEV * g.shape[2])


def _local_step(x, tgt, mod, w_in_pt, conv_w, conv_b, dt_bias, a_log, d_skip, ssd_norm_w, q_norm_w, k_norm_w,
                attn_norm_w, w_out_sh, w_ff1_sh, w_ff2_sh, norm1_w, norm2_w, core):
    shift1, scale1, gate1, shift2, scale2, gate2 = [mod[i:i + 1] for i in range(N_MOD)]
    dtb, alog, dsk = _pad_lanes(dt_bias), _pad_lanes(a_log), _pad_lanes(d_skip)
    qw, kw = jnp.tile(q_norm_w, (1, ATT_HEADS)), jnp.tile(k_norm_w, (1, ATT_HEADS))

    h1 = _norm_mod_fwd(x, norm1_w, scale1, shift1, "norm1_fwd")
    proj = _matmul(h1, w_in_pt, tb=True, tm=2048, tn=896, tk=1024, name="in_proj")
    pre, act = _conv_fwd(proj, conv_w, conv_b)
    ypre, ycat_ssd, hall = _ssd_fwd(proj, act, dtb, alog, dsk, ssd_norm_w)
    (o_att, lse), (w_out_g, w_ff1_g, w_ff2_g) = _att_fwd(proj, qw, kw, comm=_Gather2([w_out_sh, w_ff1_sh, w_ff2_sh]))
    w_out = w_out_g.reshape(2 * D_MODEL, D_MODEL)
    w_ff1 = _shards_to_cols(w_ff1_g)
    w_ff2 = w_ff2_g.reshape(D_FF, D_MODEL)
    ycat = _att_norm_fwd(o_att, attn_norm_w, ycat_ssd)
    row32, row16, vec32 = ("row", F32), ("row", BF16), ("vec", F32)
    mix, x1, h2 = _matmul_rows(ycat, w_out, _residual_norm_epilogue, [x], [gate1, norm2_w, scale2, shift2],
                               [row32, row32, row16], tm=512, name="out_proj")
    u, act_ff = _matmul(h2, w_ff1, tm=1024, tn=2048, tk=1024, name="ff1", mode="relu2")
    loss, dout, dff, dgate2 = _matmul_rows(act_ff, w_ff2, _loss_epilogue, [x1, tgt], [gate2],
                                           [("one", F32), row32, row16, vec32], tm=512, name="ff2")

    du = _matmul(dff, w_ff2, tb=True, tm=512, tn=4096, tk=1024, out_dtype=BF16, name="ff2_dx", mode="drelu2", u=u)
    g_ff2 = _matmul(act_ff, dff, ta=True, tm=512, tn=1024, tk=4096, out_dtype=BF16, name="ff2_dw")
    dx1, dshift2, dscale2, g_norm2, dmix, dgate1 = _matmul_rows(
        du, w_ff1, _norm_bwd_epilogue, [x1, dout, mix], [norm2_w, scale2, gate1],
        [row32, vec32, vec32, vec32, row16, vec32], tb=True, tm=512, name="ff1_dx")
    g_ff1 = _matmul(h2, du, ta=True, tm=1024, tn=D_FF // N_DEV, tk=4096, out_dtype=BF16, name="ff1_dw", shard_out=True)

    dy_ssd, do, stats, g_attn_norm = _matmul_rows(
        dmix, w_out, _mixer_split_epilogue, [o_att, lse], [attn_norm_w],
        [("row", F32, SSD_D_INNER), ("row", F32, ATT_D), ("row", F32, ATT_D), ("vec", F32, ATT_D)], tb=True, tm=512, name="out_proj_dx")
    g_out = _matmul(ycat, dmix, ta=True, tm=512, tn=1024, tk=4096, out_dtype=BF16, name="out_proj_dw")
    ff_slabs = [g_ff1, g_ff2.reshape(N_DEV, D_FF // N_DEV, D_MODEL)]
    (dq, dk, dv, dqw, dkw), (s_ff1, s_ff2) = _att_bwd(proj, do, stats, qw, kw, comm=_Exchange(ff_slabs, scatter=True))
    out_slabs = [g_out.astype(BF16).reshape(N_DEV, 2 * D_MODEL // N_DEV, D_MODEL)]
    (dz, dact, ddtr, da, g_dsk, g_dtb, g_ssd_norm), (s_out,) = _ssd_bwd(
        dy_ssd, ypre, proj, act, hall, dtb, alog, dsk, ssd_norm_w, comm=_Exchange(out_slabs, scatter=True))
    dxbc, g_conv_w, g_conv_b = _conv_bwd(dact, pre, proj, conv_w)
    dproj = [(dz, OFF_Z), (dxbc, OFF_XBC), (ddtr, OFF_DT), (dq, OFF_Q), (dk, OFF_K), (dv, OFF_V)]
    g_head, g_tail = _pieces_t_matmul([[dz, dxbc], [dq, dk, dv]], h1, tm=256, name="in_proj_dw")
    g_dt = _matmul(ddtr, h1, ta=True, tm=LANE, tn=1024, tk=4096, out_dtype=BF16, name="in_proj_dw_dt")[:SSD_HEADS]
    in_slabs = jnp.concatenate([g_head, g_dt, g_tail], axis=0).reshape(N_CHIP, 2, IN_W // N_DEV, D_MODEL)
    (sibling_slabs,) = _exchange(None, "swap_w_in_grads", ex=_SiblingSwap([in_slabs]))
    chip_slabs = _chip_sum(lax.dynamic_index_in_dim(in_slabs, core, axis=1, keepdims=False), sibling_slabs)
    (grad_x, dshift1, dscale1, g_norm1), (s_in,) = _matmul_rows(
        dproj, w_in_pt, _norm_bwd_epilogue, [x, dx1], [norm1_w, scale1], [row32, vec32, vec32, vec32],
        tm=256, name="in_proj_dx", comm=_ChipScatter([chip_slabs]))

    dmod = jnp.concatenate([dshift1, dscale1, dgate1, dshift2, dscale2, dgate2], axis=0)
    g_alog = da[:, :SSD_HEADS] * (-jnp.exp(a_log))
    g_qw = dqw.reshape(ATT_HEADS, HEAD_DIM).sum(axis=0, keepdims=True)
    g_kw = dkw.reshape(ATT_HEADS, HEAD_DIM).sum(axis=0, keepdims=True)
    return dict(loss=loss, grad_x=grad_x, dmod=dmod, norm1_w=g_norm1, norm2_w=g_norm2, w_in=s_in, conv_w=g_conv_w,
                conv_b=g_conv_b, dt_bias=g_dtb[:, :SSD_HEADS], a_log=g_alog, d_skip=g_dsk[:, :SSD_HEADS],
                ssd_norm_w=g_ssd_norm, q_norm_w=g_qw, k_norm_w=g_kw, attn_norm_w=g_attn_norm, w_out=s_out,
                w_ff1=s_ff1, w_ff2=s_ff2)


def _pack_w_in_rows(wt_full):
    cut = OFF_DT + SSD_HEADS
    pad = jnp.zeros((LANE - SSD_HEADS, wt_full.shape[1]), wt_full.dtype)
    return jnp.concatenate([wt_full[:cut], pad, wt_full[cut:]], axis=0)


MISC_FIELDS = (("dt_bias", SSD_HEADS), ("a_log", SSD_HEADS), ("d_skip", SSD_HEADS), ("q_norm_w", HEAD_DIM), ("k_norm_w", HEAD_DIM),
               ("loss", 1))
SMALL_LAYOUT = (("b_ada", 6), ("norm1_w", 1), ("norm2_w", 1), ("conv_w", 8), ("conv_b", 2), ("ssd_norm_w", 1),
                ("attn_norm_w", 1), ("misc", 1))


def _pack_small(vals):
    rows = []
    for name, nrow in SMALL_LAYOUT:
        if name == "misc":
            misc = jnp.concatenate([vals[f].reshape(1, n) if f in vals else jnp.zeros((1, n), F32) for f, n in MISC_FIELDS], axis=1)
            rows.append(_pad_lanes(misc, D_MODEL))
        elif name in vals:
            rows.append(vals[name].reshape(nrow, D_MODEL))
        else:
            rows.append(jnp.zeros((nrow, D_MODEL), F32))
    used = sum(n for _, n in SMALL_LAYOUT)
    rows.append(jnp.zeros((SMALL_ROWS - used, D_MODEL), F32))
    return jnp.concatenate(rows, axis=0)


def _unpack_small(packed):
    out, r = {}, 0
    for name, nrow in SMALL_LAYOUT:
        blk = packed[r:r + nrow]
        r += nrow
        if name == "misc":
            c0 = 0
            for f, n in MISC_FIELDS:
                out[f] = blk[:, c0:c0 + n]
                c0 += n
        elif name == "b_ada":
            out[name] = blk.reshape(1, N_MOD * D_MODEL)
        elif name == "conv_w":
            out[name] = blk.reshape(CONV_K, CONV_CH)
        elif name == "conv_b":
            out[name] = blk.reshape(1, CONV_CH)
        else:
            out[name] = blk
    return out


WEIGHT_NAMES = ("norm1_w", "norm2_w", "w_ada", "b_ada", "w_in", "conv_w", "conv_b", "dt_bias", "a_log", "d_skip",
                "ssd_norm_w", "q_norm_w", "k_norm_w", "attn_norm_w", "w_out", "w_ff1", "w_ff2")
SMALL_NAMES = ("norm1_w", "norm2_w", "b_ada", "conv_b", "dt_bias", "a_log", "d_skip", "ssd_norm_w", "q_norm_w",
               "k_norm_w", "attn_norm_w")


def kernel(x, c, norm1_w, norm2_w, w_ada, b_ada, w_in, conv_w, conv_b, dt_bias, a_log, d_skip, ssd_norm_w, q_norm_w, k_norm_w, attn_norm_w, w_out, w_ff1, w_ff2, loss_target, m_norm1_w, m_norm2_w, m_w_ada, m_b_ada, m_w_in, m_conv_w, m_conv_b, m_dt_bias, m_a_log, m_d_skip, m_ssd_norm_w, m_q_norm_w, m_k_norm_w, m_attn_norm_w, m_w_out, m_w_ff1, m_w_ff2, v_norm1_w, v_norm2_w, v_w_ada, v_b_ada, v_w_in, v_conv_w, v_conv_b, v_dt_bias, v_a_log, v_d_skip, v_ssd_norm_w, v_q_norm_w, v_k_norm_w, v_attn_norm_w, v_w_out, v_w_ff1, v_w_ff2):
    args = dict(locals())
    w = {n: args[n] for n in WEIGHT_NAMES}
    m = {n: args["m_" + n] for n in WEIGHT_NAMES}
    v = {n: args["v_" + n] for n in WEIGHT_NAMES}
    me = 4 * lax.axis_index("x") + 2 * lax.axis_index("y") + lax.axis_index("c")

    c_rows = jnp.pad(c, ((0, 7), (0, 0)))
    w_in_t, m_in_t, v_in_t = [jnp.transpose(t["w_in"][0]) for t in (w, m, v)]
    c_g, conv_g, w_in_g = _exchange([c_rows, w["conv_w"][0], w_in_t.astype(BF16)], "gather_w_in", scatter=False)
    c_all = c_g[:, 0, :]
    conv_full = _shards_to_cols(conv_g)
    w_in_pt = _pack_w_in_rows(w_in_g.reshape(IN_W, D_MODEL))

    mod_part = _ada_fwd(c_all, w["w_ada"][0])
    (mod_g,) = _exchange([mod_part], "gather_mod", scatter=False)
    mod_mine = lax.dynamic_index_in_dim(mod_g, me, axis=1, keepdims=False).reshape(1, N_MOD * D_MODEL) + w["b_ada"]
    mod = mod_mine.reshape(N_MOD, D_MODEL)

    res = _local_step(x[0], loss_target[0], mod, w_in_pt, conv_full, w["conv_b"], w["dt_bias"], w["a_log"], w["d_skip"],
                      w["ssd_norm_w"], w["q_norm_w"], w["k_norm_w"], w["attn_norm_w"], w["w_out"][0].astype(BF16),
                      w["w_ff1"][0].astype(BF16), w["w_ff2"][0].astype(BF16), w["norm1_w"], w["norm2_w"], lax.axis_index("c"))

    small_vals = {n: res[n] for n in SMALL_NAMES if n != "b_ada"}
    small_vals["b_ada"] = res["dmod"]
    small_vals["conv_w"] = res["conv_w"]
    small_vals["loss"] = res["loss"]
    (small_g,) = _exchange([_pack_small(small_vals)], "gather_small", scatter=False)

    grads, delta, new_m, new_v = {}, {}, {}, {}
    for name in ("w_out", "w_ff1", "w_ff2"):
        outs = _reduce_adamw(res[name], w[name][0], m[name][0], v[name][0], "adamw_" + name)
        grads[name], delta[name], new_m[name], new_v[name] = [o[None] for o in outs]
    outs = _reduce_adamw(res["w_in"], w_in_t, m_in_t, v_in_t, "adamw_w_in")
    grads["w_in"], delta["w_in"], new_m["w_in"], new_v["w_in"] = [jnp.transpose(o)[None] for o in outs]

    sm = _small_reduce_adamw(small_g, _pack_small({n: w[n] for n in SMALL_NAMES}), _pack_small({n: m[n] for n in SMALL_NAMES}),
                             _pack_small({n: v[n] for n in SMALL_NAMES}))
    sm = [_unpack_small(p) for p in sm]
    for n in SMALL_NAMES:
        grads[n], delta[n], new_m[n], new_v[n] = [p[n] for p in sm]
    shard_w = CONV_CH // N_DEV
    g_conv = lax.dynamic_slice_in_dim(sm[0]["conv_w"], me * shard_w, shard_w, axis=1)
    cw = _adamw_small(g_conv, w["conv_w"][0], m["conv_w"][0], v["conv_w"][0], "adamw_conv_w")
    grads["conv_w"] = g_conv[None]
    delta["conv_w"], new_m["conv_w"], new_v["conv_w"] = [o[None] for o in cw]

    ada_w = w_ada.shape[2]
    dmod_all = small_g[:, :N_MOD, :].reshape(N_DEV, N_MOD * D_MODEL)
    dmod_cols = lax.dynamic_slice_in_dim(dmod_all, me * ada_w, ada_w, axis=1)
    outs = _ada_bwd_adamw(c_all, dmod_cols, w["w_ada"][0], m["w_ada"][0], v["w_ada"][0])
    grads["w_ada"], delta["w_ada"], new_m["w_ada"], new_v["w_ada"] = [o[None] for o in outs]

    loss = sm[0]["loss"][0, 0]
    return (loss, res["grad_x"][None], *[grads[n] for n in WEIGHT_NAMES], *[delta[n] for n in WEIGHT_NAMES],
            *[new_m[n] for n in WEIGHT_NAMES], *[new_v[n] for n in WEIGHT_NAMES])
```

```python
import jax
import jax.numpy as jnp
from jax import lax
from jax.experimental import pallas as pl
from jax.experimental.pallas import tpu as pltpu

F32 = jnp.float32
BF16 = jnp.bfloat16
HIGHEST = lax.Precision.HIGHEST
MESH_IDS = pl.DeviceIdType.MESH

N_DEV = 8
D_MODEL = 1024
HEAD_DIM = 64
SSD_HEADS = 16
SSD_GROUPS = 4
HEADS_PER_GROUP = SSD_HEADS // SSD_GROUPS
SSD_STATE = 128
SSD_CHUNK = 128
SSD_D_INNER = SSD_HEADS * HEAD_DIM
GROUP_WIDTH = SSD_D_INNER // SSD_GROUPS
CONV_K = 4
CONV_CH = SSD_D_INNER + 2 * SSD_GROUPS * SSD_STATE
ATT_HEADS = 16
ATT_D = ATT_HEADS * HEAD_DIM
ATT_BLK = 128
DILATIONS = (1, 4, 16)
D_FF = 4 * D_MODEL
N_MOD = 6
EPS = 1e-6
IN_W = SSD_D_INNER + CONV_CH + SSD_HEADS + 3 * ATT_D
LANE = 128
OFF_Z, OFF_XBC, OFF_DT = 0, SSD_D_INNER, SSD_D_INNER + CONV_CH
OFF_Q = OFF_DT + LANE
OFF_K, OFF_V = OFF_Q + ATT_D, OFF_Q + 2 * ATT_D
IN_WP = OFF_V + ATT_D

ADAM_LR, ADAM_B1, ADAM_B2, ADAM_EPS, ADAM_WD, ADAM_STEP = 0.001, 0.9, 0.999, 1e-08, 0.01, 10
VMEM_LIMIT = 60 * 1024 * 1024
ROW_TILE = 512
SMALL_ROWS = 24


def _cparams(sem=None):
    return pltpu.CompilerParams(dimension_semantics=sem, vmem_limit_bytes=VMEM_LIMIT)


def _sigmoid(v):
    return 1.0 / (1.0 + jnp.exp(-v))


def _softplus(v):
    y = jnp.exp(-jnp.abs(v))
    small = y * (1.0 - y * (0.5 - y * (1.0 / 3.0)))
    return jnp.maximum(v, 0.0) + jnp.where(y < 0.01, small, jnp.log(1.0 + y))


def _dot(a, b, dims, precision=None):
    return lax.dot_general(a, b, (dims, ((), ())), preferred_element_type=F32, precision=precision)


NN = ((1,), (0,))
NT = ((1,), (1,))
TN = ((0,), (0,))


def _matmul(a, b, *, ta=False, tb=False, tm, tn, tk, out_dtype=F32, name, mode=None, u=None, comm=None, shard_out=False):
    m, k = (a.shape[1], a.shape[0]) if ta else a.shape
    n = b.shape[0] if tb else b.shape[1]
    assert m % tm == 0 and n % tn == 0 and k % tk == 0, (name, m, n, k)
    nk = k // tk
    a_spec = pl.BlockSpec((tk, tm), lambda i, j, kk: (kk, i)) if ta else pl.BlockSpec((tm, tk), lambda i, j, kk: (i, kk))
    b_spec = pl.BlockSpec((tn, tk), lambda i, j, kk: (j, kk)) if tb else pl.BlockSpec((tk, tn), lambda i, j, kk: (kk, j))
    o_spec = pl.BlockSpec((tm, tn), lambda i, j, kk: (i, j))
    dims = ((0,) if ta else (1,), (1,) if tb else (0,))
    n_out = 2 if mode == "relu2" else 1

    def body(*refs):
        if mode == "drelu2":
            a_ref, b_ref, u_ref = refs[:3]
            rest = refs[3:]
        else:
            a_ref, b_ref = refs[:2]
            u_ref = None
            rest = refs[2:]
        outs = rest[:n_out]
        part = _dot(a_ref[...], b_ref[...], dims)

        def finish(r):
            if mode == "relu2":
                outs[0][...] = r.astype(BF16)
                rr = jnp.maximum(r, 0.0)
                outs[1][...] = (rr * rr).astype(BF16)
            elif mode == "drelu2":
                outs[0][...] = (r * (2.0 * jnp.maximum(u_ref[...].astype(F32), 0.0))).astype(out_dtype)
            else:
                outs[0][...] = r.astype(out_dtype)

        if nk == 1:
            finish(part)
        else:
            acc = rest[n_out]
            kk = pl.program_id(2)

            @pl.when(kk == 0)
            def _():
                acc[...] = part

            @pl.when(kk > 0)
            def _():
                acc[...] += part

            @pl.when(kk == nk - 1)
            def _():
                finish(acc[...])

    in_specs = [a_spec, b_spec]
    args = [a, b]
    if mode == "drelu2":
        in_specs.append(o_spec)
        args.append(u)
    if mode == "relu2":
        out_shape = [jax.ShapeDtypeStruct((m, n), BF16), jax.ShapeDtypeStruct((m, n), BF16)]
    elif shard_out:
        out_shape = [jax.ShapeDtypeStruct((n // tn, m, tn), out_dtype)]
        o_spec = pl.BlockSpec((None, tm, tn), lambda i, j, kk: (j, i, 0))
    else:
        out_shape = [jax.ShapeDtypeStruct((m, n), out_dtype)]
    outs, comm_outs = _pcall(
        body, args, name=name, grid=(m // tm, n // tn, nk), in_specs=in_specs, out_specs=[o_spec] * n_out,
        out_shape=out_shape, scratch_shapes=[pltpu.VMEM((tm, tn), F32)] if nk > 1 else [],
        sem=("parallel", "parallel", "arbitrary"), comm=comm)
    res = tuple(outs) if mode == "relu2" else outs[0]
    return res if comm is None else (res, comm_outs)


def _pieces_t_matmul(groups, b, *, tm, name):
    k, n = b.shape
    pieces = [p for g in groups for p in g]
    starts, tiles = [], 0
    for p in pieces:
        assert p.shape[0] == k and p.shape[1] % tm == 0, (name, p.shape)
        starts.append(tiles)
        tiles += p.shape[1] // tm
    group_of, group_start, group_tiles = [], [], []
    for gi, g in enumerate(groups):
        group_start.append(starts[len(group_of)])
        group_of += [gi] * len(g)
        group_tiles.append(sum(p.shape[1] // tm for p in g))

    def clipped(block, start, count):
        return pl.BlockSpec(block, (lambda i: (0, jnp.clip(i - start, 0, count - 1))) if block[0] == k
                            else (lambda i: (jnp.clip(i - start, 0, count - 1), 0)))

    def body(*refs):
        a_refs, b_ref, o_refs = refs[:len(pieces)], refs[len(pieces)], refs[len(pieces) + 1:]
        i = pl.program_id(0)
        for a_ref, start, p, gi in zip(a_refs, starts, pieces, group_of):
            @pl.when((i >= start) & (i < start + p.shape[1] // tm))
            def _(a_ref=a_ref, o_ref=o_refs[gi]):
                o_ref[...] = _dot(a_ref[...], b_ref[...], TN).astype(BF16)

    return pl.pallas_call(
        body, name=name, grid=(tiles,),
        in_specs=[clipped((k, tm), s0, p.shape[1] // tm) for s0, p in zip(starts, pieces)] + [pl.BlockSpec((k, n), lambda i: (0, 0))],
        out_specs=[clipped((tm, n), s0, cnt) for s0, cnt in zip(group_start, group_tiles)],
        out_shape=[jax.ShapeDtypeStruct((cnt * tm, n), BF16) for cnt in group_tiles],
        compiler_params=_cparams(("arbitrary",)))(*pieces, b)


def _rms_mod(xv, nw, scale, shift):
    r = lax.rsqrt(jnp.mean(xv * xv, axis=-1, keepdims=True) + EPS)
    return ((xv * r) * nw * (1.0 + scale) + shift).astype(BF16)


def _norm_mod_fwd(x, nw, scale, shift, name):
    s, d = x.shape
    row = pl.BlockSpec((ROW_TILE, d), lambda i: (i, 0))
    vec = pl.BlockSpec((1, d), lambda i: (0, 0))

    def body(x_ref, nw_ref, sc_ref, sh_ref, h_ref):
        h_ref[...] = _rms_mod(x_ref[...], nw_ref[...], sc_ref[...], sh_ref[...])

    return pl.pallas_call(body, name=name, grid=(s // ROW_TILE,), in_specs=[row, vec, vec, vec], out_specs=row,
                          out_shape=jax.ShapeDtypeStruct((s, d), BF16), compiler_params=_cparams(("parallel",)))(x, nw, scale, shift)


def _matmul_rows(a, b, epilogue, row_in, vec_in, outs, *, tb=False, tm, name, comm=None):
    pieces = a if isinstance(a, list) else [(a, 0)]
    assert not (tb and len(pieces) > 1)
    m = pieces[0][0].shape[0]
    n = b.shape[0] if tb else b.shape[1]
    assert m % tm == 0, (name, m, tm)
    dims = ((1,), (1,) if tb else (0,))
    n_a, n_row, n_vec = len(pieces), len(row_in), len(vec_in)

    def body(*refs):
        a_refs, b_ref, rest = refs[:n_a], refs[n_a], refs[n_a + 1:]
        if n_a == 1:
            c = _dot(a_refs[0][...], b_ref[...], dims)
        else:
            c = None
            for a_ref, (piece, off) in zip(a_refs, pieces):
                part = _dot(a_ref[...], b_ref[off:off + piece.shape[1], :], dims)
                c = part if c is None else c + part
        epilogue(c, pl.program_id(0) == 0, rest[:n_row], rest[n_row:n_row + n_vec], rest[n_row + n_vec:])

    def spec(kind, width):
        block = {"row": (tm, width), "vec": (1, width), "one": (1, 1)}[kind]
        return pl.BlockSpec(block, (lambda i: (i, 0)) if kind == "row" else (lambda i: (0, 0)))

    def shape(kind, width):
        return {"row": (m, width), "vec": (1, width), "one": (1, 1)}[kind]

    outs = [(o[0], o[1], o[2] if len(o) > 2 else n) for o in outs]
    res, comm_outs = _pcall(
        body, [*[p for p, _ in pieces], b, *row_in, *vec_in], name=name, grid=(m // tm,),
        in_specs=[spec("row", p.shape[1]) for p, _ in pieces] + [pl.BlockSpec(b.shape, lambda i: (0, 0))]
        + [spec("row", r.shape[1]) for r in row_in] + [spec("vec", v.shape[1]) for v in vec_in],
        out_specs=[spec(kind, width) for kind, _, width in outs],
        out_shape=[jax.ShapeDtypeStruct(shape(kind, width), dt) for kind, dt, width in outs],
        sem=("arbitrary",), comm=comm)
    return res if comm is None else (res, comm_outs)


def _residual_norm_epilogue(mix, first, rows, vecs, outs):
    (x_ref,), (gate_ref, nw_ref, sc_ref, sh_ref), (mix_ref, x1_ref, h_ref) = rows, vecs, outs
    xv = x_ref[...] + gate_ref[...] * mix
    mix_ref[...] = mix
    x1_ref[...] = xv
    h_ref[...] = _rms_mod(xv, nw_ref[...], sc_ref[...], sh_ref[...])


def _loss_epilogue(ff, first, rows, vecs, outs):
    (x1_ref, t_ref), (g_ref,), (loss_ref, dout_ref, dff_ref, dg_ref) = rows, vecs, outs
    d = ff.shape[1]

    @pl.when(first)
    def _():
        loss_ref[...] = jnp.zeros_like(loss_ref)
        dg_ref[...] = jnp.zeros_like(dg_ref)

    err = x1_ref[...] + g_ref[...] * ff - t_ref[...]
    loss_ref[...] += (0.5 / d) * jnp.sum(err * err).reshape(1, 1)
    dout = err * (1.0 / d)
    dout_ref[...] = dout
    dff_ref[...] = (g_ref[...] * dout).astype(BF16)
    dg_ref[...] += jnp.sum(dout * ff, axis=0, keepdims=True)


def _norm_bwd_epilogue(dh, first, rows, vecs, outs):
    with_gate = len(vecs) == 3
    x_ref, dres_ref = rows[:2]
    nw_ref, sc_ref = vecs[:2]
    dx_ref, dsh_ref, dsc_ref, dnw_ref = outs[:4]

    @pl.when(first)
    def _():
        for ref in outs[1:4] + outs[5:]:
            ref[...] = jnp.zeros_like(ref)

    xv = x_ref[...]
    r = lax.rsqrt(jnp.mean(xv * xv, axis=-1, keepdims=True) + EPS)
    nrm = xv * r
    one_sc = 1.0 + sc_ref[...]
    dhn = dh * nrm
    dsh_ref[...] += jnp.sum(dh, axis=0, keepdims=True)
    dsc_ref[...] += jnp.sum(dhn, axis=0, keepdims=True) * nw_ref[...]
    dnw_ref[...] += jnp.sum(dhn, axis=0, keepdims=True) * one_sc
    dn = dh * (nw_ref[...] * one_sc)
    dx = dres_ref[...] + r * (dn - nrm * jnp.mean(dn * nrm, axis=-1, keepdims=True))
    dx_ref[...] = dx
    if with_gate:
        outs[4][...] = (vecs[2][...] * dx).astype(BF16)
        outs[5][...] += jnp.sum(dx * rows[2][...], axis=0, keepdims=True)


CONV_COLS = 256
CONV_FWD_ROWS = 2048
CONV_BWD_ROWS = 1024
CONV_SUB_ROWS = 128
HALO = 8


def _shift_down(cur, halo, k):
    if k == 0:
        return cur
    rolled = pltpu.roll(cur, k, axis=0)
    top = jnp.where(lax.broadcasted_iota(jnp.int32, halo.shape, 0) < k, pltpu.roll(halo, k, axis=0), rolled[:HALO])
    return jnp.concatenate([top, rolled[HALO:]], axis=0)


def _shift_up(cur, halo, k):
    if k == 0:
        return cur
    t = cur.shape[0]
    rolled = pltpu.roll(cur, t - k, axis=0)
    bot = jnp.where(lax.broadcasted_iota(jnp.int32, halo.shape, 0) >= HALO - k, pltpu.roll(halo, HALO - k, axis=0),
                    rolled[t - HALO:])
    return jnp.concatenate([rolled[:t - HALO], bot], axis=0)


def _conv_fwd(proj, conv_w, conv_b):
    s = proj.shape[0]
    nr = s // CONV_FWD_ROWS
    cb0 = OFF_XBC // CONV_COLS
    hb = CONV_FWD_ROWS // HALO
    cur = pl.BlockSpec((CONV_FWD_ROWS, CONV_COLS), lambda j, r: (r, cb0 + j))
    prev = pl.BlockSpec((HALO, CONV_COLS), lambda j, r: (jnp.maximum(r * hb - 1, 0), cb0 + j))
    out = pl.BlockSpec((CONV_FWD_ROWS, CONV_COLS), lambda j, r: (r, j))

    def body(u_ref, up_ref, w_ref, b_ref, pre_ref, act_ref):
        r = pl.program_id(1)
        for c in range(CONV_FWD_ROWS // CONV_SUB_ROWS):
            rows = slice(c * CONV_SUB_ROWS, (c + 1) * CONV_SUB_ROWS)
            u = u_ref[rows, :]
            halo = u_ref[c * CONV_SUB_ROWS - HALO:c * CONV_SUB_ROWS, :] if c > 0 else jnp.where(r > 0, up_ref[...], 0.0)
            acc = b_ref[...] + w_ref[CONV_K - 1:CONV_K, :] * u
            for k in range(1, CONV_K):
                acc = acc + w_ref[CONV_K - 1 - k:CONV_K - k, :] * _shift_down(u, halo, k)
            pre_ref[rows, :] = acc
            act_ref[rows, :] = acc * _sigmoid(acc)

    return pl.pallas_call(
        body, name="conv_fwd", grid=(CONV_CH // CONV_COLS, nr),
        in_specs=[cur, prev, pl.BlockSpec((CONV_K, CONV_COLS), lambda j, r: (0, j)),
                  pl.BlockSpec((1, CONV_COLS), lambda j, r: (0, j))],
        out_specs=[out, out],
        out_shape=[jax.ShapeDtypeStruct((s, CONV_CH), F32), jax.ShapeDtypeStruct((s, CONV_CH), F32)],
        compiler_params=_cparams(("parallel", "arbitrary")))(proj, proj, conv_w, conv_b)


def _conv_bwd(dact, pre, proj, conv_w):
    s = proj.shape[0]
    nr = s // CONV_BWD_ROWS
    cb0 = OFF_XBC // CONV_COLS
    hb = CONV_BWD_ROWS // HALO
    last_halo = s // HALO - 1
    n_sub = CONV_BWD_ROWS // CONV_SUB_ROWS
    cur = pl.BlockSpec((CONV_BWD_ROWS, CONV_COLS), lambda j, r: (r, j))
    nxt = pl.BlockSpec((HALO, CONV_COLS), lambda j, r: (jnp.minimum((r + 1) * hb, last_halo), j))
    ucur = pl.BlockSpec((CONV_BWD_ROWS, CONV_COLS), lambda j, r: (r, cb0 + j))
    wspec = pl.BlockSpec((CONV_K, CONV_COLS), lambda j, r: (0, j))
    bspec = pl.BlockSpec((1, CONV_COLS), lambda j, r: (0, j))

    def dsilu(p):
        sg = _sigmoid(p)
        return sg * (1.0 + p * (1.0 - sg))

    def body(da_ref, dan_ref, pre_ref, pren_ref, u_ref, w_ref, du_ref, dw_ref, db_ref):
        r = pl.program_id(1)

        @pl.when(r == 0)
        def _():
            dw_ref[...] = jnp.zeros_like(dw_ref)
            db_ref[...] = jnp.zeros_like(db_ref)

        dws = [jnp.zeros((1, CONV_COLS), F32) for _ in range(CONV_K)]
        db = jnp.zeros((1, CONV_COLS), F32)
        for c in range(n_sub):
            rows = slice(c * CONV_SUB_ROWS, (c + 1) * CONV_SUB_ROWS)
            ahead = slice((c + 1) * CONV_SUB_ROWS, (c + 1) * CONV_SUB_ROWS + HALO)
            dpre = da_ref[rows, :] * dsilu(pre_ref[rows, :])
            if c < n_sub - 1:
                dnext = da_ref[ahead, :] * dsilu(pre_ref[ahead, :])
            else:
                dnext = jnp.where(r < nr - 1, dan_ref[...] * dsilu(pren_ref[...]), 0.0)
            u = u_ref[rows, :]
            du = w_ref[CONV_K - 1:CONV_K, :] * dpre
            dws[0] = dws[0] + jnp.sum(dpre * u, axis=0, keepdims=True)
            for k in range(1, CONV_K):
                ahead_k = _shift_up(dpre, dnext, k)
                du = du + w_ref[CONV_K - 1 - k:CONV_K - k, :] * ahead_k
                dws[k] = dws[k] + jnp.sum(ahead_k * u, axis=0, keepdims=True)
            du_ref[rows, :] = du.astype(BF16)
            db = db + jnp.sum(dpre, axis=0, keepdims=True)
        dw_ref[...] += jnp.concatenate(dws[::-1], axis=0)
        db_ref[...] += db

    return pl.pallas_call(
        body, name="conv_bwd", grid=(CONV_CH // CONV_COLS, nr),
        in_specs=[cur, nxt, cur, nxt, ucur, wspec],
        out_specs=[cur, wspec, bspec],
        out_shape=[jax.ShapeDtypeStruct((s, CONV_CH), BF16), jax.ShapeDtypeStruct((CONV_K, CONV_CH), F32),
                   jax.ShapeDtypeStruct((1, CONV_CH), F32)],
        compiler_params=_cparams(("parallel", "arbitrary")))(dact, dact, pre, pre, proj, conv_w)


def _ssd_common(dtr, dtb, alog):
    lane = lax.broadcasted_iota(jnp.int32, (1, LANE), 1)
    head_lane = lane < SSD_HEADS
    dt = jnp.where(head_lane, _softplus(dtr + dtb), 0.0)
    a = jnp.where(head_lane, -jnp.exp(alog), 0.0)
    row = lax.broadcasted_iota(jnp.int32, (SSD_CHUNK, SSD_CHUNK), 0)
    col = lax.broadcasted_iota(jnp.int32, (SSD_CHUNK, SSD_CHUNK), 1)
    tril = row >= col
    cs = _dot(tril.astype(F32), dt * a, NN, precision=HIGHEST)
    return dt, a, cs, cs.T, tril, lane


def _split_bf16(v, passes):
    terms, rest = [], v
    for _ in range(passes):
        t = rest.astype(BF16)
        terms.append(t)
        rest = rest - t.astype(F32)
    return terms


def _dot_split(v, m, dims, passes):
    terms = _split_bf16(v, passes)
    if passes == 1:
        return _dot(terms[0], m, dims)
    return _dot(jnp.concatenate(terms, axis=1), jnp.concatenate([m] * passes, axis=0 if dims == NN else 1), dims)


def _ssd_constants():
    heads = jnp.arange(LANE)[:, None]
    exp_mat = (heads == (jnp.arange(SSD_D_INNER)[None, :] // HEAD_DIM)).astype(BF16)
    ind4 = ((jnp.arange(SSD_HEADS * SSD_CHUNK)[:, None] // SSD_CHUNK) == jnp.arange(LANE)[None, :]).astype(BF16)
    return exp_mat, ind4


def _expand_heads(v):
    return jnp.repeat(v[:, :SSD_HEADS], HEAD_DIM, axis=1)


def _ssd_prep(dtr, dtb, alog, exp_mat):
    dt, a, cs, cst, tril, lane = _ssd_common(dtr, dtb, alog)
    return dt, a, cs, cst, tril, lane, _dot_split(dt, exp_mat, NN, 2), _dot_split(cs, exp_mat, NN, 3)


def _chunk_decay_rows(cs, g):
    parts = []
    for e in range(HEADS_PER_GROUP):
        h = g * HEADS_PER_GROUP + e
        parts.append(jnp.broadcast_to(jnp.exp(cs[SSD_CHUNK - 1:SSD_CHUNK, h:h + 1]), (HEAD_DIM, SSD_STATE)))
    return jnp.concatenate(parts, axis=0)


def _ssd_fwd(proj, act, dtb, alog, dsk, nw):
    s = proj.shape[0]
    nc = s // SSD_CHUNK
    bc_w = SSD_GROUPS * SSD_STATE
    exp_mat, _ = _ssd_constants()

    def body(z_ref, dtr_ref, xs_ref, b_ref, c_ref, dtb_ref, alog_ref, dskx_ref, nw_ref, exp_ref,
             ypre_ref, yssd_ref, hall_ref, h_scr):
        @pl.when(pl.program_id(0) == 0)
        def _():
            h_scr[...] = jnp.zeros_like(h_scr)

        dt, a, cs, cst, tril, lane, dtx, csx = _ssd_prep(dtr_ref[...], dtb_ref[...], alog_ref[...], exp_ref[...])
        cs_last_x = csx[SSD_CHUNK - 1:SSD_CHUNK, :]
        xs = xs_ref[...]
        xdt = xs * dtx
        xdtb = xdt.astype(BF16)
        xdec = (xdt * jnp.exp(cs_last_x - csx)).astype(BF16)
        ecsx = jnp.exp(csx)
        head_of_lane = lax.broadcasted_iota(jnp.int32, (1, GROUP_WIDTH), 1) // HEAD_DIM
        for g in range(SSD_GROUPS):
            gs = slice(g * GROUP_WIDTH, (g + 1) * GROUP_WIDTH)
            bg = b_ref[:, g * SSD_STATE:(g + 1) * SSD_STATE].astype(BF16)
            cg = c_ref[:, g * SSD_STATE:(g + 1) * SSD_STATE].astype(BF16)
            cb = _dot(cg, bg, NT)
            hprev = h_scr[gs, :]
            hall_ref[0, gs, :] = hprev
            gms, rhs = [], []
            xg = xdtb[:, gs]
            for e in range(HEADS_PER_GROUP):
                h = g * HEADS_PER_GROUP + e
                lm = jnp.exp(jnp.where(tril, cs[:, h:h + 1] - cst[h:h + 1, :], -1e30))
                gms.append((cb * lm).astype(BF16))
                rhs.append(jnp.where(head_of_lane == e, xg, jnp.zeros_like(xg)))
            y = _dot(jnp.concatenate(gms, axis=1), jnp.concatenate(rhs, axis=0), NN)
            y = y + ecsx[:, gs] * _dot(cg, hprev.astype(BF16), NT)
            y = y + dskx_ref[:, gs] * xs[:, gs]
            h_scr[gs, :] = hprev * _chunk_decay_rows(cs, g) + _dot(xdec[:, gs], bg, TN)
            ypre_ref[:, gs] = y
            z = z_ref[:, gs]
            yg = y * (z * _sigmoid(z))
            r = lax.rsqrt(jnp.mean(yg * yg, axis=-1, keepdims=True) + EPS)
            yssd_ref[:, gs] = (yg * r * nw_ref[:, gs]).astype(BF16)

    row_d = lambda cb: pl.BlockSpec((SSD_CHUNK, SSD_D_INNER), lambda c: (c, cb))
    small = pl.BlockSpec((1, LANE), lambda c: (0, 0))
    wide = pl.BlockSpec((1, SSD_D_INNER), lambda c: (0, 0))
    return pl.pallas_call(
        body, name="ssd_fwd", grid=(nc,),
        in_specs=[row_d(OFF_Z // SSD_D_INNER),
                  pl.BlockSpec((SSD_CHUNK, LANE), lambda c: (c, OFF_DT // LANE)),
                  row_d(0),
                  pl.BlockSpec((SSD_CHUNK, bc_w), lambda c: (c, SSD_D_INNER // bc_w)),
                  pl.BlockSpec((SSD_CHUNK, bc_w), lambda c: (c, SSD_D_INNER // bc_w + 1)),
                  small, small, wide, wide, pl.BlockSpec((LANE, SSD_D_INNER), lambda c: (0, 0))],
        out_specs=[row_d(0), row_d(0), pl.BlockSpec((1, SSD_D_INNER, SSD_STATE), lambda c: (c, 0, 0))],
        out_shape=[jax.ShapeDtypeStruct((s, SSD_D_INNER), F32), jax.ShapeDtypeStruct((s, SSD_D_INNER + ATT_D), BF16),
                   jax.ShapeDtypeStruct((nc, SSD_D_INNER, SSD_STATE), F32)],
        scratch_shapes=[pltpu.VMEM((SSD_D_INNER, SSD_STATE), F32)],
        compiler_params=_cparams(("arbitrary",)))(proj, proj, act, act, act, dtb, alog, _expand_heads(dsk), nw, exp_mat)


def _ssd_bwd(dycat, ypre, proj, act, hall, dtb, alog, dsk, nw, comm=None):
    s = proj.shape[0]
    nc = s // SSD_CHUNK
    bc_w = SSD_GROUPS * SSD_STATE

    exp_mat, ind4 = _ssd_constants()
    seg_passes = 1

    def body(dy_ref, ypre_ref, z_ref, dtr_ref, xs_ref, b_ref, c_ref, hall_ref, dtb_ref, alog_ref, dskx_ref, nw_ref,
             exp_ref, ind4_ref, dz_ref, dact_ref, ddtr_ref, da_ref, ddsk_ref, ddtb_ref, dnw_ref, dh_scr):
        @pl.when(pl.program_id(0) == 0)
        def _():
            dh_scr[...] = jnp.zeros_like(dh_scr)
            da_ref[...] = jnp.zeros_like(da_ref)
            ddsk_ref[...] = jnp.zeros_like(ddsk_ref)
            ddtb_ref[...] = jnp.zeros_like(ddtb_ref)
            dnw_ref[...] = jnp.zeros_like(dnw_ref)

        dtr = dtr_ref[...]
        dt, a, cs, cst, tril, lane, dtx, csx = _ssd_prep(dtr, dtb_ref[...], alog_ref[...], exp_ref[...])
        cs_last_x = csx[SSD_CHUNK - 1:SSD_CHUNK, :]
        xs = xs_ref[...]
        xdt = xs * dtx
        xdtb = xdt.astype(BF16)
        decx = jnp.exp(cs_last_x - csx)
        xdecf = xdt * decx
        xdec = xdecf.astype(BF16)
        ecsx = jnp.exp(csx)
        head_of_lane = lax.broadcasted_iota(jnp.int32, (1, GROUP_WIDTH), 1) // HEAD_DIM
        last_row = lax.broadcasted_iota(jnp.int32, (SSD_CHUNK, 1), 0) == SSD_CHUNK - 1
        dcs_col = jnp.zeros((SSD_CHUNK, LANE), F32)
        dcs_row = jnp.zeros((SSD_CHUNK, LANE), F32)
        ddt = jnp.zeros((SSD_CHUNK, LANE), F32)
        ddsk = jnp.zeros((1, LANE), F32)
        hsum = jnp.zeros((1, LANE), F32)
        t1_sum = jnp.zeros((1, LANE), F32)
        for g in range(SSD_GROUPS):
            gs = slice(g * GROUP_WIDTH, (g + 1) * GROUP_WIDTH)
            bsl = slice(g * SSD_STATE, (g + 1) * SSD_STATE)
            exp_g = exp_ref[:, gs]
            ind4_g = ind4_ref[g * HEADS_PER_GROUP * SSD_CHUNK:(g + 1) * HEADS_PER_GROUP * SSD_CHUNK, :]
            z = z_ref[:, gs]
            sg = _sigmoid(z)
            sz = z * sg
            ypre = ypre_ref[:, gs]
            yg = ypre * sz
            r = lax.rsqrt(jnp.mean(yg * yg, axis=-1, keepdims=True) + EPS)
            nrm = yg * r
            dyo_n = dy_ref[:, gs]
            dnw_ref[:, gs] += jnp.sum(dyo_n * nrm, axis=0, keepdims=True)
            dn = dyo_n * nw_ref[:, gs]
            dyg = r * (dn - nrm * jnp.mean(dn * nrm, axis=-1, keepdims=True))
            dz_ref[:, gs] = (dyg * ypre * (sg * (1.0 + z * (1.0 - sg)))).astype(BF16)
            dy = dyg * sz

            bg = b_ref[:, bsl].astype(BF16)
            cg = c_ref[:, bsl].astype(BF16)
            cb = _dot(cg, bg, NT)
            hprev = hall_ref[0, gs, :]
            hb = hprev.astype(BF16)
            dhn = dh_scr[gs, :]
            dhb = dhn.astype(BF16)
            xs_g, xdt_g = xs[:, gs], xdtb[:, gs]
            w_off = _dot(cg, hb, NT)
            dyo = dy * ecsx[:, gs]
            dyob = dyo.astype(BF16)
            dcg = _dot(dyob, hb, NN)
            dh_y = _dot(dyob, cg, TN)
            r_st = _dot(bg, dhb, NT)
            dbg = _dot(xdec[:, gs], dhb, NN)
            dyb = dy.astype(BF16)
            gms, gmbs, lms, dys = [], [], [], []
            for e in range(HEADS_PER_GROUP):
                h = g * HEADS_PER_GROUP + e
                lm = jnp.exp(jnp.where(tril, cs[:, h:h + 1] - cst[h:h + 1, :], -1e30))
                gm = cb * lm
                lms.append(lm)
                gms.append(gm)
                gmbs.append(gm.astype(BF16))
                dys.append(jnp.where(head_of_lane == e, dyb, jnp.zeros_like(dyb)))
            dxdt = _dot(jnp.concatenate(gmbs, axis=0), jnp.concatenate(dys, axis=0), TN) + decx[:, gs] * r_st
            dcb = jnp.zeros((SSD_CHUNK, SSD_CHUNK), F32)
            mms = []
            for e in range(HEADS_PER_GROUP):
                dg = _dot(dys[e], xdt_g, NT)
                mms.append(dg * gms[e])
                dcb = dcb + dg * lms[e]
            seg = _dot_split(jnp.concatenate([dyo * w_off, xdecf[:, gs] * r_st, dxdt * xs_g, dy * xs_g], axis=0), exp_g, NT, seg_passes)
            v1, t1, ddt_g, dsk_g = [seg[i * SSD_CHUNK:(i + 1) * SSD_CHUNK] for i in range(4)]
            dcs_col = dcs_col + v1 - t1 + _dot_split(jnp.concatenate(mms, axis=1), ind4_g, NN, seg_passes)
            for t in _split_bf16(jnp.concatenate(mms, axis=0), seg_passes):
                dcs_row = dcs_row + _dot(ind4_g, t, TN)
            ddt = ddt + ddt_g
            ddsk = ddsk + jnp.sum(dsk_g, axis=0, keepdims=True)
            t1_sum = t1_sum + jnp.sum(t1, axis=0, keepdims=True)
            for e in range(HEADS_PER_GROUP):
                h = g * HEADS_PER_GROUP + e
                hs = slice(e * HEAD_DIM, (e + 1) * HEAD_DIM)
                hsum = hsum + jnp.where(lane == h, jnp.sum(dhn[hs, :] * hprev[hs, :]).reshape(1, 1), 0.0)
            dh_scr[gs, :] = dhn * _chunk_decay_rows(cs, g) + dh_y
            dcbb = dcb.astype(BF16)
            dact_ref[:, gs] = dxdt * dtx[:, gs] + dskx_ref[:, gs] * dy
            dact_ref[:, SSD_D_INNER + g * SSD_STATE:SSD_D_INNER + (g + 1) * SSD_STATE] = dbg + _dot(dcbb, cg, TN)
            dact_ref[:, SSD_D_INNER + bc_w + g * SSD_STATE:SSD_D_INNER + bc_w + (g + 1) * SSD_STATE] = dcg + _dot(dcbb, bg, NN)
        dlast = t1_sum + jnp.exp(cs[SSD_CHUNK - 1:SSD_CHUNK, :]) * hsum
        dcs = dcs_col - dcs_row.T + jnp.where(last_row, dlast, 0.0)
        row = lax.broadcasted_iota(jnp.int32, (SSD_CHUNK, SSD_CHUNK), 0)
        col = lax.broadcasted_iota(jnp.int32, (SSD_CHUNK, SSD_CHUNK), 1)
        dda = _dot((col >= row).astype(F32), dcs, NN, precision=HIGHEST)
        ddt = ddt + dda * a
        da_ref[...] += jnp.sum(dda * dt, axis=0, keepdims=True)
        ddtr = jnp.where(lane < SSD_HEADS, ddt * _sigmoid(dtr + dtb_ref[...]), 0.0)
        ddtr_ref[...] = ddtr.astype(BF16)
        ddtb_ref[...] += jnp.sum(ddtr, axis=0, keepdims=True)
        ddsk_ref[...] += ddsk

    rev = lambda c: nc - 1 - c
    row_d = lambda cb: pl.BlockSpec((SSD_CHUNK, SSD_D_INNER), lambda c: (rev(c), cb))
    small = pl.BlockSpec((1, LANE), lambda c: (0, 0))
    wide = pl.BlockSpec((1, SSD_D_INNER), lambda c: (0, 0))
    small_shape = jax.ShapeDtypeStruct((1, LANE), F32)
    return _pcall(
        body, (dycat, ypre, proj, proj, act, act, act, hall, dtb, alog, _expand_heads(dsk), nw, exp_mat, ind4),
        name="ssd_bwd", grid=(nc,),
        in_specs=[row_d(0), row_d(0), row_d(OFF_Z // SSD_D_INNER),
                  pl.BlockSpec((SSD_CHUNK, LANE), lambda c: (rev(c), OFF_DT // LANE)),
                  row_d(0),
                  pl.BlockSpec((SSD_CHUNK, bc_w), lambda c: (rev(c), SSD_D_INNER // bc_w)),
                  pl.BlockSpec((SSD_CHUNK, bc_w), lambda c: (rev(c), SSD_D_INNER // bc_w + 1)),
                  pl.BlockSpec((1, SSD_D_INNER, SSD_STATE), lambda c: (rev(c), 0, 0)),
                  small, small, wide, wide, pl.BlockSpec((LANE, SSD_D_INNER), lambda c: (0, 0)),
                  pl.BlockSpec((SSD_HEADS * SSD_CHUNK, LANE), lambda c: (0, 0))],
        out_specs=[row_d(0), pl.BlockSpec((SSD_CHUNK, CONV_CH), lambda c: (rev(c), 0)),
                   pl.BlockSpec((SSD_CHUNK, LANE), lambda c: (rev(c), 0)), small, small, small, wide],
        out_shape=[jax.ShapeDtypeStruct((s, SSD_D_INNER), BF16), jax.ShapeDtypeStruct((s, CONV_CH), F32),
                   jax.ShapeDtypeStruct((s, LANE), BF16), small_shape, small_shape, small_shape,
                   jax.ShapeDtypeStruct((1, SSD_D_INNER), F32)],
        scratch_shapes=[pltpu.VMEM((SSD_D_INNER, SSD_STATE), F32)], sem=("arbitrary",), comm=comm)


def _head_mean_matrix():
    row = lax.broadcasted_iota(jnp.int32, (LANE, LANE), 0) // HEAD_DIM
    col = lax.broadcasted_iota(jnp.int32, (LANE, LANE), 1) // HEAD_DIM
    return (row == col).astype(F32)


def _head_sum2(v, ones_bd):
    hi = v.astype(BF16)
    lo = (v - hi.astype(F32)).astype(BF16)
    return _dot(jnp.concatenate([hi, lo], axis=1), jnp.concatenate([ones_bd, ones_bd], axis=0), NN)


def _head_norms(xs, ws, ones_bd):
    sums = [_head_sum2(x * x, ones_bd) for x in xs]
    rs = [lax.rsqrt(ms * (1.0 / HEAD_DIM) + EPS) for ms in sums]
    return [(x * r) * w for x, r, w in zip(xs, rs, ws)], rs


def _head_norms_bwd(dns, xs, ws, rs, ones_bd):
    nrms = [x * r for x, r in zip(xs, rs)]
    dnws = [dn * w for dn, w in zip(dns, ws)]
    projs = [_head_sum2(dnw * nrm, ones_bd) for dnw, nrm in zip(dnws, nrms)]
    dxs = [r * (dnw - nrm * (pr * (1.0 / HEAD_DIM))) for r, dnw, nrm, pr in zip(rs, dnws, nrms, projs)]
    return dxs, [jnp.sum(dn * nrm, axis=0, keepdims=True) for dn, nrm in zip(dns, nrms)]


NORM_CHUNKS = 2


PRO_ROWS = 256
ATT_GROUP_FWD = 32
ATT_GROUP_BWD = 8
KEYS = 2 * ATT_BLK
NEG = -1e30
HALF = HEAD_DIM // 2


def _rows(start, size, dil):
    return pl.ds(start, size) if dil == 1 else pl.ds(start, size, stride=dil)


def _fill_bias(bias_ref):
    row = lax.broadcasted_iota(jnp.int32, (ATT_BLK, 2 * KEYS), 0)
    col = lax.broadcasted_iota(jnp.int32, (ATT_BLK, 2 * KEYS), 1) & (KEYS - 1)
    for first, off in ((0, 0), (1, ATT_BLK)):
        dist = off + row - col
        bias_ref[first] = jnp.where((dist >= 0) & (dist <= ATT_BLK), 0.0, NEG)


def _pair(a, b):
    return jnp.concatenate([jnp.broadcast_to(a, (ATT_BLK, KEYS)), jnp.broadcast_to(b, (ATT_BLK, KEYS))], axis=1)


def _split_heads(x, is_a):
    zero = jnp.zeros_like(x)
    return jnp.concatenate([jnp.where(is_a, x, zero), jnp.where(is_a, zero, x)], axis=0)


def _block_ids(b, nb):
    i = b & (nb - 1)
    q0 = pl.multiple_of(b * ATT_BLK, ATT_BLK)
    k0 = pl.multiple_of((b - jnp.minimum(i, 1)) * ATT_BLK, ATT_BLK)
    return pl.ds(q0, ATT_BLK), pl.ds(k0, KEYS), jnp.minimum(i, 1)


def _natural_rows(b, nb, dil):
    if dil == 1:
        return pl.ds(pl.multiple_of(b * ATT_BLK, ATT_BLK), ATT_BLK)
    return pl.ds(b // nb + dil * ((b & (nb - 1)) * ATT_BLK), ATT_BLK, stride=dil)


def _att_fwd(proj, qw, kw, comm=None):
    s = proj.shape[0]
    nblk = s // ATT_BLK
    assert all((s // d) // ATT_BLK >= 2 for d in DILATIONS)
    blk = lambda off: pl.BlockSpec((s, LANE), lambda i: (0, off // LANE + i))
    wspec = pl.BlockSpec((1, LANE), lambda i: (0, i))
    oblk = pl.BlockSpec((s, LANE), lambda i: (0, i))

    def body(q_ref, k_ref, v_ref, qw_ref, kw_ref, o_ref, lse_ref, qn, kn, q_cm, k_cm, v_cm, m_acc, l_acc, o_d, m_d, l_d,
             o_e, m_e, l_e, bias):
        ones_bd = _head_mean_matrix().astype(BF16)
        is_a = lax.broadcasted_iota(jnp.int32, (1, LANE), 1) < HEAD_DIM
        ones_ext = _split_heads(jnp.ones((KEYS, LANE), BF16), is_a)
        _fill_bias(bias)

        def pro(j, c):
            chunks = [pl.ds(pl.multiple_of((NORM_CHUNKS * j + u) * PRO_ROWS, PRO_ROWS), PRO_ROWS) for u in range(NORM_CHUNKS)]
            normed, _ = _head_norms([q_ref[rows, :] for rows in chunks] + [k_ref[rows, :] for rows in chunks],
                                    [qw_ref[...] * HEAD_DIM ** -0.5] * NORM_CHUNKS + [kw_ref[...]] * NORM_CHUNKS, ones_bd)
            for u, rows in enumerate(chunks):
                qn[rows, :] = normed[u]
                kn[rows, :] = normed[NORM_CHUNKS + u]
            return c

        lax.fori_loop(0, s // (NORM_CHUNKS * PRO_ROWS), pro, 0)

        results = dict(zip(DILATIONS, ((o_ref, m_acc, l_acc), (o_d, m_d, l_d), (o_e, m_e, l_e))))
        for dil in DILATIONS:
            ln = s // dil
            nb = ln // ATT_BLK
            o_out, m_out, l_out = results[dil]
            for r in range(dil):
                def relayout(j, c, dil=dil, r=r, ln=ln):
                    j0 = pl.multiple_of(j * PRO_ROWS, PRO_ROWS)
                    src = _rows(r + dil * j0, PRO_ROWS, dil)
                    dst = pl.ds(r * ln + j0, PRO_ROWS)
                    q_cm[dst, :] = qn[src, :].astype(BF16)
                    k_cm[dst, :] = kn[src, :].astype(BF16)
                    v_cm[dst, :] = v_ref[src, :].astype(BF16)
                    return c

                lax.fori_loop(0, ln // PRO_ROWS, relayout, 0)

            def step(bg, c, nb=nb, o_out=o_out, m_out=m_out, l_out=l_out):
                ids = [_block_ids(bg * ATT_GROUP_FWD + u, nb) for u in range(ATT_GROUP_FWD)]
                kbs = [_split_heads(k_cm[krows, :], is_a) for _, krows, _ in ids]
                scs = [_dot(q_cm[qrows, :], kb, NT) + bias[first] for (qrows, _, first), kb in zip(ids, kbs)]
                mas = [jnp.max(sc[:, :KEYS], axis=-1, keepdims=True) for sc in scs]
                mbs = [jnp.max(sc[:, KEYS:], axis=-1, keepdims=True) for sc in scs]
                ps = [jnp.exp(sc - _pair(ma, mb)).astype(BF16) for sc, ma, mb in zip(scs, mas, mbs)]
                vbs = [jnp.concatenate([_split_heads(v_cm[krows, :], is_a), ones_ext], axis=1) for _, krows, _ in ids]
                ols = [_dot(p, vb, NN) for p, vb in zip(ps, vbs)]
                for (qrows, _, _), ol, ma, mb in zip(ids, ols, mas, mbs):
                    o_out[qrows, :] = ol[:, :LANE]
                    l_out[qrows, :] = ol[:, LANE:]
                    m_out[qrows, :] = jnp.where(is_a, ma, mb)
                return c

            lax.fori_loop(0, nblk // ATT_GROUP_FWD, step, 0)

        for level in range(len(DILATIONS) - 1, 0, -1):
            fine_d, coarse_d = DILATIONS[level - 1], DILATIONS[level]
            ratio, ln_f, ln_c = coarse_d // fine_d, s // fine_d, s // coarse_d
            (o_f, m_f, l_f), (o_c, m_c, l_c) = results[fine_d], results[coarse_d]
            for r in range(coarse_d):
                def merge(j, c, r=r, ratio=ratio, ln_c=ln_c, start=(r % fine_d) * ln_f + r // fine_d,
                          o_f=o_f, m_f=m_f, l_f=l_f, o_c=o_c, m_c=m_c, l_c=l_c):
                    j0 = pl.multiple_of(j * PRO_ROWS, PRO_ROWS)
                    fine = _rows(start + ratio * j0, PRO_ROWS, ratio)
                    coarse = pl.ds(r * ln_c + j0, PRO_ROWS)
                    m_old, m_new = m_f[fine, :], m_c[coarse, :]
                    m = jnp.maximum(m_old, m_new)
                    a_old, a_new = jnp.exp(m_old - m), jnp.exp(m_new - m)
                    o_f[fine, :] = a_old * o_f[fine, :] + a_new * o_c[coarse, :]
                    l_f[fine, :] = a_old * l_f[fine, :] + a_new * l_c[coarse, :]
                    m_f[fine, :] = m
                    return c

                lax.fori_loop(0, ln_c // PRO_ROWS, merge, 0)

        def epi(j, c):
            rows = pl.ds(pl.multiple_of(j * PRO_ROWS, PRO_ROWS), PRO_ROWS)
            l = l_acc[rows, :]
            o_ref[rows, :] = o_ref[rows, :] / l
            lse_ref[rows, :] = m_acc[rows, :] + jnp.log(l)
            return c

        lax.fori_loop(0, s // PRO_ROWS, epi, 0)

    f = jax.ShapeDtypeStruct((s, ATT_D), F32)
    scr = pltpu.VMEM((s, LANE), F32)
    scb = pltpu.VMEM((s, LANE), BF16)
    return _pcall(
        body, (proj, proj, proj, qw, kw), name="att_fwd", grid=(ATT_D // LANE,),
        in_specs=[blk(OFF_Q), blk(OFF_K), blk(OFF_V), wspec, wspec], out_specs=[oblk, oblk], out_shape=[f, f],
        scratch_shapes=[scr, scr, scb, scb, scb] + [scr] * 8 + [pltpu.VMEM((2, ATT_BLK, 2 * KEYS), F32)],
        sem=("parallel",), comm=comm)


def _att_bwd(proj, do, stats, qw, kw, comm=None):
    s = proj.shape[0]
    nblk = s // ATT_BLK
    blk = lambda off: pl.BlockSpec((s, LANE), lambda i: (0, off // LANE + i))
    wspec = pl.BlockSpec((1, LANE), lambda i: (0, i))
    oblk = pl.BlockSpec((s, LANE), lambda i: (0, i))

    def body(q_ref, k_ref, v_ref, do_ref, st_ref, qw_ref, kw_ref, dq_ref, dk_ref, dv_ref, dqw_ref, dkw_ref,
             qn, kn, q_cm, do_cm, k_cm, v_cm, rms, dq_acc, dk_acc, dv_acc, dq_d, dk_d, dv_d, dq_e, dk_e, dv_e, bias):
        ones_bd = _head_mean_matrix().astype(BF16)
        is_a = lax.broadcasted_iota(jnp.int32, (1, LANE), 1) < HEAD_DIM
        first_half = (lax.broadcasted_iota(jnp.int32, (1, LANE), 1) & (HEAD_DIM - 1)) < HALF
        _fill_bias(bias)
        zero = jnp.zeros((PRO_ROWS, LANE), F32)
        results = dict(zip(DILATIONS, ((dq_acc, dk_acc, dv_acc), (dq_d, dk_d, dv_d), (dq_e, dk_e, dv_e))))

        def pro(j, c):
            chunks = [pl.ds(pl.multiple_of((NORM_CHUNKS * j + u) * PRO_ROWS, PRO_ROWS), PRO_ROWS) for u in range(NORM_CHUNKS)]
            normed, rs = _head_norms([q_ref[rows, :] for rows in chunks] + [k_ref[rows, :] for rows in chunks],
                                     [qw_ref[...] * HEAD_DIM ** -0.5] * NORM_CHUNKS + [kw_ref[...]] * NORM_CHUNKS, ones_bd)
            for u, rows in enumerate(chunks):
                qn[rows, :] = normed[u]
                kn[rows, :] = normed[NORM_CHUNKS + u]
                rms[rows, :] = jnp.where(first_half, rs[u], rs[NORM_CHUNKS + u])
                dk_acc[rows, :] = zero
                dv_acc[rows, :] = zero
            return c

        lax.fori_loop(0, s // (NORM_CHUNKS * PRO_ROWS), pro, 0)

        for dil in DILATIONS:
            ln = s // dil
            nb = ln // ATT_BLK
            dq_o, dk_o, dv_o = results[dil]
            for r in range(dil):
                def relayout(j, c, dil=dil, r=r, ln=ln, dk_o=dk_o, dv_o=dv_o):
                    j0 = pl.multiple_of(j * PRO_ROWS, PRO_ROWS)
                    src = _rows(r + dil * j0, PRO_ROWS, dil)
                    dst = pl.ds(r * ln + j0, PRO_ROWS)
                    q_cm[dst, :] = qn[src, :].astype(BF16)
                    k_cm[dst, :] = kn[src, :].astype(BF16)
                    v_cm[dst, :] = v_ref[src, :].astype(BF16)
                    do_cm[dst, :] = do_ref[src, :].astype(BF16)
                    if dil > 1:
                        dk_o[dst, :] = zero
                        dv_o[dst, :] = zero
                    return c

                lax.fori_loop(0, ln // PRO_ROWS, relayout, 0)

            def step(bg, c, nb=nb, dil=dil, dq_o=dq_o, dk_o=dk_o, dv_o=dv_o):
                blocks = [bg * ATT_GROUP_BWD + u for u in range(ATT_GROUP_BWD)]
                ids = [_block_ids(b, nb) for b in blocks]
                qbs = [q_cm[qrows, :] for qrows, _, _ in ids]
                dobs = [do_cm[qrows, :] for qrows, _, _ in ids]
                kbs = [_split_heads(k_cm[krows, :], is_a) for _, krows, _ in ids]
                vbs = [_split_heads(v_cm[krows, :], is_a) for _, krows, _ in ids]
                sts = [st_ref[_natural_rows(b, nb, dil), :] for b in blocks]
                scs = [_dot(qb, kb, NT) + bias[first] for qb, kb, (_, _, first) in zip(qbs, kbs, ids)]
                dps = [_dot(dob, vb, NT) for dob, vb in zip(dobs, vbs)]
                ps = [jnp.exp(sc - _pair(st[:, 0:1], st[:, HEAD_DIM:HEAD_DIM + 1])) for sc, st in zip(scs, sts)]
                dss = [(p * (dp - _pair(st[:, HALF:HALF + 1], st[:, HEAD_DIM + HALF:HEAD_DIM + HALF + 1]))).astype(BF16)
                       for p, dp, st in zip(ps, dps, sts)]
                dqs = [_dot(ds, kb, NN) for ds, kb in zip(dss, kbs)]
                dkfs = [_dot(ds, qb, TN) for ds, qb in zip(dss, qbs)]
                dvfs = [_dot(p.astype(BF16), dob, TN) for p, dob in zip(ps, dobs)]
                for (qrows, krows, _), dq, dkf, dvf in zip(ids, dqs, dkfs, dvfs):
                    dq_o[qrows, :] = dq
                    dk_o[krows, :] += jnp.where(is_a, dkf[:KEYS], dkf[KEYS:])
                    dv_o[krows, :] += jnp.where(is_a, dvf[:KEYS], dvf[KEYS:])
                return c

            lax.fori_loop(0, nblk // ATT_GROUP_BWD, step, 0)

        for level in range(len(DILATIONS) - 1, 0, -1):
            fine_d, coarse_d = DILATIONS[level - 1], DILATIONS[level]
            ratio, ln_f, ln_c = coarse_d // fine_d, s // fine_d, s // coarse_d
            for r in range(coarse_d):
                def merge(j, c, r=r, ratio=ratio, ln_c=ln_c, start=(r % fine_d) * ln_f + r // fine_d,
                          fine_bufs=results[fine_d], coarse_bufs=results[coarse_d]):
                    j0 = pl.multiple_of(j * PRO_ROWS, PRO_ROWS)
                    fine = _rows(start + ratio * j0, PRO_ROWS, ratio)
                    coarse = pl.ds(r * ln_c + j0, PRO_ROWS)
                    for f_buf, c_buf in zip(fine_bufs, coarse_bufs):
                        f_buf[fine, :] += c_buf[coarse, :]
                    return c

                lax.fori_loop(0, ln_c // PRO_ROWS, merge, 0)

        def epi(j, c):
            chunks = [pl.ds(pl.multiple_of((NORM_CHUNKS * j + u) * PRO_ROWS, PRO_ROWS), PRO_ROWS) for u in range(NORM_CHUNKS)]
            packed = [rms[rows, :] for rows in chunks]
            rs = ([jnp.where(first_half, p, pltpu.roll(p, HALF, axis=1)) for p in packed]
                  + [jnp.where(first_half, pltpu.roll(p, LANE - HALF, axis=1), p) for p in packed])
            dxs, dws = _head_norms_bwd(
                [dq_acc[rows, :] for rows in chunks] + [dk_acc[rows, :] for rows in chunks],
                [q_ref[rows, :] for rows in chunks] + [k_ref[rows, :] for rows in chunks],
                [qw_ref[...] * HEAD_DIM ** -0.5] * NORM_CHUNKS + [kw_ref[...]] * NORM_CHUNKS, rs, ones_bd)
            dqw, dkw = c
            for u, rows in enumerate(chunks):
                dq_ref[rows, :] = dxs[u].astype(BF16)
                dk_ref[rows, :] = dxs[NORM_CHUNKS + u].astype(BF16)
                dv_ref[rows, :] = dv_acc[rows, :].astype(BF16)
                dqw, dkw = dqw + dws[u], dkw + dws[NORM_CHUNKS + u]
            return dqw, dkw

        zrow = jnp.zeros((1, LANE), F32)
        dqw, dkw = lax.fori_loop(0, s // (NORM_CHUNKS * PRO_ROWS), epi, (zrow, zrow))
        dqw_ref[...] = dqw * HEAD_DIM ** -0.5
        dkw_ref[...] = dkw

    o = jax.ShapeDtypeStruct((s, ATT_D), BF16)
    ov = jax.ShapeDtypeStruct((1, ATT_D), F32)
    scr = pltpu.VMEM((s, LANE), F32)
    scb = pltpu.VMEM((s, LANE), BF16)
    return _pcall(
        body, (proj, proj, proj, do, stats, qw, kw), name="att_bwd", grid=(ATT_D // LANE,),
        in_specs=[blk(OFF_Q), blk(OFF_K), blk(OFF_V), oblk, oblk, wspec, wspec],
        out_specs=[oblk, oblk, oblk, wspec, wspec], out_shape=[o, o, o, ov, ov],
        scratch_shapes=[scr, scr, scb, scb, scb, scb] + [scr] * 10 + [pltpu.VMEM((2, ATT_BLK, 2 * KEYS), F32)],
        sem=("parallel",), comm=comm)


def _att_norm_fwd(o, nw, ycat):
    s = o.shape[0]
    row = pl.BlockSpec((ROW_TILE, ATT_D), lambda i: (i, 0))
    vec = pl.BlockSpec((1, ATT_D), lambda i: (0, 0))

    def body(o_ref, nw_ref, ycat_ref, y_ref):
        o = o_ref[...]
        r = lax.rsqrt(jnp.mean(o * o, axis=-1, keepdims=True) + EPS)
        y_ref[...] = (o * r * nw_ref[...]).astype(BF16)

    return pl.pallas_call(body, name="att_norm_fwd", grid=(s // ROW_TILE,),
                          in_specs=[row, vec, pl.BlockSpec(memory_space=pl.ANY)],
                          out_specs=pl.BlockSpec((ROW_TILE, ATT_D), lambda i: (i, 1)),
                          out_shape=jax.ShapeDtypeStruct(ycat.shape, BF16), input_output_aliases={2: 0},
                          compiler_params=_cparams(("parallel",)))(o, nw, ycat)


def _mixer_split_epilogue(dycat, first, rows, vecs, outs):
    (o_ref, lse_ref), (nw_ref,), (dyssd_ref, do_ref, st_ref, dnw_ref) = rows, vecs, outs

    @pl.when(first)
    def _():
        dnw_ref[...] = jnp.zeros_like(dnw_ref)

    dyssd_ref[...] = dycat[:, :SSD_D_INNER]
    dy = dycat[:, SSD_D_INNER:]
    o = o_ref[...]
    r = lax.rsqrt(jnp.mean(o * o, axis=-1, keepdims=True) + EPS)
    nrm = o * r
    dnw_ref[...] += jnp.sum(dy * nrm, axis=0, keepdims=True)
    dn = dy * nw_ref[...]
    do = r * (dn - nrm * jnp.mean(dn * nrm, axis=-1, keepdims=True))
    do_ref[...] = do
    ones_bd = _head_mean_matrix().astype(BF16)
    prod = do * o
    delta = jnp.concatenate([_head_sum2(prod[:, j * LANE:(j + 1) * LANE], ones_bd) for j in range(ATT_D // LANE)], axis=1)
    lane = lax.broadcasted_iota(jnp.int32, (1, ATT_D), 1)
    st_ref[...] = jnp.where((lane & (HEAD_DIM - 1)) < HALF, lse_ref[...], delta)


def _ada_fwd(c_all, w_ada):
    def body(c_ref, w_ref, o_ref):
        cv = c_ref[...]
        o_ref[...] = _dot((cv * _sigmoid(cv)).astype(BF16), w_ref[...].astype(BF16), NN)

    return pl.pallas_call(body, name="ada_fwd", out_shape=jax.ShapeDtypeStruct((c_all.shape[0], w_ada.shape[1]), F32),
                          compiler_params=_cparams())(c_all, w_ada)


def _adamw_math(g, w, m, v):
    m_new = ADAM_B1 * m + (1.0 - ADAM_B1) * g
    v_new = ADAM_B2 * v + (1.0 - ADAM_B2) * (g * g)
    m_hat = m_new / (1.0 - ADAM_B1 ** ADAM_STEP)
    v_hat = v_new / (1.0 - ADAM_B2 ** ADAM_STEP)
    delta = -ADAM_LR * (m_hat / (jnp.sqrt(v_hat) + ADAM_EPS) + ADAM_WD * w)
    return delta, m_new, v_new


def _ada_bwd_adamw(c_all, dmod_cols, w, m, v):
    rows, cols = w.shape
    tr = 256
    blk = pl.BlockSpec((tr, cols), lambda i: (i, 0))

    def body(c_ref, d_ref, w_ref, m_ref, v_ref, g_ref, dl_ref, mo_ref, vo_ref):
        cv = c_ref[...]
        ca = cv * _sigmoid(cv)
        g = ca[:, 0:1] * d_ref[0:1, :]
        for b in range(1, N_DEV):
            g = g + ca[:, b:b + 1] * d_ref[b:b + 1, :]
        g_ref[...] = g
        dl_ref[...], mo_ref[...], vo_ref[...] = _adamw_math(g, w_ref[...], m_ref[...], v_ref[...])

    o = jax.ShapeDtypeStruct((rows, cols), F32)
    return pl.pallas_call(
        body, name="ada_bwd_adamw", grid=(rows // tr,),
        in_specs=[pl.BlockSpec((tr, N_DEV), lambda i: (i, 0)), pl.BlockSpec((N_DEV, cols), lambda i: (0, 0)), blk, blk, blk],
        out_specs=[blk] * 4, out_shape=[o, o, o, o], compiler_params=_cparams(("parallel",)))(c_all.T, dmod_cols, w, m, v)


def _reduce_adamw(slabs, w, m, v, name):
    rows, cols = w.shape
    n_src = slabs.shape[0]
    if rows % 128 == 0:
        tr, steps = 128, rows // 128
        blk = pl.BlockSpec((tr, cols), lambda i: (i, 0))
        sblk = pl.BlockSpec((n_src, tr, cols), lambda i: (0, i, 0))
    else:
        tc, steps = 256, cols // 256
        blk = pl.BlockSpec((rows, tc), lambda i: (0, i))
        sblk = pl.BlockSpec((n_src, rows, tc), lambda i: (0, 0, i))

    def body(s_ref, w_ref, m_ref, v_ref, g_ref, dl_ref, mo_ref, vo_ref):
        g = s_ref[0].astype(F32)
        for src in range(1, n_src):
            g = g + s_ref[src].astype(F32)
        g_ref[...] = g
        dl_ref[...], mo_ref[...], vo_ref[...] = _adamw_math(g, w_ref[...], m_ref[...], v_ref[...])

    o = jax.ShapeDtypeStruct((rows, cols), F32)
    return pl.pallas_call(
        body, name=name, grid=(steps,), in_specs=[sblk, blk, blk, blk],
        out_specs=[blk] * 4, out_shape=[o, o, o, o], compiler_params=_cparams(("parallel",)))(slabs, w, m, v)


def _small_reduce_adamw(gathered, w, m, v):
    def body(s_ref, w_ref, m_ref, v_ref, g_ref, dl_ref, mo_ref, vo_ref):
        g = s_ref[0]
        for dev in range(1, N_DEV):
            g = g + s_ref[dev]
        g_ref[...] = g
        dl_ref[...], mo_ref[...], vo_ref[...] = _adamw_math(g, w_ref[...], m_ref[...], v_ref[...])

    o = jax.ShapeDtypeStruct(w.shape, F32)
    return pl.pallas_call(body, name="small_reduce_adamw", out_shape=[o, o, o, o], compiler_params=_cparams())(gathered, w, m, v)


def _adamw_small(g, w, m, v, name):
    def body(g_ref, w_ref, m_ref, v_ref, dl_ref, mo_ref, vo_ref):
        dl_ref[...], mo_ref[...], vo_ref[...] = _adamw_math(g_ref[...], w_ref[...], m_ref[...], v_ref[...])

    o = jax.ShapeDtypeStruct(w.shape, F32)
    return pl.pallas_call(body, name=name, out_shape=[o, o, o], compiler_params=_cparams())(g, w, m, v)


class _Exchange:
    def __init__(self, arrs, scatter):
        self.arrs, self.scatter, self.n = list(arrs), scatter, len(arrs)
        hbm = pl.BlockSpec(memory_space=pltpu.HBM)
        self.in_specs = [hbm] * self.n
        self.out_specs = [hbm] * self.n
        self.out_shape = [jax.ShapeDtypeStruct(a.shape if scatter else (N_DEV,) + a.shape, a.dtype) for a in self.arrs]
        self.scratch = [pltpu.SemaphoreType.DMA((self.n * (N_DEV - 1),)), pltpu.SemaphoreType.DMA((self.n * (N_DEV - 1),)),
                        pltpu.SemaphoreType.DMA((self.n,))]

    def _local(self, ins, outs, sems):
        me = 4 * lax.axis_index("x") + 2 * lax.axis_index("y") + lax.axis_index("c")
        return [pltpu.make_async_copy(ins[a].at[me] if self.scatter else ins[a], outs[a].at[me], sems[2].at[a])
                for a in range(self.n)]

    def _remote(self, ins, outs, sems, arriving):
        send_sems, recv_sems, _ = sems
        x, y, c = lax.axis_index("x"), lax.axis_index("y"), lax.axis_index("c")
        me = 4 * x + 2 * y + c
        remote = []
        for a in range(self.n):
            for k in range(1, N_DEV):
                px = 1 - x if k & 4 else x
                py = 1 - y if k & 2 else y
                pc = 1 - c if k & 1 else c
                peer = 4 * px + 2 * py + pc
                sem = a * (N_DEV - 1) + k - 1
                remote.append(pltpu.make_async_remote_copy(
                    src_ref=ins[a].at[peer] if self.scatter else ins[a], dst_ref=outs[a].at[peer if arriving else me],
                    send_sem=send_sems.at[sem], recv_sem=recv_sems.at[sem], device_id=(px, py, pc), device_id_type=MESH_IDS))
        return remote

    def start(self, ins, outs, sems):
        for cp in self._local(ins, outs, sems) + self._remote(ins, outs, sems, arriving=False):
            cp.start()

    def forward(self, ins, outs, sems):
        pass

    def wait(self, ins, outs, sems):
        for send, arrival in zip(self._remote(ins, outs, sems, arriving=False), self._remote(ins, outs, sems, arriving=True)):
            send.wait_send()
            arrival.wait_recv()
        for cp in self._local(ins, outs, sems):
            cp.wait()


N_CHIP = N_DEV // 2


class _SiblingSwap(_Exchange):
    def __init__(self, arrs):
        super().__init__(arrs, scatter=True)
        self.out_shape = [jax.ShapeDtypeStruct((N_CHIP,) + a.shape[2:], a.dtype) for a in self.arrs]
        self.scratch = [pltpu.SemaphoreType.DMA((self.n,)), pltpu.SemaphoreType.DMA((self.n,)), pltpu.SemaphoreType.DMA((1,))]

    def _copies(self, ins, outs, sems):
        x, y, c = lax.axis_index("x"), lax.axis_index("y"), lax.axis_index("c")
        return [pltpu.make_async_remote_copy(src_ref=ins[a].at[:, 1 - c], dst_ref=outs[a], send_sem=sems[0].at[a], recv_sem=sems[1].at[a],
                                             device_id=(x, y, 1 - c), device_id_type=MESH_IDS) for a in range(self.n)]

    def start(self, ins, outs, sems):
        for cp in self._copies(ins, outs, sems):
            cp.start()

    def wait(self, ins, outs, sems):
        for cp in self._copies(ins, outs, sems):
            cp.wait()


class _ChipScatter(_Exchange):
    def __init__(self, arrs):
        super().__init__(arrs, scatter=True)
        n_pairs = self.n * (N_CHIP - 1)
        self.scratch = [pltpu.SemaphoreType.DMA((n_pairs,)), pltpu.SemaphoreType.DMA((n_pairs,)), pltpu.SemaphoreType.DMA((self.n,))]

    def _local(self, ins, outs, sems):
        chip = 2 * lax.axis_index("x") + lax.axis_index("y")
        return [pltpu.make_async_copy(ins[a].at[chip], outs[a].at[chip], sems[2].at[a]) for a in range(self.n)]

    def _remote(self, ins, outs, sems, arriving):
        send_sems, recv_sems, _ = sems
        x, y, c = lax.axis_index("x"), lax.axis_index("y"), lax.axis_index("c")
        chip = 2 * x + y
        remote = []
        for a in range(self.n):
            for k in range(1, N_CHIP):
                px = 1 - x if k & 2 else x
                py = 1 - y if k & 1 else y
                peer = 2 * px + py
                sem = a * (N_CHIP - 1) + k - 1
                remote.append(pltpu.make_async_remote_copy(
                    src_ref=ins[a].at[peer], dst_ref=outs[a].at[peer if arriving else chip], send_sem=send_sems.at[sem],
                    recv_sem=recv_sems.at[sem], device_id=(px, py, c), device_id_type=MESH_IDS))
        return remote


def _chip_sum(mine, theirs):
    n, rows, cols = mine.shape
    blk = pl.BlockSpec((1, rows, 256), lambda q, j: (q, 0, j))

    def body(a_ref, b_ref, o_ref):
        o_ref[...] = (a_ref[...].astype(F32) + b_ref[...].astype(F32)).astype(BF16)

    return pl.pallas_call(body, name="chip_sum", grid=(n, cols // 256), in_specs=[blk, blk], out_specs=blk,
                          out_shape=jax.ShapeDtypeStruct(mine.shape, BF16),
                          compiler_params=_cparams(("parallel", "parallel")))(mine, theirs)


class _Gather2(_Exchange):
    def __init__(self, arrs):
        super().__init__(arrs, scatter=False)

    def _copies(self, ins, outs, sems):
        send_sems, recv_sems, _ = sems
        x, y, c = lax.axis_index("x"), lax.axis_index("y"), lax.axis_index("c")
        sibling = (x, y, 1 - c)
        chips = [(1 - x, y), (x, 1 - y), (1 - x, 1 - y)]
        first, passed, landed = [], [], []
        for a in range(self.n):
            def copy(k, block, to, src=None, a=a):
                slab = outs[a].at[4 * block[0] + 2 * block[1] + block[2]]
                return pltpu.make_async_remote_copy(
                    src_ref=slab if src is None else src, dst_ref=slab, send_sem=send_sems.at[a * (N_DEV - 1) + k],
                    recv_sem=recv_sems.at[a * (N_DEV - 1) + k], device_id=to, device_id_type=MESH_IDS)

            first.append(copy(0, (x, y, c), sibling, src=ins[a]))
            landed.append(copy(0, sibling, sibling))
            for j, chip in enumerate(chips):
                first.append(copy(1 + j, (x, y, c), (*chip, c), src=ins[a]))
                passed.append((copy(1 + j, (*chip, c), sibling), copy(4 + j, (*chip, c), sibling)))
                landed.append(copy(4 + j, (*chip, 1 - c), sibling))
        return first, passed, landed

    def start(self, ins, outs, sems):
        for cp in self._local(ins, outs, sems) + self._copies(ins, outs, sems)[0]:
            cp.start()

    def forward(self, ins, outs, sems):
        for arrival, onward in self._copies(ins, outs, sems)[1]:
            arrival.wait_recv()
            onward.start()

    def wait(self, ins, outs, sems):
        first, passed, landed = self._copies(ins, outs, sems)
        for arrival in landed:
            arrival.wait_recv()
        for cp in first + [onward for _, onward in passed]:
            cp.wait_send()
        for cp in self._local(ins, outs, sems):
            cp.wait()


def _split_comm_refs(refs, n_in, n_out, n_scr, comm):
    nc = comm.n if comm is not None else 0
    ns = 3 if comm is not None else 0
    pos, groups = 0, []
    for cnt in (n_in, nc, n_out, nc, n_scr, ns):
        groups.append(refs[pos:pos + cnt])
        pos += cnt
    assert pos == len(refs), (pos, len(refs))
    return groups


def _pcall(body, args, *, name, grid, in_specs, out_specs, out_shape, scratch_shapes=(), sem=None, comm=None):
    in_specs, out_specs, out_shape, scratch_shapes = list(in_specs), list(out_specs), list(out_shape), list(scratch_shapes)
    n_in, n_out, n_scr = len(in_specs), len(out_specs), len(scratch_shapes)
    if comm is None:
        kernel_body = body
    else:
        def kernel_body(*refs):
            ins, cins, outs, couts, scr, sems = _split_comm_refs(refs, n_in, n_out, n_scr, comm)
            ids = [pl.program_id(a) for a in range(len(grid))]
            first, last = ids[0] == 0, ids[0] == grid[0] - 1
            for a in range(1, len(grid)):
                first, last = first & (ids[a] == 0), last & (ids[a] == grid[a] - 1)

            middle = ids[0] == (2 * grid[0]) // 3
            for a in range(1, len(grid)):
                middle = middle & (ids[a] == 0)

            @pl.when(first)
            def _():
                comm.start(cins, couts, sems)

            @pl.when(middle)
            def _():
                comm.forward(cins, couts, sems)

            body(*ins, *outs, *scr)

            @pl.when(last)
            def _():
                comm.wait(cins, couts, sems)

        in_specs, out_specs, out_shape = in_specs + comm.in_specs, out_specs + comm.out_specs, out_shape + comm.out_shape
        scratch_shapes, args = scratch_shapes + comm.scratch, list(args) + comm.arrs
        sem = ("arbitrary",) * len(grid)
    res = pl.pallas_call(kernel_body, name=name, grid=grid, in_specs=in_specs, out_specs=out_specs, out_shape=out_shape,
                         scratch_shapes=scratch_shapes, compiler_params=_cparams(sem))(*args)
    return res[:n_out], res[n_out:]


def _exchange(arrs, name, scatter=False, ex=None):
    if ex is None:
        ex = _Exchange(arrs, scatter=True) if scatter else _Gather2(arrs)

    def body(*refs):
        _, ins, _, outs, _, sems = _split_comm_refs(refs, 0, 0, 0, ex)
        ex.start(ins, outs, sems)
        ex.forward(ins, outs, sems)
        ex.wait(ins, outs, sems)

    return pl.pallas_call(body, name=name, in_specs=ex.in_specs, out_specs=ex.out_specs, out_shape=ex.out_shape,
                          scratch_shapes=ex.scratch)(*ex.arrs)


def _pad_lanes(v, width=LANE):
    return jnp.pad(v, ((0, 0), (0, width - v.shape[1])))


def _shards_to_cols(g):
    return jnp.transpose(g, (1, 0, 2)).reshape(g.shape[1], N_DEV * g.shape[2])


def _local_step(x, tgt, mod, w_in_pt, conv_w, conv_b, dt_bias, a_log, d_skip, ssd_norm_w, q_norm_w, k_norm_w,
                attn_norm_w, w_out_sh, w_ff1_sh, w_ff2_sh, norm1_w, norm2_w, core):
    shift1, scale1, gate1, shift2, scale2, gate2 = [mod[i:i + 1] for i in range(N_MOD)]
    dtb, alog, dsk = _pad_lanes(dt_bias), _pad_lanes(a_log), _pad_lanes(d_skip)
    qw, kw = jnp.tile(q_norm_w, (1, ATT_HEADS)), jnp.tile(k_norm_w, (1, ATT_HEADS))

    h1 = _norm_mod_fwd(x, norm1_w, scale1, shift1, "norm1_fwd")
    proj = _matmul(h1, w_in_pt, tb=True, tm=2048, tn=896, tk=1024, name="in_proj")
    pre, act = _conv_fwd(proj, conv_w, conv_b)
    ypre, ycat_ssd, hall = _ssd_fwd(proj, act, dtb, alog, dsk, ssd_norm_w)
    (o_att, lse), (w_out_g, w_ff1_g, w_ff2_g) = _att_fwd(proj, qw, kw, comm=_Gather2([w_out_sh, w_ff1_sh, w_ff2_sh]))
    w_out = w_out_g.reshape(2 * D_MODEL, D_MODEL)
    w_ff1 = _shards_to_cols(w_ff1_g)
    w_ff2 = w_ff2_g.reshape(D_FF, D_MODEL)
    ycat = _att_norm_fwd(o_att, attn_norm_w, ycat_ssd)
    row32, row16, vec32 = ("row", F32), ("row", BF16), ("vec", F32)
    mix, x1, h2 = _matmul_rows(ycat, w_out, _residual_norm_epilogue, [x], [gate1, norm2_w, scale2, shift2],
                               [row32, row32, row16], tm=512, name="out_proj")
    u, act_ff = _matmul(h2, w_ff1, tm=1024, tn=2048, tk=1024, name="ff1", mode="relu2")
    loss, dout, dff, dgate2 = _matmul_rows(act_ff, w_ff2, _loss_epilogue, [x1, tgt], [gate2],
                                           [("one", F32), row32, row16, vec32], tm=512, name="ff2")

    du = _matmul(dff, w_ff2, tb=True, tm=512, tn=4096, tk=1024, out_dtype=BF16, name="ff2_dx", mode="drelu2", u=u)
    g_ff2 = _matmul(act_ff, dff, ta=True, tm=512, tn=1024, tk=4096, out_dtype=BF16, name="ff2_dw")
    dx1, dshift2, dscale2, g_norm2, dmix, dgate1 = _matmul_rows(
        du, w_ff1, _norm_bwd_epilogue, [x1, dout, mix], [norm2_w, scale2, gate1],
        [row32, vec32, vec32, vec32, row16, vec32], tb=True, tm=512, name="ff1_dx")
    g_ff1 = _matmul(h2, du, ta=True, tm=1024, tn=D_FF // N_DEV, tk=4096, out_dtype=BF16, name="ff1_dw", shard_out=True)

    dy_ssd, do, stats, g_attn_norm = _matmul_rows(
        dmix, w_out, _mixer_split_epilogue, [o_att, lse], [attn_norm_w],
        [("row", F32, SSD_D_INNER), ("row", F32, ATT_D), ("row", F32, ATT_D), ("vec", F32, ATT_D)], tb=True, tm=512, name="out_proj_dx")
    g_out = _matmul(ycat, dmix, ta=True, tm=512, tn=1024, tk=4096, out_dtype=BF16, name="out_proj_dw")
    ff_slabs = [g_ff1, g_ff2.reshape(N_DEV, D_FF // N_DEV, D_MODEL)]
    (dq, dk, dv, dqw, dkw), (s_ff1, s_ff2) = _att_bwd(proj, do, stats, qw, kw, comm=_Exchange(ff_slabs, scatter=True))
    out_slabs = [g_out.astype(BF16).reshape(N_DEV, 2 * D_MODEL // N_DEV, D_MODEL)]
    (dz, dact, ddtr, da, g_dsk, g_dtb, g_ssd_norm), (s_out,) = _ssd_bwd(
        dy_ssd, ypre, proj, act, hall, dtb, alog, dsk, ssd_norm_w, comm=_Exchange(out_slabs, scatter=True))
    dxbc, g_conv_w, g_conv_b = _conv_bwd(dact, pre, proj, conv_w)
    dproj = [(dz, OFF_Z), (dxbc, OFF_XBC), (ddtr, OFF_DT), (dq, OFF_Q), (dk, OFF_K), (dv, OFF_V)]
    g_head, g_tail = _pieces_t_matmul([[dz, dxbc], [dq, dk, dv]], h1, tm=256, name="in_proj_dw")
    g_dt = _matmul(ddtr, h1, ta=True, tm=LANE, tn=1024, tk=4096, out_dtype=BF16, name="in_proj_dw_dt")[:SSD_HEADS]
    in_slabs = jnp.concatenate([g_head, g_dt, g_tail], axis=0).reshape(N_CHIP, 2, IN_W // N_DEV, D_MODEL)
    (sibling_slabs,) = _exchange(None, "swap_w_in_grads", ex=_SiblingSwap([in_slabs]))
    chip_slabs = _chip_sum(lax.dynamic_index_in_dim(in_slabs, core, axis=1, keepdims=False), sibling_slabs)
    (grad_x, dshift1, dscale1, g_norm1), (s_in,) = _matmul_rows(
        dproj, w_in_pt, _norm_bwd_epilogue, [x, dx1], [norm1_w, scale1], [row32, vec32, vec32, vec32],
        tm=256, name="in_proj_dx", comm=_ChipScatter([chip_slabs]))

    dmod = jnp.concatenate([dshift1, dscale1, dgate1, dshift2, dscale2, dgate2], axis=0)
    g_alog = da[:, :SSD_HEADS] * (-jnp.exp(a_log))
    g_qw = dqw.reshape(ATT_HEADS, HEAD_DIM).sum(axis=0, keepdims=True)
    g_kw = dkw.reshape(ATT_HEADS, HEAD_DIM).sum(axis=0, keepdims=True)
    return dict(loss=loss, grad_x=grad_x, dmod=dmod, norm1_w=g_norm1, norm2_w=g_norm2, w_in=s_in, conv_w=g_conv_w,
                conv_b=g_conv_b, dt_bias=g_dtb[:, :SSD_HEADS], a_log=g_alog, d_skip=g_dsk[:, :SSD_HEADS],
                ssd_norm_w=g_ssd_norm, q_norm_w=g_qw, k_norm_w=g_kw, attn_norm_w=g_attn_norm, w_out=s_out,
                w_ff1=s_ff1, w_ff2=s_ff2)


def _pack_w_in_rows(wt_full):
    cut = OFF_DT + SSD_HEADS
    pad = jnp.zeros((LANE - SSD_HEADS, wt_full.shape[1]), wt_full.dtype)
    return jnp.concatenate([wt_full[:cut], pad, wt_full[cut:]], axis=0)


MISC_FIELDS = (("dt_bias", SSD_HEADS), ("a_log", SSD_HEADS), ("d_skip", SSD_HEADS), ("q_norm_w", HEAD_DIM), ("k_norm_w", HEAD_DIM),
               ("loss", 1))
SMALL_LAYOUT = (("b_ada", 6), ("norm1_w", 1), ("norm2_w", 1), ("conv_w", 8), ("conv_b", 2), ("ssd_norm_w", 1),
                ("attn_norm_w", 1), ("misc", 1))


def _pack_small(vals):
    rows = []
    for name, nrow in SMALL_LAYOUT:
        if name == "misc":
            misc = jnp.concatenate([vals[f].reshape(1, n) if f in vals else jnp.zeros((1, n), F32) for f, n in MISC_FIELDS], axis=1)
            rows.append(_pad_lanes(misc, D_MODEL))
        elif name in vals:
            rows.append(vals[name].reshape(nrow, D_MODEL))
        else:
            rows.append(jnp.zeros((nrow, D_MODEL), F32))
    used = sum(n for _, n in SMALL_LAYOUT)
    rows.append(jnp.zeros((SMALL_ROWS - used, D_MODEL), F32))
    return jnp.concatenate(rows, axis=0)


def _unpack_small(packed):
    out, r = {}, 0
    for name, nrow in SMALL_LAYOUT:
        blk = packed[r:r + nrow]
        r += nrow
        if name == "misc":
            c0 = 0
            for f, n in MISC_FIELDS:
                out[f] = blk[:, c0:c0 + n]
                c0 += n
        elif name == "b_ada":
            out[name] = blk.reshape(1, N_MOD * D_MODEL)
        elif name == "conv_w":
            out[name] = blk.reshape(CONV_K, CONV_CH)
        elif name == "conv_b":
            out[name] = blk.reshape(1, CONV_CH)
        else:
            out[name] = blk
    return out


WEIGHT_NAMES = ("norm1_w", "norm2_w", "w_ada", "b_ada", "w_in", "conv_w", "conv_b", "dt_bias", "a_log", "d_skip",
                "ssd_norm_w", "q_norm_w", "k_norm_w", "attn_norm_w", "w_out", "w_ff1", "w_ff2")
SMALL_NAMES = ("norm1_w", "norm2_w", "b_ada", "conv_b", "dt_bias", "a_log", "d_skip", "ssd_norm_w", "q_norm_w",
               "k_norm_w", "attn_norm_w")


def kernel(x, c, norm1_w, norm2_w, w_ada, b_ada, w_in, conv_w, conv_b, dt_bias, a_log, d_skip, ssd_norm_w, q_norm_w, k_norm_w, attn_norm_w, w_out, w_ff1, w_ff2, loss_target, m_norm1_w, m_norm2_w, m_w_ada, m_b_ada, m_w_in, m_conv_w, m_conv_b, m_dt_bias, m_a_log, m_d_skip, m_ssd_norm_w, m_q_norm_w, m_k_norm_w, m_attn_norm_w, m_w_out, m_w_ff1, m_w_ff2, v_norm1_w, v_norm2_w, v_w_ada, v_b_ada, v_w_in, v_conv_w, v_conv_b, v_dt_bias, v_a_log, v_d_skip, v_ssd_norm_w, v_q_norm_w, v_k_norm_w, v_attn_norm_w, v_w_out, v_w_ff1, v_w_ff2):
    args = dict(locals())
    w = {n: args[n] for n in WEIGHT_NAMES}
    m = {n: args["m_" + n] for n in WEIGHT_NAMES}
    v = {n: args["v_" + n] for n in WEIGHT_NAMES}
    me = 4 * lax.axis_index("x") + 2 * lax.axis_index("y") + lax.axis_index("c")

    c_rows = jnp.pad(c, ((0, 7), (0, 0)))
    w_in_t, m_in_t, v_in_t = [jnp.transpose(t["w_in"][0]) for t in (w, m, v)]
    c_g, conv_g, w_in_g = _exchange([c_rows, w["conv_w"][0], w_in_t.astype(BF16)], "gather_w_in", scatter=False)
    c_all = c_g[:, 0, :]
    conv_full = _shards_to_cols(conv_g)
    w_in_pt = _pack_w_in_rows(w_in_g.reshape(IN_W, D_MODEL))

    mod_part = _ada_fwd(c_all, w["w_ada"][0])
    (mod_g,) = _exchange([mod_part], "gather_mod", scatter=False)
    mod_mine = lax.dynamic_index_in_dim(mod_g, me, axis=1, keepdims=False).reshape(1, N_MOD * D_MODEL) + w["b_ada"]
    mod = mod_mine.reshape(N_MOD, D_MODEL)

    res = _local_step(x[0], loss_target[0], mod, w_in_pt, conv_full, w["conv_b"], w["dt_bias"], w["a_log"], w["d_skip"],
                      w["ssd_norm_w"], w["q_norm_w"], w["k_norm_w"], w["attn_norm_w"], w["w_out"][0].astype(BF16),
                      w["w_ff1"][0].astype(BF16), w["w_ff2"][0].astype(BF16), w["norm1_w"], w["norm2_w"], lax.axis_index("c"))

    small_vals = {n: res[n] for n in SMALL_NAMES if n != "b_ada"}
    small_vals["b_ada"] = res["dmod"]
    small_vals["conv_w"] = res["conv_w"]
    small_vals["loss"] = res["loss"]
    (small_g,) = _exchange([_pack_small(small_vals)], "gather_small", scatter=False)

    grads, delta, new_m, new_v = {}, {}, {}, {}
    for name in ("w_out", "w_ff1", "w_ff2"):
        outs = _reduce_adamw(res[name], w[name][0], m[name][0], v[name][0], "adamw_" + name)
        grads[name], delta[name], new_m[name], new_v[name] = [o[None] for o in outs]
    outs = _reduce_adamw(res["w_in"], w_in_t, m_in_t, v_in_t, "adamw_w_in")
    grads["w_in"], delta["w_in"], new_m["w_in"], new_v["w_in"] = [jnp.transpose(o)[None] for o in outs]

    sm = _small_reduce_adamw(small_g, _pack_small({n: w[n] for n in SMALL_NAMES}), _pack_small({n: m[n] for n in SMALL_NAMES}),
                             _pack_small({n: v[n] for n in SMALL_NAMES}))
    sm = [_unpack_small(p) for p in sm]
    for n in SMALL_NAMES:
        grads[n], delta[n], new_m[n], new_v[n] = [p[n] for p in sm]
    shard_w = CONV_CH // N_DEV
    g_conv = lax.dynamic_slice_in_dim(sm[0]["conv_w"], me * shard_w, shard_w, axis=1)
    cw = _adamw_small(g_conv, w["conv_w"][0], m["conv_w"][0], v["conv_w"][0], "adamw_conv_w")
    grads["conv_w"] = g_conv[None]
    delta["conv_w"], new_m["conv_w"], new_v["conv_w"] = [o[None] for o in cw]

    ada_w = w_ada.shape[2]
    dmod_all = small_g[:, :N_MOD, :].reshape(N_DEV, N_MOD * D_MODEL)
    dmod_cols = lax.dynamic_slice_in_dim(dmod_all, me * ada_w, ada_w, axis=1)
    outs = _ada_bwd_adamw(c_all, dmod_cols, w["w_ada"][0], m["w_ada"][0], v["w_ada"][0])
    grads["w_ada"], delta["w_ada"], new_m["w_ada"], new_v["w_ada"] = [o[None] for o in outs]

    loss = sm[0]["loss"][0, 0]
    return (loss, res["grad_x"][None], *[grads[n] for n in WEIGHT_NAMES], *[delta[n] for n in WEIGHT_NAMES],
            *[new_m[n] for n in WEIGHT_NAMES], *[new_v[n] for n in WEIGHT_NAMES])
```

```python
import jax
import jax.numpy as jnp
from jax import lax
from jax.experimental import pallas as pl
from jax.experimental.pallas import tpu as pltpu

F32 = jnp.float32
BF16 = jnp.bfloat16
HIGHEST = lax.Precision.HIGHEST
MESH_IDS = pl.DeviceIdType.MESH

N_DEV = 8
D_MODEL = 1024
HEAD_DIM = 64
SSD_HEADS = 16
SSD_GROUPS = 4
HEADS_PER_GROUP = SSD_HEADS // SSD_GROUPS
SSD_STATE = 128
SSD_CHUNK = 128
SSD_D_INNER = SSD_HEADS * HEAD_DIM
GROUP_WIDTH = SSD_D_INNER // SSD_GROUPS
CONV_K = 4
CONV_CH = SSD_D_INNER + 2 * SSD_GROUPS * SSD_STATE
ATT_HEADS = 16
ATT_D = ATT_HEADS * HEAD_DIM
ATT_BLK = 128
DILATIONS = (1, 4, 16)
D_FF = 4 * D_MODEL
N_MOD = 6
EPS = 1e-6
IN_W = SSD_D_INNER + CONV_CH + SSD_HEADS + 3 * ATT_D
LANE = 128
OFF_Z, OFF_XBC, OFF_DT = 0, SSD_D_INNER, SSD_D_INNER + CONV_CH
OFF_Q = OFF_DT + LANE
OFF_K, OFF_V = OFF_Q + ATT_D, OFF_Q + 2 * ATT_D
IN_WP = OFF_V + ATT_D

ADAM_LR, ADAM_B1, ADAM_B2, ADAM_EPS, ADAM_WD, ADAM_STEP = 0.001, 0.9, 0.999, 1e-08, 0.01, 10
VMEM_LIMIT = 60 * 1024 * 1024
ROW_TILE = 512
SMALL_ROWS = 24


def _cparams(sem=None):
    return pltpu.CompilerParams(dimension_semantics=sem, vmem_limit_bytes=VMEM_LIMIT)


def _sigmoid(v):
    return 1.0 / (1.0 + jnp.exp(-v))


def _softplus(v):
    y = jnp.exp(-jnp.abs(v))
    small = y * (1.0 - y * (0.5 - y * (1.0 / 3.0)))
    return jnp.maximum(v, 0.0) + jnp.where(y < 0.01, small, jnp.log(1.0 + y))


def _dot(a, b, dims, precision=None):
    return lax.dot_general(a, b, (dims, ((), ())), preferred_element_type=F32, precision=precision)


NN = ((1,), (0,))
NT = ((1,), (1,))
TN = ((0,), (0,))


def _matmul(a, b, *, ta=False, tb=False, tm, tn, tk, out_dtype=F32, name, mode=None, u=None, comm=None, shard_out=False):
    m, k = (a.shape[1], a.shape[0]) if ta else a.shape
    n = b.shape[0] if tb else b.shape[1]
    assert m % tm == 0 and n % tn == 0 and k % tk == 0, (name, m, n, k)
    nk = k // tk
    a_spec = pl.BlockSpec((tk, tm), lambda i, j, kk: (kk, i)) if ta else pl.BlockSpec((tm, tk), lambda i, j, kk: (i, kk))
    b_spec = pl.BlockSpec((tn, tk), lambda i, j, kk: (j, kk)) if tb else pl.BlockSpec((tk, tn), lambda i, j, kk: (kk, j))
    o_spec = pl.BlockSpec((tm, tn), lambda i, j, kk: (i, j))
    dims = ((0,) if ta else (1,), (1,) if tb else (0,))
    n_out = 2 if mode == "relu2" else 1

    def body(*refs):
        if mode == "drelu2":
            a_ref, b_ref, u_ref = refs[:3]
            rest = refs[3:]
        else:
            a_ref, b_ref = refs[:2]
            u_ref = None
            rest = refs[2:]
        outs = rest[:n_out]
        part = _dot(a_ref[...], b_ref[...], dims)

        def finish(r):
            if mode == "relu2":
                outs[0][...] = r.astype(BF16)
                rr = jnp.maximum(r, 0.0)
                outs[1][...] = (rr * rr).astype(BF16)
            elif mode == "drelu2":
                outs[0][...] = (r * (2.0 * jnp.maximum(u_ref[...].astype(F32), 0.0))).astype(out_dtype)
            else:
                outs[0][...] = r.astype(out_dtype)

        if nk == 1:
            finish(part)
        else:
            acc = rest[n_out]
            kk = pl.program_id(2)

            @pl.when(kk == 0)
            def _():
                acc[...] = part

            @pl.when(kk > 0)
            def _():
                acc[...] += part

            @pl.when(kk == nk - 1)
            def _():
                finish(acc[...])

    in_specs = [a_spec, b_spec]
    args = [a, b]
    if mode == "drelu2":
        in_specs.append(o_spec)
        args.append(u)
    if mode == "relu2":
        out_shape = [jax.ShapeDtypeStruct((m, n), BF16), jax.ShapeDtypeStruct((m, n), BF16)]
    elif shard_out:
        out_shape = [jax.ShapeDtypeStruct((n // tn, m, tn), out_dtype)]
        o_spec = pl.BlockSpec((None, tm, tn), lambda i, j, kk: (j, i, 0))
    else:
        out_shape = [jax.ShapeDtypeStruct((m, n), out_dtype)]
    outs, comm_outs = _pcall(
        body, args, name=name, grid=(m // tm, n // tn, nk), in_specs=in_specs, out_specs=[o_spec] * n_out,
        out_shape=out_shape, scratch_shapes=[pltpu.VMEM((tm, tn), F32)] if nk > 1 else [],
        sem=("parallel", "parallel", "arbitrary"), comm=comm)
    res = tuple(outs) if mode == "relu2" else outs[0]
    return res if comm is None else (res, comm_outs)


def _pieces_t_matmul(groups, b, *, tm, name):
    k, n = b.shape
    pieces = [p for g in groups for p in g]
    starts, tiles = [], 0
    for p in pieces:
        assert p.shape[0] == k and p.shape[1] % tm == 0, (name, p.shape)
        starts.append(tiles)
        tiles += p.shape[1] // tm
    group_of, group_start, group_tiles = [], [], []
    for gi, g in enumerate(groups):
        group_start.append(starts[len(group_of)])
        group_of += [gi] * len(g)
        group_tiles.append(sum(p.shape[1] // tm for p in g))

    def clipped(block, start, count):
        return pl.BlockSpec(block, (lambda i: (0, jnp.clip(i - start, 0, count - 1))) if block[0] == k
                            else (lambda i: (jnp.clip(i - start, 0, count - 1), 0)))

    def body(*refs):
        a_refs, b_ref, o_refs = refs[:len(pieces)], refs[len(pieces)], refs[len(pieces) + 1:]
        i = pl.program_id(0)
        for a_ref, start, p, gi in zip(a_refs, starts, pieces, group_of):
            @pl.when((i >= start) & (i < start + p.shape[1] // tm))
            def _(a_ref=a_ref, o_ref=o_refs[gi]):
                o_ref[...] = _dot(a_ref[...], b_ref[...], TN).astype(BF16)

    return pl.pallas_call(
        body, name=name, grid=(tiles,),
        in_specs=[clipped((k, tm), s0, p.shape[1] // tm) for s0, p in zip(starts, pieces)] + [pl.BlockSpec((k, n), lambda i: (0, 0))],
        out_specs=[clipped((tm, n), s0, cnt) for s0, cnt in zip(group_start, group_tiles)],
        out_shape=[jax.ShapeDtypeStruct((cnt * tm, n), BF16) for cnt in group_tiles],
        compiler_params=_cparams(("arbitrary",)))(*pieces, b)


def _rms_mod(xv, nw, scale, shift):
    r = lax.rsqrt(jnp.mean(xv * xv, axis=-1, keepdims=True) + EPS)
    return ((xv * r) * nw * (1.0 + scale) + shift).astype(BF16)


def _norm_mod_fwd(x, nw, scale, shift, name):
    s, d = x.shape
    row = pl.BlockSpec((ROW_TILE, d), lambda i: (i, 0))
    vec = pl.BlockSpec((1, d), lambda i: (0, 0))

    def body(x_ref, nw_ref, sc_ref, sh_ref, h_ref):
        h_ref[...] = _rms_mod(x_ref[...], nw_ref[...], sc_ref[...], sh_ref[...])

    return pl.pallas_call(body, name=name, grid=(s // ROW_TILE,), in_specs=[row, vec, vec, vec], out_specs=row,
                          out_shape=jax.ShapeDtypeStruct((s, d), BF16), compiler_params=_cparams(("parallel",)))(x, nw, scale, shift)


def _matmul_rows(a, b, epilogue, row_in, vec_in, outs, *, tb=False, tm, name, comm=None):
    pieces = a if isinstance(a, list) else [(a, 0)]
    assert not (tb and len(pieces) > 1)
    m = pieces[0][0].shape[0]
    n = b.shape[0] if tb else b.shape[1]
    assert m % tm == 0, (name, m, tm)
    dims = ((1,), (1,) if tb else (0,))
    n_a, n_row, n_vec = len(pieces), len(row_in), len(vec_in)

    def body(*refs):
        a_refs, b_ref, rest = refs[:n_a], refs[n_a], refs[n_a + 1:]
        if n_a == 1:
            c = _dot(a_refs[0][...], b_ref[...], dims)
        else:
            c = None
            for a_ref, (piece, off) in zip(a_refs, pieces):
                part = _dot(a_ref[...], b_ref[off:off + piece.shape[1], :], dims)
                c = part if c is None else c + part
        epilogue(c, pl.program_id(0) == 0, rest[:n_row], rest[n_row:n_row + n_vec], rest[n_row + n_vec:])

    def spec(kind, width):
        block = {"row": (tm, width), "vec": (1, width), "one": (1, 1)}[kind]
        return pl.BlockSpec(block, (lambda i: (i, 0)) if kind == "row" else (lambda i: (0, 0)))

    def shape(kind, width):
        return {"row": (m, width), "vec": (1, width), "one": (1, 1)}[kind]

    outs = [(o[0], o[1], o[2] if len(o) > 2 else n) for o in outs]
    res, comm_outs = _pcall(
        body, [*[p for p, _ in pieces], b, *row_in, *vec_in], name=name, grid=(m // tm,),
        in_specs=[spec("row", p.shape[1]) for p, _ in pieces] + [pl.BlockSpec(b.shape, lambda i: (0, 0))]
        + [spec("row", r.shape[1]) for r in row_in] + [spec("vec", v.shape[1]) for v in vec_in],
        out_specs=[spec(kind, width) for kind, _, width in outs],
        out_shape=[jax.ShapeDtypeStruct(shape(kind, width), dt) for kind, dt, width in outs],
        sem=("arbitrary",), comm=comm)
    return res if comm is None else (res, comm_outs)


def _residual_norm_epilogue(mix, first, rows, vecs, outs):
    (x_ref,), (gate_ref, nw_ref, sc_ref, sh_ref), (mix_ref, x1_ref, h_ref) = rows, vecs, outs
    xv = x_ref[...] + gate_ref[...] * mix
    mix_ref[...] = mix
    x1_ref[...] = xv
    h_ref[...] = _rms_mod(xv, nw_ref[...], sc_ref[...], sh_ref[...])


def _loss_epilogue(ff, first, rows, vecs, outs):
    (x1_ref, t_ref), (g_ref,), (loss_ref, dout_ref, dff_ref, dg_ref) = rows, vecs, outs
    d = ff.shape[1]

    @pl.when(first)
    def _():
        loss_ref[...] = jnp.zeros_like(loss_ref)
        dg_ref[...] = jnp.zeros_like(dg_ref)

    err = x1_ref[...] + g_ref[...] * ff - t_ref[...]
    loss_ref[...] += (0.5 / d) * jnp.sum(err * err).reshape(1, 1)
    dout = err * (1.0 / d)
    dout_ref[...] = dout
    dff_ref[...] = (g_ref[...] * dout).astype(BF16)
    dg_ref[...] += jnp.sum(dout * ff, axis=0, keepdims=True)


def _norm_bwd_epilogue(dh, first, rows, vecs, outs):
    with_gate = len(vecs) == 3
    x_ref, dres_ref = rows[:2]
    nw_ref, sc_ref = vecs[:2]
    dx_ref, dsh_ref, dsc_ref, dnw_ref = outs[:4]

    @pl.when(first)
    def _():
        for ref in outs[1:4] + outs[5:]:
            ref[...] = jnp.zeros_like(ref)

    xv = x_ref[...]
    r = lax.rsqrt(jnp.mean(xv * xv, axis=-1, keepdims=True) + EPS)
    nrm = xv * r
    one_sc = 1.0 + sc_ref[...]
    dhn = dh * nrm
    dsh_ref[...] += jnp.sum(dh, axis=0, keepdims=True)
    dsc_ref[...] += jnp.sum(dhn, axis=0, keepdims=True) * nw_ref[...]
    dnw_ref[...] += jnp.sum(dhn, axis=0, keepdims=True) * one_sc
    dn = dh * (nw_ref[...] * one_sc)
    dx = dres_ref[...] + r * (dn - nrm * jnp.mean(dn * nrm, axis=-1, keepdims=True))
    dx_ref[...] = dx
    if with_gate:
        outs[4][...] = (vecs[2][...] * dx).astype(BF16)
        outs[5][...] += jnp.sum(dx * rows[2][...], axis=0, keepdims=True)


CONV_COLS = 256
CONV_FWD_ROWS = 2048
CONV_BWD_ROWS = 1024
CONV_SUB_ROWS = 128
HALO = 8


def _shift_down(cur, halo, k):
    if k == 0:
        return cur
    rolled = pltpu.roll(cur, k, axis=0)
    top = jnp.where(lax.broadcasted_iota(jnp.int32, halo.shape, 0) < k, pltpu.roll(halo, k, axis=0), rolled[:HALO])
    return jnp.concatenate([top, rolled[HALO:]], axis=0)


def _shift_up(cur, halo, k):
    if k == 0:
        return cur
    t = cur.shape[0]
    rolled = pltpu.roll(cur, t - k, axis=0)
    bot = jnp.where(lax.broadcasted_iota(jnp.int32, halo.shape, 0) >= HALO - k, pltpu.roll(halo, HALO - k, axis=0),
                    rolled[t - HALO:])
    return jnp.concatenate([rolled[:t - HALO], bot], axis=0)


def _conv_fwd(proj, conv_w, conv_b):
    s = proj.shape[0]
    nr = s // CONV_FWD_ROWS
    cb0 = OFF_XBC // CONV_COLS
    hb = CONV_FWD_ROWS // HALO
    cur = pl.BlockSpec((CONV_FWD_ROWS, CONV_COLS), lambda j, r: (r, cb0 + j))
    prev = pl.BlockSpec((HALO, CONV_COLS), lambda j, r: (jnp.maximum(r * hb - 1, 0), cb0 + j))
    out = pl.BlockSpec((CONV_FWD_ROWS, CONV_COLS), lambda j, r: (r, j))

    def body(u_ref, up_ref, w_ref, b_ref, pre_ref, act_ref):
        r = pl.program_id(1)
        for c in range(CONV_FWD_ROWS // CONV_SUB_ROWS):
            rows = slice(c * CONV_SUB_ROWS, (c + 1) * CONV_SUB_ROWS)
            u = u_ref[rows, :]
            halo = u_ref[c * CONV_SUB_ROWS - HALO:c * CONV_SUB_ROWS, :] if c > 0 else jnp.where(r > 0, up_ref[...], 0.0)
            acc = b_ref[...] + w_ref[CONV_K - 1:CONV_K, :] * u
            for k in range(1, CONV_K):
                acc = acc + w_ref[CONV_K - 1 - k:CONV_K - k, :] * _shift_down(u, halo, k)
            pre_ref[rows, :] = acc
            act_ref[rows, :] = acc * _sigmoid(acc)

    return pl.pallas_call(
        body, name="conv_fwd", grid=(CONV_CH // CONV_COLS, nr),
        in_specs=[cur, prev, pl.BlockSpec((CONV_K, CONV_COLS), lambda j, r: (0, j)),
                  pl.BlockSpec((1, CONV_COLS), lambda j, r: (0, j))],
        out_specs=[out, out],
        out_shape=[jax.ShapeDtypeStruct((s, CONV_CH), F32), jax.ShapeDtypeStruct((s, CONV_CH), F32)],
        compiler_params=_cparams(("parallel", "arbitrary")))(proj, proj, conv_w, conv_b)


def _conv_bwd(dact, pre, proj, conv_w):
    s = proj.shape[0]
    nr = s // CONV_BWD_ROWS
    cb0 = OFF_XBC // CONV_COLS
    hb = CONV_BWD_ROWS // HALO
    last_halo = s // HALO - 1
    n_sub = CONV_BWD_ROWS // CONV_SUB_ROWS
    cur = pl.BlockSpec((CONV_BWD_ROWS, CONV_COLS), lambda j, r: (r, j))
    nxt = pl.BlockSpec((HALO, CONV_COLS), lambda j, r: (jnp.minimum((r + 1) * hb, last_halo), j))
    ucur = pl.BlockSpec((CONV_BWD_ROWS, CONV_COLS), lambda j, r: (r, cb0 + j))
    wspec = pl.BlockSpec((CONV_K, CONV_COLS), lambda j, r: (0, j))
    bspec = pl.BlockSpec((1, CONV_COLS), lambda j, r: (0, j))

    def dsilu(p):
        sg = _sigmoid(p)
        return sg * (1.0 + p * (1.0 - sg))

    def body(da_ref, dan_ref, pre_ref, pren_ref, u_ref, w_ref, du_ref, dw_ref, db_ref):
        r = pl.program_id(1)

        @pl.when(r == 0)
        def _():
            dw_ref[...] = jnp.zeros_like(dw_ref)
            db_ref[...] = jnp.zeros_like(db_ref)

        dws = [jnp.zeros((1, CONV_COLS), F32) for _ in range(CONV_K)]
        db = jnp.zeros((1, CONV_COLS), F32)
        for c in range(n_sub):
            rows = slice(c * CONV_SUB_ROWS, (c + 1) * CONV_SUB_ROWS)
            ahead = slice((c + 1) * CONV_SUB_ROWS, (c + 1) * CONV_SUB_ROWS + HALO)
            dpre = da_ref[rows, :] * dsilu(pre_ref[rows, :])
            if c < n_sub - 1:
                dnext = da_ref[ahead, :] * dsilu(pre_ref[ahead, :])
            else:
                dnext = jnp.where(r < nr - 1, dan_ref[...] * dsilu(pren_ref[...]), 0.0)
            u = u_ref[rows, :]
            du = w_ref[CONV_K - 1:CONV_K, :] * dpre
            dws[0] = dws[0] + jnp.sum(dpre * u, axis=0, keepdims=True)
            for k in range(1, CONV_K):
                ahead_k = _shift_up(dpre, dnext, k)
                du = du + w_ref[CONV_K - 1 - k:CONV_K - k, :] * ahead_k
                dws[k] = dws[k] + jnp.sum(ahead_k * u, axis=0, keepdims=True)
            du_ref[rows, :] = du.astype(BF16)
            db = db + jnp.sum(dpre, axis=0, keepdims=True)
        dw_ref[...] += jnp.concatenate(dws[::-1], axis=0)
        db_ref[...] += db

    return pl.pallas_call(
        body, name="conv_bwd", grid=(CONV_CH // CONV_COLS, nr),
        in_specs=[cur, nxt, cur, nxt, ucur, wspec],
        out_specs=[cur, wspec, bspec],
        out_shape=[jax.ShapeDtypeStruct((s, CONV_CH), BF16), jax.ShapeDtypeStruct((CONV_K, CONV_CH), F32),
                   jax.ShapeDtypeStruct((1, CONV_CH), F32)],
        compiler_params=_cparams(("parallel", "arbitrary")))(dact, dact, pre, pre, proj, conv_w)


def _ssd_common(dtr, dtb, alog):
    lane = lax.broadcasted_iota(jnp.int32, (1, LANE), 1)
    head_lane = lane < SSD_HEADS
    dt = jnp.where(head_lane, _softplus(dtr + dtb), 0.0)
    a = jnp.where(head_lane, -jnp.exp(alog), 0.0)
    row = lax.broadcasted_iota(jnp.int32, (SSD_CHUNK, SSD_CHUNK), 0)
    col = lax.broadcasted_iota(jnp.int32, (SSD_CHUNK, SSD_CHUNK), 1)
    tril = row >= col
    cs = _dot(tril.astype(F32), dt * a, NN, precision=HIGHEST)
    return dt, a, cs, cs.T, tril, lane


def _split_bf16(v, passes):
    terms, rest = [], v
    for _ in range(passes):
        t = rest.astype(BF16)
        terms.append(t)
        rest = rest - t.astype(F32)
    return terms


def _dot_split(v, m, dims, passes):
    terms = _split_bf16(v, passes)
    if passes == 1:
        return _dot(terms[0], m, dims)
    return _dot(jnp.concatenate(terms, axis=1), jnp.concatenate([m] * passes, axis=0 if dims == NN else 1), dims)


def _ssd_constants():
    heads = jnp.arange(LANE)[:, None]
    exp_mat = (heads == (jnp.arange(SSD_D_INNER)[None, :] // HEAD_DIM)).astype(BF16)
    ind4 = ((jnp.arange(SSD_HEADS * SSD_CHUNK)[:, None] // SSD_CHUNK) == jnp.arange(LANE)[None, :]).astype(BF16)
    return exp_mat, ind4


def _expand_heads(v):
    return jnp.repeat(v[:, :SSD_HEADS], HEAD_DIM, axis=1)


def _ssd_prep(dtr, dtb, alog, exp_mat):
    dt, a, cs, cst, tril, lane = _ssd_common(dtr, dtb, alog)
    return dt, a, cs, cst, tril, lane, _dot_split(dt, exp_mat, NN, 2), _dot_split(cs, exp_mat, NN, 3)


def _chunk_decay_rows(cs, g):
    parts = []
    for e in range(HEADS_PER_GROUP):
        h = g * HEADS_PER_GROUP + e
        parts.append(jnp.broadcast_to(jnp.exp(cs[SSD_CHUNK - 1:SSD_CHUNK, h:h + 1]), (HEAD_DIM, SSD_STATE)))
    return jnp.concatenate(parts, axis=0)


def _ssd_fwd(proj, act, dtb, alog, dsk, nw):
    s = proj.shape[0]
    nc = s // SSD_CHUNK
    bc_w = SSD_GROUPS * SSD_STATE
    exp_mat, _ = _ssd_constants()

    def body(z_ref, dtr_ref, xs_ref, b_ref, c_ref, dtb_ref, alog_ref, dskx_ref, nw_ref, exp_ref,
             ypre_ref, yssd_ref, hall_ref, h_scr):
        @pl.when(pl.program_id(0) == 0)
        def _():
            h_scr[...] = jnp.zeros_like(h_scr)

        dt, a, cs, cst, tril, lane, dtx, csx = _ssd_prep(dtr_ref[...], dtb_ref[...], alog_ref[...], exp_ref[...])
        cs_last_x = csx[SSD_CHUNK - 1:SSD_CHUNK, :]
        xs = xs_ref[...]
        xdt = xs * dtx
        xdtb = xdt.astype(BF16)
        xdec = (xdt * jnp.exp(cs_last_x - csx)).astype(BF16)
        ecsx = jnp.exp(csx)
        head_of_lane = lax.broadcasted_iota(jnp.int32, (1, GROUP_WIDTH), 1) // HEAD_DIM
        for g in range(SSD_GROUPS):
            gs = slice(g * GROUP_WIDTH, (g + 1) * GROUP_WIDTH)
            bg = b_ref[:, g * SSD_STATE:(g + 1) * SSD_STATE].astype(BF16)
            cg = c_ref[:, g * SSD_STATE:(g + 1) * SSD_STATE].astype(BF16)
            cb = _dot(cg, bg, NT)
            hprev = h_scr[gs, :]
            hall_ref[0, gs, :] = hprev
            gms, rhs = [], []
            xg = xdtb[:, gs]
            for e in range(HEADS_PER_GROUP):
                h = g * HEADS_PER_GROUP + e
                lm = jnp.exp(jnp.where(tril, cs[:, h:h + 1] - cst[h:h + 1, :], -1e30))
                gms.append((cb * lm).astype(BF16))
                rhs.append(jnp.where(head_of_lane == e, xg, jnp.zeros_like(xg)))
            y = _dot(jnp.concatenate(gms, axis=1), jnp.concatenate(rhs, axis=0), NN)
            y = y + ecsx[:, gs] * _dot(cg, hprev.astype(BF16), NT)
            y = y + dskx_ref[:, gs] * xs[:, gs]
            h_scr[gs, :] = hprev * _chunk_decay_rows(cs, g) + _dot(xdec[:, gs], bg, TN)
            ypre_ref[:, gs] = y
            z = z_ref[:, gs]
            yg = y * (z * _sigmoid(z))
            r = lax.rsqrt(jnp.mean(yg * yg, axis=-1, keepdims=True) + EPS)
            yssd_ref[:, gs] = (yg * r * nw_ref[:, gs]).astype(BF16)

    row_d = lambda cb: pl.BlockSpec((SSD_CHUNK, SSD_D_INNER), lambda c: (c, cb))
    small = pl.BlockSpec((1, LANE), lambda c: (0, 0))
    wide = pl.BlockSpec((1, SSD_D_INNER), lambda c: (0, 0))
    return pl.pallas_call(
        body, name="ssd_fwd", grid=(nc,),
        in_specs=[row_d(OFF_Z // SSD_D_INNER),
                  pl.BlockSpec((SSD_CHUNK, LANE), lambda c: (c, OFF_DT // LANE)),
                  row_d(0),
                  pl.BlockSpec((SSD_CHUNK, bc_w), lambda c: (c, SSD_D_INNER // bc_w)),
                  pl.BlockSpec((SSD_CHUNK, bc_w), lambda c: (c, SSD_D_INNER // bc_w + 1)),
                  small, small, wide, wide, pl.BlockSpec((LANE, SSD_D_INNER), lambda c: (0, 0))],
        out_specs=[row_d(0), row_d(0), pl.BlockSpec((1, SSD_D_INNER, SSD_STATE), lambda c: (c, 0, 0))],
        out_shape=[jax.ShapeDtypeStruct((s, SSD_D_INNER), F32), jax.ShapeDtypeStruct((s, SSD_D_INNER + ATT_D), BF16),
                   jax.ShapeDtypeStruct((nc, SSD_D_INNER, SSD_STATE), F32)],
        scratch_shapes=[pltpu.VMEM((SSD_D_INNER, SSD_STATE), F32)],
        compiler_params=_cparams(("arbitrary",)))(proj, proj, act, act, act, dtb, alog, _expand_heads(dsk), nw, exp_mat)


def _ssd_bwd(dycat, ypre, proj, act, hall, dtb, alog, dsk, nw, comm=None):
    s = proj.shape[0]
    nc = s // SSD_CHUNK
    bc_w = SSD_GROUPS * SSD_STATE

    exp_mat, ind4 = _ssd_constants()
    seg_passes = 1

    def body(dy_ref, ypre_ref, z_ref, dtr_ref, xs_ref, b_ref, c_ref, hall_ref, dtb_ref, alog_ref, dskx_ref, nw_ref,
             exp_ref, ind4_ref, dz_ref, dact_ref, ddtr_ref, da_ref, ddsk_ref, ddtb_ref, dnw_ref, dh_scr):
        @pl.when(pl.program_id(0) == 0)
        def _():
            dh_scr[...] = jnp.zeros_like(dh_scr)
            da_ref[...] = jnp.zeros_like(da_ref)
            ddsk_ref[...] = jnp.zeros_like(ddsk_ref)
            ddtb_ref[...] = jnp.zeros_like(ddtb_ref)
            dnw_ref[...] = jnp.zeros_like(dnw_ref)

        dtr = dtr_ref[...]
        dt, a, cs, cst, tril, lane, dtx, csx = _ssd_prep(dtr, dtb_ref[...], alog_ref[...], exp_ref[...])
        cs_last_x = csx[SSD_CHUNK - 1:SSD_CHUNK, :]
        xs = xs_ref[...]
        xdt = xs * dtx
        xdtb = xdt.astype(BF16)
        decx = jnp.exp(cs_last_x - csx)
        xdecf = xdt * decx
        xdec = xdecf.astype(BF16)
        ecsx = jnp.exp(csx)
        head_of_lane = lax.broadcasted_iota(jnp.int32, (1, GROUP_WIDTH), 1) // HEAD_DIM
        last_row = lax.broadcasted_iota(jnp.int32, (SSD_CHUNK, 1), 0) == SSD_CHUNK - 1
        dcs_col = jnp.zeros((SSD_CHUNK, LANE), F32)
        dcs_row = jnp.zeros((SSD_CHUNK, LANE), F32)
        ddt = jnp.zeros((SSD_CHUNK, LANE), F32)
        ddsk = jnp.zeros((1, LANE), F32)
        hsum = jnp.zeros((1, LANE), F32)
        t1_sum = jnp.zeros((1, LANE), F32)
        for g in range(SSD_GROUPS):
            gs = slice(g * GROUP_WIDTH, (g + 1) * GROUP_WIDTH)
            bsl = slice(g * SSD_STATE, (g + 1) * SSD_STATE)
            exp_g = exp_ref[:, gs]
            ind4_g = ind4_ref[g * HEADS_PER_GROUP * SSD_CHUNK:(g + 1) * HEADS_PER_GROUP * SSD_CHUNK, :]
            z = z_ref[:, gs]
            sg = _sigmoid(z)
            sz = z * sg
            ypre = ypre_ref[:, gs]
            yg = ypre * sz
            r = lax.rsqrt(jnp.mean(yg * yg, axis=-1, keepdims=True) + EPS)
            nrm = yg * r
            dyo_n = dy_ref[:, gs]
            dnw_ref[:, gs] += jnp.sum(dyo_n * nrm, axis=0, keepdims=True)
            dn = dyo_n * nw_ref[:, gs]
            dyg = r * (dn - nrm * jnp.mean(dn * nrm, axis=-1, keepdims=True))
            dz_ref[:, gs] = (dyg * ypre * (sg * (1.0 + z * (1.0 - sg)))).astype(BF16)
            dy = dyg * sz

            bg = b_ref[:, bsl].astype(BF16)
            cg = c_ref[:, bsl].astype(BF16)
            cb = _dot(cg, bg, NT)
            hprev = hall_ref[0, gs, :]
            hb = hprev.astype(BF16)
            dhn = dh_scr[gs, :]
            dhb = dhn.astype(BF16)
            xs_g, xdt_g = xs[:, gs], xdtb[:, gs]
            w_off = _dot(cg, hb, NT)
            dyo = dy * ecsx[:, gs]
            dyob = dyo.astype(BF16)
            dcg = _dot(dyob, hb, NN)
            dh_y = _dot(dyob, cg, TN)
            r_st = _dot(bg, dhb, NT)
            dbg = _dot(xdec[:, gs], dhb, NN)
            dyb = dy.astype(BF16)
            gms, gmbs, lms, dys = [], [], [], []
            for e in range(HEADS_PER_GROUP):
                h = g * HEADS_PER_GROUP + e
                lm = jnp.exp(jnp.where(tril, cs[:, h:h + 1] - cst[h:h + 1, :], -1e30))
                gm = cb * lm
                lms.append(lm)
                gms.append(gm)
                gmbs.append(gm.astype(BF16))
                dys.append(jnp.where(head_of_lane == e, dyb, jnp.zeros_like(dyb)))
            dxdt = _dot(jnp.concatenate(gmbs, axis=0), jnp.concatenate(dys, axis=0), TN) + decx[:, gs] * r_st
            dcb = jnp.zeros((SSD_CHUNK, SSD_CHUNK), F32)
            mms = []
            for e in range(HEADS_PER_GROUP):
                dg = _dot(dys[e], xdt_g, NT)
                mms.append(dg * gms[e])
                dcb = dcb + dg * lms[e]
            seg = _dot_split(jnp.concatenate([dyo * w_off, xdecf[:, gs] * r_st, dxdt * xs_g, dy * xs_g], axis=0), exp_g, NT, seg_passes)
            v1, t1, ddt_g, dsk_g = [seg[i * SSD_CHUNK:(i + 1) * SSD_CHUNK] for i in range(4)]
            dcs_col = dcs_col + v1 - t1 + _dot_split(jnp.concatenate(mms, axis=1), ind4_g, NN, seg_passes)
            for t in _split_bf16(jnp.concatenate(mms, axis=0), seg_passes):
                dcs_row = dcs_row + _dot(ind4_g, t, TN)
            ddt = ddt + ddt_g
            ddsk = ddsk + jnp.sum(dsk_g, axis=0, keepdims=True)
            t1_sum = t1_sum + jnp.sum(t1, axis=0, keepdims=True)
            for e in range(HEADS_PER_GROUP):
                h = g * HEADS_PER_GROUP + e
                hs = slice(e * HEAD_DIM, (e + 1) * HEAD_DIM)
                hsum = hsum + jnp.where(lane == h, jnp.sum(dhn[hs, :] * hprev[hs, :]).reshape(1, 1), 0.0)
            dh_scr[gs, :] = dhn * _chunk_decay_rows(cs, g) + dh_y
            dcbb = dcb.astype(BF16)
            dact_ref[:, gs] = dxdt * dtx[:, gs] + dskx_ref[:, gs] * dy
            dact_ref[:, SSD_D_INNER + g * SSD_STATE:SSD_D_INNER + (g + 1) * SSD_STATE] = dbg + _dot(dcbb, cg, TN)
            dact_ref[:, SSD_D_INNER + bc_w + g * SSD_STATE:SSD_D_INNER + bc_w + (g + 1) * SSD_STATE] = dcg + _dot(dcbb, bg, NN)
        dlast = t1_sum + jnp.exp(cs[SSD_CHUNK - 1:SSD_CHUNK, :]) * hsum
        dcs = dcs_col - dcs_row.T + jnp.where(last_row, dlast, 0.0)
        row = lax.broadcasted_iota(jnp.int32, (SSD_CHUNK, SSD_CHUNK), 0)
        col = lax.broadcasted_iota(jnp.int32, (SSD_CHUNK, SSD_CHUNK), 1)
        dda = _dot((col >= row).astype(F32), dcs, NN, precision=HIGHEST)
        ddt = ddt + dda * a
        da_ref[...] += jnp.sum(dda * dt, axis=0, keepdims=True)
        ddtr = jnp.where(lane < SSD_HEADS, ddt * _sigmoid(dtr + dtb_ref[...]), 0.0)
        ddtr_ref[...] = ddtr.astype(BF16)
        ddtb_ref[...] += jnp.sum(ddtr, axis=0, keepdims=True)
        ddsk_ref[...] += ddsk

    rev = lambda c: nc - 1 - c
    row_d = lambda cb: pl.BlockSpec((SSD_CHUNK, SSD_D_INNER), lambda c: (rev(c), cb))
    small = pl.BlockSpec((1, LANE), lambda c: (0, 0))
    wide = pl.BlockSpec((1, SSD_D_INNER), lambda c: (0, 0))
    small_shape = jax.ShapeDtypeStruct((1, LANE), F32)
    return _pcall(
        body, (dycat, ypre, proj, proj, act, act, act, hall, dtb, alog, _expand_heads(dsk), nw, exp_mat, ind4),
        name="ssd_bwd", grid=(nc,),
        in_specs=[row_d(0), row_d(0), row_d(OFF_Z // SSD_D_INNER),
                  pl.BlockSpec((SSD_CHUNK, LANE), lambda c: (rev(c), OFF_DT // LANE)),
                  row_d(0),
                  pl.BlockSpec((SSD_CHUNK, bc_w), lambda c: (rev(c), SSD_D_INNER // bc_w)),
                  pl.BlockSpec((SSD_CHUNK, bc_w), lambda c: (rev(c), SSD_D_INNER // bc_w + 1)),
                  pl.BlockSpec((1, SSD_D_INNER, SSD_STATE), lambda c: (rev(c), 0, 0)),
                  small, small, wide, wide, pl.BlockSpec((LANE, SSD_D_INNER), lambda c: (0, 0)),
                  pl.BlockSpec((SSD_HEADS * SSD_CHUNK, LANE), lambda c: (0, 0))],
        out_specs=[row_d(0), pl.BlockSpec((SSD_CHUNK, CONV_CH), lambda c: (rev(c), 0)),
                   pl.BlockSpec((SSD_CHUNK, LANE), lambda c: (rev(c), 0)), small, small, small, wide],
        out_shape=[jax.ShapeDtypeStruct((s, SSD_D_INNER), BF16), jax.ShapeDtypeStruct((s, CONV_CH), F32),
                   jax.ShapeDtypeStruct((s, LANE), BF16), small_shape, small_shape, small_shape,
                   jax.ShapeDtypeStruct((1, SSD_D_INNER), F32)],
        scratch_shapes=[pltpu.VMEM((SSD_D_INNER, SSD_STATE), F32)], sem=("arbitrary",), comm=comm)


def _head_mean_matrix():
    row = lax.broadcasted_iota(jnp.int32, (LANE, LANE), 0) // HEAD_DIM
    col = lax.broadcasted_iota(jnp.int32, (LANE, LANE), 1) // HEAD_DIM
    return (row == col).astype(F32)


def _head_sum2(v, ones_bd):
    hi = v.astype(BF16)
    lo = (v - hi.astype(F32)).astype(BF16)
    return _dot(jnp.concatenate([hi, lo], axis=1), jnp.concatenate([ones_bd, ones_bd], axis=0), NN)


def _head_norms(xs, ws, ones_bd):
    sums = [_head_sum2(x * x, ones_bd) for x in xs]
    rs = [lax.rsqrt(ms * (1.0 / HEAD_DIM) + EPS) for ms in sums]
    return [(x * r) * w for x, r, w in zip(xs, rs, ws)], rs


def _head_norms_bwd(dns, xs, ws, rs, ones_bd):
    nrms = [x * r for x, r in zip(xs, rs)]
    dnws = [dn * w for dn, w in zip(dns, ws)]
    projs = [_head_sum2(dnw * nrm, ones_bd) for dnw, nrm in zip(dnws, nrms)]
    dxs = [r * (dnw - nrm * (pr * (1.0 / HEAD_DIM))) for r, dnw, nrm, pr in zip(rs, dnws, nrms, projs)]
    return dxs, [jnp.sum(dn * nrm, axis=0, keepdims=True) for dn, nrm in zip(dns, nrms)]


NORM_CHUNKS = 2


PRO_ROWS = 256
ATT_GROUP_FWD = 32
ATT_GROUP_BWD = 8
KEYS = 2 * ATT_BLK
NEG = -1e30
HALF = HEAD_DIM // 2


def _rows(start, size, dil):
    return pl.ds(start, size) if dil == 1 else pl.ds(start, size, stride=dil)


def _fill_bias(bias_ref):
    row = lax.broadcasted_iota(jnp.int32, (ATT_BLK, 2 * KEYS), 0)
    col = lax.broadcasted_iota(jnp.int32, (ATT_BLK, 2 * KEYS), 1) & (KEYS - 1)
    for first, off in ((0, 0), (1, ATT_BLK)):
        dist = off + row - col
        bias_ref[first] = jnp.where((dist >= 0) & (dist <= ATT_BLK), 0.0, NEG)


def _pair(a, b):
    return jnp.concatenate([jnp.broadcast_to(a, (ATT_BLK, KEYS)), jnp.broadcast_to(b, (ATT_BLK, KEYS))], axis=1)


def _split_heads(x, is_a):
    zero = jnp.zeros_like(x)
    return jnp.concatenate([jnp.where(is_a, x, zero), jnp.where(is_a, zero, x)], axis=0)


def _block_ids(b, nb):
    i = b & (nb - 1)
    q0 = pl.multiple_of(b * ATT_BLK, ATT_BLK)
    k0 = pl.multiple_of((b - jnp.minimum(i, 1)) * ATT_BLK, ATT_BLK)
    return pl.ds(q0, ATT_BLK), pl.ds(k0, KEYS), jnp.minimum(i, 1)


def _natural_rows(b, nb, dil):
    if dil == 1:
        return pl.ds(pl.multiple_of(b * ATT_BLK, ATT_BLK), ATT_BLK)
    return pl.ds(b // nb + dil * ((b & (nb - 1)) * ATT_BLK), ATT_BLK, stride=dil)


def _att_fwd(proj, qw, kw, comm=None):
    s = proj.shape[0]
    nblk = s // ATT_BLK
    assert all((s // d) // ATT_BLK >= 2 for d in DILATIONS)
    blk = lambda off: pl.BlockSpec((s, LANE), lambda i: (0, off // LANE + i))
    wspec = pl.BlockSpec((1, LANE), lambda i: (0, i))
    oblk = pl.BlockSpec((s, LANE), lambda i: (0, i))

    def body(q_ref, k_ref, v_ref, qw_ref, kw_ref, o_ref, lse_ref, qn, kn, q_cm, k_cm, v_cm, m_acc, l_acc, o_d, m_d, l_d,
             o_e, m_e, l_e, tq, tk, tv, bias):
        ones_bd = _head_mean_matrix().astype(BF16)
        is_a = lax.broadcasted_iota(jnp.int32, (1, LANE), 1) < HEAD_DIM
        ones_ext = _split_heads(jnp.ones((KEYS, LANE), BF16), is_a)
        _fill_bias(bias)

        def pro(j, c):
            chunks = [pl.ds(pl.multiple_of((NORM_CHUNKS * j + u) * PRO_ROWS, PRO_ROWS), PRO_ROWS) for u in range(NORM_CHUNKS)]
            normed, _ = _head_norms([q_ref[rows, :] for rows in chunks] + [k_ref[rows, :] for rows in chunks],
                                    [qw_ref[...] * HEAD_DIM ** -0.5] * NORM_CHUNKS + [kw_ref[...]] * NORM_CHUNKS, ones_bd)
            for u, rows in enumerate(chunks):
                qn[rows, :] = normed[u]
                kn[rows, :] = normed[NORM_CHUNKS + u]
            return c

        lax.fori_loop(0, s // (NORM_CHUNKS * PRO_ROWS), pro, 0)

        results = dict(zip(DILATIONS, ((o_ref, m_acc, l_acc), (o_d, m_d, l_d), (o_e, m_e, l_e))))
        for dil in DILATIONS:
            ln = s // dil
            nb = ln // ATT_BLK
            o_out, m_out, l_out = results[dil]
            level = DILATIONS.index(dil)
            keep_f32 = 0 < level < len(DILATIONS) - 1
            from_temps = level >= 2
            step_rows = dil // DILATIONS[level - 1] if from_temps else dil
            for r in range(dil):
                prev = DILATIONS[level - 1] if from_temps else 1
                start = (r % prev) * (s // prev) + r // prev if from_temps else r

                def relayout(j, c, r=r, ln=ln, start=start, step_rows=step_rows, keep_f32=keep_f32, from_temps=from_temps):
                    j0 = pl.multiple_of(j * PRO_ROWS, PRO_ROWS)
                    src = _rows(start + step_rows * j0, PRO_ROWS, step_rows)
                    dst = pl.ds(r * ln + j0, PRO_ROWS)
                    qv, kv, vv = (tq[src, :], tk[src, :], tv[src, :]) if from_temps else (qn[src, :], kn[src, :], v_ref[src, :])
                    q_cm[dst, :] = qv.astype(BF16)
                    k_cm[dst, :] = kv.astype(BF16)
                    v_cm[dst, :] = vv.astype(BF16)
                    if keep_f32:
                        tq[dst, :] = qv
                        tk[dst, :] = kv
                        tv[dst, :] = vv
                    return c

                lax.fori_loop(0, ln // PRO_ROWS, relayout, 0)

            def step(bg, c, nb=nb, o_out=o_out, m_out=m_out, l_out=l_out):
                ids = [_block_ids(bg * ATT_GROUP_FWD + u, nb) for u in range(ATT_GROUP_FWD)]
                kbs = [_split_heads(k_cm[krows, :], is_a) for _, krows, _ in ids]
                scs = [_dot(q_cm[qrows, :], kb, NT) + bias[first] for (qrows, _, first), kb in zip(ids, kbs)]
                mas = [jnp.max(sc[:, :KEYS], axis=-1, keepdims=True) for sc in scs]
                mbs = [jnp.max(sc[:, KEYS:], axis=-1, keepdims=True) for sc in scs]
                ps = [jnp.exp(sc - _pair(ma, mb)).astype(BF16) for sc, ma, mb in zip(scs, mas, mbs)]
                vbs = [jnp.concatenate([_split_heads(v_cm[krows, :], is_a), ones_ext], axis=1) for _, krows, _ in ids]
                ols = [_dot(p, vb, NN) for p, vb in zip(ps, vbs)]
                for (qrows, _, _), ol, ma, mb in zip(ids, ols, mas, mbs):
                    o_out[qrows, :] = ol[:, :LANE]
                    l_out[qrows, :] = ol[:, LANE:]
                    m_out[qrows, :] = jnp.where(is_a, ma, mb)
                return c

            lax.fori_loop(0, nblk // ATT_GROUP_FWD, step, 0)

        for level in range(len(DILATIONS) - 1, 0, -1):
            fine_d, coarse_d = DILATIONS[level - 1], DILATIONS[level]
            ratio, ln_f, ln_c = coarse_d // fine_d, s // fine_d, s // coarse_d
            (o_f, m_f, l_f), (o_c, m_c, l_c) = results[fine_d], results[coarse_d]
            for r in range(coarse_d):
                def merge(j, c, r=r, ratio=ratio, ln_c=ln_c, start=(r % fine_d) * ln_f + r // fine_d,
                          o_f=o_f, m_f=m_f, l_f=l_f, o_c=o_c, m_c=m_c, l_c=l_c):
                    j0 = pl.multiple_of(j * PRO_ROWS, PRO_ROWS)
                    fine = _rows(start + ratio * j0, PRO_ROWS, ratio)
                    coarse = pl.ds(r * ln_c + j0, PRO_ROWS)
                    m_old, m_new = m_f[fine, :], m_c[coarse, :]
                    m = jnp.maximum(m_old, m_new)
                    a_old, a_new = jnp.exp(m_old - m), jnp.exp(m_new - m)
                    o_f[fine, :] = a_old * o_f[fine, :] + a_new * o_c[coarse, :]
                    l_f[fine, :] = a_old * l_f[fine, :] + a_new * l_c[coarse, :]
                    m_f[fine, :] = m
                    return c

                lax.fori_loop(0, ln_c // PRO_ROWS, merge, 0)

        def epi(j, c):
            rows = pl.ds(pl.multiple_of(j * PRO_ROWS, PRO_ROWS), PRO_ROWS)
            l = l_acc[rows, :]
            o_ref[rows, :] = o_ref[rows, :] / l
            lse_ref[rows, :] = m_acc[rows, :] + jnp.log(l)
            return c

        lax.fori_loop(0, s // PRO_ROWS, epi, 0)

    f = jax.ShapeDtypeStruct((s, ATT_D), F32)
    scr = pltpu.VMEM((s, LANE), F32)
    scb = pltpu.VMEM((s, LANE), BF16)
    return _pcall(
        body, (proj, proj, proj, qw, kw), name="att_fwd", grid=(ATT_D // LANE,),
        in_specs=[blk(OFF_Q), blk(OFF_K), blk(OFF_V), wspec, wspec], out_specs=[oblk, oblk], out_shape=[f, f],
        scratch_shapes=[scr, scr, scb, scb, scb] + [scr] * 11 + [pltpu.VMEM((2, ATT_BLK, 2 * KEYS), F32)],
        sem=("parallel",), comm=comm)


def _att_bwd(proj, do, stats, qw, kw, comm=None):
    s = proj.shape[0]
    nblk = s // ATT_BLK
    blk = lambda off: pl.BlockSpec((s, LANE), lambda i: (0, off // LANE + i))
    wspec = pl.BlockSpec((1, LANE), lambda i: (0, i))
    oblk = pl.BlockSpec((s, LANE), lambda i: (0, i))

    def body(q_ref, k_ref, v_ref, do_ref, st_ref, qw_ref, kw_ref, dq_ref, dk_ref, dv_ref, dqw_ref, dkw_ref,
             qn, kn, q_cm, do_cm, k_cm, v_cm, rms, dq_acc, dk_acc, dv_acc, dq_d, dk_d, dv_d, dq_e, dk_e, dv_e, bias):
        ones_bd = _head_mean_matrix().astype(BF16)
        is_a = lax.broadcasted_iota(jnp.int32, (1, LANE), 1) < HEAD_DIM
        first_half = (lax.broadcasted_iota(jnp.int32, (1, LANE), 1) & (HEAD_DIM - 1)) < HALF
        _fill_bias(bias)
        zero = jnp.zeros((PRO_ROWS, LANE), F32)
        results = dict(zip(DILATIONS, ((dq_acc, dk_acc, dv_acc), (dq_d, dk_d, dv_d), (dq_e, dk_e, dv_e))))

        def pro(j, c):
            chunks = [pl.ds(pl.multiple_of((NORM_CHUNKS * j + u) * PRO_ROWS, PRO_ROWS), PRO_ROWS) for u in range(NORM_CHUNKS)]
            normed, rs = _head_norms([q_ref[rows, :] for rows in chunks] + [k_ref[rows, :] for rows in chunks],
                                     [qw_ref[...] * HEAD_DIM ** -0.5] * NORM_CHUNKS + [kw_ref[...]] * NORM_CHUNKS, ones_bd)
            for u, rows in enumerate(chunks):
                qn[rows, :] = normed[u]
                kn[rows, :] = normed[NORM_CHUNKS + u]
                rms[rows, :] = jnp.where(first_half, rs[u], rs[NORM_CHUNKS + u])
                dk_acc[rows, :] = zero
                dv_acc[rows, :] = zero
            return c

        lax.fori_loop(0, s // (NORM_CHUNKS * PRO_ROWS), pro, 0)

        for dil in DILATIONS:
            ln = s // dil
            nb = ln // ATT_BLK
            dq_o, dk_o, dv_o = results[dil]
            for r in range(dil):
                def relayout(j, c, dil=dil, r=r, ln=ln, dk_o=dk_o, dv_o=dv_o):
                    j0 = pl.multiple_of(j * PRO_ROWS, PRO_ROWS)
                    src = _rows(r + dil * j0, PRO_ROWS, dil)
                    dst = pl.ds(r * ln + j0, PRO_ROWS)
                    q_cm[dst, :] = qn[src, :].astype(BF16)
                    k_cm[dst, :] = kn[src, :].astype(BF16)
                    v_cm[dst, :] = v_ref[src, :].astype(BF16)
                    do_cm[dst, :] = do_ref[src, :].astype(BF16)
                    if dil > 1:
                        dk_o[dst, :] = zero
                        dv_o[dst, :] = zero
                    return c

                lax.fori_loop(0, ln // PRO_ROWS, relayout, 0)

            def step(bg, c, nb=nb, dil=dil, dq_o=dq_o, dk_o=dk_o, dv_o=dv_o):
                blocks = [bg * ATT_GROUP_BWD + u for u in range(ATT_GROUP_BWD)]
                ids = [_block_ids(b, nb) for b in blocks]
                qbs = [q_cm[qrows, :] for qrows, _, _ in ids]
                dobs = [do_cm[qrows, :] for qrows, _, _ in ids]
                kbs = [_split_heads(k_cm[krows, :], is_a) for _, krows, _ in ids]
                vbs = [_split_heads(v_cm[krows, :], is_a) for _, krows, _ in ids]
                sts = [st_ref[_natural_rows(b, nb, dil), :] for b in blocks]
                scs = [_dot(qb, kb, NT) + bias[first] for qb, kb, (_, _, first) in zip(qbs, kbs, ids)]
                dps = [_dot(dob, vb, NT) for dob, vb in zip(dobs, vbs)]
                ps = [jnp.exp(sc - _pair(st[:, 0:1], st[:, HEAD_DIM:HEAD_DIM + 1])) for sc, st in zip(scs, sts)]
                dss = [(p * (dp - _pair(st[:, HALF:HALF + 1], st[:, HEAD_DIM + HALF:HEAD_DIM + HALF + 1]))).astype(BF16)
                       for p, dp, st in zip(ps, dps, sts)]
                dqs = [_dot(ds, kb, NN) for ds, kb in zip(dss, kbs)]
                dkfs = [_dot(ds, qb, TN) for ds, qb in zip(dss, qbs)]
                dvfs = [_dot(p.astype(BF16), dob, TN) for p, dob in zip(ps, dobs)]
                for (qrows, krows, _), dq, dkf, dvf in zip(ids, dqs, dkfs, dvfs):
                    dq_o[qrows, :] = dq
                    dk_o[krows, :] += jnp.where(is_a, dkf[:KEYS], dkf[KEYS:])
                    dv_o[krows, :] += jnp.where(is_a, dvf[:KEYS], dvf[KEYS:])
                return c

            lax.fori_loop(0, nblk // ATT_GROUP_BWD, step, 0)

        for level in range(len(DILATIONS) - 1, 0, -1):
            fine_d, coarse_d = DILATIONS[level - 1], DILATIONS[level]
            ratio, ln_f, ln_c = coarse_d // fine_d, s // fine_d, s // coarse_d
            for r in range(coarse_d):
                def merge(j, c, r=r, ratio=ratio, ln_c=ln_c, start=(r % fine_d) * ln_f + r // fine_d,
                          fine_bufs=results[fine_d], coarse_bufs=results[coarse_d]):
                    j0 = pl.multiple_of(j * PRO_ROWS, PRO_ROWS)
                    fine = _rows(start + ratio * j0, PRO_ROWS, ratio)
                    coarse = pl.ds(r * ln_c + j0, PRO_ROWS)
                    for f_buf, c_buf in zip(fine_bufs, coarse_bufs):
                        f_buf[fine, :] += c_buf[coarse, :]
                    return c

                lax.fori_loop(0, ln_c // PRO_ROWS, merge, 0)

        def epi(j, c):
            chunks = [pl.ds(pl.multiple_of((NORM_CHUNKS * j + u) * PRO_ROWS, PRO_ROWS), PRO_ROWS) for u in range(NORM_CHUNKS)]
            packed = [rms[rows, :] for rows in chunks]
            rs = ([jnp.where(first_half, p, pltpu.roll(p, HALF, axis=1)) for p in packed]
                  + [jnp.where(first_half, pltpu.roll(p, LANE - HALF, axis=1), p) for p in packed])
            dxs, dws = _head_norms_bwd(
                [dq_acc[rows, :] for rows in chunks] + [dk_acc[rows, :] for rows in chunks],
                [q_ref[rows, :] for rows in chunks] + [k_ref[rows, :] for rows in chunks],
                [qw_ref[...] * HEAD_DIM ** -0.5] * NORM_CHUNKS + [kw_ref[...]] * NORM_CHUNKS, rs, ones_bd)
            dqw, dkw = c
            for u, rows in enumerate(chunks):
                dq_ref[rows, :] = dxs[u].astype(BF16)
                dk_ref[rows, :] = dxs[NORM_CHUNKS + u].astype(BF16)
                dv_ref[rows, :] = dv_acc[rows, :].astype(BF16)
                dqw, dkw = dqw + dws[u], dkw + dws[NORM_CHUNKS + u]
            return dqw, dkw

        zrow = jnp.zeros((1, LANE), F32)
        dqw, dkw = lax.fori_loop(0, s // (NORM_CHUNKS * PRO_ROWS), epi, (zrow, zrow))
        dqw_ref[...] = dqw * HEAD_DIM ** -0.5
        dkw_ref[...] = dkw

    o = jax.ShapeDtypeStruct((s, ATT_D), BF16)
    ov = jax.ShapeDtypeStruct((1, ATT_D), F32)
    scr = pltpu.VMEM((s, LANE), F32)
    scb = pltpu.VMEM((s, LANE), BF16)
    return _pcall(
        body, (proj, proj, proj, do, stats, qw, kw), name="att_bwd", grid=(ATT_D // LANE,),
        in_specs=[blk(OFF_Q), blk(OFF_K), blk(OFF_V), oblk, oblk, wspec, wspec],
        out_specs=[oblk, oblk, oblk, wspec, wspec], out_shape=[o, o, o, ov, ov],
        scratch_shapes=[scr, scr, scb, scb, scb, scb] + [scr] * 10 + [pltpu.VMEM((2, ATT_BLK, 2 * KEYS), F32)],
        sem=("parallel",), comm=comm)


def _att_norm_fwd(o, nw, ycat):
    s = o.shape[0]
    row = pl.BlockSpec((ROW_TILE, ATT_D), lambda i: (i, 0))
    vec = pl.BlockSpec((1, ATT_D), lambda i: (0, 0))

    def body(o_ref, nw_ref, ycat_ref, y_ref):
        o = o_ref[...]
        r = lax.rsqrt(jnp.mean(o * o, axis=-1, keepdims=True) + EPS)
        y_ref[...] = (o * r * nw_ref[...]).astype(BF16)

    return pl.pallas_call(body, name="att_norm_fwd", grid=(s // ROW_TILE,),
                          in_specs=[row, vec, pl.BlockSpec(memory_space=pl.ANY)],
                          out_specs=pl.BlockSpec((ROW_TILE, ATT_D), lambda i: (i, 1)),
                          out_shape=jax.ShapeDtypeStruct(ycat.shape, BF16), input_output_aliases={2: 0},
                          compiler_params=_cparams(("parallel",)))(o, nw, ycat)


def _mixer_split_epilogue(dycat, first, rows, vecs, outs):
    (o_ref, lse_ref), (nw_ref,), (dyssd_ref, do_ref, st_ref, dnw_ref) = rows, vecs, outs

    @pl.when(first)
    def _():
        dnw_ref[...] = jnp.zeros_like(dnw_ref)

    dyssd_ref[...] = dycat[:, :SSD_D_INNER]
    dy = dycat[:, SSD_D_INNER:]
    o = o_ref[...]
    r = lax.rsqrt(jnp.mean(o * o, axis=-1, keepdims=True) + EPS)
    nrm = o * r
    dnw_ref[...] += jnp.sum(dy * nrm, axis=0, keepdims=True)
    dn = dy * nw_ref[...]
    do = r * (dn - nrm * jnp.mean(dn * nrm, axis=-1, keepdims=True))
    do_ref[...] = do
    ones_bd = _head_mean_matrix().astype(BF16)
    prod = do * o
    delta = jnp.concatenate([_head_sum2(prod[:, j * LANE:(j + 1) * LANE], ones_bd) for j in range(ATT_D // LANE)], axis=1)
    lane = lax.broadcasted_iota(jnp.int32, (1, ATT_D), 1)
    st_ref[...] = jnp.where((lane & (HEAD_DIM - 1)) < HALF, lse_ref[...], delta)


def _ada_fwd(c_all, w_ada):
    def body(c_ref, w_ref, o_ref):
        cv = c_ref[...]
        o_ref[...] = _dot((cv * _sigmoid(cv)).astype(BF16), w_ref[...].astype(BF16), NN)

    return pl.pallas_call(body, name="ada_fwd", out_shape=jax.ShapeDtypeStruct((c_all.shape[0], w_ada.shape[1]), F32),
                          compiler_params=_cparams())(c_all, w_ada)


def _adamw_math(g, w, m, v):
    m_new = ADAM_B1 * m + (1.0 - ADAM_B1) * g
    v_new = ADAM_B2 * v + (1.0 - ADAM_B2) * (g * g)
    m_hat = m_new / (1.0 - ADAM_B1 ** ADAM_STEP)
    v_hat = v_new / (1.0 - ADAM_B2 ** ADAM_STEP)
    delta = -ADAM_LR * (m_hat / (jnp.sqrt(v_hat) + ADAM_EPS) + ADAM_WD * w)
    return delta, m_new, v_new


def _ada_bwd_adamw(c_all, dmod_cols, w, m, v):
    rows, cols = w.shape
    tr = 256
    blk = pl.BlockSpec((tr, cols), lambda i: (i, 0))

    def body(c_ref, d_ref, w_ref, m_ref, v_ref, g_ref, dl_ref, mo_ref, vo_ref):
        cv = c_ref[...]
        ca = cv * _sigmoid(cv)
        g = ca[:, 0:1] * d_ref[0:1, :]
        for b in range(1, N_DEV):
            g = g + ca[:, b:b + 1] * d_ref[b:b + 1, :]
        g_ref[...] = g
        dl_ref[...], mo_ref[...], vo_ref[...] = _adamw_math(g, w_ref[...], m_ref[...], v_ref[...])

    o = jax.ShapeDtypeStruct((rows, cols), F32)
    return pl.pallas_call(
        body, name="ada_bwd_adamw", grid=(rows // tr,),
        in_specs=[pl.BlockSpec((tr, N_DEV), lambda i: (i, 0)), pl.BlockSpec((N_DEV, cols), lambda i: (0, 0)), blk, blk, blk],
        out_specs=[blk] * 4, out_shape=[o, o, o, o], compiler_params=_cparams(("parallel",)))(c_all.T, dmod_cols, w, m, v)


def _reduce_adamw(slabs, w, m, v, name):
    rows, cols = w.shape
    n_src = slabs.shape[0]
    if rows % 128 == 0:
        tr, steps = 128, rows // 128
        blk = pl.BlockSpec((tr, cols), lambda i: (i, 0))
        sblk = pl.BlockSpec((n_src, tr, cols), lambda i: (0, i, 0))
    else:
        tc, steps = 256, cols // 256
        blk = pl.BlockSpec((rows, tc), lambda i: (0, i))
        sblk = pl.BlockSpec((n_src, rows, tc), lambda i: (0, 0, i))

    def body(s_ref, w_ref, m_ref, v_ref, g_ref, dl_ref, mo_ref, vo_ref):
        g = s_ref[0].astype(F32)
        for src in range(1, n_src):
            g = g + s_ref[src].astype(F32)
        g_ref[...] = g
        dl_ref[...], mo_ref[...], vo_ref[...] = _adamw_math(g, w_ref[...], m_ref[...], v_ref[...])

    o = jax.ShapeDtypeStruct((rows, cols), F32)
    return pl.pallas_call(
        body, name=name, grid=(steps,), in_specs=[sblk, blk, blk, blk],
        out_specs=[blk] * 4, out_shape=[o, o, o, o], compiler_params=_cparams(("parallel",)))(slabs, w, m, v)


def _small_reduce_adamw(gathered, w, m, v):
    def body(s_ref, w_ref, m_ref, v_ref, g_ref, dl_ref, mo_ref, vo_ref):
        g = s_ref[0]
        for dev in range(1, N_DEV):
            g = g + s_ref[dev]
        g_ref[...] = g
        dl_ref[...], mo_ref[...], vo_ref[...] = _adamw_math(g, w_ref[...], m_ref[...], v_ref[...])

    o = jax.ShapeDtypeStruct(w.shape, F32)
    return pl.pallas_call(body, name="small_reduce_adamw", out_shape=[o, o, o, o], compiler_params=_cparams())(gathered, w, m, v)


def _adamw_small(g, w, m, v, name):
    def body(g_ref, w_ref, m_ref, v_ref, dl_ref, mo_ref, vo_ref):
        dl_ref[...], mo_ref[...], vo_ref[...] = _adamw_math(g_ref[...], w_ref[...], m_ref[...], v_ref[...])

    o = jax.ShapeDtypeStruct(w.shape, F32)
    return pl.pallas_call(body, name=name, out_shape=[o, o, o], compiler_params=_cparams())(g, w, m, v)


class _Exchange:
    def __init__(self, arrs, scatter):
        self.arrs, self.scatter, self.n = list(arrs), scatter, len(arrs)
        hbm = pl.BlockSpec(memory_space=pltpu.HBM)
        self.in_specs = [hbm] * self.n
        self.out_specs = [hbm] * self.n
        self.out_shape = [jax.ShapeDtypeStruct(a.shape if scatter else (N_DEV,) + a.shape, a.dtype) for a in self.arrs]
        self.scratch = [pltpu.SemaphoreType.DMA((self.n * (N_DEV - 1),)), pltpu.SemaphoreType.DMA((self.n * (N_DEV - 1),)),
                        pltpu.SemaphoreType.DMA((self.n,))]

    def _local(self, ins, outs, sems):
        me = 4 * lax.axis_index("x") + 2 * lax.axis_index("y") + lax.axis_index("c")
        return [pltpu.make_async_copy(ins[a].at[me] if self.scatter else ins[a], outs[a].at[me], sems[2].at[a])
                for a in range(self.n)]

    def _remote(self, ins, outs, sems, arriving):
        send_sems, recv_sems, _ = sems
        x, y, c = lax.axis_index("x"), lax.axis_index("y"), lax.axis_index("c")
        me = 4 * x + 2 * y + c
        remote = []
        for a in range(self.n):
            for k in range(1, N_DEV):
                px = 1 - x if k & 4 else x
                py = 1 - y if k & 2 else y
                pc = 1 - c if k & 1 else c
                peer = 4 * px + 2 * py + pc
                sem = a * (N_DEV - 1) + k - 1
                remote.append(pltpu.make_async_remote_copy(
                    src_ref=ins[a].at[peer] if self.scatter else ins[a], dst_ref=outs[a].at[peer if arriving else me],
                    send_sem=send_sems.at[sem], recv_sem=recv_sems.at[sem], device_id=(px, py, pc), device_id_type=MESH_IDS))
        return remote

    def start(self, ins, outs, sems):
        for cp in self._local(ins, outs, sems) + self._remote(ins, outs, sems, arriving=False):
            cp.start()

    def forward(self, ins, outs, sems):
        pass

    def wait(self, ins, outs, sems):
        for send, arrival in zip(self._remote(ins, outs, sems, arriving=False), self._remote(ins, outs, sems, arriving=True)):
            send.wait_send()
            arrival.wait_recv()
        for cp in self._local(ins, outs, sems):
            cp.wait()


N_CHIP = N_DEV // 2


class _SiblingSwap(_Exchange):
    def __init__(self, arrs):
        super().__init__(arrs, scatter=True)
        self.out_shape = [jax.ShapeDtypeStruct((N_CHIP,) + a.shape[2:], a.dtype) for a in self.arrs]
        self.scratch = [pltpu.SemaphoreType.DMA((self.n,)), pltpu.SemaphoreType.DMA((self.n,)), pltpu.SemaphoreType.DMA((1,))]

    def _copies(self, ins, outs, sems):
        x, y, c = lax.axis_index("x"), lax.axis_index("y"), lax.axis_index("c")
        return [pltpu.make_async_remote_copy(src_ref=ins[a].at[:, 1 - c], dst_ref=outs[a], send_sem=sems[0].at[a], recv_sem=sems[1].at[a],
                                             device_id=(x, y, 1 - c), device_id_type=MESH_IDS) for a in range(self.n)]

    def start(self, ins, outs, sems):
        for cp in self._copies(ins, outs, sems):
            cp.start()

    def wait(self, ins, outs, sems):
        for cp in self._copies(ins, outs, sems):
            cp.wait()


class _ChipScatter(_Exchange):
    def __init__(self, arrs):
        super().__init__(arrs, scatter=True)
        n_pairs = self.n * (N_CHIP - 1)
        self.scratch = [pltpu.SemaphoreType.DMA((n_pairs,)), pltpu.SemaphoreType.DMA((n_pairs,)), pltpu.SemaphoreType.DMA((self.n,))]

    def _local(self, ins, outs, sems):
        chip = 2 * lax.axis_index("x") + lax.axis_index("y")
        return [pltpu.make_async_copy(ins[a].at[chip], outs[a].at[chip], sems[2].at[a]) for a in range(self.n)]

    def _remote(self, ins, outs, sems, arriving):
        send_sems, recv_sems, _ = sems
        x, y, c = lax.axis_index("x"), lax.axis_index("y"), lax.axis_index("c")
        chip = 2 * x + y
        remote = []
        for a in range(self.n):
            for k in range(1, N_CHIP):
                px = 1 - x if k & 2 else x
                py = 1 - y if k & 1 else y
                peer = 2 * px + py
                sem = a * (N_CHIP - 1) + k - 1
                remote.append(pltpu.make_async_remote_copy(
                    src_ref=ins[a].at[peer], dst_ref=outs[a].at[peer if arriving else chip], send_sem=send_sems.at[sem],
                    recv_sem=recv_sems.at[sem], device_id=(px, py, c), device_id_type=MESH_IDS))
        return remote


def _chip_sum(mine, theirs):
    n, rows, cols = mine.shape
    blk = pl.BlockSpec((1, rows, 256), lambda q, j: (q, 0, j))

    def body(a_ref, b_ref, o_ref):
        o_ref[...] = (a_ref[...].astype(F32) + b_ref[...].astype(F32)).astype(BF16)

    return pl.pallas_call(body, name="chip_sum", grid=(n, cols // 256), in_specs=[blk, blk], out_specs=blk,
                          out_shape=jax.ShapeDtypeStruct(mine.shape, BF16),
                          compiler_params=_cparams(("parallel", "parallel")))(mine, theirs)


class _Gather2(_Exchange):
    def __init__(self, arrs):
        super().__init__(arrs, scatter=False)

    def _copies(self, ins, outs, sems):
        send_sems, recv_sems, _ = sems
        x, y, c = lax.axis_index("x"), lax.axis_index("y"), lax.axis_index("c")
        sibling = (x, y, 1 - c)
        chips = [(1 - x, y), (x, 1 - y), (1 - x, 1 - y)]
        first, passed, landed = [], [], []
        for a in range(self.n):
            def copy(k, block, to, src=None, a=a):
                slab = outs[a].at[4 * block[0] + 2 * block[1] + block[2]]
                return pltpu.make_async_remote_copy(
                    src_ref=slab if src is None else src, dst_ref=slab, send_sem=send_sems.at[a * (N_DEV - 1) + k],
                    recv_sem=recv_sems.at[a * (N_DEV - 1) + k], device_id=to, device_id_type=MESH_IDS)

            first.append(copy(0, (x, y, c), sibling, src=ins[a]))
            landed.append(copy(0, sibling, sibling))
            for j, chip in enumerate(chips):
                first.append(copy(1 + j, (x, y, c), (*chip, c), src=ins[a]))
                passed.append((copy(1 + j, (*chip, c), sibling), copy(4 + j, (*chip, c), sibling)))
                landed.append(copy(4 + j, (*chip, 1 - c), sibling))
        return first, passed, landed

    def start(self, ins, outs, sems):
        for cp in self._local(ins, outs, sems) + self._copies(ins, outs, sems)[0]:
            cp.start()

    def forward(self, ins, outs, sems):
        for arrival, onward in self._copies(ins, outs, sems)[1]:
            arrival.wait_recv()
            onward.start()

    def wait(self, ins, outs, sems):
        first, passed, landed = self._copies(ins, outs, sems)
        for arrival in landed:
            arrival.wait_recv()
        for cp in first + [onward for _, onward in passed]:
            cp.wait_send()
        for cp in self._local(ins, outs, sems):
            cp.wait()


def _split_comm_refs(refs, n_in, n_out, n_scr, comm):
    nc = comm.n if comm is not None else 0
    ns = 3 if comm is not None else 0
    pos, groups = 0, []
    for cnt in (n_in, nc, n_out, nc, n_scr, ns):
        groups.append(refs[pos:pos + cnt])
        pos += cnt
    assert pos == len(refs), (pos, len(refs))
    return groups


def _pcall(body, args, *, name, grid, in_specs, out_specs, out_shape, scratch_shapes=(), sem=None, comm=None):
    in_specs, out_specs, out_shape, scratch_shapes = list(in_specs), list(out_specs), list(out_shape), list(scratch_shapes)
    n_in, n_out, n_scr = len(in_specs), len(out_specs), len(scratch_shapes)
    if comm is None:
        kernel_body = body
    else:
        def kernel_body(*refs):
            ins, cins, outs, couts, scr, sems = _split_comm_refs(refs, n_in, n_out, n_scr, comm)
            ids = [pl.program_id(a) for a in range(len(grid))]
            first, last = ids[0] == 0, ids[0] == grid[0] - 1
            for a in range(1, len(grid)):
                first, last = first & (ids[a] == 0), last & (ids[a] == grid[a] - 1)

            middle = ids[0] == (2 * grid[0]) // 3
            for a in range(1, len(grid)):
                middle = middle & (ids[a] == 0)

            @pl.when(first)
            def _():
                comm.start(cins, couts, sems)

            @pl.when(middle)
            def _():
                comm.forward(cins, couts, sems)

            body(*ins, *outs, *scr)

            @pl.when(last)
            def _():
                comm.wait(cins, couts, sems)

        in_specs, out_specs, out_shape = in_specs + comm.in_specs, out_specs + comm.out_specs, out_shape + comm.out_shape
        scratch_shapes, args = scratch_shapes + comm.scratch, list(args) + comm.arrs
        sem = ("arbitrary",) * len(grid)
    res = pl.pallas_call(kernel_body, name=name, grid=grid, in_specs=in_specs, out_specs=out_specs, out_shape=out_shape,
                         scratch_shapes=scratch_shapes, compiler_params=_cparams(sem))(*args)
    return res[:n_out], res[n_out:]


def _exchange(arrs, name, scatter=False, ex=None):
    if ex is None:
        ex = _Exchange(arrs, scatter=True) if scatter else _Gather2(arrs)

    def body(*refs):
        _, ins, _, outs, _, sems = _split_comm_refs(refs, 0, 0, 0, ex)
        ex.start(ins, outs, sems)
        ex.forward(ins, outs, sems)
        ex.wait(ins, outs, sems)

    return pl.pallas_call(body, name=name, in_specs=ex.in_specs, out_specs=ex.out_specs, out_shape=ex.out_shape,
                          scratch_shapes=ex.scratch)(*ex.arrs)


def _pad_lanes(v, width=LANE):
    return jnp.pad(v, ((0, 0), (0, width - v.shape[1])))


def _shards_to_cols(g):
    return jnp.transpose(g, (1, 0, 2)).reshape(g.shape[1], N_DEV * g.shape[2])


def _local_step(x, tgt, mod, w_in_pt, conv_w, conv_b, dt_bias, a_log, d_skip, ssd_norm_w, q_norm_w, k_norm_w,
                attn_norm_w, w_out_sh, w_ff1_sh, w_ff2_sh, norm1_w, norm2_w, core):
    shift1, scale1, gate1, shift2, scale2, gate2 = [mod[i:i + 1] for i in range(N_MOD)]
    dtb, alog, dsk = _pad_lanes(dt_bias), _pad_lanes(a_log), _pad_lanes(d_skip)
    qw, kw = jnp.tile(q_norm_w, (1, ATT_HEADS)), jnp.tile(k_norm_w, (1, ATT_HEADS))

    h1 = _norm_mod_fwd(x, norm1_w, scale1, shift1, "norm1_fwd")
    proj = _matmul(h1, w_in_pt, tb=True, tm=2048, tn=896, tk=1024, name="in_proj")
    pre, act = _conv_fwd(proj, conv_w, conv_b)
    ypre, ycat_ssd, hall = _ssd_fwd(proj, act, dtb, alog, dsk, ssd_norm_w)
    (o_att, lse), (w_out_g, w_ff1_g, w_ff2_g) = _att_fwd(proj, qw, kw, comm=_Gather2([w_out_sh, w_ff1_sh, w_ff2_sh]))
    w_out = w_out_g.reshape(2 * D_MODEL, D_MODEL)
    w_ff1 = _shards_to_cols(w_ff1_g)
    w_ff2 = w_ff2_g.reshape(D_FF, D_MODEL)
    ycat = _att_norm_fwd(o_att, attn_norm_w, ycat_ssd)
    row32, row16, vec32 = ("row", F32), ("row", BF16), ("vec", F32)
    mix, x1, h2 = _matmul_rows(ycat, w_out, _residual_norm_epilogue, [x], [gate1, norm2_w, scale2, shift2],
                               [row32, row32, row16], tm=512, name="out_proj")
    u, act_ff = _matmul(h2, w_ff1, tm=1024, tn=2048, tk=1024, name="ff1", mode="relu2")
    loss, dout, dff, dgate2 = _matmul_rows(act_ff, w_ff2, _loss_epilogue, [x1, tgt], [gate2],
                                           [("one", F32), row32, row16, vec32], tm=512, name="ff2")

    du = _matmul(dff, w_ff2, tb=True, tm=512, tn=4096, tk=1024, out_dtype=BF16, name="ff2_dx", mode="drelu2", u=u)
    g_ff2 = _matmul(act_ff, dff, ta=True, tm=512, tn=1024, tk=4096, out_dtype=BF16, name="ff2_dw")
    dx1, dshift2, dscale2, g_norm2, dmix, dgate1 = _matmul_rows(
        du, w_ff1, _norm_bwd_epilogue, [x1, dout, mix], [norm2_w, scale2, gate1],
        [row32, vec32, vec32, vec32, row16, vec32], tb=True, tm=512, name="ff1_dx")
    g_ff1 = _matmul(h2, du, ta=True, tm=1024, tn=D_FF // N_DEV, tk=4096, out_dtype=BF16, name="ff1_dw", shard_out=True)

    dy_ssd, do, stats, g_attn_norm = _matmul_rows(
        dmix, w_out, _mixer_split_epilogue, [o_att, lse], [attn_norm_w],
        [("row", F32, SSD_D_INNER), ("row", F32, ATT_D), ("row", F32, ATT_D), ("vec", F32, ATT_D)], tb=True, tm=512, name="out_proj_dx")
    g_out = _matmul(ycat, dmix, ta=True, tm=512, tn=1024, tk=4096, out_dtype=BF16, name="out_proj_dw")
    ff_slabs = [g_ff1, g_ff2.reshape(N_DEV, D_FF // N_DEV, D_MODEL)]
    (dq, dk, dv, dqw, dkw), (s_ff1, s_ff2) = _att_bwd(proj, do, stats, qw, kw, comm=_Exchange(ff_slabs, scatter=True))
    out_slabs = [g_out.astype(BF16).reshape(N_DEV, 2 * D_MODEL // N_DEV, D_MODEL)]
    (dz, dact, ddtr, da, g_dsk, g_dtb, g_ssd_norm), (s_out,) = _ssd_bwd(
        dy_ssd, ypre, proj, act, hall, dtb, alog, dsk, ssd_norm_w, comm=_Exchange(out_slabs, scatter=True))
    dxbc, g_conv_w, g_conv_b = _conv_bwd(dact, pre, proj, conv_w)
    dproj = [(dz, OFF_Z), (dxbc, OFF_XBC), (ddtr, OFF_DT), (dq, OFF_Q), (dk, OFF_K), (dv, OFF_V)]
    g_head, g_tail = _pieces_t_matmul([[dz, dxbc], [dq, dk, dv]], h1, tm=256, name="in_proj_dw")
    g_dt = _matmul(ddtr, h1, ta=True, tm=LANE, tn=1024, tk=4096, out_dtype=BF16, name="in_proj_dw_dt")[:SSD_HEADS]
    in_slabs = jnp.concatenate([g_head, g_dt, g_tail], axis=0).reshape(N_CHIP, 2, IN_W // N_DEV, D_MODEL)
    (sibling_slabs,) = _exchange(None, "swap_w_in_grads", ex=_SiblingSwap([in_slabs]))
    chip_slabs = _chip_sum(lax.dynamic_index_in_dim(in_slabs, core, axis=1, keepdims=False), sibling_slabs)
    (grad_x, dshift1, dscale1, g_norm1), (s_in,) = _matmul_rows(
        dproj, w_in_pt, _norm_bwd_epilogue, [x, dx1], [norm1_w, scale1], [row32, vec32, vec32, vec32],
        tm=256, name="in_proj_dx", comm=_ChipScatter([chip_slabs]))

    dmod = jnp.concatenate([dshift1, dscale1, dgate1, dshift2, dscale2, dgate2], axis=0)
    g_alog = da[:, :SSD_HEADS] * (-jnp.exp(a_log))
    g_qw = dqw.reshape(ATT_HEADS, HEAD_DIM).sum(axis=0, keepdims=True)
    g_kw = dkw.reshape(ATT_HEADS, HEAD_DIM).sum(axis=0, keepdims=True)
    return dict(loss=loss, grad_x=grad_x, dmod=dmod, norm1_w=g_norm1, norm2_w=g_norm2, w_in=s_in, conv_w=g_conv_w,
                conv_b=g_conv_b, dt_bias=g_dtb[:, :SSD_HEADS], a_log=g_alog, d_skip=g_dsk[:, :SSD_HEADS],
                ssd_norm_w=g_ssd_norm, q_norm_w=g_qw, k_norm_w=g_kw, attn_norm_w=g_attn_norm, w_out=s_out,
                w_ff1=s_ff1, w_ff2=s_ff2)


def _pack_w_in_rows(wt_full):
    cut = OFF_DT + SSD_HEADS
    pad = jnp.zeros((LANE - SSD_HEADS, wt_full.shape[1]), wt_full.dtype)
    return jnp.concatenate([wt_full[:cut], pad, wt_full[cut:]], axis=0)


MISC_FIELDS = (("dt_bias", SSD_HEADS), ("a_log", SSD_HEADS), ("d_skip", SSD_HEADS), ("q_norm_w", HEAD_DIM), ("k_norm_w", HEAD_DIM),
               ("loss", 1))
SMALL_LAYOUT = (("b_ada", 6), ("norm1_w", 1), ("norm2_w", 1), ("conv_w", 8), ("conv_b", 2), ("ssd_norm_w", 1),
                ("attn_norm_w", 1), ("misc", 1))


def _pack_small(vals):
    rows = []
    for name, nrow in SMALL_LAYOUT:
        if name == "misc":
            misc = jnp.concatenate([vals[f].reshape(1, n) if f in vals else jnp.zeros((1, n), F32) for f, n in MISC_FIELDS], axis=1)
            rows.append(_pad_lanes(misc, D_MODEL))
        elif name in vals:
            rows.append(vals[name].reshape(nrow, D_MODEL))
        else:
            rows.append(jnp.zeros((nrow, D_MODEL), F32))
    used = sum(n for _, n in SMALL_LAYOUT)
    rows.append(jnp.zeros((SMALL_ROWS - used, D_MODEL), F32))
    return jnp.concatenate(rows, axis=0)


def _unpack_small(packed):
    out, r = {}, 0
    for name, nrow in SMALL_LAYOUT:
        blk = packed[r:r + nrow]
        r += nrow
        if name == "misc":
            c0 = 0
            for f, n in MISC_FIELDS:
                out[f] = blk[:, c0:c0 + n]
                c0 += n
        elif name == "b_ada":
            out[name] = blk.reshape(1, N_MOD * D_MODEL)
        elif name == "conv_w":
            out[name] = blk.reshape(CONV_K, CONV_CH)
        elif name == "conv_b":
            out[name] = blk.reshape(1, CONV_CH)
        else:
            out[name] = blk
    return out


WEIGHT_NAMES = ("norm1_w", "norm2_w", "w_ada", "b_ada", "w_in", "conv_w", "conv_b", "dt_bias", "a_log", "d_skip",
                "ssd_norm_w", "q_norm_w", "k_norm_w", "attn_norm_w", "w_out", "w_ff1", "w_ff2")
SMALL_NAMES = ("norm1_w", "norm2_w", "b_ada", "conv_b", "dt_bias", "a_log", "d_skip", "ssd_norm_w", "q_norm_w",
               "k_norm_w", "attn_norm_w")


def kernel(x, c, norm1_w, norm2_w, w_ada, b_ada, w_in, conv_w, conv_b, dt_bias, a_log, d_skip, ssd_norm_w, q_norm_w, k_norm_w, attn_norm_w, w_out, w_ff1, w_ff2, loss_target, m_norm1_w, m_norm2_w, m_w_ada, m_b_ada, m_w_in, m_conv_w, m_conv_b, m_dt_bias, m_a_log, m_d_skip, m_ssd_norm_w, m_q_norm_w, m_k_norm_w, m_attn_norm_w, m_w_out, m_w_ff1, m_w_ff2, v_norm1_w, v_norm2_w, v_w_ada, v_b_ada, v_w_in, v_conv_w, v_conv_b, v_dt_bias, v_a_log, v_d_skip, v_ssd_norm_w, v_q_norm_w, v_k_norm_w, v_attn_norm_w, v_w_out, v_w_ff1, v_w_ff2):
    args = dict(locals())
    w = {n: args[n] for n in WEIGHT_NAMES}
    m = {n: args["m_" + n] for n in WEIGHT_NAMES}
    v = {n: args["v_" + n] for n in WEIGHT_NAMES}
    me = 4 * lax.axis_index("x") + 2 * lax.axis_index("y") + lax.axis_index("c")

    c_rows = jnp.pad(c, ((0, 7), (0, 0)))
    w_in_t, m_in_t, v_in_t = [jnp.transpose(t["w_in"][0]) for t in (w, m, v)]
    c_g, conv_g, w_in_g = _exchange([c_rows, w["conv_w"][0], w_in_t.astype(BF16)], "gather_w_in", scatter=False)
    c_all = c_g[:, 0, :]
    conv_full = _shards_to_cols(conv_g)
    w_in_pt = _pack_w_in_rows(w_in_g.reshape(IN_W, D_MODEL))

    mod_part = _ada_fwd(c_all, w["w_ada"][0])
    (mod_g,) = _exchange([mod_part], "gather_mod", scatter=False)
    mod_mine = lax.dynamic_index_in_dim(mod_g, me, axis=1, keepdims=False).reshape(1, N_MOD * D_MODEL) + w["b_ada"]
    mod = mod_mine.reshape(N_MOD, D_MODEL)

    res = _local_step(x[0], loss_target[0], mod, w_in_pt, conv_full, w["conv_b"], w["dt_bias"], w["a_log"], w["d_skip"],
                      w["ssd_norm_w"], w["q_norm_w"], w["k_norm_w"], w["attn_norm_w"], w["w_out"][0].astype(BF16),
                      w["w_ff1"][0].astype(BF16), w["w_ff2"][0].astype(BF16), w["norm1_w"], w["norm2_w"], lax.axis_index("c"))

    small_vals = {n: res[n] for n in SMALL_NAMES if n != "b_ada"}
    small_vals["b_ada"] = res["dmod"]
    small_vals["conv_w"] = res["conv_w"]
    small_vals["loss"] = res["loss"]
    (small_g,) = _exchange([_pack_small(small_vals)], "gather_small", scatter=False)

    grads, delta, new_m, new_v = {}, {}, {}, {}
    for name in ("w_out", "w_ff1", "w_ff2"):
        outs = _reduce_adamw(res[name], w[name][0], m[name][0], v[name][0], "adamw_" + name)
        grads[name], delta[name], new_m[name], new_v[name] = [o[None] for o in outs]
    outs = _reduce_adamw(res["w_in"], w_in_t, m_in_t, v_in_t, "adamw_w_in")
    grads["w_in"], delta["w_in"], new_m["w_in"], new_v["w_in"] = [jnp.transpose(o)[None] for o in outs]

    sm = _small_reduce_adamw(small_g, _pack_small({n: w[n] for n in SMALL_NAMES}), _pack_small({n: m[n] for n in SMALL_NAMES}),
                             _pack_small({n: v[n] for n in SMALL_NAMES}))
    sm = [_unpack_small(p) for p in sm]
    for n in SMALL_NAMES:
        grads[n], delta[n], new_m[n], new_v[n] = [p[n] for p in sm]
    shard_w = CONV_CH // N_DEV
    g_conv = lax.dynamic_slice_in_dim(sm[0]["conv_w"], me * shard_w, shard_w, axis=1)
    cw = _adamw_small(g_conv, w["conv_w"][0], m["conv_w"][0], v["conv_w"][0], "adamw_conv_w")
    grads["conv_w"] = g_conv[None]
    delta["conv_w"], new_m["conv_w"], new_v["conv_w"] = [o[None] for o in cw]

    ada_w = w_ada.shape[2]
    dmod_all = small_g[:, :N_MOD, :].reshape(N_DEV, N_MOD * D_MODEL)
    dmod_cols = lax.dynamic_slice_in_dim(dmod_all, me * ada_w, ada_w, axis=1)
    outs = _ada_bwd_adamw(c_all, dmod_cols, w["w_ada"][0], m["w_ada"][0], v["w_ada"][0])
    grads["w_ada"], delta["w_ada"], new_m["w_ada"], new_v["w_ada"] = [o[None] for o in outs]

    loss = sm[0]["loss"][0, 0]
    return (loss, res["grad_x"][None], *[grads[n] for n in WEIGHT_NAMES], *[delta[n] for n in WEIGHT_NAMES],
            *[new_m[n] for n in WEIGHT_NAMES], *[new_v[n] for n in WEIGHT_NAMES])
```

```python
import jax
import jax.numpy as jnp
from jax import lax
from jax.experimental import pallas as pl
from jax.experimental.pallas import tpu as pltpu

F32 = jnp.float32
BF16 = jnp.bfloat16
HIGHEST = lax.Precision.HIGHEST
MESH_IDS = pl.DeviceIdType.MESH

N_DEV = 8
D_MODEL = 1024
HEAD_DIM = 64
SSD_HEADS = 16
SSD_GROUPS = 4
HEADS_PER_GROUP = SSD_HEADS // SSD_GROUPS
SSD_STATE = 128
SSD_CHUNK = 128
SSD_D_INNER = SSD_HEADS * HEAD_DIM
GROUP_WIDTH = SSD_D_INNER // SSD_GROUPS
CONV_K = 4
CONV_CH = SSD_D_INNER + 2 * SSD_GROUPS * SSD_STATE
ATT_HEADS = 16
ATT_D = ATT_HEADS * HEAD_DIM
ATT_BLK = 128
DILATIONS = (1, 4, 16)
D_FF = 4 * D_MODEL
N_MOD = 6
EPS = 1e-6
IN_W = SSD_D_INNER + CONV_CH + SSD_HEADS + 3 * ATT_D
LANE = 128
OFF_Z, OFF_XBC, OFF_DT = 0, SSD_D_INNER, SSD_D_INNER + CONV_CH
OFF_Q = OFF_DT + LANE
OFF_K, OFF_V = OFF_Q + ATT_D, OFF_Q + 2 * ATT_D
IN_WP = OFF_V + ATT_D

ADAM_LR, ADAM_B1, ADAM_B2, ADAM_EPS, ADAM_WD, ADAM_STEP = 0.001, 0.9, 0.999, 1e-08, 0.01, 10
VMEM_LIMIT = 60 * 1024 * 1024
ROW_TILE = 512
SMALL_ROWS = 24


def _cparams(sem=None):
    return pltpu.CompilerParams(dimension_semantics=sem, vmem_limit_bytes=VMEM_LIMIT)


def _sigmoid(v):
    return 1.0 / (1.0 + jnp.exp(-v))


def _softplus(v):
    y = jnp.exp(-jnp.abs(v))
    small = y * (1.0 - y * (0.5 - y * (1.0 / 3.0)))
    return jnp.maximum(v, 0.0) + jnp.where(y < 0.01, small, jnp.log(1.0 + y))


def _dot(a, b, dims, precision=None):
    return lax.dot_general(a, b, (dims, ((), ())), preferred_element_type=F32, precision=precision)


NN = ((1,), (0,))
NT = ((1,), (1,))
TN = ((0,), (0,))


def _matmul(a, b, *, ta=False, tb=False, tm, tn, tk, out_dtype=F32, name, mode=None, u=None, comm=None, shard_out=False):
    m, k = (a.shape[1], a.shape[0]) if ta else a.shape
    n = b.shape[0] if tb else b.shape[1]
    assert m % tm == 0 and n % tn == 0 and k % tk == 0, (name, m, n, k)
    nk = k // tk
    a_spec = pl.BlockSpec((tk, tm), lambda i, j, kk: (kk, i)) if ta else pl.BlockSpec((tm, tk), lambda i, j, kk: (i, kk))
    b_spec = pl.BlockSpec((tn, tk), lambda i, j, kk: (j, kk)) if tb else pl.BlockSpec((tk, tn), lambda i, j, kk: (kk, j))
    o_spec = pl.BlockSpec((tm, tn), lambda i, j, kk: (i, j))
    dims = ((0,) if ta else (1,), (1,) if tb else (0,))
    n_out = 2 if mode == "relu2" else 1

    def body(*refs):
        if mode == "drelu2":
            a_ref, b_ref, u_ref = refs[:3]
            rest = refs[3:]
        else:
            a_ref, b_ref = refs[:2]
            u_ref = None
            rest = refs[2:]
        outs = rest[:n_out]
        part = _dot(a_ref[...], b_ref[...], dims)

        def finish(r):
            if mode == "relu2":
                outs[0][...] = r.astype(BF16)
                rr = jnp.maximum(r, 0.0)
                outs[1][...] = (rr * rr).astype(BF16)
            elif mode == "drelu2":
                outs[0][...] = (r * (2.0 * jnp.maximum(u_ref[...].astype(F32), 0.0))).astype(out_dtype)
            else:
                outs[0][...] = r.astype(out_dtype)

        if nk == 1:
            finish(part)
        else:
            acc = rest[n_out]
            kk = pl.program_id(2)

            @pl.when(kk == 0)
            def _():
                acc[...] = part

            @pl.when(kk > 0)
            def _():
                acc[...] += part

            @pl.when(kk == nk - 1)
            def _():
                finish(acc[...])

    in_specs = [a_spec, b_spec]
    args = [a, b]
    if mode == "drelu2":
        in_specs.append(o_spec)
        args.append(u)
    if mode == "relu2":
        out_shape = [jax.ShapeDtypeStruct((m, n), BF16), jax.ShapeDtypeStruct((m, n), BF16)]
    elif shard_out:
        out_shape = [jax.ShapeDtypeStruct((n // tn, m, tn), out_dtype)]
        o_spec = pl.BlockSpec((None, tm, tn), lambda i, j, kk: (j, i, 0))
    else:
        out_shape = [jax.ShapeDtypeStruct((m, n), out_dtype)]
    outs, comm_outs = _pcall(
        body, args, name=name, grid=(m // tm, n // tn, nk), in_specs=in_specs, out_specs=[o_spec] * n_out,
        out_shape=out_shape, scratch_shapes=[pltpu.VMEM((tm, tn), F32)] if nk > 1 else [],
        sem=("parallel", "parallel", "arbitrary"), comm=comm)
    res = tuple(outs) if mode == "relu2" else outs[0]
    return res if comm is None else (res, comm_outs)


def _pieces_t_matmul(groups, b, *, tm, name):
    k, n = b.shape
    pieces = [p for g in groups for p in g]
    starts, tiles = [], 0
    for p in pieces:
        assert p.shape[0] == k and p.shape[1] % tm == 0, (name, p.shape)
        starts.append(tiles)
        tiles += p.shape[1] // tm
    group_of, group_start, group_tiles = [], [], []
    for gi, g in enumerate(groups):
        group_start.append(starts[len(group_of)])
        group_of += [gi] * len(g)
        group_tiles.append(sum(p.shape[1] // tm for p in g))

    def clipped(block, start, count):
        return pl.BlockSpec(block, (lambda i: (0, jnp.clip(i - start, 0, count - 1))) if block[0] == k
                            else (lambda i: (jnp.clip(i - start, 0, count - 1), 0)))

    def body(*refs):
        a_refs, b_ref, o_refs = refs[:len(pieces)], refs[len(pieces)], refs[len(pieces) + 1:]
        i = pl.program_id(0)
        for a_ref, start, p, gi in zip(a_refs, starts, pieces, group_of):
            @pl.when((i >= start) & (i < start + p.shape[1] // tm))
            def _(a_ref=a_ref, o_ref=o_refs[gi]):
                o_ref[...] = _dot(a_ref[...], b_ref[...], TN).astype(BF16)

    return pl.pallas_call(
        body, name=name, grid=(tiles,),
        in_specs=[clipped((k, tm), s0, p.shape[1] // tm) for s0, p in zip(starts, pieces)] + [pl.BlockSpec((k, n), lambda i: (0, 0))],
        out_specs=[clipped((tm, n), s0, cnt) for s0, cnt in zip(group_start, group_tiles)],
        out_shape=[jax.ShapeDtypeStruct((cnt * tm, n), BF16) for cnt in group_tiles],
        compiler_params=_cparams(("arbitrary",)))(*pieces, b)


def _rms_mod(xv, nw, scale, shift):
    r = lax.rsqrt(jnp.mean(xv * xv, axis=-1, keepdims=True) + EPS)
    return ((xv * r) * nw * (1.0 + scale) + shift).astype(BF16)


def _norm_mod_fwd(x, nw, scale, shift, name):
    s, d = x.shape
    row = pl.BlockSpec((ROW_TILE, d), lambda i: (i, 0))
    vec = pl.BlockSpec((1, d), lambda i: (0, 0))

    def body(x_ref, nw_ref, sc_ref, sh_ref, h_ref):
        h_ref[...] = _rms_mod(x_ref[...], nw_ref[...], sc_ref[...], sh_ref[...])

    return pl.pallas_call(body, name=name, grid=(s // ROW_TILE,), in_specs=[row, vec, vec, vec], out_specs=row,
                          out_shape=jax.ShapeDtypeStruct((s, d), BF16), compiler_params=_cparams(("parallel",)))(x, nw, scale, shift)


def _matmul_rows(a, b, epilogue, row_in, vec_in, outs, *, tb=False, tm, name, comm=None):
    pieces = a if isinstance(a, list) else [(a, 0)]
    assert not (tb and len(pieces) > 1)
    m = pieces[0][0].shape[0]
    n = b.shape[0] if tb else b.shape[1]
    assert m % tm == 0, (name, m, tm)
    dims = ((1,), (1,) if tb else (0,))
    n_a, n_row, n_vec = len(pieces), len(row_in), len(vec_in)

    def body(*refs):
        a_refs, b_ref, rest = refs[:n_a], refs[n_a], refs[n_a + 1:]
        if n_a == 1:
            c = _dot(a_refs[0][...], b_ref[...], dims)
        else:
            c = None
            for a_ref, (piece, off) in zip(a_refs, pieces):
                part = _dot(a_ref[...], b_ref[off:off + piece.shape[1], :], dims)
                c = part if c is None else c + part
        epilogue(c, pl.program_id(0) == 0, rest[:n_row], rest[n_row:n_row + n_vec], rest[n_row + n_vec:])

    def spec(kind, width):
        block = {"row": (tm, width), "vec": (1, width), "one": (1, 1)}[kind]
        return pl.BlockSpec(block, (lambda i: (i, 0)) if kind == "row" else (lambda i: (0, 0)))

    def shape(kind, width):
        return {"row": (m, width), "vec": (1, width), "one": (1, 1)}[kind]

    outs = [(o[0], o[1], o[2] if len(o) > 2 else n) for o in outs]
    res, comm_outs = _pcall(
        body, [*[p for p, _ in pieces], b, *row_in, *vec_in], name=name, grid=(m // tm,),
        in_specs=[spec("row", p.shape[1]) for p, _ in pieces] + [pl.BlockSpec(b.shape, lambda i: (0, 0))]
        + [spec("row", r.shape[1]) for r in row_in] + [spec("vec", v.shape[1]) for v in vec_in],
        out_specs=[spec(kind, width) for kind, _, width in outs],
        out_shape=[jax.ShapeDtypeStruct(shape(kind, width), dt) for kind, dt, width in outs],
        sem=("arbitrary",), comm=comm)
    return res if comm is None else (res, comm_outs)


def _residual_norm_epilogue(mix, first, rows, vecs, outs):
    (x_ref,), (gate_ref, nw_ref, sc_ref, sh_ref), (mix_ref, x1_ref, h_ref) = rows, vecs, outs
    xv = x_ref[...] + gate_ref[...] * mix
    mix_ref[...] = mix
    x1_ref[...] = xv
    h_ref[...] = _rms_mod(xv, nw_ref[...], sc_ref[...], sh_ref[...])


def _loss_epilogue(ff, first, rows, vecs, outs):
    (x1_ref, t_ref), (g_ref,), (loss_ref, dout_ref, dff_ref, dg_ref) = rows, vecs, outs
    d = ff.shape[1]

    @pl.when(first)
    def _():
        loss_ref[...] = jnp.zeros_like(loss_ref)
        dg_ref[...] = jnp.zeros_like(dg_ref)

    err = x1_ref[...] + g_ref[...] * ff - t_ref[...]
    loss_ref[...] += (0.5 / d) * jnp.sum(err * err).reshape(1, 1)
    dout = err * (1.0 / d)
    dout_ref[...] = dout
    dff_ref[...] = (g_ref[...] * dout).astype(BF16)
    dg_ref[...] += jnp.sum(dout * ff, axis=0, keepdims=True)


def _norm_bwd_epilogue(dh, first, rows, vecs, outs):
    with_gate = len(vecs) == 3
    x_ref, dres_ref = rows[:2]
    nw_ref, sc_ref = vecs[:2]
    dx_ref, dsh_ref, dsc_ref, dnw_ref = outs[:4]

    @pl.when(first)
    def _():
        for ref in outs[1:4] + outs[5:]:
            ref[...] = jnp.zeros_like(ref)

    xv = x_ref[...]
    r = lax.rsqrt(jnp.mean(xv * xv, axis=-1, keepdims=True) + EPS)
    nrm = xv * r
    one_sc = 1.0 + sc_ref[...]
    dhn = dh * nrm
    dsh_ref[...] += jnp.sum(dh, axis=0, keepdims=True)
    dsc_ref[...] += jnp.sum(dhn, axis=0, keepdims=True) * nw_ref[...]
    dnw_ref[...] += jnp.sum(dhn, axis=0, keepdims=True) * one_sc
    dn = dh * (nw_ref[...] * one_sc)
    dx = dres_ref[...] + r * (dn - nrm * jnp.mean(dn * nrm, axis=-1, keepdims=True))
    dx_ref[...] = dx
    if with_gate:
        outs[4][...] = (vecs[2][...] * dx).astype(BF16)
        outs[5][...] += jnp.sum(dx * rows[2][...], axis=0, keepdims=True)


CONV_COLS = 256
CONV_FWD_ROWS = 2048
CONV_BWD_ROWS = 1024
CONV_SUB_ROWS = 128
HALO = 8


def _shift_down(cur, halo, k):
    if k == 0:
        return cur
    rolled = pltpu.roll(cur, k, axis=0)
    top = jnp.where(lax.broadcasted_iota(jnp.int32, halo.shape, 0) < k, pltpu.roll(halo, k, axis=0), rolled[:HALO])
    return jnp.concatenate([top, rolled[HALO:]], axis=0)


def _shift_up(cur, halo, k):
    if k == 0:
        return cur
    t = cur.shape[0]
    rolled = pltpu.roll(cur, t - k, axis=0)
    bot = jnp.where(lax.broadcasted_iota(jnp.int32, halo.shape, 0) >= HALO - k, pltpu.roll(halo, HALO - k, axis=0),
                    rolled[t - HALO:])
    return jnp.concatenate([rolled[:t - HALO], bot], axis=0)


def _conv_fwd(proj, conv_w, conv_b):
    s = proj.shape[0]
    nr = s // CONV_FWD_ROWS
    cb0 = OFF_XBC // CONV_COLS
    hb = CONV_FWD_ROWS // HALO
    cur = pl.BlockSpec((CONV_FWD_ROWS, CONV_COLS), lambda j, r: (r, cb0 + j))
    prev = pl.BlockSpec((HALO, CONV_COLS), lambda j, r: (jnp.maximum(r * hb - 1, 0), cb0 + j))
    out = pl.BlockSpec((CONV_FWD_ROWS, CONV_COLS), lambda j, r: (r, j))

    def body(u_ref, up_ref, w_ref, b_ref, pre_ref, act_ref):
        r = pl.program_id(1)
        for c in range(CONV_FWD_ROWS // CONV_SUB_ROWS):
            rows = slice(c * CONV_SUB_ROWS, (c + 1) * CONV_SUB_ROWS)
            u = u_ref[rows, :]
            halo = u_ref[c * CONV_SUB_ROWS - HALO:c * CONV_SUB_ROWS, :] if c > 0 else jnp.where(r > 0, up_ref[...], 0.0)
            acc = b_ref[...] + w_ref[CONV_K - 1:CONV_K, :] * u
            for k in range(1, CONV_K):
                acc = acc + w_ref[CONV_K - 1 - k:CONV_K - k, :] * _shift_down(u, halo, k)
            pre_ref[rows, :] = acc
            act_ref[rows, :] = acc * _sigmoid(acc)

    return pl.pallas_call(
        body, name="conv_fwd", grid=(CONV_CH // CONV_COLS, nr),
        in_specs=[cur, prev, pl.BlockSpec((CONV_K, CONV_COLS), lambda j, r: (0, j)),
                  pl.BlockSpec((1, CONV_COLS), lambda j, r: (0, j))],
        out_specs=[out, out],
        out_shape=[jax.ShapeDtypeStruct((s, CONV_CH), F32), jax.ShapeDtypeStruct((s, CONV_CH), F32)],
        compiler_params=_cparams(("parallel", "arbitrary")))(proj, proj, conv_w, conv_b)


def _conv_bwd(dact, pre, proj, conv_w):
    s = proj.shape[0]
    nr = s // CONV_BWD_ROWS
    cb0 = OFF_XBC // CONV_COLS
    hb = CONV_BWD_ROWS // HALO
    last_halo = s // HALO - 1
    n_sub = CONV_BWD_ROWS // CONV_SUB_ROWS
    cur = pl.BlockSpec((CONV_BWD_ROWS, CONV_COLS), lambda j, r: (r, j))
    nxt = pl.BlockSpec((HALO, CONV_COLS), lambda j, r: (jnp.minimum((r + 1) * hb, last_halo), j))
    ucur = pl.BlockSpec((CONV_BWD_ROWS, CONV_COLS), lambda j, r: (r, cb0 + j))
    wspec = pl.BlockSpec((CONV_K, CONV_COLS), lambda j, r: (0, j))
    bspec = pl.BlockSpec((1, CONV_COLS), lambda j, r: (0, j))

    def dsilu(p):
        sg = _sigmoid(p)
        return sg * (1.0 + p * (1.0 - sg))

    def body(da_ref, dan_ref, pre_ref, pren_ref, u_ref, w_ref, du_ref, dw_ref, db_ref):
        r = pl.program_id(1)

        @pl.when(r == 0)
        def _():
            dw_ref[...] = jnp.zeros_like(dw_ref)
            db_ref[...] = jnp.zeros_like(db_ref)

        dws = [jnp.zeros((1, CONV_COLS), F32) for _ in range(CONV_K)]
        db = jnp.zeros((1, CONV_COLS), F32)
        for c in range(n_sub):
            rows = slice(c * CONV_SUB_ROWS, (c + 1) * CONV_SUB_ROWS)
            ahead = slice((c + 1) * CONV_SUB_ROWS, (c + 1) * CONV_SUB_ROWS + HALO)
            dpre = da_ref[rows, :] * dsilu(pre_ref[rows, :])
            if c < n_sub - 1:
                dnext = da_ref[ahead, :] * dsilu(pre_ref[ahead, :])
            else:
                dnext = jnp.where(r < nr - 1, dan_ref[...] * dsilu(pren_ref[...]), 0.0)
            u = u_ref[rows, :]
            du = w_ref[CONV_K - 1:CONV_K, :] * dpre
            dws[0] = dws[0] + jnp.sum(dpre * u, axis=0, keepdims=True)
            for k in range(1, CONV_K):
                ahead_k = _shift_up(dpre, dnext, k)
                du = du + w_ref[CONV_K - 1 - k:CONV_K - k, :] * ahead_k
                dws[k] = dws[k] + jnp.sum(ahead_k * u, axis=0, keepdims=True)
            du_ref[rows, :] = du.astype(BF16)
            db = db + jnp.sum(dpre, axis=0, keepdims=True)
        dw_ref[...] += jnp.concatenate(dws[::-1], axis=0)
        db_ref[...] += db

    return pl.pallas_call(
        body, name="conv_bwd", grid=(CONV_CH // CONV_COLS, nr),
        in_specs=[cur, nxt, cur, nxt, ucur, wspec],
        out_specs=[cur, wspec, bspec],
        out_shape=[jax.ShapeDtypeStruct((s, CONV_CH), BF16), jax.ShapeDtypeStruct((CONV_K, CONV_CH), F32),
                   jax.ShapeDtypeStruct((1, CONV_CH), F32)],
        compiler_params=_cparams(("parallel", "arbitrary")))(dact, dact, pre, pre, proj, conv_w)


def _ssd_common(dtr, dtb, alog):
    lane = lax.broadcasted_iota(jnp.int32, (1, LANE), 1)
    head_lane = lane < SSD_HEADS
    dt = jnp.where(head_lane, _softplus(dtr + dtb), 0.0)
    a = jnp.where(head_lane, -jnp.exp(alog), 0.0)
    row = lax.broadcasted_iota(jnp.int32, (SSD_CHUNK, SSD_CHUNK), 0)
    col = lax.broadcasted_iota(jnp.int32, (SSD_CHUNK, SSD_CHUNK), 1)
    tril = row >= col
    cs = _dot(tril.astype(F32), dt * a, NN, precision=HIGHEST)
    return dt, a, cs, cs.T, tril, lane


def _split_bf16(v, passes):
    terms, rest = [], v
    for _ in range(passes):
        t = rest.astype(BF16)
        terms.append(t)
        rest = rest - t.astype(F32)
    return terms


def _dot_split(v, m, dims, passes):
    terms = _split_bf16(v, passes)
    if passes == 1:
        return _dot(terms[0], m, dims)
    return _dot(jnp.concatenate(terms, axis=1), jnp.concatenate([m] * passes, axis=0 if dims == NN else 1), dims)


def _ssd_constants():
    heads = jnp.arange(LANE)[:, None]
    exp_mat = (heads == (jnp.arange(SSD_D_INNER)[None, :] // HEAD_DIM)).astype(BF16)
    ind4 = ((jnp.arange(SSD_HEADS * SSD_CHUNK)[:, None] // SSD_CHUNK) == jnp.arange(LANE)[None, :]).astype(BF16)
    return exp_mat, ind4


def _expand_heads(v):
    return jnp.repeat(v[:, :SSD_HEADS], HEAD_DIM, axis=1)


def _ssd_prep(dtr, dtb, alog, exp_mat):
    dt, a, cs, cst, tril, lane = _ssd_common(dtr, dtb, alog)
    return dt, a, cs, cst, tril, lane, _dot_split(dt, exp_mat, NN, 2), _dot_split(cs, exp_mat, NN, 3)


def _chunk_decay_rows(cs, g):
    parts = []
    for e in range(HEADS_PER_GROUP):
        h = g * HEADS_PER_GROUP + e
        parts.append(jnp.broadcast_to(jnp.exp(cs[SSD_CHUNK - 1:SSD_CHUNK, h:h + 1]), (HEAD_DIM, SSD_STATE)))
    return jnp.concatenate(parts, axis=0)


def _ssd_fwd(proj, act, dtb, alog, dsk, nw):
    s = proj.shape[0]
    nc = s // SSD_CHUNK
    bc_w = SSD_GROUPS * SSD_STATE
    exp_mat, _ = _ssd_constants()

    def body(z_ref, dtr_ref, xs_ref, b_ref, c_ref, dtb_ref, alog_ref, dskx_ref, nw_ref, exp_ref,
             ypre_ref, yssd_ref, hall_ref, h_scr):
        @pl.when(pl.program_id(0) == 0)
        def _():
            h_scr[...] = jnp.zeros_like(h_scr)

        dt, a, cs, cst, tril, lane, dtx, csx = _ssd_prep(dtr_ref[...], dtb_ref[...], alog_ref[...], exp_ref[...])
        cs_last_x = csx[SSD_CHUNK - 1:SSD_CHUNK, :]
        xs = xs_ref[...]
        xdt = xs * dtx
        xdtb = xdt.astype(BF16)
        xdec = (xdt * jnp.exp(cs_last_x - csx)).astype(BF16)
        ecsx = jnp.exp(csx)
        head_of_lane = lax.broadcasted_iota(jnp.int32, (1, GROUP_WIDTH), 1) // HEAD_DIM
        for g in range(SSD_GROUPS):
            gs = slice(g * GROUP_WIDTH, (g + 1) * GROUP_WIDTH)
            bg = b_ref[:, g * SSD_STATE:(g + 1) * SSD_STATE].astype(BF16)
            cg = c_ref[:, g * SSD_STATE:(g + 1) * SSD_STATE].astype(BF16)
            cb = _dot(cg, bg, NT)
            hprev = h_scr[gs, :]
            hall_ref[0, gs, :] = hprev
            gms, rhs = [], []
            xg = xdtb[:, gs]
            for e in range(HEADS_PER_GROUP):
                h = g * HEADS_PER_GROUP + e
                lm = jnp.exp(jnp.where(tril, cs[:, h:h + 1] - cst[h:h + 1, :], -1e30))
                gms.append((cb * lm).astype(BF16))
                rhs.append(jnp.where(head_of_lane == e, xg, jnp.zeros_like(xg)))
            y = _dot(jnp.concatenate(gms, axis=1), jnp.concatenate(rhs, axis=0), NN)
            y = y + ecsx[:, gs] * _dot(cg, hprev.astype(BF16), NT)
            y = y + dskx_ref[:, gs] * xs[:, gs]
            h_scr[gs, :] = hprev * _chunk_decay_rows(cs, g) + _dot(xdec[:, gs], bg, TN)
            ypre_ref[:, gs] = y
            z = z_ref[:, gs]
            yg = y * (z * _sigmoid(z))
            r = lax.rsqrt(jnp.mean(yg * yg, axis=-1, keepdims=True) + EPS)
            yssd_ref[:, gs] = (yg * r * nw_ref[:, gs]).astype(BF16)

    row_d = lambda cb: pl.BlockSpec((SSD_CHUNK, SSD_D_INNER), lambda c: (c, cb))
    small = pl.BlockSpec((1, LANE), lambda c: (0, 0))
    wide = pl.BlockSpec((1, SSD_D_INNER), lambda c: (0, 0))
    return pl.pallas_call(
        body, name="ssd_fwd", grid=(nc,),
        in_specs=[row_d(OFF_Z // SSD_D_INNER),
                  pl.BlockSpec((SSD_CHUNK, LANE), lambda c: (c, OFF_DT // LANE)),
                  row_d(0),
                  pl.BlockSpec((SSD_CHUNK, bc_w), lambda c: (c, SSD_D_INNER // bc_w)),
                  pl.BlockSpec((SSD_CHUNK, bc_w), lambda c: (c, SSD_D_INNER // bc_w + 1)),
                  small, small, wide, wide, pl.BlockSpec((LANE, SSD_D_INNER), lambda c: (0, 0))],
        out_specs=[row_d(0), row_d(0), pl.BlockSpec((1, SSD_D_INNER, SSD_STATE), lambda c: (c, 0, 0))],
        out_shape=[jax.ShapeDtypeStruct((s, SSD_D_INNER), F32), jax.ShapeDtypeStruct((s, SSD_D_INNER + ATT_D), BF16),
                   jax.ShapeDtypeStruct((nc, SSD_D_INNER, SSD_STATE), F32)],
        scratch_shapes=[pltpu.VMEM((SSD_D_INNER, SSD_STATE), F32)],
        compiler_params=_cparams(("arbitrary",)))(proj, proj, act, act, act, dtb, alog, _expand_heads(dsk), nw, exp_mat)


def _ssd_bwd(dycat, ypre, proj, act, hall, dtb, alog, dsk, nw, comm=None):
    s = proj.shape[0]
    nc = s // SSD_CHUNK
    bc_w = SSD_GROUPS * SSD_STATE

    exp_mat, ind4 = _ssd_constants()
    seg_passes = 1

    def body(dy_ref, ypre_ref, z_ref, dtr_ref, xs_ref, b_ref, c_ref, hall_ref, dtb_ref, alog_ref, dskx_ref, nw_ref,
             exp_ref, ind4_ref, dz_ref, dact_ref, ddtr_ref, da_ref, ddsk_ref, ddtb_ref, dnw_ref, dh_scr):
        @pl.when(pl.program_id(0) == 0)
        def _():
            dh_scr[...] = jnp.zeros_like(dh_scr)
            da_ref[...] = jnp.zeros_like(da_ref)
            ddsk_ref[...] = jnp.zeros_like(ddsk_ref)
            ddtb_ref[...] = jnp.zeros_like(ddtb_ref)
            dnw_ref[...] = jnp.zeros_like(dnw_ref)

        dtr = dtr_ref[...]
        dt, a, cs, cst, tril, lane, dtx, csx = _ssd_prep(dtr, dtb_ref[...], alog_ref[...], exp_ref[...])
        cs_last_x = csx[SSD_CHUNK - 1:SSD_CHUNK, :]
        xs = xs_ref[...]
        xdt = xs * dtx
        xdtb = xdt.astype(BF16)
        decx = jnp.exp(cs_last_x - csx)
        xdecf = xdt * decx
        xdec = xdecf.astype(BF16)
        ecsx = jnp.exp(csx)
        head_of_lane = lax.broadcasted_iota(jnp.int32, (1, GROUP_WIDTH), 1) // HEAD_DIM
        last_row = lax.broadcasted_iota(jnp.int32, (SSD_CHUNK, 1), 0) == SSD_CHUNK - 1
        dcs_col = jnp.zeros((SSD_CHUNK, LANE), F32)
        dcs_row = jnp.zeros((SSD_CHUNK, LANE), F32)
        ddt = jnp.zeros((SSD_CHUNK, LANE), F32)
        ddsk = jnp.zeros((1, LANE), F32)
        hsum = jnp.zeros((1, LANE), F32)
        t1_sum = jnp.zeros((1, LANE), F32)
        for g in range(SSD_GROUPS):
            gs = slice(g * GROUP_WIDTH, (g + 1) * GROUP_WIDTH)
            bsl = slice(g * SSD_STATE, (g + 1) * SSD_STATE)
            exp_g = exp_ref[:, gs]
            ind4_g = ind4_ref[g * HEADS_PER_GROUP * SSD_CHUNK:(g + 1) * HEADS_PER_GROUP * SSD_CHUNK, :]
            z = z_ref[:, gs]
            sg = _sigmoid(z)
            sz = z * sg
            ypre = ypre_ref[:, gs]
            yg = ypre * sz
            r = lax.rsqrt(jnp.mean(yg * yg, axis=-1, keepdims=True) + EPS)
            nrm = yg * r
            dyo_n = dy_ref[:, gs]
            dnw_ref[:, gs] += jnp.sum(dyo_n * nrm, axis=0, keepdims=True)
            dn = dyo_n * nw_ref[:, gs]
            dyg = r * (dn - nrm * jnp.mean(dn * nrm, axis=-1, keepdims=True))
            dz_ref[:, gs] = (dyg * ypre * (sg * (1.0 + z * (1.0 - sg)))).astype(BF16)
            dy = dyg * sz

            bg = b_ref[:, bsl].astype(BF16)
            cg = c_ref[:, bsl].astype(BF16)
            cb = _dot(cg, bg, NT)
            hprev = hall_ref[0, gs, :]
            hb = hprev.astype(BF16)
            dhn = dh_scr[gs, :]
            dhb = dhn.astype(BF16)
            xs_g, xdt_g = xs[:, gs], xdtb[:, gs]
            w_off = _dot(cg, hb, NT)
            dyo = dy * ecsx[:, gs]
            dyob = dyo.astype(BF16)
            dcg = _dot(dyob, hb, NN)
            dh_y = _dot(dyob, cg, TN)
            r_st = _dot(bg, dhb, NT)
            dbg = _dot(xdec[:, gs], dhb, NN)
            dyb = dy.astype(BF16)
            gms, gmbs, lms, dys = [], [], [], []
            for e in range(HEADS_PER_GROUP):
                h = g * HEADS_PER_GROUP + e
                lm = jnp.exp(jnp.where(tril, cs[:, h:h + 1] - cst[h:h + 1, :], -1e30))
                gm = cb * lm
                lms.append(lm)
                gms.append(gm)
                gmbs.append(gm.astype(BF16))
                dys.append(jnp.where(head_of_lane == e, dyb, jnp.zeros_like(dyb)))
            dxdt = _dot(jnp.concatenate(gmbs, axis=0), jnp.concatenate(dys, axis=0), TN) + decx[:, gs] * r_st
            dcb = jnp.zeros((SSD_CHUNK, SSD_CHUNK), F32)
            mms = []
            for e in range(HEADS_PER_GROUP):
                dg = _dot(dys[e], xdt_g, NT)
                mms.append(dg * gms[e])
                dcb = dcb + dg * lms[e]
            seg = _dot_split(jnp.concatenate([dyo * w_off, xdecf[:, gs] * r_st, dxdt * xs_g, dy * xs_g], axis=0), exp_g, NT, seg_passes)
            v1, t1, ddt_g, dsk_g = [seg[i * SSD_CHUNK:(i + 1) * SSD_CHUNK] for i in range(4)]
            dcs_col = dcs_col + v1 - t1 + _dot_split(jnp.concatenate(mms, axis=1), ind4_g, NN, seg_passes)
            for t in _split_bf16(jnp.concatenate(mms, axis=0), seg_passes):
                dcs_row = dcs_row + _dot(ind4_g, t, TN)
            ddt = ddt + ddt_g
            ddsk = ddsk + jnp.sum(dsk_g, axis=0, keepdims=True)
            t1_sum = t1_sum + jnp.sum(t1, axis=0, keepdims=True)
            for e in range(HEADS_PER_GROUP):
                h = g * HEADS_PER_GROUP + e
                hs = slice(e * HEAD_DIM, (e + 1) * HEAD_DIM)
                hsum = hsum + jnp.where(lane == h, jnp.sum(dhn[hs, :] * hprev[hs, :]).reshape(1, 1), 0.0)
            dh_scr[gs, :] = dhn * _chunk_decay_rows(cs, g) + dh_y
            dcbb = dcb.astype(BF16)
            dact_ref[:, gs] = dxdt * dtx[:, gs] + dskx_ref[:, gs] * dy
            dact_ref[:, SSD_D_INNER + g * SSD_STATE:SSD_D_INNER + (g + 1) * SSD_STATE] = dbg + _dot(dcbb, cg, TN)
            dact_ref[:, SSD_D_INNER + bc_w + g * SSD_STATE:SSD_D_INNER + bc_w + (g + 1) * SSD_STATE] = dcg + _dot(dcbb, bg, NN)
        dlast = t1_sum + jnp.exp(cs[SSD_CHUNK - 1:SSD_CHUNK, :]) * hsum
        dcs = dcs_col - dcs_row.T + jnp.where(last_row, dlast, 0.0)
        row = lax.broadcasted_iota(jnp.int32, (SSD_CHUNK, SSD_CHUNK), 0)
        col = lax.broadcasted_iota(jnp.int32, (SSD_CHUNK, SSD_CHUNK), 1)
        dda = _dot((col >= row).astype(F32), dcs, NN, precision=HIGHEST)
        ddt = ddt + dda * a
        da_ref[...] += jnp.sum(dda * dt, axis=0, keepdims=True)
        ddtr = jnp.where(lane < SSD_HEADS, ddt * _sigmoid(dtr + dtb_ref[...]), 0.0)
        ddtr_ref[...] = ddtr.astype(BF16)
        ddtb_ref[...] += jnp.sum(ddtr, axis=0, keepdims=True)
        ddsk_ref[...] += ddsk

    rev = lambda c: nc - 1 - c
    row_d = lambda cb: pl.BlockSpec((SSD_CHUNK, SSD_D_INNER), lambda c: (rev(c), cb))
    small = pl.BlockSpec((1, LANE), lambda c: (0, 0))
    wide = pl.BlockSpec((1, SSD_D_INNER), lambda c: (0, 0))
    small_shape = jax.ShapeDtypeStruct((1, LANE), F32)
    return _pcall(
        body, (dycat, ypre, proj, proj, act, act, act, hall, dtb, alog, _expand_heads(dsk), nw, exp_mat, ind4),
        name="ssd_bwd", grid=(nc,),
        in_specs=[row_d(0), row_d(0), row_d(OFF_Z // SSD_D_INNER),
                  pl.BlockSpec((SSD_CHUNK, LANE), lambda c: (rev(c), OFF_DT // LANE)),
                  row_d(0),
                  pl.BlockSpec((SSD_CHUNK, bc_w), lambda c: (rev(c), SSD_D_INNER // bc_w)),
                  pl.BlockSpec((SSD_CHUNK, bc_w), lambda c: (rev(c), SSD_D_INNER // bc_w + 1)),
                  pl.BlockSpec((1, SSD_D_INNER, SSD_STATE), lambda c: (rev(c), 0, 0)),
                  small, small, wide, wide, pl.BlockSpec((LANE, SSD_D_INNER), lambda c: (0, 0)),
                  pl.BlockSpec((SSD_HEADS * SSD_CHUNK, LANE), lambda c: (0, 0))],
        out_specs=[row_d(0), pl.BlockSpec((SSD_CHUNK, CONV_CH), lambda c: (rev(c), 0)),
                   pl.BlockSpec((SSD_CHUNK, LANE), lambda c: (rev(c), 0)), small, small, small, wide],
        out_shape=[jax.ShapeDtypeStruct((s, SSD_D_INNER), BF16), jax.ShapeDtypeStruct((s, CONV_CH), F32),
                   jax.ShapeDtypeStruct((s, LANE), BF16), small_shape, small_shape, small_shape,
                   jax.ShapeDtypeStruct((1, SSD_D_INNER), F32)],
        scratch_shapes=[pltpu.VMEM((SSD_D_INNER, SSD_STATE), F32)], sem=("arbitrary",), comm=comm)


def _head_mean_matrix():
    row = lax.broadcasted_iota(jnp.int32, (LANE, LANE), 0) // HEAD_DIM
    col = lax.broadcasted_iota(jnp.int32, (LANE, LANE), 1) // HEAD_DIM
    return (row == col).astype(F32)


def _head_sum2(v, ones_bd):
    hi = v.astype(BF16)
    lo = (v - hi.astype(F32)).astype(BF16)
    return _dot(jnp.concatenate([hi, lo], axis=1), jnp.concatenate([ones_bd, ones_bd], axis=0), NN)


def _head_norms(xs, ws, ones_bd):
    sums = [_head_sum2(x * x, ones_bd) for x in xs]
    rs = [lax.rsqrt(ms * (1.0 / HEAD_DIM) + EPS) for ms in sums]
    return [(x * r) * w for x, r, w in zip(xs, rs, ws)], rs


def _head_norms_bwd(dns, xs, ws, rs, ones_bd):
    nrms = [x * r for x, r in zip(xs, rs)]
    dnws = [dn * w for dn, w in zip(dns, ws)]
    projs = [_head_sum2(dnw * nrm, ones_bd) for dnw, nrm in zip(dnws, nrms)]
    dxs = [r * (dnw - nrm * (pr * (1.0 / HEAD_DIM))) for r, dnw, nrm, pr in zip(rs, dnws, nrms, projs)]
    return dxs, [jnp.sum(dn * nrm, axis=0, keepdims=True) for dn, nrm in zip(dns, nrms)]


NORM_CHUNKS = 2


PRO_ROWS = 256
ATT_GROUP_FWD = 32
ATT_GROUP_BWD = 8
KEYS = 2 * ATT_BLK
NEG = -1e30
HALF = HEAD_DIM // 2


def _rows(start, size, dil):
    return pl.ds(start, size) if dil == 1 else pl.ds(start, size, stride=dil)


def _fill_bias(bias_ref):
    row = lax.broadcasted_iota(jnp.int32, (ATT_BLK, 2 * KEYS), 0)
    col = lax.broadcasted_iota(jnp.int32, (ATT_BLK, 2 * KEYS), 1) & (KEYS - 1)
    for first, off in ((0, 0), (1, ATT_BLK)):
        dist = off + row - col
        bias_ref[first] = jnp.where((dist >= 0) & (dist <= ATT_BLK), 0.0, NEG)


def _pair(a, b):
    return jnp.concatenate([jnp.broadcast_to(a, (ATT_BLK, KEYS)), jnp.broadcast_to(b, (ATT_BLK, KEYS))], axis=1)


def _split_heads(x, is_a):
    zero = jnp.zeros_like(x)
    return jnp.concatenate([jnp.where(is_a, x, zero), jnp.where(is_a, zero, x)], axis=0)


def _block_ids(b, nb):
    i = b & (nb - 1)
    q0 = pl.multiple_of(b * ATT_BLK, ATT_BLK)
    k0 = pl.multiple_of((b - jnp.minimum(i, 1)) * ATT_BLK, ATT_BLK)
    return pl.ds(q0, ATT_BLK), pl.ds(k0, KEYS), jnp.minimum(i, 1)


def _natural_rows(b, nb, dil):
    if dil == 1:
        return pl.ds(pl.multiple_of(b * ATT_BLK, ATT_BLK), ATT_BLK)
    return pl.ds(b // nb + dil * ((b & (nb - 1)) * ATT_BLK), ATT_BLK, stride=dil)


def _att_fwd(proj, qw, kw, comm=None):
    s = proj.shape[0]
    nblk = s // ATT_BLK
    assert all((s // d) // ATT_BLK >= 2 for d in DILATIONS)
    blk = lambda off: pl.BlockSpec((s, LANE), lambda i: (0, off // LANE + i))
    wspec = pl.BlockSpec((1, LANE), lambda i: (0, i))
    oblk = pl.BlockSpec((s, LANE), lambda i: (0, i))

    def body(q_ref, k_ref, v_ref, qw_ref, kw_ref, o_ref, lse_ref, qn, kn, q_cm, k_cm, v_cm, m_acc, l_acc, o_d, m_d, l_d,
             o_e, m_e, l_e, tq, tk, tv, bias):
        ones_bd = _head_mean_matrix().astype(BF16)
        is_a = lax.broadcasted_iota(jnp.int32, (1, LANE), 1) < HEAD_DIM
        ones_ext = _split_heads(jnp.ones((KEYS, LANE), BF16), is_a)
        _fill_bias(bias)

        def pro(j, c):
            chunks = [pl.ds(pl.multiple_of((NORM_CHUNKS * j + u) * PRO_ROWS, PRO_ROWS), PRO_ROWS) for u in range(NORM_CHUNKS)]
            normed, _ = _head_norms([q_ref[rows, :] for rows in chunks] + [k_ref[rows, :] for rows in chunks],
                                    [qw_ref[...] * HEAD_DIM ** -0.5] * NORM_CHUNKS + [kw_ref[...]] * NORM_CHUNKS, ones_bd)
            for u, rows in enumerate(chunks):
                qn[rows, :] = normed[u]
                kn[rows, :] = normed[NORM_CHUNKS + u]
            return c

        lax.fori_loop(0, s // (NORM_CHUNKS * PRO_ROWS), pro, 0)

        results = dict(zip(DILATIONS, ((o_ref, m_acc, l_acc), (o_d, m_d, l_d), (o_e, m_e, l_e))))
        for dil in DILATIONS:
            ln = s // dil
            nb = ln // ATT_BLK
            o_out, m_out, l_out = results[dil]
            level = DILATIONS.index(dil)
            keep_f32 = 0 < level < len(DILATIONS) - 1
            from_temps = level >= 2
            step_rows = dil // DILATIONS[level - 1] if from_temps else dil
            for r in range(dil):
                prev = DILATIONS[level - 1] if from_temps else 1
                start = (r % prev) * (s // prev) + r // prev if from_temps else r

                def relayout(j, c, r=r, ln=ln, start=start, step_rows=step_rows, keep_f32=keep_f32, from_temps=from_temps):
                    j0 = pl.multiple_of(j * PRO_ROWS, PRO_ROWS)
                    src = _rows(start + step_rows * j0, PRO_ROWS, step_rows)
                    dst = pl.ds(r * ln + j0, PRO_ROWS)
                    qv, kv, vv = (tq[src, :], tk[src, :], tv[src, :]) if from_temps else (qn[src, :], kn[src, :], v_ref[src, :])
                    q_cm[dst, :] = qv.astype(BF16)
                    k_cm[dst, :] = kv.astype(BF16)
                    v_cm[dst, :] = vv.astype(BF16)
                    if keep_f32:
                        tq[dst, :] = qv
                        tk[dst, :] = kv
                        tv[dst, :] = vv
                    return c

                lax.fori_loop(0, ln // PRO_ROWS, relayout, 0)

            def step(bg, c, nb=nb, o_out=o_out, m_out=m_out, l_out=l_out):
                ids = [_block_ids(bg * ATT_GROUP_FWD + u, nb) for u in range(ATT_GROUP_FWD)]
                kbs = [_split_heads(k_cm[krows, :], is_a) for _, krows, _ in ids]
                scs = [_dot(q_cm[qrows, :], kb, NT) + bias[first] for (qrows, _, first), kb in zip(ids, kbs)]
                mas = [jnp.max(sc[:, :KEYS], axis=-1, keepdims=True) for sc in scs]
                mbs = [jnp.max(sc[:, KEYS:], axis=-1, keepdims=True) for sc in scs]
                ps = [jnp.exp(sc - _pair(ma, mb)).astype(BF16) for sc, ma, mb in zip(scs, mas, mbs)]
                vbs = [jnp.concatenate([_split_heads(v_cm[krows, :], is_a), ones_ext], axis=1) for _, krows, _ in ids]
                ols = [_dot(p, vb, NN) for p, vb in zip(ps, vbs)]
                for (qrows, _, _), ol, ma, mb in zip(ids, ols, mas, mbs):
                    o_out[qrows, :] = ol[:, :LANE]
                    l_out[qrows, :] = ol[:, LANE:]
                    m_out[qrows, :] = jnp.where(is_a, ma, mb)
                return c

            lax.fori_loop(0, nblk // ATT_GROUP_FWD, step, 0)

        for level in range(len(DILATIONS) - 1, 0, -1):
            fine_d, coarse_d = DILATIONS[level - 1], DILATIONS[level]
            ratio, ln_f, ln_c = coarse_d // fine_d, s // fine_d, s // coarse_d
            (o_f, m_f, l_f), (o_c, m_c, l_c) = results[fine_d], results[coarse_d]
            for r in range(coarse_d):
                def merge(j, c, r=r, ratio=ratio, ln_c=ln_c, start=(r % fine_d) * ln_f + r // fine_d,
                          o_f=o_f, m_f=m_f, l_f=l_f, o_c=o_c, m_c=m_c, l_c=l_c):
                    j0 = pl.multiple_of(j * PRO_ROWS, PRO_ROWS)
                    fine = _rows(start + ratio * j0, PRO_ROWS, ratio)
                    coarse = pl.ds(r * ln_c + j0, PRO_ROWS)
                    m_old, m_new = m_f[fine, :], m_c[coarse, :]
                    m = jnp.maximum(m_old, m_new)
                    a_old, a_new = jnp.exp(m_old - m), jnp.exp(m_new - m)
                    o_f[fine, :] = a_old * o_f[fine, :] + a_new * o_c[coarse, :]
                    l_f[fine, :] = a_old * l_f[fine, :] + a_new * l_c[coarse, :]
                    m_f[fine, :] = m
                    return c

                lax.fori_loop(0, ln_c // PRO_ROWS, merge, 0)

        def epi(j, c):
            rows = pl.ds(pl.multiple_of(j * PRO_ROWS, PRO_ROWS), PRO_ROWS)
            l = l_acc[rows, :]
            o_ref[rows, :] = o_ref[rows, :] / l
            lse_ref[rows, :] = m_acc[rows, :] + jnp.log(l)
            return c

        lax.fori_loop(0, s // PRO_ROWS, epi, 0)

    f = jax.ShapeDtypeStruct((s, ATT_D), F32)
    scr = pltpu.VMEM((s, LANE), F32)
    scb = pltpu.VMEM((s, LANE), BF16)
    return _pcall(
        body, (proj, proj, proj, qw, kw), name="att_fwd", grid=(ATT_D // LANE,),
        in_specs=[blk(OFF_Q), blk(OFF_K), blk(OFF_V), wspec, wspec], out_specs=[oblk, oblk], out_shape=[f, f],
        scratch_shapes=[scr, scr, scb, scb, scb] + [scr] * 11 + [pltpu.VMEM((2, ATT_BLK, 2 * KEYS), F32)],
        sem=("parallel",), comm=comm)


def _att_bwd(proj, do, stats, qw, kw, comm=None):
    s = proj.shape[0]
    nblk = s // ATT_BLK
    blk = lambda off: pl.BlockSpec((s, LANE), lambda i: (0, off // LANE + i))
    wspec = pl.BlockSpec((1, LANE), lambda i: (0, i))
    oblk = pl.BlockSpec((s, LANE), lambda i: (0, i))

    def body(q_ref, k_ref, v_ref, do_ref, st_ref, qw_ref, kw_ref, dq_ref, dk_ref, dv_ref, dqw_ref, dkw_ref,
             qn, kn, q_cm, do_cm, k_cm, v_cm, rms, dq_acc, dk_acc, dv_acc, dq_d, dk_d, dv_d, dq_e, dk_e, dv_e, bias):
        ones_bd = _head_mean_matrix().astype(BF16)
        is_a = lax.broadcasted_iota(jnp.int32, (1, LANE), 1) < HEAD_DIM
        first_half = (lax.broadcasted_iota(jnp.int32, (1, LANE), 1) & (HEAD_DIM - 1)) < HALF
        _fill_bias(bias)
        zero = jnp.zeros((PRO_ROWS, LANE), F32)
        results = dict(zip(DILATIONS, ((dq_acc, dk_acc, dv_acc), (dq_d, dk_d, dv_d), (dq_e, dk_e, dv_e))))

        def pro(j, c):
            chunks = [pl.ds(pl.multiple_of((NORM_CHUNKS * j + u) * PRO_ROWS, PRO_ROWS), PRO_ROWS) for u in range(NORM_CHUNKS)]
            normed, rs = _head_norms([q_ref[rows, :] for rows in chunks] + [k_ref[rows, :] for rows in chunks],
                                     [qw_ref[...] * HEAD_DIM ** -0.5] * NORM_CHUNKS + [kw_ref[...]] * NORM_CHUNKS, ones_bd)
            for u, rows in enumerate(chunks):
                qn[rows, :] = normed[u]
                kn[rows, :] = normed[NORM_CHUNKS + u]
                rms[rows, :] = jnp.where(first_half, rs[u], rs[NORM_CHUNKS + u])
                dk_acc[rows, :] = zero
                dv_acc[rows, :] = zero
            return c

        lax.fori_loop(0, s // (NORM_CHUNKS * PRO_ROWS), pro, 0)

        for dil in DILATIONS:
            ln = s // dil
            nb = ln // ATT_BLK
            dq_o, dk_o, dv_o = results[dil]
            level = DILATIONS.index(dil)
            keep_f32 = 0 < level < len(DILATIONS) - 1
            from_temps = level >= 2
            temps = results[DILATIONS[-1]]
            for r in range(dil):
                prev = DILATIONS[level - 1] if from_temps else 1
                start = (r % prev) * (s // prev) + r // prev if from_temps else r

                def relayout(j, c, dil=dil, r=r, ln=ln, start=start, step_rows=dil // prev):
                    j0 = pl.multiple_of(j * PRO_ROWS, PRO_ROWS)
                    nat = _rows(r + dil * j0, PRO_ROWS, dil)
                    src = _rows(start + step_rows * j0, PRO_ROWS, step_rows)
                    dst = pl.ds(r * ln + j0, PRO_ROWS)
                    qv, kv, vv = [t[src, :] for t in temps] if from_temps else (qn[src, :], kn[src, :], v_ref[src, :])
                    q_cm[dst, :] = qv.astype(BF16)
                    k_cm[dst, :] = kv.astype(BF16)
                    v_cm[dst, :] = vv.astype(BF16)
                    do_cm[dst, :] = do_ref[nat, :].astype(BF16)
                    if keep_f32:
                        for t, val in zip(temps, (qv, kv, vv)):
                            t[dst, :] = val
                    return c

                lax.fori_loop(0, ln // PRO_ROWS, relayout, 0)

            if dil > 1:
                def clear(j, c, dk_o=dk_o, dv_o=dv_o):
                    rows = pl.ds(pl.multiple_of(j * PRO_ROWS, PRO_ROWS), PRO_ROWS)
                    dk_o[rows, :] = zero
                    dv_o[rows, :] = zero
                    return c

                lax.fori_loop(0, s // PRO_ROWS, clear, 0)

            def step(bg, c, nb=nb, dil=dil, dq_o=dq_o, dk_o=dk_o, dv_o=dv_o):
                blocks = [bg * ATT_GROUP_BWD + u for u in range(ATT_GROUP_BWD)]
                ids = [_block_ids(b, nb) for b in blocks]
                qbs = [q_cm[qrows, :] for qrows, _, _ in ids]
                dobs = [do_cm[qrows, :] for qrows, _, _ in ids]
                kbs = [_split_heads(k_cm[krows, :], is_a) for _, krows, _ in ids]
                vbs = [_split_heads(v_cm[krows, :], is_a) for _, krows, _ in ids]
                sts = [st_ref[_natural_rows(b, nb, dil), :] for b in blocks]
                scs = [_dot(qb, kb, NT) + bias[first] for qb, kb, (_, _, first) in zip(qbs, kbs, ids)]
                dps = [_dot(dob, vb, NT) for dob, vb in zip(dobs, vbs)]
                ps = [jnp.exp(sc - _pair(st[:, 0:1], st[:, HEAD_DIM:HEAD_DIM + 1])) for sc, st in zip(scs, sts)]
                dss = [(p * (dp - _pair(st[:, HALF:HALF + 1], st[:, HEAD_DIM + HALF:HEAD_DIM + HALF + 1]))).astype(BF16)
                       for p, dp, st in zip(ps, dps, sts)]
                dqs = [_dot(ds, kb, NN) for ds, kb in zip(dss, kbs)]
                dkfs = [_dot(ds, qb, TN) for ds, qb in zip(dss, qbs)]
                dvfs = [_dot(p.astype(BF16), dob, TN) for p, dob in zip(ps, dobs)]
                for (qrows, krows, _), dq, dkf, dvf in zip(ids, dqs, dkfs, dvfs):
                    dq_o[qrows, :] = dq
                    dk_o[krows, :] += jnp.where(is_a, dkf[:KEYS], dkf[KEYS:])
                    dv_o[krows, :] += jnp.where(is_a, dvf[:KEYS], dvf[KEYS:])
                return c

            lax.fori_loop(0, nblk // ATT_GROUP_BWD, step, 0)

        for level in range(len(DILATIONS) - 1, 0, -1):
            fine_d, coarse_d = DILATIONS[level - 1], DILATIONS[level]
            ratio, ln_f, ln_c = coarse_d // fine_d, s // fine_d, s // coarse_d
            for r in range(coarse_d):
                def merge(j, c, r=r, ratio=ratio, ln_c=ln_c, start=(r % fine_d) * ln_f + r // fine_d,
                          fine_bufs=results[fine_d], coarse_bufs=results[coarse_d]):
                    j0 = pl.multiple_of(j * PRO_ROWS, PRO_ROWS)
                    fine = _rows(start + ratio * j0, PRO_ROWS, ratio)
                    coarse = pl.ds(r * ln_c + j0, PRO_ROWS)
                    for f_buf, c_buf in zip(fine_bufs, coarse_bufs):
                        f_buf[fine, :] += c_buf[coarse, :]
                    return c

                lax.fori_loop(0, ln_c // PRO_ROWS, merge, 0)

        def epi(j, c):
            chunks = [pl.ds(pl.multiple_of((NORM_CHUNKS * j + u) * PRO_ROWS, PRO_ROWS), PRO_ROWS) for u in range(NORM_CHUNKS)]
            packed = [rms[rows, :] for rows in chunks]
            rs = ([jnp.where(first_half, p, pltpu.roll(p, HALF, axis=1)) for p in packed]
                  + [jnp.where(first_half, pltpu.roll(p, LANE - HALF, axis=1), p) for p in packed])
            dxs, dws = _head_norms_bwd(
                [dq_acc[rows, :] for rows in chunks] + [dk_acc[rows, :] for rows in chunks],
                [q_ref[rows, :] for rows in chunks] + [k_ref[rows, :] for rows in chunks],
                [qw_ref[...] * HEAD_DIM ** -0.5] * NORM_CHUNKS + [kw_ref[...]] * NORM_CHUNKS, rs, ones_bd)
            dqw, dkw = c
            for u, rows in enumerate(chunks):
                dq_ref[rows, :] = dxs[u].astype(BF16)
                dk_ref[rows, :] = dxs[NORM_CHUNKS + u].astype(BF16)
                dv_ref[rows, :] = dv_acc[rows, :].astype(BF16)
                dqw, dkw = dqw + dws[u], dkw + dws[NORM_CHUNKS + u]
            return dqw, dkw

        zrow = jnp.zeros((1, LANE), F32)
        dqw, dkw = lax.fori_loop(0, s // (NORM_CHUNKS * PRO_ROWS), epi, (zrow, zrow))
        dqw_ref[...] = dqw * HEAD_DIM ** -0.5
        dkw_ref[...] = dkw

    o = jax.ShapeDtypeStruct((s, ATT_D), BF16)
    ov = jax.ShapeDtypeStruct((1, ATT_D), F32)
    scr = pltpu.VMEM((s, LANE), F32)
    scb = pltpu.VMEM((s, LANE), BF16)
    return _pcall(
        body, (proj, proj, proj, do, stats, qw, kw), name="att_bwd", grid=(ATT_D // LANE,),
        in_specs=[blk(OFF_Q), blk(OFF_K), blk(OFF_V), oblk, oblk, wspec, wspec],
        out_specs=[oblk, oblk, oblk, wspec, wspec], out_shape=[o, o, o, ov, ov],
        scratch_shapes=[scr, scr, scb, scb, scb, scb] + [scr] * 10 + [pltpu.VMEM((2, ATT_BLK, 2 * KEYS), F32)],
        sem=("parallel",), comm=comm)


def _att_norm_fwd(o, nw, ycat):
    s = o.shape[0]
    row = pl.BlockSpec((ROW_TILE, ATT_D), lambda i: (i, 0))
    vec = pl.BlockSpec((1, ATT_D), lambda i: (0, 0))

    def body(o_ref, nw_ref, ycat_ref, y_ref):
        o = o_ref[...]
        r = lax.rsqrt(jnp.mean(o * o, axis=-1, keepdims=True) + EPS)
        y_ref[...] = (o * r * nw_ref[...]).astype(BF16)

    return pl.pallas_call(body, name="att_norm_fwd", grid=(s // ROW_TILE,),
                          in_specs=[row, vec, pl.BlockSpec(memory_space=pl.ANY)],
                          out_specs=pl.BlockSpec((ROW_TILE, ATT_D), lambda i: (i, 1)),
                          out_shape=jax.ShapeDtypeStruct(ycat.shape, BF16), input_output_aliases={2: 0},
                          compiler_params=_cparams(("parallel",)))(o, nw, ycat)


def _mixer_split_epilogue(dycat, first, rows, vecs, outs):
    (o_ref, lse_ref), (nw_ref,), (dyssd_ref, do_ref, st_ref, dnw_ref) = rows, vecs, outs

    @pl.when(first)
    def _():
        dnw_ref[...] = jnp.zeros_like(dnw_ref)

    dyssd_ref[...] = dycat[:, :SSD_D_INNER]
    dy = dycat[:, SSD_D_INNER:]
    o = o_ref[...]
    r = lax.rsqrt(jnp.mean(o * o, axis=-1, keepdims=True) + EPS)
    nrm = o * r
    dnw_ref[...] += jnp.sum(dy * nrm, axis=0, keepdims=True)
    dn = dy * nw_ref[...]
    do = r * (dn - nrm * jnp.mean(dn * nrm, axis=-1, keepdims=True))
    do_ref[...] = do
    ones_bd = _head_mean_matrix().astype(BF16)
    prod = do * o
    delta = jnp.concatenate([_head_sum2(prod[:, j * LANE:(j + 1) * LANE], ones_bd) for j in range(ATT_D // LANE)], axis=1)
    lane = lax.broadcasted_iota(jnp.int32, (1, ATT_D), 1)
    st_ref[...] = jnp.where((lane & (HEAD_DIM - 1)) < HALF, lse_ref[...], delta)


def _ada_fwd(c_all, w_ada):
    def body(c_ref, w_ref, o_ref):
        cv = c_ref[...]
        o_ref[...] = _dot((cv * _sigmoid(cv)).astype(BF16), w_ref[...].astype(BF16), NN)

    return pl.pallas_call(body, name="ada_fwd", out_shape=jax.ShapeDtypeStruct((c_all.shape[0], w_ada.shape[1]), F32),
                          compiler_params=_cparams())(c_all, w_ada)


def _adamw_math(g, w, m, v):
    m_new = ADAM_B1 * m + (1.0 - ADAM_B1) * g
    v_new = ADAM_B2 * v + (1.0 - ADAM_B2) * (g * g)
    m_hat = m_new / (1.0 - ADAM_B1 ** ADAM_STEP)
    v_hat = v_new / (1.0 - ADAM_B2 ** ADAM_STEP)
    delta = -ADAM_LR * (m_hat / (jnp.sqrt(v_hat) + ADAM_EPS) + ADAM_WD * w)
    return delta, m_new, v_new


def _ada_bwd_adamw(c_all, dmod_cols, w, m, v):
    rows, cols = w.shape
    tr = 256
    blk = pl.BlockSpec((tr, cols), lambda i: (i, 0))

    def body(c_ref, d_ref, w_ref, m_ref, v_ref, g_ref, dl_ref, mo_ref, vo_ref):
        cv = c_ref[...]
        ca = cv * _sigmoid(cv)
        g = ca[:, 0:1] * d_ref[0:1, :]
        for b in range(1, N_DEV):
            g = g + ca[:, b:b + 1] * d_ref[b:b + 1, :]
        g_ref[...] = g
        dl_ref[...], mo_ref[...], vo_ref[...] = _adamw_math(g, w_ref[...], m_ref[...], v_ref[...])

    o = jax.ShapeDtypeStruct((rows, cols), F32)
    return pl.pallas_call(
        body, name="ada_bwd_adamw", grid=(rows // tr,),
        in_specs=[pl.BlockSpec((tr, N_DEV), lambda i: (i, 0)), pl.BlockSpec((N_DEV, cols), lambda i: (0, 0)), blk, blk, blk],
        out_specs=[blk] * 4, out_shape=[o, o, o, o], compiler_params=_cparams(("parallel",)))(c_all.T, dmod_cols, w, m, v)


def _reduce_adamw(slabs, w, m, v, name):
    rows, cols = w.shape
    n_src = slabs.shape[0]
    if rows % 128 == 0:
        tr, steps = 128, rows // 128
        blk = pl.BlockSpec((tr, cols), lambda i: (i, 0))
        sblk = pl.BlockSpec((n_src, tr, cols), lambda i: (0, i, 0))
    else:
        tc, steps = 256, cols // 256
        blk = pl.BlockSpec((rows, tc), lambda i: (0, i))
        sblk = pl.BlockSpec((n_src, rows, tc), lambda i: (0, 0, i))

    def body(s_ref, w_ref, m_ref, v_ref, g_ref, dl_ref, mo_ref, vo_ref):
        g = s_ref[0].astype(F32)
        for src in range(1, n_src):
            g = g + s_ref[src].astype(F32)
        g_ref[...] = g
        dl_ref[...], mo_ref[...], vo_ref[...] = _adamw_math(g, w_ref[...], m_ref[...], v_ref[...])

    o = jax.ShapeDtypeStruct((rows, cols), F32)
    return pl.pallas_call(
        body, name=name, grid=(steps,), in_specs=[sblk, blk, blk, blk],
        out_specs=[blk] * 4, out_shape=[o, o, o, o], compiler_params=_cparams(("parallel",)))(slabs, w, m, v)


def _small_reduce_adamw(gathered, w, m, v):
    def body(s_ref, w_ref, m_ref, v_ref, g_ref, dl_ref, mo_ref, vo_ref):
        g = s_ref[0]
        for dev in range(1, N_DEV):
            g = g + s_ref[dev]
        g_ref[...] = g
        dl_ref[...], mo_ref[...], vo_ref[...] = _adamw_math(g, w_ref[...], m_ref[...], v_ref[...])

    o = jax.ShapeDtypeStruct(w.shape, F32)
    return pl.pallas_call(body, name="small_reduce_adamw", out_shape=[o, o, o, o], compiler_params=_cparams())(gathered, w, m, v)


def _adamw_small(g, w, m, v, name):
    def body(g_ref, w_ref, m_ref, v_ref, dl_ref, mo_ref, vo_ref):
        dl_ref[...], mo_ref[...], vo_ref[...] = _adamw_math(g_ref[...], w_ref[...], m_ref[...], v_ref[...])

    o = jax.ShapeDtypeStruct(w.shape, F32)
    return pl.pallas_call(body, name=name, out_shape=[o, o, o], compiler_params=_cparams())(g, w, m, v)


class _Exchange:
    def __init__(self, arrs, scatter):
        self.arrs, self.scatter, self.n = list(arrs), scatter, len(arrs)
        hbm = pl.BlockSpec(memory_space=pltpu.HBM)
        self.in_specs = [hbm] * self.n
        self.out_specs = [hbm] * self.n
        self.out_shape = [jax.ShapeDtypeStruct(a.shape if scatter else (N_DEV,) + a.shape, a.dtype) for a in self.arrs]
        self.scratch = [pltpu.SemaphoreType.DMA((self.n * (N_DEV - 1),)), pltpu.SemaphoreType.DMA((self.n * (N_DEV - 1),)),
                        pltpu.SemaphoreType.DMA((self.n,))]

    def _local(self, ins, outs, sems):
        me = 4 * lax.axis_index("x") + 2 * lax.axis_index("y") + lax.axis_index("c")
        return [pltpu.make_async_copy(ins[a].at[me] if self.scatter else ins[a], outs[a].at[me], sems[2].at[a])
                for a in range(self.n)]

    def _remote(self, ins, outs, sems, arriving):
        send_sems, recv_sems, _ = sems
        x, y, c = lax.axis_index("x"), lax.axis_index("y"), lax.axis_index("c")
        me = 4 * x + 2 * y + c
        remote = []
        for a in range(self.n):
            for k in range(1, N_DEV):
                px = 1 - x if k & 4 else x
                py = 1 - y if k & 2 else y
                pc = 1 - c if k & 1 else c
                peer = 4 * px + 2 * py + pc
                sem = a * (N_DEV - 1) + k - 1
                remote.append(pltpu.make_async_remote_copy(
                    src_ref=ins[a].at[peer] if self.scatter else ins[a], dst_ref=outs[a].at[peer if arriving else me],
                    send_sem=send_sems.at[sem], recv_sem=recv_sems.at[sem], device_id=(px, py, pc), device_id_type=MESH_IDS))
        return remote

    def start(self, ins, outs, sems):
        for cp in self._local(ins, outs, sems) + self._remote(ins, outs, sems, arriving=False):
            cp.start()

    def forward(self, ins, outs, sems):
        pass

    def wait(self, ins, outs, sems):
        for send, arrival in zip(self._remote(ins, outs, sems, arriving=False), self._remote(ins, outs, sems, arriving=True)):
            send.wait_send()
            arrival.wait_recv()
        for cp in self._local(ins, outs, sems):
            cp.wait()


N_CHIP = N_DEV // 2


class _SiblingSwap(_Exchange):
    def __init__(self, arrs):
        super().__init__(arrs, scatter=True)
        self.out_shape = [jax.ShapeDtypeStruct((N_CHIP,) + a.shape[2:], a.dtype) for a in self.arrs]
        self.scratch = [pltpu.SemaphoreType.DMA((self.n,)), pltpu.SemaphoreType.DMA((self.n,)), pltpu.SemaphoreType.DMA((1,))]

    def _copies(self, ins, outs, sems):
        x, y, c = lax.axis_index("x"), lax.axis_index("y"), lax.axis_index("c")
        return [pltpu.make_async_remote_copy(src_ref=ins[a].at[:, 1 - c], dst_ref=outs[a], send_sem=sems[0].at[a], recv_sem=sems[1].at[a],
                                             device_id=(x, y, 1 - c), device_id_type=MESH_IDS) for a in range(self.n)]

    def start(self, ins, outs, sems):
        for cp in self._copies(ins, outs, sems):
            cp.start()

    def wait(self, ins, outs, sems):
        for cp in self._copies(ins, outs, sems):
            cp.wait()


class _ChipScatter(_Exchange):
    def __init__(self, arrs):
        super().__init__(arrs, scatter=True)
        n_pairs = self.n * (N_CHIP - 1)
        self.scratch = [pltpu.SemaphoreType.DMA((n_pairs,)), pltpu.SemaphoreType.DMA((n_pairs,)), pltpu.SemaphoreType.DMA((self.n,))]

    def _local(self, ins, outs, sems):
        chip = 2 * lax.axis_index("x") + lax.axis_index("y")
        return [pltpu.make_async_copy(ins[a].at[chip], outs[a].at[chip], sems[2].at[a]) for a in range(self.n)]

    def _remote(self, ins, outs, sems, arriving):
        send_sems, recv_sems, _ = sems
        x, y, c = lax.axis_index("x"), lax.axis_index("y"), lax.axis_index("c")
        chip = 2 * x + y
        remote = []
        for a in range(self.n):
            for k in range(1, N_CHIP):
                px = 1 - x if k & 2 else x
                py = 1 - y if k & 1 else y
                peer = 2 * px + py
                sem = a * (N_CHIP - 1) + k - 1
                remote.append(pltpu.make_async_remote_copy(
                    src_ref=ins[a].at[peer], dst_ref=outs[a].at[peer if arriving else chip], send_sem=send_sems.at[sem],
                    recv_sem=recv_sems.at[sem], device_id=(px, py, c), device_id_type=MESH_IDS))
        return remote


def _chip_sum(mine, theirs):
    n, rows, cols = mine.shape
    blk = pl.BlockSpec((1, rows, 256), lambda q, j: (q, 0, j))

    def body(a_ref, b_ref, o_ref):
        o_ref[...] = (a_ref[...].astype(F32) + b_ref[...].astype(F32)).astype(BF16)

    return pl.pallas_call(body, name="chip_sum", grid=(n, cols // 256), in_specs=[blk, blk], out_specs=blk,
                          out_shape=jax.ShapeDtypeStruct(mine.shape, BF16),
                          compiler_params=_cparams(("parallel", "parallel")))(mine, theirs)


class _Gather2(_Exchange):
    def __init__(self, arrs):
        super().__init__(arrs, scatter=False)

    def _copies(self, ins, outs, sems):
        send_sems, recv_sems, _ = sems
        x, y, c = lax.axis_index("x"), lax.axis_index("y"), lax.axis_index("c")
        sibling = (x, y, 1 - c)
        chips = [(1 - x, y), (x, 1 - y), (1 - x, 1 - y)]
        first, passed, landed = [], [], []
        for a in range(self.n):
            def copy(k, block, to, src=None, a=a):
                slab = outs[a].at[4 * block[0] + 2 * block[1] + block[2]]
                return pltpu.make_async_remote_copy(
                    src_ref=slab if src is None else src, dst_ref=slab, send_sem=send_sems.at[a * (N_DEV - 1) + k],
                    recv_sem=recv_sems.at[a * (N_DEV - 1) + k], device_id=to, device_id_type=MESH_IDS)

            first.append(copy(0, (x, y, c), sibling, src=ins[a]))
            landed.append(copy(0, sibling, sibling))
            for j, chip in enumerate(chips):
                first.append(copy(1 + j, (x, y, c), (*chip, c), src=ins[a]))
                passed.append((copy(1 + j, (*chip, c), sibling), copy(4 + j, (*chip, c), sibling)))
                landed.append(copy(4 + j, (*chip, 1 - c), sibling))
        return first, passed, landed

    def start(self, ins, outs, sems):
        for cp in self._local(ins, outs, sems) + self._copies(ins, outs, sems)[0]:
            cp.start()

    def forward(self, ins, outs, sems):
        for arrival, onward in self._copies(ins, outs, sems)[1]:
            arrival.wait_recv()
            onward.start()

    def wait(self, ins, outs, sems):
        first, passed, landed = self._copies(ins, outs, sems)
        for arrival in landed:
            arrival.wait_recv()
        for cp in first + [onward for _, onward in passed]:
            cp.wait_send()
        for cp in self._local(ins, outs, sems):
            cp.wait()


def _split_comm_refs(refs, n_in, n_out, n_scr, comm):
    nc = comm.n if comm is not None else 0
    ns = 3 if comm is not None else 0
    pos, groups = 0, []
    for cnt in (n_in, nc, n_out, nc, n_scr, ns):
        groups.append(refs[pos:pos + cnt])
        pos += cnt
    assert pos == len(refs), (pos, len(refs))
    return groups


def _pcall(body, args, *, name, grid, in_specs, out_specs, out_shape, scratch_shapes=(), sem=None, comm=None):
    in_specs, out_specs, out_shape, scratch_shapes = list(in_specs), list(out_specs), list(out_shape), list(scratch_shapes)
    n_in, n_out, n_scr = len(in_specs), len(out_specs), len(scratch_shapes)
    if comm is None:
        kernel_body = body
    else:
        def kernel_body(*refs):
            ins, cins, outs, couts, scr, sems = _split_comm_refs(refs, n_in, n_out, n_scr, comm)
            ids = [pl.program_id(a) for a in range(len(grid))]
            first, last = ids[0] == 0, ids[0] == grid[0] - 1
            for a in range(1, len(grid)):
                first, last = first & (ids[a] == 0), last & (ids[a] == grid[a] - 1)

            middle = ids[0] == (2 * grid[0]) // 3
            for a in range(1, len(grid)):
                middle = middle & (ids[a] == 0)

            @pl.when(first)
            def _():
                comm.start(cins, couts, sems)

            @pl.when(middle)
            def _():
                comm.forward(cins, couts, sems)

            body(*ins, *outs, *scr)

            @pl.when(last)
            def _():
                comm.wait(cins, couts, sems)

        in_specs, out_specs, out_shape = in_specs + comm.in_specs, out_specs + comm.out_specs, out_shape + comm.out_shape
        scratch_shapes, args = scratch_shapes + comm.scratch, list(args) + comm.arrs
        sem = ("arbitrary",) * len(grid)
    res = pl.pallas_call(kernel_body, name=name, grid=grid, in_specs=in_specs, out_specs=out_specs, out_shape=out_shape,
                         scratch_shapes=scratch_shapes, compiler_params=_cparams(sem))(*args)
    return res[:n_out], res[n_out:]


def _exchange(arrs, name, scatter=False, ex=None):
    if ex is None:
        ex = _Exchange(arrs, scatter=True) if scatter else _Gather2(arrs)

    def body(*refs):
        _, ins, _, outs, _, sems = _split_comm_refs(refs, 0, 0, 0, ex)
        ex.start(ins, outs, sems)
        ex.forward(ins, outs, sems)
        ex.wait(ins, outs, sems)

    return pl.pallas_call(body, name=name, in_specs=ex.in_specs, out_specs=ex.out_specs, out_shape=ex.out_shape,
                          scratch_shapes=ex.scratch)(*ex.arrs)


def _pad_lanes(v, width=LANE):
    return jnp.pad(v, ((0, 0), (0, width - v.shape[1])))


def _shards_to_cols(g):
    return jnp.transpose(g, (1, 0, 2)).reshape(g.shape[1], N_DEV * g.shape[2])


def _local_step(x, tgt, mod, w_in_pt, conv_w, conv_b, dt_bias, a_log, d_skip, ssd_norm_w, q_norm_w, k_norm_w,
                attn_norm_w, w_out_sh, w_ff1_sh, w_ff2_sh, norm1_w, norm2_w, core):
    shift1, scale1, gate1, shift2, scale2, gate2 = [mod[i:i + 1] for i in range(N_MOD)]
    dtb, alog, dsk = _pad_lanes(dt_bias), _pad_lanes(a_log), _pad_lanes(d_skip)
    qw, kw = jnp.tile(q_norm_w, (1, ATT_HEADS)), jnp.tile(k_norm_w, (1, ATT_HEADS))

    h1 = _norm_mod_fwd(x, norm1_w, scale1, shift1, "norm1_fwd")
    proj = _matmul(h1, w_in_pt, tb=True, tm=2048, tn=896, tk=1024, name="in_proj")
    pre, act = _conv_fwd(proj, conv_w, conv_b)
    ypre, ycat_ssd, hall = _ssd_fwd(proj, act, dtb, alog, dsk, ssd_norm_w)
    (o_att, lse), (w_out_g, w_ff1_g, w_ff2_g) = _att_fwd(proj, qw, kw, comm=_Gather2([w_out_sh, w_ff1_sh, w_ff2_sh]))
    w_out = w_out_g.reshape(2 * D_MODEL, D_MODEL)
    w_ff1 = _shards_to_cols(w_ff1_g)
    w_ff2 = w_ff2_g.reshape(D_FF, D_MODEL)
    ycat = _att_norm_fwd(o_att, attn_norm_w, ycat_ssd)
    row32, row16, vec32 = ("row", F32), ("row", BF16), ("vec", F32)
    mix, x1, h2 = _matmul_rows(ycat, w_out, _residual_norm_epilogue, [x], [gate1, norm2_w, scale2, shift2],
                               [row32, row32, row16], tm=512, name="out_proj")
    u, act_ff = _matmul(h2, w_ff1, tm=1024, tn=2048, tk=1024, name="ff1", mode="relu2")
    loss, dout, dff, dgate2 = _matmul_rows(act_ff, w_ff2, _loss_epilogue, [x1, tgt], [gate2],
                                           [("one", F32), row32, row16, vec32], tm=512, name="ff2")

    du = _matmul(dff, w_ff2, tb=True, tm=512, tn=4096, tk=1024, out_dtype=BF16, name="ff2_dx", mode="drelu2", u=u)
    g_ff2 = _matmul(act_ff, dff, ta=True, tm=512, tn=1024, tk=4096, out_dtype=BF16, name="ff2_dw")
    dx1, dshift2, dscale2, g_norm2, dmix, dgate1 = _matmul_rows(
        du, w_ff1, _norm_bwd_epilogue, [x1, dout, mix], [norm2_w, scale2, gate1],
        [row32, vec32, vec32, vec32, row16, vec32], tb=True, tm=512, name="ff1_dx")
    g_ff1 = _matmul(h2, du, ta=True, tm=1024, tn=D_FF // N_DEV, tk=4096, out_dtype=BF16, name="ff1_dw", shard_out=True)

    dy_ssd, do, stats, g_attn_norm = _matmul_rows(
        dmix, w_out, _mixer_split_epilogue, [o_att, lse], [attn_norm_w],
        [("row", F32, SSD_D_INNER), ("row", F32, ATT_D), ("row", F32, ATT_D), ("vec", F32, ATT_D)], tb=True, tm=512, name="out_proj_dx")
    g_out = _matmul(ycat, dmix, ta=True, tm=512, tn=1024, tk=4096, out_dtype=BF16, name="out_proj_dw")
    ff_slabs = [g_ff1, g_ff2.reshape(N_DEV, D_FF // N_DEV, D_MODEL)]
    (dq, dk, dv, dqw, dkw), (s_ff1, s_ff2) = _att_bwd(proj, do, stats, qw, kw, comm=_Exchange(ff_slabs, scatter=True))
    out_slabs = [g_out.astype(BF16).reshape(N_DEV, 2 * D_MODEL // N_DEV, D_MODEL)]
    (dz, dact, ddtr, da, g_dsk, g_dtb, g_ssd_norm), (s_out,) = _ssd_bwd(
        dy_ssd, ypre, proj, act, hall, dtb, alog, dsk, ssd_norm_w, comm=_Exchange(out_slabs, scatter=True))
    dxbc, g_conv_w, g_conv_b = _conv_bwd(dact, pre, proj, conv_w)
    dproj = [(dz, OFF_Z), (dxbc, OFF_XBC), (ddtr, OFF_DT), (dq, OFF_Q), (dk, OFF_K), (dv, OFF_V)]
    g_head, g_tail = _pieces_t_matmul([[dz, dxbc], [dq, dk, dv]], h1, tm=256, name="in_proj_dw")
    g_dt = _matmul(ddtr, h1, ta=True, tm=LANE, tn=1024, tk=4096, out_dtype=BF16, name="in_proj_dw_dt")[:SSD_HEADS]
    in_slabs = jnp.concatenate([g_head, g_dt, g_tail], axis=0).reshape(N_CHIP, 2, IN_W // N_DEV, D_MODEL)
    (sibling_slabs,) = _exchange(None, "swap_w_in_grads", ex=_SiblingSwap([in_slabs]))
    chip_slabs = _chip_sum(lax.dynamic_index_in_dim(in_slabs, core, axis=1, keepdims=False), sibling_slabs)
    (grad_x, dshift1, dscale1, g_norm1), (s_in,) = _matmul_rows(
        dproj, w_in_pt, _norm_bwd_epilogue, [x, dx1], [norm1_w, scale1], [row32, vec32, vec32, vec32],
        tm=256, name="in_proj_dx", comm=_ChipScatter([chip_slabs]))

    dmod = jnp.concatenate([dshift1, dscale1, dgate1, dshift2, dscale2, dgate2], axis=0)
    g_alog = da[:, :SSD_HEADS] * (-jnp.exp(a_log))
    g_qw = dqw.reshape(ATT_HEADS, HEAD_DIM).sum(axis=0, keepdims=True)
    g_kw = dkw.reshape(ATT_HEADS, HEAD_DIM).sum(axis=0, keepdims=True)
    return dict(loss=loss, grad_x=grad_x, dmod=dmod, norm1_w=g_norm1, norm2_w=g_norm2, w_in=s_in, conv_w=g_conv_w,
                conv_b=g_conv_b, dt_bias=g_dtb[:, :SSD_HEADS], a_log=g_alog, d_skip=g_dsk[:, :SSD_HEADS],
                ssd_norm_w=g_ssd_norm, q_norm_w=g_qw, k_norm_w=g_kw, attn_norm_w=g_attn_norm, w_out=s_out,
                w_ff1=s_ff1, w_ff2=s_ff2)


def _pack_w_in_rows(wt_full):
    cut = OFF_DT + SSD_HEADS
    pad = jnp.zeros((LANE - SSD_HEADS, wt_full.shape[1]), wt_full.dtype)
    return jnp.concatenate([wt_full[:cut], pad, wt_full[cut:]], axis=0)


MISC_FIELDS = (("dt_bias", SSD_HEADS), ("a_log", SSD_HEADS), ("d_skip", SSD_HEADS), ("q_norm_w", HEAD_DIM), ("k_norm_w", HEAD_DIM),
               ("loss", 1))
SMALL_LAYOUT = (("b_ada", 6), ("norm1_w", 1), ("norm2_w", 1), ("conv_w", 8), ("conv_b", 2), ("ssd_norm_w", 1),
                ("attn_norm_w", 1), ("misc", 1))


def _pack_small(vals):
    rows = []
    for name, nrow in SMALL_LAYOUT:
        if name == "misc":
            misc = jnp.concatenate([vals[f].reshape(1, n) if f in vals else jnp.zeros((1, n), F32) for f, n in MISC_FIELDS], axis=1)
            rows.append(_pad_lanes(misc, D_MODEL))
        elif name in vals:
            rows.append(vals[name].reshape(nrow, D_MODEL))
        else:
            rows.append(jnp.zeros((nrow, D_MODEL), F32))
    used = sum(n for _, n in SMALL_LAYOUT)
    rows.append(jnp.zeros((SMALL_ROWS - used, D_MODEL), F32))
    return jnp.concatenate(rows, axis=0)


def _unpack_small(packed):
    out, r = {}, 0
    for name, nrow in SMALL_LAYOUT:
        blk = packed[r:r + nrow]
        r += nrow
        if name == "misc":
            c0 = 0
            for f, n in MISC_FIELDS:
                out[f] = blk[:, c0:c0 + n]
                c0 += n
        elif name == "b_ada":
            out[name] = blk.reshape(1, N_MOD * D_MODEL)
        elif name == "conv_w":
            out[name] = blk.reshape(CONV_K, CONV_CH)
        elif name == "conv_b":
            out[name] = blk.reshape(1, CONV_CH)
        else:
            out[name] = blk
    return out


WEIGHT_NAMES = ("norm1_w", "norm2_w", "w_ada", "b_ada", "w_in", "conv_w", "conv_b", "dt_bias", "a_log", "d_skip",
                "ssd_norm_w", "q_norm_w", "k_norm_w", "attn_norm_w", "w_out", "w_ff1", "w_ff2")
SMALL_NAMES = ("norm1_w", "norm2_w", "b_ada", "conv_b", "dt_bias", "a_log", "d_skip", "ssd_norm_w", "q_norm_w",
               "k_norm_w", "attn_norm_w")


def kernel(x, c, norm1_w, norm2_w, w_ada, b_ada, w_in, conv_w, conv_b, dt_bias, a_log, d_skip, ssd_norm_w, q_norm_w, k_norm_w, attn_norm_w, w_out, w_ff1, w_ff2, loss_target, m_norm1_w, m_norm2_w, m_w_ada, m_b_ada, m_w_in, m_conv_w, m_conv_b, m_dt_bias, m_a_log, m_d_skip, m_ssd_norm_w, m_q_norm_w, m_k_norm_w, m_attn_norm_w, m_w_out, m_w_ff1, m_w_ff2, v_norm1_w, v_norm2_w, v_w_ada, v_b_ada, v_w_in, v_conv_w, v_conv_b, v_dt_bias, v_a_log, v_d_skip, v_ssd_norm_w, v_q_norm_w, v_k_norm_w, v_attn_norm_w, v_w_out, v_w_ff1, v_w_ff2):
    args = dict(locals())
    w = {n: args[n] for n in WEIGHT_NAMES}
    m = {n: args["m_" + n] for n in WEIGHT_NAMES}
    v = {n: args["v_" + n] for n in WEIGHT_NAMES}
    me = 4 * lax.axis_index("x") + 2 * lax.axis_index("y") + lax.axis_index("c")

    c_rows = jnp.pad(c, ((0, 7), (0, 0)))
    w_in_t, m_in_t, v_in_t = [jnp.transpose(t["w_in"][0]) for t in (w, m, v)]
    c_g, conv_g, w_in_g = _exchange([c_rows, w["conv_w"][0], w_in_t.astype(BF16)], "gather_w_in", scatter=False)
    c_all = c_g[:, 0, :]
    conv_full = _shards_to_cols(conv_g)
    w_in_pt = _pack_w_in_rows(w_in_g.reshape(IN_W, D_MODEL))

    mod_part = _ada_fwd(c_all, w["w_ada"][0])
    (mod_g,) = _exchange([mod_part], "gather_mod", scatter=False)
    mod_mine = lax.dynamic_index_in_dim(mod_g, me, axis=1, keepdims=False).reshape(1, N_MOD * D_MODEL) + w["b_ada"]
    mod = mod_mine.reshape(N_MOD, D_MODEL)

    res = _local_step(x[0], loss_target[0], mod, w_in_pt, conv_full, w["conv_b"], w["dt_bias"], w["a_log"], w["d_skip"],
                      w["ssd_norm_w"], w["q_norm_w"], w["k_norm_w"], w["attn_norm_w"], w["w_out"][0].astype(BF16),
                      w["w_ff1"][0].astype(BF16), w["w_ff2"][0].astype(BF16), w["norm1_w"], w["norm2_w"], lax.axis_index("c"))

    small_vals = {n: res[n] for n in SMALL_NAMES if n != "b_ada"}
    small_vals["b_ada"] = res["dmod"]
    small_vals["conv_w"] = res["conv_w"]
    small_vals["loss"] = res["loss"]
    (small_g,) = _exchange([_pack_small(small_vals)], "gather_small", scatter=False)

    grads, delta, new_m, new_v = {}, {}, {}, {}
    for name in ("w_out", "w_ff1", "w_ff2"):
        outs = _reduce_adamw(res[name], w[name][0], m[name][0], v[name][0], "adamw_" + name)
        grads[name], delta[name], new_m[name], new_v[name] = [o[None] for o in outs]
    outs = _reduce_adamw(res["w_in"], w_in_t, m_in_t, v_in_t, "adamw_w_in")
    grads["w_in"], delta["w_in"], new_m["w_in"], new_v["w_in"] = [jnp.transpose(o)[None] for o in outs]

    sm = _small_reduce_adamw(small_g, _pack_small({n: w[n] for n in SMALL_NAMES}), _pack_small({n: m[n] for n in SMALL_NAMES}),
                             _pack_small({n: v[n] for n in SMALL_NAMES}))
    sm = [_unpack_small(p) for p in sm]
    for n in SMALL_NAMES:
        grads[n], delta[n], new_m[n], new_v[n] = [p[n] for p in sm]
    shard_w = CONV_CH // N_DEV
    g_conv = lax.dynamic_slice_in_dim(sm[0]["conv_w"], me * shard_w, shard_w, axis=1)
    cw = _adamw_small(g_conv, w["conv_w"][0], m["conv_w"][0], v["conv_w"][0], "adamw_conv_w")
    grads["conv_w"] = g_conv[None]
    delta["conv_w"], new_m["conv_w"], new_v["conv_w"] = [o[None] for o in cw]

    ada_w = w_ada.shape[2]
    dmod_all = small_g[:, :N_MOD, :].reshape(N_DEV, N_MOD * D_MODEL)
    dmod_cols = lax.dynamic_slice_in_dim(dmod_all, me * ada_w, ada_w, axis=1)
    outs = _ada_bwd_adamw(c_all, dmod_cols, w["w_ada"][0], m["w_ada"][0], v["w_ada"][0])
    grads["w_ada"], delta["w_ada"], new_m["w_ada"], new_v["w_ada"] = [o[None] for o in outs]

    loss = sm[0]["loss"][0, 0]
    return (loss, res["grad_x"][None], *[grads[n] for n in WEIGHT_NAMES], *[delta[n] for n in WEIGHT_NAMES],
            *[new_m[n] for n in WEIGHT_NAMES], *[new_v[n] for n in WEIGHT_NAMES])
```

```python
import jax
import jax.numpy as jnp
from jax import lax
from jax.experimental import pallas as pl
from jax.experimental.pallas import tpu as pltpu

F32 = jnp.float32
BF16 = jnp.bfloat16
HIGHEST = lax.Precision.HIGHEST
MESH_IDS = pl.DeviceIdType.MESH

N_DEV = 8
D_MODEL = 1024
HEAD_DIM = 64
SSD_HEADS = 16
SSD_GROUPS = 4
HEADS_PER_GROUP = SSD_HEADS // SSD_GROUPS
SSD_STATE = 128
SSD_CHUNK = 128
SSD_D_INNER = SSD_HEADS * HEAD_DIM
GROUP_WIDTH = SSD_D_INNER // SSD_GROUPS
CONV_K = 4
CONV_CH = SSD_D_INNER + 2 * SSD_GROUPS * SSD_STATE
ATT_HEADS = 16
ATT_D = ATT_HEADS * HEAD_DIM
ATT_BLK = 128
DILATIONS = (1, 4, 16)
D_FF = 4 * D_MODEL
N_MOD = 6
EPS = 1e-6
IN_W = SSD_D_INNER + CONV_CH + SSD_HEADS + 3 * ATT_D
LANE = 128
OFF_Z, OFF_XBC, OFF_DT = 0, SSD_D_INNER, SSD_D_INNER + CONV_CH
OFF_Q = OFF_DT + LANE
OFF_K, OFF_V = OFF_Q + ATT_D, OFF_Q + 2 * ATT_D
IN_WP = OFF_V + ATT_D

ADAM_LR, ADAM_B1, ADAM_B2, ADAM_EPS, ADAM_WD, ADAM_STEP = 0.001, 0.9, 0.999, 1e-08, 0.01, 10
VMEM_LIMIT = 60 * 1024 * 1024
ROW_TILE = 512
SMALL_ROWS = 24


def _cparams(sem=None):
    return pltpu.CompilerParams(dimension_semantics=sem, vmem_limit_bytes=VMEM_LIMIT)


def _sigmoid(v):
    return 1.0 / (1.0 + jnp.exp(-v))


def _softplus(v):
    y = jnp.exp(-jnp.abs(v))
    small = y * (1.0 - y * (0.5 - y * (1.0 / 3.0)))
    return jnp.maximum(v, 0.0) + jnp.where(y < 0.01, small, jnp.log(1.0 + y))


def _dot(a, b, dims, precision=None):
    return lax.dot_general(a, b, (dims, ((), ())), preferred_element_type=F32, precision=precision)


NN = ((1,), (0,))
NT = ((1,), (1,))
TN = ((0,), (0,))


def _matmul(a, b, *, ta=False, tb=False, tm, tn, tk, out_dtype=F32, name, mode=None, u=None, comm=None, shard_out=False):
    m, k = (a.shape[1], a.shape[0]) if ta else a.shape
    n = b.shape[0] if tb else b.shape[1]
    assert m % tm == 0 and n % tn == 0 and k % tk == 0, (name, m, n, k)
    nk = k // tk
    a_spec = pl.BlockSpec((tk, tm), lambda i, j, kk: (kk, i)) if ta else pl.BlockSpec((tm, tk), lambda i, j, kk: (i, kk))
    b_spec = pl.BlockSpec((tn, tk), lambda i, j, kk: (j, kk)) if tb else pl.BlockSpec((tk, tn), lambda i, j, kk: (kk, j))
    o_spec = pl.BlockSpec((tm, tn), lambda i, j, kk: (i, j))
    dims = ((0,) if ta else (1,), (1,) if tb else (0,))
    n_out = 2 if mode == "relu2" else 1

    def body(*refs):
        if mode == "drelu2":
            a_ref, b_ref, u_ref = refs[:3]
            rest = refs[3:]
        else:
            a_ref, b_ref = refs[:2]
            u_ref = None
            rest = refs[2:]
        outs = rest[:n_out]
        part = _dot(a_ref[...], b_ref[...], dims)

        def finish(r):
            if mode == "relu2":
                outs[0][...] = r.astype(BF16)
                rr = jnp.maximum(r, 0.0)
                outs[1][...] = (rr * rr).astype(BF16)
            elif mode == "drelu2":
                outs[0][...] = (r * (2.0 * jnp.maximum(u_ref[...].astype(F32), 0.0))).astype(out_dtype)
            else:
                outs[0][...] = r.astype(out_dtype)

        if nk == 1:
            finish(part)
        else:
            acc = rest[n_out]
            kk = pl.program_id(2)

            @pl.when(kk == 0)
            def _():
                acc[...] = part

            @pl.when(kk > 0)
            def _():
                acc[...] += part

            @pl.when(kk == nk - 1)
            def _():
                finish(acc[...])

    in_specs = [a_spec, b_spec]
    args = [a, b]
    if mode == "drelu2":
        in_specs.append(o_spec)
        args.append(u)
    if mode == "relu2":
        out_shape = [jax.ShapeDtypeStruct((m, n), BF16), jax.ShapeDtypeStruct((m, n), BF16)]
    elif shard_out:
        out_shape = [jax.ShapeDtypeStruct((n // tn, m, tn), out_dtype)]
        o_spec = pl.BlockSpec((None, tm, tn), lambda i, j, kk: (j, i, 0))
    else:
        out_shape = [jax.ShapeDtypeStruct((m, n), out_dtype)]
    outs, comm_outs = _pcall(
        body, args, name=name, grid=(m // tm, n // tn, nk), in_specs=in_specs, out_specs=[o_spec] * n_out,
        out_shape=out_shape, scratch_shapes=[pltpu.VMEM((tm, tn), F32)] if nk > 1 else [],
        sem=("parallel", "parallel", "arbitrary"), comm=comm)
    res = tuple(outs) if mode == "relu2" else outs[0]
    return res if comm is None else (res, comm_outs)


def _pieces_t_matmul(groups, b, *, tm, name):
    k, n = b.shape
    pieces = [p for g in groups for p in g]
    starts, tiles = [], 0
    for p in pieces:
        assert p.shape[0] == k and p.shape[1] % tm == 0, (name, p.shape)
        starts.append(tiles)
        tiles += p.shape[1] // tm
    group_of, group_start, group_tiles = [], [], []
    for gi, g in enumerate(groups):
        group_start.append(starts[len(group_of)])
        group_of += [gi] * len(g)
        group_tiles.append(sum(p.shape[1] // tm for p in g))

    def clipped(block, start, count):
        return pl.BlockSpec(block, (lambda i: (0, jnp.clip(i - start, 0, count - 1))) if block[0] == k
                            else (lambda i: (jnp.clip(i - start, 0, count - 1), 0)))

    def body(*refs):
        a_refs, b_ref, o_refs = refs[:len(pieces)], refs[len(pieces)], refs[len(pieces) + 1:]
        i = pl.program_id(0)
        for a_ref, start, p, gi in zip(a_refs, starts, pieces, group_of):
            @pl.when((i >= start) & (i < start + p.shape[1] // tm))
            def _(a_ref=a_ref, o_ref=o_refs[gi]):
                o_ref[...] = _dot(a_ref[...], b_ref[...], TN).astype(BF16)

    return pl.pallas_call(
        body, name=name, grid=(tiles,),
        in_specs=[clipped((k, tm), s0, p.shape[1] // tm) for s0, p in zip(starts, pieces)] + [pl.BlockSpec((k, n), lambda i: (0, 0))],
        out_specs=[clipped((tm, n), s0, cnt) for s0, cnt in zip(group_start, group_tiles)],
        out_shape=[jax.ShapeDtypeStruct((cnt * tm, n), BF16) for cnt in group_tiles],
        compiler_params=_cparams(("arbitrary",)))(*pieces, b)


def _rms_mod(xv, nw, scale, shift):
    r = lax.rsqrt(jnp.mean(xv * xv, axis=-1, keepdims=True) + EPS)
    return ((xv * r) * nw * (1.0 + scale) + shift).astype(BF16)


def _norm_mod_fwd(x, nw, scale, shift, name):
    s, d = x.shape
    row = pl.BlockSpec((ROW_TILE, d), lambda i: (i, 0))
    vec = pl.BlockSpec((1, d), lambda i: (0, 0))

    def body(x_ref, nw_ref, sc_ref, sh_ref, h_ref):
        h_ref[...] = _rms_mod(x_ref[...], nw_ref[...], sc_ref[...], sh_ref[...])

    return pl.pallas_call(body, name=name, grid=(s // ROW_TILE,), in_specs=[row, vec, vec, vec], out_specs=row,
                          out_shape=jax.ShapeDtypeStruct((s, d), BF16), compiler_params=_cparams(("parallel",)))(x, nw, scale, shift)


def _matmul_rows(a, b, epilogue, row_in, vec_in, outs, *, tb=False, tm, name, comm=None):
    pieces = a if isinstance(a, list) else [(a, 0)]
    assert not (tb and len(pieces) > 1)
    m = pieces[0][0].shape[0]
    n = b.shape[0] if tb else b.shape[1]
    assert m % tm == 0, (name, m, tm)
    dims = ((1,), (1,) if tb else (0,))
    n_a, n_row, n_vec = len(pieces), len(row_in), len(vec_in)

    def body(*refs):
        a_refs, b_ref, rest = refs[:n_a], refs[n_a], refs[n_a + 1:]
        if n_a == 1:
            c = _dot(a_refs[0][...], b_ref[...], dims)
        else:
            c = None
            for a_ref, (piece, off) in zip(a_refs, pieces):
                part = _dot(a_ref[...], b_ref[off:off + piece.shape[1], :], dims)
                c = part if c is None else c + part
        epilogue(c, pl.program_id(0) == 0, rest[:n_row], rest[n_row:n_row + n_vec], rest[n_row + n_vec:])

    def spec(kind, width):
        block = {"row": (tm, width), "vec": (1, width), "one": (1, 1)}[kind]
        return pl.BlockSpec(block, (lambda i: (i, 0)) if kind == "row" else (lambda i: (0, 0)))

    def shape(kind, width):
        return {"row": (m, width), "vec": (1, width), "one": (1, 1)}[kind]

    outs = [(o[0], o[1], o[2] if len(o) > 2 else n) for o in outs]
    res, comm_outs = _pcall(
        body, [*[p for p, _ in pieces], b, *row_in, *vec_in], name=name, grid=(m // tm,),
        in_specs=[spec("row", p.shape[1]) for p, _ in pieces] + [pl.BlockSpec(b.shape, lambda i: (0, 0), pipeline_mode=pl.Buffered(1))]
        + [spec("row", r.shape[1]) for r in row_in] + [spec("vec", v.shape[1]) for v in vec_in],
        out_specs=[spec(kind, width) for kind, _, width in outs],
        out_shape=[jax.ShapeDtypeStruct(shape(kind, width), dt) for kind, dt, width in outs],
        sem=("arbitrary",), comm=comm)
    return res if comm is None else (res, comm_outs)


def _residual_norm_epilogue(mix, first, rows, vecs, outs):
    (x_ref,), (gate_ref, nw_ref, sc_ref, sh_ref), (mix_ref, x1_ref, h_ref) = rows, vecs, outs
    xv = x_ref[...] + gate_ref[...] * mix
    mix_ref[...] = mix
    x1_ref[...] = xv
    h_ref[...] = _rms_mod(xv, nw_ref[...], sc_ref[...], sh_ref[...])


def _loss_epilogue(ff, first, rows, vecs, outs):
    (x1_ref, t_ref), (g_ref,), (loss_ref, dout_ref, dff_ref, dg_ref) = rows, vecs, outs
    d = ff.shape[1]

    @pl.when(first)
    def _():
        loss_ref[...] = jnp.zeros_like(loss_ref)
        dg_ref[...] = jnp.zeros_like(dg_ref)

    err = x1_ref[...] + g_ref[...] * ff - t_ref[...]
    loss_ref[...] += (0.5 / d) * jnp.sum(err * err).reshape(1, 1)
    dout = err * (1.0 / d)
    dout_ref[...] = dout
    dff_ref[...] = (g_ref[...] * dout).astype(BF16)
    dg_ref[...] += jnp.sum(dout * ff, axis=0, keepdims=True)


def _norm_bwd_epilogue(dh, first, rows, vecs, outs):
    with_gate = len(vecs) == 3
    x_ref, dres_ref = rows[:2]
    nw_ref, sc_ref = vecs[:2]
    dx_ref, dsh_ref, dsc_ref, dnw_ref = outs[:4]

    @pl.when(first)
    def _():
        for ref in outs[1:4] + outs[5:]:
            ref[...] = jnp.zeros_like(ref)

    xv = x_ref[...]
    r = lax.rsqrt(jnp.mean(xv * xv, axis=-1, keepdims=True) + EPS)
    nrm = xv * r
    one_sc = 1.0 + sc_ref[...]
    dhn = dh * nrm
    dsh_ref[...] += jnp.sum(dh, axis=0, keepdims=True)
    dsc_ref[...] += jnp.sum(dhn, axis=0, keepdims=True) * nw_ref[...]
    dnw_ref[...] += jnp.sum(dhn, axis=0, keepdims=True) * one_sc
    dn = dh * (nw_ref[...] * one_sc)
    dx = dres_ref[...] + r * (dn - nrm * jnp.mean(dn * nrm, axis=-1, keepdims=True))
    dx_ref[...] = dx
    if with_gate:
        outs[4][...] = (vecs[2][...] * dx).astype(BF16)
        outs[5][...] += jnp.sum(dx * rows[2][...], axis=0, keepdims=True)


CONV_COLS = 256
CONV_FWD_ROWS = 2048
CONV_BWD_ROWS = 1024
CONV_SUB_ROWS = 128
HALO = 8


def _shift_down(cur, halo, k):
    if k == 0:
        return cur
    rolled = pltpu.roll(cur, k, axis=0)
    top = jnp.where(lax.broadcasted_iota(jnp.int32, halo.shape, 0) < k, pltpu.roll(halo, k, axis=0), rolled[:HALO])
    return jnp.concatenate([top, rolled[HALO:]], axis=0)


def _shift_up(cur, halo, k):
    if k == 0:
        return cur
    t = cur.shape[0]
    rolled = pltpu.roll(cur, t - k, axis=0)
    bot = jnp.where(lax.broadcasted_iota(jnp.int32, halo.shape, 0) >= HALO - k, pltpu.roll(halo, HALO - k, axis=0),
                    rolled[t - HALO:])
    return jnp.concatenate([rolled[:t - HALO], bot], axis=0)


def _conv_fwd(proj, conv_w, conv_b):
    s = proj.shape[0]
    nr = s // CONV_FWD_ROWS
    cb0 = OFF_XBC // CONV_COLS
    hb = CONV_FWD_ROWS // HALO
    cur = pl.BlockSpec((CONV_FWD_ROWS, CONV_COLS), lambda j, r: (r, cb0 + j))
    prev = pl.BlockSpec((HALO, CONV_COLS), lambda j, r: (jnp.maximum(r * hb - 1, 0), cb0 + j))
    out = pl.BlockSpec((CONV_FWD_ROWS, CONV_COLS), lambda j, r: (r, j))

    def body(u_ref, up_ref, w_ref, b_ref, pre_ref, act_ref):
        r = pl.program_id(1)
        for c in range(CONV_FWD_ROWS // CONV_SUB_ROWS):
            rows = slice(c * CONV_SUB_ROWS, (c + 1) * CONV_SUB_ROWS)
            u = u_ref[rows, :]
            halo = u_ref[c * CONV_SUB_ROWS - HALO:c * CONV_SUB_ROWS, :] if c > 0 else jnp.where(r > 0, up_ref[...], 0.0)
            acc = b_ref[...] + w_ref[CONV_K - 1:CONV_K, :] * u
            for k in range(1, CONV_K):
                acc = acc + w_ref[CONV_K - 1 - k:CONV_K - k, :] * _shift_down(u, halo, k)
            pre_ref[rows, :] = acc
            act_ref[rows, :] = acc * _sigmoid(acc)

    return pl.pallas_call(
        body, name="conv_fwd", grid=(CONV_CH // CONV_COLS, nr),
        in_specs=[cur, prev, pl.BlockSpec((CONV_K, CONV_COLS), lambda j, r: (0, j)),
                  pl.BlockSpec((1, CONV_COLS), lambda j, r: (0, j))],
        out_specs=[out, out],
        out_shape=[jax.ShapeDtypeStruct((s, CONV_CH), F32), jax.ShapeDtypeStruct((s, CONV_CH), F32)],
        compiler_params=_cparams(("parallel", "arbitrary")))(proj, proj, conv_w, conv_b)


def _conv_bwd(dact, pre, proj, conv_w):
    s = proj.shape[0]
    nr = s // CONV_BWD_ROWS
    cb0 = OFF_XBC // CONV_COLS
    hb = CONV_BWD_ROWS // HALO
    last_halo = s // HALO - 1
    n_sub = CONV_BWD_ROWS // CONV_SUB_ROWS
    cur = pl.BlockSpec((CONV_BWD_ROWS, CONV_COLS), lambda j, r: (r, j))
    nxt = pl.BlockSpec((HALO, CONV_COLS), lambda j, r: (jnp.minimum((r + 1) * hb, last_halo), j))
    ucur = pl.BlockSpec((CONV_BWD_ROWS, CONV_COLS), lambda j, r: (r, cb0 + j))
    wspec = pl.BlockSpec((CONV_K, CONV_COLS), lambda j, r: (0, j))
    bspec = pl.BlockSpec((1, CONV_COLS), lambda j, r: (0, j))

    def dsilu(p):
        sg = _sigmoid(p)
        return sg * (1.0 + p * (1.0 - sg))

    def body(da_ref, dan_ref, pre_ref, pren_ref, u_ref, w_ref, du_ref, dw_ref, db_ref):
        r = pl.program_id(1)

        @pl.when(r == 0)
        def _():
            dw_ref[...] = jnp.zeros_like(dw_ref)
            db_ref[...] = jnp.zeros_like(db_ref)

        dws = [jnp.zeros((1, CONV_COLS), F32) for _ in range(CONV_K)]
        db = jnp.zeros((1, CONV_COLS), F32)
        for c in range(n_sub):
            rows = slice(c * CONV_SUB_ROWS, (c + 1) * CONV_SUB_ROWS)
            ahead = slice((c + 1) * CONV_SUB_ROWS, (c + 1) * CONV_SUB_ROWS + HALO)
            dpre = da_ref[rows, :] * dsilu(pre_ref[rows, :])
            if c < n_sub - 1:
                dnext = da_ref[ahead, :] * dsilu(pre_ref[ahead, :])
            else:
                dnext = jnp.where(r < nr - 1, dan_ref[...] * dsilu(pren_ref[...]), 0.0)
            u = u_ref[rows, :]
            du = w_ref[CONV_K - 1:CONV_K, :] * dpre
            dws[0] = dws[0] + jnp.sum(dpre * u, axis=0, keepdims=True)
            for k in range(1, CONV_K):
                ahead_k = _shift_up(dpre, dnext, k)
                du = du + w_ref[CONV_K - 1 - k:CONV_K - k, :] * ahead_k
                dws[k] = dws[k] + jnp.sum(ahead_k * u, axis=0, keepdims=True)
            du_ref[rows, :] = du.astype(BF16)
            db = db + jnp.sum(dpre, axis=0, keepdims=True)
        dw_ref[...] += jnp.concatenate(dws[::-1], axis=0)
        db_ref[...] += db

    return pl.pallas_call(
        body, name="conv_bwd", grid=(CONV_CH // CONV_COLS, nr),
        in_specs=[cur, nxt, cur, nxt, ucur, wspec],
        out_specs=[cur, wspec, bspec],
        out_shape=[jax.ShapeDtypeStruct((s, CONV_CH), BF16), jax.ShapeDtypeStruct((CONV_K, CONV_CH), F32),
                   jax.ShapeDtypeStruct((1, CONV_CH), F32)],
        compiler_params=_cparams(("parallel", "arbitrary")))(dact, dact, pre, pre, proj, conv_w)


def _ssd_common(dtr, dtb, alog):
    lane = lax.broadcasted_iota(jnp.int32, (1, LANE), 1)
    head_lane = lane < SSD_HEADS
    dt = jnp.where(head_lane, _softplus(dtr + dtb), 0.0)
    a = jnp.where(head_lane, -jnp.exp(alog), 0.0)
    row = lax.broadcasted_iota(jnp.int32, (SSD_CHUNK, SSD_CHUNK), 0)
    col = lax.broadcasted_iota(jnp.int32, (SSD_CHUNK, SSD_CHUNK), 1)
    tril = row >= col
    cs = _dot(tril.astype(F32), dt * a, NN, precision=HIGHEST)
    return dt, a, cs, cs.T, tril, lane


def _split_bf16(v, passes):
    terms, rest = [], v
    for _ in range(passes):
        t = rest.astype(BF16)
        terms.append(t)
        rest = rest - t.astype(F32)
    return terms


def _dot_split(v, m, dims, passes):
    terms = _split_bf16(v, passes)
    if passes == 1:
        return _dot(terms[0], m, dims)
    return _dot(jnp.concatenate(terms, axis=1), jnp.concatenate([m] * passes, axis=0 if dims == NN else 1), dims)


def _ssd_constants():
    heads = jnp.arange(LANE)[:, None]
    exp_mat = (heads == (jnp.arange(SSD_D_INNER)[None, :] // HEAD_DIM)).astype(BF16)
    ind4 = ((jnp.arange(SSD_HEADS * SSD_CHUNK)[:, None] // SSD_CHUNK) == jnp.arange(LANE)[None, :]).astype(BF16)
    return exp_mat, ind4


def _expand_heads(v):
    return jnp.repeat(v[:, :SSD_HEADS], HEAD_DIM, axis=1)


def _ssd_prep(dtr, dtb, alog, exp_mat):
    dt, a, cs, cst, tril, lane = _ssd_common(dtr, dtb, alog)
    return dt, a, cs, cst, tril, lane, _dot_split(dt, exp_mat, NN, 2), _dot_split(cs, exp_mat, NN, 3)


def _chunk_decay_rows(cs, g):
    parts = []
    for e in range(HEADS_PER_GROUP):
        h = g * HEADS_PER_GROUP + e
        parts.append(jnp.broadcast_to(jnp.exp(cs[SSD_CHUNK - 1:SSD_CHUNK, h:h + 1]), (HEAD_DIM, SSD_STATE)))
    return jnp.concatenate(parts, axis=0)


def _ssd_fwd(proj, act, dtb, alog, dsk, nw):
    s = proj.shape[0]
    nc = s // SSD_CHUNK
    bc_w = SSD_GROUPS * SSD_STATE
    exp_mat, _ = _ssd_constants()

    def body(z_ref, dtr_ref, xs_ref, b_ref, c_ref, dtb_ref, alog_ref, dskx_ref, nw_ref, exp_ref,
             ypre_ref, yssd_ref, hall_ref, h_scr):
        @pl.when(pl.program_id(0) == 0)
        def _():
            h_scr[...] = jnp.zeros_like(h_scr)

        dt, a, cs, cst, tril, lane, dtx, csx = _ssd_prep(dtr_ref[...], dtb_ref[...], alog_ref[...], exp_ref[...])
        cs_last_x = csx[SSD_CHUNK - 1:SSD_CHUNK, :]
        xs = xs_ref[...]
        xdt = xs * dtx
        xdtb = xdt.astype(BF16)
        xdec = (xdt * jnp.exp(cs_last_x - csx)).astype(BF16)
        ecsx = jnp.exp(csx)
        head_of_lane = lax.broadcasted_iota(jnp.int32, (1, GROUP_WIDTH), 1) // HEAD_DIM
        for g in range(SSD_GROUPS):
            gs = slice(g * GROUP_WIDTH, (g + 1) * GROUP_WIDTH)
            bg = b_ref[:, g * SSD_STATE:(g + 1) * SSD_STATE].astype(BF16)
            cg = c_ref[:, g * SSD_STATE:(g + 1) * SSD_STATE].astype(BF16)
            cb = _dot(cg, bg, NT)
            hprev = h_scr[gs, :]
            hall_ref[0, gs, :] = hprev
            gms, rhs = [], []
            xg = xdtb[:, gs]
            for e in range(HEADS_PER_GROUP):
                h = g * HEADS_PER_GROUP + e
                lm = jnp.exp(jnp.where(tril, cs[:, h:h + 1] - cst[h:h + 1, :], -1e30))
                gms.append((cb * lm).astype(BF16))
                rhs.append(jnp.where(head_of_lane == e, xg, jnp.zeros_like(xg)))
            y = _dot(jnp.concatenate(gms, axis=1), jnp.concatenate(rhs, axis=0), NN)
            y = y + ecsx[:, gs] * _dot(cg, hprev.astype(BF16), NT)
            y = y + dskx_ref[:, gs] * xs[:, gs]
            h_scr[gs, :] = hprev * _chunk_decay_rows(cs, g) + _dot(xdec[:, gs], bg, TN)
            ypre_ref[:, gs] = y
            z = z_ref[:, gs]
            yg = y * (z * _sigmoid(z))
            r = lax.rsqrt(jnp.mean(yg * yg, axis=-1, keepdims=True) + EPS)
            yssd_ref[:, gs] = (yg * r * nw_ref[:, gs]).astype(BF16)

    row_d = lambda cb: pl.BlockSpec((SSD_CHUNK, SSD_D_INNER), lambda c: (c, cb))
    small = pl.BlockSpec((1, LANE), lambda c: (0, 0))
    wide = pl.BlockSpec((1, SSD_D_INNER), lambda c: (0, 0))
    return pl.pallas_call(
        body, name="ssd_fwd", grid=(nc,),
        in_specs=[row_d(OFF_Z // SSD_D_INNER),
                  pl.BlockSpec((SSD_CHUNK, LANE), lambda c: (c, OFF_DT // LANE)),
                  row_d(0),
                  pl.BlockSpec((SSD_CHUNK, bc_w), lambda c: (c, SSD_D_INNER // bc_w)),
                  pl.BlockSpec((SSD_CHUNK, bc_w), lambda c: (c, SSD_D_INNER // bc_w + 1)),
                  small, small, wide, wide, pl.BlockSpec((LANE, SSD_D_INNER), lambda c: (0, 0))],
        out_specs=[row_d(0), row_d(0), pl.BlockSpec((1, SSD_D_INNER, SSD_STATE), lambda c: (c, 0, 0))],
        out_shape=[jax.ShapeDtypeStruct((s, SSD_D_INNER), F32), jax.ShapeDtypeStruct((s, SSD_D_INNER + ATT_D), BF16),
                   jax.ShapeDtypeStruct((nc, SSD_D_INNER, SSD_STATE), F32)],
        scratch_shapes=[pltpu.VMEM((SSD_D_INNER, SSD_STATE), F32)],
        compiler_params=_cparams(("arbitrary",)))(proj, proj, act, act, act, dtb, alog, _expand_heads(dsk), nw, exp_mat)


def _ssd_bwd(dycat, ypre, proj, act, hall, dtb, alog, dsk, nw, comm=None):
    s = proj.shape[0]
    nc = s // SSD_CHUNK
    bc_w = SSD_GROUPS * SSD_STATE

    exp_mat, ind4 = _ssd_constants()
    seg_passes = 1

    def body(dy_ref, ypre_ref, z_ref, dtr_ref, xs_ref, b_ref, c_ref, hall_ref, dtb_ref, alog_ref, dskx_ref, nw_ref,
             exp_ref, ind4_ref, dz_ref, dact_ref, ddtr_ref, da_ref, ddsk_ref, ddtb_ref, dnw_ref, dh_scr):
        @pl.when(pl.program_id(0) == 0)
        def _():
            dh_scr[...] = jnp.zeros_like(dh_scr)
            da_ref[...] = jnp.zeros_like(da_ref)
            ddsk_ref[...] = jnp.zeros_like(ddsk_ref)
            ddtb_ref[...] = jnp.zeros_like(ddtb_ref)
            dnw_ref[...] = jnp.zeros_like(dnw_ref)

        dtr = dtr_ref[...]
        dt, a, cs, cst, tril, lane, dtx, csx = _ssd_prep(dtr, dtb_ref[...], alog_ref[...], exp_ref[...])
        cs_last_x = csx[SSD_CHUNK - 1:SSD_CHUNK, :]
        xs = xs_ref[...]
        xdt = xs * dtx
        xdtb = xdt.astype(BF16)
        decx = jnp.exp(cs_last_x - csx)
        xdecf = xdt * decx
        xdec = xdecf.astype(BF16)
        ecsx = jnp.exp(csx)
        head_of_lane = lax.broadcasted_iota(jnp.int32, (1, GROUP_WIDTH), 1) // HEAD_DIM
        last_row = lax.broadcasted_iota(jnp.int32, (SSD_CHUNK, 1), 0) == SSD_CHUNK - 1
        dcs_col = jnp.zeros((SSD_CHUNK, LANE), F32)
        dcs_row = jnp.zeros((SSD_CHUNK, LANE), F32)
        ddt = jnp.zeros((SSD_CHUNK, LANE), F32)
        ddsk = jnp.zeros((1, LANE), F32)
        hsum = jnp.zeros((1, LANE), F32)
        t1_sum = jnp.zeros((1, LANE), F32)
        for g in range(SSD_GROUPS):
            gs = slice(g * GROUP_WIDTH, (g + 1) * GROUP_WIDTH)
            bsl = slice(g * SSD_STATE, (g + 1) * SSD_STATE)
            exp_g = exp_ref[:, gs]
            ind4_g = ind4_ref[g * HEADS_PER_GROUP * SSD_CHUNK:(g + 1) * HEADS_PER_GROUP * SSD_CHUNK, :]
            z = z_ref[:, gs]
            sg = _sigmoid(z)
            sz = z * sg
            ypre = ypre_ref[:, gs]
            yg = ypre * sz
            r = lax.rsqrt(jnp.mean(yg * yg, axis=-1, keepdims=True) + EPS)
            nrm = yg * r
            dyo_n = dy_ref[:, gs]
            dnw_ref[:, gs] += jnp.sum(dyo_n * nrm, axis=0, keepdims=True)
            dn = dyo_n * nw_ref[:, gs]
            dyg = r * (dn - nrm * jnp.mean(dn * nrm, axis=-1, keepdims=True))
            dz_ref[:, gs] = (dyg * ypre * (sg * (1.0 + z * (1.0 - sg)))).astype(BF16)
            dy = dyg * sz

            bg = b_ref[:, bsl].astype(BF16)
            cg = c_ref[:, bsl].astype(BF16)
            cb = _dot(cg, bg, NT)
            hprev = hall_ref[0, gs, :]
            hb = hprev.astype(BF16)
            dhn = dh_scr[gs, :]
            dhb = dhn.astype(BF16)
            xs_g, xdt_g = xs[:, gs], xdtb[:, gs]
            w_off = _dot(cg, hb, NT)
            dyo = dy * ecsx[:, gs]
            dyob = dyo.astype(BF16)
            dcg = _dot(dyob, hb, NN)
            dh_y = _dot(dyob, cg, TN)
            r_st = _dot(bg, dhb, NT)
            dbg = _dot(xdec[:, gs], dhb, NN)
            dyb = dy.astype(BF16)
            gms, gmbs, lms, dys = [], [], [], []
            for e in range(HEADS_PER_GROUP):
                h = g * HEADS_PER_GROUP + e
                lm = jnp.exp(jnp.where(tril, cs[:, h:h + 1] - cst[h:h + 1, :], -1e30))
                gm = cb * lm
                lms.append(lm)
                gms.append(gm)
                gmbs.append(gm.astype(BF16))
                dys.append(jnp.where(head_of_lane == e, dyb, jnp.zeros_like(dyb)))
            dxdt = _dot(jnp.concatenate(gmbs, axis=0), jnp.concatenate(dys, axis=0), TN) + decx[:, gs] * r_st
            dcb = jnp.zeros((SSD_CHUNK, SSD_CHUNK), F32)
            mms = []
            for e in range(HEADS_PER_GROUP):
                dg = _dot(dys[e], xdt_g, NT)
                mms.append(dg * gms[e])
                dcb = dcb + dg * lms[e]
            seg = _dot_split(jnp.concatenate([dyo * w_off, xdecf[:, gs] * r_st, dxdt * xs_g, dy * xs_g], axis=0), exp_g, NT, seg_passes)
            v1, t1, ddt_g, dsk_g = [seg[i * SSD_CHUNK:(i + 1) * SSD_CHUNK] for i in range(4)]
            dcs_col = dcs_col + v1 - t1 + _dot_split(jnp.concatenate(mms, axis=1), ind4_g, NN, seg_passes)
            for t in _split_bf16(jnp.concatenate(mms, axis=0), seg_passes):
                dcs_row = dcs_row + _dot(ind4_g, t, TN)
            ddt = ddt + ddt_g
            ddsk = ddsk + jnp.sum(dsk_g, axis=0, keepdims=True)
            t1_sum = t1_sum + jnp.sum(t1, axis=0, keepdims=True)
            for e in range(HEADS_PER_GROUP):
                h = g * HEADS_PER_GROUP + e
                hs = slice(e * HEAD_DIM, (e + 1) * HEAD_DIM)
                hsum = hsum + jnp.where(lane == h, jnp.sum(dhn[hs, :] * hprev[hs, :]).reshape(1, 1), 0.0)
            dh_scr[gs, :] = dhn * _chunk_decay_rows(cs, g) + dh_y
            dcbb = dcb.astype(BF16)
            dact_ref[:, gs] = dxdt * dtx[:, gs] + dskx_ref[:, gs] * dy
            dact_ref[:, SSD_D_INNER + g * SSD_STATE:SSD_D_INNER + (g + 1) * SSD_STATE] = dbg + _dot(dcbb, cg, TN)
            dact_ref[:, SSD_D_INNER + bc_w + g * SSD_STATE:SSD_D_INNER + bc_w + (g + 1) * SSD_STATE] = dcg + _dot(dcbb, bg, NN)
        dlast = t1_sum + jnp.exp(cs[SSD_CHUNK - 1:SSD_CHUNK, :]) * hsum
        dcs = dcs_col - dcs_row.T + jnp.where(last_row, dlast, 0.0)
        row = lax.broadcasted_iota(jnp.int32, (SSD_CHUNK, SSD_CHUNK), 0)
        col = lax.broadcasted_iota(jnp.int32, (SSD_CHUNK, SSD_CHUNK), 1)
        dda = _dot((col >= row).astype(F32), dcs, NN, precision=HIGHEST)
        ddt = ddt + dda * a
        da_ref[...] += jnp.sum(dda * dt, axis=0, keepdims=True)
        ddtr = jnp.where(lane < SSD_HEADS, ddt * _sigmoid(dtr + dtb_ref[...]), 0.0)
        ddtr_ref[...] = ddtr.astype(BF16)
        ddtb_ref[...] += jnp.sum(ddtr, axis=0, keepdims=True)
        ddsk_ref[...] += ddsk

    rev = lambda c: nc - 1 - c
    row_d = lambda cb: pl.BlockSpec((SSD_CHUNK, SSD_D_INNER), lambda c: (rev(c), cb))
    small = pl.BlockSpec((1, LANE), lambda c: (0, 0))
    wide = pl.BlockSpec((1, SSD_D_INNER), lambda c: (0, 0))
    small_shape = jax.ShapeDtypeStruct((1, LANE), F32)
    return _pcall(
        body, (dycat, ypre, proj, proj, act, act, act, hall, dtb, alog, _expand_heads(dsk), nw, exp_mat, ind4),
        name="ssd_bwd", grid=(nc,),
        in_specs=[row_d(0), row_d(0), row_d(OFF_Z // SSD_D_INNER),
                  pl.BlockSpec((SSD_CHUNK, LANE), lambda c: (rev(c), OFF_DT // LANE)),
                  row_d(0),
                  pl.BlockSpec((SSD_CHUNK, bc_w), lambda c: (rev(c), SSD_D_INNER // bc_w)),
                  pl.BlockSpec((SSD_CHUNK, bc_w), lambda c: (rev(c), SSD_D_INNER // bc_w + 1)),
                  pl.BlockSpec((1, SSD_D_INNER, SSD_STATE), lambda c: (rev(c), 0, 0)),
                  small, small, wide, wide, pl.BlockSpec((LANE, SSD_D_INNER), lambda c: (0, 0)),
                  pl.BlockSpec((SSD_HEADS * SSD_CHUNK, LANE), lambda c: (0, 0))],
        out_specs=[row_d(0), pl.BlockSpec((SSD_CHUNK, CONV_CH), lambda c: (rev(c), 0)),
                   pl.BlockSpec((SSD_CHUNK, LANE), lambda c: (rev(c), 0)), small, small, small, wide],
        out_shape=[jax.ShapeDtypeStruct((s, SSD_D_INNER), BF16), jax.ShapeDtypeStruct((s, CONV_CH), F32),
                   jax.ShapeDtypeStruct((s, LANE), BF16), small_shape, small_shape, small_shape,
                   jax.ShapeDtypeStruct((1, SSD_D_INNER), F32)],
        scratch_shapes=[pltpu.VMEM((SSD_D_INNER, SSD_STATE), F32)], sem=("arbitrary",), comm=comm)


def _head_mean_matrix():
    row = lax.broadcasted_iota(jnp.int32, (LANE, LANE), 0) // HEAD_DIM
    col = lax.broadcasted_iota(jnp.int32, (LANE, LANE), 1) // HEAD_DIM
    return (row == col).astype(F32)


def _head_sum2(v, ones_bd):
    hi = v.astype(BF16)
    lo = (v - hi.astype(F32)).astype(BF16)
    return _dot(jnp.concatenate([hi, lo], axis=1), jnp.concatenate([ones_bd, ones_bd], axis=0), NN)


def _head_norms(xs, ws, ones_bd):
    sums = [_head_sum2(x * x, ones_bd) for x in xs]
    rs = [lax.rsqrt(ms * (1.0 / HEAD_DIM) + EPS) for ms in sums]
    return [(x * r) * w for x, r, w in zip(xs, rs, ws)], rs


def _head_norms_bwd(dns, xs, ws, rs, ones_bd):
    nrms = [x * r for x, r in zip(xs, rs)]
    dnws = [dn * w for dn, w in zip(dns, ws)]
    projs = [_head_sum2(dnw * nrm, ones_bd) for dnw, nrm in zip(dnws, nrms)]
    dxs = [r * (dnw - nrm * (pr * (1.0 / HEAD_DIM))) for r, dnw, nrm, pr in zip(rs, dnws, nrms, projs)]
    return dxs, [jnp.sum(dn * nrm, axis=0, keepdims=True) for dn, nrm in zip(dns, nrms)]


NORM_CHUNKS = 2


PRO_ROWS = 256
ATT_GROUP_FWD = 32
ATT_GROUP_BWD = 8
KEYS = 2 * ATT_BLK
NEG = -1e30
HALF = HEAD_DIM // 2


def _rows(start, size, dil):
    return pl.ds(start, size) if dil == 1 else pl.ds(start, size, stride=dil)


def _fill_bias(bias_ref):
    row = lax.broadcasted_iota(jnp.int32, (ATT_BLK, 2 * KEYS), 0)
    col = lax.broadcasted_iota(jnp.int32, (ATT_BLK, 2 * KEYS), 1) & (KEYS - 1)
    for first, off in ((0, 0), (1, ATT_BLK)):
        dist = off + row - col
        bias_ref[first] = jnp.where((dist >= 0) & (dist <= ATT_BLK), 0.0, NEG)


def _pair(a, b):
    return jnp.concatenate([jnp.broadcast_to(a, (ATT_BLK, KEYS)), jnp.broadcast_to(b, (ATT_BLK, KEYS))], axis=1)


def _split_heads(x, is_a):
    zero = jnp.zeros_like(x)
    return jnp.concatenate([jnp.where(is_a, x, zero), jnp.where(is_a, zero, x)], axis=0)


def _block_ids(b, nb):
    i = b & (nb - 1)
    q0 = pl.multiple_of(b * ATT_BLK, ATT_BLK)
    k0 = pl.multiple_of((b - jnp.minimum(i, 1)) * ATT_BLK, ATT_BLK)
    return pl.ds(q0, ATT_BLK), pl.ds(k0, KEYS), jnp.minimum(i, 1)


def _natural_rows(b, nb, dil):
    if dil == 1:
        return pl.ds(pl.multiple_of(b * ATT_BLK, ATT_BLK), ATT_BLK)
    return pl.ds(b // nb + dil * ((b & (nb - 1)) * ATT_BLK), ATT_BLK, stride=dil)


def _att_fwd(proj, qw, kw, comm=None):
    s = proj.shape[0]
    nblk = s // ATT_BLK
    assert all((s // d) // ATT_BLK >= 2 for d in DILATIONS)
    blk = lambda off: pl.BlockSpec((s, LANE), lambda i: (0, off // LANE + i))
    wspec = pl.BlockSpec((1, LANE), lambda i: (0, i))
    oblk = pl.BlockSpec((s, LANE), lambda i: (0, i))

    def body(q_ref, k_ref, v_ref, qw_ref, kw_ref, o_ref, lse_ref, qn, kn, q_cm, k_cm, v_cm, m_acc, l_acc, o_d, m_d, l_d,
             o_e, m_e, l_e, tq, tk, tv, bias):
        ones_bd = _head_mean_matrix().astype(BF16)
        is_a = lax.broadcasted_iota(jnp.int32, (1, LANE), 1) < HEAD_DIM
        ones_ext = _split_heads(jnp.ones((KEYS, LANE), BF16), is_a)
        _fill_bias(bias)

        def pro(j, c):
            chunks = [pl.ds(pl.multiple_of((NORM_CHUNKS * j + u) * PRO_ROWS, PRO_ROWS), PRO_ROWS) for u in range(NORM_CHUNKS)]
            normed, _ = _head_norms([q_ref[rows, :] for rows in chunks] + [k_ref[rows, :] for rows in chunks],
                                    [qw_ref[...] * HEAD_DIM ** -0.5] * NORM_CHUNKS + [kw_ref[...]] * NORM_CHUNKS, ones_bd)
            for u, rows in enumerate(chunks):
                qn[rows, :] = normed[u]
                kn[rows, :] = normed[NORM_CHUNKS + u]
            return c

        lax.fori_loop(0, s // (NORM_CHUNKS * PRO_ROWS), pro, 0)

        results = dict(zip(DILATIONS, ((o_ref, m_acc, l_acc), (o_d, m_d, l_d), (o_e, m_e, l_e))))
        for dil in DILATIONS:
            ln = s // dil
            nb = ln // ATT_BLK
            o_out, m_out, l_out = results[dil]
            level = DILATIONS.index(dil)
            keep_f32 = 0 < level < len(DILATIONS) - 1
            from_temps = level >= 2
            step_rows = dil // DILATIONS[level - 1] if from_temps else dil
            for r in range(dil):
                prev = DILATIONS[level - 1] if from_temps else 1
                start = (r % prev) * (s // prev) + r // prev if from_temps else r

                def relayout(j, c, r=r, ln=ln, start=start, step_rows=step_rows, keep_f32=keep_f32, from_temps=from_temps):
                    j0 = pl.multiple_of(j * PRO_ROWS, PRO_ROWS)
                    src = _rows(start + step_rows * j0, PRO_ROWS, step_rows)
                    dst = pl.ds(r * ln + j0, PRO_ROWS)
                    qv, kv, vv = (tq[src, :], tk[src, :], tv[src, :]) if from_temps else (qn[src, :], kn[src, :], v_ref[src, :])
                    q_cm[dst, :] = qv.astype(BF16)
                    k_cm[dst, :] = kv.astype(BF16)
                    v_cm[dst, :] = vv.astype(BF16)
                    if keep_f32:
                        tq[dst, :] = qv
                        tk[dst, :] = kv
                        tv[dst, :] = vv
                    return c

                lax.fori_loop(0, ln // PRO_ROWS, relayout, 0)

            def step(bg, c, nb=nb, o_out=o_out, m_out=m_out, l_out=l_out):
                ids = [_block_ids(bg * ATT_GROUP_FWD + u, nb) for u in range(ATT_GROUP_FWD)]
                kbs = [_split_heads(k_cm[krows, :], is_a) for _, krows, _ in ids]
                scs = [_dot(q_cm[qrows, :], kb, NT) + bias[first] for (qrows, _, first), kb in zip(ids, kbs)]
                mas = [jnp.max(sc[:, :KEYS], axis=-1, keepdims=True) for sc in scs]
                mbs = [jnp.max(sc[:, KEYS:], axis=-1, keepdims=True) for sc in scs]
                ps = [jnp.exp(sc - _pair(ma, mb)).astype(BF16) for sc, ma, mb in zip(scs, mas, mbs)]
                vbs = [jnp.concatenate([_split_heads(v_cm[krows, :], is_a), ones_ext], axis=1) for _, krows, _ in ids]
                ols = [_dot(p, vb, NN) for p, vb in zip(ps, vbs)]
                for (qrows, _, _), ol, ma, mb in zip(ids, ols, mas, mbs):
                    o_out[qrows, :] = ol[:, :LANE]
                    l_out[qrows, :] = ol[:, LANE:]
                    m_out[qrows, :] = jnp.where(is_a, ma, mb)
                return c

            lax.fori_loop(0, nblk // ATT_GROUP_FWD, step, 0)

        for level in range(len(DILATIONS) - 1, 0, -1):
            fine_d, coarse_d = DILATIONS[level - 1], DILATIONS[level]
            ratio, ln_f, ln_c = coarse_d // fine_d, s // fine_d, s // coarse_d
            (o_f, m_f, l_f), (o_c, m_c, l_c) = results[fine_d], results[coarse_d]
            for r in range(coarse_d):
                def merge(j, c, r=r, ratio=ratio, ln_c=ln_c, start=(r % fine_d) * ln_f + r // fine_d,
                          o_f=o_f, m_f=m_f, l_f=l_f, o_c=o_c, m_c=m_c, l_c=l_c):
                    j0 = pl.multiple_of(j * PRO_ROWS, PRO_ROWS)
                    fine = _rows(start + ratio * j0, PRO_ROWS, ratio)
                    coarse = pl.ds(r * ln_c + j0, PRO_ROWS)
                    m_old, m_new = m_f[fine, :], m_c[coarse, :]
                    m = jnp.maximum(m_old, m_new)
                    a_old, a_new = jnp.exp(m_old - m), jnp.exp(m_new - m)
                    o_f[fine, :] = a_old * o_f[fine, :] + a_new * o_c[coarse, :]
                    l_f[fine, :] = a_old * l_f[fine, :] + a_new * l_c[coarse, :]
                    m_f[fine, :] = m
                    return c

                lax.fori_loop(0, ln_c // PRO_ROWS, merge, 0)

        def epi(j, c):
            rows = pl.ds(pl.multiple_of(j * PRO_ROWS, PRO_ROWS), PRO_ROWS)
            l = l_acc[rows, :]
            o_ref[rows, :] = o_ref[rows, :] / l
            lse_ref[rows, :] = m_acc[rows, :] + jnp.log(l)
            return c

        lax.fori_loop(0, s // PRO_ROWS, epi, 0)

    f = jax.ShapeDtypeStruct((s, ATT_D), F32)
    scr = pltpu.VMEM((s, LANE), F32)
    scb = pltpu.VMEM((s, LANE), BF16)
    return _pcall(
        body, (proj, proj, proj, qw, kw), name="att_fwd", grid=(ATT_D // LANE,),
        in_specs=[blk(OFF_Q), blk(OFF_K), blk(OFF_V), wspec, wspec], out_specs=[oblk, oblk], out_shape=[f, f],
        scratch_shapes=[scr, scr, scb, scb, scb] + [scr] * 11 + [pltpu.VMEM((2, ATT_BLK, 2 * KEYS), F32)],
        sem=("parallel",), comm=comm)


def _att_bwd(proj, do, stats, qw, kw, comm=None):
    s = proj.shape[0]
    nblk = s // ATT_BLK
    blk = lambda off: pl.BlockSpec((s, LANE), lambda i: (0, off // LANE + i))
    wspec = pl.BlockSpec((1, LANE), lambda i: (0, i))
    oblk = pl.BlockSpec((s, LANE), lambda i: (0, i))

    def body(q_ref, k_ref, v_ref, do_ref, st_ref, qw_ref, kw_ref, dq_ref, dk_ref, dv_ref, dqw_ref, dkw_ref,
             qn, kn, q_cm, do_cm, k_cm, v_cm, rms, dq_acc, dk_acc, dv_acc, dq_d, dk_d, dv_d, dq_e, dk_e, dv_e, bias):
        ones_bd = _head_mean_matrix().astype(BF16)
        is_a = lax.broadcasted_iota(jnp.int32, (1, LANE), 1) < HEAD_DIM
        first_half = (lax.broadcasted_iota(jnp.int32, (1, LANE), 1) & (HEAD_DIM - 1)) < HALF
        _fill_bias(bias)
        zero = jnp.zeros((PRO_ROWS, LANE), F32)
        results = dict(zip(DILATIONS, ((dq_acc, dk_acc, dv_acc), (dq_d, dk_d, dv_d), (dq_e, dk_e, dv_e))))

        def pro(j, c):
            chunks = [pl.ds(pl.multiple_of((NORM_CHUNKS * j + u) * PRO_ROWS, PRO_ROWS), PRO_ROWS) for u in range(NORM_CHUNKS)]
            normed, rs = _head_norms([q_ref[rows, :] for rows in chunks] + [k_ref[rows, :] for rows in chunks],
                                     [qw_ref[...] * HEAD_DIM ** -0.5] * NORM_CHUNKS + [kw_ref[...]] * NORM_CHUNKS, ones_bd)
            for u, rows in enumerate(chunks):
                qn[rows, :] = normed[u]
                kn[rows, :] = normed[NORM_CHUNKS + u]
                rms[rows, :] = jnp.where(first_half, rs[u], rs[NORM_CHUNKS + u])
                dk_acc[rows, :] = zero
                dv_acc[rows, :] = zero
            return c

        lax.fori_loop(0, s // (NORM_CHUNKS * PRO_ROWS), pro, 0)

        for dil in DILATIONS:
            ln = s // dil
            nb = ln // ATT_BLK
            dq_o, dk_o, dv_o = results[dil]
            level = DILATIONS.index(dil)
            keep_f32 = 0 < level < len(DILATIONS) - 1
            from_temps = level >= 2
            temps = results[DILATIONS[-1]]
            for r in range(dil):
                prev = DILATIONS[level - 1] if from_temps else 1
                start = (r % prev) * (s // prev) + r // prev if from_temps else r

                def relayout(j, c, dil=dil, r=r, ln=ln, start=start, step_rows=dil // prev):
                    j0 = pl.multiple_of(j * PRO_ROWS, PRO_ROWS)
                    nat = _rows(r + dil * j0, PRO_ROWS, dil)
                    src = _rows(start + step_rows * j0, PRO_ROWS, step_rows)
                    dst = pl.ds(r * ln + j0, PRO_ROWS)
                    qv, kv, vv = [t[src, :] for t in temps] if from_temps else (qn[src, :], kn[src, :], v_ref[src, :])
                    q_cm[dst, :] = qv.astype(BF16)
                    k_cm[dst, :] = kv.astype(BF16)
                    v_cm[dst, :] = vv.astype(BF16)
                    do_cm[dst, :] = do_ref[nat, :].astype(BF16)
                    if keep_f32:
                        for t, val in zip(temps, (qv, kv, vv)):
                            t[dst, :] = val
                    return c

                lax.fori_loop(0, ln // PRO_ROWS, relayout, 0)

            if dil > 1:
                def clear(j, c, dk_o=dk_o, dv_o=dv_o):
                    rows = pl.ds(pl.multiple_of(j * PRO_ROWS, PRO_ROWS), PRO_ROWS)
                    dk_o[rows, :] = zero
                    dv_o[rows, :] = zero
                    return c

                lax.fori_loop(0, s // PRO_ROWS, clear, 0)

            def step(bg, c, nb=nb, dil=dil, dq_o=dq_o, dk_o=dk_o, dv_o=dv_o):
                blocks = [bg * ATT_GROUP_BWD + u for u in range(ATT_GROUP_BWD)]
                ids = [_block_ids(b, nb) for b in blocks]
                qbs = [q_cm[qrows, :] for qrows, _, _ in ids]
                dobs = [do_cm[qrows, :] for qrows, _, _ in ids]
                kbs = [_split_heads(k_cm[krows, :], is_a) for _, krows, _ in ids]
                vbs = [_split_heads(v_cm[krows, :], is_a) for _, krows, _ in ids]
                sts = [st_ref[_natural_rows(b, nb, dil), :] for b in blocks]
                scs = [_dot(qb, kb, NT) + bias[first] for qb, kb, (_, _, first) in zip(qbs, kbs, ids)]
                dps = [_dot(dob, vb, NT) for dob, vb in zip(dobs, vbs)]
                ps = [jnp.exp(sc - _pair(st[:, 0:1], st[:, HEAD_DIM:HEAD_DIM + 1])) for sc, st in zip(scs, sts)]
                dss = [(p * (dp - _pair(st[:, HALF:HALF + 1], st[:, HEAD_DIM + HALF:HEAD_DIM + HALF + 1]))).astype(BF16)
                       for p, dp, st in zip(ps, dps, sts)]
                dqs = [_dot(ds, kb, NN) for ds, kb in zip(dss, kbs)]
                dkfs = [_dot(ds, qb, TN) for ds, qb in zip(dss, qbs)]
                dvfs = [_dot(p.astype(BF16), dob, TN) for p, dob in zip(ps, dobs)]
                for (qrows, krows, _), dq, dkf, dvf in zip(ids, dqs, dkfs, dvfs):
                    dq_o[qrows, :] = dq
                    dk_o[krows, :] += jnp.where(is_a, dkf[:KEYS], dkf[KEYS:])
                    dv_o[krows, :] += jnp.where(is_a, dvf[:KEYS], dvf[KEYS:])
                return c

            lax.fori_loop(0, nblk // ATT_GROUP_BWD, step, 0)

        for level in range(len(DILATIONS) - 1, 0, -1):
            fine_d, coarse_d = DILATIONS[level - 1], DILATIONS[level]
            ratio, ln_f, ln_c = coarse_d // fine_d, s // fine_d, s // coarse_d
            for r in range(coarse_d):
                def merge(j, c, r=r, ratio=ratio, ln_c=ln_c, start=(r % fine_d) * ln_f + r // fine_d,
                          fine_bufs=results[fine_d], coarse_bufs=results[coarse_d]):
                    j0 = pl.multiple_of(j * PRO_ROWS, PRO_ROWS)
                    fine = _rows(start + ratio * j0, PRO_ROWS, ratio)
                    coarse = pl.ds(r * ln_c + j0, PRO_ROWS)
                    for f_buf, c_buf in zip(fine_bufs, coarse_bufs):
                        f_buf[fine, :] += c_buf[coarse, :]
                    return c

                lax.fori_loop(0, ln_c // PRO_ROWS, merge, 0)

        def epi(j, c):
            chunks = [pl.ds(pl.multiple_of((NORM_CHUNKS * j + u) * PRO_ROWS, PRO_ROWS), PRO_ROWS) for u in range(NORM_CHUNKS)]
            packed = [rms[rows, :] for rows in chunks]
            rs = ([jnp.where(first_half, p, pltpu.roll(p, HALF, axis=1)) for p in packed]
                  + [jnp.where(first_half, pltpu.roll(p, LANE - HALF, axis=1), p) for p in packed])
            dxs, dws = _head_norms_bwd(
                [dq_acc[rows, :] for rows in chunks] + [dk_acc[rows, :] for rows in chunks],
                [q_ref[rows, :] for rows in chunks] + [k_ref[rows, :] for rows in chunks],
                [qw_ref[...] * HEAD_DIM ** -0.5] * NORM_CHUNKS + [kw_ref[...]] * NORM_CHUNKS, rs, ones_bd)
            dqw, dkw = c
            for u, rows in enumerate(chunks):
                dq_ref[rows, :] = dxs[u].astype(BF16)
                dk_ref[rows, :] = dxs[NORM_CHUNKS + u].astype(BF16)
                dv_ref[rows, :] = dv_acc[rows, :].astype(BF16)
                dqw, dkw = dqw + dws[u], dkw + dws[NORM_CHUNKS + u]
            return dqw, dkw

        zrow = jnp.zeros((1, LANE), F32)
        dqw, dkw = lax.fori_loop(0, s // (NORM_CHUNKS * PRO_ROWS), epi, (zrow, zrow))
        dqw_ref[...] = dqw * HEAD_DIM ** -0.5
        dkw_ref[...] = dkw

    o = jax.ShapeDtypeStruct((s, ATT_D), BF16)
    ov = jax.ShapeDtypeStruct((1, ATT_D), F32)
    scr = pltpu.VMEM((s, LANE), F32)
    scb = pltpu.VMEM((s, LANE), BF16)
    return _pcall(
        body, (proj, proj, proj, do, stats, qw, kw), name="att_bwd", grid=(ATT_D // LANE,),
        in_specs=[blk(OFF_Q), blk(OFF_K), blk(OFF_V), oblk, oblk, wspec, wspec],
        out_specs=[oblk, oblk, oblk, wspec, wspec], out_shape=[o, o, o, ov, ov],
        scratch_shapes=[scr, scr, scb, scb, scb, scb] + [scr] * 10 + [pltpu.VMEM((2, ATT_BLK, 2 * KEYS), F32)],
        sem=("parallel",), comm=comm)


def _att_norm_fwd(o, nw, ycat):
    s = o.shape[0]
    row = pl.BlockSpec((ROW_TILE, ATT_D), lambda i: (i, 0))
    vec = pl.BlockSpec((1, ATT_D), lambda i: (0, 0))

    def body(o_ref, nw_ref, ycat_ref, y_ref):
        o = o_ref[...]
        r = lax.rsqrt(jnp.mean(o * o, axis=-1, keepdims=True) + EPS)
        y_ref[...] = (o * r * nw_ref[...]).astype(BF16)

    return pl.pallas_call(body, name="att_norm_fwd", grid=(s // ROW_TILE,),
                          in_specs=[row, vec, pl.BlockSpec(memory_space=pl.ANY)],
                          out_specs=pl.BlockSpec((ROW_TILE, ATT_D), lambda i: (i, 1)),
                          out_shape=jax.ShapeDtypeStruct(ycat.shape, BF16), input_output_aliases={2: 0},
                          compiler_params=_cparams(("parallel",)))(o, nw, ycat)


def _mixer_split_epilogue(dycat, first, rows, vecs, outs):
    (o_ref, lse_ref), (nw_ref,), (dyssd_ref, do_ref, st_ref, dnw_ref) = rows, vecs, outs

    @pl.when(first)
    def _():
        dnw_ref[...] = jnp.zeros_like(dnw_ref)

    dyssd_ref[...] = dycat[:, :SSD_D_INNER]
    dy = dycat[:, SSD_D_INNER:]
    o = o_ref[...]
    r = lax.rsqrt(jnp.mean(o * o, axis=-1, keepdims=True) + EPS)
    nrm = o * r
    dnw_ref[...] += jnp.sum(dy * nrm, axis=0, keepdims=True)
    dn = dy * nw_ref[...]
    do = r * (dn - nrm * jnp.mean(dn * nrm, axis=-1, keepdims=True))
    do_ref[...] = do
    ones_bd = _head_mean_matrix().astype(BF16)
    prod = do * o
    delta = jnp.concatenate([_head_sum2(prod[:, j * LANE:(j + 1) * LANE], ones_bd) for j in range(ATT_D // LANE)], axis=1)
    lane = lax.broadcasted_iota(jnp.int32, (1, ATT_D), 1)
    st_ref[...] = jnp.where((lane & (HEAD_DIM - 1)) < HALF, lse_ref[...], delta)


def _ada_fwd(c_all, w_ada):
    def body(c_ref, w_ref, o_ref):
        cv = c_ref[...]
        o_ref[...] = _dot((cv * _sigmoid(cv)).astype(BF16), w_ref[...].astype(BF16), NN)

    return pl.pallas_call(body, name="ada_fwd", out_shape=jax.ShapeDtypeStruct((c_all.shape[0], w_ada.shape[1]), F32),
                          compiler_params=_cparams())(c_all, w_ada)


def _adamw_math(g, w, m, v):
    m_new = ADAM_B1 * m + (1.0 - ADAM_B1) * g
    v_new = ADAM_B2 * v + (1.0 - ADAM_B2) * (g * g)
    m_hat = m_new / (1.0 - ADAM_B1 ** ADAM_STEP)
    v_hat = v_new / (1.0 - ADAM_B2 ** ADAM_STEP)
    delta = -ADAM_LR * (m_hat / (jnp.sqrt(v_hat) + ADAM_EPS) + ADAM_WD * w)
    return delta, m_new, v_new


def _ada_bwd_adamw(c_all, dmod_cols, w, m, v):
    rows, cols = w.shape
    tr = 256
    blk = pl.BlockSpec((tr, cols), lambda i: (i, 0))

    def body(c_ref, d_ref, w_ref, m_ref, v_ref, g_ref, dl_ref, mo_ref, vo_ref):
        cv = c_ref[...]
        ca = cv * _sigmoid(cv)
        g = ca[:, 0:1] * d_ref[0:1, :]
        for b in range(1, N_DEV):
            g = g + ca[:, b:b + 1] * d_ref[b:b + 1, :]
        g_ref[...] = g
        dl_ref[...], mo_ref[...], vo_ref[...] = _adamw_math(g, w_ref[...], m_ref[...], v_ref[...])

    o = jax.ShapeDtypeStruct((rows, cols), F32)
    return pl.pallas_call(
        body, name="ada_bwd_adamw", grid=(rows // tr,),
        in_specs=[pl.BlockSpec((tr, N_DEV), lambda i: (i, 0)), pl.BlockSpec((N_DEV, cols), lambda i: (0, 0)), blk, blk, blk],
        out_specs=[blk] * 4, out_shape=[o, o, o, o], compiler_params=_cparams(("parallel",)))(c_all.T, dmod_cols, w, m, v)


def _reduce_adamw(slabs, w, m, v, name):
    rows, cols = w.shape
    n_src = slabs.shape[0]
    if rows % 128 == 0:
        tr, steps = 128, rows // 128
        blk = pl.BlockSpec((tr, cols), lambda i: (i, 0))
        sblk = pl.BlockSpec((n_src, tr, cols), lambda i: (0, i, 0))
    else:
        tc, steps = 256, cols // 256
        blk = pl.BlockSpec((rows, tc), lambda i: (0, i))
        sblk = pl.BlockSpec((n_src, rows, tc), lambda i: (0, 0, i))

    def body(s_ref, w_ref, m_ref, v_ref, g_ref, dl_ref, mo_ref, vo_ref):
        g = s_ref[0].astype(F32)
        for src in range(1, n_src):
            g = g + s_ref[src].astype(F32)
        g_ref[...] = g
        dl_ref[...], mo_ref[...], vo_ref[...] = _adamw_math(g, w_ref[...], m_ref[...], v_ref[...])

    o = jax.ShapeDtypeStruct((rows, cols), F32)
    return pl.pallas_call(
        body, name=name, grid=(steps,), in_specs=[sblk, blk, blk, blk],
        out_specs=[blk] * 4, out_shape=[o, o, o, o], compiler_params=_cparams(("parallel",)))(slabs, w, m, v)


def _small_reduce_adamw(gathered, w, m, v):
    def body(s_ref, w_ref, m_ref, v_ref, g_ref, dl_ref, mo_ref, vo_ref):
        g = s_ref[0]
        for dev in range(1, N_DEV):
            g = g + s_ref[dev]
        g_ref[...] = g
        dl_ref[...], mo_ref[...], vo_ref[...] = _adamw_math(g, w_ref[...], m_ref[...], v_ref[...])

    o = jax.ShapeDtypeStruct(w.shape, F32)
    return pl.pallas_call(body, name="small_reduce_adamw", out_shape=[o, o, o, o], compiler_params=_cparams())(gathered, w, m, v)


def _adamw_small(g, w, m, v, name):
    def body(g_ref, w_ref, m_ref, v_ref, dl_ref, mo_ref, vo_ref):
        dl_ref[...], mo_ref[...], vo_ref[...] = _adamw_math(g_ref[...], w_ref[...], m_ref[...], v_ref[...])

    o = jax.ShapeDtypeStruct(w.shape, F32)
    return pl.pallas_call(body, name=name, out_shape=[o, o, o], compiler_params=_cparams())(g, w, m, v)


class _Exchange:
    def __init__(self, arrs, scatter):
        self.arrs, self.scatter, self.n = list(arrs), scatter, len(arrs)
        hbm = pl.BlockSpec(memory_space=pltpu.HBM)
        self.in_specs = [hbm] * self.n
        self.out_specs = [hbm] * self.n
        self.out_shape = [jax.ShapeDtypeStruct(a.shape if scatter else (N_DEV,) + a.shape, a.dtype) for a in self.arrs]
        self.scratch = [pltpu.SemaphoreType.DMA((self.n * (N_DEV - 1),)), pltpu.SemaphoreType.DMA((self.n * (N_DEV - 1),)),
                        pltpu.SemaphoreType.DMA((self.n,))]

    def _local(self, ins, outs, sems):
        me = 4 * lax.axis_index("x") + 2 * lax.axis_index("y") + lax.axis_index("c")
        return [pltpu.make_async_copy(ins[a].at[me] if self.scatter else ins[a], outs[a].at[me], sems[2].at[a])
                for a in range(self.n)]

    def _remote(self, ins, outs, sems, arriving):
        send_sems, recv_sems, _ = sems
        x, y, c = lax.axis_index("x"), lax.axis_index("y"), lax.axis_index("c")
        me = 4 * x + 2 * y + c
        remote = []
        for a in range(self.n):
            for k in range(1, N_DEV):
                px = 1 - x if k & 4 else x
                py = 1 - y if k & 2 else y
                pc = 1 - c if k & 1 else c
                peer = 4 * px + 2 * py + pc
                sem = a * (N_DEV - 1) + k - 1
                remote.append(pltpu.make_async_remote_copy(
                    src_ref=ins[a].at[peer] if self.scatter else ins[a], dst_ref=outs[a].at[peer if arriving else me],
                    send_sem=send_sems.at[sem], recv_sem=recv_sems.at[sem], device_id=(px, py, pc), device_id_type=MESH_IDS))
        return remote

    def start(self, ins, outs, sems):
        for cp in self._local(ins, outs, sems) + self._remote(ins, outs, sems, arriving=False):
            cp.start()

    def forward(self, ins, outs, sems):
        pass

    def wait(self, ins, outs, sems):
        for send, arrival in zip(self._remote(ins, outs, sems, arriving=False), self._remote(ins, outs, sems, arriving=True)):
            send.wait_send()
            arrival.wait_recv()
        for cp in self._local(ins, outs, sems):
            cp.wait()


N_CHIP = N_DEV // 2


class _SiblingSwap(_Exchange):
    def __init__(self, arrs):
        super().__init__(arrs, scatter=True)
        self.out_shape = [jax.ShapeDtypeStruct((N_CHIP,) + a.shape[2:], a.dtype) for a in self.arrs]
        self.scratch = [pltpu.SemaphoreType.DMA((self.n,)), pltpu.SemaphoreType.DMA((self.n,)), pltpu.SemaphoreType.DMA((1,))]

    def _copies(self, ins, outs, sems):
        x, y, c = lax.axis_index("x"), lax.axis_index("y"), lax.axis_index("c")
        return [pltpu.make_async_remote_copy(src_ref=ins[a].at[:, 1 - c], dst_ref=outs[a], send_sem=sems[0].at[a], recv_sem=sems[1].at[a],
                                             device_id=(x, y, 1 - c), device_id_type=MESH_IDS) for a in range(self.n)]

    def start(self, ins, outs, sems):
        for cp in self._copies(ins, outs, sems):
            cp.start()

    def wait(self, ins, outs, sems):
        for cp in self._copies(ins, outs, sems):
            cp.wait()


class _ChipScatter(_Exchange):
    def __init__(self, arrs):
        super().__init__(arrs, scatter=True)
        n_pairs = self.n * (N_CHIP - 1)
        self.scratch = [pltpu.SemaphoreType.DMA((n_pairs,)), pltpu.SemaphoreType.DMA((n_pairs,)), pltpu.SemaphoreType.DMA((self.n,))]

    def _local(self, ins, outs, sems):
        chip = 2 * lax.axis_index("x") + lax.axis_index("y")
        return [pltpu.make_async_copy(ins[a].at[chip], outs[a].at[chip], sems[2].at[a]) for a in range(self.n)]

    def _remote(self, ins, outs, sems, arriving):
        send_sems, recv_sems, _ = sems
        x, y, c = lax.axis_index("x"), lax.axis_index("y"), lax.axis_index("c")
        chip = 2 * x + y
        remote = []
        for a in range(self.n):
            for k in range(1, N_CHIP):
                px = 1 - x if k & 2 else x
                py = 1 - y if k & 1 else y
                peer = 2 * px + py
                sem = a * (N_CHIP - 1) + k - 1
                remote.append(pltpu.make_async_remote_copy(
                    src_ref=ins[a].at[peer], dst_ref=outs[a].at[peer if arriving else chip], send_sem=send_sems.at[sem],
                    recv_sem=recv_sems.at[sem], device_id=(px, py, c), device_id_type=MESH_IDS))
        return remote


def _chip_sum(mine, theirs):
    n, rows, cols = mine.shape
    blk = pl.BlockSpec((1, rows, 256), lambda q, j: (q, 0, j))

    def body(a_ref, b_ref, o_ref):
        o_ref[...] = (a_ref[...].astype(F32) + b_ref[...].astype(F32)).astype(BF16)

    return pl.pallas_call(body, name="chip_sum", grid=(n, cols // 256), in_specs=[blk, blk], out_specs=blk,
                          out_shape=jax.ShapeDtypeStruct(mine.shape, BF16),
                          compiler_params=_cparams(("parallel", "parallel")))(mine, theirs)


class _Gather2(_Exchange):
    def __init__(self, arrs):
        super().__init__(arrs, scatter=False)

    def _copies(self, ins, outs, sems):
        send_sems, recv_sems, _ = sems
        x, y, c = lax.axis_index("x"), lax.axis_index("y"), lax.axis_index("c")
        sibling = (x, y, 1 - c)
        chips = [(1 - x, y), (x, 1 - y), (1 - x, 1 - y)]
        first, passed, landed = [], [], []
        for a in range(self.n):
            def copy(k, block, to, src=None, a=a):
                slab = outs[a].at[4 * block[0] + 2 * block[1] + block[2]]
                return pltpu.make_async_remote_copy(
                    src_ref=slab if src is None else src, dst_ref=slab, send_sem=send_sems.at[a * (N_DEV - 1) + k],
                    recv_sem=recv_sems.at[a * (N_DEV - 1) + k], device_id=to, device_id_type=MESH_IDS)

            first.append(copy(0, (x, y, c), sibling, src=ins[a]))
            landed.append(copy(0, sibling, sibling))
            for j, chip in enumerate(chips):
                first.append(copy(1 + j, (x, y, c), (*chip, c), src=ins[a]))
                passed.append((copy(1 + j, (*chip, c), sibling), copy(4 + j, (*chip, c), sibling)))
                landed.append(copy(4 + j, (*chip, 1 - c), sibling))
        return first, passed, landed

    def start(self, ins, outs, sems):
        for cp in self._local(ins, outs, sems) + self._copies(ins, outs, sems)[0]:
            cp.start()

    def forward(self, ins, outs, sems):
        for arrival, onward in self._copies(ins, outs, sems)[1]:
            arrival.wait_recv()
            onward.start()

    def wait(self, ins, outs, sems):
        first, passed, landed = self._copies(ins, outs, sems)
        for arrival in landed:
            arrival.wait_recv()
        for cp in first + [onward for _, onward in passed]:
            cp.wait_send()
        for cp in self._local(ins, outs, sems):
            cp.wait()


def _split_comm_refs(refs, n_in, n_out, n_scr, comm):
    nc = comm.n if comm is not None else 0
    ns = 3 if comm is not None else 0
    pos, groups = 0, []
    for cnt in (n_in, nc, n_out, nc, n_scr, ns):
        groups.append(refs[pos:pos + cnt])
        pos += cnt
    assert pos == len(refs), (pos, len(refs))
    return groups


def _pcall(body, args, *, name, grid, in_specs, out_specs, out_shape, scratch_shapes=(), sem=None, comm=None):
    in_specs, out_specs, out_shape, scratch_shapes = list(in_specs), list(out_specs), list(out_shape), list(scratch_shapes)
    n_in, n_out, n_scr = len(in_specs), len(out_specs), len(scratch_shapes)
    if comm is None:
        kernel_body = body
    else:
        def kernel_body(*refs):
            ins, cins, outs, couts, scr, sems = _split_comm_refs(refs, n_in, n_out, n_scr, comm)
            ids = [pl.program_id(a) for a in range(len(grid))]
            first, last = ids[0] == 0, ids[0] == grid[0] - 1
            for a in range(1, len(grid)):
                first, last = first & (ids[a] == 0), last & (ids[a] == grid[a] - 1)

            middle = ids[0] == (2 * grid[0]) // 3
            for a in range(1, len(grid)):
                middle = middle & (ids[a] == 0)

            @pl.when(first)
            def _():
                comm.start(cins, couts, sems)

            @pl.when(middle)
            def _():
                comm.forward(cins, couts, sems)

            body(*ins, *outs, *scr)

            @pl.when(last)
            def _():
                comm.wait(cins, couts, sems)

        in_specs, out_specs, out_shape = in_specs + comm.in_specs, out_specs + comm.out_specs, out_shape + comm.out_shape
        scratch_shapes, args = scratch_shapes + comm.scratch, list(args) + comm.arrs
        sem = ("arbitrary",) * len(grid)
    res = pl.pallas_call(kernel_body, name=name, grid=grid, in_specs=in_specs, out_specs=out_specs, out_shape=out_shape,
                         scratch_shapes=scratch_shapes, compiler_params=_cparams(sem))(*args)
    return res[:n_out], res[n_out:]


def _exchange(arrs, name, scatter=False, ex=None):
    if ex is None:
        ex = _Exchange(arrs, scatter=True) if scatter else _Gather2(arrs)

    def body(*refs):
        _, ins, _, outs, _, sems = _split_comm_refs(refs, 0, 0, 0, ex)
        ex.start(ins, outs, sems)
        ex.forward(ins, outs, sems)
        ex.wait(ins, outs, sems)

    return pl.pallas_call(body, name=name, in_specs=ex.in_specs, out_specs=ex.out_specs, out_shape=ex.out_shape,
                          scratch_shapes=ex.scratch)(*ex.arrs)


def _pad_lanes(v, width=LANE):
    return jnp.pad(v, ((0, 0), (0, width - v.shape[1])))


def _shards_to_cols(g):
    return jnp.transpose(g, (1, 0, 2)).reshape(g.shape[1], N_DEV * g.shape[2])


def _local_step(x, tgt, mod, w_in_pt, conv_w, conv_b, dt_bias, a_log, d_skip, ssd_norm_w, q_norm_w, k_norm_w,
                attn_norm_w, w_out_sh, w_ff1_sh, w_ff2_sh, norm1_w, norm2_w, core):
    shift1, scale1, gate1, shift2, scale2, gate2 = [mod[i:i + 1] for i in range(N_MOD)]
    dtb, alog, dsk = _pad_lanes(dt_bias), _pad_lanes(a_log), _pad_lanes(d_skip)
    qw, kw = jnp.tile(q_norm_w, (1, ATT_HEADS)), jnp.tile(k_norm_w, (1, ATT_HEADS))

    h1 = _norm_mod_fwd(x, norm1_w, scale1, shift1, "norm1_fwd")
    proj = _matmul(h1, w_in_pt, tb=True, tm=2048, tn=896, tk=1024, name="in_proj")
    pre, act = _conv_fwd(proj, conv_w, conv_b)
    ypre, ycat_ssd, hall = _ssd_fwd(proj, act, dtb, alog, dsk, ssd_norm_w)
    (o_att, lse), (w_out_g, w_ff1_g, w_ff2_g) = _att_fwd(proj, qw, kw, comm=_Gather2([w_out_sh, w_ff1_sh, w_ff2_sh]))
    w_out = w_out_g.reshape(2 * D_MODEL, D_MODEL)
    w_ff1 = _shards_to_cols(w_ff1_g)
    w_ff2 = w_ff2_g.reshape(D_FF, D_MODEL)
    ycat = _att_norm_fwd(o_att, attn_norm_w, ycat_ssd)
    row32, row16, vec32 = ("row", F32), ("row", BF16), ("vec", F32)
    mix, x1, h2 = _matmul_rows(ycat, w_out, _residual_norm_epilogue, [x], [gate1, norm2_w, scale2, shift2],
                               [row32, row32, row16], tm=1024, name="out_proj")
    u, act_ff = _matmul(h2, w_ff1, tm=1024, tn=2048, tk=1024, name="ff1", mode="relu2")
    loss, dout, dff, dgate2 = _matmul_rows(act_ff, w_ff2, _loss_epilogue, [x1, tgt], [gate2],
                                           [("one", F32), row32, row16, vec32], tm=512, name="ff2")

    du = _matmul(dff, w_ff2, tb=True, tm=512, tn=4096, tk=1024, out_dtype=BF16, name="ff2_dx", mode="drelu2", u=u)
    g_ff2 = _matmul(act_ff, dff, ta=True, tm=512, tn=1024, tk=4096, out_dtype=BF16, name="ff2_dw")
    dx1, dshift2, dscale2, g_norm2, dmix, dgate1 = _matmul_rows(
        du, w_ff1, _norm_bwd_epilogue, [x1, dout, mix], [norm2_w, scale2, gate1],
        [row32, vec32, vec32, vec32, row16, vec32], tb=True, tm=512, name="ff1_dx")
    g_ff1 = _matmul(h2, du, ta=True, tm=1024, tn=D_FF // N_DEV, tk=4096, out_dtype=BF16, name="ff1_dw", shard_out=True)

    dy_ssd, do, stats, g_attn_norm = _matmul_rows(
        dmix, w_out, _mixer_split_epilogue, [o_att, lse], [attn_norm_w],
        [("row", F32, SSD_D_INNER), ("row", F32, ATT_D), ("row", F32, ATT_D), ("vec", F32, ATT_D)], tb=True, tm=1024, name="out_proj_dx")
    g_out = _matmul(ycat, dmix, ta=True, tm=512, tn=1024, tk=4096, out_dtype=BF16, name="out_proj_dw")
    ff_slabs = [g_ff1, g_ff2.reshape(N_DEV, D_FF // N_DEV, D_MODEL)]
    (dq, dk, dv, dqw, dkw), (s_ff1, s_ff2) = _att_bwd(proj, do, stats, qw, kw, comm=_Exchange(ff_slabs, scatter=True))
    out_slabs = [g_out.astype(BF16).reshape(N_DEV, 2 * D_MODEL // N_DEV, D_MODEL)]
    (dz, dact, ddtr, da, g_dsk, g_dtb, g_ssd_norm), (s_out,) = _ssd_bwd(
        dy_ssd, ypre, proj, act, hall, dtb, alog, dsk, ssd_norm_w, comm=_Exchange(out_slabs, scatter=True))
    dxbc, g_conv_w, g_conv_b = _conv_bwd(dact, pre, proj, conv_w)
    dproj = [(dz, OFF_Z), (dxbc, OFF_XBC), (ddtr, OFF_DT), (dq, OFF_Q), (dk, OFF_K), (dv, OFF_V)]
    g_head, g_tail = _pieces_t_matmul([[dz, dxbc], [dq, dk, dv]], h1, tm=256, name="in_proj_dw")
    g_dt = _matmul(ddtr, h1, ta=True, tm=LANE, tn=1024, tk=4096, out_dtype=BF16, name="in_proj_dw_dt")[:SSD_HEADS]
    in_slabs = jnp.concatenate([g_head, g_dt, g_tail], axis=0).reshape(N_CHIP, 2, IN_W // N_DEV, D_MODEL)
    (sibling_slabs,) = _exchange(None, "swap_w_in_grads", ex=_SiblingSwap([in_slabs]))
    chip_slabs = _chip_sum(lax.dynamic_index_in_dim(in_slabs, core, axis=1, keepdims=False), sibling_slabs)
    (grad_x, dshift1, dscale1, g_norm1), (s_in,) = _matmul_rows(
        dproj, w_in_pt, _norm_bwd_epilogue, [x, dx1], [norm1_w, scale1], [row32, vec32, vec32, vec32],
        tm=256, name="in_proj_dx", comm=_ChipScatter([chip_slabs]))

    dmod = jnp.concatenate([dshift1, dscale1, dgate1, dshift2, dscale2, dgate2], axis=0)
    g_alog = da[:, :SSD_HEADS] * (-jnp.exp(a_log))
    g_qw = dqw.reshape(ATT_HEADS, HEAD_DIM).sum(axis=0, keepdims=True)
    g_kw = dkw.reshape(ATT_HEADS, HEAD_DIM).sum(axis=0, keepdims=True)
    return dict(loss=loss, grad_x=grad_x, dmod=dmod, norm1_w=g_norm1, norm2_w=g_norm2, w_in=s_in, conv_w=g_conv_w,
                conv_b=g_conv_b, dt_bias=g_dtb[:, :SSD_HEADS], a_log=g_alog, d_skip=g_dsk[:, :SSD_HEADS],
                ssd_norm_w=g_ssd_norm, q_norm_w=g_qw, k_norm_w=g_kw, attn_norm_w=g_attn_norm, w_out=s_out,
                w_ff1=s_ff1, w_ff2=s_ff2)


def _pack_w_in_rows(wt_full):
    cut = OFF_DT + SSD_HEADS
    pad = jnp.zeros((LANE - SSD_HEADS, wt_full.shape[1]), wt_full.dtype)
    return jnp.concatenate([wt_full[:cut], pad, wt_full[cut:]], axis=0)


MISC_FIELDS = (("dt_bias", SSD_HEADS), ("a_log", SSD_HEADS), ("d_skip", SSD_HEADS), ("q_norm_w", HEAD_DIM), ("k_norm_w", HEAD_DIM),
               ("loss", 1))
SMALL_LAYOUT = (("b_ada", 6), ("norm1_w", 1), ("norm2_w", 1), ("conv_w", 8), ("conv_b", 2), ("ssd_norm_w", 1),
                ("attn_norm_w", 1), ("misc", 1))


def _pack_small(vals):
    rows = []
    for name, nrow in SMALL_LAYOUT:
        if name == "misc":
            misc = jnp.concatenate([vals[f].reshape(1, n) if f in vals else jnp.zeros((1, n), F32) for f, n in MISC_FIELDS], axis=1)
            rows.append(_pad_lanes(misc, D_MODEL))
        elif name in vals:
            rows.append(vals[name].reshape(nrow, D_MODEL))
        else:
            rows.append(jnp.zeros((nrow, D_MODEL), F32))
    used = sum(n for _, n in SMALL_LAYOUT)
    rows.append(jnp.zeros((SMALL_ROWS - used, D_MODEL), F32))
    return jnp.concatenate(rows, axis=0)


def _unpack_small(packed):
    out, r = {}, 0
    for name, nrow in SMALL_LAYOUT:
        blk = packed[r:r + nrow]
        r += nrow
        if name == "misc":
            c0 = 0
            for f, n in MISC_FIELDS:
                out[f] = blk[:, c0:c0 + n]
                c0 += n
        elif name == "b_ada":
            out[name] = blk.reshape(1, N_MOD * D_MODEL)
        elif name == "conv_w":
            out[name] = blk.reshape(CONV_K, CONV_CH)
        elif name == "conv_b":
            out[name] = blk.reshape(1, CONV_CH)
        else:
            out[name] = blk
    return out


WEIGHT_NAMES = ("norm1_w", "norm2_w", "w_ada", "b_ada", "w_in", "conv_w", "conv_b", "dt_bias", "a_log", "d_skip",
                "ssd_norm_w", "q_norm_w", "k_norm_w", "attn_norm_w", "w_out", "w_ff1", "w_ff2")
SMALL_NAMES = ("norm1_w", "norm2_w", "b_ada", "conv_b", "dt_bias", "a_log", "d_skip", "ssd_norm_w", "q_norm_w",
               "k_norm_w", "attn_norm_w")


def kernel(x, c, norm1_w, norm2_w, w_ada, b_ada, w_in, conv_w, conv_b, dt_bias, a_log, d_skip, ssd_norm_w, q_norm_w, k_norm_w, attn_norm_w, w_out, w_ff1, w_ff2, loss_target, m_norm1_w, m_norm2_w, m_w_ada, m_b_ada, m_w_in, m_conv_w, m_conv_b, m_dt_bias, m_a_log, m_d_skip, m_ssd_norm_w, m_q_norm_w, m_k_norm_w, m_attn_norm_w, m_w_out, m_w_ff1, m_w_ff2, v_norm1_w, v_norm2_w, v_w_ada, v_b_ada, v_w_in, v_conv_w, v_conv_b, v_dt_bias, v_a_log, v_d_skip, v_ssd_norm_w, v_q_norm_w, v_k_norm_w, v_attn_norm_w, v_w_out, v_w_ff1, v_w_ff2):
    args = dict(locals())
    w = {n: args[n] for n in WEIGHT_NAMES}
    m = {n: args["m_" + n] for n in WEIGHT_NAMES}
    v = {n: args["v_" + n] for n in WEIGHT_NAMES}
    me = 4 * lax.axis_index("x") + 2 * lax.axis_index("y") + lax.axis_index("c")

    c_rows = jnp.pad(c, ((0, 7), (0, 0)))
    w_in_t, m_in_t, v_in_t = [jnp.transpose(t["w_in"][0]) for t in (w, m, v)]
    c_g, conv_g, w_in_g = _exchange([c_rows, w["conv_w"][0], w_in_t.astype(BF16)], "gather_w_in", scatter=False)
    c_all = c_g[:, 0, :]
    conv_full = _shards_to_cols(conv_g)
    w_in_pt = _pack_w_in_rows(w_in_g.reshape(IN_W, D_MODEL))

    mod_part = _ada_fwd(c_all, w["w_ada"][0])
    (mod_g,) = _exchange([mod_part], "gather_mod", scatter=False)
    mod_mine = lax.dynamic_index_in_dim(mod_g, me, axis=1, keepdims=False).reshape(1, N_MOD * D_MODEL) + w["b_ada"]
    mod = mod_mine.reshape(N_MOD, D_MODEL)

    res = _local_step(x[0], loss_target[0], mod, w_in_pt, conv_full, w["conv_b"], w["dt_bias"], w["a_log"], w["d_skip"],
                      w["ssd_norm_w"], w["q_norm_w"], w["k_norm_w"], w["attn_norm_w"], w["w_out"][0].astype(BF16),
                      w["w_ff1"][0].astype(BF16), w["w_ff2"][0].astype(BF16), w["norm1_w"], w["norm2_w"], lax.axis_index("c"))

    small_vals = {n: res[n] for n in SMALL_NAMES if n != "b_ada"}
    small_vals["b_ada"] = res["dmod"]
    small_vals["conv_w"] = res["conv_w"]
    small_vals["loss"] = res["loss"]
    (small_g,) = _exchange([_pack_small(small_vals)], "gather_small", scatter=False)

    grads, delta, new_m, new_v = {}, {}, {}, {}
    for name in ("w_out", "w_ff1", "w_ff2"):
        outs = _reduce_adamw(res[name], w[name][0], m[name][0], v[name][0], "adamw_" + name)
        grads[name], delta[name], new_m[name], new_v[name] = [o[None] for o in outs]
    outs = _reduce_adamw(res["w_in"], w_in_t, m_in_t, v_in_t, "adamw_w_in")
    grads["w_in"], delta["w_in"], new_m["w_in"], new_v["w_in"] = [jnp.transpose(o)[None] for o in outs]

    sm = _small_reduce_adamw(small_g, _pack_small({n: w[n] for n in SMALL_NAMES}), _pack_small({n: m[n] for n in SMALL_NAMES}),
                             _pack_small({n: v[n] for n in SMALL_NAMES}))
    sm = [_unpack_small(p) for p in sm]
    for n in SMALL_NAMES:
        grads[n], delta[n], new_m[n], new_v[n] = [p[n] for p in sm]
    shard_w = CONV_CH // N_DEV
    g_conv = lax.dynamic_slice_in_dim(sm[0]["conv_w"], me * shard_w, shard_w, axis=1)
    cw = _adamw_small(g_conv, w["conv_w"][0], m["conv_w"][0], v["conv_w"][0], "adamw_conv_w")
    grads["conv_w"] = g_conv[None]
    delta["conv_w"], new_m["conv_w"], new_v["conv_w"] = [o[None] for o in cw]

    ada_w = w_ada.shape[2]
    dmod_all = small_g[:, :N_MOD, :].reshape(N_DEV, N_MOD * D_MODEL)
    dmod_cols = lax.dynamic_slice_in_dim(dmod_all, me * ada_w, ada_w, axis=1)
    outs = _ada_bwd_adamw(c_all, dmod_cols, w["w_ada"][0], m["w_ada"][0], v["w_ada"][0])
    grads["w_ada"], delta["w_ada"], new_m["w_ada"], new_v["w_ada"] = [o[None] for o in outs]

    loss = sm[0]["loss"][0, 0]
    return (loss, res["grad_x"][None], *[grads[n] for n in WEIGHT_NAMES], *[delta[n] for n in WEIGHT_NAMES],
            *[new_m[n] for n in WEIGHT_NAMES], *[new_v[n] for n in WEIGHT_NAMES])
```

```python
import jax
import jax.numpy as jnp
from jax import lax
from jax.experimental import pallas as pl
from jax.experimental.pallas import tpu as pltpu

F32 = jnp.float32
BF16 = jnp.bfloat16
HIGHEST = lax.Precision.HIGHEST
MESH_IDS = pl.DeviceIdType.MESH

N_DEV = 8
D_MODEL = 1024
HEAD_DIM = 64
SSD_HEADS = 16
SSD_GROUPS = 4
HEADS_PER_GROUP = SSD_HEADS // SSD_GROUPS
SSD_STATE = 128
SSD_CHUNK = 128
SSD_D_INNER = SSD_HEADS * HEAD_DIM
GROUP_WIDTH = SSD_D_INNER // SSD_GROUPS
CONV_K = 4
CONV_CH = SSD_D_INNER + 2 * SSD_GROUPS * SSD_STATE
ATT_HEADS = 16
ATT_D = ATT_HEADS * HEAD_DIM
ATT_BLK = 128
DILATIONS = (1, 4, 16)
D_FF = 4 * D_MODEL
N_MOD = 6
EPS = 1e-6
IN_W = SSD_D_INNER + CONV_CH + SSD_HEADS + 3 * ATT_D
LANE = 128
OFF_Z, OFF_XBC, OFF_DT = 0, SSD_D_INNER, SSD_D_INNER + CONV_CH
OFF_Q = OFF_DT + LANE
OFF_K, OFF_V = OFF_Q + ATT_D, OFF_Q + 2 * ATT_D
IN_WP = OFF_V + ATT_D

ADAM_LR, ADAM_B1, ADAM_B2, ADAM_EPS, ADAM_WD, ADAM_STEP = 0.001, 0.9, 0.999, 1e-08, 0.01, 10
VMEM_LIMIT = 60 * 1024 * 1024
ROW_TILE = 512
SMALL_ROWS = 24


def _cparams(sem=None):
    return pltpu.CompilerParams(dimension_semantics=sem, vmem_limit_bytes=VMEM_LIMIT)


def _sigmoid(v):
    return 1.0 / (1.0 + jnp.exp(-v))


def _softplus(v):
    y = jnp.exp(-jnp.abs(v))
    small = y * (1.0 - y * (0.5 - y * (1.0 / 3.0)))
    return jnp.maximum(v, 0.0) + jnp.where(y < 0.01, small, jnp.log(1.0 + y))


def _dot(a, b, dims, precision=None):
    return lax.dot_general(a, b, (dims, ((), ())), preferred_element_type=F32, precision=precision)


NN = ((1,), (0,))
NT = ((1,), (1,))
TN = ((0,), (0,))


def _matmul(a, b, *, ta=False, tb=False, tm, tn, tk, out_dtype=F32, name, mode=None, u=None, comm=None, shard_out=False):
    m, k = (a.shape[1], a.shape[0]) if ta else a.shape
    n = b.shape[0] if tb else b.shape[1]
    assert m % tm == 0 and n % tn == 0 and k % tk == 0, (name, m, n, k)
    nk = k // tk
    a_spec = pl.BlockSpec((tk, tm), lambda i, j, kk: (kk, i)) if ta else pl.BlockSpec((tm, tk), lambda i, j, kk: (i, kk))
    b_spec = pl.BlockSpec((tn, tk), lambda i, j, kk: (j, kk)) if tb else pl.BlockSpec((tk, tn), lambda i, j, kk: (kk, j))
    o_spec = pl.BlockSpec((tm, tn), lambda i, j, kk: (i, j))
    dims = ((0,) if ta else (1,), (1,) if tb else (0,))
    n_out = 2 if mode == "relu2" else 1

    def body(*refs):
        if mode == "drelu2":
            a_ref, b_ref, u_ref = refs[:3]
            rest = refs[3:]
        else:
            a_ref, b_ref = refs[:2]
            u_ref = None
            rest = refs[2:]
        outs = rest[:n_out]
        part = _dot(a_ref[...], b_ref[...], dims)

        def finish(r):
            if mode == "relu2":
                outs[0][...] = r.astype(BF16)
                rr = jnp.maximum(r, 0.0)
                outs[1][...] = (rr * rr).astype(BF16)
            elif mode == "drelu2":
                outs[0][...] = (r * (2.0 * jnp.maximum(u_ref[...].astype(F32), 0.0))).astype(out_dtype)
            else:
                outs[0][...] = r.astype(out_dtype)

        if nk == 1:
            finish(part)
        else:
            acc = rest[n_out]
            kk = pl.program_id(2)

            @pl.when(kk == 0)
            def _():
                acc[...] = part

            @pl.when(kk > 0)
            def _():
                acc[...] += part

            @pl.when(kk == nk - 1)
            def _():
                finish(acc[...])

    in_specs = [a_spec, b_spec]
    args = [a, b]
    if mode == "drelu2":
        in_specs.append(o_spec)
        args.append(u)
    if mode == "relu2":
        out_shape = [jax.ShapeDtypeStruct((m, n), BF16), jax.ShapeDtypeStruct((m, n), BF16)]
    elif shard_out:
        out_shape = [jax.ShapeDtypeStruct((n // tn, m, tn), out_dtype)]
        o_spec = pl.BlockSpec((None, tm, tn), lambda i, j, kk: (j, i, 0))
    else:
        out_shape = [jax.ShapeDtypeStruct((m, n), out_dtype)]
    outs, comm_outs = _pcall(
        body, args, name=name, grid=(m // tm, n // tn, nk), in_specs=in_specs, out_specs=[o_spec] * n_out,
        out_shape=out_shape, scratch_shapes=[pltpu.VMEM((tm, tn), F32)] if nk > 1 else [],
        sem=("parallel", "parallel", "arbitrary"), comm=comm)
    res = tuple(outs) if mode == "relu2" else outs[0]
    return res if comm is None else (res, comm_outs)


def _pieces_t_matmul(groups, b, *, tm, name):
    k, n = b.shape
    pieces = [p for g in groups for p in g]
    starts, tiles = [], 0
    for p in pieces:
        assert p.shape[0] == k and p.shape[1] % tm == 0, (name, p.shape)
        starts.append(tiles)
        tiles += p.shape[1] // tm
    group_of, group_start, group_tiles = [], [], []
    for gi, g in enumerate(groups):
        group_start.append(starts[len(group_of)])
        group_of += [gi] * len(g)
        group_tiles.append(sum(p.shape[1] // tm for p in g))

    def clipped(block, start, count):
        return pl.BlockSpec(block, (lambda i: (0, jnp.clip(i - start, 0, count - 1))) if block[0] == k
                            else (lambda i: (jnp.clip(i - start, 0, count - 1), 0)))

    def body(*refs):
        a_refs, b_ref, o_refs = refs[:len(pieces)], refs[len(pieces)], refs[len(pieces) + 1:]
        i = pl.program_id(0)
        for a_ref, start, p, gi in zip(a_refs, starts, pieces, group_of):
            @pl.when((i >= start) & (i < start + p.shape[1] // tm))
            def _(a_ref=a_ref, o_ref=o_refs[gi]):
                o_ref[...] = _dot(a_ref[...], b_ref[...], TN).astype(BF16)

    return pl.pallas_call(
        body, name=name, grid=(tiles,),
        in_specs=[clipped((k, tm), s0, p.shape[1] // tm) for s0, p in zip(starts, pieces)] + [pl.BlockSpec((k, n), lambda i: (0, 0))],
        out_specs=[clipped((tm, n), s0, cnt) for s0, cnt in zip(group_start, group_tiles)],
        out_shape=[jax.ShapeDtypeStruct((cnt * tm, n), BF16) for cnt in group_tiles],
        compiler_params=_cparams(("arbitrary",)))(*pieces, b)


def _rms_mod(xv, nw, scale, shift):
    r = lax.rsqrt(jnp.mean(xv * xv, axis=-1, keepdims=True) + EPS)
    return ((xv * r) * nw * (1.0 + scale) + shift).astype(BF16)


def _norm_mod_fwd(x, nw, scale, shift, name):
    s, d = x.shape
    row = pl.BlockSpec((ROW_TILE, d), lambda i: (i, 0))
    vec = pl.BlockSpec((1, d), lambda i: (0, 0))

    def body(x_ref, nw_ref, sc_ref, sh_ref, h_ref):
        h_ref[...] = _rms_mod(x_ref[...], nw_ref[...], sc_ref[...], sh_ref[...])

    return pl.pallas_call(body, name=name, grid=(s // ROW_TILE,), in_specs=[row, vec, vec, vec], out_specs=row,
                          out_shape=jax.ShapeDtypeStruct((s, d), BF16), compiler_params=_cparams(("parallel",)))(x, nw, scale, shift)


def _matmul_rows(a, b, epilogue, row_in, vec_in, outs, *, tb=False, tm, name, comm=None):
    pieces = a if isinstance(a, list) else [(a, 0)]
    assert not (tb and len(pieces) > 1)
    m = pieces[0][0].shape[0]
    n = b.shape[0] if tb else b.shape[1]
    assert m % tm == 0, (name, m, tm)
    dims = ((1,), (1,) if tb else (0,))
    n_a, n_row, n_vec = len(pieces), len(row_in), len(vec_in)

    def body(*refs):
        a_refs, b_ref, rest = refs[:n_a], refs[n_a], refs[n_a + 1:]
        if n_a == 1:
            c = _dot(a_refs[0][...], b_ref[...], dims)
        else:
            c = None
            for a_ref, (piece, off) in zip(a_refs, pieces):
                part = _dot(a_ref[...], b_ref[off:off + piece.shape[1], :], dims)
                c = part if c is None else c + part
        epilogue(c, pl.program_id(0) == 0, rest[:n_row], rest[n_row:n_row + n_vec], rest[n_row + n_vec:])

    def spec(kind, width):
        block = {"row": (tm, width), "vec": (1, width), "one": (1, 1)}[kind]
        return pl.BlockSpec(block, (lambda i: (i, 0)) if kind == "row" else (lambda i: (0, 0)))

    def shape(kind, width):
        return {"row": (m, width), "vec": (1, width), "one": (1, 1)}[kind]

    outs = [(o[0], o[1], o[2] if len(o) > 2 else n) for o in outs]
    res, comm_outs = _pcall(
        body, [*[p for p, _ in pieces], b, *row_in, *vec_in], name=name, grid=(m // tm,),
        in_specs=[spec("row", p.shape[1]) for p, _ in pieces] + [pl.BlockSpec(b.shape, lambda i: (0, 0))]
        + [spec("row", r.shape[1]) for r in row_in] + [spec("vec", v.shape[1]) for v in vec_in],
        out_specs=[spec(kind, width) for kind, _, width in outs],
        out_shape=[jax.ShapeDtypeStruct(shape(kind, width), dt) for kind, dt, width in outs],
        sem=("arbitrary",), comm=comm)
    return res if comm is None else (res, comm_outs)


def _residual_norm_epilogue(mix, first, rows, vecs, outs):
    (x_ref,), (gate_ref, nw_ref, sc_ref, sh_ref), (mix_ref, x1_ref, h_ref) = rows, vecs, outs
    xv = x_ref[...] + gate_ref[...] * mix
    mix_ref[...] = mix
    x1_ref[...] = xv
    h_ref[...] = _rms_mod(xv, nw_ref[...], sc_ref[...], sh_ref[...])


def _loss_epilogue(ff, first, rows, vecs, outs):
    (x1_ref, t_ref), (g_ref,), (loss_ref, dout_ref, dff_ref, dg_ref) = rows, vecs, outs
    d = ff.shape[1]

    @pl.when(first)
    def _():
        loss_ref[...] = jnp.zeros_like(loss_ref)
        dg_ref[...] = jnp.zeros_like(dg_ref)

    err = x1_ref[...] + g_ref[...] * ff - t_ref[...]
    loss_ref[...] += (0.5 / d) * jnp.sum(err * err).reshape(1, 1)
    dout = err * (1.0 / d)
    dout_ref[...] = dout
    dff_ref[...] = (g_ref[...] * dout).astype(BF16)
    dg_ref[...] += jnp.sum(dout * ff, axis=0, keepdims=True)


def _norm_bwd_epilogue(dh, first, rows, vecs, outs):
    with_gate = len(vecs) == 3
    x_ref, dres_ref = rows[:2]
    nw_ref, sc_ref = vecs[:2]
    dx_ref, dsh_ref, dsc_ref, dnw_ref = outs[:4]

    @pl.when(first)
    def _():
        for ref in outs[1:4] + outs[5:]:
            ref[...] = jnp.zeros_like(ref)

    xv = x_ref[...]
    r = lax.rsqrt(jnp.mean(xv * xv, axis=-1, keepdims=True) + EPS)
    nrm = xv * r
    one_sc = 1.0 + sc_ref[...]
    dhn = dh * nrm
    dsh_ref[...] += jnp.sum(dh, axis=0, keepdims=True)
    dsc_ref[...] += jnp.sum(dhn, axis=0, keepdims=True) * nw_ref[...]
    dnw_ref[...] += jnp.sum(dhn, axis=0, keepdims=True) * one_sc
    dn = dh * (nw_ref[...] * one_sc)
    dx = dres_ref[...] + r * (dn - nrm * jnp.mean(dn * nrm, axis=-1, keepdims=True))
    dx_ref[...] = dx
    if with_gate:
        outs[4][...] = (vecs[2][...] * dx).astype(BF16)
        outs[5][...] += jnp.sum(dx * rows[2][...], axis=0, keepdims=True)


CONV_COLS = 256
CONV_FWD_ROWS = 2048
CONV_BWD_ROWS = 1024
CONV_SUB_ROWS = 128
HALO = 8


def _shift_down(cur, halo, k):
    if k == 0:
        return cur
    rolled = pltpu.roll(cur, k, axis=0)
    top = jnp.where(lax.broadcasted_iota(jnp.int32, halo.shape, 0) < k, pltpu.roll(halo, k, axis=0), rolled[:HALO])
    return jnp.concatenate([top, rolled[HALO:]], axis=0)


def _shift_up(cur, halo, k):
    if k == 0:
        return cur
    t = cur.shape[0]
    rolled = pltpu.roll(cur, t - k, axis=0)
    bot = jnp.where(lax.broadcasted_iota(jnp.int32, halo.shape, 0) >= HALO - k, pltpu.roll(halo, HALO - k, axis=0),
                    rolled[t - HALO:])
    return jnp.concatenate([rolled[:t - HALO], bot], axis=0)


def _conv_fwd(proj, conv_w, conv_b):
    s = proj.shape[0]
    nr = s // CONV_FWD_ROWS
    cb0 = OFF_XBC // CONV_COLS
    hb = CONV_FWD_ROWS // HALO
    cur = pl.BlockSpec((CONV_FWD_ROWS, CONV_COLS), lambda j, r: (r, cb0 + j))
    prev = pl.BlockSpec((HALO, CONV_COLS), lambda j, r: (jnp.maximum(r * hb - 1, 0), cb0 + j))
    out = pl.BlockSpec((CONV_FWD_ROWS, CONV_COLS), lambda j, r: (r, j))

    def body(u_ref, up_ref, w_ref, b_ref, pre_ref, act_ref):
        r = pl.program_id(1)
        for c in range(CONV_FWD_ROWS // CONV_SUB_ROWS):
            rows = slice(c * CONV_SUB_ROWS, (c + 1) * CONV_SUB_ROWS)
            u = u_ref[rows, :]
            halo = u_ref[c * CONV_SUB_ROWS - HALO:c * CONV_SUB_ROWS, :] if c > 0 else jnp.where(r > 0, up_ref[...], 0.0)
            acc = b_ref[...] + w_ref[CONV_K - 1:CONV_K, :] * u
            for k in range(1, CONV_K):
                acc = acc + w_ref[CONV_K - 1 - k:CONV_K - k, :] * _shift_down(u, halo, k)
            pre_ref[rows, :] = acc
            act_ref[rows, :] = acc * _sigmoid(acc)

    return pl.pallas_call(
        body, name="conv_fwd", grid=(CONV_CH // CONV_COLS, nr),
        in_specs=[cur, prev, pl.BlockSpec((CONV_K, CONV_COLS), lambda j, r: (0, j)),
                  pl.BlockSpec((1, CONV_COLS), lambda j, r: (0, j))],
        out_specs=[out, out],
        out_shape=[jax.ShapeDtypeStruct((s, CONV_CH), F32), jax.ShapeDtypeStruct((s, CONV_CH), F32)],
        compiler_params=_cparams(("parallel", "arbitrary")))(proj, proj, conv_w, conv_b)


def _conv_bwd(dact, pre, proj, conv_w):
    s = proj.shape[0]
    nr = s // CONV_BWD_ROWS
    cb0 = OFF_XBC // CONV_COLS
    hb = CONV_BWD_ROWS // HALO
    last_halo = s // HALO - 1
    n_sub = CONV_BWD_ROWS // CONV_SUB_ROWS
    cur = pl.BlockSpec((CONV_BWD_ROWS, CONV_COLS), lambda j, r: (r, j))
    nxt = pl.BlockSpec((HALO, CONV_COLS), lambda j, r: (jnp.minimum((r + 1) * hb, last_halo), j))
    ucur = pl.BlockSpec((CONV_BWD_ROWS, CONV_COLS), lambda j, r: (r, cb0 + j))
    wspec = pl.BlockSpec((CONV_K, CONV_COLS), lambda j, r: (0, j))
    bspec = pl.BlockSpec((1, CONV_COLS), lambda j, r: (0, j))

    def dsilu(p):
        sg = _sigmoid(p)
        return sg * (1.0 + p * (1.0 - sg))

    def body(da_ref, dan_ref, pre_ref, pren_ref, u_ref, w_ref, du_ref, dw_ref, db_ref):
        r = pl.program_id(1)

        @pl.when(r == 0)
        def _():
            dw_ref[...] = jnp.zeros_like(dw_ref)
            db_ref[...] = jnp.zeros_like(db_ref)

        dws = [jnp.zeros((1, CONV_COLS), F32) for _ in range(CONV_K)]
        db = jnp.zeros((1, CONV_COLS), F32)
        for c in range(n_sub):
            rows = slice(c * CONV_SUB_ROWS, (c + 1) * CONV_SUB_ROWS)
            ahead = slice((c + 1) * CONV_SUB_ROWS, (c + 1) * CONV_SUB_ROWS + HALO)
            dpre = da_ref[rows, :] * dsilu(pre_ref[rows, :])
            if c < n_sub - 1:
                dnext = da_ref[ahead, :] * dsilu(pre_ref[ahead, :])
            else:
                dnext = jnp.where(r < nr - 1, dan_ref[...] * dsilu(pren_ref[...]), 0.0)
            u = u_ref[rows, :]
            du = w_ref[CONV_K - 1:CONV_K, :] * dpre
            dws[0] = dws[0] + jnp.sum(dpre * u, axis=0, keepdims=True)
            for k in range(1, CONV_K):
                ahead_k = _shift_up(dpre, dnext, k)
                du = du + w_ref[CONV_K - 1 - k:CONV_K - k, :] * ahead_k
                dws[k] = dws[k] + jnp.sum(ahead_k * u, axis=0, keepdims=True)
            du_ref[rows, :] = du.astype(BF16)
            db = db + jnp.sum(dpre, axis=0, keepdims=True)
        dw_ref[...] += jnp.concatenate(dws[::-1], axis=0)
        db_ref[...] += db

    return pl.pallas_call(
        body, name="conv_bwd", grid=(CONV_CH // CONV_COLS, nr),
        in_specs=[cur, nxt, cur, nxt, ucur, wspec],
        out_specs=[cur, wspec, bspec],
        out_shape=[jax.ShapeDtypeStruct((s, CONV_CH), BF16), jax.ShapeDtypeStruct((CONV_K, CONV_CH), F32),
                   jax.ShapeDtypeStruct((1, CONV_CH), F32)],
        compiler_params=_cparams(("parallel", "arbitrary")))(dact, dact, pre, pre, proj, conv_w)


def _ssd_common(dtr, dtb, alog):
    lane = lax.broadcasted_iota(jnp.int32, (1, LANE), 1)
    head_lane = lane < SSD_HEADS
    dt = jnp.where(head_lane, _softplus(dtr + dtb), 0.0)
    a = jnp.where(head_lane, -jnp.exp(alog), 0.0)
    row = lax.broadcasted_iota(jnp.int32, (SSD_CHUNK, SSD_CHUNK), 0)
    col = lax.broadcasted_iota(jnp.int32, (SSD_CHUNK, SSD_CHUNK), 1)
    tril = row >= col
    cs = _dot(tril.astype(F32), dt * a, NN, precision=HIGHEST)
    return dt, a, cs, cs.T, tril, lane


def _split_bf16(v, passes):
    terms, rest = [], v
    for _ in range(passes):
        t = rest.astype(BF16)
        terms.append(t)
        rest = rest - t.astype(F32)
    return terms


def _dot_split(v, m, dims, passes):
    terms = _split_bf16(v, passes)
    if passes == 1:
        return _dot(terms[0], m, dims)
    return _dot(jnp.concatenate(terms, axis=1), jnp.concatenate([m] * passes, axis=0 if dims == NN else 1), dims)


def _ssd_constants():
    heads = jnp.arange(LANE)[:, None]
    exp_mat = (heads == (jnp.arange(SSD_D_INNER)[None, :] // HEAD_DIM)).astype(BF16)
    ind4 = ((jnp.arange(SSD_HEADS * SSD_CHUNK)[:, None] // SSD_CHUNK) == jnp.arange(LANE)[None, :]).astype(BF16)
    return exp_mat, ind4


def _expand_heads(v):
    return jnp.repeat(v[:, :SSD_HEADS], HEAD_DIM, axis=1)


def _ssd_prep(dtr, dtb, alog, exp_mat):
    dt, a, cs, cst, tril, lane = _ssd_common(dtr, dtb, alog)
    return dt, a, cs, cst, tril, lane, _dot_split(dt, exp_mat, NN, 2), _dot_split(cs, exp_mat, NN, 3)


def _chunk_decay_rows(cs, g):
    parts = []
    for e in range(HEADS_PER_GROUP):
        h = g * HEADS_PER_GROUP + e
        parts.append(jnp.broadcast_to(jnp.exp(cs[SSD_CHUNK - 1:SSD_CHUNK, h:h + 1]), (HEAD_DIM, SSD_STATE)))
    return jnp.concatenate(parts, axis=0)


def _ssd_fwd(proj, act, dtb, alog, dsk, nw):
    s = proj.shape[0]
    nc = s // SSD_CHUNK
    bc_w = SSD_GROUPS * SSD_STATE
    exp_mat, _ = _ssd_constants()

    def body(z_ref, dtr_ref, xs_ref, b_ref, c_ref, dtb_ref, alog_ref, dskx_ref, nw_ref, exp_ref,
             ypre_ref, yssd_ref, hall_ref, h_scr):
        @pl.when(pl.program_id(0) == 0)
        def _():
            h_scr[...] = jnp.zeros_like(h_scr)

        dt, a, cs, cst, tril, lane, dtx, csx = _ssd_prep(dtr_ref[...], dtb_ref[...], alog_ref[...], exp_ref[...])
        cs_last_x = csx[SSD_CHUNK - 1:SSD_CHUNK, :]
        xs = xs_ref[...]
        xdt = xs * dtx
        xdtb = xdt.astype(BF16)
        xdec = (xdt * jnp.exp(cs_last_x - csx)).astype(BF16)
        ecsx = jnp.exp(csx)
        head_of_lane = lax.broadcasted_iota(jnp.int32, (1, GROUP_WIDTH), 1) // HEAD_DIM
        for g in range(SSD_GROUPS):
            gs = slice(g * GROUP_WIDTH, (g + 1) * GROUP_WIDTH)
            bg = b_ref[:, g * SSD_STATE:(g + 1) * SSD_STATE].astype(BF16)
            cg = c_ref[:, g * SSD_STATE:(g + 1) * SSD_STATE].astype(BF16)
            cb = _dot(cg, bg, NT)
            hprev = h_scr[gs, :]
            hall_ref[0, gs, :] = hprev
            gms, rhs = [], []
            xg = xdtb[:, gs]
            for e in range(HEADS_PER_GROUP):
                h = g * HEADS_PER_GROUP + e
                lm = jnp.exp(jnp.where(tril, cs[:, h:h + 1] - cst[h:h + 1, :], -1e30))
                gms.append((cb * lm).astype(BF16))
                rhs.append(jnp.where(head_of_lane == e, xg, jnp.zeros_like(xg)))
            y = _dot(jnp.concatenate(gms, axis=1), jnp.concatenate(rhs, axis=0), NN)
            y = y + ecsx[:, gs] * _dot(cg, hprev.astype(BF16), NT)
            y = y + dskx_ref[:, gs] * xs[:, gs]
            h_scr[gs, :] = hprev * _chunk_decay_rows(cs, g) + _dot(xdec[:, gs], bg, TN)
            ypre_ref[:, gs] = y
            z = z_ref[:, gs]
            yg = y * (z * _sigmoid(z))
            r = lax.rsqrt(jnp.mean(yg * yg, axis=-1, keepdims=True) + EPS)
            yssd_ref[:, gs] = (yg * r * nw_ref[:, gs]).astype(BF16)

    row_d = lambda cb: pl.BlockSpec((SSD_CHUNK, SSD_D_INNER), lambda c: (c, cb))
    small = pl.BlockSpec((1, LANE), lambda c: (0, 0))
    wide = pl.BlockSpec((1, SSD_D_INNER), lambda c: (0, 0))
    return pl.pallas_call(
        body, name="ssd_fwd", grid=(nc,),
        in_specs=[row_d(OFF_Z // SSD_D_INNER),
                  pl.BlockSpec((SSD_CHUNK, LANE), lambda c: (c, OFF_DT // LANE)),
                  row_d(0),
                  pl.BlockSpec((SSD_CHUNK, bc_w), lambda c: (c, SSD_D_INNER // bc_w)),
                  pl.BlockSpec((SSD_CHUNK, bc_w), lambda c: (c, SSD_D_INNER // bc_w + 1)),
                  small, small, wide, wide, pl.BlockSpec((LANE, SSD_D_INNER), lambda c: (0, 0))],
        out_specs=[row_d(0), row_d(0), pl.BlockSpec((1, SSD_D_INNER, SSD_STATE), lambda c: (c, 0, 0))],
        out_shape=[jax.ShapeDtypeStruct((s, SSD_D_INNER), F32), jax.ShapeDtypeStruct((s, SSD_D_INNER + ATT_D), BF16),
                   jax.ShapeDtypeStruct((nc, SSD_D_INNER, SSD_STATE), F32)],
        scratch_shapes=[pltpu.VMEM((SSD_D_INNER, SSD_STATE), F32)],
        compiler_params=_cparams(("arbitrary",)))(proj, proj, act, act, act, dtb, alog, _expand_heads(dsk), nw, exp_mat)


def _ssd_bwd(dycat, ypre, proj, act, hall, dtb, alog, dsk, nw, comm=None):
    s = proj.shape[0]
    nc = s // SSD_CHUNK
    bc_w = SSD_GROUPS * SSD_STATE

    exp_mat, ind4 = _ssd_constants()
    seg_passes = 1

    def body(dy_ref, ypre_ref, z_ref, dtr_ref, xs_ref, b_ref, c_ref, hall_ref, dtb_ref, alog_ref, dskx_ref, nw_ref,
             exp_ref, ind4_ref, dz_ref, dact_ref, ddtr_ref, da_ref, ddsk_ref, ddtb_ref, dnw_ref, dh_scr):
        @pl.when(pl.program_id(0) == 0)
        def _():
            dh_scr[...] = jnp.zeros_like(dh_scr)
            da_ref[...] = jnp.zeros_like(da_ref)
            ddsk_ref[...] = jnp.zeros_like(ddsk_ref)
            ddtb_ref[...] = jnp.zeros_like(ddtb_ref)
            dnw_ref[...] = jnp.zeros_like(dnw_ref)

        dtr = dtr_ref[...]
        dt, a, cs, cst, tril, lane, dtx, csx = _ssd_prep(dtr, dtb_ref[...], alog_ref[...], exp_ref[...])
        cs_last_x = csx[SSD_CHUNK - 1:SSD_CHUNK, :]
        xs = xs_ref[...]
        xdt = xs * dtx
        xdtb = xdt.astype(BF16)
        decx = jnp.exp(cs_last_x - csx)
        xdecf = xdt * decx
        xdec = xdecf.astype(BF16)
        ecsx = jnp.exp(csx)
        head_of_lane = lax.broadcasted_iota(jnp.int32, (1, GROUP_WIDTH), 1) // HEAD_DIM
        last_row = lax.broadcasted_iota(jnp.int32, (SSD_CHUNK, 1), 0) == SSD_CHUNK - 1
        dcs_col = jnp.zeros((SSD_CHUNK, LANE), F32)
        dcs_row = jnp.zeros((SSD_CHUNK, LANE), F32)
        ddt = jnp.zeros((SSD_CHUNK, LANE), F32)
        ddsk = jnp.zeros((1, LANE), F32)
        hsum = jnp.zeros((1, LANE), F32)
        t1_sum = jnp.zeros((1, LANE), F32)
        for g in range(SSD_GROUPS):
            gs = slice(g * GROUP_WIDTH, (g + 1) * GROUP_WIDTH)
            bsl = slice(g * SSD_STATE, (g + 1) * SSD_STATE)
            exp_g = exp_ref[:, gs]
            ind4_g = ind4_ref[g * HEADS_PER_GROUP * SSD_CHUNK:(g + 1) * HEADS_PER_GROUP * SSD_CHUNK, :]
            z = z_ref[:, gs]
            sg = _sigmoid(z)
            sz = z * sg
            ypre = ypre_ref[:, gs]
            yg = ypre * sz
            r = lax.rsqrt(jnp.mean(yg * yg, axis=-1, keepdims=True) + EPS)
            nrm = yg * r
            dyo_n = dy_ref[:, gs]
            dnw_ref[:, gs] += jnp.sum(dyo_n * nrm, axis=0, keepdims=True)
            dn = dyo_n * nw_ref[:, gs]
            dyg = r * (dn - nrm * jnp.mean(dn * nrm, axis=-1, keepdims=True))
            dz_ref[:, gs] = (dyg * ypre * (sg * (1.0 + z * (1.0 - sg)))).astype(BF16)
            dy = dyg * sz

            bg = b_ref[:, bsl].astype(BF16)
            cg = c_ref[:, bsl].astype(BF16)
            cb = _dot(cg, bg, NT)
            hprev = hall_ref[0, gs, :]
            hb = hprev.astype(BF16)
            dhn = dh_scr[gs, :]
            dhb = dhn.astype(BF16)
            xs_g, xdt_g = xs[:, gs], xdtb[:, gs]
            w_off = _dot(cg, hb, NT)
            dyo = dy * ecsx[:, gs]
            dyob = dyo.astype(BF16)
            dcg = _dot(dyob, hb, NN)
            dh_y = _dot(dyob, cg, TN)
            r_st = _dot(bg, dhb, NT)
            dbg = _dot(xdec[:, gs], dhb, NN)
            dyb = dy.astype(BF16)
            gms, gmbs, lms, dys = [], [], [], []
            for e in range(HEADS_PER_GROUP):
                h = g * HEADS_PER_GROUP + e
                lm = jnp.exp(jnp.where(tril, cs[:, h:h + 1] - cst[h:h + 1, :], -1e30))
                gm = cb * lm
                lms.append(lm)
                gms.append(gm)
                gmbs.append(gm.astype(BF16))
                dys.append(jnp.where(head_of_lane == e, dyb, jnp.zeros_like(dyb)))
            dxdt = _dot(jnp.concatenate(gmbs, axis=0), jnp.concatenate(dys, axis=0), TN) + decx[:, gs] * r_st
            dcb = jnp.zeros((SSD_CHUNK, SSD_CHUNK), F32)
            mms = []
            for e in range(HEADS_PER_GROUP):
                dg = _dot(dys[e], xdt_g, NT)
                mms.append(dg * gms[e])
                dcb = dcb + dg * lms[e]
            seg = _dot_split(jnp.concatenate([dyo * w_off, xdecf[:, gs] * r_st, dxdt * xs_g, dy * xs_g], axis=0), exp_g, NT, seg_passes)
            v1, t1, ddt_g, dsk_g = [seg[i * SSD_CHUNK:(i + 1) * SSD_CHUNK] for i in range(4)]
            dcs_col = dcs_col + v1 - t1 + _dot_split(jnp.concatenate(mms, axis=1), ind4_g, NN, seg_passes)
            for t in _split_bf16(jnp.concatenate(mms, axis=0), seg_passes):
                dcs_row = dcs_row + _dot(ind4_g, t, TN)
            ddt = ddt + ddt_g
            ddsk = ddsk + jnp.sum(dsk_g, axis=0, keepdims=True)
            t1_sum = t1_sum + jnp.sum(t1, axis=0, keepdims=True)
            for e in range(HEADS_PER_GROUP):
                h = g * HEADS_PER_GROUP + e
                hs = slice(e * HEAD_DIM, (e + 1) * HEAD_DIM)
                hsum = hsum + jnp.where(lane == h, jnp.sum(dhn[hs, :] * hprev[hs, :]).reshape(1, 1), 0.0)
            dh_scr[gs, :] = dhn * _chunk_decay_rows(cs, g) + dh_y
            dcbb = dcb.astype(BF16)
            dact_ref[:, gs] = dxdt * dtx[:, gs] + dskx_ref[:, gs] * dy
            dact_ref[:, SSD_D_INNER + g * SSD_STATE:SSD_D_INNER + (g + 1) * SSD_STATE] = dbg + _dot(dcbb, cg, TN)
            dact_ref[:, SSD_D_INNER + bc_w + g * SSD_STATE:SSD_D_INNER + bc_w + (g + 1) * SSD_STATE] = dcg + _dot(dcbb, bg, NN)
        dlast = t1_sum + jnp.exp(cs[SSD_CHUNK - 1:SSD_CHUNK, :]) * hsum
        dcs = dcs_col - dcs_row.T + jnp.where(last_row, dlast, 0.0)
        row = lax.broadcasted_iota(jnp.int32, (SSD_CHUNK, SSD_CHUNK), 0)
        col = lax.broadcasted_iota(jnp.int32, (SSD_CHUNK, SSD_CHUNK), 1)
        dda = _dot((col >= row).astype(F32), dcs, NN, precision=HIGHEST)
        ddt = ddt + dda * a
        da_ref[...] += jnp.sum(dda * dt, axis=0, keepdims=True)
        ddtr = jnp.where(lane < SSD_HEADS, ddt * _sigmoid(dtr + dtb_ref[...]), 0.0)
        ddtr_ref[...] = ddtr.astype(BF16)
        ddtb_ref[...] += jnp.sum(ddtr, axis=0, keepdims=True)
        ddsk_ref[...] += ddsk

    rev = lambda c: nc - 1 - c
    row_d = lambda cb: pl.BlockSpec((SSD_CHUNK, SSD_D_INNER), lambda c: (rev(c), cb))
    small = pl.BlockSpec((1, LANE), lambda c: (0, 0))
    wide = pl.BlockSpec((1, SSD_D_INNER), lambda c: (0, 0))
    small_shape = jax.ShapeDtypeStruct((1, LANE), F32)
    return _pcall(
        body, (dycat, ypre, proj, proj, act, act, act, hall, dtb, alog, _expand_heads(dsk), nw, exp_mat, ind4),
        name="ssd_bwd", grid=(nc,),
        in_specs=[row_d(0), row_d(0), row_d(OFF_Z // SSD_D_INNER),
                  pl.BlockSpec((SSD_CHUNK, LANE), lambda c: (rev(c), OFF_DT // LANE)),
                  row_d(0),
                  pl.BlockSpec((SSD_CHUNK, bc_w), lambda c: (rev(c), SSD_D_INNER // bc_w)),
                  pl.BlockSpec((SSD_CHUNK, bc_w), lambda c: (rev(c), SSD_D_INNER // bc_w + 1)),
                  pl.BlockSpec((1, SSD_D_INNER, SSD_STATE), lambda c: (rev(c), 0, 0)),
                  small, small, wide, wide, pl.BlockSpec((LANE, SSD_D_INNER), lambda c: (0, 0)),
                  pl.BlockSpec((SSD_HEADS * SSD_CHUNK, LANE), lambda c: (0, 0))],
        out_specs=[row_d(0), pl.BlockSpec((SSD_CHUNK, CONV_CH), lambda c: (rev(c), 0)),
                   pl.BlockSpec((SSD_CHUNK, LANE), lambda c: (rev(c), 0)), small, small, small, wide],
        out_shape=[jax.ShapeDtypeStruct((s, SSD_D_INNER), BF16), jax.ShapeDtypeStruct((s, CONV_CH), F32),
                   jax.ShapeDtypeStruct((s, LANE), BF16), small_shape, small_shape, small_shape,
                   jax.ShapeDtypeStruct((1, SSD_D_INNER), F32)],
        scratch_shapes=[pltpu.VMEM((SSD_D_INNER, SSD_STATE), F32)], sem=("arbitrary",), comm=comm)


def _head_mean_matrix():
    row = lax.broadcasted_iota(jnp.int32, (LANE, LANE), 0) // HEAD_DIM
    col = lax.broadcasted_iota(jnp.int32, (LANE, LANE), 1) // HEAD_DIM
    return (row == col).astype(F32)


def _head_sum2(v, ones_bd):
    hi = v.astype(BF16)
    lo = (v - hi.astype(F32)).astype(BF16)
    return _dot(jnp.concatenate([hi, lo], axis=1), jnp.concatenate([ones_bd, ones_bd], axis=0), NN)


def _head_norms(xs, ws, ones_bd):
    sums = [_head_sum2(x * x, ones_bd) for x in xs]
    rs = [lax.rsqrt(ms * (1.0 / HEAD_DIM) + EPS) for ms in sums]
    return [(x * r) * w for x, r, w in zip(xs, rs, ws)], rs


def _head_norms_bwd(dns, xs, ws, rs, ones_bd):
    nrms = [x * r for x, r in zip(xs, rs)]
    dnws = [dn * w for dn, w in zip(dns, ws)]
    projs = [_head_sum2(dnw * nrm, ones_bd) for dnw, nrm in zip(dnws, nrms)]
    dxs = [r * (dnw - nrm * (pr * (1.0 / HEAD_DIM))) for r, dnw, nrm, pr in zip(rs, dnws, nrms, projs)]
    return dxs, [jnp.sum(dn * nrm, axis=0, keepdims=True) for dn, nrm in zip(dns, nrms)]


NORM_CHUNKS = 4


PRO_ROWS = 256
ATT_GROUP_FWD = 32
ATT_GROUP_BWD = 8
KEYS = 2 * ATT_BLK
NEG = -1e30
HALF = HEAD_DIM // 2


def _rows(start, size, dil):
    return pl.ds(start, size) if dil == 1 else pl.ds(start, size, stride=dil)


def _fill_bias(bias_ref):
    row = lax.broadcasted_iota(jnp.int32, (ATT_BLK, 2 * KEYS), 0)
    col = lax.broadcasted_iota(jnp.int32, (ATT_BLK, 2 * KEYS), 1) & (KEYS - 1)
    for first, off in ((0, 0), (1, ATT_BLK)):
        dist = off + row - col
        bias_ref[first] = jnp.where((dist >= 0) & (dist <= ATT_BLK), 0.0, NEG)


def _pair(a, b):
    return jnp.concatenate([jnp.broadcast_to(a, (ATT_BLK, KEYS)), jnp.broadcast_to(b, (ATT_BLK, KEYS))], axis=1)


def _split_heads(x, is_a):
    zero = jnp.zeros_like(x)
    return jnp.concatenate([jnp.where(is_a, x, zero), jnp.where(is_a, zero, x)], axis=0)


def _block_ids(b, nb):
    i = b & (nb - 1)
    q0 = pl.multiple_of(b * ATT_BLK, ATT_BLK)
    k0 = pl.multiple_of((b - jnp.minimum(i, 1)) * ATT_BLK, ATT_BLK)
    return pl.ds(q0, ATT_BLK), pl.ds(k0, KEYS), jnp.minimum(i, 1)


def _natural_rows(b, nb, dil):
    if dil == 1:
        return pl.ds(pl.multiple_of(b * ATT_BLK, ATT_BLK), ATT_BLK)
    return pl.ds(b // nb + dil * ((b & (nb - 1)) * ATT_BLK), ATT_BLK, stride=dil)


def _att_fwd(proj, qw, kw, comm=None):
    s = proj.shape[0]
    nblk = s // ATT_BLK
    assert all((s // d) // ATT_BLK >= 2 for d in DILATIONS)
    blk = lambda off: pl.BlockSpec((s, LANE), lambda i: (0, off // LANE + i))
    wspec = pl.BlockSpec((1, LANE), lambda i: (0, i))
    oblk = pl.BlockSpec((s, LANE), lambda i: (0, i))

    def body(q_ref, k_ref, v_ref, qw_ref, kw_ref, o_ref, lse_ref, qn, kn, q_cm, k_cm, v_cm, m_acc, l_acc, o_d, m_d, l_d,
             o_e, m_e, l_e, tq, tk, tv, bias):
        ones_bd = _head_mean_matrix().astype(BF16)
        is_a = lax.broadcasted_iota(jnp.int32, (1, LANE), 1) < HEAD_DIM
        ones_ext = _split_heads(jnp.ones((KEYS, LANE), BF16), is_a)
        _fill_bias(bias)

        def pro(j, c):
            chunks = [pl.ds(pl.multiple_of((NORM_CHUNKS * j + u) * PRO_ROWS, PRO_ROWS), PRO_ROWS) for u in range(NORM_CHUNKS)]
            normed, _ = _head_norms([q_ref[rows, :] for rows in chunks] + [k_ref[rows, :] for rows in chunks],
                                    [qw_ref[...] * HEAD_DIM ** -0.5] * NORM_CHUNKS + [kw_ref[...]] * NORM_CHUNKS, ones_bd)
            for u, rows in enumerate(chunks):
                qn[rows, :] = normed[u]
                kn[rows, :] = normed[NORM_CHUNKS + u]
            return c

        lax.fori_loop(0, s // (NORM_CHUNKS * PRO_ROWS), pro, 0)

        results = dict(zip(DILATIONS, ((o_ref, m_acc, l_acc), (o_d, m_d, l_d), (o_e, m_e, l_e))))
        for dil in DILATIONS:
            ln = s // dil
            nb = ln // ATT_BLK
            o_out, m_out, l_out = results[dil]
            level = DILATIONS.index(dil)
            keep_f32 = 0 < level < len(DILATIONS) - 1
            from_temps = level >= 2
            step_rows = dil // DILATIONS[level - 1] if from_temps else dil
            for r in range(dil):
                prev = DILATIONS[level - 1] if from_temps else 1
                start = (r % prev) * (s // prev) + r // prev if from_temps else r

                def relayout(j, c, r=r, ln=ln, start=start, step_rows=step_rows, keep_f32=keep_f32, from_temps=from_temps):
                    j0 = pl.multiple_of(j * PRO_ROWS, PRO_ROWS)
                    src = _rows(start + step_rows * j0, PRO_ROWS, step_rows)
                    dst = pl.ds(r * ln + j0, PRO_ROWS)
                    qv, kv, vv = (tq[src, :], tk[src, :], tv[src, :]) if from_temps else (qn[src, :], kn[src, :], v_ref[src, :])
                    q_cm[dst, :] = qv.astype(BF16)
                    k_cm[dst, :] = kv.astype(BF16)
                    v_cm[dst, :] = vv.astype(BF16)
                    if keep_f32:
                        tq[dst, :] = qv
                        tk[dst, :] = kv
                        tv[dst, :] = vv
                    return c

                lax.fori_loop(0, ln // PRO_ROWS, relayout, 0)

            def step(bg, c, nb=nb, o_out=o_out, m_out=m_out, l_out=l_out):
                ids = [_block_ids(bg * ATT_GROUP_FWD + u, nb) for u in range(ATT_GROUP_FWD)]
                kbs = [_split_heads(k_cm[krows, :], is_a) for _, krows, _ in ids]
                scs = [_dot(q_cm[qrows, :], kb, NT) + bias[first] for (qrows, _, first), kb in zip(ids, kbs)]
                mas = [jnp.max(sc[:, :KEYS], axis=-1, keepdims=True) for sc in scs]
                mbs = [jnp.max(sc[:, KEYS:], axis=-1, keepdims=True) for sc in scs]
                ps = [jnp.exp(sc - _pair(ma, mb)).astype(BF16) for sc, ma, mb in zip(scs, mas, mbs)]
                vbs = [jnp.concatenate([_split_heads(v_cm[krows, :], is_a), ones_ext], axis=1) for _, krows, _ in ids]
                ols = [_dot(p, vb, NN) for p, vb in zip(ps, vbs)]
                for (qrows, _, _), ol, ma, mb in zip(ids, ols, mas, mbs):
                    o_out[qrows, :] = ol[:, :LANE]
                    l_out[qrows, :] = ol[:, LANE:]
                    m_out[qrows, :] = jnp.where(is_a, ma, mb)
                return c

            lax.fori_loop(0, nblk // ATT_GROUP_FWD, step, 0)

        for level in range(len(DILATIONS) - 1, 0, -1):
            fine_d, coarse_d = DILATIONS[level - 1], DILATIONS[level]
            ratio, ln_f, ln_c = coarse_d // fine_d, s // fine_d, s // coarse_d
            (o_f, m_f, l_f), (o_c, m_c, l_c) = results[fine_d], results[coarse_d]
            for r in range(coarse_d):
                def merge(j, c, r=r, ratio=ratio, ln_c=ln_c, start=(r % fine_d) * ln_f + r // fine_d,
                          o_f=o_f, m_f=m_f, l_f=l_f, o_c=o_c, m_c=m_c, l_c=l_c):
                    j0 = pl.multiple_of(j * PRO_ROWS, PRO_ROWS)
                    fine = _rows(start + ratio * j0, PRO_ROWS, ratio)
                    coarse = pl.ds(r * ln_c + j0, PRO_ROWS)
                    m_old, m_new = m_f[fine, :], m_c[coarse, :]
                    m = jnp.maximum(m_old, m_new)
                    a_old, a_new = jnp.exp(m_old - m), jnp.exp(m_new - m)
                    o_f[fine, :] = a_old * o_f[fine, :] + a_new * o_c[coarse, :]
                    l_f[fine, :] = a_old * l_f[fine, :] + a_new * l_c[coarse, :]
                    m_f[fine, :] = m
                    return c

                lax.fori_loop(0, ln_c // PRO_ROWS, merge, 0)

        def epi(j, c):
            rows = pl.ds(pl.multiple_of(j * PRO_ROWS, PRO_ROWS), PRO_ROWS)
            l = l_acc[rows, :]
            o_ref[rows, :] = o_ref[rows, :] / l
            lse_ref[rows, :] = m_acc[rows, :] + jnp.log(l)
            return c

        lax.fori_loop(0, s // PRO_ROWS, epi, 0)

    f = jax.ShapeDtypeStruct((s, ATT_D), F32)
    scr = pltpu.VMEM((s, LANE), F32)
    scb = pltpu.VMEM((s, LANE), BF16)
    return _pcall(
        body, (proj, proj, proj, qw, kw), name="att_fwd", grid=(ATT_D // LANE,),
        in_specs=[blk(OFF_Q), blk(OFF_K), blk(OFF_V), wspec, wspec], out_specs=[oblk, oblk], out_shape=[f, f],
        scratch_shapes=[scr, scr, scb, scb, scb] + [scr] * 11 + [pltpu.VMEM((2, ATT_BLK, 2 * KEYS), F32)],
        sem=("parallel",), comm=comm)


def _att_bwd(proj, do, stats, qw, kw, comm=None):
    s = proj.shape[0]
    nblk = s // ATT_BLK
    blk = lambda off: pl.BlockSpec((s, LANE), lambda i: (0, off // LANE + i))
    wspec = pl.BlockSpec((1, LANE), lambda i: (0, i))
    oblk = pl.BlockSpec((s, LANE), lambda i: (0, i))

    def body(q_ref, k_ref, v_ref, do_ref, st_ref, qw_ref, kw_ref, dq_ref, dk_ref, dv_ref, dqw_ref, dkw_ref,
             qn, kn, q_cm, do_cm, k_cm, v_cm, rms, dq_acc, dk_acc, dv_acc, dq_d, dk_d, dv_d, dq_e, dk_e, dv_e, bias):
        ones_bd = _head_mean_matrix().astype(BF16)
        is_a = lax.broadcasted_iota(jnp.int32, (1, LANE), 1) < HEAD_DIM
        first_half = (lax.broadcasted_iota(jnp.int32, (1, LANE), 1) & (HEAD_DIM - 1)) < HALF
        _fill_bias(bias)
        zero = jnp.zeros((PRO_ROWS, LANE), F32)
        results = dict(zip(DILATIONS, ((dq_acc, dk_acc, dv_acc), (dq_d, dk_d, dv_d), (dq_e, dk_e, dv_e))))

        def pro(j, c):
            chunks = [pl.ds(pl.multiple_of((NORM_CHUNKS * j + u) * PRO_ROWS, PRO_ROWS), PRO_ROWS) for u in range(NORM_CHUNKS)]
            normed, rs = _head_norms([q_ref[rows, :] for rows in chunks] + [k_ref[rows, :] for rows in chunks],
                                     [qw_ref[...] * HEAD_DIM ** -0.5] * NORM_CHUNKS + [kw_ref[...]] * NORM_CHUNKS, ones_bd)
            for u, rows in enumerate(chunks):
                qn[rows, :] = normed[u]
                kn[rows, :] = normed[NORM_CHUNKS + u]
                rms[rows, :] = jnp.where(first_half, rs[u], rs[NORM_CHUNKS + u])
                dk_acc[rows, :] = zero
                dv_acc[rows, :] = zero
            return c

        lax.fori_loop(0, s // (NORM_CHUNKS * PRO_ROWS), pro, 0)

        for dil in DILATIONS:
            ln = s // dil
            nb = ln // ATT_BLK
            dq_o, dk_o, dv_o = results[dil]
            level = DILATIONS.index(dil)
            keep_f32 = 0 < level < len(DILATIONS) - 1
            from_temps = level >= 2
            temps = results[DILATIONS[-1]]
            for r in range(dil):
                prev = DILATIONS[level - 1] if from_temps else 1
                start = (r % prev) * (s // prev) + r // prev if from_temps else r

                def relayout(j, c, dil=dil, r=r, ln=ln, start=start, step_rows=dil // prev):
                    j0 = pl.multiple_of(j * PRO_ROWS, PRO_ROWS)
                    nat = _rows(r + dil * j0, PRO_ROWS, dil)
                    src = _rows(start + step_rows * j0, PRO_ROWS, step_rows)
                    dst = pl.ds(r * ln + j0, PRO_ROWS)
                    qv, kv, vv = [t[src, :] for t in temps] if from_temps else (qn[src, :], kn[src, :], v_ref[src, :])
                    q_cm[dst, :] = qv.astype(BF16)
                    k_cm[dst, :] = kv.astype(BF16)
                    v_cm[dst, :] = vv.astype(BF16)
                    do_cm[dst, :] = do_ref[nat, :].astype(BF16)
                    if keep_f32:
                        for t, val in zip(temps, (qv, kv, vv)):
                            t[dst, :] = val
                    return c

                lax.fori_loop(0, ln // PRO_ROWS, relayout, 0)

            if dil > 1:
                def clear(j, c, dk_o=dk_o, dv_o=dv_o):
                    rows = pl.ds(pl.multiple_of(j * PRO_ROWS, PRO_ROWS), PRO_ROWS)
                    dk_o[rows, :] = zero
                    dv_o[rows, :] = zero
                    return c

                lax.fori_loop(0, s // PRO_ROWS, clear, 0)

            def step(bg, c, nb=nb, dil=dil, dq_o=dq_o, dk_o=dk_o, dv_o=dv_o):
                blocks = [bg * ATT_GROUP_BWD + u for u in range(ATT_GROUP_BWD)]
                ids = [_block_ids(b, nb) for b in blocks]
                qbs = [q_cm[qrows, :] for qrows, _, _ in ids]
                dobs = [do_cm[qrows, :] for qrows, _, _ in ids]
                kbs = [_split_heads(k_cm[krows, :], is_a) for _, krows, _ in ids]
                vbs = [_split_heads(v_cm[krows, :], is_a) for _, krows, _ in ids]
                sts = [st_ref[_natural_rows(b, nb, dil), :] for b in blocks]
                scs = [_dot(qb, kb, NT) + bias[first] for qb, kb, (_, _, first) in zip(qbs, kbs, ids)]
                dps = [_dot(dob, vb, NT) for dob, vb in zip(dobs, vbs)]
                ps = [jnp.exp(sc - _pair(st[:, 0:1], st[:, HEAD_DIM:HEAD_DIM + 1])) for sc, st in zip(scs, sts)]
                dss = [(p * (dp - _pair(st[:, HALF:HALF + 1], st[:, HEAD_DIM + HALF:HEAD_DIM + HALF + 1]))).astype(BF16)
                       for p, dp, st in zip(ps, dps, sts)]
                dqs = [_dot(ds, kb, NN) for ds, kb in zip(dss, kbs)]
                dkfs = [_dot(ds, qb, TN) for ds, qb in zip(dss, qbs)]
                dvfs = [_dot(p.astype(BF16), dob, TN) for p, dob in zip(ps, dobs)]
                for (qrows, krows, _), dq, dkf, dvf in zip(ids, dqs, dkfs, dvfs):
                    dq_o[qrows, :] = dq
                    dk_o[krows, :] += jnp.where(is_a, dkf[:KEYS], dkf[KEYS:])
                    dv_o[krows, :] += jnp.where(is_a, dvf[:KEYS], dvf[KEYS:])
                return c

            lax.fori_loop(0, nblk // ATT_GROUP_BWD, step, 0)

        for level in range(len(DILATIONS) - 1, 0, -1):
            fine_d, coarse_d = DILATIONS[level - 1], DILATIONS[level]
            ratio, ln_f, ln_c = coarse_d // fine_d, s // fine_d, s // coarse_d
            for r in range(coarse_d):
                def merge(j, c, r=r, ratio=ratio, ln_c=ln_c, start=(r % fine_d) * ln_f + r // fine_d,
                          fine_bufs=results[fine_d], coarse_bufs=results[coarse_d]):
                    j0 = pl.multiple_of(j * PRO_ROWS, PRO_ROWS)
                    fine = _rows(start + ratio * j0, PRO_ROWS, ratio)
                    coarse = pl.ds(r * ln_c + j0, PRO_ROWS)
                    for f_buf, c_buf in zip(fine_bufs, coarse_bufs):
                        f_buf[fine, :] += c_buf[coarse, :]
                    return c

                lax.fori_loop(0, ln_c // PRO_ROWS, merge, 0)

        def epi(j, c):
            chunks = [pl.ds(pl.multiple_of((NORM_CHUNKS * j + u) * PRO_ROWS, PRO_ROWS), PRO_ROWS) for u in range(NORM_CHUNKS)]
            packed = [rms[rows, :] for rows in chunks]
            rs = ([jnp.where(first_half, p, pltpu.roll(p, HALF, axis=1)) for p in packed]
                  + [jnp.where(first_half, pltpu.roll(p, LANE - HALF, axis=1), p) for p in packed])
            dxs, dws = _head_norms_bwd(
                [dq_acc[rows, :] for rows in chunks] + [dk_acc[rows, :] for rows in chunks],
                [q_ref[rows, :] for rows in chunks] + [k_ref[rows, :] for rows in chunks],
                [qw_ref[...] * HEAD_DIM ** -0.5] * NORM_CHUNKS + [kw_ref[...]] * NORM_CHUNKS, rs, ones_bd)
            dqw, dkw = c
            for u, rows in enumerate(chunks):
                dq_ref[rows, :] = dxs[u].astype(BF16)
                dk_ref[rows, :] = dxs[NORM_CHUNKS + u].astype(BF16)
                dv_ref[rows, :] = dv_acc[rows, :].astype(BF16)
                dqw, dkw = dqw + dws[u], dkw + dws[NORM_CHUNKS + u]
            return dqw, dkw

        zrow = jnp.zeros((1, LANE), F32)
        dqw, dkw = lax.fori_loop(0, s // (NORM_CHUNKS * PRO_ROWS), epi, (zrow, zrow))
        dqw_ref[...] = dqw * HEAD_DIM ** -0.5
        dkw_ref[...] = dkw

    o = jax.ShapeDtypeStruct((s, ATT_D), BF16)
    ov = jax.ShapeDtypeStruct((1, ATT_D), F32)
    scr = pltpu.VMEM((s, LANE), F32)
    scb = pltpu.VMEM((s, LANE), BF16)
    return _pcall(
        body, (proj, proj, proj, do, stats, qw, kw), name="att_bwd", grid=(ATT_D // LANE,),
        in_specs=[blk(OFF_Q), blk(OFF_K), blk(OFF_V), oblk, oblk, wspec, wspec],
        out_specs=[oblk, oblk, oblk, wspec, wspec], out_shape=[o, o, o, ov, ov],
        scratch_shapes=[scr, scr, scb, scb, scb, scb] + [scr] * 10 + [pltpu.VMEM((2, ATT_BLK, 2 * KEYS), F32)],
        sem=("parallel",), comm=comm)


def _att_norm_fwd(o, nw, ycat):
    s = o.shape[0]
    row = pl.BlockSpec((ROW_TILE, ATT_D), lambda i: (i, 0))
    vec = pl.BlockSpec((1, ATT_D), lambda i: (0, 0))

    def body(o_ref, nw_ref, ycat_ref, y_ref):
        o = o_ref[...]
        r = lax.rsqrt(jnp.mean(o * o, axis=-1, keepdims=True) + EPS)
        y_ref[...] = (o * r * nw_ref[...]).astype(BF16)

    return pl.pallas_call(body, name="att_norm_fwd", grid=(s // ROW_TILE,),
                          in_specs=[row, vec, pl.BlockSpec(memory_space=pl.ANY)],
                          out_specs=pl.BlockSpec((ROW_TILE, ATT_D), lambda i: (i, 1)),
                          out_shape=jax.ShapeDtypeStruct(ycat.shape, BF16), input_output_aliases={2: 0},
                          compiler_params=_cparams(("parallel",)))(o, nw, ycat)


def _mixer_split_epilogue(dycat, first, rows, vecs, outs):
    (o_ref, lse_ref), (nw_ref,), (dyssd_ref, do_ref, st_ref, dnw_ref) = rows, vecs, outs

    @pl.when(first)
    def _():
        dnw_ref[...] = jnp.zeros_like(dnw_ref)

    dyssd_ref[...] = dycat[:, :SSD_D_INNER]
    dy = dycat[:, SSD_D_INNER:]
    o = o_ref[...]
    r = lax.rsqrt(jnp.mean(o * o, axis=-1, keepdims=True) + EPS)
    nrm = o * r
    dnw_ref[...] += jnp.sum(dy * nrm, axis=0, keepdims=True)
    dn = dy * nw_ref[...]
    do = r * (dn - nrm * jnp.mean(dn * nrm, axis=-1, keepdims=True))
    do_ref[...] = do
    ones_bd = _head_mean_matrix().astype(BF16)
    prod = do * o
    delta = jnp.concatenate([_head_sum2(prod[:, j * LANE:(j + 1) * LANE], ones_bd) for j in range(ATT_D // LANE)], axis=1)
    lane = lax.broadcasted_iota(jnp.int32, (1, ATT_D), 1)
    st_ref[...] = jnp.where((lane & (HEAD_DIM - 1)) < HALF, lse_ref[...], delta)


def _ada_fwd(c_all, w_ada):
    def body(c_ref, w_ref, o_ref):
        cv = c_ref[...]
        o_ref[...] = _dot((cv * _sigmoid(cv)).astype(BF16), w_ref[...].astype(BF16), NN)

    return pl.pallas_call(body, name="ada_fwd", out_shape=jax.ShapeDtypeStruct((c_all.shape[0], w_ada.shape[1]), F32),
                          compiler_params=_cparams())(c_all, w_ada)


def _adamw_math(g, w, m, v):
    m_new = ADAM_B1 * m + (1.0 - ADAM_B1) * g
    v_new = ADAM_B2 * v + (1.0 - ADAM_B2) * (g * g)
    m_hat = m_new / (1.0 - ADAM_B1 ** ADAM_STEP)
    v_hat = v_new / (1.0 - ADAM_B2 ** ADAM_STEP)
    delta = -ADAM_LR * (m_hat / (jnp.sqrt(v_hat) + ADAM_EPS) + ADAM_WD * w)
    return delta, m_new, v_new


def _ada_bwd_adamw(c_all, dmod_cols, w, m, v):
    rows, cols = w.shape
    tr = 256
    blk = pl.BlockSpec((tr, cols), lambda i: (i, 0))

    def body(c_ref, d_ref, w_ref, m_ref, v_ref, g_ref, dl_ref, mo_ref, vo_ref):
        cv = c_ref[...]
        ca = cv * _sigmoid(cv)
        g = ca[:, 0:1] * d_ref[0:1, :]
        for b in range(1, N_DEV):
            g = g + ca[:, b:b + 1] * d_ref[b:b + 1, :]
        g_ref[...] = g
        dl_ref[...], mo_ref[...], vo_ref[...] = _adamw_math(g, w_ref[...], m_ref[...], v_ref[...])

    o = jax.ShapeDtypeStruct((rows, cols), F32)
    return pl.pallas_call(
        body, name="ada_bwd_adamw", grid=(rows // tr,),
        in_specs=[pl.BlockSpec((tr, N_DEV), lambda i: (i, 0)), pl.BlockSpec((N_DEV, cols), lambda i: (0, 0)), blk, blk, blk],
        out_specs=[blk] * 4, out_shape=[o, o, o, o], compiler_params=_cparams(("parallel",)))(c_all.T, dmod_cols, w, m, v)


def _reduce_adamw(slabs, w, m, v, name):
    rows, cols = w.shape
    n_src = slabs.shape[0]
    if rows % 128 == 0:
        tr, steps = 128, rows // 128
        blk = pl.BlockSpec((tr, cols), lambda i: (i, 0))
        sblk = pl.BlockSpec((n_src, tr, cols), lambda i: (0, i, 0))
    else:
        tc, steps = 256, cols // 256
        blk = pl.BlockSpec((rows, tc), lambda i: (0, i))
        sblk = pl.BlockSpec((n_src, rows, tc), lambda i: (0, 0, i))

    def body(s_ref, w_ref, m_ref, v_ref, g_ref, dl_ref, mo_ref, vo_ref):
        g = s_ref[0].astype(F32)
        for src in range(1, n_src):
            g = g + s_ref[src].astype(F32)
        g_ref[...] = g
        dl_ref[...], mo_ref[...], vo_ref[...] = _adamw_math(g, w_ref[...], m_ref[...], v_ref[...])

    o = jax.ShapeDtypeStruct((rows, cols), F32)
    return pl.pallas_call(
        body, name=name, grid=(steps,), in_specs=[sblk, blk, blk, blk],
        out_specs=[blk] * 4, out_shape=[o, o, o, o], compiler_params=_cparams(("parallel",)))(slabs, w, m, v)


def _small_reduce_adamw(gathered, w, m, v):
    def body(s_ref, w_ref, m_ref, v_ref, g_ref, dl_ref, mo_ref, vo_ref):
        g = s_ref[0]
        for dev in range(1, N_DEV):
            g = g + s_ref[dev]
        g_ref[...] = g
        dl_ref[...], mo_ref[...], vo_ref[...] = _adamw_math(g, w_ref[...], m_ref[...], v_ref[...])

    o = jax.ShapeDtypeStruct(w.shape, F32)
    return pl.pallas_call(body, name="small_reduce_adamw", out_shape=[o, o, o, o], compiler_params=_cparams())(gathered, w, m, v)


def _adamw_small(g, w, m, v, name):
    def body(g_ref, w_ref, m_ref, v_ref, dl_ref, mo_ref, vo_ref):
        dl_ref[...], mo_ref[...], vo_ref[...] = _adamw_math(g_ref[...], w_ref[...], m_ref[...], v_ref[...])

    o = jax.ShapeDtypeStruct(w.shape, F32)
    return pl.pallas_call(body, name=name, out_shape=[o, o, o], compiler_params=_cparams())(g, w, m, v)


class _Exchange:
    def __init__(self, arrs, scatter):
        self.arrs, self.scatter, self.n = list(arrs), scatter, len(arrs)
        hbm = pl.BlockSpec(memory_space=pltpu.HBM)
        self.in_specs = [hbm] * self.n
        self.out_specs = [hbm] * self.n
        self.out_shape = [jax.ShapeDtypeStruct(a.shape if scatter else (N_DEV,) + a.shape, a.dtype) for a in self.arrs]
        self.scratch = [pltpu.SemaphoreType.DMA((self.n * (N_DEV - 1),)), pltpu.SemaphoreType.DMA((self.n * (N_DEV - 1),)),
                        pltpu.SemaphoreType.DMA((self.n,))]

    def _local(self, ins, outs, sems):
        me = 4 * lax.axis_index("x") + 2 * lax.axis_index("y") + lax.axis_index("c")
        return [pltpu.make_async_copy(ins[a].at[me] if self.scatter else ins[a], outs[a].at[me], sems[2].at[a])
                for a in range(self.n)]

    def _remote(self, ins, outs, sems, arriving):
        send_sems, recv_sems, _ = sems
        x, y, c = lax.axis_index("x"), lax.axis_index("y"), lax.axis_index("c")
        me = 4 * x + 2 * y + c
        remote = []
        for a in range(self.n):
            for k in range(1, N_DEV):
                px = 1 - x if k & 4 else x
                py = 1 - y if k & 2 else y
                pc = 1 - c if k & 1 else c
                peer = 4 * px + 2 * py + pc
                sem = a * (N_DEV - 1) + k - 1
                remote.append(pltpu.make_async_remote_copy(
                    src_ref=ins[a].at[peer] if self.scatter else ins[a], dst_ref=outs[a].at[peer if arriving else me],
                    send_sem=send_sems.at[sem], recv_sem=recv_sems.at[sem], device_id=(px, py, pc), device_id_type=MESH_IDS))
        return remote

    def start(self, ins, outs, sems):
        for cp in self._local(ins, outs, sems) + self._remote(ins, outs, sems, arriving=False):
            cp.start()

    def forward(self, ins, outs, sems):
        pass

    def wait(self, ins, outs, sems):
        for send, arrival in zip(self._remote(ins, outs, sems, arriving=False), self._remote(ins, outs, sems, arriving=True)):
            send.wait_send()
            arrival.wait_recv()
        for cp in self._local(ins, outs, sems):
            cp.wait()


N_CHIP = N_DEV // 2


class _SiblingSwap(_Exchange):
    def __init__(self, arrs):
        super().__init__(arrs, scatter=True)
        self.out_shape = [jax.ShapeDtypeStruct((N_CHIP,) + a.shape[2:], a.dtype) for a in self.arrs]
        self.scratch = [pltpu.SemaphoreType.DMA((self.n,)), pltpu.SemaphoreType.DMA((self.n,)), pltpu.SemaphoreType.DMA((1,))]

    def _copies(self, ins, outs, sems):
        x, y, c = lax.axis_index("x"), lax.axis_index("y"), lax.axis_index("c")
        return [pltpu.make_async_remote_copy(src_ref=ins[a].at[:, 1 - c], dst_ref=outs[a], send_sem=sems[0].at[a], recv_sem=sems[1].at[a],
                                             device_id=(x, y, 1 - c), device_id_type=MESH_IDS) for a in range(self.n)]

    def start(self, ins, outs, sems):
        for cp in self._copies(ins, outs, sems):
            cp.start()

    def wait(self, ins, outs, sems):
        for cp in self._copies(ins, outs, sems):
            cp.wait()


class _ChipScatter(_Exchange):
    def __init__(self, arrs):
        super().__init__(arrs, scatter=True)
        n_pairs = self.n * (N_CHIP - 1)
        self.scratch = [pltpu.SemaphoreType.DMA((n_pairs,)), pltpu.SemaphoreType.DMA((n_pairs,)), pltpu.SemaphoreType.DMA((self.n,))]

    def _local(self, ins, outs, sems):
        chip = 2 * lax.axis_index("x") + lax.axis_index("y")
        return [pltpu.make_async_copy(ins[a].at[chip], outs[a].at[chip], sems[2].at[a]) for a in range(self.n)]

    def _remote(self, ins, outs, sems, arriving):
        send_sems, recv_sems, _ = sems
        x, y, c = lax.axis_index("x"), lax.axis_index("y"), lax.axis_index("c")
        chip = 2 * x + y
        remote = []
        for a in range(self.n):
            for k in range(1, N_CHIP):
                px = 1 - x if k & 2 else x
                py = 1 - y if k & 1 else y
                peer = 2 * px + py
                sem = a * (N_CHIP - 1) + k - 1
                remote.append(pltpu.make_async_remote_copy(
                    src_ref=ins[a].at[peer], dst_ref=outs[a].at[peer if arriving else chip], send_sem=send_sems.at[sem],
                    recv_sem=recv_sems.at[sem], device_id=(px, py, c), device_id_type=MESH_IDS))
        return remote


def _chip_sum(mine, theirs):
    n, rows, cols = mine.shape
    blk = pl.BlockSpec((1, rows, 256), lambda q, j: (q, 0, j))

    def body(a_ref, b_ref, o_ref):
        o_ref[...] = (a_ref[...].astype(F32) + b_ref[...].astype(F32)).astype(BF16)

    return pl.pallas_call(body, name="chip_sum", grid=(n, cols // 256), in_specs=[blk, blk], out_specs=blk,
                          out_shape=jax.ShapeDtypeStruct(mine.shape, BF16),
                          compiler_params=_cparams(("parallel", "parallel")))(mine, theirs)


class _Gather2(_Exchange):
    def __init__(self, arrs):
        super().__init__(arrs, scatter=False)

    def _copies(self, ins, outs, sems):
        send_sems, recv_sems, _ = sems
        x, y, c = lax.axis_index("x"), lax.axis_index("y"), lax.axis_index("c")
        sibling = (x, y, 1 - c)
        chips = [(1 - x, y), (x, 1 - y), (1 - x, 1 - y)]
        first, passed, landed = [], [], []
        for a in range(self.n):
            def copy(k, block, to, src=None, a=a):
                slab = outs[a].at[4 * block[0] + 2 * block[1] + block[2]]
                return pltpu.make_async_remote_copy(
                    src_ref=slab if src is None else src, dst_ref=slab, send_sem=send_sems.at[a * (N_DEV - 1) + k],
                    recv_sem=recv_sems.at[a * (N_DEV - 1) + k], device_id=to, device_id_type=MESH_IDS)

            first.append(copy(0, (x, y, c), sibling, src=ins[a]))
            landed.append(copy(0, sibling, sibling))
            for j, chip in enumerate(chips):
                first.append(copy(1 + j, (x, y, c), (*chip, c), src=ins[a]))
                passed.append((copy(1 + j, (*chip, c), sibling), copy(4 + j, (*chip, c), sibling)))
                landed.append(copy(4 + j, (*chip, 1 - c), sibling))
        return first, passed, landed

    def start(self, ins, outs, sems):
        for cp in self._local(ins, outs, sems) + self._copies(ins, outs, sems)[0]:
            cp.start()

    def forward(self, ins, outs, sems):
        for arrival, onward in self._copies(ins, outs, sems)[1]:
            arrival.wait_recv()
            onward.start()

    def wait(self, ins, outs, sems):
        first, passed, landed = self._copies(ins, outs, sems)
        for arrival in landed:
            arrival.wait_recv()
        for cp in first + [onward for _, onward in passed]:
            cp.wait_send()
        for cp in self._local(ins, outs, sems):
            cp.wait()


def _split_comm_refs(refs, n_in, n_out, n_scr, comm):
    nc = comm.n if comm is not None else 0
    ns = 3 if comm is not None else 0
    pos, groups = 0, []
    for cnt in (n_in, nc, n_out, nc, n_scr, ns):
        groups.append(refs[pos:pos + cnt])
        pos += cnt
    assert pos == len(refs), (pos, len(refs))
    return groups


def _pcall(body, args, *, name, grid, in_specs, out_specs, out_shape, scratch_shapes=(), sem=None, comm=None):
    in_specs, out_specs, out_shape, scratch_shapes = list(in_specs), list(out_specs), list(out_shape), list(scratch_shapes)
    n_in, n_out, n_scr = len(in_specs), len(out_specs), len(scratch_shapes)
    if comm is None:
        kernel_body = body
    else:
        def kernel_body(*refs):
            ins, cins, outs, couts, scr, sems = _split_comm_refs(refs, n_in, n_out, n_scr, comm)
            ids = [pl.program_id(a) for a in range(len(grid))]
            first, last = ids[0] == 0, ids[0] == grid[0] - 1
            for a in range(1, len(grid)):
                first, last = first & (ids[a] == 0), last & (ids[a] == grid[a] - 1)

            middle = ids[0] == (2 * grid[0]) // 3
            for a in range(1, len(grid)):
                middle = middle & (ids[a] == 0)

            @pl.when(first)
            def _():
                comm.start(cins, couts, sems)

            @pl.when(middle)
            def _():
                comm.forward(cins, couts, sems)

            body(*ins, *outs, *scr)

            @pl.when(last)
            def _():
                comm.wait(cins, couts, sems)

        in_specs, out_specs, out_shape = in_specs + comm.in_specs, out_specs + comm.out_specs, out_shape + comm.out_shape
        scratch_shapes, args = scratch_shapes + comm.scratch, list(args) + comm.arrs
        sem = ("arbitrary",) * len(grid)
    res = pl.pallas_call(kernel_body, name=name, grid=grid, in_specs=in_specs, out_specs=out_specs, out_shape=out_shape,
                         scratch_shapes=scratch_shapes, compiler_params=_cparams(sem))(*args)
    return res[:n_out], res[n_out:]


def _exchange(arrs, name, scatter=False, ex=None):
    if ex is None:
        ex = _Exchange(arrs, scatter=True) if scatter else _Gather2(arrs)

    def body(*refs):
        _, ins, _, outs, _, sems = _split_comm_refs(refs, 0, 0, 0, ex)
        ex.start(ins, outs, sems)
        ex.forward(ins, outs, sems)
        ex.wait(ins, outs, sems)

    return pl.pallas_call(body, name=name, in_specs=ex.in_specs, out_specs=ex.out_specs, out_shape=ex.out_shape,
                          scratch_shapes=ex.scratch)(*ex.arrs)


def _pad_lanes(v, width=LANE):
    return jnp.pad(v, ((0, 0), (0, width - v.shape[1])))


def _shards_to_cols(g):
    return jnp.transpose(g, (1, 0, 2)).reshape(g.shape[1], N_DEV * g.shape[2])


def _local_step(x, tgt, mod, w_in_pt, conv_w, conv_b, dt_bias, a_log, d_skip, ssd_norm_w, q_norm_w, k_norm_w,
                attn_norm_w, w_out_sh, w_ff1_sh, w_ff2_sh, norm1_w, norm2_w, core):
    shift1, scale1, gate1, shift2, scale2, gate2 = [mod[i:i + 1] for i in range(N_MOD)]
    dtb, alog, dsk = _pad_lanes(dt_bias), _pad_lanes(a_log), _pad_lanes(d_skip)
    qw, kw = jnp.tile(q_norm_w, (1, ATT_HEADS)), jnp.tile(k_norm_w, (1, ATT_HEADS))

    h1 = _norm_mod_fwd(x, norm1_w, scale1, shift1, "norm1_fwd")
    proj = _matmul(h1, w_in_pt, tb=True, tm=2048, tn=896, tk=1024, name="in_proj")
    pre, act = _conv_fwd(proj, conv_w, conv_b)
    ypre, ycat_ssd, hall = _ssd_fwd(proj, act, dtb, alog, dsk, ssd_norm_w)
    (o_att, lse), (w_out_g, w_ff1_g, w_ff2_g) = _att_fwd(proj, qw, kw, comm=_Gather2([w_out_sh, w_ff1_sh, w_ff2_sh]))
    w_out = w_out_g.reshape(2 * D_MODEL, D_MODEL)
    w_ff1 = _shards_to_cols(w_ff1_g)
    w_ff2 = w_ff2_g.reshape(D_FF, D_MODEL)
    ycat = _att_norm_fwd(o_att, attn_norm_w, ycat_ssd)
    row32, row16, vec32 = ("row", F32), ("row", BF16), ("vec", F32)
    mix, x1, h2 = _matmul_rows(ycat, w_out, _residual_norm_epilogue, [x], [gate1, norm2_w, scale2, shift2],
                               [row32, row32, row16], tm=512, name="out_proj")
    u, act_ff = _matmul(h2, w_ff1, tm=1024, tn=2048, tk=1024, name="ff1", mode="relu2")
    loss, dout, dff, dgate2 = _matmul_rows(act_ff, w_ff2, _loss_epilogue, [x1, tgt], [gate2],
                                           [("one", F32), row32, row16, vec32], tm=512, name="ff2")

    du = _matmul(dff, w_ff2, tb=True, tm=512, tn=4096, tk=1024, out_dtype=BF16, name="ff2_dx", mode="drelu2", u=u)
    g_ff2 = _matmul(act_ff, dff, ta=True, tm=512, tn=1024, tk=4096, out_dtype=BF16, name="ff2_dw")
    dx1, dshift2, dscale2, g_norm2, dmix, dgate1 = _matmul_rows(
        du, w_ff1, _norm_bwd_epilogue, [x1, dout, mix], [norm2_w, scale2, gate1],
        [row32, vec32, vec32, vec32, row16, vec32], tb=True, tm=512, name="ff1_dx")
    g_ff1 = _matmul(h2, du, ta=True, tm=1024, tn=D_FF // N_DEV, tk=4096, out_dtype=BF16, name="ff1_dw", shard_out=True)

    dy_ssd, do, stats, g_attn_norm = _matmul_rows(
        dmix, w_out, _mixer_split_epilogue, [o_att, lse], [attn_norm_w],
        [("row", F32, SSD_D_INNER), ("row", F32, ATT_D), ("row", F32, ATT_D), ("vec", F32, ATT_D)], tb=True, tm=512, name="out_proj_dx")
    g_out = _matmul(ycat, dmix, ta=True, tm=512, tn=1024, tk=4096, out_dtype=BF16, name="out_proj_dw")
    ff_slabs = [g_ff1, g_ff2.reshape(N_DEV, D_FF // N_DEV, D_MODEL)]
    (dq, dk, dv, dqw, dkw), (s_ff1, s_ff2) = _att_bwd(proj, do, stats, qw, kw, comm=_Exchange(ff_slabs, scatter=True))
    out_slabs = [g_out.astype(BF16).reshape(N_DEV, 2 * D_MODEL // N_DEV, D_MODEL)]
    (dz, dact, ddtr, da, g_dsk, g_dtb, g_ssd_norm), (s_out,) = _ssd_bwd(
        dy_ssd, ypre, proj, act, hall, dtb, alog, dsk, ssd_norm_w, comm=_Exchange(out_slabs, scatter=True))
    dxbc, g_conv_w, g_conv_b = _conv_bwd(dact, pre, proj, conv_w)
    dproj = [(dz, OFF_Z), (dxbc, OFF_XBC), (ddtr, OFF_DT), (dq, OFF_Q), (dk, OFF_K), (dv, OFF_V)]
    g_head, g_tail = _pieces_t_matmul([[dz, dxbc], [dq, dk, dv]], h1, tm=256, name="in_proj_dw")
    g_dt = _matmul(ddtr, h1, ta=True, tm=LANE, tn=1024, tk=4096, out_dtype=BF16, name="in_proj_dw_dt")[:SSD_HEADS]
    in_slabs = jnp.concatenate([g_head, g_dt, g_tail], axis=0).reshape(N_CHIP, 2, IN_W // N_DEV, D_MODEL)
    (sibling_slabs,) = _exchange(None, "swap_w_in_grads", ex=_SiblingSwap([in_slabs]))
    chip_slabs = _chip_sum(lax.dynamic_index_in_dim(in_slabs, core, axis=1, keepdims=False), sibling_slabs)
    (grad_x, dshift1, dscale1, g_norm1), (s_in,) = _matmul_rows(
        dproj, w_in_pt, _norm_bwd_epilogue, [x, dx1], [norm1_w, scale1], [row32, vec32, vec32, vec32],
        tm=256, name="in_proj_dx", comm=_ChipScatter([chip_slabs]))

    dmod = jnp.concatenate([dshift1, dscale1, dgate1, dshift2, dscale2, dgate2], axis=0)
    g_alog = da[:, :SSD_HEADS] * (-jnp.exp(a_log))
    g_qw = dqw.reshape(ATT_HEADS, HEAD_DIM).sum(axis=0, keepdims=True)
    g_kw = dkw.reshape(ATT_HEADS, HEAD_DIM).sum(axis=0, keepdims=True)
    return dict(loss=loss, grad_x=grad_x, dmod=dmod, norm1_w=g_norm1, norm2_w=g_norm2, w_in=s_in, conv_w=g_conv_w,
                conv_b=g_conv_b, dt_bias=g_dtb[:, :SSD_HEADS], a_log=g_alog, d_skip=g_dsk[:, :SSD_HEADS],
                ssd_norm_w=g_ssd_norm, q_norm_w=g_qw, k_norm_w=g_kw, attn_norm_w=g_attn_norm, w_out=s_out,
                w_ff1=s_ff1, w_ff2=s_ff2)


def _pack_w_in_rows(wt_full):
    cut = OFF_DT + SSD_HEADS
    pad = jnp.zeros((LANE - SSD_HEADS, wt_full.shape[1]), wt_full.dtype)
    return jnp.concatenate([wt_full[:cut], pad, wt_full[cut:]], axis=0)


MISC_FIELDS = (("dt_bias", SSD_HEADS), ("a_log", SSD_HEADS), ("d_skip", SSD_HEADS), ("q_norm_w", HEAD_DIM), ("k_norm_w", HEAD_DIM),
               ("loss", 1))
SMALL_LAYOUT = (("b_ada", 6), ("norm1_w", 1), ("norm2_w", 1), ("conv_w", 8), ("conv_b", 2), ("ssd_norm_w", 1),
                ("attn_norm_w", 1), ("misc", 1))


def _pack_small(vals):
    rows = []
    for name, nrow in SMALL_LAYOUT:
        if name == "misc":
            misc = jnp.concatenate([vals[f].reshape(1, n) if f in vals else jnp.zeros((1, n), F32) for f, n in MISC_FIELDS], axis=1)
            rows.append(_pad_lanes(misc, D_MODEL))
        elif name in vals:
            rows.append(vals[name].reshape(nrow, D_MODEL))
        else:
            rows.append(jnp.zeros((nrow, D_MODEL), F32))
    used = sum(n for _, n in SMALL_LAYOUT)
    rows.append(jnp.zeros((SMALL_ROWS - used, D_MODEL), F32))
    return jnp.concatenate(rows, axis=0)


def _unpack_small(packed):
    out, r = {}, 0
    for name, nrow in SMALL_LAYOUT:
        blk = packed[r:r + nrow]
        r += nrow
        if name == "misc":
            c0 = 0
            for f, n in MISC_FIELDS:
                out[f] = blk[:, c0:c0 + n]
                c0 += n
        elif name == "b_ada":
            out[name] = blk.reshape(1, N_MOD * D_MODEL)
        elif name == "conv_w":
            out[name] = blk.reshape(CONV_K, CONV_CH)
        elif name == "conv_b":
            out[name] = blk.reshape(1, CONV_CH)
        else:
            out[name] = blk
    return out


WEIGHT_NAMES = ("norm1_w", "norm2_w", "w_ada", "b_ada", "w_in", "conv_w", "conv_b", "dt_bias", "a_log", "d_skip",
                "ssd_norm_w", "q_norm_w", "k_norm_w", "attn_norm_w", "w_out", "w_ff1", "w_ff2")
SMALL_NAMES = ("norm1_w", "norm2_w", "b_ada", "conv_b", "dt_bias", "a_log", "d_skip", "ssd_norm_w", "q_norm_w",
               "k_norm_w", "attn_norm_w")


def kernel(x, c, norm1_w, norm2_w, w_ada, b_ada, w_in, conv_w, conv_b, dt_bias, a_log, d_skip, ssd_norm_w, q_norm_w, k_norm_w, attn_norm_w, w_out, w_ff1, w_ff2, loss_target, m_norm1_w, m_norm2_w, m_w_ada, m_b_ada, m_w_in, m_conv_w, m_conv_b, m_dt_bias, m_a_log, m_d_skip, m_ssd_norm_w, m_q_norm_w, m_k_norm_w, m_attn_norm_w, m_w_out, m_w_ff1, m_w_ff2, v_norm1_w, v_norm2_w, v_w_ada, v_b_ada, v_w_in, v_conv_w, v_conv_b, v_dt_bias, v_a_log, v_d_skip, v_ssd_norm_w, v_q_norm_w, v_k_norm_w, v_attn_norm_w, v_w_out, v_w_ff1, v_w_ff2):
    args = dict(locals())
    w = {n: args[n] for n in WEIGHT_NAMES}
    m = {n: args["m_" + n] for n in WEIGHT_NAMES}
    v = {n: args["v_" + n] for n in WEIGHT_NAMES}
    me = 4 * lax.axis_index("x") + 2 * lax.axis_index("y") + lax.axis_index("c")

    c_rows = jnp.pad(c, ((0, 7), (0, 0)))
    w_in_t, m_in_t, v_in_t = [jnp.transpose(t["w_in"][0]) for t in (w, m, v)]
    c_g, conv_g, w_in_g = _exchange([c_rows, w["conv_w"][0], w_in_t.astype(BF16)], "gather_w_in", scatter=False)
    c_all = c_g[:, 0, :]
    conv_full = _shards_to_cols(conv_g)
    w_in_pt = _pack_w_in_rows(w_in_g.reshape(IN_W, D_MODEL))

    mod_part = _ada_fwd(c_all, w["w_ada"][0])
    (mod_g,) = _exchange([mod_part], "gather_mod", scatter=False)
    mod_mine = lax.dynamic_index_in_dim(mod_g, me, axis=1, keepdims=False).reshape(1, N_MOD * D_MODEL) + w["b_ada"]
    mod = mod_mine.reshape(N_MOD, D_MODEL)

    res = _local_step(x[0], loss_target[0], mod, w_in_pt, conv_full, w["conv_b"], w["dt_bias"], w["a_log"], w["d_skip"],
                      w["ssd_norm_w"], w["q_norm_w"], w["k_norm_w"], w["attn_norm_w"], w["w_out"][0].astype(BF16),
                      w["w_ff1"][0].astype(BF16), w["w_ff2"][0].astype(BF16), w["norm1_w"], w["norm2_w"], lax.axis_index("c"))

    small_vals = {n: res[n] for n in SMALL_NAMES if n != "b_ada"}
    small_vals["b_ada"] = res["dmod"]
    small_vals["conv_w"] = res["conv_w"]
    small_vals["loss"] = res["loss"]
    (small_g,) = _exchange([_pack_small(small_vals)], "gather_small", scatter=False)

    grads, delta, new_m, new_v = {}, {}, {}, {}
    for name in ("w_out", "w_ff1", "w_ff2"):
        outs = _reduce_adamw(res[name], w[name][0], m[name][0], v[name][0], "adamw_" + name)
        grads[name], delta[name], new_m[name], new_v[name] = [o[None] for o in outs]
    outs = _reduce_adamw(res["w_in"], w_in_t, m_in_t, v_in_t, "adamw_w_in")
    grads["w_in"], delta["w_in"], new_m["w_in"], new_v["w_in"] = [jnp.transpose(o)[None] for o in outs]

    sm = _small_reduce_adamw(small_g, _pack_small({n: w[n] for n in SMALL_NAMES}), _pack_small({n: m[n] for n in SMALL_NAMES}),
                             _pack_small({n: v[n] for n in SMALL_NAMES}))
    sm = [_unpack_small(p) for p in sm]
    for n in SMALL_NAMES:
        grads[n], delta[n], new_m[n], new_v[n] = [p[n] for p in sm]
    shard_w = CONV_CH // N_DEV
    g_conv = lax.dynamic_slice_in_dim(sm[0]["conv_w"], me * shard_w, shard_w, axis=1)
    cw = _adamw_small(g_conv, w["conv_w"][0], m["conv_w"][0], v["conv_w"][0], "adamw_conv_w")
    grads["conv_w"] = g_conv[None]
    delta["conv_w"], new_m["conv_w"], new_v["conv_w"] = [o[None] for o in cw]

    ada_w = w_ada.shape[2]
    dmod_all = small_g[:, :N_MOD, :].reshape(N_DEV, N_MOD * D_MODEL)
    dmod_cols = lax.dynamic_slice_in_dim(dmod_all, me * ada_w, ada_w, axis=1)
    outs = _ada_bwd_adamw(c_all, dmod_cols, w["w_ada"][0], m["w_ada"][0], v["w_ada"][0])
    grads["w_ada"], delta["w_ada"], new_m["w_ada"], new_v["w_ada"] = [o[None] for o in outs]

    loss = sm[0]["loss"][0, 0]
    return (loss, res["grad_x"][None], *[grads[n] for n in WEIGHT_NAMES], *[delta[n] for n in WEIGHT_NAMES],
            *[new_m[n] for n in WEIGHT_NAMES], *[new_v[n] for n in WEIGHT_NAMES])
```

```python
import jax
import jax.numpy as jnp
from jax import lax
from jax.experimental import pallas as pl
from jax.experimental.pallas import tpu as pltpu

F32 = jnp.float32
BF16 = jnp.bfloat16
HIGHEST = lax.Precision.HIGHEST
MESH_IDS = pl.DeviceIdType.MESH

N_DEV = 8
D_MODEL = 1024
HEAD_DIM = 64
SSD_HEADS = 16
SSD_GROUPS = 4
HEADS_PER_GROUP = SSD_HEADS // SSD_GROUPS
SSD_STATE = 128
SSD_CHUNK = 128
SSD_D_INNER = SSD_HEADS * HEAD_DIM
GROUP_WIDTH = SSD_D_INNER // SSD_GROUPS
CONV_K = 4
CONV_CH = SSD_D_INNER + 2 * SSD_GROUPS * SSD_STATE
ATT_HEADS = 16
ATT_D = ATT_HEADS * HEAD_DIM
ATT_BLK = 128
DILATIONS = (1, 4, 16)
D_FF = 4 * D_MODEL
N_MOD = 6
EPS = 1e-6
IN_W = SSD_D_INNER + CONV_CH + SSD_HEADS + 3 * ATT_D
LANE = 128
OFF_Z, OFF_XBC, OFF_DT = 0, SSD_D_INNER, SSD_D_INNER + CONV_CH
OFF_Q = OFF_DT + LANE
OFF_K, OFF_V = OFF_Q + ATT_D, OFF_Q + 2 * ATT_D
IN_WP = OFF_V + ATT_D

ADAM_LR, ADAM_B1, ADAM_B2, ADAM_EPS, ADAM_WD, ADAM_STEP = 0.001, 0.9, 0.999, 1e-08, 0.01, 10
VMEM_LIMIT = 60 * 1024 * 1024
ROW_TILE = 512
SMALL_ROWS = 24


def _cparams(sem=None):
    return pltpu.CompilerParams(dimension_semantics=sem, vmem_limit_bytes=VMEM_LIMIT)


def _sigmoid(v):
    return 1.0 / (1.0 + jnp.exp(-v))


def _softplus(v):
    y = jnp.exp(-jnp.abs(v))
    small = y * (1.0 - y * (0.5 - y * (1.0 / 3.0)))
    return jnp.maximum(v, 0.0) + jnp.where(y < 0.01, small, jnp.log(1.0 + y))


def _dot(a, b, dims, precision=None):
    return lax.dot_general(a, b, (dims, ((), ())), preferred_element_type=F32, precision=precision)


NN = ((1,), (0,))
NT = ((1,), (1,))
TN = ((0,), (0,))


def _matmul(a, b, *, ta=False, tb=False, tm, tn, tk, out_dtype=F32, name, mode=None, u=None, comm=None, shard_out=False):
    m, k = (a.shape[1], a.shape[0]) if ta else a.shape
    n = b.shape[0] if tb else b.shape[1]
    assert m % tm == 0 and n % tn == 0 and k % tk == 0, (name, m, n, k)
    nk = k // tk
    a_spec = pl.BlockSpec((tk, tm), lambda i, j, kk: (kk, i)) if ta else pl.BlockSpec((tm, tk), lambda i, j, kk: (i, kk))
    b_spec = pl.BlockSpec((tn, tk), lambda i, j, kk: (j, kk)) if tb else pl.BlockSpec((tk, tn), lambda i, j, kk: (kk, j))
    o_spec = pl.BlockSpec((tm, tn), lambda i, j, kk: (i, j))
    dims = ((0,) if ta else (1,), (1,) if tb else (0,))
    n_out = 2 if mode == "relu2" else 1

    def body(*refs):
        if mode == "drelu2":
            a_ref, b_ref, u_ref = refs[:3]
            rest = refs[3:]
        else:
            a_ref, b_ref = refs[:2]
            u_ref = None
            rest = refs[2:]
        outs = rest[:n_out]
        part = _dot(a_ref[...], b_ref[...], dims)

        def finish(r):
            if mode == "relu2":
                outs[0][...] = r.astype(BF16)
                rr = jnp.maximum(r, 0.0)
                outs[1][...] = (rr * rr).astype(BF16)
            elif mode == "drelu2":
                outs[0][...] = (r * (2.0 * jnp.maximum(u_ref[...].astype(F32), 0.0))).astype(out_dtype)
            else:
                outs[0][...] = r.astype(out_dtype)

        if nk == 1:
            finish(part)
        else:
            acc = rest[n_out]
            kk = pl.program_id(2)

            @pl.when(kk == 0)
            def _():
                acc[...] = part

            @pl.when(kk > 0)
            def _():
                acc[...] += part

            @pl.when(kk == nk - 1)
            def _():
                finish(acc[...])

    in_specs = [a_spec, b_spec]
    args = [a, b]
    if mode == "drelu2":
        in_specs.append(o_spec)
        args.append(u)
    if mode == "relu2":
        out_shape = [jax.ShapeDtypeStruct((m, n), BF16), jax.ShapeDtypeStruct((m, n), BF16)]
    elif shard_out:
        out_shape = [jax.ShapeDtypeStruct((n // tn, m, tn), out_dtype)]
        o_spec = pl.BlockSpec((None, tm, tn), lambda i, j, kk: (j, i, 0))
    else:
        out_shape = [jax.ShapeDtypeStruct((m, n), out_dtype)]
    outs, comm_outs = _pcall(
        body, args, name=name, grid=(m // tm, n // tn, nk), in_specs=in_specs, out_specs=[o_spec] * n_out,
        out_shape=out_shape, scratch_shapes=[pltpu.VMEM((tm, tn), F32)] if nk > 1 else [],
        sem=("parallel", "parallel", "arbitrary"), comm=comm)
    res = tuple(outs) if mode == "relu2" else outs[0]
    return res if comm is None else (res, comm_outs)


def _pieces_t_matmul(groups, b, *, tm, name):
    k, n = b.shape
    pieces = [p for g in groups for p in g]
    starts, tiles = [], 0
    for p in pieces:
        assert p.shape[0] == k and p.shape[1] % tm == 0, (name, p.shape)
        starts.append(tiles)
        tiles += p.shape[1] // tm
    group_of, group_start, group_tiles = [], [], []
    for gi, g in enumerate(groups):
        group_start.append(starts[len(group_of)])
        group_of += [gi] * len(g)
        group_tiles.append(sum(p.shape[1] // tm for p in g))

    def clipped(block, start, count):
        return pl.BlockSpec(block, (lambda i: (0, jnp.clip(i - start, 0, count - 1))) if block[0] == k
                            else (lambda i: (jnp.clip(i - start, 0, count - 1), 0)))

    def body(*refs):
        a_refs, b_ref, o_refs = refs[:len(pieces)], refs[len(pieces)], refs[len(pieces) + 1:]
        i = pl.program_id(0)
        for a_ref, start, p, gi in zip(a_refs, starts, pieces, group_of):
            @pl.when((i >= start) & (i < start + p.shape[1] // tm))
            def _(a_ref=a_ref, o_ref=o_refs[gi]):
                o_ref[...] = _dot(a_ref[...], b_ref[...], TN).astype(BF16)

    return pl.pallas_call(
        body, name=name, grid=(tiles,),
        in_specs=[clipped((k, tm), s0, p.shape[1] // tm) for s0, p in zip(starts, pieces)] + [pl.BlockSpec((k, n), lambda i: (0, 0))],
        out_specs=[clipped((tm, n), s0, cnt) for s0, cnt in zip(group_start, group_tiles)],
        out_shape=[jax.ShapeDtypeStruct((cnt * tm, n), BF16) for cnt in group_tiles],
        compiler_params=_cparams(("arbitrary",)))(*pieces, b)


def _rms_mod(xv, nw, scale, shift):
    r = lax.rsqrt(jnp.mean(xv * xv, axis=-1, keepdims=True) + EPS)
    return ((xv * r) * nw * (1.0 + scale) + shift).astype(BF16)


def _norm_mod_fwd(x, nw, scale, shift, name):
    s, d = x.shape
    row = pl.BlockSpec((ROW_TILE, d), lambda i: (i, 0))
    vec = pl.BlockSpec((1, d), lambda i: (0, 0))

    def body(x_ref, nw_ref, sc_ref, sh_ref, h_ref):
        h_ref[...] = _rms_mod(x_ref[...], nw_ref[...], sc_ref[...], sh_ref[...])

    return pl.pallas_call(body, name=name, grid=(s // ROW_TILE,), in_specs=[row, vec, vec, vec], out_specs=row,
                          out_shape=jax.ShapeDtypeStruct((s, d), BF16), compiler_params=_cparams(("parallel",)))(x, nw, scale, shift)


def _matmul_rows(a, b, epilogue, row_in, vec_in, outs, *, tb=False, tm, name, comm=None):
    pieces = a if isinstance(a, list) else [(a, 0)]
    assert not (tb and len(pieces) > 1)
    m = pieces[0][0].shape[0]
    n = b.shape[0] if tb else b.shape[1]
    assert m % tm == 0, (name, m, tm)
    dims = ((1,), (1,) if tb else (0,))
    n_a, n_row, n_vec = len(pieces), len(row_in), len(vec_in)

    def body(*refs):
        a_refs, b_ref, rest = refs[:n_a], refs[n_a], refs[n_a + 1:]
        if n_a == 1:
            c = _dot(a_refs[0][...], b_ref[...], dims)
        else:
            c = None
            for a_ref, (piece, off) in zip(a_refs, pieces):
                part = _dot(a_ref[...], b_ref[off:off + piece.shape[1], :], dims)
                c = part if c is None else c + part
        epilogue(c, pl.program_id(0) == 0, rest[:n_row], rest[n_row:n_row + n_vec], rest[n_row + n_vec:])

    def spec(kind, width):
        block = {"row": (tm, width), "vec": (1, width), "one": (1, 1)}[kind]
        return pl.BlockSpec(block, (lambda i: (i, 0)) if kind == "row" else (lambda i: (0, 0)))

    def shape(kind, width):
        return {"row": (m, width), "vec": (1, width), "one": (1, 1)}[kind]

    outs = [(o[0], o[1], o[2] if len(o) > 2 else n) for o in outs]
    res, comm_outs = _pcall(
        body, [*[p for p, _ in pieces], b, *row_in, *vec_in], name=name, grid=(m // tm,),
        in_specs=[spec("row", p.shape[1]) for p, _ in pieces] + [pl.BlockSpec(b.shape, lambda i: (0, 0))]
        + [spec("row", r.shape[1]) for r in row_in] + [spec("vec", v.shape[1]) for v in vec_in],
        out_specs=[spec(kind, width) for kind, _, width in outs],
        out_shape=[jax.ShapeDtypeStruct(shape(kind, width), dt) for kind, dt, width in outs],
        sem=("arbitrary",), comm=comm)
    return res if comm is None else (res, comm_outs)


def _residual_norm_epilogue(mix, first, rows, vecs, outs):
    (x_ref,), (gate_ref, nw_ref, sc_ref, sh_ref), (mix_ref, x1_ref, h_ref) = rows, vecs, outs
    xv = x_ref[...] + gate_ref[...] * mix
    mix_ref[...] = mix
    x1_ref[...] = xv
    h_ref[...] = _rms_mod(xv, nw_ref[...], sc_ref[...], sh_ref[...])


def _loss_epilogue(ff, first, rows, vecs, outs):
    (x1_ref, t_ref), (g_ref,), (loss_ref, dout_ref, dff_ref, dg_ref) = rows, vecs, outs
    d = ff.shape[1]

    @pl.when(first)
    def _():
        loss_ref[...] = jnp.zeros_like(loss_ref)
        dg_ref[...] = jnp.zeros_like(dg_ref)

    err = x1_ref[...] + g_ref[...] * ff - t_ref[...]
    loss_ref[...] += (0.5 / d) * jnp.sum(err * err).reshape(1, 1)
    dout = err * (1.0 / d)
    dout_ref[...] = dout
    dff_ref[...] = (g_ref[...] * dout).astype(BF16)
    dg_ref[...] += jnp.sum(dout * ff, axis=0, keepdims=True)


def _norm_bwd_epilogue(dh, first, rows, vecs, outs):
    with_gate = len(vecs) == 3
    x_ref, dres_ref = rows[:2]
    nw_ref, sc_ref = vecs[:2]
    dx_ref, dsh_ref, dsc_ref, dnw_ref = outs[:4]

    @pl.when(first)
    def _():
        for ref in outs[1:4] + outs[5:]:
            ref[...] = jnp.zeros_like(ref)

    xv = x_ref[...]
    r = lax.rsqrt(jnp.mean(xv * xv, axis=-1, keepdims=True) + EPS)
    nrm = xv * r
    one_sc = 1.0 + sc_ref[...]
    dhn = dh * nrm
    dsh_ref[...] += jnp.sum(dh, axis=0, keepdims=True)
    dsc_ref[...] += jnp.sum(dhn, axis=0, keepdims=True) * nw_ref[...]
    dnw_ref[...] += jnp.sum(dhn, axis=0, keepdims=True) * one_sc
    dn = dh * (nw_ref[...] * one_sc)
    dx = dres_ref[...] + r * (dn - nrm * jnp.mean(dn * nrm, axis=-1, keepdims=True))
    dx_ref[...] = dx
    if with_gate:
        outs[4][...] = (vecs[2][...] * dx).astype(BF16)
        outs[5][...] += jnp.sum(dx * rows[2][...], axis=0, keepdims=True)


CONV_COLS = 256
CONV_FWD_ROWS = 2048
CONV_BWD_ROWS = 1024
CONV_SUB_ROWS = 128
HALO = 8


def _shift_down(cur, halo, k):
    if k == 0:
        return cur
    rolled = pltpu.roll(cur, k, axis=0)
    top = jnp.where(lax.broadcasted_iota(jnp.int32, halo.shape, 0) < k, pltpu.roll(halo, k, axis=0), rolled[:HALO])
    return jnp.concatenate([top, rolled[HALO:]], axis=0)


def _shift_up(cur, halo, k):
    if k == 0:
        return cur
    t = cur.shape[0]
    rolled = pltpu.roll(cur, t - k, axis=0)
    bot = jnp.where(lax.broadcasted_iota(jnp.int32, halo.shape, 0) >= HALO - k, pltpu.roll(halo, HALO - k, axis=0),
                    rolled[t - HALO:])
    return jnp.concatenate([rolled[:t - HALO], bot], axis=0)


def _conv_fwd(proj, conv_w, conv_b):
    s = proj.shape[0]
    nr = s // CONV_FWD_ROWS
    cb0 = OFF_XBC // CONV_COLS
    hb = CONV_FWD_ROWS // HALO
    cur = pl.BlockSpec((CONV_FWD_ROWS, CONV_COLS), lambda j, r: (r, cb0 + j))
    prev = pl.BlockSpec((HALO, CONV_COLS), lambda j, r: (jnp.maximum(r * hb - 1, 0), cb0 + j))
    out = pl.BlockSpec((CONV_FWD_ROWS, CONV_COLS), lambda j, r: (r, j))

    def body(u_ref, up_ref, w_ref, b_ref, pre_ref, act_ref):
        r = pl.program_id(1)
        for c in range(CONV_FWD_ROWS // CONV_SUB_ROWS):
            rows = slice(c * CONV_SUB_ROWS, (c + 1) * CONV_SUB_ROWS)
            u = u_ref[rows, :]
            halo = u_ref[c * CONV_SUB_ROWS - HALO:c * CONV_SUB_ROWS, :] if c > 0 else jnp.where(r > 0, up_ref[...], 0.0)
            acc = b_ref[...] + w_ref[CONV_K - 1:CONV_K, :] * u
            for k in range(1, CONV_K):
                acc = acc + w_ref[CONV_K - 1 - k:CONV_K - k, :] * _shift_down(u, halo, k)
            pre_ref[rows, :] = acc
            act_ref[rows, :] = acc * _sigmoid(acc)

    return pl.pallas_call(
        body, name="conv_fwd", grid=(CONV_CH // CONV_COLS, nr),
        in_specs=[cur, prev, pl.BlockSpec((CONV_K, CONV_COLS), lambda j, r: (0, j)),
                  pl.BlockSpec((1, CONV_COLS), lambda j, r: (0, j))],
        out_specs=[out, out],
        out_shape=[jax.ShapeDtypeStruct((s, CONV_CH), F32), jax.ShapeDtypeStruct((s, CONV_CH), F32)],
        compiler_params=_cparams(("parallel", "arbitrary")))(proj, proj, conv_w, conv_b)


def _conv_bwd(dact, pre, proj, conv_w):
    s = proj.shape[0]
    nr = s // CONV_BWD_ROWS
    cb0 = OFF_XBC // CONV_COLS
    hb = CONV_BWD_ROWS // HALO
    last_halo = s // HALO - 1
    n_sub = CONV_BWD_ROWS // CONV_SUB_ROWS
    cur = pl.BlockSpec((CONV_BWD_ROWS, CONV_COLS), lambda j, r: (r, j))
    nxt = pl.BlockSpec((HALO, CONV_COLS), lambda j, r: (jnp.minimum((r + 1) * hb, last_halo), j))
    ucur = pl.BlockSpec((CONV_BWD_ROWS, CONV_COLS), lambda j, r: (r, cb0 + j))
    wspec = pl.BlockSpec((CONV_K, CONV_COLS), lambda j, r: (0, j))
    bspec = pl.BlockSpec((1, CONV_COLS), lambda j, r: (0, j))

    def dsilu(p):
        sg = _sigmoid(p)
        return sg * (1.0 + p * (1.0 - sg))

    def body(da_ref, dan_ref, pre_ref, pren_ref, u_ref, w_ref, du_ref, dw_ref, db_ref):
        r = pl.program_id(1)

        @pl.when(r == 0)
        def _():
            dw_ref[...] = jnp.zeros_like(dw_ref)
            db_ref[...] = jnp.zeros_like(db_ref)

        dws = [jnp.zeros((1, CONV_COLS), F32) for _ in range(CONV_K)]
        db = jnp.zeros((1, CONV_COLS), F32)
        for c in range(n_sub):
            rows = slice(c * CONV_SUB_ROWS, (c + 1) * CONV_SUB_ROWS)
            ahead = slice((c + 1) * CONV_SUB_ROWS, (c + 1) * CONV_SUB_ROWS + HALO)
            dpre = da_ref[rows, :] * dsilu(pre_ref[rows, :])
            if c < n_sub - 1:
                dnext = da_ref[ahead, :] * dsilu(pre_ref[ahead, :])
            else:
                dnext = jnp.where(r < nr - 1, dan_ref[...] * dsilu(pren_ref[...]), 0.0)
            u = u_ref[rows, :]
            du = w_ref[CONV_K - 1:CONV_K, :] * dpre
            dws[0] = dws[0] + jnp.sum(dpre * u, axis=0, keepdims=True)
            for k in range(1, CONV_K):
                ahead_k = _shift_up(dpre, dnext, k)
                du = du + w_ref[CONV_K - 1 - k:CONV_K - k, :] * ahead_k
                dws[k] = dws[k] + jnp.sum(ahead_k * u, axis=0, keepdims=True)
            du_ref[rows, :] = du.astype(BF16)
            db = db + jnp.sum(dpre, axis=0, keepdims=True)
        dw_ref[...] += jnp.concatenate(dws[::-1], axis=0)
        db_ref[...] += db

    return pl.pallas_call(
        body, name="conv_bwd", grid=(CONV_CH // CONV_COLS, nr),
        in_specs=[cur, nxt, cur, nxt, ucur, wspec],
        out_specs=[cur, wspec, bspec],
        out_shape=[jax.ShapeDtypeStruct((s, CONV_CH), BF16), jax.ShapeDtypeStruct((CONV_K, CONV_CH), F32),
                   jax.ShapeDtypeStruct((1, CONV_CH), F32)],
        compiler_params=_cparams(("parallel", "arbitrary")))(dact, dact, pre, pre, proj, conv_w)


def _ssd_common(dtr, dtb, alog):
    lane = lax.broadcasted_iota(jnp.int32, (1, LANE), 1)
    head_lane = lane < SSD_HEADS
    dt = jnp.where(head_lane, _softplus(dtr + dtb), 0.0)
    a = jnp.where(head_lane, -jnp.exp(alog), 0.0)
    row = lax.broadcasted_iota(jnp.int32, (SSD_CHUNK, SSD_CHUNK), 0)
    col = lax.broadcasted_iota(jnp.int32, (SSD_CHUNK, SSD_CHUNK), 1)
    tril = row >= col
    cs = _dot(tril.astype(F32), dt * a, NN, precision=HIGHEST)
    return dt, a, cs, cs.T, tril, lane


def _split_bf16(v, passes):
    terms, rest = [], v
    for _ in range(passes):
        t = rest.astype(BF16)
        terms.append(t)
        rest = rest - t.astype(F32)
    return terms


def _dot_split(v, m, dims, passes):
    terms = _split_bf16(v, passes)
    if passes == 1:
        return _dot(terms[0], m, dims)
    return _dot(jnp.concatenate(terms, axis=1), jnp.concatenate([m] * passes, axis=0 if dims == NN else 1), dims)


def _ssd_constants():
    heads = jnp.arange(LANE)[:, None]
    exp_mat = (heads == (jnp.arange(SSD_D_INNER)[None, :] // HEAD_DIM)).astype(BF16)
    ind4 = ((jnp.arange(SSD_HEADS * SSD_CHUNK)[:, None] // SSD_CHUNK) == jnp.arange(LANE)[None, :]).astype(BF16)
    return exp_mat, ind4


def _expand_heads(v):
    return jnp.repeat(v[:, :SSD_HEADS], HEAD_DIM, axis=1)


def _ssd_prep(dtr, dtb, alog, exp_mat):
    dt, a, cs, cst, tril, lane = _ssd_common(dtr, dtb, alog)
    return dt, a, cs, cst, tril, lane, _dot_split(dt, exp_mat, NN, 2), _dot_split(cs, exp_mat, NN, 3)


def _chunk_decay_rows(cs, g):
    parts = []
    for e in range(HEADS_PER_GROUP):
        h = g * HEADS_PER_GROUP + e
        parts.append(jnp.broadcast_to(jnp.exp(cs[SSD_CHUNK - 1:SSD_CHUNK, h:h + 1]), (HEAD_DIM, SSD_STATE)))
    return jnp.concatenate(parts, axis=0)


def _ssd_fwd(proj, act, dtb, alog, dsk, nw):
    s = proj.shape[0]
    nc = s // SSD_CHUNK
    bc_w = SSD_GROUPS * SSD_STATE
    exp_mat, _ = _ssd_constants()

    def body(z_ref, dtr_ref, xs_ref, b_ref, c_ref, dtb_ref, alog_ref, dskx_ref, nw_ref, exp_ref,
             ypre_ref, yssd_ref, hall_ref, h_scr):
        @pl.when(pl.program_id(0) == 0)
        def _():
            h_scr[...] = jnp.zeros_like(h_scr)

        dt, a, cs, cst, tril, lane, dtx, csx = _ssd_prep(dtr_ref[...], dtb_ref[...], alog_ref[...], exp_ref[...])
        cs_last_x = csx[SSD_CHUNK - 1:SSD_CHUNK, :]
        xs = xs_ref[...]
        xdt = xs * dtx
        xdtb = xdt.astype(BF16)
        xdec = (xdt * jnp.exp(cs_last_x - csx)).astype(BF16)
        ecsx = jnp.exp(csx)
        head_of_lane = lax.broadcasted_iota(jnp.int32, (1, GROUP_WIDTH), 1) // HEAD_DIM
        for g in range(SSD_GROUPS):
            gs = slice(g * GROUP_WIDTH, (g + 1) * GROUP_WIDTH)
            bg = b_ref[:, g * SSD_STATE:(g + 1) * SSD_STATE].astype(BF16)
            cg = c_ref[:, g * SSD_STATE:(g + 1) * SSD_STATE].astype(BF16)
            cb = _dot(cg, bg, NT)
            hprev = h_scr[gs, :]
            hall_ref[0, gs, :] = hprev
            gms, rhs = [], []
            xg = xdtb[:, gs]
            for e in range(HEADS_PER_GROUP):
                h = g * HEADS_PER_GROUP + e
                lm = jnp.exp(jnp.where(tril, cs[:, h:h + 1] - cst[h:h + 1, :], -1e30))
                gms.append((cb * lm).astype(BF16))
                rhs.append(jnp.where(head_of_lane == e, xg, jnp.zeros_like(xg)))
            y = _dot(jnp.concatenate(gms, axis=1), jnp.concatenate(rhs, axis=0), NN)
            y = y + ecsx[:, gs] * _dot(cg, hprev.astype(BF16), NT)
            y = y + dskx_ref[:, gs] * xs[:, gs]
            h_scr[gs, :] = hprev * _chunk_decay_rows(cs, g) + _dot(xdec[:, gs], bg, TN)
            ypre_ref[:, gs] = y
            z = z_ref[:, gs]
            yg = y * (z * _sigmoid(z))
            r = lax.rsqrt(jnp.mean(yg * yg, axis=-1, keepdims=True) + EPS)
            yssd_ref[:, gs] = (yg * r * nw_ref[:, gs]).astype(BF16)

    row_d = lambda cb: pl.BlockSpec((SSD_CHUNK, SSD_D_INNER), lambda c: (c, cb))
    small = pl.BlockSpec((1, LANE), lambda c: (0, 0))
    wide = pl.BlockSpec((1, SSD_D_INNER), lambda c: (0, 0))
    return pl.pallas_call(
        body, name="ssd_fwd", grid=(nc,),
        in_specs=[row_d(OFF_Z // SSD_D_INNER),
                  pl.BlockSpec((SSD_CHUNK, LANE), lambda c: (c, OFF_DT // LANE)),
                  row_d(0),
                  pl.BlockSpec((SSD_CHUNK, bc_w), lambda c: (c, SSD_D_INNER // bc_w)),
                  pl.BlockSpec((SSD_CHUNK, bc_w), lambda c: (c, SSD_D_INNER // bc_w + 1)),
                  small, small, wide, wide, pl.BlockSpec((LANE, SSD_D_INNER), lambda c: (0, 0))],
        out_specs=[row_d(0), row_d(0), pl.BlockSpec((1, SSD_D_INNER, SSD_STATE), lambda c: (c, 0, 0))],
        out_shape=[jax.ShapeDtypeStruct((s, SSD_D_INNER), F32), jax.ShapeDtypeStruct((s, SSD_D_INNER + ATT_D), BF16),
                   jax.ShapeDtypeStruct((nc, SSD_D_INNER, SSD_STATE), F32)],
        scratch_shapes=[pltpu.VMEM((SSD_D_INNER, SSD_STATE), F32)],
        compiler_params=_cparams(("arbitrary",)))(proj, proj, act, act, act, dtb, alog, _expand_heads(dsk), nw, exp_mat)


def _ssd_bwd(dycat, ypre, proj, act, hall, dtb, alog, dsk, nw, comm=None):
    s = proj.shape[0]
    nc = s // SSD_CHUNK
    bc_w = SSD_GROUPS * SSD_STATE

    exp_mat, ind4 = _ssd_constants()
    seg_passes = 1

    def body(dy_ref, ypre_ref, z_ref, dtr_ref, xs_ref, b_ref, c_ref, hall_ref, dtb_ref, alog_ref, dskx_ref, nw_ref,
             exp_ref, ind4_ref, dz_ref, dact_ref, ddtr_ref, da_ref, ddsk_ref, ddtb_ref, dnw_ref, dh_scr):
        @pl.when(pl.program_id(0) == 0)
        def _():
            dh_scr[...] = jnp.zeros_like(dh_scr)
            da_ref[...] = jnp.zeros_like(da_ref)
            ddsk_ref[...] = jnp.zeros_like(ddsk_ref)
            ddtb_ref[...] = jnp.zeros_like(ddtb_ref)
            dnw_ref[...] = jnp.zeros_like(dnw_ref)

        dtr = dtr_ref[...]
        dt, a, cs, cst, tril, lane, dtx, csx = _ssd_prep(dtr, dtb_ref[...], alog_ref[...], exp_ref[...])
        cs_last_x = csx[SSD_CHUNK - 1:SSD_CHUNK, :]
        xs = xs_ref[...]
        xdt = xs * dtx
        xdtb = xdt.astype(BF16)
        decx = jnp.exp(cs_last_x - csx)
        xdecf = xdt * decx
        xdec = xdecf.astype(BF16)
        ecsx = jnp.exp(csx)
        head_of_lane = lax.broadcasted_iota(jnp.int32, (1, GROUP_WIDTH), 1) // HEAD_DIM
        last_row = lax.broadcasted_iota(jnp.int32, (SSD_CHUNK, 1), 0) == SSD_CHUNK - 1
        dcs_col = jnp.zeros((SSD_CHUNK, LANE), F32)
        dcs_row = jnp.zeros((SSD_CHUNK, LANE), F32)
        ddt = jnp.zeros((SSD_CHUNK, LANE), F32)
        ddsk = jnp.zeros((1, LANE), F32)
        hsum = jnp.zeros((1, LANE), F32)
        t1_sum = jnp.zeros((1, LANE), F32)
        for g in range(SSD_GROUPS):
            gs = slice(g * GROUP_WIDTH, (g + 1) * GROUP_WIDTH)
            bsl = slice(g * SSD_STATE, (g + 1) * SSD_STATE)
            exp_g = exp_ref[:, gs]
            ind4_g = ind4_ref[g * HEADS_PER_GROUP * SSD_CHUNK:(g + 1) * HEADS_PER_GROUP * SSD_CHUNK, :]
            z = z_ref[:, gs]
            sg = _sigmoid(z)
            sz = z * sg
            ypre = ypre_ref[:, gs]
            yg = ypre * sz
            r = lax.rsqrt(jnp.mean(yg * yg, axis=-1, keepdims=True) + EPS)
            nrm = yg * r
            dyo_n = dy_ref[:, gs]
            dnw_ref[:, gs] += jnp.sum(dyo_n * nrm, axis=0, keepdims=True)
            dn = dyo_n * nw_ref[:, gs]
            dyg = r * (dn - nrm * jnp.mean(dn * nrm, axis=-1, keepdims=True))
            dz_ref[:, gs] = (dyg * ypre * (sg * (1.0 + z * (1.0 - sg)))).astype(BF16)
            dy = dyg * sz

            bg = b_ref[:, bsl].astype(BF16)
            cg = c_ref[:, bsl].astype(BF16)
            cb = _dot(cg, bg, NT)
            hprev = hall_ref[0, gs, :]
            hb = hprev.astype(BF16)
            dhn = dh_scr[gs, :]
            dhb = dhn.astype(BF16)
            xs_g, xdt_g = xs[:, gs], xdtb[:, gs]
            w_off = _dot(cg, hb, NT)
            dyo = dy * ecsx[:, gs]
            dyob = dyo.astype(BF16)
            dcg = _dot(dyob, hb, NN)
            dh_y = _dot(dyob, cg, TN)
            r_st = _dot(bg, dhb, NT)
            dbg = _dot(xdec[:, gs], dhb, NN)
            dyb = dy.astype(BF16)
            gms, gmbs, lms, dys = [], [], [], []
            for e in range(HEADS_PER_GROUP):
                h = g * HEADS_PER_GROUP + e
                lm = jnp.exp(jnp.where(tril, cs[:, h:h + 1] - cst[h:h + 1, :], -1e30))
                gm = cb * lm
                lms.append(lm)
                gms.append(gm)
                gmbs.append(gm.astype(BF16))
                dys.append(jnp.where(head_of_lane == e, dyb, jnp.zeros_like(dyb)))
            dxdt = _dot(jnp.concatenate(gmbs, axis=0), jnp.concatenate(dys, axis=0), TN) + decx[:, gs] * r_st
            dcb = jnp.zeros((SSD_CHUNK, SSD_CHUNK), F32)
            mms = []
            for e in range(HEADS_PER_GROUP):
                dg = _dot(dys[e], xdt_g, NT)
                mms.append(dg * gms[e])
                dcb = dcb + dg * lms[e]
            seg = _dot_split(jnp.concatenate([dyo * w_off, xdecf[:, gs] * r_st, dxdt * xs_g, dy * xs_g], axis=0), exp_g, NT, seg_passes)
            v1, t1, ddt_g, dsk_g = [seg[i * SSD_CHUNK:(i + 1) * SSD_CHUNK] for i in range(4)]
            dcs_col = dcs_col + v1 - t1 + _dot_split(jnp.concatenate(mms, axis=1), ind4_g, NN, seg_passes)
            for t in _split_bf16(jnp.concatenate(mms, axis=0), seg_passes):
                dcs_row = dcs_row + _dot(ind4_g, t, TN)
            ddt = ddt + ddt_g
            ddsk = ddsk + jnp.sum(dsk_g, axis=0, keepdims=True)
            t1_sum = t1_sum + jnp.sum(t1, axis=0, keepdims=True)
            for e in range(HEADS_PER_GROUP):
                h = g * HEADS_PER_GROUP + e
                hs = slice(e * HEAD_DIM, (e + 1) * HEAD_DIM)
                hsum = hsum + jnp.where(lane == h, jnp.sum(dhn[hs, :] * hprev[hs, :]).reshape(1, 1), 0.0)
            dh_scr[gs, :] = dhn * _chunk_decay_rows(cs, g) + dh_y
            dcbb = dcb.astype(BF16)
            dact_ref[:, gs] = dxdt * dtx[:, gs] + dskx_ref[:, gs] * dy
            dact_ref[:, SSD_D_INNER + g * SSD_STATE:SSD_D_INNER + (g + 1) * SSD_STATE] = dbg + _dot(dcbb, cg, TN)
            dact_ref[:, SSD_D_INNER + bc_w + g * SSD_STATE:SSD_D_INNER + bc_w + (g + 1) * SSD_STATE] = dcg + _dot(dcbb, bg, NN)
        dlast = t1_sum + jnp.exp(cs[SSD_CHUNK - 1:SSD_CHUNK, :]) * hsum
        dcs = dcs_col - dcs_row.T + jnp.where(last_row, dlast, 0.0)
        row = lax.broadcasted_iota(jnp.int32, (SSD_CHUNK, SSD_CHUNK), 0)
        col = lax.broadcasted_iota(jnp.int32, (SSD_CHUNK, SSD_CHUNK), 1)
        dda = _dot((col >= row).astype(F32), dcs, NN, precision=HIGHEST)
        ddt = ddt + dda * a
        da_ref[...] += jnp.sum(dda * dt, axis=0, keepdims=True)
        ddtr = jnp.where(lane < SSD_HEADS, ddt * _sigmoid(dtr + dtb_ref[...]), 0.0)
        ddtr_ref[...] = ddtr.astype(BF16)
        ddtb_ref[...] += jnp.sum(ddtr, axis=0, keepdims=True)
        ddsk_ref[...] += ddsk

    rev = lambda c: nc - 1 - c
    row_d = lambda cb: pl.BlockSpec((SSD_CHUNK, SSD_D_INNER), lambda c: (rev(c), cb))
    small = pl.BlockSpec((1, LANE), lambda c: (0, 0))
    wide = pl.BlockSpec((1, SSD_D_INNER), lambda c: (0, 0))
    small_shape = jax.ShapeDtypeStruct((1, LANE), F32)
    return _pcall(
        body, (dycat, ypre, proj, proj, act, act, act, hall, dtb, alog, _expand_heads(dsk), nw, exp_mat, ind4),
        name="ssd_bwd", grid=(nc,),
        in_specs=[row_d(0), row_d(0), row_d(OFF_Z // SSD_D_INNER),
                  pl.BlockSpec((SSD_CHUNK, LANE), lambda c: (rev(c), OFF_DT // LANE)),
                  row_d(0),
                  pl.BlockSpec((SSD_CHUNK, bc_w), lambda c: (rev(c), SSD_D_INNER // bc_w)),
                  pl.BlockSpec((SSD_CHUNK, bc_w), lambda c: (rev(c), SSD_D_INNER // bc_w + 1)),
                  pl.BlockSpec((1, SSD_D_INNER, SSD_STATE), lambda c: (rev(c), 0, 0)),
                  small, small, wide, wide, pl.BlockSpec((LANE, SSD_D_INNER), lambda c: (0, 0)),
                  pl.BlockSpec((SSD_HEADS * SSD_CHUNK, LANE), lambda c: (0, 0))],
        out_specs=[row_d(0), pl.BlockSpec((SSD_CHUNK, CONV_CH), lambda c: (rev(c), 0)),
                   pl.BlockSpec((SSD_CHUNK, LANE), lambda c: (rev(c), 0)), small, small, small, wide],
        out_shape=[jax.ShapeDtypeStruct((s, SSD_D_INNER), BF16), jax.ShapeDtypeStruct((s, CONV_CH), F32),
                   jax.ShapeDtypeStruct((s, LANE), BF16), small_shape, small_shape, small_shape,
                   jax.ShapeDtypeStruct((1, SSD_D_INNER), F32)],
        scratch_shapes=[pltpu.VMEM((SSD_D_INNER, SSD_STATE), F32)], sem=("arbitrary",), comm=comm)


def _head_mean_matrix():
    row = lax.broadcasted_iota(jnp.int32, (LANE, LANE), 0) // HEAD_DIM
    col = lax.broadcasted_iota(jnp.int32, (LANE, LANE), 1) // HEAD_DIM
    return (row == col).astype(F32)


def _head_sum2(v, ones_bd):
    hi = v.astype(BF16)
    lo = (v - hi.astype(F32)).astype(BF16)
    return _dot(jnp.concatenate([hi, lo], axis=1), jnp.concatenate([ones_bd, ones_bd], axis=0), NN)


def _head_norms(xs, ws, ones_bd):
    sums = [_head_sum2(x * x, ones_bd) for x in xs]
    rs = [lax.rsqrt(ms * (1.0 / HEAD_DIM) + EPS) for ms in sums]
    return [(x * r) * w for x, r, w in zip(xs, rs, ws)], rs


def _head_norms_bwd(dns, xs, ws, rs, ones_bd):
    nrms = [x * r for x, r in zip(xs, rs)]
    dnws = [dn * w for dn, w in zip(dns, ws)]
    projs = [_head_sum2(dnw * nrm, ones_bd) for dnw, nrm in zip(dnws, nrms)]
    dxs = [r * (dnw - nrm * (pr * (1.0 / HEAD_DIM))) for r, dnw, nrm, pr in zip(rs, dnws, nrms, projs)]
    return dxs, [jnp.sum(dn * nrm, axis=0, keepdims=True) for dn, nrm in zip(dns, nrms)]


NORM_CHUNKS = 4


PRO_ROWS = 256
ATT_GROUP_FWD = 32
ATT_GROUP_BWD = 8
KEYS = 2 * ATT_BLK
NEG = -1e30
HALF = HEAD_DIM // 2


def _rows(start, size, dil):
    return pl.ds(start, size) if dil == 1 else pl.ds(start, size, stride=dil)


def _fill_bias(bias_ref):
    row = lax.broadcasted_iota(jnp.int32, (ATT_BLK, 2 * KEYS), 0)
    col = lax.broadcasted_iota(jnp.int32, (ATT_BLK, 2 * KEYS), 1) & (KEYS - 1)
    for first, off in ((0, 0), (1, ATT_BLK)):
        dist = off + row - col
        bias_ref[first] = jnp.where((dist >= 0) & (dist <= ATT_BLK), 0.0, NEG)


def _pair(a, b):
    return jnp.concatenate([jnp.broadcast_to(a, (ATT_BLK, KEYS)), jnp.broadcast_to(b, (ATT_BLK, KEYS))], axis=1)


def _split_heads(x, is_a):
    zero = jnp.zeros_like(x)
    return jnp.concatenate([jnp.where(is_a, x, zero), jnp.where(is_a, zero, x)], axis=0)


def _block_ids(b, nb):
    i = b & (nb - 1)
    q0 = pl.multiple_of(b * ATT_BLK, ATT_BLK)
    k0 = pl.multiple_of((b - jnp.minimum(i, 1)) * ATT_BLK, ATT_BLK)
    return pl.ds(q0, ATT_BLK), pl.ds(k0, KEYS), jnp.minimum(i, 1)


def _natural_rows(b, nb, dil):
    if dil == 1:
        return pl.ds(pl.multiple_of(b * ATT_BLK, ATT_BLK), ATT_BLK)
    return pl.ds(b // nb + dil * ((b & (nb - 1)) * ATT_BLK), ATT_BLK, stride=dil)


def _att_fwd(proj, qw, kw, comm=None):
    s = proj.shape[0]
    nblk = s // ATT_BLK
    assert all((s // d) // ATT_BLK >= 2 for d in DILATIONS)
    blk = lambda off: pl.BlockSpec((s, LANE), lambda i: (0, off // LANE + i))
    wspec = pl.BlockSpec((1, LANE), lambda i: (0, i))
    oblk = pl.BlockSpec((s, LANE), lambda i: (0, i))

    def body(q_ref, k_ref, v_ref, qw_ref, kw_ref, o_ref, lse_ref, qn, kn, q_cm, k_cm, v_cm, m_acc, l_acc, o_d, m_d, l_d,
             o_e, m_e, l_e, tq, tk, tv, bias):
        ones_bd = _head_mean_matrix().astype(BF16)
        is_a = lax.broadcasted_iota(jnp.int32, (1, LANE), 1) < HEAD_DIM
        ones_ext = _split_heads(jnp.ones((KEYS, LANE), BF16), is_a)
        _fill_bias(bias)

        def pro(j, c):
            chunks = [pl.ds(pl.multiple_of((NORM_CHUNKS * j + u) * PRO_ROWS, PRO_ROWS), PRO_ROWS) for u in range(NORM_CHUNKS)]
            normed, _ = _head_norms([q_ref[rows, :] for rows in chunks] + [k_ref[rows, :] for rows in chunks],
                                    [qw_ref[...] * HEAD_DIM ** -0.5] * NORM_CHUNKS + [kw_ref[...]] * NORM_CHUNKS, ones_bd)
            for u, rows in enumerate(chunks):
                qn[rows, :] = normed[u]
                kn[rows, :] = normed[NORM_CHUNKS + u]
            return c

        lax.fori_loop(0, s // (NORM_CHUNKS * PRO_ROWS), pro, 0)

        results = dict(zip(DILATIONS, ((o_ref, m_acc, l_acc), (o_d, m_d, l_d), (o_e, m_e, l_e))))
        for dil in DILATIONS:
            ln = s // dil
            nb = ln // ATT_BLK
            o_out, m_out, l_out = results[dil]
            level = DILATIONS.index(dil)
            keep_f32 = 0 < level < len(DILATIONS) - 1
            from_temps = level >= 2
            step_rows = dil // DILATIONS[level - 1] if from_temps else dil
            for r in range(dil):
                prev = DILATIONS[level - 1] if from_temps else 1
                start = (r % prev) * (s // prev) + r // prev if from_temps else r

                def relayout(j, c, r=r, ln=ln, start=start, step_rows=step_rows, keep_f32=keep_f32, from_temps=from_temps):
                    j0 = pl.multiple_of(j * PRO_ROWS, PRO_ROWS)
                    src = _rows(start + step_rows * j0, PRO_ROWS, step_rows)
                    dst = pl.ds(r * ln + j0, PRO_ROWS)
                    qv, kv, vv = (tq[src, :], tk[src, :], tv[src, :]) if from_temps else (qn[src, :], kn[src, :], v_ref[src, :])
                    q_cm[dst, :] = qv.astype(BF16)
                    k_cm[dst, :] = kv.astype(BF16)
                    v_cm[dst, :] = vv.astype(BF16)
                    if keep_f32:
                        tq[dst, :] = qv
                        tk[dst, :] = kv
                        tv[dst, :] = vv
                    return c

                lax.fori_loop(0, ln // PRO_ROWS, relayout, 0)

            def step(bg, c, nb=nb, o_out=o_out, m_out=m_out, l_out=l_out):
                ids = [_block_ids(bg * ATT_GROUP_FWD + u, nb) for u in range(ATT_GROUP_FWD)]
                kbs = [_split_heads(k_cm[krows, :], is_a) for _, krows, _ in ids]
                scs = [_dot(q_cm[qrows, :], kb, NT) + bias[first] for (qrows, _, first), kb in zip(ids, kbs)]
                mas = [jnp.max(sc[:, :KEYS], axis=-1, keepdims=True) for sc in scs]
                mbs = [jnp.max(sc[:, KEYS:], axis=-1, keepdims=True) for sc in scs]
                ps = [jnp.exp(sc - _pair(ma, mb)).astype(BF16) for sc, ma, mb in zip(scs, mas, mbs)]
                vbs = [jnp.concatenate([_split_heads(v_cm[krows, :], is_a), ones_ext], axis=1) for _, krows, _ in ids]
                ols = [_dot(p, vb, NN) for p, vb in zip(ps, vbs)]
                for (qrows, _, _), ol, ma, mb in zip(ids, ols, mas, mbs):
                    o_out[qrows, :] = ol[:, :LANE]
                    l_out[qrows, :] = ol[:, LANE:]
                    m_out[qrows, :] = jnp.where(is_a, ma, mb)
                return c

            lax.fori_loop(0, nblk // ATT_GROUP_FWD, step, 0)

        for level in range(len(DILATIONS) - 1, 0, -1):
            fine_d, coarse_d = DILATIONS[level - 1], DILATIONS[level]
            ratio, ln_f, ln_c = coarse_d // fine_d, s // fine_d, s // coarse_d
            (o_f, m_f, l_f), (o_c, m_c, l_c) = results[fine_d], results[coarse_d]
            for r in range(coarse_d):
                def merge(j, c, r=r, ratio=ratio, ln_c=ln_c, start=(r % fine_d) * ln_f + r // fine_d,
                          o_f=o_f, m_f=m_f, l_f=l_f, o_c=o_c, m_c=m_c, l_c=l_c):
                    j0 = pl.multiple_of(j * PRO_ROWS, PRO_ROWS)
                    fine = _rows(start + ratio * j0, PRO_ROWS, ratio)
                    coarse = pl.ds(r * ln_c + j0, PRO_ROWS)
                    m_old, m_new = m_f[fine, :], m_c[coarse, :]
                    m = jnp.maximum(m_old, m_new)
                    a_old, a_new = jnp.exp(m_old - m), jnp.exp(m_new - m)
                    o_f[fine, :] = a_old * o_f[fine, :] + a_new * o_c[coarse, :]
                    l_f[fine, :] = a_old * l_f[fine, :] + a_new * l_c[coarse, :]
                    m_f[fine, :] = m
                    return c

                lax.fori_loop(0, ln_c // PRO_ROWS, merge, 0)

        def epi(j, c):
            rows = pl.ds(pl.multiple_of(j * PRO_ROWS, PRO_ROWS), PRO_ROWS)
            l = l_acc[rows, :]
            o_ref[rows, :] = o_ref[rows, :] / l
            lse_ref[rows, :] = m_acc[rows, :] + jnp.log(l)
            return c

        lax.fori_loop(0, s // PRO_ROWS, epi, 0)

    f = jax.ShapeDtypeStruct((s, ATT_D), F32)
    scr = pltpu.VMEM((s, LANE), F32)
    scb = pltpu.VMEM((s, LANE), BF16)
    return _pcall(
        body, (proj, proj, proj, qw, kw), name="att_fwd", grid=(ATT_D // LANE,),
        in_specs=[blk(OFF_Q), blk(OFF_K), blk(OFF_V), wspec, wspec], out_specs=[oblk, oblk], out_shape=[f, f],
        scratch_shapes=[scr, scr, scb, scb, scb] + [scr] * 11 + [pltpu.VMEM((2, ATT_BLK, 2 * KEYS), F32)],
        sem=("parallel",), comm=comm)


def _att_bwd(proj, do, stats, qw, kw, comm=None):
    s = proj.shape[0]
    nblk = s // ATT_BLK
    blk = lambda off: pl.BlockSpec((s, LANE), lambda i: (0, off // LANE + i))
    wspec = pl.BlockSpec((1, LANE), lambda i: (0, i))
    oblk = pl.BlockSpec((s, LANE), lambda i: (0, i))

    def body(q_ref, k_ref, v_ref, do_ref, st_ref, qw_ref, kw_ref, dq_ref, dk_ref, dv_ref, dqw_ref, dkw_ref,
             qn, kn, q_cm, do_cm, k_cm, v_cm, rms, dq_acc, dk_acc, dv_acc, dq_d, dk_d, dv_d, dq_e, dk_e, dv_e, bias):
        ones_bd = _head_mean_matrix().astype(BF16)
        is_a = lax.broadcasted_iota(jnp.int32, (1, LANE), 1) < HEAD_DIM
        first_half = (lax.broadcasted_iota(jnp.int32, (1, LANE), 1) & (HEAD_DIM - 1)) < HALF
        _fill_bias(bias)
        zero = jnp.zeros((PRO_ROWS, LANE), F32)
        results = dict(zip(DILATIONS, ((dq_acc, dk_acc, dv_acc), (dq_d, dk_d, dv_d), (dq_e, dk_e, dv_e))))

        def pro(j, c):
            chunks = [pl.ds(pl.multiple_of((NORM_CHUNKS * j + u) * PRO_ROWS, PRO_ROWS), PRO_ROWS) for u in range(NORM_CHUNKS)]
            normed, rs = _head_norms([q_ref[rows, :] for rows in chunks] + [k_ref[rows, :] for rows in chunks],
                                     [qw_ref[...] * HEAD_DIM ** -0.5] * NORM_CHUNKS + [kw_ref[...]] * NORM_CHUNKS, ones_bd)
            for u, rows in enumerate(chunks):
                qn[rows, :] = normed[u]
                kn[rows, :] = normed[NORM_CHUNKS + u]
                rms[rows, :] = jnp.where(first_half, rs[u], rs[NORM_CHUNKS + u])
                dk_acc[rows, :] = zero
                dv_acc[rows, :] = zero
            return c

        lax.fori_loop(0, s // (NORM_CHUNKS * PRO_ROWS), pro, 0)

        for dil in DILATIONS:
            ln = s // dil
            nb = ln // ATT_BLK
            dq_o, dk_o, dv_o = results[dil]
            level = DILATIONS.index(dil)
            keep_f32 = 0 < level < len(DILATIONS) - 1
            from_temps = level >= 2
            temps = results[DILATIONS[-1]]
            for r in range(dil):
                prev = DILATIONS[level - 1] if from_temps else 1
                start = (r % prev) * (s // prev) + r // prev if from_temps else r

                def relayout(j, c, dil=dil, r=r, ln=ln, start=start, step_rows=dil // prev):
                    j0 = pl.multiple_of(j * PRO_ROWS, PRO_ROWS)
                    nat = _rows(r + dil * j0, PRO_ROWS, dil)
                    src = _rows(start + step_rows * j0, PRO_ROWS, step_rows)
                    dst = pl.ds(r * ln + j0, PRO_ROWS)
                    qv, kv, vv = [t[src, :] for t in temps] if from_temps else (qn[src, :], kn[src, :], v_ref[src, :])
                    q_cm[dst, :] = qv.astype(BF16)
                    k_cm[dst, :] = kv.astype(BF16)
                    v_cm[dst, :] = vv.astype(BF16)
                    do_cm[dst, :] = do_ref[nat, :].astype(BF16)
                    if keep_f32:
                        for t, val in zip(temps, (qv, kv, vv)):
                            t[dst, :] = val
                    return c

                lax.fori_loop(0, ln // PRO_ROWS, relayout, 0)

            if dil > 1:
                def clear(j, c, dk_o=dk_o, dv_o=dv_o):
                    rows = pl.ds(pl.multiple_of(j * PRO_ROWS, PRO_ROWS), PRO_ROWS)
                    dk_o[rows, :] = zero
                    dv_o[rows, :] = zero
                    return c

                lax.fori_loop(0, s // PRO_ROWS, clear, 0)

            def step(bg, c, nb=nb, dil=dil, dq_o=dq_o, dk_o=dk_o, dv_o=dv_o):
                blocks = [bg * ATT_GROUP_BWD + u for u in range(ATT_GROUP_BWD)]
                ids = [_block_ids(b, nb) for b in blocks]
                qbs = [q_cm[qrows, :] for qrows, _, _ in ids]
                dobs = [do_cm[qrows, :] for qrows, _, _ in ids]
                kbs = [_split_heads(k_cm[krows, :], is_a) for _, krows, _ in ids]
                vbs = [_split_heads(v_cm[krows, :], is_a) for _, krows, _ in ids]
                sts = [st_ref[_natural_rows(b, nb, dil), :] for b in blocks]
                scs = [_dot(qb, kb, NT) + bias[first] for qb, kb, (_, _, first) in zip(qbs, kbs, ids)]
                dps = [_dot(dob, vb, NT) for dob, vb in zip(dobs, vbs)]
                ps = [jnp.exp(sc - _pair(st[:, 0:1], st[:, HEAD_DIM:HEAD_DIM + 1])) for sc, st in zip(scs, sts)]
                dss = [(p * (dp - _pair(st[:, HALF:HALF + 1], st[:, HEAD_DIM + HALF:HEAD_DIM + HALF + 1]))).astype(BF16)
                       for p, dp, st in zip(ps, dps, sts)]
                dqs = [_dot(ds, kb, NN) for ds, kb in zip(dss, kbs)]
                dkfs = [_dot(ds, qb, TN) for ds, qb in zip(dss, qbs)]
                dvfs = [_dot(p.astype(BF16), dob, TN) for p, dob in zip(ps, dobs)]
                for (qrows, krows, _), dq, dkf, dvf in zip(ids, dqs, dkfs, dvfs):
                    dq_o[qrows, :] = dq
                    dk_o[krows, :] += jnp.where(is_a, dkf[:KEYS], dkf[KEYS:])
                    dv_o[krows, :] += jnp.where(is_a, dvf[:KEYS], dvf[KEYS:])
                return c

            lax.fori_loop(0, nblk // ATT_GROUP_BWD, step, 0)

        for level in range(len(DILATIONS) - 1, 0, -1):
            fine_d, coarse_d = DILATIONS[level - 1], DILATIONS[level]
            ratio, ln_f, ln_c = coarse_d // fine_d, s // fine_d, s // coarse_d
            for r in range(coarse_d):
                def merge(j, c, r=r, ratio=ratio, ln_c=ln_c, start=(r % fine_d) * ln_f + r // fine_d,
                          fine_bufs=results[fine_d], coarse_bufs=results[coarse_d]):
                    j0 = pl.multiple_of(j * PRO_ROWS, PRO_ROWS)
                    fine = _rows(start + ratio * j0, PRO_ROWS, ratio)
                    coarse = pl.ds(r * ln_c + j0, PRO_ROWS)
                    for f_buf, c_buf in zip(fine_bufs, coarse_bufs):
                        f_buf[fine, :] += c_buf[coarse, :]
                    return c

                lax.fori_loop(0, ln_c // PRO_ROWS, merge, 0)

        def epi(j, c):
            chunks = [pl.ds(pl.multiple_of((NORM_CHUNKS * j + u) * PRO_ROWS, PRO_ROWS), PRO_ROWS) for u in range(NORM_CHUNKS)]
            packed = [rms[rows, :] for rows in chunks]
            rs = ([jnp.where(first_half, p, pltpu.roll(p, HALF, axis=1)) for p in packed]
                  + [jnp.where(first_half, pltpu.roll(p, LANE - HALF, axis=1), p) for p in packed])
            dxs, dws = _head_norms_bwd(
                [dq_acc[rows, :] for rows in chunks] + [dk_acc[rows, :] for rows in chunks],
                [q_ref[rows, :] for rows in chunks] + [k_ref[rows, :] for rows in chunks],
                [qw_ref[...] * HEAD_DIM ** -0.5] * NORM_CHUNKS + [kw_ref[...]] * NORM_CHUNKS, rs, ones_bd)
            dqw, dkw = c
            for u, rows in enumerate(chunks):
                dq_ref[rows, :] = dxs[u].astype(BF16)
                dk_ref[rows, :] = dxs[NORM_CHUNKS + u].astype(BF16)
                dv_ref[rows, :] = dv_acc[rows, :].astype(BF16)
                dqw, dkw = dqw + dws[u], dkw + dws[NORM_CHUNKS + u]
            return dqw, dkw

        zrow = jnp.zeros((1, LANE), F32)
        dqw, dkw = lax.fori_loop(0, s // (NORM_CHUNKS * PRO_ROWS), epi, (zrow, zrow))
        dqw_ref[...] = dqw * HEAD_DIM ** -0.5
        dkw_ref[...] = dkw

    o = jax.ShapeDtypeStruct((s, ATT_D), BF16)
    ov = jax.ShapeDtypeStruct((1, ATT_D), F32)
    scr = pltpu.VMEM((s, LANE), F32)
    scb = pltpu.VMEM((s, LANE), BF16)
    return _pcall(
        body, (proj, proj, proj, do, stats, qw, kw), name="att_bwd", grid=(ATT_D // LANE,),
        in_specs=[blk(OFF_Q), blk(OFF_K), blk(OFF_V), oblk, oblk, wspec, wspec],
        out_specs=[oblk, oblk, oblk, wspec, wspec], out_shape=[o, o, o, ov, ov],
        scratch_shapes=[scr, scr, scb, scb, scb, scb] + [scr] * 10 + [pltpu.VMEM((2, ATT_BLK, 2 * KEYS), F32)],
        sem=("parallel",), comm=comm)


def _att_norm_fwd(o, nw, ycat):
    s = o.shape[0]
    row = pl.BlockSpec((ROW_TILE, ATT_D), lambda i: (i, 0))
    vec = pl.BlockSpec((1, ATT_D), lambda i: (0, 0))

    def body(o_ref, nw_ref, ycat_ref, y_ref):
        o = o_ref[...]
        r = lax.rsqrt(jnp.mean(o * o, axis=-1, keepdims=True) + EPS)
        y_ref[...] = (o * r * nw_ref[...]).astype(BF16)

    return pl.pallas_call(body, name="att_norm_fwd", grid=(s // ROW_TILE,),
                          in_specs=[row, vec, pl.BlockSpec(memory_space=pl.ANY)],
                          out_specs=pl.BlockSpec((ROW_TILE, ATT_D), lambda i: (i, 1)),
                          out_shape=jax.ShapeDtypeStruct(ycat.shape, BF16), input_output_aliases={2: 0},
                          compiler_params=_cparams(("parallel",)))(o, nw, ycat)


def _mixer_split_epilogue(dycat, first, rows, vecs, outs):
    (o_ref, lse_ref), (nw_ref,), (dyssd_ref, do_ref, st_ref, dnw_ref) = rows, vecs, outs

    @pl.when(first)
    def _():
        dnw_ref[...] = jnp.zeros_like(dnw_ref)

    dyssd_ref[...] = dycat[:, :SSD_D_INNER]
    dy = dycat[:, SSD_D_INNER:]
    o = o_ref[...]
    r = lax.rsqrt(jnp.mean(o * o, axis=-1, keepdims=True) + EPS)
    nrm = o * r
    dnw_ref[...] += jnp.sum(dy * nrm, axis=0, keepdims=True)
    dn = dy * nw_ref[...]
    do = r * (dn - nrm * jnp.mean(dn * nrm, axis=-1, keepdims=True))
    do_ref[...] = do
    ones_bd = _head_mean_matrix().astype(BF16)
    prod = do * o
    delta = jnp.concatenate([_head_sum2(prod[:, j * LANE:(j + 1) * LANE], ones_bd) for j in range(ATT_D // LANE)], axis=1)
    lane = lax.broadcasted_iota(jnp.int32, (1, ATT_D), 1)
    st_ref[...] = jnp.where((lane & (HEAD_DIM - 1)) < HALF, lse_ref[...], delta)


def _ada_fwd(c_all, w_ada):
    def body(c_ref, w_ref, o_ref):
        cv = c_ref[...]
        o_ref[...] = _dot((cv * _sigmoid(cv)).astype(BF16), w_ref[...].astype(BF16), NN)

    return pl.pallas_call(body, name="ada_fwd", out_shape=jax.ShapeDtypeStruct((c_all.shape[0], w_ada.shape[1]), F32),
                          compiler_params=_cparams())(c_all, w_ada)


def _adamw_math(g, w, m, v):
    m_new = ADAM_B1 * m + (1.0 - ADAM_B1) * g
    v_new = ADAM_B2 * v + (1.0 - ADAM_B2) * (g * g)
    m_hat = m_new / (1.0 - ADAM_B1 ** ADAM_STEP)
    v_hat = v_new / (1.0 - ADAM_B2 ** ADAM_STEP)
    delta = -ADAM_LR * (m_hat / (jnp.sqrt(v_hat) + ADAM_EPS) + ADAM_WD * w)
    return delta, m_new, v_new


def _ada_bwd_adamw(c_all, dmod_cols, w, m, v):
    rows, cols = w.shape
    tr = 256
    blk = pl.BlockSpec((tr, cols), lambda i: (i, 0))

    def body(c_ref, d_ref, w_ref, m_ref, v_ref, g_ref, dl_ref, mo_ref, vo_ref):
        cv = c_ref[...]
        ca = cv * _sigmoid(cv)
        g = ca[:, 0:1] * d_ref[0:1, :]
        for b in range(1, N_DEV):
            g = g + ca[:, b:b + 1] * d_ref[b:b + 1, :]
        g_ref[...] = g
        dl_ref[...], mo_ref[...], vo_ref[...] = _adamw_math(g, w_ref[...], m_ref[...], v_ref[...])

    o = jax.ShapeDtypeStruct((rows, cols), F32)
    return pl.pallas_call(
        body, name="ada_bwd_adamw", grid=(rows // tr,),
        in_specs=[pl.BlockSpec((tr, N_DEV), lambda i: (i, 0)), pl.BlockSpec((N_DEV, cols), lambda i: (0, 0)), blk, blk, blk],
        out_specs=[blk] * 4, out_shape=[o, o, o, o], compiler_params=_cparams(("parallel",)))(c_all.T, dmod_cols, w, m, v)


def _reduce_adamw(slabs, w, m, v, name):
    rows, cols = w.shape
    n_src = slabs.shape[0]
    if rows % 128 == 0:
        tr, steps = 128, rows // 128
        blk = pl.BlockSpec((tr, cols), lambda i: (i, 0))
        sblk = pl.BlockSpec((n_src, tr, cols), lambda i: (0, i, 0))
    else:
        tc, steps = 256, cols // 256
        blk = pl.BlockSpec((rows, tc), lambda i: (0, i))
        sblk = pl.BlockSpec((n_src, rows, tc), lambda i: (0, 0, i))

    def body(s_ref, w_ref, m_ref, v_ref, g_ref, dl_ref, mo_ref, vo_ref):
        g = s_ref[0].astype(F32)
        for src in range(1, n_src):
            g = g + s_ref[src].astype(F32)
        g_ref[...] = g
        dl_ref[...], mo_ref[...], vo_ref[...] = _adamw_math(g, w_ref[...], m_ref[...], v_ref[...])

    o = jax.ShapeDtypeStruct((rows, cols), F32)
    return pl.pallas_call(
        body, name=name, grid=(steps,), in_specs=[sblk, blk, blk, blk],
        out_specs=[blk] * 4, out_shape=[o, o, o, o], compiler_params=_cparams(("parallel",)))(slabs, w, m, v)


def _small_reduce_adamw(gathered, w, m, v):
    def body(s_ref, w_ref, m_ref, v_ref, g_ref, dl_ref, mo_ref, vo_ref):
        g = s_ref[0]
        for dev in range(1, N_DEV):
            g = g + s_ref[dev]
        g_ref[...] = g
        dl_ref[...], mo_ref[...], vo_ref[...] = _adamw_math(g, w_ref[...], m_ref[...], v_ref[...])

    o = jax.ShapeDtypeStruct(w.shape, F32)
    return pl.pallas_call(body, name="small_reduce_adamw", out_shape=[o, o, o, o], compiler_params=_cparams())(gathered, w, m, v)


def _adamw_small(g, w, m, v, name):
    def body(g_ref, w_ref, m_ref, v_ref, dl_ref, mo_ref, vo_ref):
        dl_ref[...], mo_ref[...], vo_ref[...] = _adamw_math(g_ref[...], w_ref[...], m_ref[...], v_ref[...])

    o = jax.ShapeDtypeStruct(w.shape, F32)
    return pl.pallas_call(body, name=name, out_shape=[o, o, o], compiler_params=_cparams())(g, w, m, v)


class _Exchange:
    def __init__(self, arrs, scatter):
        self.arrs, self.scatter, self.n = list(arrs), scatter, len(arrs)
        hbm = pl.BlockSpec(memory_space=pltpu.HBM)
        self.in_specs = [hbm] * self.n
        self.out_specs = [hbm] * self.n
        self.out_shape = [jax.ShapeDtypeStruct(a.shape if scatter else (N_DEV,) + a.shape, a.dtype) for a in self.arrs]
        self.scratch = [pltpu.SemaphoreType.DMA((self.n * (N_DEV - 1),)), pltpu.SemaphoreType.DMA((self.n * (N_DEV - 1),)),
                        pltpu.SemaphoreType.DMA((self.n,))]

    def _local(self, ins, outs, sems):
        me = 4 * lax.axis_index("x") + 2 * lax.axis_index("y") + lax.axis_index("c")
        return [pltpu.make_async_copy(ins[a].at[me] if self.scatter else ins[a], outs[a].at[me], sems[2].at[a])
                for a in range(self.n)]

    def _remote(self, ins, outs, sems, arriving):
        send_sems, recv_sems, _ = sems
        x, y, c = lax.axis_index("x"), lax.axis_index("y"), lax.axis_index("c")
        me = 4 * x + 2 * y + c
        remote = []
        for a in range(self.n):
            for k in range(1, N_DEV):
                px = 1 - x if k & 4 else x
                py = 1 - y if k & 2 else y
                pc = 1 - c if k & 1 else c
                peer = 4 * px + 2 * py + pc
                sem = a * (N_DEV - 1) + k - 1
                remote.append(pltpu.make_async_remote_copy(
                    src_ref=ins[a].at[peer] if self.scatter else ins[a], dst_ref=outs[a].at[peer if arriving else me],
                    send_sem=send_sems.at[sem], recv_sem=recv_sems.at[sem], device_id=(px, py, pc), device_id_type=MESH_IDS))
        return remote

    def start(self, ins, outs, sems):
        for cp in self._local(ins, outs, sems) + self._remote(ins, outs, sems, arriving=False):
            cp.start()

    def forward(self, ins, outs, sems):
        pass

    def wait(self, ins, outs, sems):
        for send, arrival in zip(self._remote(ins, outs, sems, arriving=False), self._remote(ins, outs, sems, arriving=True)):
            send.wait_send()
            arrival.wait_recv()
        for cp in self._local(ins, outs, sems):
            cp.wait()


N_CHIP = N_DEV // 2


class _SiblingSwap(_Exchange):
    def __init__(self, arrs):
        super().__init__(arrs, scatter=True)
        self.out_shape = [jax.ShapeDtypeStruct((N_CHIP,) + a.shape[2:], a.dtype) for a in self.arrs]
        self.scratch = [pltpu.SemaphoreType.DMA((self.n,)), pltpu.SemaphoreType.DMA((self.n,)), pltpu.SemaphoreType.DMA((1,))]

    def _copies(self, ins, outs, sems):
        x, y, c = lax.axis_index("x"), lax.axis_index("y"), lax.axis_index("c")
        return [pltpu.make_async_remote_copy(src_ref=ins[a].at[:, 1 - c], dst_ref=outs[a], send_sem=sems[0].at[a], recv_sem=sems[1].at[a],
                                             device_id=(x, y, 1 - c), device_id_type=MESH_IDS) for a in range(self.n)]

    def start(self, ins, outs, sems):
        for cp in self._copies(ins, outs, sems):
            cp.start()

    def wait(self, ins, outs, sems):
        for cp in self._copies(ins, outs, sems):
            cp.wait()


class _ChipScatter(_Exchange):
    def __init__(self, arrs):
        super().__init__(arrs, scatter=True)
        n_pairs = self.n * (N_CHIP - 1)
        self.scratch = [pltpu.SemaphoreType.DMA((n_pairs,)), pltpu.SemaphoreType.DMA((n_pairs,)), pltpu.SemaphoreType.DMA((self.n,))]

    def _local(self, ins, outs, sems):
        chip = 2 * lax.axis_index("x") + lax.axis_index("y")
        return [pltpu.make_async_copy(ins[a].at[chip], outs[a].at[chip], sems[2].at[a]) for a in range(self.n)]

    def _remote(self, ins, outs, sems, arriving):
        send_sems, recv_sems, _ = sems
        x, y, c = lax.axis_index("x"), lax.axis_index("y"), lax.axis_index("c")
        chip = 2 * x + y
        remote = []
        for a in range(self.n):
            for k in range(1, N_CHIP):
                px = 1 - x if k & 2 else x
                py = 1 - y if k & 1 else y
                peer = 2 * px + py
                sem = a * (N_CHIP - 1) + k - 1
                remote.append(pltpu.make_async_remote_copy(
                    src_ref=ins[a].at[peer], dst_ref=outs[a].at[peer if arriving else chip], send_sem=send_sems.at[sem],
                    recv_sem=recv_sems.at[sem], device_id=(px, py, c), device_id_type=MESH_IDS))
        return remote


def _chip_sum(mine, theirs):
    n, rows, cols = mine.shape
    blk = pl.BlockSpec((1, rows, 256), lambda q, j: (q, 0, j))

    def body(a_ref, b_ref, o_ref):
        o_ref[...] = (a_ref[...].astype(F32) + b_ref[...].astype(F32)).astype(BF16)

    return pl.pallas_call(body, name="chip_sum", grid=(n, cols // 256), in_specs=[blk, blk], out_specs=blk,
                          out_shape=jax.ShapeDtypeStruct(mine.shape, BF16),
                          compiler_params=_cparams(("parallel", "parallel")))(mine, theirs)


class _Gather2(_Exchange):
    def __init__(self, arrs):
        super().__init__(arrs, scatter=False)

    def _copies(self, ins, outs, sems):
        send_sems, recv_sems, _ = sems
        x, y, c = lax.axis_index("x"), lax.axis_index("y"), lax.axis_index("c")
        sibling = (x, y, 1 - c)
        chips = [(1 - x, y), (x, 1 - y), (1 - x, 1 - y)]
        first, passed, landed = [], [], []
        for a in range(self.n):
            def copy(k, block, to, src=None, a=a):
                slab = outs[a].at[4 * block[0] + 2 * block[1] + block[2]]
                return pltpu.make_async_remote_copy(
                    src_ref=slab if src is None else src, dst_ref=slab, send_sem=send_sems.at[a * (N_DEV - 1) + k],
                    recv_sem=recv_sems.at[a * (N_DEV - 1) + k], device_id=to, device_id_type=MESH_IDS)

            first.append(copy(0, (x, y, c), sibling, src=ins[a]))
            landed.append(copy(0, sibling, sibling))
            for j, chip in enumerate(chips):
                first.append(copy(1 + j, (x, y, c), (*chip, c), src=ins[a]))
                passed.append((copy(1 + j, (*chip, c), sibling), copy(4 + j, (*chip, c), sibling)))
                landed.append(copy(4 + j, (*chip, 1 - c), sibling))
        return first, passed, landed

    def start(self, ins, outs, sems):
        for cp in self._local(ins, outs, sems) + self._copies(ins, outs, sems)[0]:
            cp.start()

    def forward(self, ins, outs, sems):
        for arrival, onward in self._copies(ins, outs, sems)[1]:
            arrival.wait_recv()
            onward.start()

    def wait(self, ins, outs, sems):
        first, passed, landed = self._copies(ins, outs, sems)
        for arrival in landed:
            arrival.wait_recv()
        for cp in first + [onward for _, onward in passed]:
            cp.wait_send()
        for cp in self._local(ins, outs, sems):
            cp.wait()


def _split_comm_refs(refs, n_in, n_out, n_scr, comm):
    nc = comm.n if comm is not None else 0
    ns = 3 if comm is not None else 0
    pos, groups = 0, []
    for cnt in (n_in, nc, n_out, nc, n_scr, ns):
        groups.append(refs[pos:pos + cnt])
        pos += cnt
    assert pos == len(refs), (pos, len(refs))
    return groups


def _pcall(body, args, *, name, grid, in_specs, out_specs, out_shape, scratch_shapes=(), sem=None, comm=None):
    in_specs, out_specs, out_shape, scratch_shapes = list(in_specs), list(out_specs), list(out_shape), list(scratch_shapes)
    n_in, n_out, n_scr = len(in_specs), len(out_specs), len(scratch_shapes)
    if comm is None:
        kernel_body = body
    else:
        def kernel_body(*refs):
            ins, cins, outs, couts, scr, sems = _split_comm_refs(refs, n_in, n_out, n_scr, comm)
            ids = [pl.program_id(a) for a in range(len(grid))]
            first, last = ids[0] == 0, ids[0] == grid[0] - 1
            for a in range(1, len(grid)):
                first, last = first & (ids[a] == 0), last & (ids[a] == grid[a] - 1)

            late = ids[0] == grid[0] - 1
            for a in range(1, len(grid)):
                late = late & (ids[a] == 0)

            @pl.when(first)
            def _():
                comm.start(cins, couts, sems)

            @pl.when(late)
            def _():
                comm.forward(cins, couts, sems)

            body(*ins, *outs, *scr)

            @pl.when(last)
            def _():
                comm.wait(cins, couts, sems)

        in_specs, out_specs, out_shape = in_specs + comm.in_specs, out_specs + comm.out_specs, out_shape + comm.out_shape
        scratch_shapes, args = scratch_shapes + comm.scratch, list(args) + comm.arrs
        sem = ("arbitrary",) * len(grid)
    res = pl.pallas_call(kernel_body, name=name, grid=grid, in_specs=in_specs, out_specs=out_specs, out_shape=out_shape,
                         scratch_shapes=scratch_shapes, compiler_params=_cparams(sem))(*args)
    return res[:n_out], res[n_out:]


def _exchange(arrs, name, scatter=False, ex=None):
    if ex is None:
        ex = _Exchange(arrs, scatter=True) if scatter else _Gather2(arrs)

    def body(*refs):
        _, ins, _, outs, _, sems = _split_comm_refs(refs, 0, 0, 0, ex)
        ex.start(ins, outs, sems)
        ex.forward(ins, outs, sems)
        ex.wait(ins, outs, sems)

    return pl.pallas_call(body, name=name, in_specs=ex.in_specs, out_specs=ex.out_specs, out_shape=ex.out_shape,
                          scratch_shapes=ex.scratch)(*ex.arrs)


def _pad_lanes(v, width=LANE):
    return jnp.pad(v, ((0, 0), (0, width - v.shape[1])))


def _shards_to_cols(g):
    return jnp.transpose(g, (1, 0, 2)).reshape(g.shape[1], N_DEV * g.shape[2])


def _local_step(x, tgt, mod, w_in_pt, conv_w, conv_b, dt_bias, a_log, d_skip, ssd_norm_w, q_norm_w, k_norm_w,
                attn_norm_w, w_out_sh, w_ff1_sh, w_ff2_sh, norm1_w, norm2_w, core):
    shift1, scale1, gate1, shift2, scale2, gate2 = [mod[i:i + 1] for i in range(N_MOD)]
    dtb, alog, dsk = _pad_lanes(dt_bias), _pad_lanes(a_log), _pad_lanes(d_skip)
    qw, kw = jnp.tile(q_norm_w, (1, ATT_HEADS)), jnp.tile(k_norm_w, (1, ATT_HEADS))

    h1 = _norm_mod_fwd(x, norm1_w, scale1, shift1, "norm1_fwd")
    proj = _matmul(h1, w_in_pt, tb=True, tm=2048, tn=896, tk=1024, name="in_proj")
    pre, act = _conv_fwd(proj, conv_w, conv_b)
    ypre, ycat_ssd, hall = _ssd_fwd(proj, act, dtb, alog, dsk, ssd_norm_w)
    (o_att, lse), (w_out_g, w_ff1_g, w_ff2_g) = _att_fwd(proj, qw, kw, comm=_Gather2([w_out_sh, w_ff1_sh, w_ff2_sh]))
    w_out = w_out_g.reshape(2 * D_MODEL, D_MODEL)
    w_ff1 = _shards_to_cols(w_ff1_g)
    w_ff2 = w_ff2_g.reshape(D_FF, D_MODEL)
    ycat = _att_norm_fwd(o_att, attn_norm_w, ycat_ssd)
    row32, row16, vec32 = ("row", F32), ("row", BF16), ("vec", F32)
    mix, x1, h2 = _matmul_rows(ycat, w_out, _residual_norm_epilogue, [x], [gate1, norm2_w, scale2, shift2],
                               [row32, row32, row16], tm=512, name="out_proj")
    u, act_ff = _matmul(h2, w_ff1, tm=1024, tn=2048, tk=1024, name="ff1", mode="relu2")
    loss, dout, dff, dgate2 = _matmul_rows(act_ff, w_ff2, _loss_epilogue, [x1, tgt], [gate2],
                                           [("one", F32), row32, row16, vec32], tm=512, name="ff2")

    du = _matmul(dff, w_ff2, tb=True, tm=512, tn=4096, tk=1024, out_dtype=BF16, name="ff2_dx", mode="drelu2", u=u)
    g_ff2 = _matmul(act_ff, dff, ta=True, tm=512, tn=1024, tk=4096, out_dtype=BF16, name="ff2_dw")
    dx1, dshift2, dscale2, g_norm2, dmix, dgate1 = _matmul_rows(
        du, w_ff1, _norm_bwd_epilogue, [x1, dout, mix], [norm2_w, scale2, gate1],
        [row32, vec32, vec32, vec32, row16, vec32], tb=True, tm=512, name="ff1_dx")
    g_ff1 = _matmul(h2, du, ta=True, tm=1024, tn=D_FF // N_DEV, tk=4096, out_dtype=BF16, name="ff1_dw", shard_out=True)

    dy_ssd, do, stats, g_attn_norm = _matmul_rows(
        dmix, w_out, _mixer_split_epilogue, [o_att, lse], [attn_norm_w],
        [("row", F32, SSD_D_INNER), ("row", F32, ATT_D), ("row", F32, ATT_D), ("vec", F32, ATT_D)], tb=True, tm=512, name="out_proj_dx")
    g_out = _matmul(ycat, dmix, ta=True, tm=512, tn=1024, tk=4096, out_dtype=BF16, name="out_proj_dw")
    ff_slabs = [g_ff1, g_ff2.reshape(N_DEV, D_FF // N_DEV, D_MODEL)]
    (dq, dk, dv, dqw, dkw), (s_ff1, s_ff2) = _att_bwd(proj, do, stats, qw, kw, comm=_Exchange(ff_slabs, scatter=True))
    out_slabs = [g_out.astype(BF16).reshape(N_DEV, 2 * D_MODEL // N_DEV, D_MODEL)]
    (dz, dact, ddtr, da, g_dsk, g_dtb, g_ssd_norm), (s_out,) = _ssd_bwd(
        dy_ssd, ypre, proj, act, hall, dtb, alog, dsk, ssd_norm_w, comm=_Exchange(out_slabs, scatter=True))
    dxbc, g_conv_w, g_conv_b = _conv_bwd(dact, pre, proj, conv_w)
    dproj = [(dz, OFF_Z), (dxbc, OFF_XBC), (ddtr, OFF_DT), (dq, OFF_Q), (dk, OFF_K), (dv, OFF_V)]
    g_head, g_tail = _pieces_t_matmul([[dz, dxbc], [dq, dk, dv]], h1, tm=256, name="in_proj_dw")
    g_dt = _matmul(ddtr, h1, ta=True, tm=LANE, tn=1024, tk=4096, out_dtype=BF16, name="in_proj_dw_dt")[:SSD_HEADS]
    in_slabs = jnp.concatenate([g_head, g_dt, g_tail], axis=0).reshape(N_CHIP, 2, IN_W // N_DEV, D_MODEL)
    (sibling_slabs,) = _exchange(None, "swap_w_in_grads", ex=_SiblingSwap([in_slabs]))
    chip_slabs = _chip_sum(lax.dynamic_index_in_dim(in_slabs, core, axis=1, keepdims=False), sibling_slabs)
    (grad_x, dshift1, dscale1, g_norm1), (s_in,) = _matmul_rows(
        dproj, w_in_pt, _norm_bwd_epilogue, [x, dx1], [norm1_w, scale1], [row32, vec32, vec32, vec32],
        tm=256, name="in_proj_dx", comm=_ChipScatter([chip_slabs]))

    dmod = jnp.concatenate([dshift1, dscale1, dgate1, dshift2, dscale2, dgate2], axis=0)
    g_alog = da[:, :SSD_HEADS] * (-jnp.exp(a_log))
    g_qw = dqw.reshape(ATT_HEADS, HEAD_DIM).sum(axis=0, keepdims=True)
    g_kw = dkw.reshape(ATT_HEADS, HEAD_DIM).sum(axis=0, keepdims=True)
    return dict(loss=loss, grad_x=grad_x, dmod=dmod, norm1_w=g_norm1, norm2_w=g_norm2, w_in=s_in, conv_w=g_conv_w,
                conv_b=g_conv_b, dt_bias=g_dtb[:, :SSD_HEADS], a_log=g_alog, d_skip=g_dsk[:, :SSD_HEADS],
                ssd_norm_w=g_ssd_norm, q_norm_w=g_qw, k_norm_w=g_kw, attn_norm_w=g_attn_norm, w_out=s_out,
                w_ff1=s_ff1, w_ff2=s_ff2)


def _pack_w_in_rows(wt_full):
    cut = OFF_DT + SSD_HEADS
    pad = jnp.zeros((LANE - SSD_HEADS, wt_full.shape[1]), wt_full.dtype)
    return jnp.concatenate([wt_full[:cut], pad, wt_full[cut:]], axis=0)


MISC_FIELDS = (("dt_bias", SSD_HEADS), ("a_log", SSD_HEADS), ("d_skip", SSD_HEADS), ("q_norm_w", HEAD_DIM), ("k_norm_w", HEAD_DIM),
               ("loss", 1))
SMALL_LAYOUT = (("b_ada", 6), ("norm1_w", 1), ("norm2_w", 1), ("conv_w", 8), ("conv_b", 2), ("ssd_norm_w", 1),
                ("attn_norm_w", 1), ("misc", 1))


def _pack_small(vals):
    rows = []
    for name, nrow in SMALL_LAYOUT:
        if name == "misc":
            misc = jnp.concatenate([vals[f].reshape(1, n) if f in vals else jnp.zeros((1, n), F32) for f, n in MISC_FIELDS], axis=1)
            rows.append(_pad_lanes(misc, D_MODEL))
        elif name in vals:
            rows.append(vals[name].reshape(nrow, D_MODEL))
        else:
            rows.append(jnp.zeros((nrow, D_MODEL), F32))
    used = sum(n for _, n in SMALL_LAYOUT)
    rows.append(jnp.zeros((SMALL_ROWS - used, D_MODEL), F32))
    return jnp.concatenate(rows, axis=0)


def _unpack_small(packed):
    out, r = {}, 0
    for name, nrow in SMALL_LAYOUT:
        blk = packed[r:r + nrow]
        r += nrow
        if name == "misc":
            c0 = 0
            for f, n in MISC_FIELDS:
                out[f] = blk[:, c0:c0 + n]
                c0 += n
        elif name == "b_ada":
            out[name] = blk.reshape(1, N_MOD * D_MODEL)
        elif name == "conv_w":
            out[name] = blk.reshape(CONV_K, CONV_CH)
        elif name == "conv_b":
            out[name] = blk.reshape(1, CONV_CH)
        else:
            out[name] = blk
    return out


WEIGHT_NAMES = ("norm1_w", "norm2_w", "w_ada", "b_ada", "w_in", "conv_w", "conv_b", "dt_bias", "a_log", "d_skip",
                "ssd_norm_w", "q_norm_w", "k_norm_w", "attn_norm_w", "w_out", "w_ff1", "w_ff2")
SMALL_NAMES = ("norm1_w", "norm2_w", "b_ada", "conv_b", "dt_bias", "a_log", "d_skip", "ssd_norm_w", "q_norm_w",
               "k_norm_w", "attn_norm_w")


def kernel(x, c, norm1_w, norm2_w, w_ada, b_ada, w_in, conv_w, conv_b, dt_bias, a_log, d_skip, ssd_norm_w, q_norm_w, k_norm_w, attn_norm_w, w_out, w_ff1, w_ff2, loss_target, m_norm1_w, m_norm2_w, m_w_ada, m_b_ada, m_w_in, m_conv_w, m_conv_b, m_dt_bias, m_a_log, m_d_skip, m_ssd_norm_w, m_q_norm_w, m_k_norm_w, m_attn_norm_w, m_w_out, m_w_ff1, m_w_ff2, v_norm1_w, v_norm2_w, v_w_ada, v_b_ada, v_w_in, v_conv_w, v_conv_b, v_dt_bias, v_a_log, v_d_skip, v_ssd_norm_w, v_q_norm_w, v_k_norm_w, v_attn_norm_w, v_w_out, v_w_ff1, v_w_ff2):
    args = dict(locals())
    w = {n: args[n] for n in WEIGHT_NAMES}
    m = {n: args["m_" + n] for n in WEIGHT_NAMES}
    v = {n: args["v_" + n] for n in WEIGHT_NAMES}
    me = 4 * lax.axis_index("x") + 2 * lax.axis_index("y") + lax.axis_index("c")

    c_rows = jnp.pad(c, ((0, 7), (0, 0)))
    w_in_t, m_in_t, v_in_t = [jnp.transpose(t["w_in"][0]) for t in (w, m, v)]
    c_g, conv_g, w_in_g = _exchange([c_rows, w["conv_w"][0], w_in_t.astype(BF16)], "gather_w_in", scatter=False)
    c_all = c_g[:, 0, :]
    conv_full = _shards_to_cols(conv_g)
    w_in_pt = _pack_w_in_rows(w_in_g.reshape(IN_W, D_MODEL))

    mod_part = _ada_fwd(c_all, w["w_ada"][0])
    (mod_g,) = _exchange([mod_part], "gather_mod", scatter=False)
    mod_mine = lax.dynamic_index_in_dim(mod_g, me, axis=1, keepdims=False).reshape(1, N_MOD * D_MODEL) + w["b_ada"]
    mod = mod_mine.reshape(N_MOD, D_MODEL)

    res = _local_step(x[0], loss_target[0], mod, w_in_pt, conv_full, w["conv_b"], w["dt_bias"], w["a_log"], w["d_skip"],
                      w["ssd_norm_w"], w["q_norm_w"], w["k_norm_w"], w["attn_norm_w"], w["w_out"][0].astype(BF16),
                      w["w_ff1"][0].astype(BF16), w["w_ff2"][0].astype(BF16), w["norm1_w"], w["norm2_w"], lax.axis_index("c"))

    small_vals = {n: res[n] for n in SMALL_NAMES if n != "b_ada"}
    small_vals["b_ada"] = res["dmod"]
    small_vals["conv_w"] = res["conv_w"]
    small_vals["loss"] = res["loss"]
    (small_g,) = _exchange([_pack_small(small_vals)], "gather_small", scatter=False)

    grads, delta, new_m, new_v = {}, {}, {}, {}
    for name in ("w_out", "w_ff1", "w_ff2"):
        outs = _reduce_adamw(res[name], w[name][0], m[name][0], v[name][0], "adamw_" + name)
        grads[name], delta[name], new_m[name], new_v[name] = [o[None] for o in outs]
    outs = _reduce_adamw(res["w_in"], w_in_t, m_in_t, v_in_t, "adamw_w_in")
    grads["w_in"], delta["w_in"], new_m["w_in"], new_v["w_in"] = [jnp.transpose(o)[None] for o in outs]

    sm = _small_reduce_adamw(small_g, _pack_small({n: w[n] for n in SMALL_NAMES}), _pack_small({n: m[n] for n in SMALL_NAMES}),
                             _pack_small({n: v[n] for n in SMALL_NAMES}))
    sm = [_unpack_small(p) for p in sm]
    for n in SMALL_NAMES:
        grads[n], delta[n], new_m[n], new_v[n] = [p[n] for p in sm]
    shard_w = CONV_CH // N_DEV
    g_conv = lax.dynamic_slice_in_dim(sm[0]["conv_w"], me * shard_w, shard_w, axis=1)
    cw = _adamw_small(g_conv, w["conv_w"][0], m["conv_w"][0], v["conv_w"][0], "adamw_conv_w")
    grads["conv_w"] = g_conv[None]
    delta["conv_w"], new_m["conv_w"], new_v["conv_w"] = [o[None] for o in cw]

    ada_w = w_ada.shape[2]
    dmod_all = small_g[:, :N_MOD, :].reshape(N_DEV, N_MOD * D_MODEL)
    dmod_cols = lax.dynamic_slice_in_dim(dmod_all, me * ada_w, ada_w, axis=1)
    outs = _ada_bwd_adamw(c_all, dmod_cols, w["w_ada"][0], m["w_ada"][0], v["w_ada"][0])
    grads["w_ada"], delta["w_ada"], new_m["w_ada"], new_v["w_ada"] = [o[None] for o in outs]

    loss = sm[0]["loss"][0, 0]
    return (loss, res["grad_x"][None], *[grads[n] for n in WEIGHT_NAMES], *[delta[n] for n in WEIGHT_NAMES],
            *[new_m[n] for n in WEIGHT_NAMES], *[new_v[n] for n in WEIGHT_NAMES])
```

```python
import jax
import jax.numpy as jnp
from jax import lax
from jax.experimental import pallas as pl
from jax.experimental.pallas import tpu as pltpu

F32 = jnp.float32
BF16 = jnp.bfloat16
HIGHEST = lax.Precision.HIGHEST
MESH_IDS = pl.DeviceIdType.MESH

N_DEV = 8
D_MODEL = 1024
HEAD_DIM = 64
SSD_HEADS = 16
SSD_GROUPS = 4
HEADS_PER_GROUP = SSD_HEADS // SSD_GROUPS
SSD_STATE = 128
SSD_CHUNK = 128
SSD_D_INNER = SSD_HEADS * HEAD_DIM
GROUP_WIDTH = SSD_D_INNER // SSD_GROUPS
CONV_K = 4
CONV_CH = SSD_D_INNER + 2 * SSD_GROUPS * SSD_STATE
ATT_HEADS = 16
ATT_D = ATT_HEADS * HEAD_DIM
ATT_BLK = 128
DILATIONS = (1, 4, 16)
D_FF = 4 * D_MODEL
N_MOD = 6
EPS = 1e-6
IN_W = SSD_D_INNER + CONV_CH + SSD_HEADS + 3 * ATT_D
LANE = 128
OFF_Z, OFF_XBC, OFF_DT = 0, SSD_D_INNER, SSD_D_INNER + CONV_CH
OFF_Q = OFF_DT + LANE
OFF_K, OFF_V = OFF_Q + ATT_D, OFF_Q + 2 * ATT_D
IN_WP = OFF_V + ATT_D

ADAM_LR, ADAM_B1, ADAM_B2, ADAM_EPS, ADAM_WD, ADAM_STEP = 0.001, 0.9, 0.999, 1e-08, 0.01, 10
VMEM_LIMIT = 60 * 1024 * 1024
ROW_TILE = 512
SMALL_ROWS = 24


def _cparams(sem=None):
    return pltpu.CompilerParams(dimension_semantics=sem, vmem_limit_bytes=VMEM_LIMIT)


def _sigmoid(v):
    return 1.0 / (1.0 + jnp.exp(-v))


def _softplus(v):
    y = jnp.exp(-jnp.abs(v))
    small = y * (1.0 - y * (0.5 - y * (1.0 / 3.0)))
    return jnp.maximum(v, 0.0) + jnp.where(y < 0.01, small, jnp.log(1.0 + y))


def _dot(a, b, dims, precision=None):
    return lax.dot_general(a, b, (dims, ((), ())), preferred_element_type=F32, precision=precision)


NN = ((1,), (0,))
NT = ((1,), (1,))
TN = ((0,), (0,))


def _matmul(a, b, *, ta=False, tb=False, tm, tn, tk, out_dtype=F32, name, mode=None, u=None, comm=None, shard_out=False):
    m, k = (a.shape[1], a.shape[0]) if ta else a.shape
    n = b.shape[0] if tb else b.shape[1]
    assert m % tm == 0 and n % tn == 0 and k % tk == 0, (name, m, n, k)
    nk = k // tk
    a_spec = pl.BlockSpec((tk, tm), lambda i, j, kk: (kk, i)) if ta else pl.BlockSpec((tm, tk), lambda i, j, kk: (i, kk))
    b_spec = pl.BlockSpec((tn, tk), lambda i, j, kk: (j, kk)) if tb else pl.BlockSpec((tk, tn), lambda i, j, kk: (kk, j))
    o_spec = pl.BlockSpec((tm, tn), lambda i, j, kk: (i, j))
    dims = ((0,) if ta else (1,), (1,) if tb else (0,))
    n_out = 2 if mode == "relu2" else 1

    def body(*refs):
        if mode == "drelu2":
            a_ref, b_ref, u_ref = refs[:3]
            rest = refs[3:]
        else:
            a_ref, b_ref = refs[:2]
            u_ref = None
            rest = refs[2:]
        outs = rest[:n_out]
        part = _dot(a_ref[...], b_ref[...], dims)

        def finish(r):
            if mode == "relu2":
                outs[0][...] = r.astype(BF16)
                rr = jnp.maximum(r, 0.0)
                outs[1][...] = (rr * rr).astype(BF16)
            elif mode == "drelu2":
                outs[0][...] = (r * (2.0 * jnp.maximum(u_ref[...].astype(F32), 0.0))).astype(out_dtype)
            else:
                outs[0][...] = r.astype(out_dtype)

        if nk == 1:
            finish(part)
        else:
            acc = rest[n_out]
            kk = pl.program_id(2)

            @pl.when(kk == 0)
            def _():
                acc[...] = part

            @pl.when(kk > 0)
            def _():
                acc[...] += part

            @pl.when(kk == nk - 1)
            def _():
                finish(acc[...])

    in_specs = [a_spec, b_spec]
    args = [a, b]
    if mode == "drelu2":
        in_specs.append(o_spec)
        args.append(u)
    if mode == "relu2":
        out_shape = [jax.ShapeDtypeStruct((m, n), BF16), jax.ShapeDtypeStruct((m, n), BF16)]
    elif shard_out:
        out_shape = [jax.ShapeDtypeStruct((n // tn, m, tn), out_dtype)]
        o_spec = pl.BlockSpec((None, tm, tn), lambda i, j, kk: (j, i, 0))
    else:
        out_shape = [jax.ShapeDtypeStruct((m, n), out_dtype)]
    outs, comm_outs = _pcall(
        body, args, name=name, grid=(m // tm, n // tn, nk), in_specs=in_specs, out_specs=[o_spec] * n_out,
        out_shape=out_shape, scratch_shapes=[pltpu.VMEM((tm, tn), F32)] if nk > 1 else [],
        sem=("parallel", "parallel", "arbitrary"), comm=comm)
    res = tuple(outs) if mode == "relu2" else outs[0]
    return res if comm is None else (res, comm_outs)


def _pieces_t_matmul(groups, b, *, tm, name):
    k, n = b.shape
    pieces = [p for g in groups for p in g]
    starts, tiles = [], 0
    for p in pieces:
        assert p.shape[0] == k and p.shape[1] % tm == 0, (name, p.shape)
        starts.append(tiles)
        tiles += p.shape[1] // tm
    group_of, group_start, group_tiles = [], [], []
    for gi, g in enumerate(groups):
        group_start.append(starts[len(group_of)])
        group_of += [gi] * len(g)
        group_tiles.append(sum(p.shape[1] // tm for p in g))

    def clipped(block, start, count):
        return pl.BlockSpec(block, (lambda i: (0, jnp.clip(i - start, 0, count - 1))) if block[0] == k
                            else (lambda i: (jnp.clip(i - start, 0, count - 1), 0)))

    def body(*refs):
        a_refs, b_ref, o_refs = refs[:len(pieces)], refs[len(pieces)], refs[len(pieces) + 1:]
        i = pl.program_id(0)
        for a_ref, start, p, gi in zip(a_refs, starts, pieces, group_of):
            @pl.when((i >= start) & (i < start + p.shape[1] // tm))
            def _(a_ref=a_ref, o_ref=o_refs[gi]):
                o_ref[...] = _dot(a_ref[...], b_ref[...], TN).astype(BF16)

    return pl.pallas_call(
        body, name=name, grid=(tiles,),
        in_specs=[clipped((k, tm), s0, p.shape[1] // tm) for s0, p in zip(starts, pieces)] + [pl.BlockSpec((k, n), lambda i: (0, 0))],
        out_specs=[clipped((tm, n), s0, cnt) for s0, cnt in zip(group_start, group_tiles)],
        out_shape=[jax.ShapeDtypeStruct((cnt * tm, n), BF16) for cnt in group_tiles],
        compiler_params=_cparams(("arbitrary",)))(*pieces, b)


def _rms_mod(xv, nw, scale, shift):
    r = lax.rsqrt(jnp.mean(xv * xv, axis=-1, keepdims=True) + EPS)
    return ((xv * r) * nw * (1.0 + scale) + shift).astype(BF16)


def _norm_mod_fwd(x, nw, scale, shift, name):
    s, d = x.shape
    row = pl.BlockSpec((ROW_TILE, d), lambda i: (i, 0))
    vec = pl.BlockSpec((1, d), lambda i: (0, 0))

    def body(x_ref, nw_ref, sc_ref, sh_ref, h_ref):
        h_ref[...] = _rms_mod(x_ref[...], nw_ref[...], sc_ref[...], sh_ref[...])

    return pl.pallas_call(body, name=name, grid=(s // ROW_TILE,), in_specs=[row, vec, vec, vec], out_specs=row,
                          out_shape=jax.ShapeDtypeStruct((s, d), BF16), compiler_params=_cparams(("parallel",)))(x, nw, scale, shift)


def _matmul_rows(a, b, epilogue, row_in, vec_in, outs, *, tb=False, tm, name, comm=None):
    pieces = a if isinstance(a, list) else [(a, 0)]
    assert not (tb and len(pieces) > 1)
    m = pieces[0][0].shape[0]
    n = b.shape[0] if tb else b.shape[1]
    assert m % tm == 0, (name, m, tm)
    dims = ((1,), (1,) if tb else (0,))
    n_a, n_row, n_vec = len(pieces), len(row_in), len(vec_in)

    def body(*refs):
        a_refs, b_ref, rest = refs[:n_a], refs[n_a], refs[n_a + 1:]
        if n_a == 1:
            c = _dot(a_refs[0][...], b_ref[...], dims)
        else:
            c = None
            for a_ref, (piece, off) in zip(a_refs, pieces):
                part = _dot(a_ref[...], b_ref[off:off + piece.shape[1], :], dims)
                c = part if c is None else c + part
        epilogue(c, pl.program_id(0) == 0, rest[:n_row], rest[n_row:n_row + n_vec], rest[n_row + n_vec:])

    def spec(kind, width):
        block = {"row": (tm, width), "vec": (1, width), "one": (1, 1)}[kind]
        return pl.BlockSpec(block, (lambda i: (i, 0)) if kind == "row" else (lambda i: (0, 0)))

    def shape(kind, width):
        return {"row": (m, width), "vec": (1, width), "one": (1, 1)}[kind]

    outs = [(o[0], o[1], o[2] if len(o) > 2 else n) for o in outs]
    res, comm_outs = _pcall(
        body, [*[p for p, _ in pieces], b, *row_in, *vec_in], name=name, grid=(m // tm,),
        in_specs=[spec("row", p.shape[1]) for p, _ in pieces] + [pl.BlockSpec(b.shape, lambda i: (0, 0))]
        + [spec("row", r.shape[1]) for r in row_in] + [spec("vec", v.shape[1]) for v in vec_in],
        out_specs=[spec(kind, width) for kind, _, width in outs],
        out_shape=[jax.ShapeDtypeStruct(shape(kind, width), dt) for kind, dt, width in outs],
        sem=("arbitrary",), comm=comm)
    return res if comm is None else (res, comm_outs)


def _residual_norm_epilogue(mix, first, rows, vecs, outs):
    (x_ref,), (gate_ref, nw_ref, sc_ref, sh_ref), (mix_ref, x1_ref, h_ref) = rows, vecs, outs
    xv = x_ref[...] + gate_ref[...] * mix
    mix_ref[...] = mix
    x1_ref[...] = xv
    h_ref[...] = _rms_mod(xv, nw_ref[...], sc_ref[...], sh_ref[...])


def _loss_epilogue(ff, first, rows, vecs, outs):
    (x1_ref, t_ref), (g_ref,), (loss_ref, dout_ref, dff_ref, dg_ref) = rows, vecs, outs
    d = ff.shape[1]

    @pl.when(first)
    def _():
        loss_ref[...] = jnp.zeros_like(loss_ref)
        dg_ref[...] = jnp.zeros_like(dg_ref)

    err = x1_ref[...] + g_ref[...] * ff - t_ref[...]
    loss_ref[...] += (0.5 / d) * jnp.sum(err * err).reshape(1, 1)
    dout = err * (1.0 / d)
    dout_ref[...] = dout
    dff_ref[...] = (g_ref[...] * dout).astype(BF16)
    dg_ref[...] += jnp.sum(dout * ff, axis=0, keepdims=True)


def _norm_bwd_epilogue(dh, first, rows, vecs, outs):
    with_gate = len(vecs) == 3
    x_ref, dres_ref = rows[:2]
    nw_ref, sc_ref = vecs[:2]
    dx_ref, dsh_ref, dsc_ref, dnw_ref = outs[:4]

    @pl.when(first)
    def _():
        for ref in outs[1:4] + outs[5:]:
            ref[...] = jnp.zeros_like(ref)

    xv = x_ref[...]
    r = lax.rsqrt(jnp.mean(xv * xv, axis=-1, keepdims=True) + EPS)
    nrm = xv * r
    one_sc = 1.0 + sc_ref[...]
    dhn = dh * nrm
    dsh_ref[...] += jnp.sum(dh, axis=0, keepdims=True)
    dsc_ref[...] += jnp.sum(dhn, axis=0, keepdims=True) * nw_ref[...]
    dnw_ref[...] += jnp.sum(dhn, axis=0, keepdims=True) * one_sc
    dn = dh * (nw_ref[...] * one_sc)
    dx = dres_ref[...] + r * (dn - nrm * jnp.mean(dn * nrm, axis=-1, keepdims=True))
    dx_ref[...] = dx
    if with_gate:
        outs[4][...] = (vecs[2][...] * dx).astype(BF16)
        outs[5][...] += jnp.sum(dx * rows[2][...], axis=0, keepdims=True)


CONV_COLS = 256
CONV_FWD_ROWS = 2048
CONV_BWD_ROWS = 1024
CONV_SUB_ROWS = 128
HALO = 8


def _shift_down(cur, halo, k):
    if k == 0:
        return cur
    rolled = pltpu.roll(cur, k, axis=0)
    top = jnp.where(lax.broadcasted_iota(jnp.int32, halo.shape, 0) < k, pltpu.roll(halo, k, axis=0), rolled[:HALO])
    return jnp.concatenate([top, rolled[HALO:]], axis=0)


def _shift_up(cur, halo, k):
    if k == 0:
        return cur
    t = cur.shape[0]
    rolled = pltpu.roll(cur, t - k, axis=0)
    bot = jnp.where(lax.broadcasted_iota(jnp.int32, halo.shape, 0) >= HALO - k, pltpu.roll(halo, HALO - k, axis=0),
                    rolled[t - HALO:])
    return jnp.concatenate([rolled[:t - HALO], bot], axis=0)


def _conv_fwd(proj, conv_w, conv_b):
    s = proj.shape[0]
    nr = s // CONV_FWD_ROWS
    cb0 = OFF_XBC // CONV_COLS
    hb = CONV_FWD_ROWS // HALO
    cur = pl.BlockSpec((CONV_FWD_ROWS, CONV_COLS), lambda j, r: (r, cb0 + j))
    prev = pl.BlockSpec((HALO, CONV_COLS), lambda j, r: (jnp.maximum(r * hb - 1, 0), cb0 + j))
    out = pl.BlockSpec((CONV_FWD_ROWS, CONV_COLS), lambda j, r: (r, j))

    def body(u_ref, up_ref, w_ref, b_ref, pre_ref, act_ref):
        r = pl.program_id(1)
        for c in range(CONV_FWD_ROWS // CONV_SUB_ROWS):
            rows = slice(c * CONV_SUB_ROWS, (c + 1) * CONV_SUB_ROWS)
            u = u_ref[rows, :]
            halo = u_ref[c * CONV_SUB_ROWS - HALO:c * CONV_SUB_ROWS, :] if c > 0 else jnp.where(r > 0, up_ref[...], 0.0)
            acc = b_ref[...] + w_ref[CONV_K - 1:CONV_K, :] * u
            for k in range(1, CONV_K):
                acc = acc + w_ref[CONV_K - 1 - k:CONV_K - k, :] * _shift_down(u, halo, k)
            pre_ref[rows, :] = acc
            act_ref[rows, :] = acc * _sigmoid(acc)

    return pl.pallas_call(
        body, name="conv_fwd", grid=(CONV_CH // CONV_COLS, nr),
        in_specs=[cur, prev, pl.BlockSpec((CONV_K, CONV_COLS), lambda j, r: (0, j)),
                  pl.BlockSpec((1, CONV_COLS), lambda j, r: (0, j))],
        out_specs=[out, out],
        out_shape=[jax.ShapeDtypeStruct((s, CONV_CH), F32), jax.ShapeDtypeStruct((s, CONV_CH), F32)],
        compiler_params=_cparams(("parallel", "arbitrary")))(proj, proj, conv_w, conv_b)


def _conv_bwd(dact, pre, proj, conv_w):
    s = proj.shape[0]
    nr = s // CONV_BWD_ROWS
    cb0 = OFF_XBC // CONV_COLS
    hb = CONV_BWD_ROWS // HALO
    last_halo = s // HALO - 1
    n_sub = CONV_BWD_ROWS // CONV_SUB_ROWS
    cur = pl.BlockSpec((CONV_BWD_ROWS, CONV_COLS), lambda j, r: (r, j))
    nxt = pl.BlockSpec((HALO, CONV_COLS), lambda j, r: (jnp.minimum((r + 1) * hb, last_halo), j))
    ucur = pl.BlockSpec((CONV_BWD_ROWS, CONV_COLS), lambda j, r: (r, cb0 + j))
    wspec = pl.BlockSpec((CONV_K, CONV_COLS), lambda j, r: (0, j))
    bspec = pl.BlockSpec((1, CONV_COLS), lambda j, r: (0, j))

    def dsilu(p):
        sg = _sigmoid(p)
        return sg * (1.0 + p * (1.0 - sg))

    def body(da_ref, dan_ref, pre_ref, pren_ref, u_ref, w_ref, du_ref, dw_ref, db_ref):
        r = pl.program_id(1)

        @pl.when(r == 0)
        def _():
            dw_ref[...] = jnp.zeros_like(dw_ref)
            db_ref[...] = jnp.zeros_like(db_ref)

        dws = [jnp.zeros((1, CONV_COLS), F32) for _ in range(CONV_K)]
        db = jnp.zeros((1, CONV_COLS), F32)
        for c in range(n_sub):
            rows = slice(c * CONV_SUB_ROWS, (c + 1) * CONV_SUB_ROWS)
            ahead = slice((c + 1) * CONV_SUB_ROWS, (c + 1) * CONV_SUB_ROWS + HALO)
            dpre = da_ref[rows, :] * dsilu(pre_ref[rows, :])
            if c < n_sub - 1:
                dnext = da_ref[ahead, :] * dsilu(pre_ref[ahead, :])
            else:
                dnext = jnp.where(r < nr - 1, dan_ref[...] * dsilu(pren_ref[...]), 0.0)
            u = u_ref[rows, :]
            du = w_ref[CONV_K - 1:CONV_K, :] * dpre
            dws[0] = dws[0] + jnp.sum(dpre * u, axis=0, keepdims=True)
            for k in range(1, CONV_K):
                ahead_k = _shift_up(dpre, dnext, k)
                du = du + w_ref[CONV_K - 1 - k:CONV_K - k, :] * ahead_k
                dws[k] = dws[k] + jnp.sum(ahead_k * u, axis=0, keepdims=True)
            du_ref[rows, :] = du.astype(BF16)
            db = db + jnp.sum(dpre, axis=0, keepdims=True)
        dw_ref[...] += jnp.concatenate(dws[::-1], axis=0)
        db_ref[...] += db

    return pl.pallas_call(
        body, name="conv_bwd", grid=(CONV_CH // CONV_COLS, nr),
        in_specs=[cur, nxt, cur, nxt, ucur, wspec],
        out_specs=[cur, wspec, bspec],
        out_shape=[jax.ShapeDtypeStruct((s, CONV_CH), BF16), jax.ShapeDtypeStruct((CONV_K, CONV_CH), F32),
                   jax.ShapeDtypeStruct((1, CONV_CH), F32)],
        compiler_params=_cparams(("parallel", "arbitrary")))(dact, dact, pre, pre, proj, conv_w)


def _ssd_common(dtr, dtb, alog):
    lane = lax.broadcasted_iota(jnp.int32, (1, LANE), 1)
    head_lane = lane < SSD_HEADS
    dt = jnp.where(head_lane, _softplus(dtr + dtb), 0.0)
    a = jnp.where(head_lane, -jnp.exp(alog), 0.0)
    row = lax.broadcasted_iota(jnp.int32, (SSD_CHUNK, SSD_CHUNK), 0)
    col = lax.broadcasted_iota(jnp.int32, (SSD_CHUNK, SSD_CHUNK), 1)
    tril = row >= col
    cs = _dot(tril.astype(F32), dt * a, NN, precision=HIGHEST)
    return dt, a, cs, cs.T, tril, lane


def _split_bf16(v, passes):
    terms, rest = [], v
    for _ in range(passes):
        t = rest.astype(BF16)
        terms.append(t)
        rest = rest - t.astype(F32)
    return terms


def _dot_split(v, m, dims, passes):
    terms = _split_bf16(v, passes)
    if passes == 1:
        return _dot(terms[0], m, dims)
    return _dot(jnp.concatenate(terms, axis=1), jnp.concatenate([m] * passes, axis=0 if dims == NN else 1), dims)


def _ssd_constants():
    heads = jnp.arange(LANE)[:, None]
    exp_mat = (heads == (jnp.arange(SSD_D_INNER)[None, :] // HEAD_DIM)).astype(BF16)
    ind4 = ((jnp.arange(SSD_HEADS * SSD_CHUNK)[:, None] // SSD_CHUNK) == jnp.arange(LANE)[None, :]).astype(BF16)
    return exp_mat, ind4


def _expand_heads(v):
    return jnp.repeat(v[:, :SSD_HEADS], HEAD_DIM, axis=1)


def _ssd_prep(dtr, dtb, alog, exp_mat):
    dt, a, cs, cst, tril, lane = _ssd_common(dtr, dtb, alog)
    return dt, a, cs, cst, tril, lane, _dot_split(dt, exp_mat, NN, 2), _dot_split(cs, exp_mat, NN, 3)


def _chunk_decay_rows(cs, g):
    parts = []
    for e in range(HEADS_PER_GROUP):
        h = g * HEADS_PER_GROUP + e
        parts.append(jnp.broadcast_to(jnp.exp(cs[SSD_CHUNK - 1:SSD_CHUNK, h:h + 1]), (HEAD_DIM, SSD_STATE)))
    return jnp.concatenate(parts, axis=0)


def _ssd_fwd(proj, act, dtb, alog, dsk, nw):
    s = proj.shape[0]
    nc = s // SSD_CHUNK
    bc_w = SSD_GROUPS * SSD_STATE
    exp_mat, _ = _ssd_constants()

    def body(z_ref, dtr_ref, xs_ref, b_ref, c_ref, dtb_ref, alog_ref, dskx_ref, nw_ref, exp_ref,
             ypre_ref, yssd_ref, hall_ref, h_scr):
        @pl.when(pl.program_id(0) == 0)
        def _():
            h_scr[...] = jnp.zeros_like(h_scr)

        dt, a, cs, cst, tril, lane, dtx, csx = _ssd_prep(dtr_ref[...], dtb_ref[...], alog_ref[...], exp_ref[...])
        cs_last_x = csx[SSD_CHUNK - 1:SSD_CHUNK, :]
        xs = xs_ref[...]
        xdt = xs * dtx
        xdtb = xdt.astype(BF16)
        xdec = (xdt * jnp.exp(cs_last_x - csx)).astype(BF16)
        ecsx = jnp.exp(csx)
        head_of_lane = lax.broadcasted_iota(jnp.int32, (1, GROUP_WIDTH), 1) // HEAD_DIM
        for g in range(SSD_GROUPS):
            gs = slice(g * GROUP_WIDTH, (g + 1) * GROUP_WIDTH)
            bg = b_ref[:, g * SSD_STATE:(g + 1) * SSD_STATE].astype(BF16)
            cg = c_ref[:, g * SSD_STATE:(g + 1) * SSD_STATE].astype(BF16)
            cb = _dot(cg, bg, NT)
            hprev = h_scr[gs, :]
            hall_ref[0, gs, :] = hprev
            gms, rhs = [], []
            xg = xdtb[:, gs]
            for e in range(HEADS_PER_GROUP):
                h = g * HEADS_PER_GROUP + e
                lm = jnp.exp(jnp.where(tril, cs[:, h:h + 1] - cst[h:h + 1, :], -1e30))
                gms.append((cb * lm).astype(BF16))
                rhs.append(jnp.where(head_of_lane == e, xg, jnp.zeros_like(xg)))
            y = _dot(jnp.concatenate(gms, axis=1), jnp.concatenate(rhs, axis=0), NN)
            y = y + ecsx[:, gs] * _dot(cg, hprev.astype(BF16), NT)
            y = y + dskx_ref[:, gs] * xs[:, gs]
            h_scr[gs, :] = hprev * _chunk_decay_rows(cs, g) + _dot(xdec[:, gs], bg, TN)
            ypre_ref[:, gs] = y
            z = z_ref[:, gs]
            yg = y * (z * _sigmoid(z))
            r = lax.rsqrt(jnp.mean(yg * yg, axis=-1, keepdims=True) + EPS)
            yssd_ref[:, gs] = (yg * r * nw_ref[:, gs]).astype(BF16)

    row_d = lambda cb: pl.BlockSpec((SSD_CHUNK, SSD_D_INNER), lambda c: (c, cb))
    small = pl.BlockSpec((1, LANE), lambda c: (0, 0))
    wide = pl.BlockSpec((1, SSD_D_INNER), lambda c: (0, 0))
    return pl.pallas_call(
        body, name="ssd_fwd", grid=(nc,),
        in_specs=[row_d(OFF_Z // SSD_D_INNER),
                  pl.BlockSpec((SSD_CHUNK, LANE), lambda c: (c, OFF_DT // LANE)),
                  row_d(0),
                  pl.BlockSpec((SSD_CHUNK, bc_w), lambda c: (c, SSD_D_INNER // bc_w)),
                  pl.BlockSpec((SSD_CHUNK, bc_w), lambda c: (c, SSD_D_INNER // bc_w + 1)),
                  small, small, wide, wide, pl.BlockSpec((LANE, SSD_D_INNER), lambda c: (0, 0))],
        out_specs=[row_d(0), row_d(0), pl.BlockSpec((1, SSD_D_INNER, SSD_STATE), lambda c: (c, 0, 0))],
        out_shape=[jax.ShapeDtypeStruct((s, SSD_D_INNER), F32), jax.ShapeDtypeStruct((s, SSD_D_INNER + ATT_D), BF16),
                   jax.ShapeDtypeStruct((nc, SSD_D_INNER, SSD_STATE), F32)],
        scratch_shapes=[pltpu.VMEM((SSD_D_INNER, SSD_STATE), F32)],
        compiler_params=_cparams(("arbitrary",)))(proj, proj, act, act, act, dtb, alog, _expand_heads(dsk), nw, exp_mat)


def _ssd_bwd(dycat, ypre, proj, act, hall, dtb, alog, dsk, nw, comm=None):
    s = proj.shape[0]
    nc = s // SSD_CHUNK
    bc_w = SSD_GROUPS * SSD_STATE

    exp_mat, ind4 = _ssd_constants()
    seg_passes = 1

    def body(dy_ref, ypre_ref, z_ref, dtr_ref, xs_ref, b_ref, c_ref, hall_ref, dtb_ref, alog_ref, dskx_ref, nw_ref,
             exp_ref, ind4_ref, dz_ref, dact_ref, ddtr_ref, da_ref, ddsk_ref, ddtb_ref, dnw_ref, dh_scr):
        @pl.when(pl.program_id(0) == 0)
        def _():
            dh_scr[...] = jnp.zeros_like(dh_scr)
            da_ref[...] = jnp.zeros_like(da_ref)
            ddsk_ref[...] = jnp.zeros_like(ddsk_ref)
            ddtb_ref[...] = jnp.zeros_like(ddtb_ref)
            dnw_ref[...] = jnp.zeros_like(dnw_ref)

        dtr = dtr_ref[...]
        dt, a, cs, cst, tril, lane, dtx, csx = _ssd_prep(dtr, dtb_ref[...], alog_ref[...], exp_ref[...])
        cs_last_x = csx[SSD_CHUNK - 1:SSD_CHUNK, :]
        xs = xs_ref[...]
        xdt = xs * dtx
        xdtb = xdt.astype(BF16)
        decx = jnp.exp(cs_last_x - csx)
        xdecf = xdt * decx
        xdec = xdecf.astype(BF16)
        ecsx = jnp.exp(csx)
        head_of_lane = lax.broadcasted_iota(jnp.int32, (1, GROUP_WIDTH), 1) // HEAD_DIM
        last_row = lax.broadcasted_iota(jnp.int32, (SSD_CHUNK, 1), 0) == SSD_CHUNK - 1
        dcs_col = jnp.zeros((SSD_CHUNK, LANE), F32)
        dcs_row = jnp.zeros((SSD_CHUNK, LANE), F32)
        ddt = jnp.zeros((SSD_CHUNK, LANE), F32)
        ddsk = jnp.zeros((1, LANE), F32)
        hsum = jnp.zeros((1, LANE), F32)
        t1_sum = jnp.zeros((1, LANE), F32)
        for g in range(SSD_GROUPS):
            gs = slice(g * GROUP_WIDTH, (g + 1) * GROUP_WIDTH)
            bsl = slice(g * SSD_STATE, (g + 1) * SSD_STATE)
            exp_g = exp_ref[:, gs]
            ind4_g = ind4_ref[g * HEADS_PER_GROUP * SSD_CHUNK:(g + 1) * HEADS_PER_GROUP * SSD_CHUNK, :]
            z = z_ref[:, gs]
            sg = _sigmoid(z)
            sz = z * sg
            ypre = ypre_ref[:, gs]
            yg = ypre * sz
            r = lax.rsqrt(jnp.mean(yg * yg, axis=-1, keepdims=True) + EPS)
            nrm = yg * r
            dyo_n = dy_ref[:, gs]
            dnw_ref[:, gs] += jnp.sum(dyo_n * nrm, axis=0, keepdims=True)
            dn = dyo_n * nw_ref[:, gs]
            dyg = r * (dn - nrm * jnp.mean(dn * nrm, axis=-1, keepdims=True))
            dz_ref[:, gs] = (dyg * ypre * (sg * (1.0 + z * (1.0 - sg)))).astype(BF16)
            dy = dyg * sz

            bg = b_ref[:, bsl].astype(BF16)
            cg = c_ref[:, bsl].astype(BF16)
            cb = _dot(cg, bg, NT)
            hprev = hall_ref[0, gs, :]
            hb = hprev.astype(BF16)
            dhn = dh_scr[gs, :]
            dhb = dhn.astype(BF16)
            xs_g, xdt_g = xs[:, gs], xdtb[:, gs]
            w_off = _dot(cg, hb, NT)
            dyo = dy * ecsx[:, gs]
            dyob = dyo.astype(BF16)
            dcg = _dot(dyob, hb, NN)
            dh_y = _dot(dyob, cg, TN)
            r_st = _dot(bg, dhb, NT)
            dbg = _dot(xdec[:, gs], dhb, NN)
            dyb = dy.astype(BF16)
            gms, gmbs, lms, dys = [], [], [], []
            for e in range(HEADS_PER_GROUP):
                h = g * HEADS_PER_GROUP + e
                lm = jnp.exp(jnp.where(tril, cs[:, h:h + 1] - cst[h:h + 1, :], -1e30))
                gm = cb * lm
                lms.append(lm)
                gms.append(gm)
                gmbs.append(gm.astype(BF16))
                dys.append(jnp.where(head_of_lane == e, dyb, jnp.zeros_like(dyb)))
            dxdt = _dot(jnp.concatenate(gmbs, axis=0), jnp.concatenate(dys, axis=0), TN) + decx[:, gs] * r_st
            dcb = jnp.zeros((SSD_CHUNK, SSD_CHUNK), F32)
            mms = []
            for e in range(HEADS_PER_GROUP):
                dg = _dot(dys[e], xdt_g, NT)
                mms.append(dg * gms[e])
                dcb = dcb + dg * lms[e]
            seg = _dot_split(jnp.concatenate([dyo * w_off, xdecf[:, gs] * r_st, dxdt * xs_g, dy * xs_g], axis=0), exp_g, NT, seg_passes)
            v1, t1, ddt_g, dsk_g = [seg[i * SSD_CHUNK:(i + 1) * SSD_CHUNK] for i in range(4)]
            dcs_col = dcs_col + v1 - t1 + _dot_split(jnp.concatenate(mms, axis=1), ind4_g, NN, seg_passes)
            for t in _split_bf16(jnp.concatenate(mms, axis=0), seg_passes):
                dcs_row = dcs_row + _dot(ind4_g, t, TN)
            ddt = ddt + ddt_g
            ddsk = ddsk + jnp.sum(dsk_g, axis=0, keepdims=True)
            t1_sum = t1_sum + jnp.sum(t1, axis=0, keepdims=True)
            for e in range(HEADS_PER_GROUP):
                h = g * HEADS_PER_GROUP + e
                hs = slice(e * HEAD_DIM, (e + 1) * HEAD_DIM)
                hsum = hsum + jnp.where(lane == h, jnp.sum(dhn[hs, :] * hprev[hs, :]).reshape(1, 1), 0.0)
            dh_scr[gs, :] = dhn * _chunk_decay_rows(cs, g) + dh_y
            dcbb = dcb.astype(BF16)
            dact_ref[:, gs] = dxdt * dtx[:, gs] + dskx_ref[:, gs] * dy
            dact_ref[:, SSD_D_INNER + g * SSD_STATE:SSD_D_INNER + (g + 1) * SSD_STATE] = dbg + _dot(dcbb, cg, TN)
            dact_ref[:, SSD_D_INNER + bc_w + g * SSD_STATE:SSD_D_INNER + bc_w + (g + 1) * SSD_STATE] = dcg + _dot(dcbb, bg, NN)
        dlast = t1_sum + jnp.exp(cs[SSD_CHUNK - 1:SSD_CHUNK, :]) * hsum
        dcs = dcs_col - dcs_row.T + jnp.where(last_row, dlast, 0.0)
        row = lax.broadcasted_iota(jnp.int32, (SSD_CHUNK, SSD_CHUNK), 0)
        col = lax.broadcasted_iota(jnp.int32, (SSD_CHUNK, SSD_CHUNK), 1)
        dda = _dot((col >= row).astype(F32), dcs, NN, precision=HIGHEST)
        ddt = ddt + dda * a
        da_ref[...] += jnp.sum(dda * dt, axis=0, keepdims=True)
        ddtr = jnp.where(lane < SSD_HEADS, ddt * _sigmoid(dtr + dtb_ref[...]), 0.0)
        ddtr_ref[...] = ddtr.astype(BF16)
        ddtb_ref[...] += jnp.sum(ddtr, axis=0, keepdims=True)
        ddsk_ref[...] += ddsk

    rev = lambda c: nc - 1 - c
    row_d = lambda cb: pl.BlockSpec((SSD_CHUNK, SSD_D_INNER), lambda c: (rev(c), cb))
    small = pl.BlockSpec((1, LANE), lambda c: (0, 0))
    wide = pl.BlockSpec((1, SSD_D_INNER), lambda c: (0, 0))
    small_shape = jax.ShapeDtypeStruct((1, LANE), F32)
    return _pcall(
        body, (dycat, ypre, proj, proj, act, act, act, hall, dtb, alog, _expand_heads(dsk), nw, exp_mat, ind4),
        name="ssd_bwd", grid=(nc,),
        in_specs=[row_d(0), row_d(0), row_d(OFF_Z // SSD_D_INNER),
                  pl.BlockSpec((SSD_CHUNK, LANE), lambda c: (rev(c), OFF_DT // LANE)),
                  row_d(0),
                  pl.BlockSpec((SSD_CHUNK, bc_w), lambda c: (rev(c), SSD_D_INNER // bc_w)),
                  pl.BlockSpec((SSD_CHUNK, bc_w), lambda c: (rev(c), SSD_D_INNER // bc_w + 1)),
                  pl.BlockSpec((1, SSD_D_INNER, SSD_STATE), lambda c: (rev(c), 0, 0)),
                  small, small, wide, wide, pl.BlockSpec((LANE, SSD_D_INNER), lambda c: (0, 0)),
                  pl.BlockSpec((SSD_HEADS * SSD_CHUNK, LANE), lambda c: (0, 0))],
        out_specs=[row_d(0), pl.BlockSpec((SSD_CHUNK, CONV_CH), lambda c: (rev(c), 0)),
                   pl.BlockSpec((SSD_CHUNK, LANE), lambda c: (rev(c), 0)), small, small, small, wide],
        out_shape=[jax.ShapeDtypeStruct((s, SSD_D_INNER), BF16), jax.ShapeDtypeStruct((s, CONV_CH), F32),
                   jax.ShapeDtypeStruct((s, LANE), BF16), small_shape, small_shape, small_shape,
                   jax.ShapeDtypeStruct((1, SSD_D_INNER), F32)],
        scratch_shapes=[pltpu.VMEM((SSD_D_INNER, SSD_STATE), F32)], sem=("arbitrary",), comm=comm)


def _head_mean_matrix():
    row = lax.broadcasted_iota(jnp.int32, (LANE, LANE), 0) // HEAD_DIM
    col = lax.broadcasted_iota(jnp.int32, (LANE, LANE), 1) // HEAD_DIM
    return (row == col).astype(F32)


def _head_sum2(v, ones_bd):
    hi = v.astype(BF16)
    lo = (v - hi.astype(F32)).astype(BF16)
    return _dot(jnp.concatenate([hi, lo], axis=1), jnp.concatenate([ones_bd, ones_bd], axis=0), NN)


def _head_norms(xs, ws, ones_bd):
    sums = [_head_sum2(x * x, ones_bd) for x in xs]
    rs = [lax.rsqrt(ms * (1.0 / HEAD_DIM) + EPS) for ms in sums]
    return [(x * r) * w for x, r, w in zip(xs, rs, ws)], rs


def _head_norms_bwd(dns, xs, ws, rs, ones_bd):
    nrms = [x * r for x, r in zip(xs, rs)]
    dnws = [dn * w for dn, w in zip(dns, ws)]
    projs = [_head_sum2(dnw * nrm, ones_bd) for dnw, nrm in zip(dnws, nrms)]
    dxs = [r * (dnw - nrm * (pr * (1.0 / HEAD_DIM))) for r, dnw, nrm, pr in zip(rs, dnws, nrms, projs)]
    return dxs, [jnp.sum(dn * nrm, axis=0, keepdims=True) for dn, nrm in zip(dns, nrms)]


NORM_CHUNKS = 4


PRO_ROWS = 256
ATT_GROUP_FWD = 32
ATT_GROUP_BWD = 8
KEYS = 2 * ATT_BLK
NEG = -1e30
HALF = HEAD_DIM // 2


def _rows(start, size, dil):
    return pl.ds(start, size) if dil == 1 else pl.ds(start, size, stride=dil)


def _fill_bias(bias_ref):
    row = lax.broadcasted_iota(jnp.int32, (ATT_BLK, 2 * KEYS), 0)
    col = lax.broadcasted_iota(jnp.int32, (ATT_BLK, 2 * KEYS), 1) & (KEYS - 1)
    for first, off in ((0, 0), (1, ATT_BLK)):
        dist = off + row - col
        bias_ref[first] = jnp.where((dist >= 0) & (dist <= ATT_BLK), 0.0, NEG)


def _pair(a, b):
    return jnp.concatenate([jnp.broadcast_to(a, (ATT_BLK, KEYS)), jnp.broadcast_to(b, (ATT_BLK, KEYS))], axis=1)


def _split_heads(x, is_a):
    zero = jnp.zeros_like(x)
    return jnp.concatenate([jnp.where(is_a, x, zero), jnp.where(is_a, zero, x)], axis=0)


def _block_ids(b, nb):
    i = b & (nb - 1)
    q0 = pl.multiple_of(b * ATT_BLK, ATT_BLK)
    k0 = pl.multiple_of((b - jnp.minimum(i, 1)) * ATT_BLK, ATT_BLK)
    return pl.ds(q0, ATT_BLK), pl.ds(k0, KEYS), jnp.minimum(i, 1)


def _natural_rows(b, nb, dil):
    if dil == 1:
        return pl.ds(pl.multiple_of(b * ATT_BLK, ATT_BLK), ATT_BLK)
    return pl.ds(b // nb + dil * ((b & (nb - 1)) * ATT_BLK), ATT_BLK, stride=dil)


def _att_fwd(proj, qw, kw, comm=None):
    s = proj.shape[0]
    nblk = s // ATT_BLK
    assert all((s // d) // ATT_BLK >= 2 for d in DILATIONS)
    blk = lambda off: pl.BlockSpec((s, LANE), lambda i: (0, off // LANE + i))
    wspec = pl.BlockSpec((1, LANE), lambda i: (0, i))
    oblk = pl.BlockSpec((s, LANE), lambda i: (0, i))

    def body(q_ref, k_ref, v_ref, qw_ref, kw_ref, o_ref, lse_ref, qn, kn, q_cm, k_cm, v_cm, m_acc, l_acc, o_d, m_d, l_d,
             o_e, m_e, l_e, tq, tk, tv, bias):
        ones_bd = _head_mean_matrix().astype(BF16)
        is_a = lax.broadcasted_iota(jnp.int32, (1, LANE), 1) < HEAD_DIM
        ones_ext = _split_heads(jnp.ones((KEYS, LANE), BF16), is_a)
        _fill_bias(bias)

        def pro(j, c):
            chunks = [pl.ds(pl.multiple_of((NORM_CHUNKS * j + u) * PRO_ROWS, PRO_ROWS), PRO_ROWS) for u in range(NORM_CHUNKS)]
            normed, _ = _head_norms([q_ref[rows, :] for rows in chunks] + [k_ref[rows, :] for rows in chunks],
                                    [qw_ref[...] * HEAD_DIM ** -0.5] * NORM_CHUNKS + [kw_ref[...]] * NORM_CHUNKS, ones_bd)
            for u, rows in enumerate(chunks):
                qn[rows, :] = normed[u]
                kn[rows, :] = normed[NORM_CHUNKS + u]
            return c

        lax.fori_loop(0, s // (NORM_CHUNKS * PRO_ROWS), pro, 0)

        results = dict(zip(DILATIONS, ((o_ref, m_acc, l_acc), (o_d, m_d, l_d), (o_e, m_e, l_e))))
        for dil in DILATIONS:
            ln = s // dil
            nb = ln // ATT_BLK
            o_out, m_out, l_out = results[dil]
            level = DILATIONS.index(dil)
            keep_f32 = 0 < level < len(DILATIONS) - 1
            from_temps = level >= 2
            step_rows = dil // DILATIONS[level - 1] if from_temps else dil
            for r in range(dil):
                prev = DILATIONS[level - 1] if from_temps else 1
                start = (r % prev) * (s // prev) + r // prev if from_temps else r

                def relayout(j, c, r=r, ln=ln, start=start, step_rows=step_rows, keep_f32=keep_f32, from_temps=from_temps):
                    j0 = pl.multiple_of(j * PRO_ROWS, PRO_ROWS)
                    src = _rows(start + step_rows * j0, PRO_ROWS, step_rows)
                    dst = pl.ds(r * ln + j0, PRO_ROWS)
                    qv, kv, vv = (tq[src, :], tk[src, :], tv[src, :]) if from_temps else (qn[src, :], kn[src, :], v_ref[src, :])
                    q_cm[dst, :] = qv.astype(BF16)
                    k_cm[dst, :] = kv.astype(BF16)
                    v_cm[dst, :] = vv.astype(BF16)
                    if keep_f32:
                        tq[dst, :] = qv
                        tk[dst, :] = kv
                        tv[dst, :] = vv
                    return c

                lax.fori_loop(0, ln // PRO_ROWS, relayout, 0)

            def step(bg, c, nb=nb, o_out=o_out, m_out=m_out, l_out=l_out):
                ids = [_block_ids(bg * ATT_GROUP_FWD + u, nb) for u in range(ATT_GROUP_FWD)]
                kbs = [_split_heads(k_cm[krows, :], is_a) for _, krows, _ in ids]
                scs = [_dot(q_cm[qrows, :], kb, NT) + bias[first] for (qrows, _, first), kb in zip(ids, kbs)]
                mas = [jnp.max(sc[:, :KEYS], axis=-1, keepdims=True) for sc in scs]
                mbs = [jnp.max(sc[:, KEYS:], axis=-1, keepdims=True) for sc in scs]
                ps = [jnp.exp(sc - _pair(ma, mb)).astype(BF16) for sc, ma, mb in zip(scs, mas, mbs)]
                vbs = [jnp.concatenate([_split_heads(v_cm[krows, :], is_a), ones_ext], axis=1) for _, krows, _ in ids]
                ols = [_dot(p, vb, NN) for p, vb in zip(ps, vbs)]
                for (qrows, _, _), ol, ma, mb in zip(ids, ols, mas, mbs):
                    o_out[qrows, :] = ol[:, :LANE]
                    l_out[qrows, :] = ol[:, LANE:]
                    m_out[qrows, :] = jnp.where(is_a, ma, mb)
                return c

            lax.fori_loop(0, nblk // ATT_GROUP_FWD, step, 0)

        for level in range(len(DILATIONS) - 1, 0, -1):
            fine_d, coarse_d = DILATIONS[level - 1], DILATIONS[level]
            ratio, ln_f, ln_c = coarse_d // fine_d, s // fine_d, s // coarse_d
            (o_f, m_f, l_f), (o_c, m_c, l_c) = results[fine_d], results[coarse_d]
            for r in range(coarse_d):
                def merge(j, c, r=r, ratio=ratio, ln_c=ln_c, start=(r % fine_d) * ln_f + r // fine_d,
                          o_f=o_f, m_f=m_f, l_f=l_f, o_c=o_c, m_c=m_c, l_c=l_c):
                    j0 = pl.multiple_of(j * PRO_ROWS, PRO_ROWS)
                    fine = _rows(start + ratio * j0, PRO_ROWS, ratio)
                    coarse = pl.ds(r * ln_c + j0, PRO_ROWS)
                    m_old, m_new = m_f[fine, :], m_c[coarse, :]
                    m = jnp.maximum(m_old, m_new)
                    a_old, a_new = jnp.exp(m_old - m), jnp.exp(m_new - m)
                    o_f[fine, :] = a_old * o_f[fine, :] + a_new * o_c[coarse, :]
                    l_f[fine, :] = a_old * l_f[fine, :] + a_new * l_c[coarse, :]
                    m_f[fine, :] = m
                    return c

                lax.fori_loop(0, ln_c // PRO_ROWS, merge, 0)

        def epi(j, c):
            rows = pl.ds(pl.multiple_of(j * PRO_ROWS, PRO_ROWS), PRO_ROWS)
            l = l_acc[rows, :]
            o_ref[rows, :] = o_ref[rows, :] / l
            lse_ref[rows, :] = m_acc[rows, :] + jnp.log(l)
            return c

        lax.fori_loop(0, s // PRO_ROWS, epi, 0)

    f = jax.ShapeDtypeStruct((s, ATT_D), F32)
    scr = pltpu.VMEM((s, LANE), F32)
    scb = pltpu.VMEM((s, LANE), BF16)
    return _pcall(
        body, (proj, proj, proj, qw, kw), name="att_fwd", grid=(ATT_D // LANE,),
        in_specs=[blk(OFF_Q), blk(OFF_K), blk(OFF_V), wspec, wspec], out_specs=[oblk, oblk], out_shape=[f, f],
        scratch_shapes=[scr, scr, scb, scb, scb] + [scr] * 11 + [pltpu.VMEM((2, ATT_BLK, 2 * KEYS), F32)],
        sem=("parallel",), comm=comm)


def _att_bwd(proj, do, stats, qw, kw, comm=None):
    s = proj.shape[0]
    nblk = s // ATT_BLK
    blk = lambda off: pl.BlockSpec((s, LANE), lambda i: (0, off // LANE + i))
    wspec = pl.BlockSpec((1, LANE), lambda i: (0, i))
    oblk = pl.BlockSpec((s, LANE), lambda i: (0, i))

    def body(q_ref, k_ref, v_ref, do_ref, st_ref, qw_ref, kw_ref, dq_ref, dk_ref, dv_ref, dqw_ref, dkw_ref,
             qn, kn, q_cm, do_cm, k_cm, v_cm, rms, dq_acc, dk_acc, dv_acc, dq_d, dk_d, dv_d, dq_e, dk_e, dv_e, bias):
        ones_bd = _head_mean_matrix().astype(BF16)
        is_a = lax.broadcasted_iota(jnp.int32, (1, LANE), 1) < HEAD_DIM
        first_half = (lax.broadcasted_iota(jnp.int32, (1, LANE), 1) & (HEAD_DIM - 1)) < HALF
        _fill_bias(bias)
        zero = jnp.zeros((PRO_ROWS, LANE), F32)
        results = dict(zip(DILATIONS, ((dq_acc, dk_acc, dv_acc), (dq_d, dk_d, dv_d), (dq_e, dk_e, dv_e))))

        def pro(j, c):
            chunks = [pl.ds(pl.multiple_of((NORM_CHUNKS * j + u) * PRO_ROWS, PRO_ROWS), PRO_ROWS) for u in range(NORM_CHUNKS)]
            normed, rs = _head_norms([q_ref[rows, :] for rows in chunks] + [k_ref[rows, :] for rows in chunks],
                                     [qw_ref[...] * HEAD_DIM ** -0.5] * NORM_CHUNKS + [kw_ref[...]] * NORM_CHUNKS, ones_bd)
            for u, rows in enumerate(chunks):
                qn[rows, :] = normed[u]
                kn[rows, :] = normed[NORM_CHUNKS + u]
                rms[rows, :] = jnp.where(first_half, rs[u], rs[NORM_CHUNKS + u])
                dk_acc[rows, :] = zero
                dv_acc[rows, :] = zero
            return c

        lax.fori_loop(0, s // (NORM_CHUNKS * PRO_ROWS), pro, 0)

        for dil in DILATIONS:
            ln = s // dil
            nb = ln // ATT_BLK
            dq_o, dk_o, dv_o = results[dil]
            level = DILATIONS.index(dil)
            keep_f32 = 0 < level < len(DILATIONS) - 1
            from_temps = level >= 2
            temps = results[DILATIONS[-1]]
            for r in range(dil):
                prev = DILATIONS[level - 1] if from_temps else 1
                start = (r % prev) * (s // prev) + r // prev if from_temps else r

                def relayout(j, c, dil=dil, r=r, ln=ln, start=start, step_rows=dil // prev):
                    j0 = pl.multiple_of(j * PRO_ROWS, PRO_ROWS)
                    nat = _rows(r + dil * j0, PRO_ROWS, dil)
                    src = _rows(start + step_rows * j0, PRO_ROWS, step_rows)
                    dst = pl.ds(r * ln + j0, PRO_ROWS)
                    qv, kv, vv = [t[src, :] for t in temps] if from_temps else (qn[src, :], kn[src, :], v_ref[src, :])
                    q_cm[dst, :] = qv.astype(BF16)
                    k_cm[dst, :] = kv.astype(BF16)
                    v_cm[dst, :] = vv.astype(BF16)
                    do_cm[dst, :] = do_ref[nat, :].astype(BF16)
                    if keep_f32:
                        for t, val in zip(temps, (qv, kv, vv)):
                            t[dst, :] = val
                    return c

                lax.fori_loop(0, ln // PRO_ROWS, relayout, 0)

            if dil > 1:
                def clear(j, c, dk_o=dk_o, dv_o=dv_o):
                    rows = pl.ds(pl.multiple_of(j * PRO_ROWS, PRO_ROWS), PRO_ROWS)
                    dk_o[rows, :] = zero
                    dv_o[rows, :] = zero
                    return c

                lax.fori_loop(0, s // PRO_ROWS, clear, 0)

            def step(bg, c, nb=nb, dil=dil, dq_o=dq_o, dk_o=dk_o, dv_o=dv_o):
                blocks = [bg * ATT_GROUP_BWD + u for u in range(ATT_GROUP_BWD)]
                ids = [_block_ids(b, nb) for b in blocks]
                qbs = [q_cm[qrows, :] for qrows, _, _ in ids]
                dobs = [do_cm[qrows, :] for qrows, _, _ in ids]
                kbs = [_split_heads(k_cm[krows, :], is_a) for _, krows, _ in ids]
                vbs = [_split_heads(v_cm[krows, :], is_a) for _, krows, _ in ids]
                sts = [st_ref[_natural_rows(b, nb, dil), :] for b in blocks]
                scs = [_dot(qb, kb, NT) + bias[first] for qb, kb, (_, _, first) in zip(qbs, kbs, ids)]
                dps = [_dot(dob, vb, NT) for dob, vb in zip(dobs, vbs)]
                ps = [jnp.exp(sc - _pair(st[:, 0:1], st[:, HEAD_DIM:HEAD_DIM + 1])) for sc, st in zip(scs, sts)]
                dss = [(p * (dp - _pair(st[:, HALF:HALF + 1], st[:, HEAD_DIM + HALF:HEAD_DIM + HALF + 1]))).astype(BF16)
                       for p, dp, st in zip(ps, dps, sts)]
                dqs = [_dot(ds, kb, NN) for ds, kb in zip(dss, kbs)]
                dkfs = [_dot(ds, qb, TN) for ds, qb in zip(dss, qbs)]
                dvfs = [_dot(p.astype(BF16), dob, TN) for p, dob in zip(ps, dobs)]
                for (qrows, krows, _), dq, dkf, dvf in zip(ids, dqs, dkfs, dvfs):
                    dq_o[qrows, :] = dq
                    dk_o[krows, :] += jnp.where(is_a, dkf[:KEYS], dkf[KEYS:])
                    dv_o[krows, :] += jnp.where(is_a, dvf[:KEYS], dvf[KEYS:])
                return c

            lax.fori_loop(0, nblk // ATT_GROUP_BWD, step, 0)

        for level in range(len(DILATIONS) - 1, 0, -1):
            fine_d, coarse_d = DILATIONS[level - 1], DILATIONS[level]
            ratio, ln_f, ln_c = coarse_d // fine_d, s // fine_d, s // coarse_d
            for r in range(coarse_d):
                def merge(j, c, r=r, ratio=ratio, ln_c=ln_c, start=(r % fine_d) * ln_f + r // fine_d,
                          fine_bufs=results[fine_d], coarse_bufs=results[coarse_d]):
                    j0 = pl.multiple_of(j * PRO_ROWS, PRO_ROWS)
                    fine = _rows(start + ratio * j0, PRO_ROWS, ratio)
                    coarse = pl.ds(r * ln_c + j0, PRO_ROWS)
                    for f_buf, c_buf in zip(fine_bufs, coarse_bufs):
                        f_buf[fine, :] += c_buf[coarse, :]
                    return c

                lax.fori_loop(0, ln_c // PRO_ROWS, merge, 0)

        def epi(j, c):
            chunks = [pl.ds(pl.multiple_of((NORM_CHUNKS * j + u) * PRO_ROWS, PRO_ROWS), PRO_ROWS) for u in range(NORM_CHUNKS)]
            packed = [rms[rows, :] for rows in chunks]
            rs = ([jnp.where(first_half, p, pltpu.roll(p, HALF, axis=1)) for p in packed]
                  + [jnp.where(first_half, pltpu.roll(p, LANE - HALF, axis=1), p) for p in packed])
            dxs, dws = _head_norms_bwd(
                [dq_acc[rows, :] for rows in chunks] + [dk_acc[rows, :] for rows in chunks],
                [q_ref[rows, :] for rows in chunks] + [k_ref[rows, :] for rows in chunks],
                [qw_ref[...] * HEAD_DIM ** -0.5] * NORM_CHUNKS + [kw_ref[...]] * NORM_CHUNKS, rs, ones_bd)
            dqw, dkw = c
            for u, rows in enumerate(chunks):
                dq_ref[rows, :] = dxs[u].astype(BF16)
                dk_ref[rows, :] = dxs[NORM_CHUNKS + u].astype(BF16)
                dv_ref[rows, :] = dv_acc[rows, :].astype(BF16)
                dqw, dkw = dqw + dws[u], dkw + dws[NORM_CHUNKS + u]
            return dqw, dkw

        zrow = jnp.zeros((1, LANE), F32)
        dqw, dkw = lax.fori_loop(0, s // (NORM_CHUNKS * PRO_ROWS), epi, (zrow, zrow))
        dqw_ref[...] = dqw * HEAD_DIM ** -0.5
        dkw_ref[...] = dkw

    o = jax.ShapeDtypeStruct((s, ATT_D), BF16)
    ov = jax.ShapeDtypeStruct((1, ATT_D), F32)
    scr = pltpu.VMEM((s, LANE), F32)
    scb = pltpu.VMEM((s, LANE), BF16)
    return _pcall(
        body, (proj, proj, proj, do, stats, qw, kw), name="att_bwd", grid=(ATT_D // LANE,),
        in_specs=[blk(OFF_Q), blk(OFF_K), blk(OFF_V), oblk, oblk, wspec, wspec],
        out_specs=[oblk, oblk, oblk, wspec, wspec], out_shape=[o, o, o, ov, ov],
        scratch_shapes=[scr, scr, scb, scb, scb, scb] + [scr] * 10 + [pltpu.VMEM((2, ATT_BLK, 2 * KEYS), F32)],
        sem=("parallel",), comm=comm)


def _att_norm_fwd(o, nw, ycat):
    s = o.shape[0]
    row = pl.BlockSpec((ROW_TILE, ATT_D), lambda i: (i, 0))
    vec = pl.BlockSpec((1, ATT_D), lambda i: (0, 0))

    def body(o_ref, nw_ref, ycat_ref, y_ref):
        o = o_ref[...]
        r = lax.rsqrt(jnp.mean(o * o, axis=-1, keepdims=True) + EPS)
        y_ref[...] = (o * r * nw_ref[...]).astype(BF16)

    return pl.pallas_call(body, name="att_norm_fwd", grid=(s // ROW_TILE,),
                          in_specs=[row, vec, pl.BlockSpec(memory_space=pl.ANY)],
                          out_specs=pl.BlockSpec((ROW_TILE, ATT_D), lambda i: (i, 1)),
                          out_shape=jax.ShapeDtypeStruct(ycat.shape, BF16), input_output_aliases={2: 0},
                          compiler_params=_cparams(("parallel",)))(o, nw, ycat)


def _mixer_split_epilogue(dycat, first, rows, vecs, outs):
    (o_ref, lse_ref), (nw_ref,), (dyssd_ref, do_ref, st_ref, dnw_ref) = rows, vecs, outs

    @pl.when(first)
    def _():
        dnw_ref[...] = jnp.zeros_like(dnw_ref)

    dyssd_ref[...] = dycat[:, :SSD_D_INNER]
    dy = dycat[:, SSD_D_INNER:]
    o = o_ref[...]
    r = lax.rsqrt(jnp.mean(o * o, axis=-1, keepdims=True) + EPS)
    nrm = o * r
    dnw_ref[...] += jnp.sum(dy * nrm, axis=0, keepdims=True)
    dn = dy * nw_ref[...]
    do = r * (dn - nrm * jnp.mean(dn * nrm, axis=-1, keepdims=True))
    do_ref[...] = do
    ones_bd = _head_mean_matrix().astype(BF16)
    prod = do * o
    delta = jnp.concatenate([_head_sum2(prod[:, j * LANE:(j + 1) * LANE], ones_bd) for j in range(ATT_D // LANE)], axis=1)
    lane = lax.broadcasted_iota(jnp.int32, (1, ATT_D), 1)
    st_ref[...] = jnp.where((lane & (HEAD_DIM - 1)) < HALF, lse_ref[...], delta)


def _ada_fwd(c_all, w_ada):
    def body(c_ref, w_ref, o_ref):
        cv = c_ref[...]
        o_ref[...] = _dot((cv * _sigmoid(cv)).astype(BF16), w_ref[...].astype(BF16), NN)

    return pl.pallas_call(body, name="ada_fwd", out_shape=jax.ShapeDtypeStruct((c_all.shape[0], w_ada.shape[1]), F32),
                          compiler_params=_cparams())(c_all, w_ada)


def _adamw_math(g, w, m, v):
    m_new = ADAM_B1 * m + (1.0 - ADAM_B1) * g
    v_new = ADAM_B2 * v + (1.0 - ADAM_B2) * (g * g)
    m_hat = m_new / (1.0 - ADAM_B1 ** ADAM_STEP)
    v_hat = v_new / (1.0 - ADAM_B2 ** ADAM_STEP)
    delta = -ADAM_LR * (m_hat / (jnp.sqrt(v_hat) + ADAM_EPS) + ADAM_WD * w)
    return delta, m_new, v_new


def _ada_bwd_adamw(c_all, dmod_cols, w, m, v):
    rows, cols = w.shape
    tr = 256
    blk = pl.BlockSpec((tr, cols), lambda i: (i, 0))

    def body(c_ref, d_ref, w_ref, m_ref, v_ref, g_ref, dl_ref, mo_ref, vo_ref):
        cv = c_ref[...]
        ca = cv * _sigmoid(cv)
        g = ca[:, 0:1] * d_ref[0:1, :]
        for b in range(1, N_DEV):
            g = g + ca[:, b:b + 1] * d_ref[b:b + 1, :]
        g_ref[...] = g
        dl_ref[...], mo_ref[...], vo_ref[...] = _adamw_math(g, w_ref[...], m_ref[...], v_ref[...])

    o = jax.ShapeDtypeStruct((rows, cols), F32)
    return pl.pallas_call(
        body, name="ada_bwd_adamw", grid=(rows // tr,),
        in_specs=[pl.BlockSpec((tr, N_DEV), lambda i: (i, 0)), pl.BlockSpec((N_DEV, cols), lambda i: (0, 0)), blk, blk, blk],
        out_specs=[blk] * 4, out_shape=[o, o, o, o], compiler_params=_cparams(("parallel",)))(c_all.T, dmod_cols, w, m, v)


def _reduce_adamw(slabs, w, m, v, name):
    rows, cols = w.shape
    n_src = slabs.shape[0]
    if rows % 128 == 0:
        tr, steps = 128, rows // 128
        blk = pl.BlockSpec((tr, cols), lambda i: (i, 0))
        sblk = pl.BlockSpec((n_src, tr, cols), lambda i: (0, i, 0))
    else:
        tc, steps = 256, cols // 256
        blk = pl.BlockSpec((rows, tc), lambda i: (0, i))
        sblk = pl.BlockSpec((n_src, rows, tc), lambda i: (0, 0, i))

    def body(s_ref, w_ref, m_ref, v_ref, g_ref, dl_ref, mo_ref, vo_ref):
        g = s_ref[0].astype(F32)
        for src in range(1, n_src):
            g = g + s_ref[src].astype(F32)
        g_ref[...] = g
        dl_ref[...], mo_ref[...], vo_ref[...] = _adamw_math(g, w_ref[...], m_ref[...], v_ref[...])

    o = jax.ShapeDtypeStruct((rows, cols), F32)
    return pl.pallas_call(
        body, name=name, grid=(steps,), in_specs=[sblk, blk, blk, blk],
        out_specs=[blk] * 4, out_shape=[o, o, o, o], compiler_params=_cparams(("parallel",)))(slabs, w, m, v)


def _small_reduce_adamw(gathered, w, m, v):
    def body(s_ref, w_ref, m_ref, v_ref, g_ref, dl_ref, mo_ref, vo_ref):
        g = s_ref[0]
        for dev in range(1, N_DEV):
            g = g + s_ref[dev]
        g_ref[...] = g
        dl_ref[...], mo_ref[...], vo_ref[...] = _adamw_math(g, w_ref[...], m_ref[...], v_ref[...])

    o = jax.ShapeDtypeStruct(w.shape, F32)
    return pl.pallas_call(body, name="small_reduce_adamw", out_shape=[o, o, o, o], compiler_params=_cparams())(gathered, w, m, v)


def _adamw_small(g, w, m, v, name):
    def body(g_ref, w_ref, m_ref, v_ref, dl_ref, mo_ref, vo_ref):
        dl_ref[...], mo_ref[...], vo_ref[...] = _adamw_math(g_ref[...], w_ref[...], m_ref[...], v_ref[...])

    o = jax.ShapeDtypeStruct(w.shape, F32)
    return pl.pallas_call(body, name=name, out_shape=[o, o, o], compiler_params=_cparams())(g, w, m, v)


class _Exchange:
    def __init__(self, arrs, scatter):
        self.arrs, self.scatter, self.n = list(arrs), scatter, len(arrs)
        hbm = pl.BlockSpec(memory_space=pltpu.HBM)
        self.in_specs = [hbm] * self.n
        self.out_specs = [hbm] * self.n
        self.out_shape = [jax.ShapeDtypeStruct(a.shape if scatter else (N_DEV,) + a.shape, a.dtype) for a in self.arrs]
        self.scratch = [pltpu.SemaphoreType.DMA((self.n * (N_DEV - 1),)), pltpu.SemaphoreType.DMA((self.n * (N_DEV - 1),)),
                        pltpu.SemaphoreType.DMA((self.n,))]

    def _local(self, ins, outs, sems):
        me = 4 * lax.axis_index("x") + 2 * lax.axis_index("y") + lax.axis_index("c")
        return [pltpu.make_async_copy(ins[a].at[me] if self.scatter else ins[a], outs[a].at[me], sems[2].at[a])
                for a in range(self.n)]

    def _remote(self, ins, outs, sems, arriving):
        send_sems, recv_sems, _ = sems
        x, y, c = lax.axis_index("x"), lax.axis_index("y"), lax.axis_index("c")
        me = 4 * x + 2 * y + c
        remote = []
        for a in range(self.n):
            for k in range(1, N_DEV):
                px = 1 - x if k & 4 else x
                py = 1 - y if k & 2 else y
                pc = 1 - c if k & 1 else c
                peer = 4 * px + 2 * py + pc
                sem = a * (N_DEV - 1) + k - 1
                remote.append(pltpu.make_async_remote_copy(
                    src_ref=ins[a].at[peer] if self.scatter else ins[a], dst_ref=outs[a].at[peer if arriving else me],
                    send_sem=send_sems.at[sem], recv_sem=recv_sems.at[sem], device_id=(px, py, pc), device_id_type=MESH_IDS))
        return remote

    def start(self, ins, outs, sems):
        for cp in self._local(ins, outs, sems) + self._remote(ins, outs, sems, arriving=False):
            cp.start()

    def forward(self, ins, outs, sems):
        pass

    def wait(self, ins, outs, sems):
        for send, arrival in zip(self._remote(ins, outs, sems, arriving=False), self._remote(ins, outs, sems, arriving=True)):
            send.wait_send()
            arrival.wait_recv()
        for cp in self._local(ins, outs, sems):
            cp.wait()


N_CHIP = N_DEV // 2


class _SiblingSwap(_Exchange):
    def __init__(self, arrs):
        super().__init__(arrs, scatter=True)
        self.out_shape = [jax.ShapeDtypeStruct((N_CHIP,) + a.shape[2:], a.dtype) for a in self.arrs]
        self.scratch = [pltpu.SemaphoreType.DMA((self.n,)), pltpu.SemaphoreType.DMA((self.n,)), pltpu.SemaphoreType.DMA((1,))]

    def _copies(self, ins, outs, sems):
        x, y, c = lax.axis_index("x"), lax.axis_index("y"), lax.axis_index("c")
        return [pltpu.make_async_remote_copy(src_ref=ins[a].at[:, 1 - c], dst_ref=outs[a], send_sem=sems[0].at[a], recv_sem=sems[1].at[a],
                                             device_id=(x, y, 1 - c), device_id_type=MESH_IDS) for a in range(self.n)]

    def start(self, ins, outs, sems):
        for cp in self._copies(ins, outs, sems):
            cp.start()

    def wait(self, ins, outs, sems):
        for cp in self._copies(ins, outs, sems):
            cp.wait()


class _ChipScatter(_Exchange):
    def __init__(self, arrs):
        super().__init__(arrs, scatter=True)
        n_pairs = self.n * (N_CHIP - 1)
        self.scratch = [pltpu.SemaphoreType.DMA((n_pairs,)), pltpu.SemaphoreType.DMA((n_pairs,)), pltpu.SemaphoreType.DMA((self.n,))]

    def _local(self, ins, outs, sems):
        chip = 2 * lax.axis_index("x") + lax.axis_index("y")
        return [pltpu.make_async_copy(ins[a].at[chip], outs[a].at[chip], sems[2].at[a]) for a in range(self.n)]

    def _remote(self, ins, outs, sems, arriving):
        send_sems, recv_sems, _ = sems
        x, y, c = lax.axis_index("x"), lax.axis_index("y"), lax.axis_index("c")
        chip = 2 * x + y
        remote = []
        for a in range(self.n):
            for k in range(1, N_CHIP):
                px = 1 - x if k & 2 else x
                py = 1 - y if k & 1 else y
                peer = 2 * px + py
                sem = a * (N_CHIP - 1) + k - 1
                remote.append(pltpu.make_async_remote_copy(
                    src_ref=ins[a].at[peer], dst_ref=outs[a].at[peer if arriving else chip], send_sem=send_sems.at[sem],
                    recv_sem=recv_sems.at[sem], device_id=(px, py, c), device_id_type=MESH_IDS))
        return remote


def _chip_sum(mine, theirs):
    n, rows, cols = mine.shape
    blk = pl.BlockSpec((1, rows, 256), lambda q, j: (q, 0, j))

    def body(a_ref, b_ref, o_ref):
        o_ref[...] = (a_ref[...].astype(F32) + b_ref[...].astype(F32)).astype(BF16)

    return pl.pallas_call(body, name="chip_sum", grid=(n, cols // 256), in_specs=[blk, blk], out_specs=blk,
                          out_shape=jax.ShapeDtypeStruct(mine.shape, BF16),
                          compiler_params=_cparams(("parallel", "parallel")))(mine, theirs)


class _Gather2(_Exchange):
    def __init__(self, arrs):
        super().__init__(arrs, scatter=False)

    def _copies(self, ins, outs, sems):
        send_sems, recv_sems, _ = sems
        x, y, c = lax.axis_index("x"), lax.axis_index("y"), lax.axis_index("c")
        sibling = (x, y, 1 - c)
        chips = [(1 - x, y), (x, 1 - y), (1 - x, 1 - y)]
        first, passed, landed = [], [], []
        for a in range(self.n):
            def copy(k, block, to, src=None, a=a):
                slab = outs[a].at[4 * block[0] + 2 * block[1] + block[2]]
                return pltpu.make_async_remote_copy(
                    src_ref=slab if src is None else src, dst_ref=slab, send_sem=send_sems.at[a * (N_DEV - 1) + k],
                    recv_sem=recv_sems.at[a * (N_DEV - 1) + k], device_id=to, device_id_type=MESH_IDS)

            first.append(copy(0, (x, y, c), sibling, src=ins[a]))
            landed.append(copy(0, sibling, sibling))
            for j, chip in enumerate(chips):
                first.append(copy(1 + j, (x, y, c), (*chip, c), src=ins[a]))
                passed.append((copy(1 + j, (*chip, c), sibling), copy(4 + j, (*chip, c), sibling)))
                landed.append(copy(4 + j, (*chip, 1 - c), sibling))
        return first, passed, landed

    def start(self, ins, outs, sems):
        for cp in self._local(ins, outs, sems) + self._copies(ins, outs, sems)[0]:
            cp.start()

    def forward(self, ins, outs, sems):
        for arrival, onward in self._copies(ins, outs, sems)[1]:
            arrival.wait_recv()
            onward.start()

    def wait(self, ins, outs, sems):
        first, passed, landed = self._copies(ins, outs, sems)
        for arrival in landed:
            arrival.wait_recv()
        for cp in first + [onward for _, onward in passed]:
            cp.wait_send()
        for cp in self._local(ins, outs, sems):
            cp.wait()


def _split_comm_refs(refs, n_in, n_out, n_scr, comm):
    nc = comm.n if comm is not None else 0
    ns = 3 if comm is not None else 0
    pos, groups = 0, []
    for cnt in (n_in, nc, n_out, nc, n_scr, ns):
        groups.append(refs[pos:pos + cnt])
        pos += cnt
    assert pos == len(refs), (pos, len(refs))
    return groups


def _pcall(body, args, *, name, grid, in_specs, out_specs, out_shape, scratch_shapes=(), sem=None, comm=None):
    in_specs, out_specs, out_shape, scratch_shapes = list(in_specs), list(out_specs), list(out_shape), list(scratch_shapes)
    n_in, n_out, n_scr = len(in_specs), len(out_specs), len(scratch_shapes)
    if comm is None:
        kernel_body = body
    else:
        def kernel_body(*refs):
            ins, cins, outs, couts, scr, sems = _split_comm_refs(refs, n_in, n_out, n_scr, comm)
            ids = [pl.program_id(a) for a in range(len(grid))]
            first, last = ids[0] == 0, ids[0] == grid[0] - 1
            for a in range(1, len(grid)):
                first, last = first & (ids[a] == 0), last & (ids[a] == grid[a] - 1)

            @pl.when(first)
            def _():
                comm.start(cins, couts, sems)

            @pl.when(last)
            def _():
                comm.forward(cins, couts, sems)

            body(*ins, *outs, *scr)

            @pl.when(last)
            def _():
                comm.wait(cins, couts, sems)

        in_specs, out_specs, out_shape = in_specs + comm.in_specs, out_specs + comm.out_specs, out_shape + comm.out_shape
        scratch_shapes, args = scratch_shapes + comm.scratch, list(args) + comm.arrs
        sem = ("arbitrary",) * len(grid)
    res = pl.pallas_call(kernel_body, name=name, grid=grid, in_specs=in_specs, out_specs=out_specs, out_shape=out_shape,
                         scratch_shapes=scratch_shapes, compiler_params=_cparams(sem))(*args)
    return res[:n_out], res[n_out:]


def _exchange(arrs, name, scatter=False, ex=None):
    if ex is None:
        ex = _Exchange(arrs, scatter=True) if scatter else _Gather2(arrs)

    def body(*refs):
        _, ins, _, outs, _, sems = _split_comm_refs(refs, 0, 0, 0, ex)
        ex.start(ins, outs, sems)
        ex.forward(ins, outs, sems)
        ex.wait(ins, outs, sems)

    return pl.pallas_call(body, name=name, in_specs=ex.in_specs, out_specs=ex.out_specs, out_shape=ex.out_shape,
                          scratch_shapes=ex.scratch)(*ex.arrs)


def _pad_lanes(v, width=LANE):
    return jnp.pad(v, ((0, 0), (0, width - v.shape[1])))


def _shards_to_cols(g):
    return jnp.transpose(g, (1, 0, 2)).reshape(g.shape[1], N_DEV * g.shape[2])


def _local_step(x, tgt, mod, w_in_pt, conv_w, conv_b, dt_bias, a_log, d_skip, ssd_norm_w, q_norm_w, k_norm_w,
                attn_norm_w, w_out_sh, w_ff1_sh, w_ff2_sh, norm1_w, norm2_w, core):
    shift1, scale1, gate1, shift2, scale2, gate2 = [mod[i:i + 1] for i in range(N_MOD)]
    dtb, alog, dsk = _pad_lanes(dt_bias), _pad_lanes(a_log), _pad_lanes(d_skip)
    qw, kw = jnp.tile(q_norm_w, (1, ATT_HEADS)), jnp.tile(k_norm_w, (1, ATT_HEADS))

    h1 = _norm_mod_fwd(x, norm1_w, scale1, shift1, "norm1_fwd")
    proj, (w_ff1_g,) = _matmul(h1, w_in_pt, tb=True, tm=2048, tn=896, tk=1024, name="in_proj", comm=_Gather2([w_ff1_sh]))
    pre, act = _conv_fwd(proj, conv_w, conv_b)
    ypre, ycat_ssd, hall = _ssd_fwd(proj, act, dtb, alog, dsk, ssd_norm_w)
    (o_att, lse), (w_out_g, w_ff2_g) = _att_fwd(proj, qw, kw, comm=_Gather2([w_out_sh, w_ff2_sh]))
    w_out = w_out_g.reshape(2 * D_MODEL, D_MODEL)
    w_ff1 = _shards_to_cols(w_ff1_g)
    w_ff2 = w_ff2_g.reshape(D_FF, D_MODEL)
    ycat = _att_norm_fwd(o_att, attn_norm_w, ycat_ssd)
    row32, row16, vec32 = ("row", F32), ("row", BF16), ("vec", F32)
    mix, x1, h2 = _matmul_rows(ycat, w_out, _residual_norm_epilogue, [x], [gate1, norm2_w, scale2, shift2],
                               [row32, row32, row16], tm=512, name="out_proj")
    u, act_ff = _matmul(h2, w_ff1, tm=1024, tn=2048, tk=1024, name="ff1", mode="relu2")
    loss, dout, dff, dgate2 = _matmul_rows(act_ff, w_ff2, _loss_epilogue, [x1, tgt], [gate2],
                                           [("one", F32), row32, row16, vec32], tm=512, name="ff2")

    du = _matmul(dff, w_ff2, tb=True, tm=512, tn=4096, tk=1024, out_dtype=BF16, name="ff2_dx", mode="drelu2", u=u)
    g_ff2 = _matmul(act_ff, dff, ta=True, tm=512, tn=1024, tk=4096, out_dtype=BF16, name="ff2_dw")
    dx1, dshift2, dscale2, g_norm2, dmix, dgate1 = _matmul_rows(
        du, w_ff1, _norm_bwd_epilogue, [x1, dout, mix], [norm2_w, scale2, gate1],
        [row32, vec32, vec32, vec32, row16, vec32], tb=True, tm=512, name="ff1_dx")
    g_ff1 = _matmul(h2, du, ta=True, tm=1024, tn=D_FF // N_DEV, tk=4096, out_dtype=BF16, name="ff1_dw", shard_out=True)

    dy_ssd, do, stats, g_attn_norm = _matmul_rows(
        dmix, w_out, _mixer_split_epilogue, [o_att, lse], [attn_norm_w],
        [("row", F32, SSD_D_INNER), ("row", F32, ATT_D), ("row", F32, ATT_D), ("vec", F32, ATT_D)], tb=True, tm=512, name="out_proj_dx")
    g_out = _matmul(ycat, dmix, ta=True, tm=512, tn=1024, tk=4096, out_dtype=BF16, name="out_proj_dw")
    ff_slabs = [g_ff1, g_ff2.reshape(N_DEV, D_FF // N_DEV, D_MODEL)]
    (dq, dk, dv, dqw, dkw), (s_ff1, s_ff2) = _att_bwd(proj, do, stats, qw, kw, comm=_Exchange(ff_slabs, scatter=True))
    out_slabs = [g_out.astype(BF16).reshape(N_DEV, 2 * D_MODEL // N_DEV, D_MODEL)]
    (dz, dact, ddtr, da, g_dsk, g_dtb, g_ssd_norm), (s_out,) = _ssd_bwd(
        dy_ssd, ypre, proj, act, hall, dtb, alog, dsk, ssd_norm_w, comm=_Exchange(out_slabs, scatter=True))
    dxbc, g_conv_w, g_conv_b = _conv_bwd(dact, pre, proj, conv_w)
    dproj = [(dz, OFF_Z), (dxbc, OFF_XBC), (ddtr, OFF_DT), (dq, OFF_Q), (dk, OFF_K), (dv, OFF_V)]
    g_head, g_tail = _pieces_t_matmul([[dz, dxbc], [dq, dk, dv]], h1, tm=256, name="in_proj_dw")
    g_dt = _matmul(ddtr, h1, ta=True, tm=LANE, tn=1024, tk=4096, out_dtype=BF16, name="in_proj_dw_dt")[:SSD_HEADS]
    in_slabs = jnp.concatenate([g_head, g_dt, g_tail], axis=0).reshape(N_CHIP, 2, IN_W // N_DEV, D_MODEL)
    (sibling_slabs,) = _exchange(None, "swap_w_in_grads", ex=_SiblingSwap([in_slabs]))
    chip_slabs = _chip_sum(lax.dynamic_index_in_dim(in_slabs, core, axis=1, keepdims=False), sibling_slabs)
    (grad_x, dshift1, dscale1, g_norm1), (s_in,) = _matmul_rows(
        dproj, w_in_pt, _norm_bwd_epilogue, [x, dx1], [norm1_w, scale1], [row32, vec32, vec32, vec32],
        tm=256, name="in_proj_dx", comm=_ChipScatter([chip_slabs]))

    dmod = jnp.concatenate([dshift1, dscale1, dgate1, dshift2, dscale2, dgate2], axis=0)
    g_alog = da[:, :SSD_HEADS] * (-jnp.exp(a_log))
    g_qw = dqw.reshape(ATT_HEADS, HEAD_DIM).sum(axis=0, keepdims=True)
    g_kw = dkw.reshape(ATT_HEADS, HEAD_DIM).sum(axis=0, keepdims=True)
    return dict(loss=loss, grad_x=grad_x, dmod=dmod, norm1_w=g_norm1, norm2_w=g_norm2, w_in=s_in, conv_w=g_conv_w,
                conv_b=g_conv_b, dt_bias=g_dtb[:, :SSD_HEADS], a_log=g_alog, d_skip=g_dsk[:, :SSD_HEADS],
                ssd_norm_w=g_ssd_norm, q_norm_w=g_qw, k_norm_w=g_kw, attn_norm_w=g_attn_norm, w_out=s_out,
                w_ff1=s_ff1, w_ff2=s_ff2)


def _pack_w_in_rows(wt_full):
    cut = OFF_DT + SSD_HEADS
    pad = jnp.zeros((LANE - SSD_HEADS, wt_full.shape[1]), wt_full.dtype)
    return jnp.concatenate([wt_full[:cut], pad, wt_full[cut:]], axis=0)


MISC_FIELDS = (("dt_bias", SSD_HEADS), ("a_log", SSD_HEADS), ("d_skip", SSD_HEADS), ("q_norm_w", HEAD_DIM), ("k_norm_w", HEAD_DIM),
               ("loss", 1))
SMALL_LAYOUT = (("b_ada", 6), ("norm1_w", 1), ("norm2_w", 1), ("conv_w", 8), ("conv_b", 2), ("ssd_norm_w", 1),
                ("attn_norm_w", 1), ("misc", 1))


def _pack_small(vals):
    rows = []
    for name, nrow in SMALL_LAYOUT:
        if name == "misc":
            misc = jnp.concatenate([vals[f].reshape(1, n) if f in vals else jnp.zeros((1, n), F32) for f, n in MISC_FIELDS], axis=1)
            rows.append(_pad_lanes(misc, D_MODEL))
        elif name in vals:
            rows.append(vals[name].reshape(nrow, D_MODEL))
        else:
            rows.append(jnp.zeros((nrow, D_MODEL), F32))
    used = sum(n for _, n in SMALL_LAYOUT)
    rows.append(jnp.zeros((SMALL_ROWS - used, D_MODEL), F32))
    return jnp.concatenate(rows, axis=0)


def _unpack_small(packed):
    out, r = {}, 0
    for name, nrow in SMALL_LAYOUT:
        blk = packed[r:r + nrow]
        r += nrow
        if name == "misc":
            c0 = 0
            for f, n in MISC_FIELDS:
                out[f] = blk[:, c0:c0 + n]
                c0 += n
        elif name == "b_ada":
            out[name] = blk.reshape(1, N_MOD * D_MODEL)
        elif name == "conv_w":
            out[name] = blk.reshape(CONV_K, CONV_CH)
        elif name == "conv_b":
            out[name] = blk.reshape(1, CONV_CH)
        else:
            out[name] = blk
    return out


WEIGHT_NAMES = ("norm1_w", "norm2_w", "w_ada", "b_ada", "w_in", "conv_w", "conv_b", "dt_bias", "a_log", "d_skip",
                "ssd_norm_w", "q_norm_w", "k_norm_w", "attn_norm_w", "w_out", "w_ff1", "w_ff2")
SMALL_NAMES = ("norm1_w", "norm2_w", "b_ada", "conv_b", "dt_bias", "a_log", "d_skip", "ssd_norm_w", "q_norm_w",
               "k_norm_w", "attn_norm_w")


def kernel(x, c, norm1_w, norm2_w, w_ada, b_ada, w_in, conv_w, conv_b, dt_bias, a_log, d_skip, ssd_norm_w, q_norm_w, k_norm_w, attn_norm_w, w_out, w_ff1, w_ff2, loss_target, m_norm1_w, m_norm2_w, m_w_ada, m_b_ada, m_w_in, m_conv_w, m_conv_b, m_dt_bias, m_a_log, m_d_skip, m_ssd_norm_w, m_q_norm_w, m_k_norm_w, m_attn_norm_w, m_w_out, m_w_ff1, m_w_ff2, v_norm1_w, v_norm2_w, v_w_ada, v_b_ada, v_w_in, v_conv_w, v_conv_b, v_dt_bias, v_a_log, v_d_skip, v_ssd_norm_w, v_q_norm_w, v_k_norm_w, v_attn_norm_w, v_w_out, v_w_ff1, v_w_ff2):
    args = dict(locals())
    w = {n: args[n] for n in WEIGHT_NAMES}
    m = {n: args["m_" + n] for n in WEIGHT_NAMES}
    v = {n: args["v_" + n] for n in WEIGHT_NAMES}
    me = 4 * lax.axis_index("x") + 2 * lax.axis_index("y") + lax.axis_index("c")

    c_rows = jnp.pad(c, ((0, 7), (0, 0)))
    w_in_t, m_in_t, v_in_t = [jnp.transpose(t["w_in"][0]) for t in (w, m, v)]
    c_g, conv_g, w_in_g = _exchange([c_rows, w["conv_w"][0], w_in_t.astype(BF16)], "gather_w_in", scatter=False)
    c_all = c_g[:, 0, :]
    conv_full = _shards_to_cols(conv_g)
    w_in_pt = _pack_w_in_rows(w_in_g.reshape(IN_W, D_MODEL))

    mod_part = _ada_fwd(c_all, w["w_ada"][0])
    (mod_g,) = _exchange([mod_part], "gather_mod", scatter=False)
    mod_mine = lax.dynamic_index_in_dim(mod_g, me, axis=1, keepdims=False).reshape(1, N_MOD * D_MODEL) + w["b_ada"]
    mod = mod_mine.reshape(N_MOD, D_MODEL)

    res = _local_step(x[0], loss_target[0], mod, w_in_pt, conv_full, w["conv_b"], w["dt_bias"], w["a_log"], w["d_skip"],
                      w["ssd_norm_w"], w["q_norm_w"], w["k_norm_w"], w["attn_norm_w"], w["w_out"][0].astype(BF16),
                      w["w_ff1"][0].astype(BF16), w["w_ff2"][0].astype(BF16), w["norm1_w"], w["norm2_w"], lax.axis_index("c"))

    small_vals = {n: res[n] for n in SMALL_NAMES if n != "b_ada"}
    small_vals["b_ada"] = res["dmod"]
    small_vals["conv_w"] = res["conv_w"]
    small_vals["loss"] = res["loss"]
    (small_g,) = _exchange([_pack_small(small_vals)], "gather_small", scatter=False)

    grads, delta, new_m, new_v = {}, {}, {}, {}
    for name in ("w_out", "w_ff1", "w_ff2"):
        outs = _reduce_adamw(res[name], w[name][0], m[name][0], v[name][0], "adamw_" + name)
        grads[name], delta[name], new_m[name], new_v[name] = [o[None] for o in outs]
    outs = _reduce_adamw(res["w_in"], w_in_t, m_in_t, v_in_t, "adamw_w_in")
    grads["w_in"], delta["w_in"], new_m["w_in"], new_v["w_in"] = [jnp.transpose(o)[None] for o in outs]

    sm = _small_reduce_adamw(small_g, _pack_small({n: w[n] for n in SMALL_NAMES}), _pack_small({n: m[n] for n in SMALL_NAMES}),
                             _pack_small({n: v[n] for n in SMALL_NAMES}))
    sm = [_unpack_small(p) for p in sm]
    for n in SMALL_NAMES:
        grads[n], delta[n], new_m[n], new_v[n] = [p[n] for p in sm]
    shard_w = CONV_CH // N_DEV
    g_conv = lax.dynamic_slice_in_dim(sm[0]["conv_w"], me * shard_w, shard_w, axis=1)
    cw = _adamw_small(g_conv, w["conv_w"][0], m["conv_w"][0], v["conv_w"][0], "adamw_conv_w")
    grads["conv_w"] = g_conv[None]
    delta["conv_w"], new_m["conv_w"], new_v["conv_w"] = [o[None] for o in cw]

    ada_w = w_ada.shape[2]
    dmod_all = small_g[:, :N_MOD, :].reshape(N_DEV, N_MOD * D_MODEL)
    dmod_cols = lax.dynamic_slice_in_dim(dmod_all, me * ada_w, ada_w, axis=1)
    outs = _ada_bwd_adamw(c_all, dmod_cols, w["w_ada"][0], m["w_ada"][0], v["w_ada"][0])
    grads["w_ada"], delta["w_ada"], new_m["w_ada"], new_v["w_ada"] = [o[None] for o in outs]

    loss = sm[0]["loss"][0, 0]
    return (loss, res["grad_x"][None], *[grads[n] for n in WEIGHT_NAMES], *[delta[n] for n in WEIGHT_NAMES],
            *[new_m[n] for n in WEIGHT_NAMES], *[new_v[n] for n in WEIGHT_NAMES])
```

```python
import jax
import jax.numpy as jnp
from jax import lax
from jax.experimental import pallas as pl
from jax.experimental.pallas import tpu as pltpu

F32 = jnp.float32
BF16 = jnp.bfloat16
HIGHEST = lax.Precision.HIGHEST
MESH_IDS = pl.DeviceIdType.MESH

N_DEV = 8
D_MODEL = 1024
HEAD_DIM = 64
SSD_HEADS = 16
SSD_GROUPS = 4
HEADS_PER_GROUP = SSD_HEADS // SSD_GROUPS
SSD_STATE = 128
SSD_CHUNK = 128
SSD_D_INNER = SSD_HEADS * HEAD_DIM
GROUP_WIDTH = SSD_D_INNER // SSD_GROUPS
CONV_K = 4
CONV_CH = SSD_D_INNER + 2 * SSD_GROUPS * SSD_STATE
ATT_HEADS = 16
ATT_D = ATT_HEADS * HEAD_DIM
ATT_BLK = 128
DILATIONS = (1, 4, 16)
D_FF = 4 * D_MODEL
N_MOD = 6
EPS = 1e-6
IN_W = SSD_D_INNER + CONV_CH + SSD_HEADS + 3 * ATT_D
LANE = 128
OFF_Z, OFF_XBC, OFF_DT = 0, SSD_D_INNER, SSD_D_INNER + CONV_CH
OFF_Q = OFF_DT + LANE
OFF_K, OFF_V = OFF_Q + ATT_D, OFF_Q + 2 * ATT_D
IN_WP = OFF_V + ATT_D

ADAM_LR, ADAM_B1, ADAM_B2, ADAM_EPS, ADAM_WD, ADAM_STEP = 0.001, 0.9, 0.999, 1e-08, 0.01, 10
VMEM_LIMIT = 60 * 1024 * 1024
ROW_TILE = 512
SMALL_ROWS = 24


def _cparams(sem=None):
    return pltpu.CompilerParams(dimension_semantics=sem, vmem_limit_bytes=VMEM_LIMIT)


def _sigmoid(v):
    return 1.0 / (1.0 + jnp.exp(-v))


def _softplus(v):
    y = jnp.exp(-jnp.abs(v))
    small = y * (1.0 - y * (0.5 - y * (1.0 / 3.0)))
    return jnp.maximum(v, 0.0) + jnp.where(y < 0.01, small, jnp.log(1.0 + y))


def _dot(a, b, dims, precision=None):
    return lax.dot_general(a, b, (dims, ((), ())), preferred_element_type=F32, precision=precision)


NN = ((1,), (0,))
NT = ((1,), (1,))
TN = ((0,), (0,))


def _matmul(a, b, *, ta=False, tb=False, tm, tn, tk, out_dtype=F32, name, mode=None, u=None, comm=None, shard_out=False):
    m, k = (a.shape[1], a.shape[0]) if ta else a.shape
    n = b.shape[0] if tb else b.shape[1]
    assert m % tm == 0 and n % tn == 0 and k % tk == 0, (name, m, n, k)
    nk = k // tk
    a_spec = pl.BlockSpec((tk, tm), lambda i, j, kk: (kk, i)) if ta else pl.BlockSpec((tm, tk), lambda i, j, kk: (i, kk))
    b_spec = pl.BlockSpec((tn, tk), lambda i, j, kk: (j, kk)) if tb else pl.BlockSpec((tk, tn), lambda i, j, kk: (kk, j))
    o_spec = pl.BlockSpec((tm, tn), lambda i, j, kk: (i, j))
    dims = ((0,) if ta else (1,), (1,) if tb else (0,))
    n_out = 2 if mode == "relu2" else 1

    def body(*refs):
        if mode == "drelu2":
            a_ref, b_ref, u_ref = refs[:3]
            rest = refs[3:]
        else:
            a_ref, b_ref = refs[:2]
            u_ref = None
            rest = refs[2:]
        outs = rest[:n_out]
        part = _dot(a_ref[...], b_ref[...], dims)

        def finish(r):
            if mode == "relu2":
                outs[0][...] = r.astype(BF16)
                rr = jnp.maximum(r, 0.0)
                outs[1][...] = (rr * rr).astype(BF16)
            elif mode == "drelu2":
                outs[0][...] = (r * (2.0 * jnp.maximum(u_ref[...].astype(F32), 0.0))).astype(out_dtype)
            else:
                outs[0][...] = r.astype(out_dtype)

        if nk == 1:
            finish(part)
        else:
            acc = rest[n_out]
            kk = pl.program_id(2)

            @pl.when(kk == 0)
            def _():
                acc[...] = part

            @pl.when(kk > 0)
            def _():
                acc[...] += part

            @pl.when(kk == nk - 1)
            def _():
                finish(acc[...])

    in_specs = [a_spec, b_spec]
    args = [a, b]
    if mode == "drelu2":
        in_specs.append(o_spec)
        args.append(u)
    if mode == "relu2":
        out_shape = [jax.ShapeDtypeStruct((m, n), BF16), jax.ShapeDtypeStruct((m, n), BF16)]
    elif shard_out:
        out_shape = [jax.ShapeDtypeStruct((n // tn, m, tn), out_dtype)]
        o_spec = pl.BlockSpec((None, tm, tn), lambda i, j, kk: (j, i, 0))
    else:
        out_shape = [jax.ShapeDtypeStruct((m, n), out_dtype)]
    outs, comm_outs = _pcall(
        body, args, name=name, grid=(m // tm, n // tn, nk), in_specs=in_specs, out_specs=[o_spec] * n_out,
        out_shape=out_shape, scratch_shapes=[pltpu.VMEM((tm, tn), F32)] if nk > 1 else [],
        sem=("parallel", "parallel", "arbitrary"), comm=comm)
    res = tuple(outs) if mode == "relu2" else outs[0]
    return res if comm is None else (res, comm_outs)


def _pieces_t_matmul(groups, b, *, tm, name):
    k, n = b.shape
    pieces = [p for g in groups for p in g]
    starts, tiles = [], 0
    for p in pieces:
        assert p.shape[0] == k and p.shape[1] % tm == 0, (name, p.shape)
        starts.append(tiles)
        tiles += p.shape[1] // tm
    group_of, group_start, group_tiles = [], [], []
    for gi, g in enumerate(groups):
        group_start.append(starts[len(group_of)])
        group_of += [gi] * len(g)
        group_tiles.append(sum(p.shape[1] // tm for p in g))

    def clipped(block, start, count):
        return pl.BlockSpec(block, (lambda i: (0, jnp.clip(i - start, 0, count - 1))) if block[0] == k
                            else (lambda i: (jnp.clip(i - start, 0, count - 1), 0)))

    def body(*refs):
        a_refs, b_ref, o_refs = refs[:len(pieces)], refs[len(pieces)], refs[len(pieces) + 1:]
        i = pl.program_id(0)
        for a_ref, start, p, gi in zip(a_refs, starts, pieces, group_of):
            @pl.when((i >= start) & (i < start + p.shape[1] // tm))
            def _(a_ref=a_ref, o_ref=o_refs[gi]):
                o_ref[...] = _dot(a_ref[...], b_ref[...], TN).astype(BF16)

    return pl.pallas_call(
        body, name=name, grid=(tiles,),
        in_specs=[clipped((k, tm), s0, p.shape[1] // tm) for s0, p in zip(starts, pieces)] + [pl.BlockSpec((k, n), lambda i: (0, 0))],
        out_specs=[clipped((tm, n), s0, cnt) for s0, cnt in zip(group_start, group_tiles)],
        out_shape=[jax.ShapeDtypeStruct((cnt * tm, n), BF16) for cnt in group_tiles],
        compiler_params=_cparams(("arbitrary",)))(*pieces, b)


def _rms_mod(xv, nw, scale, shift):
    r = lax.rsqrt(jnp.mean(xv * xv, axis=-1, keepdims=True) + EPS)
    return ((xv * r) * nw * (1.0 + scale) + shift).astype(BF16)


def _norm_mod_fwd(x, nw, scale, shift, name):
    s, d = x.shape
    row = pl.BlockSpec((ROW_TILE, d), lambda i: (i, 0))
    vec = pl.BlockSpec((1, d), lambda i: (0, 0))

    def body(x_ref, nw_ref, sc_ref, sh_ref, h_ref):
        h_ref[...] = _rms_mod(x_ref[...], nw_ref[...], sc_ref[...], sh_ref[...])

    return pl.pallas_call(body, name=name, grid=(s // ROW_TILE,), in_specs=[row, vec, vec, vec], out_specs=row,
                          out_shape=jax.ShapeDtypeStruct((s, d), BF16), compiler_params=_cparams(("parallel",)))(x, nw, scale, shift)


def _matmul_rows(a, b, epilogue, row_in, vec_in, outs, *, tb=False, tm, name, comm=None):
    pieces = a if isinstance(a, list) else [(a, 0)]
    assert not (tb and len(pieces) > 1)
    m = pieces[0][0].shape[0]
    n = b.shape[0] if tb else b.shape[1]
    assert m % tm == 0, (name, m, tm)
    dims = ((1,), (1,) if tb else (0,))
    n_a, n_row, n_vec = len(pieces), len(row_in), len(vec_in)

    def body(*refs):
        a_refs, b_ref, rest = refs[:n_a], refs[n_a], refs[n_a + 1:]
        if n_a == 1:
            c = _dot(a_refs[0][...], b_ref[...], dims)
        else:
            c = None
            for a_ref, (piece, off) in zip(a_refs, pieces):
                part = _dot(a_ref[...], b_ref[off:off + piece.shape[1], :], dims)
                c = part if c is None else c + part
        epilogue(c, pl.program_id(0) == 0, rest[:n_row], rest[n_row:n_row + n_vec], rest[n_row + n_vec:])

    def spec(kind, width):
        block = {"row": (tm, width), "vec": (1, width), "one": (1, 1)}[kind]
        return pl.BlockSpec(block, (lambda i: (i, 0)) if kind == "row" else (lambda i: (0, 0)))

    def shape(kind, width):
        return {"row": (m, width), "vec": (1, width), "one": (1, 1)}[kind]

    outs = [(o[0], o[1], o[2] if len(o) > 2 else n) for o in outs]
    res, comm_outs = _pcall(
        body, [*[p for p, _ in pieces], b, *row_in, *vec_in], name=name, grid=(m // tm,),
        in_specs=[spec("row", p.shape[1]) for p, _ in pieces] + [pl.BlockSpec(b.shape, lambda i: (0, 0))]
        + [spec("row", r.shape[1]) for r in row_in] + [spec("vec", v.shape[1]) for v in vec_in],
        out_specs=[spec(kind, width) for kind, _, width in outs],
        out_shape=[jax.ShapeDtypeStruct(shape(kind, width), dt) for kind, dt, width in outs],
        sem=("arbitrary",), comm=comm)
    return res if comm is None else (res, comm_outs)


def _residual_norm_epilogue(mix, first, rows, vecs, outs):
    (x_ref,), (gate_ref, nw_ref, sc_ref, sh_ref), (mix_ref, x1_ref, h_ref) = rows, vecs, outs
    xv = x_ref[...] + gate_ref[...] * mix
    mix_ref[...] = mix
    x1_ref[...] = xv
    h_ref[...] = _rms_mod(xv, nw_ref[...], sc_ref[...], sh_ref[...])


def _loss_epilogue(ff, first, rows, vecs, outs):
    (x1_ref, t_ref), (g_ref,), (loss_ref, dout_ref, dff_ref, dg_ref) = rows, vecs, outs
    d = ff.shape[1]

    @pl.when(first)
    def _():
        loss_ref[...] = jnp.zeros_like(loss_ref)
        dg_ref[...] = jnp.zeros_like(dg_ref)

    err = x1_ref[...] + g_ref[...] * ff - t_ref[...]
    loss_ref[...] += (0.5 / d) * jnp.sum(err * err).reshape(1, 1)
    dout = err * (1.0 / d)
    dout_ref[...] = dout
    dff_ref[...] = (g_ref[...] * dout).astype(BF16)
    dg_ref[...] += jnp.sum(dout * ff, axis=0, keepdims=True)


def _norm_bwd_epilogue(dh, first, rows, vecs, outs):
    with_gate = len(vecs) == 3
    x_ref, dres_ref = rows[:2]
    nw_ref, sc_ref = vecs[:2]
    dx_ref, dsh_ref, dsc_ref, dnw_ref = outs[:4]

    @pl.when(first)
    def _():
        for ref in outs[1:4] + outs[5:]:
            ref[...] = jnp.zeros_like(ref)

    xv = x_ref[...]
    r = lax.rsqrt(jnp.mean(xv * xv, axis=-1, keepdims=True) + EPS)
    nrm = xv * r
    one_sc = 1.0 + sc_ref[...]
    dhn = dh * nrm
    dsh_ref[...] += jnp.sum(dh, axis=0, keepdims=True)
    dsc_ref[...] += jnp.sum(dhn, axis=0, keepdims=True) * nw_ref[...]
    dnw_ref[...] += jnp.sum(dhn, axis=0, keepdims=True) * one_sc
    dn = dh * (nw_ref[...] * one_sc)
    dx = dres_ref[...] + r * (dn - nrm * jnp.mean(dn * nrm, axis=-1, keepdims=True))
    dx_ref[...] = dx
    if with_gate:
        outs[4][...] = (vecs[2][...] * dx).astype(BF16)
        outs[5][...] += jnp.sum(dx * rows[2][...], axis=0, keepdims=True)


CONV_COLS = 256
CONV_FWD_ROWS = 2048
CONV_BWD_ROWS = 1024
CONV_SUB_ROWS = 128
HALO = 8


def _shift_down(cur, halo, k):
    if k == 0:
        return cur
    rolled = pltpu.roll(cur, k, axis=0)
    top = jnp.where(lax.broadcasted_iota(jnp.int32, halo.shape, 0) < k, pltpu.roll(halo, k, axis=0), rolled[:HALO])
    return jnp.concatenate([top, rolled[HALO:]], axis=0)


def _shift_up(cur, halo, k):
    if k == 0:
        return cur
    t = cur.shape[0]
    rolled = pltpu.roll(cur, t - k, axis=0)
    bot = jnp.where(lax.broadcasted_iota(jnp.int32, halo.shape, 0) >= HALO - k, pltpu.roll(halo, HALO - k, axis=0),
                    rolled[t - HALO:])
    return jnp.concatenate([rolled[:t - HALO], bot], axis=0)


def _conv_fwd(proj, conv_w, conv_b):
    s = proj.shape[0]
    nr = s // CONV_FWD_ROWS
    cb0 = OFF_XBC // CONV_COLS
    hb = CONV_FWD_ROWS // HALO
    cur = pl.BlockSpec((CONV_FWD_ROWS, CONV_COLS), lambda j, r: (r, cb0 + j))
    prev = pl.BlockSpec((HALO, CONV_COLS), lambda j, r: (jnp.maximum(r * hb - 1, 0), cb0 + j))
    out = pl.BlockSpec((CONV_FWD_ROWS, CONV_COLS), lambda j, r: (r, j))

    def body(u_ref, up_ref, w_ref, b_ref, pre_ref, act_ref):
        r = pl.program_id(1)
        for c in range(CONV_FWD_ROWS // CONV_SUB_ROWS):
            rows = slice(c * CONV_SUB_ROWS, (c + 1) * CONV_SUB_ROWS)
            u = u_ref[rows, :]
            halo = u_ref[c * CONV_SUB_ROWS - HALO:c * CONV_SUB_ROWS, :] if c > 0 else jnp.where(r > 0, up_ref[...], 0.0)
            acc = b_ref[...] + w_ref[CONV_K - 1:CONV_K, :] * u
            for k in range(1, CONV_K):
                acc = acc + w_ref[CONV_K - 1 - k:CONV_K - k, :] * _shift_down(u, halo, k)
            pre_ref[rows, :] = acc
            act_ref[rows, :] = acc * _sigmoid(acc)

    return pl.pallas_call(
        body, name="conv_fwd", grid=(CONV_CH // CONV_COLS, nr),
        in_specs=[cur, prev, pl.BlockSpec((CONV_K, CONV_COLS), lambda j, r: (0, j)),
                  pl.BlockSpec((1, CONV_COLS), lambda j, r: (0, j))],
        out_specs=[out, out],
        out_shape=[jax.ShapeDtypeStruct((s, CONV_CH), F32), jax.ShapeDtypeStruct((s, CONV_CH), F32)],
        compiler_params=_cparams(("parallel", "arbitrary")))(proj, proj, conv_w, conv_b)


def _conv_bwd(dact, pre, proj, conv_w):
    s = proj.shape[0]
    nr = s // CONV_BWD_ROWS
    cb0 = OFF_XBC // CONV_COLS
    hb = CONV_BWD_ROWS // HALO
    last_halo = s // HALO - 1
    n_sub = CONV_BWD_ROWS // CONV_SUB_ROWS
    cur = pl.BlockSpec((CONV_BWD_ROWS, CONV_COLS), lambda j, r: (r, j))
    nxt = pl.BlockSpec((HALO, CONV_COLS), lambda j, r: (jnp.minimum((r + 1) * hb, last_halo), j))
    ucur = pl.BlockSpec((CONV_BWD_ROWS, CONV_COLS), lambda j, r: (r, cb0 + j))
    wspec = pl.BlockSpec((CONV_K, CONV_COLS), lambda j, r: (0, j))
    bspec = pl.BlockSpec((1, CONV_COLS), lambda j, r: (0, j))

    def dsilu(p):
        sg = _sigmoid(p)
        return sg * (1.0 + p * (1.0 - sg))

    def body(da_ref, dan_ref, pre_ref, pren_ref, u_ref, w_ref, du_ref, dw_ref, db_ref):
        r = pl.program_id(1)

        @pl.when(r == 0)
        def _():
            dw_ref[...] = jnp.zeros_like(dw_ref)
            db_ref[...] = jnp.zeros_like(db_ref)

        dws = [jnp.zeros((1, CONV_COLS), F32) for _ in range(CONV_K)]
        db = jnp.zeros((1, CONV_COLS), F32)
        for c in range(n_sub):
            rows = slice(c * CONV_SUB_ROWS, (c + 1) * CONV_SUB_ROWS)
            ahead = slice((c + 1) * CONV_SUB_ROWS, (c + 1) * CONV_SUB_ROWS + HALO)
            dpre = da_ref[rows, :] * dsilu(pre_ref[rows, :])
            if c < n_sub - 1:
                dnext = da_ref[ahead, :] * dsilu(pre_ref[ahead, :])
            else:
                dnext = jnp.where(r < nr - 1, dan_ref[...] * dsilu(pren_ref[...]), 0.0)
            u = u_ref[rows, :]
            du = w_ref[CONV_K - 1:CONV_K, :] * dpre
            dws[0] = dws[0] + jnp.sum(dpre * u, axis=0, keepdims=True)
            for k in range(1, CONV_K):
                ahead_k = _shift_up(dpre, dnext, k)
                du = du + w_ref[CONV_K - 1 - k:CONV_K - k, :] * ahead_k
                dws[k] = dws[k] + jnp.sum(ahead_k * u, axis=0, keepdims=True)
            du_ref[rows, :] = du.astype(BF16)
            db = db + jnp.sum(dpre, axis=0, keepdims=True)
        dw_ref[...] += jnp.concatenate(dws[::-1], axis=0)
        db_ref[...] += db

    return pl.pallas_call(
        body, name="conv_bwd", grid=(CONV_CH // CONV_COLS, nr),
        in_specs=[cur, nxt, cur, nxt, ucur, wspec],
        out_specs=[cur, wspec, bspec],
        out_shape=[jax.ShapeDtypeStruct((s, CONV_CH), BF16), jax.ShapeDtypeStruct((CONV_K, CONV_CH), F32),
                   jax.ShapeDtypeStruct((1, CONV_CH), F32)],
        compiler_params=_cparams(("parallel", "arbitrary")))(dact, dact, pre, pre, proj, conv_w)


def _ssd_common(dtr, dtb, alog):
    lane = lax.broadcasted_iota(jnp.int32, (1, LANE), 1)
    head_lane = lane < SSD_HEADS
    dt = jnp.where(head_lane, _softplus(dtr + dtb), 0.0)
    a = jnp.where(head_lane, -jnp.exp(alog), 0.0)
    row = lax.broadcasted_iota(jnp.int32, (SSD_CHUNK, SSD_CHUNK), 0)
    col = lax.broadcasted_iota(jnp.int32, (SSD_CHUNK, SSD_CHUNK), 1)
    tril = row >= col
    cs = _dot(tril.astype(F32), dt * a, NN, precision=HIGHEST)
    return dt, a, cs, cs.T, tril, lane


def _split_bf16(v, passes):
    terms, rest = [], v
    for _ in range(passes):
        t = rest.astype(BF16)
        terms.append(t)
        rest = rest - t.astype(F32)
    return terms


def _dot_split(v, m, dims, passes):
    terms = _split_bf16(v, passes)
    if passes == 1:
        return _dot(terms[0], m, dims)
    return _dot(jnp.concatenate(terms, axis=1), jnp.concatenate([m] * passes, axis=0 if dims == NN else 1), dims)


def _ssd_constants():
    heads = jnp.arange(LANE)[:, None]
    exp_mat = (heads == (jnp.arange(SSD_D_INNER)[None, :] // HEAD_DIM)).astype(BF16)
    ind4 = ((jnp.arange(SSD_HEADS * SSD_CHUNK)[:, None] // SSD_CHUNK) == jnp.arange(LANE)[None, :]).astype(BF16)
    return exp_mat, ind4


def _expand_heads(v):
    return jnp.repeat(v[:, :SSD_HEADS], HEAD_DIM, axis=1)


def _ssd_prep(dtr, dtb, alog, exp_mat):
    dt, a, cs, cst, tril, lane = _ssd_common(dtr, dtb, alog)
    return dt, a, cs, cst, tril, lane, _dot_split(dt, exp_mat, NN, 2), _dot_split(cs, exp_mat, NN, 3)


def _chunk_decay_rows(cs, g):
    parts = []
    for e in range(HEADS_PER_GROUP):
        h = g * HEADS_PER_GROUP + e
        parts.append(jnp.broadcast_to(jnp.exp(cs[SSD_CHUNK - 1:SSD_CHUNK, h:h + 1]), (HEAD_DIM, SSD_STATE)))
    return jnp.concatenate(parts, axis=0)


def _ssd_fwd(proj, act, dtb, alog, dsk, nw):
    s = proj.shape[0]
    nc = s // SSD_CHUNK
    bc_w = SSD_GROUPS * SSD_STATE
    exp_mat, _ = _ssd_constants()

    def body(z_ref, dtr_ref, xs_ref, b_ref, c_ref, dtb_ref, alog_ref, dskx_ref, nw_ref, exp_ref,
             ypre_ref, yssd_ref, hall_ref, h_scr):
        @pl.when(pl.program_id(0) == 0)
        def _():
            h_scr[...] = jnp.zeros_like(h_scr)

        dt, a, cs, cst, tril, lane, dtx, csx = _ssd_prep(dtr_ref[...], dtb_ref[...], alog_ref[...], exp_ref[...])
        cs_last_x = csx[SSD_CHUNK - 1:SSD_CHUNK, :]
        xs = xs_ref[...]
        xdt = xs * dtx
        xdtb = xdt.astype(BF16)
        xdec = (xdt * jnp.exp(cs_last_x - csx)).astype(BF16)
        ecsx = jnp.exp(csx)
        head_of_lane = lax.broadcasted_iota(jnp.int32, (1, GROUP_WIDTH), 1) // HEAD_DIM
        for g in range(SSD_GROUPS):
            gs = slice(g * GROUP_WIDTH, (g + 1) * GROUP_WIDTH)
            bg = b_ref[:, g * SSD_STATE:(g + 1) * SSD_STATE].astype(BF16)
            cg = c_ref[:, g * SSD_STATE:(g + 1) * SSD_STATE].astype(BF16)
            cb = _dot(cg, bg, NT)
            hprev = h_scr[gs, :]
            hall_ref[0, gs, :] = hprev
            gms, rhs = [], []
            xg = xdtb[:, gs]
            for e in range(HEADS_PER_GROUP):
                h = g * HEADS_PER_GROUP + e
                lm = jnp.exp(jnp.where(tril, cs[:, h:h + 1] - cst[h:h + 1, :], -1e30))
                gms.append((cb * lm).astype(BF16))
                rhs.append(jnp.where(head_of_lane == e, xg, jnp.zeros_like(xg)))
            y = _dot(jnp.concatenate(gms, axis=1), jnp.concatenate(rhs, axis=0), NN)
            y = y + ecsx[:, gs] * _dot(cg, hprev.astype(BF16), NT)
            y = y + dskx_ref[:, gs] * xs[:, gs]
            h_scr[gs, :] = hprev * _chunk_decay_rows(cs, g) + _dot(xdec[:, gs], bg, TN)
            ypre_ref[:, gs] = y
            z = z_ref[:, gs]
            yg = y * (z * _sigmoid(z))
            r = lax.rsqrt(jnp.mean(yg * yg, axis=-1, keepdims=True) + EPS)
            yssd_ref[:, gs] = (yg * r * nw_ref[:, gs]).astype(BF16)

    row_d = lambda cb: pl.BlockSpec((SSD_CHUNK, SSD_D_INNER), lambda c: (c, cb))
    small = pl.BlockSpec((1, LANE), lambda c: (0, 0))
    wide = pl.BlockSpec((1, SSD_D_INNER), lambda c: (0, 0))
    return pl.pallas_call(
        body, name="ssd_fwd", grid=(nc,),
        in_specs=[row_d(OFF_Z // SSD_D_INNER),
                  pl.BlockSpec((SSD_CHUNK, LANE), lambda c: (c, OFF_DT // LANE)),
                  row_d(0),
                  pl.BlockSpec((SSD_CHUNK, bc_w), lambda c: (c, SSD_D_INNER // bc_w)),
                  pl.BlockSpec((SSD_CHUNK, bc_w), lambda c: (c, SSD_D_INNER // bc_w + 1)),
                  small, small, wide, wide, pl.BlockSpec((LANE, SSD_D_INNER), lambda c: (0, 0))],
        out_specs=[row_d(0), row_d(0), pl.BlockSpec((1, SSD_D_INNER, SSD_STATE), lambda c: (c, 0, 0))],
        out_shape=[jax.ShapeDtypeStruct((s, SSD_D_INNER), F32), jax.ShapeDtypeStruct((s, SSD_D_INNER + ATT_D), BF16),
                   jax.ShapeDtypeStruct((nc, SSD_D_INNER, SSD_STATE), F32)],
        scratch_shapes=[pltpu.VMEM((SSD_D_INNER, SSD_STATE), F32)],
        compiler_params=_cparams(("arbitrary",)))(proj, proj, act, act, act, dtb, alog, _expand_heads(dsk), nw, exp_mat)


def _ssd_bwd(dycat, ypre, proj, act, hall, dtb, alog, dsk, nw, comm=None):
    s = proj.shape[0]
    nc = s // SSD_CHUNK
    bc_w = SSD_GROUPS * SSD_STATE

    exp_mat, ind4 = _ssd_constants()
    seg_passes = 1

    def body(dy_ref, ypre_ref, z_ref, dtr_ref, xs_ref, b_ref, c_ref, hall_ref, dtb_ref, alog_ref, dskx_ref, nw_ref,
             exp_ref, ind4_ref, dz_ref, dact_ref, ddtr_ref, da_ref, ddsk_ref, ddtb_ref, dnw_ref, dh_scr):
        @pl.when(pl.program_id(0) == 0)
        def _():
            dh_scr[...] = jnp.zeros_like(dh_scr)
            da_ref[...] = jnp.zeros_like(da_ref)
            ddsk_ref[...] = jnp.zeros_like(ddsk_ref)
            ddtb_ref[...] = jnp.zeros_like(ddtb_ref)
            dnw_ref[...] = jnp.zeros_like(dnw_ref)

        dtr = dtr_ref[...]
        dt, a, cs, cst, tril, lane, dtx, csx = _ssd_prep(dtr, dtb_ref[...], alog_ref[...], exp_ref[...])
        cs_last_x = csx[SSD_CHUNK - 1:SSD_CHUNK, :]
        xs = xs_ref[...]
        xdt = xs * dtx
        xdtb = xdt.astype(BF16)
        decx = jnp.exp(cs_last_x - csx)
        xdecf = xdt * decx
        xdec = xdecf.astype(BF16)
        ecsx = jnp.exp(csx)
        head_of_lane = lax.broadcasted_iota(jnp.int32, (1, GROUP_WIDTH), 1) // HEAD_DIM
        last_row = lax.broadcasted_iota(jnp.int32, (SSD_CHUNK, 1), 0) == SSD_CHUNK - 1
        dcs_col = jnp.zeros((SSD_CHUNK, LANE), F32)
        dcs_row = jnp.zeros((SSD_CHUNK, LANE), F32)
        ddt = jnp.zeros((SSD_CHUNK, LANE), F32)
        ddsk = jnp.zeros((1, LANE), F32)
        hsum = jnp.zeros((1, LANE), F32)
        t1_sum = jnp.zeros((1, LANE), F32)
        for g in range(SSD_GROUPS):
            gs = slice(g * GROUP_WIDTH, (g + 1) * GROUP_WIDTH)
            bsl = slice(g * SSD_STATE, (g + 1) * SSD_STATE)
            exp_g = exp_ref[:, gs]
            ind4_g = ind4_ref[g * HEADS_PER_GROUP * SSD_CHUNK:(g + 1) * HEADS_PER_GROUP * SSD_CHUNK, :]
            z = z_ref[:, gs]
            sg = _sigmoid(z)
            sz = z * sg
            ypre = ypre_ref[:, gs]
            yg = ypre * sz
            r = lax.rsqrt(jnp.mean(yg * yg, axis=-1, keepdims=True) + EPS)
            nrm = yg * r
            dyo_n = dy_ref[:, gs]
            dnw_ref[:, gs] += jnp.sum(dyo_n * nrm, axis=0, keepdims=True)
            dn = dyo_n * nw_ref[:, gs]
            dyg = r * (dn - nrm * jnp.mean(dn * nrm, axis=-1, keepdims=True))
            dz_ref[:, gs] = (dyg * ypre * (sg * (1.0 + z * (1.0 - sg)))).astype(BF16)
            dy = dyg * sz

            bg = b_ref[:, bsl].astype(BF16)
            cg = c_ref[:, bsl].astype(BF16)
            cb = _dot(cg, bg, NT)
            hprev = hall_ref[0, gs, :]
            hb = hprev.astype(BF16)
            dhn = dh_scr[gs, :]
            dhb = dhn.astype(BF16)
            xs_g, xdt_g = xs[:, gs], xdtb[:, gs]
            w_off = _dot(cg, hb, NT)
            dyo = dy * ecsx[:, gs]
            dyob = dyo.astype(BF16)
            dcg = _dot(dyob, hb, NN)
            dh_y = _dot(dyob, cg, TN)
            r_st = _dot(bg, dhb, NT)
            dbg = _dot(xdec[:, gs], dhb, NN)
            dyb = dy.astype(BF16)
            gms, gmbs, lms, dys = [], [], [], []
            for e in range(HEADS_PER_GROUP):
                h = g * HEADS_PER_GROUP + e
                lm = jnp.exp(jnp.where(tril, cs[:, h:h + 1] - cst[h:h + 1, :], -1e30))
                gm = cb * lm
                lms.append(lm)
                gms.append(gm)
                gmbs.append(gm.astype(BF16))
                dys.append(jnp.where(head_of_lane == e, dyb, jnp.zeros_like(dyb)))
            dxdt = _dot(jnp.concatenate(gmbs, axis=0), jnp.concatenate(dys, axis=0), TN) + decx[:, gs] * r_st
            dcb = jnp.zeros((SSD_CHUNK, SSD_CHUNK), F32)
            mms = []
            for e in range(HEADS_PER_GROUP):
                dg = _dot(dys[e], xdt_g, NT)
                mms.append(dg * gms[e])
                dcb = dcb + dg * lms[e]
            seg = _dot_split(jnp.concatenate([dyo * w_off, xdecf[:, gs] * r_st, dxdt * xs_g, dy * xs_g], axis=0), exp_g, NT, seg_passes)
            v1, t1, ddt_g, dsk_g = [seg[i * SSD_CHUNK:(i + 1) * SSD_CHUNK] for i in range(4)]
            dcs_col = dcs_col + v1 - t1 + _dot_split(jnp.concatenate(mms, axis=1), ind4_g, NN, seg_passes)
            for t in _split_bf16(jnp.concatenate(mms, axis=0), seg_passes):
                dcs_row = dcs_row + _dot(ind4_g, t, TN)
            ddt = ddt + ddt_g
            ddsk = ddsk + jnp.sum(dsk_g, axis=0, keepdims=True)
            t1_sum = t1_sum + jnp.sum(t1, axis=0, keepdims=True)
            for e in range(HEADS_PER_GROUP):
                h = g * HEADS_PER_GROUP + e
                hs = slice(e * HEAD_DIM, (e + 1) * HEAD_DIM)
                hsum = hsum + jnp.where(lane == h, jnp.sum(dhn[hs, :] * hprev[hs, :]).reshape(1, 1), 0.0)
            dh_scr[gs, :] = dhn * _chunk_decay_rows(cs, g) + dh_y
            dcbb = dcb.astype(BF16)
            dact_ref[:, gs] = dxdt * dtx[:, gs] + dskx_ref[:, gs] * dy
            dact_ref[:, SSD_D_INNER + g * SSD_STATE:SSD_D_INNER + (g + 1) * SSD_STATE] = dbg + _dot(dcbb, cg, TN)
            dact_ref[:, SSD_D_INNER + bc_w + g * SSD_STATE:SSD_D_INNER + bc_w + (g + 1) * SSD_STATE] = dcg + _dot(dcbb, bg, NN)
        dlast = t1_sum + jnp.exp(cs[SSD_CHUNK - 1:SSD_CHUNK, :]) * hsum
        dcs = dcs_col - dcs_row.T + jnp.where(last_row, dlast, 0.0)
        row = lax.broadcasted_iota(jnp.int32, (SSD_CHUNK, SSD_CHUNK), 0)
        col = lax.broadcasted_iota(jnp.int32, (SSD_CHUNK, SSD_CHUNK), 1)
        dda = _dot((col >= row).astype(F32), dcs, NN, precision=HIGHEST)
        ddt = ddt + dda * a
        da_ref[...] += jnp.sum(dda * dt, axis=0, keepdims=True)
        ddtr = jnp.where(lane < SSD_HEADS, ddt * _sigmoid(dtr + dtb_ref[...]), 0.0)
        ddtr_ref[...] = ddtr.astype(BF16)
        ddtb_ref[...] += jnp.sum(ddtr, axis=0, keepdims=True)
        ddsk_ref[...] += ddsk

    rev = lambda c: nc - 1 - c
    row_d = lambda cb: pl.BlockSpec((SSD_CHUNK, SSD_D_INNER), lambda c: (rev(c), cb))
    small = pl.BlockSpec((1, LANE), lambda c: (0, 0))
    wide = pl.BlockSpec((1, SSD_D_INNER), lambda c: (0, 0))
    small_shape = jax.ShapeDtypeStruct((1, LANE), F32)
    return _pcall(
        body, (dycat, ypre, proj, proj, act, act, act, hall, dtb, alog, _expand_heads(dsk), nw, exp_mat, ind4),
        name="ssd_bwd", grid=(nc,),
        in_specs=[row_d(0), row_d(0), row_d(OFF_Z // SSD_D_INNER),
                  pl.BlockSpec((SSD_CHUNK, LANE), lambda c: (rev(c), OFF_DT // LANE)),
                  row_d(0),
                  pl.BlockSpec((SSD_CHUNK, bc_w), lambda c: (rev(c), SSD_D_INNER // bc_w)),
                  pl.BlockSpec((SSD_CHUNK, bc_w), lambda c: (rev(c), SSD_D_INNER // bc_w + 1)),
                  pl.BlockSpec((1, SSD_D_INNER, SSD_STATE), lambda c: (rev(c), 0, 0)),
                  small, small, wide, wide, pl.BlockSpec((LANE, SSD_D_INNER), lambda c: (0, 0)),
                  pl.BlockSpec((SSD_HEADS * SSD_CHUNK, LANE), lambda c: (0, 0))],
        out_specs=[row_d(0), pl.BlockSpec((SSD_CHUNK, CONV_CH), lambda c: (rev(c), 0)),
                   pl.BlockSpec((SSD_CHUNK, LANE), lambda c: (rev(c), 0)), small, small, small, wide],
        out_shape=[jax.ShapeDtypeStruct((s, SSD_D_INNER), BF16), jax.ShapeDtypeStruct((s, CONV_CH), F32),
                   jax.ShapeDtypeStruct((s, LANE), BF16), small_shape, small_shape, small_shape,
                   jax.ShapeDtypeStruct((1, SSD_D_INNER), F32)],
        scratch_shapes=[pltpu.VMEM((SSD_D_INNER, SSD_STATE), F32)], sem=("arbitrary",), comm=comm)


def _head_mean_matrix():
    row = lax.broadcasted_iota(jnp.int32, (LANE, LANE), 0) // HEAD_DIM
    col = lax.broadcasted_iota(jnp.int32, (LANE, LANE), 1) // HEAD_DIM
    return (row == col).astype(F32)


def _head_sum2(v, ones_bd):
    hi = v.astype(BF16)
    lo = (v - hi.astype(F32)).astype(BF16)
    return _dot(jnp.concatenate([hi, lo], axis=1), jnp.concatenate([ones_bd, ones_bd], axis=0), NN)


def _head_norms(xs, ws, ones_bd):
    sums = [_head_sum2(x * x, ones_bd) for x in xs]
    rs = [lax.rsqrt(ms * (1.0 / HEAD_DIM) + EPS) for ms in sums]
    return [(x * r) * w for x, r, w in zip(xs, rs, ws)], rs


def _head_norms_bwd(dns, xs, ws, rs, ones_bd):
    nrms = [x * r for x, r in zip(xs, rs)]
    dnws = [dn * w for dn, w in zip(dns, ws)]
    projs = [_head_sum2(dnw * nrm, ones_bd) for dnw, nrm in zip(dnws, nrms)]
    dxs = [r * (dnw - nrm * (pr * (1.0 / HEAD_DIM))) for r, dnw, nrm, pr in zip(rs, dnws, nrms, projs)]
    return dxs, [jnp.sum(dn * nrm, axis=0, keepdims=True) for dn, nrm in zip(dns, nrms)]


NORM_CHUNKS = 4


PRO_ROWS = 256
ATT_GROUP_FWD = 32
ATT_GROUP_BWD = 8
KEYS = 2 * ATT_BLK
NEG = -1e30
HALF = HEAD_DIM // 2


def _rows(start, size, dil):
    return pl.ds(start, size) if dil == 1 else pl.ds(start, size, stride=dil)


def _fill_bias(bias_ref):
    row = lax.broadcasted_iota(jnp.int32, (ATT_BLK, 2 * KEYS), 0)
    col = lax.broadcasted_iota(jnp.int32, (ATT_BLK, 2 * KEYS), 1) & (KEYS - 1)
    for first, off in ((0, 0), (1, ATT_BLK)):
        dist = off + row - col
        bias_ref[first] = jnp.where((dist >= 0) & (dist <= ATT_BLK), 0.0, NEG)


def _pair(a, b):
    return jnp.concatenate([jnp.broadcast_to(a, (ATT_BLK, KEYS)), jnp.broadcast_to(b, (ATT_BLK, KEYS))], axis=1)


def _split_heads(x, is_a):
    zero = jnp.zeros_like(x)
    return jnp.concatenate([jnp.where(is_a, x, zero), jnp.where(is_a, zero, x)], axis=0)


def _block_ids(b, nb):
    i = b & (nb - 1)
    q0 = pl.multiple_of(b * ATT_BLK, ATT_BLK)
    k0 = pl.multiple_of((b - jnp.minimum(i, 1)) * ATT_BLK, ATT_BLK)
    return pl.ds(q0, ATT_BLK), pl.ds(k0, KEYS), jnp.minimum(i, 1)


def _natural_rows(b, nb, dil):
    if dil == 1:
        return pl.ds(pl.multiple_of(b * ATT_BLK, ATT_BLK), ATT_BLK)
    return pl.ds(b // nb + dil * ((b & (nb - 1)) * ATT_BLK), ATT_BLK, stride=dil)


def _att_fwd(proj, qw, kw, comm=None):
    s = proj.shape[0]
    nblk = s // ATT_BLK
    assert all((s // d) // ATT_BLK >= 2 for d in DILATIONS)
    blk = lambda off: pl.BlockSpec((s, LANE), lambda i: (0, off // LANE + i))
    wspec = pl.BlockSpec((1, LANE), lambda i: (0, i))
    oblk = pl.BlockSpec((s, LANE), lambda i: (0, i))

    def body(q_ref, k_ref, v_ref, qw_ref, kw_ref, o_ref, lse_ref, qn, kn, q_cm, k_cm, v_cm, m_acc, l_acc, o_d, m_d, l_d,
             o_e, m_e, l_e, tq, tk, tv, bias):
        ones_bd = _head_mean_matrix().astype(BF16)
        is_a = lax.broadcasted_iota(jnp.int32, (1, LANE), 1) < HEAD_DIM
        ones_ext = _split_heads(jnp.ones((KEYS, LANE), BF16), is_a)
        _fill_bias(bias)

        def pro(j, c):
            chunks = [pl.ds(pl.multiple_of((NORM_CHUNKS * j + u) * PRO_ROWS, PRO_ROWS), PRO_ROWS) for u in range(NORM_CHUNKS)]
            normed, _ = _head_norms([q_ref[rows, :] for rows in chunks] + [k_ref[rows, :] for rows in chunks],
                                    [qw_ref[...] * HEAD_DIM ** -0.5] * NORM_CHUNKS + [kw_ref[...]] * NORM_CHUNKS, ones_bd)
            for u, rows in enumerate(chunks):
                qn[rows, :] = normed[u]
                kn[rows, :] = normed[NORM_CHUNKS + u]
            return c

        lax.fori_loop(0, s // (NORM_CHUNKS * PRO_ROWS), pro, 0)

        results = dict(zip(DILATIONS, ((o_ref, m_acc, l_acc), (o_d, m_d, l_d), (o_e, m_e, l_e))))
        for dil in DILATIONS:
            ln = s // dil
            nb = ln // ATT_BLK
            o_out, m_out, l_out = results[dil]
            level = DILATIONS.index(dil)
            keep_f32 = 0 < level < len(DILATIONS) - 1
            from_temps = level >= 2
            step_rows = dil // DILATIONS[level - 1] if from_temps else dil
            for r in range(dil):
                prev = DILATIONS[level - 1] if from_temps else 1
                start = (r % prev) * (s // prev) + r // prev if from_temps else r

                def relayout(j, c, r=r, ln=ln, start=start, step_rows=step_rows, keep_f32=keep_f32, from_temps=from_temps):
                    j0 = pl.multiple_of(j * PRO_ROWS, PRO_ROWS)
                    src = _rows(start + step_rows * j0, PRO_ROWS, step_rows)
                    dst = pl.ds(r * ln + j0, PRO_ROWS)
                    qv, kv, vv = (tq[src, :], tk[src, :], tv[src, :]) if from_temps else (qn[src, :], kn[src, :], v_ref[src, :])
                    q_cm[dst, :] = qv.astype(BF16)
                    k_cm[dst, :] = kv.astype(BF16)
                    v_cm[dst, :] = vv.astype(BF16)
                    if keep_f32:
                        tq[dst, :] = qv
                        tk[dst, :] = kv
                        tv[dst, :] = vv
                    return c

                lax.fori_loop(0, ln // PRO_ROWS, relayout, 0)

            def step(bg, c, nb=nb, o_out=o_out, m_out=m_out, l_out=l_out):
                ids = [_block_ids(bg * ATT_GROUP_FWD + u, nb) for u in range(ATT_GROUP_FWD)]
                kbs = [_split_heads(k_cm[krows, :], is_a) for _, krows, _ in ids]
                scs = [_dot(q_cm[qrows, :], kb, NT) + bias[first] for (qrows, _, first), kb in zip(ids, kbs)]
                mas = [jnp.max(sc[:, :KEYS], axis=-1, keepdims=True) for sc in scs]
                mbs = [jnp.max(sc[:, KEYS:], axis=-1, keepdims=True) for sc in scs]
                ps = [jnp.exp(sc - _pair(ma, mb)).astype(BF16) for sc, ma, mb in zip(scs, mas, mbs)]
                vbs = [jnp.concatenate([_split_heads(v_cm[krows, :], is_a), ones_ext], axis=1) for _, krows, _ in ids]
                ols = [_dot(p, vb, NN) for p, vb in zip(ps, vbs)]
                for (qrows, _, _), ol, ma, mb in zip(ids, ols, mas, mbs):
                    o_out[qrows, :] = ol[:, :LANE]
                    l_out[qrows, :] = ol[:, LANE:]
                    m_out[qrows, :] = jnp.where(is_a, ma, mb)
                return c

            lax.fori_loop(0, nblk // ATT_GROUP_FWD, step, 0)

        for level in range(len(DILATIONS) - 1, 0, -1):
            fine_d, coarse_d = DILATIONS[level - 1], DILATIONS[level]
            ratio, ln_f, ln_c = coarse_d // fine_d, s // fine_d, s // coarse_d
            (o_f, m_f, l_f), (o_c, m_c, l_c) = results[fine_d], results[coarse_d]
            for r in range(coarse_d):
                def merge(j, c, r=r, ratio=ratio, ln_c=ln_c, start=(r % fine_d) * ln_f + r // fine_d,
                          o_f=o_f, m_f=m_f, l_f=l_f, o_c=o_c, m_c=m_c, l_c=l_c):
                    j0 = pl.multiple_of(j * PRO_ROWS, PRO_ROWS)
                    fine = _rows(start + ratio * j0, PRO_ROWS, ratio)
                    coarse = pl.ds(r * ln_c + j0, PRO_ROWS)
                    m_old, m_new = m_f[fine, :], m_c[coarse, :]
                    m = jnp.maximum(m_old, m_new)
                    a_old, a_new = jnp.exp(m_old - m), jnp.exp(m_new - m)
                    o_f[fine, :] = a_old * o_f[fine, :] + a_new * o_c[coarse, :]
                    l_f[fine, :] = a_old * l_f[fine, :] + a_new * l_c[coarse, :]
                    m_f[fine, :] = m
                    return c

                lax.fori_loop(0, ln_c // PRO_ROWS, merge, 0)

        def epi(j, c):
            rows = pl.ds(pl.multiple_of(j * PRO_ROWS, PRO_ROWS), PRO_ROWS)
            l = l_acc[rows, :]
            o_ref[rows, :] = o_ref[rows, :] / l
            lse_ref[rows, :] = m_acc[rows, :] + jnp.log(l)
            return c

        lax.fori_loop(0, s // PRO_ROWS, epi, 0)

    f = jax.ShapeDtypeStruct((s, ATT_D), F32)
    scr = pltpu.VMEM((s, LANE), F32)
    scb = pltpu.VMEM((s, LANE), BF16)
    return _pcall(
        body, (proj, proj, proj, qw, kw), name="att_fwd", grid=(ATT_D // LANE,),
        in_specs=[blk(OFF_Q), blk(OFF_K), blk(OFF_V), wspec, wspec], out_specs=[oblk, oblk], out_shape=[f, f],
        scratch_shapes=[scr, scr, scb, scb, scb] + [scr] * 11 + [pltpu.VMEM((2, ATT_BLK, 2 * KEYS), F32)],
        sem=("parallel",), comm=comm)


def _att_bwd(proj, do, stats, qw, kw, comm=None):
    s = proj.shape[0]
    nblk = s // ATT_BLK
    blk = lambda off: pl.BlockSpec((s, LANE), lambda i: (0, off // LANE + i))
    wspec = pl.BlockSpec((1, LANE), lambda i: (0, i))
    oblk = pl.BlockSpec((s, LANE), lambda i: (0, i))

    def body(q_ref, k_ref, v_ref, do_ref, st_ref, qw_ref, kw_ref, dq_ref, dk_ref, dv_ref, dqw_ref, dkw_ref,
             qn, kn, q_cm, do_cm, k_cm, v_cm, rms, dq_acc, dk_acc, dv_acc, dq_d, dk_d, dv_d, dq_e, dk_e, dv_e, bias):
        ones_bd = _head_mean_matrix().astype(BF16)
        is_a = lax.broadcasted_iota(jnp.int32, (1, LANE), 1) < HEAD_DIM
        first_half = (lax.broadcasted_iota(jnp.int32, (1, LANE), 1) & (HEAD_DIM - 1)) < HALF
        _fill_bias(bias)
        zero = jnp.zeros((PRO_ROWS, LANE), F32)
        results = dict(zip(DILATIONS, ((dq_acc, dk_acc, dv_acc), (dq_d, dk_d, dv_d), (dq_e, dk_e, dv_e))))

        def pro(j, c):
            chunks = [pl.ds(pl.multiple_of((NORM_CHUNKS * j + u) * PRO_ROWS, PRO_ROWS), PRO_ROWS) for u in range(NORM_CHUNKS)]
            normed, rs = _head_norms([q_ref[rows, :] for rows in chunks] + [k_ref[rows, :] for rows in chunks],
                                     [qw_ref[...] * HEAD_DIM ** -0.5] * NORM_CHUNKS + [kw_ref[...]] * NORM_CHUNKS, ones_bd)
            for u, rows in enumerate(chunks):
                qn[rows, :] = normed[u]
                kn[rows, :] = normed[NORM_CHUNKS + u]
                rms[rows, :] = jnp.where(first_half, rs[u], rs[NORM_CHUNKS + u])
                dk_acc[rows, :] = zero
                dv_acc[rows, :] = zero
            return c

        lax.fori_loop(0, s // (NORM_CHUNKS * PRO_ROWS), pro, 0)

        for dil in DILATIONS:
            ln = s // dil
            nb = ln // ATT_BLK
            dq_o, dk_o, dv_o = results[dil]
            level = DILATIONS.index(dil)
            keep_f32 = 0 < level < len(DILATIONS) - 1
            from_temps = level >= 2
            temps = results[DILATIONS[-1]]
            for r in range(dil):
                prev = DILATIONS[level - 1] if from_temps else 1
                start = (r % prev) * (s // prev) + r // prev if from_temps else r

                def relayout(j, c, dil=dil, r=r, ln=ln, start=start, step_rows=dil // prev):
                    j0 = pl.multiple_of(j * PRO_ROWS, PRO_ROWS)
                    nat = _rows(r + dil * j0, PRO_ROWS, dil)
                    src = _rows(start + step_rows * j0, PRO_ROWS, step_rows)
                    dst = pl.ds(r * ln + j0, PRO_ROWS)
                    qv, kv, vv = [t[src, :] for t in temps] if from_temps else (qn[src, :], kn[src, :], v_ref[src, :])
                    q_cm[dst, :] = qv.astype(BF16)
                    k_cm[dst, :] = kv.astype(BF16)
                    v_cm[dst, :] = vv.astype(BF16)
                    do_cm[dst, :] = do_ref[nat, :].astype(BF16)
                    if keep_f32:
                        for t, val in zip(temps, (qv, kv, vv)):
                            t[dst, :] = val
                    return c

                lax.fori_loop(0, ln // PRO_ROWS, relayout, 0)

            if dil > 1:
                def clear(j, c, dk_o=dk_o, dv_o=dv_o):
                    rows = pl.ds(pl.multiple_of(j * PRO_ROWS, PRO_ROWS), PRO_ROWS)
                    dk_o[rows, :] = zero
                    dv_o[rows, :] = zero
                    return c

                lax.fori_loop(0, s // PRO_ROWS, clear, 0)

            def step(bg, c, nb=nb, dil=dil, dq_o=dq_o, dk_o=dk_o, dv_o=dv_o):
                blocks = [bg * ATT_GROUP_BWD + u for u in range(ATT_GROUP_BWD)]
                ids = [_block_ids(b, nb) for b in blocks]
                qbs = [q_cm[qrows, :] for qrows, _, _ in ids]
                dobs = [do_cm[qrows, :] for qrows, _, _ in ids]
                kbs = [_split_heads(k_cm[krows, :], is_a) for _, krows, _ in ids]
                vbs = [_split_heads(v_cm[krows, :], is_a) for _, krows, _ in ids]
                sts = [st_ref[_natural_rows(b, nb, dil), :] for b in blocks]
                scs = [_dot(qb, kb, NT) + bias[first] for qb, kb, (_, _, first) in zip(qbs, kbs, ids)]
                dps = [_dot(dob, vb, NT) for dob, vb in zip(dobs, vbs)]
                ps = [jnp.exp(sc - _pair(st[:, 0:1], st[:, HEAD_DIM:HEAD_DIM + 1])) for sc, st in zip(scs, sts)]
                dss = [(p * (dp - _pair(st[:, HALF:HALF + 1], st[:, HEAD_DIM + HALF:HEAD_DIM + HALF + 1]))).astype(BF16)
                       for p, dp, st in zip(ps, dps, sts)]
                dqs = [_dot(ds, kb, NN) for ds, kb in zip(dss, kbs)]
                dkfs = [_dot(ds, qb, TN) for ds, qb in zip(dss, qbs)]
                dvfs = [_dot(p.astype(BF16), dob, TN) for p, dob in zip(ps, dobs)]
                for (qrows, krows, _), dq, dkf, dvf in zip(ids, dqs, dkfs, dvfs):
                    dq_o[qrows, :] = dq
                    dk_o[krows, :] += jnp.where(is_a, dkf[:KEYS], dkf[KEYS:])
                    dv_o[krows, :] += jnp.where(is_a, dvf[:KEYS], dvf[KEYS:])
                return c

            lax.fori_loop(0, nblk // ATT_GROUP_BWD, step, 0)

        for level in range(len(DILATIONS) - 1, 0, -1):
            fine_d, coarse_d = DILATIONS[level - 1], DILATIONS[level]
            ratio, ln_f, ln_c = coarse_d // fine_d, s // fine_d, s // coarse_d
            for r in range(coarse_d):
                def merge(j, c, r=r, ratio=ratio, ln_c=ln_c, start=(r % fine_d) * ln_f + r // fine_d,
                          fine_bufs=results[fine_d], coarse_bufs=results[coarse_d]):
                    j0 = pl.multiple_of(j * PRO_ROWS, PRO_ROWS)
                    fine = _rows(start + ratio * j0, PRO_ROWS, ratio)
                    coarse = pl.ds(r * ln_c + j0, PRO_ROWS)
                    for f_buf, c_buf in zip(fine_bufs, coarse_bufs):
                        f_buf[fine, :] += c_buf[coarse, :]
                    return c

                lax.fori_loop(0, ln_c // PRO_ROWS, merge, 0)

        def epi(j, c):
            chunks = [pl.ds(pl.multiple_of((NORM_CHUNKS * j + u) * PRO_ROWS, PRO_ROWS), PRO_ROWS) for u in range(NORM_CHUNKS)]
            packed = [rms[rows, :] for rows in chunks]
            rs = ([jnp.where(first_half, p, pltpu.roll(p, HALF, axis=1)) for p in packed]
                  + [jnp.where(first_half, pltpu.roll(p, LANE - HALF, axis=1), p) for p in packed])
            dxs, dws = _head_norms_bwd(
                [dq_acc[rows, :] for rows in chunks] + [dk_acc[rows, :] for rows in chunks],
                [q_ref[rows, :] for rows in chunks] + [k_ref[rows, :] for rows in chunks],
                [qw_ref[...] * HEAD_DIM ** -0.5] * NORM_CHUNKS + [kw_ref[...]] * NORM_CHUNKS, rs, ones_bd)
            dqw, dkw = c
            for u, rows in enumerate(chunks):
                dq_ref[rows, :] = dxs[u].astype(BF16)
                dk_ref[rows, :] = dxs[NORM_CHUNKS + u].astype(BF16)
                dv_ref[rows, :] = dv_acc[rows, :].astype(BF16)
                dqw, dkw = dqw + dws[u], dkw + dws[NORM_CHUNKS + u]
            return dqw, dkw

        zrow = jnp.zeros((1, LANE), F32)
        dqw, dkw = lax.fori_loop(0, s // (NORM_CHUNKS * PRO_ROWS), epi, (zrow, zrow))
        dqw_ref[...] = dqw * HEAD_DIM ** -0.5
        dkw_ref[...] = dkw

    o = jax.ShapeDtypeStruct((s, ATT_D), BF16)
    ov = jax.ShapeDtypeStruct((1, ATT_D), F32)
    scr = pltpu.VMEM((s, LANE), F32)
    scb = pltpu.VMEM((s, LANE), BF16)
    return _pcall(
        body, (proj, proj, proj, do, stats, qw, kw), name="att_bwd", grid=(ATT_D // LANE,),
        in_specs=[blk(OFF_Q), blk(OFF_K), blk(OFF_V), oblk, oblk, wspec, wspec],
        out_specs=[oblk, oblk, oblk, wspec, wspec], out_shape=[o, o, o, ov, ov],
        scratch_shapes=[scr, scr, scb, scb, scb, scb] + [scr] * 10 + [pltpu.VMEM((2, ATT_BLK, 2 * KEYS), F32)],
        sem=("parallel",), comm=comm)


def _att_norm_fwd(o, nw, ycat):
    s = o.shape[0]
    row = pl.BlockSpec((ROW_TILE, ATT_D), lambda i: (i, 0))
    vec = pl.BlockSpec((1, ATT_D), lambda i: (0, 0))

    def body(o_ref, nw_ref, ycat_ref, y_ref):
        o = o_ref[...]
        r = lax.rsqrt(jnp.mean(o * o, axis=-1, keepdims=True) + EPS)
        y_ref[...] = (o * r * nw_ref[...]).astype(BF16)

    return pl.pallas_call(body, name="att_norm_fwd", grid=(s // ROW_TILE,),
                          in_specs=[row, vec, pl.BlockSpec(memory_space=pl.ANY)],
                          out_specs=pl.BlockSpec((ROW_TILE, ATT_D), lambda i: (i, 1)),
                          out_shape=jax.ShapeDtypeStruct(ycat.shape, BF16), input_output_aliases={2: 0},
                          compiler_params=_cparams(("parallel",)))(o, nw, ycat)


def _mixer_split_epilogue(dycat, first, rows, vecs, outs):
    (o_ref, lse_ref), (nw_ref,), (dyssd_ref, do_ref, st_ref, dnw_ref) = rows, vecs, outs

    @pl.when(first)
    def _():
        dnw_ref[...] = jnp.zeros_like(dnw_ref)

    dyssd_ref[...] = dycat[:, :SSD_D_INNER]
    dy = dycat[:, SSD_D_INNER:]
    o = o_ref[...]
    r = lax.rsqrt(jnp.mean(o * o, axis=-1, keepdims=True) + EPS)
    nrm = o * r
    dnw_ref[...] += jnp.sum(dy * nrm, axis=0, keepdims=True)
    dn = dy * nw_ref[...]
    do = r * (dn - nrm * jnp.mean(dn * nrm, axis=-1, keepdims=True))
    do_ref[...] = do
    ones_bd = _head_mean_matrix().astype(BF16)
    prod = do * o
    delta = jnp.concatenate([_head_sum2(prod[:, j * LANE:(j + 1) * LANE], ones_bd) for j in range(ATT_D // LANE)], axis=1)
    lane = lax.broadcasted_iota(jnp.int32, (1, ATT_D), 1)
    st_ref[...] = jnp.where((lane & (HEAD_DIM - 1)) < HALF, lse_ref[...], delta)


def _ada_fwd(c_all, w_ada):
    def body(c_ref, w_ref, o_ref):
        cv = c_ref[...]
        o_ref[...] = _dot((cv * _sigmoid(cv)).astype(BF16), w_ref[...].astype(BF16), NN)

    return pl.pallas_call(body, name="ada_fwd", out_shape=jax.ShapeDtypeStruct((c_all.shape[0], w_ada.shape[1]), F32),
                          compiler_params=_cparams())(c_all, w_ada)


def _adamw_math(g, w, m, v):
    m_new = ADAM_B1 * m + (1.0 - ADAM_B1) * g
    v_new = ADAM_B2 * v + (1.0 - ADAM_B2) * (g * g)
    m_hat = m_new / (1.0 - ADAM_B1 ** ADAM_STEP)
    v_hat = v_new / (1.0 - ADAM_B2 ** ADAM_STEP)
    delta = -ADAM_LR * (m_hat / (jnp.sqrt(v_hat) + ADAM_EPS) + ADAM_WD * w)
    return delta, m_new, v_new


def _ada_bwd_adamw(c_all, dmod_cols, w, m, v):
    rows, cols = w.shape
    tr = 256
    blk = pl.BlockSpec((tr, cols), lambda i: (i, 0))

    def body(c_ref, d_ref, w_ref, m_ref, v_ref, g_ref, dl_ref, mo_ref, vo_ref):
        cv = c_ref[...]
        ca = cv * _sigmoid(cv)
        g = ca[:, 0:1] * d_ref[0:1, :]
        for b in range(1, N_DEV):
            g = g + ca[:, b:b + 1] * d_ref[b:b + 1, :]
        g_ref[...] = g
        dl_ref[...], mo_ref[...], vo_ref[...] = _adamw_math(g, w_ref[...], m_ref[...], v_ref[...])

    o = jax.ShapeDtypeStruct((rows, cols), F32)
    return pl.pallas_call(
        body, name="ada_bwd_adamw", grid=(rows // tr,),
        in_specs=[pl.BlockSpec((tr, N_DEV), lambda i: (i, 0)), pl.BlockSpec((N_DEV, cols), lambda i: (0, 0)), blk, blk, blk],
        out_specs=[blk] * 4, out_shape=[o, o, o, o], compiler_params=_cparams(("parallel",)))(c_all.T, dmod_cols, w, m, v)


def _reduce_adamw(slabs, w, m, v, name):
    rows, cols = w.shape
    n_src = slabs.shape[0]
    if rows % 128 == 0:
        tr, steps = 128, rows // 128
        blk = pl.BlockSpec((tr, cols), lambda i: (i, 0))
        sblk = pl.BlockSpec((n_src, tr, cols), lambda i: (0, i, 0))
    else:
        tc, steps = 256, cols // 256
        blk = pl.BlockSpec((rows, tc), lambda i: (0, i))
        sblk = pl.BlockSpec((n_src, rows, tc), lambda i: (0, 0, i))

    def body(s_ref, w_ref, m_ref, v_ref, g_ref, dl_ref, mo_ref, vo_ref):
        g = s_ref[0].astype(F32)
        for src in range(1, n_src):
            g = g + s_ref[src].astype(F32)
        g_ref[...] = g
        dl_ref[...], mo_ref[...], vo_ref[...] = _adamw_math(g, w_ref[...], m_ref[...], v_ref[...])

    o = jax.ShapeDtypeStruct((rows, cols), F32)
    return pl.pallas_call(
        body, name=name, grid=(steps,), in_specs=[sblk, blk, blk, blk],
        out_specs=[blk] * 4, out_shape=[o, o, o, o], compiler_params=_cparams(("parallel",)))(slabs, w, m, v)


def _small_reduce_adamw(gathered, w, m, v):
    def body(s_ref, w_ref, m_ref, v_ref, g_ref, dl_ref, mo_ref, vo_ref):
        g = s_ref[0]
        for dev in range(1, N_DEV):
            g = g + s_ref[dev]
        g_ref[...] = g
        dl_ref[...], mo_ref[...], vo_ref[...] = _adamw_math(g, w_ref[...], m_ref[...], v_ref[...])

    o = jax.ShapeDtypeStruct(w.shape, F32)
    return pl.pallas_call(body, name="small_reduce_adamw", out_shape=[o, o, o, o], compiler_params=_cparams())(gathered, w, m, v)


def _adamw_small(g, w, m, v, name):
    def body(g_ref, w_ref, m_ref, v_ref, dl_ref, mo_ref, vo_ref):
        dl_ref[...], mo_ref[...], vo_ref[...] = _adamw_math(g_ref[...], w_ref[...], m_ref[...], v_ref[...])

    o = jax.ShapeDtypeStruct(w.shape, F32)
    return pl.pallas_call(body, name=name, out_shape=[o, o, o], compiler_params=_cparams())(g, w, m, v)


class _Exchange:
    def __init__(self, arrs, scatter):
        self.arrs, self.scatter, self.n = list(arrs), scatter, len(arrs)
        hbm = pl.BlockSpec(memory_space=pltpu.HBM)
        self.in_specs = [hbm] * self.n
        self.out_specs = [hbm] * self.n
        self.out_shape = [jax.ShapeDtypeStruct(a.shape if scatter else (N_DEV,) + a.shape, a.dtype) for a in self.arrs]
        self.scratch = [pltpu.SemaphoreType.DMA((self.n * (N_DEV - 1),)), pltpu.SemaphoreType.DMA((self.n * (N_DEV - 1),)),
                        pltpu.SemaphoreType.DMA((self.n,))]

    def _local(self, ins, outs, sems):
        me = 4 * lax.axis_index("x") + 2 * lax.axis_index("y") + lax.axis_index("c")
        return [pltpu.make_async_copy(ins[a].at[me] if self.scatter else ins[a], outs[a].at[me], sems[2].at[a])
                for a in range(self.n)]

    def _remote(self, ins, outs, sems, arriving):
        send_sems, recv_sems, _ = sems
        x, y, c = lax.axis_index("x"), lax.axis_index("y"), lax.axis_index("c")
        me = 4 * x + 2 * y + c
        remote = []
        for a in range(self.n):
            for k in range(1, N_DEV):
                px = 1 - x if k & 4 else x
                py = 1 - y if k & 2 else y
                pc = 1 - c if k & 1 else c
                peer = 4 * px + 2 * py + pc
                sem = a * (N_DEV - 1) + k - 1
                remote.append(pltpu.make_async_remote_copy(
                    src_ref=ins[a].at[peer] if self.scatter else ins[a], dst_ref=outs[a].at[peer if arriving else me],
                    send_sem=send_sems.at[sem], recv_sem=recv_sems.at[sem], device_id=(px, py, pc), device_id_type=MESH_IDS))
        return remote

    def start(self, ins, outs, sems):
        for cp in self._local(ins, outs, sems) + self._remote(ins, outs, sems, arriving=False):
            cp.start()

    def forward(self, ins, outs, sems):
        pass

    def wait(self, ins, outs, sems):
        for send, arrival in zip(self._remote(ins, outs, sems, arriving=False), self._remote(ins, outs, sems, arriving=True)):
            send.wait_send()
            arrival.wait_recv()
        for cp in self._local(ins, outs, sems):
            cp.wait()


N_CHIP = N_DEV // 2


class _SiblingSwap(_Exchange):
    def __init__(self, arrs):
        super().__init__(arrs, scatter=True)
        self.out_shape = [jax.ShapeDtypeStruct((N_CHIP,) + a.shape[2:], a.dtype) for a in self.arrs]
        self.scratch = [pltpu.SemaphoreType.DMA((self.n,)), pltpu.SemaphoreType.DMA((self.n,)), pltpu.SemaphoreType.DMA((1,))]

    def _copies(self, ins, outs, sems):
        x, y, c = lax.axis_index("x"), lax.axis_index("y"), lax.axis_index("c")
        return [pltpu.make_async_remote_copy(src_ref=ins[a].at[:, 1 - c], dst_ref=outs[a], send_sem=sems[0].at[a], recv_sem=sems[1].at[a],
                                             device_id=(x, y, 1 - c), device_id_type=MESH_IDS) for a in range(self.n)]

    def start(self, ins, outs, sems):
        for cp in self._copies(ins, outs, sems):
            cp.start()

    def wait(self, ins, outs, sems):
        for cp in self._copies(ins, outs, sems):
            cp.wait()


class _ChipScatter(_Exchange):
    def __init__(self, arrs):
        super().__init__(arrs, scatter=True)
        n_pairs = self.n * (N_CHIP - 1)
        self.scratch = [pltpu.SemaphoreType.DMA((n_pairs,)), pltpu.SemaphoreType.DMA((n_pairs,)), pltpu.SemaphoreType.DMA((self.n,))]

    def _local(self, ins, outs, sems):
        chip = 2 * lax.axis_index("x") + lax.axis_index("y")
        return [pltpu.make_async_copy(ins[a].at[chip], outs[a].at[chip], sems[2].at[a]) for a in range(self.n)]

    def _remote(self, ins, outs, sems, arriving):
        send_sems, recv_sems, _ = sems
        x, y, c = lax.axis_index("x"), lax.axis_index("y"), lax.axis_index("c")
        chip = 2 * x + y
        remote = []
        for a in range(self.n):
            for k in range(1, N_CHIP):
                px = 1 - x if k & 2 else x
                py = 1 - y if k & 1 else y
                peer = 2 * px + py
                sem = a * (N_CHIP - 1) + k - 1
                remote.append(pltpu.make_async_remote_copy(
                    src_ref=ins[a].at[peer], dst_ref=outs[a].at[peer if arriving else chip], send_sem=send_sems.at[sem],
                    recv_sem=recv_sems.at[sem], device_id=(px, py, c), device_id_type=MESH_IDS))
        return remote


def _chip_sum(mine, theirs):
    n, rows, cols = mine.shape
    blk = pl.BlockSpec((1, rows, 256), lambda q, j: (q, 0, j))

    def body(a_ref, b_ref, o_ref):
        o_ref[...] = (a_ref[...].astype(F32) + b_ref[...].astype(F32)).astype(BF16)

    return pl.pallas_call(body, name="chip_sum", grid=(n, cols // 256), in_specs=[blk, blk], out_specs=blk,
                          out_shape=jax.ShapeDtypeStruct(mine.shape, BF16),
                          compiler_params=_cparams(("parallel", "parallel")))(mine, theirs)


class _Gather2(_Exchange):
    def __init__(self, arrs):
        super().__init__(arrs, scatter=False)

    def _copies(self, ins, outs, sems):
        send_sems, recv_sems, _ = sems
        x, y, c = lax.axis_index("x"), lax.axis_index("y"), lax.axis_index("c")
        sibling = (x, y, 1 - c)
        chips = [(1 - x, y), (x, 1 - y), (1 - x, 1 - y)]
        first, passed, landed = [], [], []
        for a in range(self.n):
            def copy(k, block, to, src=None, a=a):
                slab = outs[a].at[4 * block[0] + 2 * block[1] + block[2]]
                return pltpu.make_async_remote_copy(
                    src_ref=slab if src is None else src, dst_ref=slab, send_sem=send_sems.at[a * (N_DEV - 1) + k],
                    recv_sem=recv_sems.at[a * (N_DEV - 1) + k], device_id=to, device_id_type=MESH_IDS)

            first.append(copy(0, (x, y, c), sibling, src=ins[a]))
            landed.append(copy(0, sibling, sibling))
            for j, chip in enumerate(chips):
                first.append(copy(1 + j, (x, y, c), (*chip, c), src=ins[a]))
                passed.append((copy(1 + j, (*chip, c), sibling), copy(4 + j, (*chip, c), sibling)))
                landed.append(copy(4 + j, (*chip, 1 - c), sibling))
        return first, passed, landed

    def start(self, ins, outs, sems):
        for cp in self._local(ins, outs, sems) + self._copies(ins, outs, sems)[0]:
            cp.start()

    def forward(self, ins, outs, sems):
        for arrival, onward in self._copies(ins, outs, sems)[1]:
            arrival.wait_recv()
            onward.start()

    def wait(self, ins, outs, sems):
        first, passed, landed = self._copies(ins, outs, sems)
        for arrival in landed:
            arrival.wait_recv()
        for cp in first + [onward for _, onward in passed]:
            cp.wait_send()
        for cp in self._local(ins, outs, sems):
            cp.wait()


def _split_comm_refs(refs, n_in, n_out, n_scr, comm):
    nc = comm.n if comm is not None else 0
    ns = 3 if comm is not None else 0
    pos, groups = 0, []
    for cnt in (n_in, nc, n_out, nc, n_scr, ns):
        groups.append(refs[pos:pos + cnt])
        pos += cnt
    assert pos == len(refs), (pos, len(refs))
    return groups


def _pcall(body, args, *, name, grid, in_specs, out_specs, out_shape, scratch_shapes=(), sem=None, comm=None):
    in_specs, out_specs, out_shape, scratch_shapes = list(in_specs), list(out_specs), list(out_shape), list(scratch_shapes)
    n_in, n_out, n_scr = len(in_specs), len(out_specs), len(scratch_shapes)
    if comm is None:
        kernel_body = body
    else:
        def kernel_body(*refs):
            ins, cins, outs, couts, scr, sems = _split_comm_refs(refs, n_in, n_out, n_scr, comm)
            ids = [pl.program_id(a) for a in range(len(grid))]
            first, last = ids[0] == 0, ids[0] == grid[0] - 1
            for a in range(1, len(grid)):
                first, last = first & (ids[a] == 0), last & (ids[a] == grid[a] - 1)

            late = ids[0] == grid[0] - 1
            for a in range(1, len(grid)):
                late = late & (ids[a] == 0)

            @pl.when(first)
            def _():
                comm.start(cins, couts, sems)

            @pl.when(late)
            def _():
                comm.forward(cins, couts, sems)

            body(*ins, *outs, *scr)

            @pl.when(last)
            def _():
                comm.wait(cins, couts, sems)

        in_specs, out_specs, out_shape = in_specs + comm.in_specs, out_specs + comm.out_specs, out_shape + comm.out_shape
        scratch_shapes, args = scratch_shapes + comm.scratch, list(args) + comm.arrs
        sem = ("arbitrary",) * len(grid)
    res = pl.pallas_call(kernel_body, name=name, grid=grid, in_specs=in_specs, out_specs=out_specs, out_shape=out_shape,
                         scratch_shapes=scratch_shapes, compiler_params=_cparams(sem))(*args)
    return res[:n_out], res[n_out:]


def _exchange(arrs, name, scatter=False, ex=None):
    if ex is None:
        ex = _Exchange(arrs, scatter=True) if scatter else _Gather2(arrs)

    def body(*refs):
        _, ins, _, outs, _, sems = _split_comm_refs(refs, 0, 0, 0, ex)
        ex.start(ins, outs, sems)
        ex.forward(ins, outs, sems)
        ex.wait(ins, outs, sems)

    return pl.pallas_call(body, name=name, in_specs=ex.in_specs, out_specs=ex.out_specs, out_shape=ex.out_shape,
                          scratch_shapes=ex.scratch)(*ex.arrs)


def _pad_lanes(v, width=LANE):
    return jnp.pad(v, ((0, 0), (0, width - v.shape[1])))


def _shards_to_cols(g):
    return jnp.transpose(g, (1, 0, 2)).reshape(g.shape[1], N_DEV * g.shape[2])


def _local_step(x, tgt, mod, w_in_pt, conv_w, conv_b, dt_bias, a_log, d_skip, ssd_norm_w, q_norm_w, k_norm_w,
                attn_norm_w, w_out_sh, w_ff1_sh, w_ff2_sh, norm1_w, norm2_w, core):
    shift1, scale1, gate1, shift2, scale2, gate2 = [mod[i:i + 1] for i in range(N_MOD)]
    dtb, alog, dsk = _pad_lanes(dt_bias), _pad_lanes(a_log), _pad_lanes(d_skip)
    qw, kw = jnp.tile(q_norm_w, (1, ATT_HEADS)), jnp.tile(k_norm_w, (1, ATT_HEADS))

    h1 = _norm_mod_fwd(x, norm1_w, scale1, shift1, "norm1_fwd")
    proj = _matmul(h1, w_in_pt, tb=True, tm=2048, tn=896, tk=1024, name="in_proj")
    pre, act = _conv_fwd(proj, conv_w, conv_b)
    ypre, ycat_ssd, hall = _ssd_fwd(proj, act, dtb, alog, dsk, ssd_norm_w)
    (o_att, lse), (w_out_g, w_ff1_g, w_ff2_g) = _att_fwd(proj, qw, kw, comm=_Gather2([w_out_sh, w_ff1_sh, w_ff2_sh]))
    w_out = w_out_g.reshape(2 * D_MODEL, D_MODEL)
    w_ff1 = _shards_to_cols(w_ff1_g)
    w_ff2 = w_ff2_g.reshape(D_FF, D_MODEL)
    ycat = _att_norm_fwd(o_att, attn_norm_w, ycat_ssd)
    row32, row16, vec32 = ("row", F32), ("row", BF16), ("vec", F32)
    mix, x1, h2 = _matmul_rows(ycat, w_out, _residual_norm_epilogue, [x], [gate1, norm2_w, scale2, shift2],
                               [row32, row32, row16], tm=512, name="out_proj")
    u, act_ff = _matmul(h2, w_ff1, tm=1024, tn=2048, tk=1024, name="ff1", mode="relu2")
    loss, dout, dff, dgate2 = _matmul_rows(act_ff, w_ff2, _loss_epilogue, [x1, tgt], [gate2],
                                           [("one", F32), row32, row16, vec32], tm=512, name="ff2")

    du = _matmul(dff, w_ff2, tb=True, tm=512, tn=4096, tk=1024, out_dtype=BF16, name="ff2_dx", mode="drelu2", u=u)
    g_ff2 = _matmul(act_ff, dff, ta=True, tm=512, tn=1024, tk=4096, out_dtype=BF16, name="ff2_dw")
    dx1, dshift2, dscale2, g_norm2, dmix, dgate1 = _matmul_rows(
        du, w_ff1, _norm_bwd_epilogue, [x1, dout, mix], [norm2_w, scale2, gate1],
        [row32, vec32, vec32, vec32, row16, vec32], tb=True, tm=512, name="ff1_dx")
    g_ff1 = _matmul(h2, du, ta=True, tm=1024, tn=D_FF // N_DEV, tk=4096, out_dtype=BF16, name="ff1_dw", shard_out=True)

    dy_ssd, do, stats, g_attn_norm = _matmul_rows(
        dmix, w_out, _mixer_split_epilogue, [o_att, lse], [attn_norm_w],
        [("row", F32, SSD_D_INNER), ("row", F32, ATT_D), ("row", F32, ATT_D), ("vec", F32, ATT_D)], tb=True, tm=512, name="out_proj_dx")
    g_out = _matmul(ycat, dmix, ta=True, tm=512, tn=1024, tk=4096, out_dtype=BF16, name="out_proj_dw")
    ff_slabs = [g_ff1, g_ff2.reshape(N_DEV, D_FF // N_DEV, D_MODEL)]
    (dq, dk, dv, dqw, dkw), (s_ff1, s_ff2) = _att_bwd(proj, do, stats, qw, kw, comm=_Exchange(ff_slabs, scatter=True))
    out_slabs = [g_out.astype(BF16).reshape(N_DEV, 2 * D_MODEL // N_DEV, D_MODEL)]
    (dz, dact, ddtr, da, g_dsk, g_dtb, g_ssd_norm), (s_out,) = _ssd_bwd(
        dy_ssd, ypre, proj, act, hall, dtb, alog, dsk, ssd_norm_w, comm=_Exchange(out_slabs, scatter=True))
    dxbc, g_conv_w, g_conv_b = _conv_bwd(dact, pre, proj, conv_w)
    dproj = [(dz, OFF_Z), (dxbc, OFF_XBC), (ddtr, OFF_DT), (dq, OFF_Q), (dk, OFF_K), (dv, OFF_V)]
    g_head, g_tail = _pieces_t_matmul([[dz, dxbc], [dq, dk, dv]], h1, tm=256, name="in_proj_dw")
    g_dt = _matmul(ddtr, h1, ta=True, tm=LANE, tn=1024, tk=4096, out_dtype=BF16, name="in_proj_dw_dt")[:SSD_HEADS]
    in_slabs = jnp.concatenate([g_head, g_dt, g_tail], axis=0).reshape(N_CHIP, 2, IN_W // N_DEV, D_MODEL)
    (sibling_slabs,) = _exchange(None, "swap_w_in_grads", ex=_SiblingSwap([in_slabs]))
    chip_slabs = _chip_sum(lax.dynamic_index_in_dim(in_slabs, core, axis=1, keepdims=False), sibling_slabs)
    (grad_x, dshift1, dscale1, g_norm1), (s_in,) = _matmul_rows(
        dproj, w_in_pt, _norm_bwd_epilogue, [x, dx1], [norm1_w, scale1], [row32, vec32, vec32, vec32],
        tm=256, name="in_proj_dx", comm=_ChipScatter([chip_slabs]))

    dmod = jnp.concatenate([dshift1, dscale1, dgate1, dshift2, dscale2, dgate2], axis=0)
    g_alog = da[:, :SSD_HEADS] * (-jnp.exp(a_log))
    g_qw = dqw.reshape(ATT_HEADS, HEAD_DIM).sum(axis=0, keepdims=True)
    g_kw = dkw.reshape(ATT_HEADS, HEAD_DIM).sum(axis=0, keepdims=True)
    return dict(loss=loss, grad_x=grad_x, dmod=dmod, norm1_w=g_norm1, norm2_w=g_norm2, w_in=s_in, conv_w=g_conv_w,
                conv_b=g_conv_b, dt_bias=g_dtb[:, :SSD_HEADS], a_log=g_alog, d_skip=g_dsk[:, :SSD_HEADS],
                ssd_norm_w=g_ssd_norm, q_norm_w=g_qw, k_norm_w=g_kw, attn_norm_w=g_attn_norm, w_out=s_out,
                w_ff1=s_ff1, w_ff2=s_ff2)


def _pack_w_in_rows(slabs, cols=256):
    rows = IN_W // N_DEV
    cut = OFF_DT + SSD_HEADS
    gap = LANE - SSD_HEADS

    def body(in_ref, out_ref):
        out_ref[cut:cut + gap, :] = jnp.zeros((gap, cols), BF16)
        for d in range(N_DEV):
            lo, hi = rows * d, rows * (d + 1)
            if hi <= cut:
                out_ref[lo:hi, :] = in_ref[d]
            elif lo >= cut:
                out_ref[lo + gap:hi + gap, :] = in_ref[d]
            else:
                out_ref[lo:cut, :] = in_ref[d, :cut - lo, :]
                out_ref[cut + gap:hi + gap, :] = in_ref[d, cut - lo:, :]

    return pl.pallas_call(body, name="pack_w_in", grid=(D_MODEL // cols,),
                          in_specs=[pl.BlockSpec((N_DEV, rows, cols), lambda i: (0, 0, i))],
                          out_specs=pl.BlockSpec((IN_WP, cols), lambda i: (0, i)),
                          out_shape=jax.ShapeDtypeStruct((IN_WP, D_MODEL), BF16),
                          compiler_params=_cparams(("parallel",)))(slabs)


MISC_FIELDS = (("dt_bias", SSD_HEADS), ("a_log", SSD_HEADS), ("d_skip", SSD_HEADS), ("q_norm_w", HEAD_DIM), ("k_norm_w", HEAD_DIM),
               ("loss", 1))
SMALL_LAYOUT = (("b_ada", 6), ("norm1_w", 1), ("norm2_w", 1), ("conv_w", 8), ("conv_b", 2), ("ssd_norm_w", 1),
                ("attn_norm_w", 1), ("misc", 1))


def _pack_small(vals):
    rows = []
    for name, nrow in SMALL_LAYOUT:
        if name == "misc":
            misc = jnp.concatenate([vals[f].reshape(1, n) if f in vals else jnp.zeros((1, n), F32) for f, n in MISC_FIELDS], axis=1)
            rows.append(_pad_lanes(misc, D_MODEL))
        elif name in vals:
            rows.append(vals[name].reshape(nrow, D_MODEL))
        else:
            rows.append(jnp.zeros((nrow, D_MODEL), F32))
    used = sum(n for _, n in SMALL_LAYOUT)
    rows.append(jnp.zeros((SMALL_ROWS - used, D_MODEL), F32))
    return jnp.concatenate(rows, axis=0)


def _unpack_small(packed):
    out, r = {}, 0
    for name, nrow in SMALL_LAYOUT:
        blk = packed[r:r + nrow]
        r += nrow
        if name == "misc":
            c0 = 0
            for f, n in MISC_FIELDS:
                out[f] = blk[:, c0:c0 + n]
                c0 += n
        elif name == "b_ada":
            out[name] = blk.reshape(1, N_MOD * D_MODEL)
        elif name == "conv_w":
            out[name] = blk.reshape(CONV_K, CONV_CH)
        elif name == "conv_b":
            out[name] = blk.reshape(1, CONV_CH)
        else:
            out[name] = blk
    return out


WEIGHT_NAMES = ("norm1_w", "norm2_w", "w_ada", "b_ada", "w_in", "conv_w", "conv_b", "dt_bias", "a_log", "d_skip",
                "ssd_norm_w", "q_norm_w", "k_norm_w", "attn_norm_w", "w_out", "w_ff1", "w_ff2")
SMALL_NAMES = ("norm1_w", "norm2_w", "b_ada", "conv_b", "dt_bias", "a_log", "d_skip", "ssd_norm_w", "q_norm_w",
               "k_norm_w", "attn_norm_w")


def kernel(x, c, norm1_w, norm2_w, w_ada, b_ada, w_in, conv_w, conv_b, dt_bias, a_log, d_skip, ssd_norm_w, q_norm_w, k_norm_w, attn_norm_w, w_out, w_ff1, w_ff2, loss_target, m_norm1_w, m_norm2_w, m_w_ada, m_b_ada, m_w_in, m_conv_w, m_conv_b, m_dt_bias, m_a_log, m_d_skip, m_ssd_norm_w, m_q_norm_w, m_k_norm_w, m_attn_norm_w, m_w_out, m_w_ff1, m_w_ff2, v_norm1_w, v_norm2_w, v_w_ada, v_b_ada, v_w_in, v_conv_w, v_conv_b, v_dt_bias, v_a_log, v_d_skip, v_ssd_norm_w, v_q_norm_w, v_k_norm_w, v_attn_norm_w, v_w_out, v_w_ff1, v_w_ff2):
    args = dict(locals())
    w = {n: args[n] for n in WEIGHT_NAMES}
    m = {n: args["m_" + n] for n in WEIGHT_NAMES}
    v = {n: args["v_" + n] for n in WEIGHT_NAMES}
    me = 4 * lax.axis_index("x") + 2 * lax.axis_index("y") + lax.axis_index("c")

    c_rows = jnp.pad(c, ((0, 7), (0, 0)))
    w_in_t, m_in_t, v_in_t = [jnp.transpose(t["w_in"][0]) for t in (w, m, v)]
    c_g, conv_g, w_in_g = _exchange([c_rows, w["conv_w"][0], w_in_t.astype(BF16)], "gather_w_in", scatter=False)
    c_all = c_g[:, 0, :]
    conv_full = _shards_to_cols(conv_g)
    w_in_pt = _pack_w_in_rows(w_in_g)

    mod_part = _ada_fwd(c_all, w["w_ada"][0])
    (mod_g,) = _exchange([mod_part], "gather_mod", scatter=False)
    mod_mine = lax.dynamic_index_in_dim(mod_g, me, axis=1, keepdims=False).reshape(1, N_MOD * D_MODEL) + w["b_ada"]
    mod = mod_mine.reshape(N_MOD, D_MODEL)

    res = _local_step(x[0], loss_target[0], mod, w_in_pt, conv_full, w["conv_b"], w["dt_bias"], w["a_log"], w["d_skip"],
                      w["ssd_norm_w"], w["q_norm_w"], w["k_norm_w"], w["attn_norm_w"], w["w_out"][0].astype(BF16),
                      w["w_ff1"][0].astype(BF16), w["w_ff2"][0].astype(BF16), w["norm1_w"], w["norm2_w"], lax.axis_index("c"))

    small_vals = {n: res[n] for n in SMALL_NAMES if n != "b_ada"}
    small_vals["b_ada"] = res["dmod"]
    small_vals["conv_w"] = res["conv_w"]
    small_vals["loss"] = res["loss"]
    (small_g,) = _exchange([_pack_small(small_vals)], "gather_small", scatter=False)

    grads, delta, new_m, new_v = {}, {}, {}, {}
    for name in ("w_out", "w_ff1", "w_ff2"):
        outs = _reduce_adamw(res[name], w[name][0], m[name][0], v[name][0], "adamw_" + name)
        grads[name], delta[name], new_m[name], new_v[name] = [o[None] for o in outs]
    outs = _reduce_adamw(res["w_in"], w_in_t, m_in_t, v_in_t, "adamw_w_in")
    grads["w_in"], delta["w_in"], new_m["w_in"], new_v["w_in"] = [jnp.transpose(o)[None] for o in outs]

    sm = _small_reduce_adamw(small_g, _pack_small({n: w[n] for n in SMALL_NAMES}), _pack_small({n: m[n] for n in SMALL_NAMES}),
                             _pack_small({n: v[n] for n in SMALL_NAMES}))
    sm = [_unpack_small(p) for p in sm]
    for n in SMALL_NAMES:
        grads[n], delta[n], new_m[n], new_v[n] = [p[n] for p in sm]
    shard_w = CONV_CH // N_DEV
    g_conv = lax.dynamic_slice_in_dim(sm[0]["conv_w"], me * shard_w, shard_w, axis=1)
    cw = _adamw_small(g_conv, w["conv_w"][0], m["conv_w"][0], v["conv_w"][0], "adamw_conv_w")
    grads["conv_w"] = g_conv[None]
    delta["conv_w"], new_m["conv_w"], new_v["conv_w"] = [o[None] for o in cw]

    ada_w = w_ada.shape[2]
    dmod_all = small_g[:, :N_MOD, :].reshape(N_DEV, N_MOD * D_MODEL)
    dmod_cols = lax.dynamic_slice_in_dim(dmod_all, me * ada_w, ada_w, axis=1)
    outs = _ada_bwd_adamw(c_all, dmod_cols, w["w_ada"][0], m["w_ada"][0], v["w_ada"][0])
    grads["w_ada"], delta["w_ada"], new_m["w_ada"], new_v["w_ada"] = [o[None] for o in outs]

    loss = sm[0]["loss"][0, 0]
    return (loss, res["grad_x"][None], *[grads[n] for n in WEIGHT_NAMES], *[delta[n] for n in WEIGHT_NAMES],
            *[new_m[n] for n in WEIGHT_NAMES], *[new_v[n] for n in WEIGHT_NAMES])
```

```python
import jax
import jax.numpy as jnp
from jax import lax
from jax.experimental import pallas as pl
from jax.experimental.pallas import tpu as pltpu

F32 = jnp.float32
BF16 = jnp.bfloat16
HIGHEST = lax.Precision.HIGHEST
MESH_IDS = pl.DeviceIdType.MESH

N_DEV = 8
D_MODEL = 1024
HEAD_DIM = 64
SSD_HEADS = 16
SSD_GROUPS = 4
HEADS_PER_GROUP = SSD_HEADS // SSD_GROUPS
SSD_STATE = 128
SSD_CHUNK = 128
SSD_D_INNER = SSD_HEADS * HEAD_DIM
GROUP_WIDTH = SSD_D_INNER // SSD_GROUPS
CONV_K = 4
CONV_CH = SSD_D_INNER + 2 * SSD_GROUPS * SSD_STATE
ATT_HEADS = 16
ATT_D = ATT_HEADS * HEAD_DIM
ATT_BLK = 128
DILATIONS = (1, 4, 16)
D_FF = 4 * D_MODEL
N_MOD = 6
EPS = 1e-6
IN_W = SSD_D_INNER + CONV_CH + SSD_HEADS + 3 * ATT_D
LANE = 128
OFF_Z, OFF_XBC, OFF_DT = 0, SSD_D_INNER, SSD_D_INNER + CONV_CH
OFF_Q = OFF_DT + LANE
OFF_K, OFF_V = OFF_Q + ATT_D, OFF_Q + 2 * ATT_D
IN_WP = OFF_V + ATT_D

ADAM_LR, ADAM_B1, ADAM_B2, ADAM_EPS, ADAM_WD, ADAM_STEP = 0.001, 0.9, 0.999, 1e-08, 0.01, 10
VMEM_LIMIT = 60 * 1024 * 1024
ROW_TILE = 512
SMALL_ROWS = 24


def _cparams(sem=None):
    return pltpu.CompilerParams(dimension_semantics=sem, vmem_limit_bytes=VMEM_LIMIT)


def _sigmoid(v):
    return 1.0 / (1.0 + jnp.exp(-v))


def _softplus(v):
    y = jnp.exp(-jnp.abs(v))
    small = y * (1.0 - y * (0.5 - y * (1.0 / 3.0)))
    return jnp.maximum(v, 0.0) + jnp.where(y < 0.01, small, jnp.log(1.0 + y))


def _dot(a, b, dims, precision=None):
    return lax.dot_general(a, b, (dims, ((), ())), preferred_element_type=F32, precision=precision)


NN = ((1,), (0,))
NT = ((1,), (1,))
TN = ((0,), (0,))


def _matmul(a, b, *, ta=False, tb=False, tm, tn, tk, out_dtype=F32, name, mode=None, u=None, comm=None, shard_out=False):
    m, k = (a.shape[1], a.shape[0]) if ta else a.shape
    n = b.shape[0] if tb else b.shape[1]
    assert m % tm == 0 and n % tn == 0 and k % tk == 0, (name, m, n, k)
    nk = k // tk
    a_spec = pl.BlockSpec((tk, tm), lambda i, j, kk: (kk, i)) if ta else pl.BlockSpec((tm, tk), lambda i, j, kk: (i, kk))
    b_spec = pl.BlockSpec((tn, tk), lambda i, j, kk: (j, kk)) if tb else pl.BlockSpec((tk, tn), lambda i, j, kk: (kk, j))
    o_spec = pl.BlockSpec((tm, tn), lambda i, j, kk: (i, j))
    dims = ((0,) if ta else (1,), (1,) if tb else (0,))
    n_out = 2 if mode == "relu2" else 1

    def body(*refs):
        if mode == "drelu2":
            a_ref, b_ref, u_ref = refs[:3]
            rest = refs[3:]
        else:
            a_ref, b_ref = refs[:2]
            u_ref = None
            rest = refs[2:]
        outs = rest[:n_out]
        part = _dot(a_ref[...], b_ref[...], dims)

        def finish(r):
            if mode == "relu2":
                outs[0][...] = r.astype(BF16)
                rr = jnp.maximum(r, 0.0)
                outs[1][...] = (rr * rr).astype(BF16)
            elif mode == "drelu2":
                outs[0][...] = (r * (2.0 * jnp.maximum(u_ref[...].astype(F32), 0.0))).astype(out_dtype)
            else:
                outs[0][...] = r.astype(out_dtype)

        if nk == 1:
            finish(part)
        else:
            acc = rest[n_out]
            kk = pl.program_id(2)

            @pl.when(kk == 0)
            def _():
                acc[...] = part

            @pl.when(kk > 0)
            def _():
                acc[...] += part

            @pl.when(kk == nk - 1)
            def _():
                finish(acc[...])

    in_specs = [a_spec, b_spec]
    args = [a, b]
    if mode == "drelu2":
        in_specs.append(o_spec)
        args.append(u)
    if mode == "relu2":
        out_shape = [jax.ShapeDtypeStruct((m, n), BF16), jax.ShapeDtypeStruct((m, n), BF16)]
    elif shard_out:
        out_shape = [jax.ShapeDtypeStruct((n // tn, m, tn), out_dtype)]
        o_spec = pl.BlockSpec((None, tm, tn), lambda i, j, kk: (j, i, 0))
    else:
        out_shape = [jax.ShapeDtypeStruct((m, n), out_dtype)]
    outs, comm_outs = _pcall(
        body, args, name=name, grid=(m // tm, n // tn, nk), in_specs=in_specs, out_specs=[o_spec] * n_out,
        out_shape=out_shape, scratch_shapes=[pltpu.VMEM((tm, tn), F32)] if nk > 1 else [],
        sem=("parallel", "parallel", "arbitrary"), comm=comm)
    res = tuple(outs) if mode == "relu2" else outs[0]
    return res if comm is None else (res, comm_outs)


def _pieces_t_matmul(groups, b, *, tm, name):
    k, n = b.shape
    pieces = [p for g in groups for p in g]
    starts, tiles = [], 0
    for p in pieces:
        assert p.shape[0] == k and p.shape[1] % tm == 0, (name, p.shape)
        starts.append(tiles)
        tiles += p.shape[1] // tm
    group_of, group_start, group_tiles = [], [], []
    for gi, g in enumerate(groups):
        group_start.append(starts[len(group_of)])
        group_of += [gi] * len(g)
        group_tiles.append(sum(p.shape[1] // tm for p in g))

    def clipped(block, start, count):
        return pl.BlockSpec(block, (lambda i: (0, jnp.clip(i - start, 0, count - 1))) if block[0] == k
                            else (lambda i: (jnp.clip(i - start, 0, count - 1), 0)))

    def body(*refs):
        a_refs, b_ref, o_refs = refs[:len(pieces)], refs[len(pieces)], refs[len(pieces) + 1:]
        i = pl.program_id(0)
        for a_ref, start, p, gi in zip(a_refs, starts, pieces, group_of):
            @pl.when((i >= start) & (i < start + p.shape[1] // tm))
            def _(a_ref=a_ref, o_ref=o_refs[gi]):
                o_ref[...] = _dot(a_ref[...], b_ref[...], TN).astype(BF16)

    return pl.pallas_call(
        body, name=name, grid=(tiles,),
        in_specs=[clipped((k, tm), s0, p.shape[1] // tm) for s0, p in zip(starts, pieces)] + [pl.BlockSpec((k, n), lambda i: (0, 0))],
        out_specs=[clipped((tm, n), s0, cnt) for s0, cnt in zip(group_start, group_tiles)],
        out_shape=[jax.ShapeDtypeStruct((cnt * tm, n), BF16) for cnt in group_tiles],
        compiler_params=_cparams(("arbitrary",)))(*pieces, b)


def _rms_mod(xv, nw, scale, shift):
    r = lax.rsqrt(jnp.mean(xv * xv, axis=-1, keepdims=True) + EPS)
    return ((xv * r) * nw * (1.0 + scale) + shift).astype(BF16)


def _norm_mod_fwd(x, nw, scale, shift, name):
    s, d = x.shape
    row = pl.BlockSpec((ROW_TILE, d), lambda i: (i, 0))
    vec = pl.BlockSpec((1, d), lambda i: (0, 0))

    def body(x_ref, nw_ref, sc_ref, sh_ref, h_ref):
        h_ref[...] = _rms_mod(x_ref[...], nw_ref[...], sc_ref[...], sh_ref[...])

    return pl.pallas_call(body, name=name, grid=(s // ROW_TILE,), in_specs=[row, vec, vec, vec], out_specs=row,
                          out_shape=jax.ShapeDtypeStruct((s, d), BF16), compiler_params=_cparams(("parallel",)))(x, nw, scale, shift)


def _matmul_rows(a, b, epilogue, row_in, vec_in, outs, *, tb=False, tm, name, comm=None):
    pieces = a if isinstance(a, list) else [(a, 0)]
    assert not (tb and len(pieces) > 1)
    m = pieces[0][0].shape[0]
    n = b.shape[0] if tb else b.shape[1]
    assert m % tm == 0, (name, m, tm)
    dims = ((1,), (1,) if tb else (0,))
    n_a, n_row, n_vec = len(pieces), len(row_in), len(vec_in)

    def body(*refs):
        a_refs, b_ref, rest = refs[:n_a], refs[n_a], refs[n_a + 1:]
        if n_a == 1:
            c = _dot(a_refs[0][...], b_ref[...], dims)
        else:
            c = None
            for a_ref, (piece, off) in zip(a_refs, pieces):
                part = _dot(a_ref[...], b_ref[off:off + piece.shape[1], :], dims)
                c = part if c is None else c + part
        epilogue(c, pl.program_id(0) == 0, rest[:n_row], rest[n_row:n_row + n_vec], rest[n_row + n_vec:])

    def spec(kind, width):
        block = {"row": (tm, width), "vec": (1, width), "one": (1, 1)}[kind]
        return pl.BlockSpec(block, (lambda i: (i, 0)) if kind == "row" else (lambda i: (0, 0)))

    def shape(kind, width):
        return {"row": (m, width), "vec": (1, width), "one": (1, 1)}[kind]

    outs = [(o[0], o[1], o[2] if len(o) > 2 else n) for o in outs]
    res, comm_outs = _pcall(
        body, [*[p for p, _ in pieces], b, *row_in, *vec_in], name=name, grid=(m // tm,),
        in_specs=[spec("row", p.shape[1]) for p, _ in pieces] + [pl.BlockSpec(b.shape, lambda i: (0, 0))]
        + [spec("row", r.shape[1]) for r in row_in] + [spec("vec", v.shape[1]) for v in vec_in],
        out_specs=[spec(kind, width) for kind, _, width in outs],
        out_shape=[jax.ShapeDtypeStruct(shape(kind, width), dt) for kind, dt, width in outs],
        sem=("arbitrary",), comm=comm)
    return res if comm is None else (res, comm_outs)


def _residual_norm_epilogue(mix, first, rows, vecs, outs):
    (x_ref,), (gate_ref, nw_ref, sc_ref, sh_ref), (mix_ref, x1_ref, h_ref) = rows, vecs, outs
    xv = x_ref[...] + gate_ref[...] * mix
    mix_ref[...] = mix
    x1_ref[...] = xv
    h_ref[...] = _rms_mod(xv, nw_ref[...], sc_ref[...], sh_ref[...])


def _loss_epilogue(ff, first, rows, vecs, outs):
    (x1_ref, t_ref), (g_ref,), (loss_ref, dout_ref, dff_ref, dg_ref) = rows, vecs, outs
    d = ff.shape[1]

    @pl.when(first)
    def _():
        loss_ref[...] = jnp.zeros_like(loss_ref)
        dg_ref[...] = jnp.zeros_like(dg_ref)

    err = x1_ref[...] + g_ref[...] * ff - t_ref[...]
    loss_ref[...] += (0.5 / d) * jnp.sum(err * err).reshape(1, 1)
    dout = err * (1.0 / d)
    dout_ref[...] = dout
    dff_ref[...] = (g_ref[...] * dout).astype(BF16)
    dg_ref[...] += jnp.sum(dout * ff, axis=0, keepdims=True)


def _norm_bwd_epilogue(dh, first, rows, vecs, outs):
    with_gate = len(vecs) == 3
    x_ref, dres_ref = rows[:2]
    nw_ref, sc_ref = vecs[:2]
    dx_ref, dsh_ref, dsc_ref, dnw_ref = outs[:4]

    @pl.when(first)
    def _():
        for ref in outs[1:4] + outs[5:]:
            ref[...] = jnp.zeros_like(ref)

    xv = x_ref[...]
    r = lax.rsqrt(jnp.mean(xv * xv, axis=-1, keepdims=True) + EPS)
    nrm = xv * r
    one_sc = 1.0 + sc_ref[...]
    dhn = dh * nrm
    dsh_ref[...] += jnp.sum(dh, axis=0, keepdims=True)
    dsc_ref[...] += jnp.sum(dhn, axis=0, keepdims=True) * nw_ref[...]
    dnw_ref[...] += jnp.sum(dhn, axis=0, keepdims=True) * one_sc
    dn = dh * (nw_ref[...] * one_sc)
    dx = dres_ref[...] + r * (dn - nrm * jnp.mean(dn * nrm, axis=-1, keepdims=True))
    dx_ref[...] = dx
    if with_gate:
        outs[4][...] = (vecs[2][...] * dx).astype(BF16)
        outs[5][...] += jnp.sum(dx * rows[2][...], axis=0, keepdims=True)


CONV_COLS = 256
CONV_FWD_ROWS = 2048
CONV_BWD_ROWS = 1024
CONV_SUB_ROWS = 128
HALO = 8


def _shift_down(cur, halo, k):
    if k == 0:
        return cur
    rolled = pltpu.roll(cur, k, axis=0)
    top = jnp.where(lax.broadcasted_iota(jnp.int32, halo.shape, 0) < k, pltpu.roll(halo, k, axis=0), rolled[:HALO])
    return jnp.concatenate([top, rolled[HALO:]], axis=0)


def _shift_up(cur, halo, k):
    if k == 0:
        return cur
    t = cur.shape[0]
    rolled = pltpu.roll(cur, t - k, axis=0)
    bot = jnp.where(lax.broadcasted_iota(jnp.int32, halo.shape, 0) >= HALO - k, pltpu.roll(halo, HALO - k, axis=0),
                    rolled[t - HALO:])
    return jnp.concatenate([rolled[:t - HALO], bot], axis=0)


def _conv_fwd(proj, conv_w, conv_b):
    s = proj.shape[0]
    nr = s // CONV_FWD_ROWS
    cb0 = OFF_XBC // CONV_COLS
    hb = CONV_FWD_ROWS // HALO
    cur = pl.BlockSpec((CONV_FWD_ROWS, CONV_COLS), lambda j, r: (r, cb0 + j))
    prev = pl.BlockSpec((HALO, CONV_COLS), lambda j, r: (jnp.maximum(r * hb - 1, 0), cb0 + j))
    out = pl.BlockSpec((CONV_FWD_ROWS, CONV_COLS), lambda j, r: (r, j))

    def body(u_ref, up_ref, w_ref, b_ref, pre_ref, act_ref):
        r = pl.program_id(1)
        for c in range(CONV_FWD_ROWS // CONV_SUB_ROWS):
            rows = slice(c * CONV_SUB_ROWS, (c + 1) * CONV_SUB_ROWS)
            u = u_ref[rows, :]
            halo = u_ref[c * CONV_SUB_ROWS - HALO:c * CONV_SUB_ROWS, :] if c > 0 else jnp.where(r > 0, up_ref[...], 0.0)
            acc = b_ref[...] + w_ref[CONV_K - 1:CONV_K, :] * u
            for k in range(1, CONV_K):
                acc = acc + w_ref[CONV_K - 1 - k:CONV_K - k, :] * _shift_down(u, halo, k)
            pre_ref[rows, :] = acc
            act_ref[rows, :] = acc * _sigmoid(acc)

    return pl.pallas_call(
        body, name="conv_fwd", grid=(CONV_CH // CONV_COLS, nr),
        in_specs=[cur, prev, pl.BlockSpec((CONV_K, CONV_COLS), lambda j, r: (0, j)),
                  pl.BlockSpec((1, CONV_COLS), lambda j, r: (0, j))],
        out_specs=[out, out],
        out_shape=[jax.ShapeDtypeStruct((s, CONV_CH), F32), jax.ShapeDtypeStruct((s, CONV_CH), F32)],
        compiler_params=_cparams(("parallel", "arbitrary")))(proj, proj, conv_w, conv_b)


def _conv_bwd(dact, pre, proj, conv_w):
    s = proj.shape[0]
    nr = s // CONV_BWD_ROWS
    cb0 = OFF_XBC // CONV_COLS
    hb = CONV_BWD_ROWS // HALO
    last_halo = s // HALO - 1
    n_sub = CONV_BWD_ROWS // CONV_SUB_ROWS
    cur = pl.BlockSpec((CONV_BWD_ROWS, CONV_COLS), lambda j, r: (r, j))
    nxt = pl.BlockSpec((HALO, CONV_COLS), lambda j, r: (jnp.minimum((r + 1) * hb, last_halo), j))
    ucur = pl.BlockSpec((CONV_BWD_ROWS, CONV_COLS), lambda j, r: (r, cb0 + j))
    wspec = pl.BlockSpec((CONV_K, CONV_COLS), lambda j, r: (0, j))
    bspec = pl.BlockSpec((1, CONV_COLS), lambda j, r: (0, j))

    def dsilu(p):
        sg = _sigmoid(p)
        return sg * (1.0 + p * (1.0 - sg))

    def body(da_ref, dan_ref, pre_ref, pren_ref, u_ref, w_ref, du_ref, dw_ref, db_ref):
        r = pl.program_id(1)

        @pl.when(r == 0)
        def _():
            dw_ref[...] = jnp.zeros_like(dw_ref)
            db_ref[...] = jnp.zeros_like(db_ref)

        dws = [jnp.zeros((1, CONV_COLS), F32) for _ in range(CONV_K)]
        db = jnp.zeros((1, CONV_COLS), F32)
        for c in range(n_sub):
            rows = slice(c * CONV_SUB_ROWS, (c + 1) * CONV_SUB_ROWS)
            ahead = slice((c + 1) * CONV_SUB_ROWS, (c + 1) * CONV_SUB_ROWS + HALO)
            dpre = da_ref[rows, :] * dsilu(pre_ref[rows, :])
            if c < n_sub - 1:
                dnext = da_ref[ahead, :] * dsilu(pre_ref[ahead, :])
            else:
                dnext = jnp.where(r < nr - 1, dan_ref[...] * dsilu(pren_ref[...]), 0.0)
            u = u_ref[rows, :]
            du = w_ref[CONV_K - 1:CONV_K, :] * dpre
            dws[0] = dws[0] + jnp.sum(dpre * u, axis=0, keepdims=True)
            for k in range(1, CONV_K):
                ahead_k = _shift_up(dpre, dnext, k)
                du = du + w_ref[CONV_K - 1 - k:CONV_K - k, :] * ahead_k
                dws[k] = dws[k] + jnp.sum(ahead_k * u, axis=0, keepdims=True)
            du_ref[rows, :] = du.astype(BF16)
            db = db + jnp.sum(dpre, axis=0, keepdims=True)
        dw_ref[...] += jnp.concatenate(dws[::-1], axis=0)
        db_ref[...] += db

    return pl.pallas_call(
        body, name="conv_bwd", grid=(CONV_CH // CONV_COLS, nr),
        in_specs=[cur, nxt, cur, nxt, ucur, wspec],
        out_specs=[cur, wspec, bspec],
        out_shape=[jax.ShapeDtypeStruct((s, CONV_CH), BF16), jax.ShapeDtypeStruct((CONV_K, CONV_CH), F32),
                   jax.ShapeDtypeStruct((1, CONV_CH), F32)],
        compiler_params=_cparams(("parallel", "arbitrary")))(dact, dact, pre, pre, proj, conv_w)


def _ssd_common(dtr, dtb, alog):
    lane = lax.broadcasted_iota(jnp.int32, (1, LANE), 1)
    head_lane = lane < SSD_HEADS
    dt = jnp.where(head_lane, _softplus(dtr + dtb), 0.0)
    a = jnp.where(head_lane, -jnp.exp(alog), 0.0)
    row = lax.broadcasted_iota(jnp.int32, (SSD_CHUNK, SSD_CHUNK), 0)
    col = lax.broadcasted_iota(jnp.int32, (SSD_CHUNK, SSD_CHUNK), 1)
    tril = row >= col
    cs = _dot(tril.astype(F32), dt * a, NN, precision=HIGHEST)
    return dt, a, cs, cs.T, tril, lane


def _split_bf16(v, passes):
    terms, rest = [], v
    for _ in range(passes):
        t = rest.astype(BF16)
        terms.append(t)
        rest = rest - t.astype(F32)
    return terms


def _dot_split(v, m, dims, passes):
    terms = _split_bf16(v, passes)
    if passes == 1:
        return _dot(terms[0], m, dims)
    return _dot(jnp.concatenate(terms, axis=1), jnp.concatenate([m] * passes, axis=0 if dims == NN else 1), dims)


def _ssd_constants():
    heads = jnp.arange(LANE)[:, None]
    exp_mat = (heads == (jnp.arange(SSD_D_INNER)[None, :] // HEAD_DIM)).astype(BF16)
    ind4 = ((jnp.arange(SSD_HEADS * SSD_CHUNK)[:, None] // SSD_CHUNK) == jnp.arange(LANE)[None, :]).astype(BF16)
    return exp_mat, ind4


def _expand_heads(v):
    return jnp.repeat(v[:, :SSD_HEADS], HEAD_DIM, axis=1)


def _ssd_prep(dtr, dtb, alog, exp_mat):
    dt, a, cs, cst, tril, lane = _ssd_common(dtr, dtb, alog)
    return dt, a, cs, cst, tril, lane, _dot_split(dt, exp_mat, NN, 2), _dot_split(cs, exp_mat, NN, 3)


def _chunk_decay_rows(cs, g):
    parts = []
    for e in range(HEADS_PER_GROUP):
        h = g * HEADS_PER_GROUP + e
        parts.append(jnp.broadcast_to(jnp.exp(cs[SSD_CHUNK - 1:SSD_CHUNK, h:h + 1]), (HEAD_DIM, SSD_STATE)))
    return jnp.concatenate(parts, axis=0)


def _ssd_fwd(proj, act, dtb, alog, dsk, nw):
    s = proj.shape[0]
    nc = s // SSD_CHUNK
    bc_w = SSD_GROUPS * SSD_STATE
    exp_mat, _ = _ssd_constants()

    def body(z_ref, dtr_ref, xs_ref, b_ref, c_ref, dtb_ref, alog_ref, dskx_ref, nw_ref, exp_ref,
             ypre_ref, yssd_ref, hall_ref, h_scr):
        @pl.when(pl.program_id(0) == 0)
        def _():
            h_scr[...] = jnp.zeros_like(h_scr)

        dt, a, cs, cst, tril, lane, dtx, csx = _ssd_prep(dtr_ref[...], dtb_ref[...], alog_ref[...], exp_ref[...])
        cs_last_x = csx[SSD_CHUNK - 1:SSD_CHUNK, :]
        xs = xs_ref[...]
        xdt = xs * dtx
        xdtb = xdt.astype(BF16)
        xdec = (xdt * jnp.exp(cs_last_x - csx)).astype(BF16)
        ecsx = jnp.exp(csx)
        head_of_lane = lax.broadcasted_iota(jnp.int32, (1, GROUP_WIDTH), 1) // HEAD_DIM
        for g in range(SSD_GROUPS):
            gs = slice(g * GROUP_WIDTH, (g + 1) * GROUP_WIDTH)
            bg = b_ref[:, g * SSD_STATE:(g + 1) * SSD_STATE].astype(BF16)
            cg = c_ref[:, g * SSD_STATE:(g + 1) * SSD_STATE].astype(BF16)
            cb = _dot(cg, bg, NT)
            hprev = h_scr[gs, :]
            hall_ref[0, gs, :] = hprev
            gms, rhs = [], []
            xg = xdtb[:, gs]
            for e in range(HEADS_PER_GROUP):
                h = g * HEADS_PER_GROUP + e
                lm = jnp.exp(jnp.where(tril, cs[:, h:h + 1] - cst[h:h + 1, :], -1e30))
                gms.append((cb * lm).astype(BF16))
                rhs.append(jnp.where(head_of_lane == e, xg, jnp.zeros_like(xg)))
            y = _dot(jnp.concatenate(gms, axis=1), jnp.concatenate(rhs, axis=0), NN)
            y = y + ecsx[:, gs] * _dot(cg, hprev.astype(BF16), NT)
            y = y + dskx_ref[:, gs] * xs[:, gs]
            h_scr[gs, :] = hprev * _chunk_decay_rows(cs, g) + _dot(xdec[:, gs], bg, TN)
            ypre_ref[:, gs] = y
            z = z_ref[:, gs]
            yg = y * (z * _sigmoid(z))
            r = lax.rsqrt(jnp.mean(yg * yg, axis=-1, keepdims=True) + EPS)
            yssd_ref[:, gs] = (yg * r * nw_ref[:, gs]).astype(BF16)

    row_d = lambda cb: pl.BlockSpec((SSD_CHUNK, SSD_D_INNER), lambda c: (c, cb))
    small = pl.BlockSpec((1, LANE), lambda c: (0, 0))
    wide = pl.BlockSpec((1, SSD_D_INNER), lambda c: (0, 0))
    return pl.pallas_call(
        body, name="ssd_fwd", grid=(nc,),
        in_specs=[row_d(OFF_Z // SSD_D_INNER),
                  pl.BlockSpec((SSD_CHUNK, LANE), lambda c: (c, OFF_DT // LANE)),
                  row_d(0),
                  pl.BlockSpec((SSD_CHUNK, bc_w), lambda c: (c, SSD_D_INNER // bc_w)),
                  pl.BlockSpec((SSD_CHUNK, bc_w), lambda c: (c, SSD_D_INNER // bc_w + 1)),
                  small, small, wide, wide, pl.BlockSpec((LANE, SSD_D_INNER), lambda c: (0, 0))],
        out_specs=[row_d(0), row_d(0), pl.BlockSpec((1, SSD_D_INNER, SSD_STATE), lambda c: (c, 0, 0))],
        out_shape=[jax.ShapeDtypeStruct((s, SSD_D_INNER), F32), jax.ShapeDtypeStruct((s, SSD_D_INNER + ATT_D), BF16),
                   jax.ShapeDtypeStruct((nc, SSD_D_INNER, SSD_STATE), F32)],
        scratch_shapes=[pltpu.VMEM((SSD_D_INNER, SSD_STATE), F32)],
        compiler_params=_cparams(("arbitrary",)))(proj, proj, act, act, act, dtb, alog, _expand_heads(dsk), nw, exp_mat)


def _ssd_bwd(dycat, ypre, proj, act, hall, dtb, alog, dsk, nw, comm=None):
    s = proj.shape[0]
    nc = s // SSD_CHUNK
    bc_w = SSD_GROUPS * SSD_STATE

    exp_mat, ind4 = _ssd_constants()
    seg_passes = 1

    def body(dy_ref, ypre_ref, z_ref, dtr_ref, xs_ref, b_ref, c_ref, hall_ref, dtb_ref, alog_ref, dskx_ref, nw_ref,
             exp_ref, ind4_ref, dz_ref, dact_ref, ddtr_ref, da_ref, ddsk_ref, ddtb_ref, dnw_ref, dh_scr):
        @pl.when(pl.program_id(0) == 0)
        def _():
            dh_scr[...] = jnp.zeros_like(dh_scr)
            da_ref[...] = jnp.zeros_like(da_ref)
            ddsk_ref[...] = jnp.zeros_like(ddsk_ref)
            ddtb_ref[...] = jnp.zeros_like(ddtb_ref)
            dnw_ref[...] = jnp.zeros_like(dnw_ref)

        dtr = dtr_ref[...]
        dt, a, cs, cst, tril, lane, dtx, csx = _ssd_prep(dtr, dtb_ref[...], alog_ref[...], exp_ref[...])
        cs_last_x = csx[SSD_CHUNK - 1:SSD_CHUNK, :]
        xs = xs_ref[...]
        xdt = xs * dtx
        xdtb = xdt.astype(BF16)
        decx = jnp.exp(cs_last_x - csx)
        xdecf = xdt * decx
        xdec = xdecf.astype(BF16)
        ecsx = jnp.exp(csx)
        head_of_lane = lax.broadcasted_iota(jnp.int32, (1, GROUP_WIDTH), 1) // HEAD_DIM
        last_row = lax.broadcasted_iota(jnp.int32, (SSD_CHUNK, 1), 0) == SSD_CHUNK - 1
        dcs_col = jnp.zeros((SSD_CHUNK, LANE), F32)
        dcs_row = jnp.zeros((SSD_CHUNK, LANE), F32)
        ddt = jnp.zeros((SSD_CHUNK, LANE), F32)
        ddsk = jnp.zeros((1, LANE), F32)
        hsum = jnp.zeros((1, LANE), F32)
        t1_sum = jnp.zeros((1, LANE), F32)
        for g in range(SSD_GROUPS):
            gs = slice(g * GROUP_WIDTH, (g + 1) * GROUP_WIDTH)
            bsl = slice(g * SSD_STATE, (g + 1) * SSD_STATE)
            exp_g = exp_ref[:, gs]
            ind4_g = ind4_ref[g * HEADS_PER_GROUP * SSD_CHUNK:(g + 1) * HEADS_PER_GROUP * SSD_CHUNK, :]
            z = z_ref[:, gs]
            sg = _sigmoid(z)
            sz = z * sg
            ypre = ypre_ref[:, gs]
            yg = ypre * sz
            r = lax.rsqrt(jnp.mean(yg * yg, axis=-1, keepdims=True) + EPS)
            nrm = yg * r
            dyo_n = dy_ref[:, gs]
            dnw_ref[:, gs] += jnp.sum(dyo_n * nrm, axis=0, keepdims=True)
            dn = dyo_n * nw_ref[:, gs]
            dyg = r * (dn - nrm * jnp.mean(dn * nrm, axis=-1, keepdims=True))
            dz_ref[:, gs] = (dyg * ypre * (sg * (1.0 + z * (1.0 - sg)))).astype(BF16)
            dy = dyg * sz

            bg = b_ref[:, bsl].astype(BF16)
            cg = c_ref[:, bsl].astype(BF16)
            cb = _dot(cg, bg, NT)
            hprev = hall_ref[0, gs, :]
            hb = hprev.astype(BF16)
            dhn = dh_scr[gs, :]
            dhb = dhn.astype(BF16)
            xs_g, xdt_g = xs[:, gs], xdtb[:, gs]
            w_off = _dot(cg, hb, NT)
            dyo = dy * ecsx[:, gs]
            dyob = dyo.astype(BF16)
            dcg = _dot(dyob, hb, NN)
            dh_y = _dot(dyob, cg, TN)
            r_st = _dot(bg, dhb, NT)
            dbg = _dot(xdec[:, gs], dhb, NN)
            dyb = dy.astype(BF16)
            gms, gmbs, lms, dys = [], [], [], []
            for e in range(HEADS_PER_GROUP):
                h = g * HEADS_PER_GROUP + e
                lm = jnp.exp(jnp.where(tril, cs[:, h:h + 1] - cst[h:h + 1, :], -1e30))
                gm = cb * lm
                lms.append(lm)
                gms.append(gm)
                gmbs.append(gm.astype(BF16))
                dys.append(jnp.where(head_of_lane == e, dyb, jnp.zeros_like(dyb)))
            dxdt = _dot(jnp.concatenate(gmbs, axis=0), jnp.concatenate(dys, axis=0), TN) + decx[:, gs] * r_st
            dcb = jnp.zeros((SSD_CHUNK, SSD_CHUNK), F32)
            mms = []
            for e in range(HEADS_PER_GROUP):
                dg = _dot(dys[e], xdt_g, NT)
                mms.append(dg * gms[e])
                dcb = dcb + dg * lms[e]
            seg = _dot_split(jnp.concatenate([dyo * w_off, xdecf[:, gs] * r_st, dxdt * xs_g, dy * xs_g], axis=0), exp_g, NT, seg_passes)
            v1, t1, ddt_g, dsk_g = [seg[i * SSD_CHUNK:(i + 1) * SSD_CHUNK] for i in range(4)]
            dcs_col = dcs_col + v1 - t1 + _dot_split(jnp.concatenate(mms, axis=1), ind4_g, NN, seg_passes)
            for t in _split_bf16(jnp.concatenate(mms, axis=0), seg_passes):
                dcs_row = dcs_row + _dot(ind4_g, t, TN)
            ddt = ddt + ddt_g
            ddsk = ddsk + jnp.sum(dsk_g, axis=0, keepdims=True)
            t1_sum = t1_sum + jnp.sum(t1, axis=0, keepdims=True)
            for e in range(HEADS_PER_GROUP):
                h = g * HEADS_PER_GROUP + e
                hs = slice(e * HEAD_DIM, (e + 1) * HEAD_DIM)
                hsum = hsum + jnp.where(lane == h, jnp.sum(dhn[hs, :] * hprev[hs, :]).reshape(1, 1), 0.0)
            dh_scr[gs, :] = dhn * _chunk_decay_rows(cs, g) + dh_y
            dcbb = dcb.astype(BF16)
            dact_ref[:, gs] = dxdt * dtx[:, gs] + dskx_ref[:, gs] * dy
            dact_ref[:, SSD_D_INNER + g * SSD_STATE:SSD_D_INNER + (g + 1) * SSD_STATE] = dbg + _dot(dcbb, cg, TN)
            dact_ref[:, SSD_D_INNER + bc_w + g * SSD_STATE:SSD_D_INNER + bc_w + (g + 1) * SSD_STATE] = dcg + _dot(dcbb, bg, NN)
        dlast = t1_sum + jnp.exp(cs[SSD_CHUNK - 1:SSD_CHUNK, :]) * hsum
        dcs = dcs_col - dcs_row.T + jnp.where(last_row, dlast, 0.0)
        row = lax.broadcasted_iota(jnp.int32, (SSD_CHUNK, SSD_CHUNK), 0)
        col = lax.broadcasted_iota(jnp.int32, (SSD_CHUNK, SSD_CHUNK), 1)
        dda = _dot((col >= row).astype(F32), dcs, NN, precision=HIGHEST)
        ddt = ddt + dda * a
        da_ref[...] += jnp.sum(dda * dt, axis=0, keepdims=True)
        ddtr = jnp.where(lane < SSD_HEADS, ddt * _sigmoid(dtr + dtb_ref[...]), 0.0)
        ddtr_ref[...] = ddtr.astype(BF16)
        ddtb_ref[...] += jnp.sum(ddtr, axis=0, keepdims=True)
        ddsk_ref[...] += ddsk

    rev = lambda c: nc - 1 - c
    row_d = lambda cb: pl.BlockSpec((SSD_CHUNK, SSD_D_INNER), lambda c: (rev(c), cb))
    small = pl.BlockSpec((1, LANE), lambda c: (0, 0))
    wide = pl.BlockSpec((1, SSD_D_INNER), lambda c: (0, 0))
    small_shape = jax.ShapeDtypeStruct((1, LANE), F32)
    return _pcall(
        body, (dycat, ypre, proj, proj, act, act, act, hall, dtb, alog, _expand_heads(dsk), nw, exp_mat, ind4),
        name="ssd_bwd", grid=(nc,),
        in_specs=[row_d(0), row_d(0), row_d(OFF_Z // SSD_D_INNER),
                  pl.BlockSpec((SSD_CHUNK, LANE), lambda c: (rev(c), OFF_DT // LANE)),
                  row_d(0),
                  pl.BlockSpec((SSD_CHUNK, bc_w), lambda c: (rev(c), SSD_D_INNER // bc_w)),
                  pl.BlockSpec((SSD_CHUNK, bc_w), lambda c: (rev(c), SSD_D_INNER // bc_w + 1)),
                  pl.BlockSpec((1, SSD_D_INNER, SSD_STATE), lambda c: (rev(c), 0, 0)),
                  small, small, wide, wide, pl.BlockSpec((LANE, SSD_D_INNER), lambda c: (0, 0)),
                  pl.BlockSpec((SSD_HEADS * SSD_CHUNK, LANE), lambda c: (0, 0))],
        out_specs=[row_d(0), pl.BlockSpec((SSD_CHUNK, CONV_CH), lambda c: (rev(c), 0)),
                   pl.BlockSpec((SSD_CHUNK, LANE), lambda c: (rev(c), 0)), small, small, small, wide],
        out_shape=[jax.ShapeDtypeStruct((s, SSD_D_INNER), BF16), jax.ShapeDtypeStruct((s, CONV_CH), F32),
                   jax.ShapeDtypeStruct((s, LANE), BF16), small_shape, small_shape, small_shape,
                   jax.ShapeDtypeStruct((1, SSD_D_INNER), F32)],
        scratch_shapes=[pltpu.VMEM((SSD_D_INNER, SSD_STATE), F32)], sem=("arbitrary",), comm=comm)


def _head_mean_matrix():
    row = lax.broadcasted_iota(jnp.int32, (LANE, LANE), 0) // HEAD_DIM
    col = lax.broadcasted_iota(jnp.int32, (LANE, LANE), 1) // HEAD_DIM
    return (row == col).astype(F32)


def _head_sum2(v, ones_bd):
    hi = v.astype(BF16)
    lo = (v - hi.astype(F32)).astype(BF16)
    return _dot(jnp.concatenate([hi, lo], axis=1), jnp.concatenate([ones_bd, ones_bd], axis=0), NN)


def _head_norms(xs, ws, ones_bd):
    sums = [_head_sum2(x * x, ones_bd) for x in xs]
    rs = [lax.rsqrt(ms * (1.0 / HEAD_DIM) + EPS) for ms in sums]
    return [(x * r) * w for x, r, w in zip(xs, rs, ws)], rs


def _head_norms_bwd(dns, xs, ws, rs, ones_bd):
    nrms = [x * r for x, r in zip(xs, rs)]
    dnws = [dn * w for dn, w in zip(dns, ws)]
    projs = [_head_sum2(dnw * nrm, ones_bd) for dnw, nrm in zip(dnws, nrms)]
    dxs = [r * (dnw - nrm * (pr * (1.0 / HEAD_DIM))) for r, dnw, nrm, pr in zip(rs, dnws, nrms, projs)]
    return dxs, [jnp.sum(dn * nrm, axis=0, keepdims=True) for dn, nrm in zip(dns, nrms)]


NORM_CHUNKS = 4


PRO_ROWS = 256
ATT_GROUP_FWD = 32
ATT_GROUP_BWD = 8
KEYS = 2 * ATT_BLK
NEG = -1e30
HALF = HEAD_DIM // 2


def _rows(start, size, dil):
    return pl.ds(start, size) if dil == 1 else pl.ds(start, size, stride=dil)


def _fill_bias(bias_ref):
    row = lax.broadcasted_iota(jnp.int32, (ATT_BLK, 2 * KEYS), 0)
    col = lax.broadcasted_iota(jnp.int32, (ATT_BLK, 2 * KEYS), 1) & (KEYS - 1)
    for first, off in ((0, 0), (1, ATT_BLK)):
        dist = off + row - col
        bias_ref[first] = jnp.where((dist >= 0) & (dist <= ATT_BLK), 0.0, NEG)


def _pair(a, b):
    return jnp.concatenate([jnp.broadcast_to(a, (ATT_BLK, KEYS)), jnp.broadcast_to(b, (ATT_BLK, KEYS))], axis=1)


def _split_heads(x, is_a):
    zero = jnp.zeros_like(x)
    return jnp.concatenate([jnp.where(is_a, x, zero), jnp.where(is_a, zero, x)], axis=0)


def _block_ids(b, nb):
    i = b & (nb - 1)
    q0 = pl.multiple_of(b * ATT_BLK, ATT_BLK)
    k0 = pl.multiple_of((b - jnp.minimum(i, 1)) * ATT_BLK, ATT_BLK)
    return pl.ds(q0, ATT_BLK), pl.ds(k0, KEYS), jnp.minimum(i, 1)


def _natural_rows(b, nb, dil):
    if dil == 1:
        return pl.ds(pl.multiple_of(b * ATT_BLK, ATT_BLK), ATT_BLK)
    return pl.ds(b // nb + dil * ((b & (nb - 1)) * ATT_BLK), ATT_BLK, stride=dil)


def _att_fwd(proj, qw, kw, comm=None):
    s = proj.shape[0]
    nblk = s // ATT_BLK
    assert all((s // d) // ATT_BLK >= 2 for d in DILATIONS)
    blk = lambda off: pl.BlockSpec((s, LANE), lambda i: (0, off // LANE + i))
    wspec = pl.BlockSpec((1, LANE), lambda i: (0, i))
    oblk = pl.BlockSpec((s, LANE), lambda i: (0, i))

    def body(q_ref, k_ref, v_ref, qw_ref, kw_ref, o_ref, lse_ref, qn, kn, q_cm, k_cm, v_cm, m_acc, l_acc, o_d, m_d, l_d,
             o_e, m_e, l_e, tq, tk, tv, bias):
        ones_bd = _head_mean_matrix().astype(BF16)
        is_a = lax.broadcasted_iota(jnp.int32, (1, LANE), 1) < HEAD_DIM
        ones_ext = _split_heads(jnp.ones((KEYS, LANE), BF16), is_a)
        _fill_bias(bias)

        def pro(j, c):
            chunks = [pl.ds(pl.multiple_of((NORM_CHUNKS * j + u) * PRO_ROWS, PRO_ROWS), PRO_ROWS) for u in range(NORM_CHUNKS)]
            normed, _ = _head_norms([q_ref[rows, :] for rows in chunks] + [k_ref[rows, :] for rows in chunks],
                                    [qw_ref[...] * HEAD_DIM ** -0.5] * NORM_CHUNKS + [kw_ref[...]] * NORM_CHUNKS, ones_bd)
            for u, rows in enumerate(chunks):
                qn[rows, :] = normed[u]
                kn[rows, :] = normed[NORM_CHUNKS + u]
            return c

        lax.fori_loop(0, s // (NORM_CHUNKS * PRO_ROWS), pro, 0)

        results = dict(zip(DILATIONS, ((o_ref, m_acc, l_acc), (o_d, m_d, l_d), (o_e, m_e, l_e))))
        for dil in DILATIONS:
            ln = s // dil
            nb = ln // ATT_BLK
            o_out, m_out, l_out = results[dil]
            level = DILATIONS.index(dil)
            keep_f32 = 0 < level < len(DILATIONS) - 1
            from_temps = level >= 2
            step_rows = dil // DILATIONS[level - 1] if from_temps else dil
            for r in range(dil):
                prev = DILATIONS[level - 1] if from_temps else 1
                start = (r % prev) * (s // prev) + r // prev if from_temps else r

                def relayout(j, c, r=r, ln=ln, start=start, step_rows=step_rows, keep_f32=keep_f32, from_temps=from_temps):
                    j0 = pl.multiple_of(j * PRO_ROWS, PRO_ROWS)
                    src = _rows(start + step_rows * j0, PRO_ROWS, step_rows)
                    dst = pl.ds(r * ln + j0, PRO_ROWS)
                    qv, kv, vv = (tq[src, :], tk[src, :], tv[src, :]) if from_temps else (qn[src, :], kn[src, :], v_ref[src, :])
                    q_cm[dst, :] = qv.astype(BF16)
                    k_cm[dst, :] = kv.astype(BF16)
                    v_cm[dst, :] = vv.astype(BF16)
                    if keep_f32:
                        tq[dst, :] = qv
                        tk[dst, :] = kv
                        tv[dst, :] = vv
                    return c

                lax.fori_loop(0, ln // PRO_ROWS, relayout, 0)

            def step(bg, c, nb=nb, o_out=o_out, m_out=m_out, l_out=l_out):
                ids = [_block_ids(bg * ATT_GROUP_FWD + u, nb) for u in range(ATT_GROUP_FWD)]
                kbs = [_split_heads(k_cm[krows, :], is_a) for _, krows, _ in ids]
                scs = [_dot(q_cm[qrows, :], kb, NT) + bias[first] for (qrows, _, first), kb in zip(ids, kbs)]
                mas = [jnp.max(sc[:, :KEYS], axis=-1, keepdims=True) for sc in scs]
                mbs = [jnp.max(sc[:, KEYS:], axis=-1, keepdims=True) for sc in scs]
                ps = [jnp.exp(sc - _pair(ma, mb)).astype(BF16) for sc, ma, mb in zip(scs, mas, mbs)]
                vbs = [jnp.concatenate([_split_heads(v_cm[krows, :], is_a), ones_ext], axis=1) for _, krows, _ in ids]
                ols = [_dot(p, vb, NN) for p, vb in zip(ps, vbs)]
                for (qrows, _, _), ol, ma, mb in zip(ids, ols, mas, mbs):
                    o_out[qrows, :] = ol[:, :LANE]
                    l_out[qrows, :] = ol[:, LANE:]
                    m_out[qrows, :] = jnp.where(is_a, ma, mb)
                return c

            lax.fori_loop(0, nblk // ATT_GROUP_FWD, step, 0)

        for level in range(len(DILATIONS) - 1, 0, -1):
            fine_d, coarse_d = DILATIONS[level - 1], DILATIONS[level]
            ratio, ln_f, ln_c = coarse_d // fine_d, s // fine_d, s // coarse_d
            (o_f, m_f, l_f), (o_c, m_c, l_c) = results[fine_d], results[coarse_d]
            for r in range(coarse_d):
                def merge(j, c, r=r, ratio=ratio, ln_c=ln_c, start=(r % fine_d) * ln_f + r // fine_d,
                          o_f=o_f, m_f=m_f, l_f=l_f, o_c=o_c, m_c=m_c, l_c=l_c):
                    j0 = pl.multiple_of(j * PRO_ROWS, PRO_ROWS)
                    fine = _rows(start + ratio * j0, PRO_ROWS, ratio)
                    coarse = pl.ds(r * ln_c + j0, PRO_ROWS)
                    m_old, m_new = m_f[fine, :], m_c[coarse, :]
                    m = jnp.maximum(m_old, m_new)
                    a_old, a_new = jnp.exp(m_old - m), jnp.exp(m_new - m)
                    o_f[fine, :] = a_old * o_f[fine, :] + a_new * o_c[coarse, :]
                    l_f[fine, :] = a_old * l_f[fine, :] + a_new * l_c[coarse, :]
                    m_f[fine, :] = m
                    return c

                lax.fori_loop(0, ln_c // PRO_ROWS, merge, 0)

        def epi(j, c):
            rows = pl.ds(pl.multiple_of(j * PRO_ROWS, PRO_ROWS), PRO_ROWS)
            l = l_acc[rows, :]
            o_ref[rows, :] = o_ref[rows, :] / l
            lse_ref[rows, :] = m_acc[rows, :] + jnp.log(l)
            return c

        lax.fori_loop(0, s // PRO_ROWS, epi, 0)

    f = jax.ShapeDtypeStruct((s, ATT_D), F32)
    scr = pltpu.VMEM((s, LANE), F32)
    scb = pltpu.VMEM((s, LANE), BF16)
    return _pcall(
        body, (proj, proj, proj, qw, kw), name="att_fwd", grid=(ATT_D // LANE,),
        in_specs=[blk(OFF_Q), blk(OFF_K), blk(OFF_V), wspec, wspec], out_specs=[oblk, oblk], out_shape=[f, f],
        scratch_shapes=[scr, scr, scb, scb, scb] + [scr] * 11 + [pltpu.VMEM((2, ATT_BLK, 2 * KEYS), F32)],
        sem=("parallel",), comm=comm)


def _att_bwd(proj, do, stats, qw, kw, comm=None):
    s = proj.shape[0]
    nblk = s // ATT_BLK
    blk = lambda off: pl.BlockSpec((s, LANE), lambda i: (0, off // LANE + i))
    wspec = pl.BlockSpec((1, LANE), lambda i: (0, i))
    oblk = pl.BlockSpec((s, LANE), lambda i: (0, i))

    def body(q_ref, k_ref, v_ref, do_ref, st_ref, qw_ref, kw_ref, dq_ref, dk_ref, dv_ref, dqw_ref, dkw_ref,
             qn, kn, q_cm, do_cm, k_cm, v_cm, rms, dq_acc, dk_acc, dv_acc, dq_d, dk_d, dv_d, dq_e, dk_e, dv_e, bias):
        ones_bd = _head_mean_matrix().astype(BF16)
        is_a = lax.broadcasted_iota(jnp.int32, (1, LANE), 1) < HEAD_DIM
        first_half = (lax.broadcasted_iota(jnp.int32, (1, LANE), 1) & (HEAD_DIM - 1)) < HALF
        _fill_bias(bias)
        zero = jnp.zeros((PRO_ROWS, LANE), F32)
        results = dict(zip(DILATIONS, ((dq_acc, dk_acc, dv_acc), (dq_d, dk_d, dv_d), (dq_e, dk_e, dv_e))))

        def pro(j, c):
            chunks = [pl.ds(pl.multiple_of((NORM_CHUNKS * j + u) * PRO_ROWS, PRO_ROWS), PRO_ROWS) for u in range(NORM_CHUNKS)]
            normed, rs = _head_norms([q_ref[rows, :] for rows in chunks] + [k_ref[rows, :] for rows in chunks],
                                     [qw_ref[...] * HEAD_DIM ** -0.5] * NORM_CHUNKS + [kw_ref[...]] * NORM_CHUNKS, ones_bd)
            for u, rows in enumerate(chunks):
                qn[rows, :] = normed[u]
                kn[rows, :] = normed[NORM_CHUNKS + u]
                rms[rows, :] = jnp.where(first_half, rs[u], rs[NORM_CHUNKS + u])
                dk_acc[rows, :] = zero
                dv_acc[rows, :] = zero
            return c

        lax.fori_loop(0, s // (NORM_CHUNKS * PRO_ROWS), pro, 0)

        for dil in DILATIONS:
            ln = s // dil
            nb = ln // ATT_BLK
            dq_o, dk_o, dv_o = results[dil]
            level = DILATIONS.index(dil)
            keep_f32 = 0 < level < len(DILATIONS) - 1
            from_temps = level >= 2
            temps = results[DILATIONS[-1]]
            for r in range(dil):
                prev = DILATIONS[level - 1] if from_temps else 1
                start = (r % prev) * (s // prev) + r // prev if from_temps else r

                def relayout(j, c, dil=dil, r=r, ln=ln, start=start, step_rows=dil // prev):
                    j0 = pl.multiple_of(j * PRO_ROWS, PRO_ROWS)
                    nat = _rows(r + dil * j0, PRO_ROWS, dil)
                    src = _rows(start + step_rows * j0, PRO_ROWS, step_rows)
                    dst = pl.ds(r * ln + j0, PRO_ROWS)
                    qv, kv, vv = [t[src, :] for t in temps] if from_temps else (qn[src, :], kn[src, :], v_ref[src, :])
                    q_cm[dst, :] = qv.astype(BF16)
                    k_cm[dst, :] = kv.astype(BF16)
                    v_cm[dst, :] = vv.astype(BF16)
                    do_cm[dst, :] = do_ref[nat, :].astype(BF16)
                    if keep_f32:
                        for t, val in zip(temps, (qv, kv, vv)):
                            t[dst, :] = val
                    return c

                lax.fori_loop(0, ln // PRO_ROWS, relayout, 0)

            if dil > 1:
                def clear(j, c, dk_o=dk_o, dv_o=dv_o):
                    rows = pl.ds(pl.multiple_of(j * PRO_ROWS, PRO_ROWS), PRO_ROWS)
                    dk_o[rows, :] = zero
                    dv_o[rows, :] = zero
                    return c

                lax.fori_loop(0, s // PRO_ROWS, clear, 0)

            def step(bg, c, nb=nb, dil=dil, dq_o=dq_o, dk_o=dk_o, dv_o=dv_o):
                blocks = [bg * ATT_GROUP_BWD + u for u in range(ATT_GROUP_BWD)]
                ids = [_block_ids(b, nb) for b in blocks]
                qbs = [q_cm[qrows, :] for qrows, _, _ in ids]
                dobs = [do_cm[qrows, :] for qrows, _, _ in ids]
                kbs = [_split_heads(k_cm[krows, :], is_a) for _, krows, _ in ids]
                vbs = [_split_heads(v_cm[krows, :], is_a) for _, krows, _ in ids]
                sts = [st_ref[_natural_rows(b, nb, dil), :] for b in blocks]
                scs = [_dot(qb, kb, NT) + bias[first] for qb, kb, (_, _, first) in zip(qbs, kbs, ids)]
                dps = [_dot(dob, vb, NT) for dob, vb in zip(dobs, vbs)]
                ps = [jnp.exp(sc - _pair(st[:, 0:1], st[:, HEAD_DIM:HEAD_DIM + 1])) for sc, st in zip(scs, sts)]
                dss = [(p * (dp - _pair(st[:, HALF:HALF + 1], st[:, HEAD_DIM + HALF:HEAD_DIM + HALF + 1]))).astype(BF16)
                       for p, dp, st in zip(ps, dps, sts)]
                dqs = [_dot(ds, kb, NN) for ds, kb in zip(dss, kbs)]
                dkfs = [_dot(ds, qb, TN) for ds, qb in zip(dss, qbs)]
                dvfs = [_dot(p.astype(BF16), dob, TN) for p, dob in zip(ps, dobs)]
                for (qrows, krows, _), dq, dkf, dvf in zip(ids, dqs, dkfs, dvfs):
                    dq_o[qrows, :] = dq
                    dk_o[krows, :] += jnp.where(is_a, dkf[:KEYS], dkf[KEYS:])
                    dv_o[krows, :] += jnp.where(is_a, dvf[:KEYS], dvf[KEYS:])
                return c

            lax.fori_loop(0, nblk // ATT_GROUP_BWD, step, 0)

        for level in range(len(DILATIONS) - 1, 0, -1):
            fine_d, coarse_d = DILATIONS[level - 1], DILATIONS[level]
            ratio, ln_f, ln_c = coarse_d // fine_d, s // fine_d, s // coarse_d
            for r in range(coarse_d):
                def merge(j, c, r=r, ratio=ratio, ln_c=ln_c, start=(r % fine_d) * ln_f + r // fine_d,
                          fine_bufs=results[fine_d], coarse_bufs=results[coarse_d]):
                    j0 = pl.multiple_of(j * PRO_ROWS, PRO_ROWS)
                    fine = _rows(start + ratio * j0, PRO_ROWS, ratio)
                    coarse = pl.ds(r * ln_c + j0, PRO_ROWS)
                    for f_buf, c_buf in zip(fine_bufs, coarse_bufs):
                        f_buf[fine, :] += c_buf[coarse, :]
                    return c

                lax.fori_loop(0, ln_c // PRO_ROWS, merge, 0)

        def epi(j, c):
            chunks = [pl.ds(pl.multiple_of((NORM_CHUNKS * j + u) * PRO_ROWS, PRO_ROWS), PRO_ROWS) for u in range(NORM_CHUNKS)]
            packed = [rms[rows, :] for rows in chunks]
            rs = ([jnp.where(first_half, p, pltpu.roll(p, HALF, axis=1)) for p in packed]
                  + [jnp.where(first_half, pltpu.roll(p, LANE - HALF, axis=1), p) for p in packed])
            dxs, dws = _head_norms_bwd(
                [dq_acc[rows, :] for rows in chunks] + [dk_acc[rows, :] for rows in chunks],
                [q_ref[rows, :] for rows in chunks] + [k_ref[rows, :] for rows in chunks],
                [qw_ref[...] * HEAD_DIM ** -0.5] * NORM_CHUNKS + [kw_ref[...]] * NORM_CHUNKS, rs, ones_bd)
            dqw, dkw = c
            for u, rows in enumerate(chunks):
                dq_ref[rows, :] = dxs[u].astype(BF16)
                dk_ref[rows, :] = dxs[NORM_CHUNKS + u].astype(BF16)
                dv_ref[rows, :] = dv_acc[rows, :].astype(BF16)
                dqw, dkw = dqw + dws[u], dkw + dws[NORM_CHUNKS + u]
            return dqw, dkw

        zrow = jnp.zeros((1, LANE), F32)
        dqw, dkw = lax.fori_loop(0, s // (NORM_CHUNKS * PRO_ROWS), epi, (zrow, zrow))
        dqw_ref[...] = dqw * HEAD_DIM ** -0.5
        dkw_ref[...] = dkw

    o = jax.ShapeDtypeStruct((s, ATT_D), BF16)
    ov = jax.ShapeDtypeStruct((1, ATT_D), F32)
    scr = pltpu.VMEM((s, LANE), F32)
    scb = pltpu.VMEM((s, LANE), BF16)
    return _pcall(
        body, (proj, proj, proj, do, stats, qw, kw), name="att_bwd", grid=(ATT_D // LANE,),
        in_specs=[blk(OFF_Q), blk(OFF_K), blk(OFF_V), oblk, oblk, wspec, wspec],
        out_specs=[oblk, oblk, oblk, wspec, wspec], out_shape=[o, o, o, ov, ov],
        scratch_shapes=[scr, scr, scb, scb, scb, scb] + [scr] * 10 + [pltpu.VMEM((2, ATT_BLK, 2 * KEYS), F32)],
        sem=("parallel",), comm=comm)


def _att_norm_fwd(o, nw, ycat):
    s = o.shape[0]
    row = pl.BlockSpec((ROW_TILE, ATT_D), lambda i: (i, 0))
    vec = pl.BlockSpec((1, ATT_D), lambda i: (0, 0))

    def body(o_ref, nw_ref, ycat_ref, y_ref):
        o = o_ref[...]
        r = lax.rsqrt(jnp.mean(o * o, axis=-1, keepdims=True) + EPS)
        y_ref[...] = (o * r * nw_ref[...]).astype(BF16)

    return pl.pallas_call(body, name="att_norm_fwd", grid=(s // ROW_TILE,),
                          in_specs=[row, vec, pl.BlockSpec(memory_space=pl.ANY)],
                          out_specs=pl.BlockSpec((ROW_TILE, ATT_D), lambda i: (i, 1)),
                          out_shape=jax.ShapeDtypeStruct(ycat.shape, BF16), input_output_aliases={2: 0},
                          compiler_params=_cparams(("parallel",)))(o, nw, ycat)


def _mixer_split_epilogue(dycat, first, rows, vecs, outs):
    (o_ref, lse_ref), (nw_ref,), (dyssd_ref, do_ref, st_ref, dnw_ref) = rows, vecs, outs

    @pl.when(first)
    def _():
        dnw_ref[...] = jnp.zeros_like(dnw_ref)

    dyssd_ref[...] = dycat[:, :SSD_D_INNER]
    dy = dycat[:, SSD_D_INNER:]
    o = o_ref[...]
    r = lax.rsqrt(jnp.mean(o * o, axis=-1, keepdims=True) + EPS)
    nrm = o * r
    dnw_ref[...] += jnp.sum(dy * nrm, axis=0, keepdims=True)
    dn = dy * nw_ref[...]
    do = r * (dn - nrm * jnp.mean(dn * nrm, axis=-1, keepdims=True))
    do_ref[...] = do
    ones_bd = _head_mean_matrix().astype(BF16)
    prod = do * o
    delta = jnp.concatenate([_head_sum2(prod[:, j * LANE:(j + 1) * LANE], ones_bd) for j in range(ATT_D // LANE)], axis=1)
    lane = lax.broadcasted_iota(jnp.int32, (1, ATT_D), 1)
    st_ref[...] = jnp.where((lane & (HEAD_DIM - 1)) < HALF, lse_ref[...], delta)


def _ada_fwd(c_all, w_ada):
    def body(c_ref, w_ref, o_ref):
        cv = c_ref[...]
        o_ref[...] = _dot((cv * _sigmoid(cv)).astype(BF16), w_ref[...].astype(BF16), NN)

    return pl.pallas_call(body, name="ada_fwd", out_shape=jax.ShapeDtypeStruct((c_all.shape[0], w_ada.shape[1]), F32),
                          compiler_params=_cparams())(c_all, w_ada)


def _adamw_math(g, w, m, v):
    m_new = ADAM_B1 * m + (1.0 - ADAM_B1) * g
    v_new = ADAM_B2 * v + (1.0 - ADAM_B2) * (g * g)
    m_hat = m_new / (1.0 - ADAM_B1 ** ADAM_STEP)
    v_hat = v_new / (1.0 - ADAM_B2 ** ADAM_STEP)
    delta = -ADAM_LR * (m_hat / (jnp.sqrt(v_hat) + ADAM_EPS) + ADAM_WD * w)
    return delta, m_new, v_new


def _ada_bwd_adamw(c_all, dmod_cols, w, m, v):
    rows, cols = w.shape
    tr = 256
    blk = pl.BlockSpec((tr, cols), lambda i: (i, 0))

    def body(c_ref, d_ref, w_ref, m_ref, v_ref, g_ref, dl_ref, mo_ref, vo_ref):
        cv = c_ref[...]
        ca = cv * _sigmoid(cv)
        g = ca[:, 0:1] * d_ref[0:1, :]
        for b in range(1, N_DEV):
            g = g + ca[:, b:b + 1] * d_ref[b:b + 1, :]
        g_ref[...] = g
        dl_ref[...], mo_ref[...], vo_ref[...] = _adamw_math(g, w_ref[...], m_ref[...], v_ref[...])

    o = jax.ShapeDtypeStruct((rows, cols), F32)
    return pl.pallas_call(
        body, name="ada_bwd_adamw", grid=(rows // tr,),
        in_specs=[pl.BlockSpec((tr, N_DEV), lambda i: (i, 0)), pl.BlockSpec((N_DEV, cols), lambda i: (0, 0)), blk, blk, blk],
        out_specs=[blk] * 4, out_shape=[o, o, o, o], compiler_params=_cparams(("parallel",)))(c_all.T, dmod_cols, w, m, v)


def _reduce_adamw(slabs, w, m, v, name):
    rows, cols = w.shape
    n_src = slabs.shape[0]
    if rows % 128 == 0:
        tr, steps = 128, rows // 128
        blk = pl.BlockSpec((tr, cols), lambda i: (i, 0))
        sblk = pl.BlockSpec((n_src, tr, cols), lambda i: (0, i, 0))
    else:
        tc, steps = 256, cols // 256
        blk = pl.BlockSpec((rows, tc), lambda i: (0, i))
        sblk = pl.BlockSpec((n_src, rows, tc), lambda i: (0, 0, i))

    def body(s_ref, w_ref, m_ref, v_ref, g_ref, dl_ref, mo_ref, vo_ref):
        g = s_ref[0].astype(F32)
        for src in range(1, n_src):
            g = g + s_ref[src].astype(F32)
        g_ref[...] = g
        dl_ref[...], mo_ref[...], vo_ref[...] = _adamw_math(g, w_ref[...], m_ref[...], v_ref[...])

    o = jax.ShapeDtypeStruct((rows, cols), F32)
    return pl.pallas_call(
        body, name=name, grid=(steps,), in_specs=[sblk, blk, blk, blk],
        out_specs=[blk] * 4, out_shape=[o, o, o, o], compiler_params=_cparams(("parallel",)))(slabs, w, m, v)


def _small_reduce_adamw(gathered, w, m, v):
    def body(s_ref, w_ref, m_ref, v_ref, g_ref, dl_ref, mo_ref, vo_ref):
        g = s_ref[0]
        for dev in range(1, N_DEV):
            g = g + s_ref[dev]
        g_ref[...] = g
        dl_ref[...], mo_ref[...], vo_ref[...] = _adamw_math(g, w_ref[...], m_ref[...], v_ref[...])

    o = jax.ShapeDtypeStruct(w.shape, F32)
    return pl.pallas_call(body, name="small_reduce_adamw", out_shape=[o, o, o, o], compiler_params=_cparams())(gathered, w, m, v)


def _adamw_small(g, w, m, v, name):
    def body(g_ref, w_ref, m_ref, v_ref, dl_ref, mo_ref, vo_ref):
        dl_ref[...], mo_ref[...], vo_ref[...] = _adamw_math(g_ref[...], w_ref[...], m_ref[...], v_ref[...])

    o = jax.ShapeDtypeStruct(w.shape, F32)
    return pl.pallas_call(body, name=name, out_shape=[o, o, o], compiler_params=_cparams())(g, w, m, v)


class _Exchange:
    def __init__(self, arrs, scatter):
        self.arrs, self.scatter, self.n = list(arrs), scatter, len(arrs)
        hbm = pl.BlockSpec(memory_space=pltpu.HBM)
        self.in_specs = [hbm] * self.n
        self.out_specs = [hbm] * self.n
        self.out_shape = [jax.ShapeDtypeStruct(a.shape if scatter else (N_DEV,) + a.shape, a.dtype) for a in self.arrs]
        self.scratch = [pltpu.SemaphoreType.DMA((self.n * (N_DEV - 1),)), pltpu.SemaphoreType.DMA((self.n * (N_DEV - 1),)),
                        pltpu.SemaphoreType.DMA((self.n,))]

    def _local(self, ins, outs, sems):
        me = 4 * lax.axis_index("x") + 2 * lax.axis_index("y") + lax.axis_index("c")
        return [pltpu.make_async_copy(ins[a].at[me] if self.scatter else ins[a], outs[a].at[me], sems[2].at[a])
                for a in range(self.n)]

    def _remote(self, ins, outs, sems, arriving):
        send_sems, recv_sems, _ = sems
        x, y, c = lax.axis_index("x"), lax.axis_index("y"), lax.axis_index("c")
        me = 4 * x + 2 * y + c
        remote = []
        for a in range(self.n):
            for k in range(1, N_DEV):
                px = 1 - x if k & 4 else x
                py = 1 - y if k & 2 else y
                pc = 1 - c if k & 1 else c
                peer = 4 * px + 2 * py + pc
                sem = a * (N_DEV - 1) + k - 1
                remote.append(pltpu.make_async_remote_copy(
                    src_ref=ins[a].at[peer] if self.scatter else ins[a], dst_ref=outs[a].at[peer if arriving else me],
                    send_sem=send_sems.at[sem], recv_sem=recv_sems.at[sem], device_id=(px, py, pc), device_id_type=MESH_IDS))
        return remote

    def start(self, ins, outs, sems):
        for cp in self._local(ins, outs, sems) + self._remote(ins, outs, sems, arriving=False):
            cp.start()

    def forward(self, ins, outs, sems):
        pass

    def wait(self, ins, outs, sems):
        for send, arrival in zip(self._remote(ins, outs, sems, arriving=False), self._remote(ins, outs, sems, arriving=True)):
            send.wait_send()
            arrival.wait_recv()
        for cp in self._local(ins, outs, sems):
            cp.wait()


N_CHIP = N_DEV // 2


class _SiblingSwap(_Exchange):
    def __init__(self, arrs):
        super().__init__(arrs, scatter=True)
        self.out_shape = [jax.ShapeDtypeStruct((N_CHIP,) + a.shape[2:], a.dtype) for a in self.arrs]
        self.scratch = [pltpu.SemaphoreType.DMA((self.n,)), pltpu.SemaphoreType.DMA((self.n,)), pltpu.SemaphoreType.DMA((1,))]

    def _copies(self, ins, outs, sems):
        x, y, c = lax.axis_index("x"), lax.axis_index("y"), lax.axis_index("c")
        return [pltpu.make_async_remote_copy(src_ref=ins[a].at[:, 1 - c], dst_ref=outs[a], send_sem=sems[0].at[a], recv_sem=sems[1].at[a],
                                             device_id=(x, y, 1 - c), device_id_type=MESH_IDS) for a in range(self.n)]

    def start(self, ins, outs, sems):
        for cp in self._copies(ins, outs, sems):
            cp.start()

    def wait(self, ins, outs, sems):
        for cp in self._copies(ins, outs, sems):
            cp.wait()


class _ChipScatter(_Exchange):
    def __init__(self, arrs):
        super().__init__(arrs, scatter=True)
        n_pairs = self.n * (N_CHIP - 1)
        self.scratch = [pltpu.SemaphoreType.DMA((n_pairs,)), pltpu.SemaphoreType.DMA((n_pairs,)), pltpu.SemaphoreType.DMA((self.n,))]

    def _local(self, ins, outs, sems):
        chip = 2 * lax.axis_index("x") + lax.axis_index("y")
        return [pltpu.make_async_copy(ins[a].at[chip], outs[a].at[chip], sems[2].at[a]) for a in range(self.n)]

    def _remote(self, ins, outs, sems, arriving):
        send_sems, recv_sems, _ = sems
        x, y, c = lax.axis_index("x"), lax.axis_index("y"), lax.axis_index("c")
        chip = 2 * x + y
        remote = []
        for a in range(self.n):
            for k in range(1, N_CHIP):
                px = 1 - x if k & 2 else x
                py = 1 - y if k & 1 else y
                peer = 2 * px + py
                sem = a * (N_CHIP - 1) + k - 1
                remote.append(pltpu.make_async_remote_copy(
                    src_ref=ins[a].at[peer], dst_ref=outs[a].at[peer if arriving else chip], send_sem=send_sems.at[sem],
                    recv_sem=recv_sems.at[sem], device_id=(px, py, c), device_id_type=MESH_IDS))
        return remote


def _chip_sum(mine, theirs):
    n, rows, cols = mine.shape
    blk = pl.BlockSpec((1, rows, 256), lambda q, j: (q, 0, j))

    def body(a_ref, b_ref, o_ref):
        o_ref[...] = (a_ref[...].astype(F32) + b_ref[...].astype(F32)).astype(BF16)

    return pl.pallas_call(body, name="chip_sum", grid=(n, cols // 256), in_specs=[blk, blk], out_specs=blk,
                          out_shape=jax.ShapeDtypeStruct(mine.shape, BF16),
                          compiler_params=_cparams(("parallel", "parallel")))(mine, theirs)


class _Gather2(_Exchange):
    def __init__(self, arrs):
        super().__init__(arrs, scatter=False)

    def _copies(self, ins, outs, sems):
        send_sems, recv_sems, _ = sems
        x, y, c = lax.axis_index("x"), lax.axis_index("y"), lax.axis_index("c")
        sibling = (x, y, 1 - c)
        chips = [(1 - x, y), (x, 1 - y), (1 - x, 1 - y)]
        first, passed, landed = [], [], []
        for a in range(self.n):
            def copy(k, block, to, src=None, a=a):
                slab = outs[a].at[4 * block[0] + 2 * block[1] + block[2]]
                return pltpu.make_async_remote_copy(
                    src_ref=slab if src is None else src, dst_ref=slab, send_sem=send_sems.at[a * (N_DEV - 1) + k],
                    recv_sem=recv_sems.at[a * (N_DEV - 1) + k], device_id=to, device_id_type=MESH_IDS)

            first.append(copy(0, (x, y, c), sibling, src=ins[a]))
            landed.append(copy(0, sibling, sibling))
            for j, chip in enumerate(chips):
                first.append(copy(1 + j, (x, y, c), (*chip, c), src=ins[a]))
                passed.append((copy(1 + j, (*chip, c), sibling), copy(4 + j, (*chip, c), sibling)))
                landed.append(copy(4 + j, (*chip, 1 - c), sibling))
        return first, passed, landed

    def start(self, ins, outs, sems):
        for cp in self._local(ins, outs, sems) + self._copies(ins, outs, sems)[0]:
            cp.start()

    def forward(self, ins, outs, sems):
        for arrival, onward in self._copies(ins, outs, sems)[1]:
            arrival.wait_recv()
            onward.start()

    def wait(self, ins, outs, sems):
        first, passed, landed = self._copies(ins, outs, sems)
        for arrival in landed:
            arrival.wait_recv()
        for cp in first + [onward for _, onward in passed]:
            cp.wait_send()
        for cp in self._local(ins, outs, sems):
            cp.wait()


def _split_comm_refs(refs, n_in, n_out, n_scr, comm):
    nc = comm.n if comm is not None else 0
    ns = 3 if comm is not None else 0
    pos, groups = 0, []
    for cnt in (n_in, nc, n_out, nc, n_scr, ns):
        groups.append(refs[pos:pos + cnt])
        pos += cnt
    assert pos == len(refs), (pos, len(refs))
    return groups


def _pcall(body, args, *, name, grid, in_specs, out_specs, out_shape, scratch_shapes=(), sem=None, comm=None):
    in_specs, out_specs, out_shape, scratch_shapes = list(in_specs), list(out_specs), list(out_shape), list(scratch_shapes)
    n_in, n_out, n_scr = len(in_specs), len(out_specs), len(scratch_shapes)
    if comm is None:
        kernel_body = body
    else:
        def kernel_body(*refs):
            ins, cins, outs, couts, scr, sems = _split_comm_refs(refs, n_in, n_out, n_scr, comm)
            ids = [pl.program_id(a) for a in range(len(grid))]
            first, last = ids[0] == 0, ids[0] == grid[0] - 1
            for a in range(1, len(grid)):
                first, last = first & (ids[a] == 0), last & (ids[a] == grid[a] - 1)

            late = ids[0] == grid[0] - 1
            for a in range(1, len(grid)):
                late = late & (ids[a] == 0)

            @pl.when(first)
            def _():
                comm.start(cins, couts, sems)

            @pl.when(late)
            def _():
                comm.forward(cins, couts, sems)

            body(*ins, *outs, *scr)

            @pl.when(last)
            def _():
                comm.wait(cins, couts, sems)

        in_specs, out_specs, out_shape = in_specs + comm.in_specs, out_specs + comm.out_specs, out_shape + comm.out_shape
        scratch_shapes, args = scratch_shapes + comm.scratch, list(args) + comm.arrs
        sem = ("arbitrary",) * len(grid)
    res = pl.pallas_call(kernel_body, name=name, grid=grid, in_specs=in_specs, out_specs=out_specs, out_shape=out_shape,
                         scratch_shapes=scratch_shapes, compiler_params=_cparams(sem))(*args)
    return res[:n_out], res[n_out:]


def _exchange(arrs, name, scatter=False, ex=None):
    if ex is None:
        ex = _Exchange(arrs, scatter=True) if scatter else _Gather2(arrs)

    def body(*refs):
        _, ins, _, outs, _, sems = _split_comm_refs(refs, 0, 0, 0, ex)
        ex.start(ins, outs, sems)
        ex.forward(ins, outs, sems)
        ex.wait(ins, outs, sems)

    return pl.pallas_call(body, name=name, in_specs=ex.in_specs, out_specs=ex.out_specs, out_shape=ex.out_shape,
                          scratch_shapes=ex.scratch)(*ex.arrs)


def _pad_lanes(v, width=LANE):
    return jnp.pad(v, ((0, 0), (0, width - v.shape[1])))


def _shards_to_cols(g):
    return jnp.transpose(g, (1, 0, 2)).reshape(g.shape[1], N_DEV * g.shape[2])


def _local_step(x, tgt, mod, w_in_pt, conv_w, conv_b, dt_bias, a_log, d_skip, ssd_norm_w, q_norm_w, k_norm_w,
                attn_norm_w, w_out_sh, w_ff1_sh, w_ff2_sh, norm1_w, norm2_w, core):
    shift1, scale1, gate1, shift2, scale2, gate2 = [mod[i:i + 1] for i in range(N_MOD)]
    dtb, alog, dsk = _pad_lanes(dt_bias), _pad_lanes(a_log), _pad_lanes(d_skip)
    qw, kw = jnp.tile(q_norm_w, (1, ATT_HEADS)), jnp.tile(k_norm_w, (1, ATT_HEADS))

    h1 = _norm_mod_fwd(x, norm1_w, scale1, shift1, "norm1_fwd")
    proj = _matmul(h1, w_in_pt, tb=True, tm=2048, tn=896, tk=1024, name="in_proj")
    pre, act = _conv_fwd(proj, conv_w, conv_b)
    ypre, ycat_ssd, hall = _ssd_fwd(proj, act, dtb, alog, dsk, ssd_norm_w)
    (o_att, lse), (w_out_g, w_ff1_g, w_ff2_g) = _att_fwd(proj, qw, kw, comm=_Gather2([w_out_sh, w_ff1_sh, w_ff2_sh]))
    w_out = w_out_g.reshape(2 * D_MODEL, D_MODEL)
    w_ff1 = _shards_to_cols(w_ff1_g)
    w_ff2 = w_ff2_g.reshape(D_FF, D_MODEL)
    ycat = _att_norm_fwd(o_att, attn_norm_w, ycat_ssd)
    row32, row16, vec32 = ("row", F32), ("row", BF16), ("vec", F32)
    mix, x1, h2 = _matmul_rows(ycat, w_out, _residual_norm_epilogue, [x], [gate1, norm2_w, scale2, shift2],
                               [row32, row32, row16], tm=512, name="out_proj")
    u, act_ff = _matmul(h2, w_ff1, tm=1024, tn=2048, tk=1024, name="ff1", mode="relu2")
    loss, dout, dff, dgate2 = _matmul_rows(act_ff, w_ff2, _loss_epilogue, [x1, tgt], [gate2],
                                           [("one", F32), row32, row16, vec32], tm=512, name="ff2")

    du = _matmul(dff, w_ff2, tb=True, tm=512, tn=4096, tk=1024, out_dtype=BF16, name="ff2_dx", mode="drelu2", u=u)
    g_ff2 = _matmul(act_ff, dff, ta=True, tm=512, tn=1024, tk=4096, out_dtype=BF16, name="ff2_dw")
    dx1, dshift2, dscale2, g_norm2, dmix, dgate1 = _matmul_rows(
        du, w_ff1, _norm_bwd_epilogue, [x1, dout, mix], [norm2_w, scale2, gate1],
        [row32, vec32, vec32, vec32, row16, vec32], tb=True, tm=512, name="ff1_dx")
    g_ff1 = _matmul(h2, du, ta=True, tm=1024, tn=D_FF // N_DEV, tk=4096, out_dtype=BF16, name="ff1_dw", shard_out=True)

    dy_ssd, do, stats, g_attn_norm = _matmul_rows(
        dmix, w_out, _mixer_split_epilogue, [o_att, lse], [attn_norm_w],
        [("row", F32, SSD_D_INNER), ("row", F32, ATT_D), ("row", F32, ATT_D), ("vec", F32, ATT_D)], tb=True, tm=512, name="out_proj_dx")
    g_out = _matmul(ycat, dmix, ta=True, tm=512, tn=1024, tk=4096, out_dtype=BF16, name="out_proj_dw")
    ff_slabs = [g_ff1, g_ff2.reshape(N_DEV, D_FF // N_DEV, D_MODEL)]
    (dq, dk, dv, dqw, dkw), (s_ff1, s_ff2) = _att_bwd(proj, do, stats, qw, kw, comm=_Exchange(ff_slabs, scatter=True))
    out_slabs = [g_out.astype(BF16).reshape(N_DEV, 2 * D_MODEL // N_DEV, D_MODEL)]
    (dz, dact, ddtr, da, g_dsk, g_dtb, g_ssd_norm), (s_out,) = _ssd_bwd(
        dy_ssd, ypre, proj, act, hall, dtb, alog, dsk, ssd_norm_w, comm=_Exchange(out_slabs, scatter=True))
    dxbc, g_conv_w, g_conv_b = _conv_bwd(dact, pre, proj, conv_w)
    dproj = [(dz, OFF_Z), (dxbc, OFF_XBC), (ddtr, OFF_DT), (dq, OFF_Q), (dk, OFF_K), (dv, OFF_V)]
    g_head, g_tail = _pieces_t_matmul([[dz, dxbc], [dq, dk, dv]], h1, tm=256, name="in_proj_dw")
    g_dt = _matmul(ddtr, h1, ta=True, tm=LANE, tn=1024, tk=4096, out_dtype=BF16, name="in_proj_dw_dt")[:SSD_HEADS]
    in_slabs = _unpack_w_in_rows([g_head, g_dt, g_tail]).reshape(N_CHIP, 2, IN_W // N_DEV, D_MODEL)
    (sibling_slabs,) = _exchange(None, "swap_w_in_grads", ex=_SiblingSwap([in_slabs]))
    chip_slabs = _chip_sum(lax.dynamic_index_in_dim(in_slabs, core, axis=1, keepdims=False), sibling_slabs)
    (grad_x, dshift1, dscale1, g_norm1), (s_in,) = _matmul_rows(
        dproj, w_in_pt, _norm_bwd_epilogue, [x, dx1], [norm1_w, scale1], [row32, vec32, vec32, vec32],
        tm=256, name="in_proj_dx", comm=_ChipScatter([chip_slabs]))

    dmod = jnp.concatenate([dshift1, dscale1, dgate1, dshift2, dscale2, dgate2], axis=0)
    g_alog = da[:, :SSD_HEADS] * (-jnp.exp(a_log))
    g_qw = dqw.reshape(ATT_HEADS, HEAD_DIM).sum(axis=0, keepdims=True)
    g_kw = dkw.reshape(ATT_HEADS, HEAD_DIM).sum(axis=0, keepdims=True)
    return dict(loss=loss, grad_x=grad_x, dmod=dmod, norm1_w=g_norm1, norm2_w=g_norm2, w_in=s_in, conv_w=g_conv_w,
                conv_b=g_conv_b, dt_bias=g_dtb[:, :SSD_HEADS], a_log=g_alog, d_skip=g_dsk[:, :SSD_HEADS],
                ssd_norm_w=g_ssd_norm, q_norm_w=g_qw, k_norm_w=g_kw, attn_norm_w=g_attn_norm, w_out=s_out,
                w_ff1=s_ff1, w_ff2=s_ff2)


def _pack_w_in_rows(slabs, cols=256):
    rows = IN_W // N_DEV
    cut = OFF_DT + SSD_HEADS
    gap = LANE - SSD_HEADS

    def body(in_ref, out_ref):
        out_ref[cut:cut + gap, :] = jnp.zeros((gap, cols), BF16)
        for d in range(N_DEV):
            lo, hi = rows * d, rows * (d + 1)
            if hi <= cut:
                out_ref[lo:hi, :] = in_ref[d]
            elif lo >= cut:
                out_ref[lo + gap:hi + gap, :] = in_ref[d]
            else:
                out_ref[lo:cut, :] = in_ref[d, :cut - lo, :]
                out_ref[cut + gap:hi + gap, :] = in_ref[d, cut - lo:, :]

    return pl.pallas_call(body, name="pack_w_in", grid=(D_MODEL // cols,),
                          in_specs=[pl.BlockSpec((N_DEV, rows, cols), lambda i: (0, 0, i))],
                          out_specs=pl.BlockSpec((IN_WP, cols), lambda i: (0, i)),
                          out_shape=jax.ShapeDtypeStruct((IN_WP, D_MODEL), BF16),
                          compiler_params=_cparams(("parallel",)))(slabs)


def _unpack_w_in_rows(pieces, cols=256):
    rows = IN_W // N_DEV
    starts = [0]
    for p in pieces:
        starts.append(starts[-1] + p.shape[0])
    assert starts[-1] == IN_W and all(p.dtype == BF16 for p in pieces)

    def body(*refs):
        out_ref = refs[-1]
        for d in range(N_DEV):
            lo, hi = rows * d, rows * (d + 1)
            for ref, ps, pe in zip(refs[:-1], starts[:-1], starts[1:]):
                a, b = max(lo, ps), min(hi, pe)
                if a < b:
                    out_ref[d, a - lo:b - lo, :] = ref[a - ps:b - ps, :]

    return pl.pallas_call(body, name="unpack_w_in_grads", grid=(D_MODEL // cols,),
                          in_specs=[pl.BlockSpec((p.shape[0], cols), lambda i: (0, i)) for p in pieces],
                          out_specs=pl.BlockSpec((N_DEV, rows, cols), lambda i: (0, 0, i)),
                          out_shape=jax.ShapeDtypeStruct((N_DEV, rows, D_MODEL), BF16),
                          compiler_params=_cparams(("parallel",)))(*pieces)


MISC_FIELDS = (("dt_bias", SSD_HEADS), ("a_log", SSD_HEADS), ("d_skip", SSD_HEADS), ("q_norm_w", HEAD_DIM), ("k_norm_w", HEAD_DIM),
               ("loss", 1))
SMALL_LAYOUT = (("b_ada", 6), ("norm1_w", 1), ("norm2_w", 1), ("conv_w", 8), ("conv_b", 2), ("ssd_norm_w", 1),
                ("attn_norm_w", 1), ("misc", 1))


def _pack_small(vals):
    rows = []
    for name, nrow in SMALL_LAYOUT:
        if name == "misc":
            misc = jnp.concatenate([vals[f].reshape(1, n) if f in vals else jnp.zeros((1, n), F32) for f, n in MISC_FIELDS], axis=1)
            rows.append(_pad_lanes(misc, D_MODEL))
        elif name in vals:
            rows.append(vals[name].reshape(nrow, D_MODEL))
        else:
            rows.append(jnp.zeros((nrow, D_MODEL), F32))
    used = sum(n for _, n in SMALL_LAYOUT)
    rows.append(jnp.zeros((SMALL_ROWS - used, D_MODEL), F32))
    return jnp.concatenate(rows, axis=0)


def _unpack_small(packed):
    out, r = {}, 0
    for name, nrow in SMALL_LAYOUT:
        blk = packed[r:r + nrow]
        r += nrow
        if name == "misc":
            c0 = 0
            for f, n in MISC_FIELDS:
                out[f] = blk[:, c0:c0 + n]
                c0 += n
        elif name == "b_ada":
            out[name] = blk.reshape(1, N_MOD * D_MODEL)
        elif name == "conv_w":
            out[name] = blk.reshape(CONV_K, CONV_CH)
        elif name == "conv_b":
            out[name] = blk.reshape(1, CONV_CH)
        else:
            out[name] = blk
    return out


WEIGHT_NAMES = ("norm1_w", "norm2_w", "w_ada", "b_ada", "w_in", "conv_w", "conv_b", "dt_bias", "a_log", "d_skip",
                "ssd_norm_w", "q_norm_w", "k_norm_w", "attn_norm_w", "w_out", "w_ff1", "w_ff2")
SMALL_NAMES = ("norm1_w", "norm2_w", "b_ada", "conv_b", "dt_bias", "a_log", "d_skip", "ssd_norm_w", "q_norm_w",
               "k_norm_w", "attn_norm_w")


def kernel(x, c, norm1_w, norm2_w, w_ada, b_ada, w_in, conv_w, conv_b, dt_bias, a_log, d_skip, ssd_norm_w, q_norm_w, k_norm_w, attn_norm_w, w_out, w_ff1, w_ff2, loss_target, m_norm1_w, m_norm2_w, m_w_ada, m_b_ada, m_w_in, m_conv_w, m_conv_b, m_dt_bias, m_a_log, m_d_skip, m_ssd_norm_w, m_q_norm_w, m_k_norm_w, m_attn_norm_w, m_w_out, m_w_ff1, m_w_ff2, v_norm1_w, v_norm2_w, v_w_ada, v_b_ada, v_w_in, v_conv_w, v_conv_b, v_dt_bias, v_a_log, v_d_skip, v_ssd_norm_w, v_q_norm_w, v_k_norm_w, v_attn_norm_w, v_w_out, v_w_ff1, v_w_ff2):
    args = dict(locals())
    w = {n: args[n] for n in WEIGHT_NAMES}
    m = {n: args["m_" + n] for n in WEIGHT_NAMES}
    v = {n: args["v_" + n] for n in WEIGHT_NAMES}
    me = 4 * lax.axis_index("x") + 2 * lax.axis_index("y") + lax.axis_index("c")

    c_rows = jnp.pad(c, ((0, 7), (0, 0)))
    w_in_t, m_in_t, v_in_t = [jnp.transpose(t["w_in"][0]) for t in (w, m, v)]
    c_g, conv_g, w_in_g = _exchange([c_rows, w["conv_w"][0], w_in_t.astype(BF16)], "gather_w_in", scatter=False)
    c_all = c_g[:, 0, :]
    conv_full = _shards_to_cols(conv_g)
    w_in_pt = _pack_w_in_rows(w_in_g)

    mod_part = _ada_fwd(c_all, w["w_ada"][0])
    (mod_g,) = _exchange([mod_part], "gather_mod", scatter=False)
    mod_mine = lax.dynamic_index_in_dim(mod_g, me, axis=1, keepdims=False).reshape(1, N_MOD * D_MODEL) + w["b_ada"]
    mod = mod_mine.reshape(N_MOD, D_MODEL)

    res = _local_step(x[0], loss_target[0], mod, w_in_pt, conv_full, w["conv_b"], w["dt_bias"], w["a_log"], w["d_skip"],
                      w["ssd_norm_w"], w["q_norm_w"], w["k_norm_w"], w["attn_norm_w"], w["w_out"][0].astype(BF16),
                      w["w_ff1"][0].astype(BF16), w["w_ff2"][0].astype(BF16), w["norm1_w"], w["norm2_w"], lax.axis_index("c"))

    small_vals = {n: res[n] for n in SMALL_NAMES if n != "b_ada"}
    small_vals["b_ada"] = res["dmod"]
    small_vals["conv_w"] = res["conv_w"]
    small_vals["loss"] = res["loss"]
    (small_g,) = _exchange([_pack_small(small_vals)], "gather_small", scatter=False)

    grads, delta, new_m, new_v = {}, {}, {}, {}
    for name in ("w_out", "w_ff1", "w_ff2"):
        outs = _reduce_adamw(res[name], w[name][0], m[name][0], v[name][0], "adamw_" + name)
        grads[name], delta[name], new_m[name], new_v[name] = [o[None] for o in outs]
    outs = _reduce_adamw(res["w_in"], w_in_t, m_in_t, v_in_t, "adamw_w_in")
    grads["w_in"], delta["w_in"], new_m["w_in"], new_v["w_in"] = [jnp.transpose(o)[None] for o in outs]

    sm = _small_reduce_adamw(small_g, _pack_small({n: w[n] for n in SMALL_NAMES}), _pack_small({n: m[n] for n in SMALL_NAMES}),
                             _pack_small({n: v[n] for n in SMALL_NAMES}))
    sm = [_unpack_small(p) for p in sm]
    for n in SMALL_NAMES:
        grads[n], delta[n], new_m[n], new_v[n] = [p[n] for p in sm]
    shard_w = CONV_CH // N_DEV
    g_conv = lax.dynamic_slice_in_dim(sm[0]["conv_w"], me * shard_w, shard_w, axis=1)
    cw = _adamw_small(g_conv, w["conv_w"][0], m["conv_w"][0], v["conv_w"][0], "adamw_conv_w")
    grads["conv_w"] = g_conv[None]
    delta["conv_w"], new_m["conv_w"], new_v["conv_w"] = [o[None] for o in cw]

    ada_w = w_ada.shape[2]
    dmod_all = small_g[:, :N_MOD, :].reshape(N_DEV, N_MOD * D_MODEL)
    dmod_cols = lax.dynamic_slice_in_dim(dmod_all, me * ada_w, ada_w, axis=1)
    outs = _ada_bwd_adamw(c_all, dmod_cols, w["w_ada"][0], m["w_ada"][0], v["w_ada"][0])
    grads["w_ada"], delta["w_ada"], new_m["w_ada"], new_v["w_ada"] = [o[None] for o in outs]

    loss = sm[0]["loss"][0, 0]
    return (loss, res["grad_x"][None], *[grads[n] for n in WEIGHT_NAMES], *[delta[n] for n in WEIGHT_NAMES],
            *[new_m[n] for n in WEIGHT_NAMES], *[new_v[n] for n in WEIGHT_NAMES])
```
